```python
import jax
import jax.numpy as jnp
from jax import lax
import numpy as np

D_MODEL = 1024
BATCH = 16
SEQ = 2048
DEPTH = 2

HEAD_DIM = 64
DIL_PATTERNS = ((128, 1), (512, 4), (2048, 16))
DIL_HEADS_PER_GROUP = 4
DIL_HEADS = len(DIL_PATTERNS) * DIL_HEADS_PER_GROUP
DIL_QBLOCK = 64
NA_HEADS = 8
GRID_W = 64
NA_ROWS_MAX = 8
NA_COLS = 16
NA_QCOLS = 16
D_FF = 2816
ROPE_THETA = 10000.0
RMS_EPS = 1e-6
NEG_INF = -1e30
DIL_QKV = 3 * DIL_HEADS * HEAD_DIM
NA_QKV = 3 * NA_HEADS * HEAD_DIM
IN_WIDTH = DIL_QKV + NA_QKV + 2 * D_MODEL

kernel_name = 'hybrid_dilated_neighbourhood_macaron'


def rms_norm(x, g):
    x32 = x.astype(jnp.float32)
    y = x32 * lax.rsqrt(jnp.mean(x32 * x32, axis=-1, keepdims=True) + RMS_EPS)
    return (y * g.astype(jnp.float32)).astype(x.dtype)


def swiglu(x, w_up, w_down):
    gate, up = jnp.split(x @ w_up, 2, axis=-1)
    return (jax.nn.silu(gate) * up) @ w_down


def rotary(t, pos):
    half = HEAD_DIM // 2
    inv_freq = ROPE_THETA ** (-jnp.arange(half, dtype=jnp.float32) / half)
    ang = pos.astype(jnp.float32)[:, None] * inv_freq[None, :]
    cos = jnp.cos(ang).astype(t.dtype)
    sin = jnp.sin(ang).astype(t.dtype)
    t1, t2 = t[..., :half], t[..., half:]
    return jnp.concatenate([t1 * cos - t2 * sin, t2 * cos + t1 * sin], axis=-1)


def dilated_window_attention(q, k, v, dilation, half):
    b, g, s, hd = q.shape
    sub_len = s // dilation
    bq = min(DIL_QBLOCK, sub_len)
    nb = -(-sub_len // bq)
    padded = nb * bq
    nk = bq + 2 * half

    def to_sub(t):
        return t.reshape(b, g, sub_len, dilation, hd).transpose(0, 1, 3, 2, 4)

    qs = jnp.pad(to_sub(q), ((0, 0), (0, 0), (0, 0), (0, padded - sub_len), (0, 0)))
    qs = qs.reshape(b, g, dilation, nb, bq, hd)
    key_idx = np.arange(nb)[:, None] * bq + np.arange(nk)[None, :]
    kpad = ((0, 0), (0, 0), (0, 0), (half, padded - sub_len + half), (0, 0))
    ks = jnp.pad(to_sub(k), kpad)[:, :, :, key_idx]
    vs = jnp.pad(to_sub(v), kpad)[:, :, :, key_idx]
    q_pos = np.arange(nb)[:, None] * bq + np.arange(bq)[None, :]
    k_pos = key_idx - half
    rel = k_pos[:, None, :] - q_pos[:, :, None]
    valid = (np.abs(rel) <= half) & (k_pos[:, None, :] >= 0) & (k_pos[:, None, :] < sub_len)
    scores = jnp.einsum('bgrnqd,bgrnkd->bgrnqk', qs, ks).astype(jnp.float32) * (hd ** -0.5)
    scores = jnp.where(valid, scores, NEG_INF)
    m = jnp.max(scores, axis=-1, keepdims=True)
    p = jnp.exp(scores - m)
    den = jnp.sum(p, axis=-1)
    o = jnp.einsum('bgrnqk,bgrnkd->bgrnqd', p.astype(v.dtype), vs).astype(jnp.float32) / den[..., None]
    lse = m[..., 0] + jnp.log(den)
    o = o.reshape(b, g, dilation, padded, hd)[:, :, :, :sub_len]
    o = o.transpose(0, 1, 3, 2, 4).reshape(b, g, s, hd)
    lse = lse.reshape(b, g, dilation, padded)[..., :sub_len].transpose(0, 1, 3, 2).reshape(b, g, s)
    return o, lse


def neighbourhood_attention(q, k, v, rel_bias):
    b, nh, s, hd = q.shape
    rows = s // GRID_W
    kr = min(NA_ROWS_MAX, rows)
    n_cb = GRID_W // NA_QCOLS
    kcw = 2 * NA_QCOLS
    q_cols = np.arange(GRID_W).reshape(n_cb, NA_QCOLS)
    key_cols = (np.clip(np.arange(n_cb) * NA_QCOLS - NA_QCOLS // 2, 0, GRID_W - kcw)[:, None]
                + np.arange(kcw)[None, :])
    win_lo = np.clip(q_cols - NA_COLS // 2, 0, GRID_W - NA_COLS)
    col_valid = ((key_cols[:, None, :] >= win_lo[:, :, None])
                 & (key_cols[:, None, :] < win_lo[:, :, None] + NA_COLS))
    col_idx = np.clip(key_cols[:, None, :] - q_cols[:, :, None] + NA_COLS - 1, 0, 2 * NA_COLS - 2)
    row_ids = np.arange(rows)
    row_lo = np.clip(row_ids - kr // 2, 0, rows - kr)
    row_idx = row_lo[:, None] + np.arange(kr)[None, :] - row_ids[:, None] + NA_ROWS_MAX - 1
    bias = rel_bias.astype(jnp.float32)[:, row_idx][..., col_idx]
    bias = jnp.where(col_valid[:, :, None, :], bias.transpose(1, 0, 3, 4, 2, 5), NEG_INF)
    kg = k.reshape(b, nh, rows, GRID_W, hd)
    vg = v.reshape(b, nh, rows, GRID_W, hd)
    q_rows = q.reshape(b, nh, rows, n_cb, NA_QCOLS, hd).transpose(2, 0, 1, 3, 4, 5)
    scale = hd ** -0.5

    def one_row(args):
        q_r, lo, bias_r = args
        k_r = lax.dynamic_slice_in_dim(kg, lo, kr, axis=2)[:, :, :, key_cols]
        v_r = lax.dynamic_slice_in_dim(vg, lo, kr, axis=2)[:, :, :, key_cols]
        sc = jnp.einsum('bhcqd,bhrckd->bhcqrk', q_r, k_r).astype(jnp.float32) * scale + bias_r
        p = jax.nn.softmax(sc.reshape(b, nh, n_cb, NA_QCOLS, kr * kcw), axis=-1).reshape(sc.shape)
        return jnp.einsum('bhcqrk,bhrckd->bhcqd', p.astype(v.dtype), v_r)

    out = lax.map(one_row, (q_rows, jnp.asarray(row_lo, dtype=jnp.int32), bias))
    return out.reshape(rows, b, nh, GRID_W, hd).transpose(1, 2, 0, 3, 4).reshape(b, nh, s, hd)


def hybrid_mixer(h, w_in, rel_bias, w_branch_a, w_branch_b, w_out, pos):
    b, s, _ = h.shape
    proj = h @ w_in
    a_qkv = proj[..., :DIL_QKV]
    b_qkv = proj[..., DIL_QKV:DIL_QKV + NA_QKV]
    gate_a, gate_b = jnp.split(jax.nn.sigmoid(proj[..., DIL_QKV + NA_QKV:]), 2, axis=-1)

    def heads(t, n):
        return t.reshape(b, s, n, HEAD_DIM).transpose(0, 2, 1, 3)

    qa, ka, va = (heads(t, DIL_HEADS) for t in jnp.split(a_qkv, 3, axis=-1))
    qa, ka = rotary(qa, pos), rotary(ka, pos)
    outs, lses = [], []
    for gi, (window, dilation) in enumerate(DIL_PATTERNS):
        grp = slice(gi * DIL_HEADS_PER_GROUP, (gi + 1) * DIL_HEADS_PER_GROUP)
        o, lse = dilated_window_attention(qa[:, grp], ka[:, grp], va[:, grp], dilation, (window // 2) // dilation)
        outs.append(o)
        lses.append(lse)
    mix_w = jax.nn.softmax(jnp.stack(lses), axis=0)
    ya = jnp.sum(mix_w[..., None] * jnp.stack(outs), axis=0).astype(h.dtype)
    ya = ya.transpose(0, 2, 1, 3).reshape(b, s, DIL_HEADS_PER_GROUP * HEAD_DIM)

    qb, kb, vb = (heads(t, NA_HEADS) for t in jnp.split(b_qkv, 3, axis=-1))
    yb = neighbourhood_attention(qb, kb, vb, rel_bias)
    yb = yb.transpose(0, 2, 1, 3).reshape(b, s, NA_HEADS * HEAD_DIM)

    merged = gate_a * (ya @ w_branch_a) + gate_b * (yb @ w_branch_b)
    return merged @ w_out


def _fwd_setup_inputs(seed: int = 0) -> dict:
    key = jax.random.key(seed)
    ks = jax.random.split(key, 14)
    f32 = jnp.float32

    def normal(k, shape, scale):
        return jax.random.normal(k, shape, f32) * scale

    def gain(k, shape):
        return 1.0 + 0.05 * jax.random.normal(k, shape, f32)

    return {
        'x': normal(ks[0], (BATCH, SEQ, D_MODEL), 1.0),
        'ffn1_norm': gain(ks[1], (DEPTH, D_MODEL)),
        'ffn1_w_up': normal(ks[2], (DEPTH, D_MODEL, 2 * D_FF), D_MODEL ** -0.5),
        'ffn1_w_down': normal(ks[3], (DEPTH, D_FF, D_MODEL), D_FF ** -0.5),
        'mix_norm': gain(ks[4], (DEPTH, D_MODEL)),
        'w_in': normal(ks[5], (DEPTH, D_MODEL, IN_WIDTH), D_MODEL ** -0.5),
        'na_rel_bias': normal(ks[6], (DEPTH, NA_HEADS, 2 * NA_ROWS_MAX - 1, 2 * NA_COLS - 1), 0.1),
        'w_branch_a': normal(ks[7], (DEPTH, DIL_HEADS_PER_GROUP * HEAD_DIM, D_MODEL), (DIL_HEADS_PER_GROUP * HEAD_DIM) ** -0.5),
        'w_branch_b': normal(ks[8], (DEPTH, NA_HEADS * HEAD_DIM, D_MODEL), (NA_HEADS * HEAD_DIM) ** -0.5),
        'w_out': normal(ks[9], (DEPTH, D_MODEL, D_MODEL), D_MODEL ** -0.5),
        'ffn2_norm': gain(ks[10], (DEPTH, D_MODEL)),
        'ffn2_w_up': normal(ks[11], (DEPTH, D_MODEL, 2 * D_FF), D_MODEL ** -0.5),
        'ffn2_w_down': normal(ks[12], (DEPTH, D_FF, D_MODEL), D_FF ** -0.5),
        'final_norm': gain(ks[13], (D_MODEL,)),
    }


def _fwd_reference(x, ffn1_norm, ffn1_w_up, ffn1_w_down, mix_norm, w_in, na_rel_bias, w_branch_a,
              w_branch_b, w_out, ffn2_norm, ffn2_w_up, ffn2_w_down, final_norm):
    pos = jnp.arange(x.shape[1])
    for l in range(DEPTH):
        x = x + 0.5 * swiglu(rms_norm(x, ffn1_norm[l]), ffn1_w_up[l], ffn1_w_down[l])
        x = x + hybrid_mixer(rms_norm(x, mix_norm[l]), w_in[l], na_rel_bias[l], w_branch_a[l],
                             w_branch_b[l], w_out[l], pos)
        x = x + 0.5 * swiglu(rms_norm(x, ffn2_norm[l]), ffn2_w_up[l], ffn2_w_down[l])
    return rms_norm(x, final_norm)


import jax as _jax
import jax.numpy as _jnp

TWIN_FORMAT = 'train_step'
FWD_PARAMS = ['x', 'ffn1_norm', 'ffn1_w_up', 'ffn1_w_down', 'mix_norm', 'w_in', 'na_rel_bias', 'w_branch_a', 'w_branch_b', 'w_out', 'ffn2_norm', 'ffn2_w_up', 'ffn2_w_down', 'final_norm']
TWIN_WEIGHTS = ['ffn1_norm', 'ffn1_w_up', 'ffn1_w_down', 'mix_norm', 'w_in', 'na_rel_bias', 'w_branch_a', 'w_branch_b', 'w_out', 'ffn2_norm', 'ffn2_w_up', 'ffn2_w_down', 'final_norm']
TWIN_DIFF_INPUT = 'x'
TWIN_INPUTS = ['x', 'ffn1_norm', 'ffn1_w_up', 'ffn1_w_down', 'mix_norm', 'w_in', 'na_rel_bias', 'w_branch_a', 'w_branch_b', 'w_out', 'ffn2_norm', 'ffn2_w_up', 'ffn2_w_down', 'final_norm', 'loss_target', 'm_ffn1_norm', 'm_ffn1_w_up', 'm_ffn1_w_down', 'm_mix_norm', 'm_w_in', 'm_na_rel_bias', 'm_w_branch_a', 'm_w_branch_b', 'm_w_out', 'm_ffn2_norm', 'm_ffn2_w_up', 'm_ffn2_w_down', 'm_final_norm', 'v_ffn1_norm', 'v_ffn1_w_up', 'v_ffn1_w_down', 'v_mix_norm', 'v_w_in', 'v_na_rel_bias', 'v_w_branch_a', 'v_w_branch_b', 'v_w_out', 'v_ffn2_norm', 'v_ffn2_w_up', 'v_ffn2_w_down', 'v_final_norm']
TWIN_OUTPUTS = ['loss', 'grad_x', 'grad_ffn1_norm', 'grad_ffn1_w_up', 'grad_ffn1_w_down', 'grad_mix_norm', 'grad_w_in', 'grad_na_rel_bias', 'grad_w_branch_a', 'grad_w_branch_b', 'grad_w_out', 'grad_ffn2_norm', 'grad_ffn2_w_up', 'grad_ffn2_w_down', 'grad_final_norm', 'delta_ffn1_norm', 'delta_ffn1_w_up', 'delta_ffn1_w_down', 'delta_mix_norm', 'delta_w_in', 'delta_na_rel_bias', 'delta_w_branch_a', 'delta_w_branch_b', 'delta_w_out', 'delta_ffn2_norm', 'delta_ffn2_w_up', 'delta_ffn2_w_down', 'delta_final_norm', 'new_m_ffn1_norm', 'new_m_ffn1_w_up', 'new_m_ffn1_w_down', 'new_m_mix_norm', 'new_m_w_in', 'new_m_na_rel_bias', 'new_m_w_branch_a', 'new_m_w_branch_b', 'new_m_w_out', 'new_m_ffn2_norm', 'new_m_ffn2_w_up', 'new_m_ffn2_w_down', 'new_m_final_norm', 'new_v_ffn1_norm', 'new_v_ffn1_w_up', 'new_v_ffn1_w_down', 'new_v_mix_norm', 'new_v_w_in', 'new_v_na_rel_bias', 'new_v_w_branch_a', 'new_v_w_branch_b', 'new_v_w_out', 'new_v_ffn2_norm', 'new_v_ffn2_w_up', 'new_v_ffn2_w_down', 'new_v_final_norm']
TWIN_LEAF_KINDS = {'loss': 'loss', 'grad_x': 'grad_x', 'grad_ffn1_norm': 'grad_w', 'grad_ffn1_w_up': 'grad_w', 'grad_ffn1_w_down': 'grad_w', 'grad_mix_norm': 'grad_w', 'grad_w_in': 'grad_w', 'grad_na_rel_bias': 'grad_w', 'grad_w_branch_a': 'grad_w', 'grad_w_branch_b': 'grad_w', 'grad_w_out': 'grad_w', 'grad_ffn2_norm': 'grad_w', 'grad_ffn2_w_up': 'grad_w', 'grad_ffn2_w_down': 'grad_w', 'grad_final_norm': 'grad_w', 'delta_ffn1_norm': 'delta_w', 'delta_ffn1_w_up': 'delta_w', 'delta_ffn1_w_down': 'delta_w', 'delta_mix_norm': 'delta_w', 'delta_w_in': 'delta_w', 'delta_na_rel_bias': 'delta_w', 'delta_w_branch_a': 'delta_w', 'delta_w_branch_b': 'delta_w', 'delta_w_out': 'delta_w', 'delta_ffn2_norm': 'delta_w', 'delta_ffn2_w_up': 'delta_w', 'delta_ffn2_w_down': 'delta_w', 'delta_final_norm': 'delta_w', 'new_m_ffn1_norm': 'new_m', 'new_m_ffn1_w_up': 'new_m', 'new_m_ffn1_w_down': 'new_m', 'new_m_mix_norm': 'new_m', 'new_m_w_in': 'new_m', 'new_m_na_rel_bias': 'new_m', 'new_m_w_branch_a': 'new_m', 'new_m_w_branch_b': 'new_m', 'new_m_w_out': 'new_m', 'new_m_ffn2_norm': 'new_m', 'new_m_ffn2_w_up': 'new_m', 'new_m_ffn2_w_down': 'new_m', 'new_m_final_norm': 'new_m', 'new_v_ffn1_norm': 'new_v', 'new_v_ffn1_w_up': 'new_v', 'new_v_ffn1_w_down': 'new_v', 'new_v_mix_norm': 'new_v', 'new_v_w_in': 'new_v', 'new_v_na_rel_bias': 'new_v', 'new_v_w_branch_a': 'new_v', 'new_v_w_branch_b': 'new_v', 'new_v_w_out': 'new_v', 'new_v_ffn2_norm': 'new_v', 'new_v_ffn2_w_up': 'new_v', 'new_v_ffn2_w_down': 'new_v', 'new_v_final_norm': 'new_v'}


def _forward(args):
    return _fwd_reference(*[args[k] for k in FWD_PARAMS])


def _output_shape():
    out = _jax.eval_shape(lambda: _forward(_fwd_setup_inputs(0)))
    return out.shape, out.dtype

N_MICROBATCH = 1
ADAM_LR = 0.001
ADAM_B1 = 0.9
ADAM_B2 = 0.999
ADAM_EPS = 1e-08
ADAM_WD = 0.01
ADAM_STEP = 10
PER_EXAMPLE_BATCH_AXIS = {'x': 0, 'loss_target': 0}
SHARED_INPUTS = []
_WEIGHT_DTYPES = {'ffn1_norm': _jnp.float32, 'ffn1_w_up': _jnp.float32, 'ffn1_w_down': _jnp.float32, 'mix_norm': _jnp.float32, 'w_in': _jnp.float32, 'na_rel_bias': _jnp.float32, 'w_branch_a': _jnp.float32, 'w_branch_b': _jnp.float32, 'w_out': _jnp.float32, 'ffn2_norm': _jnp.float32, 'ffn2_w_up': _jnp.float32, 'ffn2_w_down': _jnp.float32, 'final_norm': _jnp.float32}
MOMENT_SCALE = {'ffn1_norm': 7.828558e-02, 'ffn1_w_up': 3.315909e-02, 'ffn1_w_down': 5.399903e-02, 'mix_norm': 3.930914e-02, 'w_in': 1.628627e-02, 'na_rel_bias': 9.323757e-03, 'w_branch_a': 1.182706e-02, 'w_branch_b': 1.991369e-02, 'w_out': 2.313015e-02, 'ffn2_norm': 7.253371e-02, 'ffn2_w_up': 3.024621e-02, 'ffn2_w_down': 4.948207e-02, 'final_norm': 3.201077e+01}


def _to_microbatches(a, axis):
    t = _jnp.moveaxis(a, axis, 0)
    t = t.reshape((N_MICROBATCH, t.shape[0] // N_MICROBATCH) + t.shape[1:])
    return _jnp.moveaxis(t, 1, axis + 1)


def setup_inputs(seed: int = 0) -> dict:
    inp = _fwd_setup_inputs(seed)
    key = _jax.random.fold_in(_jax.random.key(seed), 7919)
    shape, _ = _output_shape()
    out = dict(inp)
    out["loss_target"] = _jax.random.normal(_jax.random.fold_in(key, 0), shape, _jnp.float32)
    for i, name in enumerate(TWIN_WEIGHTS):
        w = inp[name].astype(_jnp.float32)
        if MOMENT_SCALE is None:
            s = _jnp.sqrt(_jnp.mean(_jnp.square(w)) + 1e-30)
        else:
            s = MOMENT_SCALE[name]
        km, kv = _jax.random.split(_jax.random.fold_in(key, i + 1))
        out[name] = w
        out["m_" + name] = s * _jax.random.normal(km, w.shape, _jnp.float32)
        out["v_" + name] = (s * s) * _jax.random.uniform(kv, w.shape, _jnp.float32, 0.5, 1.5)
    if N_MICROBATCH > 1:
        for name, axis in PER_EXAMPLE_BATCH_AXIS.items():
            out[name] = _to_microbatches(out[name], axis)
    return {'x': out['x'], 'ffn1_norm': out['ffn1_norm'], 'ffn1_w_up': out['ffn1_w_up'], 'ffn1_w_down': out['ffn1_w_down'], 'mix_norm': out['mix_norm'], 'w_in': out['w_in'], 'na_rel_bias': out['na_rel_bias'], 'w_branch_a': out['w_branch_a'], 'w_branch_b': out['w_branch_b'], 'w_out': out['w_out'], 'ffn2_norm': out['ffn2_norm'], 'ffn2_w_up': out['ffn2_w_up'], 'ffn2_w_down': out['ffn2_w_down'], 'final_norm': out['final_norm'], 'loss_target': out['loss_target'], 'm_ffn1_norm': out['m_ffn1_norm'], 'm_ffn1_w_up': out['m_ffn1_w_up'], 'm_ffn1_w_down': out['m_ffn1_w_down'], 'm_mix_norm': out['m_mix_norm'], 'm_w_in': out['m_w_in'], 'm_na_rel_bias': out['m_na_rel_bias'], 'm_w_branch_a': out['m_w_branch_a'], 'm_w_branch_b': out['m_w_branch_b'], 'm_w_out': out['m_w_out'], 'm_ffn2_norm': out['m_ffn2_norm'], 'm_ffn2_w_up': out['m_ffn2_w_up'], 'm_ffn2_w_down': out['m_ffn2_w_down'], 'm_final_norm': out['m_final_norm'], 'v_ffn1_norm': out['v_ffn1_norm'], 'v_ffn1_w_up': out['v_ffn1_w_up'], 'v_ffn1_w_down': out['v_ffn1_w_down'], 'v_mix_norm': out['v_mix_norm'], 'v_w_in': out['v_w_in'], 'v_na_rel_bias': out['v_na_rel_bias'], 'v_w_branch_a': out['v_w_branch_a'], 'v_w_branch_b': out['v_w_branch_b'], 'v_w_out': out['v_w_out'], 'v_ffn2_norm': out['v_ffn2_norm'], 'v_ffn2_w_up': out['v_ffn2_w_up'], 'v_ffn2_w_down': out['v_ffn2_w_down'], 'v_final_norm': out['v_final_norm']}


def _loss(weights, diff, rest, loss_target):
    with _jax.named_scope("forward"):
        args = {**rest, TWIN_DIFF_INPUT: diff, **{k: w.astype(_WEIGHT_DTYPES[k]) for k, w in weights.items()}}
        y = _forward(args)
    with _jax.named_scope("loss_head"):
        err = _jnp.square(y.astype(_jnp.float32) - loss_target)
        return 0.5 * _jnp.sum(_jnp.mean(err, axis=-1)) if err.ndim else 0.5 * err


def _adamw(w, g, m, v):
    m = ADAM_B1 * m + (1.0 - ADAM_B1) * g
    v = ADAM_B2 * v + (1.0 - ADAM_B2) * _jnp.square(g)
    m_hat = m / (1.0 - ADAM_B1 ** ADAM_STEP)
    v_hat = v / (1.0 - ADAM_B2 ** ADAM_STEP)
    delta = -ADAM_LR * (m_hat / (_jnp.sqrt(v_hat) + ADAM_EPS) + ADAM_WD * w)
    return delta, m, v


def reference(x, ffn1_norm, ffn1_w_up, ffn1_w_down, mix_norm, w_in, na_rel_bias, w_branch_a, w_branch_b, w_out, ffn2_norm, ffn2_w_up, ffn2_w_down, final_norm, loss_target, m_ffn1_norm, m_ffn1_w_up, m_ffn1_w_down, m_mix_norm, m_w_in, m_na_rel_bias, m_w_branch_a, m_w_branch_b, m_w_out, m_ffn2_norm, m_ffn2_w_up, m_ffn2_w_down, m_final_norm, v_ffn1_norm, v_ffn1_w_up, v_ffn1_w_down, v_mix_norm, v_w_in, v_na_rel_bias, v_w_branch_a, v_w_branch_b, v_w_out, v_ffn2_norm, v_ffn2_w_up, v_ffn2_w_down, v_final_norm):
    given = dict(x=x, ffn1_norm=ffn1_norm, ffn1_w_up=ffn1_w_up, ffn1_w_down=ffn1_w_down, mix_norm=mix_norm, w_in=w_in, na_rel_bias=na_rel_bias, w_branch_a=w_branch_a, w_branch_b=w_branch_b, w_out=w_out, ffn2_norm=ffn2_norm, ffn2_w_up=ffn2_w_up, ffn2_w_down=ffn2_w_down, final_norm=final_norm, loss_target=loss_target, m_ffn1_norm=m_ffn1_norm, m_ffn1_w_up=m_ffn1_w_up, m_ffn1_w_down=m_ffn1_w_down, m_mix_norm=m_mix_norm, m_w_in=m_w_in, m_na_rel_bias=m_na_rel_bias, m_w_branch_a=m_w_branch_a, m_w_branch_b=m_w_branch_b, m_w_out=m_w_out, m_ffn2_norm=m_ffn2_norm, m_ffn2_w_up=m_ffn2_w_up, m_ffn2_w_down=m_ffn2_w_down, m_final_norm=m_final_norm, v_ffn1_norm=v_ffn1_norm, v_ffn1_w_up=v_ffn1_w_up, v_ffn1_w_down=v_ffn1_w_down, v_mix_norm=v_mix_norm, v_w_in=v_w_in, v_na_rel_bias=v_na_rel_bias, v_w_branch_a=v_w_branch_a, v_w_branch_b=v_w_branch_b, v_w_out=v_w_out, v_ffn2_norm=v_ffn2_norm, v_ffn2_w_up=v_ffn2_w_up, v_ffn2_w_down=v_ffn2_w_down, v_final_norm=v_final_norm)
    weights = {n: given[n] for n in TWIN_WEIGHTS}
    shared = {n: given[n] for n in SHARED_INPUTS}
    per_example = {n: given[n] for n in ['x']}
    grad_fn = _jax.value_and_grad(_loss, argnums=(0, 1))

    def one_microbatch(ex, loss_target):
        ex = dict(ex)
        diff = ex.pop(TWIN_DIFF_INPUT)
        return grad_fn(weights, diff, {**shared, **ex}, loss_target)

    if N_MICROBATCH == 1:
        loss, (grad_w, grad_x) = one_microbatch(per_example, given["loss_target"])
    else:
        def body(carry, xs):
            loss_sum, grad_sum = carry
            l_k, (gw_k, gx_k) = one_microbatch(xs[0], xs[1])
            with _jax.named_scope("update"):
                return (loss_sum + l_k, _jax.tree.map(_jnp.add, grad_sum, gw_k)), gx_k

        init = (_jnp.zeros((), _jnp.float32), _jax.tree.map(_jnp.zeros_like, weights))
        (loss, grad_w), grad_x = _jax.lax.scan(body, init, (per_example, given["loss_target"]))
    with _jax.named_scope("update"):
        delta_w, new_m, new_v = {}, {}, {}
        for n in TWIN_WEIGHTS:
            delta_w[n], new_m[n], new_v[n] = _adamw(weights[n], grad_w[n], given["m_" + n], given["v_" + n])
    return (loss, grad_x, *[grad_w[n] for n in TWIN_WEIGHTS], *[delta_w[n] for n in TWIN_WEIGHTS],
            *[new_m[n] for n in TWIN_WEIGHTS], *[new_v[n] for n in TWIN_WEIGHTS])
```

```python
import functools

import numpy as np
import jax
import jax.numpy as jnp
from jax import lax
from jax.experimental import pallas as pl
from jax.experimental.pallas import tpu as pltpu

F32, BF16 = jnp.float32, jnp.bfloat16
MESH = pl.DeviceIdType.MESH

HEAD_DIM = 64
DILATIONS = (1, 4, 16)
DIL_HALF = 64
DIL_GROUP_HEADS = 4
DIL_HEADS = 12
NA_HEADS = 8
GRID_W = 64
NA_ROWS = 8
NA_COLS = 16
ROPE_THETA = 10000.0
RMS_EPS = 1e-6
NEG_INF = -1e30
DEPTH = 2
ADAM_LR, ADAM_B1, ADAM_B2, ADAM_EPS, ADAM_WD, ADAM_STEP = 0.001, 0.9, 0.999, 1e-08, 0.01, 10
QK_SCALE = HEAD_DIM ** -0.5

N_CHIPS = 4
LANES = 128
VMEM_LIMIT = 56 * 1024 * 1024

_NN = (((1,), (0,)), ((), ()))
_NT = (((1,), (1,)), ((), ()))
_TN = (((0,), (0,)), ((), ()))


def _params(**kw):
    return pltpu.CompilerParams(vmem_limit_bytes=VMEM_LIMIT, **kw)


def _dot(a, b, dims):
    return lax.dot_general(a, b, dims, preferred_element_type=F32)


def _mm(a, b, *, mode, out_dtype, tm, tn, tk, name, alpha=1.0, res=None):
    if mode == "nn":
        (m, k), (k2, n) = a.shape, b.shape
        a_spec = pl.BlockSpec((tm, tk), lambda i, j, kk: (i, kk))
        b_spec = pl.BlockSpec((tk, tn), lambda i, j, kk: (kk, j))
        dims = _NN
    elif mode == "nt":
        (m, k), (n, k2) = a.shape, b.shape
        a_spec = pl.BlockSpec((tm, tk), lambda i, j, kk: (i, kk))
        b_spec = pl.BlockSpec((tn, tk), lambda i, j, kk: (j, kk))
        dims = _NT
    else:
        (k, m), (k2, n) = a.shape, b.shape
        a_spec = pl.BlockSpec((tk, tm), lambda i, j, kk: (kk, i))
        b_spec = pl.BlockSpec((tk, tn), lambda i, j, kk: (kk, j))
        dims = _TN
    assert k == k2 and m % tm == 0 and n % tn == 0 and k % tk == 0, (name, a.shape, b.shape)
    nk = k // tk
    o_spec = pl.BlockSpec((tm, tn), lambda i, j, kk: (i, j))
    has_res = res is not None

    def body(*refs):
        a_ref, b_ref = refs[0], refs[1]
        r_ref = refs[2] if has_res else None
        o_ref = refs[3] if has_res else refs[2]
        p = _dot(a_ref[...], b_ref[...], dims)

        def finish(acc):
            y = acc * alpha if alpha != 1.0 else acc
            if has_res:
                y = y + r_ref[...].astype(F32)
            o_ref[...] = y.astype(o_ref.dtype)

        if nk == 1:
            finish(p)
        else:
            acc_ref = refs[-1]
            kk = pl.program_id(2)

            @pl.when(kk == 0)
            def _():
                acc_ref[...] = p

            @pl.when(kk > 0)
            def _():
                acc_ref[...] += p

            @pl.when(kk == nk - 1)
            def _():
                finish(acc_ref[...])

    return pl.pallas_call(
        body, name=name, grid=(m // tm, n // tn, nk),
        in_specs=[a_spec, b_spec] + ([o_spec] if has_res else []), out_specs=o_spec,
        out_shape=jax.ShapeDtypeStruct((m, n), out_dtype),
        scratch_shapes=[pltpu.VMEM((tm, tn), F32)] if nk > 1 else [],
        compiler_params=_params(dimension_semantics=("parallel", "parallel", "arbitrary")),
    )(*((a, b, res) if has_res else (a, b)))


def _mm_swiglu_fwd(h, wg, wu, *, tm, tn, name):
    m, k = h.shape
    n = wg.shape[1]
    h_spec = pl.BlockSpec((tm, k), lambda i, j: (i, 0))
    w_spec = pl.BlockSpec((k, tn), lambda i, j: (0, j))
    o_spec = pl.BlockSpec((tm, tn), lambda i, j: (i, j))

    def body(h_ref, wg_ref, wu_ref, a_ref, g_ref, u_ref):
        hb = h_ref[...]
        g = _dot(hb, wg_ref[...], _NN)
        u = _dot(hb, wu_ref[...], _NN)
        a_ref[...] = (g * jax.nn.sigmoid(g) * u).astype(BF16)
        g_ref[...] = g.astype(BF16)
        u_ref[...] = u.astype(BF16)

    out = jax.ShapeDtypeStruct((m, n), BF16)
    return pl.pallas_call(
        body, name=name, grid=(m // tm, n // tn), in_specs=[h_spec, w_spec, w_spec],
        out_specs=[o_spec] * 3, out_shape=[out] * 3,
        compiler_params=_params(dimension_semantics=("parallel", "parallel")),
    )(h, wg, wu)


def _mm_swiglu_bwd(dy, wd, gate, up, *, alpha, tm, tn, name):
    m, k = dy.shape
    n = wd.shape[0]
    dy_spec = pl.BlockSpec((tm, k), lambda i, j: (i, 0))
    w_spec = pl.BlockSpec((tn, k), lambda i, j: (j, 0))
    o_spec = pl.BlockSpec((tm, tn), lambda i, j: (i, j))

    def body(dy_ref, w_ref, g_ref, u_ref, dg_ref, du_ref):
        da = _dot(dy_ref[...], w_ref[...], _NT) * alpha
        g = g_ref[...].astype(F32)
        u = u_ref[...].astype(F32)
        sg = jax.nn.sigmoid(g)
        dg_ref[...] = (da * u * (sg * (1.0 + g * (1.0 - sg)))).astype(BF16)
        du_ref[...] = (da * (g * sg)).astype(BF16)

    out = jax.ShapeDtypeStruct((m, n), BF16)
    return pl.pallas_call(
        body, name=name, grid=(m // tm, n // tn), in_specs=[dy_spec, w_spec, o_spec, o_spec],
        out_specs=[o_spec] * 2, out_shape=[out] * 2,
        compiler_params=_params(dimension_semantics=("parallel", "parallel")),
    )(dy, wd, gate, up)


def _rms_fwd(x, g, *, tt, name):
    t, d = x.shape

    def body(x_ref, g_ref, h_ref):
        xv = x_ref[...]
        rstd = lax.rsqrt(jnp.mean(xv * xv, axis=1, keepdims=True) + RMS_EPS)
        h_ref[...] = (xv * rstd * g_ref[...]).astype(BF16)

    return pl.pallas_call(
        body, name=name, grid=(t // tt,),
        in_specs=[pl.BlockSpec((tt, d), lambda i: (i, 0)), pl.BlockSpec((1, d), lambda i: (0, 0))],
        out_specs=pl.BlockSpec((tt, d), lambda i: (i, 0)), out_shape=jax.ShapeDtypeStruct((t, d), BF16),
        compiler_params=_params(dimension_semantics=("parallel",)),
    )(x, g)


def _rms_bwd(dh, x, g, dres, *, tt, name):
    t, d = x.shape

    def body(dh_ref, x_ref, g_ref, r_ref, dx_ref, dg_ref):
        xv = x_ref[...]
        rstd = lax.rsqrt(jnp.mean(xv * xv, axis=1, keepdims=True) + RMS_EPS)
        xhat = xv * rstd
        dhv = dh_ref[...]
        dxhat = dhv * g_ref[...]
        dx_ref[...] = r_ref[...] + rstd * (dxhat - xhat * jnp.mean(dxhat * xhat, axis=1, keepdims=True))

        @pl.when(pl.program_id(0) == 0)
        def _():
            dg_ref[...] = jnp.zeros_like(dg_ref)

        dg_ref[...] += jnp.sum(dhv * xhat, axis=0, keepdims=True)

    row = pl.BlockSpec((tt, d), lambda i: (i, 0))
    vec = pl.BlockSpec((1, d), lambda i: (0, 0))
    return pl.pallas_call(
        body, name=name, grid=(t // tt,), in_specs=[row, row, vec, row], out_specs=[row, vec],
        out_shape=[jax.ShapeDtypeStruct((t, d), F32), jax.ShapeDtypeStruct((1, d), F32)],
        compiler_params=_params(dimension_semantics=("arbitrary",)),
    )(dh, x, g, dres)


def _final_loss(x, g, target, *, tt, name):
    t, d = x.shape

    def body(x_ref, g_ref, t_ref, dx_ref, dg_ref, loss_ref):
        xv = x_ref[...]
        gv = g_ref[...]
        rstd = lax.rsqrt(jnp.mean(xv * xv, axis=1, keepdims=True) + RMS_EPS)
        xhat = xv * rstd
        err = xhat * gv - t_ref[...]
        dy = err * (1.0 / d)
        dxhat = dy * gv
        dx_ref[...] = rstd * (dxhat - xhat * jnp.mean(dxhat * xhat, axis=1, keepdims=True))

        @pl.when(pl.program_id(0) == 0)
        def _():
            dg_ref[...] = jnp.zeros_like(dg_ref)
            loss_ref[...] = jnp.zeros_like(loss_ref)

        dg_ref[...] += jnp.sum(dy * xhat, axis=0, keepdims=True)
        part = 0.5 * jnp.sum(jnp.mean(err * err, axis=1, keepdims=True), axis=0, keepdims=True)
        loss_ref[...] += jnp.broadcast_to(part, loss_ref.shape)

    row = pl.BlockSpec((tt, d), lambda i: (i, 0))
    vec = pl.BlockSpec((1, d), lambda i: (0, 0))
    one = pl.BlockSpec((1, LANES), lambda i: (0, 0))
    return pl.pallas_call(
        body, name=name, grid=(t // tt,), in_specs=[row, vec, row], out_specs=[row, vec, one],
        out_shape=[jax.ShapeDtypeStruct((t, d), F32), jax.ShapeDtypeStruct((1, d), F32),
                   jax.ShapeDtypeStruct((1, LANES), F32)],
        compiler_params=_params(dimension_semantics=("arbitrary",)),
    )(x, g, target)


def _swap_halves(x):
    lane = lax.broadcasted_iota(jnp.int32, x.shape, 1)
    return jnp.where((lane // 32) % 2 == 0, pltpu.roll(x, 96, 1), pltpu.roll(x, 32, 1))


def _rope_tables(s):
    half = HEAD_DIM // 2
    inv_freq = ROPE_THETA ** (-jnp.arange(half, dtype=F32) / half)
    ang = jnp.arange(s).astype(F32)[:, None] * inv_freq[None, :]
    cos, sin = jnp.cos(ang), jnp.sin(ang)
    return jnp.tile(cos, (1, 4)), jnp.concatenate([-sin, sin, -sin, sin], axis=1)


def _split_heads(proj, cos4, sin4, *, n_pairs, rot_pairs, scale_ranges, ts, name):
    b, s, _ = proj.shape

    def body(x_ref, c_ref, s_ref, o_ref):
        x = x_ref[...]
        p = pl.program_id(1)
        rot = x * c_ref[...] + _swap_halves(x) * s_ref[...]
        y = jnp.where(p < rot_pairs, rot, x)
        is_q = functools.reduce(jnp.logical_or, [(p >= lo) & (p < hi) for lo, hi in scale_ranges])
        y = y * jnp.where(is_q, QK_SCALE, 1.0)
        o_ref[0] = y[:, :HEAD_DIM].astype(BF16)
        o_ref[1] = y[:, HEAD_DIM:].astype(BF16)

    tab = pl.BlockSpec((ts, LANES), lambda bi, p, si: (si, 0))
    return pl.pallas_call(
        body, name=name, grid=(b, n_pairs, s // ts),
        in_specs=[pl.BlockSpec((None, ts, LANES), lambda bi, p, si: (bi, si, p)), tab, tab],
        out_specs=pl.BlockSpec((None, 2, ts, HEAD_DIM), lambda bi, p, si: (bi, p, si, 0)),
        out_shape=jax.ShapeDtypeStruct((b, 2 * n_pairs, s, HEAD_DIM), BF16),
        compiler_params=_params(dimension_semantics=("parallel", "parallel", "parallel")),
    )(proj, cos4, sin4)


def _merge_heads(dheads, cos4, sin4, *, heads_per_row, rot_pairs, scale_pairs, ts, name):
    b, hpr, r, s, _ = dheads.shape
    n_pairs = hpr * r // 2
    ppr = hpr // 2

    def body(d_ref, c_ref, s_ref, o_ref):
        dy = jnp.concatenate([d_ref[0], d_ref[1]], axis=1)
        p = pl.program_id(1)
        rot = dy * c_ref[...] - _swap_halves(dy) * s_ref[...]
        dx = jnp.where(p < rot_pairs, rot, dy)
        dx = dx * jnp.where(p < scale_pairs, QK_SCALE, 1.0)
        o_ref[...] = dx.astype(BF16)

    tab = pl.BlockSpec((ts, LANES), lambda bi, p, si: (si, 0))
    return pl.pallas_call(
        body, name=name, grid=(b, n_pairs, s // ts),
        in_specs=[pl.BlockSpec((None, 2, None, ts, HEAD_DIM), lambda bi, p, si: (bi, p % ppr, p // ppr, si, 0)), tab, tab],
        out_specs=pl.BlockSpec((None, ts, LANES), lambda bi, p, si: (bi, si, p)),
        out_shape=jax.ShapeDtypeStruct((b, s, LANES * n_pairs), BF16),
        compiler_params=_params(dimension_semantics=("parallel", "parallel", "parallel")),
    )(dheads, cos4, sin4)


def _dil_window(g, q0, tq, s):
    pad = -(-DIL_HALF * DILATIONS[g] // LANES) * LANES
    width = tq + 2 * pad
    if width >= s:
        return 0, s
    return pl.multiple_of(jnp.clip(q0 - pad, 0, s - width), LANES), width


def _dil_mask(g, q0, start, shape):
    d = DILATIONS[g]
    diff = (q0 - start) + lax.broadcasted_iota(jnp.int32, shape, 0) - lax.broadcasted_iota(jnp.int32, shape, 1)
    ok = jnp.abs(diff) <= DIL_HALF * d
    if d > 1:
        ok = ok & ((diff & (d - 1)) == 0)
    return ok


def _dil_head_spec(part, g, s):
    return pl.BlockSpec((None, None, s, HEAD_DIM), lambda b, j: (b, part * DIL_HEADS + g * DIL_GROUP_HEADS + j, 0, 0))


def _dil_attn_fwd(heads, *, tq, name):
    b, _, s, _ = heads.shape
    n_g = len(DILATIONS)

    def body(*refs):
        qkv = refs[:3 * n_g]
        o_ref, l_ref = refs[3 * n_g:]

        def step(i, carry):
            q0 = pl.multiple_of(i * tq, tq)
            scores, wins = [], []
            for g in range(n_g):
                start, width = _dil_window(g, q0, tq, s)
                sc = _dot(qkv[3 * g][pl.ds(q0, tq), :], qkv[3 * g + 1][pl.ds(start, width), :], _NT)
                scores.append(jnp.where(_dil_mask(g, q0, start, sc.shape), sc, NEG_INF))
                wins.append((start, width))
            m = functools.reduce(jnp.maximum, [jnp.max(sc, axis=1, keepdims=True) for sc in scores])
            den = jnp.zeros((tq, 1), F32)
            acc = jnp.zeros((tq, HEAD_DIM), F32)
            for g in range(n_g):
                p = jnp.exp(scores[g] - m)
                den = den + jnp.sum(p, axis=1, keepdims=True)
                acc = acc + _dot(p.astype(BF16), qkv[3 * g + 2][pl.ds(*wins[g]), :], _NN)
            o_ref[pl.ds(q0, tq), :] = (acc / den).astype(o_ref.dtype)
            l_ref[pl.ds(q0, tq), :] = m + jnp.log(den)
            return carry

        lax.fori_loop(0, s // tq, step, 0)

    out = pl.BlockSpec((None, None, s, HEAD_DIM), lambda bi, j: (bi, j, 0, 0))
    lse = pl.BlockSpec((None, None, s, 1), lambda bi, j: (bi, j, 0, 0))
    return pl.pallas_call(
        body, name=name, grid=(b, DIL_GROUP_HEADS),
        in_specs=[_dil_head_spec(part, g, s) for g in range(n_g) for part in range(3)],
        out_specs=[out, lse],
        out_shape=[jax.ShapeDtypeStruct((b, DIL_GROUP_HEADS, s, HEAD_DIM), BF16),
                   jax.ShapeDtypeStruct((b, DIL_GROUP_HEADS, s, 1), F32)],
        compiler_params=_params(dimension_semantics=("parallel", "parallel")),
    )(*([heads] * (3 * n_g)))


def _dil_attn_bwd(heads, out, lse, dout, *, tq, name):
    b, _, s, _ = heads.shape
    n_g = len(DILATIONS)

    def body(*refs):
        qkv = refs[:3 * n_g]
        o_ref, l_ref, do_ref, d_ref = refs[3 * n_g:]
        d_ref[...] = jnp.zeros_like(d_ref)

        def step(i, carry):
            q0 = pl.multiple_of(i * tq, tq)
            do = do_ref[pl.ds(q0, tq), :]
            delta = jnp.sum(do * o_ref[pl.ds(q0, tq), :].astype(F32), axis=1, keepdims=True)
            lse_b = l_ref[pl.ds(q0, tq), :]
            do_b = do.astype(BF16)
            for g in range(n_g):
                start, width = _dil_window(g, q0, tq, s)
                win = pl.ds(start, width)
                q = qkv[3 * g][pl.ds(q0, tq), :]
                k = qkv[3 * g + 1][win, :]
                v = qkv[3 * g + 2][win, :]
                sc = _dot(q, k, _NT)
                p = jnp.where(_dil_mask(g, q0, start, sc.shape), jnp.exp(sc - lse_b), 0.0)
                ds = (p * (_dot(do_b, v, _NT) - delta)).astype(BF16)
                d_ref[g, pl.ds(q0, tq), :] = _dot(ds, k, _NN)
                d_ref[n_g + g, win, :] += _dot(ds, q, _TN)
                d_ref[2 * n_g + g, win, :] += _dot(p.astype(BF16), do_b, _TN)
            return carry

        lax.fori_loop(0, s // tq, step, 0)

    per_head = lambda bi, j: (bi, j, 0, 0)
    return pl.pallas_call(
        body, name=name, grid=(b, DIL_GROUP_HEADS),
        in_specs=[_dil_head_spec(part, g, s) for g in range(n_g) for part in range(3)]
        + [pl.BlockSpec((None, None, s, HEAD_DIM), per_head), pl.BlockSpec((None, None, s, 1), per_head),
           pl.BlockSpec((None, None, s, HEAD_DIM), per_head)],
        out_specs=pl.BlockSpec((None, None, 3 * n_g, s, HEAD_DIM), lambda bi, j: (bi, j, 0, 0, 0)),
        out_shape=jax.ShapeDtypeStruct((b, DIL_GROUP_HEADS, 3 * n_g, s, HEAD_DIM), F32),
        compiler_params=_params(dimension_semantics=("parallel", "parallel")),
    )(*([heads] * (3 * n_g)), out, lse, dout)


NA_OFFSETS = NA_ROWS
NA_BIAS_ROWS = 2 * NA_ROWS - 1
NA_BIAS_COLS = 2 * NA_COLS - 1
NA_KEYS = NA_ROWS * GRID_W


def _na_onehot():
    c = np.arange(GRID_W)[:, None]
    k = np.arange(GRID_W)[None, :]
    lo = np.clip(c - NA_COLS // 2, 0, GRID_W - NA_COLS)
    valid = (k >= lo) & (k < lo + NA_COLS)
    onehot = np.zeros((GRID_W, GRID_W, LANES), np.float32)
    cc, kk = np.nonzero(valid)
    onehot[cc, kk, kk - cc + NA_COLS - 1] = 1.0
    return onehot.reshape(GRID_W * GRID_W, LANES), valid.reshape(1, GRID_W * GRID_W)


def _na_expand_bias(rel_bias, *, name):
    l, h, nr, nc = rel_bias.shape
    onehot, valid = _na_onehot()
    rows = l * h * nr
    rb = jnp.pad(rel_bias.reshape(rows, nc), ((0, 0), (0, LANES - nc)))

    def body(rb_ref, oh_ref, valid_ref, e_ref):
        e = lax.dot_general(rb_ref[...], oh_ref[...], _NT, precision=lax.Precision.HIGHEST, preferred_element_type=F32)
        e_ref[...] = jnp.where(valid_ref[...] > 0, e, NEG_INF)

    e = pl.pallas_call(
        body, name=name, out_shape=jax.ShapeDtypeStruct((rows, GRID_W * GRID_W), F32), compiler_params=_params(),
    )(rb, jnp.asarray(onehot), jnp.asarray(valid.astype(np.float32)))
    e = e.reshape(l, h, nr, GRID_W, GRID_W)
    by_off = jnp.stack([e[:, :, off:off + NA_ROWS] for off in range(NA_OFFSETS)], axis=2)
    return by_off.transpose(0, 1, 2, 4, 3, 5).reshape(l, h, NA_OFFSETS, GRID_W, NA_KEYS)


def _na_collapse_bias(dbias, *, name):
    b, h = dbias.shape[:2]
    onehot, _ = _na_onehot()

    def fold(d_ref, e_ref):
        acc = [jnp.zeros((GRID_W, GRID_W), F32) for _ in range(NA_BIAS_ROWS)]
        for bi in range(b):
            for off in range(NA_OFFSETS):
                for kr in range(NA_ROWS):
                    acc[off + kr] = acc[off + kr] + d_ref[bi, off, :, kr * GRID_W:(kr + 1) * GRID_W]
        for i in range(NA_BIAS_ROWS):
            e_ref[i] = acc[i]

    de = pl.pallas_call(
        fold, name=name + "_fold", grid=(h,),
        in_specs=[pl.BlockSpec((b, None, NA_OFFSETS, GRID_W, NA_KEYS), lambda hi: (0, hi, 0, 0, 0))],
        out_specs=pl.BlockSpec((None, NA_BIAS_ROWS, GRID_W, GRID_W), lambda hi: (hi, 0, 0, 0)),
        out_shape=jax.ShapeDtypeStruct((h, NA_BIAS_ROWS, GRID_W, GRID_W), F32),
        compiler_params=_params(dimension_semantics=("parallel",)),
    )(dbias)

    def diag(e_ref, oh_ref, o_ref):
        o_ref[...] = lax.dot_general(e_ref[...], oh_ref[...], _NN, precision=lax.Precision.HIGHEST, preferred_element_type=F32)

    rows = h * NA_BIAS_ROWS
    drb = pl.pallas_call(
        diag, name=name + "_diag", out_shape=jax.ShapeDtypeStruct((rows, LANES), F32), compiler_params=_params(),
    )(de.reshape(rows, GRID_W * GRID_W), jnp.asarray(onehot))
    return drb[:, :NA_BIAS_COLS].reshape(h, NA_BIAS_ROWS, NA_BIAS_COLS)


def _na_row(r, n_rows):
    row_lo = jnp.clip(r - NA_ROWS // 2, 0, n_rows - NA_ROWS)
    return row_lo, row_lo - r + NA_ROWS - 1


def _na_head_spec(part, first, s):
    return pl.BlockSpec((None, None, s, HEAD_DIM), lambda b, h: (b, first + part * NA_HEADS + h, 0, 0))


def _na_attn_fwd(heads, bias, *, first, name):
    b, _, s, _ = heads.shape
    n_rows = s // GRID_W

    def body(q_ref, k_ref, v_ref, b_ref, o_ref, l_ref):
        def step(r, carry):
            row_lo, off = _na_row(r, n_rows)
            rows = pl.ds(pl.multiple_of(r * GRID_W, GRID_W), GRID_W)
            win = pl.ds(pl.multiple_of(row_lo * GRID_W, GRID_W), NA_KEYS)
            sc = _dot(q_ref[rows, :], k_ref[win, :], _NT) + b_ref[off]
            m = jnp.max(sc, axis=1, keepdims=True)
            p = jnp.exp(sc - m)
            den = jnp.sum(p, axis=1, keepdims=True)
            o_ref[rows, :] = (_dot(p.astype(BF16), v_ref[win, :], _NN) / den).astype(o_ref.dtype)
            l_ref[rows, :] = m + jnp.log(den)
            return carry

        lax.fori_loop(0, n_rows, step, 0)

    per_head = lambda bi, h: (bi, h, 0, 0)
    return pl.pallas_call(
        body, name=name, grid=(b, NA_HEADS),
        in_specs=[_na_head_spec(part, first, s) for part in range(3)]
        + [pl.BlockSpec((None, NA_OFFSETS, GRID_W, NA_KEYS), lambda bi, h: (h, 0, 0, 0))],
        out_specs=[pl.BlockSpec((None, None, s, HEAD_DIM), per_head), pl.BlockSpec((None, None, s, 1), per_head)],
        out_shape=[jax.ShapeDtypeStruct((b, NA_HEADS, s, HEAD_DIM), BF16), jax.ShapeDtypeStruct((b, NA_HEADS, s, 1), F32)],
        compiler_params=_params(dimension_semantics=("parallel", "parallel")),
    )(heads, heads, heads, bias)


def _na_attn_bwd(heads, bias, out, lse, dout, *, first, name):
    b, _, s, _ = heads.shape
    n_rows = s // GRID_W

    def body(q_ref, k_ref, v_ref, b_ref, o_ref, l_ref, do_ref, d_ref, db_ref):
        d_ref[...] = jnp.zeros_like(d_ref)
        db_ref[...] = jnp.zeros_like(db_ref)

        def step(r, carry):
            row_lo, off = _na_row(r, n_rows)
            rows = pl.ds(pl.multiple_of(r * GRID_W, GRID_W), GRID_W)
            win = pl.ds(pl.multiple_of(row_lo * GRID_W, GRID_W), NA_KEYS)
            q, k, v = q_ref[rows, :], k_ref[win, :], v_ref[win, :]
            do = do_ref[rows, :]
            delta = jnp.sum(do * o_ref[rows, :].astype(F32), axis=1, keepdims=True)
            do_b = do.astype(BF16)
            p = jnp.exp(_dot(q, k, _NT) + b_ref[off] - l_ref[rows, :])
            ds = p * (_dot(do_b, v, _NT) - delta)
            db_ref[off] += ds
            ds_b = ds.astype(BF16)
            d_ref[0, rows, :] = _dot(ds_b, k, _NN)
            d_ref[1, win, :] += _dot(ds_b, q, _TN)
            d_ref[2, win, :] += _dot(p.astype(BF16), do_b, _TN)
            return carry

        lax.fori_loop(0, n_rows, step, 0)

    per_head = lambda bi, h: (bi, h, 0, 0)
    return pl.pallas_call(
        body, name=name, grid=(b, NA_HEADS),
        in_specs=[_na_head_spec(part, first, s) for part in range(3)]
        + [pl.BlockSpec((None, NA_OFFSETS, GRID_W, NA_KEYS), lambda bi, h: (h, 0, 0, 0)),
           pl.BlockSpec((None, None, s, HEAD_DIM), per_head), pl.BlockSpec((None, None, s, 1), per_head),
           pl.BlockSpec((None, None, s, HEAD_DIM), per_head)],
        out_specs=[pl.BlockSpec((None, None, 3, s, HEAD_DIM), lambda bi, h: (bi, h, 0, 0, 0)),
                   pl.BlockSpec((None, None, NA_OFFSETS, GRID_W, NA_KEYS), lambda bi, h: (bi, h, 0, 0, 0))],
        out_shape=[jax.ShapeDtypeStruct((b, NA_HEADS, 3, s, HEAD_DIM), F32),
                   jax.ShapeDtypeStruct((b, NA_HEADS, NA_OFFSETS, GRID_W, NA_KEYS), F32)],
        compiler_params=_params(dimension_semantics=("parallel", "parallel")),
    )(heads, heads, heads, bias, out, lse, dout)


GATE_TILE = 256


def _gate_fwd(proj, za, zb, *, gate_col, tt, name):
    t, d = za.shape
    nj = d // GATE_TILE
    c0 = gate_col // GATE_TILE

    def body(ga_ref, gb_ref, za_ref, zb_ref, o_ref):
        o_ref[...] = (jax.nn.sigmoid(ga_ref[...]) * za_ref[...] + jax.nn.sigmoid(gb_ref[...]) * zb_ref[...]).astype(BF16)

    blk = pl.BlockSpec((tt, GATE_TILE), lambda i, j: (i, j))
    return pl.pallas_call(
        body, name=name, grid=(t // tt, nj),
        in_specs=[pl.BlockSpec((tt, GATE_TILE), lambda i, j: (i, c0 + j)),
                  pl.BlockSpec((tt, GATE_TILE), lambda i, j: (i, c0 + nj + j)), blk, blk],
        out_specs=blk, out_shape=jax.ShapeDtypeStruct((t, d), BF16),
        compiler_params=_params(dimension_semantics=("parallel", "parallel")),
    )(proj, proj, za, zb)


def _gate_bwd(dm, proj, za, zb, *, gate_col, tt, name):
    t, d = za.shape
    nj = d // GATE_TILE
    c0 = gate_col // GATE_TILE

    def body(dm_ref, ga_ref, gb_ref, za_ref, zb_ref, dza_ref, dzb_ref, dga_ref, dgb_ref):
        dmv = dm_ref[...]
        sa, sb = jax.nn.sigmoid(ga_ref[...]), jax.nn.sigmoid(gb_ref[...])
        dza_ref[...] = (dmv * sa).astype(BF16)
        dzb_ref[...] = (dmv * sb).astype(BF16)
        dga_ref[...] = (dmv * za_ref[...] * sa * (1.0 - sa)).astype(BF16)
        dgb_ref[...] = (dmv * zb_ref[...] * sb * (1.0 - sb)).astype(BF16)

    blk = pl.BlockSpec((tt, GATE_TILE), lambda i, j: (i, j))
    out = jax.ShapeDtypeStruct((t, d), BF16)
    return pl.pallas_call(
        body, name=name, grid=(t // tt, nj),
        in_specs=[blk, pl.BlockSpec((tt, GATE_TILE), lambda i, j: (i, c0 + j)),
                  pl.BlockSpec((tt, GATE_TILE), lambda i, j: (i, c0 + nj + j)), blk, blk],
        out_specs=[blk] * 4, out_shape=[out] * 4,
        compiler_params=_params(dimension_semantics=("parallel", "parallel")),
    )(dm, proj, proj, za, zb)


def _adamw(w, g, m, v, *, name):
    shape = w.shape
    w2, g2, m2, v2 = (t.reshape(-1, shape[-1]) for t in (w, g, m, v))
    rows, cols = w2.shape
    tr = rows
    for cand in (512, 256, 128, 64, 32, 16, 8):
        if rows % cand == 0:
            tr = cand
            break

    def body(w_ref, g_ref, m_ref, v_ref, d_ref, nm_ref, nv_ref):
        gv = g_ref[...]
        nm = ADAM_B1 * m_ref[...] + (1.0 - ADAM_B1) * gv
        nv = ADAM_B2 * v_ref[...] + (1.0 - ADAM_B2) * (gv * gv)
        m_hat = nm / (1.0 - ADAM_B1 ** ADAM_STEP)
        v_hat = nv / (1.0 - ADAM_B2 ** ADAM_STEP)
        d_ref[...] = -ADAM_LR * (m_hat / (jnp.sqrt(v_hat) + ADAM_EPS) + ADAM_WD * w_ref[...])
        nm_ref[...] = nm
        nv_ref[...] = nv

    blk = pl.BlockSpec((tr, cols), lambda i: (i, 0))
    out = jax.ShapeDtypeStruct((rows, cols), F32)
    res = pl.pallas_call(
        body, name=name, grid=(rows // tr,), in_specs=[blk] * 4, out_specs=[blk] * 3, out_shape=[out] * 3,
        compiler_params=_params(dimension_semantics=("parallel",)),
    )(w2, g2, m2, v2)
    return tuple(t.reshape(shape) for t in res)


def _my_place():
    return lax.axis_index("x"), lax.axis_index("y"), lax.axis_index("c")


def _other_chips(x, y):
    return [(1 - x, y), (x, 1 - y), (1 - x, 1 - y)]


HBM = pl.BlockSpec(memory_space=pl.ANY)


def _gather_chip_shards(packed, *, name):
    def body(x_ref, out_ref, send_sems, recv_sems, local_sem):
        x, y, c = _my_place()
        mine = 2 * x + y
        sibling = (x, y, 1 - c)
        chips = _other_chips(x, y)

        def copy(k, chip_of_data, half, to, src=None):
            dst = out_ref.at[chip_of_data, half]
            return pltpu.make_async_remote_copy(src_ref=dst if src is None else src, dst_ref=dst, send_sem=send_sems.at[k],
                                                recv_sem=recv_sems.at[k], device_id=to, device_id_type=MESH)

        own = pltpu.make_async_copy(x_ref, out_ref.at[mine], local_sem)
        own.start()
        first = [copy(k, mine, c, (*chip, c), src=x_ref.at[c]) for k, chip in enumerate(chips)]
        for cp in first:
            cp.start()
        passed = [copy(3 + k, 2 * chip[0] + chip[1], c, sibling) for k, chip in enumerate(chips)]
        for k, chip in enumerate(chips):
            copy(k, 2 * chip[0] + chip[1], c, sibling).wait_recv()
            passed[k].start()
        for k, chip in enumerate(chips):
            copy(3 + k, 2 * chip[0] + chip[1], 1 - c, sibling).wait_recv()
        for cp in first + passed:
            cp.wait_send()
        own.wait()

    return pl.pallas_call(
        body, name=name, in_specs=[HBM], out_specs=HBM,
        out_shape=jax.ShapeDtypeStruct((N_CHIPS,) + packed.shape, packed.dtype),
        scratch_shapes=[pltpu.SemaphoreType.DMA((6,)), pltpu.SemaphoreType.DMA((6,)), pltpu.SemaphoreType.DMA],
        compiler_params=pltpu.CompilerParams(has_side_effects=True),
    )(packed)


def _send_to_sibling(src, *, name):
    def body(s_ref, out_ref, send_sem, recv_sem):
        x, y, c = _my_place()
        cp = pltpu.make_async_remote_copy(src_ref=s_ref.at[1 - c], dst_ref=out_ref, send_sem=send_sem,
                                          recv_sem=recv_sem, device_id=(x, y, 1 - c), device_id_type=MESH)
        cp.start()
        cp.wait()

    return pl.pallas_call(
        body, name=name, in_specs=[HBM], out_specs=HBM, out_shape=jax.ShapeDtypeStruct(src.shape[1:], src.dtype),
        scratch_shapes=[pltpu.SemaphoreType.DMA, pltpu.SemaphoreType.DMA],
        compiler_params=pltpu.CompilerParams(has_side_effects=True),
    )(src)


def _exchange_halves(half, *, name):
    def body(h_ref, out_ref, send_sem, recv_sem, local_sem):
        x, y, c = _my_place()
        own = pltpu.make_async_copy(h_ref, out_ref.at[c], local_sem)
        own.start()
        cp = pltpu.make_async_remote_copy(src_ref=h_ref, dst_ref=out_ref.at[c], send_sem=send_sem, recv_sem=recv_sem,
                                          device_id=(x, y, 1 - c), device_id_type=MESH)
        cp.start()
        pltpu.make_async_remote_copy(src_ref=h_ref, dst_ref=out_ref.at[1 - c], send_sem=send_sem, recv_sem=recv_sem,
                                     device_id=(x, y, 1 - c), device_id_type=MESH).wait_recv()
        cp.wait_send()
        own.wait()

    return pl.pallas_call(
        body, name=name, in_specs=[HBM], out_specs=HBM, out_shape=jax.ShapeDtypeStruct((2,) + half.shape, half.dtype),
        scratch_shapes=[pltpu.SemaphoreType.DMA, pltpu.SemaphoreType.DMA, pltpu.SemaphoreType.DMA],
        compiler_params=pltpu.CompilerParams(has_side_effects=True),
    )(half)


def _scatter_to_chips(parts, *, name):
    def body(p_ref, out_ref, send_sems, recv_sems, local_sem):
        x, y, c = _my_place()
        mine = 2 * x + y
        chips = _other_chips(x, y)
        own = pltpu.make_async_copy(p_ref.at[mine], out_ref.at[mine], local_sem)
        own.start()

        def copy(k, chip, src_slab, dst_slab):
            return pltpu.make_async_remote_copy(src_ref=p_ref.at[src_slab], dst_ref=out_ref.at[dst_slab], send_sem=send_sems.at[k],
                                                recv_sem=recv_sems.at[k], device_id=(*chip, c), device_id_type=MESH)

        sends = [copy(k, chip, 2 * chip[0] + chip[1], mine) for k, chip in enumerate(chips)]
        for cp in sends:
            cp.start()
        for k, chip in enumerate(chips):
            copy(k, chip, mine, 2 * chip[0] + chip[1]).wait_recv()
        for cp in sends:
            cp.wait_send()
        own.wait()

    return pl.pallas_call(
        body, name=name, in_specs=[HBM], out_specs=HBM, out_shape=jax.ShapeDtypeStruct(parts.shape, parts.dtype),
        scratch_shapes=[pltpu.SemaphoreType.DMA((3,)), pltpu.SemaphoreType.DMA((3,)), pltpu.SemaphoreType.DMA],
        compiler_params=pltpu.CompilerParams(has_side_effects=True),
    )(parts)


def _pair_add(mine2, other, *, tr, name):
    _, n, r, _ = mine2.shape
    c = lax.axis_index("c").astype(jnp.int32).reshape(1)

    def body(c_ref, a_ref, b_ref, o_ref):
        o_ref[...] = (a_ref[...].astype(F32) + b_ref[...].astype(F32)).astype(o_ref.dtype)

    return pl.pallas_call(
        body, name=name,
        grid_spec=pltpu.PrefetchScalarGridSpec(
            num_scalar_prefetch=1, grid=(n, r // tr),
            in_specs=[pl.BlockSpec((None, None, tr, LANES), lambda s, i, c_ref: (c_ref[0], s, i, 0)),
                      pl.BlockSpec((None, tr, LANES), lambda s, i, c_ref: (s, i, 0))],
            out_specs=pl.BlockSpec((None, tr, LANES), lambda s, i, c_ref: (s, i, 0))),
        out_shape=jax.ShapeDtypeStruct((n, r, LANES), mine2.dtype),
        compiler_params=_params(dimension_semantics=("parallel", "parallel")),
    )(c, mine2, other)


def _sum_slabs(slabs, *, tr, name):
    n, r, _ = slabs.shape

    def body(s_ref, o_ref):
        acc = s_ref[0].astype(F32)
        for i in range(1, n):
            acc = acc + s_ref[i].astype(F32)
        o_ref[...] = acc

    return pl.pallas_call(
        body, name=name, grid=(r // tr,), in_specs=[pl.BlockSpec((n, tr, LANES), lambda i: (0, i, 0))],
        out_specs=pl.BlockSpec((tr, LANES), lambda i: (i, 0)), out_shape=jax.ShapeDtypeStruct((r, LANES), F32),
        compiler_params=_params(dimension_semantics=("parallel",)),
    )(slabs)


def _all_sum_small(v, *, name):
    r = v.shape[0]
    relations = [(dx, dy, dc) for dx in (0, 1) for dy in (0, 1) for dc in (0, 1)][1:]

    def body(v_ref, o_ref, buf, send_sems, recv_sems):
        x, y, c = _my_place()
        me = 4 * x + 2 * y + c
        buf[me] = v_ref[...]
        peers = [(x + dx - 2 * x * dx, y + dy - 2 * y * dy, c + dc - 2 * c * dc) for dx, dy, dc in relations]

        def copy(k, slot):
            return pltpu.make_async_remote_copy(src_ref=v_ref, dst_ref=buf.at[slot], send_sem=send_sems.at[k],
                                                recv_sem=recv_sems.at[k], device_id=peers[k], device_id_type=MESH)

        sends = [copy(k, me) for k in range(len(relations))]
        for cp in sends:
            cp.start()
        for k, (px, py, pc) in enumerate(peers):
            copy(k, 4 * px + 2 * py + pc).wait_recv()
        for cp in sends:
            cp.wait_send()
        acc = buf[0]
        for i in range(1, 8):
            acc = acc + buf[i]
        o_ref[...] = acc

    vm = pl.BlockSpec(memory_space=pltpu.VMEM)
    return pl.pallas_call(
        body, name=name, in_specs=[vm], out_specs=vm, out_shape=jax.ShapeDtypeStruct((r, LANES), F32),
        scratch_shapes=[pltpu.VMEM((8, r, LANES), F32), pltpu.SemaphoreType.DMA((7,)), pltpu.SemaphoreType.DMA((7,))],
        compiler_params=pltpu.CompilerParams(has_side_effects=True),
    )(v)


SHARDED = (("ffn1_w_up", "col"), ("ffn1_w_down", "row"), ("w_in", "col"), ("w_branch_a", "col"),
           ("w_branch_b", "col"), ("w_out", "row"), ("ffn2_w_up", "col"), ("ffn2_w_down", "row"))
REPLICATED = ("ffn1_norm", "mix_norm", "na_rel_bias", "ffn2_norm", "final_norm")


def _unpack_full(flat4, shards):
    full, off = {}, 0
    for name, kind in SHARDED:
        shp = shards[name].shape
        n = int(np.prod(shp))
        seg = flat4[:, off:off + n].reshape((N_CHIPS,) + shp)
        off += n
        if kind == "col":
            full[name] = seg.transpose(1, 2, 0, 3).reshape(shp[0], shp[1], N_CHIPS * shp[2])
        else:
            full[name] = seg.transpose(1, 0, 2, 3).reshape(shp[0], N_CHIPS * shp[1], shp[2])
    return full


def _pack_by_chip(full_grads, shards):
    segs = []
    for name, kind in SHARDED:
        l, a, bb = shards[name].shape
        g = full_grads[name]
        if kind == "col":
            seg = g.reshape(l, a, N_CHIPS, bb).transpose(2, 0, 1, 3)
        else:
            seg = g.reshape(l, N_CHIPS, a, bb).transpose(1, 0, 2, 3)
        segs.append(seg.reshape(N_CHIPS, -1).astype(BF16))
    return jnp.concatenate(segs, axis=1)


def _div_tile(n, cap, mult=LANES):
    best = None
    for t in range(mult, min(n, cap) + 1, mult):
        if n % t == 0:
            best = t
    return n if best is None else best


def _ffn_fwd(x, norm_g, wg, wu, wd, tag):
    t, d = x.shape
    f = wd.shape[0]
    h = _rms_fwd(x, norm_g, tt=512, name=f"{tag}_norm")
    a, gate, up = _mm_swiglu_fwd(h, wg, wu, tm=512, tn=_div_tile(f, 1408), name=f"{tag}_up")
    x_out = _mm(a, wd, mode="nn", out_dtype=F32, tm=512, tn=d, tk=f, alpha=0.5, res=x, name=f"{tag}_down")
    return x_out, (x, h, a, gate, up)


def _ffn_bwd(dx, saved, norm_g, wg, wu, wd, tag):
    x, h, a, gate, up = saved
    t, d = x.shape
    f = wd.shape[0]
    dxb = dx.astype(BF16)
    d_wd = _mm(a, dxb, mode="tn", out_dtype=F32, tm=_div_tile(f, 1408), tn=d, tk=1024, alpha=0.5, name=f"{tag}_dwd")
    d_gate, d_up = _mm_swiglu_bwd(dxb, wd, gate, up, alpha=0.5, tm=512, tn=_div_tile(f, 1408), name=f"{tag}_da")
    tn = _div_tile(f, 1408)
    d_wg = _mm(h, d_gate, mode="tn", out_dtype=F32, tm=d, tn=tn, tk=1024, name=f"{tag}_dwg")
    d_wu = _mm(h, d_up, mode="tn", out_dtype=F32, tm=d, tn=tn, tk=1024, name=f"{tag}_dwu")
    dh = _mm(d_gate, wg, mode="nt", out_dtype=F32, tm=512, tn=d, tk=f, name=f"{tag}_dh1")
    dh = _mm(d_up, wu, mode="nt", out_dtype=F32, tm=512, tn=d, tk=f, res=dh, name=f"{tag}_dh2")
    dx_in, d_norm = _rms_bwd(dh, x, norm_g, dx, tt=512, name=f"{tag}_dnorm")
    return dx_in, d_norm, jnp.concatenate([d_wg, d_wu], axis=1), d_wd


def _to_heads(y, b, n_heads):
    t, w = y.shape
    return y.reshape(b, t // b, n_heads, HEAD_DIM).transpose(0, 2, 1, 3)


def _from_heads(y):
    b, n, s, hd = y.shape
    return y.transpose(0, 2, 1, 3).reshape(b * s, n * hd)


def _mixer_fwd(x, b, norm_g, w_in, bias, wa, wb, wo, tabs, tag):
    t, d = x.shape
    s = t // b
    n_qkv = 3 * (DIL_HEADS + NA_HEADS) * HEAD_DIM
    h = _rms_fwd(x, norm_g, tt=512, name=f"{tag}_norm")
    proj = _mm(h, w_in, mode="nn", out_dtype=F32, tm=512, tn=_div_tile(w_in.shape[1], 2944), tk=d, name=f"{tag}_in")
    heads = _split_heads(proj.reshape(b, s, -1), *tabs, n_pairs=n_qkv // LANES, rot_pairs=2 * DIL_HEADS // 2,
                         scale_ranges=((0, DIL_HEADS // 2), (3 * DIL_HEADS // 2, (3 * DIL_HEADS + NA_HEADS) // 2)),
                         ts=512, name=f"{tag}_heads")
    ya, lse_a = _dil_attn_fwd(heads, tq=256, name=f"{tag}_dil")
    yb, lse_b = _na_attn_fwd(heads, bias, first=3 * DIL_HEADS, name=f"{tag}_na")
    ya2, yb2 = _from_heads(ya), _from_heads(yb)
    za = _mm(ya2, wa, mode="nn", out_dtype=F32, tm=1024, tn=d, tk=wa.shape[0], name=f"{tag}_za")
    zb = _mm(yb2, wb, mode="nn", out_dtype=F32, tm=1024, tn=d, tk=wb.shape[0], name=f"{tag}_zb")
    merged = _gate_fwd(proj, za, zb, gate_col=n_qkv, tt=1024, name=f"{tag}_gate")
    x_out = _mm(merged, wo, mode="nn", out_dtype=F32, tm=1024, tn=d, tk=d, res=x, name=f"{tag}_out")
    return x_out, (x, h, proj, heads, ya, lse_a, yb, lse_b, ya2, yb2, za, zb, merged)


def _mixer_bwd(dx, b, saved, norm_g, w_in, bias, wa, wb, wo, tabs, tag):
    x, h, proj, heads, ya, lse_a, yb, lse_b, ya2, yb2, za, zb, merged = saved
    t, d = x.shape
    n_qkv = 3 * (DIL_HEADS + NA_HEADS) * HEAD_DIM
    dob = dx.astype(BF16)
    d_wo = _mm(merged, dob, mode="tn", out_dtype=F32, tm=d, tn=d, tk=1024, name=f"{tag}_dwo")
    dm = _mm(dob, wo, mode="nt", out_dtype=F32, tm=1024, tn=d, tk=d, name=f"{tag}_dm")
    dza, dzb, dga, dgb = _gate_bwd(dm, proj, za, zb, gate_col=n_qkv, tt=1024, name=f"{tag}_dgate")
    d_wa = _mm(ya2, dza, mode="tn", out_dtype=F32, tm=wa.shape[0], tn=d, tk=1024, name=f"{tag}_dwa")
    d_wb = _mm(yb2, dzb, mode="tn", out_dtype=F32, tm=wb.shape[0], tn=d, tk=1024, name=f"{tag}_dwb")
    dya = _mm(dza, wa, mode="nt", out_dtype=F32, tm=1024, tn=wa.shape[0], tk=d, name=f"{tag}_dya")
    dyb = _mm(dzb, wb, mode="nt", out_dtype=F32, tm=1024, tn=wb.shape[0], tk=d, name=f"{tag}_dyb")
    d_dil = _dil_attn_bwd(heads, ya, lse_a, _to_heads(dya, b, DIL_GROUP_HEADS), tq=256, name=f"{tag}_ddil")
    d_na, d_bias = _na_attn_bwd(heads, bias, yb, lse_b, _to_heads(dyb, b, NA_HEADS), first=3 * DIL_HEADS, name=f"{tag}_dna")
    dproj_a = _merge_heads(d_dil, *tabs, heads_per_row=DIL_GROUP_HEADS, rot_pairs=2 * DIL_HEADS // 2,
                           scale_pairs=DIL_HEADS // 2, ts=512, name=f"{tag}_dheads_a")
    dproj_b = _merge_heads(d_na, *tabs, heads_per_row=NA_HEADS, rot_pairs=0, scale_pairs=NA_HEADS // 2, ts=512,
                           name=f"{tag}_dheads_b")
    dproj = jnp.concatenate([dproj_a.reshape(t, -1), dproj_b.reshape(t, -1), dga, dgb], axis=1)
    n_in = w_in.shape[1]
    d_win = _mm(h, dproj, mode="tn", out_dtype=F32, tm=_div_tile(d, 512), tn=_div_tile(n_in, 2944), tk=1024, name=f"{tag}_dwin")
    dh = _mm(dproj, w_in, mode="nt", out_dtype=F32, tm=512, tn=d, tk=_div_tile(n_in, 2944), name=f"{tag}_dh")
    dx_in, d_norm = _rms_bwd(dh, x, norm_g, dx, tt=512, name=f"{tag}_dnorm")
    d_rb = _na_collapse_bias(d_bias, name=f"{tag}_dbias")
    return dx_in, d_norm, d_win, d_rb, d_wa, d_wb, d_wo


def kernel(x, ffn1_norm, ffn1_w_up, ffn1_w_down, mix_norm, w_in, na_rel_bias, w_branch_a, w_branch_b, w_out, ffn2_norm, ffn2_w_up, ffn2_w_down, final_norm, loss_target, m_ffn1_norm, m_ffn1_w_up, m_ffn1_w_down, m_mix_norm, m_w_in, m_na_rel_bias, m_w_branch_a, m_w_branch_b, m_w_out, m_ffn2_norm, m_ffn2_w_up, m_ffn2_w_down, m_final_norm, v_ffn1_norm, v_ffn1_w_up, v_ffn1_w_down, v_mix_norm, v_w_in, v_na_rel_bias, v_w_branch_a, v_w_branch_b, v_w_out, v_ffn2_norm, v_ffn2_w_up, v_ffn2_w_down, v_final_norm):
    w = dict(ffn1_norm=ffn1_norm, ffn1_w_up=ffn1_w_up, ffn1_w_down=ffn1_w_down, mix_norm=mix_norm, w_in=w_in,
             na_rel_bias=na_rel_bias, w_branch_a=w_branch_a, w_branch_b=w_branch_b, w_out=w_out, ffn2_norm=ffn2_norm,
             ffn2_w_up=ffn2_w_up, ffn2_w_down=ffn2_w_down, final_norm=final_norm)
    mom = dict(ffn1_norm=m_ffn1_norm, ffn1_w_up=m_ffn1_w_up, ffn1_w_down=m_ffn1_w_down, mix_norm=m_mix_norm, w_in=m_w_in,
               na_rel_bias=m_na_rel_bias, w_branch_a=m_w_branch_a, w_branch_b=m_w_branch_b, w_out=m_w_out,
               ffn2_norm=m_ffn2_norm, ffn2_w_up=m_ffn2_w_up, ffn2_w_down=m_ffn2_w_down, final_norm=m_final_norm)
    var = dict(ffn1_norm=v_ffn1_norm, ffn1_w_up=v_ffn1_w_up, ffn1_w_down=v_ffn1_w_down, mix_norm=v_mix_norm, w_in=v_w_in,
               na_rel_bias=v_na_rel_bias, w_branch_a=v_w_branch_a, w_branch_b=v_w_branch_b, w_out=v_w_out,
               ffn2_norm=v_ffn2_norm, ffn2_w_up=v_ffn2_w_up, ffn2_w_down=v_ffn2_w_down, final_norm=v_final_norm)
    b, s, d = x.shape
    t = b * s
    depth = ffn1_norm.shape[0]
    shards = {name: w[name] for name, _ in SHARDED}

    packed = jnp.concatenate([shards[name].astype(BF16).reshape(-1) for name, _ in SHARDED])
    n_packed = packed.shape[0]
    half_rows = n_packed // (2 * LANES)
    gathered = _gather_chip_shards(packed.reshape(2, half_rows, LANES), name="gather_weights")
    full = _unpack_full(gathered.reshape(N_CHIPS, n_packed), shards)
    f = full["ffn1_w_down"].shape[1]
    tabs = _rope_tables(s)
    bias = _na_expand_bias(na_rel_bias, name="na_bias")

    xc = x.reshape(t, d)
    saved = []
    for l in range(depth):
        lw = {k: v[l] for k, v in full.items()}
        xc, s1 = _ffn_fwd(xc, ffn1_norm[l:l + 1], lw["ffn1_w_up"][:, :f], lw["ffn1_w_up"][:, f:], lw["ffn1_w_down"], f"l{l}_ffn1")
        xc, s2 = _mixer_fwd(xc, b, mix_norm[l:l + 1], lw["w_in"], bias[l], lw["w_branch_a"], lw["w_branch_b"], lw["w_out"], tabs, f"l{l}_mix")
        xc, s3 = _ffn_fwd(xc, ffn2_norm[l:l + 1], lw["ffn2_w_up"][:, :f], lw["ffn2_w_up"][:, f:], lw["ffn2_w_down"], f"l{l}_ffn2")
        saved.append((s1, s2, s3))

    dx, d_final, loss_part = _final_loss(xc, final_norm.reshape(1, d), loss_target.reshape(t, d), tt=512, name="final_loss")
    grads = {name: [None] * depth for name in w if name != "final_norm"}
    for l in reversed(range(depth)):
        lw = {k: v[l] for k, v in full.items()}
        s1, s2, s3 = saved[l]
        dx, grads["ffn2_norm"][l], grads["ffn2_w_up"][l], grads["ffn2_w_down"][l] = _ffn_bwd(
            dx, s3, ffn2_norm[l:l + 1], lw["ffn2_w_up"][:, :f], lw["ffn2_w_up"][:, f:], lw["ffn2_w_down"], f"l{l}_ffn2")
        (dx, grads["mix_norm"][l], grads["w_in"][l], grads["na_rel_bias"][l], grads["w_branch_a"][l], grads["w_branch_b"][l],
         grads["w_out"][l]) = _mixer_bwd(dx, b, s2, mix_norm[l:l + 1], lw["w_in"], bias[l], lw["w_branch_a"], lw["w_branch_b"],
                                         lw["w_out"], tabs, f"l{l}_mix")
        dx, grads["ffn1_norm"][l], grads["ffn1_w_up"][l], grads["ffn1_w_down"][l] = _ffn_bwd(
            dx, s1, ffn1_norm[l:l + 1], lw["ffn1_w_up"][:, :f], lw["ffn1_w_up"][:, f:], lw["ffn1_w_down"], f"l{l}_ffn1")
    grad_x = dx.reshape(b, s, d)

    full_grads = {name: jnp.stack(grads[name]) for name, _ in SHARDED}
    by_chip = _pack_by_chip(full_grads, shards).reshape(N_CHIPS, 2, half_rows, LANES).transpose(1, 0, 2, 3)
    from_sibling = _send_to_sibling(by_chip, name="grads_to_sibling")
    pair = _pair_add(by_chip, from_sibling, tr=_div_tile(half_rows, 4096, 16), name="grads_pair_add")
    from_chips = _scatter_to_chips(pair, name="grads_to_chips")
    my_half = _sum_slabs(from_chips, tr=_div_tile(half_rows, 2048, 16), name="grads_chip_sum")
    flat_grad = _exchange_halves(my_half, name="grads_halves").reshape(-1)

    small = [jnp.stack(grads[name]).reshape(-1) for name in REPLICATED[:-1]] + [d_final.reshape(-1), loss_part[0, :1]]
    small_sizes = [v.shape[0] for v in small]
    small = jnp.concatenate(small)
    pad = -small.shape[0] % (8 * LANES)
    small_sum = _all_sum_small(jnp.pad(small, (0, pad)).reshape(-1, LANES), name="small_all_sum").reshape(-1)

    g_out, off = {}, 0
    for name, _ in SHARDED:
        n = int(np.prod(shards[name].shape))
        g_out[name] = flat_grad[off:off + n].reshape(shards[name].shape)
        off += n
    off = 0
    for name, n in zip(REPLICATED, small_sizes[:-1]):
        g_out[name] = small_sum[off:off + n].reshape(w[name].shape)
        off += n
    loss = small_sum[off]

    names = list(w)
    delta, new_m, new_v = {}, {}, {}
    for name in names:
        delta[name], new_m[name], new_v[name] = _adamw(w[name], g_out[name], mom[name], var[name], name=f"adamw_{name}")
    return (loss, grad_x, *[g_out[n] for n in names], *[delta[n] for n in names], *[new_m[n] for n in names],
            *[new_v[n] for n in names])
```

```python
import functools

import numpy as np
import jax
import jax.numpy as jnp
from jax import lax
from jax.experimental import pallas as pl
from jax.experimental.pallas import tpu as pltpu

F32, BF16 = jnp.float32, jnp.bfloat16
MESH = pl.DeviceIdType.MESH

HEAD_DIM = 64
DILATIONS = (1, 4, 16)
DIL_HALF = 64
DIL_GROUP_HEADS = 4
DIL_HEADS = 12
NA_HEADS = 8
GRID_W = 64
NA_ROWS = 8
NA_COLS = 16
ROPE_THETA = 10000.0
RMS_EPS = 1e-6
NEG_INF = -1e30
ADAM_LR, ADAM_B1, ADAM_B2, ADAM_EPS, ADAM_WD, ADAM_STEP = 0.001, 0.9, 0.999, 1e-08, 0.01, 10
QK_SCALE = HEAD_DIM ** -0.5

N_CHIPS = 4
LANES = 128
BF16_ROWS = 16
VMEM_LIMIT = 56 * 1024 * 1024

_NN = (((1,), (0,)), ((), ()))
_NT = (((1,), (1,)), ((), ()))
_TN = (((0,), (0,)), ((), ()))

HBM = pl.BlockSpec(memory_space=pl.ANY)


def _params(**kw):
    return pltpu.CompilerParams(vmem_limit_bytes=VMEM_LIMIT, **kw)


def _dot(a, b, dims):
    return lax.dot_general(a, b, dims, preferred_element_type=F32)


def _div_tile(n, cap, mult=LANES):
    best = None
    for t in range(mult, min(n, cap) + 1, mult):
        if n % t == 0:
            best = t
    return n if best is None else best


def _stacked(block, index, sel):
    if sel is None:
        return pl.BlockSpec(block, index)
    return pl.BlockSpec((None,) + block, lambda *g: (sel,) + index(*g))


def _mm(a, b, *, mode, out_dtype, tm, tn, tk, name, alpha=1.0, res=None, a_sel=None, b_sel=None, b_k_off=0,
        out_slab=None, out_cols=None, out_col_off=0, out_into=None):
    a2, b2 = a.shape[-2:], b.shape[-2:]
    if mode == "nn":
        (m, k), n = a2, b2[1]
        a_spec = _stacked((tm, tk), lambda i, j, kk: (i, kk), a_sel)
        b_spec = _stacked((tk, tn), lambda i, j, kk: (kk + b_k_off, j), b_sel)
        dims = _NN
    elif mode == "nt":
        (m, k), n = a2, b2[0]
        a_spec = _stacked((tm, tk), lambda i, j, kk: (i, kk), a_sel)
        b_spec = _stacked((tn, tk), lambda i, j, kk: (j, kk + b_k_off), b_sel)
        dims = _NT
    else:
        (k, m), n = a2, b2[1]
        a_spec = _stacked((tk, tm), lambda i, j, kk: (kk, i), a_sel)
        b_spec = _stacked((tk, tn), lambda i, j, kk: (kk + b_k_off, j), b_sel)
        dims = _TN
    assert m % tm == 0 and n % tn == 0 and k % tk == 0, (name, a.shape, b.shape)
    nk = k // tk
    has_res = res is not None
    if out_slab is None:
        o_spec = pl.BlockSpec((tm, tn), lambda i, j, kk: (i, j))
        out_shape = jax.ShapeDtypeStruct((m, n), out_dtype)
    else:
        o_spec = _stacked((tm, tn), lambda i, j, kk: (i, j + out_col_off), out_slab[0])
        out_shape = jax.ShapeDtypeStruct((out_slab[1], m, n if out_cols is None else out_cols), out_dtype)
    r_spec = pl.BlockSpec((tm, tn), lambda i, j, kk: (i, j))
    n_in = 2 + has_res + (out_into is not None)

    def body(*refs):
        a_ref, b_ref = refs[0], refs[1]
        r_ref = refs[2] if has_res else None
        o_ref = refs[n_in]
        p = _dot(a_ref[...], b_ref[...], dims)

        def finish(acc):
            y = acc * alpha if alpha != 1.0 else acc
            if has_res:
                y = y + r_ref[...].astype(F32)
            o_ref[...] = y.astype(o_ref.dtype)

        if nk == 1:
            finish(p)
        else:
            acc_ref = refs[n_in + 1]
            kk = pl.program_id(2)

            @pl.when(kk == 0)
            def _():
                acc_ref[...] = p

            @pl.when(kk > 0)
            def _():
                acc_ref[...] += p

            @pl.when(kk == nk - 1)
            def _():
                finish(acc_ref[...])

    operands = [a, b] + ([res] if has_res else [])
    in_specs = [a_spec, b_spec] + ([r_spec] if has_res else [])
    aliases = {}
    if out_into is not None:
        aliases = {len(operands): 0}
        operands.append(out_into)
        in_specs.append(HBM)
    return pl.pallas_call(
        body, name=name, grid=(m // tm, n // tn, nk), in_specs=in_specs, out_specs=o_spec, out_shape=out_shape,
        scratch_shapes=[pltpu.VMEM((tm, tn), F32)] if nk > 1 else [], input_output_aliases=aliases,
        compiler_params=_params(dimension_semantics=("parallel", "parallel", "arbitrary")),
    )(*operands)


def _mm_swiglu_fwd(h, w_up, layer, *, tm, tn, name):
    m, k = h.shape
    n = w_up.shape[2] // 2
    h_spec = pl.BlockSpec((tm, k), lambda i, j: (i, 0))
    wg_spec = pl.BlockSpec((None, k, tn), lambda i, j: (layer, 0, j))
    wu_spec = pl.BlockSpec((None, k, tn), lambda i, j: (layer, 0, j + n // tn))
    o_spec = pl.BlockSpec((tm, tn), lambda i, j: (i, j))

    def body(h_ref, wg_ref, wu_ref, a_ref, g_ref, u_ref):
        hb = h_ref[...]
        g = _dot(hb, wg_ref[...], _NN)
        u = _dot(hb, wu_ref[...], _NN)
        a_ref[...] = (g * jax.nn.sigmoid(g) * u).astype(BF16)
        g_ref[...] = g.astype(BF16)
        u_ref[...] = u.astype(BF16)

    out = jax.ShapeDtypeStruct((m, n), BF16)
    return pl.pallas_call(
        body, name=name, grid=(m // tm, n // tn), in_specs=[h_spec, wg_spec, wu_spec],
        out_specs=[o_spec] * 3, out_shape=[out] * 3,
        compiler_params=_params(dimension_semantics=("parallel", "parallel")),
    )(h, w_up, w_up)


def _mm_swiglu_bwd(dy, w_down, layer, gate, up, *, alpha, tm, tn, name):
    m, k = dy.shape
    n = w_down.shape[1]
    dy_spec = pl.BlockSpec((tm, k), lambda i, j: (i, 0))
    w_spec = pl.BlockSpec((None, tn, k), lambda i, j: (layer, j, 0))
    o_spec = pl.BlockSpec((tm, tn), lambda i, j: (i, j))

    def body(dy_ref, w_ref, g_ref, u_ref, dg_ref, du_ref):
        da = _dot(dy_ref[...], w_ref[...], _NT) * alpha
        g = g_ref[...].astype(F32)
        u = u_ref[...].astype(F32)
        sg = jax.nn.sigmoid(g)
        dg_ref[...] = (da * u * (sg * (1.0 + g * (1.0 - sg)))).astype(BF16)
        du_ref[...] = (da * (g * sg)).astype(BF16)

    out = jax.ShapeDtypeStruct((m, n), BF16)
    return pl.pallas_call(
        body, name=name, grid=(m // tm, n // tn), in_specs=[dy_spec, w_spec, o_spec, o_spec],
        out_specs=[o_spec] * 2, out_shape=[out] * 2,
        compiler_params=_params(dimension_semantics=("parallel", "parallel")),
    )(dy, w_down, gate, up)


def _rms_fwd(x, g, *, tt, name):
    t, d = x.shape

    def body(x_ref, g_ref, h_ref):
        xv = x_ref[...]
        rstd = lax.rsqrt(jnp.mean(xv * xv, axis=1, keepdims=True) + RMS_EPS)
        h_ref[...] = (xv * rstd * g_ref[...]).astype(BF16)

    return pl.pallas_call(
        body, name=name, grid=(t // tt,),
        in_specs=[pl.BlockSpec((tt, d), lambda i: (i, 0)), pl.BlockSpec((1, d), lambda i: (0, 0))],
        out_specs=pl.BlockSpec((tt, d), lambda i: (i, 0)), out_shape=jax.ShapeDtypeStruct((t, d), BF16),
        compiler_params=_params(dimension_semantics=("parallel",)),
    )(x, g)


def _rms_bwd(dh, x, g, dres, *, tt, name):
    t, d = x.shape

    def body(dh_ref, x_ref, g_ref, r_ref, dx_ref, dg_ref):
        xv = x_ref[...]
        rstd = lax.rsqrt(jnp.mean(xv * xv, axis=1, keepdims=True) + RMS_EPS)
        xhat = xv * rstd
        dhv = dh_ref[...]
        dxhat = dhv * g_ref[...]
        dx_ref[...] = r_ref[...] + rstd * (dxhat - xhat * jnp.mean(dxhat * xhat, axis=1, keepdims=True))

        @pl.when(pl.program_id(0) == 0)
        def _():
            dg_ref[...] = jnp.zeros_like(dg_ref)

        dg_ref[...] += jnp.sum(dhv * xhat, axis=0, keepdims=True)

    row = pl.BlockSpec((tt, d), lambda i: (i, 0))
    vec = pl.BlockSpec((1, d), lambda i: (0, 0))
    return pl.pallas_call(
        body, name=name, grid=(t // tt,), in_specs=[row, row, vec, row], out_specs=[row, vec],
        out_shape=[jax.ShapeDtypeStruct((t, d), F32), jax.ShapeDtypeStruct((1, d), F32)],
        compiler_params=_params(dimension_semantics=("arbitrary",)),
    )(dh, x, g, dres)


def _final_loss(x, g, target, *, tt, name):
    t, d = x.shape

    def body(x_ref, g_ref, t_ref, dx_ref, dg_ref, loss_ref):
        xv = x_ref[...]
        gv = g_ref[...]
        rstd = lax.rsqrt(jnp.mean(xv * xv, axis=1, keepdims=True) + RMS_EPS)
        xhat = xv * rstd
        err = xhat * gv - t_ref[...]
        dy = err * (1.0 / d)
        dxhat = dy * gv
        dx_ref[...] = rstd * (dxhat - xhat * jnp.mean(dxhat * xhat, axis=1, keepdims=True))

        @pl.when(pl.program_id(0) == 0)
        def _():
            dg_ref[...] = jnp.zeros_like(dg_ref)
            loss_ref[...] = jnp.zeros_like(loss_ref)

        dg_ref[...] += jnp.sum(dy * xhat, axis=0, keepdims=True)
        part = 0.5 * jnp.sum(jnp.mean(err * err, axis=1, keepdims=True), axis=0, keepdims=True)
        loss_ref[...] += jnp.broadcast_to(part, loss_ref.shape)

    row = pl.BlockSpec((tt, d), lambda i: (i, 0))
    vec = pl.BlockSpec((1, d), lambda i: (0, 0))
    one = pl.BlockSpec((1, LANES), lambda i: (0, 0))
    return pl.pallas_call(
        body, name=name, grid=(t // tt,), in_specs=[row, vec, row], out_specs=[row, vec, one],
        out_shape=[jax.ShapeDtypeStruct((t, d), F32), jax.ShapeDtypeStruct((1, d), F32),
                   jax.ShapeDtypeStruct((1, LANES), F32)],
        compiler_params=_params(dimension_semantics=("arbitrary",)),
    )(x, g, target)


def _swap_halves(x):
    lane = lax.broadcasted_iota(jnp.int32, x.shape, 1)
    return jnp.where((lane // 32) % 2 == 0, pltpu.roll(x, 96, 1), pltpu.roll(x, 32, 1))


def _rope_tables(s):
    half = HEAD_DIM // 2
    inv_freq = ROPE_THETA ** (-jnp.arange(half, dtype=F32) / half)
    ang = jnp.arange(s).astype(F32)[:, None] * inv_freq[None, :]
    cos, sin = jnp.cos(ang), jnp.sin(ang)
    return jnp.tile(cos, (1, 4)), jnp.concatenate([-sin, sin, -sin, sin], axis=1)


def _split_heads(proj, cos4, sin4, *, n_pairs, rot_pairs, scale_ranges, ts, name):
    b, s, _ = proj.shape

    def body(x_ref, c_ref, s_ref, o_ref):
        x = x_ref[...]
        p = pl.program_id(1)
        rot = x * c_ref[...] + _swap_halves(x) * s_ref[...]
        y = jnp.where(p < rot_pairs, rot, x)
        is_q = functools.reduce(jnp.logical_or, [(p >= lo) & (p < hi) for lo, hi in scale_ranges])
        y = y * jnp.where(is_q, QK_SCALE, 1.0)
        o_ref[0] = y[:, :HEAD_DIM].astype(BF16)
        o_ref[1] = y[:, HEAD_DIM:].astype(BF16)

    tab = pl.BlockSpec((ts, LANES), lambda bi, p, si: (si, 0))
    return pl.pallas_call(
        body, name=name, grid=(b, n_pairs, s // ts),
        in_specs=[pl.BlockSpec((None, ts, LANES), lambda bi, p, si: (bi, si, p)), tab, tab],
        out_specs=pl.BlockSpec((None, 2, ts, HEAD_DIM), lambda bi, p, si: (bi, p, si, 0)),
        out_shape=jax.ShapeDtypeStruct((b, 2 * n_pairs, s, HEAD_DIM), BF16),
        compiler_params=_params(dimension_semantics=("parallel", "parallel", "parallel")),
    )(proj, cos4, sin4)


def _merge_heads(dheads, cos4, sin4, *, heads_per_row, rot_pairs, scale_pairs, ts, out_cols, tile_off, into, name):
    b, hpr, r, s, _ = dheads.shape
    n_pairs = hpr * r // 2
    ppr = hpr // 2

    def body(d_ref, c_ref, s_ref, *rest):
        o_ref = rest[-1]
        dy = jnp.concatenate([d_ref[0], d_ref[1]], axis=1)
        p = pl.program_id(1)
        rot = dy * c_ref[...] - _swap_halves(dy) * s_ref[...]
        dx = jnp.where(p < rot_pairs, rot, dy)
        dx = dx * jnp.where(p < scale_pairs, QK_SCALE, 1.0)
        o_ref[...] = dx.astype(BF16)

    tab = pl.BlockSpec((ts, LANES), lambda bi, p, si: (si, 0))
    operands = [dheads, cos4, sin4] + ([] if into is None else [into])
    return pl.pallas_call(
        body, name=name, grid=(b, n_pairs, s // ts),
        in_specs=[pl.BlockSpec((None, 2, None, ts, HEAD_DIM), lambda bi, p, si: (bi, p % ppr, p // ppr, si, 0)), tab, tab]
        + ([] if into is None else [HBM]),
        out_specs=pl.BlockSpec((None, ts, LANES), lambda bi, p, si: (bi, si, p + tile_off)),
        out_shape=jax.ShapeDtypeStruct((b, s, out_cols), BF16),
        input_output_aliases={} if into is None else {3: 0},
        compiler_params=_params(dimension_semantics=("parallel", "parallel", "parallel")),
    )(*operands)


def _dil_window(g, q0, tq, s):
    pad = -(-DIL_HALF * DILATIONS[g] // LANES) * LANES
    width = tq + 2 * pad
    if width >= s:
        return 0, s
    return pl.multiple_of(jnp.clip(q0 - pad, 0, s - width), LANES), width


def _dil_mask(g, q0, start, shape):
    d = DILATIONS[g]
    diff = (q0 - start) + lax.broadcasted_iota(jnp.int32, shape, 0) - lax.broadcasted_iota(jnp.int32, shape, 1)
    ok = jnp.abs(diff) <= DIL_HALF * d
    if d > 1:
        ok = ok & ((diff & (d - 1)) == 0)
    return ok


def _dil_head_spec(part, g, s):
    return pl.BlockSpec((None, None, s, HEAD_DIM), lambda b, j: (b, part * DIL_HEADS + g * DIL_GROUP_HEADS + j, 0, 0))


def _dil_attn_fwd(heads, *, tq, name):
    b, _, s, _ = heads.shape
    n_g = len(DILATIONS)

    def body(*refs):
        qkv = refs[:3 * n_g]
        o_ref, l_ref = refs[3 * n_g:]

        def step(i, carry):
            q0 = pl.multiple_of(i * tq, tq)
            scores, wins = [], []
            for g in range(n_g):
                start, width = _dil_window(g, q0, tq, s)
                sc = _dot(qkv[3 * g][pl.ds(q0, tq), :], qkv[3 * g + 1][pl.ds(start, width), :], _NT)
                scores.append(jnp.where(_dil_mask(g, q0, start, sc.shape), sc, NEG_INF))
                wins.append((start, width))
            m = functools.reduce(jnp.maximum, [jnp.max(sc, axis=1, keepdims=True) for sc in scores])
            den = jnp.zeros((tq, 1), F32)
            acc = jnp.zeros((tq, HEAD_DIM), F32)
            for g in range(n_g):
                p = jnp.exp(scores[g] - m)
                den = den + jnp.sum(p, axis=1, keepdims=True)
                acc = acc + _dot(p.astype(BF16), qkv[3 * g + 2][pl.ds(*wins[g]), :], _NN)
            o_ref[pl.ds(q0, tq), :] = (acc / den).astype(o_ref.dtype)
            l_ref[pl.ds(q0, tq), :] = m + jnp.log(den)
            return carry

        lax.fori_loop(0, s // tq, step, 0)

    out = pl.BlockSpec((None, None, s, HEAD_DIM), lambda bi, j: (bi, j, 0, 0))
    lse = pl.BlockSpec((None, None, s, 1), lambda bi, j: (bi, j, 0, 0))
    return pl.pallas_call(
        body, name=name, grid=(b, DIL_GROUP_HEADS),
        in_specs=[_dil_head_spec(part, g, s) for g in range(n_g) for part in range(3)],
        out_specs=[out, lse],
        out_shape=[jax.ShapeDtypeStruct((b, DIL_GROUP_HEADS, s, HEAD_DIM), BF16),
                   jax.ShapeDtypeStruct((b, DIL_GROUP_HEADS, s, 1), F32)],
        compiler_params=_params(dimension_semantics=("parallel", "parallel")),
    )(*([heads] * (3 * n_g)))


def _dil_attn_bwd(heads, out, lse, dout, *, tq, name):
    b, _, s, _ = heads.shape
    n_g = len(DILATIONS)

    def body(*refs):
        qkv = refs[:3 * n_g]
        o_ref, l_ref, do_ref, d_ref = refs[3 * n_g:]
        d_ref[...] = jnp.zeros_like(d_ref)

        def step(i, carry):
            q0 = pl.multiple_of(i * tq, tq)
            do = do_ref[pl.ds(q0, tq), :]
            delta = jnp.sum(do * o_ref[pl.ds(q0, tq), :].astype(F32), axis=1, keepdims=True)
            lse_b = l_ref[pl.ds(q0, tq), :]
            do_b = do.astype(BF16)
            for g in range(n_g):
                start, width = _dil_window(g, q0, tq, s)
                win = pl.ds(start, width)
                q = qkv[3 * g][pl.ds(q0, tq), :]
                k = qkv[3 * g + 1][win, :]
                v = qkv[3 * g + 2][win, :]
                sc = _dot(q, k, _NT)
                p = jnp.where(_dil_mask(g, q0, start, sc.shape), jnp.exp(sc - lse_b), 0.0)
                ds = (p * (_dot(do_b, v, _NT) - delta)).astype(BF16)
                d_ref[g, pl.ds(q0, tq), :] = _dot(ds, k, _NN)
                d_ref[n_g + g, win, :] += _dot(ds, q, _TN)
                d_ref[2 * n_g + g, win, :] += _dot(p.astype(BF16), do_b, _TN)
            return carry

        lax.fori_loop(0, s // tq, step, 0)

    per_head = lambda bi, j: (bi, j, 0, 0)
    return pl.pallas_call(
        body, name=name, grid=(b, DIL_GROUP_HEADS),
        in_specs=[_dil_head_spec(part, g, s) for g in range(n_g) for part in range(3)]
        + [pl.BlockSpec((None, None, s, HEAD_DIM), per_head), pl.BlockSpec((None, None, s, 1), per_head),
           pl.BlockSpec((None, None, s, HEAD_DIM), per_head)],
        out_specs=pl.BlockSpec((None, None, 3 * n_g, s, HEAD_DIM), lambda bi, j: (bi, j, 0, 0, 0)),
        out_shape=jax.ShapeDtypeStruct((b, DIL_GROUP_HEADS, 3 * n_g, s, HEAD_DIM), F32),
        compiler_params=_params(dimension_semantics=("parallel", "parallel")),
    )(*([heads] * (3 * n_g)), out, lse, dout)


NA_OFFSETS = NA_ROWS
NA_BIAS_ROWS = 2 * NA_ROWS - 1
NA_BIAS_COLS = 2 * NA_COLS - 1
NA_KEYS = NA_ROWS * GRID_W
NA_UNROLL = 2


def _na_onehot():
    c = np.arange(GRID_W)[:, None]
    k = np.arange(GRID_W)[None, :]
    lo = np.clip(c - NA_COLS // 2, 0, GRID_W - NA_COLS)
    valid = (k >= lo) & (k < lo + NA_COLS)
    onehot = np.zeros((GRID_W, GRID_W, LANES), np.float32)
    cc, kk = np.nonzero(valid)
    onehot[cc, kk, kk - cc + NA_COLS - 1] = 1.0
    return onehot.reshape(GRID_W * GRID_W, LANES), valid.reshape(1, GRID_W * GRID_W)


def _na_expand_bias(rel_bias, *, name):
    l, h, nr, nc = rel_bias.shape
    onehot, valid = _na_onehot()
    rows = l * h * nr
    rb = jnp.pad(rel_bias.reshape(rows, nc), ((0, 0), (0, LANES - nc)))

    def body(rb_ref, oh_ref, valid_ref, e_ref):
        e = lax.dot_general(rb_ref[...], oh_ref[...], _NT, precision=lax.Precision.HIGHEST, preferred_element_type=F32)
        e_ref[...] = jnp.where(valid_ref[...] > 0, e, NEG_INF)

    e = pl.pallas_call(
        body, name=name, out_shape=jax.ShapeDtypeStruct((rows, GRID_W * GRID_W), F32), compiler_params=_params(),
    )(rb, jnp.asarray(onehot), jnp.asarray(valid.astype(np.float32)))
    e = e.reshape(l, h, nr, GRID_W, GRID_W)
    by_off = jnp.stack([e[:, :, off:off + NA_ROWS] for off in range(NA_OFFSETS)], axis=2)
    return by_off.transpose(0, 1, 2, 4, 3, 5).reshape(l, h, NA_OFFSETS, GRID_W, NA_KEYS)


def _na_collapse_bias(dbias, *, name):
    b, h = dbias.shape[:2]
    onehot, _ = _na_onehot()

    def fold(d_ref, e_ref):
        acc = [jnp.zeros((GRID_W, GRID_W), F32) for _ in range(NA_BIAS_ROWS)]
        for bi in range(b):
            for off in range(NA_OFFSETS):
                for kr in range(NA_ROWS):
                    acc[off + kr] = acc[off + kr] + d_ref[bi, off, :, kr * GRID_W:(kr + 1) * GRID_W]
        for i in range(NA_BIAS_ROWS):
            e_ref[i] = acc[i]

    de = pl.pallas_call(
        fold, name=name + "_fold", grid=(h,),
        in_specs=[pl.BlockSpec((b, None, NA_OFFSETS, GRID_W, NA_KEYS), lambda hi: (0, hi, 0, 0, 0))],
        out_specs=pl.BlockSpec((None, NA_BIAS_ROWS, GRID_W, GRID_W), lambda hi: (hi, 0, 0, 0)),
        out_shape=jax.ShapeDtypeStruct((h, NA_BIAS_ROWS, GRID_W, GRID_W), F32),
        compiler_params=_params(dimension_semantics=("parallel",)),
    )(dbias)

    def diag(e_ref, oh_ref, o_ref):
        o_ref[...] = lax.dot_general(e_ref[...], oh_ref[...], _NN, precision=lax.Precision.HIGHEST, preferred_element_type=F32)

    rows = h * NA_BIAS_ROWS
    drb = pl.pallas_call(
        diag, name=name + "_diag", out_shape=jax.ShapeDtypeStruct((rows, LANES), F32), compiler_params=_params(),
    )(de.reshape(rows, GRID_W * GRID_W), jnp.asarray(onehot))
    return drb[:, :NA_BIAS_COLS].reshape(h, NA_BIAS_ROWS, NA_BIAS_COLS)


def _na_row(r, n_rows):
    row_lo = jnp.clip(r - NA_ROWS // 2, 0, n_rows - NA_ROWS)
    return row_lo, row_lo - r + NA_ROWS - 1


def _na_head_spec(part, first, s):
    return pl.BlockSpec((None, None, s, HEAD_DIM), lambda b, h: (b, first + part * NA_HEADS + h, 0, 0))


def _na_attn_fwd(heads, bias, *, first, name):
    b, _, s, _ = heads.shape
    n_rows = s // GRID_W

    def body(q_ref, k_ref, v_ref, b_ref, o_ref, l_ref):
        def step(r, carry):
            row_lo, off = _na_row(r, n_rows)
            rows = pl.ds(pl.multiple_of(r * GRID_W, GRID_W), GRID_W)
            win = pl.ds(pl.multiple_of(row_lo * GRID_W, GRID_W), NA_KEYS)
            sc = _dot(q_ref[rows, :], k_ref[win, :], _NT) + b_ref[off]
            m = jnp.max(sc, axis=1, keepdims=True)
            p = jnp.exp(sc - m)
            den = jnp.sum(p, axis=1, keepdims=True)
            o_ref[rows, :] = (_dot(p.astype(BF16), v_ref[win, :], _NN) / den).astype(o_ref.dtype)
            l_ref[rows, :] = m + jnp.log(den)
            return carry

        lax.fori_loop(0, n_rows, step, 0, unroll=NA_UNROLL)

    per_head = lambda bi, h: (bi, h, 0, 0)
    return pl.pallas_call(
        body, name=name, grid=(b, NA_HEADS),
        in_specs=[_na_head_spec(part, first, s) for part in range(3)]
        + [pl.BlockSpec((None, NA_OFFSETS, GRID_W, NA_KEYS), lambda bi, h: (h, 0, 0, 0))],
        out_specs=[pl.BlockSpec((None, None, s, HEAD_DIM), per_head), pl.BlockSpec((None, None, s, 1), per_head)],
        out_shape=[jax.ShapeDtypeStruct((b, NA_HEADS, s, HEAD_DIM), BF16), jax.ShapeDtypeStruct((b, NA_HEADS, s, 1), F32)],
        compiler_params=_params(dimension_semantics=("parallel", "parallel")),
    )(heads, heads, heads, bias)


def _na_attn_bwd(heads, bias, out, lse, dout, *, first, name):
    b, _, s, _ = heads.shape
    n_rows = s // GRID_W

    def body(q_ref, k_ref, v_ref, b_ref, o_ref, l_ref, do_ref, d_ref, db_ref):
        d_ref[...] = jnp.zeros_like(d_ref)
        db_ref[...] = jnp.zeros_like(db_ref)

        def step(r, carry):
            row_lo, off = _na_row(r, n_rows)
            rows = pl.ds(pl.multiple_of(r * GRID_W, GRID_W), GRID_W)
            win = pl.ds(pl.multiple_of(row_lo * GRID_W, GRID_W), NA_KEYS)
            q, k, v = q_ref[rows, :], k_ref[win, :], v_ref[win, :]
            do = do_ref[rows, :]
            delta = jnp.sum(do * o_ref[rows, :].astype(F32), axis=1, keepdims=True)
            do_b = do.astype(BF16)
            p = jnp.exp(_dot(q, k, _NT) + b_ref[off] - l_ref[rows, :])
            ds = p * (_dot(do_b, v, _NT) - delta)
            db_ref[off] += ds
            ds_b = ds.astype(BF16)
            d_ref[0, rows, :] = _dot(ds_b, k, _NN)
            d_ref[1, win, :] += _dot(ds_b, q, _TN)
            d_ref[2, win, :] += _dot(p.astype(BF16), do_b, _TN)
            return carry

        lax.fori_loop(0, n_rows, step, 0, unroll=NA_UNROLL)

    per_head = lambda bi, h: (bi, h, 0, 0)
    return pl.pallas_call(
        body, name=name, grid=(b, NA_HEADS),
        in_specs=[_na_head_spec(part, first, s) for part in range(3)]
        + [pl.BlockSpec((None, NA_OFFSETS, GRID_W, NA_KEYS), lambda bi, h: (h, 0, 0, 0)),
           pl.BlockSpec((None, None, s, HEAD_DIM), per_head), pl.BlockSpec((None, None, s, 1), per_head),
           pl.BlockSpec((None, None, s, HEAD_DIM), per_head)],
        out_specs=[pl.BlockSpec((None, None, 3, s, HEAD_DIM), lambda bi, h: (bi, h, 0, 0, 0)),
                   pl.BlockSpec((None, None, NA_OFFSETS, GRID_W, NA_KEYS), lambda bi, h: (bi, h, 0, 0, 0))],
        out_shape=[jax.ShapeDtypeStruct((b, NA_HEADS, 3, s, HEAD_DIM), F32),
                   jax.ShapeDtypeStruct((b, NA_HEADS, NA_OFFSETS, GRID_W, NA_KEYS), F32)],
        compiler_params=_params(dimension_semantics=("parallel", "parallel")),
    )(heads, heads, heads, bias, out, lse, dout)


GATE_TILE = 256


def _gate_fwd(proj, z, *, gate_col, tt, name):
    _, t, d = z.shape
    nj = d // GATE_TILE
    c0 = gate_col // GATE_TILE

    def body(ga_ref, gb_ref, za_ref, zb_ref, o_ref):
        o_ref[...] = (jax.nn.sigmoid(ga_ref[...]) * za_ref[...] + jax.nn.sigmoid(gb_ref[...]) * zb_ref[...]).astype(BF16)

    return pl.pallas_call(
        body, name=name, grid=(t // tt, nj),
        in_specs=[pl.BlockSpec((tt, GATE_TILE), lambda i, j: (i, c0 + j)),
                  pl.BlockSpec((tt, GATE_TILE), lambda i, j: (i, c0 + nj + j)),
                  pl.BlockSpec((None, tt, GATE_TILE), lambda i, j: (0, i, j)),
                  pl.BlockSpec((None, tt, GATE_TILE), lambda i, j: (1, i, j))],
        out_specs=pl.BlockSpec((tt, GATE_TILE), lambda i, j: (i, j)), out_shape=jax.ShapeDtypeStruct((t, d), BF16),
        compiler_params=_params(dimension_semantics=("parallel", "parallel")),
    )(proj, proj, z, z)


def _gate_bwd(dm, proj, z, *, gate_col, tt, name):
    _, t, d = z.shape
    nj = d // GATE_TILE
    c0 = gate_col // GATE_TILE

    def body(dm_ref, g_ref, z_ref, dz_ref, dg_ref):
        dmv = dm_ref[...]
        sg = jax.nn.sigmoid(g_ref[...])
        dz_ref[...] = (dmv * sg).astype(BF16)
        dg_ref[...] = (dmv * z_ref[...] * sg * (1.0 - sg)).astype(BF16)

    return pl.pallas_call(
        body, name=name, grid=(t // tt, 2 * nj),
        in_specs=[pl.BlockSpec((tt, GATE_TILE), lambda i, j: (i, j % nj)),
                  pl.BlockSpec((tt, GATE_TILE), lambda i, j: (i, c0 + j)),
                  pl.BlockSpec((None, tt, GATE_TILE), lambda i, j: (j // nj, i, j % nj))],
        out_specs=[pl.BlockSpec((None, tt, GATE_TILE), lambda i, j: (j // nj, i, j % nj)),
                   pl.BlockSpec((tt, GATE_TILE), lambda i, j: (i, c0 + j))],
        out_shape=[jax.ShapeDtypeStruct((2, t, d), BF16), jax.ShapeDtypeStruct(proj.shape, BF16)],
        compiler_params=_params(dimension_semantics=("parallel", "parallel")),
    )(dm, proj, z)


def _adamw(w, g, m, v, *, name):
    shape = w.shape
    w2, g2, m2, v2 = (t.reshape(-1, shape[-1]) for t in (w, g, m, v))
    rows, cols = w2.shape
    tr = rows
    for cand in (512, 256, 128, 64, 32, 16, 8):
        if rows % cand == 0:
            tr = cand
            break

    def body(w_ref, g_ref, m_ref, v_ref, d_ref, nm_ref, nv_ref):
        gv = g_ref[...]
        nm = ADAM_B1 * m_ref[...] + (1.0 - ADAM_B1) * gv
        nv = ADAM_B2 * v_ref[...] + (1.0 - ADAM_B2) * (gv * gv)
        m_hat = nm / (1.0 - ADAM_B1 ** ADAM_STEP)
        v_hat = nv / (1.0 - ADAM_B2 ** ADAM_STEP)
        d_ref[...] = -ADAM_LR * (m_hat / (jnp.sqrt(v_hat) + ADAM_EPS) + ADAM_WD * w_ref[...])
        nm_ref[...] = nm
        nv_ref[...] = nv

    blk = pl.BlockSpec((tr, cols), lambda i: (i, 0))
    out = jax.ShapeDtypeStruct((rows, cols), F32)
    res = pl.pallas_call(
        body, name=name, grid=(rows // tr,), in_specs=[blk] * 4, out_specs=[blk] * 3, out_shape=[out] * 3,
        compiler_params=_params(dimension_semantics=("parallel",)),
    )(w2, g2, m2, v2)
    return tuple(t.reshape(shape) for t in res)


def _my_place():
    return lax.axis_index("x"), lax.axis_index("y"), lax.axis_index("c")


def _other_chips(x, y):
    return [(1 - x, y), (x, 1 - y), (1 - x, 1 - y)]


def _chip_no(chip):
    return 2 * chip[0] + chip[1]


def _window(ref, kind, size, chip, lead):
    if kind == "col":
        return ref.at[(*lead, slice(None), pl.ds(pl.multiple_of(chip * size, LANES), size))]
    if kind == "row":
        return ref.at[(*lead, pl.ds(pl.multiple_of(chip * size, BF16_ROWS), size), slice(None))]
    shard = size + HEAD_DIM
    if kind == "win_main":
        return ref.at[(*lead, slice(None), pl.ds(pl.multiple_of(chip * shard + HEAD_DIM * (chip % 2), LANES), size))]
    assert kind == "win_strad"
    return ref.at[(*lead, slice(None), pl.ds(pl.multiple_of(size + 2 * shard * (chip // 2), LANES), LANES))]


def _gather_weights(shards, kinds, *, name):
    n_w = len(shards)

    def full_shape(sh, kind):
        l, k, n = sh.shape
        return {"col": (l, k, N_CHIPS * n), "row": (l, N_CHIPS * k, n), "win_main": (l, k, N_CHIPS * (n + HEAD_DIM)),
                "slot": (N_CHIPS, l, k, n)}[kind]

    def body(*refs):
        src, dst = refs[:n_w], refs[n_w:2 * n_w]
        send_sems, recv_sems, local_sems = refs[2 * n_w:]
        x, y, c = _my_place()
        mine = 2 * x + y
        sibling = (x, y, 1 - c)
        chips = _other_chips(x, y)

        def win(i, chip, layer):
            if kinds[i] == "slot":
                return dst[i].at[chip] if layer is None else dst[i].at[chip, layer]
            size = shards[i].shape[1] if kinds[i] == "row" else shards[i].shape[2]
            return _window(dst[i], kinds[i], size, chip, (slice(None),) if layer is None else (layer,))

        def copy(sem, window, to, source=None):
            return pltpu.make_async_remote_copy(src_ref=window if source is None else source, dst_ref=window,
                                                send_sem=send_sems.at[sem], recv_sem=recv_sems.at[sem],
                                                device_id=to, device_id_type=MESH)

        own = [pltpu.make_async_copy(src[i], win(i, mine, None), local_sems.at[i]) for i in range(n_w)]
        for cp in own:
            cp.start()
        first = [copy(3 * i + k, win(i, mine, c), (*chip, c), source=src[i].at[c])
                 for k, chip in enumerate(chips) for i in range(n_w)]
        for cp in first:
            cp.start()
        passed = []
        for k, chip in enumerate(chips):
            for i in range(n_w):
                landed = win(i, _chip_no(chip), c)
                copy(3 * i + k, landed, sibling).wait_recv()
                passed.append(copy(3 * n_w + 3 * i + k, landed, sibling))
                passed[-1].start()
        for k, chip in enumerate(chips):
            for i in range(n_w):
                copy(3 * n_w + 3 * i + k, win(i, _chip_no(chip), 1 - c), sibling).wait_recv()
        for cp in first + passed:
            cp.wait_send()
        for cp in own:
            cp.wait()

    return pl.pallas_call(
        body, name=name, in_specs=[HBM] * n_w, out_specs=[HBM] * n_w,
        out_shape=[jax.ShapeDtypeStruct(full_shape(sh, kind), sh.dtype) for sh, kind in zip(shards, kinds)],
        scratch_shapes=[pltpu.SemaphoreType.DMA((6 * n_w,)), pltpu.SemaphoreType.DMA((6 * n_w,)),
                        pltpu.SemaphoreType.DMA((n_w,))],
    )(*shards)


def _grads_to_sibling(grads, *, name):
    n_w = len(grads)

    def body(*refs):
        src, dst = refs[:n_w], refs[n_w:2 * n_w]
        send_sems, recv_sems = refs[2 * n_w:]
        x, y, c = _my_place()
        cps = [pltpu.make_async_remote_copy(src_ref=src[i].at[1 - c], dst_ref=dst[i], send_sem=send_sems.at[i],
                                            recv_sem=recv_sems.at[i], device_id=(x, y, 1 - c), device_id_type=MESH)
               for i in range(n_w)]
        for cp in cps:
            cp.start()
        for cp in cps:
            cp.wait()

    return pl.pallas_call(
        body, name=name, in_specs=[HBM] * n_w, out_specs=[HBM] * n_w,
        out_shape=[jax.ShapeDtypeStruct(g.shape[1:], g.dtype) for g in grads],
        scratch_shapes=[pltpu.SemaphoreType.DMA((n_w,)), pltpu.SemaphoreType.DMA((n_w,))],
    )(*grads)


def _pair_add(mine2, other, *, name):
    _, k, n = mine2.shape
    tr = _div_tile(k, 512, BF16_ROWS)
    c = lax.axis_index("c").astype(jnp.int32).reshape(1)

    def body(c_ref, a_ref, b_ref, o_ref):
        o_ref[...] = (a_ref[...].astype(F32) + b_ref[...].astype(F32)).astype(o_ref.dtype)

    return pl.pallas_call(
        body, name=name,
        grid_spec=pltpu.PrefetchScalarGridSpec(
            num_scalar_prefetch=1, grid=(k // tr,),
            in_specs=[pl.BlockSpec((None, tr, n), lambda i, c_ref: (c_ref[0], i, 0)),
                      pl.BlockSpec((tr, n), lambda i, c_ref: (i, 0))],
            out_specs=pl.BlockSpec((tr, n), lambda i, c_ref: (i, 0))),
        out_shape=jax.ShapeDtypeStruct((k, n), mine2.dtype),
        compiler_params=_params(dimension_semantics=("parallel",)),
    )(c, mine2, other)


def _grads_to_chips(pairs, kinds, sizes, *, name):
    n_w = len(pairs)

    def shard_shape(p, kind, size):
        return {"col": (p.shape[0], size), "row": (size, p.shape[1]), "win_main": (p.shape[0], size),
                "win_strad": (p.shape[0], LANES)}[kind]

    def body(*refs):
        src, dst = refs[:n_w], refs[n_w:2 * n_w]
        send_sems, recv_sems, local_sems = refs[2 * n_w:]
        x, y, c = _my_place()
        mine = 2 * x + y
        chips = _other_chips(x, y)
        own = [pltpu.make_async_copy(_window(src[i], kinds[i], sizes[i], mine, ()), dst[i].at[mine], local_sems.at[i])
               for i in range(n_w)]
        for cp in own:
            cp.start()

        def copy(i, k, chip, window_of, slab):
            return pltpu.make_async_remote_copy(src_ref=_window(src[i], kinds[i], sizes[i], window_of, ()),
                                                dst_ref=dst[i].at[slab], send_sem=send_sems.at[3 * i + k],
                                                recv_sem=recv_sems.at[3 * i + k], device_id=(*chip, c), device_id_type=MESH)

        sends = [copy(i, k, chip, _chip_no(chip), mine) for k, chip in enumerate(chips) for i in range(n_w)]
        for cp in sends:
            cp.start()
        for k, chip in enumerate(chips):
            for i in range(n_w):
                copy(i, k, chip, mine, _chip_no(chip)).wait_recv()
        for cp in sends:
            cp.wait_send()
        for cp in own:
            cp.wait()

    return pl.pallas_call(
        body, name=name, in_specs=[HBM] * n_w, out_specs=[HBM] * n_w,
        out_shape=[jax.ShapeDtypeStruct((N_CHIPS,) + shard_shape(p, kind, size), p.dtype)
                   for p, kind, size in zip(pairs, kinds, sizes)],
        scratch_shapes=[pltpu.SemaphoreType.DMA((3 * n_w,)), pltpu.SemaphoreType.DMA((3 * n_w,)),
                        pltpu.SemaphoreType.DMA((n_w,))],
    )(*pairs)


def _sum_slabs(slabs, *, name):
    n_s, k, n = slabs.shape
    tr = _div_tile(k, 512, BF16_ROWS)

    def body(s_ref, o_ref):
        acc = s_ref[0].astype(F32)
        for i in range(1, n_s):
            acc = acc + s_ref[i].astype(F32)
        o_ref[...] = acc

    return pl.pallas_call(
        body, name=name, grid=(k // tr,), in_specs=[pl.BlockSpec((n_s, tr, n), lambda i: (0, i, 0))],
        out_specs=pl.BlockSpec((tr, n), lambda i: (i, 0)), out_shape=jax.ShapeDtypeStruct((k, n), F32),
        compiler_params=_params(dimension_semantics=("parallel",)),
    )(slabs)


def _exchange_layers(halves, *, name):
    n_w = len(halves)

    def body(*refs):
        src, dst = refs[:n_w], refs[n_w:2 * n_w]
        send_sems, recv_sems, local_sems = refs[2 * n_w:]
        x, y, c = _my_place()
        own = [pltpu.make_async_copy(src[i], dst[i].at[c], local_sems.at[i]) for i in range(n_w)]
        for cp in own:
            cp.start()

        def copy(i, layer):
            return pltpu.make_async_remote_copy(src_ref=src[i], dst_ref=dst[i].at[layer], send_sem=send_sems.at[i],
                                                recv_sem=recv_sems.at[i], device_id=(x, y, 1 - c), device_id_type=MESH)

        sends = [copy(i, c) for i in range(n_w)]
        for cp in sends:
            cp.start()
        for i in range(n_w):
            copy(i, 1 - c).wait_recv()
        for cp in sends:
            cp.wait_send()
        for cp in own:
            cp.wait()

    return pl.pallas_call(
        body, name=name, in_specs=[HBM] * n_w, out_specs=[HBM] * n_w,
        out_shape=[jax.ShapeDtypeStruct((2,) + h.shape, h.dtype) for h in halves],
        scratch_shapes=[pltpu.SemaphoreType.DMA((n_w,)), pltpu.SemaphoreType.DMA((n_w,)), pltpu.SemaphoreType.DMA((n_w,))],
    )(*halves)


def _all_sum_small(v, *, name):
    r = v.shape[0]
    relations = [(dx, dy, dc) for dx in (0, 1) for dy in (0, 1) for dc in (0, 1)][1:]

    def body(v_ref, o_ref, buf, send_sems, recv_sems):
        x, y, c = _my_place()
        me = 4 * x + 2 * y + c
        buf[me] = v_ref[...]
        peers = [(x + dx - 2 * x * dx, y + dy - 2 * y * dy, c + dc - 2 * c * dc) for dx, dy, dc in relations]

        def copy(k, slot):
            return pltpu.make_async_remote_copy(src_ref=v_ref, dst_ref=buf.at[slot], send_sem=send_sems.at[k],
                                                recv_sem=recv_sems.at[k], device_id=peers[k], device_id_type=MESH)

        sends = [copy(k, me) for k in range(len(relations))]
        for cp in sends:
            cp.start()
        for k, (px, py, pc) in enumerate(peers):
            copy(k, 4 * px + 2 * py + pc).wait_recv()
        for cp in sends:
            cp.wait_send()
        acc = buf[0]
        for i in range(1, 8):
            acc = acc + buf[i]
        o_ref[...] = acc

    vm = pl.BlockSpec(memory_space=pltpu.VMEM)
    return pl.pallas_call(
        body, name=name, in_specs=[vm], out_specs=vm, out_shape=jax.ShapeDtypeStruct((r, LANES), F32),
        scratch_shapes=[pltpu.VMEM((8, r, LANES), F32), pltpu.SemaphoreType.DMA((7,)), pltpu.SemaphoreType.DMA((7,))],
    )(v)


SHARDED = (("ffn1_w_up", "col"), ("ffn1_w_down", "row"), ("w_in", "win"), ("w_branch_a", "col"),
           ("w_branch_b", "col"), ("w_out", "row"), ("ffn2_w_up", "col"), ("ffn2_w_down", "row"))
REPLICATED = ("ffn1_norm", "mix_norm", "na_rel_bias", "ffn2_norm", "final_norm")


def _gather_all_weights(w):
    even = lax.axis_index("y") == 0
    shards, kinds, names = [], [], []
    for name, kind in SHARDED:
        wb = w[name].astype(BF16)
        if kind == "win":
            main = wb.shape[-1] - HEAD_DIM
            assert main % LANES == 0
            zeros = jnp.zeros(wb.shape[:-1] + (HEAD_DIM,), BF16)
            shards += [jnp.where(even, wb[..., :main], wb[..., HEAD_DIM:]),
                       jnp.where(even, jnp.concatenate([wb[..., main:], zeros], -1),
                                 jnp.concatenate([zeros, wb[..., :HEAD_DIM]], -1))]
            kinds += ["win_main", "slot"]
            names += [name, name + "_strad"]
        else:
            shards.append(wb)
            kinds.append(kind)
            names.append(name)
    full = dict(zip(names, _gather_weights(shards, kinds, name="gather_weights")))
    strad = full.pop("w_in_strad")
    for i in range(N_CHIPS // 2):
        lo = main + 2 * (main + HEAD_DIM) * i
        full["w_in"] = full["w_in"].at[:, :, lo:lo + LANES].set(strad[2 * i] + strad[2 * i + 1])
    return full


def _reduce_weight_grads(grads, shards):
    names, kinds, sizes, srcs = [], [], [], []
    for name, kind in SHARDED:
        shp = shards[name].shape
        if kind == "win":
            names += [name, name + "_strad"]
            kinds += ["win_main", "win_strad"]
            sizes += [shp[2] - HEAD_DIM] * 2
            srcs += [name, name]
        else:
            names.append(name)
            kinds.append(kind)
            sizes.append(shp[1] if kind == "row" else shp[2])
            srcs.append(name)
    uniq = [name for name, _ in SHARDED]
    arrived = dict(zip(uniq, _grads_to_sibling([grads[n] for n in uniq], name="grads_to_sibling")))
    pair = {n: _pair_add(grads[n], arrived[n], name=f"grads_pair_{n}") for n in uniq}
    slabs = _grads_to_chips([pair[s] for s in srcs], kinds, sizes, name="grads_to_chips")
    halves = [_sum_slabs(sl, name=f"grads_sum_{n}") for n, sl in zip(names, slabs)]
    out = dict(zip(names, _exchange_layers(halves, name="grads_layers")))
    strad = out.pop("w_in_strad")
    even = lax.axis_index("y") == 0
    out["w_in"] = jnp.where(even, jnp.concatenate([out["w_in"], strad[..., :HEAD_DIM]], -1),
                            jnp.concatenate([strad[..., HEAD_DIM:], out["w_in"]], -1))
    return out


class _Grads:
    def __init__(self, depth):
        self.depth = depth
        self.arrays = {}

    def put(self, weight, layer, a, b, *, cols=None, col_off=0, **kw):
        self.arrays[weight] = _mm(a, b, mode="tn", out_dtype=BF16, out_slab=(layer, self.depth), out_cols=cols,
                                  out_col_off=col_off, out_into=self.arrays.get(weight), **kw)


def _ffn_fwd(x, norm_g, w_up, w_down, layer, tag):
    t, d = x.shape
    f = w_down.shape[1]
    h = _rms_fwd(x, norm_g, tt=512, name=f"{tag}_norm")
    a, gate, up = _mm_swiglu_fwd(h, w_up, layer, tm=512, tn=_div_tile(f, 1408), name=f"{tag}_up")
    x_out = _mm(a, w_down, mode="nn", out_dtype=F32, tm=512, tn=d, tk=f, alpha=0.5, res=x, b_sel=layer, name=f"{tag}_down")
    return x_out, (x, h, a, gate, up)


def _ffn_bwd(dx, saved, norm_g, w_up, w_down, layer, grads, wname, tag):
    x, h, a, gate, up = saved
    t, d = x.shape
    f = w_down.shape[1]
    tn = _div_tile(f, 1408)
    dxb = dx.astype(BF16)
    grads.put(f"{wname}_w_down", layer, a, dxb, tm=tn, tn=d, tk=1024, alpha=0.5, name=f"{tag}_dwd")
    d_gate, d_up = _mm_swiglu_bwd(dxb, w_down, layer, gate, up, alpha=0.5, tm=512, tn=tn, name=f"{tag}_da")
    grads.put(f"{wname}_w_up", layer, h, d_gate, cols=2 * f, tm=d, tn=tn, tk=1024, name=f"{tag}_dwg")
    grads.put(f"{wname}_w_up", layer, h, d_up, cols=2 * f, col_off=f // tn, tm=d, tn=tn, tk=1024, name=f"{tag}_dwu")
    dh = _mm(d_gate, w_up, mode="nt", out_dtype=F32, tm=512, tn=d, tk=f, b_sel=layer, name=f"{tag}_dh1")
    dh = _mm(d_up, w_up, mode="nt", out_dtype=F32, tm=512, tn=d, tk=f, b_sel=layer, b_k_off=1, res=dh, name=f"{tag}_dh2")
    return _rms_bwd(dh, x, norm_g, dx, tt=512, name=f"{tag}_dnorm")


def _to_heads(y, b, n_heads):
    t, w = y.shape
    return y.reshape(b, t // b, n_heads, HEAD_DIM).transpose(0, 2, 1, 3)


def _from_heads(y):
    b, n, s, hd = y.shape
    return y.transpose(0, 2, 1, 3).reshape(b * s, n * hd)


N_QKV = 3 * (DIL_HEADS + NA_HEADS) * HEAD_DIM


def _mixer_fwd(x, b, norm_g, full, layer, bias, tabs, tag):
    t, d = x.shape
    s = t // b
    n_in = full["w_in"].shape[2]
    h = _rms_fwd(x, norm_g, tt=512, name=f"{tag}_norm")
    proj = _mm(h, full["w_in"], mode="nn", out_dtype=F32, tm=512, tn=_div_tile(n_in, 2944), tk=d, b_sel=layer, name=f"{tag}_in")
    heads = _split_heads(proj.reshape(b, s, -1), *tabs, n_pairs=N_QKV // LANES, rot_pairs=DIL_HEADS,
                         scale_ranges=((0, DIL_HEADS // 2), (3 * DIL_HEADS // 2, (3 * DIL_HEADS + NA_HEADS) // 2)),
                         ts=s, name=f"{tag}_heads")
    ya, lse_a = _dil_attn_fwd(heads, tq=256, name=f"{tag}_dil")
    yb, lse_b = _na_attn_fwd(heads, bias, first=3 * DIL_HEADS, name=f"{tag}_na")
    ya2, yb2 = _from_heads(ya), _from_heads(yb)
    z = _mm(ya2, full["w_branch_a"], mode="nn", out_dtype=F32, tm=1024, tn=d, tk=ya2.shape[1], b_sel=layer,
            out_slab=(0, 2), name=f"{tag}_za")
    z = _mm(yb2, full["w_branch_b"], mode="nn", out_dtype=F32, tm=1024, tn=d, tk=yb2.shape[1], b_sel=layer,
            out_slab=(1, 2), out_into=z, name=f"{tag}_zb")
    merged = _gate_fwd(proj, z, gate_col=N_QKV, tt=1024, name=f"{tag}_gate")
    x_out = _mm(merged, full["w_out"], mode="nn", out_dtype=F32, tm=1024, tn=d, tk=d, res=x, b_sel=layer, name=f"{tag}_out")
    return x_out, (x, h, proj, heads, ya, lse_a, yb, lse_b, ya2, yb2, z, merged)


def _mixer_bwd(dx, b, saved, norm_g, full, layer, bias, tabs, grads, tag):
    x, h, proj, heads, ya, lse_a, yb, lse_b, ya2, yb2, z, merged = saved
    t, d = x.shape
    s = t // b
    n_in = full["w_in"].shape[2]
    dob = dx.astype(BF16)
    grads.put("w_out", layer, merged, dob, tm=d, tn=d, tk=1024, name=f"{tag}_dwo")
    dm = _mm(dob, full["w_out"], mode="nt", out_dtype=F32, tm=1024, tn=d, tk=d, b_sel=layer, name=f"{tag}_dm")
    dz, dproj = _gate_bwd(dm, proj, z, gate_col=N_QKV, tt=1024, name=f"{tag}_dgate")
    grads.put("w_branch_a", layer, ya2, dz, b_sel=0, tm=ya2.shape[1], tn=d, tk=1024, name=f"{tag}_dwa")
    grads.put("w_branch_b", layer, yb2, dz, b_sel=1, tm=yb2.shape[1], tn=d, tk=1024, name=f"{tag}_dwb")
    dya = _mm(dz, full["w_branch_a"], mode="nt", out_dtype=F32, tm=1024, tn=ya2.shape[1], tk=d, a_sel=0, b_sel=layer, name=f"{tag}_dya")
    dyb = _mm(dz, full["w_branch_b"], mode="nt", out_dtype=F32, tm=1024, tn=yb2.shape[1], tk=d, a_sel=1, b_sel=layer, name=f"{tag}_dyb")
    d_dil = _dil_attn_bwd(heads, ya, lse_a, _to_heads(dya, b, DIL_GROUP_HEADS), tq=256, name=f"{tag}_ddil")
    d_na, d_bias = _na_attn_bwd(heads, bias, yb, lse_b, _to_heads(dyb, b, NA_HEADS), first=3 * DIL_HEADS, name=f"{tag}_dna")
    dproj = _merge_heads(d_dil, *tabs, heads_per_row=DIL_GROUP_HEADS, rot_pairs=DIL_HEADS, scale_pairs=DIL_HEADS // 2,
                         ts=s, out_cols=n_in, tile_off=0, into=dproj.reshape(b, s, n_in), name=f"{tag}_dheads_a")
    dproj = _merge_heads(d_na, *tabs, heads_per_row=NA_HEADS, rot_pairs=0, scale_pairs=NA_HEADS // 2, ts=s,
                         out_cols=n_in, tile_off=3 * DIL_HEADS // 2, into=dproj, name=f"{tag}_dheads_b").reshape(t, n_in)
    grads.put("w_in", layer, h, dproj, tm=_div_tile(d, 512), tn=_div_tile(n_in, 2944), tk=1024, name=f"{tag}_dwin")
    dh = _mm(dproj, full["w_in"], mode="nt", out_dtype=F32, tm=512, tn=d, tk=_div_tile(n_in, 2944), b_sel=layer, name=f"{tag}_dh")
    dx_in, d_norm = _rms_bwd(dh, x, norm_g, dx, tt=512, name=f"{tag}_dnorm")
    d_rb = _na_collapse_bias(d_bias, name=f"{tag}_dbias")
    return dx_in, d_norm, d_rb


def kernel(x, ffn1_norm, ffn1_w_up, ffn1_w_down, mix_norm, w_in, na_rel_bias, w_branch_a, w_branch_b, w_out, ffn2_norm, ffn2_w_up, ffn2_w_down, final_norm, loss_target, m_ffn1_norm, m_ffn1_w_up, m_ffn1_w_down, m_mix_norm, m_w_in, m_na_rel_bias, m_w_branch_a, m_w_branch_b, m_w_out, m_ffn2_norm, m_ffn2_w_up, m_ffn2_w_down, m_final_norm, v_ffn1_norm, v_ffn1_w_up, v_ffn1_w_down, v_mix_norm, v_w_in, v_na_rel_bias, v_w_branch_a, v_w_branch_b, v_w_out, v_ffn2_norm, v_ffn2_w_up, v_ffn2_w_down, v_final_norm):
    w = dict(ffn1_norm=ffn1_norm, ffn1_w_up=ffn1_w_up, ffn1_w_down=ffn1_w_down, mix_norm=mix_norm, w_in=w_in,
             na_rel_bias=na_rel_bias, w_branch_a=w_branch_a, w_branch_b=w_branch_b, w_out=w_out, ffn2_norm=ffn2_norm,
             ffn2_w_up=ffn2_w_up, ffn2_w_down=ffn2_w_down, final_norm=final_norm)
    mom = dict(ffn1_norm=m_ffn1_norm, ffn1_w_up=m_ffn1_w_up, ffn1_w_down=m_ffn1_w_down, mix_norm=m_mix_norm, w_in=m_w_in,
               na_rel_bias=m_na_rel_bias, w_branch_a=m_w_branch_a, w_branch_b=m_w_branch_b, w_out=m_w_out,
               ffn2_norm=m_ffn2_norm, ffn2_w_up=m_ffn2_w_up, ffn2_w_down=m_ffn2_w_down, final_norm=m_final_norm)
    var = dict(ffn1_norm=v_ffn1_norm, ffn1_w_up=v_ffn1_w_up, ffn1_w_down=v_ffn1_w_down, mix_norm=v_mix_norm, w_in=v_w_in,
               na_rel_bias=v_na_rel_bias, w_branch_a=v_w_branch_a, w_branch_b=v_w_branch_b, w_out=v_w_out,
               ffn2_norm=v_ffn2_norm, ffn2_w_up=v_ffn2_w_up, ffn2_w_down=v_ffn2_w_down, final_norm=v_final_norm)
    b, s, d = x.shape
    t = b * s
    depth = ffn1_norm.shape[0]
    assert depth == 2, "core c of a chip sends / reduces layer c"
    shards = {name: w[name] for name, _ in SHARDED}

    full = _gather_all_weights(w)
    tabs = _rope_tables(s)
    bias = _na_expand_bias(na_rel_bias, name="na_bias")

    xc = x.reshape(t, d)
    saved = []
    for l in range(depth):
        xc, s1 = _ffn_fwd(xc, ffn1_norm[l:l + 1], full["ffn1_w_up"], full["ffn1_w_down"], l, f"l{l}_ffn1")
        xc, s2 = _mixer_fwd(xc, b, mix_norm[l:l + 1], full, l, bias[l], tabs, f"l{l}_mix")
        xc, s3 = _ffn_fwd(xc, ffn2_norm[l:l + 1], full["ffn2_w_up"], full["ffn2_w_down"], l, f"l{l}_ffn2")
        saved.append((s1, s2, s3))

    dx, d_final, loss_part = _final_loss(xc, final_norm.reshape(1, d), loss_target.reshape(t, d), tt=512, name="final_loss")
    grads = _Grads(depth)
    small = {name: [None] * depth for name in REPLICATED[:-1]}
    for l in reversed(range(depth)):
        s1, s2, s3 = saved[l]
        dx, small["ffn2_norm"][l] = _ffn_bwd(dx, s3, ffn2_norm[l:l + 1], full["ffn2_w_up"], full["ffn2_w_down"], l, grads,
                                             "ffn2", f"l{l}_ffn2")
        dx, small["mix_norm"][l], small["na_rel_bias"][l] = _mixer_bwd(
            dx, b, s2, mix_norm[l:l + 1], full, l, bias[l], tabs, grads, f"l{l}_mix")
        dx, small["ffn1_norm"][l] = _ffn_bwd(dx, s1, ffn1_norm[l:l + 1], full["ffn1_w_up"], full["ffn1_w_down"], l, grads,
                                             "ffn1", f"l{l}_ffn1")
    grad_x = dx.reshape(b, s, d)

    g_out = _reduce_weight_grads(grads.arrays, shards)
    parts = [jnp.stack(small[name]).reshape(-1) for name in REPLICATED[:-1]] + [d_final.reshape(-1), loss_part[0, :1]]
    sizes = [v.shape[0] for v in parts]
    flat = jnp.concatenate(parts)
    flat = jnp.pad(flat, (0, -flat.shape[0] % (8 * LANES)))
    small_sum = _all_sum_small(flat.reshape(-1, LANES), name="small_all_sum").reshape(-1)
    off = 0
    for name, n in zip(REPLICATED, sizes[:-1]):
        g_out[name] = small_sum[off:off + n].reshape(w[name].shape)
        off += n
    loss = small_sum[off]

    names = list(w)
    delta, new_m, new_v = {}, {}, {}
    for name in names:
        delta[name], new_m[name], new_v[name] = _adamw(w[name], g_out[name], mom[name], var[name], name=f"adamw_{name}")
    return (loss, grad_x, *[g_out[n] for n in names], *[delta[n] for n in names], *[new_m[n] for n in names],
            *[new_v[n] for n in names])
```

```python
import functools

import numpy as np
import jax
import jax.numpy as jnp
from jax import lax
from jax.experimental import pallas as pl
from jax.experimental.pallas import tpu as pltpu

F32, BF16 = jnp.float32, jnp.bfloat16
MESH = pl.DeviceIdType.MESH

HEAD_DIM = 64
DILATIONS = (1, 4, 16)
DIL_HALF = 64
DIL_GROUP_HEADS = 4
DIL_HEADS = 12
NA_HEADS = 8
GRID_W = 64
NA_ROWS = 8
NA_COLS = 16
ROPE_THETA = 10000.0
RMS_EPS = 1e-6
NEG_INF = -1e30
ADAM_LR, ADAM_B1, ADAM_B2, ADAM_EPS, ADAM_WD, ADAM_STEP = 0.001, 0.9, 0.999, 1e-08, 0.01, 10
QK_SCALE = HEAD_DIM ** -0.5

N_CHIPS = 4
LANES = 128
BF16_ROWS = 16
VMEM_LIMIT = 56 * 1024 * 1024

_NN = (((1,), (0,)), ((), ()))
_NT = (((1,), (1,)), ((), ()))
_TN = (((0,), (0,)), ((), ()))

HBM = pl.BlockSpec(memory_space=pl.ANY)


def _params(**kw):
    return pltpu.CompilerParams(vmem_limit_bytes=VMEM_LIMIT, **kw)


def _dot(a, b, dims):
    return lax.dot_general(a, b, dims, preferred_element_type=F32)


def _div_tile(n, cap, mult=LANES):
    best = None
    for t in range(mult, min(n, cap) + 1, mult):
        if n % t == 0:
            best = t
    return n if best is None else best


def _stacked(block, index, sel):
    if sel is None:
        return pl.BlockSpec(block, index)
    return pl.BlockSpec((None,) + block, lambda *g: (sel,) + index(*g))


def _mm(a, b, *, mode, out_dtype, tm, tn, tk, name, alpha=1.0, res=None, a_sel=None, b_sel=None, b_k_off=0,
        out_slab=None, out_cols=None, out_col_off=0, out_into=None):
    a2, b2 = a.shape[-2:], b.shape[-2:]
    if mode == "nn":
        (m, k), n = a2, b2[1]
        a_spec = _stacked((tm, tk), lambda i, j, kk: (i, kk), a_sel)
        b_spec = _stacked((tk, tn), lambda i, j, kk: (kk + b_k_off, j), b_sel)
        dims = _NN
    elif mode == "nt":
        (m, k), n = a2, b2[0]
        a_spec = _stacked((tm, tk), lambda i, j, kk: (i, kk), a_sel)
        b_spec = _stacked((tn, tk), lambda i, j, kk: (j, kk + b_k_off), b_sel)
        dims = _NT
    else:
        (k, m), n = a2, b2[1]
        a_spec = _stacked((tk, tm), lambda i, j, kk: (kk, i), a_sel)
        b_spec = _stacked((tk, tn), lambda i, j, kk: (kk + b_k_off, j), b_sel)
        dims = _TN
    assert m % tm == 0 and n % tn == 0 and k % tk == 0, (name, a.shape, b.shape)
    nk = k // tk
    has_res = res is not None
    if out_slab is None:
        o_spec = pl.BlockSpec((tm, tn), lambda i, j, kk: (i, j))
        out_shape = jax.ShapeDtypeStruct((m, n), out_dtype)
    else:
        o_spec = _stacked((tm, tn), lambda i, j, kk: (i, j + out_col_off), out_slab[0])
        out_shape = jax.ShapeDtypeStruct((out_slab[1], m, n if out_cols is None else out_cols), out_dtype)
    r_spec = pl.BlockSpec((tm, tn), lambda i, j, kk: (i, j))
    n_in = 2 + has_res + (out_into is not None)

    def body(*refs):
        a_ref, b_ref = refs[0], refs[1]
        r_ref = refs[2] if has_res else None
        o_ref = refs[n_in]
        p = _dot(a_ref[...], b_ref[...], dims)

        def finish(acc):
            y = acc * alpha if alpha != 1.0 else acc
            if has_res:
                y = y + r_ref[...].astype(F32)
            o_ref[...] = y.astype(o_ref.dtype)

        if nk == 1:
            finish(p)
        else:
            acc_ref = refs[n_in + 1]
            kk = pl.program_id(2)

            @pl.when(kk == 0)
            def _():
                acc_ref[...] = p

            @pl.when(kk > 0)
            def _():
                acc_ref[...] += p

            @pl.when(kk == nk - 1)
            def _():
                finish(acc_ref[...])

    operands = [a, b] + ([res] if has_res else [])
    in_specs = [a_spec, b_spec] + ([r_spec] if has_res else [])
    aliases = {}
    if out_into is not None:
        aliases = {len(operands): 0}
        operands.append(out_into)
        in_specs.append(HBM)
    return pl.pallas_call(
        body, name=name, grid=(m // tm, n // tn, nk), in_specs=in_specs, out_specs=o_spec, out_shape=out_shape,
        scratch_shapes=[pltpu.VMEM((tm, tn), F32)] if nk > 1 else [], input_output_aliases=aliases,
        compiler_params=_params(dimension_semantics=("parallel", "parallel", "arbitrary")),
    )(*operands)


def _mm_swiglu_fwd(h, w_up, layer, *, tm, tn, name):
    m, k = h.shape
    n = w_up.shape[2] // 2
    h_spec = pl.BlockSpec((tm, k), lambda i, j: (i, 0))
    wg_spec = pl.BlockSpec((None, k, tn), lambda i, j: (layer, 0, j))
    wu_spec = pl.BlockSpec((None, k, tn), lambda i, j: (layer, 0, j + n // tn))
    o_spec = pl.BlockSpec((tm, tn), lambda i, j: (i, j))

    def body(h_ref, wg_ref, wu_ref, a_ref, g_ref, u_ref):
        hb = h_ref[...]
        g = _dot(hb, wg_ref[...], _NN)
        u = _dot(hb, wu_ref[...], _NN)
        a_ref[...] = (g * jax.nn.sigmoid(g) * u).astype(BF16)
        g_ref[...] = g.astype(BF16)
        u_ref[...] = u.astype(BF16)

    out = jax.ShapeDtypeStruct((m, n), BF16)
    return pl.pallas_call(
        body, name=name, grid=(m // tm, n // tn), in_specs=[h_spec, wg_spec, wu_spec],
        out_specs=[o_spec] * 3, out_shape=[out] * 3,
        compiler_params=_params(dimension_semantics=("parallel", "parallel")),
    )(h, w_up, w_up)


def _mm_swiglu_bwd(dy, w_down, layer, gate, up, *, alpha, tm, tn, name):
    m, k = dy.shape
    n = w_down.shape[1]
    dy_spec = pl.BlockSpec((tm, k), lambda i, j: (i, 0))
    w_spec = pl.BlockSpec((None, tn, k), lambda i, j: (layer, j, 0))
    o_spec = pl.BlockSpec((tm, tn), lambda i, j: (i, j))

    def body(dy_ref, w_ref, g_ref, u_ref, dg_ref, du_ref):
        da = _dot(dy_ref[...], w_ref[...], _NT) * alpha
        g = g_ref[...].astype(F32)
        u = u_ref[...].astype(F32)
        sg = jax.nn.sigmoid(g)
        dg_ref[...] = (da * u * (sg * (1.0 + g * (1.0 - sg)))).astype(BF16)
        du_ref[...] = (da * (g * sg)).astype(BF16)

    out = jax.ShapeDtypeStruct((m, n), BF16)
    return pl.pallas_call(
        body, name=name, grid=(m // tm, n // tn), in_specs=[dy_spec, w_spec, o_spec, o_spec],
        out_specs=[o_spec] * 2, out_shape=[out] * 2,
        compiler_params=_params(dimension_semantics=("parallel", "parallel")),
    )(dy, w_down, gate, up)


def _rms_fwd(x, g, *, tt, name):
    t, d = x.shape

    def body(x_ref, g_ref, h_ref):
        xv = x_ref[...]
        rstd = lax.rsqrt(jnp.mean(xv * xv, axis=1, keepdims=True) + RMS_EPS)
        h_ref[...] = (xv * rstd * g_ref[...]).astype(BF16)

    return pl.pallas_call(
        body, name=name, grid=(t // tt,),
        in_specs=[pl.BlockSpec((tt, d), lambda i: (i, 0)), pl.BlockSpec((1, d), lambda i: (0, 0))],
        out_specs=pl.BlockSpec((tt, d), lambda i: (i, 0)), out_shape=jax.ShapeDtypeStruct((t, d), BF16),
        compiler_params=_params(dimension_semantics=("parallel",)),
    )(x, g)


def _rms_bwd(dh, x, g, dres, *, tt, name):
    t, d = x.shape

    def body(dh_ref, x_ref, g_ref, r_ref, dx_ref, dg_ref):
        xv = x_ref[...]
        rstd = lax.rsqrt(jnp.mean(xv * xv, axis=1, keepdims=True) + RMS_EPS)
        xhat = xv * rstd
        dhv = dh_ref[...]
        dxhat = dhv * g_ref[...]
        dx_ref[...] = r_ref[...] + rstd * (dxhat - xhat * jnp.mean(dxhat * xhat, axis=1, keepdims=True))

        @pl.when(pl.program_id(0) == 0)
        def _():
            dg_ref[...] = jnp.zeros_like(dg_ref)

        dg_ref[...] += jnp.sum(dhv * xhat, axis=0, keepdims=True)

    row = pl.BlockSpec((tt, d), lambda i: (i, 0))
    vec = pl.BlockSpec((1, d), lambda i: (0, 0))
    return pl.pallas_call(
        body, name=name, grid=(t // tt,), in_specs=[row, row, vec, row], out_specs=[row, vec],
        out_shape=[jax.ShapeDtypeStruct((t, d), F32), jax.ShapeDtypeStruct((1, d), F32)],
        compiler_params=_params(dimension_semantics=("arbitrary",)),
    )(dh, x, g, dres)


def _final_loss(x, g, target, *, tt, name):
    t, d = x.shape

    def body(x_ref, g_ref, t_ref, dx_ref, dg_ref, loss_ref):
        xv = x_ref[...]
        gv = g_ref[...]
        rstd = lax.rsqrt(jnp.mean(xv * xv, axis=1, keepdims=True) + RMS_EPS)
        xhat = xv * rstd
        err = xhat * gv - t_ref[...]
        dy = err * (1.0 / d)
        dxhat = dy * gv
        dx_ref[...] = rstd * (dxhat - xhat * jnp.mean(dxhat * xhat, axis=1, keepdims=True))

        @pl.when(pl.program_id(0) == 0)
        def _():
            dg_ref[...] = jnp.zeros_like(dg_ref)
            loss_ref[...] = jnp.zeros_like(loss_ref)

        dg_ref[...] += jnp.sum(dy * xhat, axis=0, keepdims=True)
        part = 0.5 * jnp.sum(jnp.mean(err * err, axis=1, keepdims=True), axis=0, keepdims=True)
        loss_ref[...] += jnp.broadcast_to(part, loss_ref.shape)

    row = pl.BlockSpec((tt, d), lambda i: (i, 0))
    vec = pl.BlockSpec((1, d), lambda i: (0, 0))
    one = pl.BlockSpec((1, LANES), lambda i: (0, 0))
    return pl.pallas_call(
        body, name=name, grid=(t // tt,), in_specs=[row, vec, row], out_specs=[row, vec, one],
        out_shape=[jax.ShapeDtypeStruct((t, d), F32), jax.ShapeDtypeStruct((1, d), F32),
                   jax.ShapeDtypeStruct((1, LANES), F32)],
        compiler_params=_params(dimension_semantics=("arbitrary",)),
    )(x, g, target)


def _swap_halves(x):
    lane = lax.broadcasted_iota(jnp.int32, x.shape, 1)
    return jnp.where((lane // 32) % 2 == 0, pltpu.roll(x, 96, 1), pltpu.roll(x, 32, 1))


def _rope_tables(s):
    half = HEAD_DIM // 2
    inv_freq = ROPE_THETA ** (-jnp.arange(half, dtype=F32) / half)
    ang = jnp.arange(s).astype(F32)[:, None] * inv_freq[None, :]
    cos, sin = jnp.cos(ang), jnp.sin(ang)
    return jnp.tile(cos, (1, 4)), jnp.concatenate([-sin, sin, -sin, sin], axis=1)


def _split_heads(proj, cos4, sin4, *, n_pairs, rot_pairs, scale_ranges, ts, name):
    b, s, _ = proj.shape

    def body(x_ref, c_ref, s_ref, o_ref):
        x = x_ref[...]
        p = pl.program_id(1)
        rot = x * c_ref[...] + _swap_halves(x) * s_ref[...]
        y = jnp.where(p < rot_pairs, rot, x)
        is_q = functools.reduce(jnp.logical_or, [(p >= lo) & (p < hi) for lo, hi in scale_ranges])
        y = y * jnp.where(is_q, QK_SCALE, 1.0)
        o_ref[0] = y[:, :HEAD_DIM].astype(BF16)
        o_ref[1] = y[:, HEAD_DIM:].astype(BF16)

    tab = pl.BlockSpec((ts, LANES), lambda bi, p, si: (si, 0))
    return pl.pallas_call(
        body, name=name, grid=(b, n_pairs, s // ts),
        in_specs=[pl.BlockSpec((None, ts, LANES), lambda bi, p, si: (bi, si, p)), tab, tab],
        out_specs=pl.BlockSpec((None, 2, ts, HEAD_DIM), lambda bi, p, si: (bi, p, si, 0)),
        out_shape=jax.ShapeDtypeStruct((b, 2 * n_pairs, s, HEAD_DIM), BF16),
        compiler_params=_params(dimension_semantics=("parallel", "parallel", "parallel")),
    )(proj, cos4, sin4)


def _merge_heads(dheads, cos4, sin4, *, heads_per_row, rot_pairs, scale_pairs, ts, out_cols, tile_off, into, name):
    b, hpr, r, s, _ = dheads.shape
    n_pairs = hpr * r // 2
    ppr = hpr // 2

    def body(d_ref, c_ref, s_ref, *rest):
        o_ref = rest[-1]
        dy = jnp.concatenate([d_ref[0], d_ref[1]], axis=1)
        p = pl.program_id(1)
        rot = dy * c_ref[...] - _swap_halves(dy) * s_ref[...]
        dx = jnp.where(p < rot_pairs, rot, dy)
        dx = dx * jnp.where(p < scale_pairs, QK_SCALE, 1.0)
        o_ref[...] = dx.astype(BF16)

    tab = pl.BlockSpec((ts, LANES), lambda bi, p, si: (si, 0))
    operands = [dheads, cos4, sin4] + ([] if into is None else [into])
    return pl.pallas_call(
        body, name=name, grid=(b, n_pairs, s // ts),
        in_specs=[pl.BlockSpec((None, 2, None, ts, HEAD_DIM), lambda bi, p, si: (bi, p % ppr, p // ppr, si, 0)), tab, tab]
        + ([] if into is None else [HBM]),
        out_specs=pl.BlockSpec((None, ts, LANES), lambda bi, p, si: (bi, si, p + tile_off)),
        out_shape=jax.ShapeDtypeStruct((b, s, out_cols), BF16),
        input_output_aliases={} if into is None else {3: 0},
        compiler_params=_params(dimension_semantics=("parallel", "parallel", "parallel")),
    )(*operands)


def _dil_window(g, q0, tq, s):
    pad = -(-DIL_HALF * DILATIONS[g] // LANES) * LANES
    width = tq + 2 * pad
    if width >= s:
        return 0, s
    return pl.multiple_of(jnp.clip(q0 - pad, 0, s - width), LANES), width


def _dil_mask(g, q0, start, shape):
    d = DILATIONS[g]
    diff = (q0 - start) + lax.broadcasted_iota(jnp.int32, shape, 0) - lax.broadcasted_iota(jnp.int32, shape, 1)
    ok = jnp.abs(diff) <= DIL_HALF * d
    if d > 1:
        ok = ok & ((diff & (d - 1)) == 0)
    return ok


def _dil_head_spec(part, g, s):
    return pl.BlockSpec((None, None, s, HEAD_DIM), lambda b, j: (b, part * DIL_HEADS + g * DIL_GROUP_HEADS + j, 0, 0))


def _dil_attn_fwd(heads, *, tq, name):
    b, _, s, _ = heads.shape
    n_g = len(DILATIONS)

    def body(*refs):
        qkv = refs[:3 * n_g]
        o_ref, l_ref = refs[3 * n_g:]

        def step(i, carry):
            q0 = pl.multiple_of(i * tq, tq)
            scores, wins = [], []
            for g in range(n_g):
                start, width = _dil_window(g, q0, tq, s)
                sc = _dot(qkv[3 * g][pl.ds(q0, tq), :], qkv[3 * g + 1][pl.ds(start, width), :], _NT)
                scores.append(jnp.where(_dil_mask(g, q0, start, sc.shape), sc, NEG_INF))
                wins.append((start, width))
            m = functools.reduce(jnp.maximum, [jnp.max(sc, axis=1, keepdims=True) for sc in scores])
            den = jnp.zeros((tq, 1), F32)
            acc = jnp.zeros((tq, HEAD_DIM), F32)
            for g in range(n_g):
                p = jnp.exp(scores[g] - m)
                den = den + jnp.sum(p, axis=1, keepdims=True)
                acc = acc + _dot(p.astype(BF16), qkv[3 * g + 2][pl.ds(*wins[g]), :], _NN)
            o_ref[pl.ds(q0, tq), :] = (acc / den).astype(o_ref.dtype)
            l_ref[pl.ds(q0, tq), :] = m + jnp.log(den)
            return carry

        lax.fori_loop(0, s // tq, step, 0)

    out = pl.BlockSpec((None, None, s, HEAD_DIM), lambda bi, j: (bi, j, 0, 0))
    lse = pl.BlockSpec((None, None, s, 1), lambda bi, j: (bi, j, 0, 0))
    return pl.pallas_call(
        body, name=name, grid=(b, DIL_GROUP_HEADS),
        in_specs=[_dil_head_spec(part, g, s) for g in range(n_g) for part in range(3)],
        out_specs=[out, lse],
        out_shape=[jax.ShapeDtypeStruct((b, DIL_GROUP_HEADS, s, HEAD_DIM), BF16),
                   jax.ShapeDtypeStruct((b, DIL_GROUP_HEADS, s, 1), F32)],
        compiler_params=_params(dimension_semantics=("parallel", "parallel")),
    )(*([heads] * (3 * n_g)))


def _dil_attn_bwd(heads, out, lse, dout, *, tq, name):
    b, _, s, _ = heads.shape
    n_g = len(DILATIONS)

    def body(*refs):
        qkv = refs[:3 * n_g]
        o_ref, l_ref, do_ref, d_ref = refs[3 * n_g:]
        d_ref[...] = jnp.zeros_like(d_ref)

        def step(i, carry):
            q0 = pl.multiple_of(i * tq, tq)
            do = do_ref[pl.ds(q0, tq), :]
            delta = jnp.sum(do * o_ref[pl.ds(q0, tq), :].astype(F32), axis=1, keepdims=True)
            lse_b = l_ref[pl.ds(q0, tq), :]
            do_b = do.astype(BF16)
            for g in range(n_g):
                start, width = _dil_window(g, q0, tq, s)
                win = pl.ds(start, width)
                q = qkv[3 * g][pl.ds(q0, tq), :]
                k = qkv[3 * g + 1][win, :]
                v = qkv[3 * g + 2][win, :]
                sc = _dot(q, k, _NT)
                p = jnp.where(_dil_mask(g, q0, start, sc.shape), jnp.exp(sc - lse_b), 0.0)
                ds = (p * (_dot(do_b, v, _NT) - delta)).astype(BF16)
                d_ref[g, pl.ds(q0, tq), :] = _dot(ds, k, _NN)
                d_ref[n_g + g, win, :] += _dot(ds, q, _TN)
                d_ref[2 * n_g + g, win, :] += _dot(p.astype(BF16), do_b, _TN)
            return carry

        lax.fori_loop(0, s // tq, step, 0)

    per_head = lambda bi, j: (bi, j, 0, 0)
    return pl.pallas_call(
        body, name=name, grid=(b, DIL_GROUP_HEADS),
        in_specs=[_dil_head_spec(part, g, s) for g in range(n_g) for part in range(3)]
        + [pl.BlockSpec((None, None, s, HEAD_DIM), per_head), pl.BlockSpec((None, None, s, 1), per_head),
           pl.BlockSpec((None, None, s, HEAD_DIM), per_head)],
        out_specs=pl.BlockSpec((None, None, 3 * n_g, s, HEAD_DIM), lambda bi, j: (bi, j, 0, 0, 0)),
        out_shape=jax.ShapeDtypeStruct((b, DIL_GROUP_HEADS, 3 * n_g, s, HEAD_DIM), F32),
        compiler_params=_params(dimension_semantics=("parallel", "parallel")),
    )(*([heads] * (3 * n_g)), out, lse, dout)


NA_OFFSETS = NA_ROWS
NA_BIAS_ROWS = 2 * NA_ROWS - 1
NA_BIAS_COLS = 2 * NA_COLS - 1
NA_KEYS = NA_ROWS * GRID_W
NA_UNROLL = 2


def _na_onehot():
    c = np.arange(GRID_W)[:, None]
    k = np.arange(GRID_W)[None, :]
    lo = np.clip(c - NA_COLS // 2, 0, GRID_W - NA_COLS)
    valid = (k >= lo) & (k < lo + NA_COLS)
    onehot = np.zeros((GRID_W, GRID_W, LANES), np.float32)
    cc, kk = np.nonzero(valid)
    onehot[cc, kk, kk - cc + NA_COLS - 1] = 1.0
    return onehot.reshape(GRID_W * GRID_W, LANES), valid.reshape(1, GRID_W * GRID_W)


def _na_expand_bias(rel_bias, *, name):
    l, h, nr, nc = rel_bias.shape
    onehot, valid = _na_onehot()
    rows = l * h * nr
    rb = jnp.pad(rel_bias.reshape(rows, nc), ((0, 0), (0, LANES - nc)))

    def body(rb_ref, oh_ref, valid_ref, e_ref):
        e = lax.dot_general(rb_ref[...], oh_ref[...], _NT, precision=lax.Precision.HIGHEST, preferred_element_type=F32)
        e_ref[...] = jnp.where(valid_ref[...] > 0, e, NEG_INF)

    e = pl.pallas_call(
        body, name=name, out_shape=jax.ShapeDtypeStruct((rows, GRID_W * GRID_W), F32), compiler_params=_params(),
    )(rb, jnp.asarray(onehot), jnp.asarray(valid.astype(np.float32)))
    e = e.reshape(l, h, nr, GRID_W, GRID_W)
    by_off = jnp.stack([e[:, :, off:off + NA_ROWS] for off in range(NA_OFFSETS)], axis=2)
    return by_off.transpose(0, 1, 2, 4, 3, 5).reshape(l, h, NA_OFFSETS, GRID_W, NA_KEYS)


def _na_collapse_bias(dbias, *, name):
    b, h = dbias.shape[:2]
    onehot, _ = _na_onehot()

    def fold(d_ref, e_ref):
        acc = [jnp.zeros((GRID_W, GRID_W), F32) for _ in range(NA_BIAS_ROWS)]
        for bi in range(b):
            for off in range(NA_OFFSETS):
                for kr in range(NA_ROWS):
                    acc[off + kr] = acc[off + kr] + d_ref[bi, off, :, kr * GRID_W:(kr + 1) * GRID_W]
        for i in range(NA_BIAS_ROWS):
            e_ref[i] = acc[i]

    de = pl.pallas_call(
        fold, name=name + "_fold", grid=(h,),
        in_specs=[pl.BlockSpec((b, None, NA_OFFSETS, GRID_W, NA_KEYS), lambda hi: (0, hi, 0, 0, 0))],
        out_specs=pl.BlockSpec((None, NA_BIAS_ROWS, GRID_W, GRID_W), lambda hi: (hi, 0, 0, 0)),
        out_shape=jax.ShapeDtypeStruct((h, NA_BIAS_ROWS, GRID_W, GRID_W), F32),
        compiler_params=_params(dimension_semantics=("parallel",)),
    )(dbias)

    def diag(e_ref, oh_ref, o_ref):
        o_ref[...] = lax.dot_general(e_ref[...], oh_ref[...], _NN, precision=lax.Precision.HIGHEST, preferred_element_type=F32)

    rows = h * NA_BIAS_ROWS
    drb = pl.pallas_call(
        diag, name=name + "_diag", out_shape=jax.ShapeDtypeStruct((rows, LANES), F32), compiler_params=_params(),
    )(de.reshape(rows, GRID_W * GRID_W), jnp.asarray(onehot))
    return drb[:, :NA_BIAS_COLS].reshape(h, NA_BIAS_ROWS, NA_BIAS_COLS)


def _na_row(r, n_rows):
    row_lo = jnp.clip(r - NA_ROWS // 2, 0, n_rows - NA_ROWS)
    return row_lo, row_lo - r + NA_ROWS - 1


def _na_head_spec(part, first, s):
    return pl.BlockSpec((None, None, s, HEAD_DIM), lambda b, h: (b, first + part * NA_HEADS + h, 0, 0))


def _na_attn_fwd(heads, bias, *, first, name):
    b, _, s, _ = heads.shape
    n_rows = s // GRID_W

    def body(q_ref, k_ref, v_ref, b_ref, o_ref, l_ref):
        def step(r, carry):
            row_lo, off = _na_row(r, n_rows)
            rows = pl.ds(pl.multiple_of(r * GRID_W, GRID_W), GRID_W)
            win = pl.ds(pl.multiple_of(row_lo * GRID_W, GRID_W), NA_KEYS)
            sc = _dot(q_ref[rows, :], k_ref[win, :], _NT) + b_ref[off]
            m = jnp.max(sc, axis=1, keepdims=True)
            p = jnp.exp(sc - m)
            den = jnp.sum(p, axis=1, keepdims=True)
            o_ref[rows, :] = (_dot(p.astype(BF16), v_ref[win, :], _NN) / den).astype(o_ref.dtype)
            l_ref[rows, :] = m + jnp.log(den)
            return carry

        lax.fori_loop(0, n_rows, step, 0, unroll=NA_UNROLL)

    per_head = lambda bi, h: (bi, h, 0, 0)
    return pl.pallas_call(
        body, name=name, grid=(b, NA_HEADS),
        in_specs=[_na_head_spec(part, first, s) for part in range(3)]
        + [pl.BlockSpec((None, NA_OFFSETS, GRID_W, NA_KEYS), lambda bi, h: (h, 0, 0, 0))],
        out_specs=[pl.BlockSpec((None, None, s, HEAD_DIM), per_head), pl.BlockSpec((None, None, s, 1), per_head)],
        out_shape=[jax.ShapeDtypeStruct((b, NA_HEADS, s, HEAD_DIM), BF16), jax.ShapeDtypeStruct((b, NA_HEADS, s, 1), F32)],
        compiler_params=_params(dimension_semantics=("parallel", "parallel")),
    )(heads, heads, heads, bias)


def _na_attn_bwd(heads, bias, out, lse, dout, *, first, name):
    b, _, s, _ = heads.shape
    n_rows = s // GRID_W

    def body(q_ref, k_ref, v_ref, b_ref, o_ref, l_ref, do_ref, d_ref, db_ref):
        d_ref[...] = jnp.zeros_like(d_ref)
        db_ref[...] = jnp.zeros_like(db_ref)

        def step(r, carry):
            row_lo, off = _na_row(r, n_rows)
            rows = pl.ds(pl.multiple_of(r * GRID_W, GRID_W), GRID_W)
            win = pl.ds(pl.multiple_of(row_lo * GRID_W, GRID_W), NA_KEYS)
            q, k, v = q_ref[rows, :], k_ref[win, :], v_ref[win, :]
            do = do_ref[rows, :]
            delta = jnp.sum(do * o_ref[rows, :].astype(F32), axis=1, keepdims=True)
            do_b = do.astype(BF16)
            p = jnp.exp(_dot(q, k, _NT) + b_ref[off] - l_ref[rows, :])
            ds = p * (_dot(do_b, v, _NT) - delta)
            db_ref[off] += ds
            ds_b = ds.astype(BF16)
            d_ref[0, rows, :] = _dot(ds_b, k, _NN)
            d_ref[1, win, :] += _dot(ds_b, q, _TN)
            d_ref[2, win, :] += _dot(p.astype(BF16), do_b, _TN)
            return carry

        lax.fori_loop(0, n_rows, step, 0, unroll=NA_UNROLL)

    per_head = lambda bi, h: (bi, h, 0, 0)
    return pl.pallas_call(
        body, name=name, grid=(b, NA_HEADS),
        in_specs=[_na_head_spec(part, first, s) for part in range(3)]
        + [pl.BlockSpec((None, NA_OFFSETS, GRID_W, NA_KEYS), lambda bi, h: (h, 0, 0, 0)),
           pl.BlockSpec((None, None, s, HEAD_DIM), per_head), pl.BlockSpec((None, None, s, 1), per_head),
           pl.BlockSpec((None, None, s, HEAD_DIM), per_head)],
        out_specs=[pl.BlockSpec((None, None, 3, s, HEAD_DIM), lambda bi, h: (bi, h, 0, 0, 0)),
                   pl.BlockSpec((None, None, NA_OFFSETS, GRID_W, NA_KEYS), lambda bi, h: (bi, h, 0, 0, 0))],
        out_shape=[jax.ShapeDtypeStruct((b, NA_HEADS, 3, s, HEAD_DIM), F32),
                   jax.ShapeDtypeStruct((b, NA_HEADS, NA_OFFSETS, GRID_W, NA_KEYS), F32)],
        compiler_params=_params(dimension_semantics=("parallel", "parallel")),
    )(heads, heads, heads, bias, out, lse, dout)


GATE_TILE = 256


def _gate_fwd(proj, z, *, gate_col, tt, name):
    _, t, d = z.shape
    nj = d // GATE_TILE
    c0 = gate_col // GATE_TILE

    def body(ga_ref, gb_ref, za_ref, zb_ref, o_ref):
        o_ref[...] = (jax.nn.sigmoid(ga_ref[...]) * za_ref[...] + jax.nn.sigmoid(gb_ref[...]) * zb_ref[...]).astype(BF16)

    return pl.pallas_call(
        body, name=name, grid=(t // tt, nj),
        in_specs=[pl.BlockSpec((tt, GATE_TILE), lambda i, j: (i, c0 + j)),
                  pl.BlockSpec((tt, GATE_TILE), lambda i, j: (i, c0 + nj + j)),
                  pl.BlockSpec((None, tt, GATE_TILE), lambda i, j: (0, i, j)),
                  pl.BlockSpec((None, tt, GATE_TILE), lambda i, j: (1, i, j))],
        out_specs=pl.BlockSpec((tt, GATE_TILE), lambda i, j: (i, j)), out_shape=jax.ShapeDtypeStruct((t, d), BF16),
        compiler_params=_params(dimension_semantics=("parallel", "parallel")),
    )(proj, proj, z, z)


def _gate_bwd(dm, proj, z, *, gate_col, tt, name):
    _, t, d = z.shape
    nj = d // GATE_TILE
    c0 = gate_col // GATE_TILE

    def body(dm_ref, g_ref, z_ref, dz_ref, dg_ref):
        dmv = dm_ref[...]
        sg = jax.nn.sigmoid(g_ref[...])
        dz_ref[...] = (dmv * sg).astype(BF16)
        dg_ref[...] = (dmv * z_ref[...] * sg * (1.0 - sg)).astype(BF16)

    return pl.pallas_call(
        body, name=name, grid=(t // tt, 2 * nj),
        in_specs=[pl.BlockSpec((tt, GATE_TILE), lambda i, j: (i, j % nj)),
                  pl.BlockSpec((tt, GATE_TILE), lambda i, j: (i, c0 + j)),
                  pl.BlockSpec((None, tt, GATE_TILE), lambda i, j: (j // nj, i, j % nj))],
        out_specs=[pl.BlockSpec((None, tt, GATE_TILE), lambda i, j: (j // nj, i, j % nj)),
                   pl.BlockSpec((tt, GATE_TILE), lambda i, j: (i, c0 + j))],
        out_shape=[jax.ShapeDtypeStruct((2, t, d), BF16), jax.ShapeDtypeStruct(proj.shape, BF16)],
        compiler_params=_params(dimension_semantics=("parallel", "parallel")),
    )(dm, proj, z)


def _adamw(w, g, m, v, *, name):
    shape = w.shape
    w2, g2, m2, v2 = (t.reshape(-1, shape[-1]) for t in (w, g, m, v))
    rows, cols = w2.shape
    tr = rows
    for cand in (512, 256, 128, 64, 32, 16, 8):
        if rows % cand == 0:
            tr = cand
            break

    def body(w_ref, g_ref, m_ref, v_ref, d_ref, nm_ref, nv_ref):
        gv = g_ref[...]
        nm = ADAM_B1 * m_ref[...] + (1.0 - ADAM_B1) * gv
        nv = ADAM_B2 * v_ref[...] + (1.0 - ADAM_B2) * (gv * gv)
        m_hat = nm / (1.0 - ADAM_B1 ** ADAM_STEP)
        v_hat = nv / (1.0 - ADAM_B2 ** ADAM_STEP)
        d_ref[...] = -ADAM_LR * (m_hat / (jnp.sqrt(v_hat) + ADAM_EPS) + ADAM_WD * w_ref[...])
        nm_ref[...] = nm
        nv_ref[...] = nv

    blk = pl.BlockSpec((tr, cols), lambda i: (i, 0))
    out = jax.ShapeDtypeStruct((rows, cols), F32)
    res = pl.pallas_call(
        body, name=name, grid=(rows // tr,), in_specs=[blk] * 4, out_specs=[blk] * 3, out_shape=[out] * 3,
        compiler_params=_params(dimension_semantics=("parallel",)),
    )(w2, g2, m2, v2)
    return tuple(t.reshape(shape) for t in res)


def _my_place():
    return lax.axis_index("x"), lax.axis_index("y"), lax.axis_index("c")


def _other_chips(x, y):
    return [(1 - x, y), (x, 1 - y), (1 - x, 1 - y)]


def _chip_no(chip):
    return 2 * chip[0] + chip[1]


def _window(ref, kind, size, chip, lead):
    if kind == "col":
        return ref.at[(*lead, slice(None), pl.ds(pl.multiple_of(chip * size, LANES), size))]
    if kind == "row":
        return ref.at[(*lead, pl.ds(pl.multiple_of(chip * size, BF16_ROWS), size), slice(None))]
    shard = size + HEAD_DIM
    if kind == "win_main":
        return ref.at[(*lead, slice(None), pl.ds(pl.multiple_of(chip * shard + HEAD_DIM * (chip % 2), LANES), size))]
    assert kind == "win_strad"
    return ref.at[(*lead, slice(None), pl.ds(pl.multiple_of(size + 2 * shard * (chip // 2), LANES), LANES))]


def _gather_weights(shards, kinds, *, name):
    n_w = len(shards)

    def full_shape(sh, kind):
        l, k, n = sh.shape
        return {"col": (l, k, N_CHIPS * n), "row": (l, N_CHIPS * k, n), "win_main": (l, k, N_CHIPS * (n + HEAD_DIM)),
                "slot": (N_CHIPS, l, k, n)}[kind]

    def body(*refs):
        src, dst = refs[:n_w], refs[n_w:2 * n_w]
        send_sems, recv_sems = refs[2 * n_w:]
        x, y, c = _my_place()
        mine = 2 * x + y
        sibling = (x, y, 1 - c)
        chips = _other_chips(x, y)

        def win(i, chip, layer):
            if kinds[i] == "slot":
                return dst[i].at[chip, layer]
            size = shards[i].shape[1] if kinds[i] == "row" else shards[i].shape[2]
            return _window(dst[i], kinds[i], size, chip, (layer,))

        def copy(sem, window, to, source=None):
            return pltpu.make_async_remote_copy(src_ref=window if source is None else source, dst_ref=window,
                                                send_sem=send_sems.at[sem], recv_sem=recv_sems.at[sem],
                                                device_id=to, device_id_type=MESH)

        first =[copy(3 * i + k, win(i, mine, c), (*chip, c), source=src[i].at[c])
                 for k, chip in enumerate(chips) for i in range(n_w)]
        for cp in first:
            cp.start()
        passed = []
        for k, chip in enumerate(chips):
            for i in range(n_w):
                landed = win(i, _chip_no(chip), c)
                copy(3 * i + k, landed, sibling).wait_recv()
                passed.append(copy(3 * n_w + 3 * i + k, landed, sibling))
                passed[-1].start()
        for k, chip in enumerate(chips):
            for i in range(n_w):
                copy(3 * n_w + 3 * i + k, win(i, _chip_no(chip), 1 - c), sibling).wait_recv()
        for cp in first + passed:
            cp.wait_send()

    return pl.pallas_call(
        body, name=name, in_specs=[HBM] * n_w, out_specs=[HBM] * n_w,
        out_shape=[jax.ShapeDtypeStruct(full_shape(sh, kind), sh.dtype) for sh, kind in zip(shards, kinds)],
        scratch_shapes=[pltpu.SemaphoreType.DMA((6 * n_w,)), pltpu.SemaphoreType.DMA((6 * n_w,))],
    )(*shards)


def _grads_to_sibling(grads, *, name):
    n_w = len(grads)

    def body(*refs):
        src, dst = refs[:n_w], refs[n_w:2 * n_w]
        send_sems, recv_sems = refs[2 * n_w:]
        x, y, c = _my_place()
        cps = [pltpu.make_async_remote_copy(src_ref=src[i].at[1 - c], dst_ref=dst[i], send_sem=send_sems.at[i],
                                            recv_sem=recv_sems.at[i], device_id=(x, y, 1 - c), device_id_type=MESH)
               for i in range(n_w)]
        for cp in cps:
            cp.start()
        for cp in cps:
            cp.wait()

    return pl.pallas_call(
        body, name=name, in_specs=[HBM] * n_w, out_specs=[HBM] * n_w,
        out_shape=[jax.ShapeDtypeStruct(g.shape[1:], g.dtype) for g in grads],
        scratch_shapes=[pltpu.SemaphoreType.DMA((n_w,)), pltpu.SemaphoreType.DMA((n_w,))],
    )(*grads)


def _pair_add(mine2, other, *, name):
    _, k, n = mine2.shape
    tr = _div_tile(k, 512, BF16_ROWS)
    c = lax.axis_index("c").astype(jnp.int32).reshape(1)

    def body(c_ref, a_ref, b_ref, o_ref):
        o_ref[...] = (a_ref[...].astype(F32) + b_ref[...].astype(F32)).astype(o_ref.dtype)

    return pl.pallas_call(
        body, name=name,
        grid_spec=pltpu.PrefetchScalarGridSpec(
            num_scalar_prefetch=1, grid=(k // tr,),
            in_specs=[pl.BlockSpec((None, tr, n), lambda i, c_ref: (c_ref[0], i, 0)),
                      pl.BlockSpec((tr, n), lambda i, c_ref: (i, 0))],
            out_specs=pl.BlockSpec((tr, n), lambda i, c_ref: (i, 0))),
        out_shape=jax.ShapeDtypeStruct((k, n), mine2.dtype),
        compiler_params=_params(dimension_semantics=("parallel",)),
    )(c, mine2, other)


def _grads_to_chips(pairs, kinds, sizes, *, name):
    n_w = len(pairs)

    def shard_shape(p, kind, size):
        return {"col": (p.shape[0], size), "row": (size, p.shape[1]), "win_main": (p.shape[0], size),
                "win_strad": (p.shape[0], LANES)}[kind]

    def body(*refs):
        src, dst = refs[:n_w], refs[n_w:2 * n_w]
        send_sems, recv_sems = refs[2 * n_w:]
        x, y, c = _my_place()
        mine = 2 * x + y
        chips = _other_chips(x, y)

        def copy(i, k, chip, window_of, slab):
            return pltpu.make_async_remote_copy(src_ref=_window(src[i], kinds[i], sizes[i], window_of, ()),
                                                dst_ref=dst[i].at[slab], send_sem=send_sems.at[3 * i + k],
                                                recv_sem=recv_sems.at[3 * i + k], device_id=(*chip, c), device_id_type=MESH)

        sends = [copy(i, k, chip, _chip_no(chip), mine) for k, chip in enumerate(chips) for i in range(n_w)]
        for cp in sends:
            cp.start()
        for k, chip in enumerate(chips):
            for i in range(n_w):
                copy(i, k, chip, mine, _chip_no(chip)).wait_recv()
        for cp in sends:
            cp.wait_send()

    return pl.pallas_call(
        body, name=name, in_specs=[HBM] * n_w, out_specs=[HBM] * n_w,
        out_shape=[jax.ShapeDtypeStruct((N_CHIPS,) + shard_shape(p, kind, size), p.dtype)
                   for p, kind, size in zip(pairs, kinds, sizes)],
        scratch_shapes=[pltpu.SemaphoreType.DMA((3 * n_w,)), pltpu.SemaphoreType.DMA((3 * n_w,))],
    )(*pairs)


def _sum_slabs(slabs, pair, kind, size, *, name):
    n_s, k, n = slabs.shape
    tr = _div_tile(k, 512, BF16_ROWS)
    tc = n if kind in ("col", "row") else LANES
    x, y, c = _my_place()
    mine = 2 * x + y
    shard = size + HEAD_DIM
    row0 = mine * (k // tr) if kind == "row" else 0
    col0 = {"col": mine, "row": 0, "win_main": (mine * shard + HEAD_DIM * (mine % 2)) // LANES,
            "win_strad": (size + 2 * shard * (mine // 2)) // LANES}[kind]
    scalars = jnp.stack([c, mine, row0, col0]).astype(jnp.int32)

    def body(s_ref, slab_ref, own_ref, o_ref):
        me = s_ref[1]
        acc = jnp.zeros(o_ref.shape, F32)
        for i in range(n_s):
            acc = acc + jnp.where(me == i, own_ref[...], slab_ref[i]).astype(F32)
        o_ref[...] = acc

    return pl.pallas_call(
        body, name=name,
        grid_spec=pltpu.PrefetchScalarGridSpec(
            num_scalar_prefetch=1, grid=(k // tr, n // tc),
            in_specs=[pl.BlockSpec((n_s, tr, tc), lambda i, j, s: (0, i, j)),
                      pl.BlockSpec((tr, tc), lambda i, j, s: (s[2] + i, s[3] + j))],
            out_specs=pl.BlockSpec((None, tr, tc), lambda i, j, s: (s[0], i, j))),
        out_shape=jax.ShapeDtypeStruct((2, k, n), F32),
        compiler_params=_params(dimension_semantics=("parallel", "parallel")),
    )(scalars, slabs, pair)


def _exchange_layers(bufs, *, name):
    n_w = len(bufs)

    def body(*refs):
        dst = refs[n_w:2 * n_w]
        send_sems, recv_sems = refs[2 * n_w:]
        x, y, c = _my_place()

        def copy(i, layer):
            return pltpu.make_async_remote_copy(src_ref=dst[i].at[layer], dst_ref=dst[i].at[layer], send_sem=send_sems.at[i],
                                                recv_sem=recv_sems.at[i], device_id=(x, y, 1 - c), device_id_type=MESH)

        sends = [copy(i, c) for i in range(n_w)]
        for cp in sends:
            cp.start()
        for i in range(n_w):
            copy(i, 1 - c).wait_recv()
        for cp in sends:
            cp.wait_send()

    return pl.pallas_call(
        body, name=name, in_specs=[HBM] * n_w, out_specs=[HBM] * n_w,
        out_shape=[jax.ShapeDtypeStruct(b.shape, b.dtype) for b in bufs],
        input_output_aliases={i: i for i in range(n_w)},
        scratch_shapes=[pltpu.SemaphoreType.DMA((n_w,)), pltpu.SemaphoreType.DMA((n_w,))],
    )(*bufs)


def _all_sum_small(v, *, name):
    r = v.shape[0]
    relations = [(dx, dy, dc) for dx in (0, 1) for dy in (0, 1) for dc in (0, 1)][1:]

    def body(v_ref, o_ref, buf, send_sems, recv_sems):
        x, y, c = _my_place()
        me = 4 * x + 2 * y + c
        buf[me] = v_ref[...]
        peers = [(x + dx - 2 * x * dx, y + dy - 2 * y * dy, c + dc - 2 * c * dc) for dx, dy, dc in relations]

        def copy(k, slot):
            return pltpu.make_async_remote_copy(src_ref=v_ref, dst_ref=buf.at[slot], send_sem=send_sems.at[k],
                                                recv_sem=recv_sems.at[k], device_id=peers[k], device_id_type=MESH)

        sends = [copy(k, me) for k in range(len(relations))]
        for cp in sends:
            cp.start()
        for k, (px, py, pc) in enumerate(peers):
            copy(k, 4 * px + 2 * py + pc).wait_recv()
        for cp in sends:
            cp.wait_send()
        acc = buf[0]
        for i in range(1, 8):
            acc = acc + buf[i]
        o_ref[...] = acc

    vm = pl.BlockSpec(memory_space=pltpu.VMEM)
    return pl.pallas_call(
        body, name=name, in_specs=[vm], out_specs=vm, out_shape=jax.ShapeDtypeStruct((r, LANES), F32),
        scratch_shapes=[pltpu.VMEM((8, r, LANES), F32), pltpu.SemaphoreType.DMA((7,)), pltpu.SemaphoreType.DMA((7,))],
    )(v)


SHARDED = (("ffn1_w_up", "col"), ("ffn1_w_down", "row"), ("w_in", "win"), ("w_branch_a", "col"),
           ("w_branch_b", "col"), ("w_out", "row"), ("ffn2_w_up", "col"), ("ffn2_w_down", "row"))
REPLICATED = ("ffn1_norm", "mix_norm", "na_rel_bias", "ffn2_norm", "final_norm")


def _gather_all_weights(w):
    even = lax.axis_index("y") == 0
    shards, kinds, names = [], [], []
    for name, kind in SHARDED:
        wb = w[name].astype(BF16)
        if kind == "win":
            main = wb.shape[-1] - HEAD_DIM
            assert main % LANES == 0
            zeros = jnp.zeros(wb.shape[:-1] + (HEAD_DIM,), BF16)
            shards += [jnp.where(even, wb[..., :main], wb[..., HEAD_DIM:]),
                       jnp.where(even, jnp.concatenate([wb[..., main:], zeros], -1),
                                 jnp.concatenate([zeros, wb[..., :HEAD_DIM]], -1))]
            kinds += ["win_main", "slot"]
            names += [name, name + "_strad"]
        else:
            shards.append(wb)
            kinds.append(kind)
            names.append(name)
    full = dict(zip(names, _gather_weights(shards, kinds, name="gather_weights")))
    mine = 2 * lax.axis_index("x") + lax.axis_index("y")
    for nm, kind, sh in zip(names, kinds, shards):
        start = {"col": (0, 0, mine * sh.shape[2]), "row": (0, mine * sh.shape[1], 0), "slot": (mine, 0, 0, 0),
                 "win_main": (0, 0, mine * (sh.shape[2] + HEAD_DIM) + HEAD_DIM * (mine % 2))}[kind]
        full[nm] = lax.dynamic_update_slice(full[nm], sh[None] if kind == "slot" else sh, start)
    strad = full.pop("w_in_strad")
    for i in range(N_CHIPS // 2):
        lo = main + 2 * (main + HEAD_DIM) * i
        full["w_in"] = full["w_in"].at[:, :, lo:lo + LANES].set(strad[2 * i] + strad[2 * i + 1])
    return full


def _reduce_weight_grads(grads, shards):
    names, kinds, sizes, srcs = [], [], [], []
    for name, kind in SHARDED:
        shp = shards[name].shape
        if kind == "win":
            names += [name, name + "_strad"]
            kinds += ["win_main", "win_strad"]
            sizes += [shp[2] - HEAD_DIM] * 2
            srcs += [name, name]
        else:
            names.append(name)
            kinds.append(kind)
            sizes.append(shp[1] if kind == "row" else shp[2])
            srcs.append(name)
    uniq = [name for name, _ in SHARDED]
    arrived = dict(zip(uniq, _grads_to_sibling([grads[n] for n in uniq], name="grads_to_sibling")))
    pair = {n: _pair_add(grads[n], arrived[n], name=f"grads_pair_{n}") for n in uniq}
    slabs = _grads_to_chips([pair[s] for s in srcs], kinds, sizes, name="grads_to_chips")
    halves = [_sum_slabs(sl, pair[s], kind, size, name=f"grads_sum_{n}")
              for n, sl, s, kind, size in zip(names, slabs, srcs, kinds, sizes)]
    out = dict(zip(names, _exchange_layers(halves, name="grads_layers")))
    strad = out.pop("w_in_strad")
    even = lax.axis_index("y") == 0
    out["w_in"] = jnp.where(even, jnp.concatenate([out["w_in"], strad[..., :HEAD_DIM]], -1),
                            jnp.concatenate([strad[..., HEAD_DIM:], out["w_in"]], -1))
    return out


class _Grads:
    def __init__(self, depth):
        self.depth = depth
        self.arrays = {}

    def put(self, weight, layer, a, b, *, cols=None, col_off=0, **kw):
        self.arrays[weight] = _mm(a, b, mode="tn", out_dtype=BF16, out_slab=(layer, self.depth), out_cols=cols,
                                  out_col_off=col_off, out_into=self.arrays.get(weight), **kw)


def _ffn_fwd(x, norm_g, w_up, w_down, layer, tag):
    t, d = x.shape
    f = w_down.shape[1]
    h = _rms_fwd(x, norm_g, tt=512, name=f"{tag}_norm")
    a, gate, up = _mm_swiglu_fwd(h, w_up, layer, tm=512, tn=_div_tile(f, 1408), name=f"{tag}_up")
    x_out = _mm(a, w_down, mode="nn", out_dtype=F32, tm=512, tn=d, tk=f, alpha=0.5, res=x, b_sel=layer, name=f"{tag}_down")
    return x_out, (x, h, a, gate, up)


def _ffn_bwd(dx, saved, norm_g, w_up, w_down, layer, grads, wname, tag):
    x, h, a, gate, up = saved
    t, d = x.shape
    f = w_down.shape[1]
    tn = _div_tile(f, 1408)
    dxb = dx.astype(BF16)
    grads.put(f"{wname}_w_down", layer, a, dxb, tm=tn, tn=d, tk=1024, alpha=0.5, name=f"{tag}_dwd")
    d_gate, d_up = _mm_swiglu_bwd(dxb, w_down, layer, gate, up, alpha=0.5, tm=512, tn=tn, name=f"{tag}_da")
    grads.put(f"{wname}_w_up", layer, h, d_gate, cols=2 * f, tm=d, tn=tn, tk=1024, name=f"{tag}_dwg")
    grads.put(f"{wname}_w_up", layer, h, d_up, cols=2 * f, col_off=f // tn, tm=d, tn=tn, tk=1024, name=f"{tag}_dwu")
    dh = _mm(d_gate, w_up, mode="nt", out_dtype=F32, tm=512, tn=d, tk=f, b_sel=layer, name=f"{tag}_dh1")
    dh = _mm(d_up, w_up, mode="nt", out_dtype=F32, tm=512, tn=d, tk=f, b_sel=layer, b_k_off=1, res=dh, name=f"{tag}_dh2")
    return _rms_bwd(dh, x, norm_g, dx, tt=512, name=f"{tag}_dnorm")


def _to_heads(y, b, n_heads):
    t, w = y.shape
    return y.reshape(b, t // b, n_heads, HEAD_DIM).transpose(0, 2, 1, 3)


def _from_heads(y):
    b, n, s, hd = y.shape
    return y.transpose(0, 2, 1, 3).reshape(b * s, n * hd)


N_QKV = 3 * (DIL_HEADS + NA_HEADS) * HEAD_DIM


def _mixer_fwd(x, b, norm_g, full, layer, bias, tabs, tag):
    t, d = x.shape
    s = t // b
    n_in = full["w_in"].shape[2]
    h = _rms_fwd(x, norm_g, tt=512, name=f"{tag}_norm")
    proj = _mm(h, full["w_in"], mode="nn", out_dtype=F32, tm=512, tn=_div_tile(n_in, 2944), tk=d, b_sel=layer, name=f"{tag}_in")
    heads = _split_heads(proj.reshape(b, s, -1), *tabs, n_pairs=N_QKV // LANES, rot_pairs=DIL_HEADS,
                         scale_ranges=((0, DIL_HEADS // 2), (3 * DIL_HEADS // 2, (3 * DIL_HEADS + NA_HEADS) // 2)),
                         ts=s, name=f"{tag}_heads")
    ya, lse_a = _dil_attn_fwd(heads, tq=256, name=f"{tag}_dil")
    yb, lse_b = _na_attn_fwd(heads, bias, first=3 * DIL_HEADS, name=f"{tag}_na")
    ya2, yb2 = _from_heads(ya), _from_heads(yb)
    z = _mm(ya2, full["w_branch_a"], mode="nn", out_dtype=F32, tm=1024, tn=d, tk=ya2.shape[1], b_sel=layer,
            out_slab=(0, 2), name=f"{tag}_za")
    z = _mm(yb2, full["w_branch_b"], mode="nn", out_dtype=F32, tm=1024, tn=d, tk=yb2.shape[1], b_sel=layer,
            out_slab=(1, 2), out_into=z, name=f"{tag}_zb")
    merged = _gate_fwd(proj, z, gate_col=N_QKV, tt=1024, name=f"{tag}_gate")
    x_out = _mm(merged, full["w_out"], mode="nn", out_dtype=F32, tm=1024, tn=d, tk=d, res=x, b_sel=layer, name=f"{tag}_out")
    return x_out, (x, h, proj, heads, ya, lse_a, yb, lse_b, ya2, yb2, z, merged)


def _mixer_bwd(dx, b, saved, norm_g, full, layer, bias, tabs, grads, tag):
    x, h, proj, heads, ya, lse_a, yb, lse_b, ya2, yb2, z, merged = saved
    t, d = x.shape
    s = t // b
    n_in = full["w_in"].shape[2]
    dob = dx.astype(BF16)
    grads.put("w_out", layer, merged, dob, tm=d, tn=d, tk=1024, name=f"{tag}_dwo")
    dm = _mm(dob, full["w_out"], mode="nt", out_dtype=F32, tm=1024, tn=d, tk=d, b_sel=layer, name=f"{tag}_dm")
    dz, dproj = _gate_bwd(dm, proj, z, gate_col=N_QKV, tt=1024, name=f"{tag}_dgate")
    grads.put("w_branch_a", layer, ya2, dz, b_sel=0, tm=ya2.shape[1], tn=d, tk=1024, name=f"{tag}_dwa")
    grads.put("w_branch_b", layer, yb2, dz, b_sel=1, tm=yb2.shape[1], tn=d, tk=1024, name=f"{tag}_dwb")
    dya = _mm(dz, full["w_branch_a"], mode="nt", out_dtype=F32, tm=1024, tn=ya2.shape[1], tk=d, a_sel=0, b_sel=layer, name=f"{tag}_dya")
    dyb = _mm(dz, full["w_branch_b"], mode="nt", out_dtype=F32, tm=1024, tn=yb2.shape[1], tk=d, a_sel=1, b_sel=layer, name=f"{tag}_dyb")
    d_dil = _dil_attn_bwd(heads, ya, lse_a, _to_heads(dya, b, DIL_GROUP_HEADS), tq=256, name=f"{tag}_ddil")
    d_na, d_bias = _na_attn_bwd(heads, bias, yb, lse_b, _to_heads(dyb, b, NA_HEADS), first=3 * DIL_HEADS, name=f"{tag}_dna")
    dproj = _merge_heads(d_dil, *tabs, heads_per_row=DIL_GROUP_HEADS, rot_pairs=DIL_HEADS, scale_pairs=DIL_HEADS // 2,
                         ts=s, out_cols=n_in, tile_off=0, into=dproj.reshape(b, s, n_in), name=f"{tag}_dheads_a")
    dproj = _merge_heads(d_na, *tabs, heads_per_row=NA_HEADS, rot_pairs=0, scale_pairs=NA_HEADS // 2, ts=s,
                         out_cols=n_in, tile_off=3 * DIL_HEADS // 2, into=dproj, name=f"{tag}_dheads_b").reshape(t, n_in)
    grads.put("w_in", layer, h, dproj, tm=_div_tile(d, 512), tn=_div_tile(n_in, 2944), tk=1024, name=f"{tag}_dwin")
    dh = _mm(dproj, full["w_in"], mode="nt", out_dtype=F32, tm=512, tn=d, tk=_div_tile(n_in, 2944), b_sel=layer, name=f"{tag}_dh")
    dx_in, d_norm = _rms_bwd(dh, x, norm_g, dx, tt=512, name=f"{tag}_dnorm")
    d_rb = _na_collapse_bias(d_bias, name=f"{tag}_dbias")
    return dx_in, d_norm, d_rb


def kernel(x, ffn1_norm, ffn1_w_up, ffn1_w_down, mix_norm, w_in, na_rel_bias, w_branch_a, w_branch_b, w_out, ffn2_norm, ffn2_w_up, ffn2_w_down, final_norm, loss_target, m_ffn1_norm, m_ffn1_w_up, m_ffn1_w_down, m_mix_norm, m_w_in, m_na_rel_bias, m_w_branch_a, m_w_branch_b, m_w_out, m_ffn2_norm, m_ffn2_w_up, m_ffn2_w_down, m_final_norm, v_ffn1_norm, v_ffn1_w_up, v_ffn1_w_down, v_mix_norm, v_w_in, v_na_rel_bias, v_w_branch_a, v_w_branch_b, v_w_out, v_ffn2_norm, v_ffn2_w_up, v_ffn2_w_down, v_final_norm):
    w = dict(ffn1_norm=ffn1_norm, ffn1_w_up=ffn1_w_up, ffn1_w_down=ffn1_w_down, mix_norm=mix_norm, w_in=w_in,
             na_rel_bias=na_rel_bias, w_branch_a=w_branch_a, w_branch_b=w_branch_b, w_out=w_out, ffn2_norm=ffn2_norm,
             ffn2_w_up=ffn2_w_up, ffn2_w_down=ffn2_w_down, final_norm=final_norm)
    mom = dict(ffn1_norm=m_ffn1_norm, ffn1_w_up=m_ffn1_w_up, ffn1_w_down=m_ffn1_w_down, mix_norm=m_mix_norm, w_in=m_w_in,
               na_rel_bias=m_na_rel_bias, w_branch_a=m_w_branch_a, w_branch_b=m_w_branch_b, w_out=m_w_out,
               ffn2_norm=m_ffn2_norm, ffn2_w_up=m_ffn2_w_up, ffn2_w_down=m_ffn2_w_down, final_norm=m_final_norm)
    var = dict(ffn1_norm=v_ffn1_norm, ffn1_w_up=v_ffn1_w_up, ffn1_w_down=v_ffn1_w_down, mix_norm=v_mix_norm, w_in=v_w_in,
               na_rel_bias=v_na_rel_bias, w_branch_a=v_w_branch_a, w_branch_b=v_w_branch_b, w_out=v_w_out,
               ffn2_norm=v_ffn2_norm, ffn2_w_up=v_ffn2_w_up, ffn2_w_down=v_ffn2_w_down, final_norm=v_final_norm)
    b, s, d = x.shape
    t = b * s
    depth = ffn1_norm.shape[0]
    assert depth == 2, "core c of a chip sends / reduces layer c"
    shards = {name: w[name] for name, _ in SHARDED}

    full = _gather_all_weights(w)
    tabs = _rope_tables(s)
    bias = _na_expand_bias(na_rel_bias, name="na_bias")

    xc = x.reshape(t, d)
    saved = []
    for l in range(depth):
        xc, s1 = _ffn_fwd(xc, ffn1_norm[l:l + 1], full["ffn1_w_up"], full["ffn1_w_down"], l, f"l{l}_ffn1")
        xc, s2 = _mixer_fwd(xc, b, mix_norm[l:l + 1], full, l, bias[l], tabs, f"l{l}_mix")
        xc, s3 = _ffn_fwd(xc, ffn2_norm[l:l + 1], full["ffn2_w_up"], full["ffn2_w_down"], l, f"l{l}_ffn2")
        saved.append((s1, s2, s3))

    dx, d_final, loss_part = _final_loss(xc, final_norm.reshape(1, d), loss_target.reshape(t, d), tt=512, name="final_loss")
    grads = _Grads(depth)
    small = {name: [None] * depth for name in REPLICATED[:-1]}
    for l in reversed(range(depth)):
        s1, s2, s3 = saved[l]
        dx, small["ffn2_norm"][l] = _ffn_bwd(dx, s3, ffn2_norm[l:l + 1], full["ffn2_w_up"], full["ffn2_w_down"], l, grads,
                                             "ffn2", f"l{l}_ffn2")
        dx, small["mix_norm"][l], small["na_rel_bias"][l] = _mixer_bwd(
            dx, b, s2, mix_norm[l:l + 1], full, l, bias[l], tabs, grads, f"l{l}_mix")
        dx, small["ffn1_norm"][l] = _ffn_bwd(dx, s1, ffn1_norm[l:l + 1], full["ffn1_w_up"], full["ffn1_w_down"], l, grads,
                                             "ffn1", f"l{l}_ffn1")
    grad_x = dx.reshape(b, s, d)

    g_out = _reduce_weight_grads(grads.arrays, shards)
    parts = [jnp.stack(small[name]).reshape(-1) for name in REPLICATED[:-1]] + [d_final.reshape(-1), loss_part[0, :1]]
    sizes = [v.shape[0] for v in parts]
    flat = jnp.concatenate(parts)
    flat = jnp.pad(flat, (0, -flat.shape[0] % (8 * LANES)))
    small_sum = _all_sum_small(flat.reshape(-1, LANES), name="small_all_sum").reshape(-1)
    off = 0
    for name, n in zip(REPLICATED, sizes[:-1]):
        g_out[name] = small_sum[off:off + n].reshape(w[name].shape)
        off += n
    loss = small_sum[off]

    names = list(w)
    delta, new_m, new_v = {}, {}, {}
    for name in names:
        delta[name], new_m[name], new_v[name] = _adamw(w[name], g_out[name], mom[name], var[name], name=f"adamw_{name}")
    return (loss, grad_x, *[g_out[n] for n in names], *[delta[n] for n in names], *[new_m[n] for n in names],
            *[new_v[n] for n in names])
```

```python
import functools

import numpy as np
import jax
import jax.numpy as jnp
from jax import lax
from jax.experimental import pallas as pl
from jax.experimental.pallas import tpu as pltpu

F32, BF16 = jnp.float32, jnp.bfloat16
MESH = pl.DeviceIdType.MESH

HEAD_DIM = 64
DILATIONS = (1, 4, 16)
DIL_HALF = 64
DIL_GROUP_HEADS = 4
DIL_HEADS = 12
NA_HEADS = 8
GRID_W = 64
NA_ROWS = 8
NA_COLS = 16
ROPE_THETA = 10000.0
RMS_EPS = 1e-6
NEG_INF = -1e30
ADAM_LR, ADAM_B1, ADAM_B2, ADAM_EPS, ADAM_WD, ADAM_STEP = 0.001, 0.9, 0.999, 1e-08, 0.01, 10
QK_SCALE = HEAD_DIM ** -0.5

N_CHIPS = 4
LANES = 128
BF16_ROWS = 16
VMEM_LIMIT = 56 * 1024 * 1024

_NN = (((1,), (0,)), ((), ()))
_NT = (((1,), (1,)), ((), ()))
_TN = (((0,), (0,)), ((), ()))

HBM = pl.BlockSpec(memory_space=pl.ANY)


def _params(**kw):
    return pltpu.CompilerParams(vmem_limit_bytes=VMEM_LIMIT, **kw)


def _dot(a, b, dims):
    return lax.dot_general(a, b, dims, preferred_element_type=F32)


def _div_tile(n, cap, mult=LANES):
    best = None
    for t in range(mult, min(n, cap) + 1, mult):
        if n % t == 0:
            best = t
    return n if best is None else best


def _stacked(block, index, sel):
    if sel is None:
        return pl.BlockSpec(block, index)
    return pl.BlockSpec((None,) + block, lambda *g: (sel,) + index(*g))


def _mm(a, b, *, mode, out_dtype, tm, tn, tk, name, alpha=1.0, res=None, a_sel=None, b_sel=None, b_k_off=0,
        out_slab=None, out_cols=None, out_col_off=0, out_into=None):
    a2, b2 = a.shape[-2:], b.shape[-2:]
    if mode == "nn":
        (m, k), n = a2, b2[1]
        a_spec = _stacked((tm, tk), lambda i, j, kk: (i, kk), a_sel)
        b_spec = _stacked((tk, tn), lambda i, j, kk: (kk + b_k_off, j), b_sel)
        dims = _NN
    elif mode == "nt":
        (m, k), n = a2, b2[0]
        a_spec = _stacked((tm, tk), lambda i, j, kk: (i, kk), a_sel)
        b_spec = _stacked((tn, tk), lambda i, j, kk: (j, kk + b_k_off), b_sel)
        dims = _NT
    else:
        (k, m), n = a2, b2[1]
        a_spec = _stacked((tk, tm), lambda i, j, kk: (kk, i), a_sel)
        b_spec = _stacked((tk, tn), lambda i, j, kk: (kk + b_k_off, j), b_sel)
        dims = _TN
    assert m % tm == 0 and n % tn == 0 and k % tk == 0, (name, a.shape, b.shape)
    nk = k // tk
    has_res = res is not None
    if out_slab is None:
        o_spec = pl.BlockSpec((tm, tn), lambda i, j, kk: (i, j))
        out_shape = jax.ShapeDtypeStruct((m, n), out_dtype)
    else:
        o_spec = _stacked((tm, tn), lambda i, j, kk: (i, j + out_col_off), out_slab[0])
        out_shape = jax.ShapeDtypeStruct((out_slab[1], m, n if out_cols is None else out_cols), out_dtype)
    r_spec = pl.BlockSpec((tm, tn), lambda i, j, kk: (i, j))
    n_in = 2 + has_res + (out_into is not None)

    def body(*refs):
        a_ref, b_ref = refs[0], refs[1]
        r_ref = refs[2] if has_res else None
        o_ref = refs[n_in]
        p = _dot(a_ref[...], b_ref[...], dims)

        def finish(acc):
            y = acc * alpha if alpha != 1.0 else acc
            if has_res:
                y = y + r_ref[...].astype(F32)
            o_ref[...] = y.astype(o_ref.dtype)

        if nk == 1:
            finish(p)
        else:
            acc_ref = refs[n_in + 1]
            kk = pl.program_id(2)

            @pl.when(kk == 0)
            def _():
                acc_ref[...] = p

            @pl.when(kk > 0)
            def _():
                acc_ref[...] += p

            @pl.when(kk == nk - 1)
            def _():
                finish(acc_ref[...])

    operands = [a, b] + ([res] if has_res else [])
    in_specs = [a_spec, b_spec] + ([r_spec] if has_res else [])
    aliases = {}
    if out_into is not None:
        aliases = {len(operands): 0}
        operands.append(out_into)
        in_specs.append(HBM)
    return pl.pallas_call(
        body, name=name, grid=(m // tm, n // tn, nk), in_specs=in_specs, out_specs=o_spec, out_shape=out_shape,
        scratch_shapes=[pltpu.VMEM((tm, tn), F32)] if nk > 1 else [], input_output_aliases=aliases,
        compiler_params=_params(dimension_semantics=("parallel", "parallel", "arbitrary")),
    )(*operands)


def _mm_swiglu_fwd(h, w_up, layer, *, tm, tn, name):
    m, k = h.shape
    n = w_up.shape[2] // 2
    h_spec = pl.BlockSpec((tm, k), lambda i, j: (i, 0))
    wg_spec = pl.BlockSpec((None, k, tn), lambda i, j: (layer, 0, j))
    wu_spec = pl.BlockSpec((None, k, tn), lambda i, j: (layer, 0, j + n // tn))
    o_spec = pl.BlockSpec((tm, tn), lambda i, j: (i, j))

    def body(h_ref, wg_ref, wu_ref, a_ref, g_ref, u_ref):
        hb = h_ref[...]
        g = _dot(hb, wg_ref[...], _NN)
        u = _dot(hb, wu_ref[...], _NN)
        a_ref[...] = (g * jax.nn.sigmoid(g) * u).astype(BF16)
        g_ref[...] = g.astype(BF16)
        u_ref[...] = u.astype(BF16)

    out = jax.ShapeDtypeStruct((m, n), BF16)
    return pl.pallas_call(
        body, name=name, grid=(m // tm, n // tn), in_specs=[h_spec, wg_spec, wu_spec],
        out_specs=[o_spec] * 3, out_shape=[out] * 3,
        compiler_params=_params(dimension_semantics=("parallel", "parallel")),
    )(h, w_up, w_up)


def _mm_swiglu_bwd(dy, w_down, layer, gate, up, *, alpha, tm, tn, name):
    m, k = dy.shape
    n = w_down.shape[1]
    dy_spec = pl.BlockSpec((tm, k), lambda i, j: (i, 0))
    w_spec = pl.BlockSpec((None, tn, k), lambda i, j: (layer, j, 0))
    o_spec = pl.BlockSpec((tm, tn), lambda i, j: (i, j))

    def body(dy_ref, w_ref, g_ref, u_ref, dg_ref, du_ref):
        da = _dot(dy_ref[...], w_ref[...], _NT) * alpha
        g = g_ref[...].astype(F32)
        u = u_ref[...].astype(F32)
        sg = jax.nn.sigmoid(g)
        dg_ref[...] = (da * u * (sg * (1.0 + g * (1.0 - sg)))).astype(BF16)
        du_ref[...] = (da * (g * sg)).astype(BF16)

    out = jax.ShapeDtypeStruct((m, n), BF16)
    return pl.pallas_call(
        body, name=name, grid=(m // tm, n // tn), in_specs=[dy_spec, w_spec, o_spec, o_spec],
        out_specs=[o_spec] * 2, out_shape=[out] * 2,
        compiler_params=_params(dimension_semantics=("parallel", "parallel")),
    )(dy, w_down, gate, up)


def _rms_fwd(x, g, *, tt, name):
    t, d = x.shape

    def body(x_ref, g_ref, h_ref):
        xv = x_ref[...]
        rstd = lax.rsqrt(jnp.mean(xv * xv, axis=1, keepdims=True) + RMS_EPS)
        h_ref[...] = (xv * rstd * g_ref[...]).astype(BF16)

    return pl.pallas_call(
        body, name=name, grid=(t // tt,),
        in_specs=[pl.BlockSpec((tt, d), lambda i: (i, 0)), pl.BlockSpec((1, d), lambda i: (0, 0))],
        out_specs=pl.BlockSpec((tt, d), lambda i: (i, 0)), out_shape=jax.ShapeDtypeStruct((t, d), BF16),
        compiler_params=_params(dimension_semantics=("parallel",)),
    )(x, g)


def _rms_bwd(dh, x, g, dres, *, tt, name):
    t, d = x.shape

    def body(dh_ref, x_ref, g_ref, r_ref, dx_ref, dxb_ref, dg_ref):
        xv = x_ref[...]
        rstd = lax.rsqrt(jnp.mean(xv * xv, axis=1, keepdims=True) + RMS_EPS)
        xhat = xv * rstd
        dhv = dh_ref[...]
        dxhat = dhv * g_ref[...]
        dx = r_ref[...] + rstd * (dxhat - xhat * jnp.mean(dxhat * xhat, axis=1, keepdims=True))
        dx_ref[...] = dx
        dxb_ref[...] = dx.astype(BF16)

        @pl.when(pl.program_id(0) == 0)
        def _():
            dg_ref[...] = jnp.zeros_like(dg_ref)

        dg_ref[...] += jnp.sum(dhv * xhat, axis=0, keepdims=True)

    row = pl.BlockSpec((tt, d), lambda i: (i, 0))
    vec = pl.BlockSpec((1, d), lambda i: (0, 0))
    return pl.pallas_call(
        body, name=name, grid=(t // tt,), in_specs=[row, row, vec, row], out_specs=[row, row, vec],
        out_shape=[jax.ShapeDtypeStruct((t, d), F32), jax.ShapeDtypeStruct((t, d), BF16), jax.ShapeDtypeStruct((1, d), F32)],
        compiler_params=_params(dimension_semantics=("arbitrary",)),
    )(dh, x, g, dres)


def _final_loss(x, g, target, *, tt, name):
    t, d = x.shape

    def body(x_ref, g_ref, t_ref, dx_ref, dxb_ref, dg_ref, loss_ref):
        xv = x_ref[...]
        gv = g_ref[...]
        rstd = lax.rsqrt(jnp.mean(xv * xv, axis=1, keepdims=True) + RMS_EPS)
        xhat = xv * rstd
        err = xhat * gv - t_ref[...]
        dy = err * (1.0 / d)
        dxhat = dy * gv
        dx = rstd * (dxhat - xhat * jnp.mean(dxhat * xhat, axis=1, keepdims=True))
        dx_ref[...] = dx
        dxb_ref[...] = dx.astype(BF16)

        @pl.when(pl.program_id(0) == 0)
        def _():
            dg_ref[...] = jnp.zeros_like(dg_ref)
            loss_ref[...] = jnp.zeros_like(loss_ref)

        dg_ref[...] += jnp.sum(dy * xhat, axis=0, keepdims=True)
        part = 0.5 * jnp.sum(jnp.mean(err * err, axis=1, keepdims=True), axis=0, keepdims=True)
        loss_ref[...] += jnp.broadcast_to(part, loss_ref.shape)

    row = pl.BlockSpec((tt, d), lambda i: (i, 0))
    vec = pl.BlockSpec((1, d), lambda i: (0, 0))
    one = pl.BlockSpec((1, LANES), lambda i: (0, 0))
    return pl.pallas_call(
        body, name=name, grid=(t // tt,), in_specs=[row, vec, row], out_specs=[row, row, vec, one],
        out_shape=[jax.ShapeDtypeStruct((t, d), F32), jax.ShapeDtypeStruct((t, d), BF16), jax.ShapeDtypeStruct((1, d), F32),
                   jax.ShapeDtypeStruct((1, LANES), F32)],
        compiler_params=_params(dimension_semantics=("arbitrary",)),
    )(x, g, target)


def _swap_halves(x):
    lane = lax.broadcasted_iota(jnp.int32, x.shape, 1)
    return jnp.where((lane // 32) % 2 == 0, pltpu.roll(x, 96, 1), pltpu.roll(x, 32, 1))


def _rope_tables(s):
    half = HEAD_DIM // 2
    inv_freq = ROPE_THETA ** (-jnp.arange(half, dtype=F32) / half)
    ang = jnp.arange(s).astype(F32)[:, None] * inv_freq[None, :]
    cos, sin = jnp.cos(ang), jnp.sin(ang)
    return jnp.tile(cos, (1, 4)), jnp.concatenate([-sin, sin, -sin, sin], axis=1)


def _split_heads(proj, cos4, sin4, *, n_pairs, rot_pairs, scale_ranges, ts, name):
    b, s, _ = proj.shape

    def body(x_ref, c_ref, s_ref, o_ref):
        p = pl.program_id(1)
        is_q = functools.reduce(jnp.logical_or, [(p >= lo) & (p < hi) for lo, hi in scale_ranges])
        scale = jnp.where(is_q, QK_SCALE, 1.0)

        def put(y):
            o_ref[0] = y[:, :HEAD_DIM].astype(BF16)
            o_ref[1] = y[:, HEAD_DIM:].astype(BF16)

        @pl.when(p < rot_pairs)
        def _():
            x = x_ref[...]
            put((x * c_ref[...] + _swap_halves(x) * s_ref[...]) * scale)

        @pl.when(p >= rot_pairs)
        def _():
            put(x_ref[...] * scale)

    tab = pl.BlockSpec((ts, LANES), lambda bi, p, si: (si, 0))
    return pl.pallas_call(
        body, name=name, grid=(b, n_pairs, s // ts),
        in_specs=[pl.BlockSpec((None, ts, LANES), lambda bi, p, si: (bi, si, p)), tab, tab],
        out_specs=pl.BlockSpec((None, 2, ts, HEAD_DIM), lambda bi, p, si: (bi, p, si, 0)),
        out_shape=jax.ShapeDtypeStruct((b, 2 * n_pairs, s, HEAD_DIM), BF16),
        compiler_params=_params(dimension_semantics=("parallel", "parallel", "parallel")),
    )(proj, cos4, sin4)


def _merge_heads(dheads, cos4, sin4, *, heads_per_row, rot_pairs, scale_pairs, ts, out_cols, tile_off, into, name):
    b, hpr, r, s, _ = dheads.shape
    n_pairs = hpr * r // 2
    ppr = hpr // 2

    def body(d_ref, c_ref, s_ref, *rest):
        o_ref = rest[-1]
        p = pl.program_id(1)
        scale = jnp.where(p < scale_pairs, QK_SCALE, 1.0)

        @pl.when(p < rot_pairs)
        def _():
            dy = jnp.concatenate([d_ref[0], d_ref[1]], axis=1)
            o_ref[...] = ((dy * c_ref[...] - _swap_halves(dy) * s_ref[...]) * scale).astype(BF16)

        @pl.when(p >= rot_pairs)
        def _():
            o_ref[...] = (jnp.concatenate([d_ref[0], d_ref[1]], axis=1) * scale).astype(BF16)

    tab = pl.BlockSpec((ts, LANES), lambda bi, p, si: (si, 0))
    operands = [dheads, cos4, sin4] + ([] if into is None else [into])
    return pl.pallas_call(
        body, name=name, grid=(b, n_pairs, s // ts),
        in_specs=[pl.BlockSpec((None, 2, None, ts, HEAD_DIM), lambda bi, p, si: (bi, p % ppr, p // ppr, si, 0)), tab, tab]
        + ([] if into is None else [HBM]),
        out_specs=pl.BlockSpec((None, ts, LANES), lambda bi, p, si: (bi, si, p + tile_off)),
        out_shape=jax.ShapeDtypeStruct((b, s, out_cols), BF16),
        input_output_aliases={} if into is None else {3: 0},
        compiler_params=_params(dimension_semantics=("parallel", "parallel", "parallel")),
    )(*operands)


def _dil_window(g, q0, tq, s):
    pad = -(-DIL_HALF * DILATIONS[g] // LANES) * LANES
    width = tq + 2 * pad
    if width >= s:
        return 0, s
    return pl.multiple_of(jnp.clip(q0 - pad, 0, s - width), LANES), width


def _dil_mask(g, q0, start, shape):
    d = DILATIONS[g]
    diff = (q0 - start) + lax.broadcasted_iota(jnp.int32, shape, 0) - lax.broadcasted_iota(jnp.int32, shape, 1)
    ok = jnp.abs(diff) <= DIL_HALF * d
    if d > 1:
        ok = ok & ((diff & (d - 1)) == 0)
    return ok


def _dil_head_spec(part, g, s):
    return pl.BlockSpec((None, None, s, HEAD_DIM), lambda b, j: (b, part * DIL_HEADS + g * DIL_GROUP_HEADS + j, 0, 0))


def _dil_attn_fwd(heads, *, tq, name):
    b, _, s, _ = heads.shape
    n_g = len(DILATIONS)

    def body(*refs):
        qkv = refs[:3 * n_g]
        o_ref, l_ref = refs[3 * n_g:]

        def step(i, carry):
            q0 = pl.multiple_of(i * tq, tq)
            scores, wins = [], []
            for g in range(n_g):
                start, width = _dil_window(g, q0, tq, s)
                sc = _dot(qkv[3 * g][pl.ds(q0, tq), :], qkv[3 * g + 1][pl.ds(start, width), :], _NT)
                scores.append(jnp.where(_dil_mask(g, q0, start, sc.shape), sc, NEG_INF))
                wins.append((start, width))
            m = functools.reduce(jnp.maximum, [jnp.max(sc, axis=1, keepdims=True) for sc in scores])
            den = jnp.zeros((tq, 1), F32)
            acc = jnp.zeros((tq, HEAD_DIM), F32)
            for g in range(n_g):
                p = jnp.exp(scores[g] - m)
                den = den + jnp.sum(p, axis=1, keepdims=True)
                acc = acc + _dot(p.astype(BF16), qkv[3 * g + 2][pl.ds(*wins[g]), :], _NN)
            o_ref[pl.ds(q0, tq), :] = (acc / den).astype(o_ref.dtype)
            l_ref[pl.ds(q0, tq), :] = m + jnp.log(den)
            return carry

        lax.fori_loop(0, s // tq, step, 0)

    out = pl.BlockSpec((None, None, s, HEAD_DIM), lambda bi, j: (bi, j, 0, 0))
    lse = pl.BlockSpec((None, None, s, 1), lambda bi, j: (bi, j, 0, 0))
    return pl.pallas_call(
        body, name=name, grid=(b, DIL_GROUP_HEADS),
        in_specs=[_dil_head_spec(part, g, s) for g in range(n_g) for part in range(3)],
        out_specs=[out, lse],
        out_shape=[jax.ShapeDtypeStruct((b, DIL_GROUP_HEADS, s, HEAD_DIM), BF16),
                   jax.ShapeDtypeStruct((b, DIL_GROUP_HEADS, s, 1), F32)],
        compiler_params=_params(dimension_semantics=("parallel", "parallel")),
    )(*([heads] * (3 * n_g)))


def _dil_attn_bwd(heads, out, lse, dout, *, tq, name):
    b, _, s, _ = heads.shape
    n_g = len(DILATIONS)

    def body(*refs):
        qkv = refs[:3 * n_g]
        o_ref, l_ref, do_ref, d_ref = refs[3 * n_g:]
        d_ref[...] = jnp.zeros_like(d_ref)

        def step(i, carry):
            q0 = pl.multiple_of(i * tq, tq)
            do = do_ref[pl.ds(q0, tq), :]
            delta = jnp.sum(do * o_ref[pl.ds(q0, tq), :].astype(F32), axis=1, keepdims=True)
            lse_b = l_ref[pl.ds(q0, tq), :]
            do_b = do.astype(BF16)
            for g in range(n_g):
                start, width = _dil_window(g, q0, tq, s)
                win = pl.ds(start, width)
                q = qkv[3 * g][pl.ds(q0, tq), :]
                k = qkv[3 * g + 1][win, :]
                v = qkv[3 * g + 2][win, :]
                sc = _dot(q, k, _NT)
                p = jnp.where(_dil_mask(g, q0, start, sc.shape), jnp.exp(sc - lse_b), 0.0)
                ds = (p * (_dot(do_b, v, _NT) - delta)).astype(BF16)
                d_ref[g, pl.ds(q0, tq), :] = _dot(ds, k, _NN)
                d_ref[n_g + g, win, :] += _dot(ds, q, _TN)
                d_ref[2 * n_g + g, win, :] += _dot(p.astype(BF16), do_b, _TN)
            return carry

        lax.fori_loop(0, s // tq, step, 0)

    per_head = lambda bi, j: (bi, j, 0, 0)
    return pl.pallas_call(
        body, name=name, grid=(b, DIL_GROUP_HEADS),
        in_specs=[_dil_head_spec(part, g, s) for g in range(n_g) for part in range(3)]
        + [pl.BlockSpec((None, None, s, HEAD_DIM), per_head), pl.BlockSpec((None, None, s, 1), per_head),
           pl.BlockSpec((None, None, s, HEAD_DIM), per_head)],
        out_specs=pl.BlockSpec((None, None, 3 * n_g, s, HEAD_DIM), lambda bi, j: (bi, j, 0, 0, 0)),
        out_shape=jax.ShapeDtypeStruct((b, DIL_GROUP_HEADS, 3 * n_g, s, HEAD_DIM), F32),
        compiler_params=_params(dimension_semantics=("parallel", "parallel")),
    )(*([heads] * (3 * n_g)), out, lse, dout)


NA_BIAS_ROWS = 2 * NA_ROWS - 1
NA_BIAS_COLS = 2 * NA_COLS - 1
NA_BLOCK = 4
NA_SPAN = NA_ROWS + NA_BLOCK - 1
NA_Q = NA_BLOCK * GRID_W
NA_KEYS = NA_SPAN * GRID_W
NA_FORMS = 3


def _na_onehot():
    c = np.arange(GRID_W)[:, None]
    k = np.arange(GRID_W)[None, :]
    lo = np.clip(c - NA_COLS // 2, 0, GRID_W - NA_COLS)
    valid = (k >= lo) & (k < lo + NA_COLS)
    onehot = np.zeros((GRID_W, GRID_W, LANES), np.float32)
    cc, kk = np.nonzero(valid)
    onehot[cc, kk, kk - cc + NA_COLS - 1] = 1.0
    return onehot.reshape(GRID_W * GRID_W, LANES), valid.reshape(1, GRID_W * GRID_W)


def _na_block_rows(n_rows):
    table = np.full((NA_FORMS, NA_BLOCK, NA_SPAN), NA_BIAS_ROWS, np.int64)
    n_blocks = n_rows // NA_BLOCK
    for form, ib in enumerate((0, 1, n_blocks - 1)):
        base = min(max(NA_BLOCK * ib - NA_ROWS // 2, 0), n_rows - NA_SPAN)
        for rl in range(NA_BLOCK):
            r = NA_BLOCK * ib + rl
            row_lo = min(max(r - NA_ROWS // 2, 0), n_rows - NA_ROWS)
            for kl in range(NA_SPAN):
                if row_lo <= base + kl < row_lo + NA_ROWS:
                    table[form, rl, kl] = base + kl - r + NA_ROWS - 1
    return table


def _na_block(ib, n_rows):
    n_blocks = n_rows // NA_BLOCK
    base = jnp.clip(NA_BLOCK * ib - NA_ROWS // 2, 0, n_rows - NA_SPAN)
    return base, jnp.where(ib == 0, 0, jnp.where(ib == n_blocks - 1, 2, 1))


def _na_expand_bias(rel_bias, n_rows, *, name):
    l, h, nr, nc = rel_bias.shape
    onehot, valid = _na_onehot()
    rows = l * h * nr
    rb = jnp.pad(rel_bias.reshape(rows, nc), ((0, 0), (0, LANES - nc)))

    def body(rb_ref, oh_ref, valid_ref, e_ref):
        e = lax.dot_general(rb_ref[...], oh_ref[...], _NT, precision=lax.Precision.HIGHEST, preferred_element_type=F32)
        e_ref[...] = jnp.where(valid_ref[...] > 0, e, NEG_INF)

    e = pl.pallas_call(
        body, name=name, out_shape=jax.ShapeDtypeStruct((rows, GRID_W * GRID_W), F32), compiler_params=_params(),
    )(rb, jnp.asarray(onehot), jnp.asarray(valid.astype(np.float32)))
    e = e.reshape(l, h, nr, GRID_W, GRID_W)
    e = jnp.concatenate([e, jnp.full((l, h, 1, GRID_W, GRID_W), NEG_INF, F32)], axis=2)
    blocks = e[:, :, _na_block_rows(n_rows)]
    return blocks.transpose(0, 1, 2, 3, 5, 4, 6).reshape(l, h, NA_FORMS, NA_Q, NA_KEYS)


def _na_collapse_bias(dbias, n_rows, *, name):
    b, h = dbias.shape[:2]
    onehot, _ = _na_onehot()
    table = _na_block_rows(n_rows)

    def fold(d_ref, e_ref):
        acc = [jnp.zeros((GRID_W, GRID_W), F32) for _ in range(NA_BIAS_ROWS)]
        for form in range(NA_FORMS):
            for rl in range(NA_BLOCK):
                for kl in range(NA_SPAN):
                    i = int(table[form, rl, kl])
                    if i < NA_BIAS_ROWS:
                        for bi in range(b):
                            acc[i] = acc[i] + d_ref[bi, form, rl * GRID_W:(rl + 1) * GRID_W, kl * GRID_W:(kl + 1) * GRID_W]
        for i in range(NA_BIAS_ROWS):
            e_ref[i] = acc[i]

    de = pl.pallas_call(
        fold, name=name + "_fold", grid=(h,),
        in_specs=[pl.BlockSpec((b, None, NA_FORMS, NA_Q, NA_KEYS), lambda hi: (0, hi, 0, 0, 0))],
        out_specs=pl.BlockSpec((None, NA_BIAS_ROWS, GRID_W, GRID_W), lambda hi: (hi, 0, 0, 0)),
        out_shape=jax.ShapeDtypeStruct((h, NA_BIAS_ROWS, GRID_W, GRID_W), F32),
        compiler_params=_params(dimension_semantics=("parallel",)),
    )(dbias)

    def diag(e_ref, oh_ref, o_ref):
        o_ref[...] = lax.dot_general(e_ref[...], oh_ref[...], _NN, precision=lax.Precision.HIGHEST, preferred_element_type=F32)

    rows = h * NA_BIAS_ROWS
    drb = pl.pallas_call(
        diag, name=name + "_diag", out_shape=jax.ShapeDtypeStruct((rows, LANES), F32), compiler_params=_params(),
    )(de.reshape(rows, GRID_W * GRID_W), jnp.asarray(onehot))
    return drb[:, :NA_BIAS_COLS].reshape(h, NA_BIAS_ROWS, NA_BIAS_COLS)


def _na_head_spec(part, first, s):
    return pl.BlockSpec((None, None, s, HEAD_DIM), lambda b, h: (b, first + part * NA_HEADS + h, 0, 0))


def _na_attn_fwd(heads, bias, *, first, name):
    b, _, s, _ = heads.shape
    n_rows = s // GRID_W

    def body(q_ref, k_ref, v_ref, b_ref, o_ref, l_ref):
        def step(ib, carry):
            base, form = _na_block(ib, n_rows)
            rows = pl.ds(pl.multiple_of(ib * NA_Q, NA_Q), NA_Q)
            win = pl.ds(pl.multiple_of(base * GRID_W, GRID_W), NA_KEYS)
            sc = _dot(q_ref[rows, :], k_ref[win, :], _NT) + b_ref[form]
            m = jnp.max(sc, axis=1, keepdims=True)
            p = jnp.exp(sc - m)
            den = jnp.sum(p, axis=1, keepdims=True)
            o_ref[rows, :] = (_dot(p.astype(BF16), v_ref[win, :], _NN) / den).astype(o_ref.dtype)
            l_ref[rows, :] = m + jnp.log(den)
            return carry

        lax.fori_loop(0, n_rows // NA_BLOCK, step, 0)

    per_head = lambda bi, h: (bi, h, 0, 0)
    return pl.pallas_call(
        body, name=name, grid=(b, NA_HEADS),
        in_specs=[_na_head_spec(part, first, s) for part in range(3)]
        + [pl.BlockSpec((None, NA_FORMS, NA_Q, NA_KEYS), lambda bi, h: (h, 0, 0, 0))],
        out_specs=[pl.BlockSpec((None, None, s, HEAD_DIM), per_head), pl.BlockSpec((None, None, s, 1), per_head)],
        out_shape=[jax.ShapeDtypeStruct((b, NA_HEADS, s, HEAD_DIM), BF16), jax.ShapeDtypeStruct((b, NA_HEADS, s, 1), F32)],
        compiler_params=_params(dimension_semantics=("parallel", "parallel")),
    )(heads, heads, heads, bias)


def _na_attn_bwd(heads, bias, out, lse, dout, *, first, name):
    b, _, s, _ = heads.shape
    n_rows = s // GRID_W

    def body(q_ref, k_ref, v_ref, b_ref, o_ref, l_ref, do_ref, d_ref, db_ref):
        d_ref[...] = jnp.zeros_like(d_ref)
        db_ref[...] = jnp.zeros_like(db_ref)

        def step(ib, carry):
            base, form = _na_block(ib, n_rows)
            rows = pl.ds(pl.multiple_of(ib * NA_Q, NA_Q), NA_Q)
            win = pl.ds(pl.multiple_of(base * GRID_W, GRID_W), NA_KEYS)
            q, k, v = q_ref[rows, :], k_ref[win, :], v_ref[win, :]
            do = do_ref[rows, :]
            delta = jnp.sum(do * o_ref[rows, :].astype(F32), axis=1, keepdims=True)
            do_b = do.astype(BF16)
            p = jnp.exp(_dot(q, k, _NT) + b_ref[form] - l_ref[rows, :])
            ds = p * (_dot(do_b, v, _NT) - delta)
            db_ref[form] += ds
            ds_b = ds.astype(BF16)
            d_ref[0, rows, :] = _dot(ds_b, k, _NN)
            d_ref[1, win, :] += _dot(ds_b, q, _TN)
            d_ref[2, win, :] += _dot(p.astype(BF16), do_b, _TN)
            return carry

        lax.fori_loop(0, n_rows // NA_BLOCK, step, 0)

    per_head = lambda bi, h: (bi, h, 0, 0)
    return pl.pallas_call(
        body, name=name, grid=(b, NA_HEADS),
        in_specs=[_na_head_spec(part, first, s) for part in range(3)]
        + [pl.BlockSpec((None, NA_FORMS, NA_Q, NA_KEYS), lambda bi, h: (h, 0, 0, 0)),
           pl.BlockSpec((None, None, s, HEAD_DIM), per_head), pl.BlockSpec((None, None, s, 1), per_head),
           pl.BlockSpec((None, None, s, HEAD_DIM), per_head)],
        out_specs=[pl.BlockSpec((None, None, 3, s, HEAD_DIM), lambda bi, h: (bi, h, 0, 0, 0)),
                   pl.BlockSpec((None, None, NA_FORMS, NA_Q, NA_KEYS), lambda bi, h: (bi, h, 0, 0, 0))],
        out_shape=[jax.ShapeDtypeStruct((b, NA_HEADS, 3, s, HEAD_DIM), F32),
                   jax.ShapeDtypeStruct((b, NA_HEADS, NA_FORMS, NA_Q, NA_KEYS), F32)],
        compiler_params=_params(dimension_semantics=("parallel", "parallel")),
    )(heads, heads, heads, bias, out, lse, dout)


GATE_TILE = 256


def _gate_fwd(proj, z, *, gate_col, tt, name):
    _, t, d = z.shape
    nj = d // GATE_TILE
    c0 = gate_col // GATE_TILE

    def body(ga_ref, gb_ref, za_ref, zb_ref, o_ref):
        o_ref[...] = (jax.nn.sigmoid(ga_ref[...]) * za_ref[...] + jax.nn.sigmoid(gb_ref[...]) * zb_ref[...]).astype(BF16)

    return pl.pallas_call(
        body, name=name, grid=(t // tt, nj),
        in_specs=[pl.BlockSpec((tt, GATE_TILE), lambda i, j: (i, c0 + j)),
                  pl.BlockSpec((tt, GATE_TILE), lambda i, j: (i, c0 + nj + j)),
                  pl.BlockSpec((None, tt, GATE_TILE), lambda i, j: (0, i, j)),
                  pl.BlockSpec((None, tt, GATE_TILE), lambda i, j: (1, i, j))],
        out_specs=pl.BlockSpec((tt, GATE_TILE), lambda i, j: (i, j)), out_shape=jax.ShapeDtypeStruct((t, d), BF16),
        compiler_params=_params(dimension_semantics=("parallel", "parallel")),
    )(proj, proj, z, z)


def _gate_bwd(dm, proj, z, *, gate_col, tt, name):
    _, t, d = z.shape
    nj = d // GATE_TILE
    c0 = gate_col // GATE_TILE

    def body(dm_ref, g_ref, z_ref, dz_ref, dg_ref):
        dmv = dm_ref[...]
        sg = jax.nn.sigmoid(g_ref[...])
        dz_ref[...] = (dmv * sg).astype(BF16)
        dg_ref[...] = (dmv * z_ref[...] * sg * (1.0 - sg)).astype(BF16)

    return pl.pallas_call(
        body, name=name, grid=(t // tt, 2 * nj),
        in_specs=[pl.BlockSpec((tt, GATE_TILE), lambda i, j: (i, j % nj)),
                  pl.BlockSpec((tt, GATE_TILE), lambda i, j: (i, c0 + j)),
                  pl.BlockSpec((None, tt, GATE_TILE), lambda i, j: (j // nj, i, j % nj))],
        out_specs=[pl.BlockSpec((None, tt, GATE_TILE), lambda i, j: (j // nj, i, j % nj)),
                   pl.BlockSpec((tt, GATE_TILE), lambda i, j: (i, c0 + j))],
        out_shape=[jax.ShapeDtypeStruct((2, t, d), BF16), jax.ShapeDtypeStruct(proj.shape, BF16)],
        compiler_params=_params(dimension_semantics=("parallel", "parallel")),
    )(dm, proj, z)


def _adamw(w, g, m, v, *, name):
    shape = w.shape
    w2, g2, m2, v2 = (t.reshape(-1, shape[-1]) for t in (w, g, m, v))
    rows, cols = w2.shape
    tr = rows
    for cand in (512, 256, 128, 64, 32, 16, 8):
        if rows % cand == 0:
            tr = cand
            break

    def body(w_ref, g_ref, m_ref, v_ref, d_ref, nm_ref, nv_ref):
        gv = g_ref[...]
        nm = ADAM_B1 * m_ref[...] + (1.0 - ADAM_B1) * gv
        nv = ADAM_B2 * v_ref[...] + (1.0 - ADAM_B2) * (gv * gv)
        m_hat = nm / (1.0 - ADAM_B1 ** ADAM_STEP)
        v_hat = nv / (1.0 - ADAM_B2 ** ADAM_STEP)
        d_ref[...] = -ADAM_LR * (m_hat / (jnp.sqrt(v_hat) + ADAM_EPS) + ADAM_WD * w_ref[...])
        nm_ref[...] = nm
        nv_ref[...] = nv

    blk = pl.BlockSpec((tr, cols), lambda i: (i, 0))
    out = jax.ShapeDtypeStruct((rows, cols), F32)
    res = pl.pallas_call(
        body, name=name, grid=(rows // tr,), in_specs=[blk] * 4, out_specs=[blk] * 3, out_shape=[out] * 3,
        compiler_params=_params(dimension_semantics=("parallel",)),
    )(w2, g2, m2, v2)
    return tuple(t.reshape(shape) for t in res)


def _my_place():
    return lax.axis_index("x"), lax.axis_index("y"), lax.axis_index("c")


def _other_chips(x, y):
    return [(1 - x, y), (x, 1 - y), (1 - x, 1 - y)]


def _chip_no(chip):
    return 2 * chip[0] + chip[1]


def _window(ref, kind, size, chip, lead):
    if kind == "col":
        return ref.at[(*lead, slice(None), pl.ds(pl.multiple_of(chip * size, LANES), size))]
    if kind == "row":
        return ref.at[(*lead, pl.ds(pl.multiple_of(chip * size, BF16_ROWS), size), slice(None))]
    shard = size + HEAD_DIM
    if kind == "win_main":
        return ref.at[(*lead, slice(None), pl.ds(pl.multiple_of(chip * shard + HEAD_DIM * (chip % 2), LANES), size))]
    assert kind == "win_strad"
    return ref.at[(*lead, slice(None), pl.ds(pl.multiple_of(size + 2 * shard * (chip // 2), LANES), LANES))]


def _place_own(shard, kind, *, name):
    l, k, n = shard.shape
    full = {"col": (l, k, N_CHIPS * n), "row": (l, N_CHIPS * k, n), "win_main": (l, k, N_CHIPS * (n + HEAD_DIM)),
            "slot": (N_CHIPS, l, k, n)}[kind]
    tr = _div_tile(k, 512, BF16_ROWS)
    tc = LANES if kind == "win_main" else n
    mine = 2 * lax.axis_index("x") + lax.axis_index("y")
    row0 = mine * (k // tr) if kind == "row" else 0
    col0 = {"col": mine, "row": 0, "slot": 0, "win_main": (mine * (n + HEAD_DIM) + HEAD_DIM * (mine % 2)) // LANES}[kind]
    scalars = jnp.stack([mine, row0, col0]).astype(jnp.int32)

    def body(s_ref, i_ref, o_ref):
        o_ref[...] = i_ref[...]

    if kind == "slot":
        o_spec = pl.BlockSpec((None, None, tr, tc), lambda li, i, j, s: (s[0], li, i, j))
    else:
        o_spec = pl.BlockSpec((None, tr, tc), lambda li, i, j, s: (li, s[1] + i, s[2] + j))
    return pl.pallas_call(
        body, name=name,
        grid_spec=pltpu.PrefetchScalarGridSpec(
            num_scalar_prefetch=1, grid=(l, k // tr, n // tc),
            in_specs=[pl.BlockSpec((None, tr, tc), lambda li, i, j, s: (li, i, j))], out_specs=o_spec),
        out_shape=jax.ShapeDtypeStruct(full, shard.dtype),
        compiler_params=_params(dimension_semantics=("parallel", "parallel", "parallel")),
    )(scalars, shard)


def _gather_weights(shards, kinds, fulls, *, name):
    n_w = len(shards)

    def body(*refs):
        src, dst = refs[:n_w], refs[2 * n_w:3 * n_w]
        send_sems, recv_sems = refs[3 * n_w:]
        x, y, c = _my_place()
        mine = 2 * x + y
        sibling = (x, y, 1 - c)
        chips = _other_chips(x, y)

        def win(i, chip, layer):
            if kinds[i] == "slot":
                return dst[i].at[chip, layer]
            size = shards[i].shape[1] if kinds[i] == "row" else shards[i].shape[2]
            return _window(dst[i], kinds[i], size, chip, (layer,))

        def copy(sem, window, to, source=None):
            return pltpu.make_async_remote_copy(src_ref=window if source is None else source, dst_ref=window,
                                                send_sem=send_sems.at[sem], recv_sem=recv_sems.at[sem],
                                                device_id=to, device_id_type=MESH)

        first =[copy(3 * i + k, win(i, mine, c), (*chip, c), source=src[i].at[c])
                 for k, chip in enumerate(chips) for i in range(n_w)]
        for cp in first:
            cp.start()
        passed = []
        for k, chip in enumerate(chips):
            for i in range(n_w):
                landed = win(i, _chip_no(chip), c)
                copy(3 * i + k, landed, sibling).wait_recv()
                passed.append(copy(3 * n_w + 3 * i + k, landed, sibling))
                passed[-1].start()
        for k, chip in enumerate(chips):
            for i in range(n_w):
                copy(3 * n_w + 3 * i + k, win(i, _chip_no(chip), 1 - c), sibling).wait_recv()
        for cp in first + passed:
            cp.wait_send()

    return pl.pallas_call(
        body, name=name, in_specs=[HBM] * (2 * n_w), out_specs=[HBM] * n_w,
        out_shape=[jax.ShapeDtypeStruct(f.shape, f.dtype) for f in fulls],
        input_output_aliases={n_w + i: i for i in range(n_w)},
        scratch_shapes=[pltpu.SemaphoreType.DMA((6 * n_w,)), pltpu.SemaphoreType.DMA((6 * n_w,))],
    )(*shards, *fulls)


def _grads_to_sibling(grads, *, name):
    n_w = len(grads)

    def body(*refs):
        src, dst = refs[:n_w], refs[n_w:2 * n_w]
        send_sems, recv_sems = refs[2 * n_w:]
        x, y, c = _my_place()
        cps = [pltpu.make_async_remote_copy(src_ref=src[i].at[1 - c], dst_ref=dst[i], send_sem=send_sems.at[i],
                                            recv_sem=recv_sems.at[i], device_id=(x, y, 1 - c), device_id_type=MESH)
               for i in range(n_w)]
        for cp in cps:
            cp.start()
        for cp in cps:
            cp.wait()

    return pl.pallas_call(
        body, name=name, in_specs=[HBM] * n_w, out_specs=[HBM] * n_w,
        out_shape=[jax.ShapeDtypeStruct(g.shape[1:], g.dtype) for g in grads],
        scratch_shapes=[pltpu.SemaphoreType.DMA((n_w,)), pltpu.SemaphoreType.DMA((n_w,))],
    )(*grads)


def _pair_add(mine2, other, *, name):
    _, k, n = mine2.shape
    tr = _div_tile(k, 512, BF16_ROWS)
    c = lax.axis_index("c").astype(jnp.int32).reshape(1)

    def body(c_ref, a_ref, b_ref, o_ref):
        o_ref[...] = (a_ref[...].astype(F32) + b_ref[...].astype(F32)).astype(o_ref.dtype)

    return pl.pallas_call(
        body, name=name,
        grid_spec=pltpu.PrefetchScalarGridSpec(
            num_scalar_prefetch=1, grid=(k // tr,),
            in_specs=[pl.BlockSpec((None, tr, n), lambda i, c_ref: (c_ref[0], i, 0)),
                      pl.BlockSpec((tr, n), lambda i, c_ref: (i, 0))],
            out_specs=pl.BlockSpec((tr, n), lambda i, c_ref: (i, 0))),
        out_shape=jax.ShapeDtypeStruct((k, n), mine2.dtype),
        compiler_params=_params(dimension_semantics=("parallel",)),
    )(c, mine2, other)


def _grads_to_chips(pairs, kinds, sizes, *, name):
    n_w = len(pairs)

    def shard_shape(p, kind, size):
        return {"col": (p.shape[0], size), "row": (size, p.shape[1]), "win_main": (p.shape[0], size),
                "win_strad": (p.shape[0], LANES)}[kind]

    def body(*refs):
        src, dst = refs[:n_w], refs[n_w:2 * n_w]
        send_sems, recv_sems = refs[2 * n_w:]
        x, y, c = _my_place()
        mine = 2 * x + y
        chips = _other_chips(x, y)

        def copy(i, k, chip, window_of, slab):
            return pltpu.make_async_remote_copy(src_ref=_window(src[i], kinds[i], sizes[i], window_of, ()),
                                                dst_ref=dst[i].at[slab], send_sem=send_sems.at[3 * i + k],
                                                recv_sem=recv_sems.at[3 * i + k], device_id=(*chip, c), device_id_type=MESH)

        sends = [copy(i, k, chip, _chip_no(chip), mine) for k, chip in enumerate(chips) for i in range(n_w)]
        for cp in sends:
            cp.start()
        for k, chip in enumerate(chips):
            for i in range(n_w):
                copy(i, k, chip, mine, _chip_no(chip)).wait_recv()
        for cp in sends:
            cp.wait_send()

    return pl.pallas_call(
        body, name=name, in_specs=[HBM] * n_w, out_specs=[HBM] * n_w,
        out_shape=[jax.ShapeDtypeStruct((N_CHIPS,) + shard_shape(p, kind, size), p.dtype)
                   for p, kind, size in zip(pairs, kinds, sizes)],
        scratch_shapes=[pltpu.SemaphoreType.DMA((3 * n_w,)), pltpu.SemaphoreType.DMA((3 * n_w,))],
    )(*pairs)


def _sum_slabs(slabs, pair, kind, size, *, name):
    n_s, k, n = slabs.shape
    tr = _div_tile(k, 512, BF16_ROWS)
    tc = n if kind in ("col", "row") else LANES
    x, y, c = _my_place()
    mine = 2 * x + y
    shard = size + HEAD_DIM
    row0 = mine * (k // tr) if kind == "row" else 0
    col0 = {"col": mine, "row": 0, "win_main": (mine * shard + HEAD_DIM * (mine % 2)) // LANES,
            "win_strad": (size + 2 * shard * (mine // 2)) // LANES}[kind]
    scalars = jnp.stack([c, mine, row0, col0]).astype(jnp.int32)

    def body(s_ref, slab_ref, own_ref, o_ref):
        me = s_ref[1]
        acc = jnp.zeros(o_ref.shape, F32)
        for i in range(n_s):
            acc = acc + jnp.where(me == i, own_ref[...], slab_ref[i]).astype(F32)
        o_ref[...] = acc

    return pl.pallas_call(
        body, name=name,
        grid_spec=pltpu.PrefetchScalarGridSpec(
            num_scalar_prefetch=1, grid=(k // tr, n // tc),
            in_specs=[pl.BlockSpec((n_s, tr, tc), lambda i, j, s: (0, i, j)),
                      pl.BlockSpec((tr, tc), lambda i, j, s: (s[2] + i, s[3] + j))],
            out_specs=pl.BlockSpec((None, tr, tc), lambda i, j, s: (s[0], i, j))),
        out_shape=jax.ShapeDtypeStruct((2, k, n), F32),
        compiler_params=_params(dimension_semantics=("parallel", "parallel")),
    )(scalars, slabs, pair)


def _exchange_layers(bufs, *, name):
    n_w = len(bufs)

    def body(*refs):
        dst = refs[n_w:2 * n_w]
        send_sems, recv_sems = refs[2 * n_w:]
        x, y, c = _my_place()

        def copy(i, layer):
            return pltpu.make_async_remote_copy(src_ref=dst[i].at[layer], dst_ref=dst[i].at[layer], send_sem=send_sems.at[i],
                                                recv_sem=recv_sems.at[i], device_id=(x, y, 1 - c), device_id_type=MESH)

        sends = [copy(i, c) for i in range(n_w)]
        for cp in sends:
            cp.start()
        for i in range(n_w):
            copy(i, 1 - c).wait_recv()
        for cp in sends:
            cp.wait_send()

    return pl.pallas_call(
        body, name=name, in_specs=[HBM] * n_w, out_specs=[HBM] * n_w,
        out_shape=[jax.ShapeDtypeStruct(b.shape, b.dtype) for b in bufs],
        input_output_aliases={i: i for i in range(n_w)},
        scratch_shapes=[pltpu.SemaphoreType.DMA((n_w,)), pltpu.SemaphoreType.DMA((n_w,))],
    )(*bufs)


def _all_sum_small(v, *, name):
    r = v.shape[0]
    relations = [(dx, dy, dc) for dx in (0, 1) for dy in (0, 1) for dc in (0, 1)][1:]

    def body(v_ref, o_ref, buf, send_sems, recv_sems):
        x, y, c = _my_place()
        me = 4 * x + 2 * y + c
        buf[me] = v_ref[...]
        peers = [(x + dx - 2 * x * dx, y + dy - 2 * y * dy, c + dc - 2 * c * dc) for dx, dy, dc in relations]

        def copy(k, slot):
            return pltpu.make_async_remote_copy(src_ref=v_ref, dst_ref=buf.at[slot], send_sem=send_sems.at[k],
                                                recv_sem=recv_sems.at[k], device_id=peers[k], device_id_type=MESH)

        sends = [copy(k, me) for k in range(len(relations))]
        for cp in sends:
            cp.start()
        for k, (px, py, pc) in enumerate(peers):
            copy(k, 4 * px + 2 * py + pc).wait_recv()
        for cp in sends:
            cp.wait_send()
        acc = buf[0]
        for i in range(1, 8):
            acc = acc + buf[i]
        o_ref[...] = acc

    vm = pl.BlockSpec(memory_space=pltpu.VMEM)
    return pl.pallas_call(
        body, name=name, in_specs=[vm], out_specs=vm, out_shape=jax.ShapeDtypeStruct((r, LANES), F32),
        scratch_shapes=[pltpu.VMEM((8, r, LANES), F32), pltpu.SemaphoreType.DMA((7,)), pltpu.SemaphoreType.DMA((7,))],
    )(v)


SHARDED = (("ffn1_w_up", "col"), ("ffn1_w_down", "row"), ("w_in", "win"), ("w_branch_a", "col"),
           ("w_branch_b", "col"), ("w_out", "row"), ("ffn2_w_up", "col"), ("ffn2_w_down", "row"))
REPLICATED = ("ffn1_norm", "mix_norm", "na_rel_bias", "ffn2_norm", "final_norm")


def _gather_all_weights(w):
    even = lax.axis_index("y") == 0
    shards, kinds, names = [], [], []
    for name, kind in SHARDED:
        wb = w[name].astype(BF16)
        if kind == "win":
            main = wb.shape[-1] - HEAD_DIM
            assert main % LANES == 0
            zeros = jnp.zeros(wb.shape[:-1] + (HEAD_DIM,), BF16)
            shards += [jnp.where(even, wb[..., :main], wb[..., HEAD_DIM:]),
                       jnp.where(even, jnp.concatenate([wb[..., main:], zeros], -1),
                                 jnp.concatenate([zeros, wb[..., :HEAD_DIM]], -1))]
            kinds += ["win_main", "slot"]
            names += [name, name + "_strad"]
        else:
            shards.append(wb)
            kinds.append(kind)
            names.append(name)
    own = [_place_own(sh, kind, name=f"own_{nm}") for nm, kind, sh in zip(names, kinds, shards)]
    full = dict(zip(names, _gather_weights(shards, kinds, own, name="gather_weights")))
    strad = full.pop("w_in_strad")
    for i in range(N_CHIPS // 2):
        lo = main + 2 * (main + HEAD_DIM) * i
        full["w_in"] = full["w_in"].at[:, :, lo:lo + LANES].set(strad[2 * i] + strad[2 * i + 1])
    return full


def _reduce_weight_grads(grads, shards):
    names, kinds, sizes, srcs = [], [], [], []
    for name, kind in SHARDED:
        shp = shards[name].shape
        if kind == "win":
            names += [name, name + "_strad"]
            kinds += ["win_main", "win_strad"]
            sizes += [shp[2] - HEAD_DIM] * 2
            srcs += [name, name]
        else:
            names.append(name)
            kinds.append(kind)
            sizes.append(shp[1] if kind == "row" else shp[2])
            srcs.append(name)
    uniq = [name for name, _ in SHARDED]
    arrived = dict(zip(uniq, _grads_to_sibling([grads[n] for n in uniq], name="grads_to_sibling")))
    pair = {n: _pair_add(grads[n], arrived[n], name=f"grads_pair_{n}") for n in uniq}
    slabs = _grads_to_chips([pair[s] for s in srcs], kinds, sizes, name="grads_to_chips")
    halves = [_sum_slabs(sl, pair[s], kind, size, name=f"grads_sum_{n}")
              for n, sl, s, kind, size in zip(names, slabs, srcs, kinds, sizes)]
    out = dict(zip(names, _exchange_layers(halves, name="grads_layers")))
    strad = out.pop("w_in_strad")
    even = lax.axis_index("y") == 0
    out["w_in"] = jnp.where(even, jnp.concatenate([out["w_in"], strad[..., :HEAD_DIM]], -1),
                            jnp.concatenate([strad[..., HEAD_DIM:], out["w_in"]], -1))
    return out


class _Grads:
    def __init__(self, depth):
        self.depth = depth
        self.arrays = {}

    def put(self, weight, layer, a, b, *, cols=None, col_off=0, **kw):
        self.arrays[weight] = _mm(a, b, mode="tn", out_dtype=BF16, out_slab=(layer, self.depth), out_cols=cols,
                                  out_col_off=col_off, out_into=self.arrays.get(weight), **kw)


def _ffn_fwd(x, norm_g, w_up, w_down, layer, tag):
    t, d = x.shape
    f = w_down.shape[1]
    h = _rms_fwd(x, norm_g, tt=512, name=f"{tag}_norm")
    a, gate, up = _mm_swiglu_fwd(h, w_up, layer, tm=512, tn=_div_tile(f, 1408), name=f"{tag}_up")
    x_out = _mm(a, w_down, mode="nn", out_dtype=F32, tm=512, tn=d, tk=f, alpha=0.5, res=x, b_sel=layer, name=f"{tag}_down")
    return x_out, (x, h, a, gate, up)


def _ffn_bwd(dx, dxb, saved, norm_g, w_up, w_down, layer, grads, wname, tag):
    x, h, a, gate, up = saved
    t, d = x.shape
    f = w_down.shape[1]
    tn = _div_tile(f, 1408)
    grads.put(f"{wname}_w_down", layer, a, dxb, tm=tn, tn=d, tk=1024, alpha=0.5, name=f"{tag}_dwd")
    d_gate, d_up = _mm_swiglu_bwd(dxb, w_down, layer, gate, up, alpha=0.5, tm=512, tn=tn, name=f"{tag}_da")
    grads.put(f"{wname}_w_up", layer, h, d_gate, cols=2 * f, tm=d, tn=tn, tk=1024, name=f"{tag}_dwg")
    grads.put(f"{wname}_w_up", layer, h, d_up, cols=2 * f, col_off=f // tn, tm=d, tn=tn, tk=1024, name=f"{tag}_dwu")
    dh = _mm(d_gate, w_up, mode="nt", out_dtype=F32, tm=512, tn=d, tk=f, b_sel=layer, name=f"{tag}_dh1")
    dh = _mm(d_up, w_up, mode="nt", out_dtype=F32, tm=512, tn=d, tk=f, b_sel=layer, b_k_off=1, res=dh, name=f"{tag}_dh2")
    return _rms_bwd(dh, x, norm_g, dx, tt=512, name=f"{tag}_dnorm")


def _to_heads(y, b, n_heads):
    t, w = y.shape
    return y.reshape(b, t // b, n_heads, HEAD_DIM).transpose(0, 2, 1, 3)


def _from_heads(y):
    b, n, s, hd = y.shape
    return y.transpose(0, 2, 1, 3).reshape(b * s, n * hd)


N_QKV = 3 * (DIL_HEADS + NA_HEADS) * HEAD_DIM


def _mixer_fwd(x, b, norm_g, full, layer, bias, tabs, tag):
    t, d = x.shape
    s = t // b
    n_in = full["w_in"].shape[2]
    h = _rms_fwd(x, norm_g, tt=512, name=f"{tag}_norm")
    proj = _mm(h, full["w_in"], mode="nn", out_dtype=F32, tm=512, tn=_div_tile(n_in, 2944), tk=d, b_sel=layer, name=f"{tag}_in")
    heads = _split_heads(proj.reshape(b, s, -1), *tabs, n_pairs=N_QKV // LANES, rot_pairs=DIL_HEADS,
                         scale_ranges=((0, DIL_HEADS // 2), (3 * DIL_HEADS // 2, (3 * DIL_HEADS + NA_HEADS) // 2)),
                         ts=s, name=f"{tag}_heads")
    ya, lse_a = _dil_attn_fwd(heads, tq=256, name=f"{tag}_dil")
    yb, lse_b = _na_attn_fwd(heads, bias, first=3 * DIL_HEADS, name=f"{tag}_na")
    ya2, yb2 = _from_heads(ya), _from_heads(yb)
    z = _mm(ya2, full["w_branch_a"], mode="nn", out_dtype=F32, tm=1024, tn=d, tk=ya2.shape[1], b_sel=layer,
            out_slab=(0, 2), name=f"{tag}_za")
    z = _mm(yb2, full["w_branch_b"], mode="nn", out_dtype=F32, tm=1024, tn=d, tk=yb2.shape[1], b_sel=layer,
            out_slab=(1, 2), out_into=z, name=f"{tag}_zb")
    merged = _gate_fwd(proj, z, gate_col=N_QKV, tt=1024, name=f"{tag}_gate")
    x_out = _mm(merged, full["w_out"], mode="nn", out_dtype=F32, tm=1024, tn=d, tk=d, res=x, b_sel=layer, name=f"{tag}_out")
    return x_out, (x, h, proj, heads, ya, lse_a, yb, lse_b, ya2, yb2, z, merged)


def _mixer_bwd(dx, dob, b, saved, norm_g, full, layer, bias, tabs, grads, tag):
    x, h, proj, heads, ya, lse_a, yb, lse_b, ya2, yb2, z, merged = saved
    t, d = x.shape
    s = t // b
    n_in = full["w_in"].shape[2]
    grads.put("w_out", layer, merged, dob, tm=d, tn=d, tk=1024, name=f"{tag}_dwo")
    dm = _mm(dob, full["w_out"], mode="nt", out_dtype=F32, tm=1024, tn=d, tk=d, b_sel=layer, name=f"{tag}_dm")
    dz, dproj = _gate_bwd(dm, proj, z, gate_col=N_QKV, tt=1024, name=f"{tag}_dgate")
    grads.put("w_branch_a", layer, ya2, dz, b_sel=0, tm=ya2.shape[1], tn=d, tk=1024, name=f"{tag}_dwa")
    grads.put("w_branch_b", layer, yb2, dz, b_sel=1, tm=yb2.shape[1], tn=d, tk=1024, name=f"{tag}_dwb")
    dya = _mm(dz, full["w_branch_a"], mode="nt", out_dtype=F32, tm=1024, tn=ya2.shape[1], tk=d, a_sel=0, b_sel=layer, name=f"{tag}_dya")
    dyb = _mm(dz, full["w_branch_b"], mode="nt", out_dtype=F32, tm=1024, tn=yb2.shape[1], tk=d, a_sel=1, b_sel=layer, name=f"{tag}_dyb")
    d_dil = _dil_attn_bwd(heads, ya, lse_a, _to_heads(dya, b, DIL_GROUP_HEADS), tq=256, name=f"{tag}_ddil")
    d_na, d_bias = _na_attn_bwd(heads, bias, yb, lse_b, _to_heads(dyb, b, NA_HEADS), first=3 * DIL_HEADS, name=f"{tag}_dna")
    dproj = _merge_heads(d_dil, *tabs, heads_per_row=DIL_GROUP_HEADS, rot_pairs=DIL_HEADS, scale_pairs=DIL_HEADS // 2,
                         ts=s, out_cols=n_in, tile_off=0, into=dproj.reshape(b, s, n_in), name=f"{tag}_dheads_a")
    dproj = _merge_heads(d_na, *tabs, heads_per_row=NA_HEADS, rot_pairs=0, scale_pairs=NA_HEADS // 2, ts=s,
                         out_cols=n_in, tile_off=3 * DIL_HEADS // 2, into=dproj, name=f"{tag}_dheads_b").reshape(t, n_in)
    grads.put("w_in", layer, h, dproj, tm=_div_tile(d, 512), tn=_div_tile(n_in, 2944), tk=1024, name=f"{tag}_dwin")
    dh = _mm(dproj, full["w_in"], mode="nt", out_dtype=F32, tm=512, tn=d, tk=_div_tile(n_in, 2944), b_sel=layer, name=f"{tag}_dh")
    dx_in, dxb_in, d_norm = _rms_bwd(dh, x, norm_g, dx, tt=512, name=f"{tag}_dnorm")
    d_rb = _na_collapse_bias(d_bias, s // GRID_W, name=f"{tag}_dbias")
    return dx_in, dxb_in, d_norm, d_rb


def kernel(x, ffn1_norm, ffn1_w_up, ffn1_w_down, mix_norm, w_in, na_rel_bias, w_branch_a, w_branch_b, w_out, ffn2_norm, ffn2_w_up, ffn2_w_down, final_norm, loss_target, m_ffn1_norm, m_ffn1_w_up, m_ffn1_w_down, m_mix_norm, m_w_in, m_na_rel_bias, m_w_branch_a, m_w_branch_b, m_w_out, m_ffn2_norm, m_ffn2_w_up, m_ffn2_w_down, m_final_norm, v_ffn1_norm, v_ffn1_w_up, v_ffn1_w_down, v_mix_norm, v_w_in, v_na_rel_bias, v_w_branch_a, v_w_branch_b, v_w_out, v_ffn2_norm, v_ffn2_w_up, v_ffn2_w_down, v_final_norm):
    w = dict(ffn1_norm=ffn1_norm, ffn1_w_up=ffn1_w_up, ffn1_w_down=ffn1_w_down, mix_norm=mix_norm, w_in=w_in,
             na_rel_bias=na_rel_bias, w_branch_a=w_branch_a, w_branch_b=w_branch_b, w_out=w_out, ffn2_norm=ffn2_norm,
             ffn2_w_up=ffn2_w_up, ffn2_w_down=ffn2_w_down, final_norm=final_norm)
    mom = dict(ffn1_norm=m_ffn1_norm, ffn1_w_up=m_ffn1_w_up, ffn1_w_down=m_ffn1_w_down, mix_norm=m_mix_norm, w_in=m_w_in,
               na_rel_bias=m_na_rel_bias, w_branch_a=m_w_branch_a, w_branch_b=m_w_branch_b, w_out=m_w_out,
               ffn2_norm=m_ffn2_norm, ffn2_w_up=m_ffn2_w_up, ffn2_w_down=m_ffn2_w_down, final_norm=m_final_norm)
    var = dict(ffn1_norm=v_ffn1_norm, ffn1_w_up=v_ffn1_w_up, ffn1_w_down=v_ffn1_w_down, mix_norm=v_mix_norm, w_in=v_w_in,
               na_rel_bias=v_na_rel_bias, w_branch_a=v_w_branch_a, w_branch_b=v_w_branch_b, w_out=v_w_out,
               ffn2_norm=v_ffn2_norm, ffn2_w_up=v_ffn2_w_up, ffn2_w_down=v_ffn2_w_down, final_norm=v_final_norm)
    b, s, d = x.shape
    t = b * s
    depth = ffn1_norm.shape[0]
    assert depth == 2, "core c of a chip sends / reduces layer c"
    shards = {name: w[name] for name, _ in SHARDED}

    full = _gather_all_weights(w)
    tabs = _rope_tables(s)
    bias = _na_expand_bias(na_rel_bias, s // GRID_W, name="na_bias")

    xc = x.reshape(t, d)
    saved = []
    for l in range(depth):
        xc, s1 = _ffn_fwd(xc, ffn1_norm[l:l + 1], full["ffn1_w_up"], full["ffn1_w_down"], l, f"l{l}_ffn1")
        xc, s2 = _mixer_fwd(xc, b, mix_norm[l:l + 1], full, l, bias[l], tabs, f"l{l}_mix")
        xc, s3 = _ffn_fwd(xc, ffn2_norm[l:l + 1], full["ffn2_w_up"], full["ffn2_w_down"], l, f"l{l}_ffn2")
        saved.append((s1, s2, s3))

    dx, dxb, d_final, loss_part = _final_loss(xc, final_norm.reshape(1, d), loss_target.reshape(t, d), tt=512, name="final_loss")
    grads = _Grads(depth)
    small = {name: [None] * depth for name in REPLICATED[:-1]}
    for l in reversed(range(depth)):
        s1, s2, s3 = saved[l]
        dx, dxb, small["ffn2_norm"][l] = _ffn_bwd(dx, dxb, s3, ffn2_norm[l:l + 1], full["ffn2_w_up"], full["ffn2_w_down"],
                                                  l, grads, "ffn2", f"l{l}_ffn2")
        dx, dxb, small["mix_norm"][l], small["na_rel_bias"][l] = _mixer_bwd(
            dx, dxb, b, s2, mix_norm[l:l + 1], full, l, bias[l], tabs, grads, f"l{l}_mix")
        dx, dxb, small["ffn1_norm"][l] = _ffn_bwd(dx, dxb, s1, ffn1_norm[l:l + 1], full["ffn1_w_up"], full["ffn1_w_down"],
                                                  l, grads, "ffn1", f"l{l}_ffn1")
    grad_x = dx.reshape(b, s, d)

    g_out = _reduce_weight_grads(grads.arrays, shards)
    parts = [jnp.stack(small[name]).reshape(-1) for name in REPLICATED[:-1]] + [d_final.reshape(-1), loss_part[0, :1]]
    sizes = [v.shape[0] for v in parts]
    flat = jnp.concatenate(parts)
    flat = jnp.pad(flat, (0, -flat.shape[0] % (8 * LANES)))
    small_sum = _all_sum_small(flat.reshape(-1, LANES), name="small_all_sum").reshape(-1)
    off = 0
    for name, n in zip(REPLICATED, sizes[:-1]):
        g_out[name] = small_sum[off:off + n].reshape(w[name].shape)
        off += n
    loss = small_sum[off]

    names = list(w)
    delta, new_m, new_v = {}, {}, {}
    for name in names:
        delta[name], new_m[name], new_v[name] = _adamw(w[name], g_out[name], mom[name], var[name], name=f"adamw_{name}")
    return (loss, grad_x, *[g_out[n] for n in names], *[delta[n] for n in names], *[new_m[n] for n in names],
            *[new_v[n] for n in names])
```

```python
import functools

import numpy as np
import jax
import jax.numpy as jnp
from jax import lax
from jax.experimental import pallas as pl
from jax.experimental.pallas import tpu as pltpu

F32, BF16 = jnp.float32, jnp.bfloat16
MESH = pl.DeviceIdType.MESH

HEAD_DIM = 64
DILATIONS = (1, 4, 16)
DIL_HALF = 64
DIL_GROUP_HEADS = 4
DIL_HEADS = 12
NA_HEADS = 8
GRID_W = 64
NA_ROWS = 8
NA_COLS = 16
ROPE_THETA = 10000.0
RMS_EPS = 1e-6
NEG_INF = -1e30
ADAM_LR, ADAM_B1, ADAM_B2, ADAM_EPS, ADAM_WD, ADAM_STEP = 0.001, 0.9, 0.999, 1e-08, 0.01, 10
QK_SCALE = HEAD_DIM ** -0.5

N_CHIPS = 4
LANES = 128
BF16_ROWS = 16
VMEM_LIMIT = 56 * 1024 * 1024

_NN = (((1,), (0,)), ((), ()))
_NT = (((1,), (1,)), ((), ()))
_TN = (((0,), (0,)), ((), ()))

HBM = pl.BlockSpec(memory_space=pl.ANY)


def _params(**kw):
    return pltpu.CompilerParams(vmem_limit_bytes=VMEM_LIMIT, **kw)


def _dot(a, b, dims):
    return lax.dot_general(a, b, dims, preferred_element_type=F32)


def _div_tile(n, cap, mult=LANES):
    best = None
    for t in range(mult, min(n, cap) + 1, mult):
        if n % t == 0:
            best = t
    return n if best is None else best


def _stacked(block, index, sel):
    if sel is None:
        return pl.BlockSpec(block, index)
    return pl.BlockSpec((None,) + block, lambda *g: (sel,) + index(*g))


def _mm(a, b, *, mode, out_dtype, tm, tn, tk, name, alpha=1.0, res=None, a_sel=None, b_sel=None, b_k_off=0,
        out_slab=None, out_cols=None, out_col_off=0, out_into=None):
    a2, b2 = a.shape[-2:], b.shape[-2:]
    if mode == "nn":
        (m, k), n = a2, b2[1]
        a_spec = _stacked((tm, tk), lambda i, j, kk: (i, kk), a_sel)
        b_spec = _stacked((tk, tn), lambda i, j, kk: (kk + b_k_off, j), b_sel)
        dims = _NN
    elif mode == "nt":
        (m, k), n = a2, b2[0]
        a_spec = _stacked((tm, tk), lambda i, j, kk: (i, kk), a_sel)
        b_spec = _stacked((tn, tk), lambda i, j, kk: (j, kk + b_k_off), b_sel)
        dims = _NT
    else:
        (k, m), n = a2, b2[1]
        a_spec = _stacked((tk, tm), lambda i, j, kk: (kk, i), a_sel)
        b_spec = _stacked((tk, tn), lambda i, j, kk: (kk + b_k_off, j), b_sel)
        dims = _TN
    assert m % tm == 0 and n % tn == 0 and k % tk == 0, (name, a.shape, b.shape)
    nk = k // tk
    has_res = res is not None
    if out_slab is None:
        o_spec = pl.BlockSpec((tm, tn), lambda i, j, kk: (i, j))
        out_shape = jax.ShapeDtypeStruct((m, n), out_dtype)
    else:
        o_spec = _stacked((tm, tn), lambda i, j, kk: (i, j + out_col_off), out_slab[0])
        out_shape = jax.ShapeDtypeStruct((out_slab[1], m, n if out_cols is None else out_cols), out_dtype)
    r_spec = pl.BlockSpec((tm, tn), lambda i, j, kk: (i, j))
    n_in = 2 + has_res + (out_into is not None)

    def body(*refs):
        a_ref, b_ref = refs[0], refs[1]
        r_ref = refs[2] if has_res else None
        o_ref = refs[n_in]
        p = _dot(a_ref[...], b_ref[...], dims)

        def finish(acc):
            y = acc * alpha if alpha != 1.0 else acc
            if has_res:
                y = y + r_ref[...].astype(F32)
            o_ref[...] = y.astype(o_ref.dtype)

        if nk == 1:
            finish(p)
        else:
            acc_ref = refs[n_in + 1]
            kk = pl.program_id(2)

            @pl.when(kk == 0)
            def _():
                acc_ref[...] = p

            @pl.when(kk > 0)
            def _():
                acc_ref[...] += p

            @pl.when(kk == nk - 1)
            def _():
                finish(acc_ref[...])

    operands = [a, b] + ([res] if has_res else [])
    in_specs = [a_spec, b_spec] + ([r_spec] if has_res else [])
    aliases = {}
    if out_into is not None:
        aliases = {len(operands): 0}
        operands.append(out_into)
        in_specs.append(HBM)
    return pl.pallas_call(
        body, name=name, grid=(m // tm, n // tn, nk), in_specs=in_specs, out_specs=o_spec, out_shape=out_shape,
        scratch_shapes=[pltpu.VMEM((tm, tn), F32)] if nk > 1 else [], input_output_aliases=aliases,
        compiler_params=_params(dimension_semantics=("parallel", "parallel", "arbitrary")),
    )(*operands)


def _mm_swiglu_fwd(h, w_up, layer, *, tm, tn, name):
    m, k = h.shape
    n = w_up.shape[2] // 2
    h_spec = pl.BlockSpec((tm, k), lambda i, j: (i, 0))
    wg_spec = pl.BlockSpec((None, k, tn), lambda i, j: (layer, 0, j))
    wu_spec = pl.BlockSpec((None, k, tn), lambda i, j: (layer, 0, j + n // tn))
    o_spec = pl.BlockSpec((tm, tn), lambda i, j: (i, j))

    def body(h_ref, wg_ref, wu_ref, a_ref, g_ref, u_ref):
        hb = h_ref[...]
        g = _dot(hb, wg_ref[...], _NN)
        u = _dot(hb, wu_ref[...], _NN)
        a_ref[...] = (g * jax.nn.sigmoid(g) * u).astype(BF16)
        g_ref[...] = g.astype(BF16)
        u_ref[...] = u.astype(BF16)

    out = jax.ShapeDtypeStruct((m, n), BF16)
    return pl.pallas_call(
        body, name=name, grid=(m // tm, n // tn), in_specs=[h_spec, wg_spec, wu_spec],
        out_specs=[o_spec] * 3, out_shape=[out] * 3,
        compiler_params=_params(dimension_semantics=("parallel", "parallel")),
    )(h, w_up, w_up)


def _mm_swiglu_bwd(dy, w_down, layer, gate, up, *, alpha, tm, tn, name):
    m, k = dy.shape
    n = w_down.shape[1]
    dy_spec = pl.BlockSpec((tm, k), lambda i, j: (i, 0))
    w_spec = pl.BlockSpec((None, tn, k), lambda i, j: (layer, j, 0))
    o_spec = pl.BlockSpec((tm, tn), lambda i, j: (i, j))

    def body(dy_ref, w_ref, g_ref, u_ref, dg_ref, du_ref):
        da = _dot(dy_ref[...], w_ref[...], _NT) * alpha
        g = g_ref[...].astype(F32)
        u = u_ref[...].astype(F32)
        sg = jax.nn.sigmoid(g)
        dg_ref[...] = (da * u * (sg * (1.0 + g * (1.0 - sg)))).astype(BF16)
        du_ref[...] = (da * (g * sg)).astype(BF16)

    out = jax.ShapeDtypeStruct((m, n), BF16)
    return pl.pallas_call(
        body, name=name, grid=(m // tm, n // tn), in_specs=[dy_spec, w_spec, o_spec, o_spec],
        out_specs=[o_spec] * 2, out_shape=[out] * 2,
        compiler_params=_params(dimension_semantics=("parallel", "parallel")),
    )(dy, w_down, gate, up)


def _rms_fwd(x, g, *, tt, name):
    t, d = x.shape

    def body(x_ref, g_ref, h_ref):
        xv = x_ref[...]
        rstd = lax.rsqrt(jnp.mean(xv * xv, axis=1, keepdims=True) + RMS_EPS)
        h_ref[...] = (xv * rstd * g_ref[...]).astype(BF16)

    return pl.pallas_call(
        body, name=name, grid=(t // tt,),
        in_specs=[pl.BlockSpec((tt, d), lambda i: (i, 0)), pl.BlockSpec((1, d), lambda i: (0, 0))],
        out_specs=pl.BlockSpec((tt, d), lambda i: (i, 0)), out_shape=jax.ShapeDtypeStruct((t, d), BF16),
        compiler_params=_params(dimension_semantics=("parallel",)),
    )(x, g)


def _rms_bwd(dh, x, g, dres, *, tt, name):
    t, d = x.shape

    def body(dh_ref, x_ref, g_ref, r_ref, dx_ref, dxb_ref, dg_ref):
        xv = x_ref[...]
        rstd = lax.rsqrt(jnp.mean(xv * xv, axis=1, keepdims=True) + RMS_EPS)
        xhat = xv * rstd
        dhv = dh_ref[...]
        dxhat = dhv * g_ref[...]
        dx = r_ref[...] + rstd * (dxhat - xhat * jnp.mean(dxhat * xhat, axis=1, keepdims=True))
        dx_ref[...] = dx
        dxb_ref[...] = dx.astype(BF16)

        @pl.when(pl.program_id(0) == 0)
        def _():
            dg_ref[...] = jnp.zeros_like(dg_ref)

        dg_ref[...] += jnp.sum(dhv * xhat, axis=0, keepdims=True)

    row = pl.BlockSpec((tt, d), lambda i: (i, 0))
    vec = pl.BlockSpec((1, d), lambda i: (0, 0))
    return pl.pallas_call(
        body, name=name, grid=(t // tt,), in_specs=[row, row, vec, row], out_specs=[row, row, vec],
        out_shape=[jax.ShapeDtypeStruct((t, d), F32), jax.ShapeDtypeStruct((t, d), BF16), jax.ShapeDtypeStruct((1, d), F32)],
        compiler_params=_params(dimension_semantics=("arbitrary",)),
    )(dh, x, g, dres)


def _final_loss(x, g, target, *, tt, name):
    t, d = x.shape

    def body(x_ref, g_ref, t_ref, dx_ref, dxb_ref, dg_ref, loss_ref):
        xv = x_ref[...]
        gv = g_ref[...]
        rstd = lax.rsqrt(jnp.mean(xv * xv, axis=1, keepdims=True) + RMS_EPS)
        xhat = xv * rstd
        err = xhat * gv - t_ref[...]
        dy = err * (1.0 / d)
        dxhat = dy * gv
        dx = rstd * (dxhat - xhat * jnp.mean(dxhat * xhat, axis=1, keepdims=True))
        dx_ref[...] = dx
        dxb_ref[...] = dx.astype(BF16)

        @pl.when(pl.program_id(0) == 0)
        def _():
            dg_ref[...] = jnp.zeros_like(dg_ref)
            loss_ref[...] = jnp.zeros_like(loss_ref)

        dg_ref[...] += jnp.sum(dy * xhat, axis=0, keepdims=True)
        part = 0.5 * jnp.sum(jnp.mean(err * err, axis=1, keepdims=True), axis=0, keepdims=True)
        loss_ref[...] += jnp.broadcast_to(part, loss_ref.shape)

    row = pl.BlockSpec((tt, d), lambda i: (i, 0))
    vec = pl.BlockSpec((1, d), lambda i: (0, 0))
    one = pl.BlockSpec((1, LANES), lambda i: (0, 0))
    return pl.pallas_call(
        body, name=name, grid=(t // tt,), in_specs=[row, vec, row], out_specs=[row, row, vec, one],
        out_shape=[jax.ShapeDtypeStruct((t, d), F32), jax.ShapeDtypeStruct((t, d), BF16), jax.ShapeDtypeStruct((1, d), F32),
                   jax.ShapeDtypeStruct((1, LANES), F32)],
        compiler_params=_params(dimension_semantics=("arbitrary",)),
    )(x, g, target)


def _swap_halves(x):
    lane = lax.broadcasted_iota(jnp.int32, x.shape, 1)
    return jnp.where((lane // 32) % 2 == 0, pltpu.roll(x, 96, 1), pltpu.roll(x, 32, 1))


def _rope_tables(s):
    half = HEAD_DIM // 2
    inv_freq = ROPE_THETA ** (-jnp.arange(half, dtype=F32) / half)
    ang = jnp.arange(s).astype(F32)[:, None] * inv_freq[None, :]
    cos, sin = jnp.cos(ang), jnp.sin(ang)
    return jnp.tile(cos, (1, 4)), jnp.concatenate([-sin, sin, -sin, sin], axis=1)


def _dilation_of_tile(p):
    dilated = p < 3 * DIL_HEADS // 2
    g = (p % (DIL_HEADS // 2)) // (DIL_GROUP_HEADS // 2)
    return [(dilated & (g == gi)) | (jnp.logical_not(dilated) if gi == 0 else False) for gi in range(len(DILATIONS))]


def _residue_major(ref, d):
    s = ref.shape[0]
    if d == 1:
        return ref[...]
    return jnp.concatenate([ref[pl.ds(r, s // d, stride=d), :] for r in range(d)], axis=0)


def _split_heads(proj, cos4, sin4, *, n_pairs, rot_pairs, scale_ranges, name):
    b, s, _ = proj.shape

    def body(x_ref, c_ref, s_ref, o_ref):
        p = pl.program_id(1)
        is_q = functools.reduce(jnp.logical_or, [(p >= lo) & (p < hi) for lo, hi in scale_ranges])
        scale = jnp.where(is_q, QK_SCALE, 1.0)

        def put(y):
            o_ref[0] = y[:, :HEAD_DIM].astype(BF16)
            o_ref[1] = y[:, HEAD_DIM:].astype(BF16)

        for d, in_group in zip(DILATIONS, _dilation_of_tile(p)):
            @pl.when(in_group & (p < rot_pairs))
            def _(d=d):
                x = _residue_major(x_ref, d)
                put((x * _residue_major(c_ref, d) + _swap_halves(x) * _residue_major(s_ref, d)) * scale)

            @pl.when(in_group & (p >= rot_pairs))
            def _(d=d):
                put(_residue_major(x_ref, d) * scale)

    tab = pl.BlockSpec((s, LANES), lambda bi, p: (0, 0))
    return pl.pallas_call(
        body, name=name, grid=(b, n_pairs),
        in_specs=[pl.BlockSpec((None, s, LANES), lambda bi, p: (bi, 0, p)), tab, tab],
        out_specs=pl.BlockSpec((None, 2, s, HEAD_DIM), lambda bi, p: (bi, p, 0, 0)),
        out_shape=jax.ShapeDtypeStruct((b, 2 * n_pairs, s, HEAD_DIM), BF16),
        compiler_params=_params(dimension_semantics=("parallel", "parallel")),
    )(proj, cos4, sin4)


def _merge_heads(dheads, cos4, sin4, *, heads_per_row, rot_pairs, scale_pairs, dilated, out_cols, tile_off, into, name):
    b, hpr, r, s, _ = dheads.shape
    n_pairs = hpr * r // 2
    ppr = hpr // 2

    def body(d_ref, c_ref, s_ref, *rest):
        o_ref, t_ref = rest[-2:]
        p = pl.program_id(1)
        scale = jnp.where(p < scale_pairs, QK_SCALE, 1.0)

        def tokens(d):
            dy = jnp.concatenate([d_ref[0], d_ref[1]], axis=1)
            if d == 1:
                return dy
            for res in range(d):
                t_ref[pl.ds(res, s // d, stride=d), :] = dy[res * (s // d):(res + 1) * (s // d), :]
            return t_ref[...]

        groups = _dilation_of_tile(p) if dilated else [p >= 0]
        for d, in_group in zip(DILATIONS, groups):
            @pl.when(in_group & (p < rot_pairs))
            def _(d=d):
                dy = tokens(d)
                o_ref[...] = ((dy * c_ref[...] - _swap_halves(dy) * s_ref[...]) * scale).astype(BF16)

            @pl.when(in_group & (p >= rot_pairs))
            def _(d=d):
                o_ref[...] = (tokens(d) * scale).astype(BF16)

    tab = pl.BlockSpec((s, LANES), lambda bi, p: (0, 0))
    operands = [dheads, cos4, sin4] + ([] if into is None else [into])
    return pl.pallas_call(
        body, name=name, grid=(b, n_pairs),
        in_specs=[pl.BlockSpec((None, 2, None, s, HEAD_DIM), lambda bi, p: (bi, p % ppr, p // ppr, 0, 0)), tab, tab]
        + ([] if into is None else [HBM]),
        out_specs=pl.BlockSpec((None, s, LANES), lambda bi, p: (bi, 0, p + tile_off)),
        out_shape=jax.ShapeDtypeStruct((b, s, out_cols), BF16),
        input_output_aliases={} if into is None else {3: 0},
        scratch_shapes=[pltpu.VMEM((s, LANES), F32)],
        compiler_params=_params(dimension_semantics=("parallel", "parallel")),
    )(*operands)


def _dil_block(g, s):
    run = s // DILATIONS[g]
    tq = min(256, run)
    return tq, min(run, tq + 2 * LANES)


def _dil_keys(g, q0, s):
    run = s // DILATIONS[g]
    tq, width = _dil_block(g, s)
    lo = (q0 // run) * run
    return pl.multiple_of(jnp.clip(q0 - LANES, lo, lo + run - width), LANES)


def _dil_band(q0, start, shape):
    diff = (q0 - start) + lax.broadcasted_iota(jnp.int32, shape, 0) - lax.broadcasted_iota(jnp.int32, shape, 1)
    return jnp.abs(diff) <= DIL_HALF


def _dil_tokens(g, q0, s):
    d = DILATIONS[g]
    tq, _ = _dil_block(g, s)
    if d == 1:
        return pl.ds(q0, tq)
    run = s // d
    return pl.ds((q0 % run) * d + q0 // run, tq, stride=d)


def _dil_head_spec(part, g, s):
    return pl.BlockSpec((None, None, s, HEAD_DIM), lambda b, j: (b, part * DIL_HEADS + g * DIL_GROUP_HEADS + j, 0, 0))


def _dil_attn_fwd(heads, *, name):
    b, _, s, _ = heads.shape
    n_g = len(DILATIONS)

    def body(*refs):
        qkv = refs[:3 * n_g]
        o_ref, l_ref, og_ref, lg_ref = refs[3 * n_g:]
        for g in range(n_g):
            q_ref, k_ref, v_ref = qkv[3 * g:3 * g + 3]
            tq, width = _dil_block(g, s)

            def step(i, carry, g=g, q_ref=q_ref, k_ref=k_ref, v_ref=v_ref, tq=tq, width=width):
                q0 = pl.multiple_of(i * tq, tq)
                start = _dil_keys(g, q0, s)
                sc = _dot(q_ref[pl.ds(q0, tq), :], k_ref[pl.ds(start, width), :], _NT)
                sc = jnp.where(_dil_band(q0, start, sc.shape), sc, NEG_INF)
                m = jnp.max(sc, axis=1, keepdims=True)
                p = jnp.exp(sc - m)
                den = jnp.sum(p, axis=1, keepdims=True)
                rows = _dil_tokens(g, q0, s)
                og_ref[g, rows, :] = _dot(p.astype(BF16), v_ref[pl.ds(start, width), :], _NN) / den
                lg_ref[g, rows, :] = m + jnp.log(den)
                return carry

            lax.fori_loop(0, s // tq, step, 0)
        lses = [lg_ref[g] for g in range(n_g)]
        m = functools.reduce(jnp.maximum, lses)
        ws = [jnp.exp(l - m) for l in lses]
        den = functools.reduce(jnp.add, ws)
        o_ref[...] = (functools.reduce(jnp.add, [w * og_ref[g] for g, w in enumerate(ws)]) / den).astype(o_ref.dtype)
        l_ref[...] = m + jnp.log(den)

    out = pl.BlockSpec((None, None, s, HEAD_DIM), lambda bi, j: (bi, j, 0, 0))
    lse = pl.BlockSpec((None, None, s, 1), lambda bi, j: (bi, j, 0, 0))
    return pl.pallas_call(
        body, name=name, grid=(b, DIL_GROUP_HEADS),
        in_specs=[_dil_head_spec(part, g, s) for g in range(n_g) for part in range(3)],
        out_specs=[out, lse],
        out_shape=[jax.ShapeDtypeStruct((b, DIL_GROUP_HEADS, s, HEAD_DIM), BF16),
                   jax.ShapeDtypeStruct((b, DIL_GROUP_HEADS, s, 1), F32)],
        scratch_shapes=[pltpu.VMEM((n_g, s, HEAD_DIM), F32), pltpu.VMEM((n_g, s, 1), F32)],
        compiler_params=_params(dimension_semantics=("parallel", "parallel")),
    )(*([heads] * (3 * n_g)))


def _dil_attn_bwd(heads, out, lse, dout, *, name):
    b, _, s, _ = heads.shape
    n_g = len(DILATIONS)

    def body(*refs):
        qkv = refs[:3 * n_g]
        o_ref, l_ref, do_ref, d_ref, delta_ref = refs[3 * n_g:]
        d_ref[...] = jnp.zeros_like(d_ref)
        delta_ref[...] = jnp.sum(do_ref[...] * o_ref[...].astype(F32), axis=1, keepdims=True)
        for g in range(n_g):
            q_ref, k_ref, v_ref = qkv[3 * g:3 * g + 3]
            tq, width = _dil_block(g, s)

            def step(i, carry, g=g, q_ref=q_ref, k_ref=k_ref, v_ref=v_ref, tq=tq, width=width):
                q0 = pl.multiple_of(i * tq, tq)
                start = _dil_keys(g, q0, s)
                win = pl.ds(start, width)
                rows = _dil_tokens(g, q0, s)
                do_b = do_ref[rows, :].astype(BF16)
                q, k, v = q_ref[pl.ds(q0, tq), :], k_ref[win, :], v_ref[win, :]
                sc = _dot(q, k, _NT)
                p = jnp.where(_dil_band(q0, start, sc.shape), jnp.exp(sc - l_ref[rows, :]), 0.0)
                ds = (p * (_dot(do_b, v, _NT) - delta_ref[rows, :])).astype(BF16)
                d_ref[g, pl.ds(q0, tq), :] = _dot(ds, k, _NN)
                d_ref[n_g + g, win, :] += _dot(ds, q, _TN)
                d_ref[2 * n_g + g, win, :] += _dot(p.astype(BF16), do_b, _TN)
                return carry

            lax.fori_loop(0, s // tq, step, 0)

    per_head = lambda bi, j: (bi, j, 0, 0)
    return pl.pallas_call(
        body, name=name, grid=(b, DIL_GROUP_HEADS),
        in_specs=[_dil_head_spec(part, g, s) for g in range(n_g) for part in range(3)]
        + [pl.BlockSpec((None, None, s, HEAD_DIM), per_head), pl.BlockSpec((None, None, s, 1), per_head),
           pl.BlockSpec((None, None, s, HEAD_DIM), per_head)],
        out_specs=pl.BlockSpec((None, None, 3 * n_g, s, HEAD_DIM), lambda bi, j: (bi, j, 0, 0, 0)),
        out_shape=jax.ShapeDtypeStruct((b, DIL_GROUP_HEADS, 3 * n_g, s, HEAD_DIM), F32),
        scratch_shapes=[pltpu.VMEM((s, 1), F32)],
        compiler_params=_params(dimension_semantics=("parallel", "parallel")),
    )(*([heads] * (3 * n_g)), out, lse, dout)


NA_BIAS_ROWS = 2 * NA_ROWS - 1
NA_BIAS_COLS = 2 * NA_COLS - 1
NA_BLOCK = 4
NA_SPAN = NA_ROWS + NA_BLOCK - 1
NA_Q = NA_BLOCK * GRID_W
NA_KEYS = NA_SPAN * GRID_W
NA_FORMS = 3


def _na_onehot():
    c = np.arange(GRID_W)[:, None]
    k = np.arange(GRID_W)[None, :]
    lo = np.clip(c - NA_COLS // 2, 0, GRID_W - NA_COLS)
    valid = (k >= lo) & (k < lo + NA_COLS)
    onehot = np.zeros((GRID_W, GRID_W, LANES), np.float32)
    cc, kk = np.nonzero(valid)
    onehot[cc, kk, kk - cc + NA_COLS - 1] = 1.0
    return onehot.reshape(GRID_W * GRID_W, LANES), valid.reshape(1, GRID_W * GRID_W)


def _na_block_rows(n_rows):
    table = np.full((NA_FORMS, NA_BLOCK, NA_SPAN), NA_BIAS_ROWS, np.int64)
    n_blocks = n_rows // NA_BLOCK
    for form, ib in enumerate((0, 1, n_blocks - 1)):
        base = min(max(NA_BLOCK * ib - NA_ROWS // 2, 0), n_rows - NA_SPAN)
        for rl in range(NA_BLOCK):
            r = NA_BLOCK * ib + rl
            row_lo = min(max(r - NA_ROWS // 2, 0), n_rows - NA_ROWS)
            for kl in range(NA_SPAN):
                if row_lo <= base + kl < row_lo + NA_ROWS:
                    table[form, rl, kl] = base + kl - r + NA_ROWS - 1
    return table


def _na_block(ib, n_rows):
    n_blocks = n_rows // NA_BLOCK
    base = jnp.clip(NA_BLOCK * ib - NA_ROWS // 2, 0, n_rows - NA_SPAN)
    return base, jnp.where(ib == 0, 0, jnp.where(ib == n_blocks - 1, 2, 1))


def _na_expand_bias(rel_bias, *, name):
    l, h, nr, nc = rel_bias.shape
    onehot, valid = _na_onehot()
    rb = jnp.pad(rel_bias, ((0, 0), (0, 0), (0, 1), (0, LANES - nc))).reshape(l * h * (nr + 1), LANES)
    live = jnp.asarray(np.tile(np.arange(nr + 1) < nr, l * h).astype(np.float32)[:, None])

    def body(rb_ref, oh_ref, valid_ref, live_ref, e_ref):
        e = lax.dot_general(rb_ref[...], oh_ref[...], _NT, precision=lax.Precision.HIGHEST, preferred_element_type=F32)
        e_ref[...] = jnp.where((valid_ref[...] > 0) & (live_ref[...] > 0), e, NEG_INF)

    e = pl.pallas_call(
        body, name=name, out_shape=jax.ShapeDtypeStruct((l * h * (nr + 1), GRID_W * GRID_W), F32), compiler_params=_params(),
    )(rb, jnp.asarray(onehot), jnp.asarray(valid.astype(np.float32)), live)
    return e.reshape(l, h, nr + 1, GRID_W, GRID_W)


def _na_collapse_bias(de, *, name):
    b, h = de.shape[:2]
    onehot, _ = _na_onehot()
    rows = h * NA_BIAS_ROWS

    def diag(e_ref, oh_ref, o_ref):
        e = e_ref[0]
        for bi in range(1, b):
            e = e + e_ref[bi]
        o_ref[...] = lax.dot_general(e, oh_ref[...], _NN, precision=lax.Precision.HIGHEST, preferred_element_type=F32)

    drb = pl.pallas_call(
        diag, name=name, out_shape=jax.ShapeDtypeStruct((rows, LANES), F32), compiler_params=_params(),
    )(de.reshape(b, rows, GRID_W * GRID_W), jnp.asarray(onehot))
    return drb[:, :NA_BIAS_COLS].reshape(h, NA_BIAS_ROWS, NA_BIAS_COLS)


def _na_tiles(n_rows):
    table = _na_block_rows(n_rows)
    return [(f, rl, kl, int(table[f, rl, kl])) for f in range(NA_FORMS) for rl in range(NA_BLOCK) for kl in range(NA_SPAN)]


def _na_tile(ref, form, rl, kl):
    return ref.at[form, rl * GRID_W:(rl + 1) * GRID_W, kl * GRID_W:(kl + 1) * GRID_W]


def _na_head_spec(part, first, s):
    return pl.BlockSpec((None, None, s, HEAD_DIM), lambda b, h: (b, first + part * NA_HEADS + h, 0, 0))


def _na_attn_fwd(heads, bias, *, first, name):
    b, _, s, _ = heads.shape
    n_rows = s // GRID_W
    tiles = _na_tiles(n_rows)

    def body(q_ref, k_ref, v_ref, e_ref, o_ref, l_ref, b_ref):
        for form, rl, kl, i in tiles:
            _na_tile(b_ref, form, rl, kl)[...] = e_ref[i]

        def step(ib, carry):
            base, form = _na_block(ib, n_rows)
            rows = pl.ds(pl.multiple_of(ib * NA_Q, NA_Q), NA_Q)
            win = pl.ds(pl.multiple_of(base * GRID_W, GRID_W), NA_KEYS)
            sc = _dot(q_ref[rows, :], k_ref[win, :], _NT) + b_ref[form]
            m = jnp.max(sc, axis=1, keepdims=True)
            p = jnp.exp(sc - m)
            den = jnp.sum(p, axis=1, keepdims=True)
            o_ref[rows, :] = (_dot(p.astype(BF16), v_ref[win, :], _NN) / den).astype(o_ref.dtype)
            l_ref[rows, :] = m + jnp.log(den)
            return carry

        lax.fori_loop(0, n_rows // NA_BLOCK, step, 0)

    per_head = lambda bi, h: (bi, h, 0, 0)
    return pl.pallas_call(
        body, name=name, grid=(b, NA_HEADS),
        in_specs=[_na_head_spec(part, first, s) for part in range(3)]
        + [pl.BlockSpec((None, NA_BIAS_ROWS + 1, GRID_W, GRID_W), lambda bi, h: (h, 0, 0, 0))],
        out_specs=[pl.BlockSpec((None, None, s, HEAD_DIM), per_head), pl.BlockSpec((None, None, s, 1), per_head)],
        out_shape=[jax.ShapeDtypeStruct((b, NA_HEADS, s, HEAD_DIM), BF16), jax.ShapeDtypeStruct((b, NA_HEADS, s, 1), F32)],
        scratch_shapes=[pltpu.VMEM((NA_FORMS, NA_Q, NA_KEYS), F32)],
        compiler_params=_params(dimension_semantics=("parallel", "parallel")),
    )(heads, heads, heads, bias)


def _na_attn_bwd(heads, bias, out, lse, dout, *, first, name):
    b, _, s, _ = heads.shape
    n_rows = s // GRID_W
    tiles = _na_tiles(n_rows)

    def body(q_ref, k_ref, v_ref, e_ref, o_ref, l_ref, do_ref, d_ref, de_ref, b_ref, db_ref):
        for form, rl, kl, i in tiles:
            _na_tile(b_ref, form, rl, kl)[...] = e_ref[i]
        d_ref[...] = jnp.zeros_like(d_ref)
        db_ref[...] = jnp.zeros_like(db_ref)

        def step(ib, carry):
            base, form = _na_block(ib, n_rows)
            rows = pl.ds(pl.multiple_of(ib * NA_Q, NA_Q), NA_Q)
            win = pl.ds(pl.multiple_of(base * GRID_W, GRID_W), NA_KEYS)
            q, k, v = q_ref[rows, :], k_ref[win, :], v_ref[win, :]
            do = do_ref[rows, :]
            delta = jnp.sum(do * o_ref[rows, :].astype(F32), axis=1, keepdims=True)
            do_b = do.astype(BF16)
            p = jnp.exp(_dot(q, k, _NT) + b_ref[form] - l_ref[rows, :])
            ds = p * (_dot(do_b, v, _NT) - delta)
            db_ref[form] += ds
            ds_b = ds.astype(BF16)
            d_ref[0, rows, :] = _dot(ds_b, k, _NN)
            d_ref[1, win, :] += _dot(ds_b, q, _TN)
            d_ref[2, win, :] += _dot(p.astype(BF16), do_b, _TN)
            return carry

        lax.fori_loop(0, n_rows // NA_BLOCK, step, 0)
        acc = [None] * NA_BIAS_ROWS
        for form, rl, kl, i in tiles:
            if i < NA_BIAS_ROWS:
                t = _na_tile(db_ref, form, rl, kl)[...]
                acc[i] = t if acc[i] is None else acc[i] + t
        for i in range(NA_BIAS_ROWS):
            de_ref[i] = acc[i]

    per_head = lambda bi, h: (bi, h, 0, 0)
    return pl.pallas_call(
        body, name=name, grid=(b, NA_HEADS),
        in_specs=[_na_head_spec(part, first, s) for part in range(3)]
        + [pl.BlockSpec((None, NA_BIAS_ROWS + 1, GRID_W, GRID_W), lambda bi, h: (h, 0, 0, 0)),
           pl.BlockSpec((None, None, s, HEAD_DIM), per_head), pl.BlockSpec((None, None, s, 1), per_head),
           pl.BlockSpec((None, None, s, HEAD_DIM), per_head)],
        out_specs=[pl.BlockSpec((None, None, 3, s, HEAD_DIM), lambda bi, h: (bi, h, 0, 0, 0)),
                   pl.BlockSpec((None, None, NA_BIAS_ROWS, GRID_W, GRID_W), lambda bi, h: (bi, h, 0, 0, 0))],
        out_shape=[jax.ShapeDtypeStruct((b, NA_HEADS, 3, s, HEAD_DIM), F32),
                   jax.ShapeDtypeStruct((b, NA_HEADS, NA_BIAS_ROWS, GRID_W, GRID_W), F32)],
        scratch_shapes=[pltpu.VMEM((NA_FORMS, NA_Q, NA_KEYS), F32), pltpu.VMEM((NA_FORMS, NA_Q, NA_KEYS), F32)],
        compiler_params=_params(dimension_semantics=("parallel", "parallel")),
    )(heads, heads, heads, bias, out, lse, dout)


GATE_TILE = 256


def _gate_fwd(proj, z, *, gate_col, tt, name):
    _, t, d = z.shape
    nj = d // GATE_TILE
    c0 = gate_col // GATE_TILE

    def body(ga_ref, gb_ref, za_ref, zb_ref, o_ref):
        o_ref[...] = (jax.nn.sigmoid(ga_ref[...]) * za_ref[...] + jax.nn.sigmoid(gb_ref[...]) * zb_ref[...]).astype(BF16)

    return pl.pallas_call(
        body, name=name, grid=(t // tt, nj),
        in_specs=[pl.BlockSpec((tt, GATE_TILE), lambda i, j: (i, c0 + j)),
                  pl.BlockSpec((tt, GATE_TILE), lambda i, j: (i, c0 + nj + j)),
                  pl.BlockSpec((None, tt, GATE_TILE), lambda i, j: (0, i, j)),
                  pl.BlockSpec((None, tt, GATE_TILE), lambda i, j: (1, i, j))],
        out_specs=pl.BlockSpec((tt, GATE_TILE), lambda i, j: (i, j)), out_shape=jax.ShapeDtypeStruct((t, d), BF16),
        compiler_params=_params(dimension_semantics=("parallel", "parallel")),
    )(proj, proj, z, z)


def _gate_bwd(dm, proj, z, *, gate_col, tt, name):
    _, t, d = z.shape
    nj = d // GATE_TILE
    c0 = gate_col // GATE_TILE

    def body(dm_ref, g_ref, z_ref, dz_ref, dg_ref):
        dmv = dm_ref[...]
        sg = jax.nn.sigmoid(g_ref[...])
        dz_ref[...] = (dmv * sg).astype(BF16)
        dg_ref[...] = (dmv * z_ref[...] * sg * (1.0 - sg)).astype(BF16)

    return pl.pallas_call(
        body, name=name, grid=(t // tt, 2 * nj),
        in_specs=[pl.BlockSpec((tt, GATE_TILE), lambda i, j: (i, j % nj)),
                  pl.BlockSpec((tt, GATE_TILE), lambda i, j: (i, c0 + j)),
                  pl.BlockSpec((None, tt, GATE_TILE), lambda i, j: (j // nj, i, j % nj))],
        out_specs=[pl.BlockSpec((None, tt, GATE_TILE), lambda i, j: (j // nj, i, j % nj)),
                   pl.BlockSpec((tt, GATE_TILE), lambda i, j: (i, c0 + j))],
        out_shape=[jax.ShapeDtypeStruct((2, t, d), BF16), jax.ShapeDtypeStruct(proj.shape, BF16)],
        compiler_params=_params(dimension_semantics=("parallel", "parallel")),
    )(dm, proj, z)


def _adamw(w, g, m, v, *, name):
    shape = w.shape
    w2, g2, m2, v2 = (t.reshape(-1, shape[-1]) for t in (w, g, m, v))
    rows, cols = w2.shape
    tr = rows
    for cand in (512, 256, 128, 64, 32, 16, 8):
        if rows % cand == 0:
            tr = cand
            break

    def body(w_ref, g_ref, m_ref, v_ref, d_ref, nm_ref, nv_ref):
        gv = g_ref[...]
        nm = ADAM_B1 * m_ref[...] + (1.0 - ADAM_B1) * gv
        nv = ADAM_B2 * v_ref[...] + (1.0 - ADAM_B2) * (gv * gv)
        m_hat = nm / (1.0 - ADAM_B1 ** ADAM_STEP)
        v_hat = nv / (1.0 - ADAM_B2 ** ADAM_STEP)
        d_ref[...] = -ADAM_LR * (m_hat / (jnp.sqrt(v_hat) + ADAM_EPS) + ADAM_WD * w_ref[...])
        nm_ref[...] = nm
        nv_ref[...] = nv

    blk = pl.BlockSpec((tr, cols), lambda i: (i, 0))
    out = jax.ShapeDtypeStruct((rows, cols), F32)
    res = pl.pallas_call(
        body, name=name, grid=(rows // tr,), in_specs=[blk] * 4, out_specs=[blk] * 3, out_shape=[out] * 3,
        compiler_params=_params(dimension_semantics=("parallel",)),
    )(w2, g2, m2, v2)
    return tuple(t.reshape(shape) for t in res)


def _my_place():
    return lax.axis_index("x"), lax.axis_index("y"), lax.axis_index("c")


def _other_chips(x, y):
    return [(1 - x, y), (x, 1 - y), (1 - x, 1 - y)]


def _chip_no(chip):
    return 2 * chip[0] + chip[1]


def _window(ref, kind, size, chip, lead):
    if kind == "col":
        return ref.at[(*lead, slice(None), pl.ds(pl.multiple_of(chip * size, LANES), size))]
    if kind == "row":
        return ref.at[(*lead, pl.ds(pl.multiple_of(chip * size, BF16_ROWS), size), slice(None))]
    shard = size + HEAD_DIM
    if kind == "win_main":
        return ref.at[(*lead, slice(None), pl.ds(pl.multiple_of(chip * shard + HEAD_DIM * (chip % 2), LANES), size))]
    assert kind == "win_strad"
    return ref.at[(*lead, slice(None), pl.ds(pl.multiple_of(size + 2 * shard * (chip // 2), LANES), LANES))]


def _place_own(shard, kind, *, name):
    l, k, n = shard.shape
    full = {"col": (l, k, N_CHIPS * n), "row": (l, N_CHIPS * k, n), "win_main": (l, k, N_CHIPS * (n + HEAD_DIM)),
            "slot": (N_CHIPS, l, k, n)}[kind]
    tr = _div_tile(k, 512, BF16_ROWS)
    tc = LANES if kind == "win_main" else n
    mine = 2 * lax.axis_index("x") + lax.axis_index("y")
    row0 = mine * (k // tr) if kind == "row" else 0
    col0 = {"col": mine, "row": 0, "slot": 0, "win_main": (mine * (n + HEAD_DIM) + HEAD_DIM * (mine % 2)) // LANES}[kind]
    scalars = jnp.stack([mine, row0, col0]).astype(jnp.int32)

    def body(s_ref, i_ref, o_ref):
        o_ref[...] = i_ref[...]

    if kind == "slot":
        o_spec = pl.BlockSpec((None, None, tr, tc), lambda li, i, j, s: (s[0], li, i, j))
    else:
        o_spec = pl.BlockSpec((None, tr, tc), lambda li, i, j, s: (li, s[1] + i, s[2] + j))
    return pl.pallas_call(
        body, name=name,
        grid_spec=pltpu.PrefetchScalarGridSpec(
            num_scalar_prefetch=1, grid=(l, k // tr, n // tc),
            in_specs=[pl.BlockSpec((None, tr, tc), lambda li, i, j, s: (li, i, j))], out_specs=o_spec),
        out_shape=jax.ShapeDtypeStruct(full, shard.dtype),
        compiler_params=_params(dimension_semantics=("parallel", "parallel", "parallel")),
    )(scalars, shard)


def _gather_weights(shards, kinds, fulls, *, name):
    n_w = len(shards)

    def body(*refs):
        src, dst = refs[:n_w], refs[2 * n_w:3 * n_w]
        send_sems, recv_sems = refs[3 * n_w:]
        x, y, c = _my_place()
        mine = 2 * x + y
        sibling = (x, y, 1 - c)
        chips = _other_chips(x, y)

        def win(i, chip, layer):
            if kinds[i] == "slot":
                return dst[i].at[chip, layer]
            size = shards[i].shape[1] if kinds[i] == "row" else shards[i].shape[2]
            return _window(dst[i], kinds[i], size, chip, (layer,))

        def copy(sem, window, to, source=None):
            return pltpu.make_async_remote_copy(src_ref=window if source is None else source, dst_ref=window,
                                                send_sem=send_sems.at[sem], recv_sem=recv_sems.at[sem],
                                                device_id=to, device_id_type=MESH)

        first =[copy(3 * i + k, win(i, mine, c), (*chip, c), source=src[i].at[c])
                 for k, chip in enumerate(chips) for i in range(n_w)]
        for cp in first:
            cp.start()
        passed = []
        for k, chip in enumerate(chips):
            for i in range(n_w):
                landed = win(i, _chip_no(chip), c)
                copy(3 * i + k, landed, sibling).wait_recv()
                passed.append(copy(3 * n_w + 3 * i + k, landed, sibling))
                passed[-1].start()
        for k, chip in enumerate(chips):
            for i in range(n_w):
                copy(3 * n_w + 3 * i + k, win(i, _chip_no(chip), 1 - c), sibling).wait_recv()
        for cp in first + passed:
            cp.wait_send()

    return pl.pallas_call(
        body, name=name, in_specs=[HBM] * (2 * n_w), out_specs=[HBM] * n_w,
        out_shape=[jax.ShapeDtypeStruct(f.shape, f.dtype) for f in fulls],
        input_output_aliases={n_w + i: i for i in range(n_w)},
        scratch_shapes=[pltpu.SemaphoreType.DMA((6 * n_w,)), pltpu.SemaphoreType.DMA((6 * n_w,))],
    )(*shards, *fulls)


def _grads_to_sibling(grads, *, name):
    n_w = len(grads)

    def body(*refs):
        src, dst = refs[:n_w], refs[n_w:2 * n_w]
        send_sems, recv_sems = refs[2 * n_w:]
        x, y, c = _my_place()
        cps = [pltpu.make_async_remote_copy(src_ref=src[i].at[1 - c], dst_ref=dst[i], send_sem=send_sems.at[i],
                                            recv_sem=recv_sems.at[i], device_id=(x, y, 1 - c), device_id_type=MESH)
               for i in range(n_w)]
        for cp in cps:
            cp.start()
        for cp in cps:
            cp.wait()

    return pl.pallas_call(
        body, name=name, in_specs=[HBM] * n_w, out_specs=[HBM] * n_w,
        out_shape=[jax.ShapeDtypeStruct(g.shape[1:], g.dtype) for g in grads],
        scratch_shapes=[pltpu.SemaphoreType.DMA((n_w,)), pltpu.SemaphoreType.DMA((n_w,))],
    )(*grads)


def _pair_add(mine2, other, *, name):
    _, k, n = mine2.shape
    tr = _div_tile(k, 512, BF16_ROWS)
    c = lax.axis_index("c").astype(jnp.int32).reshape(1)

    def body(c_ref, a_ref, b_ref, o_ref):
        o_ref[...] = (a_ref[...].astype(F32) + b_ref[...].astype(F32)).astype(o_ref.dtype)

    return pl.pallas_call(
        body, name=name,
        grid_spec=pltpu.PrefetchScalarGridSpec(
            num_scalar_prefetch=1, grid=(k // tr,),
            in_specs=[pl.BlockSpec((None, tr, n), lambda i, c_ref: (c_ref[0], i, 0)),
                      pl.BlockSpec((tr, n), lambda i, c_ref: (i, 0))],
            out_specs=pl.BlockSpec((tr, n), lambda i, c_ref: (i, 0))),
        out_shape=jax.ShapeDtypeStruct((k, n), mine2.dtype),
        compiler_params=_params(dimension_semantics=("parallel",)),
    )(c, mine2, other)


def _grads_to_chips(pairs, kinds, sizes, *, name):
    n_w = len(pairs)

    def shard_shape(p, kind, size):
        return {"col": (p.shape[0], size), "row": (size, p.shape[1]), "win_main": (p.shape[0], size),
                "win_strad": (p.shape[0], LANES)}[kind]

    def body(*refs):
        src, dst = refs[:n_w], refs[n_w:2 * n_w]
        send_sems, recv_sems = refs[2 * n_w:]
        x, y, c = _my_place()
        mine = 2 * x + y
        chips = _other_chips(x, y)

        def copy(i, k, chip, window_of, slab):
            return pltpu.make_async_remote_copy(src_ref=_window(src[i], kinds[i], sizes[i], window_of, ()),
                                                dst_ref=dst[i].at[slab], send_sem=send_sems.at[3 * i + k],
                                                recv_sem=recv_sems.at[3 * i + k], device_id=(*chip, c), device_id_type=MESH)

        sends = [copy(i, k, chip, _chip_no(chip), mine) for k, chip in enumerate(chips) for i in range(n_w)]
        for cp in sends:
            cp.start()
        for k, chip in enumerate(chips):
            for i in range(n_w):
                copy(i, k, chip, mine, _chip_no(chip)).wait_recv()
        for cp in sends:
            cp.wait_send()

    return pl.pallas_call(
        body, name=name, in_specs=[HBM] * n_w, out_specs=[HBM] * n_w,
        out_shape=[jax.ShapeDtypeStruct((N_CHIPS,) + shard_shape(p, kind, size), p.dtype)
                   for p, kind, size in zip(pairs, kinds, sizes)],
        scratch_shapes=[pltpu.SemaphoreType.DMA((3 * n_w,)), pltpu.SemaphoreType.DMA((3 * n_w,))],
    )(*pairs)


def _sum_slabs(slabs, pair, kind, size, *, name):
    n_s, k, n = slabs.shape
    tr = _div_tile(k, 512, BF16_ROWS)
    tc = n if kind in ("col", "row") else LANES
    x, y, c = _my_place()
    mine = 2 * x + y
    shard = size + HEAD_DIM
    row0 = mine * (k // tr) if kind == "row" else 0
    col0 = {"col": mine, "row": 0, "win_main": (mine * shard + HEAD_DIM * (mine % 2)) // LANES,
            "win_strad": (size + 2 * shard * (mine // 2)) // LANES}[kind]
    scalars = jnp.stack([c, mine, row0, col0]).astype(jnp.int32)

    def body(s_ref, slab_ref, own_ref, o_ref):
        me = s_ref[1]
        acc = jnp.zeros(o_ref.shape, F32)
        for i in range(n_s):
            acc = acc + jnp.where(me == i, own_ref[...], slab_ref[i]).astype(F32)
        o_ref[...] = acc

    return pl.pallas_call(
        body, name=name,
        grid_spec=pltpu.PrefetchScalarGridSpec(
            num_scalar_prefetch=1, grid=(k // tr, n // tc),
            in_specs=[pl.BlockSpec((n_s, tr, tc), lambda i, j, s: (0, i, j)),
                      pl.BlockSpec((tr, tc), lambda i, j, s: (s[2] + i, s[3] + j))],
            out_specs=pl.BlockSpec((None, tr, tc), lambda i, j, s: (s[0], i, j))),
        out_shape=jax.ShapeDtypeStruct((2, k, n), F32),
        compiler_params=_params(dimension_semantics=("parallel", "parallel")),
    )(scalars, slabs, pair)


def _exchange_layers(bufs, *, name):
    n_w = len(bufs)

    def body(*refs):
        dst = refs[n_w:2 * n_w]
        send_sems, recv_sems = refs[2 * n_w:]
        x, y, c = _my_place()

        def copy(i, layer):
            return pltpu.make_async_remote_copy(src_ref=dst[i].at[layer], dst_ref=dst[i].at[layer], send_sem=send_sems.at[i],
                                                recv_sem=recv_sems.at[i], device_id=(x, y, 1 - c), device_id_type=MESH)

        sends = [copy(i, c) for i in range(n_w)]
        for cp in sends:
            cp.start()
        for i in range(n_w):
            copy(i, 1 - c).wait_recv()
        for cp in sends:
            cp.wait_send()

    return pl.pallas_call(
        body, name=name, in_specs=[HBM] * n_w, out_specs=[HBM] * n_w,
        out_shape=[jax.ShapeDtypeStruct(b.shape, b.dtype) for b in bufs],
        input_output_aliases={i: i for i in range(n_w)},
        scratch_shapes=[pltpu.SemaphoreType.DMA((n_w,)), pltpu.SemaphoreType.DMA((n_w,))],
    )(*bufs)


def _all_sum_small(v, *, name):
    r = v.shape[0]
    relations = [(dx, dy, dc) for dx in (0, 1) for dy in (0, 1) for dc in (0, 1)][1:]

    def body(v_ref, o_ref, buf, send_sems, recv_sems):
        x, y, c = _my_place()
        me = 4 * x + 2 * y + c
        buf[me] = v_ref[...]
        peers = [(x + dx - 2 * x * dx, y + dy - 2 * y * dy, c + dc - 2 * c * dc) for dx, dy, dc in relations]

        def copy(k, slot):
            return pltpu.make_async_remote_copy(src_ref=v_ref, dst_ref=buf.at[slot], send_sem=send_sems.at[k],
                                                recv_sem=recv_sems.at[k], device_id=peers[k], device_id_type=MESH)

        sends = [copy(k, me) for k in range(len(relations))]
        for cp in sends:
            cp.start()
        for k, (px, py, pc) in enumerate(peers):
            copy(k, 4 * px + 2 * py + pc).wait_recv()
        for cp in sends:
            cp.wait_send()
        acc = buf[0]
        for i in range(1, 8):
            acc = acc + buf[i]
        o_ref[...] = acc

    vm = pl.BlockSpec(memory_space=pltpu.VMEM)
    return pl.pallas_call(
        body, name=name, in_specs=[vm], out_specs=vm, out_shape=jax.ShapeDtypeStruct((r, LANES), F32),
        scratch_shapes=[pltpu.VMEM((8, r, LANES), F32), pltpu.SemaphoreType.DMA((7,)), pltpu.SemaphoreType.DMA((7,))],
    )(v)


SHARDED = (("ffn1_w_up", "col"), ("ffn1_w_down", "row"), ("w_in", "win"), ("w_branch_a", "col"),
           ("w_branch_b", "col"), ("w_out", "row"), ("ffn2_w_up", "col"), ("ffn2_w_down", "row"))
REPLICATED = ("ffn1_norm", "mix_norm", "na_rel_bias", "ffn2_norm", "final_norm")


def _gather_all_weights(w):
    even = lax.axis_index("y") == 0
    shards, kinds, names = [], [], []
    for name, kind in SHARDED:
        wb = w[name].astype(BF16)
        if kind == "win":
            main = wb.shape[-1] - HEAD_DIM
            assert main % LANES == 0
            zeros = jnp.zeros(wb.shape[:-1] + (HEAD_DIM,), BF16)
            shards += [jnp.where(even, wb[..., :main], wb[..., HEAD_DIM:]),
                       jnp.where(even, jnp.concatenate([wb[..., main:], zeros], -1),
                                 jnp.concatenate([zeros, wb[..., :HEAD_DIM]], -1))]
            kinds += ["win_main", "slot"]
            names += [name, name + "_strad"]
        else:
            shards.append(wb)
            kinds.append(kind)
            names.append(name)
    own = [_place_own(sh, kind, name=f"own_{nm}") for nm, kind, sh in zip(names, kinds, shards)]
    full = dict(zip(names, _gather_weights(shards, kinds, own, name="gather_weights")))
    strad = full.pop("w_in_strad")
    for i in range(N_CHIPS // 2):
        lo = main + 2 * (main + HEAD_DIM) * i
        full["w_in"] = full["w_in"].at[:, :, lo:lo + LANES].set(strad[2 * i] + strad[2 * i + 1])
    return full


def _reduce_weight_grads(grads, shards):
    names, kinds, sizes, srcs = [], [], [], []
    for name, kind in SHARDED:
        shp = shards[name].shape
        if kind == "win":
            names += [name, name + "_strad"]
            kinds += ["win_main", "win_strad"]
            sizes += [shp[2] - HEAD_DIM] * 2
            srcs += [name, name]
        else:
            names.append(name)
            kinds.append(kind)
            sizes.append(shp[1] if kind == "row" else shp[2])
            srcs.append(name)
    uniq = [name for name, _ in SHARDED]
    arrived = dict(zip(uniq, _grads_to_sibling([grads[n] for n in uniq], name="grads_to_sibling")))
    pair = {n: _pair_add(grads[n], arrived[n], name=f"grads_pair_{n}") for n in uniq}
    slabs = _grads_to_chips([pair[s] for s in srcs], kinds, sizes, name="grads_to_chips")
    halves = [_sum_slabs(sl, pair[s], kind, size, name=f"grads_sum_{n}")
              for n, sl, s, kind, size in zip(names, slabs, srcs, kinds, sizes)]
    out = dict(zip(names, _exchange_layers(halves, name="grads_layers")))
    strad = out.pop("w_in_strad")
    even = lax.axis_index("y") == 0
    out["w_in"] = jnp.where(even, jnp.concatenate([out["w_in"], strad[..., :HEAD_DIM]], -1),
                            jnp.concatenate([strad[..., HEAD_DIM:], out["w_in"]], -1))
    return out


class _Grads:
    def __init__(self, depth):
        self.depth = depth
        self.arrays = {}

    def put(self, weight, layer, a, b, *, cols=None, col_off=0, **kw):
        self.arrays[weight] = _mm(a, b, mode="tn", out_dtype=BF16, out_slab=(layer, self.depth), out_cols=cols,
                                  out_col_off=col_off, out_into=self.arrays.get(weight), **kw)


def _ffn_fwd(x, norm_g, w_up, w_down, layer, tag):
    t, d = x.shape
    f = w_down.shape[1]
    h = _rms_fwd(x, norm_g, tt=512, name=f"{tag}_norm")
    a, gate, up = _mm_swiglu_fwd(h, w_up, layer, tm=512, tn=_div_tile(f, 1408), name=f"{tag}_up")
    x_out = _mm(a, w_down, mode="nn", out_dtype=F32, tm=512, tn=d, tk=f, alpha=0.5, res=x, b_sel=layer, name=f"{tag}_down")
    return x_out, (x, h, a, gate, up)


def _ffn_bwd(dx, dxb, saved, norm_g, w_up, w_down, layer, grads, wname, tag):
    x, h, a, gate, up = saved
    t, d = x.shape
    f = w_down.shape[1]
    tn = _div_tile(f, 1408)
    grads.put(f"{wname}_w_down", layer, a, dxb, tm=tn, tn=d, tk=1024, alpha=0.5, name=f"{tag}_dwd")
    d_gate, d_up = _mm_swiglu_bwd(dxb, w_down, layer, gate, up, alpha=0.5, tm=512, tn=tn, name=f"{tag}_da")
    grads.put(f"{wname}_w_up", layer, h, d_gate, cols=2 * f, tm=d, tn=tn, tk=1024, name=f"{tag}_dwg")
    grads.put(f"{wname}_w_up", layer, h, d_up, cols=2 * f, col_off=f // tn, tm=d, tn=tn, tk=1024, name=f"{tag}_dwu")
    dh = _mm(d_gate, w_up, mode="nt", out_dtype=F32, tm=512, tn=d, tk=f, b_sel=layer, name=f"{tag}_dh1")
    dh = _mm(d_up, w_up, mode="nt", out_dtype=F32, tm=512, tn=d, tk=f, b_sel=layer, b_k_off=1, res=dh, name=f"{tag}_dh2")
    return _rms_bwd(dh, x, norm_g, dx, tt=512, name=f"{tag}_dnorm")


def _to_heads(y, b, n_heads):
    t, w = y.shape
    return y.reshape(b, t // b, n_heads, HEAD_DIM).transpose(0, 2, 1, 3)


def _from_heads(y):
    b, n, s, hd = y.shape
    return y.transpose(0, 2, 1, 3).reshape(b * s, n * hd)


N_QKV = 3 * (DIL_HEADS + NA_HEADS) * HEAD_DIM


def _mixer_fwd(x, b, norm_g, full, layer, bias, tabs, tag):
    t, d = x.shape
    s = t // b
    n_in = full["w_in"].shape[2]
    h = _rms_fwd(x, norm_g, tt=512, name=f"{tag}_norm")
    proj = _mm(h, full["w_in"], mode="nn", out_dtype=F32, tm=512, tn=_div_tile(n_in, 2944), tk=d, b_sel=layer, name=f"{tag}_in")
    heads = _split_heads(proj.reshape(b, s, -1), *tabs, n_pairs=N_QKV // LANES, rot_pairs=DIL_HEADS,
                         scale_ranges=((0, DIL_HEADS // 2), (3 * DIL_HEADS // 2, (3 * DIL_HEADS + NA_HEADS) // 2)),
                         name=f"{tag}_heads")
    ya, lse_a = _dil_attn_fwd(heads, name=f"{tag}_dil")
    yb, lse_b = _na_attn_fwd(heads, bias, first=3 * DIL_HEADS, name=f"{tag}_na")
    ya2, yb2 = _from_heads(ya), _from_heads(yb)
    z = _mm(ya2, full["w_branch_a"], mode="nn", out_dtype=F32, tm=1024, tn=d, tk=ya2.shape[1], b_sel=layer,
            out_slab=(0, 2), name=f"{tag}_za")
    z = _mm(yb2, full["w_branch_b"], mode="nn", out_dtype=F32, tm=1024, tn=d, tk=yb2.shape[1], b_sel=layer,
            out_slab=(1, 2), out_into=z, name=f"{tag}_zb")
    merged = _gate_fwd(proj, z, gate_col=N_QKV, tt=1024, name=f"{tag}_gate")
    x_out = _mm(merged, full["w_out"], mode="nn", out_dtype=F32, tm=1024, tn=d, tk=d, res=x, b_sel=layer, name=f"{tag}_out")
    return x_out, (x, h, proj, heads, ya, lse_a, yb, lse_b, ya2, yb2, z, merged)


def _mixer_bwd(dx, dob, b, saved, norm_g, full, layer, bias, tabs, grads, tag):
    x, h, proj, heads, ya, lse_a, yb, lse_b, ya2, yb2, z, merged = saved
    t, d = x.shape
    s = t // b
    n_in = full["w_in"].shape[2]
    grads.put("w_out", layer, merged, dob, tm=d, tn=d, tk=1024, name=f"{tag}_dwo")
    dm = _mm(dob, full["w_out"], mode="nt", out_dtype=F32, tm=1024, tn=d, tk=d, b_sel=layer, name=f"{tag}_dm")
    dz, dproj = _gate_bwd(dm, proj, z, gate_col=N_QKV, tt=1024, name=f"{tag}_dgate")
    grads.put("w_branch_a", layer, ya2, dz, b_sel=0, tm=ya2.shape[1], tn=d, tk=1024, name=f"{tag}_dwa")
    grads.put("w_branch_b", layer, yb2, dz, b_sel=1, tm=yb2.shape[1], tn=d, tk=1024, name=f"{tag}_dwb")
    dya = _mm(dz, full["w_branch_a"], mode="nt", out_dtype=F32, tm=1024, tn=ya2.shape[1], tk=d, a_sel=0, b_sel=layer, name=f"{tag}_dya")
    dyb = _mm(dz, full["w_branch_b"], mode="nt", out_dtype=F32, tm=1024, tn=yb2.shape[1], tk=d, a_sel=1, b_sel=layer, name=f"{tag}_dyb")
    d_dil = _dil_attn_bwd(heads, ya, lse_a, _to_heads(dya, b, DIL_GROUP_HEADS), name=f"{tag}_ddil")
    d_na, d_bias = _na_attn_bwd(heads, bias, yb, lse_b, _to_heads(dyb, b, NA_HEADS), first=3 * DIL_HEADS, name=f"{tag}_dna")
    dproj = _merge_heads(d_dil, *tabs, heads_per_row=DIL_GROUP_HEADS, rot_pairs=DIL_HEADS, scale_pairs=DIL_HEADS // 2,
                         dilated=True, out_cols=n_in, tile_off=0, into=dproj.reshape(b, s, n_in), name=f"{tag}_dheads_a")
    dproj = _merge_heads(d_na, *tabs, heads_per_row=NA_HEADS, rot_pairs=0, scale_pairs=NA_HEADS // 2, dilated=False,
                         out_cols=n_in, tile_off=3 * DIL_HEADS // 2, into=dproj, name=f"{tag}_dheads_b").reshape(t, n_in)
    grads.put("w_in", layer, h, dproj, tm=_div_tile(d, 512), tn=_div_tile(n_in, 2944), tk=1024, name=f"{tag}_dwin")
    dh = _mm(dproj, full["w_in"], mode="nt", out_dtype=F32, tm=512, tn=d, tk=_div_tile(n_in, 2944), b_sel=layer, name=f"{tag}_dh")
    dx_in, dxb_in, d_norm = _rms_bwd(dh, x, norm_g, dx, tt=512, name=f"{tag}_dnorm")
    d_rb = _na_collapse_bias(d_bias, name=f"{tag}_dbias")
    return dx_in, dxb_in, d_norm, d_rb


def kernel(x, ffn1_norm, ffn1_w_up, ffn1_w_down, mix_norm, w_in, na_rel_bias, w_branch_a, w_branch_b, w_out, ffn2_norm, ffn2_w_up, ffn2_w_down, final_norm, loss_target, m_ffn1_norm, m_ffn1_w_up, m_ffn1_w_down, m_mix_norm, m_w_in, m_na_rel_bias, m_w_branch_a, m_w_branch_b, m_w_out, m_ffn2_norm, m_ffn2_w_up, m_ffn2_w_down, m_final_norm, v_ffn1_norm, v_ffn1_w_up, v_ffn1_w_down, v_mix_norm, v_w_in, v_na_rel_bias, v_w_branch_a, v_w_branch_b, v_w_out, v_ffn2_norm, v_ffn2_w_up, v_ffn2_w_down, v_final_norm):
    w = dict(ffn1_norm=ffn1_norm, ffn1_w_up=ffn1_w_up, ffn1_w_down=ffn1_w_down, mix_norm=mix_norm, w_in=w_in,
             na_rel_bias=na_rel_bias, w_branch_a=w_branch_a, w_branch_b=w_branch_b, w_out=w_out, ffn2_norm=ffn2_norm,
             ffn2_w_up=ffn2_w_up, ffn2_w_down=ffn2_w_down, final_norm=final_norm)
    mom = dict(ffn1_norm=m_ffn1_norm, ffn1_w_up=m_ffn1_w_up, ffn1_w_down=m_ffn1_w_down, mix_norm=m_mix_norm, w_in=m_w_in,
               na_rel_bias=m_na_rel_bias, w_branch_a=m_w_branch_a, w_branch_b=m_w_branch_b, w_out=m_w_out,
               ffn2_norm=m_ffn2_norm, ffn2_w_up=m_ffn2_w_up, ffn2_w_down=m_ffn2_w_down, final_norm=m_final_norm)
    var = dict(ffn1_norm=v_ffn1_norm, ffn1_w_up=v_ffn1_w_up, ffn1_w_down=v_ffn1_w_down, mix_norm=v_mix_norm, w_in=v_w_in,
               na_rel_bias=v_na_rel_bias, w_branch_a=v_w_branch_a, w_branch_b=v_w_branch_b, w_out=v_w_out,
               ffn2_norm=v_ffn2_norm, ffn2_w_up=v_ffn2_w_up, ffn2_w_down=v_ffn2_w_down, final_norm=v_final_norm)
    b, s, d = x.shape
    t = b * s
    depth = ffn1_norm.shape[0]
    assert depth == 2, "core c of a chip sends / reduces layer c"
    shards = {name: w[name] for name, _ in SHARDED}

    full = _gather_all_weights(w)
    tabs = _rope_tables(s)
    bias = _na_expand_bias(na_rel_bias, name="na_bias")

    xc = x.reshape(t, d)
    saved = []
    for l in range(depth):
        xc, s1 = _ffn_fwd(xc, ffn1_norm[l:l + 1], full["ffn1_w_up"], full["ffn1_w_down"], l, f"l{l}_ffn1")
        xc, s2 = _mixer_fwd(xc, b, mix_norm[l:l + 1], full, l, bias[l], tabs, f"l{l}_mix")
        xc, s3 = _ffn_fwd(xc, ffn2_norm[l:l + 1], full["ffn2_w_up"], full["ffn2_w_down"], l, f"l{l}_ffn2")
        saved.append((s1, s2, s3))

    dx, dxb, d_final, loss_part = _final_loss(xc, final_norm.reshape(1, d), loss_target.reshape(t, d), tt=512, name="final_loss")
    grads = _Grads(depth)
    small = {name: [None] * depth for name in REPLICATED[:-1]}
    for l in reversed(range(depth)):
        s1, s2, s3 = saved[l]
        dx, dxb, small["ffn2_norm"][l] = _ffn_bwd(dx, dxb, s3, ffn2_norm[l:l + 1], full["ffn2_w_up"], full["ffn2_w_down"],
                                                  l, grads, "ffn2", f"l{l}_ffn2")
        dx, dxb, small["mix_norm"][l], small["na_rel_bias"][l] = _mixer_bwd(
            dx, dxb, b, s2, mix_norm[l:l + 1], full, l, bias[l], tabs, grads, f"l{l}_mix")
        dx, dxb, small["ffn1_norm"][l] = _ffn_bwd(dx, dxb, s1, ffn1_norm[l:l + 1], full["ffn1_w_up"], full["ffn1_w_down"],
                                                  l, grads, "ffn1", f"l{l}_ffn1")
    grad_x = dx.reshape(b, s, d)

    g_out = _reduce_weight_grads(grads.arrays, shards)
    parts = [jnp.stack(small[name]).reshape(-1) for name in REPLICATED[:-1]] + [d_final.reshape(-1), loss_part[0, :1]]
    sizes = [v.shape[0] for v in parts]
    flat = jnp.concatenate(parts)
    flat = jnp.pad(flat, (0, -flat.shape[0] % (8 * LANES)))
    small_sum = _all_sum_small(flat.reshape(-1, LANES), name="small_all_sum").reshape(-1)
    off = 0
    for name, n in zip(REPLICATED, sizes[:-1]):
        g_out[name] = small_sum[off:off + n].reshape(w[name].shape)
        off += n
    loss = small_sum[off]

    names = list(w)
    delta, new_m, new_v = {}, {}, {}
    for name in names:
        delta[name], new_m[name], new_v[name] = _adamw(w[name], g_out[name], mom[name], var[name], name=f"adamw_{name}")
    return (loss, grad_x, *[g_out[n] for n in names], *[delta[n] for n in names], *[new_m[n] for n in names],
            *[new_v[n] for n in names])
```

```python
import functools

import numpy as np
import jax
import jax.numpy as jnp
from jax import lax
from jax.experimental import pallas as pl
from jax.experimental.pallas import tpu as pltpu

F32, BF16 = jnp.float32, jnp.bfloat16
MESH = pl.DeviceIdType.MESH

HEAD_DIM = 64
DILATIONS = (1, 4, 16)
DIL_HALF = 64
DIL_GROUP_HEADS = 4
DIL_HEADS = 12
NA_HEADS = 8
GRID_W = 64
NA_ROWS = 8
NA_COLS = 16
ROPE_THETA = 10000.0
RMS_EPS = 1e-6
NEG_INF = -1e30
ADAM_LR, ADAM_B1, ADAM_B2, ADAM_EPS, ADAM_WD, ADAM_STEP = 0.001, 0.9, 0.999, 1e-08, 0.01, 10
QK_SCALE = HEAD_DIM ** -0.5

N_CHIPS = 4
LANES = 128
BF16_ROWS = 16
VMEM_LIMIT = 56 * 1024 * 1024
MXU_N = 256
ROWS_NARROW = 2048
ROWS_WIDE = 2048

_NN = (((1,), (0,)), ((), ()))
_NT = (((1,), (1,)), ((), ()))
_TN = (((0,), (0,)), ((), ()))

HBM = pl.BlockSpec(memory_space=pl.ANY)


def _params(**kw):
    return pltpu.CompilerParams(vmem_limit_bytes=VMEM_LIMIT, **kw)


def _dot(a, b, dims):
    return lax.dot_general(a, b, dims, preferred_element_type=F32)


def _div_tile(n, cap, mult=LANES):
    best = None
    for t in range(mult, min(n, cap) + 1, mult):
        if n % t == 0:
            best = t
    return n if best is None else best


def _stacked(block, index, sel):
    if sel is None:
        return pl.BlockSpec(block, index)
    return pl.BlockSpec((None,) + block, lambda *g: (sel,) + index(*g))


def _mm(a, b, *, mode, out_dtype, tm, tn, tk, name, alpha=1.0, res=None, a_sel=None, b_sel=None, b_k_off=0,
        out_slab=None, out_cols=None, out_col_off=0, out_into=None):
    a2, b2 = a.shape[-2:], b.shape[-2:]
    if mode == "nn":
        (m, k), n = a2, b2[1]
        a_spec = _stacked((tm, tk), lambda i, j, kk: (i, kk), a_sel)
        b_spec = _stacked((tk, tn), lambda i, j, kk: (kk + b_k_off, j), b_sel)
        dims = _NN
    elif mode == "nt":
        (m, k), n = a2, b2[0]
        a_spec = _stacked((tm, tk), lambda i, j, kk: (i, kk), a_sel)
        b_spec = _stacked((tn, tk), lambda i, j, kk: (j, kk + b_k_off), b_sel)
        dims = _NT
    else:
        (k, m), n = a2, b2[1]
        a_spec = _stacked((tk, tm), lambda i, j, kk: (kk, i), a_sel)
        b_spec = _stacked((tk, tn), lambda i, j, kk: (kk + b_k_off, j), b_sel)
        dims = _TN
    assert m % tm == 0 and n % tn == 0 and k % tk == 0, (name, a.shape, b.shape)
    nk = k // tk
    has_res = res is not None
    if out_slab is None:
        o_spec = pl.BlockSpec((tm, tn), lambda i, j, kk: (i, j))
        out_shape = jax.ShapeDtypeStruct((m, n), out_dtype)
    else:
        o_spec = _stacked((tm, tn), lambda i, j, kk: (i, j + out_col_off), out_slab[0])
        out_shape = jax.ShapeDtypeStruct((out_slab[1], m, n if out_cols is None else out_cols), out_dtype)
    r_spec = pl.BlockSpec((tm, tn), lambda i, j, kk: (i, j))
    n_in = 2 + has_res + (out_into is not None)

    def body(*refs):
        a_ref, b_ref = refs[0], refs[1]
        r_ref = refs[2] if has_res else None
        o_ref = refs[n_in]
        p = _dot(a_ref[...], b_ref[...], dims)

        def finish(acc):
            y = acc * alpha if alpha != 1.0 else acc
            if has_res:
                y = y + r_ref[...].astype(F32)
            o_ref[...] = y.astype(o_ref.dtype)

        if nk == 1:
            finish(p)
        else:
            acc_ref = refs[n_in + 1]
            kk = pl.program_id(2)

            @pl.when(kk == 0)
            def _():
                acc_ref[...] = p

            @pl.when(kk > 0)
            def _():
                acc_ref[...] += p

            @pl.when(kk == nk - 1)
            def _():
                finish(acc_ref[...])

    operands = [a, b] + ([res] if has_res else [])
    in_specs = [a_spec, b_spec] + ([r_spec] if has_res else [])
    aliases = {}
    if out_into is not None:
        aliases = {len(operands): 0}
        operands.append(out_into)
        in_specs.append(HBM)
    return pl.pallas_call(
        body, name=name, grid=(m // tm, n // tn, nk), in_specs=in_specs, out_specs=o_spec, out_shape=out_shape,
        scratch_shapes=[pltpu.VMEM((tm, tn), F32)] if nk > 1 else [], input_output_aliases=aliases,
        compiler_params=_params(dimension_semantics=("parallel", "parallel", "arbitrary")),
    )(*operands)


def _mm_swiglu_fwd(h, w_up, layer, *, tm, tn, name):
    m, k = h.shape
    n = w_up.shape[2] // 2
    h_spec = pl.BlockSpec((tm, k), lambda i, j: (i, 0))
    wg_spec = pl.BlockSpec((None, k, tn), lambda i, j: (layer, 0, j))
    wu_spec = pl.BlockSpec((None, k, tn), lambda i, j: (layer, 0, j + n // tn))
    o_spec = pl.BlockSpec((tm, tn), lambda i, j: (i, j))

    def body(h_ref, wg_ref, wu_ref, a_ref, g_ref, u_ref):
        hb = h_ref[...]
        g = _dot(hb, wg_ref[...], _NN)
        u = _dot(hb, wu_ref[...], _NN)
        a_ref[...] = (g * jax.nn.sigmoid(g) * u).astype(BF16)
        g_ref[...] = g.astype(BF16)
        u_ref[...] = u.astype(BF16)

    out = jax.ShapeDtypeStruct((m, n), BF16)
    return pl.pallas_call(
        body, name=name, grid=(m // tm, n // tn), in_specs=[h_spec, wg_spec, wu_spec],
        out_specs=[o_spec] * 3, out_shape=[out] * 3,
        compiler_params=_params(dimension_semantics=("parallel", "parallel")),
    )(h, w_up, w_up)


def _mm_swiglu_bwd(dy, w_down, layer, gate, up, *, alpha, tm, tn, name):
    m, k = dy.shape
    n = w_down.shape[1]
    dy_spec = pl.BlockSpec((tm, k), lambda i, j: (i, 0))
    w_spec = pl.BlockSpec((None, tn, k), lambda i, j: (layer, j, 0))
    o_spec = pl.BlockSpec((tm, tn), lambda i, j: (i, j))

    def body(dy_ref, w_ref, g_ref, u_ref, dg_ref, du_ref):
        da = _dot(dy_ref[...], w_ref[...], _NT) * alpha
        g = g_ref[...].astype(F32)
        u = u_ref[...].astype(F32)
        sg = jax.nn.sigmoid(g)
        dg_ref[...] = (da * u * (sg * (1.0 + g * (1.0 - sg)))).astype(BF16)
        du_ref[...] = (da * (g * sg)).astype(BF16)

    out = jax.ShapeDtypeStruct((m, n), BF16)
    return pl.pallas_call(
        body, name=name, grid=(m // tm, n // tn), in_specs=[dy_spec, w_spec, o_spec, o_spec],
        out_specs=[o_spec] * 2, out_shape=[out] * 2,
        compiler_params=_params(dimension_semantics=("parallel", "parallel")),
    )(dy, w_down, gate, up)


def _rms_fwd(x, g, *, tt, name):
    t, d = x.shape

    def body(x_ref, g_ref, h_ref):
        xv = x_ref[...]
        rstd = lax.rsqrt(jnp.mean(xv * xv, axis=1, keepdims=True) + RMS_EPS)
        h_ref[...] = (xv * rstd * g_ref[...]).astype(BF16)

    return pl.pallas_call(
        body, name=name, grid=(t // tt,),
        in_specs=[pl.BlockSpec((tt, d), lambda i: (i, 0)), pl.BlockSpec((1, d), lambda i: (0, 0))],
        out_specs=pl.BlockSpec((tt, d), lambda i: (i, 0)), out_shape=jax.ShapeDtypeStruct((t, d), BF16),
        compiler_params=_params(dimension_semantics=("parallel",)),
    )(x, g)


def _rms_bwd(dh, x, g, dres, *, tt, name):
    t, d = x.shape

    def body(dh_ref, x_ref, g_ref, r_ref, dx_ref, dxb_ref, dg_ref):
        xv = x_ref[...]
        rstd = lax.rsqrt(jnp.mean(xv * xv, axis=1, keepdims=True) + RMS_EPS)
        xhat = xv * rstd
        dhv = dh_ref[...]
        dxhat = dhv * g_ref[...]
        dx = r_ref[...] + rstd * (dxhat - xhat * jnp.mean(dxhat * xhat, axis=1, keepdims=True))
        dx_ref[...] = dx
        dxb_ref[...] = dx.astype(BF16)

        @pl.when(pl.program_id(0) == 0)
        def _():
            dg_ref[...] = jnp.zeros_like(dg_ref)

        dg_ref[...] += jnp.sum(dhv * xhat, axis=0, keepdims=True)

    row = pl.BlockSpec((tt, d), lambda i: (i, 0))
    vec = pl.BlockSpec((1, d), lambda i: (0, 0))
    return pl.pallas_call(
        body, name=name, grid=(t // tt,), in_specs=[row, row, vec, row], out_specs=[row, row, vec],
        out_shape=[jax.ShapeDtypeStruct((t, d), F32), jax.ShapeDtypeStruct((t, d), BF16), jax.ShapeDtypeStruct((1, d), F32)],
        compiler_params=_params(dimension_semantics=("arbitrary",)),
    )(dh, x, g, dres)


def _final_loss(x, g, target, *, tt, name):
    t, d = x.shape

    def body(x_ref, g_ref, t_ref, dx_ref, dxb_ref, dg_ref, loss_ref):
        xv = x_ref[...]
        gv = g_ref[...]
        rstd = lax.rsqrt(jnp.mean(xv * xv, axis=1, keepdims=True) + RMS_EPS)
        xhat = xv * rstd
        err = xhat * gv - t_ref[...]
        dy = err * (1.0 / d)
        dxhat = dy * gv
        dx = rstd * (dxhat - xhat * jnp.mean(dxhat * xhat, axis=1, keepdims=True))
        dx_ref[...] = dx
        dxb_ref[...] = dx.astype(BF16)

        @pl.when(pl.program_id(0) == 0)
        def _():
            dg_ref[...] = jnp.zeros_like(dg_ref)
            loss_ref[...] = jnp.zeros_like(loss_ref)

        dg_ref[...] += jnp.sum(dy * xhat, axis=0, keepdims=True)
        part = 0.5 * jnp.sum(jnp.mean(err * err, axis=1, keepdims=True), axis=0, keepdims=True)
        loss_ref[...] += jnp.broadcast_to(part, loss_ref.shape)

    row = pl.BlockSpec((tt, d), lambda i: (i, 0))
    vec = pl.BlockSpec((1, d), lambda i: (0, 0))
    one = pl.BlockSpec((1, LANES), lambda i: (0, 0))
    return pl.pallas_call(
        body, name=name, grid=(t // tt,), in_specs=[row, vec, row], out_specs=[row, row, vec, one],
        out_shape=[jax.ShapeDtypeStruct((t, d), F32), jax.ShapeDtypeStruct((t, d), BF16), jax.ShapeDtypeStruct((1, d), F32),
                   jax.ShapeDtypeStruct((1, LANES), F32)],
        compiler_params=_params(dimension_semantics=("arbitrary",)),
    )(x, g, target)


def _swap_halves(x):
    lane = lax.broadcasted_iota(jnp.int32, x.shape, 1)
    return jnp.where((lane // 32) % 2 == 0, pltpu.roll(x, 96, 1), pltpu.roll(x, 32, 1))


def _rope_tables(s):
    half = HEAD_DIM // 2
    inv_freq = ROPE_THETA ** (-jnp.arange(half, dtype=F32) / half)
    ang = jnp.arange(s).astype(F32)[:, None] * inv_freq[None, :]
    cos, sin = jnp.cos(ang), jnp.sin(ang)
    return jnp.tile(cos, (1, 4)), jnp.concatenate([-sin, sin, -sin, sin], axis=1)


def _dilation_of_tile(p):
    dilated = p < 3 * DIL_HEADS // 2
    g = (p % (DIL_HEADS // 2)) // (DIL_GROUP_HEADS // 2)
    return [(dilated & (g == gi)) | (jnp.logical_not(dilated) if gi == 0 else False) for gi in range(len(DILATIONS))]


def _residue_major(ref, d):
    s = ref.shape[0]
    if d == 1:
        return ref[...]
    return jnp.concatenate([ref[pl.ds(r, s // d, stride=d), :] for r in range(d)], axis=0)


def _split_heads(proj, cos4, sin4, *, n_pairs, rot_pairs, scale_ranges, name):
    b, s, _ = proj.shape

    def body(x_ref, c_ref, s_ref, o_ref):
        p = pl.program_id(1)
        is_q = functools.reduce(jnp.logical_or, [(p >= lo) & (p < hi) for lo, hi in scale_ranges])
        scale = jnp.where(is_q, QK_SCALE, 1.0)

        def put(y):
            o_ref[0] = y[:, :HEAD_DIM].astype(BF16)
            o_ref[1] = y[:, HEAD_DIM:].astype(BF16)

        for d, in_group in zip(DILATIONS, _dilation_of_tile(p)):
            @pl.when(in_group & (p < rot_pairs))
            def _(d=d):
                x = _residue_major(x_ref, d)
                put((x * _residue_major(c_ref, d) + _swap_halves(x) * _residue_major(s_ref, d)) * scale)

            @pl.when(in_group & (p >= rot_pairs))
            def _(d=d):
                put(_residue_major(x_ref, d) * scale)

    tab = pl.BlockSpec((s, LANES), lambda bi, p: (0, 0))
    return pl.pallas_call(
        body, name=name, grid=(b, n_pairs),
        in_specs=[pl.BlockSpec((None, s, LANES), lambda bi, p: (bi, 0, p)), tab, tab],
        out_specs=pl.BlockSpec((None, 2, s, HEAD_DIM), lambda bi, p: (bi, p, 0, 0)),
        out_shape=jax.ShapeDtypeStruct((b, 2 * n_pairs, s, HEAD_DIM), BF16),
        compiler_params=_params(dimension_semantics=("parallel", "parallel")),
    )(proj, cos4, sin4)


def _merge_heads(dheads, cos4, sin4, *, heads_per_row, rot_pairs, scale_pairs, dilated, out_cols, tile_off, into, name):
    b, hpr, r, s, _ = dheads.shape
    n_pairs = hpr * r // 2
    ppr = hpr // 2

    def body(d_ref, c_ref, s_ref, *rest):
        o_ref, t_ref = rest[-2:]
        p = pl.program_id(1)
        scale = jnp.where(p < scale_pairs, QK_SCALE, 1.0)

        def tokens(d):
            dy = jnp.concatenate([d_ref[0], d_ref[1]], axis=1)
            if d == 1:
                return dy
            for res in range(d):
                t_ref[pl.ds(res, s // d, stride=d), :] = dy[res * (s // d):(res + 1) * (s // d), :]
            return t_ref[...]

        groups = _dilation_of_tile(p) if dilated else [p >= 0]
        for d, in_group in zip(DILATIONS, groups):
            @pl.when(in_group & (p < rot_pairs))
            def _(d=d):
                dy = tokens(d)
                o_ref[...] = ((dy * c_ref[...] - _swap_halves(dy) * s_ref[...]) * scale).astype(BF16)

            @pl.when(in_group & (p >= rot_pairs))
            def _(d=d):
                o_ref[...] = (tokens(d) * scale).astype(BF16)

    tab = pl.BlockSpec((s, LANES), lambda bi, p: (0, 0))
    operands = [dheads, cos4, sin4] + ([] if into is None else [into])
    return pl.pallas_call(
        body, name=name, grid=(b, n_pairs),
        in_specs=[pl.BlockSpec((None, 2, None, s, HEAD_DIM), lambda bi, p: (bi, p % ppr, p // ppr, 0, 0)), tab, tab]
        + ([] if into is None else [HBM]),
        out_specs=pl.BlockSpec((None, s, LANES), lambda bi, p: (bi, 0, p + tile_off)),
        out_shape=jax.ShapeDtypeStruct((b, s, out_cols), BF16),
        input_output_aliases={} if into is None else {3: 0},
        scratch_shapes=[pltpu.VMEM((s, LANES), F32)],
        compiler_params=_params(dimension_semantics=("parallel", "parallel")),
    )(*operands)


DIL_TQ = 256


def _dil_block(g, s):
    run = s // DILATIONS[g]
    return DIL_TQ if run <= DIL_TQ else min(run, DIL_TQ + 2 * LANES)


def _dil_keys(g, q0, s):
    run = max(s // DILATIONS[g], DIL_TQ)
    lo = (q0 // run) * run
    return pl.multiple_of(jnp.clip(q0 - LANES, lo, lo + run - _dil_block(g, s)), LANES)


def _dil_band(g, q0, start, shape, s):
    row = q0 + lax.broadcasted_iota(jnp.int32, shape, 0)
    col = start + lax.broadcasted_iota(jnp.int32, shape, 1)
    ok = jnp.abs(row - col) <= DIL_HALF
    run = s // DILATIONS[g]
    if run < DIL_TQ:
        shift = run.bit_length() - 1
        ok = ok & ((row >> shift) == (col >> shift))
    return ok


def _dil_tokens(g, q0, s):
    d = DILATIONS[g]
    if d == 1:
        return [(0, DIL_TQ, pl.ds(q0, DIL_TQ))]
    run = s // d
    n = min(run, DIL_TQ)
    return [(lo, n, pl.ds(((q0 + lo) % run) * d + (q0 + lo) // run, n, stride=d)) for lo in range(0, DIL_TQ, n)]


def _dil_gather(ref, pieces):
    return jnp.concatenate([ref[rows, :] for _, _, rows in pieces], axis=0) if len(pieces) > 1 else ref[pieces[0][2], :]


def _dil_head_spec(part, g, s):
    return pl.BlockSpec((None, None, s, HEAD_DIM), lambda b, j: (b, part * DIL_HEADS + g * DIL_GROUP_HEADS + j, 0, 0))


def _dil_attn_fwd(heads, *, name):
    b, _, s, _ = heads.shape
    n_g = len(DILATIONS)

    def body(*refs):
        qkv = refs[:3 * n_g]
        o_ref, l_ref, og_ref, lg_ref = refs[3 * n_g:]
        for g in range(n_g):
            q_ref, k_ref, v_ref = qkv[3 * g:3 * g + 3]
            width = _dil_block(g, s)

            def step(i, carry, g=g, q_ref=q_ref, k_ref=k_ref, v_ref=v_ref, width=width):
                q0 = pl.multiple_of(i * DIL_TQ, DIL_TQ)
                start = _dil_keys(g, q0, s)
                sc = _dot(q_ref[pl.ds(q0, DIL_TQ), :], k_ref[pl.ds(start, width), :], _NT)
                sc = jnp.where(_dil_band(g, q0, start, sc.shape, s), sc, NEG_INF)
                m = jnp.max(sc, axis=1, keepdims=True)
                p = jnp.exp(sc - m)
                den = jnp.sum(p, axis=1, keepdims=True)
                o = _dot(p.astype(BF16), v_ref[pl.ds(start, width), :], _NN) / den
                lse = m + jnp.log(den)
                for lo, n, rows in _dil_tokens(g, q0, s):
                    og_ref[g, rows, :] = o[lo:lo + n]
                    lg_ref[g, rows, :] = lse[lo:lo + n]
                return carry

            lax.fori_loop(0, s // DIL_TQ, step, 0)
        lses = [lg_ref[g] for g in range(n_g)]
        m = functools.reduce(jnp.maximum, lses)
        ws = [jnp.exp(l - m) for l in lses]
        den = functools.reduce(jnp.add, ws)
        o_ref[...] = (functools.reduce(jnp.add, [w * og_ref[g] for g, w in enumerate(ws)]) / den).astype(o_ref.dtype)
        l_ref[...] = m + jnp.log(den)

    out = pl.BlockSpec((None, None, s, HEAD_DIM), lambda bi, j: (bi, j, 0, 0))
    lse = pl.BlockSpec((None, None, s, 1), lambda bi, j: (bi, j, 0, 0))
    return pl.pallas_call(
        body, name=name, grid=(b, DIL_GROUP_HEADS),
        in_specs=[_dil_head_spec(part, g, s) for g in range(n_g) for part in range(3)],
        out_specs=[out, lse],
        out_shape=[jax.ShapeDtypeStruct((b, DIL_GROUP_HEADS, s, HEAD_DIM), BF16),
                   jax.ShapeDtypeStruct((b, DIL_GROUP_HEADS, s, 1), F32)],
        scratch_shapes=[pltpu.VMEM((n_g, s, HEAD_DIM), F32), pltpu.VMEM((n_g, s, 1), F32)],
        compiler_params=_params(dimension_semantics=("parallel", "parallel")),
    )(*([heads] * (3 * n_g)))


def _dil_attn_bwd(heads, out, lse, dout, *, name):
    b, _, s, _ = heads.shape
    n_g = len(DILATIONS)

    def body(*refs):
        qkv = refs[:3 * n_g]
        o_ref, l_ref, do_ref, d_ref, delta_ref = refs[3 * n_g:]
        d_ref[...] = jnp.zeros_like(d_ref)
        delta_ref[...] = jnp.sum(do_ref[...] * o_ref[...].astype(F32), axis=1, keepdims=True)
        for g in range(n_g):
            q_ref, k_ref, v_ref = qkv[3 * g:3 * g + 3]
            width = _dil_block(g, s)

            def step(i, carry, g=g, q_ref=q_ref, k_ref=k_ref, v_ref=v_ref, width=width):
                q0 = pl.multiple_of(i * DIL_TQ, DIL_TQ)
                start = _dil_keys(g, q0, s)
                win = pl.ds(start, width)
                pieces = _dil_tokens(g, q0, s)
                do_b = _dil_gather(do_ref, pieces).astype(BF16)
                q, k, v = q_ref[pl.ds(q0, DIL_TQ), :], k_ref[win, :], v_ref[win, :]
                sc = _dot(q, k, _NT)
                p = jnp.where(_dil_band(g, q0, start, sc.shape, s), jnp.exp(sc - _dil_gather(l_ref, pieces)), 0.0)
                ds = (p * (_dot(do_b, v, _NT) - _dil_gather(delta_ref, pieces))).astype(BF16)
                d_ref[g, pl.ds(q0, DIL_TQ), :] = _dot(ds, k, _NN)
                d_ref[n_g + g, win, :] += _dot(ds, q, _TN)
                d_ref[2 * n_g + g, win, :] += _dot(p.astype(BF16), do_b, _TN)
                return carry

            lax.fori_loop(0, s // DIL_TQ, step, 0)

    per_head = lambda bi, j: (bi, j, 0, 0)
    return pl.pallas_call(
        body, name=name, grid=(b, DIL_GROUP_HEADS),
        in_specs=[_dil_head_spec(part, g, s) for g in range(n_g) for part in range(3)]
        + [pl.BlockSpec((None, None, s, HEAD_DIM), per_head), pl.BlockSpec((None, None, s, 1), per_head),
           pl.BlockSpec((None, None, s, HEAD_DIM), per_head)],
        out_specs=pl.BlockSpec((None, None, 3 * n_g, s, HEAD_DIM), lambda bi, j: (bi, j, 0, 0, 0)),
        out_shape=jax.ShapeDtypeStruct((b, DIL_GROUP_HEADS, 3 * n_g, s, HEAD_DIM), F32),
        scratch_shapes=[pltpu.VMEM((s, 1), F32)],
        compiler_params=_params(dimension_semantics=("parallel", "parallel")),
    )(*([heads] * (3 * n_g)), out, lse, dout)


NA_BIAS_ROWS = 2 * NA_ROWS - 1
NA_BIAS_COLS = 2 * NA_COLS - 1
NA_BLOCK = 4
NA_SPAN = NA_ROWS + NA_BLOCK - 1
NA_Q = NA_BLOCK * GRID_W
NA_KEYS = NA_SPAN * GRID_W
NA_FORMS = 3


def _na_onehot():
    c = np.arange(GRID_W)[:, None]
    k = np.arange(GRID_W)[None, :]
    lo = np.clip(c - NA_COLS // 2, 0, GRID_W - NA_COLS)
    valid = (k >= lo) & (k < lo + NA_COLS)
    onehot = np.zeros((GRID_W, GRID_W, LANES), np.float32)
    cc, kk = np.nonzero(valid)
    onehot[cc, kk, kk - cc + NA_COLS - 1] = 1.0
    return onehot.reshape(GRID_W * GRID_W, LANES), valid.reshape(1, GRID_W * GRID_W)


def _na_block_rows(n_rows):
    table = np.full((NA_FORMS, NA_BLOCK, NA_SPAN), NA_BIAS_ROWS, np.int64)
    n_blocks = n_rows // NA_BLOCK
    for form, ib in enumerate((0, 1, n_blocks - 1)):
        base = min(max(NA_BLOCK * ib - NA_ROWS // 2, 0), n_rows - NA_SPAN)
        for rl in range(NA_BLOCK):
            r = NA_BLOCK * ib + rl
            row_lo = min(max(r - NA_ROWS // 2, 0), n_rows - NA_ROWS)
            for kl in range(NA_SPAN):
                if row_lo <= base + kl < row_lo + NA_ROWS:
                    table[form, rl, kl] = base + kl - r + NA_ROWS - 1
    return table


def _na_block(ib, n_rows):
    n_blocks = n_rows // NA_BLOCK
    base = jnp.clip(NA_BLOCK * ib - NA_ROWS // 2, 0, n_rows - NA_SPAN)
    return base, jnp.where(ib == 0, 0, jnp.where(ib == n_blocks - 1, 2, 1))


def _na_expand_bias(rel_bias, *, name):
    l, h, nr, nc = rel_bias.shape
    onehot, valid = _na_onehot()
    rb = jnp.pad(rel_bias, ((0, 0), (0, 0), (0, 1), (0, LANES - nc))).reshape(l * h * (nr + 1), LANES)
    live = jnp.asarray(np.tile(np.arange(nr + 1) < nr, l * h).astype(np.float32)[:, None])

    def body(rb_ref, oh_ref, valid_ref, live_ref, e_ref):
        e = lax.dot_general(rb_ref[...], oh_ref[...], _NT, precision=lax.Precision.HIGHEST, preferred_element_type=F32)
        e_ref[...] = jnp.where((valid_ref[...] > 0) & (live_ref[...] > 0), e, NEG_INF)

    e = pl.pallas_call(
        body, name=name, out_shape=jax.ShapeDtypeStruct((l * h * (nr + 1), GRID_W * GRID_W), F32), compiler_params=_params(),
    )(rb, jnp.asarray(onehot), jnp.asarray(valid.astype(np.float32)), live)
    return e.reshape(l, h, nr + 1, GRID_W, GRID_W)


def _na_collapse_bias(de, *, name):
    b, h = de.shape[:2]
    onehot, _ = _na_onehot()
    rows = h * NA_BIAS_ROWS

    def diag(e_ref, oh_ref, o_ref):
        e = e_ref[0]
        for bi in range(1, b):
            e = e + e_ref[bi]
        o_ref[...] = lax.dot_general(e, oh_ref[...], _NN, precision=lax.Precision.HIGHEST, preferred_element_type=F32)

    drb = pl.pallas_call(
        diag, name=name, out_shape=jax.ShapeDtypeStruct((rows, LANES), F32), compiler_params=_params(),
    )(de.reshape(b, rows, GRID_W * GRID_W), jnp.asarray(onehot))
    return drb[:, :NA_BIAS_COLS].reshape(h, NA_BIAS_ROWS, NA_BIAS_COLS)


def _na_tiles(n_rows):
    table = _na_block_rows(n_rows)
    return [(f, rl, kl, int(table[f, rl, kl])) for f in range(NA_FORMS) for rl in range(NA_BLOCK) for kl in range(NA_SPAN)]


def _na_tile(ref, form, rl, kl):
    return ref.at[form, rl * GRID_W:(rl + 1) * GRID_W, kl * GRID_W:(kl + 1) * GRID_W]


def _na_head_spec(part, first, s):
    return pl.BlockSpec((None, None, s, HEAD_DIM), lambda b, h: (b, first + part * NA_HEADS + h, 0, 0))


def _na_attn_fwd(heads, bias, *, first, name):
    b, _, s, _ = heads.shape
    n_rows = s // GRID_W
    tiles = _na_tiles(n_rows)

    def body(q_ref, k_ref, v_ref, e_ref, o_ref, l_ref, b_ref):
        for form, rl, kl, i in tiles:
            _na_tile(b_ref, form, rl, kl)[...] = e_ref[i]

        def step(ib, carry):
            base, form = _na_block(ib, n_rows)
            rows = pl.ds(pl.multiple_of(ib * NA_Q, NA_Q), NA_Q)
            win = pl.ds(pl.multiple_of(base * GRID_W, GRID_W), NA_KEYS)
            sc = _dot(q_ref[rows, :], k_ref[win, :], _NT) + b_ref[form]
            m = jnp.max(sc, axis=1, keepdims=True)
            p = jnp.exp(sc - m)
            den = jnp.sum(p, axis=1, keepdims=True)
            o_ref[rows, :] = (_dot(p.astype(BF16), v_ref[win, :], _NN) / den).astype(o_ref.dtype)
            l_ref[rows, :] = m + jnp.log(den)
            return carry

        lax.fori_loop(0, n_rows // NA_BLOCK, step, 0)

    per_head = lambda bi, h: (bi, h, 0, 0)
    return pl.pallas_call(
        body, name=name, grid=(b, NA_HEADS),
        in_specs=[_na_head_spec(part, first, s) for part in range(3)]
        + [pl.BlockSpec((None, NA_BIAS_ROWS + 1, GRID_W, GRID_W), lambda bi, h: (h, 0, 0, 0))],
        out_specs=[pl.BlockSpec((None, None, s, HEAD_DIM), per_head), pl.BlockSpec((None, None, s, 1), per_head)],
        out_shape=[jax.ShapeDtypeStruct((b, NA_HEADS, s, HEAD_DIM), BF16), jax.ShapeDtypeStruct((b, NA_HEADS, s, 1), F32)],
        scratch_shapes=[pltpu.VMEM((NA_FORMS, NA_Q, NA_KEYS), F32)],
        compiler_params=_params(dimension_semantics=("parallel", "parallel")),
    )(heads, heads, heads, bias)


def _na_attn_bwd(heads, bias, out, lse, dout, *, first, name):
    b, _, s, _ = heads.shape
    n_rows = s // GRID_W
    tiles = _na_tiles(n_rows)

    def body(q_ref, k_ref, v_ref, e_ref, o_ref, l_ref, do_ref, d_ref, de_ref, b_ref, db_ref):
        for form, rl, kl, i in tiles:
            _na_tile(b_ref, form, rl, kl)[...] = e_ref[i]
        d_ref[...] = jnp.zeros_like(d_ref)
        db_ref[...] = jnp.zeros_like(db_ref)

        def step(ib, carry):
            base, form = _na_block(ib, n_rows)
            rows = pl.ds(pl.multiple_of(ib * NA_Q, NA_Q), NA_Q)
            win = pl.ds(pl.multiple_of(base * GRID_W, GRID_W), NA_KEYS)
            q, k, v = q_ref[rows, :], k_ref[win, :], v_ref[win, :]
            do = do_ref[rows, :]
            delta = jnp.sum(do * o_ref[rows, :].astype(F32), axis=1, keepdims=True)
            do_b = do.astype(BF16)
            p = jnp.exp(_dot(q, k, _NT) + b_ref[form] - l_ref[rows, :])
            ds = p * (_dot(do_b, v, _NT) - delta)
            db_ref[form] += ds
            ds_b = ds.astype(BF16)
            d_ref[0, rows, :] = _dot(ds_b, k, _NN)
            d_ref[1, win, :] += _dot(ds_b, q, _TN)
            d_ref[2, win, :] += _dot(p.astype(BF16), do_b, _TN)
            return carry

        lax.fori_loop(0, n_rows // NA_BLOCK, step, 0)
        acc = [None] * NA_BIAS_ROWS
        for form, rl, kl, i in tiles:
            if i < NA_BIAS_ROWS:
                t = _na_tile(db_ref, form, rl, kl)[...]
                acc[i] = t if acc[i] is None else acc[i] + t
        for i in range(NA_BIAS_ROWS):
            de_ref[i] = acc[i]

    per_head = lambda bi, h: (bi, h, 0, 0)
    return pl.pallas_call(
        body, name=name, grid=(b, NA_HEADS),
        in_specs=[_na_head_spec(part, first, s) for part in range(3)]
        + [pl.BlockSpec((None, NA_BIAS_ROWS + 1, GRID_W, GRID_W), lambda bi, h: (h, 0, 0, 0)),
           pl.BlockSpec((None, None, s, HEAD_DIM), per_head), pl.BlockSpec((None, None, s, 1), per_head),
           pl.BlockSpec((None, None, s, HEAD_DIM), per_head)],
        out_specs=[pl.BlockSpec((None, None, 3, s, HEAD_DIM), lambda bi, h: (bi, h, 0, 0, 0)),
                   pl.BlockSpec((None, None, NA_BIAS_ROWS, GRID_W, GRID_W), lambda bi, h: (bi, h, 0, 0, 0))],
        out_shape=[jax.ShapeDtypeStruct((b, NA_HEADS, 3, s, HEAD_DIM), F32),
                   jax.ShapeDtypeStruct((b, NA_HEADS, NA_BIAS_ROWS, GRID_W, GRID_W), F32)],
        scratch_shapes=[pltpu.VMEM((NA_FORMS, NA_Q, NA_KEYS), F32), pltpu.VMEM((NA_FORMS, NA_Q, NA_KEYS), F32)],
        compiler_params=_params(dimension_semantics=("parallel", "parallel")),
    )(heads, heads, heads, bias, out, lse, dout)


GATE_TILE = 256


def _gate_fwd(proj, z, *, gate_col, tt, name):
    _, t, d = z.shape
    nj = d // GATE_TILE
    c0 = gate_col // GATE_TILE

    def body(ga_ref, gb_ref, za_ref, zb_ref, o_ref):
        o_ref[...] = (jax.nn.sigmoid(ga_ref[...]) * za_ref[...] + jax.nn.sigmoid(gb_ref[...]) * zb_ref[...]).astype(BF16)

    return pl.pallas_call(
        body, name=name, grid=(t // tt, nj),
        in_specs=[pl.BlockSpec((tt, GATE_TILE), lambda i, j: (i, c0 + j)),
                  pl.BlockSpec((tt, GATE_TILE), lambda i, j: (i, c0 + nj + j)),
                  pl.BlockSpec((None, tt, GATE_TILE), lambda i, j: (0, i, j)),
                  pl.BlockSpec((None, tt, GATE_TILE), lambda i, j: (1, i, j))],
        out_specs=pl.BlockSpec((tt, GATE_TILE), lambda i, j: (i, j)), out_shape=jax.ShapeDtypeStruct((t, d), BF16),
        compiler_params=_params(dimension_semantics=("parallel", "parallel")),
    )(proj, proj, z, z)


def _gate_bwd(dm, proj, z, *, gate_col, tt, name):
    _, t, d = z.shape
    nj = d // GATE_TILE
    c0 = gate_col // GATE_TILE

    def body(dm_ref, g_ref, z_ref, dz_ref, dg_ref):
        dmv = dm_ref[...]
        sg = jax.nn.sigmoid(g_ref[...])
        dz_ref[...] = (dmv * sg).astype(BF16)
        dg_ref[...] = (dmv * z_ref[...] * sg * (1.0 - sg)).astype(BF16)

    return pl.pallas_call(
        body, name=name, grid=(t // tt, 2 * nj),
        in_specs=[pl.BlockSpec((tt, GATE_TILE), lambda i, j: (i, j % nj)),
                  pl.BlockSpec((tt, GATE_TILE), lambda i, j: (i, c0 + j)),
                  pl.BlockSpec((None, tt, GATE_TILE), lambda i, j: (j // nj, i, j % nj))],
        out_specs=[pl.BlockSpec((None, tt, GATE_TILE), lambda i, j: (j // nj, i, j % nj)),
                   pl.BlockSpec((tt, GATE_TILE), lambda i, j: (i, c0 + j))],
        out_shape=[jax.ShapeDtypeStruct((2, t, d), BF16), jax.ShapeDtypeStruct(proj.shape, BF16)],
        compiler_params=_params(dimension_semantics=("parallel", "parallel")),
    )(dm, proj, z)


def _adamw(w, g, m, v, *, name):
    shape = w.shape
    w2, g2, m2, v2 = (t.reshape(-1, shape[-1]) for t in (w, g, m, v))
    rows, cols = w2.shape
    tr = rows
    for cand in (512, 256, 128, 64, 32, 16, 8):
        if rows % cand == 0:
            tr = cand
            break

    def body(w_ref, g_ref, m_ref, v_ref, d_ref, nm_ref, nv_ref):
        gv = g_ref[...]
        nm = ADAM_B1 * m_ref[...] + (1.0 - ADAM_B1) * gv
        nv = ADAM_B2 * v_ref[...] + (1.0 - ADAM_B2) * (gv * gv)
        m_hat = nm / (1.0 - ADAM_B1 ** ADAM_STEP)
        v_hat = nv / (1.0 - ADAM_B2 ** ADAM_STEP)
        d_ref[...] = -ADAM_LR * (m_hat / (jnp.sqrt(v_hat) + ADAM_EPS) + ADAM_WD * w_ref[...])
        nm_ref[...] = nm
        nv_ref[...] = nv

    blk = pl.BlockSpec((tr, cols), lambda i: (i, 0))
    out = jax.ShapeDtypeStruct((rows, cols), F32)
    res = pl.pallas_call(
        body, name=name, grid=(rows // tr,), in_specs=[blk] * 4, out_specs=[blk] * 3, out_shape=[out] * 3,
        compiler_params=_params(dimension_semantics=("parallel",)),
    )(w2, g2, m2, v2)
    return tuple(t.reshape(shape) for t in res)


def _my_place():
    return lax.axis_index("x"), lax.axis_index("y"), lax.axis_index("c")


def _other_chips(x, y):
    return [(1 - x, y), (x, 1 - y), (1 - x, 1 - y)]


def _chip_no(chip):
    return 2 * chip[0] + chip[1]


def _window(ref, kind, size, chip, lead):
    if kind == "col":
        return ref.at[(*lead, slice(None), pl.ds(pl.multiple_of(chip * size, LANES), size))]
    if kind == "row":
        return ref.at[(*lead, pl.ds(pl.multiple_of(chip * size, BF16_ROWS), size), slice(None))]
    shard = size + HEAD_DIM
    if kind == "win_main":
        return ref.at[(*lead, slice(None), pl.ds(pl.multiple_of(chip * shard + HEAD_DIM * (chip % 2), LANES), size))]
    assert kind == "win_strad"
    return ref.at[(*lead, slice(None), pl.ds(pl.multiple_of(size + 2 * shard * (chip // 2), LANES), LANES))]


def _place_own(shard, kind, *, name):
    l, k, n = shard.shape
    full = {"col": (l, k, N_CHIPS * n), "row": (l, N_CHIPS * k, n), "win_main": (l, k, N_CHIPS * (n + HEAD_DIM)),
            "slot": (N_CHIPS, l, k, n)}[kind]
    tr = _div_tile(k, 512, BF16_ROWS)
    tc = LANES if kind == "win_main" else n
    mine = 2 * lax.axis_index("x") + lax.axis_index("y")
    row0 = mine * (k // tr) if kind == "row" else 0
    col0 = {"col": mine, "row": 0, "slot": 0, "win_main": (mine * (n + HEAD_DIM) + HEAD_DIM * (mine % 2)) // LANES}[kind]
    scalars = jnp.stack([mine, row0, col0]).astype(jnp.int32)

    def body(s_ref, i_ref, o_ref):
        o_ref[...] = i_ref[...]

    if kind == "slot":
        o_spec = pl.BlockSpec((None, None, tr, tc), lambda li, i, j, s: (s[0], li, i, j))
    else:
        o_spec = pl.BlockSpec((None, tr, tc), lambda li, i, j, s: (li, s[1] + i, s[2] + j))
    return pl.pallas_call(
        body, name=name,
        grid_spec=pltpu.PrefetchScalarGridSpec(
            num_scalar_prefetch=1, grid=(l, k // tr, n // tc),
            in_specs=[pl.BlockSpec((None, tr, tc), lambda li, i, j, s: (li, i, j))], out_specs=o_spec),
        out_shape=jax.ShapeDtypeStruct(full, shard.dtype),
        compiler_params=_params(dimension_semantics=("parallel", "parallel", "parallel")),
    )(scalars, shard)


def _gather_weights(shards, kinds, fulls, *, name):
    n_w = len(shards)

    def body(*refs):
        src, dst = refs[:n_w], refs[2 * n_w:3 * n_w]
        send_sems, recv_sems = refs[3 * n_w:]
        x, y, c = _my_place()
        mine = 2 * x + y
        sibling = (x, y, 1 - c)
        chips = _other_chips(x, y)

        def win(i, chip, layer):
            if kinds[i] == "slot":
                return dst[i].at[chip, layer]
            size = shards[i].shape[1] if kinds[i] == "row" else shards[i].shape[2]
            return _window(dst[i], kinds[i], size, chip, (layer,))

        def copy(sem, window, to, source=None):
            return pltpu.make_async_remote_copy(src_ref=window if source is None else source, dst_ref=window,
                                                send_sem=send_sems.at[sem], recv_sem=recv_sems.at[sem],
                                                device_id=to, device_id_type=MESH)

        first =[copy(3 * i + k, win(i, mine, c), (*chip, c), source=src[i].at[c])
                 for k, chip in enumerate(chips) for i in range(n_w)]
        for cp in first:
            cp.start()
        passed = []
        for k, chip in enumerate(chips):
            for i in range(n_w):
                landed = win(i, _chip_no(chip), c)
                copy(3 * i + k, landed, sibling).wait_recv()
                passed.append(copy(3 * n_w + 3 * i + k, landed, sibling))
                passed[-1].start()
        for k, chip in enumerate(chips):
            for i in range(n_w):
                copy(3 * n_w + 3 * i + k, win(i, _chip_no(chip), 1 - c), sibling).wait_recv()
        for cp in first + passed:
            cp.wait_send()

    return pl.pallas_call(
        body, name=name, in_specs=[HBM] * (2 * n_w), out_specs=[HBM] * n_w,
        out_shape=[jax.ShapeDtypeStruct(f.shape, f.dtype) for f in fulls],
        input_output_aliases={n_w + i: i for i in range(n_w)},
        scratch_shapes=[pltpu.SemaphoreType.DMA((6 * n_w,)), pltpu.SemaphoreType.DMA((6 * n_w,))],
    )(*shards, *fulls)


def _grads_to_sibling(grads, *, name):
    n_w = len(grads)

    def body(*refs):
        src, dst = refs[:n_w], refs[n_w:2 * n_w]
        send_sems, recv_sems = refs[2 * n_w:]
        x, y, c = _my_place()
        cps = [pltpu.make_async_remote_copy(src_ref=src[i].at[1 - c], dst_ref=dst[i], send_sem=send_sems.at[i],
                                            recv_sem=recv_sems.at[i], device_id=(x, y, 1 - c), device_id_type=MESH)
               for i in range(n_w)]
        for cp in cps:
            cp.start()
        for cp in cps:
            cp.wait()

    return pl.pallas_call(
        body, name=name, in_specs=[HBM] * n_w, out_specs=[HBM] * n_w,
        out_shape=[jax.ShapeDtypeStruct(g.shape[1:], g.dtype) for g in grads],
        scratch_shapes=[pltpu.SemaphoreType.DMA((n_w,)), pltpu.SemaphoreType.DMA((n_w,))],
    )(*grads)


def _pair_add(mine2, other, *, name):
    _, k, n = mine2.shape
    tr = _div_tile(k, 512, BF16_ROWS)
    c = lax.axis_index("c").astype(jnp.int32).reshape(1)

    def body(c_ref, a_ref, b_ref, o_ref):
        o_ref[...] = (a_ref[...].astype(F32) + b_ref[...].astype(F32)).astype(o_ref.dtype)

    return pl.pallas_call(
        body, name=name,
        grid_spec=pltpu.PrefetchScalarGridSpec(
            num_scalar_prefetch=1, grid=(k // tr,),
            in_specs=[pl.BlockSpec((None, tr, n), lambda i, c_ref: (c_ref[0], i, 0)),
                      pl.BlockSpec((tr, n), lambda i, c_ref: (i, 0))],
            out_specs=pl.BlockSpec((tr, n), lambda i, c_ref: (i, 0))),
        out_shape=jax.ShapeDtypeStruct((k, n), mine2.dtype),
        compiler_params=_params(dimension_semantics=("parallel",)),
    )(c, mine2, other)


def _grads_to_chips(pairs, kinds, sizes, *, name):
    n_w = len(pairs)

    def shard_shape(p, kind, size):
        return {"col": (p.shape[0], size), "row": (size, p.shape[1]), "win_main": (p.shape[0], size),
                "win_strad": (p.shape[0], LANES)}[kind]

    def body(*refs):
        src, dst = refs[:n_w], refs[n_w:2 * n_w]
        send_sems, recv_sems = refs[2 * n_w:]
        x, y, c = _my_place()
        mine = 2 * x + y
        chips = _other_chips(x, y)

        def copy(i, k, chip, window_of, slab):
            return pltpu.make_async_remote_copy(src_ref=_window(src[i], kinds[i], sizes[i], window_of, ()),
                                                dst_ref=dst[i].at[slab], send_sem=send_sems.at[3 * i + k],
                                                recv_sem=recv_sems.at[3 * i + k], device_id=(*chip, c), device_id_type=MESH)

        sends = [copy(i, k, chip, _chip_no(chip), mine) for k, chip in enumerate(chips) for i in range(n_w)]
        for cp in sends:
            cp.start()
        for k, chip in enumerate(chips):
            for i in range(n_w):
                copy(i, k, chip, mine, _chip_no(chip)).wait_recv()
        for cp in sends:
            cp.wait_send()

    return pl.pallas_call(
        body, name=name, in_specs=[HBM] * n_w, out_specs=[HBM] * n_w,
        out_shape=[jax.ShapeDtypeStruct((N_CHIPS,) + shard_shape(p, kind, size), p.dtype)
                   for p, kind, size in zip(pairs, kinds, sizes)],
        scratch_shapes=[pltpu.SemaphoreType.DMA((3 * n_w,)), pltpu.SemaphoreType.DMA((3 * n_w,))],
    )(*pairs)


def _sum_slabs(slabs, pair, kind, size, *, name):
    n_s, k, n = slabs.shape
    tr = _div_tile(k, 512, BF16_ROWS)
    tc = n if kind in ("col", "row") else LANES
    x, y, c = _my_place()
    mine = 2 * x + y
    shard = size + HEAD_DIM
    row0 = mine * (k // tr) if kind == "row" else 0
    col0 = {"col": mine, "row": 0, "win_main": (mine * shard + HEAD_DIM * (mine % 2)) // LANES,
            "win_strad": (size + 2 * shard * (mine // 2)) // LANES}[kind]
    scalars = jnp.stack([c, mine, row0, col0]).astype(jnp.int32)

    def body(s_ref, slab_ref, own_ref, o_ref):
        me = s_ref[1]
        acc = jnp.zeros(o_ref.shape, F32)
        for i in range(n_s):
            acc = acc + jnp.where(me == i, own_ref[...], slab_ref[i]).astype(F32)
        o_ref[...] = acc

    return pl.pallas_call(
        body, name=name,
        grid_spec=pltpu.PrefetchScalarGridSpec(
            num_scalar_prefetch=1, grid=(k // tr, n // tc),
            in_specs=[pl.BlockSpec((n_s, tr, tc), lambda i, j, s: (0, i, j)),
                      pl.BlockSpec((tr, tc), lambda i, j, s: (s[2] + i, s[3] + j))],
            out_specs=pl.BlockSpec((None, tr, tc), lambda i, j, s: (s[0], i, j))),
        out_shape=jax.ShapeDtypeStruct((2, k, n), F32),
        compiler_params=_params(dimension_semantics=("parallel", "parallel")),
    )(scalars, slabs, pair)


def _exchange_layers(bufs, *, name):
    n_w = len(bufs)

    def body(*refs):
        dst = refs[n_w:2 * n_w]
        send_sems, recv_sems = refs[2 * n_w:]
        x, y, c = _my_place()

        def copy(i, layer):
            return pltpu.make_async_remote_copy(src_ref=dst[i].at[layer], dst_ref=dst[i].at[layer], send_sem=send_sems.at[i],
                                                recv_sem=recv_sems.at[i], device_id=(x, y, 1 - c), device_id_type=MESH)

        sends = [copy(i, c) for i in range(n_w)]
        for cp in sends:
            cp.start()
        for i in range(n_w):
            copy(i, 1 - c).wait_recv()
        for cp in sends:
            cp.wait_send()

    return pl.pallas_call(
        body, name=name, in_specs=[HBM] * n_w, out_specs=[HBM] * n_w,
        out_shape=[jax.ShapeDtypeStruct(b.shape, b.dtype) for b in bufs],
        input_output_aliases={i: i for i in range(n_w)},
        scratch_shapes=[pltpu.SemaphoreType.DMA((n_w,)), pltpu.SemaphoreType.DMA((n_w,))],
    )(*bufs)


def _all_sum_small(v, *, name):
    r = v.shape[0]
    relations = [(dx, dy, dc) for dx in (0, 1) for dy in (0, 1) for dc in (0, 1)][1:]

    def body(v_ref, o_ref, buf, send_sems, recv_sems):
        x, y, c = _my_place()
        me = 4 * x + 2 * y + c
        buf[me] = v_ref[...]
        peers = [(x + dx - 2 * x * dx, y + dy - 2 * y * dy, c + dc - 2 * c * dc) for dx, dy, dc in relations]

        def copy(k, slot):
            return pltpu.make_async_remote_copy(src_ref=v_ref, dst_ref=buf.at[slot], send_sem=send_sems.at[k],
                                                recv_sem=recv_sems.at[k], device_id=peers[k], device_id_type=MESH)

        sends = [copy(k, me) for k in range(len(relations))]
        for cp in sends:
            cp.start()
        for k, (px, py, pc) in enumerate(peers):
            copy(k, 4 * px + 2 * py + pc).wait_recv()
        for cp in sends:
            cp.wait_send()
        acc = buf[0]
        for i in range(1, 8):
            acc = acc + buf[i]
        o_ref[...] = acc

    vm = pl.BlockSpec(memory_space=pltpu.VMEM)
    return pl.pallas_call(
        body, name=name, in_specs=[vm], out_specs=vm, out_shape=jax.ShapeDtypeStruct((r, LANES), F32),
        scratch_shapes=[pltpu.VMEM((8, r, LANES), F32), pltpu.SemaphoreType.DMA((7,)), pltpu.SemaphoreType.DMA((7,))],
    )(v)


SHARDED = (("ffn1_w_up", "col"), ("ffn1_w_down", "row"), ("w_in", "win"), ("w_branch_a", "col"),
           ("w_branch_b", "col"), ("w_out", "row"), ("ffn2_w_up", "col"), ("ffn2_w_down", "row"))
REPLICATED = ("ffn1_norm", "mix_norm", "na_rel_bias", "ffn2_norm", "final_norm")


def _gather_all_weights(w):
    even = lax.axis_index("y") == 0
    shards, kinds, names = [], [], []
    for name, kind in SHARDED:
        wb = w[name].astype(BF16)
        if kind == "win":
            main = wb.shape[-1] - HEAD_DIM
            assert main % LANES == 0
            zeros = jnp.zeros(wb.shape[:-1] + (HEAD_DIM,), BF16)
            shards += [jnp.where(even, wb[..., :main], wb[..., HEAD_DIM:]),
                       jnp.where(even, jnp.concatenate([wb[..., main:], zeros], -1),
                                 jnp.concatenate([zeros, wb[..., :HEAD_DIM]], -1))]
            kinds += ["win_main", "slot"]
            names += [name, name + "_strad"]
        else:
            shards.append(wb)
            kinds.append(kind)
            names.append(name)
    own = [_place_own(sh, kind, name=f"own_{nm}") for nm, kind, sh in zip(names, kinds, shards)]
    full = dict(zip(names, _gather_weights(shards, kinds, own, name="gather_weights")))
    strad = full.pop("w_in_strad")
    for i in range(N_CHIPS // 2):
        lo = main + 2 * (main + HEAD_DIM) * i
        full["w_in"] = full["w_in"].at[:, :, lo:lo + LANES].set(strad[2 * i] + strad[2 * i + 1])
    return full


def _reduce_weight_grads(grads, shards):
    names, kinds, sizes, srcs = [], [], [], []
    for name, kind in SHARDED:
        shp = shards[name].shape
        if kind == "win":
            names += [name, name + "_strad"]
            kinds += ["win_main", "win_strad"]
            sizes += [shp[2] - HEAD_DIM] * 2
            srcs += [name, name]
        else:
            names.append(name)
            kinds.append(kind)
            sizes.append(shp[1] if kind == "row" else shp[2])
            srcs.append(name)
    uniq = [name for name, _ in SHARDED]
    arrived = dict(zip(uniq, _grads_to_sibling([grads[n] for n in uniq], name="grads_to_sibling")))
    pair = {n: _pair_add(grads[n], arrived[n], name=f"grads_pair_{n}") for n in uniq}
    slabs = _grads_to_chips([pair[s] for s in srcs], kinds, sizes, name="grads_to_chips")
    halves = [_sum_slabs(sl, pair[s], kind, size, name=f"grads_sum_{n}")
              for n, sl, s, kind, size in zip(names, slabs, srcs, kinds, sizes)]
    out = dict(zip(names, _exchange_layers(halves, name="grads_layers")))
    strad = out.pop("w_in_strad")
    even = lax.axis_index("y") == 0
    out["w_in"] = jnp.where(even, jnp.concatenate([out["w_in"], strad[..., :HEAD_DIM]], -1),
                            jnp.concatenate([strad[..., HEAD_DIM:], out["w_in"]], -1))
    return out


class _Grads:
    def __init__(self, depth):
        self.depth = depth
        self.arrays = {}

    def put(self, weight, layer, a, b, *, cols=None, col_off=0, **kw):
        self.arrays[weight] = _mm(a, b, mode="tn", out_dtype=BF16, out_slab=(layer, self.depth), out_cols=cols,
                                  out_col_off=col_off, out_into=self.arrays.get(weight), **kw)


def _ffn_fwd(x, norm_g, w_up, w_down, layer, tag):
    t, d = x.shape
    f = w_down.shape[1]
    h = _rms_fwd(x, norm_g, tt=512, name=f"{tag}_norm")
    a, gate, up = _mm_swiglu_fwd(h, w_up, layer, tm=_div_tile(t, ROWS_NARROW, 8), tn=MXU_N, name=f"{tag}_up")
    x_out = _mm(a, w_down, mode="nn", out_dtype=F32, tm=_div_tile(t, ROWS_WIDE, 8), tn=MXU_N, tk=f, alpha=0.5, res=x, b_sel=layer,
                name=f"{tag}_down")
    return x_out, (x, h, a, gate, up)


def _ffn_bwd(dx, dxb, saved, norm_g, w_up, w_down, layer, grads, wname, tag):
    x, h, a, gate, up = saved
    t, d = x.shape
    f = w_down.shape[1]
    tn = _div_tile(f, 1408)
    grads.put(f"{wname}_w_down", layer, a, dxb, tm=tn, tn=d, tk=1024, alpha=0.5, name=f"{tag}_dwd")
    d_gate, d_up = _mm_swiglu_bwd(dxb, w_down, layer, gate, up, alpha=0.5, tm=_div_tile(t, ROWS_NARROW, 8), tn=MXU_N, name=f"{tag}_da")
    grads.put(f"{wname}_w_up", layer, h, d_gate, cols=2 * f, tm=d, tn=tn, tk=1024, name=f"{tag}_dwg")
    grads.put(f"{wname}_w_up", layer, h, d_up, cols=2 * f, col_off=f // tn, tm=d, tn=tn, tk=1024, name=f"{tag}_dwu")
    dh = _mm(d_gate, w_up, mode="nt", out_dtype=F32, tm=_div_tile(t, ROWS_WIDE, 8), tn=MXU_N, tk=f, b_sel=layer, name=f"{tag}_dh1")
    dh = _mm(d_up, w_up, mode="nt", out_dtype=F32, tm=_div_tile(t, ROWS_WIDE, 8), tn=MXU_N, tk=f, b_sel=layer, b_k_off=1, res=dh,
             name=f"{tag}_dh2")
    return _rms_bwd(dh, x, norm_g, dx, tt=512, name=f"{tag}_dnorm")


def _to_heads(y, b, n_heads):
    t, w = y.shape
    return y.reshape(b, t // b, n_heads, HEAD_DIM).transpose(0, 2, 1, 3)


def _from_heads(y):
    b, n, s, hd = y.shape
    return y.transpose(0, 2, 1, 3).reshape(b * s, n * hd)


N_QKV = 3 * (DIL_HEADS + NA_HEADS) * HEAD_DIM


def _mixer_fwd(x, b, norm_g, full, layer, bias, tabs, tag):
    t, d = x.shape
    s = t // b
    n_in = full["w_in"].shape[2]
    h = _rms_fwd(x, norm_g, tt=512, name=f"{tag}_norm")
    proj = _mm(h, full["w_in"], mode="nn", out_dtype=F32, tm=_div_tile(t, ROWS_NARROW, 8), tn=MXU_N, tk=d, b_sel=layer, name=f"{tag}_in")
    heads = _split_heads(proj.reshape(b, s, -1), *tabs, n_pairs=N_QKV // LANES, rot_pairs=DIL_HEADS,
                         scale_ranges=((0, DIL_HEADS // 2), (3 * DIL_HEADS // 2, (3 * DIL_HEADS + NA_HEADS) // 2)),
                         name=f"{tag}_heads")
    ya, lse_a = _dil_attn_fwd(heads, name=f"{tag}_dil")
    yb, lse_b = _na_attn_fwd(heads, bias, first=3 * DIL_HEADS, name=f"{tag}_na")
    ya2, yb2 = _from_heads(ya), _from_heads(yb)
    z = _mm(ya2, full["w_branch_a"], mode="nn", out_dtype=F32, tm=_div_tile(t, ROWS_NARROW, 8), tn=MXU_N, tk=ya2.shape[1], b_sel=layer,
            out_slab=(0, 2), name=f"{tag}_za")
    z = _mm(yb2, full["w_branch_b"], mode="nn", out_dtype=F32, tm=_div_tile(t, ROWS_NARROW, 8), tn=MXU_N, tk=yb2.shape[1], b_sel=layer,
            out_slab=(1, 2), out_into=z, name=f"{tag}_zb")
    merged = _gate_fwd(proj, z, gate_col=N_QKV, tt=1024, name=f"{tag}_gate")
    x_out = _mm(merged, full["w_out"], mode="nn", out_dtype=F32, tm=_div_tile(t, ROWS_NARROW, 8), tn=MXU_N, tk=d, res=x, b_sel=layer,
                name=f"{tag}_out")
    return x_out, (x, h, proj, heads, ya, lse_a, yb, lse_b, ya2, yb2, z, merged)


def _mixer_bwd(dx, dob, b, saved, norm_g, full, layer, bias, tabs, grads, tag):
    x, h, proj, heads, ya, lse_a, yb, lse_b, ya2, yb2, z, merged = saved
    t, d = x.shape
    s = t // b
    n_in = full["w_in"].shape[2]
    grads.put("w_out", layer, merged, dob, tm=d, tn=d, tk=1024, name=f"{tag}_dwo")
    dm = _mm(dob, full["w_out"], mode="nt", out_dtype=F32, tm=_div_tile(t, ROWS_NARROW, 8), tn=MXU_N, tk=d, b_sel=layer, name=f"{tag}_dm")
    dz, dproj = _gate_bwd(dm, proj, z, gate_col=N_QKV, tt=1024, name=f"{tag}_dgate")
    grads.put("w_branch_a", layer, ya2, dz, b_sel=0, tm=ya2.shape[1], tn=d, tk=1024, name=f"{tag}_dwa")
    grads.put("w_branch_b", layer, yb2, dz, b_sel=1, tm=yb2.shape[1], tn=d, tk=1024, name=f"{tag}_dwb")
    dya = _mm(dz, full["w_branch_a"], mode="nt", out_dtype=F32, tm=_div_tile(t, ROWS_NARROW, 8), tn=MXU_N, tk=d, a_sel=0, b_sel=layer,
              name=f"{tag}_dya")
    dyb = _mm(dz, full["w_branch_b"], mode="nt", out_dtype=F32, tm=_div_tile(t, ROWS_NARROW, 8), tn=MXU_N, tk=d, a_sel=1, b_sel=layer,
              name=f"{tag}_dyb")
    d_dil = _dil_attn_bwd(heads, ya, lse_a, _to_heads(dya, b, DIL_GROUP_HEADS), name=f"{tag}_ddil")
    d_na, d_bias = _na_attn_bwd(heads, bias, yb, lse_b, _to_heads(dyb, b, NA_HEADS), first=3 * DIL_HEADS, name=f"{tag}_dna")
    dproj = _merge_heads(d_dil, *tabs, heads_per_row=DIL_GROUP_HEADS, rot_pairs=DIL_HEADS, scale_pairs=DIL_HEADS // 2,
                         dilated=True, out_cols=n_in, tile_off=0, into=dproj.reshape(b, s, n_in), name=f"{tag}_dheads_a")
    dproj = _merge_heads(d_na, *tabs, heads_per_row=NA_HEADS, rot_pairs=0, scale_pairs=NA_HEADS // 2, dilated=False,
                         out_cols=n_in, tile_off=3 * DIL_HEADS // 2, into=dproj, name=f"{tag}_dheads_b").reshape(t, n_in)
    grads.put("w_in", layer, h, dproj, tm=_div_tile(d, 512), tn=_div_tile(n_in, 2944), tk=1024, name=f"{tag}_dwin")
    dh = _mm(dproj, full["w_in"], mode="nt", out_dtype=F32, tm=_div_tile(t, ROWS_WIDE, 8), tn=MXU_N, tk=_div_tile(n_in, 2944), b_sel=layer,
             name=f"{tag}_dh")
    dx_in, dxb_in, d_norm = _rms_bwd(dh, x, norm_g, dx, tt=512, name=f"{tag}_dnorm")
    d_rb = _na_collapse_bias(d_bias, name=f"{tag}_dbias")
    return dx_in, dxb_in, d_norm, d_rb


def kernel(x, ffn1_norm, ffn1_w_up, ffn1_w_down, mix_norm, w_in, na_rel_bias, w_branch_a, w_branch_b, w_out, ffn2_norm, ffn2_w_up, ffn2_w_down, final_norm, loss_target, m_ffn1_norm, m_ffn1_w_up, m_ffn1_w_down, m_mix_norm, m_w_in, m_na_rel_bias, m_w_branch_a, m_w_branch_b, m_w_out, m_ffn2_norm, m_ffn2_w_up, m_ffn2_w_down, m_final_norm, v_ffn1_norm, v_ffn1_w_up, v_ffn1_w_down, v_mix_norm, v_w_in, v_na_rel_bias, v_w_branch_a, v_w_branch_b, v_w_out, v_ffn2_norm, v_ffn2_w_up, v_ffn2_w_down, v_final_norm):
    w = dict(ffn1_norm=ffn1_norm, ffn1_w_up=ffn1_w_up, ffn1_w_down=ffn1_w_down, mix_norm=mix_norm, w_in=w_in,
             na_rel_bias=na_rel_bias, w_branch_a=w_branch_a, w_branch_b=w_branch_b, w_out=w_out, ffn2_norm=ffn2_norm,
             ffn2_w_up=ffn2_w_up, ffn2_w_down=ffn2_w_down, final_norm=final_norm)
    mom = dict(ffn1_norm=m_ffn1_norm, ffn1_w_up=m_ffn1_w_up, ffn1_w_down=m_ffn1_w_down, mix_norm=m_mix_norm, w_in=m_w_in,
               na_rel_bias=m_na_rel_bias, w_branch_a=m_w_branch_a, w_branch_b=m_w_branch_b, w_out=m_w_out,
               ffn2_norm=m_ffn2_norm, ffn2_w_up=m_ffn2_w_up, ffn2_w_down=m_ffn2_w_down, final_norm=m_final_norm)
    var = dict(ffn1_norm=v_ffn1_norm, ffn1_w_up=v_ffn1_w_up, ffn1_w_down=v_ffn1_w_down, mix_norm=v_mix_norm, w_in=v_w_in,
               na_rel_bias=v_na_rel_bias, w_branch_a=v_w_branch_a, w_branch_b=v_w_branch_b, w_out=v_w_out,
               ffn2_norm=v_ffn2_norm, ffn2_w_up=v_ffn2_w_up, ffn2_w_down=v_ffn2_w_down, final_norm=v_final_norm)
    b, s, d = x.shape
    t = b * s
    depth = ffn1_norm.shape[0]
    assert depth == 2, "core c of a chip sends / reduces layer c"
    shards = {name: w[name] for name, _ in SHARDED}

    full = _gather_all_weights(w)
    tabs = _rope_tables(s)
    bias = _na_expand_bias(na_rel_bias, name="na_bias")

    xc = x.reshape(t, d)
    saved = []
    for l in range(depth):
        xc, s1 = _ffn_fwd(xc, ffn1_norm[l:l + 1], full["ffn1_w_up"], full["ffn1_w_down"], l, f"l{l}_ffn1")
        xc, s2 = _mixer_fwd(xc, b, mix_norm[l:l + 1], full, l, bias[l], tabs, f"l{l}_mix")
        xc, s3 = _ffn_fwd(xc, ffn2_norm[l:l + 1], full["ffn2_w_up"], full["ffn2_w_down"], l, f"l{l}_ffn2")
        saved.append((s1, s2, s3))

    dx, dxb, d_final, loss_part = _final_loss(xc, final_norm.reshape(1, d), loss_target.reshape(t, d), tt=512, name="final_loss")
    grads = _Grads(depth)
    small = {name: [None] * depth for name in REPLICATED[:-1]}
    for l in reversed(range(depth)):
        s1, s2, s3 = saved[l]
        dx, dxb, small["ffn2_norm"][l] = _ffn_bwd(dx, dxb, s3, ffn2_norm[l:l + 1], full["ffn2_w_up"], full["ffn2_w_down"],
                                                  l, grads, "ffn2", f"l{l}_ffn2")
        dx, dxb, small["mix_norm"][l], small["na_rel_bias"][l] = _mixer_bwd(
            dx, dxb, b, s2, mix_norm[l:l + 1], full, l, bias[l], tabs, grads, f"l{l}_mix")
        dx, dxb, small["ffn1_norm"][l] = _ffn_bwd(dx, dxb, s1, ffn1_norm[l:l + 1], full["ffn1_w_up"], full["ffn1_w_down"],
                                                  l, grads, "ffn1", f"l{l}_ffn1")
    grad_x = dx.reshape(b, s, d)

    g_out = _reduce_weight_grads(grads.arrays, shards)
    parts = [jnp.stack(small[name]).reshape(-1) for name in REPLICATED[:-1]] + [d_final.reshape(-1), loss_part[0, :1]]
    sizes = [v.shape[0] for v in parts]
    flat = jnp.concatenate(parts)
    flat = jnp.pad(flat, (0, -flat.shape[0] % (8 * LANES)))
    small_sum = _all_sum_small(flat.reshape(-1, LANES), name="small_all_sum").reshape(-1)
    off = 0
    for name, n in zip(REPLICATED, sizes[:-1]):
        g_out[name] = small_sum[off:off + n].reshape(w[name].shape)
        off += n
    loss = small_sum[off]

    names = list(w)
    delta, new_m, new_v = {}, {}, {}
    for name in names:
        delta[name], new_m[name], new_v[name] = _adamw(w[name], g_out[name], mom[name], var[name], name=f"adamw_{name}")
    return (loss, grad_x, *[g_out[n] for n in names], *[delta[n] for n in names], *[new_m[n] for n in names],
            *[new_v[n] for n in names])
```

```python
import functools

import numpy as np
import jax
import jax.numpy as jnp
from jax import lax
from jax.experimental import pallas as pl
from jax.experimental.pallas import tpu as pltpu

F32, BF16 = jnp.float32, jnp.bfloat16
MESH = pl.DeviceIdType.MESH

HEAD_DIM = 64
DILATIONS = (1, 4, 16)
DIL_HALF = 64
DIL_GROUP_HEADS = 4
DIL_HEADS = 12
NA_HEADS = 8
GRID_W = 64
NA_ROWS = 8
NA_COLS = 16
ROPE_THETA = 10000.0
RMS_EPS = 1e-6
NEG_INF = -1e30
ADAM_LR, ADAM_B1, ADAM_B2, ADAM_EPS, ADAM_WD, ADAM_STEP = 0.001, 0.9, 0.999, 1e-08, 0.01, 10
QK_SCALE = HEAD_DIM ** -0.5

N_CHIPS = 4
LANES = 128
BF16_ROWS = 16
VMEM_LIMIT = 56 * 1024 * 1024
MXU_N = 256
ROWS_NARROW = 2048
ROWS_WIDE = 512

_NN = (((1,), (0,)), ((), ()))
_NT = (((1,), (1,)), ((), ()))
_TN = (((0,), (0,)), ((), ()))

HBM = pl.BlockSpec(memory_space=pl.ANY)


def _params(**kw):
    return pltpu.CompilerParams(vmem_limit_bytes=VMEM_LIMIT, **kw)


def _dot(a, b, dims):
    return lax.dot_general(a, b, dims, preferred_element_type=F32)


def _div_tile(n, cap, mult=LANES):
    best = None
    for t in range(mult, min(n, cap) + 1, mult):
        if n % t == 0:
            best = t
    return n if best is None else best


def _stacked(block, index, sel):
    if sel is None:
        return pl.BlockSpec(block, index)
    return pl.BlockSpec((None,) + block, lambda *g: (sel,) + index(*g))


def _mm(a, b, *, mode, out_dtype, tm, tn, tk, name, alpha=1.0, res=None, a_sel=None, b_sel=None, b_k_off=0,
        out_slab=None, out_cols=None, out_col_off=0, out_into=None):
    a2, b2 = a.shape[-2:], b.shape[-2:]
    if mode == "nn":
        (m, k), n = a2, b2[1]
        a_spec = _stacked((tm, tk), lambda i, j, kk: (i, kk), a_sel)
        b_spec = _stacked((tk, tn), lambda i, j, kk: (kk + b_k_off, j), b_sel)
        dims = _NN
    elif mode == "nt":
        (m, k), n = a2, b2[0]
        a_spec = _stacked((tm, tk), lambda i, j, kk: (i, kk), a_sel)
        b_spec = _stacked((tn, tk), lambda i, j, kk: (j, kk + b_k_off), b_sel)
        dims = _NT
    else:
        (k, m), n = a2, b2[1]
        a_spec = _stacked((tk, tm), lambda i, j, kk: (kk, i), a_sel)
        b_spec = _stacked((tk, tn), lambda i, j, kk: (kk + b_k_off, j), b_sel)
        dims = _TN
    assert m % tm == 0 and n % tn == 0 and k % tk == 0, (name, a.shape, b.shape)
    nk = k // tk
    has_res = res is not None
    if out_slab is None:
        o_spec = pl.BlockSpec((tm, tn), lambda i, j, kk: (i, j))
        out_shape = jax.ShapeDtypeStruct((m, n), out_dtype)
    else:
        o_spec = _stacked((tm, tn), lambda i, j, kk: (i, j + out_col_off), out_slab[0])
        out_shape = jax.ShapeDtypeStruct((out_slab[1], m, n if out_cols is None else out_cols), out_dtype)
    r_spec = pl.BlockSpec((tm, tn), lambda i, j, kk: (i, j))
    n_in = 2 + has_res + (out_into is not None)

    def body(*refs):
        a_ref, b_ref = refs[0], refs[1]
        r_ref = refs[2] if has_res else None
        o_ref = refs[n_in]
        p = _dot(a_ref[...], b_ref[...], dims)

        def finish(acc):
            y = acc * alpha if alpha != 1.0 else acc
            if has_res:
                y = y + r_ref[...].astype(F32)
            o_ref[...] = y.astype(o_ref.dtype)

        if nk == 1:
            finish(p)
        else:
            acc_ref = refs[n_in + 1]
            kk = pl.program_id(2)

            @pl.when(kk == 0)
            def _():
                acc_ref[...] = p

            @pl.when(kk > 0)
            def _():
                acc_ref[...] += p

            @pl.when(kk == nk - 1)
            def _():
                finish(acc_ref[...])

    operands = [a, b] + ([res] if has_res else [])
    in_specs = [a_spec, b_spec] + ([r_spec] if has_res else [])
    aliases = {}
    if out_into is not None:
        aliases = {len(operands): 0}
        operands.append(out_into)
        in_specs.append(HBM)
    return pl.pallas_call(
        body, name=name, grid=(m // tm, n // tn, nk), in_specs=in_specs, out_specs=o_spec, out_shape=out_shape,
        scratch_shapes=[pltpu.VMEM((tm, tn), F32)] if nk > 1 else [], input_output_aliases=aliases,
        compiler_params=_params(dimension_semantics=("parallel", "parallel", "arbitrary")),
    )(*operands)


def _mm_swiglu_fwd(h, w_up, *, tm, tn, name):
    m, k = h.shape
    n = w_up.shape[1] // 2
    h_spec = pl.BlockSpec((tm, k), lambda i, j: (i, 0))
    wg_spec = pl.BlockSpec((k, tn), lambda i, j: (0, j))
    wu_spec = pl.BlockSpec((k, tn), lambda i, j: (0, j + n // tn))
    o_spec = pl.BlockSpec((tm, tn), lambda i, j: (i, j))

    def body(h_ref, wg_ref, wu_ref, a_ref, g_ref, u_ref):
        hb = h_ref[...]
        g = _dot(hb, wg_ref[...], _NN)
        u = _dot(hb, wu_ref[...], _NN)
        a_ref[...] = (g * jax.nn.sigmoid(g) * u).astype(BF16)
        g_ref[...] = g.astype(BF16)
        u_ref[...] = u.astype(BF16)

    out = jax.ShapeDtypeStruct((m, n), BF16)
    return pl.pallas_call(
        body, name=name, grid=(m // tm, n // tn), in_specs=[h_spec, wg_spec, wu_spec],
        out_specs=[o_spec] * 3, out_shape=[out] * 3,
        compiler_params=_params(dimension_semantics=("parallel", "parallel")),
    )(h, w_up, w_up)


def _mm_swiglu_bwd(dy, w_down, gate, up, *, alpha, tm, tn, name):
    m, k = dy.shape
    n = w_down.shape[0]
    dy_spec = pl.BlockSpec((tm, k), lambda i, j: (i, 0))
    w_spec = pl.BlockSpec((tn, k), lambda i, j: (j, 0))
    o_spec = pl.BlockSpec((tm, tn), lambda i, j: (i, j))

    def body(dy_ref, w_ref, g_ref, u_ref, dg_ref, du_ref):
        da = _dot(dy_ref[...], w_ref[...], _NT) * alpha
        g = g_ref[...].astype(F32)
        u = u_ref[...].astype(F32)
        sg = jax.nn.sigmoid(g)
        dg_ref[...] = (da * u * (sg * (1.0 + g * (1.0 - sg)))).astype(BF16)
        du_ref[...] = (da * (g * sg)).astype(BF16)

    out = jax.ShapeDtypeStruct((m, n), BF16)
    return pl.pallas_call(
        body, name=name, grid=(m // tm, n // tn), in_specs=[dy_spec, w_spec, o_spec, o_spec],
        out_specs=[o_spec] * 2, out_shape=[out] * 2,
        compiler_params=_params(dimension_semantics=("parallel", "parallel")),
    )(dy, w_down, gate, up)


def _rms_fwd(x, g, *, tt, name):
    t, d = x.shape

    def body(x_ref, g_ref, h_ref):
        xv = x_ref[...]
        rstd = lax.rsqrt(jnp.mean(xv * xv, axis=1, keepdims=True) + RMS_EPS)
        h_ref[...] = (xv * rstd * g_ref[...]).astype(BF16)

    return pl.pallas_call(
        body, name=name, grid=(t // tt,),
        in_specs=[pl.BlockSpec((tt, d), lambda i: (i, 0)), pl.BlockSpec((1, d), lambda i: (0, 0))],
        out_specs=pl.BlockSpec((tt, d), lambda i: (i, 0)), out_shape=jax.ShapeDtypeStruct((t, d), BF16),
        compiler_params=_params(dimension_semantics=("parallel",)),
    )(x, g)


def _rms_bwd(dh, x, g, dres, *, tt, name):
    t, d = x.shape

    def body(dh_ref, x_ref, g_ref, r_ref, dx_ref, dxb_ref, dg_ref):
        xv = x_ref[...]
        rstd = lax.rsqrt(jnp.mean(xv * xv, axis=1, keepdims=True) + RMS_EPS)
        xhat = xv * rstd
        dhv = dh_ref[...]
        dxhat = dhv * g_ref[...]
        dx = r_ref[...] + rstd * (dxhat - xhat * jnp.mean(dxhat * xhat, axis=1, keepdims=True))
        dx_ref[...] = dx
        dxb_ref[...] = dx.astype(BF16)

        @pl.when(pl.program_id(0) == 0)
        def _():
            dg_ref[...] = jnp.zeros_like(dg_ref)

        dg_ref[...] += jnp.sum(dhv * xhat, axis=0, keepdims=True)

    row = pl.BlockSpec((tt, d), lambda i: (i, 0))
    vec = pl.BlockSpec((1, d), lambda i: (0, 0))
    return pl.pallas_call(
        body, name=name, grid=(t // tt,), in_specs=[row, row, vec, row], out_specs=[row, row, vec],
        out_shape=[jax.ShapeDtypeStruct((t, d), F32), jax.ShapeDtypeStruct((t, d), BF16), jax.ShapeDtypeStruct((1, d), F32)],
        compiler_params=_params(dimension_semantics=("arbitrary",)),
    )(dh, x, g, dres)


def _final_loss(x, g, target, *, tt, name):
    t, d = x.shape

    def body(x_ref, g_ref, t_ref, dx_ref, dxb_ref, dg_ref, loss_ref):
        xv = x_ref[...]
        gv = g_ref[...]
        rstd = lax.rsqrt(jnp.mean(xv * xv, axis=1, keepdims=True) + RMS_EPS)
        xhat = xv * rstd
        err = xhat * gv - t_ref[...]
        dy = err * (1.0 / d)
        dxhat = dy * gv
        dx = rstd * (dxhat - xhat * jnp.mean(dxhat * xhat, axis=1, keepdims=True))
        dx_ref[...] = dx
        dxb_ref[...] = dx.astype(BF16)

        @pl.when(pl.program_id(0) == 0)
        def _():
            dg_ref[...] = jnp.zeros_like(dg_ref)
            loss_ref[...] = jnp.zeros_like(loss_ref)

        dg_ref[...] += jnp.sum(dy * xhat, axis=0, keepdims=True)
        part = 0.5 * jnp.sum(jnp.mean(err * err, axis=1, keepdims=True), axis=0, keepdims=True)
        loss_ref[...] += jnp.broadcast_to(part, loss_ref.shape)

    row = pl.BlockSpec((tt, d), lambda i: (i, 0))
    vec = pl.BlockSpec((1, d), lambda i: (0, 0))
    one = pl.BlockSpec((1, LANES), lambda i: (0, 0))
    return pl.pallas_call(
        body, name=name, grid=(t // tt,), in_specs=[row, vec, row], out_specs=[row, row, vec, one],
        out_shape=[jax.ShapeDtypeStruct((t, d), F32), jax.ShapeDtypeStruct((t, d), BF16), jax.ShapeDtypeStruct((1, d), F32),
                   jax.ShapeDtypeStruct((1, LANES), F32)],
        compiler_params=_params(dimension_semantics=("arbitrary",)),
    )(x, g, target)


def _swap_halves(x):
    lane = lax.broadcasted_iota(jnp.int32, x.shape, 1)
    return jnp.where((lane // 32) % 2 == 0, pltpu.roll(x, 96, 1), pltpu.roll(x, 32, 1))


def _rope_tables(s):
    half = HEAD_DIM // 2
    inv_freq = ROPE_THETA ** (-jnp.arange(half, dtype=F32) / half)
    ang = jnp.arange(s).astype(F32)[:, None] * inv_freq[None, :]
    cos, sin = jnp.cos(ang), jnp.sin(ang)
    return jnp.tile(cos, (1, 4)), jnp.concatenate([-sin, sin, -sin, sin], axis=1)


def _dilation_of_tile(p):
    dilated = p < 3 * DIL_HEADS // 2
    g = (p % (DIL_HEADS // 2)) // (DIL_GROUP_HEADS // 2)
    return [(dilated & (g == gi)) | (jnp.logical_not(dilated) if gi == 0 else False) for gi in range(len(DILATIONS))]


def _residue_major(ref, d):
    s = ref.shape[0]
    if d == 1:
        return ref[...]
    return jnp.concatenate([ref[pl.ds(r, s // d, stride=d), :] for r in range(d)], axis=0)


def _split_heads(proj, cos4, sin4, *, n_pairs, rot_pairs, scale_ranges, name):
    b, s, _ = proj.shape

    def body(x_ref, c_ref, s_ref, o_ref):
        p = pl.program_id(1)
        is_q = functools.reduce(jnp.logical_or, [(p >= lo) & (p < hi) for lo, hi in scale_ranges])
        scale = jnp.where(is_q, QK_SCALE, 1.0)

        def put(y):
            o_ref[0] = y[:, :HEAD_DIM].astype(BF16)
            o_ref[1] = y[:, HEAD_DIM:].astype(BF16)

        for d, in_group in zip(DILATIONS, _dilation_of_tile(p)):
            @pl.when(in_group & (p < rot_pairs))
            def _(d=d):
                x = _residue_major(x_ref, d)
                put((x * _residue_major(c_ref, d) + _swap_halves(x) * _residue_major(s_ref, d)) * scale)

            @pl.when(in_group & (p >= rot_pairs))
            def _(d=d):
                put(_residue_major(x_ref, d) * scale)

    tab = pl.BlockSpec((s, LANES), lambda bi, p: (0, 0))
    return pl.pallas_call(
        body, name=name, grid=(b, n_pairs),
        in_specs=[pl.BlockSpec((None, s, LANES), lambda bi, p: (bi, 0, p)), tab, tab],
        out_specs=pl.BlockSpec((None, 2, s, HEAD_DIM), lambda bi, p: (bi, p, 0, 0)),
        out_shape=jax.ShapeDtypeStruct((b, 2 * n_pairs, s, HEAD_DIM), BF16),
        compiler_params=_params(dimension_semantics=("parallel", "parallel")),
    )(proj, cos4, sin4)


def _merge_heads(dheads, cos4, sin4, *, heads_per_row, rot_pairs, scale_pairs, dilated, out_cols, tile_off, into, name):
    b, hpr, r, s, _ = dheads.shape
    n_pairs = hpr * r // 2
    ppr = hpr // 2

    def body(d_ref, c_ref, s_ref, *rest):
        o_ref, t_ref = rest[-2:]
        p = pl.program_id(1)
        scale = jnp.where(p < scale_pairs, QK_SCALE, 1.0)

        def tokens(d):
            dy = jnp.concatenate([d_ref[0], d_ref[1]], axis=1)
            if d == 1:
                return dy
            for res in range(d):
                t_ref[pl.ds(res, s // d, stride=d), :] = dy[res * (s // d):(res + 1) * (s // d), :]
            return t_ref[...]

        groups = _dilation_of_tile(p) if dilated else [p >= 0]
        for d, in_group in zip(DILATIONS, groups):
            @pl.when(in_group & (p < rot_pairs))
            def _(d=d):
                dy = tokens(d)
                o_ref[...] = ((dy * c_ref[...] - _swap_halves(dy) * s_ref[...]) * scale).astype(BF16)

            @pl.when(in_group & (p >= rot_pairs))
            def _(d=d):
                o_ref[...] = (tokens(d) * scale).astype(BF16)

    tab = pl.BlockSpec((s, LANES), lambda bi, p: (0, 0))
    operands = [dheads, cos4, sin4] + ([] if into is None else [into])
    return pl.pallas_call(
        body, name=name, grid=(b, n_pairs),
        in_specs=[pl.BlockSpec((None, 2, None, s, HEAD_DIM), lambda bi, p: (bi, p % ppr, p // ppr, 0, 0)), tab, tab]
        + ([] if into is None else [HBM]),
        out_specs=pl.BlockSpec((None, s, LANES), lambda bi, p: (bi, 0, p + tile_off)),
        out_shape=jax.ShapeDtypeStruct((b, s, out_cols), BF16),
        input_output_aliases={} if into is None else {3: 0},
        scratch_shapes=[pltpu.VMEM((s, LANES), F32)],
        compiler_params=_params(dimension_semantics=("parallel", "parallel")),
    )(*operands)


DIL_TQ = 256


def _dil_block(g, s):
    run = s // DILATIONS[g]
    return DIL_TQ if run <= DIL_TQ else min(run, DIL_TQ + 2 * LANES)


def _dil_keys(g, q0, s):
    run = max(s // DILATIONS[g], DIL_TQ)
    lo = (q0 // run) * run
    return pl.multiple_of(jnp.clip(q0 - LANES, lo, lo + run - _dil_block(g, s)), LANES)


def _dil_band(g, q0, start, shape, s):
    row = q0 + lax.broadcasted_iota(jnp.int32, shape, 0)
    col = start + lax.broadcasted_iota(jnp.int32, shape, 1)
    ok = jnp.abs(row - col) <= DIL_HALF
    run = s // DILATIONS[g]
    if run < DIL_TQ:
        shift = run.bit_length() - 1
        ok = ok & ((row >> shift) == (col >> shift))
    return ok


def _dil_tokens(g, q0, s):
    d = DILATIONS[g]
    if d == 1:
        return [(0, DIL_TQ, pl.ds(q0, DIL_TQ))]
    run = s // d
    n = min(run, DIL_TQ)
    return [(lo, n, pl.ds(((q0 + lo) % run) * d + (q0 + lo) // run, n, stride=d)) for lo in range(0, DIL_TQ, n)]


def _dil_gather(ref, pieces):
    return jnp.concatenate([ref[rows, :] for _, _, rows in pieces], axis=0) if len(pieces) > 1 else ref[pieces[0][2], :]


def _dil_head_spec(part, g, s):
    return pl.BlockSpec((None, None, s, HEAD_DIM), lambda b, j: (b, part * DIL_HEADS + g * DIL_GROUP_HEADS + j, 0, 0))


def _dil_attn_fwd(heads, *, name):
    b, _, s, _ = heads.shape
    n_g = len(DILATIONS)

    def body(*refs):
        qkv = refs[:3 * n_g]
        o_ref, l_ref, og_ref, lg_ref = refs[3 * n_g:]
        for g in range(n_g):
            q_ref, k_ref, v_ref = qkv[3 * g:3 * g + 3]
            width = _dil_block(g, s)

            def step(i, carry, g=g, q_ref=q_ref, k_ref=k_ref, v_ref=v_ref, width=width):
                q0 = pl.multiple_of(i * DIL_TQ, DIL_TQ)
                start = _dil_keys(g, q0, s)
                sc = _dot(q_ref[pl.ds(q0, DIL_TQ), :], k_ref[pl.ds(start, width), :], _NT)
                sc = jnp.where(_dil_band(g, q0, start, sc.shape, s), sc, NEG_INF)
                m = jnp.max(sc, axis=1, keepdims=True)
                p = jnp.exp(sc - m)
                den = jnp.sum(p, axis=1, keepdims=True)
                o = _dot(p.astype(BF16), v_ref[pl.ds(start, width), :], _NN) / den
                lse = m + jnp.log(den)
                for lo, n, rows in _dil_tokens(g, q0, s):
                    og_ref[g, rows, :] = o[lo:lo + n]
                    lg_ref[g, rows, :] = lse[lo:lo + n]
                return carry

            lax.fori_loop(0, s // DIL_TQ, step, 0)
        lses = [lg_ref[g] for g in range(n_g)]
        m = functools.reduce(jnp.maximum, lses)
        ws = [jnp.exp(l - m) for l in lses]
        den = functools.reduce(jnp.add, ws)
        o_ref[...] = (functools.reduce(jnp.add, [w * og_ref[g] for g, w in enumerate(ws)]) / den).astype(o_ref.dtype)
        l_ref[...] = m + jnp.log(den)

    out = pl.BlockSpec((None, None, s, HEAD_DIM), lambda bi, j: (bi, j, 0, 0))
    lse = pl.BlockSpec((None, None, s, 1), lambda bi, j: (bi, j, 0, 0))
    return pl.pallas_call(
        body, name=name, grid=(b, DIL_GROUP_HEADS),
        in_specs=[_dil_head_spec(part, g, s) for g in range(n_g) for part in range(3)],
        out_specs=[out, lse],
        out_shape=[jax.ShapeDtypeStruct((b, DIL_GROUP_HEADS, s, HEAD_DIM), BF16),
                   jax.ShapeDtypeStruct((b, DIL_GROUP_HEADS, s, 1), F32)],
        scratch_shapes=[pltpu.VMEM((n_g, s, HEAD_DIM), F32), pltpu.VMEM((n_g, s, 1), F32)],
        compiler_params=_params(dimension_semantics=("parallel", "parallel")),
    )(*([heads] * (3 * n_g)))


def _dil_attn_bwd(heads, out, lse, dout, *, name):
    b, _, s, _ = heads.shape
    n_g = len(DILATIONS)

    def body(*refs):
        qkv = refs[:3 * n_g]
        o_ref, l_ref, do_ref, d_ref, delta_ref = refs[3 * n_g:]
        d_ref[...] = jnp.zeros_like(d_ref)
        delta_ref[...] = jnp.sum(do_ref[...] * o_ref[...].astype(F32), axis=1, keepdims=True)
        for g in range(n_g):
            q_ref, k_ref, v_ref = qkv[3 * g:3 * g + 3]
            width = _dil_block(g, s)

            def step(i, carry, g=g, q_ref=q_ref, k_ref=k_ref, v_ref=v_ref, width=width):
                q0 = pl.multiple_of(i * DIL_TQ, DIL_TQ)
                start = _dil_keys(g, q0, s)
                win = pl.ds(start, width)
                pieces = _dil_tokens(g, q0, s)
                do_b = _dil_gather(do_ref, pieces).astype(BF16)
                q, k, v = q_ref[pl.ds(q0, DIL_TQ), :], k_ref[win, :], v_ref[win, :]
                sc = _dot(q, k, _NT)
                p = jnp.where(_dil_band(g, q0, start, sc.shape, s), jnp.exp(sc - _dil_gather(l_ref, pieces)), 0.0)
                ds = (p * (_dot(do_b, v, _NT) - _dil_gather(delta_ref, pieces))).astype(BF16)
                d_ref[g, pl.ds(q0, DIL_TQ), :] = _dot(ds, k, _NN)
                d_ref[n_g + g, win, :] += _dot(ds, q, _TN)
                d_ref[2 * n_g + g, win, :] += _dot(p.astype(BF16), do_b, _TN)
                return carry

            lax.fori_loop(0, s // DIL_TQ, step, 0)

    per_head = lambda bi, j: (bi, j, 0, 0)
    return pl.pallas_call(
        body, name=name, grid=(b, DIL_GROUP_HEADS),
        in_specs=[_dil_head_spec(part, g, s) for g in range(n_g) for part in range(3)]
        + [pl.BlockSpec((None, None, s, HEAD_DIM), per_head), pl.BlockSpec((None, None, s, 1), per_head),
           pl.BlockSpec((None, None, s, HEAD_DIM), per_head)],
        out_specs=pl.BlockSpec((None, None, 3 * n_g, s, HEAD_DIM), lambda bi, j: (bi, j, 0, 0, 0)),
        out_shape=jax.ShapeDtypeStruct((b, DIL_GROUP_HEADS, 3 * n_g, s, HEAD_DIM), F32),
        scratch_shapes=[pltpu.VMEM((s, 1), F32)],
        compiler_params=_params(dimension_semantics=("parallel", "parallel")),
    )(*([heads] * (3 * n_g)), out, lse, dout)


NA_BIAS_ROWS = 2 * NA_ROWS - 1
NA_BIAS_COLS = 2 * NA_COLS - 1
NA_BLOCK = 4
NA_SPAN = NA_ROWS + NA_BLOCK - 1
NA_Q = NA_BLOCK * GRID_W
NA_KEYS = NA_SPAN * GRID_W
NA_FORMS = 3


def _na_onehot():
    c = np.arange(GRID_W)[:, None]
    k = np.arange(GRID_W)[None, :]
    lo = np.clip(c - NA_COLS // 2, 0, GRID_W - NA_COLS)
    valid = (k >= lo) & (k < lo + NA_COLS)
    onehot = np.zeros((GRID_W, GRID_W, LANES), np.float32)
    cc, kk = np.nonzero(valid)
    onehot[cc, kk, kk - cc + NA_COLS - 1] = 1.0
    return onehot.reshape(GRID_W * GRID_W, LANES), valid.reshape(1, GRID_W * GRID_W)


def _na_block_rows(n_rows):
    table = np.full((NA_FORMS, NA_BLOCK, NA_SPAN), NA_BIAS_ROWS, np.int64)
    n_blocks = n_rows // NA_BLOCK
    for form, ib in enumerate((0, 1, n_blocks - 1)):
        base = min(max(NA_BLOCK * ib - NA_ROWS // 2, 0), n_rows - NA_SPAN)
        for rl in range(NA_BLOCK):
            r = NA_BLOCK * ib + rl
            row_lo = min(max(r - NA_ROWS // 2, 0), n_rows - NA_ROWS)
            for kl in range(NA_SPAN):
                if row_lo <= base + kl < row_lo + NA_ROWS:
                    table[form, rl, kl] = base + kl - r + NA_ROWS - 1
    return table


def _na_block(ib, n_rows):
    n_blocks = n_rows // NA_BLOCK
    base = jnp.clip(NA_BLOCK * ib - NA_ROWS // 2, 0, n_rows - NA_SPAN)
    return base, jnp.where(ib == 0, 0, jnp.where(ib == n_blocks - 1, 2, 1))


def _na_expand_bias(rel_bias, *, name):
    l, h, nr, nc = rel_bias.shape
    onehot, valid = _na_onehot()
    rb = jnp.pad(rel_bias, ((0, 0), (0, 0), (0, 1), (0, LANES - nc))).reshape(l * h * (nr + 1), LANES)
    live = jnp.asarray(np.tile(np.arange(nr + 1) < nr, l * h).astype(np.float32)[:, None])

    def body(rb_ref, oh_ref, valid_ref, live_ref, e_ref):
        e = lax.dot_general(rb_ref[...], oh_ref[...], _NT, precision=lax.Precision.HIGHEST, preferred_element_type=F32)
        e_ref[...] = jnp.where((valid_ref[...] > 0) & (live_ref[...] > 0), e, NEG_INF)

    e = pl.pallas_call(
        body, name=name, out_shape=jax.ShapeDtypeStruct((l * h * (nr + 1), GRID_W * GRID_W), F32), compiler_params=_params(),
    )(rb, jnp.asarray(onehot), jnp.asarray(valid.astype(np.float32)), live)
    return e.reshape(l, h, nr + 1, GRID_W, GRID_W)


def _na_collapse_bias(de, *, name):
    b, h = de.shape[:2]
    onehot, _ = _na_onehot()
    rows = h * NA_BIAS_ROWS

    def diag(e_ref, oh_ref, o_ref):
        e = e_ref[0]
        for bi in range(1, b):
            e = e + e_ref[bi]
        o_ref[...] = lax.dot_general(e, oh_ref[...], _NN, precision=lax.Precision.HIGHEST, preferred_element_type=F32)

    drb = pl.pallas_call(
        diag, name=name, out_shape=jax.ShapeDtypeStruct((rows, LANES), F32), compiler_params=_params(),
    )(de.reshape(b, rows, GRID_W * GRID_W), jnp.asarray(onehot))
    return drb[:, :NA_BIAS_COLS].reshape(h, NA_BIAS_ROWS, NA_BIAS_COLS)


def _na_tiles(n_rows):
    table = _na_block_rows(n_rows)
    return [(f, rl, kl, int(table[f, rl, kl])) for f in range(NA_FORMS) for rl in range(NA_BLOCK) for kl in range(NA_SPAN)]


def _na_tile(ref, form, rl, kl):
    return ref.at[form, rl * GRID_W:(rl + 1) * GRID_W, kl * GRID_W:(kl + 1) * GRID_W]


def _na_head_spec(part, first, s):
    return pl.BlockSpec((None, None, s, HEAD_DIM), lambda b, h: (b, first + part * NA_HEADS + h, 0, 0))


def _na_attn_fwd(heads, bias, *, first, name):
    b, _, s, _ = heads.shape
    n_rows = s // GRID_W
    tiles = _na_tiles(n_rows)

    def body(q_ref, k_ref, v_ref, e_ref, o_ref, l_ref, b_ref):
        for form, rl, kl, i in tiles:
            _na_tile(b_ref, form, rl, kl)[...] = e_ref[i]

        def step(ib, carry):
            base, form = _na_block(ib, n_rows)
            rows = pl.ds(pl.multiple_of(ib * NA_Q, NA_Q), NA_Q)
            win = pl.ds(pl.multiple_of(base * GRID_W, GRID_W), NA_KEYS)
            sc = _dot(q_ref[rows, :], k_ref[win, :], _NT) + b_ref[form]
            m = jnp.max(sc, axis=1, keepdims=True)
            p = jnp.exp(sc - m)
            den = jnp.sum(p, axis=1, keepdims=True)
            o_ref[rows, :] = (_dot(p.astype(BF16), v_ref[win, :], _NN) / den).astype(o_ref.dtype)
            l_ref[rows, :] = m + jnp.log(den)
            return carry

        lax.fori_loop(0, n_rows // NA_BLOCK, step, 0)

    per_head = lambda bi, h: (bi, h, 0, 0)
    return pl.pallas_call(
        body, name=name, grid=(b, NA_HEADS),
        in_specs=[_na_head_spec(part, first, s) for part in range(3)]
        + [pl.BlockSpec((None, NA_BIAS_ROWS + 1, GRID_W, GRID_W), lambda bi, h: (h, 0, 0, 0))],
        out_specs=[pl.BlockSpec((None, None, s, HEAD_DIM), per_head), pl.BlockSpec((None, None, s, 1), per_head)],
        out_shape=[jax.ShapeDtypeStruct((b, NA_HEADS, s, HEAD_DIM), BF16), jax.ShapeDtypeStruct((b, NA_HEADS, s, 1), F32)],
        scratch_shapes=[pltpu.VMEM((NA_FORMS, NA_Q, NA_KEYS), F32)],
        compiler_params=_params(dimension_semantics=("parallel", "parallel")),
    )(heads, heads, heads, bias)


def _na_attn_bwd(heads, bias, out, lse, dout, *, first, name):
    b, _, s, _ = heads.shape
    n_rows = s // GRID_W
    tiles = _na_tiles(n_rows)

    def body(q_ref, k_ref, v_ref, e_ref, o_ref, l_ref, do_ref, d_ref, de_ref, b_ref, db_ref):
        for form, rl, kl, i in tiles:
            _na_tile(b_ref, form, rl, kl)[...] = e_ref[i]
        d_ref[...] = jnp.zeros_like(d_ref)
        db_ref[...] = jnp.zeros_like(db_ref)

        def step(ib, carry):
            base, form = _na_block(ib, n_rows)
            rows = pl.ds(pl.multiple_of(ib * NA_Q, NA_Q), NA_Q)
            win = pl.ds(pl.multiple_of(base * GRID_W, GRID_W), NA_KEYS)
            q, k, v = q_ref[rows, :], k_ref[win, :], v_ref[win, :]
            do = do_ref[rows, :]
            delta = jnp.sum(do * o_ref[rows, :].astype(F32), axis=1, keepdims=True)
            do_b = do.astype(BF16)
            p = jnp.exp(_dot(q, k, _NT) + b_ref[form] - l_ref[rows, :])
            ds = p * (_dot(do_b, v, _NT) - delta)
            db_ref[form] += ds
            ds_b = ds.astype(BF16)
            d_ref[0, rows, :] = _dot(ds_b, k, _NN)
            d_ref[1, win, :] += _dot(ds_b, q, _TN)
            d_ref[2, win, :] += _dot(p.astype(BF16), do_b, _TN)
            return carry

        lax.fori_loop(0, n_rows // NA_BLOCK, step, 0)
        acc = [None] * NA_BIAS_ROWS
        for form, rl, kl, i in tiles:
            if i < NA_BIAS_ROWS:
                t = _na_tile(db_ref, form, rl, kl)[...]
                acc[i] = t if acc[i] is None else acc[i] + t
        for i in range(NA_BIAS_ROWS):
            de_ref[i] = acc[i]

    per_head = lambda bi, h: (bi, h, 0, 0)
    return pl.pallas_call(
        body, name=name, grid=(b, NA_HEADS),
        in_specs=[_na_head_spec(part, first, s) for part in range(3)]
        + [pl.BlockSpec((None, NA_BIAS_ROWS + 1, GRID_W, GRID_W), lambda bi, h: (h, 0, 0, 0)),
           pl.BlockSpec((None, None, s, HEAD_DIM), per_head), pl.BlockSpec((None, None, s, 1), per_head),
           pl.BlockSpec((None, None, s, HEAD_DIM), per_head)],
        out_specs=[pl.BlockSpec((None, None, 3, s, HEAD_DIM), lambda bi, h: (bi, h, 0, 0, 0)),
                   pl.BlockSpec((None, None, NA_BIAS_ROWS, GRID_W, GRID_W), lambda bi, h: (bi, h, 0, 0, 0))],
        out_shape=[jax.ShapeDtypeStruct((b, NA_HEADS, 3, s, HEAD_DIM), F32),
                   jax.ShapeDtypeStruct((b, NA_HEADS, NA_BIAS_ROWS, GRID_W, GRID_W), F32)],
        scratch_shapes=[pltpu.VMEM((NA_FORMS, NA_Q, NA_KEYS), F32), pltpu.VMEM((NA_FORMS, NA_Q, NA_KEYS), F32)],
        compiler_params=_params(dimension_semantics=("parallel", "parallel")),
    )(heads, heads, heads, bias, out, lse, dout)


GATE_TILE = 256


def _gate_fwd(proj, z, *, gate_col, tt, name):
    _, t, d = z.shape
    nj = d // GATE_TILE
    c0 = gate_col // GATE_TILE

    def body(ga_ref, gb_ref, za_ref, zb_ref, o_ref):
        o_ref[...] = (jax.nn.sigmoid(ga_ref[...]) * za_ref[...] + jax.nn.sigmoid(gb_ref[...]) * zb_ref[...]).astype(BF16)

    return pl.pallas_call(
        body, name=name, grid=(t // tt, nj),
        in_specs=[pl.BlockSpec((tt, GATE_TILE), lambda i, j: (i, c0 + j)),
                  pl.BlockSpec((tt, GATE_TILE), lambda i, j: (i, c0 + nj + j)),
                  pl.BlockSpec((None, tt, GATE_TILE), lambda i, j: (0, i, j)),
                  pl.BlockSpec((None, tt, GATE_TILE), lambda i, j: (1, i, j))],
        out_specs=pl.BlockSpec((tt, GATE_TILE), lambda i, j: (i, j)), out_shape=jax.ShapeDtypeStruct((t, d), BF16),
        compiler_params=_params(dimension_semantics=("parallel", "parallel")),
    )(proj, proj, z, z)


def _gate_bwd(dm, proj, z, *, gate_col, tt, name):
    _, t, d = z.shape
    nj = d // GATE_TILE
    c0 = gate_col // GATE_TILE

    def body(dm_ref, g_ref, z_ref, dz_ref, dg_ref):
        dmv = dm_ref[...]
        sg = jax.nn.sigmoid(g_ref[...])
        dz_ref[...] = (dmv * sg).astype(BF16)
        dg_ref[...] = (dmv * z_ref[...] * sg * (1.0 - sg)).astype(BF16)

    return pl.pallas_call(
        body, name=name, grid=(t // tt, 2 * nj),
        in_specs=[pl.BlockSpec((tt, GATE_TILE), lambda i, j: (i, j % nj)),
                  pl.BlockSpec((tt, GATE_TILE), lambda i, j: (i, c0 + j)),
                  pl.BlockSpec((None, tt, GATE_TILE), lambda i, j: (j // nj, i, j % nj))],
        out_specs=[pl.BlockSpec((None, tt, GATE_TILE), lambda i, j: (j // nj, i, j % nj)),
                   pl.BlockSpec((tt, GATE_TILE), lambda i, j: (i, c0 + j))],
        out_shape=[jax.ShapeDtypeStruct((2, t, d), BF16), jax.ShapeDtypeStruct(proj.shape, BF16)],
        compiler_params=_params(dimension_semantics=("parallel", "parallel")),
    )(dm, proj, z)


def _adamw(w, g, m, v, *, name):
    shape = w.shape
    w2, g2, m2, v2 = (t.reshape(-1, shape[-1]) for t in (w, g, m, v))
    rows, cols = w2.shape
    tr = rows
    for cand in (512, 256, 128, 64, 32, 16, 8):
        if rows % cand == 0:
            tr = cand
            break

    def body(w_ref, g_ref, m_ref, v_ref, d_ref, nm_ref, nv_ref):
        gv = g_ref[...]
        nm = ADAM_B1 * m_ref[...] + (1.0 - ADAM_B1) * gv
        nv = ADAM_B2 * v_ref[...] + (1.0 - ADAM_B2) * (gv * gv)
        m_hat = nm / (1.0 - ADAM_B1 ** ADAM_STEP)
        v_hat = nv / (1.0 - ADAM_B2 ** ADAM_STEP)
        d_ref[...] = -ADAM_LR * (m_hat / (jnp.sqrt(v_hat) + ADAM_EPS) + ADAM_WD * w_ref[...])
        nm_ref[...] = nm
        nv_ref[...] = nv

    blk = pl.BlockSpec((tr, cols), lambda i: (i, 0))
    out = jax.ShapeDtypeStruct((rows, cols), F32)
    res = pl.pallas_call(
        body, name=name, grid=(rows // tr,), in_specs=[blk] * 4, out_specs=[blk] * 3, out_shape=[out] * 3,
        compiler_params=_params(dimension_semantics=("parallel",)),
    )(w2, g2, m2, v2)
    return tuple(t.reshape(shape) for t in res)


def _my_place():
    return lax.axis_index("x"), lax.axis_index("y"), lax.axis_index("c")


def _other_chips(x, y):
    return [(1 - x, y), (x, 1 - y), (1 - x, 1 - y)]


def _chip_no(chip):
    return 2 * chip[0] + chip[1]


def _window(ref, kind, size, chip, lead):
    if kind == "col":
        return ref.at[(*lead, slice(None), pl.ds(pl.multiple_of(chip * size, LANES), size))]
    if kind == "row":
        return ref.at[(*lead, pl.ds(pl.multiple_of(chip * size, BF16_ROWS), size), slice(None))]
    shard = size + HEAD_DIM
    if kind == "win_main":
        return ref.at[(*lead, slice(None), pl.ds(pl.multiple_of(chip * shard + HEAD_DIM * (chip % 2), LANES), size))]
    assert kind == "win_strad"
    return ref.at[(*lead, slice(None), pl.ds(pl.multiple_of(size + 2 * shard * (chip // 2), LANES), LANES))]


def _full_shape(shard, kind):
    _, k, n = shard.shape
    return {"col": (k, N_CHIPS * n), "row": (N_CHIPS * k, n), "win_main": (k, N_CHIPS * (n + HEAD_DIM)),
            "slot": (N_CHIPS, k, n)}[kind]


def _place_own(shard, kind, layer, *, name):
    _, k, n = shard.shape
    tr = _div_tile(k, 512, BF16_ROWS)
    tc = LANES if kind == "win_main" else n
    mine = 2 * lax.axis_index("x") + lax.axis_index("y")
    row0 = mine * (k // tr) if kind == "row" else 0
    col0 = {"col": mine, "row": 0, "slot": 0, "win_main": (mine * (n + HEAD_DIM) + HEAD_DIM * (mine % 2)) // LANES}[kind]
    scalars = jnp.stack([mine, row0, col0]).astype(jnp.int32)

    def body(s_ref, i_ref, o_ref):
        o_ref[...] = i_ref[...]

    if kind == "slot":
        o_spec = pl.BlockSpec((None, tr, tc), lambda i, j, s: (s[0], i, j))
    else:
        o_spec = pl.BlockSpec((tr, tc), lambda i, j, s: (s[1] + i, s[2] + j))
    return pl.pallas_call(
        body, name=name,
        grid_spec=pltpu.PrefetchScalarGridSpec(
            num_scalar_prefetch=1, grid=(k // tr, n // tc),
            in_specs=[pl.BlockSpec((None, tr, tc), lambda i, j, s: (layer, i, j))], out_specs=o_spec),
        out_shape=jax.ShapeDtypeStruct(_full_shape(shard, kind), shard.dtype),
        compiler_params=_params(dimension_semantics=("parallel", "parallel")),
    )(scalars, shard)


class _GatherPlan:
    def __init__(self, src, dst, shapes, kinds, layer, send_sems, recv_sems):
        self.src, self.dst, self.shapes, self.kinds, self.layer = src, dst, shapes, kinds, layer
        self.send_sems, self.recv_sems = send_sems, recv_sems
        self.x, self.y, self.c = _my_place()
        self.mine = 2 * self.x + self.y
        self.chips = _other_chips(self.x, self.y)
        self.n = len(src)

    def half(self, i, chip, half):
        _, k, n = self.shapes[i]
        kind, dst, hk = self.kinds[i], self.dst[i], k // 2
        if kind == "slot":
            return dst.at[chip, pl.ds(pl.multiple_of(half * hk, BF16_ROWS), hk), :]
        if kind == "row":
            return dst.at[pl.ds(pl.multiple_of(chip * k + half * hk, BF16_ROWS), hk), :]
        col0 = chip * n if kind == "col" else chip * (n + HEAD_DIM) + HEAD_DIM * (chip % 2)
        return dst.at[pl.ds(pl.multiple_of(half * hk, BF16_ROWS), hk), pl.ds(pl.multiple_of(col0, LANES), n)]

    def _copy(self, sem, window, to, source=None):
        return pltpu.make_async_remote_copy(src_ref=window if source is None else source, dst_ref=window,
                                            send_sem=self.send_sems.at[sem], recv_sem=self.recv_sems.at[sem],
                                            device_id=to, device_id_type=MESH)

    def sends(self):
        out = []
        for k, chip in enumerate(self.chips):
            for i in range(self.n):
                hk = self.shapes[i][1] // 2
                mine = self.src[i].at[self.layer, pl.ds(pl.multiple_of(self.c * hk, BF16_ROWS), hk), :]
                out.append(self._copy(3 * i + k, self.half(i, self.mine, self.c), (*chip, self.c), source=mine))
        return out

    def arrivals(self):
        return [self._copy(3 * i + k, self.half(i, _chip_no(chip), self.c), (*chip, self.c))
                for k, chip in enumerate(self.chips) for i in range(self.n)]

    def forwards(self, first_sem):
        sibling = (self.x, self.y, 1 - self.c)
        return [self._copy(first_sem + 3 * i + k, self.half(i, _chip_no(chip), self.c), sibling)
                for k, chip in enumerate(self.chips) for i in range(self.n)]

    def forwarded(self, first_sem):
        sibling = (self.x, self.y, 1 - self.c)
        return [self._copy(first_sem + 3 * i + k, self.half(i, _chip_no(chip), 1 - self.c), sibling)
                for k, chip in enumerate(self.chips) for i in range(self.n)]


def _gather_layer(shards, kinds, fulls, layer, *, name):
    n_w = len(shards)
    shapes = [sh.shape for sh in shards]

    def body(*refs):
        plan = _GatherPlan(refs[:n_w], refs[2 * n_w:3 * n_w], shapes, kinds, layer, *refs[3 * n_w:])
        sends = plan.sends()
        for cp in sends:
            cp.start()
        passed = plan.forwards(3 * n_w)
        for landed, onward in zip(plan.arrivals(), passed):
            landed.wait_recv()
            onward.start()
        for cp in plan.forwarded(3 * n_w):
            cp.wait_recv()
        for cp in sends + passed:
            cp.wait_send()

    return pl.pallas_call(
        body, name=name, in_specs=[HBM] * (2 * n_w), out_specs=[HBM] * n_w,
        out_shape=[jax.ShapeDtypeStruct(f.shape, f.dtype) for f in fulls],
        input_output_aliases={n_w + i: i for i in range(n_w)},
        scratch_shapes=[pltpu.SemaphoreType.DMA((6 * n_w,)), pltpu.SemaphoreType.DMA((6 * n_w,))],
    )(*shards, *fulls)


IN_HBM = pl.BlockSpec(memory_space=pltpu.HBM)
IN_SEM = pl.BlockSpec(memory_space=pltpu.SEMAPHORE)
DATAFLOW = pltpu.SideEffectType.DATAFLOW_SIDE_EFFECTING


def _gather_layer_start(shards, kinds, fulls, layer, *, name):
    n_w = len(shards)
    shapes = [sh.shape for sh in shards]

    def body(*refs):
        plan = _GatherPlan(refs[:n_w], refs[n_w:2 * n_w], shapes, kinds, layer, refs[2 * n_w], refs[2 * n_w + 1])
        for cp in plan.sends():
            cp.start()
        token = refs[-1]
        token[...] = jnp.zeros_like(token)

    operands = [pltpu.with_memory_space_constraint(a, pltpu.HBM) for a in (*shards, *fulls)]
    res = pl.pallas_call(
        body, name=name, in_specs=[IN_HBM] * (2 * n_w),
        out_specs=(IN_SEM, IN_SEM, *([IN_HBM] * (2 * n_w)), pl.BlockSpec(memory_space=pltpu.VMEM)),
        out_shape=(pltpu.SemaphoreType.DMA((3 * n_w,)), pltpu.SemaphoreType.DMA((3 * n_w,)),
                   *[pltpu.HBM(a.shape, a.dtype) for a in operands], jax.ShapeDtypeStruct((8, LANES), F32)),
        input_output_aliases={i: 2 + i for i in range(2 * n_w)},
        compiler_params=pltpu.CompilerParams(has_side_effects=DATAFLOW),
    )(*operands)
    return res[0], res[1], res[2:2 + n_w], res[2 + n_w:2 + 2 * n_w], res[-1]


def _gather_layer_wait(send_sems, recv_sems, shards, fulls, kinds, layer, after, *, name):
    n_w = len(shards)
    shapes = [sh.shape for sh in shards]

    def body(*refs):
        plan = _GatherPlan(refs[:n_w], refs[n_w:2 * n_w], shapes, kinds, layer, refs[2 * n_w], refs[2 * n_w + 1])
        for cp in plan.sends():
            cp.wait_send()
        for cp in plan.arrivals():
            cp.wait_recv()

    res = pl.pallas_call(
        body, name=name, in_specs=[IN_HBM] * (2 * n_w) + [IN_SEM, IN_SEM, pl.BlockSpec(memory_space=pl.ANY)],
        out_specs=[IN_HBM] * (2 * n_w), out_shape=[pltpu.HBM(a.shape, a.dtype) for a in (*shards, *fulls)],
        input_output_aliases={i: i for i in range(2 * n_w)},
        compiler_params=pltpu.CompilerParams(has_side_effects=DATAFLOW),
    )(*shards, *fulls, send_sems, recv_sems, after)
    return res[n_w:]


def _gather_layer_forward(shapes, kinds, fulls, *, name):
    n_w = len(fulls)

    def body(*refs):
        plan = _GatherPlan([None] * n_w, refs[n_w:2 * n_w], shapes, kinds, 0, *refs[2 * n_w:])
        passed = plan.forwards(0)
        for cp in passed:
            cp.start()
        for cp in plan.forwarded(0):
            cp.wait_recv()
        for cp in passed:
            cp.wait_send()

    return pl.pallas_call(
        body, name=name, in_specs=[HBM] * n_w, out_specs=[HBM] * n_w,
        out_shape=[jax.ShapeDtypeStruct(f.shape, f.dtype) for f in fulls],
        input_output_aliases={i: i for i in range(n_w)},
        scratch_shapes=[pltpu.SemaphoreType.DMA((3 * n_w,)), pltpu.SemaphoreType.DMA((3 * n_w,))],
    )(*fulls)


def _grads_to_sibling(grads, *, name):
    n_w = len(grads)

    def body(*refs):
        src, dst = refs[:n_w], refs[n_w:2 * n_w]
        send_sems, recv_sems = refs[2 * n_w:]
        x, y, c = _my_place()
        cps = [pltpu.make_async_remote_copy(src_ref=src[i].at[1 - c], dst_ref=dst[i], send_sem=send_sems.at[i],
                                            recv_sem=recv_sems.at[i], device_id=(x, y, 1 - c), device_id_type=MESH)
               for i in range(n_w)]
        for cp in cps:
            cp.start()
        for cp in cps:
            cp.wait()

    return pl.pallas_call(
        body, name=name, in_specs=[HBM] * n_w, out_specs=[HBM] * n_w,
        out_shape=[jax.ShapeDtypeStruct(g.shape[1:], g.dtype) for g in grads],
        scratch_shapes=[pltpu.SemaphoreType.DMA((n_w,)), pltpu.SemaphoreType.DMA((n_w,))],
    )(*grads)


def _pair_add(mine2, other, *, name):
    _, k, n = mine2.shape
    tr = _div_tile(k, 512, BF16_ROWS)
    c = lax.axis_index("c").astype(jnp.int32).reshape(1)

    def body(c_ref, a_ref, b_ref, o_ref):
        o_ref[...] = (a_ref[...].astype(F32) + b_ref[...].astype(F32)).astype(o_ref.dtype)

    return pl.pallas_call(
        body, name=name,
        grid_spec=pltpu.PrefetchScalarGridSpec(
            num_scalar_prefetch=1, grid=(k // tr,),
            in_specs=[pl.BlockSpec((None, tr, n), lambda i, c_ref: (c_ref[0], i, 0)),
                      pl.BlockSpec((tr, n), lambda i, c_ref: (i, 0))],
            out_specs=pl.BlockSpec((tr, n), lambda i, c_ref: (i, 0))),
        out_shape=jax.ShapeDtypeStruct((k, n), mine2.dtype),
        compiler_params=_params(dimension_semantics=("parallel",)),
    )(c, mine2, other)


def _grads_to_chips(pairs, kinds, sizes, *, name):
    n_w = len(pairs)

    def shard_shape(p, kind, size):
        return {"col": (p.shape[0], size), "row": (size, p.shape[1]), "win_main": (p.shape[0], size),
                "win_strad": (p.shape[0], LANES)}[kind]

    def body(*refs):
        src, dst = refs[:n_w], refs[n_w:2 * n_w]
        send_sems, recv_sems = refs[2 * n_w:]
        x, y, c = _my_place()
        mine = 2 * x + y
        chips = _other_chips(x, y)

        def copy(i, k, chip, window_of, slab):
            return pltpu.make_async_remote_copy(src_ref=_window(src[i], kinds[i], sizes[i], window_of, ()),
                                                dst_ref=dst[i].at[slab], send_sem=send_sems.at[3 * i + k],
                                                recv_sem=recv_sems.at[3 * i + k], device_id=(*chip, c), device_id_type=MESH)

        sends = [copy(i, k, chip, _chip_no(chip), mine) for k, chip in enumerate(chips) for i in range(n_w)]
        for cp in sends:
            cp.start()
        for k, chip in enumerate(chips):
            for i in range(n_w):
                copy(i, k, chip, mine, _chip_no(chip)).wait_recv()
        for cp in sends:
            cp.wait_send()

    return pl.pallas_call(
        body, name=name, in_specs=[HBM] * n_w, out_specs=[HBM] * n_w,
        out_shape=[jax.ShapeDtypeStruct((N_CHIPS,) + shard_shape(p, kind, size), p.dtype)
                   for p, kind, size in zip(pairs, kinds, sizes)],
        scratch_shapes=[pltpu.SemaphoreType.DMA((3 * n_w,)), pltpu.SemaphoreType.DMA((3 * n_w,))],
    )(*pairs)


def _sum_slabs(slabs, pair, kind, size, *, name):
    n_s, k, n = slabs.shape
    tr = _div_tile(k, 512, BF16_ROWS)
    tc = n if kind in ("col", "row") else LANES
    x, y, c = _my_place()
    mine = 2 * x + y
    shard = size + HEAD_DIM
    row0 = mine * (k // tr) if kind == "row" else 0
    col0 = {"col": mine, "row": 0, "win_main": (mine * shard + HEAD_DIM * (mine % 2)) // LANES,
            "win_strad": (size + 2 * shard * (mine // 2)) // LANES}[kind]
    scalars = jnp.stack([c, mine, row0, col0]).astype(jnp.int32)

    def body(s_ref, slab_ref, own_ref, o_ref):
        me = s_ref[1]
        acc = jnp.zeros(o_ref.shape, F32)
        for i in range(n_s):
            acc = acc + jnp.where(me == i, own_ref[...], slab_ref[i]).astype(F32)
        o_ref[...] = acc

    return pl.pallas_call(
        body, name=name,
        grid_spec=pltpu.PrefetchScalarGridSpec(
            num_scalar_prefetch=1, grid=(k // tr, n // tc),
            in_specs=[pl.BlockSpec((n_s, tr, tc), lambda i, j, s: (0, i, j)),
                      pl.BlockSpec((tr, tc), lambda i, j, s: (s[2] + i, s[3] + j))],
            out_specs=pl.BlockSpec((None, tr, tc), lambda i, j, s: (s[0], i, j))),
        out_shape=jax.ShapeDtypeStruct((2, k, n), F32),
        compiler_params=_params(dimension_semantics=("parallel", "parallel")),
    )(scalars, slabs, pair)


def _exchange_layers(bufs, *, name):
    n_w = len(bufs)

    def body(*refs):
        dst = refs[n_w:2 * n_w]
        send_sems, recv_sems = refs[2 * n_w:]
        x, y, c = _my_place()

        def copy(i, layer):
            return pltpu.make_async_remote_copy(src_ref=dst[i].at[layer], dst_ref=dst[i].at[layer], send_sem=send_sems.at[i],
                                                recv_sem=recv_sems.at[i], device_id=(x, y, 1 - c), device_id_type=MESH)

        sends = [copy(i, c) for i in range(n_w)]
        for cp in sends:
            cp.start()
        for i in range(n_w):
            copy(i, 1 - c).wait_recv()
        for cp in sends:
            cp.wait_send()

    return pl.pallas_call(
        body, name=name, in_specs=[HBM] * n_w, out_specs=[HBM] * n_w,
        out_shape=[jax.ShapeDtypeStruct(b.shape, b.dtype) for b in bufs],
        input_output_aliases={i: i for i in range(n_w)},
        scratch_shapes=[pltpu.SemaphoreType.DMA((n_w,)), pltpu.SemaphoreType.DMA((n_w,))],
    )(*bufs)


def _all_sum_small(v, *, name):
    r = v.shape[0]
    relations = [(dx, dy, dc) for dx in (0, 1) for dy in (0, 1) for dc in (0, 1)][1:]

    def body(v_ref, o_ref, buf, send_sems, recv_sems):
        x, y, c = _my_place()
        me = 4 * x + 2 * y + c
        buf[me] = v_ref[...]
        peers = [(x + dx - 2 * x * dx, y + dy - 2 * y * dy, c + dc - 2 * c * dc) for dx, dy, dc in relations]

        def copy(k, slot):
            return pltpu.make_async_remote_copy(src_ref=v_ref, dst_ref=buf.at[slot], send_sem=send_sems.at[k],
                                                recv_sem=recv_sems.at[k], device_id=peers[k], device_id_type=MESH)

        sends = [copy(k, me) for k in range(len(relations))]
        for cp in sends:
            cp.start()
        for k, (px, py, pc) in enumerate(peers):
            copy(k, 4 * px + 2 * py + pc).wait_recv()
        for cp in sends:
            cp.wait_send()
        acc = buf[0]
        for i in range(1, 8):
            acc = acc + buf[i]
        o_ref[...] = acc

    vm = pl.BlockSpec(memory_space=pltpu.VMEM)
    return pl.pallas_call(
        body, name=name, in_specs=[vm], out_specs=vm, out_shape=jax.ShapeDtypeStruct((r, LANES), F32),
        scratch_shapes=[pltpu.VMEM((8, r, LANES), F32), pltpu.SemaphoreType.DMA((7,)), pltpu.SemaphoreType.DMA((7,))],
    )(v)


SHARDED = (("ffn1_w_up", "col"), ("ffn1_w_down", "row"), ("w_in", "win"), ("w_branch_a", "col"),
           ("w_branch_b", "col"), ("w_out", "row"), ("ffn2_w_up", "col"), ("ffn2_w_down", "row"))
REPLICATED = ("ffn1_norm", "mix_norm", "na_rel_bias", "ffn2_norm", "final_norm")


def _weight_pieces(w):
    even = lax.axis_index("y") == 0
    shards, kinds, names = [], [], []
    for name, kind in SHARDED:
        wb = w[name].astype(BF16)
        if kind == "win":
            main = wb.shape[-1] - HEAD_DIM
            assert main % LANES == 0
            zeros = jnp.zeros(wb.shape[:-1] + (HEAD_DIM,), BF16)
            shards += [jnp.where(even, wb[..., :main], wb[..., HEAD_DIM:]),
                       jnp.where(even, jnp.concatenate([wb[..., main:], zeros], -1),
                                 jnp.concatenate([zeros, wb[..., :HEAD_DIM]], -1))]
            kinds += ["win_main", "slot"]
            names += [name, name + "_strad"]
        else:
            shards.append(wb)
            kinds.append(kind)
            names.append(name)
    return names, kinds, shards


def _finish_w_in(full):
    full = dict(full)
    strad = full.pop("w_in_strad")
    main = full["w_in"].shape[1] // N_CHIPS - HEAD_DIM
    for i in range(N_CHIPS // 2):
        lo = main + 2 * (main + HEAD_DIM) * i
        full["w_in"] = full["w_in"].at[:, lo:lo + LANES].set(strad[2 * i] + strad[2 * i + 1])
    return full


def _reduce_weight_grads(grads, shards):
    names, kinds, sizes, srcs = [], [], [], []
    for name, kind in SHARDED:
        shp = shards[name].shape
        if kind == "win":
            names += [name, name + "_strad"]
            kinds += ["win_main", "win_strad"]
            sizes += [shp[2] - HEAD_DIM] * 2
            srcs += [name, name]
        else:
            names.append(name)
            kinds.append(kind)
            sizes.append(shp[1] if kind == "row" else shp[2])
            srcs.append(name)
    uniq = [name for name, _ in SHARDED]
    arrived = dict(zip(uniq, _grads_to_sibling([grads[n] for n in uniq], name="grads_to_sibling")))
    pair = {n: _pair_add(grads[n], arrived[n], name=f"grads_pair_{n}") for n in uniq}
    slabs = _grads_to_chips([pair[s] for s in srcs], kinds, sizes, name="grads_to_chips")
    halves = [_sum_slabs(sl, pair[s], kind, size, name=f"grads_sum_{n}")
              for n, sl, s, kind, size in zip(names, slabs, srcs, kinds, sizes)]
    out = dict(zip(names, _exchange_layers(halves, name="grads_layers")))
    strad = out.pop("w_in_strad")
    even = lax.axis_index("y") == 0
    out["w_in"] = jnp.where(even, jnp.concatenate([out["w_in"], strad[..., :HEAD_DIM]], -1),
                            jnp.concatenate([strad[..., HEAD_DIM:], out["w_in"]], -1))
    return out


class _Grads:
    def __init__(self, depth):
        self.depth = depth
        self.arrays = {}

    def put(self, weight, layer, a, b, *, cols=None, col_off=0, **kw):
        self.arrays[weight] = _mm(a, b, mode="tn", out_dtype=BF16, out_slab=(layer, self.depth), out_cols=cols,
                                  out_col_off=col_off, out_into=self.arrays.get(weight), **kw)


def _ffn_fwd(x, norm_g, w_up, w_down, tag):
    t, d = x.shape
    f = w_down.shape[0]
    h = _rms_fwd(x, norm_g, tt=512, name=f"{tag}_norm")
    a, gate, up = _mm_swiglu_fwd(h, w_up, tm=_div_tile(t, ROWS_NARROW, 8), tn=MXU_N, name=f"{tag}_up")
    x_out = _mm(a, w_down, mode="nn", out_dtype=F32, tm=_div_tile(t, ROWS_WIDE, 8), tn=d, tk=f, alpha=0.5, res=x, name=f"{tag}_down")
    return x_out, (x, h, a, gate, up)


def _ffn_bwd(dx, dxb, saved, norm_g, w_up, w_down, layer, grads, wname, tag):
    x, h, a, gate, up = saved
    t, d = x.shape
    f = w_down.shape[0]
    tn = _div_tile(f, 1408)
    grads.put(f"{wname}_w_down", layer, a, dxb, tm=tn, tn=d, tk=1024, alpha=0.5, name=f"{tag}_dwd")
    d_gate, d_up = _mm_swiglu_bwd(dxb, w_down, gate, up, alpha=0.5, tm=_div_tile(t, ROWS_NARROW, 8), tn=MXU_N, name=f"{tag}_da")
    grads.put(f"{wname}_w_up", layer, h, d_gate, cols=2 * f, tm=d, tn=tn, tk=1024, name=f"{tag}_dwg")
    grads.put(f"{wname}_w_up", layer, h, d_up, cols=2 * f, col_off=f // tn, tm=d, tn=tn, tk=1024, name=f"{tag}_dwu")
    dh = _mm(d_gate, w_up, mode="nt", out_dtype=F32, tm=_div_tile(t, ROWS_WIDE, 8), tn=d, tk=f, name=f"{tag}_dh1")
    dh = _mm(d_up, w_up, mode="nt", out_dtype=F32, tm=_div_tile(t, ROWS_WIDE, 8), tn=d, tk=f, b_k_off=1, res=dh, name=f"{tag}_dh2")
    return _rms_bwd(dh, x, norm_g, dx, tt=512, name=f"{tag}_dnorm")


def _to_heads(y, b, n_heads):
    t, w = y.shape
    return y.reshape(b, t // b, n_heads, HEAD_DIM).transpose(0, 2, 1, 3)


def _from_heads(y):
    b, n, s, hd = y.shape
    return y.transpose(0, 2, 1, 3).reshape(b * s, n * hd)


N_QKV = 3 * (DIL_HEADS + NA_HEADS) * HEAD_DIM


def _mixer_fwd(x, b, norm_g, full, bias, tabs, tag):
    t, d = x.shape
    s = t // b
    n_in = full["w_in"].shape[1]
    h = _rms_fwd(x, norm_g, tt=512, name=f"{tag}_norm")
    proj = _mm(h, full["w_in"], mode="nn", out_dtype=F32, tm=_div_tile(t, ROWS_NARROW, 8), tn=MXU_N, tk=d, name=f"{tag}_in")
    heads = _split_heads(proj.reshape(b, s, -1), *tabs, n_pairs=N_QKV // LANES, rot_pairs=DIL_HEADS,
                         scale_ranges=((0, DIL_HEADS // 2), (3 * DIL_HEADS // 2, (3 * DIL_HEADS + NA_HEADS) // 2)),
                         name=f"{tag}_heads")
    ya, lse_a = _dil_attn_fwd(heads, name=f"{tag}_dil")
    yb, lse_b = _na_attn_fwd(heads, bias, first=3 * DIL_HEADS, name=f"{tag}_na")
    ya2, yb2 = _from_heads(ya), _from_heads(yb)
    z = _mm(ya2, full["w_branch_a"], mode="nn", out_dtype=F32, tm=_div_tile(t, ROWS_NARROW, 8), tn=MXU_N, tk=ya2.shape[1],
            out_slab=(0, 2), name=f"{tag}_za")
    z = _mm(yb2, full["w_branch_b"], mode="nn", out_dtype=F32, tm=_div_tile(t, ROWS_NARROW, 8), tn=MXU_N, tk=yb2.shape[1],
            out_slab=(1, 2), out_into=z, name=f"{tag}_zb")
    merged = _gate_fwd(proj, z, gate_col=N_QKV, tt=1024, name=f"{tag}_gate")
    x_out = _mm(merged, full["w_out"], mode="nn", out_dtype=F32, tm=_div_tile(t, ROWS_NARROW, 8), tn=MXU_N, tk=d, res=x, name=f"{tag}_out")
    return x_out, (x, h, proj, heads, ya, lse_a, yb, lse_b, ya2, yb2, z, merged)


def _mixer_bwd(dx, dob, b, saved, norm_g, full, layer, bias, tabs, grads, tag):
    x, h, proj, heads, ya, lse_a, yb, lse_b, ya2, yb2, z, merged = saved
    t, d = x.shape
    s = t // b
    n_in = full["w_in"].shape[1]
    grads.put("w_out", layer, merged, dob, tm=d, tn=d, tk=1024, name=f"{tag}_dwo")
    dm = _mm(dob, full["w_out"], mode="nt", out_dtype=F32, tm=_div_tile(t, ROWS_NARROW, 8), tn=MXU_N, tk=d, name=f"{tag}_dm")
    dz, dproj = _gate_bwd(dm, proj, z, gate_col=N_QKV, tt=1024, name=f"{tag}_dgate")
    grads.put("w_branch_a", layer, ya2, dz, b_sel=0, tm=ya2.shape[1], tn=d, tk=1024, name=f"{tag}_dwa")
    grads.put("w_branch_b", layer, yb2, dz, b_sel=1, tm=yb2.shape[1], tn=d, tk=1024, name=f"{tag}_dwb")
    dya = _mm(dz, full["w_branch_a"], mode="nt", out_dtype=F32, tm=_div_tile(t, ROWS_NARROW, 8), tn=MXU_N, tk=d, a_sel=0, name=f"{tag}_dya")
    dyb = _mm(dz, full["w_branch_b"], mode="nt", out_dtype=F32, tm=_div_tile(t, ROWS_NARROW, 8), tn=MXU_N, tk=d, a_sel=1, name=f"{tag}_dyb")
    d_dil = _dil_attn_bwd(heads, ya, lse_a, _to_heads(dya, b, DIL_GROUP_HEADS), name=f"{tag}_ddil")
    d_na, d_bias = _na_attn_bwd(heads, bias, yb, lse_b, _to_heads(dyb, b, NA_HEADS), first=3 * DIL_HEADS, name=f"{tag}_dna")
    dproj = _merge_heads(d_dil, *tabs, heads_per_row=DIL_GROUP_HEADS, rot_pairs=DIL_HEADS, scale_pairs=DIL_HEADS // 2,
                         dilated=True, out_cols=n_in, tile_off=0, into=dproj.reshape(b, s, n_in), name=f"{tag}_dheads_a")
    dproj = _merge_heads(d_na, *tabs, heads_per_row=NA_HEADS, rot_pairs=0, scale_pairs=NA_HEADS // 2, dilated=False,
                         out_cols=n_in, tile_off=3 * DIL_HEADS // 2, into=dproj, name=f"{tag}_dheads_b").reshape(t, n_in)
    grads.put("w_in", layer, h, dproj, tm=_div_tile(d, 512), tn=_div_tile(n_in, 2944), tk=1024, name=f"{tag}_dwin")
    dh = _mm(dproj, full["w_in"], mode="nt", out_dtype=F32, tm=_div_tile(t, ROWS_WIDE, 8), tn=d, tk=_div_tile(n_in, 2944), name=f"{tag}_dh")
    dx_in, dxb_in, d_norm = _rms_bwd(dh, x, norm_g, dx, tt=512, name=f"{tag}_dnorm")
    d_rb = _na_collapse_bias(d_bias, name=f"{tag}_dbias")
    return dx_in, dxb_in, d_norm, d_rb


def kernel(x, ffn1_norm, ffn1_w_up, ffn1_w_down, mix_norm, w_in, na_rel_bias, w_branch_a, w_branch_b, w_out, ffn2_norm, ffn2_w_up, ffn2_w_down, final_norm, loss_target, m_ffn1_norm, m_ffn1_w_up, m_ffn1_w_down, m_mix_norm, m_w_in, m_na_rel_bias, m_w_branch_a, m_w_branch_b, m_w_out, m_ffn2_norm, m_ffn2_w_up, m_ffn2_w_down, m_final_norm, v_ffn1_norm, v_ffn1_w_up, v_ffn1_w_down, v_mix_norm, v_w_in, v_na_rel_bias, v_w_branch_a, v_w_branch_b, v_w_out, v_ffn2_norm, v_ffn2_w_up, v_ffn2_w_down, v_final_norm):
    w = dict(ffn1_norm=ffn1_norm, ffn1_w_up=ffn1_w_up, ffn1_w_down=ffn1_w_down, mix_norm=mix_norm, w_in=w_in,
             na_rel_bias=na_rel_bias, w_branch_a=w_branch_a, w_branch_b=w_branch_b, w_out=w_out, ffn2_norm=ffn2_norm,
             ffn2_w_up=ffn2_w_up, ffn2_w_down=ffn2_w_down, final_norm=final_norm)
    mom = dict(ffn1_norm=m_ffn1_norm, ffn1_w_up=m_ffn1_w_up, ffn1_w_down=m_ffn1_w_down, mix_norm=m_mix_norm, w_in=m_w_in,
               na_rel_bias=m_na_rel_bias, w_branch_a=m_w_branch_a, w_branch_b=m_w_branch_b, w_out=m_w_out,
               ffn2_norm=m_ffn2_norm, ffn2_w_up=m_ffn2_w_up, ffn2_w_down=m_ffn2_w_down, final_norm=m_final_norm)
    var = dict(ffn1_norm=v_ffn1_norm, ffn1_w_up=v_ffn1_w_up, ffn1_w_down=v_ffn1_w_down, mix_norm=v_mix_norm, w_in=v_w_in,
               na_rel_bias=v_na_rel_bias, w_branch_a=v_w_branch_a, w_branch_b=v_w_branch_b, w_out=v_w_out,
               ffn2_norm=v_ffn2_norm, ffn2_w_up=v_ffn2_w_up, ffn2_w_down=v_ffn2_w_down, final_norm=v_final_norm)
    b, s, d = x.shape
    t = b * s
    depth = ffn1_norm.shape[0]
    assert depth == 2, "core c of a chip sends / reduces layer c"
    shards = {name: w[name] for name, _ in SHARDED}

    names, kinds, pieces = _weight_pieces(w)
    by_layer = [[p[l:l + 1] for p in pieces] for l in range(depth)]
    shapes = [p.shape for p in by_layer[0]]
    own = [[_place_own(p, kind, 0, name=f"own{l}_{nm}") for nm, kind, p in zip(names, kinds, by_layer[l])] for l in range(depth)]
    full = [_finish_w_in(zip(names, _gather_layer(by_layer[0], kinds, own[0], 0, name="gather_l0"))), None]
    send_sems, recv_sems, in_flight, landing, token = _gather_layer_start(by_layer[1], kinds, own[1], 0, name="gather_l1_start")
    tabs = _rope_tables(s)
    bias = _na_expand_bias(na_rel_bias, name="na_bias")

    xc = x.reshape(t, d)
    saved = []
    for l in range(depth):
        gain = ffn1_norm[l:l + 1]
        if l == 0:
            gain = gain + token[:1, :1]
        else:
            landed = _gather_layer_wait(send_sems, recv_sems, in_flight, landing, kinds, 0, xc, name="gather_l1_wait")
            full[1] = _finish_w_in(zip(names, _gather_layer_forward(shapes, kinds, landed, name="gather_l1_forward")))
        xc, s1 = _ffn_fwd(xc, gain, full[l]["ffn1_w_up"], full[l]["ffn1_w_down"], f"l{l}_ffn1")
        xc, s2 = _mixer_fwd(xc, b, mix_norm[l:l + 1], full[l], bias[l], tabs, f"l{l}_mix")
        xc, s3 = _ffn_fwd(xc, ffn2_norm[l:l + 1], full[l]["ffn2_w_up"], full[l]["ffn2_w_down"], f"l{l}_ffn2")
        saved.append((s1, s2, s3))

    dx, dxb, d_final, loss_part = _final_loss(xc, final_norm.reshape(1, d), loss_target.reshape(t, d), tt=512, name="final_loss")
    grads = _Grads(depth)
    small = {name: [None] * depth for name in REPLICATED[:-1]}
    for l in reversed(range(depth)):
        s1, s2, s3 = saved[l]
        dx, dxb, small["ffn2_norm"][l] = _ffn_bwd(dx, dxb, s3, ffn2_norm[l:l + 1], full[l]["ffn2_w_up"], full[l]["ffn2_w_down"],
                                                  l, grads, "ffn2", f"l{l}_ffn2")
        dx, dxb, small["mix_norm"][l], small["na_rel_bias"][l] = _mixer_bwd(
            dx, dxb, b, s2, mix_norm[l:l + 1], full[l], l, bias[l], tabs, grads, f"l{l}_mix")
        dx, dxb, small["ffn1_norm"][l] = _ffn_bwd(dx, dxb, s1, ffn1_norm[l:l + 1], full[l]["ffn1_w_up"], full[l]["ffn1_w_down"],
                                                  l, grads, "ffn1", f"l{l}_ffn1")
    grad_x = dx.reshape(b, s, d)

    g_out = _reduce_weight_grads(grads.arrays, shards)
    parts = [jnp.stack(small[name]).reshape(-1) for name in REPLICATED[:-1]] + [d_final.reshape(-1), loss_part[0, :1]]
    sizes = [v.shape[0] for v in parts]
    flat = jnp.concatenate(parts)
    flat = jnp.pad(flat, (0, -flat.shape[0] % (8 * LANES)))
    small_sum = _all_sum_small(flat.reshape(-1, LANES), name="small_all_sum").reshape(-1)
    off = 0
    for name, n in zip(REPLICATED, sizes[:-1]):
        g_out[name] = small_sum[off:off + n].reshape(w[name].shape)
        off += n
    loss = small_sum[off]

    names = list(w)
    delta, new_m, new_v = {}, {}, {}
    for name in names:
        delta[name], new_m[name], new_v[name] = _adamw(w[name], g_out[name], mom[name], var[name], name=f"adamw_{name}")
    return (loss, grad_x, *[g_out[n] for n in names], *[delta[n] for n in names], *[new_m[n] for n in names],
            *[new_v[n] for n in names])
```

```python
import functools

import numpy as np
import jax
import jax.numpy as jnp
from jax import lax
from jax.experimental import pallas as pl
from jax.experimental.pallas import tpu as pltpu

F32, BF16 = jnp.float32, jnp.bfloat16
MESH = pl.DeviceIdType.MESH

HEAD_DIM = 64
DILATIONS = (1, 4, 16)
DIL_HALF = 64
DIL_GROUP_HEADS = 4
DIL_HEADS = 12
NA_HEADS = 8
GRID_W = 64
NA_ROWS = 8
NA_COLS = 16
ROPE_THETA = 10000.0
RMS_EPS = 1e-6
NEG_INF = -1e30
ADAM_LR, ADAM_B1, ADAM_B2, ADAM_EPS, ADAM_WD, ADAM_STEP = 0.001, 0.9, 0.999, 1e-08, 0.01, 10
QK_SCALE = HEAD_DIM ** -0.5

N_CHIPS = 4
LANES = 128
BF16_ROWS = 16
VMEM_LIMIT = 56 * 1024 * 1024
MXU_N = 256
ROWS_NARROW = 2048
ROWS_WIDE = 512

_NN = (((1,), (0,)), ((), ()))
_NT = (((1,), (1,)), ((), ()))
_TN = (((0,), (0,)), ((), ()))

HBM = pl.BlockSpec(memory_space=pl.ANY)


def _params(**kw):
    return pltpu.CompilerParams(vmem_limit_bytes=VMEM_LIMIT, **kw)


def _dot(a, b, dims):
    return lax.dot_general(a, b, dims, preferred_element_type=F32)


def _div_tile(n, cap, mult=LANES):
    best = None
    for t in range(mult, min(n, cap) + 1, mult):
        if n % t == 0:
            best = t
    return n if best is None else best


def _stacked(block, index, sel):
    if sel is None:
        return pl.BlockSpec(block, index)
    return pl.BlockSpec((None,) + block, lambda *g: (sel,) + index(*g))


def _mm(a, b, *, mode, out_dtype, tm, tn, tk, name, alpha=1.0, res=None, a_sel=None, b_sel=None, b_k_off=0,
        out_slab=None, out_cols=None, out_col_off=0, out_into=None):
    a2, b2 = a.shape[-2:], b.shape[-2:]
    if mode == "nn":
        (m, k), n = a2, b2[1]
        a_spec = _stacked((tm, tk), lambda i, j, kk: (i, kk), a_sel)
        b_spec = _stacked((tk, tn), lambda i, j, kk: (kk + b_k_off, j), b_sel)
        dims = _NN
    elif mode == "nt":
        (m, k), n = a2, b2[0]
        a_spec = _stacked((tm, tk), lambda i, j, kk: (i, kk), a_sel)
        b_spec = _stacked((tn, tk), lambda i, j, kk: (j, kk + b_k_off), b_sel)
        dims = _NT
    else:
        (k, m), n = a2, b2[1]
        a_spec = _stacked((tk, tm), lambda i, j, kk: (kk, i), a_sel)
        b_spec = _stacked((tk, tn), lambda i, j, kk: (kk + b_k_off, j), b_sel)
        dims = _TN
    assert m % tm == 0 and n % tn == 0 and k % tk == 0, (name, a.shape, b.shape)
    nk = k // tk
    has_res = res is not None
    if out_slab is None:
        o_spec = pl.BlockSpec((tm, tn), lambda i, j, kk: (i, j))
        out_shape = jax.ShapeDtypeStruct((m, n), out_dtype)
    else:
        o_spec = _stacked((tm, tn), lambda i, j, kk: (i, j + out_col_off), out_slab[0])
        out_shape = jax.ShapeDtypeStruct((out_slab[1], m, n if out_cols is None else out_cols), out_dtype)
    r_spec = pl.BlockSpec((tm, tn), lambda i, j, kk: (i, j))
    n_in = 2 + has_res + (out_into is not None)

    def body(*refs):
        a_ref, b_ref = refs[0], refs[1]
        r_ref = refs[2] if has_res else None
        o_ref = refs[n_in]
        p = _dot(a_ref[...], b_ref[...], dims)

        def finish(acc):
            y = acc * alpha if alpha != 1.0 else acc
            if has_res:
                y = y + r_ref[...].astype(F32)
            o_ref[...] = y.astype(o_ref.dtype)

        if nk == 1:
            finish(p)
        else:
            acc_ref = refs[n_in + 1]
            kk = pl.program_id(2)

            @pl.when(kk == 0)
            def _():
                acc_ref[...] = p

            @pl.when(kk > 0)
            def _():
                acc_ref[...] += p

            @pl.when(kk == nk - 1)
            def _():
                finish(acc_ref[...])

    operands = [a, b] + ([res] if has_res else [])
    in_specs = [a_spec, b_spec] + ([r_spec] if has_res else [])
    aliases = {}
    if out_into is not None:
        aliases = {len(operands): 0}
        operands.append(out_into)
        in_specs.append(HBM)
    return pl.pallas_call(
        body, name=name, grid=(m // tm, n // tn, nk), in_specs=in_specs, out_specs=o_spec, out_shape=out_shape,
        scratch_shapes=[pltpu.VMEM((tm, tn), F32)] if nk > 1 else [], input_output_aliases=aliases,
        compiler_params=_params(dimension_semantics=("parallel", "parallel", "arbitrary")),
    )(*operands)


def _mm_swiglu_fwd(h, w_up, *, tm, tn, name):
    m, k = h.shape
    n = w_up.shape[1] // 2
    h_spec = pl.BlockSpec((tm, k), lambda i, j: (i, 0))
    wg_spec = pl.BlockSpec((k, tn), lambda i, j: (0, j))
    wu_spec = pl.BlockSpec((k, tn), lambda i, j: (0, j + n // tn))
    o_spec = pl.BlockSpec((tm, tn), lambda i, j: (i, j))

    def body(h_ref, wg_ref, wu_ref, a_ref, g_ref, u_ref):
        hb = h_ref[...]
        g = _dot(hb, wg_ref[...], _NN)
        u = _dot(hb, wu_ref[...], _NN)
        a_ref[...] = (g * jax.nn.sigmoid(g) * u).astype(BF16)
        g_ref[...] = g.astype(BF16)
        u_ref[...] = u.astype(BF16)

    out = jax.ShapeDtypeStruct((m, n), BF16)
    return pl.pallas_call(
        body, name=name, grid=(m // tm, n // tn), in_specs=[h_spec, wg_spec, wu_spec],
        out_specs=[o_spec] * 3, out_shape=[out] * 3,
        compiler_params=_params(dimension_semantics=("parallel", "parallel")),
    )(h, w_up, w_up)


def _mm_swiglu_bwd(dy, w_down, gate, up, *, alpha, tm, tn, name):
    m, k = dy.shape
    n = w_down.shape[0]
    dy_spec = pl.BlockSpec((tm, k), lambda i, j: (i, 0))
    w_spec = pl.BlockSpec((tn, k), lambda i, j: (j, 0))
    o_spec = pl.BlockSpec((tm, tn), lambda i, j: (i, j))

    def body(dy_ref, w_ref, g_ref, u_ref, dg_ref, du_ref):
        da = _dot(dy_ref[...], w_ref[...], _NT) * alpha
        g = g_ref[...].astype(F32)
        u = u_ref[...].astype(F32)
        sg = jax.nn.sigmoid(g)
        dg_ref[...] = (da * u * (sg * (1.0 + g * (1.0 - sg)))).astype(BF16)
        du_ref[...] = (da * (g * sg)).astype(BF16)

    out = jax.ShapeDtypeStruct((m, n), BF16)
    return pl.pallas_call(
        body, name=name, grid=(m // tm, n // tn), in_specs=[dy_spec, w_spec, o_spec, o_spec],
        out_specs=[o_spec] * 2, out_shape=[out] * 2,
        compiler_params=_params(dimension_semantics=("parallel", "parallel")),
    )(dy, w_down, gate, up)


def _rms_fwd(x, g, *, tt, name):
    t, d = x.shape

    def body(x_ref, g_ref, h_ref):
        xv = x_ref[...]
        rstd = lax.rsqrt(jnp.mean(xv * xv, axis=1, keepdims=True) + RMS_EPS)
        h_ref[...] = (xv * rstd * g_ref[...]).astype(BF16)

    return pl.pallas_call(
        body, name=name, grid=(t // tt,),
        in_specs=[pl.BlockSpec((tt, d), lambda i: (i, 0)), pl.BlockSpec((1, d), lambda i: (0, 0))],
        out_specs=pl.BlockSpec((tt, d), lambda i: (i, 0)), out_shape=jax.ShapeDtypeStruct((t, d), BF16),
        compiler_params=_params(dimension_semantics=("parallel",)),
    )(x, g)


def _rms_bwd(dh, x, g, dres, *, tt, name):
    t, d = x.shape

    def body(dh_ref, x_ref, g_ref, r_ref, dx_ref, dxb_ref, dg_ref):
        xv = x_ref[...]
        rstd = lax.rsqrt(jnp.mean(xv * xv, axis=1, keepdims=True) + RMS_EPS)
        xhat = xv * rstd
        dhv = dh_ref[...]
        dxhat = dhv * g_ref[...]
        dx = r_ref[...] + rstd * (dxhat - xhat * jnp.mean(dxhat * xhat, axis=1, keepdims=True))
        dx_ref[...] = dx
        dxb_ref[...] = dx.astype(BF16)

        @pl.when(pl.program_id(0) == 0)
        def _():
            dg_ref[...] = jnp.zeros_like(dg_ref)

        dg_ref[...] += jnp.sum(dhv * xhat, axis=0, keepdims=True)

    row = pl.BlockSpec((tt, d), lambda i: (i, 0))
    vec = pl.BlockSpec((1, d), lambda i: (0, 0))
    return pl.pallas_call(
        body, name=name, grid=(t // tt,), in_specs=[row, row, vec, row], out_specs=[row, row, vec],
        out_shape=[jax.ShapeDtypeStruct((t, d), F32), jax.ShapeDtypeStruct((t, d), BF16), jax.ShapeDtypeStruct((1, d), F32)],
        compiler_params=_params(dimension_semantics=("arbitrary",)),
    )(dh, x, g, dres)


def _final_loss(x, g, target, *, tt, name):
    t, d = x.shape

    def body(x_ref, g_ref, t_ref, dx_ref, dxb_ref, dg_ref, loss_ref):
        xv = x_ref[...]
        gv = g_ref[...]
        rstd = lax.rsqrt(jnp.mean(xv * xv, axis=1, keepdims=True) + RMS_EPS)
        xhat = xv * rstd
        err = xhat * gv - t_ref[...]
        dy = err * (1.0 / d)
        dxhat = dy * gv
        dx = rstd * (dxhat - xhat * jnp.mean(dxhat * xhat, axis=1, keepdims=True))
        dx_ref[...] = dx
        dxb_ref[...] = dx.astype(BF16)

        @pl.when(pl.program_id(0) == 0)
        def _():
            dg_ref[...] = jnp.zeros_like(dg_ref)
            loss_ref[...] = jnp.zeros_like(loss_ref)

        dg_ref[...] += jnp.sum(dy * xhat, axis=0, keepdims=True)
        part = 0.5 * jnp.sum(jnp.mean(err * err, axis=1, keepdims=True), axis=0, keepdims=True)
        loss_ref[...] += jnp.broadcast_to(part, loss_ref.shape)

    row = pl.BlockSpec((tt, d), lambda i: (i, 0))
    vec = pl.BlockSpec((1, d), lambda i: (0, 0))
    one = pl.BlockSpec((1, LANES), lambda i: (0, 0))
    return pl.pallas_call(
        body, name=name, grid=(t // tt,), in_specs=[row, vec, row], out_specs=[row, row, vec, one],
        out_shape=[jax.ShapeDtypeStruct((t, d), F32), jax.ShapeDtypeStruct((t, d), BF16), jax.ShapeDtypeStruct((1, d), F32),
                   jax.ShapeDtypeStruct((1, LANES), F32)],
        compiler_params=_params(dimension_semantics=("arbitrary",)),
    )(x, g, target)


def _swap_halves(x):
    lane = lax.broadcasted_iota(jnp.int32, x.shape, 1)
    return jnp.where((lane // 32) % 2 == 0, pltpu.roll(x, 96, 1), pltpu.roll(x, 32, 1))


def _rope_tables(s):
    half = HEAD_DIM // 2
    inv_freq = ROPE_THETA ** (-jnp.arange(half, dtype=F32) / half)
    ang = jnp.arange(s).astype(F32)[:, None] * inv_freq[None, :]
    cos, sin = jnp.cos(ang), jnp.sin(ang)
    return jnp.tile(cos, (1, 4)), jnp.concatenate([-sin, sin, -sin, sin], axis=1)


def _dilation_of_tile(p):
    dilated = p < 3 * DIL_HEADS // 2
    g = (p % (DIL_HEADS // 2)) // (DIL_GROUP_HEADS // 2)
    return [(dilated & (g == gi)) | (jnp.logical_not(dilated) if gi == 0 else False) for gi in range(len(DILATIONS))]


def _residue_major(ref, d):
    s = ref.shape[0]
    if d == 1:
        return ref[...]
    return jnp.concatenate([ref[pl.ds(r, s // d, stride=d), :] for r in range(d)], axis=0)


def _split_heads(proj, cos4, sin4, *, n_pairs, rot_pairs, scale_ranges, name):
    b, s, _ = proj.shape

    def body(x_ref, c_ref, s_ref, o_ref):
        p = pl.program_id(1)
        is_q = functools.reduce(jnp.logical_or, [(p >= lo) & (p < hi) for lo, hi in scale_ranges])
        scale = jnp.where(is_q, QK_SCALE, 1.0)

        def put(y):
            o_ref[0] = y[:, :HEAD_DIM].astype(BF16)
            o_ref[1] = y[:, HEAD_DIM:].astype(BF16)

        for d, in_group in zip(DILATIONS, _dilation_of_tile(p)):
            @pl.when(in_group & (p < rot_pairs))
            def _(d=d):
                x = _residue_major(x_ref, d)
                put((x * _residue_major(c_ref, d) + _swap_halves(x) * _residue_major(s_ref, d)) * scale)

            @pl.when(in_group & (p >= rot_pairs))
            def _(d=d):
                put(_residue_major(x_ref, d) * scale)

    tab = pl.BlockSpec((s, LANES), lambda bi, p: (0, 0))
    return pl.pallas_call(
        body, name=name, grid=(b, n_pairs),
        in_specs=[pl.BlockSpec((None, s, LANES), lambda bi, p: (bi, 0, p)), tab, tab],
        out_specs=pl.BlockSpec((None, 2, s, HEAD_DIM), lambda bi, p: (bi, p, 0, 0)),
        out_shape=jax.ShapeDtypeStruct((b, 2 * n_pairs, s, HEAD_DIM), BF16),
        compiler_params=_params(dimension_semantics=("parallel", "parallel")),
    )(proj, cos4, sin4)


def _merge_heads(dheads, cos4, sin4, *, heads_per_row, rot_pairs, scale_pairs, dilated, out_cols, tile_off, into, name):
    b, hpr, r, s, _ = dheads.shape
    n_pairs = hpr * r // 2
    ppr = hpr // 2

    def body(d_ref, c_ref, s_ref, *rest):
        o_ref, t_ref = rest[-2:]
        p = pl.program_id(1)
        scale = jnp.where(p < scale_pairs, QK_SCALE, 1.0)

        def tokens(d):
            dy = jnp.concatenate([d_ref[0], d_ref[1]], axis=1)
            if d == 1:
                return dy
            for res in range(d):
                t_ref[pl.ds(res, s // d, stride=d), :] = dy[res * (s // d):(res + 1) * (s // d), :]
            return t_ref[...]

        groups = _dilation_of_tile(p) if dilated else [p >= 0]
        for d, in_group in zip(DILATIONS, groups):
            @pl.when(in_group & (p < rot_pairs))
            def _(d=d):
                dy = tokens(d)
                o_ref[...] = ((dy * c_ref[...] - _swap_halves(dy) * s_ref[...]) * scale).astype(BF16)

            @pl.when(in_group & (p >= rot_pairs))
            def _(d=d):
                o_ref[...] = (tokens(d) * scale).astype(BF16)

    tab = pl.BlockSpec((s, LANES), lambda bi, p: (0, 0))
    operands = [dheads, cos4, sin4] + ([] if into is None else [into])
    return pl.pallas_call(
        body, name=name, grid=(b, n_pairs),
        in_specs=[pl.BlockSpec((None, 2, None, s, HEAD_DIM), lambda bi, p: (bi, p % ppr, p // ppr, 0, 0)), tab, tab]
        + ([] if into is None else [HBM]),
        out_specs=pl.BlockSpec((None, s, LANES), lambda bi, p: (bi, 0, p + tile_off)),
        out_shape=jax.ShapeDtypeStruct((b, s, out_cols), BF16),
        input_output_aliases={} if into is None else {3: 0},
        scratch_shapes=[pltpu.VMEM((s, LANES), F32)],
        compiler_params=_params(dimension_semantics=("parallel", "parallel")),
    )(*operands)


DIL_TQ = 256


def _dil_block(g, s):
    run = s // DILATIONS[g]
    return DIL_TQ if run <= DIL_TQ else min(run, DIL_TQ + 2 * LANES)


def _dil_keys(g, q0, s):
    run = max(s // DILATIONS[g], DIL_TQ)
    lo = (q0 // run) * run
    return pl.multiple_of(jnp.clip(q0 - LANES, lo, lo + run - _dil_block(g, s)), LANES)


def _dil_band(g, q0, start, shape, s):
    row = q0 + lax.broadcasted_iota(jnp.int32, shape, 0)
    col = start + lax.broadcasted_iota(jnp.int32, shape, 1)
    ok = jnp.abs(row - col) <= DIL_HALF
    run = s // DILATIONS[g]
    if run < DIL_TQ:
        shift = run.bit_length() - 1
        ok = ok & ((row >> shift) == (col >> shift))
    return ok


def _dil_tokens(g, q0, s):
    d = DILATIONS[g]
    if d == 1:
        return [(0, DIL_TQ, pl.ds(q0, DIL_TQ))]
    run = s // d
    n = min(run, DIL_TQ)
    return [(lo, n, pl.ds(((q0 + lo) % run) * d + (q0 + lo) // run, n, stride=d)) for lo in range(0, DIL_TQ, n)]


def _dil_gather(ref, pieces):
    return jnp.concatenate([ref[rows, :] for _, _, rows in pieces], axis=0) if len(pieces) > 1 else ref[pieces[0][2], :]


def _dil_head_spec(part, g, s):
    return pl.BlockSpec((None, None, s, HEAD_DIM), lambda b, j: (b, part * DIL_HEADS + g * DIL_GROUP_HEADS + j, 0, 0))


def _dil_attn_fwd(heads, *, name):
    b, _, s, _ = heads.shape
    n_g = len(DILATIONS)

    def body(*refs):
        qkv = refs[:3 * n_g]
        o_ref, l_ref, og_ref, lg_ref = refs[3 * n_g:]
        for g in range(n_g):
            q_ref, k_ref, v_ref = qkv[3 * g:3 * g + 3]
            width = _dil_block(g, s)

            def step(i, carry, g=g, q_ref=q_ref, k_ref=k_ref, v_ref=v_ref, width=width):
                q0 = pl.multiple_of(i * DIL_TQ, DIL_TQ)
                start = _dil_keys(g, q0, s)
                sc = _dot(q_ref[pl.ds(q0, DIL_TQ), :], k_ref[pl.ds(start, width), :], _NT)
                sc = jnp.where(_dil_band(g, q0, start, sc.shape, s), sc, NEG_INF)
                m = jnp.max(sc, axis=1, keepdims=True)
                p = jnp.exp(sc - m)
                den = jnp.sum(p, axis=1, keepdims=True)
                o = _dot(p.astype(BF16), v_ref[pl.ds(start, width), :], _NN) / den
                lse = m + jnp.log(den)
                for lo, n, rows in _dil_tokens(g, q0, s):
                    og_ref[g, rows, :] = o[lo:lo + n]
                    lg_ref[g, rows, :] = lse[lo:lo + n]
                return carry

            lax.fori_loop(0, s // DIL_TQ, step, 0)
        lses = [lg_ref[g] for g in range(n_g)]
        m = functools.reduce(jnp.maximum, lses)
        ws = [jnp.exp(l - m) for l in lses]
        den = functools.reduce(jnp.add, ws)
        o_ref[...] = (functools.reduce(jnp.add, [w * og_ref[g] for g, w in enumerate(ws)]) / den).astype(o_ref.dtype)
        l_ref[...] = m + jnp.log(den)

    out = pl.BlockSpec((None, None, s, HEAD_DIM), lambda bi, j: (bi, j, 0, 0))
    lse = pl.BlockSpec((None, None, s, 1), lambda bi, j: (bi, j, 0, 0))
    return pl.pallas_call(
        body, name=name, grid=(b, DIL_GROUP_HEADS),
        in_specs=[_dil_head_spec(part, g, s) for g in range(n_g) for part in range(3)],
        out_specs=[out, lse],
        out_shape=[jax.ShapeDtypeStruct((b, DIL_GROUP_HEADS, s, HEAD_DIM), BF16),
                   jax.ShapeDtypeStruct((b, DIL_GROUP_HEADS, s, 1), F32)],
        scratch_shapes=[pltpu.VMEM((n_g, s, HEAD_DIM), F32), pltpu.VMEM((n_g, s, 1), F32)],
        compiler_params=_params(dimension_semantics=("parallel", "parallel")),
    )(*([heads] * (3 * n_g)))


def _dil_attn_bwd(heads, out, lse, dout, *, name):
    b, _, s, _ = heads.shape
    n_g = len(DILATIONS)

    def body(*refs):
        qkv = refs[:3 * n_g]
        o_ref, l_ref, do_ref, d_ref, delta_ref = refs[3 * n_g:]
        d_ref[...] = jnp.zeros_like(d_ref)
        delta_ref[...] = jnp.sum(do_ref[...] * o_ref[...].astype(F32), axis=1, keepdims=True)
        for g in range(n_g):
            q_ref, k_ref, v_ref = qkv[3 * g:3 * g + 3]
            width = _dil_block(g, s)

            def step(i, carry, g=g, q_ref=q_ref, k_ref=k_ref, v_ref=v_ref, width=width):
                q0 = pl.multiple_of(i * DIL_TQ, DIL_TQ)
                start = _dil_keys(g, q0, s)
                win = pl.ds(start, width)
                pieces = _dil_tokens(g, q0, s)
                do_b = _dil_gather(do_ref, pieces).astype(BF16)
                q, k, v = q_ref[pl.ds(q0, DIL_TQ), :], k_ref[win, :], v_ref[win, :]
                sc = _dot(q, k, _NT)
                p = jnp.where(_dil_band(g, q0, start, sc.shape, s), jnp.exp(sc - _dil_gather(l_ref, pieces)), 0.0)
                ds = (p * (_dot(do_b, v, _NT) - _dil_gather(delta_ref, pieces))).astype(BF16)
                d_ref[g, pl.ds(q0, DIL_TQ), :] = _dot(ds, k, _NN)
                d_ref[n_g + g, win, :] += _dot(ds, q, _TN)
                d_ref[2 * n_g + g, win, :] += _dot(p.astype(BF16), do_b, _TN)
                return carry

            lax.fori_loop(0, s // DIL_TQ, step, 0)

    per_head = lambda bi, j: (bi, j, 0, 0)
    return pl.pallas_call(
        body, name=name, grid=(b, DIL_GROUP_HEADS),
        in_specs=[_dil_head_spec(part, g, s) for g in range(n_g) for part in range(3)]
        + [pl.BlockSpec((None, None, s, HEAD_DIM), per_head), pl.BlockSpec((None, None, s, 1), per_head),
           pl.BlockSpec((None, None, s, HEAD_DIM), per_head)],
        out_specs=pl.BlockSpec((None, None, 3 * n_g, s, HEAD_DIM), lambda bi, j: (bi, j, 0, 0, 0)),
        out_shape=jax.ShapeDtypeStruct((b, DIL_GROUP_HEADS, 3 * n_g, s, HEAD_DIM), F32),
        scratch_shapes=[pltpu.VMEM((s, 1), F32)],
        compiler_params=_params(dimension_semantics=("parallel", "parallel")),
    )(*([heads] * (3 * n_g)), out, lse, dout)


NA_BIAS_ROWS = 2 * NA_ROWS - 1
NA_BIAS_COLS = 2 * NA_COLS - 1
NA_BLOCK = 4
NA_SPAN = NA_ROWS + NA_BLOCK - 1
NA_Q = NA_BLOCK * GRID_W
NA_KEYS = NA_SPAN * GRID_W
NA_FORMS = 3


def _na_onehot():
    c = np.arange(GRID_W)[:, None]
    k = np.arange(GRID_W)[None, :]
    lo = np.clip(c - NA_COLS // 2, 0, GRID_W - NA_COLS)
    valid = (k >= lo) & (k < lo + NA_COLS)
    onehot = np.zeros((GRID_W, GRID_W, LANES), np.float32)
    cc, kk = np.nonzero(valid)
    onehot[cc, kk, kk - cc + NA_COLS - 1] = 1.0
    return onehot.reshape(GRID_W * GRID_W, LANES), valid.reshape(1, GRID_W * GRID_W)


def _na_block_rows(n_rows):
    table = np.full((NA_FORMS, NA_BLOCK, NA_SPAN), NA_BIAS_ROWS, np.int64)
    n_blocks = n_rows // NA_BLOCK
    for form, ib in enumerate((0, 1, n_blocks - 1)):
        base = min(max(NA_BLOCK * ib - NA_ROWS // 2, 0), n_rows - NA_SPAN)
        for rl in range(NA_BLOCK):
            r = NA_BLOCK * ib + rl
            row_lo = min(max(r - NA_ROWS // 2, 0), n_rows - NA_ROWS)
            for kl in range(NA_SPAN):
                if row_lo <= base + kl < row_lo + NA_ROWS:
                    table[form, rl, kl] = base + kl - r + NA_ROWS - 1
    return table


def _na_block(ib, n_rows):
    n_blocks = n_rows // NA_BLOCK
    base = jnp.clip(NA_BLOCK * ib - NA_ROWS // 2, 0, n_rows - NA_SPAN)
    return base, jnp.where(ib == 0, 0, jnp.where(ib == n_blocks - 1, 2, 1))


def _na_expand_bias(rel_bias, *, name):
    l, h, nr, nc = rel_bias.shape
    onehot, valid = _na_onehot()
    rb = jnp.pad(rel_bias, ((0, 0), (0, 0), (0, 1), (0, LANES - nc))).reshape(l * h * (nr + 1), LANES)
    live = jnp.asarray(np.tile(np.arange(nr + 1) < nr, l * h).astype(np.float32)[:, None])

    def body(rb_ref, oh_ref, valid_ref, live_ref, e_ref):
        e = lax.dot_general(rb_ref[...], oh_ref[...], _NT, precision=lax.Precision.HIGHEST, preferred_element_type=F32)
        e_ref[...] = jnp.where((valid_ref[...] > 0) & (live_ref[...] > 0), e, NEG_INF)

    e = pl.pallas_call(
        body, name=name, out_shape=jax.ShapeDtypeStruct((l * h * (nr + 1), GRID_W * GRID_W), F32), compiler_params=_params(),
    )(rb, jnp.asarray(onehot), jnp.asarray(valid.astype(np.float32)), live)
    return e.reshape(l, h, nr + 1, GRID_W, GRID_W)


def _na_collapse_bias(de, *, name):
    b, h = de.shape[:2]
    onehot, _ = _na_onehot()
    rows = h * NA_BIAS_ROWS

    def diag(e_ref, oh_ref, o_ref):
        e = e_ref[0]
        for bi in range(1, b):
            e = e + e_ref[bi]
        o_ref[...] = lax.dot_general(e, oh_ref[...], _NN, precision=lax.Precision.HIGHEST, preferred_element_type=F32)

    drb = pl.pallas_call(
        diag, name=name, out_shape=jax.ShapeDtypeStruct((rows, LANES), F32), compiler_params=_params(),
    )(de.reshape(b, rows, GRID_W * GRID_W), jnp.asarray(onehot))
    return drb[:, :NA_BIAS_COLS].reshape(h, NA_BIAS_ROWS, NA_BIAS_COLS)


def _na_tiles(n_rows):
    table = _na_block_rows(n_rows)
    return [(f, rl, kl, int(table[f, rl, kl])) for f in range(NA_FORMS) for rl in range(NA_BLOCK) for kl in range(NA_SPAN)]


def _na_tile(ref, form, rl, kl):
    return ref.at[form, rl * GRID_W:(rl + 1) * GRID_W, kl * GRID_W:(kl + 1) * GRID_W]


def _na_head_spec(part, first, s):
    return pl.BlockSpec((None, None, s, HEAD_DIM), lambda b, h: (b, first + part * NA_HEADS + h, 0, 0))


def _na_attn_fwd(heads, bias, *, first, name):
    b, _, s, _ = heads.shape
    n_rows = s // GRID_W
    tiles = _na_tiles(n_rows)

    def body(q_ref, k_ref, v_ref, e_ref, o_ref, l_ref, b_ref):
        for form, rl, kl, i in tiles:
            _na_tile(b_ref, form, rl, kl)[...] = e_ref[i]

        def step(ib, carry):
            base, form = _na_block(ib, n_rows)
            rows = pl.ds(pl.multiple_of(ib * NA_Q, NA_Q), NA_Q)
            win = pl.ds(pl.multiple_of(base * GRID_W, GRID_W), NA_KEYS)
            sc = _dot(q_ref[rows, :], k_ref[win, :], _NT) + b_ref[form]
            m = jnp.max(sc, axis=1, keepdims=True)
            p = jnp.exp(sc - m)
            den = jnp.sum(p, axis=1, keepdims=True)
            o_ref[rows, :] = (_dot(p.astype(BF16), v_ref[win, :], _NN) / den).astype(o_ref.dtype)
            l_ref[rows, :] = m + jnp.log(den)
            return carry

        lax.fori_loop(0, n_rows // NA_BLOCK, step, 0)

    per_head = lambda bi, h: (bi, h, 0, 0)
    return pl.pallas_call(
        body, name=name, grid=(b, NA_HEADS),
        in_specs=[_na_head_spec(part, first, s) for part in range(3)]
        + [pl.BlockSpec((None, NA_BIAS_ROWS + 1, GRID_W, GRID_W), lambda bi, h: (h, 0, 0, 0))],
        out_specs=[pl.BlockSpec((None, None, s, HEAD_DIM), per_head), pl.BlockSpec((None, None, s, 1), per_head)],
        out_shape=[jax.ShapeDtypeStruct((b, NA_HEADS, s, HEAD_DIM), BF16), jax.ShapeDtypeStruct((b, NA_HEADS, s, 1), F32)],
        scratch_shapes=[pltpu.VMEM((NA_FORMS, NA_Q, NA_KEYS), F32)],
        compiler_params=_params(dimension_semantics=("parallel", "parallel")),
    )(heads, heads, heads, bias)


def _na_attn_bwd(heads, bias, out, lse, dout, *, first, name):
    b, _, s, _ = heads.shape
    n_rows = s // GRID_W
    tiles = _na_tiles(n_rows)

    def body(q_ref, k_ref, v_ref, e_ref, o_ref, l_ref, do_ref, d_ref, de_ref, b_ref, db_ref):
        for form, rl, kl, i in tiles:
            _na_tile(b_ref, form, rl, kl)[...] = e_ref[i]
        d_ref[...] = jnp.zeros_like(d_ref)
        db_ref[...] = jnp.zeros_like(db_ref)

        def step(ib, carry):
            base, form = _na_block(ib, n_rows)
            rows = pl.ds(pl.multiple_of(ib * NA_Q, NA_Q), NA_Q)
            win = pl.ds(pl.multiple_of(base * GRID_W, GRID_W), NA_KEYS)
            q, k, v = q_ref[rows, :], k_ref[win, :], v_ref[win, :]
            do = do_ref[rows, :]
            delta = jnp.sum(do * o_ref[rows, :].astype(F32), axis=1, keepdims=True)
            do_b = do.astype(BF16)
            p = jnp.exp(_dot(q, k, _NT) + b_ref[form] - l_ref[rows, :])
            ds = p * (_dot(do_b, v, _NT) - delta)
            db_ref[form] += ds
            ds_b = ds.astype(BF16)
            d_ref[0, rows, :] = _dot(ds_b, k, _NN)
            d_ref[1, win, :] += _dot(ds_b, q, _TN)
            d_ref[2, win, :] += _dot(p.astype(BF16), do_b, _TN)
            return carry

        lax.fori_loop(0, n_rows // NA_BLOCK, step, 0)
        acc = [None] * NA_BIAS_ROWS
        for form, rl, kl, i in tiles:
            if i < NA_BIAS_ROWS:
                t = _na_tile(db_ref, form, rl, kl)[...]
                acc[i] = t if acc[i] is None else acc[i] + t
        for i in range(NA_BIAS_ROWS):
            de_ref[i] = acc[i]

    per_head = lambda bi, h: (bi, h, 0, 0)
    return pl.pallas_call(
        body, name=name, grid=(b, NA_HEADS),
        in_specs=[_na_head_spec(part, first, s) for part in range(3)]
        + [pl.BlockSpec((None, NA_BIAS_ROWS + 1, GRID_W, GRID_W), lambda bi, h: (h, 0, 0, 0)),
           pl.BlockSpec((None, None, s, HEAD_DIM), per_head), pl.BlockSpec((None, None, s, 1), per_head),
           pl.BlockSpec((None, None, s, HEAD_DIM), per_head)],
        out_specs=[pl.BlockSpec((None, None, 3, s, HEAD_DIM), lambda bi, h: (bi, h, 0, 0, 0)),
                   pl.BlockSpec((None, None, NA_BIAS_ROWS, GRID_W, GRID_W), lambda bi, h: (bi, h, 0, 0, 0))],
        out_shape=[jax.ShapeDtypeStruct((b, NA_HEADS, 3, s, HEAD_DIM), F32),
                   jax.ShapeDtypeStruct((b, NA_HEADS, NA_BIAS_ROWS, GRID_W, GRID_W), F32)],
        scratch_shapes=[pltpu.VMEM((NA_FORMS, NA_Q, NA_KEYS), F32), pltpu.VMEM((NA_FORMS, NA_Q, NA_KEYS), F32)],
        compiler_params=_params(dimension_semantics=("parallel", "parallel")),
    )(heads, heads, heads, bias, out, lse, dout)


GATE_TILE = 256


def _gate_fwd(proj, z, *, gate_col, tt, name):
    _, t, d = z.shape
    nj = d // GATE_TILE
    c0 = gate_col // GATE_TILE

    def body(ga_ref, gb_ref, za_ref, zb_ref, o_ref):
        o_ref[...] = (jax.nn.sigmoid(ga_ref[...]) * za_ref[...] + jax.nn.sigmoid(gb_ref[...]) * zb_ref[...]).astype(BF16)

    return pl.pallas_call(
        body, name=name, grid=(t // tt, nj),
        in_specs=[pl.BlockSpec((tt, GATE_TILE), lambda i, j: (i, c0 + j)),
                  pl.BlockSpec((tt, GATE_TILE), lambda i, j: (i, c0 + nj + j)),
                  pl.BlockSpec((None, tt, GATE_TILE), lambda i, j: (0, i, j)),
                  pl.BlockSpec((None, tt, GATE_TILE), lambda i, j: (1, i, j))],
        out_specs=pl.BlockSpec((tt, GATE_TILE), lambda i, j: (i, j)), out_shape=jax.ShapeDtypeStruct((t, d), BF16),
        compiler_params=_params(dimension_semantics=("parallel", "parallel")),
    )(proj, proj, z, z)


def _gate_bwd(dm, proj, z, *, gate_col, tt, name):
    _, t, d = z.shape
    nj = d // GATE_TILE
    c0 = gate_col // GATE_TILE

    def body(dm_ref, g_ref, z_ref, dz_ref, dg_ref):
        dmv = dm_ref[...]
        sg = jax.nn.sigmoid(g_ref[...])
        dz_ref[...] = (dmv * sg).astype(BF16)
        dg_ref[...] = (dmv * z_ref[...] * sg * (1.0 - sg)).astype(BF16)

    return pl.pallas_call(
        body, name=name, grid=(t // tt, 2 * nj),
        in_specs=[pl.BlockSpec((tt, GATE_TILE), lambda i, j: (i, j % nj)),
                  pl.BlockSpec((tt, GATE_TILE), lambda i, j: (i, c0 + j)),
                  pl.BlockSpec((None, tt, GATE_TILE), lambda i, j: (j // nj, i, j % nj))],
        out_specs=[pl.BlockSpec((None, tt, GATE_TILE), lambda i, j: (j // nj, i, j % nj)),
                   pl.BlockSpec((tt, GATE_TILE), lambda i, j: (i, c0 + j))],
        out_shape=[jax.ShapeDtypeStruct((2, t, d), BF16), jax.ShapeDtypeStruct(proj.shape, BF16)],
        compiler_params=_params(dimension_semantics=("parallel", "parallel")),
    )(dm, proj, z)


def _adamw(w, g, m, v, *, name):
    shape = w.shape
    w2, g2, m2, v2 = (t.reshape(-1, shape[-1]) for t in (w, g, m, v))
    rows, cols = w2.shape
    tr = rows
    for cand in (512, 256, 128, 64, 32, 16, 8):
        if rows % cand == 0:
            tr = cand
            break

    def body(w_ref, g_ref, m_ref, v_ref, d_ref, nm_ref, nv_ref):
        gv = g_ref[...]
        nm = ADAM_B1 * m_ref[...] + (1.0 - ADAM_B1) * gv
        nv = ADAM_B2 * v_ref[...] + (1.0 - ADAM_B2) * (gv * gv)
        m_hat = nm / (1.0 - ADAM_B1 ** ADAM_STEP)
        v_hat = nv / (1.0 - ADAM_B2 ** ADAM_STEP)
        d_ref[...] = -ADAM_LR * (m_hat / (jnp.sqrt(v_hat) + ADAM_EPS) + ADAM_WD * w_ref[...])
        nm_ref[...] = nm
        nv_ref[...] = nv

    blk = pl.BlockSpec((tr, cols), lambda i: (i, 0))
    out = jax.ShapeDtypeStruct((rows, cols), F32)
    res = pl.pallas_call(
        body, name=name, grid=(rows // tr,), in_specs=[blk] * 4, out_specs=[blk] * 3, out_shape=[out] * 3,
        compiler_params=_params(dimension_semantics=("parallel",)),
    )(w2, g2, m2, v2)
    return tuple(t.reshape(shape) for t in res)


def _my_place():
    return lax.axis_index("x"), lax.axis_index("y"), lax.axis_index("c")


def _other_chips(x, y):
    return [(1 - x, y), (x, 1 - y), (1 - x, 1 - y)]


def _chip_no(chip):
    return 2 * chip[0] + chip[1]


def _window(ref, kind, size, chip, lead):
    if kind == "col":
        return ref.at[(*lead, slice(None), pl.ds(pl.multiple_of(chip * size, LANES), size))]
    if kind == "row":
        return ref.at[(*lead, pl.ds(pl.multiple_of(chip * size, BF16_ROWS), size), slice(None))]
    shard = size + HEAD_DIM
    if kind == "win_main":
        return ref.at[(*lead, slice(None), pl.ds(pl.multiple_of(chip * shard + HEAD_DIM * (chip % 2), LANES), size))]
    assert kind == "win_strad"
    return ref.at[(*lead, slice(None), pl.ds(pl.multiple_of(size + 2 * shard * (chip // 2), LANES), LANES))]


def _full_shape(shard, kind):
    _, k, n = shard.shape
    return {"col": (k, N_CHIPS * n), "row": (N_CHIPS * k, n), "win_main": (k, N_CHIPS * (n + HEAD_DIM)),
            "slot": (N_CHIPS, k, n)}[kind]


def _place_own(shard, kind, layer, *, name):
    _, k, n = shard.shape
    tr = _div_tile(k, 512, BF16_ROWS)
    tc = LANES if kind == "win_main" else n
    mine = 2 * lax.axis_index("x") + lax.axis_index("y")
    row0 = mine * (k // tr) if kind == "row" else 0
    col0 = {"col": mine, "row": 0, "slot": 0, "win_main": (mine * (n + HEAD_DIM) + HEAD_DIM * (mine % 2)) // LANES}[kind]
    scalars = jnp.stack([mine, row0, col0]).astype(jnp.int32)

    def body(s_ref, i_ref, o_ref):
        o_ref[...] = i_ref[...]

    if kind == "slot":
        o_spec = pl.BlockSpec((None, tr, tc), lambda i, j, s: (s[0], i, j))
    else:
        o_spec = pl.BlockSpec((tr, tc), lambda i, j, s: (s[1] + i, s[2] + j))
    return pl.pallas_call(
        body, name=name,
        grid_spec=pltpu.PrefetchScalarGridSpec(
            num_scalar_prefetch=1, grid=(k // tr, n // tc),
            in_specs=[pl.BlockSpec((None, tr, tc), lambda i, j, s: (layer, i, j))], out_specs=o_spec),
        out_shape=jax.ShapeDtypeStruct(_full_shape(shard, kind), shard.dtype),
        compiler_params=_params(dimension_semantics=("parallel", "parallel")),
    )(scalars, shard)


class _GatherPlan:
    def __init__(self, src, dst, shapes, kinds, layer, send_sems, recv_sems):
        self.src, self.dst, self.shapes, self.kinds, self.layer = src, dst, shapes, kinds, layer
        self.send_sems, self.recv_sems = send_sems, recv_sems
        self.x, self.y, self.c = _my_place()
        self.mine = 2 * self.x + self.y
        self.chips = _other_chips(self.x, self.y)
        self.n = len(src)

    def half(self, i, chip, half):
        _, k, n = self.shapes[i]
        kind, dst, hk = self.kinds[i], self.dst[i], k // 2
        if kind == "slot":
            return dst.at[chip, pl.ds(pl.multiple_of(half * hk, BF16_ROWS), hk), :]
        if kind == "row":
            return dst.at[pl.ds(pl.multiple_of(chip * k + half * hk, BF16_ROWS), hk), :]
        col0 = chip * n if kind == "col" else chip * (n + HEAD_DIM) + HEAD_DIM * (chip % 2)
        return dst.at[pl.ds(pl.multiple_of(half * hk, BF16_ROWS), hk), pl.ds(pl.multiple_of(col0, LANES), n)]

    def _copy(self, sem, window, to, source=None):
        return pltpu.make_async_remote_copy(src_ref=window if source is None else source, dst_ref=window,
                                            send_sem=self.send_sems.at[sem], recv_sem=self.recv_sems.at[sem],
                                            device_id=to, device_id_type=MESH)

    def sends(self):
        out = []
        for k, chip in enumerate(self.chips):
            for i in range(self.n):
                hk = self.shapes[i][1] // 2
                mine = self.src[i].at[self.layer, pl.ds(pl.multiple_of(self.c * hk, BF16_ROWS), hk), :]
                out.append(self._copy(3 * i + k, self.half(i, self.mine, self.c), (*chip, self.c), source=mine))
        return out

    def arrivals(self):
        return [self._copy(3 * i + k, self.half(i, _chip_no(chip), self.c), (*chip, self.c))
                for k, chip in enumerate(self.chips) for i in range(self.n)]

    def forwards(self, first_sem):
        sibling = (self.x, self.y, 1 - self.c)
        return [self._copy(first_sem + 3 * i + k, self.half(i, _chip_no(chip), self.c), sibling)
                for k, chip in enumerate(self.chips) for i in range(self.n)]

    def forwarded(self, first_sem):
        sibling = (self.x, self.y, 1 - self.c)
        return [self._copy(first_sem + 3 * i + k, self.half(i, _chip_no(chip), 1 - self.c), sibling)
                for k, chip in enumerate(self.chips) for i in range(self.n)]


def _gather_layer(shards, kinds, fulls, layer, *, name):
    n_w = len(shards)
    shapes = [sh.shape for sh in shards]

    def body(*refs):
        plan = _GatherPlan(refs[:n_w], refs[2 * n_w:3 * n_w], shapes, kinds, layer, *refs[3 * n_w:])
        sends = plan.sends()
        for cp in sends:
            cp.start()
        passed = plan.forwards(3 * n_w)
        for landed, onward in zip(plan.arrivals(), passed):
            landed.wait_recv()
            onward.start()
        for cp in plan.forwarded(3 * n_w):
            cp.wait_recv()
        for cp in sends + passed:
            cp.wait_send()

    return pl.pallas_call(
        body, name=name, in_specs=[HBM] * (2 * n_w), out_specs=[HBM] * n_w,
        out_shape=[jax.ShapeDtypeStruct(f.shape, f.dtype) for f in fulls],
        input_output_aliases={n_w + i: i for i in range(n_w)},
        scratch_shapes=[pltpu.SemaphoreType.DMA((6 * n_w,)), pltpu.SemaphoreType.DMA((6 * n_w,))],
    )(*shards, *fulls)


IN_HBM = pl.BlockSpec(memory_space=pltpu.HBM)
IN_SEM = pl.BlockSpec(memory_space=pltpu.SEMAPHORE)
DATAFLOW = pltpu.SideEffectType.DATAFLOW_SIDE_EFFECTING


def _gather_layer_start(shards, kinds, fulls, layer, after, *, name):
    n_w = len(shards)
    shapes = [sh.shape for sh in shards]

    def body(*refs):
        plan = _GatherPlan(refs[:n_w], refs[n_w:2 * n_w], shapes, kinds, layer, refs[2 * n_w + 1], refs[2 * n_w + 2])
        for cp in plan.sends():
            cp.start()
        token = refs[-1]
        token[...] = jnp.zeros_like(token)

    operands = [pltpu.with_memory_space_constraint(a, pltpu.HBM) for a in (*shards, *fulls)]
    res = pl.pallas_call(
        body, name=name, in_specs=[IN_HBM] * (2 * n_w) + [pl.BlockSpec(memory_space=pl.ANY)],
        out_specs=(IN_SEM, IN_SEM, *([IN_HBM] * (2 * n_w)), pl.BlockSpec(memory_space=pltpu.VMEM)),
        out_shape=(pltpu.SemaphoreType.DMA((3 * n_w,)), pltpu.SemaphoreType.DMA((3 * n_w,)),
                   *[pltpu.HBM(a.shape, a.dtype) for a in operands], jax.ShapeDtypeStruct((8, LANES), F32)),
        input_output_aliases={i: 2 + i for i in range(2 * n_w)},
        compiler_params=pltpu.CompilerParams(has_side_effects=DATAFLOW),
    )(*operands, after)
    return res[0], res[1], res[2:2 + n_w], res[2 + n_w:2 + 2 * n_w], res[-1]


def _gather_layer_wait(send_sems, recv_sems, shards, fulls, kinds, layer, after, *, name):
    n_w = len(shards)
    shapes = [sh.shape for sh in shards]

    def body(*refs):
        plan = _GatherPlan(refs[:n_w], refs[n_w:2 * n_w], shapes, kinds, layer, refs[2 * n_w], refs[2 * n_w + 1])
        for cp in plan.sends():
            cp.wait_send()
        for cp in plan.arrivals():
            cp.wait_recv()

    res = pl.pallas_call(
        body, name=name, in_specs=[IN_HBM] * (2 * n_w) + [IN_SEM, IN_SEM, pl.BlockSpec(memory_space=pl.ANY)],
        out_specs=[IN_HBM] * (2 * n_w), out_shape=[pltpu.HBM(a.shape, a.dtype) for a in (*shards, *fulls)],
        input_output_aliases={i: i for i in range(2 * n_w)},
        compiler_params=pltpu.CompilerParams(has_side_effects=DATAFLOW),
    )(*shards, *fulls, send_sems, recv_sems, after)
    return res[n_w:]


def _gather_layer_forward(shapes, kinds, fulls, *, name):
    n_w = len(fulls)

    def body(*refs):
        plan = _GatherPlan([None] * n_w, refs[n_w:2 * n_w], shapes, kinds, 0, *refs[2 * n_w:])
        passed = plan.forwards(0)
        for cp in passed:
            cp.start()
        for cp in plan.forwarded(0):
            cp.wait_recv()
        for cp in passed:
            cp.wait_send()

    return pl.pallas_call(
        body, name=name, in_specs=[HBM] * n_w, out_specs=[HBM] * n_w,
        out_shape=[jax.ShapeDtypeStruct(f.shape, f.dtype) for f in fulls],
        input_output_aliases={i: i for i in range(n_w)},
        scratch_shapes=[pltpu.SemaphoreType.DMA((3 * n_w,)), pltpu.SemaphoreType.DMA((3 * n_w,))],
    )(*fulls)


def _grads_to_sibling(grads, *, name):
    n_w = len(grads)

    def body(*refs):
        src, dst = refs[:n_w], refs[n_w:2 * n_w]
        send_sems, recv_sems = refs[2 * n_w:]
        x, y, c = _my_place()
        cps = [pltpu.make_async_remote_copy(src_ref=src[i].at[1 - c], dst_ref=dst[i], send_sem=send_sems.at[i],
                                            recv_sem=recv_sems.at[i], device_id=(x, y, 1 - c), device_id_type=MESH)
               for i in range(n_w)]
        for cp in cps:
            cp.start()
        for cp in cps:
            cp.wait()

    return pl.pallas_call(
        body, name=name, in_specs=[HBM] * n_w, out_specs=[HBM] * n_w,
        out_shape=[jax.ShapeDtypeStruct(g.shape[1:], g.dtype) for g in grads],
        scratch_shapes=[pltpu.SemaphoreType.DMA((n_w,)), pltpu.SemaphoreType.DMA((n_w,))],
    )(*grads)


def _pair_add(mine2, other, *, name):
    _, k, n = mine2.shape
    tr = _div_tile(k, 512, BF16_ROWS)
    c = lax.axis_index("c").astype(jnp.int32).reshape(1)

    def body(c_ref, a_ref, b_ref, o_ref):
        o_ref[...] = (a_ref[...].astype(F32) + b_ref[...].astype(F32)).astype(o_ref.dtype)

    return pl.pallas_call(
        body, name=name,
        grid_spec=pltpu.PrefetchScalarGridSpec(
            num_scalar_prefetch=1, grid=(k // tr,),
            in_specs=[pl.BlockSpec((None, tr, n), lambda i, c_ref: (c_ref[0], i, 0)),
                      pl.BlockSpec((tr, n), lambda i, c_ref: (i, 0))],
            out_specs=pl.BlockSpec((tr, n), lambda i, c_ref: (i, 0))),
        out_shape=jax.ShapeDtypeStruct((k, n), mine2.dtype),
        compiler_params=_params(dimension_semantics=("parallel",)),
    )(c, mine2, other)


def _grads_to_chips(pairs, kinds, sizes, *, name):
    n_w = len(pairs)

    def shard_shape(p, kind, size):
        return {"col": (p.shape[0], size), "row": (size, p.shape[1]), "win_main": (p.shape[0], size),
                "win_strad": (p.shape[0], LANES)}[kind]

    def body(*refs):
        src, dst = refs[:n_w], refs[n_w:2 * n_w]
        send_sems, recv_sems = refs[2 * n_w:]
        x, y, c = _my_place()
        mine = 2 * x + y
        chips = _other_chips(x, y)

        def copy(i, k, chip, window_of, slab):
            return pltpu.make_async_remote_copy(src_ref=_window(src[i], kinds[i], sizes[i], window_of, ()),
                                                dst_ref=dst[i].at[slab], send_sem=send_sems.at[3 * i + k],
                                                recv_sem=recv_sems.at[3 * i + k], device_id=(*chip, c), device_id_type=MESH)

        sends = [copy(i, k, chip, _chip_no(chip), mine) for k, chip in enumerate(chips) for i in range(n_w)]
        for cp in sends:
            cp.start()
        for k, chip in enumerate(chips):
            for i in range(n_w):
                copy(i, k, chip, mine, _chip_no(chip)).wait_recv()
        for cp in sends:
            cp.wait_send()

    return pl.pallas_call(
        body, name=name, in_specs=[HBM] * n_w, out_specs=[HBM] * n_w,
        out_shape=[jax.ShapeDtypeStruct((N_CHIPS,) + shard_shape(p, kind, size), p.dtype)
                   for p, kind, size in zip(pairs, kinds, sizes)],
        scratch_shapes=[pltpu.SemaphoreType.DMA((3 * n_w,)), pltpu.SemaphoreType.DMA((3 * n_w,))],
    )(*pairs)


def _sum_slabs(slabs, pair, kind, size, *, name):
    n_s, k, n = slabs.shape
    tr = _div_tile(k, 512, BF16_ROWS)
    tc = n if kind in ("col", "row") else LANES
    x, y, c = _my_place()
    mine = 2 * x + y
    shard = size + HEAD_DIM
    row0 = mine * (k // tr) if kind == "row" else 0
    col0 = {"col": mine, "row": 0, "win_main": (mine * shard + HEAD_DIM * (mine % 2)) // LANES,
            "win_strad": (size + 2 * shard * (mine // 2)) // LANES}[kind]
    scalars = jnp.stack([c, mine, row0, col0]).astype(jnp.int32)

    def body(s_ref, slab_ref, own_ref, o_ref):
        me = s_ref[1]
        acc = jnp.zeros(o_ref.shape, F32)
        for i in range(n_s):
            acc = acc + jnp.where(me == i, own_ref[...], slab_ref[i]).astype(F32)
        o_ref[...] = acc

    return pl.pallas_call(
        body, name=name,
        grid_spec=pltpu.PrefetchScalarGridSpec(
            num_scalar_prefetch=1, grid=(k // tr, n // tc),
            in_specs=[pl.BlockSpec((n_s, tr, tc), lambda i, j, s: (0, i, j)),
                      pl.BlockSpec((tr, tc), lambda i, j, s: (s[2] + i, s[3] + j))],
            out_specs=pl.BlockSpec((None, tr, tc), lambda i, j, s: (s[0], i, j))),
        out_shape=jax.ShapeDtypeStruct((2, k, n), F32),
        compiler_params=_params(dimension_semantics=("parallel", "parallel")),
    )(scalars, slabs, pair)


def _exchange_layers(bufs, *, name):
    n_w = len(bufs)

    def body(*refs):
        dst = refs[n_w:2 * n_w]
        send_sems, recv_sems = refs[2 * n_w:]
        x, y, c = _my_place()

        def copy(i, layer):
            return pltpu.make_async_remote_copy(src_ref=dst[i].at[layer], dst_ref=dst[i].at[layer], send_sem=send_sems.at[i],
                                                recv_sem=recv_sems.at[i], device_id=(x, y, 1 - c), device_id_type=MESH)

        sends = [copy(i, c) for i in range(n_w)]
        for cp in sends:
            cp.start()
        for i in range(n_w):
            copy(i, 1 - c).wait_recv()
        for cp in sends:
            cp.wait_send()

    return pl.pallas_call(
        body, name=name, in_specs=[HBM] * n_w, out_specs=[HBM] * n_w,
        out_shape=[jax.ShapeDtypeStruct(b.shape, b.dtype) for b in bufs],
        input_output_aliases={i: i for i in range(n_w)},
        scratch_shapes=[pltpu.SemaphoreType.DMA((n_w,)), pltpu.SemaphoreType.DMA((n_w,))],
    )(*bufs)


def _all_sum_small(v, *, name):
    r = v.shape[0]
    relations = [(dx, dy, dc) for dx in (0, 1) for dy in (0, 1) for dc in (0, 1)][1:]

    def body(v_ref, o_ref, buf, send_sems, recv_sems):
        x, y, c = _my_place()
        me = 4 * x + 2 * y + c
        buf[me] = v_ref[...]
        peers = [(x + dx - 2 * x * dx, y + dy - 2 * y * dy, c + dc - 2 * c * dc) for dx, dy, dc in relations]

        def copy(k, slot):
            return pltpu.make_async_remote_copy(src_ref=v_ref, dst_ref=buf.at[slot], send_sem=send_sems.at[k],
                                                recv_sem=recv_sems.at[k], device_id=peers[k], device_id_type=MESH)

        sends = [copy(k, me) for k in range(len(relations))]
        for cp in sends:
            cp.start()
        for k, (px, py, pc) in enumerate(peers):
            copy(k, 4 * px + 2 * py + pc).wait_recv()
        for cp in sends:
            cp.wait_send()
        acc = buf[0]
        for i in range(1, 8):
            acc = acc + buf[i]
        o_ref[...] = acc

    vm = pl.BlockSpec(memory_space=pltpu.VMEM)
    return pl.pallas_call(
        body, name=name, in_specs=[vm], out_specs=vm, out_shape=jax.ShapeDtypeStruct((r, LANES), F32),
        scratch_shapes=[pltpu.VMEM((8, r, LANES), F32), pltpu.SemaphoreType.DMA((7,)), pltpu.SemaphoreType.DMA((7,))],
    )(v)


SHARDED = (("ffn1_w_up", "col"), ("ffn1_w_down", "row"), ("w_in", "win"), ("w_branch_a", "col"),
           ("w_branch_b", "col"), ("w_out", "row"), ("ffn2_w_up", "col"), ("ffn2_w_down", "row"))
REPLICATED = ("ffn1_norm", "mix_norm", "na_rel_bias", "ffn2_norm", "final_norm")


def _weight_pieces(w):
    even = lax.axis_index("y") == 0
    shards, kinds, names = [], [], []
    for name, kind in SHARDED:
        wb = w[name].astype(BF16)
        if kind == "win":
            main = wb.shape[-1] - HEAD_DIM
            assert main % LANES == 0
            zeros = jnp.zeros(wb.shape[:-1] + (HEAD_DIM,), BF16)
            shards += [jnp.where(even, wb[..., :main], wb[..., HEAD_DIM:]),
                       jnp.where(even, jnp.concatenate([wb[..., main:], zeros], -1),
                                 jnp.concatenate([zeros, wb[..., :HEAD_DIM]], -1))]
            kinds += ["win_main", "slot"]
            names += [name, name + "_strad"]
        else:
            shards.append(wb)
            kinds.append(kind)
            names.append(name)
    return names, kinds, shards


def _finish_w_in(full):
    full = dict(full)
    strad = full.pop("w_in_strad")
    main = full["w_in"].shape[1] // N_CHIPS - HEAD_DIM
    for i in range(N_CHIPS // 2):
        lo = main + 2 * (main + HEAD_DIM) * i
        full["w_in"] = full["w_in"].at[:, lo:lo + LANES].set(strad[2 * i] + strad[2 * i + 1])
    return full


def _reduce_weight_grads(grads, shards):
    names, kinds, sizes, srcs = [], [], [], []
    for name, kind in SHARDED:
        shp = shards[name].shape
        if kind == "win":
            names += [name, name + "_strad"]
            kinds += ["win_main", "win_strad"]
            sizes += [shp[2] - HEAD_DIM] * 2
            srcs += [name, name]
        else:
            names.append(name)
            kinds.append(kind)
            sizes.append(shp[1] if kind == "row" else shp[2])
            srcs.append(name)
    uniq = [name for name, _ in SHARDED]
    arrived = dict(zip(uniq, _grads_to_sibling([grads[n] for n in uniq], name="grads_to_sibling")))
    pair = {n: _pair_add(grads[n], arrived[n], name=f"grads_pair_{n}") for n in uniq}
    slabs = _grads_to_chips([pair[s] for s in srcs], kinds, sizes, name="grads_to_chips")
    halves = [_sum_slabs(sl, pair[s], kind, size, name=f"grads_sum_{n}")
              for n, sl, s, kind, size in zip(names, slabs, srcs, kinds, sizes)]
    out = dict(zip(names, _exchange_layers(halves, name="grads_layers")))
    strad = out.pop("w_in_strad")
    even = lax.axis_index("y") == 0
    out["w_in"] = jnp.where(even, jnp.concatenate([out["w_in"], strad[..., :HEAD_DIM]], -1),
                            jnp.concatenate([strad[..., HEAD_DIM:], out["w_in"]], -1))
    return out


class _Grads:
    def __init__(self, depth):
        self.depth = depth
        self.arrays = {}

    def put(self, weight, layer, a, b, *, cols=None, col_off=0, **kw):
        self.arrays[weight] = _mm(a, b, mode="tn", out_dtype=BF16, out_slab=(layer, self.depth), out_cols=cols,
                                  out_col_off=col_off, out_into=self.arrays.get(weight), **kw)


def _ffn_fwd(x, norm_g, w_up, w_down, tag):
    t, d = x.shape
    f = w_down.shape[0]
    h = _rms_fwd(x, norm_g, tt=512, name=f"{tag}_norm")
    a, gate, up = _mm_swiglu_fwd(h, w_up, tm=_div_tile(t, ROWS_NARROW, 8), tn=MXU_N, name=f"{tag}_up")
    x_out = _mm(a, w_down, mode="nn", out_dtype=F32, tm=_div_tile(t, ROWS_WIDE, 8), tn=d, tk=f, alpha=0.5, res=x, name=f"{tag}_down")
    return x_out, (x, h, a, gate, up)


def _ffn_bwd(dx, dxb, saved, norm_g, w_up, w_down, layer, grads, wname, tag):
    x, h, a, gate, up = saved
    t, d = x.shape
    f = w_down.shape[0]
    tn = _div_tile(f, 1408)
    grads.put(f"{wname}_w_down", layer, a, dxb, tm=tn, tn=d, tk=1024, alpha=0.5, name=f"{tag}_dwd")
    d_gate, d_up = _mm_swiglu_bwd(dxb, w_down, gate, up, alpha=0.5, tm=_div_tile(t, ROWS_NARROW, 8), tn=MXU_N, name=f"{tag}_da")
    grads.put(f"{wname}_w_up", layer, h, d_gate, cols=2 * f, tm=d, tn=tn, tk=1024, name=f"{tag}_dwg")
    grads.put(f"{wname}_w_up", layer, h, d_up, cols=2 * f, col_off=f // tn, tm=d, tn=tn, tk=1024, name=f"{tag}_dwu")
    dh = _mm(d_gate, w_up, mode="nt", out_dtype=F32, tm=_div_tile(t, ROWS_WIDE, 8), tn=d, tk=f, name=f"{tag}_dh1")
    dh = _mm(d_up, w_up, mode="nt", out_dtype=F32, tm=_div_tile(t, ROWS_WIDE, 8), tn=d, tk=f, b_k_off=1, res=dh, name=f"{tag}_dh2")
    return _rms_bwd(dh, x, norm_g, dx, tt=512, name=f"{tag}_dnorm")


def _to_heads(y, b, n_heads):
    t, w = y.shape
    return y.reshape(b, t // b, n_heads, HEAD_DIM).transpose(0, 2, 1, 3)


def _from_heads(y):
    b, n, s, hd = y.shape
    return y.transpose(0, 2, 1, 3).reshape(b * s, n * hd)


N_QKV = 3 * (DIL_HEADS + NA_HEADS) * HEAD_DIM


def _mixer_fwd(x, b, norm_g, full, bias, tabs, tag):
    t, d = x.shape
    s = t // b
    n_in = full["w_in"].shape[1]
    h = _rms_fwd(x, norm_g, tt=512, name=f"{tag}_norm")
    proj = _mm(h, full["w_in"], mode="nn", out_dtype=F32, tm=_div_tile(t, ROWS_NARROW, 8), tn=MXU_N, tk=d, name=f"{tag}_in")
    heads = _split_heads(proj.reshape(b, s, -1), *tabs, n_pairs=N_QKV // LANES, rot_pairs=DIL_HEADS,
                         scale_ranges=((0, DIL_HEADS // 2), (3 * DIL_HEADS // 2, (3 * DIL_HEADS + NA_HEADS) // 2)),
                         name=f"{tag}_heads")
    ya, lse_a = _dil_attn_fwd(heads, name=f"{tag}_dil")
    yb, lse_b = _na_attn_fwd(heads, bias, first=3 * DIL_HEADS, name=f"{tag}_na")
    ya2, yb2 = _from_heads(ya), _from_heads(yb)
    z = _mm(ya2, full["w_branch_a"], mode="nn", out_dtype=F32, tm=_div_tile(t, ROWS_NARROW, 8), tn=MXU_N, tk=ya2.shape[1],
            out_slab=(0, 2), name=f"{tag}_za")
    z = _mm(yb2, full["w_branch_b"], mode="nn", out_dtype=F32, tm=_div_tile(t, ROWS_NARROW, 8), tn=MXU_N, tk=yb2.shape[1],
            out_slab=(1, 2), out_into=z, name=f"{tag}_zb")
    merged = _gate_fwd(proj, z, gate_col=N_QKV, tt=1024, name=f"{tag}_gate")
    x_out = _mm(merged, full["w_out"], mode="nn", out_dtype=F32, tm=_div_tile(t, ROWS_NARROW, 8), tn=MXU_N, tk=d, res=x, name=f"{tag}_out")
    return x_out, (x, h, proj, heads, ya, lse_a, yb, lse_b, ya2, yb2, z, merged)


def _mixer_bwd(dx, dob, b, saved, norm_g, full, layer, bias, tabs, grads, tag):
    x, h, proj, heads, ya, lse_a, yb, lse_b, ya2, yb2, z, merged = saved
    t, d = x.shape
    s = t // b
    n_in = full["w_in"].shape[1]
    grads.put("w_out", layer, merged, dob, tm=d, tn=d, tk=1024, name=f"{tag}_dwo")
    dm = _mm(dob, full["w_out"], mode="nt", out_dtype=F32, tm=_div_tile(t, ROWS_NARROW, 8), tn=MXU_N, tk=d, name=f"{tag}_dm")
    dz, dproj = _gate_bwd(dm, proj, z, gate_col=N_QKV, tt=1024, name=f"{tag}_dgate")
    grads.put("w_branch_a", layer, ya2, dz, b_sel=0, tm=ya2.shape[1], tn=d, tk=1024, name=f"{tag}_dwa")
    grads.put("w_branch_b", layer, yb2, dz, b_sel=1, tm=yb2.shape[1], tn=d, tk=1024, name=f"{tag}_dwb")
    dya = _mm(dz, full["w_branch_a"], mode="nt", out_dtype=F32, tm=_div_tile(t, ROWS_NARROW, 8), tn=MXU_N, tk=d, a_sel=0, name=f"{tag}_dya")
    dyb = _mm(dz, full["w_branch_b"], mode="nt", out_dtype=F32, tm=_div_tile(t, ROWS_NARROW, 8), tn=MXU_N, tk=d, a_sel=1, name=f"{tag}_dyb")
    d_dil = _dil_attn_bwd(heads, ya, lse_a, _to_heads(dya, b, DIL_GROUP_HEADS), name=f"{tag}_ddil")
    d_na, d_bias = _na_attn_bwd(heads, bias, yb, lse_b, _to_heads(dyb, b, NA_HEADS), first=3 * DIL_HEADS, name=f"{tag}_dna")
    dproj = _merge_heads(d_dil, *tabs, heads_per_row=DIL_GROUP_HEADS, rot_pairs=DIL_HEADS, scale_pairs=DIL_HEADS // 2,
                         dilated=True, out_cols=n_in, tile_off=0, into=dproj.reshape(b, s, n_in), name=f"{tag}_dheads_a")
    dproj = _merge_heads(d_na, *tabs, heads_per_row=NA_HEADS, rot_pairs=0, scale_pairs=NA_HEADS // 2, dilated=False,
                         out_cols=n_in, tile_off=3 * DIL_HEADS // 2, into=dproj, name=f"{tag}_dheads_b").reshape(t, n_in)
    grads.put("w_in", layer, h, dproj, tm=_div_tile(d, 512), tn=_div_tile(n_in, 2944), tk=1024, name=f"{tag}_dwin")
    dh = _mm(dproj, full["w_in"], mode="nt", out_dtype=F32, tm=_div_tile(t, ROWS_WIDE, 8), tn=d, tk=_div_tile(n_in, 2944), name=f"{tag}_dh")
    dx_in, dxb_in, d_norm = _rms_bwd(dh, x, norm_g, dx, tt=512, name=f"{tag}_dnorm")
    d_rb = _na_collapse_bias(d_bias, name=f"{tag}_dbias")
    return dx_in, dxb_in, d_norm, d_rb


def kernel(x, ffn1_norm, ffn1_w_up, ffn1_w_down, mix_norm, w_in, na_rel_bias, w_branch_a, w_branch_b, w_out, ffn2_norm, ffn2_w_up, ffn2_w_down, final_norm, loss_target, m_ffn1_norm, m_ffn1_w_up, m_ffn1_w_down, m_mix_norm, m_w_in, m_na_rel_bias, m_w_branch_a, m_w_branch_b, m_w_out, m_ffn2_norm, m_ffn2_w_up, m_ffn2_w_down, m_final_norm, v_ffn1_norm, v_ffn1_w_up, v_ffn1_w_down, v_mix_norm, v_w_in, v_na_rel_bias, v_w_branch_a, v_w_branch_b, v_w_out, v_ffn2_norm, v_ffn2_w_up, v_ffn2_w_down, v_final_norm):
    w = dict(ffn1_norm=ffn1_norm, ffn1_w_up=ffn1_w_up, ffn1_w_down=ffn1_w_down, mix_norm=mix_norm, w_in=w_in,
             na_rel_bias=na_rel_bias, w_branch_a=w_branch_a, w_branch_b=w_branch_b, w_out=w_out, ffn2_norm=ffn2_norm,
             ffn2_w_up=ffn2_w_up, ffn2_w_down=ffn2_w_down, final_norm=final_norm)
    mom = dict(ffn1_norm=m_ffn1_norm, ffn1_w_up=m_ffn1_w_up, ffn1_w_down=m_ffn1_w_down, mix_norm=m_mix_norm, w_in=m_w_in,
               na_rel_bias=m_na_rel_bias, w_branch_a=m_w_branch_a, w_branch_b=m_w_branch_b, w_out=m_w_out,
               ffn2_norm=m_ffn2_norm, ffn2_w_up=m_ffn2_w_up, ffn2_w_down=m_ffn2_w_down, final_norm=m_final_norm)
    var = dict(ffn1_norm=v_ffn1_norm, ffn1_w_up=v_ffn1_w_up, ffn1_w_down=v_ffn1_w_down, mix_norm=v_mix_norm, w_in=v_w_in,
               na_rel_bias=v_na_rel_bias, w_branch_a=v_w_branch_a, w_branch_b=v_w_branch_b, w_out=v_w_out,
               ffn2_norm=v_ffn2_norm, ffn2_w_up=v_ffn2_w_up, ffn2_w_down=v_ffn2_w_down, final_norm=v_final_norm)
    b, s, d = x.shape
    t = b * s
    depth = ffn1_norm.shape[0]
    assert depth == 2, "core c of a chip sends / reduces layer c"
    shards = {name: w[name] for name, _ in SHARDED}

    names, kinds, pieces = _weight_pieces(w)
    by_layer = [[p[l:l + 1] for p in pieces] for l in range(depth)]
    shapes = [p.shape for p in by_layer[0]]
    own = [[_place_own(p, kind, 0, name=f"own{l}_{nm}") for nm, kind, p in zip(names, kinds, by_layer[l])] for l in range(depth)]
    gathered = _gather_layer(by_layer[0], kinds, own[0], 0, name="gather_l0")
    full = [_finish_w_in(zip(names, gathered)), None]
    send_sems, recv_sems, in_flight, landing, token = _gather_layer_start(by_layer[1], kinds, own[1], 0, gathered[0],
                                                                          name="gather_l1_start")
    tabs = _rope_tables(s)
    bias = _na_expand_bias(na_rel_bias, name="na_bias")

    xc = x.reshape(t, d)
    saved = []
    for l in range(depth):
        gain = ffn1_norm[l:l + 1]
        if l == 0:
            gain = gain + token[:1, :1]
        else:
            landed = _gather_layer_wait(send_sems, recv_sems, in_flight, landing, kinds, 0, xc, name="gather_l1_wait")
            full[1] = _finish_w_in(zip(names, _gather_layer_forward(shapes, kinds, landed, name="gather_l1_forward")))
        xc, s1 = _ffn_fwd(xc, gain, full[l]["ffn1_w_up"], full[l]["ffn1_w_down"], f"l{l}_ffn1")
        xc, s2 = _mixer_fwd(xc, b, mix_norm[l:l + 1], full[l], bias[l], tabs, f"l{l}_mix")
        xc, s3 = _ffn_fwd(xc, ffn2_norm[l:l + 1], full[l]["ffn2_w_up"], full[l]["ffn2_w_down"], f"l{l}_ffn2")
        saved.append((s1, s2, s3))

    dx, dxb, d_final, loss_part = _final_loss(xc, final_norm.reshape(1, d), loss_target.reshape(t, d), tt=512, name="final_loss")
    grads = _Grads(depth)
    small = {name: [None] * depth for name in REPLICATED[:-1]}
    for l in reversed(range(depth)):
        s1, s2, s3 = saved[l]
        dx, dxb, small["ffn2_norm"][l] = _ffn_bwd(dx, dxb, s3, ffn2_norm[l:l + 1], full[l]["ffn2_w_up"], full[l]["ffn2_w_down"],
                                                  l, grads, "ffn2", f"l{l}_ffn2")
        dx, dxb, small["mix_norm"][l], small["na_rel_bias"][l] = _mixer_bwd(
            dx, dxb, b, s2, mix_norm[l:l + 1], full[l], l, bias[l], tabs, grads, f"l{l}_mix")
        dx, dxb, small["ffn1_norm"][l] = _ffn_bwd(dx, dxb, s1, ffn1_norm[l:l + 1], full[l]["ffn1_w_up"], full[l]["ffn1_w_down"],
                                                  l, grads, "ffn1", f"l{l}_ffn1")
    grad_x = dx.reshape(b, s, d)

    g_out = _reduce_weight_grads(grads.arrays, shards)
    parts = [jnp.stack(small[name]).reshape(-1) for name in REPLICATED[:-1]] + [d_final.reshape(-1), loss_part[0, :1]]
    sizes = [v.shape[0] for v in parts]
    flat = jnp.concatenate(parts)
    flat = jnp.pad(flat, (0, -flat.shape[0] % (8 * LANES)))
    small_sum = _all_sum_small(flat.reshape(-1, LANES), name="small_all_sum").reshape(-1)
    off = 0
    for name, n in zip(REPLICATED, sizes[:-1]):
        g_out[name] = small_sum[off:off + n].reshape(w[name].shape)
        off += n
    loss = small_sum[off]

    names = list(w)
    delta, new_m, new_v = {}, {}, {}
    for name in names:
        delta[name], new_m[name], new_v[name] = _adamw(w[name], g_out[name], mom[name], var[name], name=f"adamw_{name}")
    return (loss, grad_x, *[g_out[n] for n in names], *[delta[n] for n in names], *[new_m[n] for n in names],
            *[new_v[n] for n in names])
```

```python
import functools

import numpy as np
import jax
import jax.numpy as jnp
from jax import lax
from jax.experimental import pallas as pl
from jax.experimental.pallas import tpu as pltpu

F32, BF16 = jnp.float32, jnp.bfloat16
MESH = pl.DeviceIdType.MESH

HEAD_DIM = 64
DILATIONS = (1, 4, 16)
DIL_HALF = 64
DIL_GROUP_HEADS = 4
DIL_HEADS = 12
NA_HEADS = 8
GRID_W = 64
NA_ROWS = 8
NA_COLS = 16
ROPE_THETA = 10000.0
RMS_EPS = 1e-6
NEG_INF = -1e30
ADAM_LR, ADAM_B1, ADAM_B2, ADAM_EPS, ADAM_WD, ADAM_STEP = 0.001, 0.9, 0.999, 1e-08, 0.01, 10
QK_SCALE = HEAD_DIM ** -0.5

N_CHIPS = 4
LANES = 128
BF16_ROWS = 16
VMEM_LIMIT = 56 * 1024 * 1024
MXU_N = 256
ROWS_NARROW = 2048
ROWS_WIDE = 512

_NN = (((1,), (0,)), ((), ()))
_NT = (((1,), (1,)), ((), ()))
_TN = (((0,), (0,)), ((), ()))

HBM = pl.BlockSpec(memory_space=pl.ANY)


def _params(**kw):
    return pltpu.CompilerParams(vmem_limit_bytes=VMEM_LIMIT, **kw)


def _dot(a, b, dims):
    return lax.dot_general(a, b, dims, preferred_element_type=F32)


def _div_tile(n, cap, mult=LANES):
    best = None
    for t in range(mult, min(n, cap) + 1, mult):
        if n % t == 0:
            best = t
    return n if best is None else best


def _stacked(block, index, sel):
    if sel is None:
        return pl.BlockSpec(block, index)
    return pl.BlockSpec((None,) + block, lambda *g: (sel,) + index(*g))


def _mm(a, b, *, mode, out_dtype, tm, tn, tk, name, alpha=1.0, res=None, a_sel=None, b_sel=None, b_k_off=0,
        out_slab=None, out_cols=None, out_col_off=0, out_into=None):
    a2, b2 = a.shape[-2:], b.shape[-2:]
    if mode == "nn":
        (m, k), n = a2, b2[1]
        a_spec = _stacked((tm, tk), lambda i, j, kk: (i, kk), a_sel)
        b_spec = _stacked((tk, tn), lambda i, j, kk: (kk + b_k_off, j), b_sel)
        dims = _NN
    elif mode == "nt":
        (m, k), n = a2, b2[0]
        a_spec = _stacked((tm, tk), lambda i, j, kk: (i, kk), a_sel)
        b_spec = _stacked((tn, tk), lambda i, j, kk: (j, kk + b_k_off), b_sel)
        dims = _NT
    else:
        (k, m), n = a2, b2[1]
        a_spec = _stacked((tk, tm), lambda i, j, kk: (kk, i), a_sel)
        b_spec = _stacked((tk, tn), lambda i, j, kk: (kk + b_k_off, j), b_sel)
        dims = _TN
    assert m % tm == 0 and n % tn == 0 and k % tk == 0, (name, a.shape, b.shape)
    nk = k // tk
    has_res = res is not None
    if out_slab is None:
        o_spec = pl.BlockSpec((tm, tn), lambda i, j, kk: (i, j + out_col_off))
        out_shape = jax.ShapeDtypeStruct((m, n if out_cols is None else out_cols), out_dtype)
    else:
        o_spec = _stacked((tm, tn), lambda i, j, kk: (i, j + out_col_off), out_slab[0])
        out_shape = jax.ShapeDtypeStruct((out_slab[1], m, n if out_cols is None else out_cols), out_dtype)
    r_spec = pl.BlockSpec((tm, tn), lambda i, j, kk: (i, j))
    n_in = 2 + has_res + (out_into is not None)

    def body(*refs):
        a_ref, b_ref = refs[0], refs[1]
        r_ref = refs[2] if has_res else None
        o_ref = refs[n_in]
        p = _dot(a_ref[...], b_ref[...], dims)

        def finish(acc):
            y = acc * alpha if alpha != 1.0 else acc
            if has_res:
                y = y + r_ref[...].astype(F32)
            o_ref[...] = y.astype(o_ref.dtype)

        if nk == 1:
            finish(p)
        else:
            acc_ref = refs[n_in + 1]
            kk = pl.program_id(2)

            @pl.when(kk == 0)
            def _():
                acc_ref[...] = p

            @pl.when(kk > 0)
            def _():
                acc_ref[...] += p

            @pl.when(kk == nk - 1)
            def _():
                finish(acc_ref[...])

    operands = [a, b] + ([res] if has_res else [])
    in_specs = [a_spec, b_spec] + ([r_spec] if has_res else [])
    aliases = {}
    if out_into is not None:
        aliases = {len(operands): 0}
        operands.append(out_into)
        in_specs.append(HBM)
    return pl.pallas_call(
        body, name=name, grid=(m // tm, n // tn, nk), in_specs=in_specs, out_specs=o_spec, out_shape=out_shape,
        scratch_shapes=[pltpu.VMEM((tm, tn), F32)] if nk > 1 else [], input_output_aliases=aliases,
        compiler_params=_params(dimension_semantics=("parallel", "parallel", "arbitrary")),
    )(*operands)


def _mm_swiglu_fwd(h, w_up, *, tm, tn, name):
    m, k = h.shape
    n = w_up.shape[1] // 2
    h_spec = pl.BlockSpec((tm, k), lambda i, j: (i, 0))
    wg_spec = pl.BlockSpec((k, tn), lambda i, j: (0, j))
    wu_spec = pl.BlockSpec((k, tn), lambda i, j: (0, j + n // tn))
    o_spec = pl.BlockSpec((tm, tn), lambda i, j: (i, j))

    def body(h_ref, wg_ref, wu_ref, a_ref, g_ref, u_ref):
        hb = h_ref[...]
        g = _dot(hb, wg_ref[...], _NN)
        u = _dot(hb, wu_ref[...], _NN)
        a_ref[...] = (g * jax.nn.sigmoid(g) * u).astype(BF16)
        g_ref[...] = g.astype(BF16)
        u_ref[...] = u.astype(BF16)

    out = jax.ShapeDtypeStruct((m, n), BF16)
    return pl.pallas_call(
        body, name=name, grid=(m // tm, n // tn), in_specs=[h_spec, wg_spec, wu_spec],
        out_specs=[o_spec] * 3, out_shape=[out] * 3,
        compiler_params=_params(dimension_semantics=("parallel", "parallel")),
    )(h, w_up, w_up)


def _mm_swiglu_bwd(dy, w_down, gate, up, *, alpha, tm, tn, name):
    m, k = dy.shape
    n = w_down.shape[0]
    dy_spec = pl.BlockSpec((tm, k), lambda i, j: (i, 0))
    w_spec = pl.BlockSpec((tn, k), lambda i, j: (j, 0))
    o_spec = pl.BlockSpec((tm, tn), lambda i, j: (i, j))

    def body(dy_ref, w_ref, g_ref, u_ref, dg_ref, du_ref):
        da = _dot(dy_ref[...], w_ref[...], _NT) * alpha
        g = g_ref[...].astype(F32)
        u = u_ref[...].astype(F32)
        sg = jax.nn.sigmoid(g)
        dg_ref[...] = (da * u * (sg * (1.0 + g * (1.0 - sg)))).astype(BF16)
        du_ref[...] = (da * (g * sg)).astype(BF16)

    out = jax.ShapeDtypeStruct((m, n), BF16)
    return pl.pallas_call(
        body, name=name, grid=(m // tm, n // tn), in_specs=[dy_spec, w_spec, o_spec, o_spec],
        out_specs=[o_spec] * 2, out_shape=[out] * 2,
        compiler_params=_params(dimension_semantics=("parallel", "parallel")),
    )(dy, w_down, gate, up)


def _rms_fwd(x, g, *, tt, name):
    t, d = x.shape

    def body(x_ref, g_ref, h_ref):
        xv = x_ref[...]
        rstd = lax.rsqrt(jnp.mean(xv * xv, axis=1, keepdims=True) + RMS_EPS)
        h_ref[...] = (xv * rstd * g_ref[...]).astype(BF16)

    return pl.pallas_call(
        body, name=name, grid=(t // tt,),
        in_specs=[pl.BlockSpec((tt, d), lambda i: (i, 0)), pl.BlockSpec((1, d), lambda i: (0, 0))],
        out_specs=pl.BlockSpec((tt, d), lambda i: (i, 0)), out_shape=jax.ShapeDtypeStruct((t, d), BF16),
        compiler_params=_params(dimension_semantics=("parallel",)),
    )(x, g)


def _rms_bwd(dh, x, g, dres, *, tt, name):
    t, d = x.shape

    def body(dh_ref, x_ref, g_ref, r_ref, dx_ref, dxb_ref, dg_ref):
        xv = x_ref[...]
        rstd = lax.rsqrt(jnp.mean(xv * xv, axis=1, keepdims=True) + RMS_EPS)
        xhat = xv * rstd
        dhv = dh_ref[...]
        dxhat = dhv * g_ref[...]
        dx = r_ref[...] + rstd * (dxhat - xhat * jnp.mean(dxhat * xhat, axis=1, keepdims=True))
        dx_ref[...] = dx
        dxb_ref[...] = dx.astype(BF16)

        @pl.when(pl.program_id(0) == 0)
        def _():
            dg_ref[...] = jnp.zeros_like(dg_ref)

        dg_ref[...] += jnp.sum(dhv * xhat, axis=0, keepdims=True)

    row = pl.BlockSpec((tt, d), lambda i: (i, 0))
    vec = pl.BlockSpec((1, d), lambda i: (0, 0))
    return pl.pallas_call(
        body, name=name, grid=(t // tt,), in_specs=[row, row, vec, row], out_specs=[row, row, vec],
        out_shape=[jax.ShapeDtypeStruct((t, d), F32), jax.ShapeDtypeStruct((t, d), BF16), jax.ShapeDtypeStruct((1, d), F32)],
        compiler_params=_params(dimension_semantics=("arbitrary",)),
    )(dh, x, g, dres)


def _final_loss(x, g, target, *, tt, name):
    t, d = x.shape

    def body(x_ref, g_ref, t_ref, dx_ref, dxb_ref, dg_ref, loss_ref):
        xv = x_ref[...]
        gv = g_ref[...]
        rstd = lax.rsqrt(jnp.mean(xv * xv, axis=1, keepdims=True) + RMS_EPS)
        xhat = xv * rstd
        err = xhat * gv - t_ref[...]
        dy = err * (1.0 / d)
        dxhat = dy * gv
        dx = rstd * (dxhat - xhat * jnp.mean(dxhat * xhat, axis=1, keepdims=True))
        dx_ref[...] = dx
        dxb_ref[...] = dx.astype(BF16)

        @pl.when(pl.program_id(0) == 0)
        def _():
            dg_ref[...] = jnp.zeros_like(dg_ref)
            loss_ref[...] = jnp.zeros_like(loss_ref)

        dg_ref[...] += jnp.sum(dy * xhat, axis=0, keepdims=True)
        part = 0.5 * jnp.sum(jnp.mean(err * err, axis=1, keepdims=True), axis=0, keepdims=True)
        loss_ref[...] += jnp.broadcast_to(part, loss_ref.shape)

    row = pl.BlockSpec((tt, d), lambda i: (i, 0))
    vec = pl.BlockSpec((1, d), lambda i: (0, 0))
    one = pl.BlockSpec((1, LANES), lambda i: (0, 0))
    return pl.pallas_call(
        body, name=name, grid=(t // tt,), in_specs=[row, vec, row], out_specs=[row, row, vec, one],
        out_shape=[jax.ShapeDtypeStruct((t, d), F32), jax.ShapeDtypeStruct((t, d), BF16), jax.ShapeDtypeStruct((1, d), F32),
                   jax.ShapeDtypeStruct((1, LANES), F32)],
        compiler_params=_params(dimension_semantics=("arbitrary",)),
    )(x, g, target)


def _swap_halves(x):
    lane = lax.broadcasted_iota(jnp.int32, x.shape, 1)
    return jnp.where((lane // 32) % 2 == 0, pltpu.roll(x, 96, 1), pltpu.roll(x, 32, 1))


def _rope_tables(s):
    half = HEAD_DIM // 2
    inv_freq = ROPE_THETA ** (-jnp.arange(half, dtype=F32) / half)
    ang = jnp.arange(s).astype(F32)[:, None] * inv_freq[None, :]
    cos, sin = jnp.cos(ang), jnp.sin(ang)
    return jnp.tile(cos, (1, 4)), jnp.concatenate([-sin, sin, -sin, sin], axis=1)


def _dilation_of_tile(p):
    dilated = p < 3 * DIL_HEADS // 2
    g = (p % (DIL_HEADS // 2)) // (DIL_GROUP_HEADS // 2)
    return [(dilated & (g == gi)) | (jnp.logical_not(dilated) if gi == 0 else False) for gi in range(len(DILATIONS))]


def _residue_major(ref, d):
    s = ref.shape[0]
    if d == 1:
        return ref[...]
    return jnp.concatenate([ref[pl.ds(r, s // d, stride=d), :] for r in range(d)], axis=0)


def _split_heads(proj, cos4, sin4, *, n_pairs, rot_pairs, scale_ranges, name):
    b, s, _ = proj.shape

    def body(x_ref, c_ref, s_ref, o_ref):
        p = pl.program_id(1)
        is_q = functools.reduce(jnp.logical_or, [(p >= lo) & (p < hi) for lo, hi in scale_ranges])
        scale = jnp.where(is_q, QK_SCALE, 1.0)

        def put(y):
            o_ref[0] = y[:, :HEAD_DIM].astype(BF16)
            o_ref[1] = y[:, HEAD_DIM:].astype(BF16)

        for d, in_group in zip(DILATIONS, _dilation_of_tile(p)):
            @pl.when(in_group & (p < rot_pairs))
            def _(d=d):
                x = _residue_major(x_ref, d)
                put((x * _residue_major(c_ref, d) + _swap_halves(x) * _residue_major(s_ref, d)) * scale)

            @pl.when(in_group & (p >= rot_pairs))
            def _(d=d):
                put(_residue_major(x_ref, d) * scale)

    tab = pl.BlockSpec((s, LANES), lambda bi, p: (0, 0))
    return pl.pallas_call(
        body, name=name, grid=(b, n_pairs),
        in_specs=[pl.BlockSpec((None, s, LANES), lambda bi, p: (bi, 0, p)), tab, tab],
        out_specs=pl.BlockSpec((None, 2, s, HEAD_DIM), lambda bi, p: (bi, p, 0, 0)),
        out_shape=jax.ShapeDtypeStruct((b, 2 * n_pairs, s, HEAD_DIM), BF16),
        compiler_params=_params(dimension_semantics=("parallel", "parallel")),
    )(proj, cos4, sin4)


def _merge_heads(dheads, cos4, sin4, *, heads_per_row, rot_pairs, scale_pairs, dilated, out_cols, tile_off, into, name):
    b, hpr, r, s, _ = dheads.shape
    n_pairs = hpr * r // 2
    ppr = hpr // 2

    def body(d_ref, c_ref, s_ref, *rest):
        o_ref, t_ref = rest[-2:]
        p = pl.program_id(1)
        scale = jnp.where(p < scale_pairs, QK_SCALE, 1.0)

        def tokens(d):
            dy = jnp.concatenate([d_ref[0], d_ref[1]], axis=1)
            if d == 1:
                return dy
            for res in range(d):
                t_ref[pl.ds(res, s // d, stride=d), :] = dy[res * (s // d):(res + 1) * (s // d), :]
            return t_ref[...]

        groups = _dilation_of_tile(p) if dilated else [p >= 0]
        for d, in_group in zip(DILATIONS, groups):
            @pl.when(in_group & (p < rot_pairs))
            def _(d=d):
                dy = tokens(d)
                o_ref[...] = ((dy * c_ref[...] - _swap_halves(dy) * s_ref[...]) * scale).astype(BF16)

            @pl.when(in_group & (p >= rot_pairs))
            def _(d=d):
                o_ref[...] = (tokens(d) * scale).astype(BF16)

    tab = pl.BlockSpec((s, LANES), lambda bi, p: (0, 0))
    operands = [dheads, cos4, sin4] + ([] if into is None else [into])
    return pl.pallas_call(
        body, name=name, grid=(b, n_pairs),
        in_specs=[pl.BlockSpec((None, 2, None, s, HEAD_DIM), lambda bi, p: (bi, p % ppr, p // ppr, 0, 0)), tab, tab]
        + ([] if into is None else [HBM]),
        out_specs=pl.BlockSpec((None, s, LANES), lambda bi, p: (bi, 0, p + tile_off)),
        out_shape=jax.ShapeDtypeStruct((b, s, out_cols), BF16),
        input_output_aliases={} if into is None else {3: 0},
        scratch_shapes=[pltpu.VMEM((s, LANES), F32)],
        compiler_params=_params(dimension_semantics=("parallel", "parallel")),
    )(*operands)


DIL_TQ = 256


def _dil_block(g, s):
    run = s // DILATIONS[g]
    return DIL_TQ if run <= DIL_TQ else min(run, DIL_TQ + 2 * LANES)


def _dil_keys(g, q0, s):
    run = max(s // DILATIONS[g], DIL_TQ)
    lo = (q0 // run) * run
    return pl.multiple_of(jnp.clip(q0 - LANES, lo, lo + run - _dil_block(g, s)), LANES)


def _dil_band(g, q0, start, shape, s):
    row = q0 + lax.broadcasted_iota(jnp.int32, shape, 0)
    col = start + lax.broadcasted_iota(jnp.int32, shape, 1)
    ok = jnp.abs(row - col) <= DIL_HALF
    run = s // DILATIONS[g]
    if run < DIL_TQ:
        shift = run.bit_length() - 1
        ok = ok & ((row >> shift) == (col >> shift))
    return ok


def _dil_tokens(g, q0, s):
    d = DILATIONS[g]
    if d == 1:
        return [(0, DIL_TQ, pl.ds(q0, DIL_TQ))]
    run = s // d
    n = min(run, DIL_TQ)
    return [(lo, n, pl.ds(((q0 + lo) % run) * d + (q0 + lo) // run, n, stride=d)) for lo in range(0, DIL_TQ, n)]


def _dil_gather(ref, pieces):
    return jnp.concatenate([ref[rows, :] for _, _, rows in pieces], axis=0) if len(pieces) > 1 else ref[pieces[0][2], :]


def _dil_head_spec(part, g, s):
    return pl.BlockSpec((None, None, s, HEAD_DIM), lambda b, j: (b, part * DIL_HEADS + g * DIL_GROUP_HEADS + j, 0, 0))


def _dil_attn_fwd(heads, *, name):
    b, _, s, _ = heads.shape
    n_g = len(DILATIONS)

    def body(*refs):
        qkv = refs[:3 * n_g]
        o_ref, l_ref, og_ref, lg_ref = refs[3 * n_g:]
        for g in range(n_g):
            q_ref, k_ref, v_ref = qkv[3 * g:3 * g + 3]
            width = _dil_block(g, s)

            def step(i, carry, g=g, q_ref=q_ref, k_ref=k_ref, v_ref=v_ref, width=width):
                q0 = pl.multiple_of(i * DIL_TQ, DIL_TQ)
                start = _dil_keys(g, q0, s)
                sc = _dot(q_ref[pl.ds(q0, DIL_TQ), :], k_ref[pl.ds(start, width), :], _NT)
                sc = jnp.where(_dil_band(g, q0, start, sc.shape, s), sc, NEG_INF)
                m = jnp.max(sc, axis=1, keepdims=True)
                p = jnp.exp(sc - m)
                den = jnp.sum(p, axis=1, keepdims=True)
                o = _dot(p.astype(BF16), v_ref[pl.ds(start, width), :], _NN) / den
                lse = m + jnp.log(den)
                for lo, n, rows in _dil_tokens(g, q0, s):
                    og_ref[g, rows, :] = o[lo:lo + n]
                    lg_ref[g, rows, :] = lse[lo:lo + n]
                return carry

            lax.fori_loop(0, s // DIL_TQ, step, 0)
        lses = [lg_ref[g] for g in range(n_g)]
        m = functools.reduce(jnp.maximum, lses)
        ws = [jnp.exp(l - m) for l in lses]
        den = functools.reduce(jnp.add, ws)
        o_ref[...] = (functools.reduce(jnp.add, [w * og_ref[g] for g, w in enumerate(ws)]) / den).astype(o_ref.dtype)
        l_ref[...] = m + jnp.log(den)

    out = pl.BlockSpec((None, None, s, HEAD_DIM), lambda bi, j: (bi, j, 0, 0))
    lse = pl.BlockSpec((None, None, s, 1), lambda bi, j: (bi, j, 0, 0))
    return pl.pallas_call(
        body, name=name, grid=(b, DIL_GROUP_HEADS),
        in_specs=[_dil_head_spec(part, g, s) for g in range(n_g) for part in range(3)],
        out_specs=[out, lse],
        out_shape=[jax.ShapeDtypeStruct((b, DIL_GROUP_HEADS, s, HEAD_DIM), BF16),
                   jax.ShapeDtypeStruct((b, DIL_GROUP_HEADS, s, 1), F32)],
        scratch_shapes=[pltpu.VMEM((n_g, s, HEAD_DIM), F32), pltpu.VMEM((n_g, s, 1), F32)],
        compiler_params=_params(dimension_semantics=("parallel", "parallel")),
    )(*([heads] * (3 * n_g)))


def _dil_attn_bwd(heads, out, lse, dout, *, name):
    b, _, s, _ = heads.shape
    n_g = len(DILATIONS)

    def body(*refs):
        qkv = refs[:3 * n_g]
        o_ref, l_ref, do_ref, d_ref, delta_ref = refs[3 * n_g:]
        d_ref[...] = jnp.zeros_like(d_ref)
        delta_ref[...] = jnp.sum(do_ref[...] * o_ref[...].astype(F32), axis=1, keepdims=True)
        for g in range(n_g):
            q_ref, k_ref, v_ref = qkv[3 * g:3 * g + 3]
            width = _dil_block(g, s)

            def step(i, carry, g=g, q_ref=q_ref, k_ref=k_ref, v_ref=v_ref, width=width):
                q0 = pl.multiple_of(i * DIL_TQ, DIL_TQ)
                start = _dil_keys(g, q0, s)
                win = pl.ds(start, width)
                pieces = _dil_tokens(g, q0, s)
                do_b = _dil_gather(do_ref, pieces).astype(BF16)
                q, k, v = q_ref[pl.ds(q0, DIL_TQ), :], k_ref[win, :], v_ref[win, :]
                sc = _dot(q, k, _NT)
                p = jnp.where(_dil_band(g, q0, start, sc.shape, s), jnp.exp(sc - _dil_gather(l_ref, pieces)), 0.0)
                ds = (p * (_dot(do_b, v, _NT) - _dil_gather(delta_ref, pieces))).astype(BF16)
                d_ref[g, pl.ds(q0, DIL_TQ), :] = _dot(ds, k, _NN)
                d_ref[n_g + g, win, :] += _dot(ds, q, _TN)
                d_ref[2 * n_g + g, win, :] += _dot(p.astype(BF16), do_b, _TN)
                return carry

            lax.fori_loop(0, s // DIL_TQ, step, 0)

    per_head = lambda bi, j: (bi, j, 0, 0)
    return pl.pallas_call(
        body, name=name, grid=(b, DIL_GROUP_HEADS),
        in_specs=[_dil_head_spec(part, g, s) for g in range(n_g) for part in range(3)]
        + [pl.BlockSpec((None, None, s, HEAD_DIM), per_head), pl.BlockSpec((None, None, s, 1), per_head),
           pl.BlockSpec((None, None, s, HEAD_DIM), per_head)],
        out_specs=pl.BlockSpec((None, None, 3 * n_g, s, HEAD_DIM), lambda bi, j: (bi, j, 0, 0, 0)),
        out_shape=jax.ShapeDtypeStruct((b, DIL_GROUP_HEADS, 3 * n_g, s, HEAD_DIM), F32),
        scratch_shapes=[pltpu.VMEM((s, 1), F32)],
        compiler_params=_params(dimension_semantics=("parallel", "parallel")),
    )(*([heads] * (3 * n_g)), out, lse, dout)


NA_BIAS_ROWS = 2 * NA_ROWS - 1
NA_BIAS_COLS = 2 * NA_COLS - 1
NA_BLOCK = 4
NA_SPAN = NA_ROWS + NA_BLOCK - 1
NA_Q = NA_BLOCK * GRID_W
NA_KEYS = NA_SPAN * GRID_W
NA_FORMS = 3


def _na_onehot():
    c = np.arange(GRID_W)[:, None]
    k = np.arange(GRID_W)[None, :]
    lo = np.clip(c - NA_COLS // 2, 0, GRID_W - NA_COLS)
    valid = (k >= lo) & (k < lo + NA_COLS)
    onehot = np.zeros((GRID_W, GRID_W, LANES), np.float32)
    cc, kk = np.nonzero(valid)
    onehot[cc, kk, kk - cc + NA_COLS - 1] = 1.0
    return onehot.reshape(GRID_W * GRID_W, LANES), valid.reshape(1, GRID_W * GRID_W)


def _na_block_rows(n_rows):
    table = np.full((NA_FORMS, NA_BLOCK, NA_SPAN), NA_BIAS_ROWS, np.int64)
    n_blocks = n_rows // NA_BLOCK
    for form, ib in enumerate((0, 1, n_blocks - 1)):
        base = min(max(NA_BLOCK * ib - NA_ROWS // 2, 0), n_rows - NA_SPAN)
        for rl in range(NA_BLOCK):
            r = NA_BLOCK * ib + rl
            row_lo = min(max(r - NA_ROWS // 2, 0), n_rows - NA_ROWS)
            for kl in range(NA_SPAN):
                if row_lo <= base + kl < row_lo + NA_ROWS:
                    table[form, rl, kl] = base + kl - r + NA_ROWS - 1
    return table


def _na_block(ib, n_rows):
    n_blocks = n_rows // NA_BLOCK
    base = jnp.clip(NA_BLOCK * ib - NA_ROWS // 2, 0, n_rows - NA_SPAN)
    return base, jnp.where(ib == 0, 0, jnp.where(ib == n_blocks - 1, 2, 1))


def _na_expand_bias(rel_bias, *, name):
    l, h, nr, nc = rel_bias.shape
    onehot, valid = _na_onehot()
    rb = jnp.pad(rel_bias, ((0, 0), (0, 0), (0, 1), (0, LANES - nc))).reshape(l * h * (nr + 1), LANES)
    live = jnp.asarray(np.tile(np.arange(nr + 1) < nr, l * h).astype(np.float32)[:, None])

    def body(rb_ref, oh_ref, valid_ref, live_ref, e_ref):
        e = lax.dot_general(rb_ref[...], oh_ref[...], _NT, precision=lax.Precision.HIGHEST, preferred_element_type=F32)
        e_ref[...] = jnp.where((valid_ref[...] > 0) & (live_ref[...] > 0), e, NEG_INF)

    e = pl.pallas_call(
        body, name=name, out_shape=jax.ShapeDtypeStruct((l * h * (nr + 1), GRID_W * GRID_W), F32), compiler_params=_params(),
    )(rb, jnp.asarray(onehot), jnp.asarray(valid.astype(np.float32)), live)
    return e.reshape(l, h, nr + 1, GRID_W, GRID_W)


def _na_collapse_bias(de, *, name):
    b, h = de.shape[:2]
    onehot, _ = _na_onehot()
    rows = h * NA_BIAS_ROWS

    def diag(e_ref, oh_ref, o_ref):
        e = e_ref[0]
        for bi in range(1, b):
            e = e + e_ref[bi]
        o_ref[...] = lax.dot_general(e, oh_ref[...], _NN, precision=lax.Precision.HIGHEST, preferred_element_type=F32)

    drb = pl.pallas_call(
        diag, name=name, out_shape=jax.ShapeDtypeStruct((rows, LANES), F32), compiler_params=_params(),
    )(de.reshape(b, rows, GRID_W * GRID_W), jnp.asarray(onehot))
    return drb[:, :NA_BIAS_COLS].reshape(h, NA_BIAS_ROWS, NA_BIAS_COLS)


def _na_tiles(n_rows):
    table = _na_block_rows(n_rows)
    return [(f, rl, kl, int(table[f, rl, kl])) for f in range(NA_FORMS) for rl in range(NA_BLOCK) for kl in range(NA_SPAN)]


def _na_tile(ref, form, rl, kl):
    return ref.at[form, rl * GRID_W:(rl + 1) * GRID_W, kl * GRID_W:(kl + 1) * GRID_W]


def _na_head_spec(part, first, s):
    return pl.BlockSpec((None, None, s, HEAD_DIM), lambda b, h: (b, first + part * NA_HEADS + h, 0, 0))


def _na_attn_fwd(heads, bias, *, first, name):
    b, _, s, _ = heads.shape
    n_rows = s // GRID_W
    tiles = _na_tiles(n_rows)

    def body(q_ref, k_ref, v_ref, e_ref, o_ref, l_ref, b_ref):
        for form, rl, kl, i in tiles:
            _na_tile(b_ref, form, rl, kl)[...] = e_ref[i]

        def step(ib, carry):
            base, form = _na_block(ib, n_rows)
            rows = pl.ds(pl.multiple_of(ib * NA_Q, NA_Q), NA_Q)
            win = pl.ds(pl.multiple_of(base * GRID_W, GRID_W), NA_KEYS)
            sc = _dot(q_ref[rows, :], k_ref[win, :], _NT) + b_ref[form]
            m = jnp.max(sc, axis=1, keepdims=True)
            p = jnp.exp(sc - m)
            den = jnp.sum(p, axis=1, keepdims=True)
            o_ref[rows, :] = (_dot(p.astype(BF16), v_ref[win, :], _NN) / den).astype(o_ref.dtype)
            l_ref[rows, :] = m + jnp.log(den)
            return carry

        lax.fori_loop(0, n_rows // NA_BLOCK, step, 0)

    per_head = lambda bi, h: (bi, h, 0, 0)
    return pl.pallas_call(
        body, name=name, grid=(b, NA_HEADS),
        in_specs=[_na_head_spec(part, first, s) for part in range(3)]
        + [pl.BlockSpec((None, NA_BIAS_ROWS + 1, GRID_W, GRID_W), lambda bi, h: (h, 0, 0, 0))],
        out_specs=[pl.BlockSpec((None, None, s, HEAD_DIM), per_head), pl.BlockSpec((None, None, s, 1), per_head)],
        out_shape=[jax.ShapeDtypeStruct((b, NA_HEADS, s, HEAD_DIM), BF16), jax.ShapeDtypeStruct((b, NA_HEADS, s, 1), F32)],
        scratch_shapes=[pltpu.VMEM((NA_FORMS, NA_Q, NA_KEYS), F32)],
        compiler_params=_params(dimension_semantics=("parallel", "parallel")),
    )(heads, heads, heads, bias)


def _na_attn_bwd(heads, bias, out, lse, dout, *, first, name):
    b, _, s, _ = heads.shape
    n_rows = s // GRID_W
    tiles = _na_tiles(n_rows)

    def body(q_ref, k_ref, v_ref, e_ref, o_ref, l_ref, do_ref, d_ref, de_ref, b_ref, db_ref):
        for form, rl, kl, i in tiles:
            _na_tile(b_ref, form, rl, kl)[...] = e_ref[i]
        d_ref[...] = jnp.zeros_like(d_ref)
        db_ref[...] = jnp.zeros_like(db_ref)

        def step(ib, carry):
            base, form = _na_block(ib, n_rows)
            rows = pl.ds(pl.multiple_of(ib * NA_Q, NA_Q), NA_Q)
            win = pl.ds(pl.multiple_of(base * GRID_W, GRID_W), NA_KEYS)
            q, k, v = q_ref[rows, :], k_ref[win, :], v_ref[win, :]
            do = do_ref[rows, :]
            delta = jnp.sum(do * o_ref[rows, :].astype(F32), axis=1, keepdims=True)
            do_b = do.astype(BF16)
            p = jnp.exp(_dot(q, k, _NT) + b_ref[form] - l_ref[rows, :])
            ds = p * (_dot(do_b, v, _NT) - delta)
            db_ref[form] += ds
            ds_b = ds.astype(BF16)
            d_ref[0, rows, :] = _dot(ds_b, k, _NN)
            d_ref[1, win, :] += _dot(ds_b, q, _TN)
            d_ref[2, win, :] += _dot(p.astype(BF16), do_b, _TN)
            return carry

        lax.fori_loop(0, n_rows // NA_BLOCK, step, 0)
        acc = [None] * NA_BIAS_ROWS
        for form, rl, kl, i in tiles:
            if i < NA_BIAS_ROWS:
                t = _na_tile(db_ref, form, rl, kl)[...]
                acc[i] = t if acc[i] is None else acc[i] + t
        for i in range(NA_BIAS_ROWS):
            de_ref[i] = acc[i]

    per_head = lambda bi, h: (bi, h, 0, 0)
    return pl.pallas_call(
        body, name=name, grid=(b, NA_HEADS),
        in_specs=[_na_head_spec(part, first, s) for part in range(3)]
        + [pl.BlockSpec((None, NA_BIAS_ROWS + 1, GRID_W, GRID_W), lambda bi, h: (h, 0, 0, 0)),
           pl.BlockSpec((None, None, s, HEAD_DIM), per_head), pl.BlockSpec((None, None, s, 1), per_head),
           pl.BlockSpec((None, None, s, HEAD_DIM), per_head)],
        out_specs=[pl.BlockSpec((None, None, 3, s, HEAD_DIM), lambda bi, h: (bi, h, 0, 0, 0)),
                   pl.BlockSpec((None, None, NA_BIAS_ROWS, GRID_W, GRID_W), lambda bi, h: (bi, h, 0, 0, 0))],
        out_shape=[jax.ShapeDtypeStruct((b, NA_HEADS, 3, s, HEAD_DIM), F32),
                   jax.ShapeDtypeStruct((b, NA_HEADS, NA_BIAS_ROWS, GRID_W, GRID_W), F32)],
        scratch_shapes=[pltpu.VMEM((NA_FORMS, NA_Q, NA_KEYS), F32), pltpu.VMEM((NA_FORMS, NA_Q, NA_KEYS), F32)],
        compiler_params=_params(dimension_semantics=("parallel", "parallel")),
    )(heads, heads, heads, bias, out, lse, dout)


GATE_TILE = 256


def _gate_fwd(proj, z, *, gate_col, tt, name):
    _, t, d = z.shape
    nj = d // GATE_TILE
    c0 = gate_col // GATE_TILE

    def body(ga_ref, gb_ref, za_ref, zb_ref, o_ref):
        o_ref[...] = (jax.nn.sigmoid(ga_ref[...]) * za_ref[...] + jax.nn.sigmoid(gb_ref[...]) * zb_ref[...]).astype(BF16)

    return pl.pallas_call(
        body, name=name, grid=(t // tt, nj),
        in_specs=[pl.BlockSpec((tt, GATE_TILE), lambda i, j: (i, c0 + j)),
                  pl.BlockSpec((tt, GATE_TILE), lambda i, j: (i, c0 + nj + j)),
                  pl.BlockSpec((None, tt, GATE_TILE), lambda i, j: (0, i, j)),
                  pl.BlockSpec((None, tt, GATE_TILE), lambda i, j: (1, i, j))],
        out_specs=pl.BlockSpec((tt, GATE_TILE), lambda i, j: (i, j)), out_shape=jax.ShapeDtypeStruct((t, d), BF16),
        compiler_params=_params(dimension_semantics=("parallel", "parallel")),
    )(proj, proj, z, z)


def _gate_bwd(dm, proj, z, *, gate_col, tt, name):
    _, t, d = z.shape
    nj = d // GATE_TILE
    c0 = gate_col // GATE_TILE

    def body(dm_ref, g_ref, z_ref, dz_ref, dg_ref):
        dmv = dm_ref[...]
        sg = jax.nn.sigmoid(g_ref[...])
        dz_ref[...] = (dmv * sg).astype(BF16)
        dg_ref[...] = (dmv * z_ref[...] * sg * (1.0 - sg)).astype(BF16)

    return pl.pallas_call(
        body, name=name, grid=(t // tt, 2 * nj),
        in_specs=[pl.BlockSpec((tt, GATE_TILE), lambda i, j: (i, j % nj)),
                  pl.BlockSpec((tt, GATE_TILE), lambda i, j: (i, c0 + j)),
                  pl.BlockSpec((None, tt, GATE_TILE), lambda i, j: (j // nj, i, j % nj))],
        out_specs=[pl.BlockSpec((None, tt, GATE_TILE), lambda i, j: (j // nj, i, j % nj)),
                   pl.BlockSpec((tt, GATE_TILE), lambda i, j: (i, c0 + j))],
        out_shape=[jax.ShapeDtypeStruct((2, t, d), BF16), jax.ShapeDtypeStruct(proj.shape, BF16)],
        compiler_params=_params(dimension_semantics=("parallel", "parallel")),
    )(dm, proj, z)


def _adamw(w, g, m, v, *, name):
    shape = w.shape
    w2, g2, m2, v2 = (t.reshape(-1, shape[-1]) for t in (w, g, m, v))
    rows, cols = w2.shape
    tr = rows
    for cand in (512, 256, 128, 64, 32, 16, 8):
        if rows % cand == 0:
            tr = cand
            break

    def body(w_ref, g_ref, m_ref, v_ref, d_ref, nm_ref, nv_ref):
        gv = g_ref[...]
        nm = ADAM_B1 * m_ref[...] + (1.0 - ADAM_B1) * gv
        nv = ADAM_B2 * v_ref[...] + (1.0 - ADAM_B2) * (gv * gv)
        m_hat = nm / (1.0 - ADAM_B1 ** ADAM_STEP)
        v_hat = nv / (1.0 - ADAM_B2 ** ADAM_STEP)
        d_ref[...] = -ADAM_LR * (m_hat / (jnp.sqrt(v_hat) + ADAM_EPS) + ADAM_WD * w_ref[...])
        nm_ref[...] = nm
        nv_ref[...] = nv

    blk = pl.BlockSpec((tr, cols), lambda i: (i, 0))
    out = jax.ShapeDtypeStruct((rows, cols), F32)
    res = pl.pallas_call(
        body, name=name, grid=(rows // tr,), in_specs=[blk] * 4, out_specs=[blk] * 3, out_shape=[out] * 3,
        compiler_params=_params(dimension_semantics=("parallel",)),
    )(w2, g2, m2, v2)
    return tuple(t.reshape(shape) for t in res)


def _my_place():
    return lax.axis_index("x"), lax.axis_index("y"), lax.axis_index("c")


def _other_chips(x, y):
    return [(1 - x, y), (x, 1 - y), (1 - x, 1 - y)]


def _chip_no(chip):
    return 2 * chip[0] + chip[1]


def _window(ref, kind, size, chip, lead):
    if kind == "col":
        return ref.at[(*lead, slice(None), pl.ds(pl.multiple_of(chip * size, LANES), size))]
    if kind == "row":
        return ref.at[(*lead, pl.ds(pl.multiple_of(chip * size, BF16_ROWS), size), slice(None))]
    shard = size + HEAD_DIM
    if kind == "win_main":
        return ref.at[(*lead, slice(None), pl.ds(pl.multiple_of(chip * shard + HEAD_DIM * (chip % 2), LANES), size))]
    assert kind == "win_strad"
    return ref.at[(*lead, slice(None), pl.ds(pl.multiple_of(size + 2 * shard * (chip // 2), LANES), LANES))]


def _full_shape(shard, kind):
    _, k, n = shard.shape
    return {"col": (k, N_CHIPS * n), "row": (N_CHIPS * k, n), "win_main": (k, N_CHIPS * (n + HEAD_DIM)),
            "slot": (N_CHIPS, k, n)}[kind]


def _place_own(shard, kind, layer, *, name):
    _, k, n = shard.shape
    tr = _div_tile(k, 512, BF16_ROWS)
    tc = LANES if kind == "win_main" else n
    mine = 2 * lax.axis_index("x") + lax.axis_index("y")
    row0 = mine * (k // tr) if kind == "row" else 0
    col0 = {"col": mine, "row": 0, "slot": 0, "win_main": (mine * (n + HEAD_DIM) + HEAD_DIM * (mine % 2)) // LANES}[kind]
    scalars = jnp.stack([mine, row0, col0]).astype(jnp.int32)

    def body(s_ref, i_ref, o_ref):
        o_ref[...] = i_ref[...]

    if kind == "slot":
        o_spec = pl.BlockSpec((None, tr, tc), lambda i, j, s: (s[0], i, j))
    else:
        o_spec = pl.BlockSpec((tr, tc), lambda i, j, s: (s[1] + i, s[2] + j))
    return pl.pallas_call(
        body, name=name,
        grid_spec=pltpu.PrefetchScalarGridSpec(
            num_scalar_prefetch=1, grid=(k // tr, n // tc),
            in_specs=[pl.BlockSpec((None, tr, tc), lambda i, j, s: (layer, i, j))], out_specs=o_spec),
        out_shape=jax.ShapeDtypeStruct(_full_shape(shard, kind), shard.dtype),
        compiler_params=_params(dimension_semantics=("parallel", "parallel")),
    )(scalars, shard)


class _GatherPlan:
    def __init__(self, src, dst, shapes, kinds, layer, send_sems, recv_sems):
        self.src, self.dst, self.shapes, self.kinds, self.layer = src, dst, shapes, kinds, layer
        self.send_sems, self.recv_sems = send_sems, recv_sems
        self.x, self.y, self.c = _my_place()
        self.mine = 2 * self.x + self.y
        self.chips = _other_chips(self.x, self.y)
        self.n = len(src)

    def half(self, i, chip, half):
        _, k, n = self.shapes[i]
        kind, dst, hk = self.kinds[i], self.dst[i], k // 2
        if kind == "slot":
            return dst.at[chip, pl.ds(pl.multiple_of(half * hk, BF16_ROWS), hk), :]
        if kind == "row":
            return dst.at[pl.ds(pl.multiple_of(chip * k + half * hk, BF16_ROWS), hk), :]
        col0 = chip * n if kind == "col" else chip * (n + HEAD_DIM) + HEAD_DIM * (chip % 2)
        return dst.at[pl.ds(pl.multiple_of(half * hk, BF16_ROWS), hk), pl.ds(pl.multiple_of(col0, LANES), n)]

    def _copy(self, sem, window, to, source=None):
        return pltpu.make_async_remote_copy(src_ref=window if source is None else source, dst_ref=window,
                                            send_sem=self.send_sems.at[sem], recv_sem=self.recv_sems.at[sem],
                                            device_id=to, device_id_type=MESH)

    def sends(self):
        out = []
        for k, chip in enumerate(self.chips):
            for i in range(self.n):
                hk = self.shapes[i][1] // 2
                mine = self.src[i].at[self.layer, pl.ds(pl.multiple_of(self.c * hk, BF16_ROWS), hk), :]
                out.append(self._copy(3 * i + k, self.half(i, self.mine, self.c), (*chip, self.c), source=mine))
        return out

    def arrivals(self):
        return [self._copy(3 * i + k, self.half(i, _chip_no(chip), self.c), (*chip, self.c))
                for k, chip in enumerate(self.chips) for i in range(self.n)]

    def forwards(self, first_sem):
        sibling = (self.x, self.y, 1 - self.c)
        return [self._copy(first_sem + 3 * i + k, self.half(i, _chip_no(chip), self.c), sibling)
                for k, chip in enumerate(self.chips) for i in range(self.n)]

    def forwarded(self, first_sem):
        sibling = (self.x, self.y, 1 - self.c)
        return [self._copy(first_sem + 3 * i + k, self.half(i, _chip_no(chip), 1 - self.c), sibling)
                for k, chip in enumerate(self.chips) for i in range(self.n)]


def _gather_layer(shards, kinds, fulls, layer, *, name):
    n_w = len(shards)
    shapes = [sh.shape for sh in shards]

    def body(*refs):
        plan = _GatherPlan(refs[:n_w], refs[2 * n_w:3 * n_w], shapes, kinds, layer, *refs[3 * n_w:])
        sends = plan.sends()
        for cp in sends:
            cp.start()
        passed = plan.forwards(3 * n_w)
        for landed, onward in zip(plan.arrivals(), passed):
            landed.wait_recv()
            onward.start()
        for cp in plan.forwarded(3 * n_w):
            cp.wait_recv()
        for cp in sends + passed:
            cp.wait_send()

    return pl.pallas_call(
        body, name=name, in_specs=[HBM] * (2 * n_w), out_specs=[HBM] * n_w,
        out_shape=[jax.ShapeDtypeStruct(f.shape, f.dtype) for f in fulls],
        input_output_aliases={n_w + i: i for i in range(n_w)},
        scratch_shapes=[pltpu.SemaphoreType.DMA((6 * n_w,)), pltpu.SemaphoreType.DMA((6 * n_w,))],
    )(*shards, *fulls)


IN_HBM = pl.BlockSpec(memory_space=pltpu.HBM)
IN_SEM = pl.BlockSpec(memory_space=pltpu.SEMAPHORE)
DATAFLOW = pltpu.SideEffectType.DATAFLOW_SIDE_EFFECTING


def _gather_layer_start(shards, kinds, fulls, layer, after, *, name):
    n_w = len(shards)
    shapes = [sh.shape for sh in shards]

    def body(*refs):
        plan = _GatherPlan(refs[:n_w], refs[n_w:2 * n_w], shapes, kinds, layer, refs[2 * n_w + 1], refs[2 * n_w + 2])
        for cp in plan.sends():
            cp.start()
        token = refs[-1]
        token[...] = jnp.zeros_like(token)

    operands = [pltpu.with_memory_space_constraint(a, pltpu.HBM) for a in (*shards, *fulls)]
    res = pl.pallas_call(
        body, name=name, in_specs=[IN_HBM] * (2 * n_w) + [pl.BlockSpec(memory_space=pl.ANY)],
        out_specs=(IN_SEM, IN_SEM, *([IN_HBM] * (2 * n_w)), pl.BlockSpec(memory_space=pltpu.VMEM)),
        out_shape=(pltpu.SemaphoreType.DMA((3 * n_w,)), pltpu.SemaphoreType.DMA((3 * n_w,)),
                   *[pltpu.HBM(a.shape, a.dtype) for a in operands], jax.ShapeDtypeStruct((8, LANES), F32)),
        input_output_aliases={i: 2 + i for i in range(2 * n_w)},
        compiler_params=pltpu.CompilerParams(has_side_effects=DATAFLOW),
    )(*operands, after)
    return res[0], res[1], res[2:2 + n_w], res[2 + n_w:2 + 2 * n_w], res[-1]


def _gather_layer_wait(send_sems, recv_sems, shards, fulls, kinds, layer, after, *, name):
    n_w = len(shards)
    shapes = [sh.shape for sh in shards]

    def body(*refs):
        plan = _GatherPlan(refs[:n_w], refs[n_w:2 * n_w], shapes, kinds, layer, refs[2 * n_w], refs[2 * n_w + 1])
        for cp in plan.sends():
            cp.wait_send()
        for cp in plan.arrivals():
            cp.wait_recv()

    res = pl.pallas_call(
        body, name=name, in_specs=[IN_HBM] * (2 * n_w) + [IN_SEM, IN_SEM, pl.BlockSpec(memory_space=pl.ANY)],
        out_specs=[IN_HBM] * (2 * n_w), out_shape=[pltpu.HBM(a.shape, a.dtype) for a in (*shards, *fulls)],
        input_output_aliases={i: i for i in range(2 * n_w)},
        compiler_params=pltpu.CompilerParams(has_side_effects=DATAFLOW),
    )(*shards, *fulls, send_sems, recv_sems, after)
    return res[n_w:]


def _gather_layer_forward(shapes, kinds, fulls, *, name):
    n_w = len(fulls)

    def body(*refs):
        plan = _GatherPlan([None] * n_w, refs[n_w:2 * n_w], shapes, kinds, 0, *refs[2 * n_w:])
        passed = plan.forwards(0)
        for cp in passed:
            cp.start()
        for cp in plan.forwarded(0):
            cp.wait_recv()
        for cp in passed:
            cp.wait_send()

    return pl.pallas_call(
        body, name=name, in_specs=[HBM] * n_w, out_specs=[HBM] * n_w,
        out_shape=[jax.ShapeDtypeStruct(f.shape, f.dtype) for f in fulls],
        input_output_aliases={i: i for i in range(n_w)},
        scratch_shapes=[pltpu.SemaphoreType.DMA((3 * n_w,)), pltpu.SemaphoreType.DMA((3 * n_w,))],
    )(*fulls)


def _grads_to_sibling(grads, layer, *, name):
    n_w = len(grads)

    def body(*refs):
        src, dst = refs[:n_w], refs[n_w:2 * n_w]
        send_sems, recv_sems = refs[2 * n_w:]
        x, y, c = _my_place()
        cps = [pltpu.make_async_remote_copy(src_ref=src[i], dst_ref=dst[i], send_sem=send_sems.at[i],
                                            recv_sem=recv_sems.at[i], device_id=(x, y, layer), device_id_type=MESH)
               for i in range(n_w)]

        @pl.when(c != layer)
        def _():
            for cp in cps:
                cp.start()
            for cp in cps:
                cp.wait_send()

        @pl.when(c == layer)
        def _():
            for cp in cps:
                cp.wait_recv()

    return pl.pallas_call(
        body, name=name, in_specs=[HBM] * n_w, out_specs=[HBM] * n_w,
        out_shape=[jax.ShapeDtypeStruct(g.shape, g.dtype) for g in grads],
        scratch_shapes=[pltpu.SemaphoreType.DMA((n_w,)), pltpu.SemaphoreType.DMA((n_w,))],
    )(*grads)


def _pair_add(mine, other, *, name):
    k, n = mine.shape
    tr = _div_tile(k, 512, BF16_ROWS)

    def body(a_ref, b_ref, o_ref):
        o_ref[...] = (a_ref[...].astype(F32) + b_ref[...].astype(F32)).astype(o_ref.dtype)

    blk = pl.BlockSpec((tr, n), lambda i: (i, 0))
    return pl.pallas_call(
        body, name=name, grid=(k // tr,), in_specs=[blk, blk], out_specs=blk,
        out_shape=jax.ShapeDtypeStruct((k, n), mine.dtype), compiler_params=_params(dimension_semantics=("parallel",)),
    )(mine, other)


class _ScatterPlan:
    def __init__(self, src, dst, kinds, sizes, layer, send_sems, recv_sems):
        self.src, self.dst, self.kinds, self.sizes, self.layer = src, dst, kinds, sizes, layer
        self.send_sems, self.recv_sems = send_sems, recv_sems
        self.x, self.y, self.c = _my_place()
        self.mine = 2 * self.x + self.y
        self.chips = _other_chips(self.x, self.y)
        self.n = len(src)

    def _copy(self, i, k, chip, window_of, slab):
        return pltpu.make_async_remote_copy(src_ref=_window(self.src[i], self.kinds[i], self.sizes[i], window_of, ()),
                                            dst_ref=self.dst[i].at[slab], send_sem=self.send_sems.at[3 * i + k],
                                            recv_sem=self.recv_sems.at[3 * i + k], device_id=(*chip, self.layer),
                                            device_id_type=MESH)

    def sends(self):
        return [self._copy(i, k, chip, _chip_no(chip), self.mine) for k, chip in enumerate(self.chips) for i in range(self.n)]

    def arrivals(self):
        return [self._copy(i, k, chip, self.mine, _chip_no(chip)) for k, chip in enumerate(self.chips) for i in range(self.n)]


def _slab_shape(p, kind, size):
    return (N_CHIPS,) + {"col": (p.shape[0], size), "row": (size, p.shape[1]), "win_main": (p.shape[0], size),
                         "win_strad": (p.shape[0], LANES)}[kind]


def _grads_to_chips(pairs, kinds, sizes, layer, *, name):
    n_w = len(pairs)

    def body(*refs):
        plan = _ScatterPlan(refs[:n_w], refs[n_w:2 * n_w], kinds, sizes, layer, *refs[2 * n_w:])

        @pl.when(plan.c == layer)
        def _():
            sends = plan.sends()
            for cp in sends:
                cp.start()
            for cp in plan.arrivals():
                cp.wait_recv()
            for cp in sends:
                cp.wait_send()

    return pl.pallas_call(
        body, name=name, in_specs=[HBM] * n_w, out_specs=[HBM] * n_w,
        out_shape=[jax.ShapeDtypeStruct(_slab_shape(p, kind, size), p.dtype) for p, kind, size in zip(pairs, kinds, sizes)],
        scratch_shapes=[pltpu.SemaphoreType.DMA((3 * n_w,)), pltpu.SemaphoreType.DMA((3 * n_w,))],
    )(*pairs)


def _grads_to_chips_start(pairs, kinds, sizes, layer, *, name):
    n_w = len(pairs)

    def body(*refs):
        plan = _ScatterPlan(refs[:n_w], refs[n_w:2 * n_w], kinds, sizes, layer, refs[2 * n_w], refs[2 * n_w + 1])

        @pl.when(plan.c == layer)
        def _():
            for cp in plan.sends():
                cp.start()

    slabs = [lax.empty(_slab_shape(p, kind, size), p.dtype) for p, kind, size in zip(pairs, kinds, sizes)]
    operands = [pltpu.with_memory_space_constraint(a, pltpu.HBM) for a in (*pairs, *slabs)]
    res = pl.pallas_call(
        body, name=name, in_specs=[IN_HBM] * (2 * n_w), out_specs=(IN_SEM, IN_SEM, *([IN_HBM] * (2 * n_w))),
        out_shape=(pltpu.SemaphoreType.DMA((3 * n_w,)), pltpu.SemaphoreType.DMA((3 * n_w,)),
                   *[pltpu.HBM(a.shape, a.dtype) for a in operands]),
        input_output_aliases={i: 2 + i for i in range(2 * n_w)},
        compiler_params=pltpu.CompilerParams(has_side_effects=DATAFLOW),
    )(*operands)
    return res[0], res[1], res[2:2 + n_w], res[2 + n_w:]


def _grads_to_chips_wait(send_sems, recv_sems, pairs, slabs, kinds, sizes, layer, after, *, name):
    n_w = len(pairs)

    def body(*refs):
        plan = _ScatterPlan(refs[:n_w], refs[n_w:2 * n_w], kinds, sizes, layer, refs[2 * n_w], refs[2 * n_w + 1])

        @pl.when(plan.c == layer)
        def _():
            for cp in plan.sends():
                cp.wait_send()
            for cp in plan.arrivals():
                cp.wait_recv()

    res = pl.pallas_call(
        body, name=name, in_specs=[IN_HBM] * (2 * n_w) + [IN_SEM, IN_SEM, pl.BlockSpec(memory_space=pl.ANY)],
        out_specs=[IN_HBM] * (2 * n_w), out_shape=[pltpu.HBM(a.shape, a.dtype) for a in (*pairs, *slabs)],
        input_output_aliases={i: i for i in range(2 * n_w)},
        compiler_params=pltpu.CompilerParams(has_side_effects=DATAFLOW),
    )(*pairs, *slabs, send_sems, recv_sems, after)
    return res[:n_w], res[n_w:]


def _sum_slabs(slabs, pair, kind, size, layer, into, *, name):
    n_s, k, n = slabs.shape
    tr = _div_tile(k, 512, BF16_ROWS)
    tc = n if kind in ("col", "row") else LANES
    x, y, _ = _my_place()
    mine = 2 * x + y
    shard = size + HEAD_DIM
    row0 = mine * (k // tr) if kind == "row" else 0
    col0 = {"col": mine, "row": 0, "win_main": (mine * shard + HEAD_DIM * (mine % 2)) // LANES,
            "win_strad": (size + 2 * shard * (mine // 2)) // LANES}[kind]
    scalars = jnp.stack([mine, row0, col0]).astype(jnp.int32)

    def body(s_ref, slab_ref, own_ref, *rest):
        o_ref = rest[-1]
        me = s_ref[0]
        acc = jnp.zeros(o_ref.shape, F32)
        for i in range(n_s):
            acc = acc + jnp.where(me == i, own_ref[...], slab_ref[i]).astype(F32)
        o_ref[...] = acc

    operands = [scalars, slabs, pair] + ([] if into is None else [into])
    return pl.pallas_call(
        body, name=name,
        grid_spec=pltpu.PrefetchScalarGridSpec(
            num_scalar_prefetch=1, grid=(k // tr, n // tc),
            in_specs=[pl.BlockSpec((n_s, tr, tc), lambda i, j, s: (0, i, j)),
                      pl.BlockSpec((tr, tc), lambda i, j, s: (s[1] + i, s[2] + j))] + ([] if into is None else [HBM]),
            out_specs=pl.BlockSpec((None, tr, tc), lambda i, j, s: (layer, i, j))),
        out_shape=jax.ShapeDtypeStruct((2, k, n), F32),
        input_output_aliases={} if into is None else {3: 0},
        compiler_params=_params(dimension_semantics=("parallel", "parallel")),
    )(*operands)


def _exchange_layers(bufs, *, name):
    n_w = len(bufs)

    def body(*refs):
        dst = refs[n_w:2 * n_w]
        send_sems, recv_sems = refs[2 * n_w:]
        x, y, c = _my_place()

        def copy(i, layer):
            return pltpu.make_async_remote_copy(src_ref=dst[i].at[layer], dst_ref=dst[i].at[layer], send_sem=send_sems.at[i],
                                                recv_sem=recv_sems.at[i], device_id=(x, y, 1 - c), device_id_type=MESH)

        sends = [copy(i, c) for i in range(n_w)]
        for cp in sends:
            cp.start()
        for i in range(n_w):
            copy(i, 1 - c).wait_recv()
        for cp in sends:
            cp.wait_send()

    return pl.pallas_call(
        body, name=name, in_specs=[HBM] * n_w, out_specs=[HBM] * n_w,
        out_shape=[jax.ShapeDtypeStruct(b.shape, b.dtype) for b in bufs],
        input_output_aliases={i: i for i in range(n_w)},
        scratch_shapes=[pltpu.SemaphoreType.DMA((n_w,)), pltpu.SemaphoreType.DMA((n_w,))],
    )(*bufs)


def _all_sum_small(v, *, name):
    r = v.shape[0]
    relations = [(dx, dy, dc) for dx in (0, 1) for dy in (0, 1) for dc in (0, 1)][1:]

    def body(v_ref, o_ref, buf, send_sems, recv_sems):
        x, y, c = _my_place()
        me = 4 * x + 2 * y + c
        buf[me] = v_ref[...]
        peers = [(x + dx - 2 * x * dx, y + dy - 2 * y * dy, c + dc - 2 * c * dc) for dx, dy, dc in relations]

        def copy(k, slot):
            return pltpu.make_async_remote_copy(src_ref=v_ref, dst_ref=buf.at[slot], send_sem=send_sems.at[k],
                                                recv_sem=recv_sems.at[k], device_id=peers[k], device_id_type=MESH)

        sends = [copy(k, me) for k in range(len(relations))]
        for cp in sends:
            cp.start()
        for k, (px, py, pc) in enumerate(peers):
            copy(k, 4 * px + 2 * py + pc).wait_recv()
        for cp in sends:
            cp.wait_send()
        acc = buf[0]
        for i in range(1, 8):
            acc = acc + buf[i]
        o_ref[...] = acc

    vm = pl.BlockSpec(memory_space=pltpu.VMEM)
    return pl.pallas_call(
        body, name=name, in_specs=[vm], out_specs=vm, out_shape=jax.ShapeDtypeStruct((r, LANES), F32),
        scratch_shapes=[pltpu.VMEM((8, r, LANES), F32), pltpu.SemaphoreType.DMA((7,)), pltpu.SemaphoreType.DMA((7,))],
    )(v)


SHARDED = (("ffn1_w_up", "col"), ("ffn1_w_down", "row"), ("w_in", "win"), ("w_branch_a", "col"),
           ("w_branch_b", "col"), ("w_out", "row"), ("ffn2_w_up", "col"), ("ffn2_w_down", "row"))
REPLICATED = ("ffn1_norm", "mix_norm", "na_rel_bias", "ffn2_norm", "final_norm")


def _weight_pieces(w):
    even = lax.axis_index("y") == 0
    shards, kinds, names = [], [], []
    for name, kind in SHARDED:
        wb = w[name].astype(BF16)
        if kind == "win":
            main = wb.shape[-1] - HEAD_DIM
            assert main % LANES == 0
            zeros = jnp.zeros(wb.shape[:-1] + (HEAD_DIM,), BF16)
            shards += [jnp.where(even, wb[..., :main], wb[..., HEAD_DIM:]),
                       jnp.where(even, jnp.concatenate([wb[..., main:], zeros], -1),
                                 jnp.concatenate([zeros, wb[..., :HEAD_DIM]], -1))]
            kinds += ["win_main", "slot"]
            names += [name, name + "_strad"]
        else:
            shards.append(wb)
            kinds.append(kind)
            names.append(name)
    return names, kinds, shards


def _finish_w_in(full):
    full = dict(full)
    strad = full.pop("w_in_strad")
    main = full["w_in"].shape[1] // N_CHIPS - HEAD_DIM
    for i in range(N_CHIPS // 2):
        lo = main + 2 * (main + HEAD_DIM) * i
        full["w_in"] = full["w_in"].at[:, lo:lo + LANES].set(strad[2 * i] + strad[2 * i + 1])
    return full


def _scatter_pieces(shards):
    names, kinds, sizes, srcs = [], [], [], []
    for name, kind in SHARDED:
        shp = shards[name].shape
        if kind == "win":
            names += [name, name + "_strad"]
            kinds += ["win_main", "win_strad"]
            sizes += [shp[2] - HEAD_DIM] * 2
            srcs += [name, name]
        else:
            names.append(name)
            kinds.append(kind)
            sizes.append(shp[1] if kind == "row" else shp[2])
            srcs.append(name)
    return names, kinds, sizes, srcs


def _pair_sums(grads, layer):
    uniq = [name for name, _ in SHARDED]
    arrived = _grads_to_sibling([grads[n] for n in uniq], layer, name=f"grads{layer}_to_sibling")
    return {n: _pair_add(grads[n], a, name=f"grads{layer}_pair_{n}") for n, a in zip(uniq, arrived)}


def _finish_weight_grads(reduced, names):
    out = dict(zip(names, _exchange_layers(reduced, name="grads_layers")))
    strad = out.pop("w_in_strad")
    even = lax.axis_index("y") == 0
    out["w_in"] = jnp.where(even, jnp.concatenate([out["w_in"], strad[..., :HEAD_DIM]], -1),
                            jnp.concatenate([strad[..., HEAD_DIM:], out["w_in"]], -1))
    return out


class _Grads:
    def __init__(self):
        self.arrays = {}

    def put(self, weight, layer, a, b, *, cols=None, col_off=0, **kw):
        self.arrays[weight, layer] = _mm(a, b, mode="tn", out_dtype=BF16, out_cols=cols, out_col_off=col_off,
                                         out_into=self.arrays.get((weight, layer)), **kw)

    def of_layer(self, layer):
        return {weight: a for (weight, l), a in self.arrays.items() if l == layer}


def _ffn_fwd(x, norm_g, w_up, w_down, tag):
    t, d = x.shape
    f = w_down.shape[0]
    h = _rms_fwd(x, norm_g, tt=512, name=f"{tag}_norm")
    a, gate, up = _mm_swiglu_fwd(h, w_up, tm=_div_tile(t, ROWS_NARROW, 8), tn=MXU_N, name=f"{tag}_up")
    x_out = _mm(a, w_down, mode="nn", out_dtype=F32, tm=_div_tile(t, ROWS_WIDE, 8), tn=d, tk=f, alpha=0.5, res=x, name=f"{tag}_down")
    return x_out, (x, h, a, gate, up)


def _ffn_bwd(dx, dxb, saved, norm_g, w_up, w_down, layer, grads, wname, tag):
    x, h, a, gate, up = saved
    t, d = x.shape
    f = w_down.shape[0]
    tn = _div_tile(f, 1408)
    grads.put(f"{wname}_w_down", layer, a, dxb, tm=tn, tn=d, tk=1024, alpha=0.5, name=f"{tag}_dwd")
    d_gate, d_up = _mm_swiglu_bwd(dxb, w_down, gate, up, alpha=0.5, tm=_div_tile(t, ROWS_NARROW, 8), tn=MXU_N, name=f"{tag}_da")
    grads.put(f"{wname}_w_up", layer, h, d_gate, cols=2 * f, tm=d, tn=tn, tk=1024, name=f"{tag}_dwg")
    grads.put(f"{wname}_w_up", layer, h, d_up, cols=2 * f, col_off=f // tn, tm=d, tn=tn, tk=1024, name=f"{tag}_dwu")
    dh = _mm(d_gate, w_up, mode="nt", out_dtype=F32, tm=_div_tile(t, ROWS_WIDE, 8), tn=d, tk=f, name=f"{tag}_dh1")
    dh = _mm(d_up, w_up, mode="nt", out_dtype=F32, tm=_div_tile(t, ROWS_WIDE, 8), tn=d, tk=f, b_k_off=1, res=dh, name=f"{tag}_dh2")
    return _rms_bwd(dh, x, norm_g, dx, tt=512, name=f"{tag}_dnorm")


def _to_heads(y, b, n_heads):
    t, w = y.shape
    return y.reshape(b, t // b, n_heads, HEAD_DIM).transpose(0, 2, 1, 3)


def _from_heads(y):
    b, n, s, hd = y.shape
    return y.transpose(0, 2, 1, 3).reshape(b * s, n * hd)


N_QKV = 3 * (DIL_HEADS + NA_HEADS) * HEAD_DIM


def _mixer_fwd(x, b, norm_g, full, bias, tabs, tag):
    t, d = x.shape
    s = t // b
    n_in = full["w_in"].shape[1]
    h = _rms_fwd(x, norm_g, tt=512, name=f"{tag}_norm")
    proj = _mm(h, full["w_in"], mode="nn", out_dtype=F32, tm=_div_tile(t, ROWS_NARROW, 8), tn=MXU_N, tk=d, name=f"{tag}_in")
    heads = _split_heads(proj.reshape(b, s, -1), *tabs, n_pairs=N_QKV // LANES, rot_pairs=DIL_HEADS,
                         scale_ranges=((0, DIL_HEADS // 2), (3 * DIL_HEADS // 2, (3 * DIL_HEADS + NA_HEADS) // 2)),
                         name=f"{tag}_heads")
    ya, lse_a = _dil_attn_fwd(heads, name=f"{tag}_dil")
    yb, lse_b = _na_attn_fwd(heads, bias, first=3 * DIL_HEADS, name=f"{tag}_na")
    ya2, yb2 = _from_heads(ya), _from_heads(yb)
    z = _mm(ya2, full["w_branch_a"], mode="nn", out_dtype=F32, tm=_div_tile(t, ROWS_NARROW, 8), tn=MXU_N, tk=ya2.shape[1],
            out_slab=(0, 2), name=f"{tag}_za")
    z = _mm(yb2, full["w_branch_b"], mode="nn", out_dtype=F32, tm=_div_tile(t, ROWS_NARROW, 8), tn=MXU_N, tk=yb2.shape[1],
            out_slab=(1, 2), out_into=z, name=f"{tag}_zb")
    merged = _gate_fwd(proj, z, gate_col=N_QKV, tt=1024, name=f"{tag}_gate")
    x_out = _mm(merged, full["w_out"], mode="nn", out_dtype=F32, tm=_div_tile(t, ROWS_NARROW, 8), tn=MXU_N, tk=d, res=x, name=f"{tag}_out")
    return x_out, (x, h, proj, heads, ya, lse_a, yb, lse_b, ya2, yb2, z, merged)


def _mixer_bwd(dx, dob, b, saved, norm_g, full, layer, bias, tabs, grads, tag):
    x, h, proj, heads, ya, lse_a, yb, lse_b, ya2, yb2, z, merged = saved
    t, d = x.shape
    s = t // b
    n_in = full["w_in"].shape[1]
    grads.put("w_out", layer, merged, dob, tm=d, tn=d, tk=1024, name=f"{tag}_dwo")
    dm = _mm(dob, full["w_out"], mode="nt", out_dtype=F32, tm=_div_tile(t, ROWS_NARROW, 8), tn=MXU_N, tk=d, name=f"{tag}_dm")
    dz, dproj = _gate_bwd(dm, proj, z, gate_col=N_QKV, tt=1024, name=f"{tag}_dgate")
    grads.put("w_branch_a", layer, ya2, dz, b_sel=0, tm=ya2.shape[1], tn=d, tk=1024, name=f"{tag}_dwa")
    grads.put("w_branch_b", layer, yb2, dz, b_sel=1, tm=yb2.shape[1], tn=d, tk=1024, name=f"{tag}_dwb")
    dya = _mm(dz, full["w_branch_a"], mode="nt", out_dtype=F32, tm=_div_tile(t, ROWS_NARROW, 8), tn=MXU_N, tk=d, a_sel=0, name=f"{tag}_dya")
    dyb = _mm(dz, full["w_branch_b"], mode="nt", out_dtype=F32, tm=_div_tile(t, ROWS_NARROW, 8), tn=MXU_N, tk=d, a_sel=1, name=f"{tag}_dyb")
    d_dil = _dil_attn_bwd(heads, ya, lse_a, _to_heads(dya, b, DIL_GROUP_HEADS), name=f"{tag}_ddil")
    d_na, d_bias = _na_attn_bwd(heads, bias, yb, lse_b, _to_heads(dyb, b, NA_HEADS), first=3 * DIL_HEADS, name=f"{tag}_dna")
    dproj = _merge_heads(d_dil, *tabs, heads_per_row=DIL_GROUP_HEADS, rot_pairs=DIL_HEADS, scale_pairs=DIL_HEADS // 2,
                         dilated=True, out_cols=n_in, tile_off=0, into=dproj.reshape(b, s, n_in), name=f"{tag}_dheads_a")
    dproj = _merge_heads(d_na, *tabs, heads_per_row=NA_HEADS, rot_pairs=0, scale_pairs=NA_HEADS // 2, dilated=False,
                         out_cols=n_in, tile_off=3 * DIL_HEADS // 2, into=dproj, name=f"{tag}_dheads_b").reshape(t, n_in)
    grads.put("w_in", layer, h, dproj, tm=_div_tile(d, 512), tn=_div_tile(n_in, 2944), tk=1024, name=f"{tag}_dwin")
    dh = _mm(dproj, full["w_in"], mode="nt", out_dtype=F32, tm=_div_tile(t, ROWS_WIDE, 8), tn=d, tk=_div_tile(n_in, 2944), name=f"{tag}_dh")
    dx_in, dxb_in, d_norm = _rms_bwd(dh, x, norm_g, dx, tt=512, name=f"{tag}_dnorm")
    d_rb = _na_collapse_bias(d_bias, name=f"{tag}_dbias")
    return dx_in, dxb_in, d_norm, d_rb


def kernel(x, ffn1_norm, ffn1_w_up, ffn1_w_down, mix_norm, w_in, na_rel_bias, w_branch_a, w_branch_b, w_out, ffn2_norm, ffn2_w_up, ffn2_w_down, final_norm, loss_target, m_ffn1_norm, m_ffn1_w_up, m_ffn1_w_down, m_mix_norm, m_w_in, m_na_rel_bias, m_w_branch_a, m_w_branch_b, m_w_out, m_ffn2_norm, m_ffn2_w_up, m_ffn2_w_down, m_final_norm, v_ffn1_norm, v_ffn1_w_up, v_ffn1_w_down, v_mix_norm, v_w_in, v_na_rel_bias, v_w_branch_a, v_w_branch_b, v_w_out, v_ffn2_norm, v_ffn2_w_up, v_ffn2_w_down, v_final_norm):
    w = dict(ffn1_norm=ffn1_norm, ffn1_w_up=ffn1_w_up, ffn1_w_down=ffn1_w_down, mix_norm=mix_norm, w_in=w_in,
             na_rel_bias=na_rel_bias, w_branch_a=w_branch_a, w_branch_b=w_branch_b, w_out=w_out, ffn2_norm=ffn2_norm,
             ffn2_w_up=ffn2_w_up, ffn2_w_down=ffn2_w_down, final_norm=final_norm)
    mom = dict(ffn1_norm=m_ffn1_norm, ffn1_w_up=m_ffn1_w_up, ffn1_w_down=m_ffn1_w_down, mix_norm=m_mix_norm, w_in=m_w_in,
               na_rel_bias=m_na_rel_bias, w_branch_a=m_w_branch_a, w_branch_b=m_w_branch_b, w_out=m_w_out,
               ffn2_norm=m_ffn2_norm, ffn2_w_up=m_ffn2_w_up, ffn2_w_down=m_ffn2_w_down, final_norm=m_final_norm)
    var = dict(ffn1_norm=v_ffn1_norm, ffn1_w_up=v_ffn1_w_up, ffn1_w_down=v_ffn1_w_down, mix_norm=v_mix_norm, w_in=v_w_in,
               na_rel_bias=v_na_rel_bias, w_branch_a=v_w_branch_a, w_branch_b=v_w_branch_b, w_out=v_w_out,
               ffn2_norm=v_ffn2_norm, ffn2_w_up=v_ffn2_w_up, ffn2_w_down=v_ffn2_w_down, final_norm=v_final_norm)
    b, s, d = x.shape
    t = b * s
    depth = ffn1_norm.shape[0]
    assert depth == 2, "core c of a chip sends / reduces layer c"
    shards = {name: w[name] for name, _ in SHARDED}

    names, kinds, pieces = _weight_pieces(w)
    by_layer = [[p[l:l + 1] for p in pieces] for l in range(depth)]
    shapes = [p.shape for p in by_layer[0]]
    own = [[_place_own(p, kind, 0, name=f"own{l}_{nm}") for nm, kind, p in zip(names, kinds, by_layer[l])] for l in range(depth)]
    gathered = _gather_layer(by_layer[0], kinds, own[0], 0, name="gather_l0")
    full = [_finish_w_in(zip(names, gathered)), None]
    send_sems, recv_sems, in_flight, landing, token = _gather_layer_start(by_layer[1], kinds, own[1], 0, gathered[0],
                                                                          name="gather_l1_start")
    tabs = _rope_tables(s)
    bias = _na_expand_bias(na_rel_bias, name="na_bias")

    xc = x.reshape(t, d)
    saved = []
    for l in range(depth):
        gain = ffn1_norm[l:l + 1]
        if l == 0:
            gain = gain + token[:1, :1]
        else:
            landed = _gather_layer_wait(send_sems, recv_sems, in_flight, landing, kinds, 0, xc, name="gather_l1_wait")
            full[1] = _finish_w_in(zip(names, _gather_layer_forward(shapes, kinds, landed, name="gather_l1_forward")))
        xc, s1 = _ffn_fwd(xc, gain, full[l]["ffn1_w_up"], full[l]["ffn1_w_down"], f"l{l}_ffn1")
        xc, s2 = _mixer_fwd(xc, b, mix_norm[l:l + 1], full[l], bias[l], tabs, f"l{l}_mix")
        xc, s3 = _ffn_fwd(xc, ffn2_norm[l:l + 1], full[l]["ffn2_w_up"], full[l]["ffn2_w_down"], f"l{l}_ffn2")
        saved.append((s1, s2, s3))

    dx, dxb, d_final, loss_part = _final_loss(xc, final_norm.reshape(1, d), loss_target.reshape(t, d), tt=512, name="final_loss")
    grads = _Grads()
    piece_names, piece_kinds, piece_sizes, piece_srcs = _scatter_pieces(shards)
    reduced = [None] * len(piece_names)
    small = {name: [None] * depth for name in REPLICATED[:-1]}
    for l in reversed(range(depth)):
        s1, s2, s3 = saved[l]
        dx, dxb, small["ffn2_norm"][l] = _ffn_bwd(dx, dxb, s3, ffn2_norm[l:l + 1], full[l]["ffn2_w_up"], full[l]["ffn2_w_down"],
                                                  l, grads, "ffn2", f"l{l}_ffn2")
        dx, dxb, small["mix_norm"][l], small["na_rel_bias"][l] = _mixer_bwd(
            dx, dxb, b, s2, mix_norm[l:l + 1], full[l], l, bias[l], tabs, grads, f"l{l}_mix")
        dx, dxb, small["ffn1_norm"][l] = _ffn_bwd(dx, dxb, s1, ffn1_norm[l:l + 1], full[l]["ffn1_w_up"], full[l]["ffn1_w_down"],
                                                  l, grads, "ffn1", f"l{l}_ffn1")
        pair = _pair_sums(grads.of_layer(l), l)
        pairs = [pair[src] for src in piece_srcs]
        if l > 0:
            in_flight = _grads_to_chips_start(pairs, piece_kinds, piece_sizes, l, name=f"grads{l}_to_chips_start")
            continue
        arrived = [_grads_to_chips_wait(*in_flight, piece_kinds, piece_sizes, 1, dx, name="grads1_to_chips_wait"),
                   (pairs, _grads_to_chips(pairs, piece_kinds, piece_sizes, 0, name="grads0_to_chips"))]
        for layer, (srcs, slabs) in zip((1, 0), arrived):
            reduced = [_sum_slabs(sl, p, kind, size, layer, r, name=f"grads{layer}_sum_{nm}")
                       for sl, p, kind, size, r, nm in zip(slabs, srcs, piece_kinds, piece_sizes, reduced, piece_names)]
    grad_x = dx.reshape(b, s, d)
    g_out = _finish_weight_grads(reduced, piece_names)

    parts = [jnp.stack(small[name]).reshape(-1) for name in REPLICATED[:-1]] + [d_final.reshape(-1), loss_part[0, :1]]
    sizes = [v.shape[0] for v in parts]
    flat = jnp.concatenate(parts)
    flat = jnp.pad(flat, (0, -flat.shape[0] % (8 * LANES)))
    small_sum = _all_sum_small(flat.reshape(-1, LANES), name="small_all_sum").reshape(-1)
    off = 0
    for name, n in zip(REPLICATED, sizes[:-1]):
        g_out[name] = small_sum[off:off + n].reshape(w[name].shape)
        off += n
    loss = small_sum[off]

    names = list(w)
    delta, new_m, new_v = {}, {}, {}
    for name in names:
        delta[name], new_m[name], new_v[name] = _adamw(w[name], g_out[name], mom[name], var[name], name=f"adamw_{name}")
    return (loss, grad_x, *[g_out[n] for n in names], *[delta[n] for n in names], *[new_m[n] for n in names],
            *[new_v[n] for n in names])
```

```python
import functools

import numpy as np
import jax
import jax.numpy as jnp
from jax import lax
from jax.experimental import pallas as pl
from jax.experimental.pallas import tpu as pltpu

F32, BF16 = jnp.float32, jnp.bfloat16
MESH = pl.DeviceIdType.MESH

HEAD_DIM = 64
DILATIONS = (1, 4, 16)
DIL_HALF = 64
DIL_GROUP_HEADS = 4
DIL_HEADS = 12
NA_HEADS = 8
GRID_W = 64
NA_ROWS = 8
NA_COLS = 16
ROPE_THETA = 10000.0
RMS_EPS = 1e-6
NEG_INF = -1e30
ADAM_LR, ADAM_B1, ADAM_B2, ADAM_EPS, ADAM_WD, ADAM_STEP = 0.001, 0.9, 0.999, 1e-08, 0.01, 10
QK_SCALE = HEAD_DIM ** -0.5

N_CHIPS = 4
LANES = 128
BF16_ROWS = 16
VMEM_LIMIT = 56 * 1024 * 1024
MXU_N = 256
ROWS_NARROW = 2048
ROWS_WIDE = 512

_NN = (((1,), (0,)), ((), ()))
_NT = (((1,), (1,)), ((), ()))
_TN = (((0,), (0,)), ((), ()))

HBM = pl.BlockSpec(memory_space=pl.ANY)


def _params(**kw):
    return pltpu.CompilerParams(vmem_limit_bytes=VMEM_LIMIT, **kw)


def _dot(a, b, dims):
    return lax.dot_general(a, b, dims, preferred_element_type=F32)


def _div_tile(n, cap, mult=LANES):
    best = None
    for t in range(mult, min(n, cap) + 1, mult):
        if n % t == 0:
            best = t
    return n if best is None else best


def _stacked(block, index, sel):
    if sel is None:
        return pl.BlockSpec(block, index)
    return pl.BlockSpec((None,) + block, lambda *g: (sel,) + index(*g))


def _mm(a, b, *, mode, out_dtype, tm, tn, tk, name, alpha=1.0, res=None, a_sel=None, b_sel=None, b_k_off=0,
        out_slab=None, out_cols=None, out_col_off=0, out_into=None):
    a2, b2 = a.shape[-2:], b.shape[-2:]
    if mode == "nn":
        (m, k), n = a2, b2[1]
        a_spec = _stacked((tm, tk), lambda i, j, kk: (i, kk), a_sel)
        b_spec = _stacked((tk, tn), lambda i, j, kk: (kk + b_k_off, j), b_sel)
        dims = _NN
    elif mode == "nt":
        (m, k), n = a2, b2[0]
        a_spec = _stacked((tm, tk), lambda i, j, kk: (i, kk), a_sel)
        b_spec = _stacked((tn, tk), lambda i, j, kk: (j, kk + b_k_off), b_sel)
        dims = _NT
    else:
        (k, m), n = a2, b2[1]
        a_spec = _stacked((tk, tm), lambda i, j, kk: (kk, i), a_sel)
        b_spec = _stacked((tk, tn), lambda i, j, kk: (kk + b_k_off, j), b_sel)
        dims = _TN
    assert m % tm == 0 and n % tn == 0 and k % tk == 0, (name, a.shape, b.shape)
    nk = k // tk
    has_res = res is not None
    if out_slab is None:
        o_spec = pl.BlockSpec((tm, tn), lambda i, j, kk: (i, j + out_col_off))
        out_shape = jax.ShapeDtypeStruct((m, n if out_cols is None else out_cols), out_dtype)
    else:
        o_spec = _stacked((tm, tn), lambda i, j, kk: (i, j + out_col_off), out_slab[0])
        out_shape = jax.ShapeDtypeStruct((out_slab[1], m, n if out_cols is None else out_cols), out_dtype)
    r_spec = pl.BlockSpec((tm, tn), lambda i, j, kk: (i, j))
    n_in = 2 + has_res + (out_into is not None)

    def body(*refs):
        a_ref, b_ref = refs[0], refs[1]
        r_ref = refs[2] if has_res else None
        o_ref = refs[n_in]
        p = _dot(a_ref[...], b_ref[...], dims)

        def finish(acc):
            y = acc * alpha if alpha != 1.0 else acc
            if has_res:
                y = y + r_ref[...].astype(F32)
            o_ref[...] = y.astype(o_ref.dtype)

        if nk == 1:
            finish(p)
        else:
            acc_ref = refs[n_in + 1]
            kk = pl.program_id(2)

            @pl.when(kk == 0)
            def _():
                acc_ref[...] = p

            @pl.when(kk > 0)
            def _():
                acc_ref[...] += p

            @pl.when(kk == nk - 1)
            def _():
                finish(acc_ref[...])

    operands = [a, b] + ([res] if has_res else [])
    in_specs = [a_spec, b_spec] + ([r_spec] if has_res else [])
    aliases = {}
    if out_into is not None:
        aliases = {len(operands): 0}
        operands.append(out_into)
        in_specs.append(HBM)
    return pl.pallas_call(
        body, name=name, grid=(m // tm, n // tn, nk), in_specs=in_specs, out_specs=o_spec, out_shape=out_shape,
        scratch_shapes=[pltpu.VMEM((tm, tn), F32)] if nk > 1 else [], input_output_aliases=aliases,
        compiler_params=_params(dimension_semantics=("parallel", "parallel", "arbitrary")),
    )(*operands)


def _mm_swiglu_fwd(h, w_up, *, tm, tn, name):
    m, k = h.shape
    n = w_up.shape[1] // 2
    h_spec = pl.BlockSpec((tm, k), lambda i, j: (i, 0))
    wg_spec = pl.BlockSpec((k, tn), lambda i, j: (0, j))
    wu_spec = pl.BlockSpec((k, tn), lambda i, j: (0, j + n // tn))
    o_spec = pl.BlockSpec((tm, tn), lambda i, j: (i, j))

    def body(h_ref, wg_ref, wu_ref, a_ref, g_ref, u_ref):
        hb = h_ref[...]
        g = _dot(hb, wg_ref[...], _NN)
        u = _dot(hb, wu_ref[...], _NN)
        a_ref[...] = (g * jax.nn.sigmoid(g) * u).astype(BF16)
        g_ref[...] = g.astype(BF16)
        u_ref[...] = u.astype(BF16)

    out = jax.ShapeDtypeStruct((m, n), BF16)
    return pl.pallas_call(
        body, name=name, grid=(m // tm, n // tn), in_specs=[h_spec, wg_spec, wu_spec],
        out_specs=[o_spec] * 3, out_shape=[out] * 3,
        compiler_params=_params(dimension_semantics=("parallel", "parallel")),
    )(h, w_up, w_up)


def _mm_swiglu_bwd(dy, w_down, gate, up, *, alpha, tm, tn, name):
    m, k = dy.shape
    n = w_down.shape[0]
    dy_spec = pl.BlockSpec((tm, k), lambda i, j: (i, 0))
    w_spec = pl.BlockSpec((tn, k), lambda i, j: (j, 0))
    o_spec = pl.BlockSpec((tm, tn), lambda i, j: (i, j))

    def body(dy_ref, w_ref, g_ref, u_ref, dg_ref, du_ref):
        da = _dot(dy_ref[...], w_ref[...], _NT) * alpha
        g = g_ref[...].astype(F32)
        u = u_ref[...].astype(F32)
        sg = jax.nn.sigmoid(g)
        dg_ref[...] = (da * u * (sg * (1.0 + g * (1.0 - sg)))).astype(BF16)
        du_ref[...] = (da * (g * sg)).astype(BF16)

    out = jax.ShapeDtypeStruct((m, n), BF16)
    return pl.pallas_call(
        body, name=name, grid=(m // tm, n // tn), in_specs=[dy_spec, w_spec, o_spec, o_spec],
        out_specs=[o_spec] * 2, out_shape=[out] * 2,
        compiler_params=_params(dimension_semantics=("parallel", "parallel")),
    )(dy, w_down, gate, up)


def _rms_fwd(x, g, *, tt, name):
    t, d = x.shape

    def body(x_ref, g_ref, h_ref):
        xv = x_ref[...]
        rstd = lax.rsqrt(jnp.mean(xv * xv, axis=1, keepdims=True) + RMS_EPS)
        h_ref[...] = (xv * rstd * g_ref[...]).astype(BF16)

    return pl.pallas_call(
        body, name=name, grid=(t // tt,),
        in_specs=[pl.BlockSpec((tt, d), lambda i: (i, 0)), pl.BlockSpec((1, d), lambda i: (0, 0))],
        out_specs=pl.BlockSpec((tt, d), lambda i: (i, 0)), out_shape=jax.ShapeDtypeStruct((t, d), BF16),
        compiler_params=_params(dimension_semantics=("parallel",)),
    )(x, g)


def _rms_bwd(dh, x, g, dres, *, tt, name):
    t, d = x.shape

    def body(dh_ref, x_ref, g_ref, r_ref, dx_ref, dxb_ref, dg_ref):
        xv = x_ref[...]
        rstd = lax.rsqrt(jnp.mean(xv * xv, axis=1, keepdims=True) + RMS_EPS)
        xhat = xv * rstd
        dhv = dh_ref[...]
        dxhat = dhv * g_ref[...]
        dx = r_ref[...] + rstd * (dxhat - xhat * jnp.mean(dxhat * xhat, axis=1, keepdims=True))
        dx_ref[...] = dx
        dxb_ref[...] = dx.astype(BF16)

        @pl.when(pl.program_id(0) == 0)
        def _():
            dg_ref[...] = jnp.zeros_like(dg_ref)

        dg_ref[...] += jnp.sum(dhv * xhat, axis=0, keepdims=True)

    row = pl.BlockSpec((tt, d), lambda i: (i, 0))
    vec = pl.BlockSpec((1, d), lambda i: (0, 0))
    return pl.pallas_call(
        body, name=name, grid=(t // tt,), in_specs=[row, row, vec, row], out_specs=[row, row, vec],
        out_shape=[jax.ShapeDtypeStruct((t, d), F32), jax.ShapeDtypeStruct((t, d), BF16), jax.ShapeDtypeStruct((1, d), F32)],
        compiler_params=_params(dimension_semantics=("arbitrary",)),
    )(dh, x, g, dres)


def _final_loss(x, g, target, *, tt, name):
    t, d = x.shape

    def body(x_ref, g_ref, t_ref, dx_ref, dxb_ref, dg_ref, loss_ref):
        xv = x_ref[...]
        gv = g_ref[...]
        rstd = lax.rsqrt(jnp.mean(xv * xv, axis=1, keepdims=True) + RMS_EPS)
        xhat = xv * rstd
        err = xhat * gv - t_ref[...]
        dy = err * (1.0 / d)
        dxhat = dy * gv
        dx = rstd * (dxhat - xhat * jnp.mean(dxhat * xhat, axis=1, keepdims=True))
        dx_ref[...] = dx
        dxb_ref[...] = dx.astype(BF16)

        @pl.when(pl.program_id(0) == 0)
        def _():
            dg_ref[...] = jnp.zeros_like(dg_ref)
            loss_ref[...] = jnp.zeros_like(loss_ref)

        dg_ref[...] += jnp.sum(dy * xhat, axis=0, keepdims=True)
        part = 0.5 * jnp.sum(jnp.mean(err * err, axis=1, keepdims=True), axis=0, keepdims=True)
        loss_ref[...] += jnp.broadcast_to(part, loss_ref.shape)

    row = pl.BlockSpec((tt, d), lambda i: (i, 0))
    vec = pl.BlockSpec((1, d), lambda i: (0, 0))
    one = pl.BlockSpec((1, LANES), lambda i: (0, 0))
    return pl.pallas_call(
        body, name=name, grid=(t // tt,), in_specs=[row, vec, row], out_specs=[row, row, vec, one],
        out_shape=[jax.ShapeDtypeStruct((t, d), F32), jax.ShapeDtypeStruct((t, d), BF16), jax.ShapeDtypeStruct((1, d), F32),
                   jax.ShapeDtypeStruct((1, LANES), F32)],
        compiler_params=_params(dimension_semantics=("arbitrary",)),
    )(x, g, target)


def _swap_halves(x):
    lane = lax.broadcasted_iota(jnp.int32, x.shape, 1)
    return jnp.where((lane // 32) % 2 == 0, pltpu.roll(x, 96, 1), pltpu.roll(x, 32, 1))


def _rope_tables(s):
    half = HEAD_DIM // 2
    inv_freq = ROPE_THETA ** (-jnp.arange(half, dtype=F32) / half)
    ang = jnp.arange(s).astype(F32)[:, None] * inv_freq[None, :]
    cos, sin = jnp.cos(ang), jnp.sin(ang)
    return jnp.tile(cos, (1, 4)), jnp.concatenate([-sin, sin, -sin, sin], axis=1)


def _dilation_of_tile(p):
    dilated = p < 3 * DIL_HEADS // 2
    g = (p % (DIL_HEADS // 2)) // (DIL_GROUP_HEADS // 2)
    return [(dilated & (g == gi)) | (jnp.logical_not(dilated) if gi == 0 else False) for gi in range(len(DILATIONS))]


def _residue_major(ref, d):
    s = ref.shape[0]
    if d == 1:
        return ref[...]
    return jnp.concatenate([ref[pl.ds(r, s // d, stride=d), :] for r in range(d)], axis=0)


def _split_heads(proj, cos4, sin4, *, n_pairs, rot_pairs, scale_ranges, name):
    b, s, _ = proj.shape

    def body(x_ref, c_ref, s_ref, o_ref):
        p = pl.program_id(1)
        is_q = functools.reduce(jnp.logical_or, [(p >= lo) & (p < hi) for lo, hi in scale_ranges])
        scale = jnp.where(is_q, QK_SCALE, 1.0)

        def put(y):
            o_ref[0] = y[:, :HEAD_DIM].astype(BF16)
            o_ref[1] = y[:, HEAD_DIM:].astype(BF16)

        for d, in_group in zip(DILATIONS, _dilation_of_tile(p)):
            @pl.when(in_group & (p < rot_pairs))
            def _(d=d):
                x = _residue_major(x_ref, d)
                put((x * _residue_major(c_ref, d) + _swap_halves(x) * _residue_major(s_ref, d)) * scale)

            @pl.when(in_group & (p >= rot_pairs))
            def _(d=d):
                put(_residue_major(x_ref, d) * scale)

    tab = pl.BlockSpec((s, LANES), lambda bi, p: (0, 0))
    return pl.pallas_call(
        body, name=name, grid=(b, n_pairs),
        in_specs=[pl.BlockSpec((None, s, LANES), lambda bi, p: (bi, 0, p)), tab, tab],
        out_specs=pl.BlockSpec((None, 2, s, HEAD_DIM), lambda bi, p: (bi, p, 0, 0)),
        out_shape=jax.ShapeDtypeStruct((b, 2 * n_pairs, s, HEAD_DIM), BF16),
        compiler_params=_params(dimension_semantics=("parallel", "parallel")),
    )(proj, cos4, sin4)


def _merge_heads(dheads, cos4, sin4, *, heads_per_row, rot_pairs, scale_pairs, dilated, out_cols, tile_off, into, name):
    b, hpr, r, s, _ = dheads.shape
    n_pairs = hpr * r // 2
    ppr = hpr // 2

    def body(d_ref, c_ref, s_ref, *rest):
        o_ref, t_ref = rest[-2:]
        p = pl.program_id(1)
        scale = jnp.where(p < scale_pairs, QK_SCALE, 1.0)

        def tokens(d):
            dy = jnp.concatenate([d_ref[0], d_ref[1]], axis=1)
            if d == 1:
                return dy
            for res in range(d):
                t_ref[pl.ds(res, s // d, stride=d), :] = dy[res * (s // d):(res + 1) * (s // d), :]
            return t_ref[...]

        groups = _dilation_of_tile(p) if dilated else [p >= 0]
        for d, in_group in zip(DILATIONS, groups):
            @pl.when(in_group & (p < rot_pairs))
            def _(d=d):
                dy = tokens(d)
                o_ref[...] = ((dy * c_ref[...] - _swap_halves(dy) * s_ref[...]) * scale).astype(BF16)

            @pl.when(in_group & (p >= rot_pairs))
            def _(d=d):
                o_ref[...] = (tokens(d) * scale).astype(BF16)

    tab = pl.BlockSpec((s, LANES), lambda bi, p: (0, 0))
    operands = [dheads, cos4, sin4] + ([] if into is None else [into])
    return pl.pallas_call(
        body, name=name, grid=(b, n_pairs),
        in_specs=[pl.BlockSpec((None, 2, None, s, HEAD_DIM), lambda bi, p: (bi, p % ppr, p // ppr, 0, 0)), tab, tab]
        + ([] if into is None else [HBM]),
        out_specs=pl.BlockSpec((None, s, LANES), lambda bi, p: (bi, 0, p + tile_off)),
        out_shape=jax.ShapeDtypeStruct((b, s, out_cols), BF16),
        input_output_aliases={} if into is None else {3: 0},
        scratch_shapes=[pltpu.VMEM((s, LANES), F32)],
        compiler_params=_params(dimension_semantics=("parallel", "parallel")),
    )(*operands)


DIL_TQ = 256


def _dil_block(g, s):
    run = s // DILATIONS[g]
    return DIL_TQ if run <= DIL_TQ else min(run, DIL_TQ + 2 * LANES)


def _dil_keys(g, q0, s):
    run = max(s // DILATIONS[g], DIL_TQ)
    lo = (q0 // run) * run
    return pl.multiple_of(jnp.clip(q0 - LANES, lo, lo + run - _dil_block(g, s)), LANES)


def _dil_band(g, q0, start, shape, s):
    row = q0 + lax.broadcasted_iota(jnp.int32, shape, 0)
    col = start + lax.broadcasted_iota(jnp.int32, shape, 1)
    ok = jnp.abs(row - col) <= DIL_HALF
    run = s // DILATIONS[g]
    if run < DIL_TQ:
        shift = run.bit_length() - 1
        ok = ok & ((row >> shift) == (col >> shift))
    return ok


def _dil_tokens(g, q0, s):
    d = DILATIONS[g]
    if d == 1:
        return [(0, DIL_TQ, pl.ds(q0, DIL_TQ))]
    run = s // d
    n = min(run, DIL_TQ)
    return [(lo, n, pl.ds(((q0 + lo) % run) * d + (q0 + lo) // run, n, stride=d)) for lo in range(0, DIL_TQ, n)]


def _dil_gather(ref, pieces):
    return jnp.concatenate([ref[rows, :] for _, _, rows in pieces], axis=0) if len(pieces) > 1 else ref[pieces[0][2], :]


def _dil_head_spec(part, g, s):
    return pl.BlockSpec((None, None, s, HEAD_DIM), lambda b, j: (b, part * DIL_HEADS + g * DIL_GROUP_HEADS + j, 0, 0))


def _dil_attn_fwd(heads, *, name):
    b, _, s, _ = heads.shape
    n_g = len(DILATIONS)

    def body(*refs):
        qkv = refs[:3 * n_g]
        o_ref, l_ref, og_ref, lg_ref = refs[3 * n_g:]
        for g in range(n_g):
            q_ref, k_ref, v_ref = qkv[3 * g:3 * g + 3]
            width = _dil_block(g, s)

            def step(i, carry, g=g, q_ref=q_ref, k_ref=k_ref, v_ref=v_ref, width=width):
                q0 = pl.multiple_of(i * DIL_TQ, DIL_TQ)
                start = _dil_keys(g, q0, s)
                sc = _dot(q_ref[pl.ds(q0, DIL_TQ), :], k_ref[pl.ds(start, width), :], _NT)
                sc = jnp.where(_dil_band(g, q0, start, sc.shape, s), sc, NEG_INF)
                m = jnp.max(sc, axis=1, keepdims=True)
                p = jnp.exp(sc - m)
                den = jnp.sum(p, axis=1, keepdims=True)
                o = _dot(p.astype(BF16), v_ref[pl.ds(start, width), :], _NN) / den
                lse = m + jnp.log(den)
                for lo, n, rows in _dil_tokens(g, q0, s):
                    og_ref[g, rows, :] = o[lo:lo + n]
                    lg_ref[g, rows, :] = lse[lo:lo + n]
                return carry

            lax.fori_loop(0, s // DIL_TQ, step, 0)
        lses = [lg_ref[g] for g in range(n_g)]
        m = functools.reduce(jnp.maximum, lses)
        ws = [jnp.exp(l - m) for l in lses]
        den = functools.reduce(jnp.add, ws)
        o_ref[...] = (functools.reduce(jnp.add, [w * og_ref[g] for g, w in enumerate(ws)]) / den).astype(o_ref.dtype)
        l_ref[...] = m + jnp.log(den)

    out = pl.BlockSpec((None, None, s, HEAD_DIM), lambda bi, j: (bi, j, 0, 0))
    lse = pl.BlockSpec((None, None, s, 1), lambda bi, j: (bi, j, 0, 0))
    return pl.pallas_call(
        body, name=name, grid=(b, DIL_GROUP_HEADS),
        in_specs=[_dil_head_spec(part, g, s) for g in range(n_g) for part in range(3)],
        out_specs=[out, lse],
        out_shape=[jax.ShapeDtypeStruct((b, DIL_GROUP_HEADS, s, HEAD_DIM), BF16),
                   jax.ShapeDtypeStruct((b, DIL_GROUP_HEADS, s, 1), F32)],
        scratch_shapes=[pltpu.VMEM((n_g, s, HEAD_DIM), F32), pltpu.VMEM((n_g, s, 1), F32)],
        compiler_params=_params(dimension_semantics=("parallel", "parallel")),
    )(*([heads] * (3 * n_g)))


def _dil_attn_bwd(heads, out, lse, dout, *, name):
    b, _, s, _ = heads.shape
    n_g = len(DILATIONS)

    def body(*refs):
        qkv = refs[:3 * n_g]
        o_ref, l_ref, do_ref, d_ref, delta_ref = refs[3 * n_g:]
        d_ref[...] = jnp.zeros_like(d_ref)
        delta_ref[...] = jnp.sum(do_ref[...] * o_ref[...].astype(F32), axis=1, keepdims=True)
        for g in range(n_g):
            q_ref, k_ref, v_ref = qkv[3 * g:3 * g + 3]
            width = _dil_block(g, s)

            def step(i, carry, g=g, q_ref=q_ref, k_ref=k_ref, v_ref=v_ref, width=width):
                q0 = pl.multiple_of(i * DIL_TQ, DIL_TQ)
                start = _dil_keys(g, q0, s)
                win = pl.ds(start, width)
                pieces = _dil_tokens(g, q0, s)
                do_b = _dil_gather(do_ref, pieces).astype(BF16)
                q, k, v = q_ref[pl.ds(q0, DIL_TQ), :], k_ref[win, :], v_ref[win, :]
                sc = _dot(q, k, _NT)
                p = jnp.where(_dil_band(g, q0, start, sc.shape, s), jnp.exp(sc - _dil_gather(l_ref, pieces)), 0.0)
                ds = (p * (_dot(do_b, v, _NT) - _dil_gather(delta_ref, pieces))).astype(BF16)
                d_ref[g, pl.ds(q0, DIL_TQ), :] = _dot(ds, k, _NN)
                d_ref[n_g + g, win, :] += _dot(ds, q, _TN)
                d_ref[2 * n_g + g, win, :] += _dot(p.astype(BF16), do_b, _TN)
                return carry

            lax.fori_loop(0, s // DIL_TQ, step, 0)

    per_head = lambda bi, j: (bi, j, 0, 0)
    return pl.pallas_call(
        body, name=name, grid=(b, DIL_GROUP_HEADS),
        in_specs=[_dil_head_spec(part, g, s) for g in range(n_g) for part in range(3)]
        + [pl.BlockSpec((None, None, s, HEAD_DIM), per_head), pl.BlockSpec((None, None, s, 1), per_head),
           pl.BlockSpec((None, None, s, HEAD_DIM), per_head)],
        out_specs=pl.BlockSpec((None, None, 3 * n_g, s, HEAD_DIM), lambda bi, j: (bi, j, 0, 0, 0)),
        out_shape=jax.ShapeDtypeStruct((b, DIL_GROUP_HEADS, 3 * n_g, s, HEAD_DIM), F32),
        scratch_shapes=[pltpu.VMEM((s, 1), F32)],
        compiler_params=_params(dimension_semantics=("parallel", "parallel")),
    )(*([heads] * (3 * n_g)), out, lse, dout)


NA_BIAS_ROWS = 2 * NA_ROWS - 1
NA_BIAS_COLS = 2 * NA_COLS - 1
NA_BLOCK = 4
NA_SPAN = NA_ROWS + NA_BLOCK - 1
NA_Q = NA_BLOCK * GRID_W
NA_KEYS = NA_SPAN * GRID_W
NA_FORMS = 3


def _na_onehot():
    c = np.arange(GRID_W)[:, None]
    k = np.arange(GRID_W)[None, :]
    lo = np.clip(c - NA_COLS // 2, 0, GRID_W - NA_COLS)
    valid = (k >= lo) & (k < lo + NA_COLS)
    onehot = np.zeros((GRID_W, GRID_W, LANES), np.float32)
    cc, kk = np.nonzero(valid)
    onehot[cc, kk, kk - cc + NA_COLS - 1] = 1.0
    return onehot.reshape(GRID_W * GRID_W, LANES), valid.reshape(1, GRID_W * GRID_W)


def _na_block_rows(n_rows):
    table = np.full((NA_FORMS, NA_BLOCK, NA_SPAN), NA_BIAS_ROWS, np.int64)
    n_blocks = n_rows // NA_BLOCK
    for form, ib in enumerate((0, 1, n_blocks - 1)):
        base = min(max(NA_BLOCK * ib - NA_ROWS // 2, 0), n_rows - NA_SPAN)
        for rl in range(NA_BLOCK):
            r = NA_BLOCK * ib + rl
            row_lo = min(max(r - NA_ROWS // 2, 0), n_rows - NA_ROWS)
            for kl in range(NA_SPAN):
                if row_lo <= base + kl < row_lo + NA_ROWS:
                    table[form, rl, kl] = base + kl - r + NA_ROWS - 1
    return table


def _na_block(ib, n_rows):
    n_blocks = n_rows // NA_BLOCK
    base = jnp.clip(NA_BLOCK * ib - NA_ROWS // 2, 0, n_rows - NA_SPAN)
    return base, jnp.where(ib == 0, 0, jnp.where(ib == n_blocks - 1, 2, 1))


def _na_expand_bias(rel_bias, *, name):
    l, h, nr, nc = rel_bias.shape
    onehot, valid = _na_onehot()
    rb = jnp.pad(rel_bias, ((0, 0), (0, 0), (0, 1), (0, LANES - nc))).reshape(l * h * (nr + 1), LANES)
    live = jnp.asarray(np.tile(np.arange(nr + 1) < nr, l * h).astype(np.float32)[:, None])

    def body(rb_ref, oh_ref, valid_ref, live_ref, e_ref):
        e = lax.dot_general(rb_ref[...], oh_ref[...], _NT, precision=lax.Precision.HIGHEST, preferred_element_type=F32)
        e_ref[...] = jnp.where((valid_ref[...] > 0) & (live_ref[...] > 0), e, NEG_INF)

    e = pl.pallas_call(
        body, name=name, out_shape=jax.ShapeDtypeStruct((l * h * (nr + 1), GRID_W * GRID_W), F32), compiler_params=_params(),
    )(rb, jnp.asarray(onehot), jnp.asarray(valid.astype(np.float32)), live)
    return e.reshape(l, h, nr + 1, GRID_W, GRID_W)


def _na_collapse_bias(de, *, name):
    b, h = de.shape[:2]
    onehot, _ = _na_onehot()
    rows = h * NA_BIAS_ROWS

    def diag(e_ref, oh_ref, o_ref):
        e = e_ref[0]
        for bi in range(1, b):
            e = e + e_ref[bi]
        o_ref[...] = lax.dot_general(e, oh_ref[...], _NN, precision=lax.Precision.HIGHEST, preferred_element_type=F32)

    drb = pl.pallas_call(
        diag, name=name, out_shape=jax.ShapeDtypeStruct((rows, LANES), F32), compiler_params=_params(),
    )(de.reshape(b, rows, GRID_W * GRID_W), jnp.asarray(onehot))
    return drb[:, :NA_BIAS_COLS].reshape(h, NA_BIAS_ROWS, NA_BIAS_COLS)


def _na_tiles(n_rows):
    table = _na_block_rows(n_rows)
    return [(f, rl, kl, int(table[f, rl, kl])) for f in range(NA_FORMS) for rl in range(NA_BLOCK) for kl in range(NA_SPAN)]


def _na_tile(ref, form, rl, kl):
    return ref.at[form, rl * GRID_W:(rl + 1) * GRID_W, kl * GRID_W:(kl + 1) * GRID_W]


def _na_head_spec(part, first, s):
    return pl.BlockSpec((None, None, s, HEAD_DIM), lambda b, h: (b, first + part * NA_HEADS + h, 0, 0))


def _na_attn_fwd(heads, bias, *, first, name):
    b, _, s, _ = heads.shape
    n_rows = s // GRID_W
    tiles = _na_tiles(n_rows)

    def body(q_ref, k_ref, v_ref, e_ref, o_ref, l_ref, b_ref):
        for form, rl, kl, i in tiles:
            _na_tile(b_ref, form, rl, kl)[...] = e_ref[i]

        def step(ib, carry):
            base, form = _na_block(ib, n_rows)
            rows = pl.ds(pl.multiple_of(ib * NA_Q, NA_Q), NA_Q)
            win = pl.ds(pl.multiple_of(base * GRID_W, GRID_W), NA_KEYS)
            sc = _dot(q_ref[rows, :], k_ref[win, :], _NT) + b_ref[form]
            m = jnp.max(sc, axis=1, keepdims=True)
            p = jnp.exp(sc - m)
            den = jnp.sum(p, axis=1, keepdims=True)
            o_ref[rows, :] = (_dot(p.astype(BF16), v_ref[win, :], _NN) / den).astype(o_ref.dtype)
            l_ref[rows, :] = m + jnp.log(den)
            return carry

        lax.fori_loop(0, n_rows // NA_BLOCK, step, 0)

    per_head = lambda bi, h: (bi, h, 0, 0)
    return pl.pallas_call(
        body, name=name, grid=(b, NA_HEADS),
        in_specs=[_na_head_spec(part, first, s) for part in range(3)]
        + [pl.BlockSpec((None, NA_BIAS_ROWS + 1, GRID_W, GRID_W), lambda bi, h: (h, 0, 0, 0))],
        out_specs=[pl.BlockSpec((None, None, s, HEAD_DIM), per_head), pl.BlockSpec((None, None, s, 1), per_head)],
        out_shape=[jax.ShapeDtypeStruct((b, NA_HEADS, s, HEAD_DIM), BF16), jax.ShapeDtypeStruct((b, NA_HEADS, s, 1), F32)],
        scratch_shapes=[pltpu.VMEM((NA_FORMS, NA_Q, NA_KEYS), F32)],
        compiler_params=_params(dimension_semantics=("parallel", "parallel")),
    )(heads, heads, heads, bias)


def _na_attn_bwd(heads, bias, out, lse, dout, *, first, name):
    b, _, s, _ = heads.shape
    n_rows = s // GRID_W
    tiles = _na_tiles(n_rows)

    def body(q_ref, k_ref, v_ref, e_ref, o_ref, l_ref, do_ref, d_ref, de_ref, b_ref, db_ref):
        for form, rl, kl, i in tiles:
            _na_tile(b_ref, form, rl, kl)[...] = e_ref[i]
        d_ref[...] = jnp.zeros_like(d_ref)
        db_ref[...] = jnp.zeros_like(db_ref)

        def step(ib, carry):
            base, form = _na_block(ib, n_rows)
            rows = pl.ds(pl.multiple_of(ib * NA_Q, NA_Q), NA_Q)
            win = pl.ds(pl.multiple_of(base * GRID_W, GRID_W), NA_KEYS)
            q, k, v = q_ref[rows, :], k_ref[win, :], v_ref[win, :]
            do = do_ref[rows, :]
            delta = jnp.sum(do * o_ref[rows, :].astype(F32), axis=1, keepdims=True)
            do_b = do.astype(BF16)
            p = jnp.exp(_dot(q, k, _NT) + b_ref[form] - l_ref[rows, :])
            ds = p * (_dot(do_b, v, _NT) - delta)
            db_ref[form] += ds
            ds_b = ds.astype(BF16)
            d_ref[0, rows, :] = _dot(ds_b, k, _NN)
            d_ref[1, win, :] += _dot(ds_b, q, _TN)
            d_ref[2, win, :] += _dot(p.astype(BF16), do_b, _TN)
            return carry

        lax.fori_loop(0, n_rows // NA_BLOCK, step, 0)
        acc = [None] * NA_BIAS_ROWS
        for form, rl, kl, i in tiles:
            if i < NA_BIAS_ROWS:
                t = _na_tile(db_ref, form, rl, kl)[...]
                acc[i] = t if acc[i] is None else acc[i] + t
        for i in range(NA_BIAS_ROWS):
            de_ref[i] = acc[i]

    per_head = lambda bi, h: (bi, h, 0, 0)
    return pl.pallas_call(
        body, name=name, grid=(b, NA_HEADS),
        in_specs=[_na_head_spec(part, first, s) for part in range(3)]
        + [pl.BlockSpec((None, NA_BIAS_ROWS + 1, GRID_W, GRID_W), lambda bi, h: (h, 0, 0, 0)),
           pl.BlockSpec((None, None, s, HEAD_DIM), per_head), pl.BlockSpec((None, None, s, 1), per_head),
           pl.BlockSpec((None, None, s, HEAD_DIM), per_head)],
        out_specs=[pl.BlockSpec((None, None, 3, s, HEAD_DIM), lambda bi, h: (bi, h, 0, 0, 0)),
                   pl.BlockSpec((None, None, NA_BIAS_ROWS, GRID_W, GRID_W), lambda bi, h: (bi, h, 0, 0, 0))],
        out_shape=[jax.ShapeDtypeStruct((b, NA_HEADS, 3, s, HEAD_DIM), F32),
                   jax.ShapeDtypeStruct((b, NA_HEADS, NA_BIAS_ROWS, GRID_W, GRID_W), F32)],
        scratch_shapes=[pltpu.VMEM((NA_FORMS, NA_Q, NA_KEYS), F32), pltpu.VMEM((NA_FORMS, NA_Q, NA_KEYS), F32)],
        compiler_params=_params(dimension_semantics=("parallel", "parallel")),
    )(heads, heads, heads, bias, out, lse, dout)


GATE_TILE = 256


def _gate_fwd(proj, z, *, gate_col, tt, name):
    _, t, d = z.shape
    nj = d // GATE_TILE
    c0 = gate_col // GATE_TILE

    def body(ga_ref, gb_ref, za_ref, zb_ref, o_ref):
        o_ref[...] = (jax.nn.sigmoid(ga_ref[...]) * za_ref[...] + jax.nn.sigmoid(gb_ref[...]) * zb_ref[...]).astype(BF16)

    return pl.pallas_call(
        body, name=name, grid=(t // tt, nj),
        in_specs=[pl.BlockSpec((tt, GATE_TILE), lambda i, j: (i, c0 + j)),
                  pl.BlockSpec((tt, GATE_TILE), lambda i, j: (i, c0 + nj + j)),
                  pl.BlockSpec((None, tt, GATE_TILE), lambda i, j: (0, i, j)),
                  pl.BlockSpec((None, tt, GATE_TILE), lambda i, j: (1, i, j))],
        out_specs=pl.BlockSpec((tt, GATE_TILE), lambda i, j: (i, j)), out_shape=jax.ShapeDtypeStruct((t, d), BF16),
        compiler_params=_params(dimension_semantics=("parallel", "parallel")),
    )(proj, proj, z, z)


def _gate_bwd(dm, proj, z, *, gate_col, tt, name):
    _, t, d = z.shape
    nj = d // GATE_TILE
    c0 = gate_col // GATE_TILE

    def body(dm_ref, g_ref, z_ref, dz_ref, dg_ref):
        dmv = dm_ref[...]
        sg = jax.nn.sigmoid(g_ref[...])
        dz_ref[...] = (dmv * sg).astype(BF16)
        dg_ref[...] = (dmv * z_ref[...] * sg * (1.0 - sg)).astype(BF16)

    return pl.pallas_call(
        body, name=name, grid=(t // tt, 2 * nj),
        in_specs=[pl.BlockSpec((tt, GATE_TILE), lambda i, j: (i, j % nj)),
                  pl.BlockSpec((tt, GATE_TILE), lambda i, j: (i, c0 + j)),
                  pl.BlockSpec((None, tt, GATE_TILE), lambda i, j: (j // nj, i, j % nj))],
        out_specs=[pl.BlockSpec((None, tt, GATE_TILE), lambda i, j: (j // nj, i, j % nj)),
                   pl.BlockSpec((tt, GATE_TILE), lambda i, j: (i, c0 + j))],
        out_shape=[jax.ShapeDtypeStruct((2, t, d), BF16), jax.ShapeDtypeStruct(proj.shape, BF16)],
        compiler_params=_params(dimension_semantics=("parallel", "parallel")),
    )(dm, proj, z)


def _adamw(w, g, m, v, *, name):
    shape = w.shape
    w2, g2, m2, v2 = (t.reshape(-1, shape[-1]) for t in (w, g, m, v))
    rows, cols = w2.shape
    tr = rows
    for cand in (512, 256, 128, 64, 32, 16, 8):
        if rows % cand == 0:
            tr = cand
            break

    def body(w_ref, g_ref, m_ref, v_ref, d_ref, nm_ref, nv_ref):
        gv = g_ref[...]
        nm = ADAM_B1 * m_ref[...] + (1.0 - ADAM_B1) * gv
        nv = ADAM_B2 * v_ref[...] + (1.0 - ADAM_B2) * (gv * gv)
        m_hat = nm / (1.0 - ADAM_B1 ** ADAM_STEP)
        v_hat = nv / (1.0 - ADAM_B2 ** ADAM_STEP)
        d_ref[...] = -ADAM_LR * (m_hat / (jnp.sqrt(v_hat) + ADAM_EPS) + ADAM_WD * w_ref[...])
        nm_ref[...] = nm
        nv_ref[...] = nv

    blk = pl.BlockSpec((tr, cols), lambda i: (i, 0))
    out = jax.ShapeDtypeStruct((rows, cols), F32)
    res = pl.pallas_call(
        body, name=name, grid=(rows // tr,), in_specs=[blk] * 4, out_specs=[blk] * 3, out_shape=[out] * 3,
        compiler_params=_params(dimension_semantics=("parallel",)),
    )(w2, g2, m2, v2)
    return tuple(t.reshape(shape) for t in res)


def _my_place():
    return lax.axis_index("x"), lax.axis_index("y"), lax.axis_index("c")


def _other_chips(x, y):
    return [(1 - x, y), (x, 1 - y), (1 - x, 1 - y)]


def _chip_no(chip):
    return 2 * chip[0] + chip[1]


def _window(ref, kind, size, chip, lead):
    if kind == "col":
        return ref.at[(*lead, slice(None), pl.ds(pl.multiple_of(chip * size, LANES), size))]
    if kind == "row":
        return ref.at[(*lead, pl.ds(pl.multiple_of(chip * size, BF16_ROWS), size), slice(None))]
    shard = size + HEAD_DIM
    if kind == "win_main":
        return ref.at[(*lead, slice(None), pl.ds(pl.multiple_of(chip * shard + HEAD_DIM * (chip % 2), LANES), size))]
    assert kind == "win_strad"
    return ref.at[(*lead, slice(None), pl.ds(pl.multiple_of(size + 2 * shard * (chip // 2), LANES), LANES))]


def _full_shape(shard, kind):
    _, k, n = shard.shape
    return {"col": (k, N_CHIPS * n), "row": (N_CHIPS * k, n), "win_main": (k, N_CHIPS * (n + HEAD_DIM)),
            "slot": (N_CHIPS, k, n)}[kind]


def _place_own(shard, kind, layer, *, name):
    _, k, n = shard.shape
    tr = _div_tile(k, 512, BF16_ROWS)
    tc = LANES if kind == "win_main" else n
    mine = 2 * lax.axis_index("x") + lax.axis_index("y")
    row0 = mine * (k // tr) if kind == "row" else 0
    col0 = {"col": mine, "row": 0, "slot": 0, "win_main": (mine * (n + HEAD_DIM) + HEAD_DIM * (mine % 2)) // LANES}[kind]
    scalars = jnp.stack([mine, row0, col0]).astype(jnp.int32)

    def body(s_ref, i_ref, o_ref):
        o_ref[...] = i_ref[...]

    if kind == "slot":
        o_spec = pl.BlockSpec((None, tr, tc), lambda i, j, s: (s[0], i, j))
    else:
        o_spec = pl.BlockSpec((tr, tc), lambda i, j, s: (s[1] + i, s[2] + j))
    return pl.pallas_call(
        body, name=name,
        grid_spec=pltpu.PrefetchScalarGridSpec(
            num_scalar_prefetch=1, grid=(k // tr, n // tc),
            in_specs=[pl.BlockSpec((None, tr, tc), lambda i, j, s: (layer, i, j))], out_specs=o_spec),
        out_shape=jax.ShapeDtypeStruct(_full_shape(shard, kind), shard.dtype),
        compiler_params=_params(dimension_semantics=("parallel", "parallel")),
    )(scalars, shard)


class _GatherPlan:
    def __init__(self, src, dst, shapes, kinds, layer, send_sems, recv_sems):
        self.src, self.dst, self.shapes, self.kinds, self.layer = src, dst, shapes, kinds, layer
        self.send_sems, self.recv_sems = send_sems, recv_sems
        self.x, self.y, self.c = _my_place()
        self.mine = 2 * self.x + self.y
        self.chips = _other_chips(self.x, self.y)
        self.n = len(src)

    def half(self, i, chip, half):
        _, k, n = self.shapes[i]
        kind, dst, hk = self.kinds[i], self.dst[i], k // 2
        if kind == "slot":
            return dst.at[chip, pl.ds(pl.multiple_of(half * hk, BF16_ROWS), hk), :]
        if kind == "row":
            return dst.at[pl.ds(pl.multiple_of(chip * k + half * hk, BF16_ROWS), hk), :]
        col0 = chip * n if kind == "col" else chip * (n + HEAD_DIM) + HEAD_DIM * (chip % 2)
        return dst.at[pl.ds(pl.multiple_of(half * hk, BF16_ROWS), hk), pl.ds(pl.multiple_of(col0, LANES), n)]

    def _copy(self, sem, window, to, source=None):
        return pltpu.make_async_remote_copy(src_ref=window if source is None else source, dst_ref=window,
                                            send_sem=self.send_sems.at[sem], recv_sem=self.recv_sems.at[sem],
                                            device_id=to, device_id_type=MESH)

    def sends(self):
        out = []
        for k, chip in enumerate(self.chips):
            for i in range(self.n):
                hk = self.shapes[i][1] // 2
                mine = self.src[i].at[self.layer, pl.ds(pl.multiple_of(self.c * hk, BF16_ROWS), hk), :]
                out.append(self._copy(3 * i + k, self.half(i, self.mine, self.c), (*chip, self.c), source=mine))
        return out

    def arrivals(self):
        return [self._copy(3 * i + k, self.half(i, _chip_no(chip), self.c), (*chip, self.c))
                for k, chip in enumerate(self.chips) for i in range(self.n)]

    def forwards(self, first_sem):
        sibling = (self.x, self.y, 1 - self.c)
        return [self._copy(first_sem + 3 * i + k, self.half(i, _chip_no(chip), self.c), sibling)
                for k, chip in enumerate(self.chips) for i in range(self.n)]

    def forwarded(self, first_sem):
        sibling = (self.x, self.y, 1 - self.c)
        return [self._copy(first_sem + 3 * i + k, self.half(i, _chip_no(chip), 1 - self.c), sibling)
                for k, chip in enumerate(self.chips) for i in range(self.n)]


def _gather_layer(shards, kinds, fulls, layer, *, name):
    n_w = len(shards)
    shapes = [sh.shape for sh in shards]

    def body(*refs):
        plan = _GatherPlan(refs[:n_w], refs[2 * n_w:3 * n_w], shapes, kinds, layer, *refs[3 * n_w:])
        sends = plan.sends()
        for cp in sends:
            cp.start()
        passed = plan.forwards(3 * n_w)
        for landed, onward in zip(plan.arrivals(), passed):
            landed.wait_recv()
            onward.start()
        for cp in plan.forwarded(3 * n_w):
            cp.wait_recv()
        for cp in sends + passed:
            cp.wait_send()

    return pl.pallas_call(
        body, name=name, in_specs=[HBM] * (2 * n_w), out_specs=[HBM] * n_w,
        out_shape=[jax.ShapeDtypeStruct(f.shape, f.dtype) for f in fulls],
        input_output_aliases={n_w + i: i for i in range(n_w)},
        scratch_shapes=[pltpu.SemaphoreType.DMA((6 * n_w,)), pltpu.SemaphoreType.DMA((6 * n_w,))],
    )(*shards, *fulls)


IN_HBM = pl.BlockSpec(memory_space=pltpu.HBM)
IN_SEM = pl.BlockSpec(memory_space=pltpu.SEMAPHORE)
DATAFLOW = pltpu.SideEffectType.DATAFLOW_SIDE_EFFECTING


def _gather_layer_start(shards, kinds, fulls, layer, after, *, name):
    n_w = len(shards)
    shapes = [sh.shape for sh in shards]

    def body(*refs):
        plan = _GatherPlan(refs[:n_w], refs[n_w:2 * n_w], shapes, kinds, layer, refs[2 * n_w + 1], refs[2 * n_w + 2])
        for cp in plan.sends():
            cp.start()
        token = refs[-1]
        token[...] = jnp.zeros_like(token)

    operands = [pltpu.with_memory_space_constraint(a, pltpu.HBM) for a in (*shards, *fulls)]
    res = pl.pallas_call(
        body, name=name, in_specs=[IN_HBM] * (2 * n_w) + [pl.BlockSpec(memory_space=pl.ANY)],
        out_specs=(IN_SEM, IN_SEM, *([IN_HBM] * (2 * n_w)), pl.BlockSpec(memory_space=pltpu.VMEM)),
        out_shape=(pltpu.SemaphoreType.DMA((3 * n_w,)), pltpu.SemaphoreType.DMA((3 * n_w,)),
                   *[pltpu.HBM(a.shape, a.dtype) for a in operands], jax.ShapeDtypeStruct((8, LANES), F32)),
        input_output_aliases={i: 2 + i for i in range(2 * n_w)},
        compiler_params=pltpu.CompilerParams(has_side_effects=DATAFLOW),
    )(*operands, after)
    return res[0], res[1], res[2:2 + n_w], res[2 + n_w:2 + 2 * n_w], res[-1]


def _gather_layer_wait(send_sems, recv_sems, shards, fulls, kinds, layer, after, *, name):
    n_w = len(shards)
    shapes = [sh.shape for sh in shards]

    def body(*refs):
        plan = _GatherPlan(refs[:n_w], refs[n_w:2 * n_w], shapes, kinds, layer, refs[2 * n_w], refs[2 * n_w + 1])
        for cp in plan.sends():
            cp.wait_send()
        for cp in plan.arrivals():
            cp.wait_recv()

    res = pl.pallas_call(
        body, name=name, in_specs=[IN_HBM] * (2 * n_w) + [IN_SEM, IN_SEM, pl.BlockSpec(memory_space=pl.ANY)],
        out_specs=[IN_HBM] * (2 * n_w), out_shape=[pltpu.HBM(a.shape, a.dtype) for a in (*shards, *fulls)],
        input_output_aliases={i: i for i in range(2 * n_w)},
        compiler_params=pltpu.CompilerParams(has_side_effects=DATAFLOW),
    )(*shards, *fulls, send_sems, recv_sems, after)
    return res[n_w:]


def _gather_layer_forward(shapes, kinds, fulls, *, name):
    n_w = len(fulls)

    def body(*refs):
        plan = _GatherPlan([None] * n_w, refs[n_w:2 * n_w], shapes, kinds, 0, *refs[2 * n_w:])
        passed = plan.forwards(0)
        for cp in passed:
            cp.start()
        for cp in plan.forwarded(0):
            cp.wait_recv()
        for cp in passed:
            cp.wait_send()

    return pl.pallas_call(
        body, name=name, in_specs=[HBM] * n_w, out_specs=[HBM] * n_w,
        out_shape=[jax.ShapeDtypeStruct(f.shape, f.dtype) for f in fulls],
        input_output_aliases={i: i for i in range(n_w)},
        scratch_shapes=[pltpu.SemaphoreType.DMA((3 * n_w,)), pltpu.SemaphoreType.DMA((3 * n_w,))],
    )(*fulls)


def _grads_to_sibling(grads, layer, *, name):
    n_w = len(grads)

    def body(*refs):
        src, dst = refs[:n_w], refs[n_w:2 * n_w]
        send_sems, recv_sems = refs[2 * n_w:]
        x, y, c = _my_place()
        cps = [pltpu.make_async_remote_copy(src_ref=src[i], dst_ref=dst[i], send_sem=send_sems.at[i],
                                            recv_sem=recv_sems.at[i], device_id=(x, y, layer), device_id_type=MESH)
               for i in range(n_w)]

        @pl.when(c != layer)
        def _():
            for cp in cps:
                cp.start()
            for cp in cps:
                cp.wait_send()

        @pl.when(c == layer)
        def _():
            for cp in cps:
                cp.wait_recv()

    return pl.pallas_call(
        body, name=name, in_specs=[HBM] * n_w, out_specs=[HBM] * n_w,
        out_shape=[jax.ShapeDtypeStruct(g.shape, g.dtype) for g in grads],
        scratch_shapes=[pltpu.SemaphoreType.DMA((n_w,)), pltpu.SemaphoreType.DMA((n_w,))],
    )(*grads)


def _on_core(layer):
    return (lax.axis_index("c") == layer).astype(jnp.int32).reshape(1)


def _pair_add(mine, other, layer, *, name):
    k, n = mine.shape
    tr = _div_tile(k, 512, BF16_ROWS)

    def body(on_ref, a_ref, b_ref, o_ref):
        @pl.when(on_ref[0] == 1)
        def _():
            o_ref[...] = (a_ref[...].astype(F32) + b_ref[...].astype(F32)).astype(o_ref.dtype)

    blk = pl.BlockSpec((tr, n), lambda i, on: (i * on[0], 0))
    return pl.pallas_call(
        body, name=name,
        grid_spec=pltpu.PrefetchScalarGridSpec(num_scalar_prefetch=1, grid=(k // tr,), in_specs=[blk, blk], out_specs=blk),
        out_shape=jax.ShapeDtypeStruct((k, n), mine.dtype), compiler_params=_params(dimension_semantics=("arbitrary",)),
    )(_on_core(layer), mine, other)


class _ScatterPlan:
    def __init__(self, src, dst, kinds, sizes, layer, send_sems, recv_sems):
        self.src, self.dst, self.kinds, self.sizes, self.layer = src, dst, kinds, sizes, layer
        self.send_sems, self.recv_sems = send_sems, recv_sems
        self.x, self.y, self.c = _my_place()
        self.mine = 2 * self.x + self.y
        self.chips = _other_chips(self.x, self.y)
        self.n = len(src)

    def _copy(self, i, k, chip, window_of, slab):
        return pltpu.make_async_remote_copy(src_ref=_window(self.src[i], self.kinds[i], self.sizes[i], window_of, ()),
                                            dst_ref=self.dst[i].at[slab], send_sem=self.send_sems.at[3 * i + k],
                                            recv_sem=self.recv_sems.at[3 * i + k], device_id=(*chip, self.layer),
                                            device_id_type=MESH)

    def sends(self):
        return [self._copy(i, k, chip, _chip_no(chip), self.mine) for k, chip in enumerate(self.chips) for i in range(self.n)]

    def arrivals(self):
        return [self._copy(i, k, chip, self.mine, _chip_no(chip)) for k, chip in enumerate(self.chips) for i in range(self.n)]


def _slab_shape(p, kind, size):
    return (N_CHIPS,) + {"col": (p.shape[0], size), "row": (size, p.shape[1]), "win_main": (p.shape[0], size),
                         "win_strad": (p.shape[0], LANES)}[kind]


def _grads_to_chips(pairs, kinds, sizes, layer, *, name):
    n_w = len(pairs)

    def body(*refs):
        plan = _ScatterPlan(refs[:n_w], refs[n_w:2 * n_w], kinds, sizes, layer, *refs[2 * n_w:])

        @pl.when(plan.c == layer)
        def _():
            sends = plan.sends()
            for cp in sends:
                cp.start()
            for cp in plan.arrivals():
                cp.wait_recv()
            for cp in sends:
                cp.wait_send()

    return pl.pallas_call(
        body, name=name, in_specs=[HBM] * n_w, out_specs=[HBM] * n_w,
        out_shape=[jax.ShapeDtypeStruct(_slab_shape(p, kind, size), p.dtype) for p, kind, size in zip(pairs, kinds, sizes)],
        scratch_shapes=[pltpu.SemaphoreType.DMA((3 * n_w,)), pltpu.SemaphoreType.DMA((3 * n_w,))],
    )(*pairs)


def _grads_to_chips_start(pairs, kinds, sizes, layer, *, name):
    n_w = len(pairs)

    def body(*refs):
        plan = _ScatterPlan(refs[:n_w], refs[n_w:2 * n_w], kinds, sizes, layer, refs[2 * n_w], refs[2 * n_w + 1])

        @pl.when(plan.c == layer)
        def _():
            for cp in plan.sends():
                cp.start()

    slabs = [lax.empty(_slab_shape(p, kind, size), p.dtype) for p, kind, size in zip(pairs, kinds, sizes)]
    operands = [pltpu.with_memory_space_constraint(a, pltpu.HBM) for a in (*pairs, *slabs)]
    res = pl.pallas_call(
        body, name=name, in_specs=[IN_HBM] * (2 * n_w), out_specs=(IN_SEM, IN_SEM, *([IN_HBM] * (2 * n_w))),
        out_shape=(pltpu.SemaphoreType.DMA((3 * n_w,)), pltpu.SemaphoreType.DMA((3 * n_w,)),
                   *[pltpu.HBM(a.shape, a.dtype) for a in operands]),
        input_output_aliases={i: 2 + i for i in range(2 * n_w)},
        compiler_params=pltpu.CompilerParams(has_side_effects=DATAFLOW),
    )(*operands)
    return res[0], res[1], res[2:2 + n_w], res[2 + n_w:]


def _grads_to_chips_wait(send_sems, recv_sems, pairs, slabs, kinds, sizes, layer, after, *, name):
    n_w = len(pairs)

    def body(*refs):
        plan = _ScatterPlan(refs[:n_w], refs[n_w:2 * n_w], kinds, sizes, layer, refs[2 * n_w], refs[2 * n_w + 1])

        @pl.when(plan.c == layer)
        def _():
            for cp in plan.sends():
                cp.wait_send()
            for cp in plan.arrivals():
                cp.wait_recv()

    res = pl.pallas_call(
        body, name=name, in_specs=[IN_HBM] * (2 * n_w) + [IN_SEM, IN_SEM, pl.BlockSpec(memory_space=pl.ANY)],
        out_specs=[IN_HBM] * (2 * n_w), out_shape=[pltpu.HBM(a.shape, a.dtype) for a in (*pairs, *slabs)],
        input_output_aliases={i: i for i in range(2 * n_w)},
        compiler_params=pltpu.CompilerParams(has_side_effects=DATAFLOW),
    )(*pairs, *slabs, send_sems, recv_sems, after)
    return res[:n_w], res[n_w:]


def _sum_slabs(slabs, pair, kind, size, layer, into, *, name):
    n_s, k, n = slabs.shape
    tr = _div_tile(k, 512, BF16_ROWS)
    tc = n if kind in ("col", "row") else LANES
    x, y, _ = _my_place()
    mine = 2 * x + y
    shard = size + HEAD_DIM
    row0 = mine * (k // tr) if kind == "row" else 0
    col0 = {"col": mine, "row": 0, "win_main": (mine * shard + HEAD_DIM * (mine % 2)) // LANES,
            "win_strad": (size + 2 * shard * (mine // 2)) // LANES}[kind]
    on = _on_core(layer)[0]
    scalars = jnp.stack([mine, row0 * on, col0 * on, on]).astype(jnp.int32)

    def body(s_ref, slab_ref, own_ref, *rest):
        o_ref = rest[-1]
        me = s_ref[0]

        @pl.when(s_ref[3] == 1)
        def _():
            acc = jnp.zeros(o_ref.shape, F32)
            for i in range(n_s):
                acc = acc + jnp.where(me == i, own_ref[...], slab_ref[i]).astype(F32)
            o_ref[...] = acc

    operands = [scalars, slabs, pair] + ([] if into is None else [into])
    return pl.pallas_call(
        body, name=name,
        grid_spec=pltpu.PrefetchScalarGridSpec(
            num_scalar_prefetch=1, grid=(k // tr, n // tc),
            in_specs=[pl.BlockSpec((n_s, tr, tc), lambda i, j, s: (0, i * s[3], j * s[3])),
                      pl.BlockSpec((tr, tc), lambda i, j, s: (s[1] + i * s[3], s[2] + j * s[3]))]
            + ([] if into is None else [HBM]),
            out_specs=pl.BlockSpec((None, tr, tc), lambda i, j, s: (layer, i * s[3], j * s[3]))),
        out_shape=jax.ShapeDtypeStruct((2, k, n), F32),
        input_output_aliases={} if into is None else {3: 0},
        compiler_params=_params(dimension_semantics=("arbitrary", "arbitrary")),
    )(*operands)


def _exchange_layers(bufs, *, name):
    n_w = len(bufs)

    def body(*refs):
        dst = refs[n_w:2 * n_w]
        send_sems, recv_sems = refs[2 * n_w:]
        x, y, c = _my_place()

        def copy(i, layer):
            return pltpu.make_async_remote_copy(src_ref=dst[i].at[layer], dst_ref=dst[i].at[layer], send_sem=send_sems.at[i],
                                                recv_sem=recv_sems.at[i], device_id=(x, y, 1 - c), device_id_type=MESH)

        sends = [copy(i, c) for i in range(n_w)]
        for cp in sends:
            cp.start()
        for i in range(n_w):
            copy(i, 1 - c).wait_recv()
        for cp in sends:
            cp.wait_send()

    return pl.pallas_call(
        body, name=name, in_specs=[HBM] * n_w, out_specs=[HBM] * n_w,
        out_shape=[jax.ShapeDtypeStruct(b.shape, b.dtype) for b in bufs],
        input_output_aliases={i: i for i in range(n_w)},
        scratch_shapes=[pltpu.SemaphoreType.DMA((n_w,)), pltpu.SemaphoreType.DMA((n_w,))],
    )(*bufs)


def _all_sum_small(v, *, name):
    r = v.shape[0]
    relations = [(dx, dy, dc) for dx in (0, 1) for dy in (0, 1) for dc in (0, 1)][1:]

    def body(v_ref, o_ref, buf, send_sems, recv_sems):
        x, y, c = _my_place()
        me = 4 * x + 2 * y + c
        buf[me] = v_ref[...]
        peers = [(x + dx - 2 * x * dx, y + dy - 2 * y * dy, c + dc - 2 * c * dc) for dx, dy, dc in relations]

        def copy(k, slot):
            return pltpu.make_async_remote_copy(src_ref=v_ref, dst_ref=buf.at[slot], send_sem=send_sems.at[k],
                                                recv_sem=recv_sems.at[k], device_id=peers[k], device_id_type=MESH)

        sends = [copy(k, me) for k in range(len(relations))]
        for cp in sends:
            cp.start()
        for k, (px, py, pc) in enumerate(peers):
            copy(k, 4 * px + 2 * py + pc).wait_recv()
        for cp in sends:
            cp.wait_send()
        acc = buf[0]
        for i in range(1, 8):
            acc = acc + buf[i]
        o_ref[...] = acc

    vm = pl.BlockSpec(memory_space=pltpu.VMEM)
    return pl.pallas_call(
        body, name=name, in_specs=[vm], out_specs=vm, out_shape=jax.ShapeDtypeStruct((r, LANES), F32),
        scratch_shapes=[pltpu.VMEM((8, r, LANES), F32), pltpu.SemaphoreType.DMA((7,)), pltpu.SemaphoreType.DMA((7,))],
    )(v)


SHARDED = (("ffn1_w_up", "col"), ("ffn1_w_down", "row"), ("w_in", "win"), ("w_branch_a", "col"),
           ("w_branch_b", "col"), ("w_out", "row"), ("ffn2_w_up", "col"), ("ffn2_w_down", "row"))
REPLICATED = ("ffn1_norm", "mix_norm", "na_rel_bias", "ffn2_norm", "final_norm")


def _weight_pieces(w):
    even = lax.axis_index("y") == 0
    shards, kinds, names = [], [], []
    for name, kind in SHARDED:
        wb = w[name].astype(BF16)
        if kind == "win":
            main = wb.shape[-1] - HEAD_DIM
            assert main % LANES == 0
            zeros = jnp.zeros(wb.shape[:-1] + (HEAD_DIM,), BF16)
            shards += [jnp.where(even, wb[..., :main], wb[..., HEAD_DIM:]),
                       jnp.where(even, jnp.concatenate([wb[..., main:], zeros], -1),
                                 jnp.concatenate([zeros, wb[..., :HEAD_DIM]], -1))]
            kinds += ["win_main", "slot"]
            names += [name, name + "_strad"]
        else:
            shards.append(wb)
            kinds.append(kind)
            names.append(name)
    return names, kinds, shards


def _finish_w_in(full):
    full = dict(full)
    strad = full.pop("w_in_strad")
    main = full["w_in"].shape[1] // N_CHIPS - HEAD_DIM
    for i in range(N_CHIPS // 2):
        lo = main + 2 * (main + HEAD_DIM) * i
        full["w_in"] = full["w_in"].at[:, lo:lo + LANES].set(strad[2 * i] + strad[2 * i + 1])
    return full


def _scatter_pieces(shards):
    names, kinds, sizes, srcs = [], [], [], []
    for name, kind in SHARDED:
        shp = shards[name].shape
        if kind == "win":
            names += [name, name + "_strad"]
            kinds += ["win_main", "win_strad"]
            sizes += [shp[2] - HEAD_DIM] * 2
            srcs += [name, name]
        else:
            names.append(name)
            kinds.append(kind)
            sizes.append(shp[1] if kind == "row" else shp[2])
            srcs.append(name)
    return names, kinds, sizes, srcs


def _pair_sums(grads, layer, tag):
    uniq = list(grads)
    arrived = _grads_to_sibling([grads[n] for n in uniq], layer, name=f"{tag}_to_sibling")
    return {n: _pair_add(grads[n], a, layer, name=f"{tag}_pair_{n}") for n, a in zip(uniq, arrived)}


def _finish_weight_grads(reduced, names):
    out = dict(zip(names, _exchange_layers(reduced, name="grads_layers")))
    strad = out.pop("w_in_strad")
    even = lax.axis_index("y") == 0
    out["w_in"] = jnp.where(even, jnp.concatenate([out["w_in"], strad[..., :HEAD_DIM]], -1),
                            jnp.concatenate([strad[..., HEAD_DIM:], out["w_in"]], -1))
    return out


class _Grads:
    def __init__(self):
        self.arrays = {}

    def put(self, weight, layer, a, b, *, cols=None, col_off=0, **kw):
        self.arrays[weight, layer] = _mm(a, b, mode="tn", out_dtype=BF16, out_cols=cols, out_col_off=col_off,
                                         out_into=self.arrays.get((weight, layer)), **kw)


def _ffn_fwd(x, norm_g, w_up, w_down, tag):
    t, d = x.shape
    f = w_down.shape[0]
    h = _rms_fwd(x, norm_g, tt=512, name=f"{tag}_norm")
    a, gate, up = _mm_swiglu_fwd(h, w_up, tm=_div_tile(t, ROWS_NARROW, 8), tn=MXU_N, name=f"{tag}_up")
    x_out = _mm(a, w_down, mode="nn", out_dtype=F32, tm=_div_tile(t, ROWS_WIDE, 8), tn=d, tk=f, alpha=0.5, res=x, name=f"{tag}_down")
    return x_out, (x, h, a, gate, up)


def _ffn_bwd(dx, dxb, saved, norm_g, w_up, w_down, layer, grads, wname, tag):
    x, h, a, gate, up = saved
    t, d = x.shape
    f = w_down.shape[0]
    tn = _div_tile(f, 1408)
    grads.put(f"{wname}_w_down", layer, a, dxb, tm=tn, tn=d, tk=1024, alpha=0.5, name=f"{tag}_dwd")
    d_gate, d_up = _mm_swiglu_bwd(dxb, w_down, gate, up, alpha=0.5, tm=_div_tile(t, ROWS_NARROW, 8), tn=MXU_N, name=f"{tag}_da")
    grads.put(f"{wname}_w_up", layer, h, d_gate, cols=2 * f, tm=d, tn=tn, tk=1024, name=f"{tag}_dwg")
    grads.put(f"{wname}_w_up", layer, h, d_up, cols=2 * f, col_off=f // tn, tm=d, tn=tn, tk=1024, name=f"{tag}_dwu")
    dh = _mm(d_gate, w_up, mode="nt", out_dtype=F32, tm=_div_tile(t, ROWS_WIDE, 8), tn=d, tk=f, name=f"{tag}_dh1")
    dh = _mm(d_up, w_up, mode="nt", out_dtype=F32, tm=_div_tile(t, ROWS_WIDE, 8), tn=d, tk=f, b_k_off=1, res=dh, name=f"{tag}_dh2")
    return _rms_bwd(dh, x, norm_g, dx, tt=512, name=f"{tag}_dnorm")


def _to_heads(y, b, n_heads):
    t, w = y.shape
    return y.reshape(b, t // b, n_heads, HEAD_DIM).transpose(0, 2, 1, 3)


def _from_heads(y):
    b, n, s, hd = y.shape
    return y.transpose(0, 2, 1, 3).reshape(b * s, n * hd)


N_QKV = 3 * (DIL_HEADS + NA_HEADS) * HEAD_DIM


def _mixer_fwd(x, b, norm_g, full, bias, tabs, tag):
    t, d = x.shape
    s = t // b
    n_in = full["w_in"].shape[1]
    h = _rms_fwd(x, norm_g, tt=512, name=f"{tag}_norm")
    proj = _mm(h, full["w_in"], mode="nn", out_dtype=F32, tm=_div_tile(t, ROWS_NARROW, 8), tn=MXU_N, tk=d, name=f"{tag}_in")
    heads = _split_heads(proj.reshape(b, s, -1), *tabs, n_pairs=N_QKV // LANES, rot_pairs=DIL_HEADS,
                         scale_ranges=((0, DIL_HEADS // 2), (3 * DIL_HEADS // 2, (3 * DIL_HEADS + NA_HEADS) // 2)),
                         name=f"{tag}_heads")
    ya, lse_a = _dil_attn_fwd(heads, name=f"{tag}_dil")
    yb, lse_b = _na_attn_fwd(heads, bias, first=3 * DIL_HEADS, name=f"{tag}_na")
    ya2, yb2 = _from_heads(ya), _from_heads(yb)
    z = _mm(ya2, full["w_branch_a"], mode="nn", out_dtype=F32, tm=_div_tile(t, ROWS_NARROW, 8), tn=MXU_N, tk=ya2.shape[1],
            out_slab=(0, 2), name=f"{tag}_za")
    z = _mm(yb2, full["w_branch_b"], mode="nn", out_dtype=F32, tm=_div_tile(t, ROWS_NARROW, 8), tn=MXU_N, tk=yb2.shape[1],
            out_slab=(1, 2), out_into=z, name=f"{tag}_zb")
    merged = _gate_fwd(proj, z, gate_col=N_QKV, tt=1024, name=f"{tag}_gate")
    x_out = _mm(merged, full["w_out"], mode="nn", out_dtype=F32, tm=_div_tile(t, ROWS_NARROW, 8), tn=MXU_N, tk=d, res=x, name=f"{tag}_out")
    return x_out, (x, h, proj, heads, ya, lse_a, yb, lse_b, ya2, yb2, z, merged)


def _mixer_bwd(dx, dob, b, saved, norm_g, full, layer, bias, tabs, grads, tag):
    x, h, proj, heads, ya, lse_a, yb, lse_b, ya2, yb2, z, merged = saved
    t, d = x.shape
    s = t // b
    n_in = full["w_in"].shape[1]
    grads.put("w_out", layer, merged, dob, tm=d, tn=d, tk=1024, name=f"{tag}_dwo")
    dm = _mm(dob, full["w_out"], mode="nt", out_dtype=F32, tm=_div_tile(t, ROWS_NARROW, 8), tn=MXU_N, tk=d, name=f"{tag}_dm")
    dz, dproj = _gate_bwd(dm, proj, z, gate_col=N_QKV, tt=1024, name=f"{tag}_dgate")
    grads.put("w_branch_a", layer, ya2, dz, b_sel=0, tm=ya2.shape[1], tn=d, tk=1024, name=f"{tag}_dwa")
    grads.put("w_branch_b", layer, yb2, dz, b_sel=1, tm=yb2.shape[1], tn=d, tk=1024, name=f"{tag}_dwb")
    dya = _mm(dz, full["w_branch_a"], mode="nt", out_dtype=F32, tm=_div_tile(t, ROWS_NARROW, 8), tn=MXU_N, tk=d, a_sel=0, name=f"{tag}_dya")
    dyb = _mm(dz, full["w_branch_b"], mode="nt", out_dtype=F32, tm=_div_tile(t, ROWS_NARROW, 8), tn=MXU_N, tk=d, a_sel=1, name=f"{tag}_dyb")
    d_dil = _dil_attn_bwd(heads, ya, lse_a, _to_heads(dya, b, DIL_GROUP_HEADS), name=f"{tag}_ddil")
    d_na, d_bias = _na_attn_bwd(heads, bias, yb, lse_b, _to_heads(dyb, b, NA_HEADS), first=3 * DIL_HEADS, name=f"{tag}_dna")
    dproj = _merge_heads(d_dil, *tabs, heads_per_row=DIL_GROUP_HEADS, rot_pairs=DIL_HEADS, scale_pairs=DIL_HEADS // 2,
                         dilated=True, out_cols=n_in, tile_off=0, into=dproj.reshape(b, s, n_in), name=f"{tag}_dheads_a")
    dproj = _merge_heads(d_na, *tabs, heads_per_row=NA_HEADS, rot_pairs=0, scale_pairs=NA_HEADS // 2, dilated=False,
                         out_cols=n_in, tile_off=3 * DIL_HEADS // 2, into=dproj, name=f"{tag}_dheads_b").reshape(t, n_in)
    grads.put("w_in", layer, h, dproj, tm=_div_tile(d, 512), tn=_div_tile(n_in, 2944), tk=1024, name=f"{tag}_dwin")
    dh = _mm(dproj, full["w_in"], mode="nt", out_dtype=F32, tm=_div_tile(t, ROWS_WIDE, 8), tn=d, tk=_div_tile(n_in, 2944), name=f"{tag}_dh")
    dx_in, dxb_in, d_norm = _rms_bwd(dh, x, norm_g, dx, tt=512, name=f"{tag}_dnorm")
    d_rb = _na_collapse_bias(d_bias, name=f"{tag}_dbias")
    return dx_in, dxb_in, d_norm, d_rb


def kernel(x, ffn1_norm, ffn1_w_up, ffn1_w_down, mix_norm, w_in, na_rel_bias, w_branch_a, w_branch_b, w_out, ffn2_norm, ffn2_w_up, ffn2_w_down, final_norm, loss_target, m_ffn1_norm, m_ffn1_w_up, m_ffn1_w_down, m_mix_norm, m_w_in, m_na_rel_bias, m_w_branch_a, m_w_branch_b, m_w_out, m_ffn2_norm, m_ffn2_w_up, m_ffn2_w_down, m_final_norm, v_ffn1_norm, v_ffn1_w_up, v_ffn1_w_down, v_mix_norm, v_w_in, v_na_rel_bias, v_w_branch_a, v_w_branch_b, v_w_out, v_ffn2_norm, v_ffn2_w_up, v_ffn2_w_down, v_final_norm):
    w = dict(ffn1_norm=ffn1_norm, ffn1_w_up=ffn1_w_up, ffn1_w_down=ffn1_w_down, mix_norm=mix_norm, w_in=w_in,
             na_rel_bias=na_rel_bias, w_branch_a=w_branch_a, w_branch_b=w_branch_b, w_out=w_out, ffn2_norm=ffn2_norm,
             ffn2_w_up=ffn2_w_up, ffn2_w_down=ffn2_w_down, final_norm=final_norm)
    mom = dict(ffn1_norm=m_ffn1_norm, ffn1_w_up=m_ffn1_w_up, ffn1_w_down=m_ffn1_w_down, mix_norm=m_mix_norm, w_in=m_w_in,
               na_rel_bias=m_na_rel_bias, w_branch_a=m_w_branch_a, w_branch_b=m_w_branch_b, w_out=m_w_out,
               ffn2_norm=m_ffn2_norm, ffn2_w_up=m_ffn2_w_up, ffn2_w_down=m_ffn2_w_down, final_norm=m_final_norm)
    var = dict(ffn1_norm=v_ffn1_norm, ffn1_w_up=v_ffn1_w_up, ffn1_w_down=v_ffn1_w_down, mix_norm=v_mix_norm, w_in=v_w_in,
               na_rel_bias=v_na_rel_bias, w_branch_a=v_w_branch_a, w_branch_b=v_w_branch_b, w_out=v_w_out,
               ffn2_norm=v_ffn2_norm, ffn2_w_up=v_ffn2_w_up, ffn2_w_down=v_ffn2_w_down, final_norm=v_final_norm)
    b, s, d = x.shape
    t = b * s
    depth = ffn1_norm.shape[0]
    assert depth == 2, "core c of a chip sends / reduces layer c"
    shards = {name: w[name] for name, _ in SHARDED}

    names, kinds, pieces = _weight_pieces(w)
    by_layer = [[p[l:l + 1] for p in pieces] for l in range(depth)]
    shapes = [p.shape for p in by_layer[0]]
    own = [[_place_own(p, kind, 0, name=f"own{l}_{nm}") for nm, kind, p in zip(names, kinds, by_layer[l])] for l in range(depth)]
    gathered = _gather_layer(by_layer[0], kinds, own[0], 0, name="gather_l0")
    full = [_finish_w_in(zip(names, gathered)), None]
    send_sems, recv_sems, in_flight, landing, token = _gather_layer_start(by_layer[1], kinds, own[1], 0, gathered[0],
                                                                          name="gather_l1_start")
    tabs = _rope_tables(s)
    bias = _na_expand_bias(na_rel_bias, name="na_bias")

    xc = x.reshape(t, d)
    saved = []
    for l in range(depth):
        gain = ffn1_norm[l:l + 1]
        if l == 0:
            gain = gain + token[:1, :1]
        else:
            landed = _gather_layer_wait(send_sems, recv_sems, in_flight, landing, kinds, 0, xc, name="gather_l1_wait")
            full[1] = _finish_w_in(zip(names, _gather_layer_forward(shapes, kinds, landed, name="gather_l1_forward")))
        xc, s1 = _ffn_fwd(xc, gain, full[l]["ffn1_w_up"], full[l]["ffn1_w_down"], f"l{l}_ffn1")
        xc, s2 = _mixer_fwd(xc, b, mix_norm[l:l + 1], full[l], bias[l], tabs, f"l{l}_mix")
        xc, s3 = _ffn_fwd(xc, ffn2_norm[l:l + 1], full[l]["ffn2_w_up"], full[l]["ffn2_w_down"], f"l{l}_ffn2")
        saved.append((s1, s2, s3))

    dx, dxb, d_final, loss_part = _final_loss(xc, final_norm.reshape(1, d), loss_target.reshape(t, d), tt=512, name="final_loss")
    grads = _Grads()
    piece_names, piece_kinds, piece_sizes, piece_srcs = _scatter_pieces(shards)
    scattered = []

    def scatter(layer, weights, last):
        tag = f"grads{layer}_{weights[0]}"
        pair = _pair_sums({wn: grads.arrays[wn, layer] for wn in weights}, layer, tag)
        idx = [i for i, src in enumerate(piece_srcs) if src in weights]
        pairs, kinds, sizes = [pair[piece_srcs[i]] for i in idx], [piece_kinds[i] for i in idx], [piece_sizes[i] for i in idx]
        if last:
            scattered.append((layer, idx, (pairs, _grads_to_chips(pairs, kinds, sizes, layer, name=f"{tag}_to_chips"))))
        else:
            scattered.append((layer, idx, _grads_to_chips_start(pairs, kinds, sizes, layer, name=f"{tag}_to_chips_start")))
    small = {name: [None] * depth for name in REPLICATED[:-1]}
    for l in reversed(range(depth)):
        s1, s2, s3 = saved[l]
        dx, dxb, small["ffn2_norm"][l] = _ffn_bwd(dx, dxb, s3, ffn2_norm[l:l + 1], full[l]["ffn2_w_up"], full[l]["ffn2_w_down"],
                                                  l, grads, "ffn2", f"l{l}_ffn2")
        scatter(l, ["ffn2_w_up", "ffn2_w_down"], False)
        dx, dxb, small["mix_norm"][l], small["na_rel_bias"][l] = _mixer_bwd(
            dx, dxb, b, s2, mix_norm[l:l + 1], full[l], l, bias[l], tabs, grads, f"l{l}_mix")
        scatter(l, ["w_in", "w_branch_a", "w_branch_b", "w_out"], False)
        dx, dxb, small["ffn1_norm"][l] = _ffn_bwd(dx, dxb, s1, ffn1_norm[l:l + 1], full[l]["ffn1_w_up"], full[l]["ffn1_w_down"],
                                                  l, grads, "ffn1", f"l{l}_ffn1")
        scatter(l, ["ffn1_w_up", "ffn1_w_down"], l == 0)
    grad_x = dx.reshape(b, s, d)
    reduced = [None] * len(piece_names)
    for layer, idx, state in scattered:
        if len(state) > 2:
            state = _grads_to_chips_wait(*state, [piece_kinds[i] for i in idx], [piece_sizes[i] for i in idx], layer, dx,
                                         name=f"grads{layer}_{piece_names[idx[0]]}_to_chips_wait")
        for i, p, sl in zip(idx, *state):
            reduced[i] = _sum_slabs(sl, p, piece_kinds[i], piece_sizes[i], layer, reduced[i],
                                    name=f"grads{layer}_sum_{piece_names[i]}")
    g_out = _finish_weight_grads(reduced, piece_names)

    parts = [jnp.stack(small[name]).reshape(-1) for name in REPLICATED[:-1]] + [d_final.reshape(-1), loss_part[0, :1]]
    sizes = [v.shape[0] for v in parts]
    flat = jnp.concatenate(parts)
    flat = jnp.pad(flat, (0, -flat.shape[0] % (8 * LANES)))
    small_sum = _all_sum_small(flat.reshape(-1, LANES), name="small_all_sum").reshape(-1)
    off = 0
    for name, n in zip(REPLICATED, sizes[:-1]):
        g_out[name] = small_sum[off:off + n].reshape(w[name].shape)
        off += n
    loss = small_sum[off]

    names = list(w)
    delta, new_m, new_v = {}, {}, {}
    for name in names:
        delta[name], new_m[name], new_v[name] = _adamw(w[name], g_out[name], mom[name], var[name], name=f"adamw_{name}")
    return (loss, grad_x, *[g_out[n] for n in names], *[delta[n] for n in names], *[new_m[n] for n in names],
            *[new_v[n] for n in names])
```

```python
import functools

import numpy as np
import jax
import jax.numpy as jnp
from jax import lax
from jax.experimental import pallas as pl
from jax.experimental.pallas import tpu as pltpu

F32, BF16 = jnp.float32, jnp.bfloat16
MESH = pl.DeviceIdType.MESH

HEAD_DIM = 64
DILATIONS = (1, 4, 16)
DIL_HALF = 64
DIL_GROUP_HEADS = 4
DIL_HEADS = 12
NA_HEADS = 8
GRID_W = 64
NA_ROWS = 8
NA_COLS = 16
ROPE_THETA = 10000.0
RMS_EPS = 1e-6
NEG_INF = -1e30
ADAM_LR, ADAM_B1, ADAM_B2, ADAM_EPS, ADAM_WD, ADAM_STEP = 0.001, 0.9, 0.999, 1e-08, 0.01, 10
QK_SCALE = HEAD_DIM ** -0.5

N_CHIPS = 4
LANES = 128
BF16_ROWS = 16
VMEM_LIMIT = 56 * 1024 * 1024
MXU_N = 256
ROWS_NARROW = 2048
ROWS_WIDE = 512

_NN = (((1,), (0,)), ((), ()))
_NT = (((1,), (1,)), ((), ()))
_TN = (((0,), (0,)), ((), ()))

HBM = pl.BlockSpec(memory_space=pl.ANY)


def _params(**kw):
    return pltpu.CompilerParams(vmem_limit_bytes=VMEM_LIMIT, **kw)


def _dot(a, b, dims):
    return lax.dot_general(a, b, dims, preferred_element_type=F32)


def _div_tile(n, cap, mult=LANES):
    best = None
    for t in range(mult, min(n, cap) + 1, mult):
        if n % t == 0:
            best = t
    return n if best is None else best


def _stacked(block, index, sel):
    if sel is None:
        return pl.BlockSpec(block, index)
    return pl.BlockSpec((None,) + block, lambda *g: (sel,) + index(*g))


def _mm(a, b, *, mode, out_dtype, tm, tn, tk, name, alpha=1.0, res=None, a_sel=None, b_sel=None, b_k_off=0,
        out_slab=None, out_cols=None, out_col_off=0, out_into=None):
    a2, b2 = a.shape[-2:], b.shape[-2:]
    if mode == "nn":
        (m, k), n = a2, b2[1]
        a_spec = _stacked((tm, tk), lambda i, j, kk: (i, kk), a_sel)
        b_spec = _stacked((tk, tn), lambda i, j, kk: (kk + b_k_off, j), b_sel)
        dims = _NN
    elif mode == "nt":
        (m, k), n = a2, b2[0]
        a_spec = _stacked((tm, tk), lambda i, j, kk: (i, kk), a_sel)
        b_spec = _stacked((tn, tk), lambda i, j, kk: (j, kk + b_k_off), b_sel)
        dims = _NT
    else:
        (k, m), n = a2, b2[1]
        a_spec = _stacked((tk, tm), lambda i, j, kk: (kk, i), a_sel)
        b_spec = _stacked((tk, tn), lambda i, j, kk: (kk + b_k_off, j), b_sel)
        dims = _TN
    assert m % tm == 0 and n % tn == 0 and k % tk == 0, (name, a.shape, b.shape)
    nk = k // tk
    has_res = res is not None
    if out_slab is None:
        o_spec = pl.BlockSpec((tm, tn), lambda i, j, kk: (i, j + out_col_off))
        out_shape = jax.ShapeDtypeStruct((m, n if out_cols is None else out_cols), out_dtype)
    else:
        o_spec = _stacked((tm, tn), lambda i, j, kk: (i, j + out_col_off), out_slab[0])
        out_shape = jax.ShapeDtypeStruct((out_slab[1], m, n if out_cols is None else out_cols), out_dtype)
    r_spec = pl.BlockSpec((tm, tn), lambda i, j, kk: (i, j))
    n_in = 2 + has_res + (out_into is not None)

    def body(*refs):
        a_ref, b_ref = refs[0], refs[1]
        r_ref = refs[2] if has_res else None
        o_ref = refs[n_in]
        p = _dot(a_ref[...], b_ref[...], dims)

        def finish(acc):
            y = acc * alpha if alpha != 1.0 else acc
            if has_res:
                y = y + r_ref[...].astype(F32)
            o_ref[...] = y.astype(o_ref.dtype)

        if nk == 1:
            finish(p)
        else:
            acc_ref = refs[n_in + 1]
            kk = pl.program_id(2)

            @pl.when(kk == 0)
            def _():
                acc_ref[...] = p

            @pl.when(kk > 0)
            def _():
                acc_ref[...] += p

            @pl.when(kk == nk - 1)
            def _():
                finish(acc_ref[...])

    operands = [a, b] + ([res] if has_res else [])
    in_specs = [a_spec, b_spec] + ([r_spec] if has_res else [])
    aliases = {}
    if out_into is not None:
        aliases = {len(operands): 0}
        operands.append(out_into)
        in_specs.append(HBM)
    return pl.pallas_call(
        body, name=name, grid=(m // tm, n // tn, nk), in_specs=in_specs, out_specs=o_spec, out_shape=out_shape,
        scratch_shapes=[pltpu.VMEM((tm, tn), F32)] if nk > 1 else [], input_output_aliases=aliases,
        compiler_params=_params(dimension_semantics=("parallel", "parallel", "arbitrary")),
    )(*operands)


def _mm_swiglu_fwd(h, w_up, *, tm, tn, name):
    m, k = h.shape
    n = w_up.shape[1] // 2
    h_spec = pl.BlockSpec((tm, k), lambda i, j: (i, 0))
    wg_spec = pl.BlockSpec((k, tn), lambda i, j: (0, j))
    wu_spec = pl.BlockSpec((k, tn), lambda i, j: (0, j + n // tn))
    o_spec = pl.BlockSpec((tm, tn), lambda i, j: (i, j))

    def body(h_ref, wg_ref, wu_ref, a_ref, g_ref, u_ref):
        hb = h_ref[...]
        g = _dot(hb, wg_ref[...], _NN)
        u = _dot(hb, wu_ref[...], _NN)
        a_ref[...] = (g * jax.nn.sigmoid(g) * u).astype(BF16)
        g_ref[...] = g.astype(BF16)
        u_ref[...] = u.astype(BF16)

    out = jax.ShapeDtypeStruct((m, n), BF16)
    return pl.pallas_call(
        body, name=name, grid=(m // tm, n // tn), in_specs=[h_spec, wg_spec, wu_spec],
        out_specs=[o_spec] * 3, out_shape=[out] * 3,
        compiler_params=_params(dimension_semantics=("parallel", "parallel")),
    )(h, w_up, w_up)


def _mm_swiglu_bwd(dy, w_down, gate, up, *, alpha, tm, tn, name):
    m, k = dy.shape
    n = w_down.shape[0]
    dy_spec = pl.BlockSpec((tm, k), lambda i, j: (i, 0))
    w_spec = pl.BlockSpec((tn, k), lambda i, j: (j, 0))
    o_spec = pl.BlockSpec((tm, tn), lambda i, j: (i, j))

    def body(dy_ref, w_ref, g_ref, u_ref, dg_ref, du_ref):
        da = _dot(dy_ref[...], w_ref[...], _NT) * alpha
        g = g_ref[...].astype(F32)
        u = u_ref[...].astype(F32)
        sg = jax.nn.sigmoid(g)
        dg_ref[...] = (da * u * (sg * (1.0 + g * (1.0 - sg)))).astype(BF16)
        du_ref[...] = (da * (g * sg)).astype(BF16)

    out = jax.ShapeDtypeStruct((m, n), BF16)
    return pl.pallas_call(
        body, name=name, grid=(m // tm, n // tn), in_specs=[dy_spec, w_spec, o_spec, o_spec],
        out_specs=[o_spec] * 2, out_shape=[out] * 2,
        compiler_params=_params(dimension_semantics=("parallel", "parallel")),
    )(dy, w_down, gate, up)


def _rms_fwd(x, g, *, tt, name):
    t, d = x.shape

    def body(x_ref, g_ref, h_ref):
        xv = x_ref[...]
        rstd = lax.rsqrt(jnp.mean(xv * xv, axis=1, keepdims=True) + RMS_EPS)
        h_ref[...] = (xv * rstd * g_ref[...]).astype(BF16)

    return pl.pallas_call(
        body, name=name, grid=(t // tt,),
        in_specs=[pl.BlockSpec((tt, d), lambda i: (i, 0)), pl.BlockSpec((1, d), lambda i: (0, 0))],
        out_specs=pl.BlockSpec((tt, d), lambda i: (i, 0)), out_shape=jax.ShapeDtypeStruct((t, d), BF16),
        compiler_params=_params(dimension_semantics=("parallel",)),
    )(x, g)


def _rms_bwd(dh, x, g, dres, *, tt, name):
    t, d = x.shape

    def body(dh_ref, x_ref, g_ref, r_ref, dx_ref, dxb_ref, dg_ref):
        xv = x_ref[...]
        rstd = lax.rsqrt(jnp.mean(xv * xv, axis=1, keepdims=True) + RMS_EPS)
        xhat = xv * rstd
        dhv = dh_ref[...]
        dxhat = dhv * g_ref[...]
        dx = r_ref[...] + rstd * (dxhat - xhat * jnp.mean(dxhat * xhat, axis=1, keepdims=True))
        dx_ref[...] = dx
        dxb_ref[...] = dx.astype(BF16)

        @pl.when(pl.program_id(0) == 0)
        def _():
            dg_ref[...] = jnp.zeros_like(dg_ref)

        dg_ref[...] += jnp.sum(dhv * xhat, axis=0, keepdims=True)

    row = pl.BlockSpec((tt, d), lambda i: (i, 0))
    vec = pl.BlockSpec((1, d), lambda i: (0, 0))
    return pl.pallas_call(
        body, name=name, grid=(t // tt,), in_specs=[row, row, vec, row], out_specs=[row, row, vec],
        out_shape=[jax.ShapeDtypeStruct((t, d), F32), jax.ShapeDtypeStruct((t, d), BF16), jax.ShapeDtypeStruct((1, d), F32)],
        compiler_params=_params(dimension_semantics=("arbitrary",)),
    )(dh, x, g, dres)


def _final_loss(x, g, target, *, tt, name):
    t, d = x.shape

    def body(x_ref, g_ref, t_ref, dx_ref, dxb_ref, dg_ref, loss_ref):
        xv = x_ref[...]
        gv = g_ref[...]
        rstd = lax.rsqrt(jnp.mean(xv * xv, axis=1, keepdims=True) + RMS_EPS)
        xhat = xv * rstd
        err = xhat * gv - t_ref[...]
        dy = err * (1.0 / d)
        dxhat = dy * gv
        dx = rstd * (dxhat - xhat * jnp.mean(dxhat * xhat, axis=1, keepdims=True))
        dx_ref[...] = dx
        dxb_ref[...] = dx.astype(BF16)

        @pl.when(pl.program_id(0) == 0)
        def _():
            dg_ref[...] = jnp.zeros_like(dg_ref)
            loss_ref[...] = jnp.zeros_like(loss_ref)

        dg_ref[...] += jnp.sum(dy * xhat, axis=0, keepdims=True)
        part = 0.5 * jnp.sum(jnp.mean(err * err, axis=1, keepdims=True), axis=0, keepdims=True)
        loss_ref[...] += jnp.broadcast_to(part, loss_ref.shape)

    row = pl.BlockSpec((tt, d), lambda i: (i, 0))
    vec = pl.BlockSpec((1, d), lambda i: (0, 0))
    one = pl.BlockSpec((1, LANES), lambda i: (0, 0))
    return pl.pallas_call(
        body, name=name, grid=(t // tt,), in_specs=[row, vec, row], out_specs=[row, row, vec, one],
        out_shape=[jax.ShapeDtypeStruct((t, d), F32), jax.ShapeDtypeStruct((t, d), BF16), jax.ShapeDtypeStruct((1, d), F32),
                   jax.ShapeDtypeStruct((1, LANES), F32)],
        compiler_params=_params(dimension_semantics=("arbitrary",)),
    )(x, g, target)


def _swap_halves(x):
    lane = lax.broadcasted_iota(jnp.int32, x.shape, 1)
    return jnp.where((lane // 32) % 2 == 0, pltpu.roll(x, 96, 1), pltpu.roll(x, 32, 1))


def _rope_tables(s):
    half = HEAD_DIM // 2
    inv_freq = ROPE_THETA ** (-jnp.arange(half, dtype=F32) / half)
    ang = jnp.arange(s).astype(F32)[:, None] * inv_freq[None, :]
    cos, sin = jnp.cos(ang), jnp.sin(ang)
    return jnp.tile(cos, (1, 4)), jnp.concatenate([-sin, sin, -sin, sin], axis=1)


def _dilation_of_tile(p):
    dilated = p < 3 * DIL_HEADS // 2
    g = (p % (DIL_HEADS // 2)) // (DIL_GROUP_HEADS // 2)
    return [(dilated & (g == gi)) | (jnp.logical_not(dilated) if gi == 0 else False) for gi in range(len(DILATIONS))]


def _residue_major(ref, d):
    s = ref.shape[0]
    if d == 1:
        return ref[...]
    return jnp.concatenate([ref[pl.ds(r, s // d, stride=d), :] for r in range(d)], axis=0)


def _split_heads(proj, cos4, sin4, *, n_pairs, rot_pairs, scale_ranges, name):
    b, s, _ = proj.shape

    def body(x_ref, c_ref, s_ref, o_ref):
        p = pl.program_id(1)
        is_q = functools.reduce(jnp.logical_or, [(p >= lo) & (p < hi) for lo, hi in scale_ranges])
        scale = jnp.where(is_q, QK_SCALE, 1.0)

        def put(y):
            o_ref[0] = y[:, :HEAD_DIM].astype(BF16)
            o_ref[1] = y[:, HEAD_DIM:].astype(BF16)

        for d, in_group in zip(DILATIONS, _dilation_of_tile(p)):
            @pl.when(in_group & (p < rot_pairs))
            def _(d=d):
                x = _residue_major(x_ref, d)
                put((x * _residue_major(c_ref, d) + _swap_halves(x) * _residue_major(s_ref, d)) * scale)

            @pl.when(in_group & (p >= rot_pairs))
            def _(d=d):
                put(_residue_major(x_ref, d) * scale)

    tab = pl.BlockSpec((s, LANES), lambda bi, p: (0, 0))
    return pl.pallas_call(
        body, name=name, grid=(b, n_pairs),
        in_specs=[pl.BlockSpec((None, s, LANES), lambda bi, p: (bi, 0, p)), tab, tab],
        out_specs=pl.BlockSpec((None, 2, s, HEAD_DIM), lambda bi, p: (bi, p, 0, 0)),
        out_shape=jax.ShapeDtypeStruct((b, 2 * n_pairs, s, HEAD_DIM), BF16),
        compiler_params=_params(dimension_semantics=("parallel", "parallel")),
    )(proj, cos4, sin4)


def _merge_heads(dheads, cos4, sin4, *, heads_per_row, rot_pairs, scale_pairs, dilated, out_cols, tile_off, into, name):
    b, hpr, r, s, _ = dheads.shape
    n_pairs = hpr * r // 2
    ppr = hpr // 2

    def body(d_ref, c_ref, s_ref, *rest):
        o_ref, t_ref = rest[-2:]
        p = pl.program_id(1)
        scale = jnp.where(p < scale_pairs, QK_SCALE, 1.0)

        def tokens(d):
            dy = jnp.concatenate([d_ref[0], d_ref[1]], axis=1)
            if d == 1:
                return dy
            for res in range(d):
                t_ref[pl.ds(res, s // d, stride=d), :] = dy[res * (s // d):(res + 1) * (s // d), :]
            return t_ref[...]

        groups = _dilation_of_tile(p) if dilated else [p >= 0]
        for d, in_group in zip(DILATIONS, groups):
            @pl.when(in_group & (p < rot_pairs))
            def _(d=d):
                dy = tokens(d)
                o_ref[...] = ((dy * c_ref[...] - _swap_halves(dy) * s_ref[...]) * scale).astype(BF16)

            @pl.when(in_group & (p >= rot_pairs))
            def _(d=d):
                o_ref[...] = (tokens(d) * scale).astype(BF16)

    tab = pl.BlockSpec((s, LANES), lambda bi, p: (0, 0))
    operands = [dheads, cos4, sin4] + ([] if into is None else [into])
    return pl.pallas_call(
        body, name=name, grid=(b, n_pairs),
        in_specs=[pl.BlockSpec((None, 2, None, s, HEAD_DIM), lambda bi, p: (bi, p % ppr, p // ppr, 0, 0)), tab, tab]
        + ([] if into is None else [HBM]),
        out_specs=pl.BlockSpec((None, s, LANES), lambda bi, p: (bi, 0, p + tile_off)),
        out_shape=jax.ShapeDtypeStruct((b, s, out_cols), BF16),
        input_output_aliases={} if into is None else {3: 0},
        scratch_shapes=[pltpu.VMEM((s, LANES), F32)],
        compiler_params=_params(dimension_semantics=("parallel", "parallel")),
    )(*operands)


DIL_TQ = 256


def _dil_block(g, s):
    run = s // DILATIONS[g]
    return DIL_TQ if run <= DIL_TQ else min(run, DIL_TQ + 2 * LANES)


def _dil_keys(g, q0, s):
    run = max(s // DILATIONS[g], DIL_TQ)
    lo = (q0 // run) * run
    return pl.multiple_of(jnp.clip(q0 - LANES, lo, lo + run - _dil_block(g, s)), LANES)


def _dil_band(g, q0, start, shape, s):
    row = q0 + lax.broadcasted_iota(jnp.int32, shape, 0)
    col = start + lax.broadcasted_iota(jnp.int32, shape, 1)
    ok = jnp.abs(row - col) <= DIL_HALF
    run = s // DILATIONS[g]
    if run < DIL_TQ:
        shift = run.bit_length() - 1
        ok = ok & ((row >> shift) == (col >> shift))
    return ok


def _dil_tokens(g, q0, s):
    d = DILATIONS[g]
    if d == 1:
        return [(0, DIL_TQ, pl.ds(q0, DIL_TQ))]
    run = s // d
    n = min(run, DIL_TQ)
    return [(lo, n, pl.ds(((q0 + lo) % run) * d + (q0 + lo) // run, n, stride=d)) for lo in range(0, DIL_TQ, n)]


def _dil_gather(ref, pieces):
    return jnp.concatenate([ref[rows, :] for _, _, rows in pieces], axis=0) if len(pieces) > 1 else ref[pieces[0][2], :]


def _dil_head_spec(part, g, s):
    return pl.BlockSpec((None, None, s, HEAD_DIM), lambda b, j: (b, part * DIL_HEADS + g * DIL_GROUP_HEADS + j, 0, 0))


def _dil_attn_fwd(heads, *, name):
    b, _, s, _ = heads.shape
    n_g = len(DILATIONS)

    def body(*refs):
        qkv = refs[:3 * n_g]
        o_ref, l_ref, og_ref, lg_ref = refs[3 * n_g:]
        for g in range(n_g):
            q_ref, k_ref, v_ref = qkv[3 * g:3 * g + 3]
            width = _dil_block(g, s)

            def step(i, carry, g=g, q_ref=q_ref, k_ref=k_ref, v_ref=v_ref, width=width):
                q0 = pl.multiple_of(i * DIL_TQ, DIL_TQ)
                start = _dil_keys(g, q0, s)
                sc = _dot(q_ref[pl.ds(q0, DIL_TQ), :], k_ref[pl.ds(start, width), :], _NT)
                sc = jnp.where(_dil_band(g, q0, start, sc.shape, s), sc, NEG_INF)
                m = jnp.max(sc, axis=1, keepdims=True)
                p = jnp.exp(sc - m)
                den = jnp.sum(p, axis=1, keepdims=True)
                o = _dot(p.astype(BF16), v_ref[pl.ds(start, width), :], _NN) / den
                lse = m + jnp.log(den)
                for lo, n, rows in _dil_tokens(g, q0, s):
                    og_ref[g, rows, :] = o[lo:lo + n]
                    lg_ref[g, rows, :] = lse[lo:lo + n]
                return carry

            lax.fori_loop(0, s // DIL_TQ, step, 0)
        lses = [lg_ref[g] for g in range(n_g)]
        m = functools.reduce(jnp.maximum, lses)
        ws = [jnp.exp(l - m) for l in lses]
        den = functools.reduce(jnp.add, ws)
        o_ref[...] = (functools.reduce(jnp.add, [w * og_ref[g] for g, w in enumerate(ws)]) / den).astype(o_ref.dtype)
        l_ref[...] = m + jnp.log(den)

    out = pl.BlockSpec((None, None, s, HEAD_DIM), lambda bi, j: (bi, j, 0, 0))
    lse = pl.BlockSpec((None, None, s, 1), lambda bi, j: (bi, j, 0, 0))
    return pl.pallas_call(
        body, name=name, grid=(b, DIL_GROUP_HEADS),
        in_specs=[_dil_head_spec(part, g, s) for g in range(n_g) for part in range(3)],
        out_specs=[out, lse],
        out_shape=[jax.ShapeDtypeStruct((b, DIL_GROUP_HEADS, s, HEAD_DIM), BF16),
                   jax.ShapeDtypeStruct((b, DIL_GROUP_HEADS, s, 1), F32)],
        scratch_shapes=[pltpu.VMEM((n_g, s, HEAD_DIM), F32), pltpu.VMEM((n_g, s, 1), F32)],
        compiler_params=_params(dimension_semantics=("parallel", "parallel")),
    )(*([heads] * (3 * n_g)))


def _dil_attn_bwd(heads, out, lse, dout, *, name):
    b, _, s, _ = heads.shape
    n_g = len(DILATIONS)

    def body(*refs):
        qkv = refs[:3 * n_g]
        o_ref, l_ref, do_ref, d_ref, delta_ref = refs[3 * n_g:]
        d_ref[...] = jnp.zeros_like(d_ref)
        delta_ref[...] = jnp.sum(do_ref[...] * o_ref[...].astype(F32), axis=1, keepdims=True)
        for g in range(n_g):
            q_ref, k_ref, v_ref = qkv[3 * g:3 * g + 3]
            width = _dil_block(g, s)

            def step(i, carry, g=g, q_ref=q_ref, k_ref=k_ref, v_ref=v_ref, width=width):
                q0 = pl.multiple_of(i * DIL_TQ, DIL_TQ)
                start = _dil_keys(g, q0, s)
                win = pl.ds(start, width)
                pieces = _dil_tokens(g, q0, s)
                do_b = _dil_gather(do_ref, pieces).astype(BF16)
                q, k, v = q_ref[pl.ds(q0, DIL_TQ), :], k_ref[win, :], v_ref[win, :]
                sc = _dot(q, k, _NT)
                p = jnp.where(_dil_band(g, q0, start, sc.shape, s), jnp.exp(sc - _dil_gather(l_ref, pieces)), 0.0)
                ds = (p * (_dot(do_b, v, _NT) - _dil_gather(delta_ref, pieces))).astype(BF16)
                d_ref[g, pl.ds(q0, DIL_TQ), :] = _dot(ds, k, _NN)
                d_ref[n_g + g, win, :] += _dot(ds, q, _TN)
                d_ref[2 * n_g + g, win, :] += _dot(p.astype(BF16), do_b, _TN)
                return carry

            lax.fori_loop(0, s // DIL_TQ, step, 0)

    per_head = lambda bi, j: (bi, j, 0, 0)
    return pl.pallas_call(
        body, name=name, grid=(b, DIL_GROUP_HEADS),
        in_specs=[_dil_head_spec(part, g, s) for g in range(n_g) for part in range(3)]
        + [pl.BlockSpec((None, None, s, HEAD_DIM), per_head), pl.BlockSpec((None, None, s, 1), per_head),
           pl.BlockSpec((None, None, s, HEAD_DIM), per_head)],
        out_specs=pl.BlockSpec((None, None, 3 * n_g, s, HEAD_DIM), lambda bi, j: (bi, j, 0, 0, 0)),
        out_shape=jax.ShapeDtypeStruct((b, DIL_GROUP_HEADS, 3 * n_g, s, HEAD_DIM), F32),
        scratch_shapes=[pltpu.VMEM((s, 1), F32)],
        compiler_params=_params(dimension_semantics=("parallel", "parallel")),
    )(*([heads] * (3 * n_g)), out, lse, dout)


NA_BIAS_ROWS = 2 * NA_ROWS - 1
NA_BIAS_COLS = 2 * NA_COLS - 1
NA_BLOCK = 4
NA_SPAN = NA_ROWS + NA_BLOCK - 1
NA_Q = NA_BLOCK * GRID_W
NA_KEYS = NA_SPAN * GRID_W
NA_FORMS = 3


def _na_onehot():
    c = np.arange(GRID_W)[:, None]
    k = np.arange(GRID_W)[None, :]
    lo = np.clip(c - NA_COLS // 2, 0, GRID_W - NA_COLS)
    valid = (k >= lo) & (k < lo + NA_COLS)
    onehot = np.zeros((GRID_W, GRID_W, LANES), np.float32)
    cc, kk = np.nonzero(valid)
    onehot[cc, kk, kk - cc + NA_COLS - 1] = 1.0
    return onehot.reshape(GRID_W * GRID_W, LANES), valid.reshape(1, GRID_W * GRID_W)


def _na_block_rows(n_rows):
    table = np.full((NA_FORMS, NA_BLOCK, NA_SPAN), NA_BIAS_ROWS, np.int64)
    n_blocks = n_rows // NA_BLOCK
    for form, ib in enumerate((0, 1, n_blocks - 1)):
        base = min(max(NA_BLOCK * ib - NA_ROWS // 2, 0), n_rows - NA_SPAN)
        for rl in range(NA_BLOCK):
            r = NA_BLOCK * ib + rl
            row_lo = min(max(r - NA_ROWS // 2, 0), n_rows - NA_ROWS)
            for kl in range(NA_SPAN):
                if row_lo <= base + kl < row_lo + NA_ROWS:
                    table[form, rl, kl] = base + kl - r + NA_ROWS - 1
    return table


def _na_block(ib, n_rows):
    n_blocks = n_rows // NA_BLOCK
    base = jnp.clip(NA_BLOCK * ib - NA_ROWS // 2, 0, n_rows - NA_SPAN)
    return base, jnp.where(ib == 0, 0, jnp.where(ib == n_blocks - 1, 2, 1))


def _na_expand_bias(rel_bias, *, name):
    l, h, nr, nc = rel_bias.shape
    onehot, valid = _na_onehot()
    rb = jnp.pad(rel_bias, ((0, 0), (0, 0), (0, 1), (0, LANES - nc))).reshape(l * h * (nr + 1), LANES)
    live = jnp.asarray(np.tile(np.arange(nr + 1) < nr, l * h).astype(np.float32)[:, None])

    def body(rb_ref, oh_ref, valid_ref, live_ref, e_ref):
        e = lax.dot_general(rb_ref[...], oh_ref[...], _NT, precision=lax.Precision.HIGHEST, preferred_element_type=F32)
        e_ref[...] = jnp.where((valid_ref[...] > 0) & (live_ref[...] > 0), e, NEG_INF)

    e = pl.pallas_call(
        body, name=name, out_shape=jax.ShapeDtypeStruct((l * h * (nr + 1), GRID_W * GRID_W), F32), compiler_params=_params(),
    )(rb, jnp.asarray(onehot), jnp.asarray(valid.astype(np.float32)), live)
    return e.reshape(l, h, nr + 1, GRID_W, GRID_W)


def _na_collapse_bias(de, *, name):
    b, h = de.shape[:2]
    onehot, _ = _na_onehot()
    rows = h * NA_BIAS_ROWS

    def diag(e_ref, oh_ref, o_ref):
        e = e_ref[0]
        for bi in range(1, b):
            e = e + e_ref[bi]
        o_ref[...] = lax.dot_general(e, oh_ref[...], _NN, precision=lax.Precision.HIGHEST, preferred_element_type=F32)

    drb = pl.pallas_call(
        diag, name=name, out_shape=jax.ShapeDtypeStruct((rows, LANES), F32), compiler_params=_params(),
    )(de.reshape(b, rows, GRID_W * GRID_W), jnp.asarray(onehot))
    return drb[:, :NA_BIAS_COLS].reshape(h, NA_BIAS_ROWS, NA_BIAS_COLS)


def _na_tiles(n_rows):
    table = _na_block_rows(n_rows)
    return [(f, rl, kl, int(table[f, rl, kl])) for f in range(NA_FORMS) for rl in range(NA_BLOCK) for kl in range(NA_SPAN)]


def _na_tile(ref, form, rl, kl):
    return ref.at[form, rl * GRID_W:(rl + 1) * GRID_W, kl * GRID_W:(kl + 1) * GRID_W]


def _na_head_spec(part, first, s):
    return pl.BlockSpec((None, None, s, HEAD_DIM), lambda b, h: (b, first + part * NA_HEADS + h, 0, 0))


def _na_attn_fwd(heads, bias, *, first, name):
    b, _, s, _ = heads.shape
    n_rows = s // GRID_W
    tiles = _na_tiles(n_rows)

    def body(q_ref, k_ref, v_ref, e_ref, o_ref, l_ref, b_ref):
        for form, rl, kl, i in tiles:
            _na_tile(b_ref, form, rl, kl)[...] = e_ref[i]

        def step(ib, carry):
            base, form = _na_block(ib, n_rows)
            rows = pl.ds(pl.multiple_of(ib * NA_Q, NA_Q), NA_Q)
            win = pl.ds(pl.multiple_of(base * GRID_W, GRID_W), NA_KEYS)
            sc = _dot(q_ref[rows, :], k_ref[win, :], _NT) + b_ref[form]
            m = jnp.max(sc, axis=1, keepdims=True)
            p = jnp.exp(sc - m)
            den = jnp.sum(p, axis=1, keepdims=True)
            o_ref[rows, :] = (_dot(p.astype(BF16), v_ref[win, :], _NN) / den).astype(o_ref.dtype)
            l_ref[rows, :] = m + jnp.log(den)
            return carry

        lax.fori_loop(0, n_rows // NA_BLOCK, step, 0)

    per_head = lambda bi, h: (bi, h, 0, 0)
    return pl.pallas_call(
        body, name=name, grid=(b, NA_HEADS),
        in_specs=[_na_head_spec(part, first, s) for part in range(3)]
        + [pl.BlockSpec((None, NA_BIAS_ROWS + 1, GRID_W, GRID_W), lambda bi, h: (h, 0, 0, 0))],
        out_specs=[pl.BlockSpec((None, None, s, HEAD_DIM), per_head), pl.BlockSpec((None, None, s, 1), per_head)],
        out_shape=[jax.ShapeDtypeStruct((b, NA_HEADS, s, HEAD_DIM), BF16), jax.ShapeDtypeStruct((b, NA_HEADS, s, 1), F32)],
        scratch_shapes=[pltpu.VMEM((NA_FORMS, NA_Q, NA_KEYS), F32)],
        compiler_params=_params(dimension_semantics=("parallel", "parallel")),
    )(heads, heads, heads, bias)


def _na_attn_bwd(heads, bias, out, lse, dout, *, first, name):
    b, _, s, _ = heads.shape
    n_rows = s // GRID_W
    tiles = _na_tiles(n_rows)

    def body(q_ref, k_ref, v_ref, e_ref, o_ref, l_ref, do_ref, d_ref, de_ref, b_ref, db_ref):
        for form, rl, kl, i in tiles:
            _na_tile(b_ref, form, rl, kl)[...] = e_ref[i]
        d_ref[...] = jnp.zeros_like(d_ref)
        db_ref[...] = jnp.zeros_like(db_ref)

        def step(ib, carry):
            base, form = _na_block(ib, n_rows)
            rows = pl.ds(pl.multiple_of(ib * NA_Q, NA_Q), NA_Q)
            win = pl.ds(pl.multiple_of(base * GRID_W, GRID_W), NA_KEYS)
            q, k, v = q_ref[rows, :], k_ref[win, :], v_ref[win, :]
            do = do_ref[rows, :]
            delta = jnp.sum(do * o_ref[rows, :].astype(F32), axis=1, keepdims=True)
            do_b = do.astype(BF16)
            p = jnp.exp(_dot(q, k, _NT) + b_ref[form] - l_ref[rows, :])
            ds = p * (_dot(do_b, v, _NT) - delta)
            db_ref[form] += ds
            ds_b = ds.astype(BF16)
            d_ref[0, rows, :] = _dot(ds_b, k, _NN)
            d_ref[1, win, :] += _dot(ds_b, q, _TN)
            d_ref[2, win, :] += _dot(p.astype(BF16), do_b, _TN)
            return carry

        lax.fori_loop(0, n_rows // NA_BLOCK, step, 0)
        acc = [None] * NA_BIAS_ROWS
        for form, rl, kl, i in tiles:
            if i < NA_BIAS_ROWS:
                t = _na_tile(db_ref, form, rl, kl)[...]
                acc[i] = t if acc[i] is None else acc[i] + t
        for i in range(NA_BIAS_ROWS):
            de_ref[i] = acc[i]

    per_head = lambda bi, h: (bi, h, 0, 0)
    return pl.pallas_call(
        body, name=name, grid=(b, NA_HEADS),
        in_specs=[_na_head_spec(part, first, s) for part in range(3)]
        + [pl.BlockSpec((None, NA_BIAS_ROWS + 1, GRID_W, GRID_W), lambda bi, h: (h, 0, 0, 0)),
           pl.BlockSpec((None, None, s, HEAD_DIM), per_head), pl.BlockSpec((None, None, s, 1), per_head),
           pl.BlockSpec((None, None, s, HEAD_DIM), per_head)],
        out_specs=[pl.BlockSpec((None, None, 3, s, HEAD_DIM), lambda bi, h: (bi, h, 0, 0, 0)),
                   pl.BlockSpec((None, None, NA_BIAS_ROWS, GRID_W, GRID_W), lambda bi, h: (bi, h, 0, 0, 0))],
        out_shape=[jax.ShapeDtypeStruct((b, NA_HEADS, 3, s, HEAD_DIM), F32),
                   jax.ShapeDtypeStruct((b, NA_HEADS, NA_BIAS_ROWS, GRID_W, GRID_W), F32)],
        scratch_shapes=[pltpu.VMEM((NA_FORMS, NA_Q, NA_KEYS), F32), pltpu.VMEM((NA_FORMS, NA_Q, NA_KEYS), F32)],
        compiler_params=_params(dimension_semantics=("parallel", "parallel")),
    )(heads, heads, heads, bias, out, lse, dout)


GATE_TILE = 256


def _gate_fwd(proj, z, *, gate_col, tt, name):
    _, t, d = z.shape
    nj = d // GATE_TILE
    c0 = gate_col // GATE_TILE

    def body(ga_ref, gb_ref, za_ref, zb_ref, o_ref):
        o_ref[...] = (jax.nn.sigmoid(ga_ref[...]) * za_ref[...] + jax.nn.sigmoid(gb_ref[...]) * zb_ref[...]).astype(BF16)

    return pl.pallas_call(
        body, name=name, grid=(t // tt, nj),
        in_specs=[pl.BlockSpec((tt, GATE_TILE), lambda i, j: (i, c0 + j)),
                  pl.BlockSpec((tt, GATE_TILE), lambda i, j: (i, c0 + nj + j)),
                  pl.BlockSpec((None, tt, GATE_TILE), lambda i, j: (0, i, j)),
                  pl.BlockSpec((None, tt, GATE_TILE), lambda i, j: (1, i, j))],
        out_specs=pl.BlockSpec((tt, GATE_TILE), lambda i, j: (i, j)), out_shape=jax.ShapeDtypeStruct((t, d), BF16),
        compiler_params=_params(dimension_semantics=("parallel", "parallel")),
    )(proj, proj, z, z)


def _gate_bwd(dm, proj, z, *, gate_col, tt, name):
    _, t, d = z.shape
    nj = d // GATE_TILE
    c0 = gate_col // GATE_TILE

    def body(dm_ref, g_ref, z_ref, dz_ref, dg_ref):
        dmv = dm_ref[...]
        sg = jax.nn.sigmoid(g_ref[...])
        dz_ref[...] = (dmv * sg).astype(BF16)
        dg_ref[...] = (dmv * z_ref[...] * sg * (1.0 - sg)).astype(BF16)

    return pl.pallas_call(
        body, name=name, grid=(t // tt, 2 * nj),
        in_specs=[pl.BlockSpec((tt, GATE_TILE), lambda i, j: (i, j % nj)),
                  pl.BlockSpec((tt, GATE_TILE), lambda i, j: (i, c0 + j)),
                  pl.BlockSpec((None, tt, GATE_TILE), lambda i, j: (j // nj, i, j % nj))],
        out_specs=[pl.BlockSpec((None, tt, GATE_TILE), lambda i, j: (j // nj, i, j % nj)),
                   pl.BlockSpec((tt, GATE_TILE), lambda i, j: (i, c0 + j))],
        out_shape=[jax.ShapeDtypeStruct((2, t, d), BF16), jax.ShapeDtypeStruct(proj.shape, BF16)],
        compiler_params=_params(dimension_semantics=("parallel", "parallel")),
    )(dm, proj, z)


def _adamw(w, g, m, v, *, name):
    shape = w.shape
    w2, g2, m2, v2 = (t.reshape(-1, shape[-1]) for t in (w, g, m, v))
    rows, cols = w2.shape
    tr = rows
    for cand in (512, 256, 128, 64, 32, 16, 8):
        if rows % cand == 0:
            tr = cand
            break

    def body(w_ref, g_ref, m_ref, v_ref, d_ref, nm_ref, nv_ref):
        gv = g_ref[...]
        nm = ADAM_B1 * m_ref[...] + (1.0 - ADAM_B1) * gv
        nv = ADAM_B2 * v_ref[...] + (1.0 - ADAM_B2) * (gv * gv)
        m_hat = nm / (1.0 - ADAM_B1 ** ADAM_STEP)
        v_hat = nv / (1.0 - ADAM_B2 ** ADAM_STEP)
        d_ref[...] = -ADAM_LR * (m_hat / (jnp.sqrt(v_hat) + ADAM_EPS) + ADAM_WD * w_ref[...])
        nm_ref[...] = nm
        nv_ref[...] = nv

    blk = pl.BlockSpec((tr, cols), lambda i: (i, 0))
    out = jax.ShapeDtypeStruct((rows, cols), F32)
    res = pl.pallas_call(
        body, name=name, grid=(rows // tr,), in_specs=[blk] * 4, out_specs=[blk] * 3, out_shape=[out] * 3,
        compiler_params=_params(dimension_semantics=("parallel",)),
    )(w2, g2, m2, v2)
    return tuple(t.reshape(shape) for t in res)


def _my_place():
    return lax.axis_index("x"), lax.axis_index("y"), lax.axis_index("c")


def _other_chips(x, y):
    return [(1 - x, y), (x, 1 - y), (1 - x, 1 - y)]


def _chip_no(chip):
    return 2 * chip[0] + chip[1]


def _window(ref, kind, size, chip, lead):
    if kind == "col":
        return ref.at[(*lead, slice(None), pl.ds(pl.multiple_of(chip * size, LANES), size))]
    if kind == "row":
        return ref.at[(*lead, pl.ds(pl.multiple_of(chip * size, BF16_ROWS), size), slice(None))]
    shard = size + HEAD_DIM
    if kind == "win_main":
        return ref.at[(*lead, slice(None), pl.ds(pl.multiple_of(chip * shard + HEAD_DIM * (chip % 2), LANES), size))]
    assert kind == "win_strad"
    return ref.at[(*lead, slice(None), pl.ds(pl.multiple_of(size + 2 * shard * (chip // 2), LANES), LANES))]


def _full_shape(shard, kind):
    _, k, n = shard.shape
    return {"col": (k, N_CHIPS * n), "row": (N_CHIPS * k, n), "win_main": (k, N_CHIPS * (n + HEAD_DIM)),
            "slot": (N_CHIPS, k, n)}[kind]


def _place_own(shard, kind, layer, *, name):
    _, k, n = shard.shape
    tr = _div_tile(k, 512, BF16_ROWS)
    tc = LANES if kind == "win_main" else n
    mine = 2 * lax.axis_index("x") + lax.axis_index("y")
    row0 = mine * (k // tr) if kind == "row" else 0
    col0 = {"col": mine, "row": 0, "slot": 0, "win_main": (mine * (n + HEAD_DIM) + HEAD_DIM * (mine % 2)) // LANES}[kind]
    scalars = jnp.stack([mine, row0, col0]).astype(jnp.int32)

    def body(s_ref, i_ref, o_ref):
        o_ref[...] = i_ref[...]

    if kind == "slot":
        o_spec = pl.BlockSpec((None, tr, tc), lambda i, j, s: (s[0], i, j))
    else:
        o_spec = pl.BlockSpec((tr, tc), lambda i, j, s: (s[1] + i, s[2] + j))
    return pl.pallas_call(
        body, name=name,
        grid_spec=pltpu.PrefetchScalarGridSpec(
            num_scalar_prefetch=1, grid=(k // tr, n // tc),
            in_specs=[pl.BlockSpec((None, tr, tc), lambda i, j, s: (layer, i, j))], out_specs=o_spec),
        out_shape=jax.ShapeDtypeStruct(_full_shape(shard, kind), shard.dtype),
        compiler_params=_params(dimension_semantics=("parallel", "parallel")),
    )(scalars, shard)


class _GatherPlan:
    def __init__(self, src, dst, shapes, kinds, layer, send_sems, recv_sems):
        self.src, self.dst, self.shapes, self.kinds, self.layer = src, dst, shapes, kinds, layer
        self.send_sems, self.recv_sems = send_sems, recv_sems
        self.x, self.y, self.c = _my_place()
        self.mine = 2 * self.x + self.y
        self.chips = _other_chips(self.x, self.y)
        self.n = len(src)

    def half(self, i, chip, half):
        _, k, n = self.shapes[i]
        kind, dst, hk = self.kinds[i], self.dst[i], k // 2
        if kind == "slot":
            return dst.at[chip, pl.ds(pl.multiple_of(half * hk, BF16_ROWS), hk), :]
        if kind == "row":
            return dst.at[pl.ds(pl.multiple_of(chip * k + half * hk, BF16_ROWS), hk), :]
        col0 = chip * n if kind == "col" else chip * (n + HEAD_DIM) + HEAD_DIM * (chip % 2)
        return dst.at[pl.ds(pl.multiple_of(half * hk, BF16_ROWS), hk), pl.ds(pl.multiple_of(col0, LANES), n)]

    def _copy(self, sem, window, to, source=None):
        return pltpu.make_async_remote_copy(src_ref=window if source is None else source, dst_ref=window,
                                            send_sem=self.send_sems.at[sem], recv_sem=self.recv_sems.at[sem],
                                            device_id=to, device_id_type=MESH)

    def sends(self):
        out = []
        for k, chip in enumerate(self.chips):
            for i in range(self.n):
                hk = self.shapes[i][1] // 2
                mine = self.src[i].at[self.layer, pl.ds(pl.multiple_of(self.c * hk, BF16_ROWS), hk), :]
                out.append(self._copy(3 * i + k, self.half(i, self.mine, self.c), (*chip, self.c), source=mine))
        return out

    def arrivals(self):
        return [self._copy(3 * i + k, self.half(i, _chip_no(chip), self.c), (*chip, self.c))
                for k, chip in enumerate(self.chips) for i in range(self.n)]

    def forwards(self, first_sem):
        sibling = (self.x, self.y, 1 - self.c)
        return [self._copy(first_sem + 3 * i + k, self.half(i, _chip_no(chip), self.c), sibling)
                for k, chip in enumerate(self.chips) for i in range(self.n)]

    def forwarded(self, first_sem):
        sibling = (self.x, self.y, 1 - self.c)
        return [self._copy(first_sem + 3 * i + k, self.half(i, _chip_no(chip), 1 - self.c), sibling)
                for k, chip in enumerate(self.chips) for i in range(self.n)]


def _gather_layer(shards, kinds, fulls, layer, *, name):
    n_w = len(shards)
    shapes = [sh.shape for sh in shards]

    def body(*refs):
        plan = _GatherPlan(refs[:n_w], refs[2 * n_w:3 * n_w], shapes, kinds, layer, *refs[3 * n_w:])
        sends = plan.sends()
        for cp in sends:
            cp.start()
        passed = plan.forwards(3 * n_w)
        for landed, onward in zip(plan.arrivals(), passed):
            landed.wait_recv()
            onward.start()
        for cp in plan.forwarded(3 * n_w):
            cp.wait_recv()
        for cp in sends + passed:
            cp.wait_send()

    return pl.pallas_call(
        body, name=name, in_specs=[HBM] * (2 * n_w), out_specs=[HBM] * n_w,
        out_shape=[jax.ShapeDtypeStruct(f.shape, f.dtype) for f in fulls],
        input_output_aliases={n_w + i: i for i in range(n_w)},
        scratch_shapes=[pltpu.SemaphoreType.DMA((6 * n_w,)), pltpu.SemaphoreType.DMA((6 * n_w,))],
    )(*shards, *fulls)


IN_HBM = pl.BlockSpec(memory_space=pltpu.HBM)
IN_SEM = pl.BlockSpec(memory_space=pltpu.SEMAPHORE)
DATAFLOW = pltpu.SideEffectType.DATAFLOW_SIDE_EFFECTING


def _gather_layer_start(shards, kinds, fulls, layer, after, *, name):
    n_w = len(shards)
    shapes = [sh.shape for sh in shards]

    def body(*refs):
        plan = _GatherPlan(refs[:n_w], refs[n_w:2 * n_w], shapes, kinds, layer, refs[2 * n_w + 1], refs[2 * n_w + 2])
        for cp in plan.sends():
            cp.start()
        token = refs[-1]
        token[...] = jnp.zeros_like(token)

    operands = [pltpu.with_memory_space_constraint(a, pltpu.HBM) for a in (*shards, *fulls)]
    res = pl.pallas_call(
        body, name=name, in_specs=[IN_HBM] * (2 * n_w) + [pl.BlockSpec(memory_space=pl.ANY)],
        out_specs=(IN_SEM, IN_SEM, *([IN_HBM] * (2 * n_w)), pl.BlockSpec(memory_space=pltpu.VMEM)),
        out_shape=(pltpu.SemaphoreType.DMA((3 * n_w,)), pltpu.SemaphoreType.DMA((3 * n_w,)),
                   *[pltpu.HBM(a.shape, a.dtype) for a in operands], jax.ShapeDtypeStruct((8, LANES), F32)),
        input_output_aliases={i: 2 + i for i in range(2 * n_w)},
        compiler_params=pltpu.CompilerParams(has_side_effects=DATAFLOW),
    )(*operands, after)
    return res[0], res[1], res[2:2 + n_w], res[2 + n_w:2 + 2 * n_w], res[-1]


def _gather_layer_wait(send_sems, recv_sems, shards, fulls, kinds, layer, after, *, name):
    n_w = len(shards)
    shapes = [sh.shape for sh in shards]

    def body(*refs):
        plan = _GatherPlan(refs[:n_w], refs[n_w:2 * n_w], shapes, kinds, layer, refs[2 * n_w], refs[2 * n_w + 1])
        for cp in plan.sends():
            cp.wait_send()
        for cp in plan.arrivals():
            cp.wait_recv()

    res = pl.pallas_call(
        body, name=name, in_specs=[IN_HBM] * (2 * n_w) + [IN_SEM, IN_SEM, pl.BlockSpec(memory_space=pl.ANY)],
        out_specs=[IN_HBM] * (2 * n_w), out_shape=[pltpu.HBM(a.shape, a.dtype) for a in (*shards, *fulls)],
        input_output_aliases={i: i for i in range(2 * n_w)},
        compiler_params=pltpu.CompilerParams(has_side_effects=DATAFLOW),
    )(*shards, *fulls, send_sems, recv_sems, after)
    return res[n_w:]


def _gather_layer_forward(shapes, kinds, fulls, *, name):
    n_w = len(fulls)

    def body(*refs):
        plan = _GatherPlan([None] * n_w, refs[n_w:2 * n_w], shapes, kinds, 0, *refs[2 * n_w:])
        passed = plan.forwards(0)
        for cp in passed:
            cp.start()
        for cp in plan.forwarded(0):
            cp.wait_recv()
        for cp in passed:
            cp.wait_send()

    return pl.pallas_call(
        body, name=name, in_specs=[HBM] * n_w, out_specs=[HBM] * n_w,
        out_shape=[jax.ShapeDtypeStruct(f.shape, f.dtype) for f in fulls],
        input_output_aliases={i: i for i in range(n_w)},
        scratch_shapes=[pltpu.SemaphoreType.DMA((3 * n_w,)), pltpu.SemaphoreType.DMA((3 * n_w,))],
    )(*fulls)


def _grads_to_sibling(grads, layer, *, name):
    n_w = len(grads)

    def body(*refs):
        src, dst = refs[:n_w], refs[n_w:2 * n_w]
        send_sems, recv_sems = refs[2 * n_w:]
        x, y, c = _my_place()
        cps = [pltpu.make_async_remote_copy(src_ref=src[i], dst_ref=dst[i], send_sem=send_sems.at[i],
                                            recv_sem=recv_sems.at[i], device_id=(x, y, layer), device_id_type=MESH)
               for i in range(n_w)]

        @pl.when(c != layer)
        def _():
            for cp in cps:
                cp.start()
            for cp in cps:
                cp.wait_send()

        @pl.when(c == layer)
        def _():
            for cp in cps:
                cp.wait_recv()

    return pl.pallas_call(
        body, name=name, in_specs=[HBM] * n_w, out_specs=[HBM] * n_w,
        out_shape=[jax.ShapeDtypeStruct(g.shape, g.dtype) for g in grads],
        scratch_shapes=[pltpu.SemaphoreType.DMA((n_w,)), pltpu.SemaphoreType.DMA((n_w,))],
    )(*grads)


def _on_core(layer):
    return (lax.axis_index("c") == layer).astype(jnp.int32).reshape(1)


def _pair_add(mine, other, layer, *, name):
    k, n = mine.shape
    tr = _div_tile(k, 512, BF16_ROWS)

    def body(on_ref, a_ref, b_ref, o_ref):
        @pl.when(on_ref[0] == 1)
        def _():
            o_ref[...] = (a_ref[...].astype(F32) + b_ref[...].astype(F32)).astype(o_ref.dtype)

    blk = pl.BlockSpec((tr, n), lambda i, on: (i * on[0], 0))
    return pl.pallas_call(
        body, name=name,
        grid_spec=pltpu.PrefetchScalarGridSpec(num_scalar_prefetch=1, grid=(k // tr,), in_specs=[blk, blk], out_specs=blk),
        out_shape=jax.ShapeDtypeStruct((k, n), mine.dtype), compiler_params=_params(dimension_semantics=("arbitrary",)),
    )(_on_core(layer), mine, other)


class _ScatterPlan:
    def __init__(self, src, dst, kinds, sizes, layer, send_sems, recv_sems):
        self.src, self.dst, self.kinds, self.sizes, self.layer = src, dst, kinds, sizes, layer
        self.send_sems, self.recv_sems = send_sems, recv_sems
        self.x, self.y, self.c = _my_place()
        self.mine = 2 * self.x + self.y
        self.chips = _other_chips(self.x, self.y)
        self.n = len(src)

    def _copy(self, i, k, chip, window_of, slab):
        return pltpu.make_async_remote_copy(src_ref=_window(self.src[i], self.kinds[i], self.sizes[i], window_of, ()),
                                            dst_ref=self.dst[i].at[slab], send_sem=self.send_sems.at[3 * i + k],
                                            recv_sem=self.recv_sems.at[3 * i + k], device_id=(*chip, self.layer),
                                            device_id_type=MESH)

    def sends(self):
        return [self._copy(i, k, chip, _chip_no(chip), self.mine) for k, chip in enumerate(self.chips) for i in range(self.n)]

    def arrivals(self):
        return [self._copy(i, k, chip, self.mine, _chip_no(chip)) for k, chip in enumerate(self.chips) for i in range(self.n)]


def _slab_shape(p, kind, size):
    return (N_CHIPS,) + {"col": (p.shape[0], size), "row": (size, p.shape[1]), "win_main": (p.shape[0], size),
                         "win_strad": (p.shape[0], LANES)}[kind]


def _grads_to_chips_start(pairs, kinds, sizes, layer, *, name):
    n_w = len(pairs)

    def body(*refs):
        plan = _ScatterPlan(refs[:n_w], refs[n_w:2 * n_w], kinds, sizes, layer, refs[2 * n_w], refs[2 * n_w + 1])

        @pl.when(plan.c == layer)
        def _():
            for cp in plan.sends():
                cp.start()

    slabs = [lax.empty(_slab_shape(p, kind, size), p.dtype) for p, kind, size in zip(pairs, kinds, sizes)]
    operands = [pltpu.with_memory_space_constraint(a, pltpu.HBM) for a in (*pairs, *slabs)]
    res = pl.pallas_call(
        body, name=name, in_specs=[IN_HBM] * (2 * n_w), out_specs=(IN_SEM, IN_SEM, *([IN_HBM] * (2 * n_w))),
        out_shape=(pltpu.SemaphoreType.DMA((3 * n_w,)), pltpu.SemaphoreType.DMA((3 * n_w,)),
                   *[pltpu.HBM(a.shape, a.dtype) for a in operands]),
        input_output_aliases={i: 2 + i for i in range(2 * n_w)},
        compiler_params=pltpu.CompilerParams(has_side_effects=DATAFLOW),
    )(*operands)
    return res[0], res[1], res[2:2 + n_w], res[2 + n_w:]


def _grads_to_chips_wait(send_sems, recv_sems, pairs, slabs, kinds, sizes, layer, after, *, name):
    n_w = len(pairs)

    def body(*refs):
        plan = _ScatterPlan(refs[:n_w], refs[n_w:2 * n_w], kinds, sizes, layer, refs[2 * n_w], refs[2 * n_w + 1])

        @pl.when(plan.c == layer)
        def _():
            for cp in plan.sends():
                cp.wait_send()
            for cp in plan.arrivals():
                cp.wait_recv()

    res = pl.pallas_call(
        body, name=name, in_specs=[IN_HBM] * (2 * n_w) + [IN_SEM, IN_SEM, pl.BlockSpec(memory_space=pl.ANY)],
        out_specs=[IN_HBM] * (2 * n_w), out_shape=[pltpu.HBM(a.shape, a.dtype) for a in (*pairs, *slabs)],
        input_output_aliases={i: i for i in range(2 * n_w)},
        compiler_params=pltpu.CompilerParams(has_side_effects=DATAFLOW),
    )(*pairs, *slabs, send_sems, recv_sems, after)
    return res[:n_w], res[n_w:]


def _sum_slabs(slabs, pair, kind, size, layer, into, *, name):
    n_s, k, n = slabs.shape
    tr = _div_tile(k, 512, BF16_ROWS)
    tc = n if kind in ("col", "row") else LANES
    x, y, _ = _my_place()
    mine = 2 * x + y
    shard = size + HEAD_DIM
    row0 = mine * (k // tr) if kind == "row" else 0
    col0 = {"col": mine, "row": 0, "win_main": (mine * shard + HEAD_DIM * (mine % 2)) // LANES,
            "win_strad": (size + 2 * shard * (mine // 2)) // LANES}[kind]
    on = _on_core(layer)[0]
    scalars = jnp.stack([mine, row0 * on, col0 * on, on]).astype(jnp.int32)

    def body(s_ref, slab_ref, own_ref, *rest):
        o_ref = rest[-1]
        me = s_ref[0]

        @pl.when(s_ref[3] == 1)
        def _():
            acc = jnp.zeros(o_ref.shape, F32)
            for i in range(n_s):
                acc = acc + jnp.where(me == i, own_ref[...], slab_ref[i]).astype(F32)
            o_ref[...] = acc

    operands = [scalars, slabs, pair] + ([] if into is None else [into])
    return pl.pallas_call(
        body, name=name,
        grid_spec=pltpu.PrefetchScalarGridSpec(
            num_scalar_prefetch=1, grid=(k // tr, n // tc),
            in_specs=[pl.BlockSpec((n_s, tr, tc), lambda i, j, s: (0, i * s[3], j * s[3])),
                      pl.BlockSpec((tr, tc), lambda i, j, s: (s[1] + i * s[3], s[2] + j * s[3]))]
            + ([] if into is None else [HBM]),
            out_specs=pl.BlockSpec((None, tr, tc), lambda i, j, s: (layer, i * s[3], j * s[3]))),
        out_shape=jax.ShapeDtypeStruct((2, k, n), F32),
        input_output_aliases={} if into is None else {3: 0},
        compiler_params=_params(dimension_semantics=("arbitrary", "arbitrary")),
    )(*operands)


def _exchange_layers(bufs, *, name):
    n_w = len(bufs)

    def body(*refs):
        dst = refs[n_w:2 * n_w]
        send_sems, recv_sems = refs[2 * n_w:]
        x, y, c = _my_place()

        def copy(i, layer):
            return pltpu.make_async_remote_copy(src_ref=dst[i].at[layer], dst_ref=dst[i].at[layer], send_sem=send_sems.at[i],
                                                recv_sem=recv_sems.at[i], device_id=(x, y, 1 - c), device_id_type=MESH)

        sends = [copy(i, c) for i in range(n_w)]
        for cp in sends:
            cp.start()
        for i in range(n_w):
            copy(i, 1 - c).wait_recv()
        for cp in sends:
            cp.wait_send()

    return pl.pallas_call(
        body, name=name, in_specs=[HBM] * n_w, out_specs=[HBM] * n_w,
        out_shape=[jax.ShapeDtypeStruct(b.shape, b.dtype) for b in bufs],
        input_output_aliases={i: i for i in range(n_w)},
        scratch_shapes=[pltpu.SemaphoreType.DMA((n_w,)), pltpu.SemaphoreType.DMA((n_w,))],
    )(*bufs)


def _all_sum_small(v, *, name):
    r = v.shape[0]
    relations = [(dx, dy, dc) for dx in (0, 1) for dy in (0, 1) for dc in (0, 1)][1:]

    def body(v_ref, o_ref, buf, send_sems, recv_sems):
        x, y, c = _my_place()
        me = 4 * x + 2 * y + c
        buf[me] = v_ref[...]
        peers = [(x + dx - 2 * x * dx, y + dy - 2 * y * dy, c + dc - 2 * c * dc) for dx, dy, dc in relations]

        def copy(k, slot):
            return pltpu.make_async_remote_copy(src_ref=v_ref, dst_ref=buf.at[slot], send_sem=send_sems.at[k],
                                                recv_sem=recv_sems.at[k], device_id=peers[k], device_id_type=MESH)

        sends = [copy(k, me) for k in range(len(relations))]
        for cp in sends:
            cp.start()
        for k, (px, py, pc) in enumerate(peers):
            copy(k, 4 * px + 2 * py + pc).wait_recv()
        for cp in sends:
            cp.wait_send()
        acc = buf[0]
        for i in range(1, 8):
            acc = acc + buf[i]
        o_ref[...] = acc

    vm = pl.BlockSpec(memory_space=pltpu.VMEM)
    return pl.pallas_call(
        body, name=name, in_specs=[vm], out_specs=vm, out_shape=jax.ShapeDtypeStruct((r, LANES), F32),
        scratch_shapes=[pltpu.VMEM((8, r, LANES), F32), pltpu.SemaphoreType.DMA((7,)), pltpu.SemaphoreType.DMA((7,))],
    )(v)


SHARDED = (("ffn1_w_up", "col"), ("ffn1_w_down", "row"), ("w_in", "win"), ("w_branch_a", "col"),
           ("w_branch_b", "col"), ("w_out", "row"), ("ffn2_w_up", "col"), ("ffn2_w_down", "row"))
REPLICATED = ("ffn1_norm", "mix_norm", "na_rel_bias", "ffn2_norm", "final_norm")


def _weight_pieces(w):
    even = lax.axis_index("y") == 0
    shards, kinds, names = [], [], []
    for name, kind in SHARDED:
        wb = w[name].astype(BF16)
        if kind == "win":
            main = wb.shape[-1] - HEAD_DIM
            assert main % LANES == 0
            zeros = jnp.zeros(wb.shape[:-1] + (HEAD_DIM,), BF16)
            shards += [jnp.where(even, wb[..., :main], wb[..., HEAD_DIM:]),
                       jnp.where(even, jnp.concatenate([wb[..., main:], zeros], -1),
                                 jnp.concatenate([zeros, wb[..., :HEAD_DIM]], -1))]
            kinds += ["win_main", "slot"]
            names += [name, name + "_strad"]
        else:
            shards.append(wb)
            kinds.append(kind)
            names.append(name)
    return names, kinds, shards


def _finish_w_in(full):
    full = dict(full)
    strad = full.pop("w_in_strad")
    main = full["w_in"].shape[1] // N_CHIPS - HEAD_DIM
    for i in range(N_CHIPS // 2):
        lo = main + 2 * (main + HEAD_DIM) * i
        full["w_in"] = full["w_in"].at[:, lo:lo + LANES].set(strad[2 * i] + strad[2 * i + 1])
    return full


def _scatter_pieces(shards):
    names, kinds, sizes, srcs = [], [], [], []
    for name, kind in SHARDED:
        shp = shards[name].shape
        if kind == "win":
            names += [name, name + "_strad"]
            kinds += ["win_main", "win_strad"]
            sizes += [shp[2] - HEAD_DIM] * 2
            srcs += [name, name]
        else:
            names.append(name)
            kinds.append(kind)
            sizes.append(shp[1] if kind == "row" else shp[2])
            srcs.append(name)
    return names, kinds, sizes, srcs


def _pair_sums(grads, layer, tag):
    uniq = list(grads)
    arrived = _grads_to_sibling([grads[n] for n in uniq], layer, name=f"{tag}_to_sibling")
    return {n: _pair_add(grads[n], a, layer, name=f"{tag}_pair_{n}") for n, a in zip(uniq, arrived)}


def _finish_weight_grads(reduced, names):
    out = dict(zip(names, _exchange_layers(reduced, name="grads_layers")))
    strad = out.pop("w_in_strad")
    even = lax.axis_index("y") == 0
    out["w_in"] = jnp.where(even, jnp.concatenate([out["w_in"], strad[..., :HEAD_DIM]], -1),
                            jnp.concatenate([strad[..., HEAD_DIM:], out["w_in"]], -1))
    return out


class _Grads:
    def __init__(self):
        self.arrays = {}

    def put(self, weight, layer, a, b, *, cols=None, col_off=0, **kw):
        self.arrays[weight, layer] = _mm(a, b, mode="tn", out_dtype=BF16, out_cols=cols, out_col_off=col_off,
                                         out_into=self.arrays.get((weight, layer)), **kw)


def _ffn_fwd(x, norm_g, w_up, w_down, tag):
    t, d = x.shape
    f = w_down.shape[0]
    h = _rms_fwd(x, norm_g, tt=512, name=f"{tag}_norm")
    a, gate, up = _mm_swiglu_fwd(h, w_up, tm=_div_tile(t, ROWS_NARROW, 8), tn=MXU_N, name=f"{tag}_up")
    x_out = _mm(a, w_down, mode="nn", out_dtype=F32, tm=_div_tile(t, ROWS_WIDE, 8), tn=d, tk=f, alpha=0.5, res=x, name=f"{tag}_down")
    return x_out, (x, h, a, gate, up)


def _ffn_bwd(dx, dxb, saved, norm_g, w_up, w_down, layer, grads, wname, tag, scatter):
    x, h, a, gate, up = saved
    t, d = x.shape
    f = w_down.shape[0]
    tn = _div_tile(f, 1408)
    grads.put(f"{wname}_w_down", layer, a, dxb, tm=tn, tn=d, tk=1024, alpha=0.5, name=f"{tag}_dwd")
    d_gate, d_up = _mm_swiglu_bwd(dxb, w_down, gate, up, alpha=0.5, tm=_div_tile(t, ROWS_NARROW, 8), tn=MXU_N, name=f"{tag}_da")
    grads.put(f"{wname}_w_up", layer, h, d_gate, cols=2 * f, tm=d, tn=tn, tk=1024, name=f"{tag}_dwg")
    grads.put(f"{wname}_w_up", layer, h, d_up, cols=2 * f, col_off=f // tn, tm=d, tn=tn, tk=1024, name=f"{tag}_dwu")
    scatter(layer, [f"{wname}_w_up", f"{wname}_w_down"])
    dh = _mm(d_gate, w_up, mode="nt", out_dtype=F32, tm=_div_tile(t, ROWS_WIDE, 8), tn=d, tk=f, name=f"{tag}_dh1")
    dh = _mm(d_up, w_up, mode="nt", out_dtype=F32, tm=_div_tile(t, ROWS_WIDE, 8), tn=d, tk=f, b_k_off=1, res=dh, name=f"{tag}_dh2")
    return _rms_bwd(dh, x, norm_g, dx, tt=512, name=f"{tag}_dnorm")


def _to_heads(y, b, n_heads):
    t, w = y.shape
    return y.reshape(b, t // b, n_heads, HEAD_DIM).transpose(0, 2, 1, 3)


def _from_heads(y):
    b, n, s, hd = y.shape
    return y.transpose(0, 2, 1, 3).reshape(b * s, n * hd)


N_QKV = 3 * (DIL_HEADS + NA_HEADS) * HEAD_DIM


def _mixer_fwd(x, b, norm_g, full, bias, tabs, tag):
    t, d = x.shape
    s = t // b
    n_in = full["w_in"].shape[1]
    h = _rms_fwd(x, norm_g, tt=512, name=f"{tag}_norm")
    proj = _mm(h, full["w_in"], mode="nn", out_dtype=F32, tm=_div_tile(t, ROWS_NARROW, 8), tn=MXU_N, tk=d, name=f"{tag}_in")
    heads = _split_heads(proj.reshape(b, s, -1), *tabs, n_pairs=N_QKV // LANES, rot_pairs=DIL_HEADS,
                         scale_ranges=((0, DIL_HEADS // 2), (3 * DIL_HEADS // 2, (3 * DIL_HEADS + NA_HEADS) // 2)),
                         name=f"{tag}_heads")
    ya, lse_a = _dil_attn_fwd(heads, name=f"{tag}_dil")
    yb, lse_b = _na_attn_fwd(heads, bias, first=3 * DIL_HEADS, name=f"{tag}_na")
    ya2, yb2 = _from_heads(ya), _from_heads(yb)
    z = _mm(ya2, full["w_branch_a"], mode="nn", out_dtype=F32, tm=_div_tile(t, ROWS_NARROW, 8), tn=MXU_N, tk=ya2.shape[1],
            out_slab=(0, 2), name=f"{tag}_za")
    z = _mm(yb2, full["w_branch_b"], mode="nn", out_dtype=F32, tm=_div_tile(t, ROWS_NARROW, 8), tn=MXU_N, tk=yb2.shape[1],
            out_slab=(1, 2), out_into=z, name=f"{tag}_zb")
    merged = _gate_fwd(proj, z, gate_col=N_QKV, tt=1024, name=f"{tag}_gate")
    x_out = _mm(merged, full["w_out"], mode="nn", out_dtype=F32, tm=_div_tile(t, ROWS_NARROW, 8), tn=MXU_N, tk=d, res=x, name=f"{tag}_out")
    return x_out, (x, h, proj, heads, ya, lse_a, yb, lse_b, ya2, yb2, z, merged)


def _mixer_bwd(dx, dob, b, saved, norm_g, full, layer, bias, tabs, grads, tag, scatter):
    x, h, proj, heads, ya, lse_a, yb, lse_b, ya2, yb2, z, merged = saved
    t, d = x.shape
    s = t // b
    n_in = full["w_in"].shape[1]
    grads.put("w_out", layer, merged, dob, tm=d, tn=d, tk=1024, name=f"{tag}_dwo")
    dm = _mm(dob, full["w_out"], mode="nt", out_dtype=F32, tm=_div_tile(t, ROWS_NARROW, 8), tn=MXU_N, tk=d, name=f"{tag}_dm")
    dz, dproj = _gate_bwd(dm, proj, z, gate_col=N_QKV, tt=1024, name=f"{tag}_dgate")
    grads.put("w_branch_a", layer, ya2, dz, b_sel=0, tm=ya2.shape[1], tn=d, tk=1024, name=f"{tag}_dwa")
    grads.put("w_branch_b", layer, yb2, dz, b_sel=1, tm=yb2.shape[1], tn=d, tk=1024, name=f"{tag}_dwb")
    scatter(layer, ["w_out", "w_branch_a", "w_branch_b"])
    dya = _mm(dz, full["w_branch_a"], mode="nt", out_dtype=F32, tm=_div_tile(t, ROWS_NARROW, 8), tn=MXU_N, tk=d, a_sel=0, name=f"{tag}_dya")
    dyb = _mm(dz, full["w_branch_b"], mode="nt", out_dtype=F32, tm=_div_tile(t, ROWS_NARROW, 8), tn=MXU_N, tk=d, a_sel=1, name=f"{tag}_dyb")
    d_dil = _dil_attn_bwd(heads, ya, lse_a, _to_heads(dya, b, DIL_GROUP_HEADS), name=f"{tag}_ddil")
    d_na, d_bias = _na_attn_bwd(heads, bias, yb, lse_b, _to_heads(dyb, b, NA_HEADS), first=3 * DIL_HEADS, name=f"{tag}_dna")
    dproj = _merge_heads(d_dil, *tabs, heads_per_row=DIL_GROUP_HEADS, rot_pairs=DIL_HEADS, scale_pairs=DIL_HEADS // 2,
                         dilated=True, out_cols=n_in, tile_off=0, into=dproj.reshape(b, s, n_in), name=f"{tag}_dheads_a")
    dproj = _merge_heads(d_na, *tabs, heads_per_row=NA_HEADS, rot_pairs=0, scale_pairs=NA_HEADS // 2, dilated=False,
                         out_cols=n_in, tile_off=3 * DIL_HEADS // 2, into=dproj, name=f"{tag}_dheads_b").reshape(t, n_in)
    grads.put("w_in", layer, h, dproj, tm=_div_tile(d, 512), tn=_div_tile(n_in, 2944), tk=1024, name=f"{tag}_dwin")
    scatter(layer, ["w_in"])
    dh = _mm(dproj, full["w_in"], mode="nt", out_dtype=F32, tm=_div_tile(t, ROWS_WIDE, 8), tn=d, tk=_div_tile(n_in, 2944), name=f"{tag}_dh")
    dx_in, dxb_in, d_norm = _rms_bwd(dh, x, norm_g, dx, tt=512, name=f"{tag}_dnorm")
    d_rb = _na_collapse_bias(d_bias, name=f"{tag}_dbias")
    return dx_in, dxb_in, d_norm, d_rb


def kernel(x, ffn1_norm, ffn1_w_up, ffn1_w_down, mix_norm, w_in, na_rel_bias, w_branch_a, w_branch_b, w_out, ffn2_norm, ffn2_w_up, ffn2_w_down, final_norm, loss_target, m_ffn1_norm, m_ffn1_w_up, m_ffn1_w_down, m_mix_norm, m_w_in, m_na_rel_bias, m_w_branch_a, m_w_branch_b, m_w_out, m_ffn2_norm, m_ffn2_w_up, m_ffn2_w_down, m_final_norm, v_ffn1_norm, v_ffn1_w_up, v_ffn1_w_down, v_mix_norm, v_w_in, v_na_rel_bias, v_w_branch_a, v_w_branch_b, v_w_out, v_ffn2_norm, v_ffn2_w_up, v_ffn2_w_down, v_final_norm):
    w = dict(ffn1_norm=ffn1_norm, ffn1_w_up=ffn1_w_up, ffn1_w_down=ffn1_w_down, mix_norm=mix_norm, w_in=w_in,
             na_rel_bias=na_rel_bias, w_branch_a=w_branch_a, w_branch_b=w_branch_b, w_out=w_out, ffn2_norm=ffn2_norm,
             ffn2_w_up=ffn2_w_up, ffn2_w_down=ffn2_w_down, final_norm=final_norm)
    mom = dict(ffn1_norm=m_ffn1_norm, ffn1_w_up=m_ffn1_w_up, ffn1_w_down=m_ffn1_w_down, mix_norm=m_mix_norm, w_in=m_w_in,
               na_rel_bias=m_na_rel_bias, w_branch_a=m_w_branch_a, w_branch_b=m_w_branch_b, w_out=m_w_out,
               ffn2_norm=m_ffn2_norm, ffn2_w_up=m_ffn2_w_up, ffn2_w_down=m_ffn2_w_down, final_norm=m_final_norm)
    var = dict(ffn1_norm=v_ffn1_norm, ffn1_w_up=v_ffn1_w_up, ffn1_w_down=v_ffn1_w_down, mix_norm=v_mix_norm, w_in=v_w_in,
               na_rel_bias=v_na_rel_bias, w_branch_a=v_w_branch_a, w_branch_b=v_w_branch_b, w_out=v_w_out,
               ffn2_norm=v_ffn2_norm, ffn2_w_up=v_ffn2_w_up, ffn2_w_down=v_ffn2_w_down, final_norm=v_final_norm)
    b, s, d = x.shape
    t = b * s
    depth = ffn1_norm.shape[0]
    assert depth == 2, "core c of a chip sends / reduces layer c"
    shards = {name: w[name] for name, _ in SHARDED}

    names, kinds, pieces = _weight_pieces(w)
    by_layer = [[p[l:l + 1] for p in pieces] for l in range(depth)]
    own = [[_place_own(p, kind, 0, name=f"own{l}_{nm}") for nm, kind, p in zip(names, kinds, by_layer[l])] for l in range(depth)]
    full = [{}, {}]

    def gather_start(layer, group, after, tag):
        idx = [i for i, nm in enumerate(names) if nm in group]
        pick = lambda seq: [seq[i] for i in idx]
        *state, token = _gather_layer_start(pick(by_layer[layer]), pick(kinds), pick(own[layer]), 0, after, name=f"{tag}_start")
        return (layer, idx, tag, state), token[:1, :1]

    def gather_finish(started, after):
        layer, idx, tag, state = started
        pick = lambda seq: [seq[i] for i in idx]
        landed = _gather_layer_wait(*state, pick(kinds), 0, after, name=f"{tag}_wait")
        done = _gather_layer_forward([by_layer[layer][i].shape for i in idx], pick(kinds), landed, name=f"{tag}_forward")
        full[layer].update(zip(pick(names), done))
        return done[0]

    ffn1, mixer, ffn2 = names[:2], names[2:7], names[7:]
    assert mixer[0] == "w_in" and ffn2[0] == "ffn2_w_up", names
    first = _gather_layer(by_layer[0][:2], kinds[:2], own[0][:2], 0, name="gather_l0_ffn1")
    full[0].update(zip(ffn1, first))
    l0_mixer, token_mixer = gather_start(0, mixer, first[0], "gather_l0_mixer")
    tabs = _rope_tables(s)
    bias = _na_expand_bias(na_rel_bias, name="na_bias")

    saved = []
    xc, s1 = _ffn_fwd(x.reshape(t, d), ffn1_norm[:1] + token_mixer, full[0]["ffn1_w_up"], full[0]["ffn1_w_down"], "l0_ffn1")
    landed = gather_finish(l0_mixer, xc)
    full[0] = _finish_w_in(full[0])
    l0_ffn2, token_ffn2 = gather_start(0, ffn2, landed, "gather_l0_ffn2")
    layer1, token_layer1 = gather_start(1, names, landed, "gather_l1")
    xc, s2 = _mixer_fwd(xc, b, mix_norm[:1] + token_ffn2 + token_layer1, full[0], bias[0], tabs, "l0_mix")
    gather_finish(l0_ffn2, xc)
    xc, s3 = _ffn_fwd(xc, ffn2_norm[:1], full[0]["ffn2_w_up"], full[0]["ffn2_w_down"], "l0_ffn2")
    saved.append((s1, s2, s3))
    gather_finish(layer1, xc)
    full[1] = _finish_w_in(full[1])
    for l in range(1, depth):
        xc, s1 = _ffn_fwd(xc, ffn1_norm[l:l + 1], full[l]["ffn1_w_up"], full[l]["ffn1_w_down"], f"l{l}_ffn1")
        xc, s2 = _mixer_fwd(xc, b, mix_norm[l:l + 1], full[l], bias[l], tabs, f"l{l}_mix")
        xc, s3 = _ffn_fwd(xc, ffn2_norm[l:l + 1], full[l]["ffn2_w_up"], full[l]["ffn2_w_down"], f"l{l}_ffn2")
        saved.append((s1, s2, s3))

    dx, dxb, d_final, loss_part = _final_loss(xc, final_norm.reshape(1, d), loss_target.reshape(t, d), tt=512, name="final_loss")
    grads = _Grads()
    piece_names, piece_kinds, piece_sizes, piece_srcs = _scatter_pieces(shards)
    scattered = []

    def scatter(layer, weights):
        tag = f"grads{layer}_{weights[0]}"
        pair = _pair_sums({wn: grads.arrays[wn, layer] for wn in weights}, layer, tag)
        idx = [i for i, src in enumerate(piece_srcs) if src in weights]
        pick = lambda seq: [seq[i] for i in idx]
        scattered.append((layer, idx, _grads_to_chips_start([pair[src] for src in pick(piece_srcs)], pick(piece_kinds),
                                                            pick(piece_sizes), layer, name=f"{tag}_to_chips_start")))
    small = {name: [None] * depth for name in REPLICATED[:-1]}
    for l in reversed(range(depth)):
        s1, s2, s3 = saved[l]
        dx, dxb, small["ffn2_norm"][l] = _ffn_bwd(dx, dxb, s3, ffn2_norm[l:l + 1], full[l]["ffn2_w_up"], full[l]["ffn2_w_down"],
                                                  l, grads, "ffn2", f"l{l}_ffn2", scatter)
        dx, dxb, small["mix_norm"][l], small["na_rel_bias"][l] = _mixer_bwd(
            dx, dxb, b, s2, mix_norm[l:l + 1], full[l], l, bias[l], tabs, grads, f"l{l}_mix", scatter)
        dx, dxb, small["ffn1_norm"][l] = _ffn_bwd(dx, dxb, s1, ffn1_norm[l:l + 1], full[l]["ffn1_w_up"], full[l]["ffn1_w_down"],
                                                  l, grads, "ffn1", f"l{l}_ffn1", scatter)
    grad_x = dx.reshape(b, s, d)
    reduced = [None] * len(piece_names)
    for layer, idx, state in scattered:
        state = _grads_to_chips_wait(*state, [piece_kinds[i] for i in idx], [piece_sizes[i] for i in idx], layer, dx,
                                     name=f"grads{layer}_{piece_names[idx[0]]}_to_chips_wait")
        for i, p, sl in zip(idx, *state):
            reduced[i] = _sum_slabs(sl, p, piece_kinds[i], piece_sizes[i], layer, reduced[i],
                                    name=f"grads{layer}_sum_{piece_names[i]}")
    g_out = _finish_weight_grads(reduced, piece_names)

    parts = [jnp.stack(small[name]).reshape(-1) for name in REPLICATED[:-1]] + [d_final.reshape(-1), loss_part[0, :1]]
    sizes = [v.shape[0] for v in parts]
    flat = jnp.concatenate(parts)
    flat = jnp.pad(flat, (0, -flat.shape[0] % (8 * LANES)))
    small_sum = _all_sum_small(flat.reshape(-1, LANES), name="small_all_sum").reshape(-1)
    off = 0
    for name, n in zip(REPLICATED, sizes[:-1]):
        g_out[name] = small_sum[off:off + n].reshape(w[name].shape)
        off += n
    loss = small_sum[off]

    names = list(w)
    delta, new_m, new_v = {}, {}, {}
    for name in names:
        delta[name], new_m[name], new_v[name] = _adamw(w[name], g_out[name], mom[name], var[name], name=f"adamw_{name}")
    return (loss, grad_x, *[g_out[n] for n in names], *[delta[n] for n in names], *[new_m[n] for n in names],
            *[new_v[n] for n in names])
```

```python
import functools

import numpy as np
import jax
import jax.numpy as jnp
from jax import lax
from jax.experimental import pallas as pl
from jax.experimental.pallas import tpu as pltpu

F32, BF16 = jnp.float32, jnp.bfloat16
MESH = pl.DeviceIdType.MESH

HEAD_DIM = 64
DILATIONS = (1, 4, 16)
DIL_HALF = 64
DIL_GROUP_HEADS = 4
DIL_HEADS = 12
NA_HEADS = 8
GRID_W = 64
NA_ROWS = 8
NA_COLS = 16
ROPE_THETA = 10000.0
RMS_EPS = 1e-6
NEG_INF = -1e30
ADAM_LR, ADAM_B1, ADAM_B2, ADAM_EPS, ADAM_WD, ADAM_STEP = 0.001, 0.9, 0.999, 1e-08, 0.01, 10
QK_SCALE = HEAD_DIM ** -0.5

N_CHIPS = 4
LANES = 128
BF16_ROWS = 16
VMEM_LIMIT = 56 * 1024 * 1024
MXU_N = 256
ROWS_NARROW = 2048
ROWS_WIDE = 512

_NN = (((1,), (0,)), ((), ()))
_NT = (((1,), (1,)), ((), ()))
_TN = (((0,), (0,)), ((), ()))

HBM = pl.BlockSpec(memory_space=pl.ANY)


def _params(**kw):
    return pltpu.CompilerParams(vmem_limit_bytes=VMEM_LIMIT, **kw)


def _dot(a, b, dims):
    return lax.dot_general(a, b, dims, preferred_element_type=F32)


def _div_tile(n, cap, mult=LANES):
    best = None
    for t in range(mult, min(n, cap) + 1, mult):
        if n % t == 0:
            best = t
    return n if best is None else best


def _stacked(block, index, sel):
    if sel is None:
        return pl.BlockSpec(block, index)
    return pl.BlockSpec((None,) + block, lambda *g: (sel,) + index(*g))


def _mm(a, b, *, mode, out_dtype, tm, tn, tk, name, alpha=1.0, res=None, a_sel=None, b_sel=None, b_k_off=0,
        out_slab=None, out_cols=None, out_col_off=0, out_into=None):
    a2, b2 = a.shape[-2:], b.shape[-2:]
    if mode == "nn":
        (m, k), n = a2, b2[1]
        a_spec = _stacked((tm, tk), lambda i, j, kk: (i, kk), a_sel)
        b_spec = _stacked((tk, tn), lambda i, j, kk: (kk + b_k_off, j), b_sel)
        dims = _NN
    elif mode == "nt":
        (m, k), n = a2, b2[0]
        a_spec = _stacked((tm, tk), lambda i, j, kk: (i, kk), a_sel)
        b_spec = _stacked((tn, tk), lambda i, j, kk: (j, kk + b_k_off), b_sel)
        dims = _NT
    else:
        (k, m), n = a2, b2[1]
        a_spec = _stacked((tk, tm), lambda i, j, kk: (kk, i), a_sel)
        b_spec = _stacked((tk, tn), lambda i, j, kk: (kk + b_k_off, j), b_sel)
        dims = _TN
    assert m % tm == 0 and n % tn == 0 and k % tk == 0, (name, a.shape, b.shape)
    nk = k // tk
    has_res = res is not None
    if out_slab is None:
        o_spec = pl.BlockSpec((tm, tn), lambda i, j, kk: (i, j + out_col_off))
        out_shape = jax.ShapeDtypeStruct((m, n if out_cols is None else out_cols), out_dtype)
    else:
        o_spec = _stacked((tm, tn), lambda i, j, kk: (i, j + out_col_off), out_slab[0])
        out_shape = jax.ShapeDtypeStruct((out_slab[1], m, n if out_cols is None else out_cols), out_dtype)
    r_spec = pl.BlockSpec((tm, tn), lambda i, j, kk: (i, j))
    n_in = 2 + has_res + (out_into is not None)

    def body(*refs):
        a_ref, b_ref = refs[0], refs[1]
        r_ref = refs[2] if has_res else None
        o_ref = refs[n_in]
        p = _dot(a_ref[...], b_ref[...], dims)

        def finish(acc):
            y = acc * alpha if alpha != 1.0 else acc
            if has_res:
                y = y + r_ref[...].astype(F32)
            o_ref[...] = y.astype(o_ref.dtype)

        if nk == 1:
            finish(p)
        else:
            acc_ref = refs[n_in + 1]
            kk = pl.program_id(2)

            @pl.when(kk == 0)
            def _():
                acc_ref[...] = p

            @pl.when(kk > 0)
            def _():
                acc_ref[...] += p

            @pl.when(kk == nk - 1)
            def _():
                finish(acc_ref[...])

    operands = [a, b] + ([res] if has_res else [])
    in_specs = [a_spec, b_spec] + ([r_spec] if has_res else [])
    aliases = {}
    if out_into is not None:
        aliases = {len(operands): 0}
        operands.append(out_into)
        in_specs.append(HBM)
    return pl.pallas_call(
        body, name=name, grid=(m // tm, n // tn, nk), in_specs=in_specs, out_specs=o_spec, out_shape=out_shape,
        scratch_shapes=[pltpu.VMEM((tm, tn), F32)] if nk > 1 else [], input_output_aliases=aliases,
        compiler_params=_params(dimension_semantics=("parallel", "parallel", "arbitrary")),
    )(*operands)


def _mm_swiglu_fwd(h, w_up, *, tm, tn, name):
    m, k = h.shape
    n = w_up.shape[1] // 2
    h_spec = pl.BlockSpec((tm, k), lambda i, j: (i, 0))
    wg_spec = pl.BlockSpec((k, tn), lambda i, j: (0, j))
    wu_spec = pl.BlockSpec((k, tn), lambda i, j: (0, j + n // tn))
    o_spec = pl.BlockSpec((tm, tn), lambda i, j: (i, j))

    def body(h_ref, wg_ref, wu_ref, a_ref, g_ref, u_ref):
        hb = h_ref[...]
        g = _dot(hb, wg_ref[...], _NN)
        u = _dot(hb, wu_ref[...], _NN)
        a_ref[...] = (g * jax.nn.sigmoid(g) * u).astype(BF16)
        g_ref[...] = g.astype(BF16)
        u_ref[...] = u.astype(BF16)

    out = jax.ShapeDtypeStruct((m, n), BF16)
    return pl.pallas_call(
        body, name=name, grid=(m // tm, n // tn), in_specs=[h_spec, wg_spec, wu_spec],
        out_specs=[o_spec] * 3, out_shape=[out] * 3,
        compiler_params=_params(dimension_semantics=("parallel", "parallel")),
    )(h, w_up, w_up)


def _mm_swiglu_bwd(dy, w_down, gate, up, *, alpha, tm, tn, name):
    m, k = dy.shape
    n = w_down.shape[0]
    dy_spec = pl.BlockSpec((tm, k), lambda i, j: (i, 0))
    w_spec = pl.BlockSpec((tn, k), lambda i, j: (j, 0))
    o_spec = pl.BlockSpec((tm, tn), lambda i, j: (i, j))

    def body(dy_ref, w_ref, g_ref, u_ref, dg_ref, du_ref):
        da = _dot(dy_ref[...], w_ref[...], _NT) * alpha
        g = g_ref[...].astype(F32)
        u = u_ref[...].astype(F32)
        sg = jax.nn.sigmoid(g)
        dg_ref[...] = (da * u * (sg * (1.0 + g * (1.0 - sg)))).astype(BF16)
        du_ref[...] = (da * (g * sg)).astype(BF16)

    out = jax.ShapeDtypeStruct((m, n), BF16)
    return pl.pallas_call(
        body, name=name, grid=(m // tm, n // tn), in_specs=[dy_spec, w_spec, o_spec, o_spec],
        out_specs=[o_spec] * 2, out_shape=[out] * 2,
        compiler_params=_params(dimension_semantics=("parallel", "parallel")),
    )(dy, w_down, gate, up)


def _rms_fwd(x, g, *, tt, name):
    t, d = x.shape

    def body(x_ref, g_ref, h_ref):
        xv = x_ref[...]
        rstd = lax.rsqrt(jnp.mean(xv * xv, axis=1, keepdims=True) + RMS_EPS)
        h_ref[...] = (xv * rstd * g_ref[...]).astype(BF16)

    return pl.pallas_call(
        body, name=name, grid=(t // tt,),
        in_specs=[pl.BlockSpec((tt, d), lambda i: (i, 0)), pl.BlockSpec((1, d), lambda i: (0, 0))],
        out_specs=pl.BlockSpec((tt, d), lambda i: (i, 0)), out_shape=jax.ShapeDtypeStruct((t, d), BF16),
        compiler_params=_params(dimension_semantics=("parallel",)),
    )(x, g)


def _rms_bwd(dh, x, g, dres, *, tt, name):
    t, d = x.shape

    def body(dh_ref, x_ref, g_ref, r_ref, dx_ref, dxb_ref, dg_ref):
        xv = x_ref[...]
        rstd = lax.rsqrt(jnp.mean(xv * xv, axis=1, keepdims=True) + RMS_EPS)
        xhat = xv * rstd
        dhv = dh_ref[...]
        dxhat = dhv * g_ref[...]
        dx = r_ref[...] + rstd * (dxhat - xhat * jnp.mean(dxhat * xhat, axis=1, keepdims=True))
        dx_ref[...] = dx
        dxb_ref[...] = dx.astype(BF16)

        @pl.when(pl.program_id(0) == 0)
        def _():
            dg_ref[...] = jnp.zeros_like(dg_ref)

        dg_ref[...] += jnp.sum(dhv * xhat, axis=0, keepdims=True)

    row = pl.BlockSpec((tt, d), lambda i: (i, 0))
    vec = pl.BlockSpec((1, d), lambda i: (0, 0))
    return pl.pallas_call(
        body, name=name, grid=(t // tt,), in_specs=[row, row, vec, row], out_specs=[row, row, vec],
        out_shape=[jax.ShapeDtypeStruct((t, d), F32), jax.ShapeDtypeStruct((t, d), BF16), jax.ShapeDtypeStruct((1, d), F32)],
        compiler_params=_params(dimension_semantics=("arbitrary",)),
    )(dh, x, g, dres)


def _final_loss(x, g, target, *, tt, name):
    t, d = x.shape

    def body(x_ref, g_ref, t_ref, dx_ref, dxb_ref, dg_ref, loss_ref):
        xv = x_ref[...]
        gv = g_ref[...]
        rstd = lax.rsqrt(jnp.mean(xv * xv, axis=1, keepdims=True) + RMS_EPS)
        xhat = xv * rstd
        err = xhat * gv - t_ref[...]
        dy = err * (1.0 / d)
        dxhat = dy * gv
        dx = rstd * (dxhat - xhat * jnp.mean(dxhat * xhat, axis=1, keepdims=True))
        dx_ref[...] = dx
        dxb_ref[...] = dx.astype(BF16)

        @pl.when(pl.program_id(0) == 0)
        def _():
            dg_ref[...] = jnp.zeros_like(dg_ref)
            loss_ref[...] = jnp.zeros_like(loss_ref)

        dg_ref[...] += jnp.sum(dy * xhat, axis=0, keepdims=True)
        part = 0.5 * jnp.sum(jnp.mean(err * err, axis=1, keepdims=True), axis=0, keepdims=True)
        loss_ref[...] += jnp.broadcast_to(part, loss_ref.shape)

    row = pl.BlockSpec((tt, d), lambda i: (i, 0))
    vec = pl.BlockSpec((1, d), lambda i: (0, 0))
    one = pl.BlockSpec((1, LANES), lambda i: (0, 0))
    return pl.pallas_call(
        body, name=name, grid=(t // tt,), in_specs=[row, vec, row], out_specs=[row, row, vec, one],
        out_shape=[jax.ShapeDtypeStruct((t, d), F32), jax.ShapeDtypeStruct((t, d), BF16), jax.ShapeDtypeStruct((1, d), F32),
                   jax.ShapeDtypeStruct((1, LANES), F32)],
        compiler_params=_params(dimension_semantics=("arbitrary",)),
    )(x, g, target)


def _swap_halves(x):
    lane = lax.broadcasted_iota(jnp.int32, x.shape, 1)
    return jnp.where((lane // 32) % 2 == 0, pltpu.roll(x, 96, 1), pltpu.roll(x, 32, 1))


def _rope_tables(s):
    half = HEAD_DIM // 2
    inv_freq = ROPE_THETA ** (-jnp.arange(half, dtype=F32) / half)
    ang = jnp.arange(s).astype(F32)[:, None] * inv_freq[None, :]
    cos, sin = jnp.cos(ang), jnp.sin(ang)
    return jnp.tile(cos, (1, 4)), jnp.concatenate([-sin, sin, -sin, sin], axis=1)


def _dilation_of_tile(p):
    dilated = p < 3 * DIL_HEADS // 2
    g = (p % (DIL_HEADS // 2)) // (DIL_GROUP_HEADS // 2)
    return [(dilated & (g == gi)) | (jnp.logical_not(dilated) if gi == 0 else False) for gi in range(len(DILATIONS))]


def _residue_major(ref, d):
    s = ref.shape[0]
    if d == 1:
        return ref[...]
    return jnp.concatenate([ref[pl.ds(r, s // d, stride=d), :] for r in range(d)], axis=0)


def _split_heads(proj, cos4, sin4, *, n_pairs, rot_pairs, scale_ranges, name):
    b, s, _ = proj.shape

    def body(x_ref, c_ref, s_ref, o_ref):
        p = pl.program_id(1)
        is_q = functools.reduce(jnp.logical_or, [(p >= lo) & (p < hi) for lo, hi in scale_ranges])
        scale = jnp.where(is_q, QK_SCALE, 1.0)

        def put(y):
            o_ref[0] = y[:, :HEAD_DIM].astype(BF16)
            o_ref[1] = y[:, HEAD_DIM:].astype(BF16)

        for d, in_group in zip(DILATIONS, _dilation_of_tile(p)):
            @pl.when(in_group & (p < rot_pairs))
            def _(d=d):
                x = _residue_major(x_ref, d)
                put((x * _residue_major(c_ref, d) + _swap_halves(x) * _residue_major(s_ref, d)) * scale)

            @pl.when(in_group & (p >= rot_pairs))
            def _(d=d):
                put(_residue_major(x_ref, d) * scale)

    tab = pl.BlockSpec((s, LANES), lambda bi, p: (0, 0))
    return pl.pallas_call(
        body, name=name, grid=(b, n_pairs),
        in_specs=[pl.BlockSpec((None, s, LANES), lambda bi, p: (bi, 0, p)), tab, tab],
        out_specs=pl.BlockSpec((None, 2, s, HEAD_DIM), lambda bi, p: (bi, p, 0, 0)),
        out_shape=jax.ShapeDtypeStruct((b, 2 * n_pairs, s, HEAD_DIM), BF16),
        compiler_params=_params(dimension_semantics=("parallel", "parallel")),
    )(proj, cos4, sin4)


def _merge_heads(dheads, cos4, sin4, *, heads_per_row, rot_pairs, scale_pairs, dilated, out_cols, tile_off, into, name):
    b, hpr, r, s, _ = dheads.shape
    n_pairs = hpr * r // 2
    ppr = hpr // 2

    def body(d_ref, c_ref, s_ref, *rest):
        o_ref, t_ref = rest[-2:]
        p = pl.program_id(1)
        scale = jnp.where(p < scale_pairs, QK_SCALE, 1.0)

        def tokens(d):
            dy = jnp.concatenate([d_ref[0], d_ref[1]], axis=1)
            if d == 1:
                return dy
            for res in range(d):
                t_ref[pl.ds(res, s // d, stride=d), :] = dy[res * (s // d):(res + 1) * (s // d), :]
            return t_ref[...]

        groups = _dilation_of_tile(p) if dilated else [p >= 0]
        for d, in_group in zip(DILATIONS, groups):
            @pl.when(in_group & (p < rot_pairs))
            def _(d=d):
                dy = tokens(d)
                o_ref[...] = ((dy * c_ref[...] - _swap_halves(dy) * s_ref[...]) * scale).astype(BF16)

            @pl.when(in_group & (p >= rot_pairs))
            def _(d=d):
                o_ref[...] = (tokens(d) * scale).astype(BF16)

    tab = pl.BlockSpec((s, LANES), lambda bi, p: (0, 0))
    operands = [dheads, cos4, sin4] + ([] if into is None else [into])
    return pl.pallas_call(
        body, name=name, grid=(b, n_pairs),
        in_specs=[pl.BlockSpec((None, 2, None, s, HEAD_DIM), lambda bi, p: (bi, p % ppr, p // ppr, 0, 0)), tab, tab]
        + ([] if into is None else [HBM]),
        out_specs=pl.BlockSpec((None, s, LANES), lambda bi, p: (bi, 0, p + tile_off)),
        out_shape=jax.ShapeDtypeStruct((b, s, out_cols), BF16),
        input_output_aliases={} if into is None else {3: 0},
        scratch_shapes=[pltpu.VMEM((s, LANES), F32)],
        compiler_params=_params(dimension_semantics=("parallel", "parallel")),
    )(*operands)


DIL_TQ = 256


def _dil_block(g, s):
    run = s // DILATIONS[g]
    return DIL_TQ if run <= DIL_TQ else min(run, DIL_TQ + 2 * LANES)


def _dil_keys(g, q0, s):
    run = max(s // DILATIONS[g], DIL_TQ)
    lo = (q0 // run) * run
    return pl.multiple_of(jnp.clip(q0 - LANES, lo, lo + run - _dil_block(g, s)), LANES)


def _dil_band(g, q0, start, shape, s):
    row = q0 + lax.broadcasted_iota(jnp.int32, shape, 0)
    col = start + lax.broadcasted_iota(jnp.int32, shape, 1)
    ok = jnp.abs(row - col) <= DIL_HALF
    run = s // DILATIONS[g]
    if run < DIL_TQ:
        shift = run.bit_length() - 1
        ok = ok & ((row >> shift) == (col >> shift))
    return ok


def _dil_tokens(g, q0, s):
    d = DILATIONS[g]
    if d == 1:
        return [(0, DIL_TQ, pl.ds(q0, DIL_TQ))]
    run = s // d
    n = min(run, DIL_TQ)
    return [(lo, n, pl.ds(((q0 + lo) % run) * d + (q0 + lo) // run, n, stride=d)) for lo in range(0, DIL_TQ, n)]


def _dil_gather(ref, pieces):
    return jnp.concatenate([ref[rows, :] for _, _, rows in pieces], axis=0) if len(pieces) > 1 else ref[pieces[0][2], :]


def _dil_head_spec(part, g, s):
    return pl.BlockSpec((None, None, s, HEAD_DIM), lambda b, j: (b, part * DIL_HEADS + g * DIL_GROUP_HEADS + j, 0, 0))


def _dil_attn_fwd(heads, *, name):
    b, _, s, _ = heads.shape
    n_g = len(DILATIONS)

    def body(*refs):
        qkv = refs[:3 * n_g]
        o_ref, l_ref, og_ref, lg_ref = refs[3 * n_g:]
        for g in range(n_g):
            q_ref, k_ref, v_ref = qkv[3 * g:3 * g + 3]
            width = _dil_block(g, s)

            def step(i, carry, g=g, q_ref=q_ref, k_ref=k_ref, v_ref=v_ref, width=width):
                q0 = pl.multiple_of(i * DIL_TQ, DIL_TQ)
                start = _dil_keys(g, q0, s)
                sc = _dot(q_ref[pl.ds(q0, DIL_TQ), :], k_ref[pl.ds(start, width), :], _NT)
                sc = jnp.where(_dil_band(g, q0, start, sc.shape, s), sc, NEG_INF)
                m = jnp.max(sc, axis=1, keepdims=True)
                p = jnp.exp(sc - m)
                den = jnp.sum(p, axis=1, keepdims=True)
                o = _dot(p.astype(BF16), v_ref[pl.ds(start, width), :], _NN) / den
                lse = m + jnp.log(den)
                for lo, n, rows in _dil_tokens(g, q0, s):
                    og_ref[g, rows, :] = o[lo:lo + n]
                    lg_ref[g, rows, :] = lse[lo:lo + n]
                return carry

            lax.fori_loop(0, s // DIL_TQ, step, 0)
        lses = [lg_ref[g] for g in range(n_g)]
        m = functools.reduce(jnp.maximum, lses)
        ws = [jnp.exp(l - m) for l in lses]
        den = functools.reduce(jnp.add, ws)
        o_ref[...] = (functools.reduce(jnp.add, [w * og_ref[g] for g, w in enumerate(ws)]) / den).astype(o_ref.dtype)
        l_ref[...] = m + jnp.log(den)

    out = pl.BlockSpec((None, None, s, HEAD_DIM), lambda bi, j: (bi, j, 0, 0))
    lse = pl.BlockSpec((None, None, s, 1), lambda bi, j: (bi, j, 0, 0))
    return pl.pallas_call(
        body, name=name, grid=(b, DIL_GROUP_HEADS),
        in_specs=[_dil_head_spec(part, g, s) for g in range(n_g) for part in range(3)],
        out_specs=[out, lse],
        out_shape=[jax.ShapeDtypeStruct((b, DIL_GROUP_HEADS, s, HEAD_DIM), BF16),
                   jax.ShapeDtypeStruct((b, DIL_GROUP_HEADS, s, 1), F32)],
        scratch_shapes=[pltpu.VMEM((n_g, s, HEAD_DIM), F32), pltpu.VMEM((n_g, s, 1), F32)],
        compiler_params=_params(dimension_semantics=("parallel", "parallel")),
    )(*([heads] * (3 * n_g)))


def _dil_attn_bwd(heads, out, lse, dout, *, name):
    b, _, s, _ = heads.shape
    n_g = len(DILATIONS)

    def body(*refs):
        qkv = refs[:3 * n_g]
        o_ref, l_ref, do_ref, d_ref, delta_ref = refs[3 * n_g:]
        d_ref[...] = jnp.zeros_like(d_ref)
        delta_ref[...] = jnp.sum(do_ref[...] * o_ref[...].astype(F32), axis=1, keepdims=True)
        for g in range(n_g):
            q_ref, k_ref, v_ref = qkv[3 * g:3 * g + 3]
            width = _dil_block(g, s)

            def step(i, carry, g=g, q_ref=q_ref, k_ref=k_ref, v_ref=v_ref, width=width):
                q0 = pl.multiple_of(i * DIL_TQ, DIL_TQ)
                start = _dil_keys(g, q0, s)
                win = pl.ds(start, width)
                pieces = _dil_tokens(g, q0, s)
                do_b = _dil_gather(do_ref, pieces).astype(BF16)
                q, k, v = q_ref[pl.ds(q0, DIL_TQ), :], k_ref[win, :], v_ref[win, :]
                sc = _dot(q, k, _NT)
                p = jnp.where(_dil_band(g, q0, start, sc.shape, s), jnp.exp(sc - _dil_gather(l_ref, pieces)), 0.0)
                ds = (p * (_dot(do_b, v, _NT) - _dil_gather(delta_ref, pieces))).astype(BF16)
                d_ref[g, pl.ds(q0, DIL_TQ), :] = _dot(ds, k, _NN)
                d_ref[n_g + g, win, :] += _dot(ds, q, _TN)
                d_ref[2 * n_g + g, win, :] += _dot(p.astype(BF16), do_b, _TN)
                return carry

            lax.fori_loop(0, s // DIL_TQ, step, 0)

    per_head = lambda bi, j: (bi, j, 0, 0)
    return pl.pallas_call(
        body, name=name, grid=(b, DIL_GROUP_HEADS),
        in_specs=[_dil_head_spec(part, g, s) for g in range(n_g) for part in range(3)]
        + [pl.BlockSpec((None, None, s, HEAD_DIM), per_head), pl.BlockSpec((None, None, s, 1), per_head),
           pl.BlockSpec((None, None, s, HEAD_DIM), per_head)],
        out_specs=pl.BlockSpec((None, None, 3 * n_g, s, HEAD_DIM), lambda bi, j: (bi, j, 0, 0, 0)),
        out_shape=jax.ShapeDtypeStruct((b, DIL_GROUP_HEADS, 3 * n_g, s, HEAD_DIM), F32),
        scratch_shapes=[pltpu.VMEM((s, 1), F32)],
        compiler_params=_params(dimension_semantics=("parallel", "parallel")),
    )(*([heads] * (3 * n_g)), out, lse, dout)


NA_BIAS_ROWS = 2 * NA_ROWS - 1
NA_BIAS_COLS = 2 * NA_COLS - 1
NA_BLOCK = 4
NA_SPAN = NA_ROWS + NA_BLOCK - 1
NA_Q = NA_BLOCK * GRID_W
NA_KEYS = NA_SPAN * GRID_W
NA_FORMS = 3


def _na_onehot():
    c = np.arange(GRID_W)[:, None]
    k = np.arange(GRID_W)[None, :]
    lo = np.clip(c - NA_COLS // 2, 0, GRID_W - NA_COLS)
    valid = (k >= lo) & (k < lo + NA_COLS)
    onehot = np.zeros((GRID_W, GRID_W, LANES), np.float32)
    cc, kk = np.nonzero(valid)
    onehot[cc, kk, kk - cc + NA_COLS - 1] = 1.0
    return onehot.reshape(GRID_W * GRID_W, LANES), valid.reshape(1, GRID_W * GRID_W)


def _na_block_rows(n_rows):
    table = np.full((NA_FORMS, NA_BLOCK, NA_SPAN), NA_BIAS_ROWS, np.int64)
    n_blocks = n_rows // NA_BLOCK
    for form, ib in enumerate((0, 1, n_blocks - 1)):
        base = min(max(NA_BLOCK * ib - NA_ROWS // 2, 0), n_rows - NA_SPAN)
        for rl in range(NA_BLOCK):
            r = NA_BLOCK * ib + rl
            row_lo = min(max(r - NA_ROWS // 2, 0), n_rows - NA_ROWS)
            for kl in range(NA_SPAN):
                if row_lo <= base + kl < row_lo + NA_ROWS:
                    table[form, rl, kl] = base + kl - r + NA_ROWS - 1
    return table


def _na_block(ib, n_rows):
    n_blocks = n_rows // NA_BLOCK
    base = jnp.clip(NA_BLOCK * ib - NA_ROWS // 2, 0, n_rows - NA_SPAN)
    return base, jnp.where(ib == 0, 0, jnp.where(ib == n_blocks - 1, 2, 1))


def _na_expand_bias(rel_bias, *, name):
    l, h, nr, nc = rel_bias.shape
    onehot, valid = _na_onehot()
    rb = jnp.pad(rel_bias, ((0, 0), (0, 0), (0, 1), (0, LANES - nc))).reshape(l * h * (nr + 1), LANES)
    live = jnp.asarray(np.tile(np.arange(nr + 1) < nr, l * h).astype(np.float32)[:, None])

    def body(rb_ref, oh_ref, valid_ref, live_ref, e_ref):
        e = lax.dot_general(rb_ref[...], oh_ref[...], _NT, precision=lax.Precision.HIGHEST, preferred_element_type=F32)
        e_ref[...] = jnp.where((valid_ref[...] > 0) & (live_ref[...] > 0), e, NEG_INF)

    e = pl.pallas_call(
        body, name=name, out_shape=jax.ShapeDtypeStruct((l * h * (nr + 1), GRID_W * GRID_W), F32), compiler_params=_params(),
    )(rb, jnp.asarray(onehot), jnp.asarray(valid.astype(np.float32)), live)
    return e.reshape(l, h, nr + 1, GRID_W, GRID_W)


def _na_collapse_bias(de, *, name):
    b, h = de.shape[:2]
    onehot, _ = _na_onehot()
    rows = h * NA_BIAS_ROWS

    def diag(e_ref, oh_ref, o_ref):
        e = e_ref[0]
        for bi in range(1, b):
            e = e + e_ref[bi]
        o_ref[...] = lax.dot_general(e, oh_ref[...], _NN, precision=lax.Precision.HIGHEST, preferred_element_type=F32)

    drb = pl.pallas_call(
        diag, name=name, out_shape=jax.ShapeDtypeStruct((rows, LANES), F32), compiler_params=_params(),
    )(de.reshape(b, rows, GRID_W * GRID_W), jnp.asarray(onehot))
    return drb[:, :NA_BIAS_COLS].reshape(h, NA_BIAS_ROWS, NA_BIAS_COLS)


def _na_tiles(n_rows):
    table = _na_block_rows(n_rows)
    return [(f, rl, kl, int(table[f, rl, kl])) for f in range(NA_FORMS) for rl in range(NA_BLOCK) for kl in range(NA_SPAN)]


def _na_tile(ref, form, rl, kl):
    return ref.at[form, rl * GRID_W:(rl + 1) * GRID_W, kl * GRID_W:(kl + 1) * GRID_W]


def _na_head_spec(part, first, s):
    return pl.BlockSpec((None, None, s, HEAD_DIM), lambda b, h: (b, first + part * NA_HEADS + h, 0, 0))


def _na_attn_fwd(heads, bias, *, first, name):
    b, _, s, _ = heads.shape
    n_rows = s // GRID_W
    tiles = _na_tiles(n_rows)

    def body(q_ref, k_ref, v_ref, e_ref, o_ref, l_ref, b_ref):
        for form, rl, kl, i in tiles:
            _na_tile(b_ref, form, rl, kl)[...] = e_ref[i]

        def step(ib, carry):
            base, form = _na_block(ib, n_rows)
            rows = pl.ds(pl.multiple_of(ib * NA_Q, NA_Q), NA_Q)
            win = pl.ds(pl.multiple_of(base * GRID_W, GRID_W), NA_KEYS)
            sc = _dot(q_ref[rows, :], k_ref[win, :], _NT) + b_ref[form]
            m = jnp.max(sc, axis=1, keepdims=True)
            p = jnp.exp(sc - m)
            den = jnp.sum(p, axis=1, keepdims=True)
            o_ref[rows, :] = (_dot(p.astype(BF16), v_ref[win, :], _NN) / den).astype(o_ref.dtype)
            l_ref[rows, :] = m + jnp.log(den)
            return carry

        lax.fori_loop(0, n_rows // NA_BLOCK, step, 0)

    per_head = lambda bi, h: (bi, h, 0, 0)
    return pl.pallas_call(
        body, name=name, grid=(b, NA_HEADS),
        in_specs=[_na_head_spec(part, first, s) for part in range(3)]
        + [pl.BlockSpec((None, NA_BIAS_ROWS + 1, GRID_W, GRID_W), lambda bi, h: (h, 0, 0, 0))],
        out_specs=[pl.BlockSpec((None, None, s, HEAD_DIM), per_head), pl.BlockSpec((None, None, s, 1), per_head)],
        out_shape=[jax.ShapeDtypeStruct((b, NA_HEADS, s, HEAD_DIM), BF16), jax.ShapeDtypeStruct((b, NA_HEADS, s, 1), F32)],
        scratch_shapes=[pltpu.VMEM((NA_FORMS, NA_Q, NA_KEYS), F32)],
        compiler_params=_params(dimension_semantics=("parallel", "parallel")),
    )(heads, heads, heads, bias)


def _na_attn_bwd(heads, bias, out, lse, dout, *, first, name):
    b, _, s, _ = heads.shape
    n_rows = s // GRID_W
    tiles = _na_tiles(n_rows)

    def body(q_ref, k_ref, v_ref, e_ref, o_ref, l_ref, do_ref, d_ref, de_ref, b_ref, db_ref):
        for form, rl, kl, i in tiles:
            _na_tile(b_ref, form, rl, kl)[...] = e_ref[i]
        d_ref[...] = jnp.zeros_like(d_ref)
        db_ref[...] = jnp.zeros_like(db_ref)

        def step(ib, carry):
            base, form = _na_block(ib, n_rows)
            rows = pl.ds(pl.multiple_of(ib * NA_Q, NA_Q), NA_Q)
            win = pl.ds(pl.multiple_of(base * GRID_W, GRID_W), NA_KEYS)
            q, k, v = q_ref[rows, :], k_ref[win, :], v_ref[win, :]
            do = do_ref[rows, :]
            delta = jnp.sum(do * o_ref[rows, :].astype(F32), axis=1, keepdims=True)
            do_b = do.astype(BF16)
            p = jnp.exp(_dot(q, k, _NT) + b_ref[form] - l_ref[rows, :])
            ds = p * (_dot(do_b, v, _NT) - delta)
            db_ref[form] += ds
            ds_b = ds.astype(BF16)
            d_ref[0, rows, :] = _dot(ds_b, k, _NN)
            d_ref[1, win, :] += _dot(ds_b, q, _TN)
            d_ref[2, win, :] += _dot(p.astype(BF16), do_b, _TN)
            return carry

        lax.fori_loop(0, n_rows // NA_BLOCK, step, 0)
        acc = [None] * NA_BIAS_ROWS
        for form, rl, kl, i in tiles:
            if i < NA_BIAS_ROWS:
                t = _na_tile(db_ref, form, rl, kl)[...]
                acc[i] = t if acc[i] is None else acc[i] + t
        for i in range(NA_BIAS_ROWS):
            de_ref[i] = acc[i]

    per_head = lambda bi, h: (bi, h, 0, 0)
    return pl.pallas_call(
        body, name=name, grid=(b, NA_HEADS),
        in_specs=[_na_head_spec(part, first, s) for part in range(3)]
        + [pl.BlockSpec((None, NA_BIAS_ROWS + 1, GRID_W, GRID_W), lambda bi, h: (h, 0, 0, 0)),
           pl.BlockSpec((None, None, s, HEAD_DIM), per_head), pl.BlockSpec((None, None, s, 1), per_head),
           pl.BlockSpec((None, None, s, HEAD_DIM), per_head)],
        out_specs=[pl.BlockSpec((None, None, 3, s, HEAD_DIM), lambda bi, h: (bi, h, 0, 0, 0)),
                   pl.BlockSpec((None, None, NA_BIAS_ROWS, GRID_W, GRID_W), lambda bi, h: (bi, h, 0, 0, 0))],
        out_shape=[jax.ShapeDtypeStruct((b, NA_HEADS, 3, s, HEAD_DIM), F32),
                   jax.ShapeDtypeStruct((b, NA_HEADS, NA_BIAS_ROWS, GRID_W, GRID_W), F32)],
        scratch_shapes=[pltpu.VMEM((NA_FORMS, NA_Q, NA_KEYS), F32), pltpu.VMEM((NA_FORMS, NA_Q, NA_KEYS), F32)],
        compiler_params=_params(dimension_semantics=("parallel", "parallel")),
    )(heads, heads, heads, bias, out, lse, dout)


GATE_TILE = 256


def _gate_fwd(proj, z, *, gate_col, tt, name):
    _, t, d = z.shape
    nj = d // GATE_TILE
    c0 = gate_col // GATE_TILE

    def body(ga_ref, gb_ref, za_ref, zb_ref, o_ref):
        o_ref[...] = (jax.nn.sigmoid(ga_ref[...]) * za_ref[...] + jax.nn.sigmoid(gb_ref[...]) * zb_ref[...]).astype(BF16)

    return pl.pallas_call(
        body, name=name, grid=(t // tt, nj),
        in_specs=[pl.BlockSpec((tt, GATE_TILE), lambda i, j: (i, c0 + j)),
                  pl.BlockSpec((tt, GATE_TILE), lambda i, j: (i, c0 + nj + j)),
                  pl.BlockSpec((None, tt, GATE_TILE), lambda i, j: (0, i, j)),
                  pl.BlockSpec((None, tt, GATE_TILE), lambda i, j: (1, i, j))],
        out_specs=pl.BlockSpec((tt, GATE_TILE), lambda i, j: (i, j)), out_shape=jax.ShapeDtypeStruct((t, d), BF16),
        compiler_params=_params(dimension_semantics=("parallel", "parallel")),
    )(proj, proj, z, z)


def _gate_bwd(dm, proj, z, *, gate_col, tt, name):
    _, t, d = z.shape
    nj = d // GATE_TILE
    c0 = gate_col // GATE_TILE

    def body(dm_ref, g_ref, z_ref, dz_ref, dg_ref):
        dmv = dm_ref[...]
        sg = jax.nn.sigmoid(g_ref[...])
        dz_ref[...] = (dmv * sg).astype(BF16)
        dg_ref[...] = (dmv * z_ref[...] * sg * (1.0 - sg)).astype(BF16)

    return pl.pallas_call(
        body, name=name, grid=(t // tt, 2 * nj),
        in_specs=[pl.BlockSpec((tt, GATE_TILE), lambda i, j: (i, j % nj)),
                  pl.BlockSpec((tt, GATE_TILE), lambda i, j: (i, c0 + j)),
                  pl.BlockSpec((None, tt, GATE_TILE), lambda i, j: (j // nj, i, j % nj))],
        out_specs=[pl.BlockSpec((None, tt, GATE_TILE), lambda i, j: (j // nj, i, j % nj)),
                   pl.BlockSpec((tt, GATE_TILE), lambda i, j: (i, c0 + j))],
        out_shape=[jax.ShapeDtypeStruct((2, t, d), BF16), jax.ShapeDtypeStruct(proj.shape, BF16)],
        compiler_params=_params(dimension_semantics=("parallel", "parallel")),
    )(dm, proj, z)


def _adamw(w, g, m, v, *, name):
    shape = w.shape
    w2, g2, m2, v2 = (t.reshape(-1, shape[-1]) for t in (w, g, m, v))
    rows, cols = w2.shape
    tr = rows
    for cand in (512, 256, 128, 64, 32, 16, 8):
        if rows % cand == 0:
            tr = cand
            break

    def body(w_ref, g_ref, m_ref, v_ref, d_ref, nm_ref, nv_ref):
        gv = g_ref[...]
        nm = ADAM_B1 * m_ref[...] + (1.0 - ADAM_B1) * gv
        nv = ADAM_B2 * v_ref[...] + (1.0 - ADAM_B2) * (gv * gv)
        m_hat = nm / (1.0 - ADAM_B1 ** ADAM_STEP)
        v_hat = nv / (1.0 - ADAM_B2 ** ADAM_STEP)
        d_ref[...] = -ADAM_LR * (m_hat / (jnp.sqrt(v_hat) + ADAM_EPS) + ADAM_WD * w_ref[...])
        nm_ref[...] = nm
        nv_ref[...] = nv

    blk = pl.BlockSpec((tr, cols), lambda i: (i, 0))
    out = jax.ShapeDtypeStruct((rows, cols), F32)
    res = pl.pallas_call(
        body, name=name, grid=(rows // tr,), in_specs=[blk] * 4, out_specs=[blk] * 3, out_shape=[out] * 3,
        compiler_params=_params(dimension_semantics=("parallel",)),
    )(w2, g2, m2, v2)
    return tuple(t.reshape(shape) for t in res)


def _my_place():
    return lax.axis_index("x"), lax.axis_index("y"), lax.axis_index("c")


def _other_chips(x, y):
    return [(1 - x, y), (x, 1 - y), (1 - x, 1 - y)]


def _chip_no(chip):
    return 2 * chip[0] + chip[1]


def _window(ref, kind, size, chip, lead):
    if kind == "col":
        return ref.at[(*lead, slice(None), pl.ds(pl.multiple_of(chip * size, LANES), size))]
    if kind == "row":
        return ref.at[(*lead, pl.ds(pl.multiple_of(chip * size, BF16_ROWS), size), slice(None))]
    shard = size + HEAD_DIM
    if kind == "win_main":
        return ref.at[(*lead, slice(None), pl.ds(pl.multiple_of(chip * shard + HEAD_DIM * (chip % 2), LANES), size))]
    assert kind == "win_strad"
    return ref.at[(*lead, slice(None), pl.ds(pl.multiple_of(size + 2 * shard * (chip // 2), LANES), LANES))]


def _full_shape(shard, kind):
    _, k, n = shard.shape
    return {"col": (k, N_CHIPS * n), "row": (N_CHIPS * k, n), "win_main": (k, N_CHIPS * (n + HEAD_DIM)),
            "slot": (N_CHIPS, k, n)}[kind]


def _place_own(shard, kind, layer, *, name):
    _, k, n = shard.shape
    tr = _div_tile(k, 512, BF16_ROWS)
    tc = LANES if kind == "win_main" else n
    mine = 2 * lax.axis_index("x") + lax.axis_index("y")
    row0 = mine * (k // tr) if kind == "row" else 0
    col0 = {"col": mine, "row": 0, "slot": 0, "win_main": (mine * (n + HEAD_DIM) + HEAD_DIM * (mine % 2)) // LANES}[kind]
    scalars = jnp.stack([mine, row0, col0]).astype(jnp.int32)

    def body(s_ref, i_ref, o_ref):
        o_ref[...] = i_ref[...]

    if kind == "slot":
        o_spec = pl.BlockSpec((None, tr, tc), lambda i, j, s: (s[0], i, j))
    else:
        o_spec = pl.BlockSpec((tr, tc), lambda i, j, s: (s[1] + i, s[2] + j))
    return pl.pallas_call(
        body, name=name,
        grid_spec=pltpu.PrefetchScalarGridSpec(
            num_scalar_prefetch=1, grid=(k // tr, n // tc),
            in_specs=[pl.BlockSpec((None, tr, tc), lambda i, j, s: (layer, i, j))], out_specs=o_spec),
        out_shape=jax.ShapeDtypeStruct(_full_shape(shard, kind), shard.dtype),
        compiler_params=_params(dimension_semantics=("parallel", "parallel")),
    )(scalars, shard)


class _GatherPlan:
    def __init__(self, src, dst, shapes, kinds, layer, send_sems, recv_sems):
        self.src, self.dst, self.shapes, self.kinds, self.layer = src, dst, shapes, kinds, layer
        self.send_sems, self.recv_sems = send_sems, recv_sems
        self.x, self.y, self.c = _my_place()
        self.mine = 2 * self.x + self.y
        self.chips = _other_chips(self.x, self.y)
        self.n = len(src)

    def half(self, i, chip, half):
        _, k, n = self.shapes[i]
        kind, dst, hk = self.kinds[i], self.dst[i], k // 2
        if kind == "slot":
            return dst.at[chip, pl.ds(pl.multiple_of(half * hk, BF16_ROWS), hk), :]
        if kind == "row":
            return dst.at[pl.ds(pl.multiple_of(chip * k + half * hk, BF16_ROWS), hk), :]
        col0 = chip * n if kind == "col" else chip * (n + HEAD_DIM) + HEAD_DIM * (chip % 2)
        return dst.at[pl.ds(pl.multiple_of(half * hk, BF16_ROWS), hk), pl.ds(pl.multiple_of(col0, LANES), n)]

    def _copy(self, sem, window, to, source=None):
        return pltpu.make_async_remote_copy(src_ref=window if source is None else source, dst_ref=window,
                                            send_sem=self.send_sems.at[sem], recv_sem=self.recv_sems.at[sem],
                                            device_id=to, device_id_type=MESH)

    def sends(self):
        out = []
        for k, chip in enumerate(self.chips):
            for i in range(self.n):
                hk = self.shapes[i][1] // 2
                mine = self.src[i].at[self.layer, pl.ds(pl.multiple_of(self.c * hk, BF16_ROWS), hk), :]
                out.append(self._copy(3 * i + k, self.half(i, self.mine, self.c), (*chip, self.c), source=mine))
        return out

    def arrivals(self):
        return [self._copy(3 * i + k, self.half(i, _chip_no(chip), self.c), (*chip, self.c))
                for k, chip in enumerate(self.chips) for i in range(self.n)]

    def forwards(self, first_sem):
        sibling = (self.x, self.y, 1 - self.c)
        return [self._copy(first_sem + 3 * i + k, self.half(i, _chip_no(chip), self.c), sibling)
                for k, chip in enumerate(self.chips) for i in range(self.n)]

    def forwarded(self, first_sem):
        sibling = (self.x, self.y, 1 - self.c)
        return [self._copy(first_sem + 3 * i + k, self.half(i, _chip_no(chip), 1 - self.c), sibling)
                for k, chip in enumerate(self.chips) for i in range(self.n)]


IN_HBM = pl.BlockSpec(memory_space=pltpu.HBM)
IN_SEM = pl.BlockSpec(memory_space=pltpu.SEMAPHORE)
DATAFLOW = pltpu.SideEffectType.DATAFLOW_SIDE_EFFECTING


def _gather_layer_start(shards, kinds, fulls, layer, after, *, name):
    n_w = len(shards)
    shapes = [sh.shape for sh in shards]

    def body(*refs):
        plan = _GatherPlan(refs[:n_w], refs[n_w:2 * n_w], shapes, kinds, layer, refs[2 * n_w + 1], refs[2 * n_w + 2])
        for cp in plan.sends():
            cp.start()
        token = refs[-1]
        token[...] = jnp.zeros_like(token)

    operands = [pltpu.with_memory_space_constraint(a, pltpu.HBM) for a in (*shards, *fulls)]
    res = pl.pallas_call(
        body, name=name, in_specs=[IN_HBM] * (2 * n_w) + [pl.BlockSpec(memory_space=pl.ANY)],
        out_specs=(IN_SEM, IN_SEM, *([IN_HBM] * (2 * n_w)), pl.BlockSpec(memory_space=pltpu.VMEM)),
        out_shape=(pltpu.SemaphoreType.DMA((3 * n_w,)), pltpu.SemaphoreType.DMA((3 * n_w,)),
                   *[pltpu.HBM(a.shape, a.dtype) for a in operands], jax.ShapeDtypeStruct((8, LANES), F32)),
        input_output_aliases={i: 2 + i for i in range(2 * n_w)},
        compiler_params=pltpu.CompilerParams(has_side_effects=DATAFLOW),
    )(*operands, after)
    return res[0], res[1], res[2:2 + n_w], res[2 + n_w:2 + 2 * n_w], res[-1]


def _gather_layer_wait(send_sems, recv_sems, shards, fulls, kinds, layer, after, *, name):
    n_w = len(shards)
    shapes = [sh.shape for sh in shards]

    def body(*refs):
        plan = _GatherPlan(refs[:n_w], refs[n_w:2 * n_w], shapes, kinds, layer, refs[2 * n_w], refs[2 * n_w + 1])
        for cp in plan.sends():
            cp.wait_send()
        for cp in plan.arrivals():
            cp.wait_recv()

    res = pl.pallas_call(
        body, name=name, in_specs=[IN_HBM] * (2 * n_w) + [IN_SEM, IN_SEM, pl.BlockSpec(memory_space=pl.ANY)],
        out_specs=[IN_HBM] * (2 * n_w), out_shape=[pltpu.HBM(a.shape, a.dtype) for a in (*shards, *fulls)],
        input_output_aliases={i: i for i in range(2 * n_w)},
        compiler_params=pltpu.CompilerParams(has_side_effects=DATAFLOW),
    )(*shards, *fulls, send_sems, recv_sems, after)
    return res[n_w:]


def _gather_layer_forward(shapes, kinds, fulls, *, name):
    n_w = len(fulls)

    def body(*refs):
        plan = _GatherPlan([None] * n_w, refs[n_w:2 * n_w], shapes, kinds, 0, *refs[2 * n_w:])
        passed = plan.forwards(0)
        for cp in passed:
            cp.start()
        for cp in plan.forwarded(0):
            cp.wait_recv()
        for cp in passed:
            cp.wait_send()

    return pl.pallas_call(
        body, name=name, in_specs=[HBM] * n_w, out_specs=[HBM] * n_w,
        out_shape=[jax.ShapeDtypeStruct(f.shape, f.dtype) for f in fulls],
        input_output_aliases={i: i for i in range(n_w)},
        scratch_shapes=[pltpu.SemaphoreType.DMA((3 * n_w,)), pltpu.SemaphoreType.DMA((3 * n_w,))],
    )(*fulls)


def _grads_to_sibling(grads, layer, *, name):
    n_w = len(grads)

    def body(*refs):
        src, dst = refs[:n_w], refs[n_w:2 * n_w]
        send_sems, recv_sems = refs[2 * n_w:]
        x, y, c = _my_place()
        cps = [pltpu.make_async_remote_copy(src_ref=src[i], dst_ref=dst[i], send_sem=send_sems.at[i],
                                            recv_sem=recv_sems.at[i], device_id=(x, y, layer), device_id_type=MESH)
               for i in range(n_w)]

        @pl.when(c != layer)
        def _():
            for cp in cps:
                cp.start()
            for cp in cps:
                cp.wait_send()

        @pl.when(c == layer)
        def _():
            for cp in cps:
                cp.wait_recv()

    return pl.pallas_call(
        body, name=name, in_specs=[HBM] * n_w, out_specs=[HBM] * n_w,
        out_shape=[jax.ShapeDtypeStruct(g.shape, g.dtype) for g in grads],
        scratch_shapes=[pltpu.SemaphoreType.DMA((n_w,)), pltpu.SemaphoreType.DMA((n_w,))],
    )(*grads)


def _on_core(layer):
    return (lax.axis_index("c") == layer).astype(jnp.int32).reshape(1)


def _pair_add(mine, other, layer, *, name):
    k, n = mine.shape
    tr = _div_tile(k, 512, BF16_ROWS)

    def body(on_ref, a_ref, b_ref, o_ref):
        @pl.when(on_ref[0] == 1)
        def _():
            o_ref[...] = (a_ref[...].astype(F32) + b_ref[...].astype(F32)).astype(o_ref.dtype)

    blk = pl.BlockSpec((tr, n), lambda i, on: (i * on[0], 0))
    return pl.pallas_call(
        body, name=name,
        grid_spec=pltpu.PrefetchScalarGridSpec(num_scalar_prefetch=1, grid=(k // tr,), in_specs=[blk, blk], out_specs=blk),
        out_shape=jax.ShapeDtypeStruct((k, n), mine.dtype), compiler_params=_params(dimension_semantics=("arbitrary",)),
    )(_on_core(layer), mine, other)


class _ScatterPlan:
    def __init__(self, src, dst, kinds, sizes, layer, send_sems, recv_sems):
        self.src, self.dst, self.kinds, self.sizes, self.layer = src, dst, kinds, sizes, layer
        self.send_sems, self.recv_sems = send_sems, recv_sems
        self.x, self.y, self.c = _my_place()
        self.mine = 2 * self.x + self.y
        self.chips = _other_chips(self.x, self.y)
        self.n = len(src)

    def _copy(self, i, k, chip, window_of, slab):
        return pltpu.make_async_remote_copy(src_ref=_window(self.src[i], self.kinds[i], self.sizes[i], window_of, ()),
                                            dst_ref=self.dst[i].at[slab], send_sem=self.send_sems.at[3 * i + k],
                                            recv_sem=self.recv_sems.at[3 * i + k], device_id=(*chip, self.layer),
                                            device_id_type=MESH)

    def sends(self):
        return [self._copy(i, k, chip, _chip_no(chip), self.mine) for k, chip in enumerate(self.chips) for i in range(self.n)]

    def arrivals(self):
        return [self._copy(i, k, chip, self.mine, _chip_no(chip)) for k, chip in enumerate(self.chips) for i in range(self.n)]


def _slab_shape(p, kind, size):
    return (N_CHIPS,) + {"col": (p.shape[0], size), "row": (size, p.shape[1]), "win_main": (p.shape[0], size),
                         "win_strad": (p.shape[0], LANES)}[kind]


def _grads_to_chips_start(pairs, kinds, sizes, layer, *, name):
    n_w = len(pairs)

    def body(*refs):
        plan = _ScatterPlan(refs[:n_w], refs[n_w:2 * n_w], kinds, sizes, layer, refs[2 * n_w], refs[2 * n_w + 1])

        @pl.when(plan.c == layer)
        def _():
            for cp in plan.sends():
                cp.start()

    slabs = [lax.empty(_slab_shape(p, kind, size), p.dtype) for p, kind, size in zip(pairs, kinds, sizes)]
    operands = [pltpu.with_memory_space_constraint(a, pltpu.HBM) for a in (*pairs, *slabs)]
    res = pl.pallas_call(
        body, name=name, in_specs=[IN_HBM] * (2 * n_w), out_specs=(IN_SEM, IN_SEM, *([IN_HBM] * (2 * n_w))),
        out_shape=(pltpu.SemaphoreType.DMA((3 * n_w,)), pltpu.SemaphoreType.DMA((3 * n_w,)),
                   *[pltpu.HBM(a.shape, a.dtype) for a in operands]),
        input_output_aliases={i: 2 + i for i in range(2 * n_w)},
        compiler_params=pltpu.CompilerParams(has_side_effects=DATAFLOW),
    )(*operands)
    return res[0], res[1], res[2:2 + n_w], res[2 + n_w:]


def _grads_to_chips_wait(send_sems, recv_sems, pairs, slabs, kinds, sizes, layer, after, *, name):
    n_w = len(pairs)

    def body(*refs):
        plan = _ScatterPlan(refs[:n_w], refs[n_w:2 * n_w], kinds, sizes, layer, refs[2 * n_w], refs[2 * n_w + 1])

        @pl.when(plan.c == layer)
        def _():
            for cp in plan.sends():
                cp.wait_send()
            for cp in plan.arrivals():
                cp.wait_recv()

    res = pl.pallas_call(
        body, name=name, in_specs=[IN_HBM] * (2 * n_w) + [IN_SEM, IN_SEM, pl.BlockSpec(memory_space=pl.ANY)],
        out_specs=[IN_HBM] * (2 * n_w), out_shape=[pltpu.HBM(a.shape, a.dtype) for a in (*pairs, *slabs)],
        input_output_aliases={i: i for i in range(2 * n_w)},
        compiler_params=pltpu.CompilerParams(has_side_effects=DATAFLOW),
    )(*pairs, *slabs, send_sems, recv_sems, after)
    return res[:n_w], res[n_w:]


def _sum_slabs(slabs, pair, kind, size, layer, into, *, name):
    n_s, k, n = slabs.shape
    tr = _div_tile(k, 512, BF16_ROWS)
    tc = n if kind in ("col", "row") else LANES
    x, y, _ = _my_place()
    mine = 2 * x + y
    shard = size + HEAD_DIM
    row0 = mine * (k // tr) if kind == "row" else 0
    col0 = {"col": mine, "row": 0, "win_main": (mine * shard + HEAD_DIM * (mine % 2)) // LANES,
            "win_strad": (size + 2 * shard * (mine // 2)) // LANES}[kind]
    on = _on_core(layer)[0]
    scalars = jnp.stack([mine, row0 * on, col0 * on, on]).astype(jnp.int32)

    def body(s_ref, slab_ref, own_ref, *rest):
        o_ref = rest[-1]
        me = s_ref[0]

        @pl.when(s_ref[3] == 1)
        def _():
            acc = jnp.zeros(o_ref.shape, F32)
            for i in range(n_s):
                acc = acc + jnp.where(me == i, own_ref[...], slab_ref[i]).astype(F32)
            o_ref[...] = acc

    operands = [scalars, slabs, pair] + ([] if into is None else [into])
    return pl.pallas_call(
        body, name=name,
        grid_spec=pltpu.PrefetchScalarGridSpec(
            num_scalar_prefetch=1, grid=(k // tr, n // tc),
            in_specs=[pl.BlockSpec((n_s, tr, tc), lambda i, j, s: (0, i * s[3], j * s[3])),
                      pl.BlockSpec((tr, tc), lambda i, j, s: (s[1] + i * s[3], s[2] + j * s[3]))]
            + ([] if into is None else [HBM]),
            out_specs=pl.BlockSpec((None, tr, tc), lambda i, j, s: (layer, i * s[3], j * s[3]))),
        out_shape=jax.ShapeDtypeStruct((2, k, n), F32),
        input_output_aliases={} if into is None else {3: 0},
        compiler_params=_params(dimension_semantics=("arbitrary", "arbitrary")),
    )(*operands)


def _exchange_layers(bufs, *, name):
    n_w = len(bufs)

    def body(*refs):
        dst = refs[n_w:2 * n_w]
        send_sems, recv_sems = refs[2 * n_w:]
        x, y, c = _my_place()

        def copy(i, layer):
            return pltpu.make_async_remote_copy(src_ref=dst[i].at[layer], dst_ref=dst[i].at[layer], send_sem=send_sems.at[i],
                                                recv_sem=recv_sems.at[i], device_id=(x, y, 1 - c), device_id_type=MESH)

        sends = [copy(i, c) for i in range(n_w)]
        for cp in sends:
            cp.start()
        for i in range(n_w):
            copy(i, 1 - c).wait_recv()
        for cp in sends:
            cp.wait_send()

    return pl.pallas_call(
        body, name=name, in_specs=[HBM] * n_w, out_specs=[HBM] * n_w,
        out_shape=[jax.ShapeDtypeStruct(b.shape, b.dtype) for b in bufs],
        input_output_aliases={i: i for i in range(n_w)},
        scratch_shapes=[pltpu.SemaphoreType.DMA((n_w,)), pltpu.SemaphoreType.DMA((n_w,))],
    )(*bufs)


def _all_sum_small(v, *, name):
    r = v.shape[0]
    relations = [(dx, dy, dc) for dx in (0, 1) for dy in (0, 1) for dc in (0, 1)][1:]

    def body(v_ref, o_ref, buf, send_sems, recv_sems):
        x, y, c = _my_place()
        me = 4 * x + 2 * y + c
        buf[me] = v_ref[...]
        peers = [(x + dx - 2 * x * dx, y + dy - 2 * y * dy, c + dc - 2 * c * dc) for dx, dy, dc in relations]

        def copy(k, slot):
            return pltpu.make_async_remote_copy(src_ref=v_ref, dst_ref=buf.at[slot], send_sem=send_sems.at[k],
                                                recv_sem=recv_sems.at[k], device_id=peers[k], device_id_type=MESH)

        sends = [copy(k, me) for k in range(len(relations))]
        for cp in sends:
            cp.start()
        for k, (px, py, pc) in enumerate(peers):
            copy(k, 4 * px + 2 * py + pc).wait_recv()
        for cp in sends:
            cp.wait_send()
        acc = buf[0]
        for i in range(1, 8):
            acc = acc + buf[i]
        o_ref[...] = acc

    vm = pl.BlockSpec(memory_space=pltpu.VMEM)
    return pl.pallas_call(
        body, name=name, in_specs=[vm], out_specs=vm, out_shape=jax.ShapeDtypeStruct((r, LANES), F32),
        scratch_shapes=[pltpu.VMEM((8, r, LANES), F32), pltpu.SemaphoreType.DMA((7,)), pltpu.SemaphoreType.DMA((7,))],
    )(v)


SHARDED = (("ffn1_w_up", "col"), ("ffn1_w_down", "row"), ("w_in", "win"), ("w_branch_a", "col"),
           ("w_branch_b", "col"), ("w_out", "row"), ("ffn2_w_up", "col"), ("ffn2_w_down", "row"))
REPLICATED = ("ffn1_norm", "mix_norm", "na_rel_bias", "ffn2_norm", "final_norm")


def _weight_pieces(w):
    even = lax.axis_index("y") == 0
    shards, kinds, names = [], [], []
    for name, kind in SHARDED:
        wb = w[name].astype(BF16)
        if kind == "win":
            main = wb.shape[-1] - HEAD_DIM
            assert main % LANES == 0
            zeros = jnp.zeros(wb.shape[:-1] + (HEAD_DIM,), BF16)
            shards += [jnp.where(even, wb[..., :main], wb[..., HEAD_DIM:]),
                       jnp.where(even, jnp.concatenate([wb[..., main:], zeros], -1),
                                 jnp.concatenate([zeros, wb[..., :HEAD_DIM]], -1))]
            kinds += ["win_main", "slot"]
            names += [name, name + "_strad"]
        else:
            shards.append(wb)
            kinds.append(kind)
            names.append(name)
    return names, kinds, shards


def _finish_w_in(full):
    full = dict(full)
    strad = full.pop("w_in_strad")
    main = full["w_in"].shape[1] // N_CHIPS - HEAD_DIM
    for i in range(N_CHIPS // 2):
        lo = main + 2 * (main + HEAD_DIM) * i
        full["w_in"] = full["w_in"].at[:, lo:lo + LANES].set(strad[2 * i] + strad[2 * i + 1])
    return full


def _scatter_pieces(shards):
    names, kinds, sizes, srcs = [], [], [], []
    for name, kind in SHARDED:
        shp = shards[name].shape
        if kind == "win":
            names += [name, name + "_strad"]
            kinds += ["win_main", "win_strad"]
            sizes += [shp[2] - HEAD_DIM] * 2
            srcs += [name, name]
        else:
            names.append(name)
            kinds.append(kind)
            sizes.append(shp[1] if kind == "row" else shp[2])
            srcs.append(name)
    return names, kinds, sizes, srcs


def _pair_sums(grads, layer, tag):
    uniq = list(grads)
    arrived = _grads_to_sibling([grads[n] for n in uniq], layer, name=f"{tag}_to_sibling")
    return {n: _pair_add(grads[n], a, layer, name=f"{tag}_pair_{n}") for n, a in zip(uniq, arrived)}


def _finish_weight_grads(reduced, names, tag):
    out = dict(zip(names, _exchange_layers(reduced, name=f"{tag}_layers")))
    if "w_in_strad" in out:
        strad = out.pop("w_in_strad")
        even = lax.axis_index("y") == 0
        out["w_in"] = jnp.where(even, jnp.concatenate([out["w_in"], strad[..., :HEAD_DIM]], -1),
                                jnp.concatenate([strad[..., HEAD_DIM:], out["w_in"]], -1))
    return out


class _Grads:
    def __init__(self):
        self.arrays = {}

    def put(self, weight, layer, a, b, *, cols=None, col_off=0, **kw):
        self.arrays[weight, layer] = _mm(a, b, mode="tn", out_dtype=BF16, out_cols=cols, out_col_off=col_off,
                                         out_into=self.arrays.get((weight, layer)), **kw)


def _ffn_fwd(x, h, w_up, w_down, tag):
    t, d = x.shape
    f = w_down.shape[0]
    a, gate, up = _mm_swiglu_fwd(h, w_up, tm=_div_tile(t, ROWS_NARROW, 8), tn=MXU_N, name=f"{tag}_up")
    x_out = _mm(a, w_down, mode="nn", out_dtype=F32, tm=_div_tile(t, ROWS_WIDE, 8), tn=d, tk=f, alpha=0.5, res=x, name=f"{tag}_down")
    return x_out, (x, h, a, gate, up)


def _ffn_bwd(dx, dxb, saved, norm_g, w_up, w_down, layer, grads, wname, tag, scatter):
    x, h, a, gate, up = saved
    t, d = x.shape
    f = w_down.shape[0]
    tn = _div_tile(f, 1408)
    grads.put(f"{wname}_w_down", layer, a, dxb, tm=tn, tn=d, tk=1024, alpha=0.5, name=f"{tag}_dwd")
    d_gate, d_up = _mm_swiglu_bwd(dxb, w_down, gate, up, alpha=0.5, tm=_div_tile(t, ROWS_NARROW, 8), tn=MXU_N, name=f"{tag}_da")
    grads.put(f"{wname}_w_up", layer, h, d_gate, cols=2 * f, tm=d, tn=tn, tk=1024, name=f"{tag}_dwg")
    grads.put(f"{wname}_w_up", layer, h, d_up, cols=2 * f, col_off=f // tn, tm=d, tn=tn, tk=1024, name=f"{tag}_dwu")
    scatter(layer, [f"{wname}_w_up", f"{wname}_w_down"])
    dh = _mm(d_gate, w_up, mode="nt", out_dtype=F32, tm=_div_tile(t, ROWS_WIDE, 8), tn=d, tk=f, name=f"{tag}_dh1")
    dh = _mm(d_up, w_up, mode="nt", out_dtype=F32, tm=_div_tile(t, ROWS_WIDE, 8), tn=d, tk=f, b_k_off=1, res=dh, name=f"{tag}_dh2")
    return _rms_bwd(dh, x, norm_g, dx, tt=512, name=f"{tag}_dnorm")


def _to_heads(y, b, n_heads):
    t, w = y.shape
    return y.reshape(b, t // b, n_heads, HEAD_DIM).transpose(0, 2, 1, 3)


def _from_heads(y):
    b, n, s, hd = y.shape
    return y.transpose(0, 2, 1, 3).reshape(b * s, n * hd)


N_QKV = 3 * (DIL_HEADS + NA_HEADS) * HEAD_DIM


def _mixer_fwd(x, b, norm_g, full, bias, tabs, tag):
    t, d = x.shape
    s = t // b
    n_in = full["w_in"].shape[1]
    h = _rms_fwd(x, norm_g, tt=512, name=f"{tag}_norm")
    proj = _mm(h, full["w_in"], mode="nn", out_dtype=F32, tm=_div_tile(t, ROWS_NARROW, 8), tn=MXU_N, tk=d, name=f"{tag}_in")
    heads = _split_heads(proj.reshape(b, s, -1), *tabs, n_pairs=N_QKV // LANES, rot_pairs=DIL_HEADS,
                         scale_ranges=((0, DIL_HEADS // 2), (3 * DIL_HEADS // 2, (3 * DIL_HEADS + NA_HEADS) // 2)),
                         name=f"{tag}_heads")
    ya, lse_a = _dil_attn_fwd(heads, name=f"{tag}_dil")
    yb, lse_b = _na_attn_fwd(heads, bias, first=3 * DIL_HEADS, name=f"{tag}_na")
    ya2, yb2 = _from_heads(ya), _from_heads(yb)
    z = _mm(ya2, full["w_branch_a"], mode="nn", out_dtype=F32, tm=_div_tile(t, ROWS_NARROW, 8), tn=MXU_N, tk=ya2.shape[1],
            out_slab=(0, 2), name=f"{tag}_za")
    z = _mm(yb2, full["w_branch_b"], mode="nn", out_dtype=F32, tm=_div_tile(t, ROWS_NARROW, 8), tn=MXU_N, tk=yb2.shape[1],
            out_slab=(1, 2), out_into=z, name=f"{tag}_zb")
    merged = _gate_fwd(proj, z, gate_col=N_QKV, tt=1024, name=f"{tag}_gate")
    x_out = _mm(merged, full["w_out"], mode="nn", out_dtype=F32, tm=_div_tile(t, ROWS_NARROW, 8), tn=MXU_N, tk=d, res=x, name=f"{tag}_out")
    return x_out, (x, h, proj, heads, ya, lse_a, yb, lse_b, ya2, yb2, z, merged)


def _mixer_bwd(dx, dob, b, saved, norm_g, full, layer, bias, tabs, grads, tag, scatter):
    x, h, proj, heads, ya, lse_a, yb, lse_b, ya2, yb2, z, merged = saved
    t, d = x.shape
    s = t // b
    n_in = full["w_in"].shape[1]
    grads.put("w_out", layer, merged, dob, tm=d, tn=d, tk=1024, name=f"{tag}_dwo")
    dm = _mm(dob, full["w_out"], mode="nt", out_dtype=F32, tm=_div_tile(t, ROWS_NARROW, 8), tn=MXU_N, tk=d, name=f"{tag}_dm")
    dz, dproj = _gate_bwd(dm, proj, z, gate_col=N_QKV, tt=1024, name=f"{tag}_dgate")
    grads.put("w_branch_a", layer, ya2, dz, b_sel=0, tm=ya2.shape[1], tn=d, tk=1024, name=f"{tag}_dwa")
    grads.put("w_branch_b", layer, yb2, dz, b_sel=1, tm=yb2.shape[1], tn=d, tk=1024, name=f"{tag}_dwb")
    scatter(layer, ["w_out", "w_branch_a", "w_branch_b"])
    dya = _mm(dz, full["w_branch_a"], mode="nt", out_dtype=F32, tm=_div_tile(t, ROWS_NARROW, 8), tn=MXU_N, tk=d, a_sel=0, name=f"{tag}_dya")
    dyb = _mm(dz, full["w_branch_b"], mode="nt", out_dtype=F32, tm=_div_tile(t, ROWS_NARROW, 8), tn=MXU_N, tk=d, a_sel=1, name=f"{tag}_dyb")
    d_dil = _dil_attn_bwd(heads, ya, lse_a, _to_heads(dya, b, DIL_GROUP_HEADS), name=f"{tag}_ddil")
    d_na, d_bias = _na_attn_bwd(heads, bias, yb, lse_b, _to_heads(dyb, b, NA_HEADS), first=3 * DIL_HEADS, name=f"{tag}_dna")
    dproj = _merge_heads(d_dil, *tabs, heads_per_row=DIL_GROUP_HEADS, rot_pairs=DIL_HEADS, scale_pairs=DIL_HEADS // 2,
                         dilated=True, out_cols=n_in, tile_off=0, into=dproj.reshape(b, s, n_in), name=f"{tag}_dheads_a")
    dproj = _merge_heads(d_na, *tabs, heads_per_row=NA_HEADS, rot_pairs=0, scale_pairs=NA_HEADS // 2, dilated=False,
                         out_cols=n_in, tile_off=3 * DIL_HEADS // 2, into=dproj, name=f"{tag}_dheads_b").reshape(t, n_in)
    grads.put("w_in", layer, h, dproj, tm=_div_tile(d, 512), tn=_div_tile(n_in, 2944), tk=1024, name=f"{tag}_dwin")
    scatter(layer, ["w_in"])
    dh = _mm(dproj, full["w_in"], mode="nt", out_dtype=F32, tm=_div_tile(t, ROWS_WIDE, 8), tn=d, tk=_div_tile(n_in, 2944), name=f"{tag}_dh")
    dx_in, dxb_in, d_norm = _rms_bwd(dh, x, norm_g, dx, tt=512, name=f"{tag}_dnorm")
    d_rb = _na_collapse_bias(d_bias, name=f"{tag}_dbias")
    return dx_in, dxb_in, d_norm, d_rb


def kernel(x, ffn1_norm, ffn1_w_up, ffn1_w_down, mix_norm, w_in, na_rel_bias, w_branch_a, w_branch_b, w_out, ffn2_norm, ffn2_w_up, ffn2_w_down, final_norm, loss_target, m_ffn1_norm, m_ffn1_w_up, m_ffn1_w_down, m_mix_norm, m_w_in, m_na_rel_bias, m_w_branch_a, m_w_branch_b, m_w_out, m_ffn2_norm, m_ffn2_w_up, m_ffn2_w_down, m_final_norm, v_ffn1_norm, v_ffn1_w_up, v_ffn1_w_down, v_mix_norm, v_w_in, v_na_rel_bias, v_w_branch_a, v_w_branch_b, v_w_out, v_ffn2_norm, v_ffn2_w_up, v_ffn2_w_down, v_final_norm):
    w = dict(ffn1_norm=ffn1_norm, ffn1_w_up=ffn1_w_up, ffn1_w_down=ffn1_w_down, mix_norm=mix_norm, w_in=w_in,
             na_rel_bias=na_rel_bias, w_branch_a=w_branch_a, w_branch_b=w_branch_b, w_out=w_out, ffn2_norm=ffn2_norm,
             ffn2_w_up=ffn2_w_up, ffn2_w_down=ffn2_w_down, final_norm=final_norm)
    mom = dict(ffn1_norm=m_ffn1_norm, ffn1_w_up=m_ffn1_w_up, ffn1_w_down=m_ffn1_w_down, mix_norm=m_mix_norm, w_in=m_w_in,
               na_rel_bias=m_na_rel_bias, w_branch_a=m_w_branch_a, w_branch_b=m_w_branch_b, w_out=m_w_out,
               ffn2_norm=m_ffn2_norm, ffn2_w_up=m_ffn2_w_up, ffn2_w_down=m_ffn2_w_down, final_norm=m_final_norm)
    var = dict(ffn1_norm=v_ffn1_norm, ffn1_w_up=v_ffn1_w_up, ffn1_w_down=v_ffn1_w_down, mix_norm=v_mix_norm, w_in=v_w_in,
               na_rel_bias=v_na_rel_bias, w_branch_a=v_w_branch_a, w_branch_b=v_w_branch_b, w_out=v_w_out,
               ffn2_norm=v_ffn2_norm, ffn2_w_up=v_ffn2_w_up, ffn2_w_down=v_ffn2_w_down, final_norm=v_final_norm)
    b, s, d = x.shape
    t = b * s
    depth = ffn1_norm.shape[0]
    assert depth == 2, "core c of a chip sends / reduces layer c"
    shards = {name: w[name] for name, _ in SHARDED}

    names, kinds, pieces = _weight_pieces(w)
    by_layer = [[p[l:l + 1] for p in pieces] for l in range(depth)]
    own = [[_place_own(p, kind, 0, name=f"own{l}_{nm}") for nm, kind, p in zip(names, kinds, by_layer[l])] for l in range(depth)]
    full = [{}, {}]

    def gather_start(layer, group, after, tag):
        idx = [i for i, nm in enumerate(names) if nm in group]
        pick = lambda seq: [seq[i] for i in idx]
        *state, token = _gather_layer_start(pick(by_layer[layer]), pick(kinds), pick(own[layer]), 0, after, name=f"{tag}_start")
        return (layer, idx, tag, state), token[:1, :1]

    def gather_finish(started, after):
        layer, idx, tag, state = started
        pick = lambda seq: [seq[i] for i in idx]
        landed = _gather_layer_wait(*state, pick(kinds), 0, after, name=f"{tag}_wait")
        done = _gather_layer_forward([by_layer[layer][i].shape for i in idx], pick(kinds), landed, name=f"{tag}_forward")
        full[layer].update(zip(pick(names), done))
        return done[0]

    ffn1, mixer, ffn2 = names[:2], names[2:7], names[7:]
    assert mixer[0] == "w_in" and ffn2[0] == "ffn2_w_up", names
    xc = x.reshape(t, d)
    l0_ffn1, token_ffn1 = gather_start(0, ffn1, xc, "gather_l0_ffn1")
    tabs = _rope_tables(s)
    bias = _na_expand_bias(na_rel_bias, name="na_bias")

    saved = []
    h = _rms_fwd(xc, ffn1_norm[:1] + token_ffn1, tt=512, name="l0_ffn1_norm")
    landed = gather_finish(l0_ffn1, h)
    l0_mixer, token_mixer = gather_start(0, mixer, landed, "gather_l0_mixer")
    xc, s1 = _ffn_fwd(xc, h + token_mixer.astype(BF16), full[0]["ffn1_w_up"], full[0]["ffn1_w_down"], "l0_ffn1")
    landed = gather_finish(l0_mixer, xc)
    full[0] = _finish_w_in(full[0])
    l0_ffn2, token_ffn2 = gather_start(0, ffn2, landed, "gather_l0_ffn2")
    layer1, token_layer1 = gather_start(1, names, landed, "gather_l1")
    xc, s2 = _mixer_fwd(xc, b, mix_norm[:1] + token_ffn2 + token_layer1, full[0], bias[0], tabs, "l0_mix")
    gather_finish(l0_ffn2, xc)
    xc, s3 = _ffn_fwd(xc, _rms_fwd(xc, ffn2_norm[:1], tt=512, name="l0_ffn2_norm"), full[0]["ffn2_w_up"], full[0]["ffn2_w_down"],
                      "l0_ffn2")
    saved.append((s1, s2, s3))
    gather_finish(layer1, xc)
    full[1] = _finish_w_in(full[1])
    for l in range(1, depth):
        xc, s1 = _ffn_fwd(xc, _rms_fwd(xc, ffn1_norm[l:l + 1], tt=512, name=f"l{l}_ffn1_norm"), full[l]["ffn1_w_up"],
                          full[l]["ffn1_w_down"], f"l{l}_ffn1")
        xc, s2 = _mixer_fwd(xc, b, mix_norm[l:l + 1], full[l], bias[l], tabs, f"l{l}_mix")
        xc, s3 = _ffn_fwd(xc, _rms_fwd(xc, ffn2_norm[l:l + 1], tt=512, name=f"l{l}_ffn2_norm"), full[l]["ffn2_w_up"],
                          full[l]["ffn2_w_down"], f"l{l}_ffn2")
        saved.append((s1, s2, s3))

    dx, dxb, d_final, loss_part = _final_loss(xc, final_norm.reshape(1, d), loss_target.reshape(t, d), tt=512, name="final_loss")
    grads = _Grads()
    piece_names, piece_kinds, piece_sizes, piece_srcs = _scatter_pieces(shards)
    scattered = []

    def scatter(layer, weights):
        tag = f"grads{layer}_{weights[0]}"
        pair = _pair_sums({wn: grads.arrays[wn, layer] for wn in weights}, layer, tag)
        idx = [i for i, src in enumerate(piece_srcs) if src in weights]
        pick = lambda seq: [seq[i] for i in idx]
        scattered.append((layer, idx, _grads_to_chips_start([pair[src] for src in pick(piece_srcs)], pick(piece_kinds),
                                                            pick(piece_sizes), layer, name=f"{tag}_to_chips_start")))
    small = {name: [None] * depth for name in REPLICATED[:-1]}
    for l in reversed(range(depth)):
        s1, s2, s3 = saved[l]
        dx, dxb, small["ffn2_norm"][l] = _ffn_bwd(dx, dxb, s3, ffn2_norm[l:l + 1], full[l]["ffn2_w_up"], full[l]["ffn2_w_down"],
                                                  l, grads, "ffn2", f"l{l}_ffn2", scatter)
        dx, dxb, small["mix_norm"][l], small["na_rel_bias"][l] = _mixer_bwd(
            dx, dxb, b, s2, mix_norm[l:l + 1], full[l], l, bias[l], tabs, grads, f"l{l}_mix", scatter)
        dx, dxb, small["ffn1_norm"][l] = _ffn_bwd(dx, dxb, s1, ffn1_norm[l:l + 1], full[l]["ffn1_w_up"], full[l]["ffn1_w_down"],
                                                  l, grads, "ffn1", f"l{l}_ffn1", scatter)
    grad_x = dx.reshape(b, s, d)
    reduced = [None] * len(piece_names)

    def arrive(group, after):
        layer, idx, state = group
        state = _grads_to_chips_wait(*state, [piece_kinds[i] for i in idx], [piece_sizes[i] for i in idx], layer, after,
                                     name=f"grads{layer}_{piece_names[idx[0]]}_to_chips_wait")
        for i, p, sl in zip(idx, *state):
            reduced[i] = _sum_slabs(sl, p, piece_kinds[i], piece_sizes[i], layer, reduced[i],
                                    name=f"grads{layer}_sum_{piece_names[i]}")
        return idx

    for group in scattered[:-1]:
        arrive(group, dx)
    late = scattered[-1][1]
    early = [i for i in range(len(piece_names)) if i not in late]
    g_out = _finish_weight_grads([reduced[i] for i in early], [piece_names[i] for i in early], "grads_early")

    parts = [jnp.stack(small[name]).reshape(-1) for name in REPLICATED[:-1]] + [d_final.reshape(-1), loss_part[0, :1]]
    sizes = [v.shape[0] for v in parts]
    flat = jnp.concatenate(parts)
    flat = jnp.pad(flat, (0, -flat.shape[0] % (8 * LANES)))
    small_sum = _all_sum_small(flat.reshape(-1, LANES), name="small_all_sum").reshape(-1)
    off = 0
    for name, n in zip(REPLICATED, sizes[:-1]):
        g_out[name] = small_sum[off:off + n].reshape(w[name].shape)
        off += n
    loss = small_sum[off]

    names = list(w)
    delta, new_m, new_v = {}, {}, {}
    for name in [n for n in names if n in g_out]:
        delta[name], new_m[name], new_v[name] = _adamw(w[name], g_out[name], mom[name], var[name], name=f"adamw_{name}")
    arrive(scattered[-1], delta["w_in"])
    g_out.update(_finish_weight_grads([reduced[i] for i in late], [piece_names[i] for i in late], "grads_late"))
    for name in [n for n in names if n not in delta]:
        delta[name], new_m[name], new_v[name] = _adamw(w[name], g_out[name], mom[name], var[name], name=f"adamw_{name}")
    return (loss, grad_x, *[g_out[n] for n in names], *[delta[n] for n in names], *[new_m[n] for n in names],
            *[new_v[n] for n in names])
```

```python
import functools

import numpy as np
import jax
import jax.numpy as jnp
from jax import lax
from jax.experimental import pallas as pl
from jax.experimental.pallas import tpu as pltpu

F32, BF16 = jnp.float32, jnp.bfloat16
MESH = pl.DeviceIdType.MESH

HEAD_DIM = 64
DILATIONS = (1, 4, 16)
DIL_HALF = 64
DIL_GROUP_HEADS = 4
DIL_HEADS = 12
NA_HEADS = 8
GRID_W = 64
NA_ROWS = 8
NA_COLS = 16
ROPE_THETA = 10000.0
RMS_EPS = 1e-6
NEG_INF = -1e30
ADAM_LR, ADAM_B1, ADAM_B2, ADAM_EPS, ADAM_WD, ADAM_STEP = 0.001, 0.9, 0.999, 1e-08, 0.01, 10
QK_SCALE = HEAD_DIM ** -0.5

N_CHIPS = 4
LANES = 128
BF16_ROWS = 16
VMEM_LIMIT = 56 * 1024 * 1024
MXU_N = 256
ROWS_NARROW = 2048
ROWS_WIDE = 512

_NN = (((1,), (0,)), ((), ()))
_NT = (((1,), (1,)), ((), ()))
_TN = (((0,), (0,)), ((), ()))

HBM = pl.BlockSpec(memory_space=pl.ANY)


def _params(**kw):
    return pltpu.CompilerParams(vmem_limit_bytes=VMEM_LIMIT, **kw)


def _dot(a, b, dims):
    return lax.dot_general(a, b, dims, preferred_element_type=F32)


def _div_tile(n, cap, mult=LANES):
    best = None
    for t in range(mult, min(n, cap) + 1, mult):
        if n % t == 0:
            best = t
    return n if best is None else best


def _stacked(block, index, sel):
    if sel is None:
        return pl.BlockSpec(block, index)
    return pl.BlockSpec((None,) + block, lambda *g: (sel,) + index(*g))


def _mm(a, b, *, mode, out_dtype, tm, tn, tk, name, alpha=1.0, res=None, a_sel=None, b_sel=None, b_k_off=0,
        out_slab=None, out_cols=None, out_col_off=0, out_into=None):
    a2, b2 = a.shape[-2:], b.shape[-2:]
    if mode == "nn":
        (m, k), n = a2, b2[1]
        a_spec = _stacked((tm, tk), lambda i, j, kk: (i, kk), a_sel)
        b_spec = _stacked((tk, tn), lambda i, j, kk: (kk + b_k_off, j), b_sel)
        dims = _NN
    elif mode == "nt":
        (m, k), n = a2, b2[0]
        a_spec = _stacked((tm, tk), lambda i, j, kk: (i, kk), a_sel)
        b_spec = _stacked((tn, tk), lambda i, j, kk: (j, kk + b_k_off), b_sel)
        dims = _NT
    else:
        (k, m), n = a2, b2[1]
        a_spec = _stacked((tk, tm), lambda i, j, kk: (kk, i), a_sel)
        b_spec = _stacked((tk, tn), lambda i, j, kk: (kk + b_k_off, j), b_sel)
        dims = _TN
    assert m % tm == 0 and n % tn == 0 and k % tk == 0, (name, a.shape, b.shape)
    nk = k // tk
    has_res = res is not None
    if out_slab is None:
        o_spec = pl.BlockSpec((tm, tn), lambda i, j, kk: (i, j + out_col_off))
        out_shape = jax.ShapeDtypeStruct((m, n if out_cols is None else out_cols), out_dtype)
    else:
        o_spec = _stacked((tm, tn), lambda i, j, kk: (i, j + out_col_off), out_slab[0])
        out_shape = jax.ShapeDtypeStruct((out_slab[1], m, n if out_cols is None else out_cols), out_dtype)
    r_spec = pl.BlockSpec((tm, tn), lambda i, j, kk: (i, j))
    n_in = 2 + has_res + (out_into is not None)

    def body(*refs):
        a_ref, b_ref = refs[0], refs[1]
        r_ref = refs[2] if has_res else None
        o_ref = refs[n_in]
        p = _dot(a_ref[...], b_ref[...], dims)

        def finish(acc):
            y = acc * alpha if alpha != 1.0 else acc
            if has_res:
                y = y + r_ref[...].astype(F32)
            o_ref[...] = y.astype(o_ref.dtype)

        if nk == 1:
            finish(p)
        else:
            acc_ref = refs[n_in + 1]
            kk = pl.program_id(2)

            @pl.when(kk == 0)
            def _():
                acc_ref[...] = p

            @pl.when(kk > 0)
            def _():
                acc_ref[...] += p

            @pl.when(kk == nk - 1)
            def _():
                finish(acc_ref[...])

    operands = [a, b] + ([res] if has_res else [])
    in_specs = [a_spec, b_spec] + ([r_spec] if has_res else [])
    aliases = {}
    if out_into is not None:
        aliases = {len(operands): 0}
        operands.append(out_into)
        in_specs.append(HBM)
    return pl.pallas_call(
        body, name=name, grid=(m // tm, n // tn, nk), in_specs=in_specs, out_specs=o_spec, out_shape=out_shape,
        scratch_shapes=[pltpu.VMEM((tm, tn), F32)] if nk > 1 else [], input_output_aliases=aliases,
        compiler_params=_params(dimension_semantics=("parallel", "parallel", "arbitrary")),
    )(*operands)


def _mm_swiglu_fwd(h, w_up, *, tm, tn, name):
    m, k = h.shape
    n = w_up.shape[1] // 2
    h_spec = pl.BlockSpec((tm, k), lambda i, j: (i, 0))
    wg_spec = pl.BlockSpec((k, tn), lambda i, j: (0, j))
    wu_spec = pl.BlockSpec((k, tn), lambda i, j: (0, j + n // tn))
    o_spec = pl.BlockSpec((tm, tn), lambda i, j: (i, j))

    def body(h_ref, wg_ref, wu_ref, a_ref, g_ref, u_ref):
        hb = h_ref[...]
        g = _dot(hb, wg_ref[...], _NN)
        u = _dot(hb, wu_ref[...], _NN)
        a_ref[...] = (g * jax.nn.sigmoid(g) * u).astype(BF16)
        g_ref[...] = g.astype(BF16)
        u_ref[...] = u.astype(BF16)

    out = jax.ShapeDtypeStruct((m, n), BF16)
    return pl.pallas_call(
        body, name=name, grid=(m // tm, n // tn), in_specs=[h_spec, wg_spec, wu_spec],
        out_specs=[o_spec] * 3, out_shape=[out] * 3,
        compiler_params=_params(dimension_semantics=("parallel", "parallel")),
    )(h, w_up, w_up)


def _mm_swiglu_bwd(dy, w_down, gate, up, *, alpha, tm, tn, name):
    m, k = dy.shape
    n = w_down.shape[0]
    dy_spec = pl.BlockSpec((tm, k), lambda i, j: (i, 0))
    w_spec = pl.BlockSpec((tn, k), lambda i, j: (j, 0))
    o_spec = pl.BlockSpec((tm, tn), lambda i, j: (i, j))

    def body(dy_ref, w_ref, g_ref, u_ref, dg_ref, du_ref):
        da = _dot(dy_ref[...], w_ref[...], _NT) * alpha
        g = g_ref[...].astype(F32)
        u = u_ref[...].astype(F32)
        sg = jax.nn.sigmoid(g)
        dg_ref[...] = (da * u * (sg * (1.0 + g * (1.0 - sg)))).astype(BF16)
        du_ref[...] = (da * (g * sg)).astype(BF16)

    out = jax.ShapeDtypeStruct((m, n), BF16)
    return pl.pallas_call(
        body, name=name, grid=(m // tm, n // tn), in_specs=[dy_spec, w_spec, o_spec, o_spec],
        out_specs=[o_spec] * 2, out_shape=[out] * 2,
        compiler_params=_params(dimension_semantics=("parallel", "parallel")),
    )(dy, w_down, gate, up)


def _rms_fwd(x, g, *, tt, name):
    t, d = x.shape

    def body(x_ref, g_ref, h_ref):
        xv = x_ref[...]
        rstd = lax.rsqrt(jnp.mean(xv * xv, axis=1, keepdims=True) + RMS_EPS)
        h_ref[...] = (xv * rstd * g_ref[...]).astype(BF16)

    return pl.pallas_call(
        body, name=name, grid=(t // tt,),
        in_specs=[pl.BlockSpec((tt, d), lambda i: (i, 0)), pl.BlockSpec((1, d), lambda i: (0, 0))],
        out_specs=pl.BlockSpec((tt, d), lambda i: (i, 0)), out_shape=jax.ShapeDtypeStruct((t, d), BF16),
        compiler_params=_params(dimension_semantics=("parallel",)),
    )(x, g)


def _rms_bwd(dh, x, g, dres, *, tt, name):
    t, d = x.shape

    def body(dh_ref, x_ref, g_ref, r_ref, dx_ref, dxb_ref, dg_ref):
        xv = x_ref[...]
        rstd = lax.rsqrt(jnp.mean(xv * xv, axis=1, keepdims=True) + RMS_EPS)
        xhat = xv * rstd
        dhv = dh_ref[...]
        dxhat = dhv * g_ref[...]
        dx = r_ref[...] + rstd * (dxhat - xhat * jnp.mean(dxhat * xhat, axis=1, keepdims=True))
        dx_ref[...] = dx
        dxb_ref[...] = dx.astype(BF16)

        @pl.when(pl.program_id(0) == 0)
        def _():
            dg_ref[...] = jnp.zeros_like(dg_ref)

        dg_ref[...] += jnp.sum(dhv * xhat, axis=0, keepdims=True)

    row = pl.BlockSpec((tt, d), lambda i: (i, 0))
    vec = pl.BlockSpec((1, d), lambda i: (0, 0))
    return pl.pallas_call(
        body, name=name, grid=(t // tt,), in_specs=[row, row, vec, row], out_specs=[row, row, vec],
        out_shape=[jax.ShapeDtypeStruct((t, d), F32), jax.ShapeDtypeStruct((t, d), BF16), jax.ShapeDtypeStruct((1, d), F32)],
        compiler_params=_params(dimension_semantics=("arbitrary",)),
    )(dh, x, g, dres)


def _final_loss(x, g, target, *, tt, name):
    t, d = x.shape

    def body(x_ref, g_ref, t_ref, dx_ref, dxb_ref, dg_ref, loss_ref):
        xv = x_ref[...]
        gv = g_ref[...]
        rstd = lax.rsqrt(jnp.mean(xv * xv, axis=1, keepdims=True) + RMS_EPS)
        xhat = xv * rstd
        err = xhat * gv - t_ref[...]
        dy = err * (1.0 / d)
        dxhat = dy * gv
        dx = rstd * (dxhat - xhat * jnp.mean(dxhat * xhat, axis=1, keepdims=True))
        dx_ref[...] = dx
        dxb_ref[...] = dx.astype(BF16)

        @pl.when(pl.program_id(0) == 0)
        def _():
            dg_ref[...] = jnp.zeros_like(dg_ref)
            loss_ref[...] = jnp.zeros_like(loss_ref)

        dg_ref[...] += jnp.sum(dy * xhat, axis=0, keepdims=True)
        part = 0.5 * jnp.sum(jnp.mean(err * err, axis=1, keepdims=True), axis=0, keepdims=True)
        loss_ref[...] += jnp.broadcast_to(part, loss_ref.shape)

    row = pl.BlockSpec((tt, d), lambda i: (i, 0))
    vec = pl.BlockSpec((1, d), lambda i: (0, 0))
    one = pl.BlockSpec((1, LANES), lambda i: (0, 0))
    return pl.pallas_call(
        body, name=name, grid=(t // tt,), in_specs=[row, vec, row], out_specs=[row, row, vec, one],
        out_shape=[jax.ShapeDtypeStruct((t, d), F32), jax.ShapeDtypeStruct((t, d), BF16), jax.ShapeDtypeStruct((1, d), F32),
                   jax.ShapeDtypeStruct((1, LANES), F32)],
        compiler_params=_params(dimension_semantics=("arbitrary",)),
    )(x, g, target)


def _swap_halves(x):
    lane = lax.broadcasted_iota(jnp.int32, x.shape, 1)
    return jnp.where((lane // 32) % 2 == 0, pltpu.roll(x, 96, 1), pltpu.roll(x, 32, 1))


def _rope_tables(s):
    half = HEAD_DIM // 2
    inv_freq = ROPE_THETA ** (-jnp.arange(half, dtype=F32) / half)
    ang = jnp.arange(s).astype(F32)[:, None] * inv_freq[None, :]
    cos, sin = jnp.cos(ang), jnp.sin(ang)
    return jnp.tile(cos, (1, 4)), jnp.concatenate([-sin, sin, -sin, sin], axis=1)


def _dilation_of_tile(p):
    dilated = p < 3 * DIL_HEADS // 2
    g = (p % (DIL_HEADS // 2)) // (DIL_GROUP_HEADS // 2)
    return [(dilated & (g == gi)) | (jnp.logical_not(dilated) if gi == 0 else False) for gi in range(len(DILATIONS))]


def _residue_major(ref, d):
    s = ref.shape[0]
    if d == 1:
        return ref[...]
    return jnp.concatenate([ref[pl.ds(r, s // d, stride=d), :] for r in range(d)], axis=0)


def _split_heads(proj, cos4, sin4, *, n_pairs, rot_pairs, scale_ranges, name):
    b, s, _ = proj.shape

    def body(x_ref, c_ref, s_ref, o_ref):
        p = pl.program_id(1)
        is_q = functools.reduce(jnp.logical_or, [(p >= lo) & (p < hi) for lo, hi in scale_ranges])
        scale = jnp.where(is_q, QK_SCALE, 1.0)

        def put(y):
            o_ref[0] = y[:, :HEAD_DIM].astype(BF16)
            o_ref[1] = y[:, HEAD_DIM:].astype(BF16)

        for d, in_group in zip(DILATIONS, _dilation_of_tile(p)):
            @pl.when(in_group & (p < rot_pairs))
            def _(d=d):
                x = _residue_major(x_ref, d)
                put((x * _residue_major(c_ref, d) + _swap_halves(x) * _residue_major(s_ref, d)) * scale)

            @pl.when(in_group & (p >= rot_pairs))
            def _(d=d):
                put(_residue_major(x_ref, d) * scale)

    tab = pl.BlockSpec((s, LANES), lambda bi, p: (0, 0))
    return pl.pallas_call(
        body, name=name, grid=(b, n_pairs),
        in_specs=[pl.BlockSpec((None, s, LANES), lambda bi, p: (bi, 0, p)), tab, tab],
        out_specs=pl.BlockSpec((None, 2, s, HEAD_DIM), lambda bi, p: (bi, p, 0, 0)),
        out_shape=jax.ShapeDtypeStruct((b, 2 * n_pairs, s, HEAD_DIM), BF16),
        compiler_params=_params(dimension_semantics=("parallel", "parallel")),
    )(proj, cos4, sin4)


def _merge_heads(dheads, cos4, sin4, *, heads_per_row, rot_pairs, scale_pairs, dilated, out_cols, tile_off, into, name):
    b, hpr, r, s, _ = dheads.shape
    n_pairs = hpr * r // 2
    ppr = hpr // 2

    def body(d_ref, c_ref, s_ref, *rest):
        o_ref, t_ref = rest[-2:]
        p = pl.program_id(1)
        scale = jnp.where(p < scale_pairs, QK_SCALE, 1.0)

        def tokens(d):
            dy = jnp.concatenate([d_ref[0], d_ref[1]], axis=1)
            if d == 1:
                return dy
            for res in range(d):
                t_ref[pl.ds(res, s // d, stride=d), :] = dy[res * (s // d):(res + 1) * (s // d), :]
            return t_ref[...]

        groups = _dilation_of_tile(p) if dilated else [p >= 0]
        for d, in_group in zip(DILATIONS, groups):
            @pl.when(in_group & (p < rot_pairs))
            def _(d=d):
                dy = tokens(d)
                o_ref[...] = ((dy * c_ref[...] - _swap_halves(dy) * s_ref[...]) * scale).astype(BF16)

            @pl.when(in_group & (p >= rot_pairs))
            def _(d=d):
                o_ref[...] = (tokens(d) * scale).astype(BF16)

    tab = pl.BlockSpec((s, LANES), lambda bi, p: (0, 0))
    operands = [dheads, cos4, sin4] + ([] if into is None else [into])
    return pl.pallas_call(
        body, name=name, grid=(b, n_pairs),
        in_specs=[pl.BlockSpec((None, 2, None, s, HEAD_DIM), lambda bi, p: (bi, p % ppr, p // ppr, 0, 0)), tab, tab]
        + ([] if into is None else [HBM]),
        out_specs=pl.BlockSpec((None, s, LANES), lambda bi, p: (bi, 0, p + tile_off)),
        out_shape=jax.ShapeDtypeStruct((b, s, out_cols), BF16),
        input_output_aliases={} if into is None else {3: 0},
        scratch_shapes=[pltpu.VMEM((s, LANES), F32)],
        compiler_params=_params(dimension_semantics=("parallel", "parallel")),
    )(*operands)


DIL_TQ = 256


def _dil_block(g, s):
    run = s // DILATIONS[g]
    return DIL_TQ if run <= DIL_TQ else min(run, DIL_TQ + 2 * LANES)


def _dil_keys(g, q0, s):
    run = max(s // DILATIONS[g], DIL_TQ)
    lo = (q0 // run) * run
    return pl.multiple_of(jnp.clip(q0 - LANES, lo, lo + run - _dil_block(g, s)), LANES)


def _dil_band(g, q0, start, shape, s):
    row = q0 + lax.broadcasted_iota(jnp.int32, shape, 0)
    col = start + lax.broadcasted_iota(jnp.int32, shape, 1)
    ok = jnp.abs(row - col) <= DIL_HALF
    run = s // DILATIONS[g]
    if run < DIL_TQ:
        shift = run.bit_length() - 1
        ok = ok & ((row >> shift) == (col >> shift))
    return ok


def _dil_tokens(g, q0, s):
    d = DILATIONS[g]
    if d == 1:
        return [(0, DIL_TQ, pl.ds(q0, DIL_TQ))]
    run = s // d
    n = min(run, DIL_TQ)
    return [(lo, n, pl.ds(((q0 + lo) % run) * d + (q0 + lo) // run, n, stride=d)) for lo in range(0, DIL_TQ, n)]


def _dil_gather(ref, pieces):
    return jnp.concatenate([ref[rows, :] for _, _, rows in pieces], axis=0) if len(pieces) > 1 else ref[pieces[0][2], :]


def _dil_head_spec(part, g, s):
    return pl.BlockSpec((None, None, s, HEAD_DIM), lambda b, j: (b, part * DIL_HEADS + g * DIL_GROUP_HEADS + j, 0, 0))


def _dil_attn_fwd(heads, *, name):
    b, _, s, _ = heads.shape
    n_g = len(DILATIONS)

    def body(*refs):
        qkv = refs[:3 * n_g]
        o_ref, l_ref, og_ref, lg_ref = refs[3 * n_g:]
        for g in range(n_g):
            q_ref, k_ref, v_ref = qkv[3 * g:3 * g + 3]
            width = _dil_block(g, s)

            def step(i, carry, g=g, q_ref=q_ref, k_ref=k_ref, v_ref=v_ref, width=width):
                q0 = pl.multiple_of(i * DIL_TQ, DIL_TQ)
                start = _dil_keys(g, q0, s)
                sc = _dot(q_ref[pl.ds(q0, DIL_TQ), :], k_ref[pl.ds(start, width), :], _NT)
                sc = jnp.where(_dil_band(g, q0, start, sc.shape, s), sc, NEG_INF)
                m = jnp.max(sc, axis=1, keepdims=True)
                p = jnp.exp(sc - m)
                den = jnp.sum(p, axis=1, keepdims=True)
                o = _dot(p.astype(BF16), v_ref[pl.ds(start, width), :], _NN) / den
                lse = m + jnp.log(den)
                for lo, n, rows in _dil_tokens(g, q0, s):
                    og_ref[g, rows, :] = o[lo:lo + n]
                    lg_ref[g, rows, :] = lse[lo:lo + n]
                return carry

            lax.fori_loop(0, s // DIL_TQ, step, 0)
        lses = [lg_ref[g] for g in range(n_g)]
        m = functools.reduce(jnp.maximum, lses)
        ws = [jnp.exp(l - m) for l in lses]
        den = functools.reduce(jnp.add, ws)
        o_ref[...] = (functools.reduce(jnp.add, [w * og_ref[g] for g, w in enumerate(ws)]) / den).astype(o_ref.dtype)
        l_ref[...] = m + jnp.log(den)

    out = pl.BlockSpec((None, None, s, HEAD_DIM), lambda bi, j: (bi, j, 0, 0))
    lse = pl.BlockSpec((None, None, s, 1), lambda bi, j: (bi, j, 0, 0))
    return pl.pallas_call(
        body, name=name, grid=(b, DIL_GROUP_HEADS),
        in_specs=[_dil_head_spec(part, g, s) for g in range(n_g) for part in range(3)],
        out_specs=[out, lse],
        out_shape=[jax.ShapeDtypeStruct((b, DIL_GROUP_HEADS, s, HEAD_DIM), BF16),
                   jax.ShapeDtypeStruct((b, DIL_GROUP_HEADS, s, 1), F32)],
        scratch_shapes=[pltpu.VMEM((n_g, s, HEAD_DIM), F32), pltpu.VMEM((n_g, s, 1), F32)],
        compiler_params=_params(dimension_semantics=("parallel", "parallel")),
    )(*([heads] * (3 * n_g)))


def _dil_attn_bwd(heads, out, lse, dout, *, name):
    b, _, s, _ = heads.shape
    n_g = len(DILATIONS)

    def body(*refs):
        qkv = refs[:3 * n_g]
        o_ref, l_ref, do_ref, d_ref, delta_ref = refs[3 * n_g:]
        d_ref[...] = jnp.zeros_like(d_ref)
        delta_ref[...] = jnp.sum(do_ref[...] * o_ref[...].astype(F32), axis=1, keepdims=True)
        for g in range(n_g):
            q_ref, k_ref, v_ref = qkv[3 * g:3 * g + 3]
            width = _dil_block(g, s)

            def step(i, carry, g=g, q_ref=q_ref, k_ref=k_ref, v_ref=v_ref, width=width):
                q0 = pl.multiple_of(i * DIL_TQ, DIL_TQ)
                start = _dil_keys(g, q0, s)
                win = pl.ds(start, width)
                pieces = _dil_tokens(g, q0, s)
                do_b = _dil_gather(do_ref, pieces).astype(BF16)
                q, k, v = q_ref[pl.ds(q0, DIL_TQ), :], k_ref[win, :], v_ref[win, :]
                sc = _dot(q, k, _NT)
                p = jnp.where(_dil_band(g, q0, start, sc.shape, s), jnp.exp(sc - _dil_gather(l_ref, pieces)), 0.0)
                ds = (p * (_dot(do_b, v, _NT) - _dil_gather(delta_ref, pieces))).astype(BF16)
                d_ref[g, pl.ds(q0, DIL_TQ), :] = _dot(ds, k, _NN)
                d_ref[n_g + g, win, :] += _dot(ds, q, _TN)
                d_ref[2 * n_g + g, win, :] += _dot(p.astype(BF16), do_b, _TN)
                return carry

            lax.fori_loop(0, s // DIL_TQ, step, 0)

    per_head = lambda bi, j: (bi, j, 0, 0)
    return pl.pallas_call(
        body, name=name, grid=(b, DIL_GROUP_HEADS),
        in_specs=[_dil_head_spec(part, g, s) for g in range(n_g) for part in range(3)]
        + [pl.BlockSpec((None, None, s, HEAD_DIM), per_head), pl.BlockSpec((None, None, s, 1), per_head),
           pl.BlockSpec((None, None, s, HEAD_DIM), per_head)],
        out_specs=pl.BlockSpec((None, None, 3 * n_g, s, HEAD_DIM), lambda bi, j: (bi, j, 0, 0, 0)),
        out_shape=jax.ShapeDtypeStruct((b, DIL_GROUP_HEADS, 3 * n_g, s, HEAD_DIM), F32),
        scratch_shapes=[pltpu.VMEM((s, 1), F32)],
        compiler_params=_params(dimension_semantics=("parallel", "parallel")),
    )(*([heads] * (3 * n_g)), out, lse, dout)


NA_BIAS_ROWS = 2 * NA_ROWS - 1
NA_BIAS_COLS = 2 * NA_COLS - 1
NA_BLOCK = 4
NA_SPAN = NA_ROWS + NA_BLOCK - 1
NA_Q = NA_BLOCK * GRID_W
NA_KEYS = NA_SPAN * GRID_W
NA_FORMS = 3


def _na_onehot():
    c = np.arange(GRID_W)[:, None]
    k = np.arange(GRID_W)[None, :]
    lo = np.clip(c - NA_COLS // 2, 0, GRID_W - NA_COLS)
    valid = (k >= lo) & (k < lo + NA_COLS)
    onehot = np.zeros((GRID_W, GRID_W, LANES), np.float32)
    cc, kk = np.nonzero(valid)
    onehot[cc, kk, kk - cc + NA_COLS - 1] = 1.0
    return onehot.reshape(GRID_W * GRID_W, LANES), valid.reshape(1, GRID_W * GRID_W)


def _na_block_rows(n_rows):
    table = np.full((NA_FORMS, NA_BLOCK, NA_SPAN), NA_BIAS_ROWS, np.int64)
    n_blocks = n_rows // NA_BLOCK
    for form, ib in enumerate((0, 1, n_blocks - 1)):
        base = min(max(NA_BLOCK * ib - NA_ROWS // 2, 0), n_rows - NA_SPAN)
        for rl in range(NA_BLOCK):
            r = NA_BLOCK * ib + rl
            row_lo = min(max(r - NA_ROWS // 2, 0), n_rows - NA_ROWS)
            for kl in range(NA_SPAN):
                if row_lo <= base + kl < row_lo + NA_ROWS:
                    table[form, rl, kl] = base + kl - r + NA_ROWS - 1
    return table


def _na_block(ib, n_rows):
    n_blocks = n_rows // NA_BLOCK
    base = jnp.clip(NA_BLOCK * ib - NA_ROWS // 2, 0, n_rows - NA_SPAN)
    return base, jnp.where(ib == 0, 0, jnp.where(ib == n_blocks - 1, 2, 1))


def _na_expand_bias(rel_bias, *, name):
    l, h, nr, nc = rel_bias.shape
    onehot, valid = _na_onehot()
    rb = jnp.pad(rel_bias, ((0, 0), (0, 0), (0, 1), (0, LANES - nc))).reshape(l * h * (nr + 1), LANES)
    live = jnp.asarray(np.tile(np.arange(nr + 1) < nr, l * h).astype(np.float32)[:, None])

    def body(rb_ref, oh_ref, valid_ref, live_ref, e_ref):
        e = lax.dot_general(rb_ref[...], oh_ref[...], _NT, precision=lax.Precision.HIGHEST, preferred_element_type=F32)
        e_ref[...] = jnp.where((valid_ref[...] > 0) & (live_ref[...] > 0), e, NEG_INF)

    e = pl.pallas_call(
        body, name=name, out_shape=jax.ShapeDtypeStruct((l * h * (nr + 1), GRID_W * GRID_W), F32), compiler_params=_params(),
    )(rb, jnp.asarray(onehot), jnp.asarray(valid.astype(np.float32)), live)
    return e.reshape(l, h, nr + 1, GRID_W, GRID_W)


def _na_collapse_bias(de, *, name):
    b, h = de.shape[:2]
    onehot, _ = _na_onehot()
    rows = h * NA_BIAS_ROWS

    def diag(e_ref, oh_ref, o_ref):
        e = e_ref[0]
        for bi in range(1, b):
            e = e + e_ref[bi]
        o_ref[...] = lax.dot_general(e, oh_ref[...], _NN, precision=lax.Precision.HIGHEST, preferred_element_type=F32)

    drb = pl.pallas_call(
        diag, name=name, out_shape=jax.ShapeDtypeStruct((rows, LANES), F32), compiler_params=_params(),
    )(de.reshape(b, rows, GRID_W * GRID_W), jnp.asarray(onehot))
    return drb[:, :NA_BIAS_COLS].reshape(h, NA_BIAS_ROWS, NA_BIAS_COLS)


def _na_tiles(n_rows):
    table = _na_block_rows(n_rows)
    return [(f, rl, kl, int(table[f, rl, kl])) for f in range(NA_FORMS) for rl in range(NA_BLOCK) for kl in range(NA_SPAN)]


def _na_tile(ref, form, rl, kl):
    return ref.at[form, rl * GRID_W:(rl + 1) * GRID_W, kl * GRID_W:(kl + 1) * GRID_W]


def _na_head_spec(part, first, s):
    return pl.BlockSpec((None, None, s, HEAD_DIM), lambda b, h: (b, first + part * NA_HEADS + h, 0, 0))


def _na_attn_fwd(heads, bias, *, first, name):
    b, _, s, _ = heads.shape
    n_rows = s // GRID_W
    tiles = _na_tiles(n_rows)

    def body(q_ref, k_ref, v_ref, e_ref, o_ref, l_ref, b_ref):
        for form, rl, kl, i in tiles:
            _na_tile(b_ref, form, rl, kl)[...] = e_ref[i]

        def step(ib, carry):
            base, form = _na_block(ib, n_rows)
            rows = pl.ds(pl.multiple_of(ib * NA_Q, NA_Q), NA_Q)
            win = pl.ds(pl.multiple_of(base * GRID_W, GRID_W), NA_KEYS)
            sc = _dot(q_ref[rows, :], k_ref[win, :], _NT) + b_ref[form]
            m = jnp.max(sc, axis=1, keepdims=True)
            p = jnp.exp(sc - m)
            den = jnp.sum(p, axis=1, keepdims=True)
            o_ref[rows, :] = (_dot(p.astype(BF16), v_ref[win, :], _NN) / den).astype(o_ref.dtype)
            l_ref[rows, :] = m + jnp.log(den)
            return carry

        lax.fori_loop(0, n_rows // NA_BLOCK, step, 0)

    per_head = lambda bi, h: (bi, h, 0, 0)
    return pl.pallas_call(
        body, name=name, grid=(b, NA_HEADS),
        in_specs=[_na_head_spec(part, first, s) for part in range(3)]
        + [pl.BlockSpec((None, NA_BIAS_ROWS + 1, GRID_W, GRID_W), lambda bi, h: (h, 0, 0, 0))],
        out_specs=[pl.BlockSpec((None, None, s, HEAD_DIM), per_head), pl.BlockSpec((None, None, s, 1), per_head)],
        out_shape=[jax.ShapeDtypeStruct((b, NA_HEADS, s, HEAD_DIM), BF16), jax.ShapeDtypeStruct((b, NA_HEADS, s, 1), F32)],
        scratch_shapes=[pltpu.VMEM((NA_FORMS, NA_Q, NA_KEYS), F32)],
        compiler_params=_params(dimension_semantics=("parallel", "parallel")),
    )(heads, heads, heads, bias)


def _na_attn_bwd(heads, bias, out, lse, dout, *, first, name):
    b, _, s, _ = heads.shape
    n_rows = s // GRID_W
    tiles = _na_tiles(n_rows)

    def body(q_ref, k_ref, v_ref, e_ref, o_ref, l_ref, do_ref, d_ref, de_ref, b_ref, db_ref):
        for form, rl, kl, i in tiles:
            _na_tile(b_ref, form, rl, kl)[...] = e_ref[i]
        d_ref[...] = jnp.zeros_like(d_ref)
        db_ref[...] = jnp.zeros_like(db_ref)

        def step(ib, carry):
            base, form = _na_block(ib, n_rows)
            rows = pl.ds(pl.multiple_of(ib * NA_Q, NA_Q), NA_Q)
            win = pl.ds(pl.multiple_of(base * GRID_W, GRID_W), NA_KEYS)
            q, k, v = q_ref[rows, :], k_ref[win, :], v_ref[win, :]
            do = do_ref[rows, :]
            delta = jnp.sum(do * o_ref[rows, :].astype(F32), axis=1, keepdims=True)
            do_b = do.astype(BF16)
            p = jnp.exp(_dot(q, k, _NT) + b_ref[form] - l_ref[rows, :])
            ds = p * (_dot(do_b, v, _NT) - delta)
            db_ref[form] += ds
            ds_b = ds.astype(BF16)
            d_ref[0, rows, :] = _dot(ds_b, k, _NN)
            d_ref[1, win, :] += _dot(ds_b, q, _TN)
            d_ref[2, win, :] += _dot(p.astype(BF16), do_b, _TN)
            return carry

        lax.fori_loop(0, n_rows // NA_BLOCK, step, 0)
        acc = [None] * NA_BIAS_ROWS
        for form, rl, kl, i in tiles:
            if i < NA_BIAS_ROWS:
                t = _na_tile(db_ref, form, rl, kl)[...]
                acc[i] = t if acc[i] is None else acc[i] + t
        for i in range(NA_BIAS_ROWS):
            de_ref[i] = acc[i]

    per_head = lambda bi, h: (bi, h, 0, 0)
    return pl.pallas_call(
        body, name=name, grid=(b, NA_HEADS),
        in_specs=[_na_head_spec(part, first, s) for part in range(3)]
        + [pl.BlockSpec((None, NA_BIAS_ROWS + 1, GRID_W, GRID_W), lambda bi, h: (h, 0, 0, 0)),
           pl.BlockSpec((None, None, s, HEAD_DIM), per_head), pl.BlockSpec((None, None, s, 1), per_head),
           pl.BlockSpec((None, None, s, HEAD_DIM), per_head)],
        out_specs=[pl.BlockSpec((None, None, 3, s, HEAD_DIM), lambda bi, h: (bi, h, 0, 0, 0)),
                   pl.BlockSpec((None, None, NA_BIAS_ROWS, GRID_W, GRID_W), lambda bi, h: (bi, h, 0, 0, 0))],
        out_shape=[jax.ShapeDtypeStruct((b, NA_HEADS, 3, s, HEAD_DIM), F32),
                   jax.ShapeDtypeStruct((b, NA_HEADS, NA_BIAS_ROWS, GRID_W, GRID_W), F32)],
        scratch_shapes=[pltpu.VMEM((NA_FORMS, NA_Q, NA_KEYS), F32), pltpu.VMEM((NA_FORMS, NA_Q, NA_KEYS), F32)],
        compiler_params=_params(dimension_semantics=("parallel", "parallel")),
    )(heads, heads, heads, bias, out, lse, dout)


GATE_TILE = 256


def _gate_fwd(proj, z, *, gate_col, tt, name):
    _, t, d = z.shape
    nj = d // GATE_TILE
    c0 = gate_col // GATE_TILE

    def body(ga_ref, gb_ref, za_ref, zb_ref, o_ref):
        o_ref[...] = (jax.nn.sigmoid(ga_ref[...]) * za_ref[...] + jax.nn.sigmoid(gb_ref[...]) * zb_ref[...]).astype(BF16)

    return pl.pallas_call(
        body, name=name, grid=(t // tt, nj),
        in_specs=[pl.BlockSpec((tt, GATE_TILE), lambda i, j: (i, c0 + j)),
                  pl.BlockSpec((tt, GATE_TILE), lambda i, j: (i, c0 + nj + j)),
                  pl.BlockSpec((None, tt, GATE_TILE), lambda i, j: (0, i, j)),
                  pl.BlockSpec((None, tt, GATE_TILE), lambda i, j: (1, i, j))],
        out_specs=pl.BlockSpec((tt, GATE_TILE), lambda i, j: (i, j)), out_shape=jax.ShapeDtypeStruct((t, d), BF16),
        compiler_params=_params(dimension_semantics=("parallel", "parallel")),
    )(proj, proj, z, z)


def _gate_bwd(dm, proj, z, *, gate_col, tt, name):
    _, t, d = z.shape
    nj = d // GATE_TILE
    c0 = gate_col // GATE_TILE

    def body(dm_ref, g_ref, z_ref, dz_ref, dg_ref):
        dmv = dm_ref[...]
        sg = jax.nn.sigmoid(g_ref[...])
        dz_ref[...] = (dmv * sg).astype(BF16)
        dg_ref[...] = (dmv * z_ref[...] * sg * (1.0 - sg)).astype(BF16)

    return pl.pallas_call(
        body, name=name, grid=(t // tt, 2 * nj),
        in_specs=[pl.BlockSpec((tt, GATE_TILE), lambda i, j: (i, j % nj)),
                  pl.BlockSpec((tt, GATE_TILE), lambda i, j: (i, c0 + j)),
                  pl.BlockSpec((None, tt, GATE_TILE), lambda i, j: (j // nj, i, j % nj))],
        out_specs=[pl.BlockSpec((None, tt, GATE_TILE), lambda i, j: (j // nj, i, j % nj)),
                   pl.BlockSpec((tt, GATE_TILE), lambda i, j: (i, c0 + j))],
        out_shape=[jax.ShapeDtypeStruct((2, t, d), BF16), jax.ShapeDtypeStruct(proj.shape, BF16)],
        compiler_params=_params(dimension_semantics=("parallel", "parallel")),
    )(dm, proj, z)


def _adamw(w, g, m, v, *, name):
    shape = w.shape
    w2, g2, m2, v2 = (t.reshape(-1, shape[-1]) for t in (w, g, m, v))
    rows, cols = w2.shape
    tr = rows
    for cand in (512, 256, 128, 64, 32, 16, 8):
        if rows % cand == 0:
            tr = cand
            break

    def body(w_ref, g_ref, m_ref, v_ref, d_ref, nm_ref, nv_ref):
        gv = g_ref[...]
        nm = ADAM_B1 * m_ref[...] + (1.0 - ADAM_B1) * gv
        nv = ADAM_B2 * v_ref[...] + (1.0 - ADAM_B2) * (gv * gv)
        m_hat = nm / (1.0 - ADAM_B1 ** ADAM_STEP)
        v_hat = nv / (1.0 - ADAM_B2 ** ADAM_STEP)
        d_ref[...] = -ADAM_LR * (m_hat / (jnp.sqrt(v_hat) + ADAM_EPS) + ADAM_WD * w_ref[...])
        nm_ref[...] = nm
        nv_ref[...] = nv

    blk = pl.BlockSpec((tr, cols), lambda i: (i, 0))
    out = jax.ShapeDtypeStruct((rows, cols), F32)
    res = pl.pallas_call(
        body, name=name, grid=(rows // tr,), in_specs=[blk] * 4, out_specs=[blk] * 3, out_shape=[out] * 3,
        compiler_params=_params(dimension_semantics=("parallel",)),
    )(w2, g2, m2, v2)
    return tuple(t.reshape(shape) for t in res)


def _my_place():
    return lax.axis_index("x"), lax.axis_index("y"), lax.axis_index("c")


def _other_chips(x, y):
    return [(1 - x, y), (x, 1 - y), (1 - x, 1 - y)]


def _chip_no(chip):
    return 2 * chip[0] + chip[1]


def _window(ref, kind, size, chip, lead):
    if kind == "col":
        return ref.at[(*lead, slice(None), pl.ds(pl.multiple_of(chip * size, LANES), size))]
    if kind == "row":
        return ref.at[(*lead, pl.ds(pl.multiple_of(chip * size, BF16_ROWS), size), slice(None))]
    shard = size + HEAD_DIM
    if kind == "win_main":
        return ref.at[(*lead, slice(None), pl.ds(pl.multiple_of(chip * shard + HEAD_DIM * (chip % 2), LANES), size))]
    assert kind == "win_strad"
    return ref.at[(*lead, slice(None), pl.ds(pl.multiple_of(size + 2 * shard * (chip // 2), LANES), LANES))]


def _full_shape(shard, kind):
    _, k, n = shard.shape
    return {"col": (k, N_CHIPS * n), "row": (N_CHIPS * k, n), "win_main": (k, N_CHIPS * (n + HEAD_DIM)),
            "slot": (N_CHIPS, k, n)}[kind]


def _place_own(shard, kind, layer, *, name):
    _, k, n = shard.shape
    tr = _div_tile(k, 512, BF16_ROWS)
    tc = LANES if kind == "win_main" else n
    mine = 2 * lax.axis_index("x") + lax.axis_index("y")
    row0 = mine * (k // tr) if kind == "row" else 0
    col0 = {"col": mine, "row": 0, "slot": 0, "win_main": (mine * (n + HEAD_DIM) + HEAD_DIM * (mine % 2)) // LANES}[kind]
    scalars = jnp.stack([mine, row0, col0]).astype(jnp.int32)

    def body(s_ref, i_ref, o_ref):
        o_ref[...] = i_ref[...]

    if kind == "slot":
        o_spec = pl.BlockSpec((None, tr, tc), lambda i, j, s: (s[0], i, j))
    else:
        o_spec = pl.BlockSpec((tr, tc), lambda i, j, s: (s[1] + i, s[2] + j))
    return pl.pallas_call(
        body, name=name,
        grid_spec=pltpu.PrefetchScalarGridSpec(
            num_scalar_prefetch=1, grid=(k // tr, n // tc),
            in_specs=[pl.BlockSpec((None, tr, tc), lambda i, j, s: (layer, i, j))], out_specs=o_spec),
        out_shape=jax.ShapeDtypeStruct(_full_shape(shard, kind), shard.dtype),
        compiler_params=_params(dimension_semantics=("parallel", "parallel")),
    )(scalars, shard)


class _GatherPlan:
    def __init__(self, src, dst, shapes, kinds, layer, send_sems, recv_sems):
        self.src, self.dst, self.shapes, self.kinds, self.layer = src, dst, shapes, kinds, layer
        self.send_sems, self.recv_sems = send_sems, recv_sems
        self.x, self.y, self.c = _my_place()
        self.mine = 2 * self.x + self.y
        self.chips = _other_chips(self.x, self.y)
        self.n = len(src)

    def half(self, i, chip, half):
        _, k, n = self.shapes[i]
        kind, dst, hk = self.kinds[i], self.dst[i], k // 2
        if kind == "slot":
            return dst.at[chip, pl.ds(pl.multiple_of(half * hk, BF16_ROWS), hk), :]
        if kind == "row":
            return dst.at[pl.ds(pl.multiple_of(chip * k + half * hk, BF16_ROWS), hk), :]
        col0 = chip * n if kind == "col" else chip * (n + HEAD_DIM) + HEAD_DIM * (chip % 2)
        return dst.at[pl.ds(pl.multiple_of(half * hk, BF16_ROWS), hk), pl.ds(pl.multiple_of(col0, LANES), n)]

    def _copy(self, sem, window, to, source=None):
        return pltpu.make_async_remote_copy(src_ref=window if source is None else source, dst_ref=window,
                                            send_sem=self.send_sems.at[sem], recv_sem=self.recv_sems.at[sem],
                                            device_id=to, device_id_type=MESH)

    def sends(self):
        out = []
        for k, chip in enumerate(self.chips):
            for i in range(self.n):
                hk = self.shapes[i][1] // 2
                mine = self.src[i].at[self.layer, pl.ds(pl.multiple_of(self.c * hk, BF16_ROWS), hk), :]
                out.append(self._copy(3 * i + k, self.half(i, self.mine, self.c), (*chip, self.c), source=mine))
        return out

    def arrivals(self):
        return [self._copy(3 * i + k, self.half(i, _chip_no(chip), self.c), (*chip, self.c))
                for k, chip in enumerate(self.chips) for i in range(self.n)]

    def forwards(self, first_sem):
        sibling = (self.x, self.y, 1 - self.c)
        return [self._copy(first_sem + 3 * i + k, self.half(i, _chip_no(chip), self.c), sibling)
                for k, chip in enumerate(self.chips) for i in range(self.n)]

    def forwarded(self, first_sem):
        sibling = (self.x, self.y, 1 - self.c)
        return [self._copy(first_sem + 3 * i + k, self.half(i, _chip_no(chip), 1 - self.c), sibling)
                for k, chip in enumerate(self.chips) for i in range(self.n)]


IN_HBM = pl.BlockSpec(memory_space=pltpu.HBM)
IN_SEM = pl.BlockSpec(memory_space=pltpu.SEMAPHORE)
DATAFLOW = pltpu.SideEffectType.DATAFLOW_SIDE_EFFECTING


def _gather_layer_start(shards, kinds, fulls, layer, after, *, name):
    n_w = len(shards)
    shapes = [sh.shape for sh in shards]

    def body(*refs):
        plan = _GatherPlan(refs[:n_w], refs[n_w:2 * n_w], shapes, kinds, layer, refs[2 * n_w + 1], refs[2 * n_w + 2])
        for cp in plan.sends():
            cp.start()
        token = refs[-1]
        token[...] = jnp.zeros_like(token)

    operands = [pltpu.with_memory_space_constraint(a, pltpu.HBM) for a in (*shards, *fulls)]
    res = pl.pallas_call(
        body, name=name, in_specs=[IN_HBM] * (2 * n_w) + [pl.BlockSpec(memory_space=pl.ANY)],
        out_specs=(IN_SEM, IN_SEM, *([IN_HBM] * (2 * n_w)), pl.BlockSpec(memory_space=pltpu.VMEM)),
        out_shape=(pltpu.SemaphoreType.DMA((3 * n_w,)), pltpu.SemaphoreType.DMA((3 * n_w,)),
                   *[pltpu.HBM(a.shape, a.dtype) for a in operands], jax.ShapeDtypeStruct((8, LANES), F32)),
        input_output_aliases={i: 2 + i for i in range(2 * n_w)},
        compiler_params=pltpu.CompilerParams(has_side_effects=DATAFLOW),
    )(*operands, after)
    return res[0], res[1], res[2:2 + n_w], res[2 + n_w:2 + 2 * n_w], res[-1]


def _gather_layer_wait(send_sems, recv_sems, shards, fulls, kinds, layer, after, *, name):
    n_w = len(shards)
    shapes = [sh.shape for sh in shards]

    def body(*refs):
        plan = _GatherPlan(refs[:n_w], refs[n_w:2 * n_w], shapes, kinds, layer, refs[2 * n_w], refs[2 * n_w + 1])
        for cp in plan.sends():
            cp.wait_send()
        for cp in plan.arrivals():
            cp.wait_recv()

    res = pl.pallas_call(
        body, name=name, in_specs=[IN_HBM] * (2 * n_w) + [IN_SEM, IN_SEM, pl.BlockSpec(memory_space=pl.ANY)],
        out_specs=[IN_HBM] * (2 * n_w), out_shape=[pltpu.HBM(a.shape, a.dtype) for a in (*shards, *fulls)],
        input_output_aliases={i: i for i in range(2 * n_w)},
        compiler_params=pltpu.CompilerParams(has_side_effects=DATAFLOW),
    )(*shards, *fulls, send_sems, recv_sems, after)
    return res[n_w:]


def _gather_layer_forward(shapes, kinds, fulls, *, name):
    n_w = len(fulls)

    def body(*refs):
        plan = _GatherPlan([None] * n_w, refs[n_w:2 * n_w], shapes, kinds, 0, *refs[2 * n_w:])
        passed = plan.forwards(0)
        for cp in passed:
            cp.start()
        for cp in plan.forwarded(0):
            cp.wait_recv()
        for cp in passed:
            cp.wait_send()

    return pl.pallas_call(
        body, name=name, in_specs=[HBM] * n_w, out_specs=[HBM] * n_w,
        out_shape=[jax.ShapeDtypeStruct(f.shape, f.dtype) for f in fulls],
        input_output_aliases={i: i for i in range(n_w)},
        scratch_shapes=[pltpu.SemaphoreType.DMA((3 * n_w,)), pltpu.SemaphoreType.DMA((3 * n_w,))],
    )(*fulls)


def _grads_to_sibling(grads, layer, *, name):
    n_w = len(grads)

    def body(*refs):
        src, dst = refs[:n_w], refs[n_w:2 * n_w]
        send_sems, recv_sems = refs[2 * n_w:]
        x, y, c = _my_place()
        cps = [pltpu.make_async_remote_copy(src_ref=src[i], dst_ref=dst[i], send_sem=send_sems.at[i],
                                            recv_sem=recv_sems.at[i], device_id=(x, y, layer), device_id_type=MESH)
               for i in range(n_w)]

        @pl.when(c != layer)
        def _():
            for cp in cps:
                cp.start()
            for cp in cps:
                cp.wait_send()

        @pl.when(c == layer)
        def _():
            for cp in cps:
                cp.wait_recv()

    return pl.pallas_call(
        body, name=name, in_specs=[HBM] * n_w, out_specs=[HBM] * n_w,
        out_shape=[jax.ShapeDtypeStruct(g.shape, g.dtype) for g in grads],
        scratch_shapes=[pltpu.SemaphoreType.DMA((n_w,)), pltpu.SemaphoreType.DMA((n_w,))],
    )(*grads)


def _on_core(layer):
    return (lax.axis_index("c") == layer).astype(jnp.int32).reshape(1)


def _pair_add(mine, other, layer, *, name):
    k, n = mine.shape
    tr = _div_tile(k, 512, BF16_ROWS)

    def body(on_ref, a_ref, b_ref, o_ref):
        @pl.when(on_ref[0] == 1)
        def _():
            o_ref[...] = (a_ref[...].astype(F32) + b_ref[...].astype(F32)).astype(o_ref.dtype)

    blk = pl.BlockSpec((tr, n), lambda i, on: (i * on[0], 0))
    return pl.pallas_call(
        body, name=name,
        grid_spec=pltpu.PrefetchScalarGridSpec(num_scalar_prefetch=1, grid=(k // tr,), in_specs=[blk, blk], out_specs=blk),
        out_shape=jax.ShapeDtypeStruct((k, n), mine.dtype), compiler_params=_params(dimension_semantics=("arbitrary",)),
    )(_on_core(layer), mine, other)


class _ScatterPlan:
    def __init__(self, src, dst, kinds, sizes, layer, send_sems, recv_sems):
        self.src, self.dst, self.kinds, self.sizes, self.layer = src, dst, kinds, sizes, layer
        self.send_sems, self.recv_sems = send_sems, recv_sems
        self.x, self.y, self.c = _my_place()
        self.mine = 2 * self.x + self.y
        self.chips = _other_chips(self.x, self.y)
        self.n = len(src)

    def _copy(self, i, k, chip, window_of, slab):
        return pltpu.make_async_remote_copy(src_ref=_window(self.src[i], self.kinds[i], self.sizes[i], window_of, ()),
                                            dst_ref=self.dst[i].at[slab], send_sem=self.send_sems.at[3 * i + k],
                                            recv_sem=self.recv_sems.at[3 * i + k], device_id=(*chip, self.layer),
                                            device_id_type=MESH)

    def sends(self):
        return [self._copy(i, k, chip, _chip_no(chip), self.mine) for k, chip in enumerate(self.chips) for i in range(self.n)]

    def arrivals(self):
        return [self._copy(i, k, chip, self.mine, _chip_no(chip)) for k, chip in enumerate(self.chips) for i in range(self.n)]


def _slab_shape(p, kind, size):
    return (N_CHIPS,) + {"col": (p.shape[0], size), "row": (size, p.shape[1]), "win_main": (p.shape[0], size),
                         "win_strad": (p.shape[0], LANES)}[kind]


def _grads_to_chips_start(pairs, kinds, sizes, layer, *, name):
    n_w = len(pairs)

    def body(*refs):
        plan = _ScatterPlan(refs[:n_w], refs[n_w:2 * n_w], kinds, sizes, layer, refs[2 * n_w], refs[2 * n_w + 1])

        @pl.when(plan.c == layer)
        def _():
            for cp in plan.sends():
                cp.start()

        token = refs[-1]
        token[...] = jnp.zeros_like(token)

    slabs = [lax.empty(_slab_shape(p, kind, size), p.dtype) for p, kind, size in zip(pairs, kinds, sizes)]
    operands = [pltpu.with_memory_space_constraint(a, pltpu.HBM) for a in (*pairs, *slabs)]
    res = pl.pallas_call(
        body, name=name, in_specs=[IN_HBM] * (2 * n_w),
        out_specs=(IN_SEM, IN_SEM, *([IN_HBM] * (2 * n_w)), pl.BlockSpec(memory_space=pltpu.VMEM)),
        out_shape=(pltpu.SemaphoreType.DMA((3 * n_w,)), pltpu.SemaphoreType.DMA((3 * n_w,)),
                   *[pltpu.HBM(a.shape, a.dtype) for a in operands], jax.ShapeDtypeStruct((8, LANES), F32)),
        input_output_aliases={i: 2 + i for i in range(2 * n_w)},
        compiler_params=pltpu.CompilerParams(has_side_effects=DATAFLOW),
    )(*operands)
    return res[0], res[1], res[2:2 + n_w], res[2 + n_w:2 + 2 * n_w], res[-1]


def _grads_to_chips_wait(send_sems, recv_sems, pairs, slabs, kinds, sizes, layer, after, *, name):
    n_w = len(pairs)

    def body(*refs):
        plan = _ScatterPlan(refs[:n_w], refs[n_w:2 * n_w], kinds, sizes, layer, refs[2 * n_w], refs[2 * n_w + 1])

        @pl.when(plan.c == layer)
        def _():
            for cp in plan.sends():
                cp.wait_send()
            for cp in plan.arrivals():
                cp.wait_recv()

    res = pl.pallas_call(
        body, name=name, in_specs=[IN_HBM] * (2 * n_w) + [IN_SEM, IN_SEM, pl.BlockSpec(memory_space=pl.ANY)],
        out_specs=[IN_HBM] * (2 * n_w), out_shape=[pltpu.HBM(a.shape, a.dtype) for a in (*pairs, *slabs)],
        input_output_aliases={i: i for i in range(2 * n_w)},
        compiler_params=pltpu.CompilerParams(has_side_effects=DATAFLOW),
    )(*pairs, *slabs, send_sems, recv_sems, after)
    return res[:n_w], res[n_w:]


def _sum_slabs(slabs, pair, kind, size, layer, into, *, name):
    n_s, k, n = slabs.shape
    tr = _div_tile(k, 512, BF16_ROWS)
    tc = n if kind in ("col", "row") else LANES
    x, y, _ = _my_place()
    mine = 2 * x + y
    shard = size + HEAD_DIM
    row0 = mine * (k // tr) if kind == "row" else 0
    col0 = {"col": mine, "row": 0, "win_main": (mine * shard + HEAD_DIM * (mine % 2)) // LANES,
            "win_strad": (size + 2 * shard * (mine // 2)) // LANES}[kind]
    on = _on_core(layer)[0]
    scalars = jnp.stack([mine, row0 * on, col0 * on, on]).astype(jnp.int32)

    def body(s_ref, slab_ref, own_ref, *rest):
        o_ref = rest[-1]
        me = s_ref[0]

        @pl.when(s_ref[3] == 1)
        def _():
            acc = jnp.zeros(o_ref.shape, F32)
            for i in range(n_s):
                acc = acc + jnp.where(me == i, own_ref[...], slab_ref[i]).astype(F32)
            o_ref[...] = acc

    operands = [scalars, slabs, pair] + ([] if into is None else [into])
    return pl.pallas_call(
        body, name=name,
        grid_spec=pltpu.PrefetchScalarGridSpec(
            num_scalar_prefetch=1, grid=(k // tr, n // tc),
            in_specs=[pl.BlockSpec((n_s, tr, tc), lambda i, j, s: (0, i * s[3], j * s[3])),
                      pl.BlockSpec((tr, tc), lambda i, j, s: (s[1] + i * s[3], s[2] + j * s[3]))]
            + ([] if into is None else [HBM]),
            out_specs=pl.BlockSpec((None, tr, tc), lambda i, j, s: (layer, i * s[3], j * s[3]))),
        out_shape=jax.ShapeDtypeStruct((2, k, n), F32),
        input_output_aliases={} if into is None else {3: 0},
        compiler_params=_params(dimension_semantics=("arbitrary", "arbitrary")),
    )(*operands)


def _exchange_layers(bufs, *, name):
    n_w = len(bufs)

    def body(*refs):
        dst = refs[n_w:2 * n_w]
        send_sems, recv_sems = refs[2 * n_w:]
        x, y, c = _my_place()

        def copy(i, layer):
            return pltpu.make_async_remote_copy(src_ref=dst[i].at[layer], dst_ref=dst[i].at[layer], send_sem=send_sems.at[i],
                                                recv_sem=recv_sems.at[i], device_id=(x, y, 1 - c), device_id_type=MESH)

        sends = [copy(i, c) for i in range(n_w)]
        for cp in sends:
            cp.start()
        for i in range(n_w):
            copy(i, 1 - c).wait_recv()
        for cp in sends:
            cp.wait_send()

    return pl.pallas_call(
        body, name=name, in_specs=[HBM] * n_w, out_specs=[HBM] * n_w,
        out_shape=[jax.ShapeDtypeStruct(b.shape, b.dtype) for b in bufs],
        input_output_aliases={i: i for i in range(n_w)},
        scratch_shapes=[pltpu.SemaphoreType.DMA((n_w,)), pltpu.SemaphoreType.DMA((n_w,))],
    )(*bufs)


def _all_sum_small(v, *, name):
    r = v.shape[0]
    relations = [(dx, dy, dc) for dx in (0, 1) for dy in (0, 1) for dc in (0, 1)][1:]

    def body(v_ref, o_ref, buf, send_sems, recv_sems):
        x, y, c = _my_place()
        me = 4 * x + 2 * y + c
        buf[me] = v_ref[...]
        peers = [(x + dx - 2 * x * dx, y + dy - 2 * y * dy, c + dc - 2 * c * dc) for dx, dy, dc in relations]

        def copy(k, slot):
            return pltpu.make_async_remote_copy(src_ref=v_ref, dst_ref=buf.at[slot], send_sem=send_sems.at[k],
                                                recv_sem=recv_sems.at[k], device_id=peers[k], device_id_type=MESH)

        sends = [copy(k, me) for k in range(len(relations))]
        for cp in sends:
            cp.start()
        for k, (px, py, pc) in enumerate(peers):
            copy(k, 4 * px + 2 * py + pc).wait_recv()
        for cp in sends:
            cp.wait_send()
        acc = buf[0]
        for i in range(1, 8):
            acc = acc + buf[i]
        o_ref[...] = acc

    vm = pl.BlockSpec(memory_space=pltpu.VMEM)
    return pl.pallas_call(
        body, name=name, in_specs=[vm], out_specs=vm, out_shape=jax.ShapeDtypeStruct((r, LANES), F32),
        scratch_shapes=[pltpu.VMEM((8, r, LANES), F32), pltpu.SemaphoreType.DMA((7,)), pltpu.SemaphoreType.DMA((7,))],
    )(v)


SHARDED = (("ffn1_w_up", "col"), ("ffn1_w_down", "row"), ("w_in", "win"), ("w_branch_a", "col"),
           ("w_branch_b", "col"), ("w_out", "row"), ("ffn2_w_up", "col"), ("ffn2_w_down", "row"))
REPLICATED = ("ffn1_norm", "mix_norm", "na_rel_bias", "ffn2_norm", "final_norm")


def _weight_pieces(w):
    even = lax.axis_index("y") == 0
    shards, kinds, names = [], [], []
    for name, kind in SHARDED:
        wb = w[name].astype(BF16)
        if kind == "win":
            main = wb.shape[-1] - HEAD_DIM
            assert main % LANES == 0
            zeros = jnp.zeros(wb.shape[:-1] + (HEAD_DIM,), BF16)
            shards += [jnp.where(even, wb[..., :main], wb[..., HEAD_DIM:]),
                       jnp.where(even, jnp.concatenate([wb[..., main:], zeros], -1),
                                 jnp.concatenate([zeros, wb[..., :HEAD_DIM]], -1))]
            kinds += ["win_main", "slot"]
            names += [name, name + "_strad"]
        else:
            shards.append(wb)
            kinds.append(kind)
            names.append(name)
    return names, kinds, shards


def _finish_w_in(full):
    full = dict(full)
    strad = full.pop("w_in_strad")
    main = full["w_in"].shape[1] // N_CHIPS - HEAD_DIM
    for i in range(N_CHIPS // 2):
        lo = main + 2 * (main + HEAD_DIM) * i
        full["w_in"] = full["w_in"].at[:, lo:lo + LANES].set(strad[2 * i] + strad[2 * i + 1])
    return full


def _scatter_pieces(shards):
    names, kinds, sizes, srcs = [], [], [], []
    for name, kind in SHARDED:
        shp = shards[name].shape
        if kind == "win":
            names += [name, name + "_strad"]
            kinds += ["win_main", "win_strad"]
            sizes += [shp[2] - HEAD_DIM] * 2
            srcs += [name, name]
        else:
            names.append(name)
            kinds.append(kind)
            sizes.append(shp[1] if kind == "row" else shp[2])
            srcs.append(name)
    return names, kinds, sizes, srcs


def _pair_sums(grads, layer, tag):
    uniq = list(grads)
    arrived = _grads_to_sibling([grads[n] for n in uniq], layer, name=f"{tag}_to_sibling")
    return {n: _pair_add(grads[n], a, layer, name=f"{tag}_pair_{n}") for n, a in zip(uniq, arrived)}


def _finish_weight_grads(reduced, names, tag):
    out = dict(zip(names, _exchange_layers(reduced, name=f"{tag}_layers")))
    if "w_in_strad" in out:
        strad = out.pop("w_in_strad")
        even = lax.axis_index("y") == 0
        out["w_in"] = jnp.where(even, jnp.concatenate([out["w_in"], strad[..., :HEAD_DIM]], -1),
                                jnp.concatenate([strad[..., HEAD_DIM:], out["w_in"]], -1))
    return out


class _Grads:
    def __init__(self):
        self.arrays = {}

    def put(self, weight, layer, a, b, *, cols=None, col_off=0, **kw):
        self.arrays[weight, layer] = _mm(a, b, mode="tn", out_dtype=BF16, out_cols=cols, out_col_off=col_off,
                                         out_into=self.arrays.get((weight, layer)), **kw)


def _ffn_fwd(x, h, w_up, w_down, tag):
    t, d = x.shape
    f = w_down.shape[0]
    a, gate, up = _mm_swiglu_fwd(h, w_up, tm=_div_tile(t, ROWS_NARROW, 8), tn=MXU_N, name=f"{tag}_up")
    x_out = _mm(a, w_down, mode="nn", out_dtype=F32, tm=_div_tile(t, ROWS_WIDE, 8), tn=d, tk=f, alpha=0.5, res=x, name=f"{tag}_down")
    return x_out, (x, h, a, gate, up)


def _ffn_bwd(dx, dxb, saved, norm_g, w_up, w_down, layer, grads, wname, tag, scatter):
    x, h, a, gate, up = saved
    t, d = x.shape
    f = w_down.shape[0]
    tn = _div_tile(f, 1408)
    grads.put(f"{wname}_w_down", layer, a, dxb, tm=tn, tn=d, tk=1024, alpha=0.5, name=f"{tag}_dwd")
    d_gate, d_up = _mm_swiglu_bwd(dxb, w_down, gate, up, alpha=0.5, tm=_div_tile(t, ROWS_NARROW, 8), tn=MXU_N, name=f"{tag}_da")
    grads.put(f"{wname}_w_up", layer, h, d_gate, cols=2 * f, tm=d, tn=tn, tk=1024, name=f"{tag}_dwg")
    grads.put(f"{wname}_w_up", layer, h, d_up, cols=2 * f, col_off=f // tn, tm=d, tn=tn, tk=1024, name=f"{tag}_dwu")
    started = scatter(layer, [f"{wname}_w_up", f"{wname}_w_down"])
    dh = _mm(d_gate, w_up, mode="nt", out_dtype=F32, tm=_div_tile(t, ROWS_WIDE, 8), tn=d, tk=f, name=f"{tag}_dh1")
    dh = _mm(d_up, w_up, mode="nt", out_dtype=F32, tm=_div_tile(t, ROWS_WIDE, 8), tn=d, tk=f, b_k_off=1, res=dh, name=f"{tag}_dh2")
    return _rms_bwd(dh, x, norm_g + started, dx, tt=512, name=f"{tag}_dnorm")


def _to_heads(y, b, n_heads):
    t, w = y.shape
    return y.reshape(b, t // b, n_heads, HEAD_DIM).transpose(0, 2, 1, 3)


def _from_heads(y):
    b, n, s, hd = y.shape
    return y.transpose(0, 2, 1, 3).reshape(b * s, n * hd)


N_QKV = 3 * (DIL_HEADS + NA_HEADS) * HEAD_DIM


def _mixer_fwd(x, b, norm_g, full, bias, tabs, tag):
    t, d = x.shape
    s = t // b
    n_in = full["w_in"].shape[1]
    h = _rms_fwd(x, norm_g, tt=512, name=f"{tag}_norm")
    proj = _mm(h, full["w_in"], mode="nn", out_dtype=F32, tm=_div_tile(t, ROWS_NARROW, 8), tn=MXU_N, tk=d, name=f"{tag}_in")
    heads = _split_heads(proj.reshape(b, s, -1), *tabs, n_pairs=N_QKV // LANES, rot_pairs=DIL_HEADS,
                         scale_ranges=((0, DIL_HEADS // 2), (3 * DIL_HEADS // 2, (3 * DIL_HEADS + NA_HEADS) // 2)),
                         name=f"{tag}_heads")
    ya, lse_a = _dil_attn_fwd(heads, name=f"{tag}_dil")
    yb, lse_b = _na_attn_fwd(heads, bias, first=3 * DIL_HEADS, name=f"{tag}_na")
    ya2, yb2 = _from_heads(ya), _from_heads(yb)
    z = _mm(ya2, full["w_branch_a"], mode="nn", out_dtype=F32, tm=_div_tile(t, ROWS_NARROW, 8), tn=MXU_N, tk=ya2.shape[1],
            out_slab=(0, 2), name=f"{tag}_za")
    z = _mm(yb2, full["w_branch_b"], mode="nn", out_dtype=F32, tm=_div_tile(t, ROWS_NARROW, 8), tn=MXU_N, tk=yb2.shape[1],
            out_slab=(1, 2), out_into=z, name=f"{tag}_zb")
    merged = _gate_fwd(proj, z, gate_col=N_QKV, tt=1024, name=f"{tag}_gate")
    x_out = _mm(merged, full["w_out"], mode="nn", out_dtype=F32, tm=_div_tile(t, ROWS_NARROW, 8), tn=MXU_N, tk=d, res=x, name=f"{tag}_out")
    return x_out, (x, h, proj, heads, ya, lse_a, yb, lse_b, ya2, yb2, z, merged)


def _mixer_bwd(dx, dob, b, saved, norm_g, full, layer, bias, tabs, grads, tag, scatter):
    x, h, proj, heads, ya, lse_a, yb, lse_b, ya2, yb2, z, merged = saved
    t, d = x.shape
    s = t // b
    n_in = full["w_in"].shape[1]
    grads.put("w_out", layer, merged, dob, tm=d, tn=d, tk=1024, name=f"{tag}_dwo")
    dm = _mm(dob, full["w_out"], mode="nt", out_dtype=F32, tm=_div_tile(t, ROWS_NARROW, 8), tn=MXU_N, tk=d, name=f"{tag}_dm")
    dz, dproj = _gate_bwd(dm, proj, z, gate_col=N_QKV, tt=1024, name=f"{tag}_dgate")
    grads.put("w_branch_a", layer, ya2, dz, b_sel=0, tm=ya2.shape[1], tn=d, tk=1024, name=f"{tag}_dwa")
    grads.put("w_branch_b", layer, yb2, dz, b_sel=1, tm=yb2.shape[1], tn=d, tk=1024, name=f"{tag}_dwb")
    started = scatter(layer, ["w_out", "w_branch_a", "w_branch_b"])
    dya = _mm(dz, full["w_branch_a"], mode="nt", out_dtype=F32, tm=_div_tile(t, ROWS_NARROW, 8), tn=MXU_N, tk=d, a_sel=0, name=f"{tag}_dya")
    dyb = _mm(dz, full["w_branch_b"], mode="nt", out_dtype=F32, tm=_div_tile(t, ROWS_NARROW, 8), tn=MXU_N, tk=d, a_sel=1, name=f"{tag}_dyb")
    d_dil = _dil_attn_bwd(heads, ya, lse_a, _to_heads(dya, b, DIL_GROUP_HEADS), name=f"{tag}_ddil")
    d_na, d_bias = _na_attn_bwd(heads, bias, yb, lse_b, _to_heads(dyb, b, NA_HEADS), first=3 * DIL_HEADS, name=f"{tag}_dna")
    dproj = _merge_heads(d_dil, *tabs, heads_per_row=DIL_GROUP_HEADS, rot_pairs=DIL_HEADS, scale_pairs=DIL_HEADS // 2,
                         dilated=True, out_cols=n_in, tile_off=0, into=dproj.reshape(b, s, n_in), name=f"{tag}_dheads_a")
    dproj = _merge_heads(d_na, *tabs, heads_per_row=NA_HEADS, rot_pairs=0, scale_pairs=NA_HEADS // 2, dilated=False,
                         out_cols=n_in, tile_off=3 * DIL_HEADS // 2, into=dproj, name=f"{tag}_dheads_b").reshape(t, n_in)
    grads.put("w_in", layer, h, dproj, tm=_div_tile(d, 512), tn=_div_tile(n_in, 2944), tk=1024, name=f"{tag}_dwin")
    started = started + scatter(layer, ["w_in"])
    dh = _mm(dproj, full["w_in"], mode="nt", out_dtype=F32, tm=_div_tile(t, ROWS_WIDE, 8), tn=d, tk=_div_tile(n_in, 2944), name=f"{tag}_dh")
    dx_in, dxb_in, d_norm = _rms_bwd(dh, x, norm_g + started, dx, tt=512, name=f"{tag}_dnorm")
    d_rb =_na_collapse_bias(d_bias, name=f"{tag}_dbias")
    return dx_in, dxb_in, d_norm, d_rb


def kernel(x, ffn1_norm, ffn1_w_up, ffn1_w_down, mix_norm, w_in, na_rel_bias, w_branch_a, w_branch_b, w_out, ffn2_norm, ffn2_w_up, ffn2_w_down, final_norm, loss_target, m_ffn1_norm, m_ffn1_w_up, m_ffn1_w_down, m_mix_norm, m_w_in, m_na_rel_bias, m_w_branch_a, m_w_branch_b, m_w_out, m_ffn2_norm, m_ffn2_w_up, m_ffn2_w_down, m_final_norm, v_ffn1_norm, v_ffn1_w_up, v_ffn1_w_down, v_mix_norm, v_w_in, v_na_rel_bias, v_w_branch_a, v_w_branch_b, v_w_out, v_ffn2_norm, v_ffn2_w_up, v_ffn2_w_down, v_final_norm):
    w = dict(ffn1_norm=ffn1_norm, ffn1_w_up=ffn1_w_up, ffn1_w_down=ffn1_w_down, mix_norm=mix_norm, w_in=w_in,
             na_rel_bias=na_rel_bias, w_branch_a=w_branch_a, w_branch_b=w_branch_b, w_out=w_out, ffn2_norm=ffn2_norm,
             ffn2_w_up=ffn2_w_up, ffn2_w_down=ffn2_w_down, final_norm=final_norm)
    mom = dict(ffn1_norm=m_ffn1_norm, ffn1_w_up=m_ffn1_w_up, ffn1_w_down=m_ffn1_w_down, mix_norm=m_mix_norm, w_in=m_w_in,
               na_rel_bias=m_na_rel_bias, w_branch_a=m_w_branch_a, w_branch_b=m_w_branch_b, w_out=m_w_out,
               ffn2_norm=m_ffn2_norm, ffn2_w_up=m_ffn2_w_up, ffn2_w_down=m_ffn2_w_down, final_norm=m_final_norm)
    var = dict(ffn1_norm=v_ffn1_norm, ffn1_w_up=v_ffn1_w_up, ffn1_w_down=v_ffn1_w_down, mix_norm=v_mix_norm, w_in=v_w_in,
               na_rel_bias=v_na_rel_bias, w_branch_a=v_w_branch_a, w_branch_b=v_w_branch_b, w_out=v_w_out,
               ffn2_norm=v_ffn2_norm, ffn2_w_up=v_ffn2_w_up, ffn2_w_down=v_ffn2_w_down, final_norm=v_final_norm)
    b, s, d = x.shape
    t = b * s
    depth = ffn1_norm.shape[0]
    assert depth == 2, "core c of a chip sends / reduces layer c"
    shards = {name: w[name] for name, _ in SHARDED}

    names, kinds, pieces = _weight_pieces(w)
    by_layer = [[p[l:l + 1] for p in pieces] for l in range(depth)]
    own = [[_place_own(p, kind, 0, name=f"own{l}_{nm}") for nm, kind, p in zip(names, kinds, by_layer[l])] for l in range(depth)]
    full = [{}, {}]

    def gather_start(layer, group, after, tag):
        idx = [i for i, nm in enumerate(names) if nm in group]
        pick = lambda seq: [seq[i] for i in idx]
        *state, token = _gather_layer_start(pick(by_layer[layer]), pick(kinds), pick(own[layer]), 0, after, name=f"{tag}_start")
        return (layer, idx, tag, state), token[:1, :1]

    def gather_finish(started, after):
        layer, idx, tag, state = started
        pick = lambda seq: [seq[i] for i in idx]
        landed = _gather_layer_wait(*state, pick(kinds), 0, after, name=f"{tag}_wait")
        done = _gather_layer_forward([by_layer[layer][i].shape for i in idx], pick(kinds), landed, name=f"{tag}_forward")
        full[layer].update(zip(pick(names), done))
        return done[0]

    ffn1, mixer, ffn2 = names[:2], names[2:7], names[7:]
    assert mixer[0] == "w_in" and ffn2[0] == "ffn2_w_up", names
    xc = x.reshape(t, d)
    l0_ffn1, token_ffn1 = gather_start(0, ffn1, xc, "gather_l0_ffn1")
    tabs = _rope_tables(s)
    bias = _na_expand_bias(na_rel_bias, name="na_bias")

    saved = []
    h = _rms_fwd(xc, ffn1_norm[:1] + token_ffn1, tt=512, name="l0_ffn1_norm")
    landed = gather_finish(l0_ffn1, h)
    l0_mixer, token_mixer = gather_start(0, mixer, landed, "gather_l0_mixer")
    xc, s1 = _ffn_fwd(xc, h + token_mixer.astype(BF16), full[0]["ffn1_w_up"], full[0]["ffn1_w_down"], "l0_ffn1")
    landed = gather_finish(l0_mixer, xc)
    full[0] = _finish_w_in(full[0])
    l0_ffn2, token_ffn2 = gather_start(0, ffn2, landed, "gather_l0_ffn2")
    layer1, token_layer1 = gather_start(1, names, landed, "gather_l1")
    xc, s2 = _mixer_fwd(xc, b, mix_norm[:1] + token_ffn2 + token_layer1, full[0], bias[0], tabs, "l0_mix")
    gather_finish(l0_ffn2, xc)
    xc, s3 = _ffn_fwd(xc, _rms_fwd(xc, ffn2_norm[:1], tt=512, name="l0_ffn2_norm"), full[0]["ffn2_w_up"], full[0]["ffn2_w_down"],
                      "l0_ffn2")
    saved.append((s1, s2, s3))
    gather_finish(layer1, xc)
    full[1] = _finish_w_in(full[1])
    for l in range(1, depth):
        xc, s1 = _ffn_fwd(xc, _rms_fwd(xc, ffn1_norm[l:l + 1], tt=512, name=f"l{l}_ffn1_norm"), full[l]["ffn1_w_up"],
                          full[l]["ffn1_w_down"], f"l{l}_ffn1")
        xc, s2 = _mixer_fwd(xc, b, mix_norm[l:l + 1], full[l], bias[l], tabs, f"l{l}_mix")
        xc, s3 = _ffn_fwd(xc, _rms_fwd(xc, ffn2_norm[l:l + 1], tt=512, name=f"l{l}_ffn2_norm"), full[l]["ffn2_w_up"],
                          full[l]["ffn2_w_down"], f"l{l}_ffn2")
        saved.append((s1, s2, s3))

    dx, dxb, d_final, loss_part = _final_loss(xc, final_norm.reshape(1, d), loss_target.reshape(t, d), tt=512, name="final_loss")
    grads = _Grads()
    piece_names, piece_kinds, piece_sizes, piece_srcs = _scatter_pieces(shards)
    scattered = []

    def scatter(layer, weights):
        tag = f"grads{layer}_{weights[0]}"
        pair = _pair_sums({wn: grads.arrays[wn, layer] for wn in weights}, layer, tag)
        idx = [i for i, src in enumerate(piece_srcs) if src in weights]
        pick = lambda seq: [seq[i] for i in idx]
        *state, token = _grads_to_chips_start([pair[src] for src in pick(piece_srcs)], pick(piece_kinds), pick(piece_sizes),
                                              layer, name=f"{tag}_to_chips_start")
        scattered.append((layer, idx, state))
        return token[:1, :1]
    small = {name: [None] * depth for name in REPLICATED[:-1]}
    for l in reversed(range(depth)):
        s1, s2, s3 = saved[l]
        dx, dxb, small["ffn2_norm"][l] = _ffn_bwd(dx, dxb, s3, ffn2_norm[l:l + 1], full[l]["ffn2_w_up"], full[l]["ffn2_w_down"],
                                                  l, grads, "ffn2", f"l{l}_ffn2", scatter)
        dx, dxb, small["mix_norm"][l], small["na_rel_bias"][l] = _mixer_bwd(
            dx, dxb, b, s2, mix_norm[l:l + 1], full[l], l, bias[l], tabs, grads, f"l{l}_mix", scatter)
        dx, dxb, small["ffn1_norm"][l] = _ffn_bwd(dx, dxb, s1, ffn1_norm[l:l + 1], full[l]["ffn1_w_up"], full[l]["ffn1_w_down"],
                                                  l, grads, "ffn1", f"l{l}_ffn1", scatter)
    grad_x = dx.reshape(b, s, d)
    reduced = [None] * len(piece_names)

    def arrive(group, after):
        layer, idx, state = group
        state = _grads_to_chips_wait(*state, [piece_kinds[i] for i in idx], [piece_sizes[i] for i in idx], layer, after,
                                     name=f"grads{layer}_{piece_names[idx[0]]}_to_chips_wait")
        for i, p, sl in zip(idx, *state):
            reduced[i] = _sum_slabs(sl, p, piece_kinds[i], piece_sizes[i], layer, reduced[i],
                                    name=f"grads{layer}_sum_{piece_names[i]}")
        return idx

    for group in scattered[:-1]:
        arrive(group, dx)
    late = scattered[-1][1]
    early = [i for i in range(len(piece_names)) if i not in late]
    g_out = _finish_weight_grads([reduced[i] for i in early], [piece_names[i] for i in early], "grads_early")

    parts = [jnp.stack(small[name]).reshape(-1) for name in REPLICATED[:-1]] + [d_final.reshape(-1), loss_part[0, :1]]
    sizes = [v.shape[0] for v in parts]
    flat = jnp.concatenate(parts)
    flat = jnp.pad(flat, (0, -flat.shape[0] % (8 * LANES)))
    small_sum = _all_sum_small(flat.reshape(-1, LANES), name="small_all_sum").reshape(-1)
    off = 0
    for name, n in zip(REPLICATED, sizes[:-1]):
        g_out[name] = small_sum[off:off + n].reshape(w[name].shape)
        off += n
    loss = small_sum[off]

    names = list(w)
    delta, new_m, new_v = {}, {}, {}
    for name in [n for n in names if n in g_out]:
        delta[name], new_m[name], new_v[name] = _adamw(w[name], g_out[name], mom[name], var[name], name=f"adamw_{name}")
    arrive(scattered[-1], delta["w_in"])
    g_out.update(_finish_weight_grads([reduced[i] for i in late], [piece_names[i] for i in late], "grads_late"))
    for name in [n for n in names if n not in delta]:
        delta[name], new_m[name], new_v[name] = _adamw(w[name], g_out[name], mom[name], var[name], name=f"adamw_{name}")
    return (loss, grad_x, *[g_out[n] for n in names], *[delta[n] for n in names], *[new_m[n] for n in names],
            *[new_v[n] for n in names])
```

```python
import functools

import numpy as np
import jax
import jax.numpy as jnp
from jax import lax
from jax.experimental import pallas as pl
from jax.experimental.pallas import tpu as pltpu

F32, BF16 = jnp.float32, jnp.bfloat16
MESH = pl.DeviceIdType.MESH

HEAD_DIM = 64
DILATIONS = (1, 4, 16)
DIL_HALF = 64
DIL_GROUP_HEADS = 4
DIL_HEADS = 12
NA_HEADS = 8
GRID_W = 64
NA_ROWS = 8
NA_COLS = 16
ROPE_THETA = 10000.0
RMS_EPS = 1e-6
NEG_INF = -1e30
ADAM_LR, ADAM_B1, ADAM_B2, ADAM_EPS, ADAM_WD, ADAM_STEP = 0.001, 0.9, 0.999, 1e-08, 0.01, 10
QK_SCALE = HEAD_DIM ** -0.5

N_CHIPS = 4
LANES = 128
BF16_ROWS = 16
VMEM_LIMIT = 56 * 1024 * 1024
MXU_N = 256
ROWS_NARROW = 2048
ROWS_WIDE = 512

_NN = (((1,), (0,)), ((), ()))
_NT = (((1,), (1,)), ((), ()))
_TN = (((0,), (0,)), ((), ()))

HBM = pl.BlockSpec(memory_space=pl.ANY)


def _params(**kw):
    return pltpu.CompilerParams(vmem_limit_bytes=VMEM_LIMIT, **kw)


def _dot(a, b, dims):
    return lax.dot_general(a, b, dims, preferred_element_type=F32)


def _div_tile(n, cap, mult=LANES):
    best = None
    for t in range(mult, min(n, cap) + 1, mult):
        if n % t == 0:
            best = t
    return n if best is None else best


def _stacked(block, index, sel):
    if sel is None:
        return pl.BlockSpec(block, index)
    return pl.BlockSpec((None,) + block, lambda *g: (sel,) + index(*g))


def _mm(a, b, *, mode, out_dtype, tm, tn, tk, name, alpha=1.0, res=None, a_sel=None, b_sel=None, b_k_off=0,
        out_slab=None, out_cols=None, out_col_off=0, out_into=None):
    a2, b2 = a.shape[-2:], b.shape[-2:]
    if mode == "nn":
        (m, k), n = a2, b2[1]
        a_spec = _stacked((tm, tk), lambda i, j, kk: (i, kk), a_sel)
        b_spec = _stacked((tk, tn), lambda i, j, kk: (kk + b_k_off, j), b_sel)
        dims = _NN
    elif mode == "nt":
        (m, k), n = a2, b2[0]
        a_spec = _stacked((tm, tk), lambda i, j, kk: (i, kk), a_sel)
        b_spec = _stacked((tn, tk), lambda i, j, kk: (j, kk + b_k_off), b_sel)
        dims = _NT
    else:
        (k, m), n = a2, b2[1]
        a_spec = _stacked((tk, tm), lambda i, j, kk: (kk, i), a_sel)
        b_spec = _stacked((tk, tn), lambda i, j, kk: (kk + b_k_off, j), b_sel)
        dims = _TN
    assert m % tm == 0 and n % tn == 0 and k % tk == 0, (name, a.shape, b.shape)
    nk = k // tk
    has_res = res is not None
    if out_slab is None:
        o_spec = pl.BlockSpec((tm, tn), lambda i, j, kk: (i, j + out_col_off))
        out_shape = jax.ShapeDtypeStruct((m, n if out_cols is None else out_cols), out_dtype)
    else:
        o_spec = _stacked((tm, tn), lambda i, j, kk: (i, j + out_col_off), out_slab[0])
        out_shape = jax.ShapeDtypeStruct((out_slab[1], m, n if out_cols is None else out_cols), out_dtype)
    r_spec = pl.BlockSpec((tm, tn), lambda i, j, kk: (i, j))
    n_in = 2 + has_res + (out_into is not None)

    def body(*refs):
        a_ref, b_ref = refs[0], refs[1]
        r_ref = refs[2] if has_res else None
        o_ref = refs[n_in]
        p = _dot(a_ref[...], b_ref[...], dims)

        def finish(acc):
            y = acc * alpha if alpha != 1.0 else acc
            if has_res:
                y = y + r_ref[...].astype(F32)
            o_ref[...] = y.astype(o_ref.dtype)

        if nk == 1:
            finish(p)
        else:
            acc_ref = refs[n_in + 1]
            kk = pl.program_id(2)

            @pl.when(kk == 0)
            def _():
                acc_ref[...] = p

            @pl.when(kk > 0)
            def _():
                acc_ref[...] += p

            @pl.when(kk == nk - 1)
            def _():
                finish(acc_ref[...])

    operands = [a, b] + ([res] if has_res else [])
    in_specs = [a_spec, b_spec] + ([r_spec] if has_res else [])
    aliases = {}
    if out_into is not None:
        aliases = {len(operands): 0}
        operands.append(out_into)
        in_specs.append(HBM)
    return pl.pallas_call(
        body, name=name, grid=(m // tm, n // tn, nk), in_specs=in_specs, out_specs=o_spec, out_shape=out_shape,
        scratch_shapes=[pltpu.VMEM((tm, tn), F32)] if nk > 1 else [], input_output_aliases=aliases,
        compiler_params=_params(dimension_semantics=("parallel", "parallel", "arbitrary")),
    )(*operands)


def _mm_swiglu_fwd(h, w_up, *, tm, tn, name):
    m, k = h.shape
    n = w_up.shape[1] // 2
    h_spec = pl.BlockSpec((tm, k), lambda i, j: (i, 0))
    wg_spec = pl.BlockSpec((k, tn), lambda i, j: (0, j))
    wu_spec = pl.BlockSpec((k, tn), lambda i, j: (0, j + n // tn))
    o_spec = pl.BlockSpec((tm, tn), lambda i, j: (i, j))

    def body(h_ref, wg_ref, wu_ref, a_ref, g_ref, u_ref):
        hb = h_ref[...]
        g = _dot(hb, wg_ref[...], _NN)
        u = _dot(hb, wu_ref[...], _NN)
        a_ref[...] = (g * jax.nn.sigmoid(g) * u).astype(BF16)
        g_ref[...] = g.astype(BF16)
        u_ref[...] = u.astype(BF16)

    out = jax.ShapeDtypeStruct((m, n), BF16)
    return pl.pallas_call(
        body, name=name, grid=(m // tm, n // tn), in_specs=[h_spec, wg_spec, wu_spec],
        out_specs=[o_spec] * 3, out_shape=[out] * 3,
        compiler_params=_params(dimension_semantics=("parallel", "parallel")),
    )(h, w_up, w_up)


def _mm_swiglu_bwd(dy, w_down, gate, up, *, alpha, tm, tn, name):
    m, k = dy.shape
    n = w_down.shape[0]
    dy_spec = pl.BlockSpec((tm, k), lambda i, j: (i, 0))
    w_spec = pl.BlockSpec((tn, k), lambda i, j: (j, 0))
    o_spec = pl.BlockSpec((tm, tn), lambda i, j: (i, j))

    def body(dy_ref, w_ref, g_ref, u_ref, dg_ref, du_ref):
        da = _dot(dy_ref[...], w_ref[...], _NT) * alpha
        g = g_ref[...].astype(F32)
        u = u_ref[...].astype(F32)
        sg = jax.nn.sigmoid(g)
        dg_ref[...] = (da * u * (sg * (1.0 + g * (1.0 - sg)))).astype(BF16)
        du_ref[...] = (da * (g * sg)).astype(BF16)

    out = jax.ShapeDtypeStruct((m, n), BF16)
    return pl.pallas_call(
        body, name=name, grid=(m // tm, n // tn), in_specs=[dy_spec, w_spec, o_spec, o_spec],
        out_specs=[o_spec] * 2, out_shape=[out] * 2,
        compiler_params=_params(dimension_semantics=("parallel", "parallel")),
    )(dy, w_down, gate, up)


def _rms_fwd(x, g, *, tt, name):
    t, d = x.shape

    def body(x_ref, g_ref, h_ref):
        xv = x_ref[...]
        rstd = lax.rsqrt(jnp.mean(xv * xv, axis=1, keepdims=True) + RMS_EPS)
        h_ref[...] = (xv * rstd * g_ref[...]).astype(BF16)

    return pl.pallas_call(
        body, name=name, grid=(t // tt,),
        in_specs=[pl.BlockSpec((tt, d), lambda i: (i, 0)), pl.BlockSpec((1, d), lambda i: (0, 0))],
        out_specs=pl.BlockSpec((tt, d), lambda i: (i, 0)), out_shape=jax.ShapeDtypeStruct((t, d), BF16),
        compiler_params=_params(dimension_semantics=("parallel",)),
    )(x, g)


def _rms_bwd(dh, x, g, dres, *, tt, name):
    t, d = x.shape

    def body(dh_ref, x_ref, g_ref, r_ref, dx_ref, dxb_ref, dg_ref):
        xv = x_ref[...]
        rstd = lax.rsqrt(jnp.mean(xv * xv, axis=1, keepdims=True) + RMS_EPS)
        xhat = xv * rstd
        dhv = dh_ref[...]
        dxhat = dhv * g_ref[...]
        dx = r_ref[...] + rstd * (dxhat - xhat * jnp.mean(dxhat * xhat, axis=1, keepdims=True))
        dx_ref[...] = dx
        dxb_ref[...] = dx.astype(BF16)

        @pl.when(pl.program_id(0) == 0)
        def _():
            dg_ref[...] = jnp.zeros_like(dg_ref)

        dg_ref[...] += jnp.sum(dhv * xhat, axis=0, keepdims=True)

    row = pl.BlockSpec((tt, d), lambda i: (i, 0))
    vec = pl.BlockSpec((1, d), lambda i: (0, 0))
    return pl.pallas_call(
        body, name=name, grid=(t // tt,), in_specs=[row, row, vec, row], out_specs=[row, row, vec],
        out_shape=[jax.ShapeDtypeStruct((t, d), F32), jax.ShapeDtypeStruct((t, d), BF16), jax.ShapeDtypeStruct((1, d), F32)],
        compiler_params=_params(dimension_semantics=("arbitrary",)),
    )(dh, x, g, dres)


def _final_loss(x, g, target, *, tt, name):
    t, d = x.shape

    def body(x_ref, g_ref, t_ref, dx_ref, dxb_ref, dg_ref, loss_ref):
        xv = x_ref[...]
        gv = g_ref[...]
        rstd = lax.rsqrt(jnp.mean(xv * xv, axis=1, keepdims=True) + RMS_EPS)
        xhat = xv * rstd
        err = xhat * gv - t_ref[...]
        dy = err * (1.0 / d)
        dxhat = dy * gv
        dx = rstd * (dxhat - xhat * jnp.mean(dxhat * xhat, axis=1, keepdims=True))
        dx_ref[...] = dx
        dxb_ref[...] = dx.astype(BF16)

        @pl.when(pl.program_id(0) == 0)
        def _():
            dg_ref[...] = jnp.zeros_like(dg_ref)
            loss_ref[...] = jnp.zeros_like(loss_ref)

        dg_ref[...] += jnp.sum(dy * xhat, axis=0, keepdims=True)
        part = 0.5 * jnp.sum(jnp.mean(err * err, axis=1, keepdims=True), axis=0, keepdims=True)
        loss_ref[...] += jnp.broadcast_to(part, loss_ref.shape)

    row = pl.BlockSpec((tt, d), lambda i: (i, 0))
    vec = pl.BlockSpec((1, d), lambda i: (0, 0))
    one = pl.BlockSpec((1, LANES), lambda i: (0, 0))
    return pl.pallas_call(
        body, name=name, grid=(t // tt,), in_specs=[row, vec, row], out_specs=[row, row, vec, one],
        out_shape=[jax.ShapeDtypeStruct((t, d), F32), jax.ShapeDtypeStruct((t, d), BF16), jax.ShapeDtypeStruct((1, d), F32),
                   jax.ShapeDtypeStruct((1, LANES), F32)],
        compiler_params=_params(dimension_semantics=("arbitrary",)),
    )(x, g, target)


def _swap_halves(x):
    lane = lax.broadcasted_iota(jnp.int32, x.shape, 1)
    return jnp.where((lane // 32) % 2 == 0, pltpu.roll(x, 96, 1), pltpu.roll(x, 32, 1))


def _rope_tables(s):
    half = HEAD_DIM // 2
    inv_freq = ROPE_THETA ** (-jnp.arange(half, dtype=F32) / half)
    ang = jnp.arange(s).astype(F32)[:, None] * inv_freq[None, :]
    cos, sin = jnp.cos(ang), jnp.sin(ang)
    return jnp.tile(cos, (1, 4)), jnp.concatenate([-sin, sin, -sin, sin], axis=1)


def _dilation_of_tile(p):
    dilated = p < 3 * DIL_HEADS // 2
    g = (p % (DIL_HEADS // 2)) // (DIL_GROUP_HEADS // 2)
    return [(dilated & (g == gi)) | (jnp.logical_not(dilated) if gi == 0 else False) for gi in range(len(DILATIONS))]


def _residue_major(ref, d):
    s = ref.shape[0]
    if d == 1:
        return ref[...]
    return jnp.concatenate([ref[pl.ds(r, s // d, stride=d), :] for r in range(d)], axis=0)


def _split_heads(proj, cos4, sin4, *, n_pairs, rot_pairs, scale_ranges, name):
    b, s, _ = proj.shape

    def body(x_ref, c_ref, s_ref, o_ref):
        p = pl.program_id(1)
        is_q = functools.reduce(jnp.logical_or, [(p >= lo) & (p < hi) for lo, hi in scale_ranges])
        scale = jnp.where(is_q, QK_SCALE, 1.0)

        def put(y):
            o_ref[0] = y[:, :HEAD_DIM].astype(BF16)
            o_ref[1] = y[:, HEAD_DIM:].astype(BF16)

        for d, in_group in zip(DILATIONS, _dilation_of_tile(p)):
            @pl.when(in_group & (p < rot_pairs))
            def _(d=d):
                x = _residue_major(x_ref, d)
                put((x * _residue_major(c_ref, d) + _swap_halves(x) * _residue_major(s_ref, d)) * scale)

            @pl.when(in_group & (p >= rot_pairs))
            def _(d=d):
                put(_residue_major(x_ref, d) * scale)

    tab = pl.BlockSpec((s, LANES), lambda bi, p: (0, 0))
    return pl.pallas_call(
        body, name=name, grid=(b, n_pairs),
        in_specs=[pl.BlockSpec((None, s, LANES), lambda bi, p: (bi, 0, p)), tab, tab],
        out_specs=pl.BlockSpec((None, 2, s, HEAD_DIM), lambda bi, p: (bi, p, 0, 0)),
        out_shape=jax.ShapeDtypeStruct((b, 2 * n_pairs, s, HEAD_DIM), BF16),
        compiler_params=_params(dimension_semantics=("parallel", "parallel")),
    )(proj, cos4, sin4)


def _merge_heads(dheads, cos4, sin4, *, heads_per_row, rot_pairs, scale_pairs, dilated, out_cols, tile_off, into, name):
    b, hpr, r, s, _ = dheads.shape
    n_pairs = hpr * r // 2
    ppr = hpr // 2

    def body(d_ref, c_ref, s_ref, *rest):
        o_ref, t_ref = rest[-2:]
        p = pl.program_id(1)
        scale = jnp.where(p < scale_pairs, QK_SCALE, 1.0)

        def tokens(d):
            dy = jnp.concatenate([d_ref[0], d_ref[1]], axis=1)
            if d == 1:
                return dy
            for res in range(d):
                t_ref[pl.ds(res, s // d, stride=d), :] = dy[res * (s // d):(res + 1) * (s // d), :]
            return t_ref[...]

        groups = _dilation_of_tile(p) if dilated else [p >= 0]
        for d, in_group in zip(DILATIONS, groups):
            @pl.when(in_group & (p < rot_pairs))
            def _(d=d):
                dy = tokens(d)
                o_ref[...] = ((dy * c_ref[...] - _swap_halves(dy) * s_ref[...]) * scale).astype(BF16)

            @pl.when(in_group & (p >= rot_pairs))
            def _(d=d):
                o_ref[...] = (tokens(d) * scale).astype(BF16)

    tab = pl.BlockSpec((s, LANES), lambda bi, p: (0, 0))
    operands = [dheads, cos4, sin4] + ([] if into is None else [into])
    return pl.pallas_call(
        body, name=name, grid=(b, n_pairs),
        in_specs=[pl.BlockSpec((None, 2, None, s, HEAD_DIM), lambda bi, p: (bi, p % ppr, p // ppr, 0, 0)), tab, tab]
        + ([] if into is None else [HBM]),
        out_specs=pl.BlockSpec((None, s, LANES), lambda bi, p: (bi, 0, p + tile_off)),
        out_shape=jax.ShapeDtypeStruct((b, s, out_cols), BF16),
        input_output_aliases={} if into is None else {3: 0},
        scratch_shapes=[pltpu.VMEM((s, LANES), F32)],
        compiler_params=_params(dimension_semantics=("parallel", "parallel")),
    )(*operands)


DIL_TQ = 256


def _dil_block(g, s):
    run = s // DILATIONS[g]
    return DIL_TQ if run <= DIL_TQ else min(run, DIL_TQ + 2 * LANES)


def _dil_keys(g, q0, s):
    run = max(s // DILATIONS[g], DIL_TQ)
    lo = (q0 // run) * run
    return pl.multiple_of(jnp.clip(q0 - LANES, lo, lo + run - _dil_block(g, s)), LANES)


def _dil_band(g, q0, start, shape, s):
    row = q0 + lax.broadcasted_iota(jnp.int32, shape, 0)
    col = start + lax.broadcasted_iota(jnp.int32, shape, 1)
    ok = jnp.abs(row - col) <= DIL_HALF
    run = s // DILATIONS[g]
    if run < DIL_TQ:
        shift = run.bit_length() - 1
        ok = ok & ((row >> shift) == (col >> shift))
    return ok


def _dil_tokens(g, q0, s):
    d = DILATIONS[g]
    if d == 1:
        return [(0, DIL_TQ, pl.ds(q0, DIL_TQ))]
    run = s // d
    n = min(run, DIL_TQ)
    return [(lo, n, pl.ds(((q0 + lo) % run) * d + (q0 + lo) // run, n, stride=d)) for lo in range(0, DIL_TQ, n)]


def _dil_gather(ref, pieces):
    return jnp.concatenate([ref[rows, :] for _, _, rows in pieces], axis=0) if len(pieces) > 1 else ref[pieces[0][2], :]


def _dil_head_spec(part, g, s):
    return pl.BlockSpec((None, None, s, HEAD_DIM), lambda b, j: (b, part * DIL_HEADS + g * DIL_GROUP_HEADS + j, 0, 0))


def _dil_attn_fwd(heads, *, name):
    b, _, s, _ = heads.shape
    n_g = len(DILATIONS)

    def body(*refs):
        qkv = refs[:3 * n_g]
        o_ref, l_ref, og_ref, lg_ref = refs[3 * n_g:]
        for g in range(n_g):
            q_ref, k_ref, v_ref = qkv[3 * g:3 * g + 3]
            width = _dil_block(g, s)

            def step(i, carry, g=g, q_ref=q_ref, k_ref=k_ref, v_ref=v_ref, width=width):
                q0 = pl.multiple_of(i * DIL_TQ, DIL_TQ)
                start = _dil_keys(g, q0, s)
                sc = _dot(q_ref[pl.ds(q0, DIL_TQ), :], k_ref[pl.ds(start, width), :], _NT)
                sc = jnp.where(_dil_band(g, q0, start, sc.shape, s), sc, NEG_INF)
                m = jnp.max(sc, axis=1, keepdims=True)
                p = jnp.exp(sc - m)
                den = jnp.sum(p, axis=1, keepdims=True)
                o = _dot(p.astype(BF16), v_ref[pl.ds(start, width), :], _NN) / den
                lse = m + jnp.log(den)
                for lo, n, rows in _dil_tokens(g, q0, s):
                    og_ref[g, rows, :] = o[lo:lo + n]
                    lg_ref[g, rows, :] = lse[lo:lo + n]
                return carry

            lax.fori_loop(0, s // DIL_TQ, step, 0)
        lses = [lg_ref[g] for g in range(n_g)]
        m = functools.reduce(jnp.maximum, lses)
        ws = [jnp.exp(l - m) for l in lses]
        den = functools.reduce(jnp.add, ws)
        o_ref[...] = (functools.reduce(jnp.add, [w * og_ref[g] for g, w in enumerate(ws)]) / den).astype(o_ref.dtype)
        l_ref[...] = m + jnp.log(den)

    out = pl.BlockSpec((None, None, s, HEAD_DIM), lambda bi, j: (bi, j, 0, 0))
    lse = pl.BlockSpec((None, None, s, 1), lambda bi, j: (bi, j, 0, 0))
    return pl.pallas_call(
        body, name=name, grid=(b, DIL_GROUP_HEADS),
        in_specs=[_dil_head_spec(part, g, s) for g in range(n_g) for part in range(3)],
        out_specs=[out, lse],
        out_shape=[jax.ShapeDtypeStruct((b, DIL_GROUP_HEADS, s, HEAD_DIM), BF16),
                   jax.ShapeDtypeStruct((b, DIL_GROUP_HEADS, s, 1), F32)],
        scratch_shapes=[pltpu.VMEM((n_g, s, HEAD_DIM), F32), pltpu.VMEM((n_g, s, 1), F32)],
        compiler_params=_params(dimension_semantics=("parallel", "parallel")),
    )(*([heads] * (3 * n_g)))


def _dil_attn_bwd(heads, out, lse, dout, *, name):
    b, _, s, _ = heads.shape
    n_g = len(DILATIONS)

    def body(*refs):
        qkv = refs[:3 * n_g]
        o_ref, l_ref, do_ref, d_ref, delta_ref = refs[3 * n_g:]
        d_ref[...] = jnp.zeros_like(d_ref)
        delta_ref[...] = jnp.sum(do_ref[...] * o_ref[...].astype(F32), axis=1, keepdims=True)
        for g in range(n_g):
            q_ref, k_ref, v_ref = qkv[3 * g:3 * g + 3]
            width = _dil_block(g, s)

            def step(i, carry, g=g, q_ref=q_ref, k_ref=k_ref, v_ref=v_ref, width=width):
                q0 = pl.multiple_of(i * DIL_TQ, DIL_TQ)
                start = _dil_keys(g, q0, s)
                win = pl.ds(start, width)
                pieces = _dil_tokens(g, q0, s)
                do_b = _dil_gather(do_ref, pieces).astype(BF16)
                q, k, v = q_ref[pl.ds(q0, DIL_TQ), :], k_ref[win, :], v_ref[win, :]
                sc = _dot(q, k, _NT)
                p = jnp.where(_dil_band(g, q0, start, sc.shape, s), jnp.exp(sc - _dil_gather(l_ref, pieces)), 0.0)
                ds = (p * (_dot(do_b, v, _NT) - _dil_gather(delta_ref, pieces))).astype(BF16)
                d_ref[g, pl.ds(q0, DIL_TQ), :] = _dot(ds, k, _NN)
                d_ref[n_g + g, win, :] += _dot(ds, q, _TN)
                d_ref[2 * n_g + g, win, :] += _dot(p.astype(BF16), do_b, _TN)
                return carry

            lax.fori_loop(0, s // DIL_TQ, step, 0)

    per_head = lambda bi, j: (bi, j, 0, 0)
    return pl.pallas_call(
        body, name=name, grid=(b, DIL_GROUP_HEADS),
        in_specs=[_dil_head_spec(part, g, s) for g in range(n_g) for part in range(3)]
        + [pl.BlockSpec((None, None, s, HEAD_DIM), per_head), pl.BlockSpec((None, None, s, 1), per_head),
           pl.BlockSpec((None, None, s, HEAD_DIM), per_head)],
        out_specs=pl.BlockSpec((None, None, 3 * n_g, s, HEAD_DIM), lambda bi, j: (bi, j, 0, 0, 0)),
        out_shape=jax.ShapeDtypeStruct((b, DIL_GROUP_HEADS, 3 * n_g, s, HEAD_DIM), F32),
        scratch_shapes=[pltpu.VMEM((s, 1), F32)],
        compiler_params=_params(dimension_semantics=("parallel", "parallel")),
    )(*([heads] * (3 * n_g)), out, lse, dout)


NA_BIAS_ROWS = 2 * NA_ROWS - 1
NA_BIAS_COLS = 2 * NA_COLS - 1
NA_BLOCK = 4
NA_SPAN = NA_ROWS + NA_BLOCK - 1
NA_Q = NA_BLOCK * GRID_W
NA_KEYS = NA_SPAN * GRID_W
NA_FORMS = 3


def _na_onehot():
    c = np.arange(GRID_W)[:, None]
    k = np.arange(GRID_W)[None, :]
    lo = np.clip(c - NA_COLS // 2, 0, GRID_W - NA_COLS)
    valid = (k >= lo) & (k < lo + NA_COLS)
    onehot = np.zeros((GRID_W, GRID_W, LANES), np.float32)
    cc, kk = np.nonzero(valid)
    onehot[cc, kk, kk - cc + NA_COLS - 1] = 1.0
    return onehot.reshape(GRID_W * GRID_W, LANES), valid.reshape(1, GRID_W * GRID_W)


def _na_block_rows(n_rows):
    table = np.full((NA_FORMS, NA_BLOCK, NA_SPAN), NA_BIAS_ROWS, np.int64)
    n_blocks = n_rows // NA_BLOCK
    for form, ib in enumerate((0, 1, n_blocks - 1)):
        base = min(max(NA_BLOCK * ib - NA_ROWS // 2, 0), n_rows - NA_SPAN)
        for rl in range(NA_BLOCK):
            r = NA_BLOCK * ib + rl
            row_lo = min(max(r - NA_ROWS // 2, 0), n_rows - NA_ROWS)
            for kl in range(NA_SPAN):
                if row_lo <= base + kl < row_lo + NA_ROWS:
                    table[form, rl, kl] = base + kl - r + NA_ROWS - 1
    return table


def _na_block(ib, n_rows):
    n_blocks = n_rows // NA_BLOCK
    base = jnp.clip(NA_BLOCK * ib - NA_ROWS // 2, 0, n_rows - NA_SPAN)
    return base, jnp.where(ib == 0, 0, jnp.where(ib == n_blocks - 1, 2, 1))


def _na_expand_bias(rel_bias, *, name):
    l, h, nr, nc = rel_bias.shape
    onehot, valid = _na_onehot()
    rb = jnp.pad(rel_bias, ((0, 0), (0, 0), (0, 1), (0, LANES - nc))).reshape(l * h * (nr + 1), LANES)
    live = jnp.asarray(np.tile(np.arange(nr + 1) < nr, l * h).astype(np.float32)[:, None])

    def body(rb_ref, oh_ref, valid_ref, live_ref, e_ref):
        e = lax.dot_general(rb_ref[...], oh_ref[...], _NT, precision=lax.Precision.HIGHEST, preferred_element_type=F32)
        e_ref[...] = jnp.where((valid_ref[...] > 0) & (live_ref[...] > 0), e, NEG_INF)

    e = pl.pallas_call(
        body, name=name, out_shape=jax.ShapeDtypeStruct((l * h * (nr + 1), GRID_W * GRID_W), F32), compiler_params=_params(),
    )(rb, jnp.asarray(onehot), jnp.asarray(valid.astype(np.float32)), live)
    return e.reshape(l, h, nr + 1, GRID_W, GRID_W)


def _na_collapse_bias(de, *, name):
    b, h = de.shape[:2]
    onehot, _ = _na_onehot()
    rows = h * NA_BIAS_ROWS

    def diag(e_ref, oh_ref, o_ref):
        e = e_ref[0]
        for bi in range(1, b):
            e = e + e_ref[bi]
        o_ref[...] = lax.dot_general(e, oh_ref[...], _NN, precision=lax.Precision.HIGHEST, preferred_element_type=F32)

    drb = pl.pallas_call(
        diag, name=name, out_shape=jax.ShapeDtypeStruct((rows, LANES), F32), compiler_params=_params(),
    )(de.reshape(b, rows, GRID_W * GRID_W), jnp.asarray(onehot))
    return drb[:, :NA_BIAS_COLS].reshape(h, NA_BIAS_ROWS, NA_BIAS_COLS)


def _na_tiles(n_rows):
    table = _na_block_rows(n_rows)
    return [(f, rl, kl, int(table[f, rl, kl])) for f in range(NA_FORMS) for rl in range(NA_BLOCK) for kl in range(NA_SPAN)]


def _na_tile(ref, form, rl, kl):
    return ref.at[form, rl * GRID_W:(rl + 1) * GRID_W, kl * GRID_W:(kl + 1) * GRID_W]


def _na_head_spec(part, first, s):
    return pl.BlockSpec((None, None, s, HEAD_DIM), lambda b, h: (b, first + part * NA_HEADS + h, 0, 0))


def _na_attn_fwd(heads, bias, *, first, name):
    b, _, s, _ = heads.shape
    n_rows = s // GRID_W
    tiles = _na_tiles(n_rows)

    def body(q_ref, k_ref, v_ref, e_ref, o_ref, l_ref, b_ref):
        for form, rl, kl, i in tiles:
            _na_tile(b_ref, form, rl, kl)[...] = e_ref[i]

        def step(ib, carry):
            base, form = _na_block(ib, n_rows)
            rows = pl.ds(pl.multiple_of(ib * NA_Q, NA_Q), NA_Q)
            win = pl.ds(pl.multiple_of(base * GRID_W, GRID_W), NA_KEYS)
            sc = _dot(q_ref[rows, :], k_ref[win, :], _NT) + b_ref[form]
            m = jnp.max(sc, axis=1, keepdims=True)
            p = jnp.exp(sc - m)
            den = jnp.sum(p, axis=1, keepdims=True)
            o_ref[rows, :] = (_dot(p.astype(BF16), v_ref[win, :], _NN) / den).astype(o_ref.dtype)
            l_ref[rows, :] = m + jnp.log(den)
            return carry

        lax.fori_loop(0, n_rows // NA_BLOCK, step, 0)

    per_head = lambda bi, h: (bi, h, 0, 0)
    return pl.pallas_call(
        body, name=name, grid=(b, NA_HEADS),
        in_specs=[_na_head_spec(part, first, s) for part in range(3)]
        + [pl.BlockSpec((None, NA_BIAS_ROWS + 1, GRID_W, GRID_W), lambda bi, h: (h, 0, 0, 0))],
        out_specs=[pl.BlockSpec((None, None, s, HEAD_DIM), per_head), pl.BlockSpec((None, None, s, 1), per_head)],
        out_shape=[jax.ShapeDtypeStruct((b, NA_HEADS, s, HEAD_DIM), BF16), jax.ShapeDtypeStruct((b, NA_HEADS, s, 1), F32)],
        scratch_shapes=[pltpu.VMEM((NA_FORMS, NA_Q, NA_KEYS), F32)],
        compiler_params=_params(dimension_semantics=("parallel", "parallel")),
    )(heads, heads, heads, bias)


def _na_attn_bwd(heads, bias, out, lse, dout, *, first, name):
    b, _, s, _ = heads.shape
    n_rows = s // GRID_W
    tiles = _na_tiles(n_rows)

    def body(q_ref, k_ref, v_ref, e_ref, o_ref, l_ref, do_ref, d_ref, de_ref, b_ref, db_ref):
        for form, rl, kl, i in tiles:
            _na_tile(b_ref, form, rl, kl)[...] = e_ref[i]
        d_ref[...] = jnp.zeros_like(d_ref)
        db_ref[...] = jnp.zeros_like(db_ref)

        def step(ib, carry):
            base, form = _na_block(ib, n_rows)
            rows = pl.ds(pl.multiple_of(ib * NA_Q, NA_Q), NA_Q)
            win = pl.ds(pl.multiple_of(base * GRID_W, GRID_W), NA_KEYS)
            q, k, v = q_ref[rows, :], k_ref[win, :], v_ref[win, :]
            do = do_ref[rows, :]
            delta = jnp.sum(do * o_ref[rows, :].astype(F32), axis=1, keepdims=True)
            do_b = do.astype(BF16)
            p = jnp.exp(_dot(q, k, _NT) + b_ref[form] - l_ref[rows, :])
            ds = p * (_dot(do_b, v, _NT) - delta)
            db_ref[form] += ds
            ds_b = ds.astype(BF16)
            d_ref[0, rows, :] = _dot(ds_b, k, _NN)
            d_ref[1, win, :] += _dot(ds_b, q, _TN)
            d_ref[2, win, :] += _dot(p.astype(BF16), do_b, _TN)
            return carry

        lax.fori_loop(0, n_rows // NA_BLOCK, step, 0)
        acc = [None] * NA_BIAS_ROWS
        for form, rl, kl, i in tiles:
            if i < NA_BIAS_ROWS:
                t = _na_tile(db_ref, form, rl, kl)[...]
                acc[i] = t if acc[i] is None else acc[i] + t
        for i in range(NA_BIAS_ROWS):
            de_ref[i] = acc[i]

    per_head = lambda bi, h: (bi, h, 0, 0)
    return pl.pallas_call(
        body, name=name, grid=(b, NA_HEADS),
        in_specs=[_na_head_spec(part, first, s) for part in range(3)]
        + [pl.BlockSpec((None, NA_BIAS_ROWS + 1, GRID_W, GRID_W), lambda bi, h: (h, 0, 0, 0)),
           pl.BlockSpec((None, None, s, HEAD_DIM), per_head), pl.BlockSpec((None, None, s, 1), per_head),
           pl.BlockSpec((None, None, s, HEAD_DIM), per_head)],
        out_specs=[pl.BlockSpec((None, None, 3, s, HEAD_DIM), lambda bi, h: (bi, h, 0, 0, 0)),
                   pl.BlockSpec((None, None, NA_BIAS_ROWS, GRID_W, GRID_W), lambda bi, h: (bi, h, 0, 0, 0))],
        out_shape=[jax.ShapeDtypeStruct((b, NA_HEADS, 3, s, HEAD_DIM), F32),
                   jax.ShapeDtypeStruct((b, NA_HEADS, NA_BIAS_ROWS, GRID_W, GRID_W), F32)],
        scratch_shapes=[pltpu.VMEM((NA_FORMS, NA_Q, NA_KEYS), F32), pltpu.VMEM((NA_FORMS, NA_Q, NA_KEYS), F32)],
        compiler_params=_params(dimension_semantics=("parallel", "parallel")),
    )(heads, heads, heads, bias, out, lse, dout)


GATE_TILE = 256


def _gate_fwd(proj, z, *, gate_col, tt, name):
    _, t, d = z.shape
    nj = d // GATE_TILE
    c0 = gate_col // GATE_TILE

    def body(ga_ref, gb_ref, za_ref, zb_ref, o_ref):
        o_ref[...] = (jax.nn.sigmoid(ga_ref[...]) * za_ref[...] + jax.nn.sigmoid(gb_ref[...]) * zb_ref[...]).astype(BF16)

    return pl.pallas_call(
        body, name=name, grid=(t // tt, nj),
        in_specs=[pl.BlockSpec((tt, GATE_TILE), lambda i, j: (i, c0 + j)),
                  pl.BlockSpec((tt, GATE_TILE), lambda i, j: (i, c0 + nj + j)),
                  pl.BlockSpec((None, tt, GATE_TILE), lambda i, j: (0, i, j)),
                  pl.BlockSpec((None, tt, GATE_TILE), lambda i, j: (1, i, j))],
        out_specs=pl.BlockSpec((tt, GATE_TILE), lambda i, j: (i, j)), out_shape=jax.ShapeDtypeStruct((t, d), BF16),
        compiler_params=_params(dimension_semantics=("parallel", "parallel")),
    )(proj, proj, z, z)


def _gate_bwd(dm, proj, z, *, gate_col, tt, name):
    _, t, d = z.shape
    nj = d // GATE_TILE
    c0 = gate_col // GATE_TILE

    def body(dm_ref, g_ref, z_ref, dz_ref, dg_ref):
        dmv = dm_ref[...]
        sg = jax.nn.sigmoid(g_ref[...])
        dz_ref[...] = (dmv * sg).astype(BF16)
        dg_ref[...] = (dmv * z_ref[...] * sg * (1.0 - sg)).astype(BF16)

    return pl.pallas_call(
        body, name=name, grid=(t // tt, 2 * nj),
        in_specs=[pl.BlockSpec((tt, GATE_TILE), lambda i, j: (i, j % nj)),
                  pl.BlockSpec((tt, GATE_TILE), lambda i, j: (i, c0 + j)),
                  pl.BlockSpec((None, tt, GATE_TILE), lambda i, j: (j // nj, i, j % nj))],
        out_specs=[pl.BlockSpec((None, tt, GATE_TILE), lambda i, j: (j // nj, i, j % nj)),
                   pl.BlockSpec((tt, GATE_TILE), lambda i, j: (i, c0 + j))],
        out_shape=[jax.ShapeDtypeStruct((2, t, d), BF16), jax.ShapeDtypeStruct(proj.shape, BF16)],
        compiler_params=_params(dimension_semantics=("parallel", "parallel")),
    )(dm, proj, z)


def _adamw(w, g, m, v, *, name):
    shape = w.shape
    w2, g2, m2, v2 = (t.reshape(-1, shape[-1]) for t in (w, g, m, v))
    rows, cols = w2.shape
    tr = rows
    for cand in (512, 256, 128, 64, 32, 16, 8):
        if rows % cand == 0:
            tr = cand
            break

    def body(w_ref, g_ref, m_ref, v_ref, d_ref, nm_ref, nv_ref):
        gv = g_ref[...]
        nm = ADAM_B1 * m_ref[...] + (1.0 - ADAM_B1) * gv
        nv = ADAM_B2 * v_ref[...] + (1.0 - ADAM_B2) * (gv * gv)
        m_hat = nm / (1.0 - ADAM_B1 ** ADAM_STEP)
        v_hat = nv / (1.0 - ADAM_B2 ** ADAM_STEP)
        d_ref[...] = -ADAM_LR * (m_hat / (jnp.sqrt(v_hat) + ADAM_EPS) + ADAM_WD * w_ref[...])
        nm_ref[...] = nm
        nv_ref[...] = nv

    blk = pl.BlockSpec((tr, cols), lambda i: (i, 0))
    out = jax.ShapeDtypeStruct((rows, cols), F32)
    res = pl.pallas_call(
        body, name=name, grid=(rows // tr,), in_specs=[blk] * 4, out_specs=[blk] * 3, out_shape=[out] * 3,
        compiler_params=_params(dimension_semantics=("parallel",)),
    )(w2, g2, m2, v2)
    return tuple(t.reshape(shape) for t in res)


def _my_place():
    return lax.axis_index("x"), lax.axis_index("y"), lax.axis_index("c")


def _other_chips(x, y):
    return [(1 - x, y), (x, 1 - y), (1 - x, 1 - y)]


def _chip_no(chip):
    return 2 * chip[0] + chip[1]


def _window(ref, kind, size, chip, lead):
    if kind == "col":
        return ref.at[(*lead, slice(None), pl.ds(pl.multiple_of(chip * size, LANES), size))]
    if kind == "row":
        return ref.at[(*lead, pl.ds(pl.multiple_of(chip * size, BF16_ROWS), size), slice(None))]
    shard = size + HEAD_DIM
    if kind == "win_main":
        return ref.at[(*lead, slice(None), pl.ds(pl.multiple_of(chip * shard + HEAD_DIM * (chip % 2), LANES), size))]
    assert kind == "win_strad"
    return ref.at[(*lead, slice(None), pl.ds(pl.multiple_of(size + 2 * shard * (chip // 2), LANES), LANES))]


def _full_shape(shard, kind):
    _, k, n = shard.shape
    return {"col": (k, N_CHIPS * n), "row": (N_CHIPS * k, n), "win_main": (k, N_CHIPS * (n + HEAD_DIM)),
            "slot": (N_CHIPS, k, n)}[kind]


def _place_own(shard, kind, layer, *, name):
    _, k, n = shard.shape
    tr = _div_tile(k, 512, BF16_ROWS)
    tc = LANES if kind == "win_main" else n
    mine = 2 * lax.axis_index("x") + lax.axis_index("y")
    row0 = mine * (k // tr) if kind == "row" else 0
    col0 = {"col": mine, "row": 0, "slot": 0, "win_main": (mine * (n + HEAD_DIM) + HEAD_DIM * (mine % 2)) // LANES}[kind]
    scalars = jnp.stack([mine, row0, col0]).astype(jnp.int32)

    def body(s_ref, i_ref, o_ref):
        o_ref[...] = i_ref[...]

    if kind == "slot":
        o_spec = pl.BlockSpec((None, tr, tc), lambda i, j, s: (s[0], i, j))
    else:
        o_spec = pl.BlockSpec((tr, tc), lambda i, j, s: (s[1] + i, s[2] + j))
    return pl.pallas_call(
        body, name=name,
        grid_spec=pltpu.PrefetchScalarGridSpec(
            num_scalar_prefetch=1, grid=(k // tr, n // tc),
            in_specs=[pl.BlockSpec((None, tr, tc), lambda i, j, s: (layer, i, j))], out_specs=o_spec),
        out_shape=jax.ShapeDtypeStruct(_full_shape(shard, kind), shard.dtype),
        compiler_params=_params(dimension_semantics=("parallel", "parallel")),
    )(scalars, shard)


class _GatherPlan:
    def __init__(self, src, dst, shapes, kinds, layer, send_sems, recv_sems):
        self.src, self.dst, self.shapes, self.kinds, self.layer = src, dst, shapes, kinds, layer
        self.send_sems, self.recv_sems = send_sems, recv_sems
        self.x, self.y, self.c = _my_place()
        self.mine = 2 * self.x + self.y
        self.chips = _other_chips(self.x, self.y)
        self.n = len(src)

    def half(self, i, chip, half):
        _, k, n = self.shapes[i]
        kind, dst, hk = self.kinds[i], self.dst[i], k // 2
        if kind == "slot":
            return dst.at[chip, pl.ds(pl.multiple_of(half * hk, BF16_ROWS), hk), :]
        if kind == "row":
            return dst.at[pl.ds(pl.multiple_of(chip * k + half * hk, BF16_ROWS), hk), :]
        col0 = chip * n if kind == "col" else chip * (n + HEAD_DIM) + HEAD_DIM * (chip % 2)
        return dst.at[pl.ds(pl.multiple_of(half * hk, BF16_ROWS), hk), pl.ds(pl.multiple_of(col0, LANES), n)]

    def _copy(self, sem, window, to, source=None):
        return pltpu.make_async_remote_copy(src_ref=window if source is None else source, dst_ref=window,
                                            send_sem=self.send_sems.at[sem], recv_sem=self.recv_sems.at[sem],
                                            device_id=to, device_id_type=MESH)

    def sends(self):
        out = []
        for k, chip in enumerate(self.chips):
            for i in range(self.n):
                hk = self.shapes[i][1] // 2
                mine = self.src[i].at[self.layer, pl.ds(pl.multiple_of(self.c * hk, BF16_ROWS), hk), :]
                out.append(self._copy(3 * i + k, self.half(i, self.mine, self.c), (*chip, self.c), source=mine))
        return out

    def arrivals(self):
        return [self._copy(3 * i + k, self.half(i, _chip_no(chip), self.c), (*chip, self.c))
                for k, chip in enumerate(self.chips) for i in range(self.n)]

    def forwards(self, first_sem):
        sibling = (self.x, self.y, 1 - self.c)
        return [self._copy(first_sem + 3 * i + k, self.half(i, _chip_no(chip), self.c), sibling)
                for k, chip in enumerate(self.chips) for i in range(self.n)]

    def forwarded(self, first_sem):
        sibling = (self.x, self.y, 1 - self.c)
        return [self._copy(first_sem + 3 * i + k, self.half(i, _chip_no(chip), 1 - self.c), sibling)
                for k, chip in enumerate(self.chips) for i in range(self.n)]


IN_HBM = pl.BlockSpec(memory_space=pltpu.HBM)
IN_SEM = pl.BlockSpec(memory_space=pltpu.SEMAPHORE)
DATAFLOW = pltpu.SideEffectType.DATAFLOW_SIDE_EFFECTING


def _gather_layer_start(shards, kinds, fulls, layer, after, *, name):
    n_w = len(shards)
    shapes = [sh.shape for sh in shards]

    def body(*refs):
        plan = _GatherPlan(refs[:n_w], refs[n_w:2 * n_w], shapes, kinds, layer, refs[2 * n_w + 1], refs[2 * n_w + 2])
        for cp in plan.sends():
            cp.start()
        token = refs[-1]
        token[...] = jnp.zeros_like(token)

    operands = [pltpu.with_memory_space_constraint(a, pltpu.HBM) for a in (*shards, *fulls)]
    res = pl.pallas_call(
        body, name=name, in_specs=[IN_HBM] * (2 * n_w) + [pl.BlockSpec(memory_space=pl.ANY)],
        out_specs=(IN_SEM, IN_SEM, *([IN_HBM] * (2 * n_w)), pl.BlockSpec(memory_space=pltpu.VMEM)),
        out_shape=(pltpu.SemaphoreType.DMA((3 * n_w,)), pltpu.SemaphoreType.DMA((3 * n_w,)),
                   *[pltpu.HBM(a.shape, a.dtype) for a in operands], jax.ShapeDtypeStruct((8, LANES), F32)),
        input_output_aliases={i: 2 + i for i in range(2 * n_w)},
        compiler_params=pltpu.CompilerParams(has_side_effects=DATAFLOW),
    )(*operands, after)
    return res[0], res[1], res[2:2 + n_w], res[2 + n_w:2 + 2 * n_w], res[-1]


def _gather_layer_wait(send_sems, recv_sems, shards, fulls, kinds, layer, after, *, name):
    n_w = len(shards)
    shapes = [sh.shape for sh in shards]

    def body(*refs):
        plan = _GatherPlan(refs[:n_w], refs[n_w:2 * n_w], shapes, kinds, layer, refs[2 * n_w], refs[2 * n_w + 1])
        for cp in plan.sends():
            cp.wait_send()
        for cp in plan.arrivals():
            cp.wait_recv()

    res = pl.pallas_call(
        body, name=name, in_specs=[IN_HBM] * (2 * n_w) + [IN_SEM, IN_SEM, pl.BlockSpec(memory_space=pl.ANY)],
        out_specs=[IN_HBM] * (2 * n_w), out_shape=[pltpu.HBM(a.shape, a.dtype) for a in (*shards, *fulls)],
        input_output_aliases={i: i for i in range(2 * n_w)},
        compiler_params=pltpu.CompilerParams(has_side_effects=DATAFLOW),
    )(*shards, *fulls, send_sems, recv_sems, after)
    return res[n_w:]


def _gather_layer_forward(shapes, kinds, fulls, *, name):
    n_w = len(fulls)

    def body(*refs):
        plan = _GatherPlan([None] * n_w, refs[n_w:2 * n_w], shapes, kinds, 0, *refs[2 * n_w:])
        passed = plan.forwards(0)
        for cp in passed:
            cp.start()
        for cp in plan.forwarded(0):
            cp.wait_recv()
        for cp in passed:
            cp.wait_send()

    return pl.pallas_call(
        body, name=name, in_specs=[HBM] * n_w, out_specs=[HBM] * n_w,
        out_shape=[jax.ShapeDtypeStruct(f.shape, f.dtype) for f in fulls],
        input_output_aliases={i: i for i in range(n_w)},
        scratch_shapes=[pltpu.SemaphoreType.DMA((3 * n_w,)), pltpu.SemaphoreType.DMA((3 * n_w,))],
    )(*fulls)


def _on_core(layer):
    return (lax.axis_index("c") == layer).astype(jnp.int32).reshape(1)


N_DEVICES = 2 * N_CHIPS


class _ScatterPlan:
    def __init__(self, src, dst, kinds, sizes, layer, send_sems, recv_sems):
        self.src, self.dst, self.kinds, self.sizes, self.layer = src, dst, kinds, sizes, layer
        self.send_sems, self.recv_sems = send_sems, recv_sems
        self.x, self.y, self.c = _my_place()
        self.mine = 2 * self.x + self.y
        self.chips = _other_chips(self.x, self.y)
        self.n = len(src)

    def _copy(self, i, k, window_of, from_chip, from_core, to):
        return pltpu.make_async_remote_copy(src_ref=_window(self.src[i], self.kinds[i], self.sizes[i], window_of, ()),
                                            dst_ref=self.dst[i].at[2 * from_chip + from_core],
                                            send_sem=self.send_sems.at[4 * i + k],
                                            recv_sem=self.recv_sems.at[2 * (4 * i + k) + from_core],
                                            device_id=to, device_id_type=MESH)

    def to_chips(self):
        return [self._copy(i, k, _chip_no(chip), self.mine, self.c, (*chip, self.layer))
                for k, chip in enumerate(self.chips) for i in range(self.n)]

    def to_sibling(self):
        return [self._copy(i, 3, self.mine, self.mine, self.c, (self.x, self.y, self.layer)) for i in range(self.n)]

    def arrivals(self):
        out = [self._copy(i, k, self.mine, _chip_no(chip), core, (*chip, core))
               for k, chip in enumerate(self.chips) for core in (0, 1) for i in range(self.n)]
        return out + [self._copy(i, 3, self.mine, self.mine, 1 - self.layer, (self.x, self.y, 1 - self.layer))
                      for i in range(self.n)]


def _slab_shape(p, kind, size):
    return (N_DEVICES,) + {"col": (p.shape[0], size), "row": (size, p.shape[1]), "win_main": (p.shape[0], size),
                           "win_strad": (p.shape[0], LANES)}[kind]


def _grads_to_chips_start(pairs, kinds, sizes, layer, *, name):
    n_w = len(pairs)

    def body(*refs):
        plan = _ScatterPlan(refs[:n_w], refs[n_w:2 * n_w], kinds, sizes, layer, refs[2 * n_w], refs[2 * n_w + 1])
        for cp in plan.to_chips():
            cp.start()

        @pl.when(plan.c != layer)
        def _():
            for cp in plan.to_sibling():
                cp.start()

        token = refs[-1]
        token[...] = jnp.zeros_like(token)

    slabs = [lax.empty(_slab_shape(p, kind, size), p.dtype) for p, kind, size in zip(pairs, kinds, sizes)]
    operands = [pltpu.with_memory_space_constraint(a, pltpu.HBM) for a in (*pairs, *slabs)]
    res = pl.pallas_call(
        body, name=name, in_specs=[IN_HBM] * (2 * n_w),
        out_specs=(IN_SEM, IN_SEM, *([IN_HBM] * (2 * n_w)), pl.BlockSpec(memory_space=pltpu.VMEM)),
        out_shape=(pltpu.SemaphoreType.DMA((4 * n_w,)), pltpu.SemaphoreType.DMA((8 * n_w,)),
                   *[pltpu.HBM(a.shape, a.dtype) for a in operands], jax.ShapeDtypeStruct((8, LANES), F32)),
        input_output_aliases={i: 2 + i for i in range(2 * n_w)},
        compiler_params=pltpu.CompilerParams(has_side_effects=DATAFLOW),
    )(*operands)
    return res[0], res[1], res[2:2 + n_w], res[2 + n_w:2 + 2 * n_w], res[-1]


def _grads_to_chips_wait(send_sems, recv_sems, pairs, slabs, kinds, sizes, layer, after, *, name):
    n_w = len(pairs)

    def body(*refs):
        plan = _ScatterPlan(refs[:n_w], refs[n_w:2 * n_w], kinds, sizes, layer, refs[2 * n_w], refs[2 * n_w + 1])
        for cp in plan.to_chips():
            cp.wait_send()

        @pl.when(plan.c != layer)
        def _():
            for cp in plan.to_sibling():
                cp.wait_send()

        @pl.when(plan.c == layer)
        def _():
            for cp in plan.arrivals():
                cp.wait_recv()

    res = pl.pallas_call(
        body, name=name, in_specs=[IN_HBM] * (2 * n_w) + [IN_SEM, IN_SEM, pl.BlockSpec(memory_space=pl.ANY)],
        out_specs=[IN_HBM] * (2 * n_w), out_shape=[pltpu.HBM(a.shape, a.dtype) for a in (*pairs, *slabs)],
        input_output_aliases={i: i for i in range(2 * n_w)},
        compiler_params=pltpu.CompilerParams(has_side_effects=DATAFLOW),
    )(*pairs, *slabs, send_sems, recv_sems, after)
    return res[:n_w], res[n_w:]


def _sum_slabs(slabs, pair, kind, size, layer, into, *, name):
    n_s, k, n = slabs.shape
    tr = _div_tile(k, 512, BF16_ROWS)
    tc = n if kind in ("col", "row") else LANES
    x, y, _ = _my_place()
    mine = 2 * x + y
    shard = size + HEAD_DIM
    row0 = mine * (k // tr) if kind == "row" else 0
    col0 = {"col": mine, "row": 0, "win_main": (mine * shard + HEAD_DIM * (mine % 2)) // LANES,
            "win_strad": (size + 2 * shard * (mine // 2)) // LANES}[kind]
    on = _on_core(layer)[0]
    scalars = jnp.stack([2 * mine + layer, row0 * on, col0 * on, on]).astype(jnp.int32)

    def body(s_ref, slab_ref, own_ref, *rest):
        o_ref = rest[-1]
        me = s_ref[0]

        @pl.when(s_ref[3] == 1)
        def _():
            acc = jnp.zeros(o_ref.shape, F32)
            for i in range(n_s):
                acc = acc + jnp.where(me == i, own_ref[...], slab_ref[i]).astype(F32)
            o_ref[...] = acc

    operands = [scalars, slabs, pair] + ([] if into is None else [into])
    return pl.pallas_call(
        body, name=name,
        grid_spec=pltpu.PrefetchScalarGridSpec(
            num_scalar_prefetch=1, grid=(k // tr, n // tc),
            in_specs=[pl.BlockSpec((n_s, tr, tc), lambda i, j, s: (0, i * s[3], j * s[3])),
                      pl.BlockSpec((tr, tc), lambda i, j, s: (s[1] + i * s[3], s[2] + j * s[3]))]
            + ([] if into is None else [HBM]),
            out_specs=pl.BlockSpec((None, tr, tc), lambda i, j, s: (layer, i * s[3], j * s[3]))),
        out_shape=jax.ShapeDtypeStruct((2, k, n), F32),
        input_output_aliases={} if into is None else {3: 0},
        compiler_params=_params(dimension_semantics=("arbitrary", "arbitrary")),
    )(*operands)


def _exchange_layers(bufs, *, name):
    n_w = len(bufs)

    def body(*refs):
        dst = refs[n_w:2 * n_w]
        send_sems, recv_sems = refs[2 * n_w:]
        x, y, c = _my_place()

        def copy(i, layer):
            return pltpu.make_async_remote_copy(src_ref=dst[i].at[layer], dst_ref=dst[i].at[layer], send_sem=send_sems.at[i],
                                                recv_sem=recv_sems.at[i], device_id=(x, y, 1 - c), device_id_type=MESH)

        sends = [copy(i, c) for i in range(n_w)]
        for cp in sends:
            cp.start()
        for i in range(n_w):
            copy(i, 1 - c).wait_recv()
        for cp in sends:
            cp.wait_send()

    return pl.pallas_call(
        body, name=name, in_specs=[HBM] * n_w, out_specs=[HBM] * n_w,
        out_shape=[jax.ShapeDtypeStruct(b.shape, b.dtype) for b in bufs],
        input_output_aliases={i: i for i in range(n_w)},
        scratch_shapes=[pltpu.SemaphoreType.DMA((n_w,)), pltpu.SemaphoreType.DMA((n_w,))],
    )(*bufs)


def _all_sum_small(v, *, name):
    r = v.shape[0]
    relations = [(dx, dy, dc) for dx in (0, 1) for dy in (0, 1) for dc in (0, 1)][1:]

    def body(v_ref, o_ref, buf, send_sems, recv_sems):
        x, y, c = _my_place()
        me = 4 * x + 2 * y + c
        buf[me] = v_ref[...]
        peers = [(x + dx - 2 * x * dx, y + dy - 2 * y * dy, c + dc - 2 * c * dc) for dx, dy, dc in relations]

        def copy(k, slot):
            return pltpu.make_async_remote_copy(src_ref=v_ref, dst_ref=buf.at[slot], send_sem=send_sems.at[k],
                                                recv_sem=recv_sems.at[k], device_id=peers[k], device_id_type=MESH)

        sends = [copy(k, me) for k in range(len(relations))]
        for cp in sends:
            cp.start()
        for k, (px, py, pc) in enumerate(peers):
            copy(k, 4 * px + 2 * py + pc).wait_recv()
        for cp in sends:
            cp.wait_send()
        acc = buf[0]
        for i in range(1, 8):
            acc = acc + buf[i]
        o_ref[...] = acc

    vm = pl.BlockSpec(memory_space=pltpu.VMEM)
    return pl.pallas_call(
        body, name=name, in_specs=[vm], out_specs=vm, out_shape=jax.ShapeDtypeStruct((r, LANES), F32),
        scratch_shapes=[pltpu.VMEM((8, r, LANES), F32), pltpu.SemaphoreType.DMA((7,)), pltpu.SemaphoreType.DMA((7,))],
    )(v)


SHARDED = (("ffn1_w_up", "col"), ("ffn1_w_down", "row"), ("w_in", "win"), ("w_branch_a", "col"),
           ("w_branch_b", "col"), ("w_out", "row"), ("ffn2_w_up", "col"), ("ffn2_w_down", "row"))
REPLICATED = ("ffn1_norm", "mix_norm", "na_rel_bias", "ffn2_norm", "final_norm")


def _weight_pieces(w):
    even = lax.axis_index("y") == 0
    shards, kinds, names = [], [], []
    for name, kind in SHARDED:
        wb = w[name].astype(BF16)
        if kind == "win":
            main = wb.shape[-1] - HEAD_DIM
            assert main % LANES == 0
            zeros = jnp.zeros(wb.shape[:-1] + (HEAD_DIM,), BF16)
            shards += [jnp.where(even, wb[..., :main], wb[..., HEAD_DIM:]),
                       jnp.where(even, jnp.concatenate([wb[..., main:], zeros], -1),
                                 jnp.concatenate([zeros, wb[..., :HEAD_DIM]], -1))]
            kinds += ["win_main", "slot"]
            names += [name, name + "_strad"]
        else:
            shards.append(wb)
            kinds.append(kind)
            names.append(name)
    return names, kinds, shards


def _finish_w_in(full):
    full = dict(full)
    strad = full.pop("w_in_strad")
    main = full["w_in"].shape[1] // N_CHIPS - HEAD_DIM
    for i in range(N_CHIPS // 2):
        lo = main + 2 * (main + HEAD_DIM) * i
        full["w_in"] = full["w_in"].at[:, lo:lo + LANES].set(strad[2 * i] + strad[2 * i + 1])
    return full


def _scatter_pieces(shards):
    names, kinds, sizes, srcs = [], [], [], []
    for name, kind in SHARDED:
        shp = shards[name].shape
        if kind == "win":
            names += [name, name + "_strad"]
            kinds += ["win_main", "win_strad"]
            sizes += [shp[2] - HEAD_DIM] * 2
            srcs += [name, name]
        else:
            names.append(name)
            kinds.append(kind)
            sizes.append(shp[1] if kind == "row" else shp[2])
            srcs.append(name)
    return names, kinds, sizes, srcs


def _finish_weight_grads(reduced, names, tag):
    out = dict(zip(names, _exchange_layers(reduced, name=f"{tag}_layers")))
    if "w_in_strad" in out:
        strad = out.pop("w_in_strad")
        even = lax.axis_index("y") == 0
        out["w_in"] = jnp.where(even, jnp.concatenate([out["w_in"], strad[..., :HEAD_DIM]], -1),
                                jnp.concatenate([strad[..., HEAD_DIM:], out["w_in"]], -1))
    return out


class _Grads:
    def __init__(self):
        self.arrays = {}

    def put(self, weight, layer, a, b, *, cols=None, col_off=0, **kw):
        self.arrays[weight, layer] = _mm(a, b, mode="tn", out_dtype=BF16, out_cols=cols, out_col_off=col_off,
                                         out_into=self.arrays.get((weight, layer)), **kw)


def _ffn_fwd(x, h, w_up, w_down, tag):
    t, d = x.shape
    f = w_down.shape[0]
    a, gate, up = _mm_swiglu_fwd(h, w_up, tm=_div_tile(t, ROWS_NARROW, 8), tn=MXU_N, name=f"{tag}_up")
    x_out = _mm(a, w_down, mode="nn", out_dtype=F32, tm=_div_tile(t, ROWS_WIDE, 8), tn=d, tk=f, alpha=0.5, res=x, name=f"{tag}_down")
    return x_out, (x, h, a, gate, up)


def _ffn_bwd(dx, dxb, saved, norm_g, w_up, w_down, layer, grads, wname, tag, scatter):
    x, h, a, gate, up = saved
    t, d = x.shape
    f = w_down.shape[0]
    tn = _div_tile(f, 1408)
    grads.put(f"{wname}_w_down", layer, a, dxb, tm=tn, tn=d, tk=1024, alpha=0.5, name=f"{tag}_dwd")
    d_gate, d_up = _mm_swiglu_bwd(dxb, w_down, gate, up, alpha=0.5, tm=_div_tile(t, ROWS_NARROW, 8), tn=MXU_N, name=f"{tag}_da")
    grads.put(f"{wname}_w_up", layer, h, d_gate, cols=2 * f, tm=d, tn=tn, tk=1024, name=f"{tag}_dwg")
    grads.put(f"{wname}_w_up", layer, h, d_up, cols=2 * f, col_off=f // tn, tm=d, tn=tn, tk=1024, name=f"{tag}_dwu")
    started = scatter(layer, [f"{wname}_w_up", f"{wname}_w_down"])
    dh = _mm(d_gate, w_up, mode="nt", out_dtype=F32, tm=_div_tile(t, ROWS_WIDE, 8), tn=d, tk=f, name=f"{tag}_dh1")
    dh = _mm(d_up, w_up, mode="nt", out_dtype=F32, tm=_div_tile(t, ROWS_WIDE, 8), tn=d, tk=f, b_k_off=1, res=dh, name=f"{tag}_dh2")
    return _rms_bwd(dh, x, norm_g + started, dx, tt=512, name=f"{tag}_dnorm")


def _to_heads(y, b, n_heads):
    t, w = y.shape
    return y.reshape(b, t // b, n_heads, HEAD_DIM).transpose(0, 2, 1, 3)


def _from_heads(y):
    b, n, s, hd = y.shape
    return y.transpose(0, 2, 1, 3).reshape(b * s, n * hd)


N_QKV = 3 * (DIL_HEADS + NA_HEADS) * HEAD_DIM


def _mixer_fwd(x, b, norm_g, full, bias, tabs, tag):
    t, d = x.shape
    s = t // b
    n_in = full["w_in"].shape[1]
    h = _rms_fwd(x, norm_g, tt=512, name=f"{tag}_norm")
    proj = _mm(h, full["w_in"], mode="nn", out_dtype=F32, tm=_div_tile(t, ROWS_NARROW, 8), tn=MXU_N, tk=d, name=f"{tag}_in")
    heads = _split_heads(proj.reshape(b, s, -1), *tabs, n_pairs=N_QKV // LANES, rot_pairs=DIL_HEADS,
                         scale_ranges=((0, DIL_HEADS // 2), (3 * DIL_HEADS // 2, (3 * DIL_HEADS + NA_HEADS) // 2)),
                         name=f"{tag}_heads")
    ya, lse_a = _dil_attn_fwd(heads, name=f"{tag}_dil")
    yb, lse_b = _na_attn_fwd(heads, bias, first=3 * DIL_HEADS, name=f"{tag}_na")
    ya2, yb2 = _from_heads(ya), _from_heads(yb)
    z = _mm(ya2, full["w_branch_a"], mode="nn", out_dtype=F32, tm=_div_tile(t, ROWS_NARROW, 8), tn=MXU_N, tk=ya2.shape[1],
            out_slab=(0, 2), name=f"{tag}_za")
    z = _mm(yb2, full["w_branch_b"], mode="nn", out_dtype=F32, tm=_div_tile(t, ROWS_NARROW, 8), tn=MXU_N, tk=yb2.shape[1],
            out_slab=(1, 2), out_into=z, name=f"{tag}_zb")
    merged = _gate_fwd(proj, z, gate_col=N_QKV, tt=1024, name=f"{tag}_gate")
    x_out = _mm(merged, full["w_out"], mode="nn", out_dtype=F32, tm=_div_tile(t, ROWS_NARROW, 8), tn=MXU_N, tk=d, res=x, name=f"{tag}_out")
    return x_out, (x, h, proj, heads, ya, lse_a, yb, lse_b, ya2, yb2, z, merged)


def _mixer_bwd(dx, dob, b, saved, norm_g, full, layer, bias, tabs, grads, tag, scatter):
    x, h, proj, heads, ya, lse_a, yb, lse_b, ya2, yb2, z, merged = saved
    t, d = x.shape
    s = t // b
    n_in = full["w_in"].shape[1]
    grads.put("w_out", layer, merged, dob, tm=d, tn=d, tk=1024, name=f"{tag}_dwo")
    dm = _mm(dob, full["w_out"], mode="nt", out_dtype=F32, tm=_div_tile(t, ROWS_NARROW, 8), tn=MXU_N, tk=d, name=f"{tag}_dm")
    dz, dproj = _gate_bwd(dm, proj, z, gate_col=N_QKV, tt=1024, name=f"{tag}_dgate")
    grads.put("w_branch_a", layer, ya2, dz, b_sel=0, tm=ya2.shape[1], tn=d, tk=1024, name=f"{tag}_dwa")
    grads.put("w_branch_b", layer, yb2, dz, b_sel=1, tm=yb2.shape[1], tn=d, tk=1024, name=f"{tag}_dwb")
    started = scatter(layer, ["w_out", "w_branch_a", "w_branch_b"])
    dya = _mm(dz, full["w_branch_a"], mode="nt", out_dtype=F32, tm=_div_tile(t, ROWS_NARROW, 8), tn=MXU_N, tk=d, a_sel=0, name=f"{tag}_dya")
    dyb = _mm(dz, full["w_branch_b"], mode="nt", out_dtype=F32, tm=_div_tile(t, ROWS_NARROW, 8), tn=MXU_N, tk=d, a_sel=1, name=f"{tag}_dyb")
    d_dil = _dil_attn_bwd(heads, ya, lse_a, _to_heads(dya, b, DIL_GROUP_HEADS), name=f"{tag}_ddil")
    d_na, d_bias = _na_attn_bwd(heads, bias, yb, lse_b, _to_heads(dyb, b, NA_HEADS), first=3 * DIL_HEADS, name=f"{tag}_dna")
    dproj = _merge_heads(d_dil, *tabs, heads_per_row=DIL_GROUP_HEADS, rot_pairs=DIL_HEADS, scale_pairs=DIL_HEADS // 2,
                         dilated=True, out_cols=n_in, tile_off=0, into=dproj.reshape(b, s, n_in), name=f"{tag}_dheads_a")
    dproj = _merge_heads(d_na, *tabs, heads_per_row=NA_HEADS, rot_pairs=0, scale_pairs=NA_HEADS // 2, dilated=False,
                         out_cols=n_in, tile_off=3 * DIL_HEADS // 2, into=dproj, name=f"{tag}_dheads_b").reshape(t, n_in)
    grads.put("w_in", layer, h, dproj, tm=_div_tile(d, 512), tn=_div_tile(n_in, 2944), tk=1024, name=f"{tag}_dwin")
    started = started + scatter(layer, ["w_in"])
    dh = _mm(dproj, full["w_in"], mode="nt", out_dtype=F32, tm=_div_tile(t, ROWS_WIDE, 8), tn=d, tk=_div_tile(n_in, 2944), name=f"{tag}_dh")
    dx_in, dxb_in, d_norm = _rms_bwd(dh, x, norm_g + started, dx, tt=512, name=f"{tag}_dnorm")
    d_rb = _na_collapse_bias(d_bias, name=f"{tag}_dbias")
    return dx_in, dxb_in, d_norm, d_rb


def kernel(x, ffn1_norm, ffn1_w_up, ffn1_w_down, mix_norm, w_in, na_rel_bias, w_branch_a, w_branch_b, w_out, ffn2_norm, ffn2_w_up, ffn2_w_down, final_norm, loss_target, m_ffn1_norm, m_ffn1_w_up, m_ffn1_w_down, m_mix_norm, m_w_in, m_na_rel_bias, m_w_branch_a, m_w_branch_b, m_w_out, m_ffn2_norm, m_ffn2_w_up, m_ffn2_w_down, m_final_norm, v_ffn1_norm, v_ffn1_w_up, v_ffn1_w_down, v_mix_norm, v_w_in, v_na_rel_bias, v_w_branch_a, v_w_branch_b, v_w_out, v_ffn2_norm, v_ffn2_w_up, v_ffn2_w_down, v_final_norm):
    w = dict(ffn1_norm=ffn1_norm, ffn1_w_up=ffn1_w_up, ffn1_w_down=ffn1_w_down, mix_norm=mix_norm, w_in=w_in,
             na_rel_bias=na_rel_bias, w_branch_a=w_branch_a, w_branch_b=w_branch_b, w_out=w_out, ffn2_norm=ffn2_norm,
             ffn2_w_up=ffn2_w_up, ffn2_w_down=ffn2_w_down, final_norm=final_norm)
    mom = dict(ffn1_norm=m_ffn1_norm, ffn1_w_up=m_ffn1_w_up, ffn1_w_down=m_ffn1_w_down, mix_norm=m_mix_norm, w_in=m_w_in,
               na_rel_bias=m_na_rel_bias, w_branch_a=m_w_branch_a, w_branch_b=m_w_branch_b, w_out=m_w_out,
               ffn2_norm=m_ffn2_norm, ffn2_w_up=m_ffn2_w_up, ffn2_w_down=m_ffn2_w_down, final_norm=m_final_norm)
    var = dict(ffn1_norm=v_ffn1_norm, ffn1_w_up=v_ffn1_w_up, ffn1_w_down=v_ffn1_w_down, mix_norm=v_mix_norm, w_in=v_w_in,
               na_rel_bias=v_na_rel_bias, w_branch_a=v_w_branch_a, w_branch_b=v_w_branch_b, w_out=v_w_out,
               ffn2_norm=v_ffn2_norm, ffn2_w_up=v_ffn2_w_up, ffn2_w_down=v_ffn2_w_down, final_norm=v_final_norm)
    b, s, d = x.shape
    t = b * s
    depth = ffn1_norm.shape[0]
    assert depth == 2, "core c of a chip sends / reduces layer c"
    shards = {name: w[name] for name, _ in SHARDED}

    names, kinds, pieces = _weight_pieces(w)
    by_layer = [[p[l:l + 1] for p in pieces] for l in range(depth)]
    own = [[_place_own(p, kind, 0, name=f"own{l}_{nm}") for nm, kind, p in zip(names, kinds, by_layer[l])] for l in range(depth)]
    full = [{}, {}]

    def gather_start(layer, group, after, tag):
        idx = [i for i, nm in enumerate(names) if nm in group]
        pick = lambda seq: [seq[i] for i in idx]
        *state, token = _gather_layer_start(pick(by_layer[layer]), pick(kinds), pick(own[layer]), 0, after, name=f"{tag}_start")
        return (layer, idx, tag, state), token[:1, :1]

    def gather_finish(started, after):
        layer, idx, tag, state = started
        pick = lambda seq: [seq[i] for i in idx]
        landed = _gather_layer_wait(*state, pick(kinds), 0, after, name=f"{tag}_wait")
        done = _gather_layer_forward([by_layer[layer][i].shape for i in idx], pick(kinds), landed, name=f"{tag}_forward")
        full[layer].update(zip(pick(names), done))
        return done[0]

    ffn1, mixer, ffn2 = names[:2], names[2:7], names[7:]
    assert mixer[0] == "w_in" and ffn2[0] == "ffn2_w_up", names
    xc = x.reshape(t, d)
    l0_ffn1, token_ffn1 = gather_start(0, ffn1, xc, "gather_l0_ffn1")
    tabs = _rope_tables(s)
    bias = _na_expand_bias(na_rel_bias, name="na_bias")

    saved = []
    h = _rms_fwd(xc, ffn1_norm[:1] + token_ffn1, tt=512, name="l0_ffn1_norm")
    landed = gather_finish(l0_ffn1, h)
    l0_mixer, token_mixer = gather_start(0, mixer, landed, "gather_l0_mixer")
    xc, s1 = _ffn_fwd(xc, h + token_mixer.astype(BF16), full[0]["ffn1_w_up"], full[0]["ffn1_w_down"], "l0_ffn1")
    landed = gather_finish(l0_mixer, xc)
    full[0] = _finish_w_in(full[0])
    l0_ffn2, token_ffn2 = gather_start(0, ffn2, landed, "gather_l0_ffn2")
    layer1, token_layer1 = gather_start(1, names, landed, "gather_l1")
    xc, s2 = _mixer_fwd(xc, b, mix_norm[:1] + token_ffn2 + token_layer1, full[0], bias[0], tabs, "l0_mix")
    gather_finish(l0_ffn2, xc)
    xc, s3 = _ffn_fwd(xc, _rms_fwd(xc, ffn2_norm[:1], tt=512, name="l0_ffn2_norm"), full[0]["ffn2_w_up"], full[0]["ffn2_w_down"],
                      "l0_ffn2")
    saved.append((s1, s2, s3))
    gather_finish(layer1, xc)
    full[1] = _finish_w_in(full[1])
    for l in range(1, depth):
        xc, s1 = _ffn_fwd(xc, _rms_fwd(xc, ffn1_norm[l:l + 1], tt=512, name=f"l{l}_ffn1_norm"), full[l]["ffn1_w_up"],
                          full[l]["ffn1_w_down"], f"l{l}_ffn1")
        xc, s2 = _mixer_fwd(xc, b, mix_norm[l:l + 1], full[l], bias[l], tabs, f"l{l}_mix")
        xc, s3 = _ffn_fwd(xc, _rms_fwd(xc, ffn2_norm[l:l + 1], tt=512, name=f"l{l}_ffn2_norm"), full[l]["ffn2_w_up"],
                          full[l]["ffn2_w_down"], f"l{l}_ffn2")
        saved.append((s1, s2, s3))

    dx, dxb, d_final, loss_part = _final_loss(xc, final_norm.reshape(1, d), loss_target.reshape(t, d), tt=512, name="final_loss")
    grads = _Grads()
    piece_names, piece_kinds, piece_sizes, piece_srcs = _scatter_pieces(shards)
    scattered = []

    def scatter(layer, weights):
        tag = f"grads{layer}_{weights[0]}"
        idx = [i for i, src in enumerate(piece_srcs) if src in weights]
        pick = lambda seq: [seq[i] for i in idx]
        *state, token = _grads_to_chips_start([grads.arrays[src, layer] for src in pick(piece_srcs)], pick(piece_kinds),
                                              pick(piece_sizes), layer, name=f"{tag}_to_chips_start")
        scattered.append((layer, idx, state))
        return token[:1, :1]
    small = {name: [None] * depth for name in REPLICATED[:-1]}
    for l in reversed(range(depth)):
        s1, s2, s3 = saved[l]
        dx, dxb, small["ffn2_norm"][l] = _ffn_bwd(dx, dxb, s3, ffn2_norm[l:l + 1], full[l]["ffn2_w_up"], full[l]["ffn2_w_down"],
                                                  l, grads, "ffn2", f"l{l}_ffn2", scatter)
        dx, dxb, small["mix_norm"][l], small["na_rel_bias"][l] = _mixer_bwd(
            dx, dxb, b, s2, mix_norm[l:l + 1], full[l], l, bias[l], tabs, grads, f"l{l}_mix", scatter)
        dx, dxb, small["ffn1_norm"][l] = _ffn_bwd(dx, dxb, s1, ffn1_norm[l:l + 1], full[l]["ffn1_w_up"], full[l]["ffn1_w_down"],
                                                  l, grads, "ffn1", f"l{l}_ffn1", scatter)
    grad_x = dx.reshape(b, s, d)
    reduced = [None] * len(piece_names)

    def arrive(group, after):
        layer, idx, state = group
        state = _grads_to_chips_wait(*state, [piece_kinds[i] for i in idx], [piece_sizes[i] for i in idx], layer, after,
                                     name=f"grads{layer}_{piece_names[idx[0]]}_to_chips_wait")
        for i, p, sl in zip(idx, *state):
            reduced[i] = _sum_slabs(sl, p, piece_kinds[i], piece_sizes[i], layer, reduced[i],
                                    name=f"grads{layer}_sum_{piece_names[i]}")
        return idx

    for group in scattered[:-1]:
        arrive(group, dx)
    late = scattered[-1][1]
    early = [i for i in range(len(piece_names)) if i not in late]
    g_out = _finish_weight_grads([reduced[i] for i in early], [piece_names[i] for i in early], "grads_early")

    parts = [jnp.stack(small[name]).reshape(-1) for name in REPLICATED[:-1]] + [d_final.reshape(-1), loss_part[0, :1]]
    sizes = [v.shape[0] for v in parts]
    flat = jnp.concatenate(parts)
    flat = jnp.pad(flat, (0, -flat.shape[0] % (8 * LANES)))
    small_sum = _all_sum_small(flat.reshape(-1, LANES), name="small_all_sum").reshape(-1)
    off = 0
    for name, n in zip(REPLICATED, sizes[:-1]):
        g_out[name] = small_sum[off:off + n].reshape(w[name].shape)
        off += n
    loss = small_sum[off]

    names = list(w)
    delta, new_m, new_v = {}, {}, {}
    for name in [n for n in names if n in g_out]:
        delta[name], new_m[name], new_v[name] = _adamw(w[name], g_out[name], mom[name], var[name], name=f"adamw_{name}")
    arrive(scattered[-1], delta["w_in"])
    g_out.update(_finish_weight_grads([reduced[i] for i in late], [piece_names[i] for i in late], "grads_late"))
    for name in [n for n in names if n not in delta]:
        delta[name], new_m[name], new_v[name] = _adamw(w[name], g_out[name], mom[name], var[name], name=f"adamw_{name}")
    return (loss, grad_x, *[g_out[n] for n in names], *[delta[n] for n in names], *[new_m[n] for n in names],
            *[new_v[n] for n in names])
```

```python
import functools

import numpy as np
import jax
import jax.numpy as jnp
from jax import lax
from jax.experimental import pallas as pl
from jax.experimental.pallas import tpu as pltpu

F32, BF16 = jnp.float32, jnp.bfloat16
MESH = pl.DeviceIdType.MESH

HEAD_DIM = 64
DILATIONS = (1, 4, 16)
DIL_HALF = 64
DIL_GROUP_HEADS = 4
DIL_HEADS = 12
NA_HEADS = 8
GRID_W = 64
NA_ROWS = 8
NA_COLS = 16
ROPE_THETA = 10000.0
RMS_EPS = 1e-6
NEG_INF = -1e30
ADAM_LR, ADAM_B1, ADAM_B2, ADAM_EPS, ADAM_WD, ADAM_STEP = 0.001, 0.9, 0.999, 1e-08, 0.01, 10
QK_SCALE = HEAD_DIM ** -0.5

N_CHIPS = 4
LANES = 128
BF16_ROWS = 16
VMEM_LIMIT = 56 * 1024 * 1024
MXU_N = 256
ROWS_NARROW = 2048
ROWS_WIDE = 512
BLOCKS_IN_FLIGHT = 2

_NN = (((1,), (0,)), ((), ()))
_NT = (((1,), (1,)), ((), ()))
_TN = (((0,), (0,)), ((), ()))

HBM = pl.BlockSpec(memory_space=pl.ANY)


def _params(**kw):
    return pltpu.CompilerParams(vmem_limit_bytes=VMEM_LIMIT, **kw)


def _dot(a, b, dims):
    return lax.dot_general(a, b, dims, preferred_element_type=F32)


def _div_tile(n, cap, mult=LANES):
    best = None
    for t in range(mult, min(n, cap) + 1, mult):
        if n % t == 0:
            best = t
    return n if best is None else best


def _stacked(block, index, sel):
    if sel is None:
        return pl.BlockSpec(block, index)
    return pl.BlockSpec((None,) + block, lambda *g: (sel,) + index(*g))


def _mm(a, b, *, mode, out_dtype, tm, tn, tk, name, alpha=1.0, res=None, a_sel=None, b_sel=None, b_k_off=0,
        out_slab=None, out_cols=None, out_col_off=0, out_into=None):
    a2, b2 = a.shape[-2:], b.shape[-2:]
    if mode == "nn":
        (m, k), n = a2, b2[1]
        a_spec = _stacked((tm, tk), lambda i, j, kk: (i, kk), a_sel)
        b_spec = _stacked((tk, tn), lambda i, j, kk: (kk + b_k_off, j), b_sel)
        dims = _NN
    elif mode == "nt":
        (m, k), n = a2, b2[0]
        a_spec = _stacked((tm, tk), lambda i, j, kk: (i, kk), a_sel)
        b_spec = _stacked((tn, tk), lambda i, j, kk: (j, kk + b_k_off), b_sel)
        dims = _NT
    else:
        (k, m), n = a2, b2[1]
        a_spec = _stacked((tk, tm), lambda i, j, kk: (kk, i), a_sel)
        b_spec = _stacked((tk, tn), lambda i, j, kk: (kk + b_k_off, j), b_sel)
        dims = _TN
    assert m % tm == 0 and n % tn == 0 and k % tk == 0, (name, a.shape, b.shape)
    nk = k // tk
    has_res = res is not None
    if out_slab is None:
        o_spec = pl.BlockSpec((tm, tn), lambda i, j, kk: (i, j + out_col_off))
        out_shape = jax.ShapeDtypeStruct((m, n if out_cols is None else out_cols), out_dtype)
    else:
        o_spec = _stacked((tm, tn), lambda i, j, kk: (i, j + out_col_off), out_slab[0])
        out_shape = jax.ShapeDtypeStruct((out_slab[1], m, n if out_cols is None else out_cols), out_dtype)
    r_spec = pl.BlockSpec((tm, tn), lambda i, j, kk: (i, j))
    n_in = 2 + has_res + (out_into is not None)

    def body(*refs):
        a_ref, b_ref = refs[0], refs[1]
        r_ref = refs[2] if has_res else None
        o_ref = refs[n_in]
        p = _dot(a_ref[...], b_ref[...], dims)

        def finish(acc):
            y = acc * alpha if alpha != 1.0 else acc
            if has_res:
                y = y + r_ref[...].astype(F32)
            o_ref[...] = y.astype(o_ref.dtype)

        if nk == 1:
            finish(p)
        else:
            acc_ref = refs[n_in + 1]
            kk = pl.program_id(2)

            @pl.when(kk == 0)
            def _():
                acc_ref[...] = p

            @pl.when(kk > 0)
            def _():
                acc_ref[...] += p

            @pl.when(kk == nk - 1)
            def _():
                finish(acc_ref[...])

    operands = [a, b] + ([res] if has_res else [])
    in_specs = [a_spec, b_spec] + ([r_spec] if has_res else [])
    aliases = {}
    if out_into is not None:
        aliases = {len(operands): 0}
        operands.append(out_into)
        in_specs.append(HBM)
    return pl.pallas_call(
        body, name=name, grid=(m // tm, n // tn, nk), in_specs=in_specs, out_specs=o_spec, out_shape=out_shape,
        scratch_shapes=[pltpu.VMEM((tm, tn), F32)] if nk > 1 else [], input_output_aliases=aliases,
        compiler_params=_params(dimension_semantics=("parallel", "parallel", "arbitrary")),
    )(*operands)


def _mm_swiglu_fwd(h, w_up, *, tm, tn, name):
    m, k = h.shape
    n = w_up.shape[1] // 2
    h_spec = pl.BlockSpec((tm, k), lambda i, j: (i, 0))
    wg_spec = pl.BlockSpec((k, tn), lambda i, j: (0, j))
    wu_spec = pl.BlockSpec((k, tn), lambda i, j: (0, j + n // tn))
    o_spec = pl.BlockSpec((tm, tn), lambda i, j: (i, j))

    def body(h_ref, wg_ref, wu_ref, a_ref, g_ref, u_ref):
        hb = h_ref[...]
        g = _dot(hb, wg_ref[...], _NN)
        u = _dot(hb, wu_ref[...], _NN)
        a_ref[...] = (g * jax.nn.sigmoid(g) * u).astype(BF16)
        g_ref[...] = g.astype(BF16)
        u_ref[...] = u.astype(BF16)

    out = jax.ShapeDtypeStruct((m, n), BF16)
    return pl.pallas_call(
        body, name=name, grid=(m // tm, n // tn), in_specs=[h_spec, wg_spec, wu_spec],
        out_specs=[o_spec] * 3, out_shape=[out] * 3,
        compiler_params=_params(dimension_semantics=("parallel", "parallel")),
    )(h, w_up, w_up)


def _mm_swiglu_bwd(dy, w_down, gate, up, *, alpha, tm, tn, name):
    m, k = dy.shape
    n = w_down.shape[0]
    dy_spec = pl.BlockSpec((tm, k), lambda i, j: (i, 0))
    w_spec = pl.BlockSpec((tn, k), lambda i, j: (j, 0))
    o_spec = pl.BlockSpec((tm, tn), lambda i, j: (i, j))

    def body(dy_ref, w_ref, g_ref, u_ref, dg_ref, du_ref):
        da = _dot(dy_ref[...], w_ref[...], _NT) * alpha
        g = g_ref[...].astype(F32)
        u = u_ref[...].astype(F32)
        sg = jax.nn.sigmoid(g)
        dg_ref[...] = (da * u * (sg * (1.0 + g * (1.0 - sg)))).astype(BF16)
        du_ref[...] = (da * (g * sg)).astype(BF16)

    out = jax.ShapeDtypeStruct((m, n), BF16)
    return pl.pallas_call(
        body, name=name, grid=(m // tm, n // tn), in_specs=[dy_spec, w_spec, o_spec, o_spec],
        out_specs=[o_spec] * 2, out_shape=[out] * 2,
        compiler_params=_params(dimension_semantics=("parallel", "parallel")),
    )(dy, w_down, gate, up)


def _rms_fwd(x, g, *, tt, name):
    t, d = x.shape

    def body(x_ref, g_ref, h_ref):
        xv = x_ref[...]
        rstd = lax.rsqrt(jnp.mean(xv * xv, axis=1, keepdims=True) + RMS_EPS)
        h_ref[...] = (xv * rstd * g_ref[...]).astype(BF16)

    return pl.pallas_call(
        body, name=name, grid=(t // tt,),
        in_specs=[pl.BlockSpec((tt, d), lambda i: (i, 0)), pl.BlockSpec((1, d), lambda i: (0, 0))],
        out_specs=pl.BlockSpec((tt, d), lambda i: (i, 0)), out_shape=jax.ShapeDtypeStruct((t, d), BF16),
        compiler_params=_params(dimension_semantics=("parallel",)),
    )(x, g)


def _rms_bwd(dh, x, g, dres, *, tt, name):
    t, d = x.shape

    def body(dh_ref, x_ref, g_ref, r_ref, dx_ref, dxb_ref, dg_ref):
        xv = x_ref[...]
        rstd = lax.rsqrt(jnp.mean(xv * xv, axis=1, keepdims=True) + RMS_EPS)
        xhat = xv * rstd
        dhv = dh_ref[...]
        dxhat = dhv * g_ref[...]
        dx = r_ref[...] + rstd * (dxhat - xhat * jnp.mean(dxhat * xhat, axis=1, keepdims=True))
        dx_ref[...] = dx
        dxb_ref[...] = dx.astype(BF16)

        @pl.when(pl.program_id(0) == 0)
        def _():
            dg_ref[...] = jnp.zeros_like(dg_ref)

        dg_ref[...] += jnp.sum(dhv * xhat, axis=0, keepdims=True)

    row = pl.BlockSpec((tt, d), lambda i: (i, 0))
    vec = pl.BlockSpec((1, d), lambda i: (0, 0))
    return pl.pallas_call(
        body, name=name, grid=(t // tt,), in_specs=[row, row, vec, row], out_specs=[row, row, vec],
        out_shape=[jax.ShapeDtypeStruct((t, d), F32), jax.ShapeDtypeStruct((t, d), BF16), jax.ShapeDtypeStruct((1, d), F32)],
        compiler_params=_params(dimension_semantics=("arbitrary",)),
    )(dh, x, g, dres)


def _final_loss(x, g, target, *, tt, name):
    t, d = x.shape

    def body(x_ref, g_ref, t_ref, dx_ref, dxb_ref, dg_ref, loss_ref):
        xv = x_ref[...]
        gv = g_ref[...]
        rstd = lax.rsqrt(jnp.mean(xv * xv, axis=1, keepdims=True) + RMS_EPS)
        xhat = xv * rstd
        err = xhat * gv - t_ref[...]
        dy = err * (1.0 / d)
        dxhat = dy * gv
        dx = rstd * (dxhat - xhat * jnp.mean(dxhat * xhat, axis=1, keepdims=True))
        dx_ref[...] = dx
        dxb_ref[...] = dx.astype(BF16)

        @pl.when(pl.program_id(0) == 0)
        def _():
            dg_ref[...] = jnp.zeros_like(dg_ref)
            loss_ref[...] = jnp.zeros_like(loss_ref)

        dg_ref[...] += jnp.sum(dy * xhat, axis=0, keepdims=True)
        part = 0.5 * jnp.sum(jnp.mean(err * err, axis=1, keepdims=True), axis=0, keepdims=True)
        loss_ref[...] += jnp.broadcast_to(part, loss_ref.shape)

    row = pl.BlockSpec((tt, d), lambda i: (i, 0))
    vec = pl.BlockSpec((1, d), lambda i: (0, 0))
    one = pl.BlockSpec((1, LANES), lambda i: (0, 0))
    return pl.pallas_call(
        body, name=name, grid=(t // tt,), in_specs=[row, vec, row], out_specs=[row, row, vec, one],
        out_shape=[jax.ShapeDtypeStruct((t, d), F32), jax.ShapeDtypeStruct((t, d), BF16), jax.ShapeDtypeStruct((1, d), F32),
                   jax.ShapeDtypeStruct((1, LANES), F32)],
        compiler_params=_params(dimension_semantics=("arbitrary",)),
    )(x, g, target)


def _swap_halves(x):
    lane = lax.broadcasted_iota(jnp.int32, x.shape, 1)
    return jnp.where((lane // 32) % 2 == 0, pltpu.roll(x, 96, 1), pltpu.roll(x, 32, 1))


def _rope_tables(s):
    half = HEAD_DIM // 2
    inv_freq = ROPE_THETA ** (-jnp.arange(half, dtype=F32) / half)
    ang = jnp.arange(s).astype(F32)[:, None] * inv_freq[None, :]
    cos, sin = jnp.cos(ang), jnp.sin(ang)
    return jnp.tile(cos, (1, 4)), jnp.concatenate([-sin, sin, -sin, sin], axis=1)


def _dilation_of_tile(p):
    dilated = p < 3 * DIL_HEADS // 2
    g = (p % (DIL_HEADS // 2)) // (DIL_GROUP_HEADS // 2)
    return [(dilated & (g == gi)) | (jnp.logical_not(dilated) if gi == 0 else False) for gi in range(len(DILATIONS))]


def _residue_major(ref, d):
    s = ref.shape[0]
    if d == 1:
        return ref[...]
    return jnp.concatenate([ref[pl.ds(r, s // d, stride=d), :] for r in range(d)], axis=0)


def _split_heads(proj, cos4, sin4, *, n_pairs, rot_pairs, scale_ranges, name):
    b, s, _ = proj.shape

    def body(x_ref, c_ref, s_ref, o_ref):
        p = pl.program_id(1)
        is_q = functools.reduce(jnp.logical_or, [(p >= lo) & (p < hi) for lo, hi in scale_ranges])
        scale = jnp.where(is_q, QK_SCALE, 1.0)

        def put(y):
            o_ref[0] = y[:, :HEAD_DIM].astype(BF16)
            o_ref[1] = y[:, HEAD_DIM:].astype(BF16)

        for d, in_group in zip(DILATIONS, _dilation_of_tile(p)):
            @pl.when(in_group & (p < rot_pairs))
            def _(d=d):
                x = _residue_major(x_ref, d)
                put((x * _residue_major(c_ref, d) + _swap_halves(x) * _residue_major(s_ref, d)) * scale)

            @pl.when(in_group & (p >= rot_pairs))
            def _(d=d):
                put(_residue_major(x_ref, d) * scale)

    tab = pl.BlockSpec((s, LANES), lambda bi, p: (0, 0))
    return pl.pallas_call(
        body, name=name, grid=(b, n_pairs),
        in_specs=[pl.BlockSpec((None, s, LANES), lambda bi, p: (bi, 0, p)), tab, tab],
        out_specs=pl.BlockSpec((None, 2, s, HEAD_DIM), lambda bi, p: (bi, p, 0, 0)),
        out_shape=jax.ShapeDtypeStruct((b, 2 * n_pairs, s, HEAD_DIM), BF16),
        compiler_params=_params(dimension_semantics=("parallel", "parallel")),
    )(proj, cos4, sin4)


def _merge_heads(dheads, cos4, sin4, *, heads_per_row, rot_pairs, scale_pairs, dilated, out_cols, tile_off, into, name):
    b, hpr, r, s, _ = dheads.shape
    n_pairs = hpr * r // 2
    ppr = hpr // 2

    def body(d_ref, c_ref, s_ref, *rest):
        o_ref, t_ref = rest[-2:]
        p = pl.program_id(1)
        scale = jnp.where(p < scale_pairs, QK_SCALE, 1.0)

        def tokens(d):
            dy = jnp.concatenate([d_ref[0], d_ref[1]], axis=1)
            if d == 1:
                return dy
            for res in range(d):
                t_ref[pl.ds(res, s // d, stride=d), :] = dy[res * (s // d):(res + 1) * (s // d), :]
            return t_ref[...]

        groups = _dilation_of_tile(p) if dilated else [p >= 0]
        for d, in_group in zip(DILATIONS, groups):
            @pl.when(in_group & (p < rot_pairs))
            def _(d=d):
                dy = tokens(d)
                o_ref[...] = ((dy * c_ref[...] - _swap_halves(dy) * s_ref[...]) * scale).astype(BF16)

            @pl.when(in_group & (p >= rot_pairs))
            def _(d=d):
                o_ref[...] = (tokens(d) * scale).astype(BF16)

    tab = pl.BlockSpec((s, LANES), lambda bi, p: (0, 0))
    operands = [dheads, cos4, sin4] + ([] if into is None else [into])
    return pl.pallas_call(
        body, name=name, grid=(b, n_pairs),
        in_specs=[pl.BlockSpec((None, 2, None, s, HEAD_DIM), lambda bi, p: (bi, p % ppr, p // ppr, 0, 0)), tab, tab]
        + ([] if into is None else [HBM]),
        out_specs=pl.BlockSpec((None, s, LANES), lambda bi, p: (bi, 0, p + tile_off)),
        out_shape=jax.ShapeDtypeStruct((b, s, out_cols), BF16),
        input_output_aliases={} if into is None else {3: 0},
        scratch_shapes=[pltpu.VMEM((s, LANES), F32)],
        compiler_params=_params(dimension_semantics=("parallel", "parallel")),
    )(*operands)


DIL_TQ = 256


def _dil_block(g, s):
    run = s // DILATIONS[g]
    return DIL_TQ if run <= DIL_TQ else min(run, DIL_TQ + 2 * LANES)


def _dil_keys(g, q0, s):
    run = max(s // DILATIONS[g], DIL_TQ)
    lo = (q0 // run) * run
    return pl.multiple_of(jnp.clip(q0 - LANES, lo, lo + run - _dil_block(g, s)), LANES)


def _dil_band(g, q0, start, shape, s):
    row = q0 + lax.broadcasted_iota(jnp.int32, shape, 0)
    col = start + lax.broadcasted_iota(jnp.int32, shape, 1)
    ok = jnp.abs(row - col) <= DIL_HALF
    run = s // DILATIONS[g]
    if run < DIL_TQ:
        shift = run.bit_length() - 1
        ok = ok & ((row >> shift) == (col >> shift))
    return ok


def _dil_tokens(g, q0, s):
    d = DILATIONS[g]
    if d == 1:
        return [(0, DIL_TQ, pl.ds(q0, DIL_TQ))]
    run = s // d
    n = min(run, DIL_TQ)
    return [(lo, n, pl.ds(((q0 + lo) % run) * d + (q0 + lo) // run, n, stride=d)) for lo in range(0, DIL_TQ, n)]


def _dil_gather(ref, pieces):
    return jnp.concatenate([ref[rows, :] for _, _, rows in pieces], axis=0) if len(pieces) > 1 else ref[pieces[0][2], :]


def _dil_head_spec(part, g, s):
    return pl.BlockSpec((None, None, s, HEAD_DIM), lambda b, j: (b, part * DIL_HEADS + g * DIL_GROUP_HEADS + j, 0, 0))


def _dil_attn_fwd(heads, *, name):
    b, _, s, _ = heads.shape
    n_g = len(DILATIONS)

    def body(*refs):
        qkv = refs[:3 * n_g]
        o_ref, l_ref, og_ref, lg_ref = refs[3 * n_g:]
        for g in range(n_g):
            q_ref, k_ref, v_ref = qkv[3 * g:3 * g + 3]
            width = _dil_block(g, s)

            def step(i, carry, g=g, q_ref=q_ref, k_ref=k_ref, v_ref=v_ref, width=width):
                q0 = pl.multiple_of(i * DIL_TQ, DIL_TQ)
                start = _dil_keys(g, q0, s)
                sc = _dot(q_ref[pl.ds(q0, DIL_TQ), :], k_ref[pl.ds(start, width), :], _NT)
                sc = jnp.where(_dil_band(g, q0, start, sc.shape, s), sc, NEG_INF)
                m = jnp.max(sc, axis=1, keepdims=True)
                p = jnp.exp(sc - m)
                den = jnp.sum(p, axis=1, keepdims=True)
                o = _dot(p.astype(BF16), v_ref[pl.ds(start, width), :], _NN) / den
                lse = m + jnp.log(den)
                for lo, n, rows in _dil_tokens(g, q0, s):
                    og_ref[g, rows, :] = o[lo:lo + n]
                    lg_ref[g, rows, :] = lse[lo:lo + n]
                return carry

            lax.fori_loop(0, s // DIL_TQ, step, 0, unroll=BLOCKS_IN_FLIGHT)
        lses = [lg_ref[g] for g in range(n_g)]
        m = functools.reduce(jnp.maximum, lses)
        ws = [jnp.exp(l - m) for l in lses]
        den = functools.reduce(jnp.add, ws)
        o_ref[...] = (functools.reduce(jnp.add, [w * og_ref[g] for g, w in enumerate(ws)]) / den).astype(o_ref.dtype)
        l_ref[...] = m + jnp.log(den)

    out = pl.BlockSpec((None, None, s, HEAD_DIM), lambda bi, j: (bi, j, 0, 0))
    lse = pl.BlockSpec((None, None, s, 1), lambda bi, j: (bi, j, 0, 0))
    return pl.pallas_call(
        body, name=name, grid=(b, DIL_GROUP_HEADS),
        in_specs=[_dil_head_spec(part, g, s) for g in range(n_g) for part in range(3)],
        out_specs=[out, lse],
        out_shape=[jax.ShapeDtypeStruct((b, DIL_GROUP_HEADS, s, HEAD_DIM), BF16),
                   jax.ShapeDtypeStruct((b, DIL_GROUP_HEADS, s, 1), F32)],
        scratch_shapes=[pltpu.VMEM((n_g, s, HEAD_DIM), F32), pltpu.VMEM((n_g, s, 1), F32)],
        compiler_params=_params(dimension_semantics=("parallel", "parallel")),
    )(*([heads] * (3 * n_g)))


def _dil_attn_bwd(heads, out, lse, dout, *, name):
    b, _, s, _ = heads.shape
    n_g = len(DILATIONS)

    def body(*refs):
        qkv = refs[:3 * n_g]
        o_ref, l_ref, do_ref, d_ref, delta_ref = refs[3 * n_g:]
        d_ref[...] = jnp.zeros_like(d_ref)
        delta_ref[...] = jnp.sum(do_ref[...] * o_ref[...].astype(F32), axis=1, keepdims=True)
        for g in range(n_g):
            q_ref, k_ref, v_ref = qkv[3 * g:3 * g + 3]
            width = _dil_block(g, s)

            def step(i, carry, g=g, q_ref=q_ref, k_ref=k_ref, v_ref=v_ref, width=width):
                q0 = pl.multiple_of(i * DIL_TQ, DIL_TQ)
                start = _dil_keys(g, q0, s)
                win = pl.ds(start, width)
                pieces = _dil_tokens(g, q0, s)
                do_b = _dil_gather(do_ref, pieces).astype(BF16)
                q, k, v = q_ref[pl.ds(q0, DIL_TQ), :], k_ref[win, :], v_ref[win, :]
                sc = _dot(q, k, _NT)
                p = jnp.where(_dil_band(g, q0, start, sc.shape, s), jnp.exp(sc - _dil_gather(l_ref, pieces)), 0.0)
                ds = (p * (_dot(do_b, v, _NT) - _dil_gather(delta_ref, pieces))).astype(BF16)
                d_ref[g, pl.ds(q0, DIL_TQ), :] = _dot(ds, k, _NN)
                d_ref[n_g + g, win, :] += _dot(ds, q, _TN)
                d_ref[2 * n_g + g, win, :] += _dot(p.astype(BF16), do_b, _TN)
                return carry

            lax.fori_loop(0, s // DIL_TQ, step, 0, unroll=BLOCKS_IN_FLIGHT)

    per_head = lambda bi, j: (bi, j, 0, 0)
    return pl.pallas_call(
        body, name=name, grid=(b, DIL_GROUP_HEADS),
        in_specs=[_dil_head_spec(part, g, s) for g in range(n_g) for part in range(3)]
        + [pl.BlockSpec((None, None, s, HEAD_DIM), per_head), pl.BlockSpec((None, None, s, 1), per_head),
           pl.BlockSpec((None, None, s, HEAD_DIM), per_head)],
        out_specs=pl.BlockSpec((None, None, 3 * n_g, s, HEAD_DIM), lambda bi, j: (bi, j, 0, 0, 0)),
        out_shape=jax.ShapeDtypeStruct((b, DIL_GROUP_HEADS, 3 * n_g, s, HEAD_DIM), F32),
        scratch_shapes=[pltpu.VMEM((s, 1), F32)],
        compiler_params=_params(dimension_semantics=("parallel", "parallel")),
    )(*([heads] * (3 * n_g)), out, lse, dout)


NA_BIAS_ROWS = 2 * NA_ROWS - 1
NA_BIAS_COLS = 2 * NA_COLS - 1
NA_BLOCK = 4
NA_SPAN = NA_ROWS + NA_BLOCK - 1
NA_Q = NA_BLOCK * GRID_W
NA_KEYS = NA_SPAN * GRID_W
NA_FORMS = 3


def _na_onehot():
    c = np.arange(GRID_W)[:, None]
    k = np.arange(GRID_W)[None, :]
    lo = np.clip(c - NA_COLS // 2, 0, GRID_W - NA_COLS)
    valid = (k >= lo) & (k < lo + NA_COLS)
    onehot = np.zeros((GRID_W, GRID_W, LANES), np.float32)
    cc, kk = np.nonzero(valid)
    onehot[cc, kk, kk - cc + NA_COLS - 1] = 1.0
    return onehot.reshape(GRID_W * GRID_W, LANES), valid.reshape(1, GRID_W * GRID_W)


def _na_block_rows(n_rows):
    table = np.full((NA_FORMS, NA_BLOCK, NA_SPAN), NA_BIAS_ROWS, np.int64)
    n_blocks = n_rows // NA_BLOCK
    for form, ib in enumerate((0, 1, n_blocks - 1)):
        base = min(max(NA_BLOCK * ib - NA_ROWS // 2, 0), n_rows - NA_SPAN)
        for rl in range(NA_BLOCK):
            r = NA_BLOCK * ib + rl
            row_lo = min(max(r - NA_ROWS // 2, 0), n_rows - NA_ROWS)
            for kl in range(NA_SPAN):
                if row_lo <= base + kl < row_lo + NA_ROWS:
                    table[form, rl, kl] = base + kl - r + NA_ROWS - 1
    return table


def _na_block(ib, n_rows):
    n_blocks = n_rows // NA_BLOCK
    base = jnp.clip(NA_BLOCK * ib - NA_ROWS // 2, 0, n_rows - NA_SPAN)
    return base, jnp.where(ib == 0, 0, jnp.where(ib == n_blocks - 1, 2, 1))


def _na_expand_bias(rel_bias, *, name):
    l, h, nr, nc = rel_bias.shape
    onehot, valid = _na_onehot()
    rb = jnp.pad(rel_bias, ((0, 0), (0, 0), (0, 1), (0, LANES - nc))).reshape(l * h * (nr + 1), LANES)
    live = jnp.asarray(np.tile(np.arange(nr + 1) < nr, l * h).astype(np.float32)[:, None])

    def body(rb_ref, oh_ref, valid_ref, live_ref, e_ref):
        e = lax.dot_general(rb_ref[...], oh_ref[...], _NT, precision=lax.Precision.HIGHEST, preferred_element_type=F32)
        e_ref[...] = jnp.where((valid_ref[...] > 0) & (live_ref[...] > 0), e, NEG_INF)

    e = pl.pallas_call(
        body, name=name, out_shape=jax.ShapeDtypeStruct((l * h * (nr + 1), GRID_W * GRID_W), F32), compiler_params=_params(),
    )(rb, jnp.asarray(onehot), jnp.asarray(valid.astype(np.float32)), live)
    return e.reshape(l, h, nr + 1, GRID_W, GRID_W)


def _na_collapse_bias(de, *, name):
    b, h = de.shape[:2]
    onehot, _ = _na_onehot()
    rows = h * NA_BIAS_ROWS

    def diag(e_ref, oh_ref, o_ref):
        e = e_ref[0]
        for bi in range(1, b):
            e = e + e_ref[bi]
        o_ref[...] = lax.dot_general(e, oh_ref[...], _NN, precision=lax.Precision.HIGHEST, preferred_element_type=F32)

    drb = pl.pallas_call(
        diag, name=name, out_shape=jax.ShapeDtypeStruct((rows, LANES), F32), compiler_params=_params(),
    )(de.reshape(b, rows, GRID_W * GRID_W), jnp.asarray(onehot))
    return drb[:, :NA_BIAS_COLS].reshape(h, NA_BIAS_ROWS, NA_BIAS_COLS)


def _na_tiles(n_rows):
    table = _na_block_rows(n_rows)
    return [(f, rl, kl, int(table[f, rl, kl])) for f in range(NA_FORMS) for rl in range(NA_BLOCK) for kl in range(NA_SPAN)]


def _na_tile(ref, form, rl, kl):
    return ref.at[form, rl * GRID_W:(rl + 1) * GRID_W, kl * GRID_W:(kl + 1) * GRID_W]


def _na_head_spec(part, first, s):
    return pl.BlockSpec((None, None, s, HEAD_DIM), lambda b, h: (b, first + part * NA_HEADS + h, 0, 0))


def _na_attn_fwd(heads, bias, *, first, name):
    b, _, s, _ = heads.shape
    n_rows = s // GRID_W
    tiles = _na_tiles(n_rows)

    def body(q_ref, k_ref, v_ref, e_ref, o_ref, l_ref, b_ref):
        for form, rl, kl, i in tiles:
            _na_tile(b_ref, form, rl, kl)[...] = e_ref[i]

        def step(ib, carry):
            base, form = _na_block(ib, n_rows)
            rows = pl.ds(pl.multiple_of(ib * NA_Q, NA_Q), NA_Q)
            win = pl.ds(pl.multiple_of(base * GRID_W, GRID_W), NA_KEYS)
            sc = _dot(q_ref[rows, :], k_ref[win, :], _NT) + b_ref[form]
            m = jnp.max(sc, axis=1, keepdims=True)
            p = jnp.exp(sc - m)
            den = jnp.sum(p, axis=1, keepdims=True)
            o_ref[rows, :] = (_dot(p.astype(BF16), v_ref[win, :], _NN) / den).astype(o_ref.dtype)
            l_ref[rows, :] = m + jnp.log(den)
            return carry

        lax.fori_loop(0, n_rows // NA_BLOCK, step, 0, unroll=BLOCKS_IN_FLIGHT)

    per_head = lambda bi, h: (bi, h, 0, 0)
    return pl.pallas_call(
        body, name=name, grid=(b, NA_HEADS),
        in_specs=[_na_head_spec(part, first, s) for part in range(3)]
        + [pl.BlockSpec((None, NA_BIAS_ROWS + 1, GRID_W, GRID_W), lambda bi, h: (h, 0, 0, 0))],
        out_specs=[pl.BlockSpec((None, None, s, HEAD_DIM), per_head), pl.BlockSpec((None, None, s, 1), per_head)],
        out_shape=[jax.ShapeDtypeStruct((b, NA_HEADS, s, HEAD_DIM), BF16), jax.ShapeDtypeStruct((b, NA_HEADS, s, 1), F32)],
        scratch_shapes=[pltpu.VMEM((NA_FORMS, NA_Q, NA_KEYS), F32)],
        compiler_params=_params(dimension_semantics=("parallel", "parallel")),
    )(heads, heads, heads, bias)


def _na_attn_bwd(heads, bias, out, lse, dout, *, first, name):
    b, _, s, _ = heads.shape
    n_rows = s // GRID_W
    tiles = _na_tiles(n_rows)

    def body(q_ref, k_ref, v_ref, e_ref, o_ref, l_ref, do_ref, d_ref, de_ref, b_ref, db_ref):
        for form, rl, kl, i in tiles:
            _na_tile(b_ref, form, rl, kl)[...] = e_ref[i]
        d_ref[...] = jnp.zeros_like(d_ref)
        db_ref[...] = jnp.zeros_like(db_ref)

        def step(ib, carry):
            base, form = _na_block(ib, n_rows)
            rows = pl.ds(pl.multiple_of(ib * NA_Q, NA_Q), NA_Q)
            win = pl.ds(pl.multiple_of(base * GRID_W, GRID_W), NA_KEYS)
            q, k, v = q_ref[rows, :], k_ref[win, :], v_ref[win, :]
            do = do_ref[rows, :]
            delta = jnp.sum(do * o_ref[rows, :].astype(F32), axis=1, keepdims=True)
            do_b = do.astype(BF16)
            p = jnp.exp(_dot(q, k, _NT) + b_ref[form] - l_ref[rows, :])
            ds = p * (_dot(do_b, v, _NT) - delta)
            db_ref[form] += ds
            ds_b = ds.astype(BF16)
            d_ref[0, rows, :] = _dot(ds_b, k, _NN)
            d_ref[1, win, :] += _dot(ds_b, q, _TN)
            d_ref[2, win, :] += _dot(p.astype(BF16), do_b, _TN)
            return carry

        lax.fori_loop(0, n_rows // NA_BLOCK, step, 0, unroll=BLOCKS_IN_FLIGHT)
        acc = [None] * NA_BIAS_ROWS
        for form, rl, kl, i in tiles:
            if i < NA_BIAS_ROWS:
                t = _na_tile(db_ref, form, rl, kl)[...]
                acc[i] = t if acc[i] is None else acc[i] + t
        for i in range(NA_BIAS_ROWS):
            de_ref[i] = acc[i]

    per_head = lambda bi, h: (bi, h, 0, 0)
    return pl.pallas_call(
        body, name=name, grid=(b, NA_HEADS),
        in_specs=[_na_head_spec(part, first, s) for part in range(3)]
        + [pl.BlockSpec((None, NA_BIAS_ROWS + 1, GRID_W, GRID_W), lambda bi, h: (h, 0, 0, 0)),
           pl.BlockSpec((None, None, s, HEAD_DIM), per_head), pl.BlockSpec((None, None, s, 1), per_head),
           pl.BlockSpec((None, None, s, HEAD_DIM), per_head)],
        out_specs=[pl.BlockSpec((None, None, 3, s, HEAD_DIM), lambda bi, h: (bi, h, 0, 0, 0)),
                   pl.BlockSpec((None, None, NA_BIAS_ROWS, GRID_W, GRID_W), lambda bi, h: (bi, h, 0, 0, 0))],
        out_shape=[jax.ShapeDtypeStruct((b, NA_HEADS, 3, s, HEAD_DIM), F32),
                   jax.ShapeDtypeStruct((b, NA_HEADS, NA_BIAS_ROWS, GRID_W, GRID_W), F32)],
        scratch_shapes=[pltpu.VMEM((NA_FORMS, NA_Q, NA_KEYS), F32), pltpu.VMEM((NA_FORMS, NA_Q, NA_KEYS), F32)],
        compiler_params=_params(dimension_semantics=("parallel", "parallel")),
    )(heads, heads, heads, bias, out, lse, dout)


GATE_TILE = 256


def _gate_fwd(proj, z, *, gate_col, tt, name):
    _, t, d = z.shape
    nj = d // GATE_TILE
    c0 = gate_col // GATE_TILE

    def body(ga_ref, gb_ref, za_ref, zb_ref, o_ref):
        o_ref[...] = (jax.nn.sigmoid(ga_ref[...]) * za_ref[...] + jax.nn.sigmoid(gb_ref[...]) * zb_ref[...]).astype(BF16)

    return pl.pallas_call(
        body, name=name, grid=(t // tt, nj),
        in_specs=[pl.BlockSpec((tt, GATE_TILE), lambda i, j: (i, c0 + j)),
                  pl.BlockSpec((tt, GATE_TILE), lambda i, j: (i, c0 + nj + j)),
                  pl.BlockSpec((None, tt, GATE_TILE), lambda i, j: (0, i, j)),
                  pl.BlockSpec((None, tt, GATE_TILE), lambda i, j: (1, i, j))],
        out_specs=pl.BlockSpec((tt, GATE_TILE), lambda i, j: (i, j)), out_shape=jax.ShapeDtypeStruct((t, d), BF16),
        compiler_params=_params(dimension_semantics=("parallel", "parallel")),
    )(proj, proj, z, z)


def _gate_bwd(dm, proj, z, *, gate_col, tt, name):
    _, t, d = z.shape
    nj = d // GATE_TILE
    c0 = gate_col // GATE_TILE

    def body(dm_ref, g_ref, z_ref, dz_ref, dg_ref):
        dmv = dm_ref[...]
        sg = jax.nn.sigmoid(g_ref[...])
        dz_ref[...] = (dmv * sg).astype(BF16)
        dg_ref[...] = (dmv * z_ref[...] * sg * (1.0 - sg)).astype(BF16)

    return pl.pallas_call(
        body, name=name, grid=(t // tt, 2 * nj),
        in_specs=[pl.BlockSpec((tt, GATE_TILE), lambda i, j: (i, j % nj)),
                  pl.BlockSpec((tt, GATE_TILE), lambda i, j: (i, c0 + j)),
                  pl.BlockSpec((None, tt, GATE_TILE), lambda i, j: (j // nj, i, j % nj))],
        out_specs=[pl.BlockSpec((None, tt, GATE_TILE), lambda i, j: (j // nj, i, j % nj)),
                   pl.BlockSpec((tt, GATE_TILE), lambda i, j: (i, c0 + j))],
        out_shape=[jax.ShapeDtypeStruct((2, t, d), BF16), jax.ShapeDtypeStruct(proj.shape, BF16)],
        compiler_params=_params(dimension_semantics=("parallel", "parallel")),
    )(dm, proj, z)


def _adamw(w, g, m, v, *, name):
    shape = w.shape
    if w.ndim == 3:
        w2, g2, m2, v2 = w, g, m, v
    else:
        w2, g2, m2, v2 = (t.reshape(1, -1, shape[-1]) for t in (w, g, m, v))
    lead, rows, cols = w2.shape
    tr = rows
    for cand in (512, 256, 128, 64, 32, 16, 8):
        if rows % cand == 0:
            tr = cand
            break

    def body(w_ref, g_ref, m_ref, v_ref, d_ref, nm_ref, nv_ref):
        gv = g_ref[...]
        nm = ADAM_B1 * m_ref[...] + (1.0 - ADAM_B1) * gv
        nv = ADAM_B2 * v_ref[...] + (1.0 - ADAM_B2) * (gv * gv)
        m_hat = nm / (1.0 - ADAM_B1 ** ADAM_STEP)
        v_hat = nv / (1.0 - ADAM_B2 ** ADAM_STEP)
        d_ref[...] = -ADAM_LR * (m_hat / (jnp.sqrt(v_hat) + ADAM_EPS) + ADAM_WD * w_ref[...])
        nm_ref[...] = nm
        nv_ref[...] = nv

    blk = pl.BlockSpec((None, tr, cols), lambda l, i: (l, i, 0))
    out = jax.ShapeDtypeStruct((lead, rows, cols), F32)
    res = pl.pallas_call(
        body, name=name, grid=(lead, rows // tr), in_specs=[blk] * 4, out_specs=[blk] * 3, out_shape=[out] * 3,
        compiler_params=_params(dimension_semantics=("parallel", "parallel")),
    )(w2, g2, m2, v2)
    return tuple(t.reshape(shape) for t in res)


def _my_place():
    return lax.axis_index("x"), lax.axis_index("y"), lax.axis_index("c")


def _other_chips(x, y):
    return [(1 - x, y), (x, 1 - y), (1 - x, 1 - y)]


def _chip_no(chip):
    return 2 * chip[0] + chip[1]


def _window(ref, kind, size, chip, lead):
    if kind == "col":
        return ref.at[(*lead, slice(None), pl.ds(pl.multiple_of(chip * size, LANES), size))]
    if kind == "row":
        return ref.at[(*lead, pl.ds(pl.multiple_of(chip * size, BF16_ROWS), size), slice(None))]
    shard = size + HEAD_DIM
    if kind == "win_main":
        return ref.at[(*lead, slice(None), pl.ds(pl.multiple_of(chip * shard + HEAD_DIM * (chip % 2), LANES), size))]
    assert kind == "win_strad"
    return ref.at[(*lead, slice(None), pl.ds(pl.multiple_of(size + 2 * shard * (chip // 2), LANES), LANES))]


def _full_shape(shard, kind):
    _, k, n = shard.shape
    return {"col": (k, N_CHIPS * n), "row": (N_CHIPS * k, n), "win_main": (k, N_CHIPS * (n + HEAD_DIM)),
            "slot": (N_CHIPS, k, n)}[kind]


def _place_own(shard, kind, layer, *, name):
    _, k, n = shard.shape
    tr = _div_tile(k, 512, BF16_ROWS)
    tc = LANES if kind == "win_main" else n
    mine = 2 * lax.axis_index("x") + lax.axis_index("y")
    row0 = mine * (k // tr) if kind == "row" else 0
    col0 = {"col": mine, "row": 0, "slot": 0, "win_main": (mine * (n + HEAD_DIM) + HEAD_DIM * (mine % 2)) // LANES}[kind]
    scalars = jnp.stack([mine, row0, col0]).astype(jnp.int32)

    def body(s_ref, i_ref, o_ref):
        o_ref[...] = i_ref[...]

    if kind == "slot":
        o_spec = pl.BlockSpec((None, tr, tc), lambda i, j, s: (s[0], i, j))
    else:
        o_spec = pl.BlockSpec((tr, tc), lambda i, j, s: (s[1] + i, s[2] + j))
    return pl.pallas_call(
        body, name=name,
        grid_spec=pltpu.PrefetchScalarGridSpec(
            num_scalar_prefetch=1, grid=(k // tr, n // tc),
            in_specs=[pl.BlockSpec((None, tr, tc), lambda i, j, s: (layer, i, j))], out_specs=o_spec),
        out_shape=jax.ShapeDtypeStruct(_full_shape(shard, kind), shard.dtype),
        compiler_params=_params(dimension_semantics=("parallel", "parallel")),
    )(scalars, shard)


class _GatherPlan:
    def __init__(self, src, dst, shapes, kinds, layer, send_sems, recv_sems):
        self.src, self.dst, self.shapes, self.kinds, self.layer = src, dst, shapes, kinds, layer
        self.send_sems, self.recv_sems = send_sems, recv_sems
        self.x, self.y, self.c = _my_place()
        self.mine = 2 * self.x + self.y
        self.chips = _other_chips(self.x, self.y)
        self.n = len(src)

    def half(self, i, chip, half):
        _, k, n = self.shapes[i]
        kind, dst, hk = self.kinds[i], self.dst[i], k // 2
        if kind == "slot":
            return dst.at[chip, pl.ds(pl.multiple_of(half * hk, BF16_ROWS), hk), :]
        if kind == "row":
            return dst.at[pl.ds(pl.multiple_of(chip * k + half * hk, BF16_ROWS), hk), :]
        col0 = chip * n if kind == "col" else chip * (n + HEAD_DIM) + HEAD_DIM * (chip % 2)
        return dst.at[pl.ds(pl.multiple_of(half * hk, BF16_ROWS), hk), pl.ds(pl.multiple_of(col0, LANES), n)]

    def _copy(self, sem, window, to, source=None):
        return pltpu.make_async_remote_copy(src_ref=window if source is None else source, dst_ref=window,
                                            send_sem=self.send_sems.at[sem], recv_sem=self.recv_sems.at[sem],
                                            device_id=to, device_id_type=MESH)

    def sends(self):
        out = []
        for k, chip in enumerate(self.chips):
            for i in range(self.n):
                hk = self.shapes[i][1] // 2
                mine = self.src[i].at[self.layer, pl.ds(pl.multiple_of(self.c * hk, BF16_ROWS), hk), :]
                out.append(self._copy(3 * i + k, self.half(i, self.mine, self.c), (*chip, self.c), source=mine))
        return out

    def arrivals(self):
        return [self._copy(3 * i + k, self.half(i, _chip_no(chip), self.c), (*chip, self.c))
                for k, chip in enumerate(self.chips) for i in range(self.n)]

    def forwards(self, first_sem):
        sibling = (self.x, self.y, 1 - self.c)
        return [self._copy(first_sem + 3 * i + k, self.half(i, _chip_no(chip), self.c), sibling)
                for k, chip in enumerate(self.chips) for i in range(self.n)]

    def forwarded(self, first_sem):
        sibling = (self.x, self.y, 1 - self.c)
        return [self._copy(first_sem + 3 * i + k, self.half(i, _chip_no(chip), 1 - self.c), sibling)
                for k, chip in enumerate(self.chips) for i in range(self.n)]


IN_HBM = pl.BlockSpec(memory_space=pltpu.HBM)
IN_SEM = pl.BlockSpec(memory_space=pltpu.SEMAPHORE)
DATAFLOW = pltpu.SideEffectType.DATAFLOW_SIDE_EFFECTING


def _gather_layer_start(shards, kinds, fulls, layer, after, *, name):
    n_w = len(shards)
    shapes = [sh.shape for sh in shards]

    def body(*refs):
        plan = _GatherPlan(refs[:n_w], refs[n_w:2 * n_w], shapes, kinds, layer, refs[2 * n_w + 1], refs[2 * n_w + 2])
        for cp in plan.sends():
            cp.start()
        token = refs[-1]
        token[...] = jnp.zeros_like(token)

    operands = [pltpu.with_memory_space_constraint(a, pltpu.HBM) for a in (*shards, *fulls)]
    res = pl.pallas_call(
        body, name=name, in_specs=[IN_HBM] * (2 * n_w) + [pl.BlockSpec(memory_space=pl.ANY)],
        out_specs=(IN_SEM, IN_SEM, *([IN_HBM] * (2 * n_w)), pl.BlockSpec(memory_space=pltpu.VMEM)),
        out_shape=(pltpu.SemaphoreType.DMA((3 * n_w,)), pltpu.SemaphoreType.DMA((3 * n_w,)),
                   *[pltpu.HBM(a.shape, a.dtype) for a in operands], jax.ShapeDtypeStruct((8, LANES), F32)),
        input_output_aliases={i: 2 + i for i in range(2 * n_w)},
        compiler_params=pltpu.CompilerParams(has_side_effects=DATAFLOW),
    )(*operands, after)
    return res[0], res[1], res[2:2 + n_w], res[2 + n_w:2 + 2 * n_w], res[-1]


def _gather_layer_wait(send_sems, recv_sems, shards, fulls, kinds, layer, after, *, name):
    n_w = len(shards)
    shapes = [sh.shape for sh in shards]

    def body(*refs):
        plan = _GatherPlan(refs[:n_w], refs[n_w:2 * n_w], shapes, kinds, layer, refs[2 * n_w], refs[2 * n_w + 1])
        for cp in plan.sends():
            cp.wait_send()
        for cp in plan.arrivals():
            cp.wait_recv()

    res = pl.pallas_call(
        body, name=name, in_specs=[IN_HBM] * (2 * n_w) + [IN_SEM, IN_SEM, pl.BlockSpec(memory_space=pl.ANY)],
        out_specs=[IN_HBM] * (2 * n_w), out_shape=[pltpu.HBM(a.shape, a.dtype) for a in (*shards, *fulls)],
        input_output_aliases={i: i for i in range(2 * n_w)},
        compiler_params=pltpu.CompilerParams(has_side_effects=DATAFLOW),
    )(*shards, *fulls, send_sems, recv_sems, after)
    return res[n_w:]


def _gather_layer_forward(shapes, kinds, fulls, *, name):
    n_w = len(fulls)

    def body(*refs):
        plan = _GatherPlan([None] * n_w, refs[n_w:2 * n_w], shapes, kinds, 0, *refs[2 * n_w:])
        passed = plan.forwards(0)
        for cp in passed:
            cp.start()
        for cp in plan.forwarded(0):
            cp.wait_recv()
        for cp in passed:
            cp.wait_send()

    return pl.pallas_call(
        body, name=name, in_specs=[HBM] * n_w, out_specs=[HBM] * n_w,
        out_shape=[jax.ShapeDtypeStruct(f.shape, f.dtype) for f in fulls],
        input_output_aliases={i: i for i in range(n_w)},
        scratch_shapes=[pltpu.SemaphoreType.DMA((3 * n_w,)), pltpu.SemaphoreType.DMA((3 * n_w,))],
    )(*fulls)


def _on_core(layer):
    return (lax.axis_index("c") == layer).astype(jnp.int32).reshape(1)


N_DEVICES = 2 * N_CHIPS


class _ScatterPlan:
    def __init__(self, src, dst, kinds, sizes, layer, send_sems, recv_sems):
        self.src, self.dst, self.kinds, self.sizes, self.layer = src, dst, kinds, sizes, layer
        self.send_sems, self.recv_sems = send_sems, recv_sems
        self.x, self.y, self.c = _my_place()
        self.mine = 2 * self.x + self.y
        self.chips = _other_chips(self.x, self.y)
        self.n = len(src)

    def _copy(self, i, k, window_of, from_chip, from_core, to):
        return pltpu.make_async_remote_copy(src_ref=_window(self.src[i], self.kinds[i], self.sizes[i], window_of, ()),
                                            dst_ref=self.dst[i].at[2 * from_chip + from_core],
                                            send_sem=self.send_sems.at[4 * i + k],
                                            recv_sem=self.recv_sems.at[2 * (4 * i + k) + from_core],
                                            device_id=to, device_id_type=MESH)

    def to_chips(self):
        return [self._copy(i, k, _chip_no(chip), self.mine, self.c, (*chip, self.layer))
                for k, chip in enumerate(self.chips) for i in range(self.n)]

    def to_sibling(self):
        return [self._copy(i, 3, self.mine, self.mine, self.c, (self.x, self.y, self.layer)) for i in range(self.n)]

    def arrivals(self):
        out = [self._copy(i, k, self.mine, _chip_no(chip), core, (*chip, core))
               for k, chip in enumerate(self.chips) for core in (0, 1) for i in range(self.n)]
        return out + [self._copy(i, 3, self.mine, self.mine, 1 - self.layer, (self.x, self.y, 1 - self.layer))
                      for i in range(self.n)]


def _slab_shape(p, kind, size):
    return (N_DEVICES,) + {"col": (p.shape[0], size), "row": (size, p.shape[1]), "win_main": (p.shape[0], size),
                           "win_strad": (p.shape[0], LANES)}[kind]


def _grads_to_chips_start(pairs, kinds, sizes, layer, *, name):
    n_w = len(pairs)

    def body(*refs):
        plan = _ScatterPlan(refs[:n_w], refs[n_w:2 * n_w], kinds, sizes, layer, refs[2 * n_w], refs[2 * n_w + 1])
        for cp in plan.to_chips():
            cp.start()

        @pl.when(plan.c != layer)
        def _():
            for cp in plan.to_sibling():
                cp.start()

        token = refs[-1]
        token[...] = jnp.zeros_like(token)

    slabs = [lax.empty(_slab_shape(p, kind, size), p.dtype) for p, kind, size in zip(pairs, kinds, sizes)]
    operands = [pltpu.with_memory_space_constraint(a, pltpu.HBM) for a in (*pairs, *slabs)]
    res = pl.pallas_call(
        body, name=name, in_specs=[IN_HBM] * (2 * n_w),
        out_specs=(IN_SEM, IN_SEM, *([IN_HBM] * (2 * n_w)), pl.BlockSpec(memory_space=pltpu.VMEM)),
        out_shape=(pltpu.SemaphoreType.DMA((4 * n_w,)), pltpu.SemaphoreType.DMA((8 * n_w,)),
                   *[pltpu.HBM(a.shape, a.dtype) for a in operands], jax.ShapeDtypeStruct((8, LANES), F32)),
        input_output_aliases={i: 2 + i for i in range(2 * n_w)},
        compiler_params=pltpu.CompilerParams(has_side_effects=DATAFLOW),
    )(*operands)
    return res[0], res[1], res[2:2 + n_w], res[2 + n_w:2 + 2 * n_w], res[-1]


def _grads_to_chips_wait(send_sems, recv_sems, pairs, slabs, kinds, sizes, layer, after, *, name):
    n_w = len(pairs)

    def body(*refs):
        plan = _ScatterPlan(refs[:n_w], refs[n_w:2 * n_w], kinds, sizes, layer, refs[2 * n_w], refs[2 * n_w + 1])
        for cp in plan.to_chips():
            cp.wait_send()

        @pl.when(plan.c != layer)
        def _():
            for cp in plan.to_sibling():
                cp.wait_send()

        @pl.when(plan.c == layer)
        def _():
            for cp in plan.arrivals():
                cp.wait_recv()

    res = pl.pallas_call(
        body, name=name, in_specs=[IN_HBM] * (2 * n_w) + [IN_SEM, IN_SEM, pl.BlockSpec(memory_space=pl.ANY)],
        out_specs=[IN_HBM] * (2 * n_w), out_shape=[pltpu.HBM(a.shape, a.dtype) for a in (*pairs, *slabs)],
        input_output_aliases={i: i for i in range(2 * n_w)},
        compiler_params=pltpu.CompilerParams(has_side_effects=DATAFLOW),
    )(*pairs, *slabs, send_sems, recv_sems, after)
    return res[:n_w], res[n_w:]


def _sum_slabs(slabs, pair, kind, size, layer, into, *, name):
    n_s, k, n = slabs.shape
    tr = _div_tile(k, 512, BF16_ROWS)
    tc = n if kind in ("col", "row") else LANES
    x, y, _ = _my_place()
    mine = 2 * x + y
    shard = size + HEAD_DIM
    row0 = mine * (k // tr) if kind == "row" else 0
    col0 = {"col": mine, "row": 0, "win_main": (mine * shard + HEAD_DIM * (mine % 2)) // LANES,
            "win_strad": (size + 2 * shard * (mine // 2)) // LANES}[kind]
    on = _on_core(layer)[0]
    scalars = jnp.stack([2 * mine + layer, row0 * on, col0 * on, on]).astype(jnp.int32)

    def body(s_ref, slab_ref, own_ref, *rest):
        o_ref = rest[-1]
        me = s_ref[0]

        @pl.when(s_ref[3] == 1)
        def _():
            acc = jnp.zeros(o_ref.shape, F32)
            for i in range(n_s):
                acc = acc + jnp.where(me == i, own_ref[...], slab_ref[i]).astype(F32)
            o_ref[...] = acc

    operands = [scalars, slabs, pair] + ([] if into is None else [into])
    return pl.pallas_call(
        body, name=name,
        grid_spec=pltpu.PrefetchScalarGridSpec(
            num_scalar_prefetch=1, grid=(k // tr, n // tc),
            in_specs=[pl.BlockSpec((n_s, tr, tc), lambda i, j, s: (0, i * s[3], j * s[3])),
                      pl.BlockSpec((tr, tc), lambda i, j, s: (s[1] + i * s[3], s[2] + j * s[3]))]
            + ([] if into is None else [HBM]),
            out_specs=pl.BlockSpec((None, tr, tc), lambda i, j, s: (layer, i * s[3], j * s[3]))),
        out_shape=jax.ShapeDtypeStruct((2, k, n), F32),
        input_output_aliases={} if into is None else {3: 0},
        compiler_params=_params(dimension_semantics=("arbitrary", "arbitrary")),
    )(*operands)


def _exchange_layers(bufs, *, name):
    n_w = len(bufs)

    def body(*refs):
        dst = refs[n_w:2 * n_w]
        send_sems, recv_sems = refs[2 * n_w:]
        x, y, c = _my_place()

        def copy(i, layer):
            return pltpu.make_async_remote_copy(src_ref=dst[i].at[layer], dst_ref=dst[i].at[layer], send_sem=send_sems.at[i],
                                                recv_sem=recv_sems.at[i], device_id=(x, y, 1 - c), device_id_type=MESH)

        sends = [copy(i, c) for i in range(n_w)]
        for cp in sends:
            cp.start()
        for i in range(n_w):
            copy(i, 1 - c).wait_recv()
        for cp in sends:
            cp.wait_send()

    return pl.pallas_call(
        body, name=name, in_specs=[HBM] * n_w, out_specs=[HBM] * n_w,
        out_shape=[jax.ShapeDtypeStruct(b.shape, b.dtype) for b in bufs],
        input_output_aliases={i: i for i in range(n_w)},
        scratch_shapes=[pltpu.SemaphoreType.DMA((n_w,)), pltpu.SemaphoreType.DMA((n_w,))],
    )(*bufs)


def _all_sum_small(v, *, name):
    r = v.shape[0]
    relations = [(dx, dy, dc) for dx in (0, 1) for dy in (0, 1) for dc in (0, 1)][1:]

    def body(v_ref, o_ref, buf, send_sems, recv_sems):
        x, y, c = _my_place()
        me = 4 * x + 2 * y + c
        buf[me] = v_ref[...]
        peers = [(x + dx - 2 * x * dx, y + dy - 2 * y * dy, c + dc - 2 * c * dc) for dx, dy, dc in relations]

        def copy(k, slot):
            return pltpu.make_async_remote_copy(src_ref=v_ref, dst_ref=buf.at[slot], send_sem=send_sems.at[k],
                                                recv_sem=recv_sems.at[k], device_id=peers[k], device_id_type=MESH)

        sends = [copy(k, me) for k in range(len(relations))]
        for cp in sends:
            cp.start()
        for k, (px, py, pc) in enumerate(peers):
            copy(k, 4 * px + 2 * py + pc).wait_recv()
        for cp in sends:
            cp.wait_send()
        acc = buf[0]
        for i in range(1, 8):
            acc = acc + buf[i]
        o_ref[...] = acc

    vm = pl.BlockSpec(memory_space=pltpu.VMEM)
    return pl.pallas_call(
        body, name=name, in_specs=[vm], out_specs=vm, out_shape=jax.ShapeDtypeStruct((r, LANES), F32),
        scratch_shapes=[pltpu.VMEM((8, r, LANES), F32), pltpu.SemaphoreType.DMA((7,)), pltpu.SemaphoreType.DMA((7,))],
    )(v)


SHARDED = (("ffn1_w_up", "col"), ("ffn1_w_down", "row"), ("w_in", "win"), ("w_branch_a", "col"),
           ("w_branch_b", "col"), ("w_out", "row"), ("ffn2_w_up", "col"), ("ffn2_w_down", "row"))
REPLICATED = ("ffn1_norm", "mix_norm", "na_rel_bias", "ffn2_norm", "final_norm")


def _weight_pieces(w):
    even = lax.axis_index("y") == 0
    shards, kinds, names = [], [], []
    for name, kind in SHARDED:
        wb = w[name].astype(BF16)
        if kind == "win":
            main = wb.shape[-1] - HEAD_DIM
            assert main % LANES == 0
            zeros = jnp.zeros(wb.shape[:-1] + (HEAD_DIM,), BF16)
            shards += [jnp.where(even, wb[..., :main], wb[..., HEAD_DIM:]),
                       jnp.where(even, jnp.concatenate([wb[..., main:], zeros], -1),
                                 jnp.concatenate([zeros, wb[..., :HEAD_DIM]], -1))]
            kinds += ["win_main", "slot"]
            names += [name, name + "_strad"]
        else:
            shards.append(wb)
            kinds.append(kind)
            names.append(name)
    return names, kinds, shards


def _finish_w_in(full):
    full = dict(full)
    strad = full.pop("w_in_strad")
    main = full["w_in"].shape[1] // N_CHIPS - HEAD_DIM
    for i in range(N_CHIPS // 2):
        lo = main + 2 * (main + HEAD_DIM) * i
        full["w_in"] = full["w_in"].at[:, lo:lo + LANES].set(strad[2 * i] + strad[2 * i + 1])
    return full


def _scatter_pieces(shards):
    names, kinds, sizes, srcs = [], [], [], []
    for name, kind in SHARDED:
        shp = shards[name].shape
        if kind == "win":
            names += [name, name + "_strad"]
            kinds += ["win_main", "win_strad"]
            sizes += [shp[2] - HEAD_DIM] * 2
            srcs += [name, name]
        else:
            names.append(name)
            kinds.append(kind)
            sizes.append(shp[1] if kind == "row" else shp[2])
            srcs.append(name)
    return names, kinds, sizes, srcs


def _finish_weight_grads(reduced, names, tag):
    out = dict(zip(names, _exchange_layers(reduced, name=f"{tag}_layers")))
    if "w_in_strad" in out:
        strad = out.pop("w_in_strad")
        even = lax.axis_index("y") == 0
        out["w_in"] = jnp.where(even, jnp.concatenate([out["w_in"], strad[..., :HEAD_DIM]], -1),
                                jnp.concatenate([strad[..., HEAD_DIM:], out["w_in"]], -1))
    return out


class _Grads:
    def __init__(self):
        self.arrays = {}

    def put(self, weight, layer, a, b, *, cols=None, col_off=0, **kw):
        self.arrays[weight, layer] = _mm(a, b, mode="tn", out_dtype=BF16, out_cols=cols, out_col_off=col_off,
                                         out_into=self.arrays.get((weight, layer)), **kw)


def _ffn_fwd(x, h, w_up, w_down, tag):
    t, d = x.shape
    f = w_down.shape[0]
    a, gate, up = _mm_swiglu_fwd(h, w_up, tm=_div_tile(t, ROWS_NARROW, 8), tn=MXU_N, name=f"{tag}_up")
    x_out = _mm(a, w_down, mode="nn", out_dtype=F32, tm=_div_tile(t, ROWS_WIDE, 8), tn=d, tk=f, alpha=0.5, res=x, name=f"{tag}_down")
    return x_out, (x, h, a, gate, up)


def _ffn_bwd(dx, dxb, saved, norm_g, w_up, w_down, layer, grads, wname, tag, scatter):
    x, h, a, gate, up = saved
    t, d = x.shape
    f = w_down.shape[0]
    tn = _div_tile(f, 1408)
    grads.put(f"{wname}_w_down", layer, a, dxb, tm=tn, tn=d, tk=1024, alpha=0.5, name=f"{tag}_dwd")
    d_gate, d_up = _mm_swiglu_bwd(dxb, w_down, gate, up, alpha=0.5, tm=_div_tile(t, ROWS_NARROW, 8), tn=MXU_N, name=f"{tag}_da")
    grads.put(f"{wname}_w_up", layer, h, d_gate, cols=2 * f, tm=d, tn=tn, tk=1024, name=f"{tag}_dwg")
    grads.put(f"{wname}_w_up", layer, h, d_up, cols=2 * f, col_off=f // tn, tm=d, tn=tn, tk=1024, name=f"{tag}_dwu")
    started = scatter(layer, [f"{wname}_w_up", f"{wname}_w_down"])
    dh = _mm(d_gate, w_up, mode="nt", out_dtype=F32, tm=_div_tile(t, ROWS_WIDE, 8), tn=d, tk=f, name=f"{tag}_dh1")
    dh = _mm(d_up, w_up, mode="nt", out_dtype=F32, tm=_div_tile(t, ROWS_WIDE, 8), tn=d, tk=f, b_k_off=1, res=dh, name=f"{tag}_dh2")
    return _rms_bwd(dh, x, norm_g + started, dx, tt=512, name=f"{tag}_dnorm")


def _to_heads(y, b, n_heads):
    t, w = y.shape
    return y.reshape(b, t // b, n_heads, HEAD_DIM).transpose(0, 2, 1, 3)


def _from_heads(y):
    b, n, s, hd = y.shape
    return y.transpose(0, 2, 1, 3).reshape(b * s, n * hd)


N_QKV = 3 * (DIL_HEADS + NA_HEADS) * HEAD_DIM


def _mixer_fwd(x, b, norm_g, full, bias, tabs, tag):
    t, d = x.shape
    s = t // b
    n_in = full["w_in"].shape[1]
    h = _rms_fwd(x, norm_g, tt=512, name=f"{tag}_norm")
    proj = _mm(h, full["w_in"], mode="nn", out_dtype=F32, tm=_div_tile(t, ROWS_NARROW, 8), tn=MXU_N, tk=d, name=f"{tag}_in")
    heads = _split_heads(proj.reshape(b, s, -1), *tabs, n_pairs=N_QKV // LANES, rot_pairs=DIL_HEADS,
                         scale_ranges=((0, DIL_HEADS // 2), (3 * DIL_HEADS // 2, (3 * DIL_HEADS + NA_HEADS) // 2)),
                         name=f"{tag}_heads")
    ya, lse_a = _dil_attn_fwd(heads, name=f"{tag}_dil")
    yb, lse_b = _na_attn_fwd(heads, bias, first=3 * DIL_HEADS, name=f"{tag}_na")
    ya2, yb2 = _from_heads(ya), _from_heads(yb)
    z = _mm(ya2, full["w_branch_a"], mode="nn", out_dtype=F32, tm=_div_tile(t, ROWS_NARROW, 8), tn=MXU_N, tk=ya2.shape[1],
            out_slab=(0, 2), name=f"{tag}_za")
    z = _mm(yb2, full["w_branch_b"], mode="nn", out_dtype=F32, tm=_div_tile(t, ROWS_NARROW, 8), tn=MXU_N, tk=yb2.shape[1],
            out_slab=(1, 2), out_into=z, name=f"{tag}_zb")
    merged = _gate_fwd(proj, z, gate_col=N_QKV, tt=1024, name=f"{tag}_gate")
    x_out = _mm(merged, full["w_out"], mode="nn", out_dtype=F32, tm=_div_tile(t, ROWS_NARROW, 8), tn=MXU_N, tk=d, res=x, name=f"{tag}_out")
    return x_out, (x, h, proj, heads, ya, lse_a, yb, lse_b, ya2, yb2, z, merged)


def _mixer_bwd(dx, dob, b, saved, norm_g, full, layer, bias, tabs, grads, tag, scatter):
    x, h, proj, heads, ya, lse_a, yb, lse_b, ya2, yb2, z, merged = saved
    t, d = x.shape
    s = t // b
    n_in = full["w_in"].shape[1]
    grads.put("w_out", layer, merged, dob, tm=d, tn=d, tk=1024, name=f"{tag}_dwo")
    dm = _mm(dob, full["w_out"], mode="nt", out_dtype=F32, tm=_div_tile(t, ROWS_NARROW, 8), tn=MXU_N, tk=d, name=f"{tag}_dm")
    dz, dproj = _gate_bwd(dm, proj, z, gate_col=N_QKV, tt=1024, name=f"{tag}_dgate")
    grads.put("w_branch_a", layer, ya2, dz, b_sel=0, tm=ya2.shape[1], tn=d, tk=1024, name=f"{tag}_dwa")
    grads.put("w_branch_b", layer, yb2, dz, b_sel=1, tm=yb2.shape[1], tn=d, tk=1024, name=f"{tag}_dwb")
    started = scatter(layer, ["w_out", "w_branch_a", "w_branch_b"])
    dya = _mm(dz, full["w_branch_a"], mode="nt", out_dtype=F32, tm=_div_tile(t, ROWS_NARROW, 8), tn=MXU_N, tk=d, a_sel=0, name=f"{tag}_dya")
    dyb = _mm(dz, full["w_branch_b"], mode="nt", out_dtype=F32, tm=_div_tile(t, ROWS_NARROW, 8), tn=MXU_N, tk=d, a_sel=1, name=f"{tag}_dyb")
    d_dil = _dil_attn_bwd(heads, ya, lse_a, _to_heads(dya, b, DIL_GROUP_HEADS), name=f"{tag}_ddil")
    d_na, d_bias = _na_attn_bwd(heads, bias, yb, lse_b, _to_heads(dyb, b, NA_HEADS), first=3 * DIL_HEADS, name=f"{tag}_dna")
    dproj = _merge_heads(d_dil, *tabs, heads_per_row=DIL_GROUP_HEADS, rot_pairs=DIL_HEADS, scale_pairs=DIL_HEADS // 2,
                         dilated=True, out_cols=n_in, tile_off=0, into=dproj.reshape(b, s, n_in), name=f"{tag}_dheads_a")
    dproj = _merge_heads(d_na, *tabs, heads_per_row=NA_HEADS, rot_pairs=0, scale_pairs=NA_HEADS // 2, dilated=False,
                         out_cols=n_in, tile_off=3 * DIL_HEADS // 2, into=dproj, name=f"{tag}_dheads_b").reshape(t, n_in)
    grads.put("w_in", layer, h, dproj, tm=_div_tile(d, 512), tn=_div_tile(n_in, 2944), tk=1024, name=f"{tag}_dwin")
    started = started + scatter(layer, ["w_in"])
    dh = _mm(dproj, full["w_in"], mode="nt", out_dtype=F32, tm=_div_tile(t, ROWS_WIDE, 8), tn=d, tk=_div_tile(n_in, 2944), name=f"{tag}_dh")
    dx_in, dxb_in, d_norm = _rms_bwd(dh, x, norm_g + started, dx, tt=512, name=f"{tag}_dnorm")
    d_rb = _na_collapse_bias(d_bias, name=f"{tag}_dbias")
    return dx_in, dxb_in, d_norm, d_rb


def kernel(x, ffn1_norm, ffn1_w_up, ffn1_w_down, mix_norm, w_in, na_rel_bias, w_branch_a, w_branch_b, w_out, ffn2_norm, ffn2_w_up, ffn2_w_down, final_norm, loss_target, m_ffn1_norm, m_ffn1_w_up, m_ffn1_w_down, m_mix_norm, m_w_in, m_na_rel_bias, m_w_branch_a, m_w_branch_b, m_w_out, m_ffn2_norm, m_ffn2_w_up, m_ffn2_w_down, m_final_norm, v_ffn1_norm, v_ffn1_w_up, v_ffn1_w_down, v_mix_norm, v_w_in, v_na_rel_bias, v_w_branch_a, v_w_branch_b, v_w_out, v_ffn2_norm, v_ffn2_w_up, v_ffn2_w_down, v_final_norm):
    w = dict(ffn1_norm=ffn1_norm, ffn1_w_up=ffn1_w_up, ffn1_w_down=ffn1_w_down, mix_norm=mix_norm, w_in=w_in,
             na_rel_bias=na_rel_bias, w_branch_a=w_branch_a, w_branch_b=w_branch_b, w_out=w_out, ffn2_norm=ffn2_norm,
             ffn2_w_up=ffn2_w_up, ffn2_w_down=ffn2_w_down, final_norm=final_norm)
    mom = dict(ffn1_norm=m_ffn1_norm, ffn1_w_up=m_ffn1_w_up, ffn1_w_down=m_ffn1_w_down, mix_norm=m_mix_norm, w_in=m_w_in,
               na_rel_bias=m_na_rel_bias, w_branch_a=m_w_branch_a, w_branch_b=m_w_branch_b, w_out=m_w_out,
               ffn2_norm=m_ffn2_norm, ffn2_w_up=m_ffn2_w_up, ffn2_w_down=m_ffn2_w_down, final_norm=m_final_norm)
    var = dict(ffn1_norm=v_ffn1_norm, ffn1_w_up=v_ffn1_w_up, ffn1_w_down=v_ffn1_w_down, mix_norm=v_mix_norm, w_in=v_w_in,
               na_rel_bias=v_na_rel_bias, w_branch_a=v_w_branch_a, w_branch_b=v_w_branch_b, w_out=v_w_out,
               ffn2_norm=v_ffn2_norm, ffn2_w_up=v_ffn2_w_up, ffn2_w_down=v_ffn2_w_down, final_norm=v_final_norm)
    b, s, d = x.shape
    t = b * s
    depth = ffn1_norm.shape[0]
    assert depth == 2, "core c of a chip sends / reduces layer c"
    shards = {name: w[name] for name, _ in SHARDED}

    names, kinds, pieces = _weight_pieces(w)
    by_layer = [[p[l:l + 1] for p in pieces] for l in range(depth)]
    own = [[_place_own(p, kind, 0, name=f"own{l}_{nm}") for nm, kind, p in zip(names, kinds, by_layer[l])] for l in range(depth)]
    full = [{}, {}]

    def gather_start(layer, group, after, tag):
        idx = [i for i, nm in enumerate(names) if nm in group]
        pick = lambda seq: [seq[i] for i in idx]
        *state, token = _gather_layer_start(pick(by_layer[layer]), pick(kinds), pick(own[layer]), 0, after, name=f"{tag}_start")
        return (layer, idx, tag, state), token[:1, :1]

    def gather_finish(started, after):
        layer, idx, tag, state = started
        pick = lambda seq: [seq[i] for i in idx]
        landed = _gather_layer_wait(*state, pick(kinds), 0, after, name=f"{tag}_wait")
        done = _gather_layer_forward([by_layer[layer][i].shape for i in idx], pick(kinds), landed, name=f"{tag}_forward")
        full[layer].update(zip(pick(names), done))
        return done[0]

    ffn1, mixer, ffn2 = names[:2], names[2:7], names[7:]
    assert mixer[0] == "w_in" and ffn2[0] == "ffn2_w_up", names
    xc = x.reshape(t, d)
    l0_ffn1, token_ffn1 = gather_start(0, ffn1, xc, "gather_l0_ffn1")
    tabs = _rope_tables(s)
    bias = _na_expand_bias(na_rel_bias, name="na_bias")

    saved = []
    h = _rms_fwd(xc, ffn1_norm[:1] + token_ffn1, tt=512, name="l0_ffn1_norm")
    landed = gather_finish(l0_ffn1, h)
    l0_mixer, token_mixer = gather_start(0, mixer, landed, "gather_l0_mixer")
    xc, s1 = _ffn_fwd(xc, h + token_mixer.astype(BF16), full[0]["ffn1_w_up"], full[0]["ffn1_w_down"], "l0_ffn1")
    landed = gather_finish(l0_mixer, xc)
    full[0] = _finish_w_in(full[0])
    l0_ffn2, token_ffn2 = gather_start(0, ffn2, landed, "gather_l0_ffn2")
    layer1, token_layer1 = gather_start(1, names, landed, "gather_l1")
    xc, s2 = _mixer_fwd(xc, b, mix_norm[:1] + token_ffn2 + token_layer1, full[0], bias[0], tabs, "l0_mix")
    gather_finish(l0_ffn2, xc)
    xc, s3 = _ffn_fwd(xc, _rms_fwd(xc, ffn2_norm[:1], tt=512, name="l0_ffn2_norm"), full[0]["ffn2_w_up"], full[0]["ffn2_w_down"],
                      "l0_ffn2")
    saved.append((s1, s2, s3))
    gather_finish(layer1, xc)
    full[1] = _finish_w_in(full[1])
    for l in range(1, depth):
        xc, s1 = _ffn_fwd(xc, _rms_fwd(xc, ffn1_norm[l:l + 1], tt=512, name=f"l{l}_ffn1_norm"), full[l]["ffn1_w_up"],
                          full[l]["ffn1_w_down"], f"l{l}_ffn1")
        xc, s2 = _mixer_fwd(xc, b, mix_norm[l:l + 1], full[l], bias[l], tabs, f"l{l}_mix")
        xc, s3 = _ffn_fwd(xc, _rms_fwd(xc, ffn2_norm[l:l + 1], tt=512, name=f"l{l}_ffn2_norm"), full[l]["ffn2_w_up"],
                          full[l]["ffn2_w_down"], f"l{l}_ffn2")
        saved.append((s1, s2, s3))

    dx, dxb, d_final, loss_part = _final_loss(xc, final_norm.reshape(1, d), loss_target.reshape(t, d), tt=512, name="final_loss")
    grads = _Grads()
    piece_names, piece_kinds, piece_sizes, piece_srcs = _scatter_pieces(shards)
    scattered = []

    def scatter(layer, weights):
        tag = f"grads{layer}_{weights[0]}"
        idx = [i for i, src in enumerate(piece_srcs) if src in weights]
        pick = lambda seq: [seq[i] for i in idx]
        *state, token = _grads_to_chips_start([grads.arrays[src, layer] for src in pick(piece_srcs)], pick(piece_kinds),
                                              pick(piece_sizes), layer, name=f"{tag}_to_chips_start")
        scattered.append((layer, idx, state))
        return token[:1, :1]
    small = {name: [None] * depth for name in REPLICATED[:-1]}
    for l in reversed(range(depth)):
        s1, s2, s3 = saved[l]
        dx, dxb, small["ffn2_norm"][l] = _ffn_bwd(dx, dxb, s3, ffn2_norm[l:l + 1], full[l]["ffn2_w_up"], full[l]["ffn2_w_down"],
                                                  l, grads, "ffn2", f"l{l}_ffn2", scatter)
        dx, dxb, small["mix_norm"][l], small["na_rel_bias"][l] = _mixer_bwd(
            dx, dxb, b, s2, mix_norm[l:l + 1], full[l], l, bias[l], tabs, grads, f"l{l}_mix", scatter)
        dx, dxb, small["ffn1_norm"][l] = _ffn_bwd(dx, dxb, s1, ffn1_norm[l:l + 1], full[l]["ffn1_w_up"], full[l]["ffn1_w_down"],
                                                  l, grads, "ffn1", f"l{l}_ffn1", scatter)
    grad_x = dx.reshape(b, s, d)
    reduced = [None] * len(piece_names)

    def arrive(group, after):
        layer, idx, state = group
        state = _grads_to_chips_wait(*state, [piece_kinds[i] for i in idx], [piece_sizes[i] for i in idx], layer, after,
                                     name=f"grads{layer}_{piece_names[idx[0]]}_to_chips_wait")
        for i, p, sl in zip(idx, *state):
            reduced[i] = _sum_slabs(sl, p, piece_kinds[i], piece_sizes[i], layer, reduced[i],
                                    name=f"grads{layer}_sum_{piece_names[i]}")
        return idx

    for group in scattered[:-1]:
        arrive(group, dx)
    late = scattered[-1][1]
    early = [i for i in range(len(piece_names)) if i not in late]
    g_out = _finish_weight_grads([reduced[i] for i in early], [piece_names[i] for i in early], "grads_early")

    parts = [jnp.stack(small[name]).reshape(-1) for name in REPLICATED[:-1]] + [d_final.reshape(-1), loss_part[0, :1]]
    sizes = [v.shape[0] for v in parts]
    flat = jnp.concatenate(parts)
    flat = jnp.pad(flat, (0, -flat.shape[0] % (8 * LANES)))
    small_sum = _all_sum_small(flat.reshape(-1, LANES), name="small_all_sum").reshape(-1)
    off = 0
    for name, n in zip(REPLICATED, sizes[:-1]):
        g_out[name] = small_sum[off:off + n].reshape(w[name].shape)
        off += n
    loss = small_sum[off]

    names = list(w)
    delta, new_m, new_v = {}, {}, {}
    for name in [n for n in names if n in g_out]:
        delta[name], new_m[name], new_v[name] = _adamw(w[name], g_out[name], mom[name], var[name], name=f"adamw_{name}")
    arrive(scattered[-1], delta["w_in"])
    g_out.update(_finish_weight_grads([reduced[i] for i in late], [piece_names[i] for i in late], "grads_late"))
    for name in [n for n in names if n not in delta]:
        delta[name], new_m[name], new_v[name] = _adamw(w[name], g_out[name], mom[name], var[name], name=f"adamw_{name}")
    return (loss, grad_x, *[g_out[n] for n in names], *[delta[n] for n in names], *[new_m[n] for n in names],
            *[new_v[n] for n in names])
```

```python
import functools

import numpy as np
import jax
import jax.numpy as jnp
from jax import lax
from jax.experimental import pallas as pl
from jax.experimental.pallas import tpu as pltpu

F32, BF16 = jnp.float32, jnp.bfloat16
MESH = pl.DeviceIdType.MESH

HEAD_DIM = 64
DILATIONS = (1, 4, 16)
DIL_HALF = 64
DIL_GROUP_HEADS = 4
DIL_HEADS = 12
NA_HEADS = 8
GRID_W = 64
NA_ROWS = 8
NA_COLS = 16
ROPE_THETA = 10000.0
RMS_EPS = 1e-6
NEG_INF = -1e30
ADAM_LR, ADAM_B1, ADAM_B2, ADAM_EPS, ADAM_WD, ADAM_STEP = 0.001, 0.9, 0.999, 1e-08, 0.01, 10
QK_SCALE = HEAD_DIM ** -0.5

N_CHIPS = 4
LANES = 128
BF16_ROWS = 16
VMEM_LIMIT = 56 * 1024 * 1024
MXU_N = 256
ROWS_NARROW = 2048
ROWS_WIDE = 512
ROWS_CONTRACTED = 2048
BLOCKS_IN_FLIGHT = 4

_NN = (((1,), (0,)), ((), ()))
_NT = (((1,), (1,)), ((), ()))
_TN = (((0,), (0,)), ((), ()))

HBM = pl.BlockSpec(memory_space=pl.ANY)


def _params(**kw):
    return pltpu.CompilerParams(vmem_limit_bytes=VMEM_LIMIT, **kw)


def _dot(a, b, dims):
    return lax.dot_general(a, b, dims, preferred_element_type=F32)


def _div_tile(n, cap, mult=LANES):
    best = None
    for t in range(mult, min(n, cap) + 1, mult):
        if n % t == 0:
            best = t
    return n if best is None else best


def _stacked(block, index, sel):
    if sel is None:
        return pl.BlockSpec(block, index)
    return pl.BlockSpec((None,) + block, lambda *g: (sel,) + index(*g))


def _mm(a, b, *, mode, out_dtype, tm, tn, tk, name, alpha=1.0, res=None, a_sel=None, b_sel=None, b_k_off=0,
        out_slab=None, out_cols=None, out_col_off=0, out_into=None):
    a2, b2 = a.shape[-2:], b.shape[-2:]
    if mode == "nn":
        (m, k), n = a2, b2[1]
        a_spec = _stacked((tm, tk), lambda i, j, kk: (i, kk), a_sel)
        b_spec = _stacked((tk, tn), lambda i, j, kk: (kk + b_k_off, j), b_sel)
        dims = _NN
    elif mode == "nt":
        (m, k), n = a2, b2[0]
        a_spec = _stacked((tm, tk), lambda i, j, kk: (i, kk), a_sel)
        b_spec = _stacked((tn, tk), lambda i, j, kk: (j, kk + b_k_off), b_sel)
        dims = _NT
    else:
        (k, m), n = a2, b2[1]
        a_spec = _stacked((tk, tm), lambda i, j, kk: (kk, i), a_sel)
        b_spec = _stacked((tk, tn), lambda i, j, kk: (kk + b_k_off, j), b_sel)
        dims = _TN
    assert m % tm == 0 and n % tn == 0 and k % tk == 0, (name, a.shape, b.shape)
    nk = k // tk
    has_res = res is not None
    if out_slab is None:
        o_spec = pl.BlockSpec((tm, tn), lambda i, j, kk: (i, j + out_col_off))
        out_shape = jax.ShapeDtypeStruct((m, n if out_cols is None else out_cols), out_dtype)
    else:
        o_spec = _stacked((tm, tn), lambda i, j, kk: (i, j + out_col_off), out_slab[0])
        out_shape = jax.ShapeDtypeStruct((out_slab[1], m, n if out_cols is None else out_cols), out_dtype)
    r_spec = pl.BlockSpec((tm, tn), lambda i, j, kk: (i, j))
    n_in = 2 + has_res + (out_into is not None)

    def body(*refs):
        a_ref, b_ref = refs[0], refs[1]
        r_ref = refs[2] if has_res else None
        o_ref = refs[n_in]
        p = _dot(a_ref[...], b_ref[...], dims)

        def finish(acc):
            y = acc * alpha if alpha != 1.0 else acc
            if has_res:
                y = y + r_ref[...].astype(F32)
            o_ref[...] = y.astype(o_ref.dtype)

        if nk == 1:
            finish(p)
        else:
            acc_ref = refs[n_in + 1]
            kk = pl.program_id(2)

            @pl.when(kk == 0)
            def _():
                acc_ref[...] = p

            @pl.when(kk > 0)
            def _():
                acc_ref[...] += p

            @pl.when(kk == nk - 1)
            def _():
                finish(acc_ref[...])

    operands = [a, b] + ([res] if has_res else [])
    in_specs = [a_spec, b_spec] + ([r_spec] if has_res else [])
    aliases = {}
    if out_into is not None:
        aliases = {len(operands): 0}
        operands.append(out_into)
        in_specs.append(HBM)
    return pl.pallas_call(
        body, name=name, grid=(m // tm, n // tn, nk), in_specs=in_specs, out_specs=o_spec, out_shape=out_shape,
        scratch_shapes=[pltpu.VMEM((tm, tn), F32)] if nk > 1 else [], input_output_aliases=aliases,
        compiler_params=_params(dimension_semantics=("parallel", "parallel", "arbitrary")),
    )(*operands)


def _mm_swiglu_fwd(h, w_up, *, tm, tn, name):
    m, k = h.shape
    n = w_up.shape[1] // 2
    h_spec = pl.BlockSpec((tm, k), lambda i, j: (i, 0))
    wg_spec = pl.BlockSpec((k, tn), lambda i, j: (0, j))
    wu_spec = pl.BlockSpec((k, tn), lambda i, j: (0, j + n // tn))
    o_spec = pl.BlockSpec((tm, tn), lambda i, j: (i, j))

    def body(h_ref, wg_ref, wu_ref, a_ref, g_ref, u_ref):
        hb = h_ref[...]
        g = _dot(hb, wg_ref[...], _NN)
        u = _dot(hb, wu_ref[...], _NN)
        a_ref[...] = (g * jax.nn.sigmoid(g) * u).astype(BF16)
        g_ref[...] = g.astype(BF16)
        u_ref[...] = u.astype(BF16)

    out = jax.ShapeDtypeStruct((m, n), BF16)
    return pl.pallas_call(
        body, name=name, grid=(m // tm, n // tn), in_specs=[h_spec, wg_spec, wu_spec],
        out_specs=[o_spec] * 3, out_shape=[out] * 3,
        compiler_params=_params(dimension_semantics=("parallel", "parallel")),
    )(h, w_up, w_up)


def _mm_swiglu_bwd(dy, w_down, gate, up, *, alpha, tm, tn, name):
    m, k = dy.shape
    n = w_down.shape[0]
    dy_spec = pl.BlockSpec((tm, k), lambda i, j: (i, 0))
    w_spec = pl.BlockSpec((tn, k), lambda i, j: (j, 0))
    o_spec = pl.BlockSpec((tm, tn), lambda i, j: (i, j))

    def body(dy_ref, w_ref, g_ref, u_ref, dg_ref, du_ref):
        da = _dot(dy_ref[...], w_ref[...], _NT) * alpha
        g = g_ref[...].astype(F32)
        u = u_ref[...].astype(F32)
        sg = jax.nn.sigmoid(g)
        dg_ref[...] = (da * u * (sg * (1.0 + g * (1.0 - sg)))).astype(BF16)
        du_ref[...] = (da * (g * sg)).astype(BF16)

    out = jax.ShapeDtypeStruct((m, n), BF16)
    return pl.pallas_call(
        body, name=name, grid=(m // tm, n // tn), in_specs=[dy_spec, w_spec, o_spec, o_spec],
        out_specs=[o_spec] * 2, out_shape=[out] * 2,
        compiler_params=_params(dimension_semantics=("parallel", "parallel")),
    )(dy, w_down, gate, up)


def _rms_fwd(x, g, *, tt, name):
    t, d = x.shape

    def body(x_ref, g_ref, h_ref):
        xv = x_ref[...]
        rstd = lax.rsqrt(jnp.mean(xv * xv, axis=1, keepdims=True) + RMS_EPS)
        h_ref[...] = (xv * rstd * g_ref[...]).astype(BF16)

    return pl.pallas_call(
        body, name=name, grid=(t // tt,),
        in_specs=[pl.BlockSpec((tt, d), lambda i: (i, 0)), pl.BlockSpec((1, d), lambda i: (0, 0))],
        out_specs=pl.BlockSpec((tt, d), lambda i: (i, 0)), out_shape=jax.ShapeDtypeStruct((t, d), BF16),
        compiler_params=_params(dimension_semantics=("parallel",)),
    )(x, g)


def _rms_bwd(dh, x, g, dres, *, tt, name):
    t, d = x.shape

    def body(dh_ref, x_ref, g_ref, r_ref, dx_ref, dxb_ref, dg_ref):
        xv = x_ref[...]
        rstd = lax.rsqrt(jnp.mean(xv * xv, axis=1, keepdims=True) + RMS_EPS)
        xhat = xv * rstd
        dhv = dh_ref[...]
        dxhat = dhv * g_ref[...]
        dx = r_ref[...] + rstd * (dxhat - xhat * jnp.mean(dxhat * xhat, axis=1, keepdims=True))
        dx_ref[...] = dx
        dxb_ref[...] = dx.astype(BF16)

        @pl.when(pl.program_id(0) == 0)
        def _():
            dg_ref[...] = jnp.zeros_like(dg_ref)

        dg_ref[...] += jnp.sum(dhv * xhat, axis=0, keepdims=True)

    row = pl.BlockSpec((tt, d), lambda i: (i, 0))
    vec = pl.BlockSpec((1, d), lambda i: (0, 0))
    return pl.pallas_call(
        body, name=name, grid=(t // tt,), in_specs=[row, row, vec, row], out_specs=[row, row, vec],
        out_shape=[jax.ShapeDtypeStruct((t, d), F32), jax.ShapeDtypeStruct((t, d), BF16), jax.ShapeDtypeStruct((1, d), F32)],
        compiler_params=_params(dimension_semantics=("arbitrary",)),
    )(dh, x, g, dres)


def _final_loss(x, g, target, *, tt, name):
    t, d = x.shape

    def body(x_ref, g_ref, t_ref, dx_ref, dxb_ref, dg_ref, loss_ref):
        xv = x_ref[...]
        gv = g_ref[...]
        rstd = lax.rsqrt(jnp.mean(xv * xv, axis=1, keepdims=True) + RMS_EPS)
        xhat = xv * rstd
        err = xhat * gv - t_ref[...]
        dy = err * (1.0 / d)
        dxhat = dy * gv
        dx = rstd * (dxhat - xhat * jnp.mean(dxhat * xhat, axis=1, keepdims=True))
        dx_ref[...] = dx
        dxb_ref[...] = dx.astype(BF16)

        @pl.when(pl.program_id(0) == 0)
        def _():
            dg_ref[...] = jnp.zeros_like(dg_ref)
            loss_ref[...] = jnp.zeros_like(loss_ref)

        dg_ref[...] += jnp.sum(dy * xhat, axis=0, keepdims=True)
        part = 0.5 * jnp.sum(jnp.mean(err * err, axis=1, keepdims=True), axis=0, keepdims=True)
        loss_ref[...] += jnp.broadcast_to(part, loss_ref.shape)

    row = pl.BlockSpec((tt, d), lambda i: (i, 0))
    vec = pl.BlockSpec((1, d), lambda i: (0, 0))
    one = pl.BlockSpec((1, LANES), lambda i: (0, 0))
    return pl.pallas_call(
        body, name=name, grid=(t // tt,), in_specs=[row, vec, row], out_specs=[row, row, vec, one],
        out_shape=[jax.ShapeDtypeStruct((t, d), F32), jax.ShapeDtypeStruct((t, d), BF16), jax.ShapeDtypeStruct((1, d), F32),
                   jax.ShapeDtypeStruct((1, LANES), F32)],
        compiler_params=_params(dimension_semantics=("arbitrary",)),
    )(x, g, target)


def _swap_halves(x):
    lane = lax.broadcasted_iota(jnp.int32, x.shape, 1)
    return jnp.where((lane // 32) % 2 == 0, pltpu.roll(x, 96, 1), pltpu.roll(x, 32, 1))


def _rope_tables(s):
    half = HEAD_DIM // 2
    inv_freq = ROPE_THETA ** (-jnp.arange(half, dtype=F32) / half)
    ang = jnp.arange(s).astype(F32)[:, None] * inv_freq[None, :]
    cos, sin = jnp.cos(ang), jnp.sin(ang)
    return jnp.tile(cos, (1, 4)), jnp.concatenate([-sin, sin, -sin, sin], axis=1)


def _dilation_of_tile(p):
    dilated = p < 3 * DIL_HEADS // 2
    g = (p % (DIL_HEADS // 2)) // (DIL_GROUP_HEADS // 2)
    return [(dilated & (g == gi)) | (jnp.logical_not(dilated) if gi == 0 else False) for gi in range(len(DILATIONS))]


def _residue_major(ref, d):
    s = ref.shape[0]
    if d == 1:
        return ref[...]
    return jnp.concatenate([ref[pl.ds(r, s // d, stride=d), :] for r in range(d)], axis=0)


def _split_heads(proj, cos4, sin4, *, n_pairs, rot_pairs, scale_ranges, name):
    b, s, _ = proj.shape

    def body(x_ref, c_ref, s_ref, o_ref):
        p = pl.program_id(1)
        is_q = functools.reduce(jnp.logical_or, [(p >= lo) & (p < hi) for lo, hi in scale_ranges])
        scale = jnp.where(is_q, QK_SCALE, 1.0)

        def put(y):
            o_ref[0] = y[:, :HEAD_DIM].astype(BF16)
            o_ref[1] = y[:, HEAD_DIM:].astype(BF16)

        for d, in_group in zip(DILATIONS, _dilation_of_tile(p)):
            @pl.when(in_group & (p < rot_pairs))
            def _(d=d):
                x = _residue_major(x_ref, d)
                put((x * _residue_major(c_ref, d) + _swap_halves(x) * _residue_major(s_ref, d)) * scale)

            @pl.when(in_group & (p >= rot_pairs))
            def _(d=d):
                put(_residue_major(x_ref, d) * scale)

    tab = pl.BlockSpec((s, LANES), lambda bi, p: (0, 0))
    return pl.pallas_call(
        body, name=name, grid=(b, n_pairs),
        in_specs=[pl.BlockSpec((None, s, LANES), lambda bi, p: (bi, 0, p)), tab, tab],
        out_specs=pl.BlockSpec((None, 2, s, HEAD_DIM), lambda bi, p: (bi, p, 0, 0)),
        out_shape=jax.ShapeDtypeStruct((b, 2 * n_pairs, s, HEAD_DIM), BF16),
        compiler_params=_params(dimension_semantics=("parallel", "parallel")),
    )(proj, cos4, sin4)


def _merge_heads(dheads, cos4, sin4, *, heads_per_row, rot_pairs, scale_pairs, dilated, out_cols, tile_off, into, name):
    b, hpr, r, s, _ = dheads.shape
    n_pairs = hpr * r // 2
    ppr = hpr // 2

    def body(d_ref, c_ref, s_ref, *rest):
        o_ref, t_ref = rest[-2:]
        p = pl.program_id(1)
        scale = jnp.where(p < scale_pairs, QK_SCALE, 1.0)

        def tokens(d):
            dy = jnp.concatenate([d_ref[0], d_ref[1]], axis=1)
            if d == 1:
                return dy
            for res in range(d):
                t_ref[pl.ds(res, s // d, stride=d), :] = dy[res * (s // d):(res + 1) * (s // d), :]
            return t_ref[...]

        groups = _dilation_of_tile(p) if dilated else [p >= 0]
        for d, in_group in zip(DILATIONS, groups):
            @pl.when(in_group & (p < rot_pairs))
            def _(d=d):
                dy = tokens(d)
                o_ref[...] = ((dy * c_ref[...] - _swap_halves(dy) * s_ref[...]) * scale).astype(BF16)

            @pl.when(in_group & (p >= rot_pairs))
            def _(d=d):
                o_ref[...] = (tokens(d) * scale).astype(BF16)

    tab = pl.BlockSpec((s, LANES), lambda bi, p: (0, 0))
    operands = [dheads, cos4, sin4] + ([] if into is None else [into])
    return pl.pallas_call(
        body, name=name, grid=(b, n_pairs),
        in_specs=[pl.BlockSpec((None, 2, None, s, HEAD_DIM), lambda bi, p: (bi, p % ppr, p // ppr, 0, 0)), tab, tab]
        + ([] if into is None else [HBM]),
        out_specs=pl.BlockSpec((None, s, LANES), lambda bi, p: (bi, 0, p + tile_off)),
        out_shape=jax.ShapeDtypeStruct((b, s, out_cols), BF16),
        input_output_aliases={} if into is None else {3: 0},
        scratch_shapes=[pltpu.VMEM((s, LANES), F32)],
        compiler_params=_params(dimension_semantics=("parallel", "parallel")),
    )(*operands)


DIL_TQ = 256


def _dil_block(g, s):
    run = s // DILATIONS[g]
    return DIL_TQ if run <= DIL_TQ else min(run, DIL_TQ + 2 * LANES)


def _dil_keys(g, q0, s):
    run = max(s // DILATIONS[g], DIL_TQ)
    lo = (q0 // run) * run
    return pl.multiple_of(jnp.clip(q0 - LANES, lo, lo + run - _dil_block(g, s)), LANES)


def _dil_band(g, q0, start, shape, s):
    row = q0 + lax.broadcasted_iota(jnp.int32, shape, 0)
    col = start + lax.broadcasted_iota(jnp.int32, shape, 1)
    ok = jnp.abs(row - col) <= DIL_HALF
    run = s // DILATIONS[g]
    if run < DIL_TQ:
        shift = run.bit_length() - 1
        ok = ok & ((row >> shift) == (col >> shift))
    return ok


def _dil_tokens(g, q0, s):
    d = DILATIONS[g]
    if d == 1:
        return [(0, DIL_TQ, pl.ds(q0, DIL_TQ))]
    run = s // d
    n = min(run, DIL_TQ)
    return [(lo, n, pl.ds(((q0 + lo) % run) * d + (q0 + lo) // run, n, stride=d)) for lo in range(0, DIL_TQ, n)]


def _dil_gather(ref, pieces):
    return jnp.concatenate([ref[rows, :] for _, _, rows in pieces], axis=0) if len(pieces) > 1 else ref[pieces[0][2], :]


def _dil_head_spec(part, g, s):
    return pl.BlockSpec((None, None, s, HEAD_DIM), lambda b, j: (b, part * DIL_HEADS + g * DIL_GROUP_HEADS + j, 0, 0))


def _dil_attn_fwd(heads, *, name):
    b, _, s, _ = heads.shape
    n_g = len(DILATIONS)

    def body(*refs):
        qkv = refs[:3 * n_g]
        o_ref, l_ref, og_ref, lg_ref = refs[3 * n_g:]
        for g in range(n_g):
            q_ref, k_ref, v_ref = qkv[3 * g:3 * g + 3]
            width = _dil_block(g, s)

            def step(i, carry, g=g, q_ref=q_ref, k_ref=k_ref, v_ref=v_ref, width=width):
                q0 = pl.multiple_of(i * DIL_TQ, DIL_TQ)
                start = _dil_keys(g, q0, s)
                sc = _dot(q_ref[pl.ds(q0, DIL_TQ), :], k_ref[pl.ds(start, width), :], _NT)
                sc = jnp.where(_dil_band(g, q0, start, sc.shape, s), sc, NEG_INF)
                m = jnp.max(sc, axis=1, keepdims=True)
                p = jnp.exp(sc - m)
                den = jnp.sum(p, axis=1, keepdims=True)
                o = _dot(p.astype(BF16), v_ref[pl.ds(start, width), :], _NN) / den
                lse = m + jnp.log(den)
                for lo, n, rows in _dil_tokens(g, q0, s):
                    og_ref[g, rows, :] = o[lo:lo + n]
                    lg_ref[g, rows, :] = lse[lo:lo + n]
                return carry

            lax.fori_loop(0, s // DIL_TQ, step, 0, unroll=BLOCKS_IN_FLIGHT)
        lses = [lg_ref[g] for g in range(n_g)]
        m = functools.reduce(jnp.maximum, lses)
        ws = [jnp.exp(l - m) for l in lses]
        den = functools.reduce(jnp.add, ws)
        o_ref[...] = (functools.reduce(jnp.add, [w * og_ref[g] for g, w in enumerate(ws)]) / den).astype(o_ref.dtype)
        l_ref[...] = m + jnp.log(den)

    out = pl.BlockSpec((None, None, s, HEAD_DIM), lambda bi, j: (bi, j, 0, 0))
    lse = pl.BlockSpec((None, None, s, 1), lambda bi, j: (bi, j, 0, 0))
    return pl.pallas_call(
        body, name=name, grid=(b, DIL_GROUP_HEADS),
        in_specs=[_dil_head_spec(part, g, s) for g in range(n_g) for part in range(3)],
        out_specs=[out, lse],
        out_shape=[jax.ShapeDtypeStruct((b, DIL_GROUP_HEADS, s, HEAD_DIM), BF16),
                   jax.ShapeDtypeStruct((b, DIL_GROUP_HEADS, s, 1), F32)],
        scratch_shapes=[pltpu.VMEM((n_g, s, HEAD_DIM), F32), pltpu.VMEM((n_g, s, 1), F32)],
        compiler_params=_params(dimension_semantics=("parallel", "parallel")),
    )(*([heads] * (3 * n_g)))


def _dil_attn_bwd(heads, out, lse, dout, *, name):
    b, _, s, _ = heads.shape
    n_g = len(DILATIONS)

    def body(*refs):
        qkv = refs[:3 * n_g]
        o_ref, l_ref, do_ref, d_ref, delta_ref = refs[3 * n_g:]
        d_ref[...] = jnp.zeros_like(d_ref)
        delta_ref[...] = jnp.sum(do_ref[...] * o_ref[...].astype(F32), axis=1, keepdims=True)
        for g in range(n_g):
            q_ref, k_ref, v_ref = qkv[3 * g:3 * g + 3]
            width = _dil_block(g, s)

            def step(i, carry, g=g, q_ref=q_ref, k_ref=k_ref, v_ref=v_ref, width=width):
                q0 = pl.multiple_of(i * DIL_TQ, DIL_TQ)
                start = _dil_keys(g, q0, s)
                win = pl.ds(start, width)
                pieces = _dil_tokens(g, q0, s)
                do_b = _dil_gather(do_ref, pieces).astype(BF16)
                q, k, v = q_ref[pl.ds(q0, DIL_TQ), :], k_ref[win, :], v_ref[win, :]
                sc = _dot(q, k, _NT)
                p = jnp.where(_dil_band(g, q0, start, sc.shape, s), jnp.exp(sc - _dil_gather(l_ref, pieces)), 0.0)
                ds = (p * (_dot(do_b, v, _NT) - _dil_gather(delta_ref, pieces))).astype(BF16)
                d_ref[g, pl.ds(q0, DIL_TQ), :] = _dot(ds, k, _NN)
                d_ref[n_g + g, win, :] += _dot(ds, q, _TN)
                d_ref[2 * n_g + g, win, :] += _dot(p.astype(BF16), do_b, _TN)
                return carry

            lax.fori_loop(0, s // DIL_TQ, step, 0, unroll=BLOCKS_IN_FLIGHT)

    per_head = lambda bi, j: (bi, j, 0, 0)
    return pl.pallas_call(
        body, name=name, grid=(b, DIL_GROUP_HEADS),
        in_specs=[_dil_head_spec(part, g, s) for g in range(n_g) for part in range(3)]
        + [pl.BlockSpec((None, None, s, HEAD_DIM), per_head), pl.BlockSpec((None, None, s, 1), per_head),
           pl.BlockSpec((None, None, s, HEAD_DIM), per_head)],
        out_specs=pl.BlockSpec((None, None, 3 * n_g, s, HEAD_DIM), lambda bi, j: (bi, j, 0, 0, 0)),
        out_shape=jax.ShapeDtypeStruct((b, DIL_GROUP_HEADS, 3 * n_g, s, HEAD_DIM), F32),
        scratch_shapes=[pltpu.VMEM((s, 1), F32)],
        compiler_params=_params(dimension_semantics=("parallel", "parallel")),
    )(*([heads] * (3 * n_g)), out, lse, dout)


NA_BIAS_ROWS = 2 * NA_ROWS - 1
NA_BIAS_COLS = 2 * NA_COLS - 1
NA_BLOCK = 4
NA_SPAN = NA_ROWS + NA_BLOCK - 1
NA_Q = NA_BLOCK * GRID_W
NA_KEYS = NA_SPAN * GRID_W
NA_FORMS = 3


def _na_onehot():
    c = np.arange(GRID_W)[:, None]
    k = np.arange(GRID_W)[None, :]
    lo = np.clip(c - NA_COLS // 2, 0, GRID_W - NA_COLS)
    valid = (k >= lo) & (k < lo + NA_COLS)
    onehot = np.zeros((GRID_W, GRID_W, LANES), np.float32)
    cc, kk = np.nonzero(valid)
    onehot[cc, kk, kk - cc + NA_COLS - 1] = 1.0
    return onehot.reshape(GRID_W * GRID_W, LANES), valid.reshape(1, GRID_W * GRID_W)


def _na_block_rows(n_rows):
    table = np.full((NA_FORMS, NA_BLOCK, NA_SPAN), NA_BIAS_ROWS, np.int64)
    n_blocks = n_rows // NA_BLOCK
    for form, ib in enumerate((0, 1, n_blocks - 1)):
        base = min(max(NA_BLOCK * ib - NA_ROWS // 2, 0), n_rows - NA_SPAN)
        for rl in range(NA_BLOCK):
            r = NA_BLOCK * ib + rl
            row_lo = min(max(r - NA_ROWS // 2, 0), n_rows - NA_ROWS)
            for kl in range(NA_SPAN):
                if row_lo <= base + kl < row_lo + NA_ROWS:
                    table[form, rl, kl] = base + kl - r + NA_ROWS - 1
    return table


def _na_block(ib, n_rows):
    n_blocks = n_rows // NA_BLOCK
    base = jnp.clip(NA_BLOCK * ib - NA_ROWS // 2, 0, n_rows - NA_SPAN)
    return base, jnp.where(ib == 0, 0, jnp.where(ib == n_blocks - 1, 2, 1))


def _na_expand_bias(rel_bias, *, name):
    l, h, nr, nc = rel_bias.shape
    onehot, valid = _na_onehot()
    rb = jnp.pad(rel_bias, ((0, 0), (0, 0), (0, 1), (0, LANES - nc))).reshape(l * h * (nr + 1), LANES)
    live = jnp.asarray(np.tile(np.arange(nr + 1) < nr, l * h).astype(np.float32)[:, None])

    def body(rb_ref, oh_ref, valid_ref, live_ref, e_ref):
        e = lax.dot_general(rb_ref[...], oh_ref[...], _NT, precision=lax.Precision.HIGHEST, preferred_element_type=F32)
        e_ref[...] = jnp.where((valid_ref[...] > 0) & (live_ref[...] > 0), e, NEG_INF)

    e = pl.pallas_call(
        body, name=name, out_shape=jax.ShapeDtypeStruct((l * h * (nr + 1), GRID_W * GRID_W), F32), compiler_params=_params(),
    )(rb, jnp.asarray(onehot), jnp.asarray(valid.astype(np.float32)), live)
    return e.reshape(l, h, nr + 1, GRID_W, GRID_W)


def _na_collapse_bias(de, *, name):
    b, h = de.shape[:2]
    onehot, _ = _na_onehot()
    rows = h * NA_BIAS_ROWS

    def diag(e_ref, oh_ref, o_ref):
        e = e_ref[0]
        for bi in range(1, b):
            e = e + e_ref[bi]
        o_ref[...] = lax.dot_general(e, oh_ref[...], _NN, precision=lax.Precision.HIGHEST, preferred_element_type=F32)

    drb = pl.pallas_call(
        diag, name=name, out_shape=jax.ShapeDtypeStruct((rows, LANES), F32), compiler_params=_params(),
    )(de.reshape(b, rows, GRID_W * GRID_W), jnp.asarray(onehot))
    return drb[:, :NA_BIAS_COLS].reshape(h, NA_BIAS_ROWS, NA_BIAS_COLS)


def _na_tiles(n_rows):
    table = _na_block_rows(n_rows)
    return [(f, rl, kl, int(table[f, rl, kl])) for f in range(NA_FORMS) for rl in range(NA_BLOCK) for kl in range(NA_SPAN)]


def _na_tile(ref, form, rl, kl):
    return ref.at[form, rl * GRID_W:(rl + 1) * GRID_W, kl * GRID_W:(kl + 1) * GRID_W]


def _na_head_spec(part, first, s):
    return pl.BlockSpec((None, None, s, HEAD_DIM), lambda b, h: (b, first + part * NA_HEADS + h, 0, 0))


def _na_attn_fwd(heads, bias, *, first, name):
    b, _, s, _ = heads.shape
    n_rows = s // GRID_W
    tiles = _na_tiles(n_rows)

    def body(q_ref, k_ref, v_ref, e_ref, o_ref, l_ref, b_ref):
        for form, rl, kl, i in tiles:
            _na_tile(b_ref, form, rl, kl)[...] = e_ref[i]

        def step(ib, carry):
            base, form = _na_block(ib, n_rows)
            rows = pl.ds(pl.multiple_of(ib * NA_Q, NA_Q), NA_Q)
            win = pl.ds(pl.multiple_of(base * GRID_W, GRID_W), NA_KEYS)
            sc = _dot(q_ref[rows, :], k_ref[win, :], _NT) + b_ref[form]
            m = jnp.max(sc, axis=1, keepdims=True)
            p = jnp.exp(sc - m)
            den = jnp.sum(p, axis=1, keepdims=True)
            o_ref[rows, :] = (_dot(p.astype(BF16), v_ref[win, :], _NN) / den).astype(o_ref.dtype)
            l_ref[rows, :] = m + jnp.log(den)
            return carry

        lax.fori_loop(0, n_rows // NA_BLOCK, step, 0, unroll=BLOCKS_IN_FLIGHT)

    per_head = lambda bi, h: (bi, h, 0, 0)
    return pl.pallas_call(
        body, name=name, grid=(b, NA_HEADS),
        in_specs=[_na_head_spec(part, first, s) for part in range(3)]
        + [pl.BlockSpec((None, NA_BIAS_ROWS + 1, GRID_W, GRID_W), lambda bi, h: (h, 0, 0, 0))],
        out_specs=[pl.BlockSpec((None, None, s, HEAD_DIM), per_head), pl.BlockSpec((None, None, s, 1), per_head)],
        out_shape=[jax.ShapeDtypeStruct((b, NA_HEADS, s, HEAD_DIM), BF16), jax.ShapeDtypeStruct((b, NA_HEADS, s, 1), F32)],
        scratch_shapes=[pltpu.VMEM((NA_FORMS, NA_Q, NA_KEYS), F32)],
        compiler_params=_params(dimension_semantics=("parallel", "parallel")),
    )(heads, heads, heads, bias)


def _na_attn_bwd(heads, bias, out, lse, dout, *, first, name):
    b, _, s, _ = heads.shape
    n_rows = s // GRID_W
    tiles = _na_tiles(n_rows)

    def body(q_ref, k_ref, v_ref, e_ref, o_ref, l_ref, do_ref, d_ref, de_ref, b_ref, db_ref):
        for form, rl, kl, i in tiles:
            _na_tile(b_ref, form, rl, kl)[...] = e_ref[i]
        d_ref[...] = jnp.zeros_like(d_ref)
        db_ref[...] = jnp.zeros_like(db_ref)

        def step(ib, carry):
            base, form = _na_block(ib, n_rows)
            rows = pl.ds(pl.multiple_of(ib * NA_Q, NA_Q), NA_Q)
            win = pl.ds(pl.multiple_of(base * GRID_W, GRID_W), NA_KEYS)
            q, k, v = q_ref[rows, :], k_ref[win, :], v_ref[win, :]
            do = do_ref[rows, :]
            delta = jnp.sum(do * o_ref[rows, :].astype(F32), axis=1, keepdims=True)
            do_b = do.astype(BF16)
            p = jnp.exp(_dot(q, k, _NT) + b_ref[form] - l_ref[rows, :])
            ds = p * (_dot(do_b, v, _NT) - delta)
            db_ref[form] += ds
            ds_b = ds.astype(BF16)
            d_ref[0, rows, :] = _dot(ds_b, k, _NN)
            d_ref[1, win, :] += _dot(ds_b, q, _TN)
            d_ref[2, win, :] += _dot(p.astype(BF16), do_b, _TN)
            return carry

        lax.fori_loop(0, n_rows // NA_BLOCK, step, 0, unroll=BLOCKS_IN_FLIGHT)
        acc = [None] * NA_BIAS_ROWS
        for form, rl, kl, i in tiles:
            if i < NA_BIAS_ROWS:
                t = _na_tile(db_ref, form, rl, kl)[...]
                acc[i] = t if acc[i] is None else acc[i] + t
        for i in range(NA_BIAS_ROWS):
            de_ref[i] = acc[i]

    per_head = lambda bi, h: (bi, h, 0, 0)
    return pl.pallas_call(
        body, name=name, grid=(b, NA_HEADS),
        in_specs=[_na_head_spec(part, first, s) for part in range(3)]
        + [pl.BlockSpec((None, NA_BIAS_ROWS + 1, GRID_W, GRID_W), lambda bi, h: (h, 0, 0, 0)),
           pl.BlockSpec((None, None, s, HEAD_DIM), per_head), pl.BlockSpec((None, None, s, 1), per_head),
           pl.BlockSpec((None, None, s, HEAD_DIM), per_head)],
        out_specs=[pl.BlockSpec((None, None, 3, s, HEAD_DIM), lambda bi, h: (bi, h, 0, 0, 0)),
                   pl.BlockSpec((None, None, NA_BIAS_ROWS, GRID_W, GRID_W), lambda bi, h: (bi, h, 0, 0, 0))],
        out_shape=[jax.ShapeDtypeStruct((b, NA_HEADS, 3, s, HEAD_DIM), F32),
                   jax.ShapeDtypeStruct((b, NA_HEADS, NA_BIAS_ROWS, GRID_W, GRID_W), F32)],
        scratch_shapes=[pltpu.VMEM((NA_FORMS, NA_Q, NA_KEYS), F32), pltpu.VMEM((NA_FORMS, NA_Q, NA_KEYS), F32)],
        compiler_params=_params(dimension_semantics=("parallel", "parallel")),
    )(heads, heads, heads, bias, out, lse, dout)


GATE_TILE = 256


def _gate_fwd(proj, z, *, gate_col, tt, name):
    _, t, d = z.shape
    nj = d // GATE_TILE
    c0 = gate_col // GATE_TILE

    def body(ga_ref, gb_ref, za_ref, zb_ref, o_ref):
        o_ref[...] = (jax.nn.sigmoid(ga_ref[...]) * za_ref[...] + jax.nn.sigmoid(gb_ref[...]) * zb_ref[...]).astype(BF16)

    return pl.pallas_call(
        body, name=name, grid=(t // tt, nj),
        in_specs=[pl.BlockSpec((tt, GATE_TILE), lambda i, j: (i, c0 + j)),
                  pl.BlockSpec((tt, GATE_TILE), lambda i, j: (i, c0 + nj + j)),
                  pl.BlockSpec((None, tt, GATE_TILE), lambda i, j: (0, i, j)),
                  pl.BlockSpec((None, tt, GATE_TILE), lambda i, j: (1, i, j))],
        out_specs=pl.BlockSpec((tt, GATE_TILE), lambda i, j: (i, j)), out_shape=jax.ShapeDtypeStruct((t, d), BF16),
        compiler_params=_params(dimension_semantics=("parallel", "parallel")),
    )(proj, proj, z, z)


def _gate_bwd(dm, proj, z, *, gate_col, tt, name):
    _, t, d = z.shape
    nj = d // GATE_TILE
    c0 = gate_col // GATE_TILE

    def body(dm_ref, g_ref, z_ref, dz_ref, dg_ref):
        dmv = dm_ref[...]
        sg = jax.nn.sigmoid(g_ref[...])
        dz_ref[...] = (dmv * sg).astype(BF16)
        dg_ref[...] = (dmv * z_ref[...] * sg * (1.0 - sg)).astype(BF16)

    return pl.pallas_call(
        body, name=name, grid=(t // tt, 2 * nj),
        in_specs=[pl.BlockSpec((tt, GATE_TILE), lambda i, j: (i, j % nj)),
                  pl.BlockSpec((tt, GATE_TILE), lambda i, j: (i, c0 + j)),
                  pl.BlockSpec((None, tt, GATE_TILE), lambda i, j: (j // nj, i, j % nj))],
        out_specs=[pl.BlockSpec((None, tt, GATE_TILE), lambda i, j: (j // nj, i, j % nj)),
                   pl.BlockSpec((tt, GATE_TILE), lambda i, j: (i, c0 + j))],
        out_shape=[jax.ShapeDtypeStruct((2, t, d), BF16), jax.ShapeDtypeStruct(proj.shape, BF16)],
        compiler_params=_params(dimension_semantics=("parallel", "parallel")),
    )(dm, proj, z)


def _adamw(w, g, m, v, *, name):
    shape = w.shape
    if w.ndim == 3:
        w2, g2, m2, v2 = w, g, m, v
    else:
        w2, g2, m2, v2 = (t.reshape(1, -1, shape[-1]) for t in (w, g, m, v))
    lead, rows, cols = w2.shape
    tr = rows
    for cand in (512, 256, 128, 64, 32, 16, 8):
        if rows % cand == 0:
            tr = cand
            break

    def body(w_ref, g_ref, m_ref, v_ref, d_ref, nm_ref, nv_ref):
        gv = g_ref[...]
        nm = ADAM_B1 * m_ref[...] + (1.0 - ADAM_B1) * gv
        nv = ADAM_B2 * v_ref[...] + (1.0 - ADAM_B2) * (gv * gv)
        m_hat = nm / (1.0 - ADAM_B1 ** ADAM_STEP)
        v_hat = nv / (1.0 - ADAM_B2 ** ADAM_STEP)
        d_ref[...] = -ADAM_LR * (m_hat / (jnp.sqrt(v_hat) + ADAM_EPS) + ADAM_WD * w_ref[...])
        nm_ref[...] = nm
        nv_ref[...] = nv

    blk = pl.BlockSpec((None, tr, cols), lambda l, i: (l, i, 0))
    out = jax.ShapeDtypeStruct((lead, rows, cols), F32)
    res = pl.pallas_call(
        body, name=name, grid=(lead, rows // tr), in_specs=[blk] * 4, out_specs=[blk] * 3, out_shape=[out] * 3,
        compiler_params=_params(dimension_semantics=("parallel", "parallel")),
    )(w2, g2, m2, v2)
    return tuple(t.reshape(shape) for t in res)


def _my_place():
    return lax.axis_index("x"), lax.axis_index("y"), lax.axis_index("c")


def _other_chips(x, y):
    return [(1 - x, y), (x, 1 - y), (1 - x, 1 - y)]


def _chip_no(chip):
    return 2 * chip[0] + chip[1]


def _window(ref, kind, size, chip, lead):
    if kind == "col":
        return ref.at[(*lead, slice(None), pl.ds(pl.multiple_of(chip * size, LANES), size))]
    if kind == "row":
        return ref.at[(*lead, pl.ds(pl.multiple_of(chip * size, BF16_ROWS), size), slice(None))]
    shard = size + HEAD_DIM
    if kind == "win_main":
        return ref.at[(*lead, slice(None), pl.ds(pl.multiple_of(chip * shard + HEAD_DIM * (chip % 2), LANES), size))]
    assert kind == "win_strad"
    return ref.at[(*lead, slice(None), pl.ds(pl.multiple_of(size + 2 * shard * (chip // 2), LANES), LANES))]


def _full_shape(shard, kind):
    _, k, n = shard.shape
    return {"col": (k, N_CHIPS * n), "row": (N_CHIPS * k, n), "win_main": (k, N_CHIPS * (n + HEAD_DIM)),
            "slot": (N_CHIPS, k, n)}[kind]


def _place_own(shard, kind, layer, *, name):
    _, k, n = shard.shape
    tr = _div_tile(k, 512, BF16_ROWS)
    tc = LANES if kind == "win_main" else n
    mine = 2 * lax.axis_index("x") + lax.axis_index("y")
    row0 = mine * (k // tr) if kind == "row" else 0
    col0 = {"col": mine, "row": 0, "slot": 0, "win_main": (mine * (n + HEAD_DIM) + HEAD_DIM * (mine % 2)) // LANES}[kind]
    scalars = jnp.stack([mine, row0, col0]).astype(jnp.int32)

    def body(s_ref, i_ref, o_ref):
        o_ref[...] = i_ref[...]

    if kind == "slot":
        o_spec = pl.BlockSpec((None, tr, tc), lambda i, j, s: (s[0], i, j))
    else:
        o_spec = pl.BlockSpec((tr, tc), lambda i, j, s: (s[1] + i, s[2] + j))
    return pl.pallas_call(
        body, name=name,
        grid_spec=pltpu.PrefetchScalarGridSpec(
            num_scalar_prefetch=1, grid=(k // tr, n // tc),
            in_specs=[pl.BlockSpec((None, tr, tc), lambda i, j, s: (layer, i, j))], out_specs=o_spec),
        out_shape=jax.ShapeDtypeStruct(_full_shape(shard, kind), shard.dtype),
        compiler_params=_params(dimension_semantics=("parallel", "parallel")),
    )(scalars, shard)


class _GatherPlan:
    def __init__(self, src, dst, shapes, kinds, layer, send_sems, recv_sems):
        self.src, self.dst, self.shapes, self.kinds, self.layer = src, dst, shapes, kinds, layer
        self.send_sems, self.recv_sems = send_sems, recv_sems
        self.x, self.y, self.c = _my_place()
        self.mine = 2 * self.x + self.y
        self.chips = _other_chips(self.x, self.y)
        self.n = len(src)

    def half(self, i, chip, half):
        _, k, n = self.shapes[i]
        kind, dst, hk = self.kinds[i], self.dst[i], k // 2
        if kind == "slot":
            return dst.at[chip, pl.ds(pl.multiple_of(half * hk, BF16_ROWS), hk), :]
        if kind == "row":
            return dst.at[pl.ds(pl.multiple_of(chip * k + half * hk, BF16_ROWS), hk), :]
        col0 = chip * n if kind == "col" else chip * (n + HEAD_DIM) + HEAD_DIM * (chip % 2)
        return dst.at[pl.ds(pl.multiple_of(half * hk, BF16_ROWS), hk), pl.ds(pl.multiple_of(col0, LANES), n)]

    def _copy(self, sem, window, to, source=None):
        return pltpu.make_async_remote_copy(src_ref=window if source is None else source, dst_ref=window,
                                            send_sem=self.send_sems.at[sem], recv_sem=self.recv_sems.at[sem],
                                            device_id=to, device_id_type=MESH)

    def sends(self):
        out = []
        for k, chip in enumerate(self.chips):
            for i in range(self.n):
                hk = self.shapes[i][1] // 2
                mine = self.src[i].at[self.layer, pl.ds(pl.multiple_of(self.c * hk, BF16_ROWS), hk), :]
                out.append(self._copy(3 * i + k, self.half(i, self.mine, self.c), (*chip, self.c), source=mine))
        return out

    def arrivals(self):
        return [self._copy(3 * i + k, self.half(i, _chip_no(chip), self.c), (*chip, self.c))
                for k, chip in enumerate(self.chips) for i in range(self.n)]

    def forwards(self, first_sem):
        sibling = (self.x, self.y, 1 - self.c)
        return [self._copy(first_sem + 3 * i + k, self.half(i, _chip_no(chip), self.c), sibling)
                for k, chip in enumerate(self.chips) for i in range(self.n)]

    def forwarded(self, first_sem):
        sibling = (self.x, self.y, 1 - self.c)
        return [self._copy(first_sem + 3 * i + k, self.half(i, _chip_no(chip), 1 - self.c), sibling)
                for k, chip in enumerate(self.chips) for i in range(self.n)]


IN_HBM = pl.BlockSpec(memory_space=pltpu.HBM)
IN_SEM = pl.BlockSpec(memory_space=pltpu.SEMAPHORE)
DATAFLOW = pltpu.SideEffectType.DATAFLOW_SIDE_EFFECTING


def _gather_layer_start(shards, kinds, fulls, layer, after, *, name):
    n_w = len(shards)
    shapes = [sh.shape for sh in shards]

    def body(*refs):
        plan = _GatherPlan(refs[:n_w], refs[n_w:2 * n_w], shapes, kinds, layer, refs[2 * n_w + 1], refs[2 * n_w + 2])
        for cp in plan.sends():
            cp.start()
        token = refs[-1]
        token[...] = jnp.zeros_like(token)

    operands = [pltpu.with_memory_space_constraint(a, pltpu.HBM) for a in (*shards, *fulls)]
    res = pl.pallas_call(
        body, name=name, in_specs=[IN_HBM] * (2 * n_w) + [pl.BlockSpec(memory_space=pl.ANY)],
        out_specs=(IN_SEM, IN_SEM, *([IN_HBM] * (2 * n_w)), pl.BlockSpec(memory_space=pltpu.VMEM)),
        out_shape=(pltpu.SemaphoreType.DMA((3 * n_w,)), pltpu.SemaphoreType.DMA((3 * n_w,)),
                   *[pltpu.HBM(a.shape, a.dtype) for a in operands], jax.ShapeDtypeStruct((8, LANES), F32)),
        input_output_aliases={i: 2 + i for i in range(2 * n_w)},
        compiler_params=pltpu.CompilerParams(has_side_effects=DATAFLOW),
    )(*operands, after)
    return res[0], res[1], res[2:2 + n_w], res[2 + n_w:2 + 2 * n_w], res[-1]


def _gather_layer_wait(send_sems, recv_sems, shards, fulls, kinds, layer, after, *, name):
    n_w = len(shards)
    shapes = [sh.shape for sh in shards]

    def body(*refs):
        plan = _GatherPlan(refs[:n_w], refs[n_w:2 * n_w], shapes, kinds, layer, refs[2 * n_w], refs[2 * n_w + 1])
        for cp in plan.sends():
            cp.wait_send()
        for cp in plan.arrivals():
            cp.wait_recv()

    res = pl.pallas_call(
        body, name=name, in_specs=[IN_HBM] * (2 * n_w) + [IN_SEM, IN_SEM, pl.BlockSpec(memory_space=pl.ANY)],
        out_specs=[IN_HBM] * (2 * n_w), out_shape=[pltpu.HBM(a.shape, a.dtype) for a in (*shards, *fulls)],
        input_output_aliases={i: i for i in range(2 * n_w)},
        compiler_params=pltpu.CompilerParams(has_side_effects=DATAFLOW),
    )(*shards, *fulls, send_sems, recv_sems, after)
    return res[n_w:]


def _gather_layer_forward(shapes, kinds, fulls, *, name):
    n_w = len(fulls)

    def body(*refs):
        plan = _GatherPlan([None] * n_w, refs[n_w:2 * n_w], shapes, kinds, 0, *refs[2 * n_w:])
        passed = plan.forwards(0)
        for cp in passed:
            cp.start()
        for cp in plan.forwarded(0):
            cp.wait_recv()
        for cp in passed:
            cp.wait_send()

    return pl.pallas_call(
        body, name=name, in_specs=[HBM] * n_w, out_specs=[HBM] * n_w,
        out_shape=[jax.ShapeDtypeStruct(f.shape, f.dtype) for f in fulls],
        input_output_aliases={i: i for i in range(n_w)},
        scratch_shapes=[pltpu.SemaphoreType.DMA((3 * n_w,)), pltpu.SemaphoreType.DMA((3 * n_w,))],
    )(*fulls)


def _on_core(layer):
    return (lax.axis_index("c") == layer).astype(jnp.int32).reshape(1)


N_DEVICES = 2 * N_CHIPS


class _ScatterPlan:
    def __init__(self, src, dst, kinds, sizes, layer, send_sems, recv_sems):
        self.src, self.dst, self.kinds, self.sizes, self.layer = src, dst, kinds, sizes, layer
        self.send_sems, self.recv_sems = send_sems, recv_sems
        self.x, self.y, self.c = _my_place()
        self.mine = 2 * self.x + self.y
        self.chips = _other_chips(self.x, self.y)
        self.n = len(src)

    def _copy(self, i, k, window_of, from_chip, from_core, to):
        return pltpu.make_async_remote_copy(src_ref=_window(self.src[i], self.kinds[i], self.sizes[i], window_of, ()),
                                            dst_ref=self.dst[i].at[2 * from_chip + from_core],
                                            send_sem=self.send_sems.at[4 * i + k],
                                            recv_sem=self.recv_sems.at[2 * (4 * i + k) + from_core],
                                            device_id=to, device_id_type=MESH)

    def to_chips(self):
        return [self._copy(i, k, _chip_no(chip), self.mine, self.c, (*chip, self.layer))
                for k, chip in enumerate(self.chips) for i in range(self.n)]

    def to_sibling(self):
        return [self._copy(i, 3, self.mine, self.mine, self.c, (self.x, self.y, self.layer)) for i in range(self.n)]

    def arrivals(self):
        out = [self._copy(i, k, self.mine, _chip_no(chip), core, (*chip, core))
               for k, chip in enumerate(self.chips) for core in (0, 1) for i in range(self.n)]
        return out + [self._copy(i, 3, self.mine, self.mine, 1 - self.layer, (self.x, self.y, 1 - self.layer))
                      for i in range(self.n)]


def _slab_shape(p, kind, size):
    return (N_DEVICES,) + {"col": (p.shape[0], size), "row": (size, p.shape[1]), "win_main": (p.shape[0], size),
                           "win_strad": (p.shape[0], LANES)}[kind]


def _grads_to_chips_start(pairs, kinds, sizes, layer, *, name):
    n_w = len(pairs)

    def body(*refs):
        plan = _ScatterPlan(refs[:n_w], refs[n_w:2 * n_w], kinds, sizes, layer, refs[2 * n_w], refs[2 * n_w + 1])
        for cp in plan.to_chips():
            cp.start()

        @pl.when(plan.c != layer)
        def _():
            for cp in plan.to_sibling():
                cp.start()

        token = refs[-1]
        token[...] = jnp.zeros_like(token)

    slabs = [lax.empty(_slab_shape(p, kind, size), p.dtype) for p, kind, size in zip(pairs, kinds, sizes)]
    operands = [pltpu.with_memory_space_constraint(a, pltpu.HBM) for a in (*pairs, *slabs)]
    res = pl.pallas_call(
        body, name=name, in_specs=[IN_HBM] * (2 * n_w),
        out_specs=(IN_SEM, IN_SEM, *([IN_HBM] * (2 * n_w)), pl.BlockSpec(memory_space=pltpu.VMEM)),
        out_shape=(pltpu.SemaphoreType.DMA((4 * n_w,)), pltpu.SemaphoreType.DMA((8 * n_w,)),
                   *[pltpu.HBM(a.shape, a.dtype) for a in operands], jax.ShapeDtypeStruct((8, LANES), F32)),
        input_output_aliases={i: 2 + i for i in range(2 * n_w)},
        compiler_params=pltpu.CompilerParams(has_side_effects=DATAFLOW),
    )(*operands)
    return res[0], res[1], res[2:2 + n_w], res[2 + n_w:2 + 2 * n_w], res[-1]


def _grads_to_chips_wait(send_sems, recv_sems, pairs, slabs, kinds, sizes, layer, after, *, name):
    n_w = len(pairs)

    def body(*refs):
        plan = _ScatterPlan(refs[:n_w], refs[n_w:2 * n_w], kinds, sizes, layer, refs[2 * n_w], refs[2 * n_w + 1])
        for cp in plan.to_chips():
            cp.wait_send()

        @pl.when(plan.c != layer)
        def _():
            for cp in plan.to_sibling():
                cp.wait_send()

        @pl.when(plan.c == layer)
        def _():
            for cp in plan.arrivals():
                cp.wait_recv()

    res = pl.pallas_call(
        body, name=name, in_specs=[IN_HBM] * (2 * n_w) + [IN_SEM, IN_SEM, pl.BlockSpec(memory_space=pl.ANY)],
        out_specs=[IN_HBM] * (2 * n_w), out_shape=[pltpu.HBM(a.shape, a.dtype) for a in (*pairs, *slabs)],
        input_output_aliases={i: i for i in range(2 * n_w)},
        compiler_params=pltpu.CompilerParams(has_side_effects=DATAFLOW),
    )(*pairs, *slabs, send_sems, recv_sems, after)
    return res[:n_w], res[n_w:]


def _sum_slabs(slabs, pair, kind, size, layer, into, *, name):
    n_s, k, n = slabs.shape
    tr = _div_tile(k, 512, BF16_ROWS)
    tc = n if kind in ("col", "row") else LANES
    x, y, _ = _my_place()
    mine = 2 * x + y
    shard = size + HEAD_DIM
    row0 = mine * (k // tr) if kind == "row" else 0
    col0 = {"col": mine, "row": 0, "win_main": (mine * shard + HEAD_DIM * (mine % 2)) // LANES,
            "win_strad": (size + 2 * shard * (mine // 2)) // LANES}[kind]
    on = _on_core(layer)[0]
    scalars = jnp.stack([2 * mine + layer, row0 * on, col0 * on, on]).astype(jnp.int32)

    def body(s_ref, slab_ref, own_ref, *rest):
        o_ref = rest[-1]
        me = s_ref[0]

        @pl.when(s_ref[3] == 1)
        def _():
            acc = jnp.zeros(o_ref.shape, F32)
            for i in range(n_s):
                acc = acc + jnp.where(me == i, own_ref[...], slab_ref[i]).astype(F32)
            o_ref[...] = acc

    operands = [scalars, slabs, pair] + ([] if into is None else [into])
    return pl.pallas_call(
        body, name=name,
        grid_spec=pltpu.PrefetchScalarGridSpec(
            num_scalar_prefetch=1, grid=(k // tr, n // tc),
            in_specs=[pl.BlockSpec((n_s, tr, tc), lambda i, j, s: (0, i * s[3], j * s[3])),
                      pl.BlockSpec((tr, tc), lambda i, j, s: (s[1] + i * s[3], s[2] + j * s[3]))]
            + ([] if into is None else [HBM]),
            out_specs=pl.BlockSpec((None, tr, tc), lambda i, j, s: (layer, i * s[3], j * s[3]))),
        out_shape=jax.ShapeDtypeStruct((2, k, n), F32),
        input_output_aliases={} if into is None else {3: 0},
        compiler_params=_params(dimension_semantics=("arbitrary", "arbitrary")),
    )(*operands)


def _exchange_layers(bufs, *, name):
    n_w = len(bufs)

    def body(*refs):
        dst = refs[n_w:2 * n_w]
        send_sems, recv_sems = refs[2 * n_w:]
        x, y, c = _my_place()

        def copy(i, layer):
            return pltpu.make_async_remote_copy(src_ref=dst[i].at[layer], dst_ref=dst[i].at[layer], send_sem=send_sems.at[i],
                                                recv_sem=recv_sems.at[i], device_id=(x, y, 1 - c), device_id_type=MESH)

        sends = [copy(i, c) for i in range(n_w)]
        for cp in sends:
            cp.start()
        for i in range(n_w):
            copy(i, 1 - c).wait_recv()
        for cp in sends:
            cp.wait_send()

    return pl.pallas_call(
        body, name=name, in_specs=[HBM] * n_w, out_specs=[HBM] * n_w,
        out_shape=[jax.ShapeDtypeStruct(b.shape, b.dtype) for b in bufs],
        input_output_aliases={i: i for i in range(n_w)},
        scratch_shapes=[pltpu.SemaphoreType.DMA((n_w,)), pltpu.SemaphoreType.DMA((n_w,))],
    )(*bufs)


def _all_sum_small(v, *, name):
    r = v.shape[0]
    relations = [(dx, dy, dc) for dx in (0, 1) for dy in (0, 1) for dc in (0, 1)][1:]

    def body(v_ref, o_ref, buf, send_sems, recv_sems):
        x, y, c = _my_place()
        me = 4 * x + 2 * y + c
        buf[me] = v_ref[...]
        peers = [(x + dx - 2 * x * dx, y + dy - 2 * y * dy, c + dc - 2 * c * dc) for dx, dy, dc in relations]

        def copy(k, slot):
            return pltpu.make_async_remote_copy(src_ref=v_ref, dst_ref=buf.at[slot], send_sem=send_sems.at[k],
                                                recv_sem=recv_sems.at[k], device_id=peers[k], device_id_type=MESH)

        sends = [copy(k, me) for k in range(len(relations))]
        for cp in sends:
            cp.start()
        for k, (px, py, pc) in enumerate(peers):
            copy(k, 4 * px + 2 * py + pc).wait_recv()
        for cp in sends:
            cp.wait_send()
        acc = buf[0]
        for i in range(1, 8):
            acc = acc + buf[i]
        o_ref[...] = acc

    vm = pl.BlockSpec(memory_space=pltpu.VMEM)
    return pl.pallas_call(
        body, name=name, in_specs=[vm], out_specs=vm, out_shape=jax.ShapeDtypeStruct((r, LANES), F32),
        scratch_shapes=[pltpu.VMEM((8, r, LANES), F32), pltpu.SemaphoreType.DMA((7,)), pltpu.SemaphoreType.DMA((7,))],
    )(v)


SHARDED = (("ffn1_w_up", "col"), ("ffn1_w_down", "row"), ("w_in", "win"), ("w_branch_a", "col"),
           ("w_branch_b", "col"), ("w_out", "row"), ("ffn2_w_up", "col"), ("ffn2_w_down", "row"))
REPLICATED = ("ffn1_norm", "mix_norm", "na_rel_bias", "ffn2_norm", "final_norm")


def _weight_pieces(w):
    even = lax.axis_index("y") == 0
    shards, kinds, names = [], [], []
    for name, kind in SHARDED:
        wb = w[name].astype(BF16)
        if kind == "win":
            main = wb.shape[-1] - HEAD_DIM
            assert main % LANES == 0
            zeros = jnp.zeros(wb.shape[:-1] + (HEAD_DIM,), BF16)
            shards += [jnp.where(even, wb[..., :main], wb[..., HEAD_DIM:]),
                       jnp.where(even, jnp.concatenate([wb[..., main:], zeros], -1),
                                 jnp.concatenate([zeros, wb[..., :HEAD_DIM]], -1))]
            kinds += ["win_main", "slot"]
            names += [name, name + "_strad"]
        else:
            shards.append(wb)
            kinds.append(kind)
            names.append(name)
    return names, kinds, shards


def _finish_w_in(full):
    full = dict(full)
    strad = full.pop("w_in_strad")
    main = full["w_in"].shape[1] // N_CHIPS - HEAD_DIM
    for i in range(N_CHIPS // 2):
        lo = main + 2 * (main + HEAD_DIM) * i
        full["w_in"] = full["w_in"].at[:, lo:lo + LANES].set(strad[2 * i] + strad[2 * i + 1])
    return full


def _scatter_pieces(shards):
    names, kinds, sizes, srcs = [], [], [], []
    for name, kind in SHARDED:
        shp = shards[name].shape
        if kind == "win":
            names += [name, name + "_strad"]
            kinds += ["win_main", "win_strad"]
            sizes += [shp[2] - HEAD_DIM] * 2
            srcs += [name, name]
        else:
            names.append(name)
            kinds.append(kind)
            sizes.append(shp[1] if kind == "row" else shp[2])
            srcs.append(name)
    return names, kinds, sizes, srcs


def _finish_weight_grads(reduced, names, tag):
    out = dict(zip(names, _exchange_layers(reduced, name=f"{tag}_layers")))
    if "w_in_strad" in out:
        strad = out.pop("w_in_strad")
        even = lax.axis_index("y") == 0
        out["w_in"] = jnp.where(even, jnp.concatenate([out["w_in"], strad[..., :HEAD_DIM]], -1),
                                jnp.concatenate([strad[..., HEAD_DIM:], out["w_in"]], -1))
    return out


class _Grads:
    def __init__(self):
        self.arrays = {}

    def put(self, weight, layer, a, b, *, cols=None, col_off=0, **kw):
        self.arrays[weight, layer] = _mm(a, b, mode="tn", out_dtype=BF16, out_cols=cols, out_col_off=col_off,
                                         out_into=self.arrays.get((weight, layer)), **kw)


def _ffn_fwd(x, h, w_up, w_down, tag):
    t, d = x.shape
    f = w_down.shape[0]
    a, gate, up = _mm_swiglu_fwd(h, w_up, tm=_div_tile(t, ROWS_NARROW, 8), tn=MXU_N, name=f"{tag}_up")
    x_out = _mm(a, w_down, mode="nn", out_dtype=F32, tm=_div_tile(t, ROWS_WIDE, 8), tn=d, tk=f, alpha=0.5, res=x, name=f"{tag}_down")
    return x_out, (x, h, a, gate, up)


def _ffn_bwd(dx, dxb, saved, norm_g, w_up, w_down, layer, grads, wname, tag, scatter):
    x, h, a, gate, up = saved
    t, d = x.shape
    f = w_down.shape[0]
    tn = _div_tile(f, 1408)
    grads.put(f"{wname}_w_down", layer, a, dxb, tm=tn, tn=d, tk=ROWS_CONTRACTED, alpha=0.5, name=f"{tag}_dwd")
    d_gate, d_up = _mm_swiglu_bwd(dxb, w_down, gate, up, alpha=0.5, tm=_div_tile(t, ROWS_NARROW, 8), tn=MXU_N, name=f"{tag}_da")
    grads.put(f"{wname}_w_up", layer, h, d_gate, cols=2 * f, tm=d, tn=tn, tk=ROWS_CONTRACTED, name=f"{tag}_dwg")
    grads.put(f"{wname}_w_up", layer, h, d_up, cols=2 * f, col_off=f // tn, tm=d, tn=tn, tk=ROWS_CONTRACTED, name=f"{tag}_dwu")
    started = scatter(layer, [f"{wname}_w_up", f"{wname}_w_down"])
    dh = _mm(d_gate, w_up, mode="nt", out_dtype=F32, tm=_div_tile(t, ROWS_WIDE, 8), tn=d, tk=f, name=f"{tag}_dh1")
    dh = _mm(d_up, w_up, mode="nt", out_dtype=F32, tm=_div_tile(t, ROWS_WIDE, 8), tn=d, tk=f, b_k_off=1, res=dh, name=f"{tag}_dh2")
    return _rms_bwd(dh, x, norm_g + started, dx, tt=512, name=f"{tag}_dnorm")


def _to_heads(y, b, n_heads):
    t, w = y.shape
    return y.reshape(b, t // b, n_heads, HEAD_DIM).transpose(0, 2, 1, 3)


def _from_heads(y):
    b, n, s, hd = y.shape
    return y.transpose(0, 2, 1, 3).reshape(b * s, n * hd)


N_QKV = 3 * (DIL_HEADS + NA_HEADS) * HEAD_DIM


def _mixer_fwd(x, b, norm_g, full, bias, tabs, tag):
    t, d = x.shape
    s = t // b
    n_in = full["w_in"].shape[1]
    h = _rms_fwd(x, norm_g, tt=512, name=f"{tag}_norm")
    proj = _mm(h, full["w_in"], mode="nn", out_dtype=F32, tm=_div_tile(t, ROWS_NARROW, 8), tn=MXU_N, tk=d, name=f"{tag}_in")
    heads = _split_heads(proj.reshape(b, s, -1), *tabs, n_pairs=N_QKV // LANES, rot_pairs=DIL_HEADS,
                         scale_ranges=((0, DIL_HEADS // 2), (3 * DIL_HEADS // 2, (3 * DIL_HEADS + NA_HEADS) // 2)),
                         name=f"{tag}_heads")
    ya, lse_a = _dil_attn_fwd(heads, name=f"{tag}_dil")
    yb, lse_b = _na_attn_fwd(heads, bias, first=3 * DIL_HEADS, name=f"{tag}_na")
    ya2, yb2 = _from_heads(ya), _from_heads(yb)
    z = _mm(ya2, full["w_branch_a"], mode="nn", out_dtype=F32, tm=_div_tile(t, ROWS_NARROW, 8), tn=MXU_N, tk=ya2.shape[1],
            out_slab=(0, 2), name=f"{tag}_za")
    z = _mm(yb2, full["w_branch_b"], mode="nn", out_dtype=F32, tm=_div_tile(t, ROWS_NARROW, 8), tn=MXU_N, tk=yb2.shape[1],
            out_slab=(1, 2), out_into=z, name=f"{tag}_zb")
    merged = _gate_fwd(proj, z, gate_col=N_QKV, tt=1024, name=f"{tag}_gate")
    x_out = _mm(merged, full["w_out"], mode="nn", out_dtype=F32, tm=_div_tile(t, ROWS_NARROW, 8), tn=MXU_N, tk=d, res=x, name=f"{tag}_out")
    return x_out, (x, h, proj, heads, ya, lse_a, yb, lse_b, ya2, yb2, z, merged)


def _mixer_bwd(dx, dob, b, saved, norm_g, full, layer, bias, tabs, grads, tag, scatter):
    x, h, proj, heads, ya, lse_a, yb, lse_b, ya2, yb2, z, merged = saved
    t, d = x.shape
    s = t // b
    n_in = full["w_in"].shape[1]
    grads.put("w_out", layer, merged, dob, tm=d, tn=d, tk=ROWS_CONTRACTED, name=f"{tag}_dwo")
    dm = _mm(dob, full["w_out"], mode="nt", out_dtype=F32, tm=_div_tile(t, ROWS_NARROW, 8), tn=MXU_N, tk=d, name=f"{tag}_dm")
    dz, dproj = _gate_bwd(dm, proj, z, gate_col=N_QKV, tt=1024, name=f"{tag}_dgate")
    grads.put("w_branch_a", layer, ya2, dz, b_sel=0, tm=ya2.shape[1], tn=d, tk=ROWS_CONTRACTED, name=f"{tag}_dwa")
    grads.put("w_branch_b", layer, yb2, dz, b_sel=1, tm=yb2.shape[1], tn=d, tk=ROWS_CONTRACTED, name=f"{tag}_dwb")
    started = scatter(layer, ["w_out", "w_branch_a", "w_branch_b"])
    dya = _mm(dz, full["w_branch_a"], mode="nt", out_dtype=F32, tm=_div_tile(t, ROWS_NARROW, 8), tn=MXU_N, tk=d, a_sel=0, name=f"{tag}_dya")
    dyb = _mm(dz, full["w_branch_b"], mode="nt", out_dtype=F32, tm=_div_tile(t, ROWS_NARROW, 8), tn=MXU_N, tk=d, a_sel=1, name=f"{tag}_dyb")
    d_dil = _dil_attn_bwd(heads, ya, lse_a, _to_heads(dya, b, DIL_GROUP_HEADS), name=f"{tag}_ddil")
    d_na, d_bias = _na_attn_bwd(heads, bias, yb, lse_b, _to_heads(dyb, b, NA_HEADS), first=3 * DIL_HEADS, name=f"{tag}_dna")
    dproj = _merge_heads(d_dil, *tabs, heads_per_row=DIL_GROUP_HEADS, rot_pairs=DIL_HEADS, scale_pairs=DIL_HEADS // 2,
                         dilated=True, out_cols=n_in, tile_off=0, into=dproj.reshape(b, s, n_in), name=f"{tag}_dheads_a")
    dproj = _merge_heads(d_na, *tabs, heads_per_row=NA_HEADS, rot_pairs=0, scale_pairs=NA_HEADS // 2, dilated=False,
                         out_cols=n_in, tile_off=3 * DIL_HEADS // 2, into=dproj, name=f"{tag}_dheads_b").reshape(t, n_in)
    grads.put("w_in", layer, h, dproj, tm=_div_tile(d, 512), tn=_div_tile(n_in, 2944), tk=ROWS_CONTRACTED, name=f"{tag}_dwin")
    started = started + scatter(layer, ["w_in"])
    dh = _mm(dproj, full["w_in"], mode="nt", out_dtype=F32, tm=_div_tile(t, ROWS_WIDE, 8), tn=d, tk=_div_tile(n_in, 2944), name=f"{tag}_dh")
    dx_in, dxb_in, d_norm = _rms_bwd(dh, x, norm_g + started, dx, tt=512, name=f"{tag}_dnorm")
    d_rb = _na_collapse_bias(d_bias, name=f"{tag}_dbias")
    return dx_in, dxb_in, d_norm, d_rb


def kernel(x, ffn1_norm, ffn1_w_up, ffn1_w_down, mix_norm, w_in, na_rel_bias, w_branch_a, w_branch_b, w_out, ffn2_norm, ffn2_w_up, ffn2_w_down, final_norm, loss_target, m_ffn1_norm, m_ffn1_w_up, m_ffn1_w_down, m_mix_norm, m_w_in, m_na_rel_bias, m_w_branch_a, m_w_branch_b, m_w_out, m_ffn2_norm, m_ffn2_w_up, m_ffn2_w_down, m_final_norm, v_ffn1_norm, v_ffn1_w_up, v_ffn1_w_down, v_mix_norm, v_w_in, v_na_rel_bias, v_w_branch_a, v_w_branch_b, v_w_out, v_ffn2_norm, v_ffn2_w_up, v_ffn2_w_down, v_final_norm):
    w = dict(ffn1_norm=ffn1_norm, ffn1_w_up=ffn1_w_up, ffn1_w_down=ffn1_w_down, mix_norm=mix_norm, w_in=w_in,
             na_rel_bias=na_rel_bias, w_branch_a=w_branch_a, w_branch_b=w_branch_b, w_out=w_out, ffn2_norm=ffn2_norm,
             ffn2_w_up=ffn2_w_up, ffn2_w_down=ffn2_w_down, final_norm=final_norm)
    mom = dict(ffn1_norm=m_ffn1_norm, ffn1_w_up=m_ffn1_w_up, ffn1_w_down=m_ffn1_w_down, mix_norm=m_mix_norm, w_in=m_w_in,
               na_rel_bias=m_na_rel_bias, w_branch_a=m_w_branch_a, w_branch_b=m_w_branch_b, w_out=m_w_out,
               ffn2_norm=m_ffn2_norm, ffn2_w_up=m_ffn2_w_up, ffn2_w_down=m_ffn2_w_down, final_norm=m_final_norm)
    var = dict(ffn1_norm=v_ffn1_norm, ffn1_w_up=v_ffn1_w_up, ffn1_w_down=v_ffn1_w_down, mix_norm=v_mix_norm, w_in=v_w_in,
               na_rel_bias=v_na_rel_bias, w_branch_a=v_w_branch_a, w_branch_b=v_w_branch_b, w_out=v_w_out,
               ffn2_norm=v_ffn2_norm, ffn2_w_up=v_ffn2_w_up, ffn2_w_down=v_ffn2_w_down, final_norm=v_final_norm)
    b, s, d = x.shape
    t = b * s
    depth = ffn1_norm.shape[0]
    assert depth == 2, "core c of a chip sends / reduces layer c"
    shards = {name: w[name] for name, _ in SHARDED}

    names, kinds, pieces = _weight_pieces(w)
    by_layer = [[p[l:l + 1] for p in pieces] for l in range(depth)]
    own = [[_place_own(p, kind, 0, name=f"own{l}_{nm}") for nm, kind, p in zip(names, kinds, by_layer[l])] for l in range(depth)]
    full = [{}, {}]

    def gather_start(layer, group, after, tag):
        idx = [i for i, nm in enumerate(names) if nm in group]
        pick = lambda seq: [seq[i] for i in idx]
        *state, token = _gather_layer_start(pick(by_layer[layer]), pick(kinds), pick(own[layer]), 0, after, name=f"{tag}_start")
        return (layer, idx, tag, state), token[:1, :1]

    def gather_finish(started, after):
        layer, idx, tag, state = started
        pick = lambda seq: [seq[i] for i in idx]
        landed = _gather_layer_wait(*state, pick(kinds), 0, after, name=f"{tag}_wait")
        done = _gather_layer_forward([by_layer[layer][i].shape for i in idx], pick(kinds), landed, name=f"{tag}_forward")
        full[layer].update(zip(pick(names), done))
        return done[0]

    ffn1, mixer, ffn2 = names[:2], names[2:7], names[7:]
    assert mixer[0] == "w_in" and ffn2[0] == "ffn2_w_up", names
    xc = x.reshape(t, d)
    l0_ffn1, token_ffn1 = gather_start(0, ffn1, xc, "gather_l0_ffn1")
    tabs = _rope_tables(s)
    bias = _na_expand_bias(na_rel_bias, name="na_bias")

    saved = []
    h = _rms_fwd(xc, ffn1_norm[:1] + token_ffn1, tt=512, name="l0_ffn1_norm")
    landed = gather_finish(l0_ffn1, h)
    l0_mixer, token_mixer = gather_start(0, mixer, landed, "gather_l0_mixer")
    xc, s1 = _ffn_fwd(xc, h + token_mixer.astype(BF16), full[0]["ffn1_w_up"], full[0]["ffn1_w_down"], "l0_ffn1")
    landed = gather_finish(l0_mixer, xc)
    full[0] = _finish_w_in(full[0])
    l0_ffn2, token_ffn2 = gather_start(0, ffn2, landed, "gather_l0_ffn2")
    layer1, token_layer1 = gather_start(1, names, landed, "gather_l1")
    xc, s2 = _mixer_fwd(xc, b, mix_norm[:1] + token_ffn2 + token_layer1, full[0], bias[0], tabs, "l0_mix")
    gather_finish(l0_ffn2, xc)
    xc, s3 = _ffn_fwd(xc, _rms_fwd(xc, ffn2_norm[:1], tt=512, name="l0_ffn2_norm"), full[0]["ffn2_w_up"], full[0]["ffn2_w_down"],
                      "l0_ffn2")
    saved.append((s1, s2, s3))
    gather_finish(layer1, xc)
    full[1] = _finish_w_in(full[1])
    for l in range(1, depth):
        xc, s1 = _ffn_fwd(xc, _rms_fwd(xc, ffn1_norm[l:l + 1], tt=512, name=f"l{l}_ffn1_norm"), full[l]["ffn1_w_up"],
                          full[l]["ffn1_w_down"], f"l{l}_ffn1")
        xc, s2 = _mixer_fwd(xc, b, mix_norm[l:l + 1], full[l], bias[l], tabs, f"l{l}_mix")
        xc, s3 = _ffn_fwd(xc, _rms_fwd(xc, ffn2_norm[l:l + 1], tt=512, name=f"l{l}_ffn2_norm"), full[l]["ffn2_w_up"],
                          full[l]["ffn2_w_down"], f"l{l}_ffn2")
        saved.append((s1, s2, s3))

    dx, dxb, d_final, loss_part = _final_loss(xc, final_norm.reshape(1, d), loss_target.reshape(t, d), tt=512, name="final_loss")
    grads = _Grads()
    piece_names, piece_kinds, piece_sizes, piece_srcs = _scatter_pieces(shards)
    scattered = []

    def scatter(layer, weights):
        tag = f"grads{layer}_{weights[0]}"
        idx = [i for i, src in enumerate(piece_srcs) if src in weights]
        pick = lambda seq: [seq[i] for i in idx]
        *state, token = _grads_to_chips_start([grads.arrays[src, layer] for src in pick(piece_srcs)], pick(piece_kinds),
                                              pick(piece_sizes), layer, name=f"{tag}_to_chips_start")
        scattered.append((layer, idx, state))
        return token[:1, :1]
    small = {name: [None] * depth for name in REPLICATED[:-1]}
    for l in reversed(range(depth)):
        s1, s2, s3 = saved[l]
        dx, dxb, small["ffn2_norm"][l] = _ffn_bwd(dx, dxb, s3, ffn2_norm[l:l + 1], full[l]["ffn2_w_up"], full[l]["ffn2_w_down"],
                                                  l, grads, "ffn2", f"l{l}_ffn2", scatter)
        dx, dxb, small["mix_norm"][l], small["na_rel_bias"][l] = _mixer_bwd(
            dx, dxb, b, s2, mix_norm[l:l + 1], full[l], l, bias[l], tabs, grads, f"l{l}_mix", scatter)
        dx, dxb, small["ffn1_norm"][l] = _ffn_bwd(dx, dxb, s1, ffn1_norm[l:l + 1], full[l]["ffn1_w_up"], full[l]["ffn1_w_down"],
                                                  l, grads, "ffn1", f"l{l}_ffn1", scatter)
    grad_x = dx.reshape(b, s, d)
    reduced = [None] * len(piece_names)

    def arrive(group, after):
        layer, idx, state = group
        state = _grads_to_chips_wait(*state, [piece_kinds[i] for i in idx], [piece_sizes[i] for i in idx], layer, after,
                                     name=f"grads{layer}_{piece_names[idx[0]]}_to_chips_wait")
        for i, p, sl in zip(idx, *state):
            reduced[i] = _sum_slabs(sl, p, piece_kinds[i], piece_sizes[i], layer, reduced[i],
                                    name=f"grads{layer}_sum_{piece_names[i]}")
        return idx

    for group in scattered[:-1]:
        arrive(group, dx)
    late = scattered[-1][1]
    early = [i for i in range(len(piece_names)) if i not in late]
    g_out = _finish_weight_grads([reduced[i] for i in early], [piece_names[i] for i in early], "grads_early")

    parts = [jnp.stack(small[name]).reshape(-1) for name in REPLICATED[:-1]] + [d_final.reshape(-1), loss_part[0, :1]]
    sizes = [v.shape[0] for v in parts]
    flat = jnp.concatenate(parts)
    flat = jnp.pad(flat, (0, -flat.shape[0] % (8 * LANES)))
    small_sum = _all_sum_small(flat.reshape(-1, LANES), name="small_all_sum").reshape(-1)
    off = 0
    for name, n in zip(REPLICATED, sizes[:-1]):
        g_out[name] = small_sum[off:off + n].reshape(w[name].shape)
        off += n
    loss = small_sum[off]

    names = list(w)
    delta, new_m, new_v = {}, {}, {}
    for name in [n for n in names if n in g_out]:
        delta[name], new_m[name], new_v[name] = _adamw(w[name], g_out[name], mom[name], var[name], name=f"adamw_{name}")
    arrive(scattered[-1], delta["w_in"])
    g_out.update(_finish_weight_grads([reduced[i] for i in late], [piece_names[i] for i in late], "grads_late"))
    for name in [n for n in names if n not in delta]:
        delta[name], new_m[name], new_v[name] = _adamw(w[name], g_out[name], mom[name], var[name], name=f"adamw_{name}")
    return (loss, grad_x, *[g_out[n] for n in names], *[delta[n] for n in names], *[new_m[n] for n in names],
            *[new_v[n] for n in names])
```

```python
import functools

import numpy as np
import jax
import jax.numpy as jnp
from jax import lax
from jax.experimental import pallas as pl
from jax.experimental.pallas import tpu as pltpu

F32, BF16 = jnp.float32, jnp.bfloat16
MESH = pl.DeviceIdType.MESH

HEAD_DIM = 64
DILATIONS = (1, 4, 16)
DIL_HALF = 64
DIL_GROUP_HEADS = 4
DIL_HEADS = 12
NA_HEADS = 8
GRID_W = 64
NA_ROWS = 8
NA_COLS = 16
ROPE_THETA = 10000.0
RMS_EPS = 1e-6
NEG_INF = -1e30
ADAM_LR, ADAM_B1, ADAM_B2, ADAM_EPS, ADAM_WD, ADAM_STEP = 0.001, 0.9, 0.999, 1e-08, 0.01, 10
QK_SCALE = HEAD_DIM ** -0.5

N_CHIPS = 4
LANES = 128
BF16_ROWS = 16
VMEM_LIMIT = 56 * 1024 * 1024
MXU_N = 256
ROWS_NARROW = 2048
ROWS_WIDE = 512
ROWS_CONTRACTED = 4096
BLOCKS_IN_FLIGHT = 8

_NN = (((1,), (0,)), ((), ()))
_NT = (((1,), (1,)), ((), ()))
_TN = (((0,), (0,)), ((), ()))

HBM = pl.BlockSpec(memory_space=pl.ANY)


def _params(**kw):
    return pltpu.CompilerParams(vmem_limit_bytes=VMEM_LIMIT, **kw)


def _dot(a, b, dims):
    return lax.dot_general(a, b, dims, preferred_element_type=F32)


def _div_tile(n, cap, mult=LANES):
    best = None
    for t in range(mult, min(n, cap) + 1, mult):
        if n % t == 0:
            best = t
    return n if best is None else best


def _stacked(block, index, sel):
    if sel is None:
        return pl.BlockSpec(block, index)
    return pl.BlockSpec((None,) + block, lambda *g: (sel,) + index(*g))


def _mm(a, b, *, mode, out_dtype, tm, tn, tk, name, alpha=1.0, res=None, a_sel=None, b_sel=None, b_k_off=0,
        out_slab=None, out_cols=None, out_col_off=0, out_into=None):
    a2, b2 = a.shape[-2:], b.shape[-2:]
    if mode == "nn":
        (m, k), n = a2, b2[1]
        a_spec = _stacked((tm, tk), lambda i, j, kk: (i, kk), a_sel)
        b_spec = _stacked((tk, tn), lambda i, j, kk: (kk + b_k_off, j), b_sel)
        dims = _NN
    elif mode == "nt":
        (m, k), n = a2, b2[0]
        a_spec = _stacked((tm, tk), lambda i, j, kk: (i, kk), a_sel)
        b_spec = _stacked((tn, tk), lambda i, j, kk: (j, kk + b_k_off), b_sel)
        dims = _NT
    else:
        (k, m), n = a2, b2[1]
        a_spec = _stacked((tk, tm), lambda i, j, kk: (kk, i), a_sel)
        b_spec = _stacked((tk, tn), lambda i, j, kk: (kk + b_k_off, j), b_sel)
        dims = _TN
    assert m % tm == 0 and n % tn == 0 and k % tk == 0, (name, a.shape, b.shape)
    nk = k // tk
    has_res = res is not None
    if out_slab is None:
        o_spec = pl.BlockSpec((tm, tn), lambda i, j, kk: (i, j + out_col_off))
        out_shape = jax.ShapeDtypeStruct((m, n if out_cols is None else out_cols), out_dtype)
    else:
        o_spec = _stacked((tm, tn), lambda i, j, kk: (i, j + out_col_off), out_slab[0])
        out_shape = jax.ShapeDtypeStruct((out_slab[1], m, n if out_cols is None else out_cols), out_dtype)
    r_spec = pl.BlockSpec((tm, tn), lambda i, j, kk: (i, j))
    n_in = 2 + has_res + (out_into is not None)

    def body(*refs):
        a_ref, b_ref = refs[0], refs[1]
        r_ref = refs[2] if has_res else None
        o_ref = refs[n_in]
        p = _dot(a_ref[...], b_ref[...], dims)

        def finish(acc):
            y = acc * alpha if alpha != 1.0 else acc
            if has_res:
                y = y + r_ref[...].astype(F32)
            o_ref[...] = y.astype(o_ref.dtype)

        if nk == 1:
            finish(p)
        else:
            acc_ref = refs[n_in + 1]
            kk = pl.program_id(2)

            @pl.when(kk == 0)
            def _():
                acc_ref[...] = p

            @pl.when(kk > 0)
            def _():
                acc_ref[...] += p

            @pl.when(kk == nk - 1)
            def _():
                finish(acc_ref[...])

    operands = [a, b] + ([res] if has_res else [])
    in_specs = [a_spec, b_spec] + ([r_spec] if has_res else [])
    aliases = {}
    if out_into is not None:
        aliases = {len(operands): 0}
        operands.append(out_into)
        in_specs.append(HBM)
    return pl.pallas_call(
        body, name=name, grid=(m // tm, n // tn, nk), in_specs=in_specs, out_specs=o_spec, out_shape=out_shape,
        scratch_shapes=[pltpu.VMEM((tm, tn), F32)] if nk > 1 else [], input_output_aliases=aliases,
        compiler_params=_params(dimension_semantics=("parallel", "parallel", "arbitrary")),
    )(*operands)


def _mm_swiglu_fwd(h, w_up, *, tm, tn, name):
    m, k = h.shape
    n = w_up.shape[1] // 2
    h_spec = pl.BlockSpec((tm, k), lambda i, j: (i, 0))
    wg_spec = pl.BlockSpec((k, tn), lambda i, j: (0, j))
    wu_spec = pl.BlockSpec((k, tn), lambda i, j: (0, j + n // tn))
    o_spec = pl.BlockSpec((tm, tn), lambda i, j: (i, j))

    def body(h_ref, wg_ref, wu_ref, a_ref, g_ref, u_ref):
        hb = h_ref[...]
        g = _dot(hb, wg_ref[...], _NN)
        u = _dot(hb, wu_ref[...], _NN)
        a_ref[...] = (g * jax.nn.sigmoid(g) * u).astype(BF16)
        g_ref[...] = g.astype(BF16)
        u_ref[...] = u.astype(BF16)

    out = jax.ShapeDtypeStruct((m, n), BF16)
    return pl.pallas_call(
        body, name=name, grid=(m // tm, n // tn), in_specs=[h_spec, wg_spec, wu_spec],
        out_specs=[o_spec] * 3, out_shape=[out] * 3,
        compiler_params=_params(dimension_semantics=("parallel", "parallel")),
    )(h, w_up, w_up)


def _mm_swiglu_bwd(dy, w_down, gate, up, *, alpha, tm, tn, name):
    m, k = dy.shape
    n = w_down.shape[0]
    dy_spec = pl.BlockSpec((tm, k), lambda i, j: (i, 0))
    w_spec = pl.BlockSpec((tn, k), lambda i, j: (j, 0))
    o_spec = pl.BlockSpec((tm, tn), lambda i, j: (i, j))

    def body(dy_ref, w_ref, g_ref, u_ref, dg_ref, du_ref):
        da = _dot(dy_ref[...], w_ref[...], _NT) * alpha
        g = g_ref[...].astype(F32)
        u = u_ref[...].astype(F32)
        sg = jax.nn.sigmoid(g)
        dg_ref[...] = (da * u * (sg * (1.0 + g * (1.0 - sg)))).astype(BF16)
        du_ref[...] = (da * (g * sg)).astype(BF16)

    out = jax.ShapeDtypeStruct((m, n), BF16)
    return pl.pallas_call(
        body, name=name, grid=(m // tm, n // tn), in_specs=[dy_spec, w_spec, o_spec, o_spec],
        out_specs=[o_spec] * 2, out_shape=[out] * 2,
        compiler_params=_params(dimension_semantics=("parallel", "parallel")),
    )(dy, w_down, gate, up)


def _rms_fwd(x, g, *, tt, name):
    t, d = x.shape

    def body(x_ref, g_ref, h_ref):
        xv = x_ref[...]
        rstd = lax.rsqrt(jnp.mean(xv * xv, axis=1, keepdims=True) + RMS_EPS)
        h_ref[...] = (xv * rstd * g_ref[...]).astype(BF16)

    return pl.pallas_call(
        body, name=name, grid=(t // tt,),
        in_specs=[pl.BlockSpec((tt, d), lambda i: (i, 0)), pl.BlockSpec((1, d), lambda i: (0, 0))],
        out_specs=pl.BlockSpec((tt, d), lambda i: (i, 0)), out_shape=jax.ShapeDtypeStruct((t, d), BF16),
        compiler_params=_params(dimension_semantics=("parallel",)),
    )(x, g)


def _rms_bwd(dh, x, g, dres, *, tt, name):
    t, d = x.shape

    def body(dh_ref, x_ref, g_ref, r_ref, dx_ref, dxb_ref, dg_ref):
        xv = x_ref[...]
        rstd = lax.rsqrt(jnp.mean(xv * xv, axis=1, keepdims=True) + RMS_EPS)
        xhat = xv * rstd
        dhv = dh_ref[...]
        dxhat = dhv * g_ref[...]
        dx = r_ref[...] + rstd * (dxhat - xhat * jnp.mean(dxhat * xhat, axis=1, keepdims=True))
        dx_ref[...] = dx
        dxb_ref[...] = dx.astype(BF16)

        @pl.when(pl.program_id(0) == 0)
        def _():
            dg_ref[...] = jnp.zeros_like(dg_ref)

        dg_ref[...] += jnp.sum(dhv * xhat, axis=0, keepdims=True)

    row = pl.BlockSpec((tt, d), lambda i: (i, 0))
    vec = pl.BlockSpec((1, d), lambda i: (0, 0))
    return pl.pallas_call(
        body, name=name, grid=(t // tt,), in_specs=[row, row, vec, row], out_specs=[row, row, vec],
        out_shape=[jax.ShapeDtypeStruct((t, d), F32), jax.ShapeDtypeStruct((t, d), BF16), jax.ShapeDtypeStruct((1, d), F32)],
        compiler_params=_params(dimension_semantics=("arbitrary",)),
    )(dh, x, g, dres)


def _final_loss(x, g, target, *, tt, name):
    t, d = x.shape

    def body(x_ref, g_ref, t_ref, dx_ref, dxb_ref, dg_ref, loss_ref):
        xv = x_ref[...]
        gv = g_ref[...]
        rstd = lax.rsqrt(jnp.mean(xv * xv, axis=1, keepdims=True) + RMS_EPS)
        xhat = xv * rstd
        err = xhat * gv - t_ref[...]
        dy = err * (1.0 / d)
        dxhat = dy * gv
        dx = rstd * (dxhat - xhat * jnp.mean(dxhat * xhat, axis=1, keepdims=True))
        dx_ref[...] = dx
        dxb_ref[...] = dx.astype(BF16)

        @pl.when(pl.program_id(0) == 0)
        def _():
            dg_ref[...] = jnp.zeros_like(dg_ref)
            loss_ref[...] = jnp.zeros_like(loss_ref)

        dg_ref[...] += jnp.sum(dy * xhat, axis=0, keepdims=True)
        part = 0.5 * jnp.sum(jnp.mean(err * err, axis=1, keepdims=True), axis=0, keepdims=True)
        loss_ref[...] += jnp.broadcast_to(part, loss_ref.shape)

    row = pl.BlockSpec((tt, d), lambda i: (i, 0))
    vec = pl.BlockSpec((1, d), lambda i: (0, 0))
    one = pl.BlockSpec((1, LANES), lambda i: (0, 0))
    return pl.pallas_call(
        body, name=name, grid=(t // tt,), in_specs=[row, vec, row], out_specs=[row, row, vec, one],
        out_shape=[jax.ShapeDtypeStruct((t, d), F32), jax.ShapeDtypeStruct((t, d), BF16), jax.ShapeDtypeStruct((1, d), F32),
                   jax.ShapeDtypeStruct((1, LANES), F32)],
        compiler_params=_params(dimension_semantics=("arbitrary",)),
    )(x, g, target)


def _swap_halves(x):
    lane = lax.broadcasted_iota(jnp.int32, x.shape, 1)
    return jnp.where((lane // 32) % 2 == 0, pltpu.roll(x, 96, 1), pltpu.roll(x, 32, 1))


def _rope_tables(s):
    half = HEAD_DIM // 2
    inv_freq = ROPE_THETA ** (-jnp.arange(half, dtype=F32) / half)
    ang = jnp.arange(s).astype(F32)[:, None] * inv_freq[None, :]
    cos, sin = jnp.cos(ang), jnp.sin(ang)
    return jnp.tile(cos, (1, 4)), jnp.concatenate([-sin, sin, -sin, sin], axis=1)


def _dilation_of_tile(p):
    dilated = p < 3 * DIL_HEADS // 2
    g = (p % (DIL_HEADS // 2)) // (DIL_GROUP_HEADS // 2)
    return [(dilated & (g == gi)) | (jnp.logical_not(dilated) if gi == 0 else False) for gi in range(len(DILATIONS))]


def _residue_major(ref, d):
    s = ref.shape[0]
    if d == 1:
        return ref[...]
    return jnp.concatenate([ref[pl.ds(r, s // d, stride=d), :] for r in range(d)], axis=0)


def _split_heads(proj, cos4, sin4, *, n_pairs, rot_pairs, scale_ranges, name):
    b, s, _ = proj.shape

    def body(x_ref, c_ref, s_ref, o_ref):
        p = pl.program_id(1)
        is_q = functools.reduce(jnp.logical_or, [(p >= lo) & (p < hi) for lo, hi in scale_ranges])
        scale = jnp.where(is_q, QK_SCALE, 1.0)

        def put(y):
            o_ref[0] = y[:, :HEAD_DIM].astype(BF16)
            o_ref[1] = y[:, HEAD_DIM:].astype(BF16)

        for d, in_group in zip(DILATIONS, _dilation_of_tile(p)):
            @pl.when(in_group & (p < rot_pairs))
            def _(d=d):
                x = _residue_major(x_ref, d)
                put((x * _residue_major(c_ref, d) + _swap_halves(x) * _residue_major(s_ref, d)) * scale)

            @pl.when(in_group & (p >= rot_pairs))
            def _(d=d):
                put(_residue_major(x_ref, d) * scale)

    tab = pl.BlockSpec((s, LANES), lambda bi, p: (0, 0))
    return pl.pallas_call(
        body, name=name, grid=(b, n_pairs),
        in_specs=[pl.BlockSpec((None, s, LANES), lambda bi, p: (bi, 0, p)), tab, tab],
        out_specs=pl.BlockSpec((None, 2, s, HEAD_DIM), lambda bi, p: (bi, p, 0, 0)),
        out_shape=jax.ShapeDtypeStruct((b, 2 * n_pairs, s, HEAD_DIM), BF16),
        compiler_params=_params(dimension_semantics=("parallel", "parallel")),
    )(proj, cos4, sin4)


def _merge_heads(dheads, cos4, sin4, *, heads_per_row, rot_pairs, scale_pairs, dilated, out_cols, tile_off, into, name):
    b, hpr, r, s, _ = dheads.shape
    n_pairs = hpr * r // 2
    ppr = hpr // 2

    def body(d_ref, c_ref, s_ref, *rest):
        o_ref, t_ref = rest[-2:]
        p = pl.program_id(1)
        scale = jnp.where(p < scale_pairs, QK_SCALE, 1.0)

        def tokens(d):
            dy = jnp.concatenate([d_ref[0], d_ref[1]], axis=1)
            if d == 1:
                return dy
            for res in range(d):
                t_ref[pl.ds(res, s // d, stride=d), :] = dy[res * (s // d):(res + 1) * (s // d), :]
            return t_ref[...]

        groups = _dilation_of_tile(p) if dilated else [p >= 0]
        for d, in_group in zip(DILATIONS, groups):
            @pl.when(in_group & (p < rot_pairs))
            def _(d=d):
                dy = tokens(d)
                o_ref[...] = ((dy * c_ref[...] - _swap_halves(dy) * s_ref[...]) * scale).astype(BF16)

            @pl.when(in_group & (p >= rot_pairs))
            def _(d=d):
                o_ref[...] = (tokens(d) * scale).astype(BF16)

    tab = pl.BlockSpec((s, LANES), lambda bi, p: (0, 0))
    operands = [dheads, cos4, sin4] + ([] if into is None else [into])
    return pl.pallas_call(
        body, name=name, grid=(b, n_pairs),
        in_specs=[pl.BlockSpec((None, 2, None, s, HEAD_DIM), lambda bi, p: (bi, p % ppr, p // ppr, 0, 0)), tab, tab]
        + ([] if into is None else [HBM]),
        out_specs=pl.BlockSpec((None, s, LANES), lambda bi, p: (bi, 0, p + tile_off)),
        out_shape=jax.ShapeDtypeStruct((b, s, out_cols), BF16),
        input_output_aliases={} if into is None else {3: 0},
        scratch_shapes=[pltpu.VMEM((s, LANES), F32)],
        compiler_params=_params(dimension_semantics=("parallel", "parallel")),
    )(*operands)


DIL_TQ = 256


def _dil_block(g, s):
    run = s // DILATIONS[g]
    return DIL_TQ if run <= DIL_TQ else min(run, DIL_TQ + 2 * LANES)


def _dil_keys(g, q0, s):
    run = max(s // DILATIONS[g], DIL_TQ)
    lo = (q0 // run) * run
    return pl.multiple_of(jnp.clip(q0 - LANES, lo, lo + run - _dil_block(g, s)), LANES)


def _dil_band(g, q0, start, shape, s):
    row = q0 + lax.broadcasted_iota(jnp.int32, shape, 0)
    col = start + lax.broadcasted_iota(jnp.int32, shape, 1)
    ok = jnp.abs(row - col) <= DIL_HALF
    run = s // DILATIONS[g]
    if run < DIL_TQ:
        shift = run.bit_length() - 1
        ok = ok & ((row >> shift) == (col >> shift))
    return ok


def _dil_tokens(g, q0, s):
    d = DILATIONS[g]
    if d == 1:
        return [(0, DIL_TQ, pl.ds(q0, DIL_TQ))]
    run = s // d
    n = min(run, DIL_TQ)
    return [(lo, n, pl.ds(((q0 + lo) % run) * d + (q0 + lo) // run, n, stride=d)) for lo in range(0, DIL_TQ, n)]


def _dil_gather(ref, pieces):
    return jnp.concatenate([ref[rows, :] for _, _, rows in pieces], axis=0) if len(pieces) > 1 else ref[pieces[0][2], :]


def _dil_head_spec(part, g, s):
    return pl.BlockSpec((None, None, s, HEAD_DIM), lambda b, j: (b, part * DIL_HEADS + g * DIL_GROUP_HEADS + j, 0, 0))


def _dil_attn_fwd(heads, *, name):
    b, _, s, _ = heads.shape
    n_g = len(DILATIONS)

    def body(*refs):
        qkv = refs[:3 * n_g]
        o_ref, l_ref, og_ref, lg_ref = refs[3 * n_g:]
        for g in range(n_g):
            q_ref, k_ref, v_ref = qkv[3 * g:3 * g + 3]
            width = _dil_block(g, s)

            def step(i, carry, g=g, q_ref=q_ref, k_ref=k_ref, v_ref=v_ref, width=width):
                q0 = pl.multiple_of(i * DIL_TQ, DIL_TQ)
                start = _dil_keys(g, q0, s)
                sc = _dot(q_ref[pl.ds(q0, DIL_TQ), :], k_ref[pl.ds(start, width), :], _NT)
                sc = jnp.where(_dil_band(g, q0, start, sc.shape, s), sc, NEG_INF)
                m = jnp.max(sc, axis=1, keepdims=True)
                p = jnp.exp(sc - m)
                den = jnp.sum(p, axis=1, keepdims=True)
                o = _dot(p.astype(BF16), v_ref[pl.ds(start, width), :], _NN) / den
                lse = m + jnp.log(den)
                for lo, n, rows in _dil_tokens(g, q0, s):
                    og_ref[g, rows, :] = o[lo:lo + n]
                    lg_ref[g, rows, :] = lse[lo:lo + n]
                return carry

            lax.fori_loop(0, s // DIL_TQ, step, 0, unroll=BLOCKS_IN_FLIGHT)
        lses = [lg_ref[g] for g in range(n_g)]
        m = functools.reduce(jnp.maximum, lses)
        ws = [jnp.exp(l - m) for l in lses]
        den = functools.reduce(jnp.add, ws)
        o_ref[...] = (functools.reduce(jnp.add, [w * og_ref[g] for g, w in enumerate(ws)]) / den).astype(o_ref.dtype)
        l_ref[...] = m + jnp.log(den)

    out = pl.BlockSpec((None, None, s, HEAD_DIM), lambda bi, j: (bi, j, 0, 0))
    lse = pl.BlockSpec((None, None, s, 1), lambda bi, j: (bi, j, 0, 0))
    return pl.pallas_call(
        body, name=name, grid=(b, DIL_GROUP_HEADS),
        in_specs=[_dil_head_spec(part, g, s) for g in range(n_g) for part in range(3)],
        out_specs=[out, lse],
        out_shape=[jax.ShapeDtypeStruct((b, DIL_GROUP_HEADS, s, HEAD_DIM), BF16),
                   jax.ShapeDtypeStruct((b, DIL_GROUP_HEADS, s, 1), F32)],
        scratch_shapes=[pltpu.VMEM((n_g, s, HEAD_DIM), F32), pltpu.VMEM((n_g, s, 1), F32)],
        compiler_params=_params(dimension_semantics=("parallel", "parallel")),
    )(*([heads] * (3 * n_g)))


def _dil_attn_bwd(heads, out, lse, dout, *, name):
    b, _, s, _ = heads.shape
    n_g = len(DILATIONS)

    def body(*refs):
        qkv = refs[:3 * n_g]
        o_ref, l_ref, do_ref, d_ref, delta_ref = refs[3 * n_g:]
        d_ref[...] = jnp.zeros_like(d_ref)
        delta_ref[...] = jnp.sum(do_ref[...] * o_ref[...].astype(F32), axis=1, keepdims=True)
        for g in range(n_g):
            q_ref, k_ref, v_ref = qkv[3 * g:3 * g + 3]
            width = _dil_block(g, s)

            def step(i, carry, g=g, q_ref=q_ref, k_ref=k_ref, v_ref=v_ref, width=width):
                q0 = pl.multiple_of(i * DIL_TQ, DIL_TQ)
                start = _dil_keys(g, q0, s)
                win = pl.ds(start, width)
                pieces = _dil_tokens(g, q0, s)
                do_b = _dil_gather(do_ref, pieces).astype(BF16)
                q, k, v = q_ref[pl.ds(q0, DIL_TQ), :], k_ref[win, :], v_ref[win, :]
                sc = _dot(q, k, _NT)
                p = jnp.where(_dil_band(g, q0, start, sc.shape, s), jnp.exp(sc - _dil_gather(l_ref, pieces)), 0.0)
                ds = (p * (_dot(do_b, v, _NT) - _dil_gather(delta_ref, pieces))).astype(BF16)
                d_ref[g, pl.ds(q0, DIL_TQ), :] = _dot(ds, k, _NN)
                d_ref[n_g + g, win, :] += _dot(ds, q, _TN)
                d_ref[2 * n_g + g, win, :] += _dot(p.astype(BF16), do_b, _TN)
                return carry

            lax.fori_loop(0, s // DIL_TQ, step, 0, unroll=BLOCKS_IN_FLIGHT)

    per_head = lambda bi, j: (bi, j, 0, 0)
    return pl.pallas_call(
        body, name=name, grid=(b, DIL_GROUP_HEADS),
        in_specs=[_dil_head_spec(part, g, s) for g in range(n_g) for part in range(3)]
        + [pl.BlockSpec((None, None, s, HEAD_DIM), per_head), pl.BlockSpec((None, None, s, 1), per_head),
           pl.BlockSpec((None, None, s, HEAD_DIM), per_head)],
        out_specs=pl.BlockSpec((None, None, 3 * n_g, s, HEAD_DIM), lambda bi, j: (bi, j, 0, 0, 0)),
        out_shape=jax.ShapeDtypeStruct((b, DIL_GROUP_HEADS, 3 * n_g, s, HEAD_DIM), F32),
        scratch_shapes=[pltpu.VMEM((s, 1), F32)],
        compiler_params=_params(dimension_semantics=("parallel", "parallel")),
    )(*([heads] * (3 * n_g)), out, lse, dout)


NA_BIAS_ROWS = 2 * NA_ROWS - 1
NA_BIAS_COLS = 2 * NA_COLS - 1
NA_BLOCK = 4
NA_SPAN = NA_ROWS + NA_BLOCK - 1
NA_Q = NA_BLOCK * GRID_W
NA_KEYS = NA_SPAN * GRID_W
NA_FORMS = 3


def _na_onehot():
    c = np.arange(GRID_W)[:, None]
    k = np.arange(GRID_W)[None, :]
    lo = np.clip(c - NA_COLS // 2, 0, GRID_W - NA_COLS)
    valid = (k >= lo) & (k < lo + NA_COLS)
    onehot = np.zeros((GRID_W, GRID_W, LANES), np.float32)
    cc, kk = np.nonzero(valid)
    onehot[cc, kk, kk - cc + NA_COLS - 1] = 1.0
    return onehot.reshape(GRID_W * GRID_W, LANES), valid.reshape(1, GRID_W * GRID_W)


def _na_block_rows(n_rows):
    table = np.full((NA_FORMS, NA_BLOCK, NA_SPAN), NA_BIAS_ROWS, np.int64)
    n_blocks = n_rows // NA_BLOCK
    for form, ib in enumerate((0, 1, n_blocks - 1)):
        base = min(max(NA_BLOCK * ib - NA_ROWS // 2, 0), n_rows - NA_SPAN)
        for rl in range(NA_BLOCK):
            r = NA_BLOCK * ib + rl
            row_lo = min(max(r - NA_ROWS // 2, 0), n_rows - NA_ROWS)
            for kl in range(NA_SPAN):
                if row_lo <= base + kl < row_lo + NA_ROWS:
                    table[form, rl, kl] = base + kl - r + NA_ROWS - 1
    return table


def _na_block(ib, n_rows):
    n_blocks = n_rows // NA_BLOCK
    base = jnp.clip(NA_BLOCK * ib - NA_ROWS // 2, 0, n_rows - NA_SPAN)
    return base, jnp.where(ib == 0, 0, jnp.where(ib == n_blocks - 1, 2, 1))


def _na_expand_bias(rel_bias, *, name):
    l, h, nr, nc = rel_bias.shape
    onehot, valid = _na_onehot()
    rb = jnp.pad(rel_bias, ((0, 0), (0, 0), (0, 1), (0, LANES - nc))).reshape(l * h * (nr + 1), LANES)
    live = jnp.asarray(np.tile(np.arange(nr + 1) < nr, l * h).astype(np.float32)[:, None])

    def body(rb_ref, oh_ref, valid_ref, live_ref, e_ref):
        e = lax.dot_general(rb_ref[...], oh_ref[...], _NT, precision=lax.Precision.HIGHEST, preferred_element_type=F32)
        e_ref[...] = jnp.where((valid_ref[...] > 0) & (live_ref[...] > 0), e, NEG_INF)

    e = pl.pallas_call(
        body, name=name, out_shape=jax.ShapeDtypeStruct((l * h * (nr + 1), GRID_W * GRID_W), F32), compiler_params=_params(),
    )(rb, jnp.asarray(onehot), jnp.asarray(valid.astype(np.float32)), live)
    return e.reshape(l, h, nr + 1, GRID_W, GRID_W)


def _na_collapse_bias(de, *, name):
    b, h = de.shape[:2]
    onehot, _ = _na_onehot()
    rows = h * NA_BIAS_ROWS

    def diag(e_ref, oh_ref, o_ref):
        e = e_ref[0]
        for bi in range(1, b):
            e = e + e_ref[bi]
        o_ref[...] = lax.dot_general(e, oh_ref[...], _NN, precision=lax.Precision.HIGHEST, preferred_element_type=F32)

    drb = pl.pallas_call(
        diag, name=name, out_shape=jax.ShapeDtypeStruct((rows, LANES), F32), compiler_params=_params(),
    )(de.reshape(b, rows, GRID_W * GRID_W), jnp.asarray(onehot))
    return drb[:, :NA_BIAS_COLS].reshape(h, NA_BIAS_ROWS, NA_BIAS_COLS)


def _na_tiles(n_rows):
    table = _na_block_rows(n_rows)
    return [(f, rl, kl, int(table[f, rl, kl])) for f in range(NA_FORMS) for rl in range(NA_BLOCK) for kl in range(NA_SPAN)]


def _na_tile(ref, form, rl, kl):
    return ref.at[form, rl * GRID_W:(rl + 1) * GRID_W, kl * GRID_W:(kl + 1) * GRID_W]


def _na_head_spec(part, first, s):
    return pl.BlockSpec((None, None, s, HEAD_DIM), lambda b, h: (b, first + part * NA_HEADS + h, 0, 0))


def _na_attn_fwd(heads, bias, *, first, name):
    b, _, s, _ = heads.shape
    n_rows = s // GRID_W
    tiles = _na_tiles(n_rows)

    def body(q_ref, k_ref, v_ref, e_ref, o_ref, l_ref, b_ref):
        for form, rl, kl, i in tiles:
            _na_tile(b_ref, form, rl, kl)[...] = e_ref[i]

        def step(ib, carry):
            base, form = _na_block(ib, n_rows)
            rows = pl.ds(pl.multiple_of(ib * NA_Q, NA_Q), NA_Q)
            win = pl.ds(pl.multiple_of(base * GRID_W, GRID_W), NA_KEYS)
            sc = _dot(q_ref[rows, :], k_ref[win, :], _NT) + b_ref[form]
            m = jnp.max(sc, axis=1, keepdims=True)
            p = jnp.exp(sc - m)
            den = jnp.sum(p, axis=1, keepdims=True)
            o_ref[rows, :] = (_dot(p.astype(BF16), v_ref[win, :], _NN) / den).astype(o_ref.dtype)
            l_ref[rows, :] = m + jnp.log(den)
            return carry

        lax.fori_loop(0, n_rows // NA_BLOCK, step, 0, unroll=BLOCKS_IN_FLIGHT)

    per_head = lambda bi, h: (bi, h, 0, 0)
    return pl.pallas_call(
        body, name=name, grid=(b, NA_HEADS),
        in_specs=[_na_head_spec(part, first, s) for part in range(3)]
        + [pl.BlockSpec((None, NA_BIAS_ROWS + 1, GRID_W, GRID_W), lambda bi, h: (h, 0, 0, 0))],
        out_specs=[pl.BlockSpec((None, None, s, HEAD_DIM), per_head), pl.BlockSpec((None, None, s, 1), per_head)],
        out_shape=[jax.ShapeDtypeStruct((b, NA_HEADS, s, HEAD_DIM), BF16), jax.ShapeDtypeStruct((b, NA_HEADS, s, 1), F32)],
        scratch_shapes=[pltpu.VMEM((NA_FORMS, NA_Q, NA_KEYS), F32)],
        compiler_params=_params(dimension_semantics=("parallel", "parallel")),
    )(heads, heads, heads, bias)


def _na_attn_bwd(heads, bias, out, lse, dout, *, first, name):
    b, _, s, _ = heads.shape
    n_rows = s // GRID_W
    tiles = _na_tiles(n_rows)

    def body(q_ref, k_ref, v_ref, e_ref, o_ref, l_ref, do_ref, d_ref, de_ref, b_ref, db_ref):
        for form, rl, kl, i in tiles:
            _na_tile(b_ref, form, rl, kl)[...] = e_ref[i]
        d_ref[...] = jnp.zeros_like(d_ref)
        db_ref[...] = jnp.zeros_like(db_ref)

        def step(ib, carry):
            base, form = _na_block(ib, n_rows)
            rows = pl.ds(pl.multiple_of(ib * NA_Q, NA_Q), NA_Q)
            win = pl.ds(pl.multiple_of(base * GRID_W, GRID_W), NA_KEYS)
            q, k, v = q_ref[rows, :], k_ref[win, :], v_ref[win, :]
            do = do_ref[rows, :]
            delta = jnp.sum(do * o_ref[rows, :].astype(F32), axis=1, keepdims=True)
            do_b = do.astype(BF16)
            p = jnp.exp(_dot(q, k, _NT) + b_ref[form] - l_ref[rows, :])
            ds = p * (_dot(do_b, v, _NT) - delta)
            db_ref[form] += ds
            ds_b = ds.astype(BF16)
            d_ref[0, rows, :] = _dot(ds_b, k, _NN)
            d_ref[1, win, :] += _dot(ds_b, q, _TN)
            d_ref[2, win, :] += _dot(p.astype(BF16), do_b, _TN)
            return carry

        lax.fori_loop(0, n_rows // NA_BLOCK, step, 0, unroll=BLOCKS_IN_FLIGHT)
        acc = [None] * NA_BIAS_ROWS
        for form, rl, kl, i in tiles:
            if i < NA_BIAS_ROWS:
                t = _na_tile(db_ref, form, rl, kl)[...]
                acc[i] = t if acc[i] is None else acc[i] + t
        for i in range(NA_BIAS_ROWS):
            de_ref[i] = acc[i]

    per_head = lambda bi, h: (bi, h, 0, 0)
    return pl.pallas_call(
        body, name=name, grid=(b, NA_HEADS),
        in_specs=[_na_head_spec(part, first, s) for part in range(3)]
        + [pl.BlockSpec((None, NA_BIAS_ROWS + 1, GRID_W, GRID_W), lambda bi, h: (h, 0, 0, 0)),
           pl.BlockSpec((None, None, s, HEAD_DIM), per_head), pl.BlockSpec((None, None, s, 1), per_head),
           pl.BlockSpec((None, None, s, HEAD_DIM), per_head)],
        out_specs=[pl.BlockSpec((None, None, 3, s, HEAD_DIM), lambda bi, h: (bi, h, 0, 0, 0)),
                   pl.BlockSpec((None, None, NA_BIAS_ROWS, GRID_W, GRID_W), lambda bi, h: (bi, h, 0, 0, 0))],
        out_shape=[jax.ShapeDtypeStruct((b, NA_HEADS, 3, s, HEAD_DIM), F32),
                   jax.ShapeDtypeStruct((b, NA_HEADS, NA_BIAS_ROWS, GRID_W, GRID_W), F32)],
        scratch_shapes=[pltpu.VMEM((NA_FORMS, NA_Q, NA_KEYS), F32), pltpu.VMEM((NA_FORMS, NA_Q, NA_KEYS), F32)],
        compiler_params=_params(dimension_semantics=("parallel", "parallel")),
    )(heads, heads, heads, bias, out, lse, dout)


GATE_TILE = 256


def _gate_fwd(proj, z, *, gate_col, tt, name):
    _, t, d = z.shape
    nj = d // GATE_TILE
    c0 = gate_col // GATE_TILE

    def body(ga_ref, gb_ref, za_ref, zb_ref, o_ref):
        o_ref[...] = (jax.nn.sigmoid(ga_ref[...]) * za_ref[...] + jax.nn.sigmoid(gb_ref[...]) * zb_ref[...]).astype(BF16)

    return pl.pallas_call(
        body, name=name, grid=(t // tt, nj),
        in_specs=[pl.BlockSpec((tt, GATE_TILE), lambda i, j: (i, c0 + j)),
                  pl.BlockSpec((tt, GATE_TILE), lambda i, j: (i, c0 + nj + j)),
                  pl.BlockSpec((None, tt, GATE_TILE), lambda i, j: (0, i, j)),
                  pl.BlockSpec((None, tt, GATE_TILE), lambda i, j: (1, i, j))],
        out_specs=pl.BlockSpec((tt, GATE_TILE), lambda i, j: (i, j)), out_shape=jax.ShapeDtypeStruct((t, d), BF16),
        compiler_params=_params(dimension_semantics=("parallel", "parallel")),
    )(proj, proj, z, z)


def _gate_bwd(dm, proj, z, *, gate_col, tt, name):
    _, t, d = z.shape
    nj = d // GATE_TILE
    c0 = gate_col // GATE_TILE

    def body(dm_ref, g_ref, z_ref, dz_ref, dg_ref):
        dmv = dm_ref[...]
        sg = jax.nn.sigmoid(g_ref[...])
        dz_ref[...] = (dmv * sg).astype(BF16)
        dg_ref[...] = (dmv * z_ref[...] * sg * (1.0 - sg)).astype(BF16)

    return pl.pallas_call(
        body, name=name, grid=(t // tt, 2 * nj),
        in_specs=[pl.BlockSpec((tt, GATE_TILE), lambda i, j: (i, j % nj)),
                  pl.BlockSpec((tt, GATE_TILE), lambda i, j: (i, c0 + j)),
                  pl.BlockSpec((None, tt, GATE_TILE), lambda i, j: (j // nj, i, j % nj))],
        out_specs=[pl.BlockSpec((None, tt, GATE_TILE), lambda i, j: (j // nj, i, j % nj)),
                   pl.BlockSpec((tt, GATE_TILE), lambda i, j: (i, c0 + j))],
        out_shape=[jax.ShapeDtypeStruct((2, t, d), BF16), jax.ShapeDtypeStruct(proj.shape, BF16)],
        compiler_params=_params(dimension_semantics=("parallel", "parallel")),
    )(dm, proj, z)


def _adamw(w, g, m, v, *, name):
    shape = w.shape
    if w.ndim == 3:
        w2, g2, m2, v2 = w, g, m, v
    else:
        w2, g2, m2, v2 = (t.reshape(1, -1, shape[-1]) for t in (w, g, m, v))
    lead, rows, cols = w2.shape
    tr = rows
    for cand in (512, 256, 128, 64, 32, 16, 8):
        if rows % cand == 0:
            tr = cand
            break

    def body(w_ref, g_ref, m_ref, v_ref, d_ref, nm_ref, nv_ref):
        gv = g_ref[...]
        nm = ADAM_B1 * m_ref[...] + (1.0 - ADAM_B1) * gv
        nv = ADAM_B2 * v_ref[...] + (1.0 - ADAM_B2) * (gv * gv)
        m_hat = nm / (1.0 - ADAM_B1 ** ADAM_STEP)
        v_hat = nv / (1.0 - ADAM_B2 ** ADAM_STEP)
        d_ref[...] = -ADAM_LR * (m_hat / (jnp.sqrt(v_hat) + ADAM_EPS) + ADAM_WD * w_ref[...])
        nm_ref[...] = nm
        nv_ref[...] = nv

    blk = pl.BlockSpec((None, tr, cols), lambda l, i: (l, i, 0))
    out = jax.ShapeDtypeStruct((lead, rows, cols), F32)
    res = pl.pallas_call(
        body, name=name, grid=(lead, rows // tr), in_specs=[blk] * 4, out_specs=[blk] * 3, out_shape=[out] * 3,
        compiler_params=_params(dimension_semantics=("parallel", "parallel")),
    )(w2, g2, m2, v2)
    return tuple(t.reshape(shape) for t in res)


def _my_place():
    return lax.axis_index("x"), lax.axis_index("y"), lax.axis_index("c")


def _other_chips(x, y):
    return [(1 - x, y), (x, 1 - y), (1 - x, 1 - y)]


def _chip_no(chip):
    return 2 * chip[0] + chip[1]


def _window(ref, kind, size, chip, lead):
    if kind == "col":
        return ref.at[(*lead, slice(None), pl.ds(pl.multiple_of(chip * size, LANES), size))]
    if kind == "row":
        return ref.at[(*lead, pl.ds(pl.multiple_of(chip * size, BF16_ROWS), size), slice(None))]
    shard = size + HEAD_DIM
    if kind == "win_main":
        return ref.at[(*lead, slice(None), pl.ds(pl.multiple_of(chip * shard + HEAD_DIM * (chip % 2), LANES), size))]
    assert kind == "win_strad"
    return ref.at[(*lead, slice(None), pl.ds(pl.multiple_of(size + 2 * shard * (chip // 2), LANES), LANES))]


def _full_shape(shard, kind):
    _, k, n = shard.shape
    return {"col": (k, N_CHIPS * n), "row": (N_CHIPS * k, n), "win_main": (k, N_CHIPS * (n + HEAD_DIM)),
            "slot": (N_CHIPS, k, n)}[kind]


def _place_own(shard, kind, layer, *, name):
    _, k, n = shard.shape
    tr = _div_tile(k, 512, BF16_ROWS)
    tc = LANES if kind == "win_main" else n
    mine = 2 * lax.axis_index("x") + lax.axis_index("y")
    row0 = mine * (k // tr) if kind == "row" else 0
    col0 = {"col": mine, "row": 0, "slot": 0, "win_main": (mine * (n + HEAD_DIM) + HEAD_DIM * (mine % 2)) // LANES}[kind]
    scalars = jnp.stack([mine, row0, col0]).astype(jnp.int32)

    def body(s_ref, i_ref, o_ref):
        o_ref[...] = i_ref[...]

    if kind == "slot":
        o_spec = pl.BlockSpec((None, tr, tc), lambda i, j, s: (s[0], i, j))
    else:
        o_spec = pl.BlockSpec((tr, tc), lambda i, j, s: (s[1] + i, s[2] + j))
    return pl.pallas_call(
        body, name=name,
        grid_spec=pltpu.PrefetchScalarGridSpec(
            num_scalar_prefetch=1, grid=(k // tr, n // tc),
            in_specs=[pl.BlockSpec((None, tr, tc), lambda i, j, s: (layer, i, j))], out_specs=o_spec),
        out_shape=jax.ShapeDtypeStruct(_full_shape(shard, kind), shard.dtype),
        compiler_params=_params(dimension_semantics=("parallel", "parallel")),
    )(scalars, shard)


class _GatherPlan:
    def __init__(self, src, dst, shapes, kinds, layer, send_sems, recv_sems):
        self.src, self.dst, self.shapes, self.kinds, self.layer = src, dst, shapes, kinds, layer
        self.send_sems, self.recv_sems = send_sems, recv_sems
        self.x, self.y, self.c = _my_place()
        self.mine = 2 * self.x + self.y
        self.chips = _other_chips(self.x, self.y)
        self.n = len(src)

    def half(self, i, chip, half):
        _, k, n = self.shapes[i]
        kind, dst, hk = self.kinds[i], self.dst[i], k // 2
        if kind == "slot":
            return dst.at[chip, pl.ds(pl.multiple_of(half * hk, BF16_ROWS), hk), :]
        if kind == "row":
            return dst.at[pl.ds(pl.multiple_of(chip * k + half * hk, BF16_ROWS), hk), :]
        col0 = chip * n if kind == "col" else chip * (n + HEAD_DIM) + HEAD_DIM * (chip % 2)
        return dst.at[pl.ds(pl.multiple_of(half * hk, BF16_ROWS), hk), pl.ds(pl.multiple_of(col0, LANES), n)]

    def _copy(self, sem, window, to, source=None):
        return pltpu.make_async_remote_copy(src_ref=window if source is None else source, dst_ref=window,
                                            send_sem=self.send_sems.at[sem], recv_sem=self.recv_sems.at[sem],
                                            device_id=to, device_id_type=MESH)

    def sends(self):
        out = []
        for k, chip in enumerate(self.chips):
            for i in range(self.n):
                hk = self.shapes[i][1] // 2
                mine = self.src[i].at[self.layer, pl.ds(pl.multiple_of(self.c * hk, BF16_ROWS), hk), :]
                out.append(self._copy(3 * i + k, self.half(i, self.mine, self.c), (*chip, self.c), source=mine))
        return out

    def arrivals(self):
        return [self._copy(3 * i + k, self.half(i, _chip_no(chip), self.c), (*chip, self.c))
                for k, chip in enumerate(self.chips) for i in range(self.n)]

    def forwards(self, first_sem):
        sibling = (self.x, self.y, 1 - self.c)
        return [self._copy(first_sem + 3 * i + k, self.half(i, _chip_no(chip), self.c), sibling)
                for k, chip in enumerate(self.chips) for i in range(self.n)]

    def forwarded(self, first_sem):
        sibling = (self.x, self.y, 1 - self.c)
        return [self._copy(first_sem + 3 * i + k, self.half(i, _chip_no(chip), 1 - self.c), sibling)
                for k, chip in enumerate(self.chips) for i in range(self.n)]


IN_HBM = pl.BlockSpec(memory_space=pltpu.HBM)
IN_SEM = pl.BlockSpec(memory_space=pltpu.SEMAPHORE)
DATAFLOW = pltpu.SideEffectType.DATAFLOW_SIDE_EFFECTING


def _gather_layer_start(shards, kinds, fulls, layer, after, *, name):
    n_w = len(shards)
    shapes = [sh.shape for sh in shards]

    def body(*refs):
        plan = _GatherPlan(refs[:n_w], refs[n_w:2 * n_w], shapes, kinds, layer, refs[2 * n_w + 1], refs[2 * n_w + 2])
        for cp in plan.sends():
            cp.start()
        token = refs[-1]
        token[...] = jnp.zeros_like(token)

    operands = [pltpu.with_memory_space_constraint(a, pltpu.HBM) for a in (*shards, *fulls)]
    res = pl.pallas_call(
        body, name=name, in_specs=[IN_HBM] * (2 * n_w) + [pl.BlockSpec(memory_space=pl.ANY)],
        out_specs=(IN_SEM, IN_SEM, *([IN_HBM] * (2 * n_w)), pl.BlockSpec(memory_space=pltpu.VMEM)),
        out_shape=(pltpu.SemaphoreType.DMA((3 * n_w,)), pltpu.SemaphoreType.DMA((3 * n_w,)),
                   *[pltpu.HBM(a.shape, a.dtype) for a in operands], jax.ShapeDtypeStruct((8, LANES), F32)),
        input_output_aliases={i: 2 + i for i in range(2 * n_w)},
        compiler_params=pltpu.CompilerParams(has_side_effects=DATAFLOW),
    )(*operands, after)
    return res[0], res[1], res[2:2 + n_w], res[2 + n_w:2 + 2 * n_w], res[-1]


def _gather_layer_wait(send_sems, recv_sems, shards, fulls, kinds, layer, after, *, name):
    n_w = len(shards)
    shapes = [sh.shape for sh in shards]

    def body(*refs):
        plan = _GatherPlan(refs[:n_w], refs[n_w:2 * n_w], shapes, kinds, layer, refs[2 * n_w], refs[2 * n_w + 1])
        for cp in plan.sends():
            cp.wait_send()
        for cp in plan.arrivals():
            cp.wait_recv()

    res = pl.pallas_call(
        body, name=name, in_specs=[IN_HBM] * (2 * n_w) + [IN_SEM, IN_SEM, pl.BlockSpec(memory_space=pl.ANY)],
        out_specs=[IN_HBM] * (2 * n_w), out_shape=[pltpu.HBM(a.shape, a.dtype) for a in (*shards, *fulls)],
        input_output_aliases={i: i for i in range(2 * n_w)},
        compiler_params=pltpu.CompilerParams(has_side_effects=DATAFLOW),
    )(*shards, *fulls, send_sems, recv_sems, after)
    return res[n_w:]


def _gather_layer_forward(shapes, kinds, fulls, *, name):
    n_w = len(fulls)

    def body(*refs):
        plan = _GatherPlan([None] * n_w, refs[n_w:2 * n_w], shapes, kinds, 0, *refs[2 * n_w:])
        passed = plan.forwards(0)
        for cp in passed:
            cp.start()
        for cp in plan.forwarded(0):
            cp.wait_recv()
        for cp in passed:
            cp.wait_send()

    return pl.pallas_call(
        body, name=name, in_specs=[HBM] * n_w, out_specs=[HBM] * n_w,
        out_shape=[jax.ShapeDtypeStruct(f.shape, f.dtype) for f in fulls],
        input_output_aliases={i: i for i in range(n_w)},
        scratch_shapes=[pltpu.SemaphoreType.DMA((3 * n_w,)), pltpu.SemaphoreType.DMA((3 * n_w,))],
    )(*fulls)


def _on_core(layer):
    return (lax.axis_index("c") == layer).astype(jnp.int32).reshape(1)


N_DEVICES = 2 * N_CHIPS


class _ScatterPlan:
    def __init__(self, src, dst, kinds, sizes, layer, send_sems, recv_sems):
        self.src, self.dst, self.kinds, self.sizes, self.layer = src, dst, kinds, sizes, layer
        self.send_sems, self.recv_sems = send_sems, recv_sems
        self.x, self.y, self.c = _my_place()
        self.mine = 2 * self.x + self.y
        self.chips = _other_chips(self.x, self.y)
        self.n = len(src)

    def _copy(self, i, k, window_of, from_chip, from_core, to):
        return pltpu.make_async_remote_copy(src_ref=_window(self.src[i], self.kinds[i], self.sizes[i], window_of, ()),
                                            dst_ref=self.dst[i].at[2 * from_chip + from_core],
                                            send_sem=self.send_sems.at[4 * i + k],
                                            recv_sem=self.recv_sems.at[2 * (4 * i + k) + from_core],
                                            device_id=to, device_id_type=MESH)

    def to_chips(self):
        return [self._copy(i, k, _chip_no(chip), self.mine, self.c, (*chip, self.layer))
                for k, chip in enumerate(self.chips) for i in range(self.n)]

    def to_sibling(self):
        return [self._copy(i, 3, self.mine, self.mine, self.c, (self.x, self.y, self.layer)) for i in range(self.n)]

    def arrivals(self):
        out = [self._copy(i, k, self.mine, _chip_no(chip), core, (*chip, core))
               for k, chip in enumerate(self.chips) for core in (0, 1) for i in range(self.n)]
        return out + [self._copy(i, 3, self.mine, self.mine, 1 - self.layer, (self.x, self.y, 1 - self.layer))
                      for i in range(self.n)]


def _slab_shape(p, kind, size):
    return (N_DEVICES,) + {"col": (p.shape[0], size), "row": (size, p.shape[1]), "win_main": (p.shape[0], size),
                           "win_strad": (p.shape[0], LANES)}[kind]


def _grads_to_chips_start(pairs, kinds, sizes, layer, *, name):
    n_w = len(pairs)

    def body(*refs):
        plan = _ScatterPlan(refs[:n_w], refs[n_w:2 * n_w], kinds, sizes, layer, refs[2 * n_w], refs[2 * n_w + 1])
        for cp in plan.to_chips():
            cp.start()

        @pl.when(plan.c != layer)
        def _():
            for cp in plan.to_sibling():
                cp.start()

        token = refs[-1]
        token[...] = jnp.zeros_like(token)

    slabs = [lax.empty(_slab_shape(p, kind, size), p.dtype) for p, kind, size in zip(pairs, kinds, sizes)]
    operands = [pltpu.with_memory_space_constraint(a, pltpu.HBM) for a in (*pairs, *slabs)]
    res = pl.pallas_call(
        body, name=name, in_specs=[IN_HBM] * (2 * n_w),
        out_specs=(IN_SEM, IN_SEM, *([IN_HBM] * (2 * n_w)), pl.BlockSpec(memory_space=pltpu.VMEM)),
        out_shape=(pltpu.SemaphoreType.DMA((4 * n_w,)), pltpu.SemaphoreType.DMA((8 * n_w,)),
                   *[pltpu.HBM(a.shape, a.dtype) for a in operands], jax.ShapeDtypeStruct((8, LANES), F32)),
        input_output_aliases={i: 2 + i for i in range(2 * n_w)},
        compiler_params=pltpu.CompilerParams(has_side_effects=DATAFLOW),
    )(*operands)
    return res[0], res[1], res[2:2 + n_w], res[2 + n_w:2 + 2 * n_w], res[-1]


def _grads_to_chips_wait(send_sems, recv_sems, pairs, slabs, kinds, sizes, layer, after, *, name):
    n_w = len(pairs)

    def body(*refs):
        plan = _ScatterPlan(refs[:n_w], refs[n_w:2 * n_w], kinds, sizes, layer, refs[2 * n_w], refs[2 * n_w + 1])
        for cp in plan.to_chips():
            cp.wait_send()

        @pl.when(plan.c != layer)
        def _():
            for cp in plan.to_sibling():
                cp.wait_send()

        @pl.when(plan.c == layer)
        def _():
            for cp in plan.arrivals():
                cp.wait_recv()

    res = pl.pallas_call(
        body, name=name, in_specs=[IN_HBM] * (2 * n_w) + [IN_SEM, IN_SEM, pl.BlockSpec(memory_space=pl.ANY)],
        out_specs=[IN_HBM] * (2 * n_w), out_shape=[pltpu.HBM(a.shape, a.dtype) for a in (*pairs, *slabs)],
        input_output_aliases={i: i for i in range(2 * n_w)},
        compiler_params=pltpu.CompilerParams(has_side_effects=DATAFLOW),
    )(*pairs, *slabs, send_sems, recv_sems, after)
    return res[:n_w], res[n_w:]


def _sum_slabs(slabs, pair, kind, size, layer, into, *, name):
    n_s, k, n = slabs.shape
    tr = _div_tile(k, 512, BF16_ROWS)
    tc = n if kind in ("col", "row") else LANES
    x, y, _ = _my_place()
    mine = 2 * x + y
    shard = size + HEAD_DIM
    row0 = mine * (k // tr) if kind == "row" else 0
    col0 = {"col": mine, "row": 0, "win_main": (mine * shard + HEAD_DIM * (mine % 2)) // LANES,
            "win_strad": (size + 2 * shard * (mine // 2)) // LANES}[kind]
    on = _on_core(layer)[0]
    scalars = jnp.stack([2 * mine + layer, row0 * on, col0 * on, on]).astype(jnp.int32)

    def body(s_ref, slab_ref, own_ref, *rest):
        o_ref = rest[-1]
        me = s_ref[0]

        @pl.when(s_ref[3] == 1)
        def _():
            acc = jnp.zeros(o_ref.shape, F32)
            for i in range(n_s):
                acc = acc + jnp.where(me == i, own_ref[...], slab_ref[i]).astype(F32)
            o_ref[...] = acc

    operands = [scalars, slabs, pair] + ([] if into is None else [into])
    return pl.pallas_call(
        body, name=name,
        grid_spec=pltpu.PrefetchScalarGridSpec(
            num_scalar_prefetch=1, grid=(k // tr, n // tc),
            in_specs=[pl.BlockSpec((n_s, tr, tc), lambda i, j, s: (0, i * s[3], j * s[3])),
                      pl.BlockSpec((tr, tc), lambda i, j, s: (s[1] + i * s[3], s[2] + j * s[3]))]
            + ([] if into is None else [HBM]),
            out_specs=pl.BlockSpec((None, tr, tc), lambda i, j, s: (layer, i * s[3], j * s[3]))),
        out_shape=jax.ShapeDtypeStruct((2, k, n), F32),
        input_output_aliases={} if into is None else {3: 0},
        compiler_params=_params(dimension_semantics=("arbitrary", "arbitrary")),
    )(*operands)


def _exchange_layers(bufs, *, name):
    n_w = len(bufs)

    def body(*refs):
        dst = refs[n_w:2 * n_w]
        send_sems, recv_sems = refs[2 * n_w:]
        x, y, c = _my_place()

        def copy(i, layer):
            return pltpu.make_async_remote_copy(src_ref=dst[i].at[layer], dst_ref=dst[i].at[layer], send_sem=send_sems.at[i],
                                                recv_sem=recv_sems.at[i], device_id=(x, y, 1 - c), device_id_type=MESH)

        sends = [copy(i, c) for i in range(n_w)]
        for cp in sends:
            cp.start()
        for i in range(n_w):
            copy(i, 1 - c).wait_recv()
        for cp in sends:
            cp.wait_send()

    return pl.pallas_call(
        body, name=name, in_specs=[HBM] * n_w, out_specs=[HBM] * n_w,
        out_shape=[jax.ShapeDtypeStruct(b.shape, b.dtype) for b in bufs],
        input_output_aliases={i: i for i in range(n_w)},
        scratch_shapes=[pltpu.SemaphoreType.DMA((n_w,)), pltpu.SemaphoreType.DMA((n_w,))],
    )(*bufs)


def _all_sum_small(v, *, name):
    r = v.shape[0]
    relations = [(dx, dy, dc) for dx in (0, 1) for dy in (0, 1) for dc in (0, 1)][1:]

    def body(v_ref, o_ref, buf, send_sems, recv_sems):
        x, y, c = _my_place()
        me = 4 * x + 2 * y + c
        buf[me] = v_ref[...]
        peers = [(x + dx - 2 * x * dx, y + dy - 2 * y * dy, c + dc - 2 * c * dc) for dx, dy, dc in relations]

        def copy(k, slot):
            return pltpu.make_async_remote_copy(src_ref=v_ref, dst_ref=buf.at[slot], send_sem=send_sems.at[k],
                                                recv_sem=recv_sems.at[k], device_id=peers[k], device_id_type=MESH)

        sends = [copy(k, me) for k in range(len(relations))]
        for cp in sends:
            cp.start()
        for k, (px, py, pc) in enumerate(peers):
            copy(k, 4 * px + 2 * py + pc).wait_recv()
        for cp in sends:
            cp.wait_send()
        acc = buf[0]
        for i in range(1, 8):
            acc = acc + buf[i]
        o_ref[...] = acc

    vm = pl.BlockSpec(memory_space=pltpu.VMEM)
    return pl.pallas_call(
        body, name=name, in_specs=[vm], out_specs=vm, out_shape=jax.ShapeDtypeStruct((r, LANES), F32),
        scratch_shapes=[pltpu.VMEM((8, r, LANES), F32), pltpu.SemaphoreType.DMA((7,)), pltpu.SemaphoreType.DMA((7,))],
    )(v)


SHARDED = (("ffn1_w_up", "col"), ("ffn1_w_down", "row"), ("w_in", "win"), ("w_branch_a", "col"),
           ("w_branch_b", "col"), ("w_out", "row"), ("ffn2_w_up", "col"), ("ffn2_w_down", "row"))
REPLICATED = ("ffn1_norm", "mix_norm", "na_rel_bias", "ffn2_norm", "final_norm")


def _weight_pieces(w):
    even = lax.axis_index("y") == 0
    shards, kinds, names = [], [], []
    for name, kind in SHARDED:
        wb = w[name].astype(BF16)
        if kind == "win":
            main = wb.shape[-1] - HEAD_DIM
            assert main % LANES == 0
            zeros = jnp.zeros(wb.shape[:-1] + (HEAD_DIM,), BF16)
            shards += [jnp.where(even, wb[..., :main], wb[..., HEAD_DIM:]),
                       jnp.where(even, jnp.concatenate([wb[..., main:], zeros], -1),
                                 jnp.concatenate([zeros, wb[..., :HEAD_DIM]], -1))]
            kinds += ["win_main", "slot"]
            names += [name, name + "_strad"]
        else:
            shards.append(wb)
            kinds.append(kind)
            names.append(name)
    return names, kinds, shards


def _finish_w_in(full):
    full = dict(full)
    strad = full.pop("w_in_strad")
    main = full["w_in"].shape[1] // N_CHIPS - HEAD_DIM
    for i in range(N_CHIPS // 2):
        lo = main + 2 * (main + HEAD_DIM) * i
        full["w_in"] = full["w_in"].at[:, lo:lo + LANES].set(strad[2 * i] + strad[2 * i + 1])
    return full


def _scatter_pieces(shards):
    names, kinds, sizes, srcs = [], [], [], []
    for name, kind in SHARDED:
        shp = shards[name].shape
        if kind == "win":
            names += [name, name + "_strad"]
            kinds += ["win_main", "win_strad"]
            sizes += [shp[2] - HEAD_DIM] * 2
            srcs += [name, name]
        else:
            names.append(name)
            kinds.append(kind)
            sizes.append(shp[1] if kind == "row" else shp[2])
            srcs.append(name)
    return names, kinds, sizes, srcs


def _finish_weight_grads(reduced, names, tag):
    out = dict(zip(names, _exchange_layers(reduced, name=f"{tag}_layers")))
    if "w_in_strad" in out:
        strad = out.pop("w_in_strad")
        even = lax.axis_index("y") == 0
        out["w_in"] = jnp.where(even, jnp.concatenate([out["w_in"], strad[..., :HEAD_DIM]], -1),
                                jnp.concatenate([strad[..., HEAD_DIM:], out["w_in"]], -1))
    return out


class _Grads:
    def __init__(self):
        self.arrays = {}

    def put(self, weight, layer, a, b, *, cols=None, col_off=0, **kw):
        self.arrays[weight, layer] = _mm(a, b, mode="tn", out_dtype=BF16, out_cols=cols, out_col_off=col_off,
                                         out_into=self.arrays.get((weight, layer)), **kw)


def _ffn_fwd(x, h, w_up, w_down, tag):
    t, d = x.shape
    f = w_down.shape[0]
    a, gate, up = _mm_swiglu_fwd(h, w_up, tm=_div_tile(t, ROWS_NARROW, 8), tn=MXU_N, name=f"{tag}_up")
    x_out = _mm(a, w_down, mode="nn", out_dtype=F32, tm=_div_tile(t, ROWS_WIDE, 8), tn=d, tk=f, alpha=0.5, res=x, name=f"{tag}_down")
    return x_out, (x, h, a, gate, up)


def _ffn_bwd(dx, dxb, saved, norm_g, w_up, w_down, layer, grads, wname, tag, scatter):
    x, h, a, gate, up = saved
    t, d = x.shape
    f = w_down.shape[0]
    tn = _div_tile(f, 1408)
    grads.put(f"{wname}_w_down", layer, a, dxb, tm=tn, tn=d, tk=ROWS_CONTRACTED, alpha=0.5, name=f"{tag}_dwd")
    d_gate, d_up = _mm_swiglu_bwd(dxb, w_down, gate, up, alpha=0.5, tm=_div_tile(t, ROWS_NARROW, 8), tn=MXU_N, name=f"{tag}_da")
    grads.put(f"{wname}_w_up", layer, h, d_gate, cols=2 * f, tm=d, tn=tn, tk=ROWS_CONTRACTED, name=f"{tag}_dwg")
    grads.put(f"{wname}_w_up", layer, h, d_up, cols=2 * f, col_off=f // tn, tm=d, tn=tn, tk=ROWS_CONTRACTED, name=f"{tag}_dwu")
    started = scatter(layer, [f"{wname}_w_up", f"{wname}_w_down"])
    dh = _mm(d_gate, w_up, mode="nt", out_dtype=F32, tm=_div_tile(t, ROWS_WIDE, 8), tn=d, tk=f, name=f"{tag}_dh1")
    dh = _mm(d_up, w_up, mode="nt", out_dtype=F32, tm=_div_tile(t, ROWS_WIDE, 8), tn=d, tk=f, b_k_off=1, res=dh, name=f"{tag}_dh2")
    return _rms_bwd(dh, x, norm_g + started, dx, tt=512, name=f"{tag}_dnorm")


def _to_heads(y, b, n_heads):
    t, w = y.shape
    return y.reshape(b, t // b, n_heads, HEAD_DIM).transpose(0, 2, 1, 3)


def _from_heads(y):
    b, n, s, hd = y.shape
    return y.transpose(0, 2, 1, 3).reshape(b * s, n * hd)


N_QKV = 3 * (DIL_HEADS + NA_HEADS) * HEAD_DIM


def _mixer_fwd(x, b, norm_g, full, bias, tabs, tag):
    t, d = x.shape
    s = t // b
    n_in = full["w_in"].shape[1]
    h = _rms_fwd(x, norm_g, tt=512, name=f"{tag}_norm")
    proj = _mm(h, full["w_in"], mode="nn", out_dtype=F32, tm=_div_tile(t, ROWS_NARROW, 8), tn=MXU_N, tk=d, name=f"{tag}_in")
    heads = _split_heads(proj.reshape(b, s, -1), *tabs, n_pairs=N_QKV // LANES, rot_pairs=DIL_HEADS,
                         scale_ranges=((0, DIL_HEADS // 2), (3 * DIL_HEADS // 2, (3 * DIL_HEADS + NA_HEADS) // 2)),
                         name=f"{tag}_heads")
    ya, lse_a = _dil_attn_fwd(heads, name=f"{tag}_dil")
    yb, lse_b = _na_attn_fwd(heads, bias, first=3 * DIL_HEADS, name=f"{tag}_na")
    ya2, yb2 = _from_heads(ya), _from_heads(yb)
    z = _mm(ya2, full["w_branch_a"], mode="nn", out_dtype=F32, tm=_div_tile(t, ROWS_NARROW, 8), tn=MXU_N, tk=ya2.shape[1],
            out_slab=(0, 2), name=f"{tag}_za")
    z = _mm(yb2, full["w_branch_b"], mode="nn", out_dtype=F32, tm=_div_tile(t, ROWS_NARROW, 8), tn=MXU_N, tk=yb2.shape[1],
            out_slab=(1, 2), out_into=z, name=f"{tag}_zb")
    merged = _gate_fwd(proj, z, gate_col=N_QKV, tt=1024, name=f"{tag}_gate")
    x_out = _mm(merged, full["w_out"], mode="nn", out_dtype=F32, tm=_div_tile(t, ROWS_NARROW, 8), tn=MXU_N, tk=d, res=x, name=f"{tag}_out")
    return x_out, (x, h, proj, heads, ya, lse_a, yb, lse_b, ya2, yb2, z, merged)


def _mixer_bwd(dx, dob, b, saved, norm_g, full, layer, bias, tabs, grads, tag, scatter):
    x, h, proj, heads, ya, lse_a, yb, lse_b, ya2, yb2, z, merged = saved
    t, d = x.shape
    s = t // b
    n_in = full["w_in"].shape[1]
    grads.put("w_out", layer, merged, dob, tm=d, tn=d, tk=ROWS_CONTRACTED, name=f"{tag}_dwo")
    dm = _mm(dob, full["w_out"], mode="nt", out_dtype=F32, tm=_div_tile(t, ROWS_NARROW, 8), tn=MXU_N, tk=d, name=f"{tag}_dm")
    dz, dproj = _gate_bwd(dm, proj, z, gate_col=N_QKV, tt=1024, name=f"{tag}_dgate")
    grads.put("w_branch_a", layer, ya2, dz, b_sel=0, tm=ya2.shape[1], tn=d, tk=ROWS_CONTRACTED, name=f"{tag}_dwa")
    grads.put("w_branch_b", layer, yb2, dz, b_sel=1, tm=yb2.shape[1], tn=d, tk=ROWS_CONTRACTED, name=f"{tag}_dwb")
    started = scatter(layer, ["w_out", "w_branch_a", "w_branch_b"])
    dya = _mm(dz, full["w_branch_a"], mode="nt", out_dtype=F32, tm=_div_tile(t, ROWS_NARROW, 8), tn=MXU_N, tk=d, a_sel=0, name=f"{tag}_dya")
    dyb = _mm(dz, full["w_branch_b"], mode="nt", out_dtype=F32, tm=_div_tile(t, ROWS_NARROW, 8), tn=MXU_N, tk=d, a_sel=1, name=f"{tag}_dyb")
    d_dil = _dil_attn_bwd(heads, ya, lse_a, _to_heads(dya, b, DIL_GROUP_HEADS), name=f"{tag}_ddil")
    d_na, d_bias = _na_attn_bwd(heads, bias, yb, lse_b, _to_heads(dyb, b, NA_HEADS), first=3 * DIL_HEADS, name=f"{tag}_dna")
    dproj = _merge_heads(d_dil, *tabs, heads_per_row=DIL_GROUP_HEADS, rot_pairs=DIL_HEADS, scale_pairs=DIL_HEADS // 2,
                         dilated=True, out_cols=n_in, tile_off=0, into=dproj.reshape(b, s, n_in), name=f"{tag}_dheads_a")
    dproj = _merge_heads(d_na, *tabs, heads_per_row=NA_HEADS, rot_pairs=0, scale_pairs=NA_HEADS // 2, dilated=False,
                         out_cols=n_in, tile_off=3 * DIL_HEADS // 2, into=dproj, name=f"{tag}_dheads_b").reshape(t, n_in)
    grads.put("w_in", layer, h, dproj, tm=_div_tile(d, 512), tn=_div_tile(n_in, 2944), tk=ROWS_CONTRACTED // 2, name=f"{tag}_dwin")
    started = started + scatter(layer, ["w_in"])
    dh = _mm(dproj, full["w_in"], mode="nt", out_dtype=F32, tm=_div_tile(t, ROWS_WIDE, 8), tn=d, tk=_div_tile(n_in, 2944), name=f"{tag}_dh")
    dx_in, dxb_in, d_norm = _rms_bwd(dh, x, norm_g + started, dx, tt=512, name=f"{tag}_dnorm")
    d_rb = _na_collapse_bias(d_bias, name=f"{tag}_dbias")
    return dx_in, dxb_in, d_norm, d_rb


def kernel(x, ffn1_norm, ffn1_w_up, ffn1_w_down, mix_norm, w_in, na_rel_bias, w_branch_a, w_branch_b, w_out, ffn2_norm, ffn2_w_up, ffn2_w_down, final_norm, loss_target, m_ffn1_norm, m_ffn1_w_up, m_ffn1_w_down, m_mix_norm, m_w_in, m_na_rel_bias, m_w_branch_a, m_w_branch_b, m_w_out, m_ffn2_norm, m_ffn2_w_up, m_ffn2_w_down, m_final_norm, v_ffn1_norm, v_ffn1_w_up, v_ffn1_w_down, v_mix_norm, v_w_in, v_na_rel_bias, v_w_branch_a, v_w_branch_b, v_w_out, v_ffn2_norm, v_ffn2_w_up, v_ffn2_w_down, v_final_norm):
    w = dict(ffn1_norm=ffn1_norm, ffn1_w_up=ffn1_w_up, ffn1_w_down=ffn1_w_down, mix_norm=mix_norm, w_in=w_in,
             na_rel_bias=na_rel_bias, w_branch_a=w_branch_a, w_branch_b=w_branch_b, w_out=w_out, ffn2_norm=ffn2_norm,
             ffn2_w_up=ffn2_w_up, ffn2_w_down=ffn2_w_down, final_norm=final_norm)
    mom = dict(ffn1_norm=m_ffn1_norm, ffn1_w_up=m_ffn1_w_up, ffn1_w_down=m_ffn1_w_down, mix_norm=m_mix_norm, w_in=m_w_in,
               na_rel_bias=m_na_rel_bias, w_branch_a=m_w_branch_a, w_branch_b=m_w_branch_b, w_out=m_w_out,
               ffn2_norm=m_ffn2_norm, ffn2_w_up=m_ffn2_w_up, ffn2_w_down=m_ffn2_w_down, final_norm=m_final_norm)
    var = dict(ffn1_norm=v_ffn1_norm, ffn1_w_up=v_ffn1_w_up, ffn1_w_down=v_ffn1_w_down, mix_norm=v_mix_norm, w_in=v_w_in,
               na_rel_bias=v_na_rel_bias, w_branch_a=v_w_branch_a, w_branch_b=v_w_branch_b, w_out=v_w_out,
               ffn2_norm=v_ffn2_norm, ffn2_w_up=v_ffn2_w_up, ffn2_w_down=v_ffn2_w_down, final_norm=v_final_norm)
    b, s, d = x.shape
    t = b * s
    depth = ffn1_norm.shape[0]
    assert depth == 2, "core c of a chip sends / reduces layer c"
    shards = {name: w[name] for name, _ in SHARDED}

    names, kinds, pieces = _weight_pieces(w)
    by_layer = [[p[l:l + 1] for p in pieces] for l in range(depth)]
    own = [[_place_own(p, kind, 0, name=f"own{l}_{nm}") for nm, kind, p in zip(names, kinds, by_layer[l])] for l in range(depth)]
    full = [{}, {}]

    def gather_start(layer, group, after, tag):
        idx = [i for i, nm in enumerate(names) if nm in group]
        pick = lambda seq: [seq[i] for i in idx]
        *state, token = _gather_layer_start(pick(by_layer[layer]), pick(kinds), pick(own[layer]), 0, after, name=f"{tag}_start")
        return (layer, idx, tag, state), token[:1, :1]

    def gather_finish(started, after):
        layer, idx, tag, state = started
        pick = lambda seq: [seq[i] for i in idx]
        landed = _gather_layer_wait(*state, pick(kinds), 0, after, name=f"{tag}_wait")
        done = _gather_layer_forward([by_layer[layer][i].shape for i in idx], pick(kinds), landed, name=f"{tag}_forward")
        full[layer].update(zip(pick(names), done))
        return done[0]

    ffn1, mixer, ffn2 = names[:2], names[2:7], names[7:]
    assert mixer[0] == "w_in" and ffn2[0] == "ffn2_w_up", names
    xc = x.reshape(t, d)
    l0_ffn1, token_ffn1 = gather_start(0, ffn1, xc, "gather_l0_ffn1")
    tabs = _rope_tables(s)
    bias = _na_expand_bias(na_rel_bias, name="na_bias")

    saved = []
    h = _rms_fwd(xc, ffn1_norm[:1] + token_ffn1, tt=512, name="l0_ffn1_norm")
    landed = gather_finish(l0_ffn1, h)
    l0_mixer, token_mixer = gather_start(0, mixer, landed, "gather_l0_mixer")
    xc, s1 = _ffn_fwd(xc, h + token_mixer.astype(BF16), full[0]["ffn1_w_up"], full[0]["ffn1_w_down"], "l0_ffn1")
    landed = gather_finish(l0_mixer, xc)
    full[0] = _finish_w_in(full[0])
    l0_ffn2, token_ffn2 = gather_start(0, ffn2, landed, "gather_l0_ffn2")
    layer1, token_layer1 = gather_start(1, names, landed, "gather_l1")
    xc, s2 = _mixer_fwd(xc, b, mix_norm[:1] + token_ffn2 + token_layer1, full[0], bias[0], tabs, "l0_mix")
    gather_finish(l0_ffn2, xc)
    xc, s3 = _ffn_fwd(xc, _rms_fwd(xc, ffn2_norm[:1], tt=512, name="l0_ffn2_norm"), full[0]["ffn2_w_up"], full[0]["ffn2_w_down"],
                      "l0_ffn2")
    saved.append((s1, s2, s3))
    gather_finish(layer1, xc)
    full[1] = _finish_w_in(full[1])
    for l in range(1, depth):
        xc, s1 = _ffn_fwd(xc, _rms_fwd(xc, ffn1_norm[l:l + 1], tt=512, name=f"l{l}_ffn1_norm"), full[l]["ffn1_w_up"],
                          full[l]["ffn1_w_down"], f"l{l}_ffn1")
        xc, s2 = _mixer_fwd(xc, b, mix_norm[l:l + 1], full[l], bias[l], tabs, f"l{l}_mix")
        xc, s3 = _ffn_fwd(xc, _rms_fwd(xc, ffn2_norm[l:l + 1], tt=512, name=f"l{l}_ffn2_norm"), full[l]["ffn2_w_up"],
                          full[l]["ffn2_w_down"], f"l{l}_ffn2")
        saved.append((s1, s2, s3))

    dx, dxb, d_final, loss_part = _final_loss(xc, final_norm.reshape(1, d), loss_target.reshape(t, d), tt=512, name="final_loss")
    grads = _Grads()
    piece_names, piece_kinds, piece_sizes, piece_srcs = _scatter_pieces(shards)
    scattered = []

    def scatter(layer, weights):
        tag = f"grads{layer}_{weights[0]}"
        idx = [i for i, src in enumerate(piece_srcs) if src in weights]
        pick = lambda seq: [seq[i] for i in idx]
        *state, token = _grads_to_chips_start([grads.arrays[src, layer] for src in pick(piece_srcs)], pick(piece_kinds),
                                              pick(piece_sizes), layer, name=f"{tag}_to_chips_start")
        scattered.append((layer, idx, state))
        return token[:1, :1]
    small = {name: [None] * depth for name in REPLICATED[:-1]}
    for l in reversed(range(depth)):
        s1, s2, s3 = saved[l]
        dx, dxb, small["ffn2_norm"][l] = _ffn_bwd(dx, dxb, s3, ffn2_norm[l:l + 1], full[l]["ffn2_w_up"], full[l]["ffn2_w_down"],
                                                  l, grads, "ffn2", f"l{l}_ffn2", scatter)
        dx, dxb, small["mix_norm"][l], small["na_rel_bias"][l] = _mixer_bwd(
            dx, dxb, b, s2, mix_norm[l:l + 1], full[l], l, bias[l], tabs, grads, f"l{l}_mix", scatter)
        dx, dxb, small["ffn1_norm"][l] = _ffn_bwd(dx, dxb, s1, ffn1_norm[l:l + 1], full[l]["ffn1_w_up"], full[l]["ffn1_w_down"],
                                                  l, grads, "ffn1", f"l{l}_ffn1", scatter)
    grad_x = dx.reshape(b, s, d)
    reduced = [None] * len(piece_names)

    def arrive(group, after):
        layer, idx, state = group
        state = _grads_to_chips_wait(*state, [piece_kinds[i] for i in idx], [piece_sizes[i] for i in idx], layer, after,
                                     name=f"grads{layer}_{piece_names[idx[0]]}_to_chips_wait")
        for i, p, sl in zip(idx, *state):
            reduced[i] = _sum_slabs(sl, p, piece_kinds[i], piece_sizes[i], layer, reduced[i],
                                    name=f"grads{layer}_sum_{piece_names[i]}")
        return idx

    for group in scattered[:-1]:
        arrive(group, dx)
    late = scattered[-1][1]
    early = [i for i in range(len(piece_names)) if i not in late]
    g_out = _finish_weight_grads([reduced[i] for i in early], [piece_names[i] for i in early], "grads_early")

    parts = [jnp.stack(small[name]).reshape(-1) for name in REPLICATED[:-1]] + [d_final.reshape(-1), loss_part[0, :1]]
    sizes = [v.shape[0] for v in parts]
    flat = jnp.concatenate(parts)
    flat = jnp.pad(flat, (0, -flat.shape[0] % (8 * LANES)))
    small_sum = _all_sum_small(flat.reshape(-1, LANES), name="small_all_sum").reshape(-1)
    off = 0
    for name, n in zip(REPLICATED, sizes[:-1]):
        g_out[name] = small_sum[off:off + n].reshape(w[name].shape)
        off += n
    loss = small_sum[off]

    names = list(w)
    delta, new_m, new_v = {}, {}, {}
    for name in [n for n in names if n in g_out]:
        delta[name], new_m[name], new_v[name] = _adamw(w[name], g_out[name], mom[name], var[name], name=f"adamw_{name}")
    arrive(scattered[-1], delta["w_in"])
    g_out.update(_finish_weight_grads([reduced[i] for i in late], [piece_names[i] for i in late], "grads_late"))
    for name in [n for n in names if n not in delta]:
        delta[name], new_m[name], new_v[name] = _adamw(w[name], g_out[name], mom[name], var[name], name=f"adamw_{name}")
    return (loss, grad_x, *[g_out[n] for n in names], *[delta[n] for n in names], *[new_m[n] for n in names],
            *[new_v[n] for n in names])
```

```python
import functools

import numpy as np
import jax
import jax.numpy as jnp
from jax import lax
from jax.experimental import pallas as pl
from jax.experimental.pallas import tpu as pltpu

F32, BF16 = jnp.float32, jnp.bfloat16
MESH = pl.DeviceIdType.MESH

HEAD_DIM = 64
DILATIONS = (1, 4, 16)
DIL_HALF = 64
DIL_GROUP_HEADS = 4
DIL_HEADS = 12
NA_HEADS = 8
GRID_W = 64
NA_ROWS = 8
NA_COLS = 16
ROPE_THETA = 10000.0
RMS_EPS = 1e-6
NEG_INF = -1e30
ADAM_LR, ADAM_B1, ADAM_B2, ADAM_EPS, ADAM_WD, ADAM_STEP = 0.001, 0.9, 0.999, 1e-08, 0.01, 10
QK_SCALE = HEAD_DIM ** -0.5

N_CHIPS = 4
LANES = 128
BF16_ROWS = 16
VMEM_LIMIT = 56 * 1024 * 1024
MXU_N = 256
ROWS_NARROW = 2048
ROWS_WIDE = 512
ROWS_CONTRACTED = 4096
BLOCKS_IN_FLIGHT = 8

_NN = (((1,), (0,)), ((), ()))
_NT = (((1,), (1,)), ((), ()))
_TN = (((0,), (0,)), ((), ()))

HBM = pl.BlockSpec(memory_space=pl.ANY)


def _params(**kw):
    return pltpu.CompilerParams(vmem_limit_bytes=VMEM_LIMIT, **kw)


def _dot(a, b, dims):
    return lax.dot_general(a, b, dims, preferred_element_type=F32)


def _div_tile(n, cap, mult=LANES):
    best = None
    for t in range(mult, min(n, cap) + 1, mult):
        if n % t == 0:
            best = t
    return n if best is None else best


def _stacked(block, index, sel):
    if sel is None:
        return pl.BlockSpec(block, index)
    return pl.BlockSpec((None,) + block, lambda *g: (sel,) + index(*g))


def _mm(a, b, *, mode, out_dtype, tm, tn, tk, name, alpha=1.0, res=None, a_sel=None, b_sel=None, b_k_off=0,
        out_slab=None, out_cols=None, out_col_off=0, out_into=None):
    a2, b2 = a.shape[-2:], b.shape[-2:]
    if mode == "nn":
        (m, k), n = a2, b2[1]
        a_spec = _stacked((tm, tk), lambda i, j, kk: (i, kk), a_sel)
        b_spec = _stacked((tk, tn), lambda i, j, kk: (kk + b_k_off, j), b_sel)
        dims = _NN
    elif mode == "nt":
        (m, k), n = a2, b2[0]
        a_spec = _stacked((tm, tk), lambda i, j, kk: (i, kk), a_sel)
        b_spec = _stacked((tn, tk), lambda i, j, kk: (j, kk + b_k_off), b_sel)
        dims = _NT
    else:
        (k, m), n = a2, b2[1]
        a_spec = _stacked((tk, tm), lambda i, j, kk: (kk, i), a_sel)
        b_spec = _stacked((tk, tn), lambda i, j, kk: (kk + b_k_off, j), b_sel)
        dims = _TN
    assert m % tm == 0 and n % tn == 0 and k % tk == 0, (name, a.shape, b.shape)
    nk = k // tk
    has_res = res is not None
    if out_slab is None:
        o_spec = pl.BlockSpec((tm, tn), lambda i, j, kk: (i, j + out_col_off))
        out_shape = jax.ShapeDtypeStruct((m, n if out_cols is None else out_cols), out_dtype)
    else:
        o_spec = _stacked((tm, tn), lambda i, j, kk: (i, j + out_col_off), out_slab[0])
        out_shape = jax.ShapeDtypeStruct((out_slab[1], m, n if out_cols is None else out_cols), out_dtype)
    r_spec = pl.BlockSpec((tm, tn), lambda i, j, kk: (i, j))
    n_in = 2 + has_res + (out_into is not None)

    def body(*refs):
        a_ref, b_ref = refs[0], refs[1]
        r_ref = refs[2] if has_res else None
        o_ref = refs[n_in]
        p = _dot(a_ref[...], b_ref[...], dims)

        def finish(acc):
            y = acc * alpha if alpha != 1.0 else acc
            if has_res:
                y = y + r_ref[...].astype(F32)
            o_ref[...] = y.astype(o_ref.dtype)

        if nk == 1:
            finish(p)
        else:
            acc_ref = refs[n_in + 1]
            kk = pl.program_id(2)

            @pl.when(kk == 0)
            def _():
                acc_ref[...] = p

            @pl.when(kk > 0)
            def _():
                acc_ref[...] += p

            @pl.when(kk == nk - 1)
            def _():
                finish(acc_ref[...])

    operands = [a, b] + ([res] if has_res else [])
    in_specs = [a_spec, b_spec] + ([r_spec] if has_res else [])
    aliases = {}
    if out_into is not None:
        aliases = {len(operands): 0}
        operands.append(out_into)
        in_specs.append(HBM)
    return pl.pallas_call(
        body, name=name, grid=(m // tm, n // tn, nk), in_specs=in_specs, out_specs=o_spec, out_shape=out_shape,
        scratch_shapes=[pltpu.VMEM((tm, tn), F32)] if nk > 1 else [], input_output_aliases=aliases,
        compiler_params=_params(dimension_semantics=("parallel", "parallel", "arbitrary")),
    )(*operands)


def _mm_swiglu_fwd(h, w_up, *, tm, tn, name):
    m, k = h.shape
    n = w_up.shape[1] // 2
    h_spec = pl.BlockSpec((tm, k), lambda i, j: (i, 0))
    wg_spec = pl.BlockSpec((k, tn), lambda i, j: (0, j))
    wu_spec = pl.BlockSpec((k, tn), lambda i, j: (0, j + n // tn))
    o_spec = pl.BlockSpec((tm, tn), lambda i, j: (i, j))

    def body(h_ref, wg_ref, wu_ref, a_ref, g_ref, u_ref):
        hb = h_ref[...]
        g = _dot(hb, wg_ref[...], _NN)
        u = _dot(hb, wu_ref[...], _NN)
        a_ref[...] = (g * jax.nn.sigmoid(g) * u).astype(BF16)
        g_ref[...] = g.astype(BF16)
        u_ref[...] = u.astype(BF16)

    out = jax.ShapeDtypeStruct((m, n), BF16)
    return pl.pallas_call(
        body, name=name, grid=(m // tm, n // tn), in_specs=[h_spec, wg_spec, wu_spec],
        out_specs=[o_spec] * 3, out_shape=[out] * 3,
        compiler_params=_params(dimension_semantics=("parallel", "parallel")),
    )(h, w_up, w_up)


def _mm_swiglu_bwd(dy, w_down, gate, up, *, alpha, tm, tn, name):
    m, k = dy.shape
    n = w_down.shape[0]
    dy_spec = pl.BlockSpec((tm, k), lambda i, j: (i, 0))
    w_spec = pl.BlockSpec((tn, k), lambda i, j: (j, 0))
    o_spec = pl.BlockSpec((tm, tn), lambda i, j: (i, j))

    def body(dy_ref, w_ref, g_ref, u_ref, dg_ref, du_ref):
        da = _dot(dy_ref[...], w_ref[...], _NT) * alpha
        g = g_ref[...].astype(F32)
        u = u_ref[...].astype(F32)
        sg = jax.nn.sigmoid(g)
        dg_ref[...] = (da * u * (sg * (1.0 + g * (1.0 - sg)))).astype(BF16)
        du_ref[...] = (da * (g * sg)).astype(BF16)

    out = jax.ShapeDtypeStruct((m, n), BF16)
    return pl.pallas_call(
        body, name=name, grid=(m // tm, n // tn), in_specs=[dy_spec, w_spec, o_spec, o_spec],
        out_specs=[o_spec] * 2, out_shape=[out] * 2,
        compiler_params=_params(dimension_semantics=("parallel", "parallel")),
    )(dy, w_down, gate, up)


def _rms_fwd(x, g, *, tt, name):
    t, d = x.shape

    def body(x_ref, g_ref, h_ref):
        xv = x_ref[...]
        rstd = lax.rsqrt(jnp.mean(xv * xv, axis=1, keepdims=True) + RMS_EPS)
        h_ref[...] = (xv * rstd * g_ref[...]).astype(BF16)

    return pl.pallas_call(
        body, name=name, grid=(t // tt,),
        in_specs=[pl.BlockSpec((tt, d), lambda i: (i, 0)), pl.BlockSpec((1, d), lambda i: (0, 0))],
        out_specs=pl.BlockSpec((tt, d), lambda i: (i, 0)), out_shape=jax.ShapeDtypeStruct((t, d), BF16),
        compiler_params=_params(dimension_semantics=("parallel",)),
    )(x, g)


def _rms_bwd(dh, x, g, dres, *, tt, name):
    t, d = x.shape

    def body(dh_ref, x_ref, g_ref, r_ref, dx_ref, dxb_ref, dg_ref):
        xv = x_ref[...]
        rstd = lax.rsqrt(jnp.mean(xv * xv, axis=1, keepdims=True) + RMS_EPS)
        xhat = xv * rstd
        dhv = dh_ref[...]
        dxhat = dhv * g_ref[...]
        dx = r_ref[...] + rstd * (dxhat - xhat * jnp.mean(dxhat * xhat, axis=1, keepdims=True))
        dx_ref[...] = dx
        dxb_ref[...] = dx.astype(BF16)

        @pl.when(pl.program_id(0) == 0)
        def _():
            dg_ref[...] = jnp.zeros_like(dg_ref)

        dg_ref[...] += jnp.sum(dhv * xhat, axis=0, keepdims=True)

    row = pl.BlockSpec((tt, d), lambda i: (i, 0))
    vec = pl.BlockSpec((1, d), lambda i: (0, 0))
    return pl.pallas_call(
        body, name=name, grid=(t // tt,), in_specs=[row, row, vec, row], out_specs=[row, row, vec],
        out_shape=[jax.ShapeDtypeStruct((t, d), F32), jax.ShapeDtypeStruct((t, d), BF16), jax.ShapeDtypeStruct((1, d), F32)],
        compiler_params=_params(dimension_semantics=("arbitrary",)),
    )(dh, x, g, dres)


def _final_loss(x, g, target, *, tt, name):
    t, d = x.shape

    def body(x_ref, g_ref, t_ref, dx_ref, dxb_ref, dg_ref, loss_ref):
        xv = x_ref[...]
        gv = g_ref[...]
        rstd = lax.rsqrt(jnp.mean(xv * xv, axis=1, keepdims=True) + RMS_EPS)
        xhat = xv * rstd
        err = xhat * gv - t_ref[...]
        dy = err * (1.0 / d)
        dxhat = dy * gv
        dx = rstd * (dxhat - xhat * jnp.mean(dxhat * xhat, axis=1, keepdims=True))
        dx_ref[...] = dx
        dxb_ref[...] = dx.astype(BF16)

        @pl.when(pl.program_id(0) == 0)
        def _():
            dg_ref[...] = jnp.zeros_like(dg_ref)
            loss_ref[...] = jnp.zeros_like(loss_ref)

        dg_ref[...] += jnp.sum(dy * xhat, axis=0, keepdims=True)
        part = 0.5 * jnp.sum(jnp.mean(err * err, axis=1, keepdims=True), axis=0, keepdims=True)
        loss_ref[...] += jnp.broadcast_to(part, loss_ref.shape)

    row = pl.BlockSpec((tt, d), lambda i: (i, 0))
    vec = pl.BlockSpec((1, d), lambda i: (0, 0))
    one = pl.BlockSpec((1, LANES), lambda i: (0, 0))
    return pl.pallas_call(
        body, name=name, grid=(t // tt,), in_specs=[row, vec, row], out_specs=[row, row, vec, one],
        out_shape=[jax.ShapeDtypeStruct((t, d), F32), jax.ShapeDtypeStruct((t, d), BF16), jax.ShapeDtypeStruct((1, d), F32),
                   jax.ShapeDtypeStruct((1, LANES), F32)],
        compiler_params=_params(dimension_semantics=("arbitrary",)),
    )(x, g, target)


def _swap_halves(x):
    lane = lax.broadcasted_iota(jnp.int32, x.shape, 1)
    return jnp.where((lane // 32) % 2 == 0, pltpu.roll(x, 96, 1), pltpu.roll(x, 32, 1))


def _rope_tables(s):
    half = HEAD_DIM // 2
    inv_freq = ROPE_THETA ** (-jnp.arange(half, dtype=F32) / half)
    ang = jnp.arange(s).astype(F32)[:, None] * inv_freq[None, :]
    cos, sin = jnp.cos(ang), jnp.sin(ang)
    return jnp.tile(cos, (1, 4)), jnp.concatenate([-sin, sin, -sin, sin], axis=1)


def _dilation_of_tile(p):
    dilated = p < 3 * DIL_HEADS // 2
    g = (p % (DIL_HEADS // 2)) // (DIL_GROUP_HEADS // 2)
    return [(dilated & (g == gi)) | (jnp.logical_not(dilated) if gi == 0 else False) for gi in range(len(DILATIONS))]


def _residue_major(ref, d):
    s = ref.shape[0]
    if d == 1:
        return ref[...]
    return jnp.concatenate([ref[pl.ds(r, s // d, stride=d), :] for r in range(d)], axis=0)


def _split_heads(proj, cos4, sin4, *, n_pairs, rot_pairs, scale_ranges, name):
    b, s, _ = proj.shape

    def body(x_ref, c_ref, s_ref, o_ref):
        p = pl.program_id(1)
        is_q = functools.reduce(jnp.logical_or, [(p >= lo) & (p < hi) for lo, hi in scale_ranges])
        scale = jnp.where(is_q, QK_SCALE, 1.0)

        def put(y):
            o_ref[0] = y[:, :HEAD_DIM].astype(BF16)
            o_ref[1] = y[:, HEAD_DIM:].astype(BF16)

        for d, in_group in zip(DILATIONS, _dilation_of_tile(p)):
            @pl.when(in_group & (p < rot_pairs))
            def _(d=d):
                x = _residue_major(x_ref, d)
                put((x * _residue_major(c_ref, d) + _swap_halves(x) * _residue_major(s_ref, d)) * scale)

            @pl.when(in_group & (p >= rot_pairs))
            def _(d=d):
                put(_residue_major(x_ref, d) * scale)

    tab = pl.BlockSpec((s, LANES), lambda bi, p: (0, 0))
    return pl.pallas_call(
        body, name=name, grid=(b, n_pairs),
        in_specs=[pl.BlockSpec((None, s, LANES), lambda bi, p: (bi, 0, p)), tab, tab],
        out_specs=pl.BlockSpec((None, 2, s, HEAD_DIM), lambda bi, p: (bi, p, 0, 0)),
        out_shape=jax.ShapeDtypeStruct((b, 2 * n_pairs, s, HEAD_DIM), BF16),
        compiler_params=_params(dimension_semantics=("parallel", "parallel")),
    )(proj, cos4, sin4)


def _merge_heads(dheads, cos4, sin4, *, heads_per_row, rot_pairs, scale_pairs, dilated, out_cols, tile_off, into, name):
    b, hpr, r, s, _ = dheads.shape
    n_pairs = hpr * r // 2
    ppr = hpr // 2

    def body(d_ref, c_ref, s_ref, *rest):
        o_ref, t_ref = rest[-2:]
        p = pl.program_id(1)
        scale = jnp.where(p < scale_pairs, QK_SCALE, 1.0)

        def tokens(d):
            dy = jnp.concatenate([d_ref[0], d_ref[1]], axis=1)
            if d == 1:
                return dy
            for res in range(d):
                t_ref[pl.ds(res, s // d, stride=d), :] = dy[res * (s // d):(res + 1) * (s // d), :]
            return t_ref[...]

        groups = _dilation_of_tile(p) if dilated else [p >= 0]
        for d, in_group in zip(DILATIONS, groups):
            @pl.when(in_group & (p < rot_pairs))
            def _(d=d):
                dy = tokens(d)
                o_ref[...] = ((dy * c_ref[...] - _swap_halves(dy) * s_ref[...]) * scale).astype(BF16)

            @pl.when(in_group & (p >= rot_pairs))
            def _(d=d):
                o_ref[...] = (tokens(d) * scale).astype(BF16)

    tab = pl.BlockSpec((s, LANES), lambda bi, p: (0, 0))
    operands = [dheads, cos4, sin4] + ([] if into is None else [into])
    return pl.pallas_call(
        body, name=name, grid=(b, n_pairs),
        in_specs=[pl.BlockSpec((None, 2, None, s, HEAD_DIM), lambda bi, p: (bi, p % ppr, p // ppr, 0, 0)), tab, tab]
        + ([] if into is None else [HBM]),
        out_specs=pl.BlockSpec((None, s, LANES), lambda bi, p: (bi, 0, p + tile_off)),
        out_shape=jax.ShapeDtypeStruct((b, s, out_cols), BF16),
        input_output_aliases={} if into is None else {3: 0},
        scratch_shapes=[pltpu.VMEM((s, LANES), F32)],
        compiler_params=_params(dimension_semantics=("parallel", "parallel")),
    )(*operands)


DIL_TQ = 256


def _pair_half(ref, rows, odd):
    return jnp.where(odd, ref[rows, HEAD_DIM:], ref[rows, :HEAD_DIM])


def _put_pair_half(ref, rows, odd, val):
    @pl.when(jnp.logical_not(odd))
    def _():
        ref[rows, :HEAD_DIM] = val

    @pl.when(odd)
    def _():
        ref[rows, HEAD_DIM:] = val


def _dil_block(g, s):
    run = s // DILATIONS[g]
    return DIL_TQ if run <= DIL_TQ else min(run, DIL_TQ + 2 * LANES)


def _dil_keys(g, q0, s):
    run = max(s // DILATIONS[g], DIL_TQ)
    lo = (q0 // run) * run
    return pl.multiple_of(jnp.clip(q0 - LANES, lo, lo + run - _dil_block(g, s)), LANES)


def _dil_band(g, q0, start, shape, s):
    row = q0 + lax.broadcasted_iota(jnp.int32, shape, 0)
    col = start + lax.broadcasted_iota(jnp.int32, shape, 1)
    ok = jnp.abs(row - col) <= DIL_HALF
    run = s // DILATIONS[g]
    if run < DIL_TQ:
        shift = run.bit_length() - 1
        ok = ok & ((row >> shift) == (col >> shift))
    return ok


def _dil_tokens(g, q0, s):
    d = DILATIONS[g]
    if d == 1:
        return [(0, DIL_TQ, pl.ds(q0, DIL_TQ))]
    run = s // d
    n = min(run, DIL_TQ)
    return [(lo, n, pl.ds(((q0 + lo) % run) * d + (q0 + lo) // run, n, stride=d)) for lo in range(0, DIL_TQ, n)]


def _dil_gather(ref, pieces):
    return jnp.concatenate([ref[rows, :] for _, _, rows in pieces], axis=0) if len(pieces) > 1 else ref[pieces[0][2], :]


def _dil_head_spec(part, g, s):
    return pl.BlockSpec((None, None, s, HEAD_DIM), lambda b, j: (b, part * DIL_HEADS + g * DIL_GROUP_HEADS + j, 0, 0))


def _dil_attn_fwd(heads, *, name):
    b, _, s, _ = heads.shape
    n_g = len(DILATIONS)

    def body(*refs):
        qkv = refs[:3 * n_g]
        o_ref, l_ref, og_ref, lg_ref = refs[3 * n_g:]
        for g in range(n_g):
            q_ref, k_ref, v_ref = qkv[3 * g:3 * g + 3]
            width = _dil_block(g, s)

            def step(i, carry, g=g, q_ref=q_ref, k_ref=k_ref, v_ref=v_ref, width=width):
                q0 = pl.multiple_of(i * DIL_TQ, DIL_TQ)
                start = _dil_keys(g, q0, s)
                sc = _dot(q_ref[pl.ds(q0, DIL_TQ), :], k_ref[pl.ds(start, width), :], _NT)
                sc = jnp.where(_dil_band(g, q0, start, sc.shape, s), sc, NEG_INF)
                m = jnp.max(sc, axis=1, keepdims=True)
                p = jnp.exp(sc - m)
                den = jnp.sum(p, axis=1, keepdims=True)
                o = _dot(p.astype(BF16), v_ref[pl.ds(start, width), :], _NN) / den
                lse = m + jnp.log(den)
                for lo, n, rows in _dil_tokens(g, q0, s):
                    og_ref[g, rows, :] = o[lo:lo + n]
                    lg_ref[g, rows, :] = lse[lo:lo + n]
                return carry

            lax.fori_loop(0, s // DIL_TQ, step, 0, unroll=BLOCKS_IN_FLIGHT)
        lses = [lg_ref[g] for g in range(n_g)]
        m = functools.reduce(jnp.maximum, lses)
        ws = [jnp.exp(l - m) for l in lses]
        den = functools.reduce(jnp.add, ws)
        mixed = functools.reduce(jnp.add, [w * og_ref[g] for g, w in enumerate(ws)]) / den
        _put_pair_half(o_ref, slice(None), pl.program_id(1) % 2 == 1, mixed.astype(o_ref.dtype))
        l_ref[...] = m + jnp.log(den)

    out = pl.BlockSpec((None, s, 2 * HEAD_DIM), lambda bi, j: (bi, 0, j // 2))
    lse = pl.BlockSpec((None, None, s, 1), lambda bi, j: (bi, j, 0, 0))
    return pl.pallas_call(
        body, name=name, grid=(b, DIL_GROUP_HEADS),
        in_specs=[_dil_head_spec(part, g, s) for g in range(n_g) for part in range(3)],
        out_specs=[out, lse],
        out_shape=[jax.ShapeDtypeStruct((b, s, DIL_GROUP_HEADS * HEAD_DIM), BF16),
                   jax.ShapeDtypeStruct((b, DIL_GROUP_HEADS, s, 1), F32)],
        scratch_shapes=[pltpu.VMEM((n_g, s, HEAD_DIM), F32), pltpu.VMEM((n_g, s, 1), F32)],
        compiler_params=_params(dimension_semantics=("parallel", "arbitrary")),
    )(*([heads] * (3 * n_g)))


def _dil_attn_bwd(heads, out, lse, dout, *, name):
    b, _, s, _ = heads.shape
    n_g = len(DILATIONS)

    def body(*refs):
        qkv = refs[:3 * n_g]
        o_ref, l_ref, dout_ref, d_ref, delta_ref, do_ref = refs[3 * n_g:]
        d_ref[...] = jnp.zeros_like(d_ref)
        odd = pl.program_id(1) % 2 == 1
        do_ref[...] = _pair_half(dout_ref, slice(None), odd)
        delta_ref[...] = jnp.sum(do_ref[...] * _pair_half(o_ref, slice(None), odd).astype(F32), axis=1, keepdims=True)
        for g in range(n_g):
            q_ref, k_ref, v_ref = qkv[3 * g:3 * g + 3]
            width = _dil_block(g, s)

            def step(i, carry, g=g, q_ref=q_ref, k_ref=k_ref, v_ref=v_ref, width=width):
                q0 = pl.multiple_of(i * DIL_TQ, DIL_TQ)
                start = _dil_keys(g, q0, s)
                win = pl.ds(start, width)
                pieces = _dil_tokens(g, q0, s)
                do_b = _dil_gather(do_ref, pieces).astype(BF16)
                q, k, v = q_ref[pl.ds(q0, DIL_TQ), :], k_ref[win, :], v_ref[win, :]
                sc = _dot(q, k, _NT)
                p = jnp.where(_dil_band(g, q0, start, sc.shape, s), jnp.exp(sc - _dil_gather(l_ref, pieces)), 0.0)
                ds = (p * (_dot(do_b, v, _NT) - _dil_gather(delta_ref, pieces))).astype(BF16)
                d_ref[g, pl.ds(q0, DIL_TQ), :] = _dot(ds, k, _NN)
                d_ref[n_g + g, win, :] += _dot(ds, q, _TN)
                d_ref[2 * n_g + g, win, :] += _dot(p.astype(BF16), do_b, _TN)
                return carry

            lax.fori_loop(0, s // DIL_TQ, step, 0, unroll=BLOCKS_IN_FLIGHT)

    per_head = lambda bi, j: (bi, j, 0, 0)
    pair = pl.BlockSpec((None, s, 2 * HEAD_DIM), lambda bi, j: (bi, 0, j // 2))
    return pl.pallas_call(
        body, name=name, grid=(b, DIL_GROUP_HEADS),
        in_specs=[_dil_head_spec(part, g, s) for g in range(n_g) for part in range(3)]
        + [pair, pl.BlockSpec((None, None, s, 1), per_head), pair],
        out_specs=pl.BlockSpec((None, None, 3 * n_g, s, HEAD_DIM), lambda bi, j: (bi, j, 0, 0, 0)),
        out_shape=jax.ShapeDtypeStruct((b, DIL_GROUP_HEADS, 3 * n_g, s, HEAD_DIM), F32),
        scratch_shapes=[pltpu.VMEM((s, 1), F32), pltpu.VMEM((s, HEAD_DIM), F32)],
        compiler_params=_params(dimension_semantics=("parallel", "parallel")),
    )(*([heads] * (3 * n_g)), out, lse, dout)


NA_BIAS_ROWS = 2 * NA_ROWS - 1
NA_BIAS_COLS = 2 * NA_COLS - 1
NA_BLOCK = 4
NA_SPAN = NA_ROWS + NA_BLOCK - 1
NA_Q = NA_BLOCK * GRID_W
NA_KEYS = NA_SPAN * GRID_W
NA_FORMS = 3


def _na_onehot():
    c = np.arange(GRID_W)[:, None]
    k = np.arange(GRID_W)[None, :]
    lo = np.clip(c - NA_COLS // 2, 0, GRID_W - NA_COLS)
    valid = (k >= lo) & (k < lo + NA_COLS)
    onehot = np.zeros((GRID_W, GRID_W, LANES), np.float32)
    cc, kk = np.nonzero(valid)
    onehot[cc, kk, kk - cc + NA_COLS - 1] = 1.0
    return onehot.reshape(GRID_W * GRID_W, LANES), valid.reshape(1, GRID_W * GRID_W)


def _na_block_rows(n_rows):
    table = np.full((NA_FORMS, NA_BLOCK, NA_SPAN), NA_BIAS_ROWS, np.int64)
    n_blocks = n_rows // NA_BLOCK
    for form, ib in enumerate((0, 1, n_blocks - 1)):
        base = min(max(NA_BLOCK * ib - NA_ROWS // 2, 0), n_rows - NA_SPAN)
        for rl in range(NA_BLOCK):
            r = NA_BLOCK * ib + rl
            row_lo = min(max(r - NA_ROWS // 2, 0), n_rows - NA_ROWS)
            for kl in range(NA_SPAN):
                if row_lo <= base + kl < row_lo + NA_ROWS:
                    table[form, rl, kl] = base + kl - r + NA_ROWS - 1
    return table


def _na_block(ib, n_rows):
    n_blocks = n_rows // NA_BLOCK
    base = jnp.clip(NA_BLOCK * ib - NA_ROWS // 2, 0, n_rows - NA_SPAN)
    return base, jnp.where(ib == 0, 0, jnp.where(ib == n_blocks - 1, 2, 1))


def _na_expand_bias(rel_bias, *, name):
    l, h, nr, nc = rel_bias.shape
    onehot, valid = _na_onehot()
    rb = jnp.pad(rel_bias, ((0, 0), (0, 0), (0, 1), (0, LANES - nc))).reshape(l * h * (nr + 1), LANES)
    live = jnp.asarray(np.tile(np.arange(nr + 1) < nr, l * h).astype(np.float32)[:, None])

    def body(rb_ref, oh_ref, valid_ref, live_ref, e_ref):
        e = lax.dot_general(rb_ref[...], oh_ref[...], _NT, precision=lax.Precision.HIGHEST, preferred_element_type=F32)
        e_ref[...] = jnp.where((valid_ref[...] > 0) & (live_ref[...] > 0), e, NEG_INF)

    e = pl.pallas_call(
        body, name=name, out_shape=jax.ShapeDtypeStruct((l * h * (nr + 1), GRID_W * GRID_W), F32), compiler_params=_params(),
    )(rb, jnp.asarray(onehot), jnp.asarray(valid.astype(np.float32)), live)
    return e.reshape(l, h, nr + 1, GRID_W, GRID_W)


def _na_collapse_bias(de, *, name):
    b, h = de.shape[:2]
    onehot, _ = _na_onehot()
    rows = h * NA_BIAS_ROWS

    def diag(e_ref, oh_ref, o_ref):
        e = e_ref[0]
        for bi in range(1, b):
            e = e + e_ref[bi]
        o_ref[...] = lax.dot_general(e, oh_ref[...], _NN, precision=lax.Precision.HIGHEST, preferred_element_type=F32)

    drb = pl.pallas_call(
        diag, name=name, out_shape=jax.ShapeDtypeStruct((rows, LANES), F32), compiler_params=_params(),
    )(de.reshape(b, rows, GRID_W * GRID_W), jnp.asarray(onehot))
    return drb[:, :NA_BIAS_COLS].reshape(h, NA_BIAS_ROWS, NA_BIAS_COLS)


def _na_tiles(n_rows):
    table = _na_block_rows(n_rows)
    return [(f, rl, kl, int(table[f, rl, kl])) for f in range(NA_FORMS) for rl in range(NA_BLOCK) for kl in range(NA_SPAN)]


def _na_tile(ref, form, rl, kl):
    return ref.at[form, rl * GRID_W:(rl + 1) * GRID_W, kl * GRID_W:(kl + 1) * GRID_W]


def _na_head_spec(part, first, s):
    return pl.BlockSpec((None, None, s, HEAD_DIM), lambda b, h: (b, first + part * NA_HEADS + h, 0, 0))


def _na_attn_fwd(heads, bias, *, first, name):
    b, _, s, _ = heads.shape
    n_rows = s // GRID_W
    tiles = _na_tiles(n_rows)

    def body(q_ref, k_ref, v_ref, e_ref, o_ref, l_ref, b_ref):
        for form, rl, kl, i in tiles:
            _na_tile(b_ref, form, rl, kl)[...] = e_ref[i]

        def step(ib, carry):
            base, form = _na_block(ib, n_rows)
            rows = pl.ds(pl.multiple_of(ib * NA_Q, NA_Q), NA_Q)
            win = pl.ds(pl.multiple_of(base * GRID_W, GRID_W), NA_KEYS)
            sc = _dot(q_ref[rows, :], k_ref[win, :], _NT) + b_ref[form]
            m = jnp.max(sc, axis=1, keepdims=True)
            p = jnp.exp(sc - m)
            den = jnp.sum(p, axis=1, keepdims=True)
            _put_pair_half(o_ref, rows, pl.program_id(1) % 2 == 1, (_dot(p.astype(BF16), v_ref[win, :], _NN) / den).astype(o_ref.dtype))
            l_ref[rows, :] = m + jnp.log(den)
            return carry

        lax.fori_loop(0, n_rows // NA_BLOCK, step, 0, unroll=BLOCKS_IN_FLIGHT)

    per_head = lambda bi, h: (bi, h, 0, 0)
    return pl.pallas_call(
        body, name=name, grid=(b, NA_HEADS),
        in_specs=[_na_head_spec(part, first, s) for part in range(3)]
        + [pl.BlockSpec((None, NA_BIAS_ROWS + 1, GRID_W, GRID_W), lambda bi, h: (h, 0, 0, 0))],
        out_specs=[pl.BlockSpec((None, s, 2 * HEAD_DIM), lambda bi, h: (bi, 0, h // 2)), pl.BlockSpec((None, None, s, 1), per_head)],
        out_shape=[jax.ShapeDtypeStruct((b, s, NA_HEADS * HEAD_DIM), BF16), jax.ShapeDtypeStruct((b, NA_HEADS, s, 1), F32)],
        scratch_shapes=[pltpu.VMEM((NA_FORMS, NA_Q, NA_KEYS), F32)],
        compiler_params=_params(dimension_semantics=("parallel", "arbitrary")),
    )(heads, heads, heads, bias)


def _na_attn_bwd(heads, bias, out, lse, dout, *, first, name):
    b, _, s, _ = heads.shape
    n_rows = s // GRID_W
    tiles = _na_tiles(n_rows)

    def body(q_ref, k_ref, v_ref, e_ref, o_ref, l_ref, do_ref, d_ref, de_ref, b_ref, db_ref):
        for form, rl, kl, i in tiles:
            _na_tile(b_ref, form, rl, kl)[...] = e_ref[i]
        d_ref[...] = jnp.zeros_like(d_ref)
        db_ref[...] = jnp.zeros_like(db_ref)

        def step(ib, carry):
            base, form = _na_block(ib, n_rows)
            rows = pl.ds(pl.multiple_of(ib * NA_Q, NA_Q), NA_Q)
            win = pl.ds(pl.multiple_of(base * GRID_W, GRID_W), NA_KEYS)
            q, k, v = q_ref[rows, :], k_ref[win, :], v_ref[win, :]
            odd = pl.program_id(1) % 2 == 1
            do = _pair_half(do_ref, rows, odd)
            delta = jnp.sum(do * _pair_half(o_ref, rows, odd).astype(F32), axis=1, keepdims=True)
            do_b = do.astype(BF16)
            p = jnp.exp(_dot(q, k, _NT) + b_ref[form] - l_ref[rows, :])
            ds = p * (_dot(do_b, v, _NT) - delta)
            db_ref[form] += ds
            ds_b = ds.astype(BF16)
            d_ref[0, rows, :] = _dot(ds_b, k, _NN)
            d_ref[1, win, :] += _dot(ds_b, q, _TN)
            d_ref[2, win, :] += _dot(p.astype(BF16), do_b, _TN)
            return carry

        lax.fori_loop(0, n_rows // NA_BLOCK, step, 0, unroll=BLOCKS_IN_FLIGHT)
        acc = [None] * NA_BIAS_ROWS
        for form, rl, kl, i in tiles:
            if i < NA_BIAS_ROWS:
                t = _na_tile(db_ref, form, rl, kl)[...]
                acc[i] = t if acc[i] is None else acc[i] + t
        for i in range(NA_BIAS_ROWS):
            de_ref[i] = acc[i]

    per_head = lambda bi, h: (bi, h, 0, 0)
    return pl.pallas_call(
        body, name=name, grid=(b, NA_HEADS),
        in_specs=[_na_head_spec(part, first, s) for part in range(3)]
        + [pl.BlockSpec((None, NA_BIAS_ROWS + 1, GRID_W, GRID_W), lambda bi, h: (h, 0, 0, 0)),
           pl.BlockSpec((None, s, 2 * HEAD_DIM), lambda bi, h: (bi, 0, h // 2)), pl.BlockSpec((None, None, s, 1), per_head),
           pl.BlockSpec((None, s, 2 * HEAD_DIM), lambda bi, h: (bi, 0, h // 2))],
        out_specs=[pl.BlockSpec((None, None, 3, s, HEAD_DIM), lambda bi, h: (bi, h, 0, 0, 0)),
                   pl.BlockSpec((None, None, NA_BIAS_ROWS, GRID_W, GRID_W), lambda bi, h: (bi, h, 0, 0, 0))],
        out_shape=[jax.ShapeDtypeStruct((b, NA_HEADS, 3, s, HEAD_DIM), F32),
                   jax.ShapeDtypeStruct((b, NA_HEADS, NA_BIAS_ROWS, GRID_W, GRID_W), F32)],
        scratch_shapes=[pltpu.VMEM((NA_FORMS, NA_Q, NA_KEYS), F32), pltpu.VMEM((NA_FORMS, NA_Q, NA_KEYS), F32)],
        compiler_params=_params(dimension_semantics=("parallel", "parallel")),
    )(heads, heads, heads, bias, out, lse, dout)


GATE_TILE = 256


def _gate_fwd(proj, z, *, gate_col, tt, name):
    _, t, d = z.shape
    nj = d // GATE_TILE
    c0 = gate_col // GATE_TILE

    def body(ga_ref, gb_ref, za_ref, zb_ref, o_ref):
        o_ref[...] = (jax.nn.sigmoid(ga_ref[...]) * za_ref[...] + jax.nn.sigmoid(gb_ref[...]) * zb_ref[...]).astype(BF16)

    return pl.pallas_call(
        body, name=name, grid=(t // tt, nj),
        in_specs=[pl.BlockSpec((tt, GATE_TILE), lambda i, j: (i, c0 + j)),
                  pl.BlockSpec((tt, GATE_TILE), lambda i, j: (i, c0 + nj + j)),
                  pl.BlockSpec((None, tt, GATE_TILE), lambda i, j: (0, i, j)),
                  pl.BlockSpec((None, tt, GATE_TILE), lambda i, j: (1, i, j))],
        out_specs=pl.BlockSpec((tt, GATE_TILE), lambda i, j: (i, j)), out_shape=jax.ShapeDtypeStruct((t, d), BF16),
        compiler_params=_params(dimension_semantics=("parallel", "parallel")),
    )(proj, proj, z, z)


def _gate_bwd(dm, proj, z, *, gate_col, tt, name):
    _, t, d = z.shape
    nj = d // GATE_TILE
    c0 = gate_col // GATE_TILE

    def body(dm_ref, g_ref, z_ref, dz_ref, dg_ref):
        dmv = dm_ref[...]
        sg = jax.nn.sigmoid(g_ref[...])
        dz_ref[...] = (dmv * sg).astype(BF16)
        dg_ref[...] = (dmv * z_ref[...] * sg * (1.0 - sg)).astype(BF16)

    return pl.pallas_call(
        body, name=name, grid=(t // tt, 2 * nj),
        in_specs=[pl.BlockSpec((tt, GATE_TILE), lambda i, j: (i, j % nj)),
                  pl.BlockSpec((tt, GATE_TILE), lambda i, j: (i, c0 + j)),
                  pl.BlockSpec((None, tt, GATE_TILE), lambda i, j: (j // nj, i, j % nj))],
        out_specs=[pl.BlockSpec((None, tt, GATE_TILE), lambda i, j: (j // nj, i, j % nj)),
                   pl.BlockSpec((tt, GATE_TILE), lambda i, j: (i, c0 + j))],
        out_shape=[jax.ShapeDtypeStruct((2, t, d), BF16), jax.ShapeDtypeStruct(proj.shape, BF16)],
        compiler_params=_params(dimension_semantics=("parallel", "parallel")),
    )(dm, proj, z)


def _adamw(w, g, m, v, *, name):
    shape = w.shape
    if w.ndim == 3:
        w2, g2, m2, v2 = w, g, m, v
    else:
        w2, g2, m2, v2 = (t.reshape(1, -1, shape[-1]) for t in (w, g, m, v))
    lead, rows, cols = w2.shape
    tr = rows
    for cand in (512, 256, 128, 64, 32, 16, 8):
        if rows % cand == 0:
            tr = cand
            break

    def body(w_ref, g_ref, m_ref, v_ref, d_ref, nm_ref, nv_ref):
        gv = g_ref[...]
        nm = ADAM_B1 * m_ref[...] + (1.0 - ADAM_B1) * gv
        nv = ADAM_B2 * v_ref[...] + (1.0 - ADAM_B2) * (gv * gv)
        m_hat = nm / (1.0 - ADAM_B1 ** ADAM_STEP)
        v_hat = nv / (1.0 - ADAM_B2 ** ADAM_STEP)
        d_ref[...] = -ADAM_LR * (m_hat / (jnp.sqrt(v_hat) + ADAM_EPS) + ADAM_WD * w_ref[...])
        nm_ref[...] = nm
        nv_ref[...] = nv

    blk = pl.BlockSpec((None, tr, cols), lambda l, i: (l, i, 0))
    out = jax.ShapeDtypeStruct((lead, rows, cols), F32)
    res = pl.pallas_call(
        body, name=name, grid=(lead, rows // tr), in_specs=[blk] * 4, out_specs=[blk] * 3, out_shape=[out] * 3,
        compiler_params=_params(dimension_semantics=("parallel", "parallel")),
    )(w2, g2, m2, v2)
    return tuple(t.reshape(shape) for t in res)


def _my_place():
    return lax.axis_index("x"), lax.axis_index("y"), lax.axis_index("c")


def _other_chips(x, y):
    return [(1 - x, y), (x, 1 - y), (1 - x, 1 - y)]


def _chip_no(chip):
    return 2 * chip[0] + chip[1]


def _window(ref, kind, size, chip, lead):
    if kind == "col":
        return ref.at[(*lead, slice(None), pl.ds(pl.multiple_of(chip * size, LANES), size))]
    if kind == "row":
        return ref.at[(*lead, pl.ds(pl.multiple_of(chip * size, BF16_ROWS), size), slice(None))]
    shard = size + HEAD_DIM
    if kind == "win_main":
        return ref.at[(*lead, slice(None), pl.ds(pl.multiple_of(chip * shard + HEAD_DIM * (chip % 2), LANES), size))]
    assert kind == "win_strad"
    return ref.at[(*lead, slice(None), pl.ds(pl.multiple_of(size + 2 * shard * (chip // 2), LANES), LANES))]


def _full_shape(shard, kind):
    _, k, n = shard.shape
    return {"col": (k, N_CHIPS * n), "row": (N_CHIPS * k, n), "win_main": (k, N_CHIPS * (n + HEAD_DIM)),
            "slot": (N_CHIPS, k, n)}[kind]


def _place_own(shard, kind, layer, *, name):
    _, k, n = shard.shape
    tr = _div_tile(k, 512, BF16_ROWS)
    tc = LANES if kind == "win_main" else n
    mine = 2 * lax.axis_index("x") + lax.axis_index("y")
    row0 = mine * (k // tr) if kind == "row" else 0
    col0 = {"col": mine, "row": 0, "slot": 0, "win_main": (mine * (n + HEAD_DIM) + HEAD_DIM * (mine % 2)) // LANES}[kind]
    scalars = jnp.stack([mine, row0, col0]).astype(jnp.int32)

    def body(s_ref, i_ref, o_ref):
        o_ref[...] = i_ref[...]

    if kind == "slot":
        o_spec = pl.BlockSpec((None, tr, tc), lambda i, j, s: (s[0], i, j))
    else:
        o_spec = pl.BlockSpec((tr, tc), lambda i, j, s: (s[1] + i, s[2] + j))
    return pl.pallas_call(
        body, name=name,
        grid_spec=pltpu.PrefetchScalarGridSpec(
            num_scalar_prefetch=1, grid=(k // tr, n // tc),
            in_specs=[pl.BlockSpec((None, tr, tc), lambda i, j, s: (layer, i, j))], out_specs=o_spec),
        out_shape=jax.ShapeDtypeStruct(_full_shape(shard, kind), shard.dtype),
        compiler_params=_params(dimension_semantics=("parallel", "parallel")),
    )(scalars, shard)


class _GatherPlan:
    def __init__(self, src, dst, shapes, kinds, layer, send_sems, recv_sems):
        self.src, self.dst, self.shapes, self.kinds, self.layer = src, dst, shapes, kinds, layer
        self.send_sems, self.recv_sems = send_sems, recv_sems
        self.x, self.y, self.c = _my_place()
        self.mine = 2 * self.x + self.y
        self.chips = _other_chips(self.x, self.y)
        self.n = len(src)

    def half(self, i, chip, half):
        _, k, n = self.shapes[i]
        kind, dst, hk = self.kinds[i], self.dst[i], k // 2
        if kind == "slot":
            return dst.at[chip, pl.ds(pl.multiple_of(half * hk, BF16_ROWS), hk), :]
        if kind == "row":
            return dst.at[pl.ds(pl.multiple_of(chip * k + half * hk, BF16_ROWS), hk), :]
        col0 = chip * n if kind == "col" else chip * (n + HEAD_DIM) + HEAD_DIM * (chip % 2)
        return dst.at[pl.ds(pl.multiple_of(half * hk, BF16_ROWS), hk), pl.ds(pl.multiple_of(col0, LANES), n)]

    def _copy(self, sem, window, to, source=None):
        return pltpu.make_async_remote_copy(src_ref=window if source is None else source, dst_ref=window,
                                            send_sem=self.send_sems.at[sem], recv_sem=self.recv_sems.at[sem],
                                            device_id=to, device_id_type=MESH)

    def sends(self):
        out = []
        for k, chip in enumerate(self.chips):
            for i in range(self.n):
                hk = self.shapes[i][1] // 2
                mine = self.src[i].at[self.layer, pl.ds(pl.multiple_of(self.c * hk, BF16_ROWS), hk), :]
                out.append(self._copy(3 * i + k, self.half(i, self.mine, self.c), (*chip, self.c), source=mine))
        return out

    def arrivals(self):
        return [self._copy(3 * i + k, self.half(i, _chip_no(chip), self.c), (*chip, self.c))
                for k, chip in enumerate(self.chips) for i in range(self.n)]

    def forwards(self, first_sem):
        sibling = (self.x, self.y, 1 - self.c)
        return [self._copy(first_sem + 3 * i + k, self.half(i, _chip_no(chip), self.c), sibling)
                for k, chip in enumerate(self.chips) for i in range(self.n)]

    def forwarded(self, first_sem):
        sibling = (self.x, self.y, 1 - self.c)
        return [self._copy(first_sem + 3 * i + k, self.half(i, _chip_no(chip), 1 - self.c), sibling)
                for k, chip in enumerate(self.chips) for i in range(self.n)]


IN_HBM = pl.BlockSpec(memory_space=pltpu.HBM)
IN_SEM = pl.BlockSpec(memory_space=pltpu.SEMAPHORE)
DATAFLOW = pltpu.SideEffectType.DATAFLOW_SIDE_EFFECTING


def _gather_layer_start(shards, kinds, fulls, layer, after, *, name):
    n_w = len(shards)
    shapes = [sh.shape for sh in shards]

    def body(*refs):
        plan = _GatherPlan(refs[:n_w], refs[n_w:2 * n_w], shapes, kinds, layer, refs[2 * n_w + 1], refs[2 * n_w + 2])
        for cp in plan.sends():
            cp.start()
        token = refs[-1]
        token[...] = jnp.zeros_like(token)

    operands = [pltpu.with_memory_space_constraint(a, pltpu.HBM) for a in (*shards, *fulls)]
    res = pl.pallas_call(
        body, name=name, in_specs=[IN_HBM] * (2 * n_w) + [pl.BlockSpec(memory_space=pl.ANY)],
        out_specs=(IN_SEM, IN_SEM, *([IN_HBM] * (2 * n_w)), pl.BlockSpec(memory_space=pltpu.VMEM)),
        out_shape=(pltpu.SemaphoreType.DMA((3 * n_w,)), pltpu.SemaphoreType.DMA((3 * n_w,)),
                   *[pltpu.HBM(a.shape, a.dtype) for a in operands], jax.ShapeDtypeStruct((8, LANES), F32)),
        input_output_aliases={i: 2 + i for i in range(2 * n_w)},
        compiler_params=pltpu.CompilerParams(has_side_effects=DATAFLOW),
    )(*operands, after)
    return res[0], res[1], res[2:2 + n_w], res[2 + n_w:2 + 2 * n_w], res[-1]


def _gather_layer_wait(send_sems, recv_sems, shards, fulls, kinds, layer, after, *, name):
    n_w = len(shards)
    shapes = [sh.shape for sh in shards]

    def body(*refs):
        plan = _GatherPlan(refs[:n_w], refs[n_w:2 * n_w], shapes, kinds, layer, refs[2 * n_w], refs[2 * n_w + 1])
        for cp in plan.sends():
            cp.wait_send()
        for cp in plan.arrivals():
            cp.wait_recv()

    res = pl.pallas_call(
        body, name=name, in_specs=[IN_HBM] * (2 * n_w) + [IN_SEM, IN_SEM, pl.BlockSpec(memory_space=pl.ANY)],
        out_specs=[IN_HBM] * (2 * n_w), out_shape=[pltpu.HBM(a.shape, a.dtype) for a in (*shards, *fulls)],
        input_output_aliases={i: i for i in range(2 * n_w)},
        compiler_params=pltpu.CompilerParams(has_side_effects=DATAFLOW),
    )(*shards, *fulls, send_sems, recv_sems, after)
    return res[n_w:]


def _gather_layer_forward(shapes, kinds, fulls, *, name):
    n_w = len(fulls)

    def body(*refs):
        plan = _GatherPlan([None] * n_w, refs[n_w:2 * n_w], shapes, kinds, 0, *refs[2 * n_w:])
        passed = plan.forwards(0)
        for cp in passed:
            cp.start()
        for cp in plan.forwarded(0):
            cp.wait_recv()
        for cp in passed:
            cp.wait_send()

    return pl.pallas_call(
        body, name=name, in_specs=[HBM] * n_w, out_specs=[HBM] * n_w,
        out_shape=[jax.ShapeDtypeStruct(f.shape, f.dtype) for f in fulls],
        input_output_aliases={i: i for i in range(n_w)},
        scratch_shapes=[pltpu.SemaphoreType.DMA((3 * n_w,)), pltpu.SemaphoreType.DMA((3 * n_w,))],
    )(*fulls)


def _on_core(layer):
    return (lax.axis_index("c") == layer).astype(jnp.int32).reshape(1)


N_DEVICES = 2 * N_CHIPS


class _ScatterPlan:
    def __init__(self, src, dst, kinds, sizes, layer, send_sems, recv_sems):
        self.src, self.dst, self.kinds, self.sizes, self.layer = src, dst, kinds, sizes, layer
        self.send_sems, self.recv_sems = send_sems, recv_sems
        self.x, self.y, self.c = _my_place()
        self.mine = 2 * self.x + self.y
        self.chips = _other_chips(self.x, self.y)
        self.n = len(src)

    def _copy(self, i, k, window_of, from_chip, from_core, to):
        return pltpu.make_async_remote_copy(src_ref=_window(self.src[i], self.kinds[i], self.sizes[i], window_of, ()),
                                            dst_ref=self.dst[i].at[2 * from_chip + from_core],
                                            send_sem=self.send_sems.at[4 * i + k],
                                            recv_sem=self.recv_sems.at[2 * (4 * i + k) + from_core],
                                            device_id=to, device_id_type=MESH)

    def to_chips(self):
        return [self._copy(i, k, _chip_no(chip), self.mine, self.c, (*chip, self.layer))
                for k, chip in enumerate(self.chips) for i in range(self.n)]

    def to_sibling(self):
        return [self._copy(i, 3, self.mine, self.mine, self.c, (self.x, self.y, self.layer)) for i in range(self.n)]

    def arrivals(self):
        out = [self._copy(i, k, self.mine, _chip_no(chip), core, (*chip, core))
               for k, chip in enumerate(self.chips) for core in (0, 1) for i in range(self.n)]
        return out + [self._copy(i, 3, self.mine, self.mine, 1 - self.layer, (self.x, self.y, 1 - self.layer))
                      for i in range(self.n)]


def _slab_shape(p, kind, size):
    return (N_DEVICES,) + {"col": (p.shape[0], size), "row": (size, p.shape[1]), "win_main": (p.shape[0], size),
                           "win_strad": (p.shape[0], LANES)}[kind]


def _grads_to_chips_start(pairs, kinds, sizes, layer, *, name):
    n_w = len(pairs)

    def body(*refs):
        plan = _ScatterPlan(refs[:n_w], refs[n_w:2 * n_w], kinds, sizes, layer, refs[2 * n_w], refs[2 * n_w + 1])
        for cp in plan.to_chips():
            cp.start()

        @pl.when(plan.c != layer)
        def _():
            for cp in plan.to_sibling():
                cp.start()

        token = refs[-1]
        token[...] = jnp.zeros_like(token)

    slabs = [lax.empty(_slab_shape(p, kind, size), p.dtype) for p, kind, size in zip(pairs, kinds, sizes)]
    operands = [pltpu.with_memory_space_constraint(a, pltpu.HBM) for a in (*pairs, *slabs)]
    res = pl.pallas_call(
        body, name=name, in_specs=[IN_HBM] * (2 * n_w),
        out_specs=(IN_SEM, IN_SEM, *([IN_HBM] * (2 * n_w)), pl.BlockSpec(memory_space=pltpu.VMEM)),
        out_shape=(pltpu.SemaphoreType.DMA((4 * n_w,)), pltpu.SemaphoreType.DMA((8 * n_w,)),
                   *[pltpu.HBM(a.shape, a.dtype) for a in operands], jax.ShapeDtypeStruct((8, LANES), F32)),
        input_output_aliases={i: 2 + i for i in range(2 * n_w)},
        compiler_params=pltpu.CompilerParams(has_side_effects=DATAFLOW),
    )(*operands)
    return res[0], res[1], res[2:2 + n_w], res[2 + n_w:2 + 2 * n_w], res[-1]


def _grads_to_chips_wait(send_sems, recv_sems, pairs, slabs, kinds, sizes, layer, after, *, name):
    n_w = len(pairs)

    def body(*refs):
        plan = _ScatterPlan(refs[:n_w], refs[n_w:2 * n_w], kinds, sizes, layer, refs[2 * n_w], refs[2 * n_w + 1])
        for cp in plan.to_chips():
            cp.wait_send()

        @pl.when(plan.c != layer)
        def _():
            for cp in plan.to_sibling():
                cp.wait_send()

        @pl.when(plan.c == layer)
        def _():
            for cp in plan.arrivals():
                cp.wait_recv()

    res = pl.pallas_call(
        body, name=name, in_specs=[IN_HBM] * (2 * n_w) + [IN_SEM, IN_SEM, pl.BlockSpec(memory_space=pl.ANY)],
        out_specs=[IN_HBM] * (2 * n_w), out_shape=[pltpu.HBM(a.shape, a.dtype) for a in (*pairs, *slabs)],
        input_output_aliases={i: i for i in range(2 * n_w)},
        compiler_params=pltpu.CompilerParams(has_side_effects=DATAFLOW),
    )(*pairs, *slabs, send_sems, recv_sems, after)
    return res[:n_w], res[n_w:]


def _sum_slabs(slabs, pair, kind, size, layer, into, *, name):
    n_s, k, n = slabs.shape
    tr = _div_tile(k, 512, BF16_ROWS)
    tc = n if kind in ("col", "row") else LANES
    x, y, _ = _my_place()
    mine = 2 * x + y
    shard = size + HEAD_DIM
    row0 = mine * (k // tr) if kind == "row" else 0
    col0 = {"col": mine, "row": 0, "win_main": (mine * shard + HEAD_DIM * (mine % 2)) // LANES,
            "win_strad": (size + 2 * shard * (mine // 2)) // LANES}[kind]
    on = _on_core(layer)[0]
    scalars = jnp.stack([2 * mine + layer, row0 * on, col0 * on, on]).astype(jnp.int32)

    def body(s_ref, slab_ref, own_ref, *rest):
        o_ref = rest[-1]
        me = s_ref[0]

        @pl.when(s_ref[3] == 1)
        def _():
            acc = jnp.zeros(o_ref.shape, F32)
            for i in range(n_s):
                acc = acc + jnp.where(me == i, own_ref[...], slab_ref[i]).astype(F32)
            o_ref[...] = acc

    operands = [scalars, slabs, pair] + ([] if into is None else [into])
    return pl.pallas_call(
        body, name=name,
        grid_spec=pltpu.PrefetchScalarGridSpec(
            num_scalar_prefetch=1, grid=(k // tr, n // tc),
            in_specs=[pl.BlockSpec((n_s, tr, tc), lambda i, j, s: (0, i * s[3], j * s[3])),
                      pl.BlockSpec((tr, tc), lambda i, j, s: (s[1] + i * s[3], s[2] + j * s[3]))]
            + ([] if into is None else [HBM]),
            out_specs=pl.BlockSpec((None, tr, tc), lambda i, j, s: (layer, i * s[3], j * s[3]))),
        out_shape=jax.ShapeDtypeStruct((2, k, n), F32),
        input_output_aliases={} if into is None else {3: 0},
        compiler_params=_params(dimension_semantics=("arbitrary", "arbitrary")),
    )(*operands)


def _exchange_layers(bufs, *, name):
    n_w = len(bufs)

    def body(*refs):
        dst = refs[n_w:2 * n_w]
        send_sems, recv_sems = refs[2 * n_w:]
        x, y, c = _my_place()

        def copy(i, layer):
            return pltpu.make_async_remote_copy(src_ref=dst[i].at[layer], dst_ref=dst[i].at[layer], send_sem=send_sems.at[i],
                                                recv_sem=recv_sems.at[i], device_id=(x, y, 1 - c), device_id_type=MESH)

        sends = [copy(i, c) for i in range(n_w)]
        for cp in sends:
            cp.start()
        for i in range(n_w):
            copy(i, 1 - c).wait_recv()
        for cp in sends:
            cp.wait_send()

    return pl.pallas_call(
        body, name=name, in_specs=[HBM] * n_w, out_specs=[HBM] * n_w,
        out_shape=[jax.ShapeDtypeStruct(b.shape, b.dtype) for b in bufs],
        input_output_aliases={i: i for i in range(n_w)},
        scratch_shapes=[pltpu.SemaphoreType.DMA((n_w,)), pltpu.SemaphoreType.DMA((n_w,))],
    )(*bufs)


def _all_sum_small(v, *, name):
    r = v.shape[0]
    relations = [(dx, dy, dc) for dx in (0, 1) for dy in (0, 1) for dc in (0, 1)][1:]

    def body(v_ref, o_ref, buf, send_sems, recv_sems):
        x, y, c = _my_place()
        me = 4 * x + 2 * y + c
        buf[me] = v_ref[...]
        peers = [(x + dx - 2 * x * dx, y + dy - 2 * y * dy, c + dc - 2 * c * dc) for dx, dy, dc in relations]

        def copy(k, slot):
            return pltpu.make_async_remote_copy(src_ref=v_ref, dst_ref=buf.at[slot], send_sem=send_sems.at[k],
                                                recv_sem=recv_sems.at[k], device_id=peers[k], device_id_type=MESH)

        sends = [copy(k, me) for k in range(len(relations))]
        for cp in sends:
            cp.start()
        for k, (px, py, pc) in enumerate(peers):
            copy(k, 4 * px + 2 * py + pc).wait_recv()
        for cp in sends:
            cp.wait_send()
        acc = buf[0]
        for i in range(1, 8):
            acc = acc + buf[i]
        o_ref[...] = acc

    vm = pl.BlockSpec(memory_space=pltpu.VMEM)
    return pl.pallas_call(
        body, name=name, in_specs=[vm], out_specs=vm, out_shape=jax.ShapeDtypeStruct((r, LANES), F32),
        scratch_shapes=[pltpu.VMEM((8, r, LANES), F32), pltpu.SemaphoreType.DMA((7,)), pltpu.SemaphoreType.DMA((7,))],
    )(v)


SHARDED = (("ffn1_w_up", "col"), ("ffn1_w_down", "row"), ("w_in", "win"), ("w_branch_a", "col"),
           ("w_branch_b", "col"), ("w_out", "row"), ("ffn2_w_up", "col"), ("ffn2_w_down", "row"))
REPLICATED = ("ffn1_norm", "mix_norm", "na_rel_bias", "ffn2_norm", "final_norm")


def _weight_pieces(w):
    even = lax.axis_index("y") == 0
    shards, kinds, names = [], [], []
    for name, kind in SHARDED:
        wb = w[name].astype(BF16)
        if kind == "win":
            main = wb.shape[-1] - HEAD_DIM
            assert main % LANES == 0
            zeros = jnp.zeros(wb.shape[:-1] + (HEAD_DIM,), BF16)
            shards += [jnp.where(even, wb[..., :main], wb[..., HEAD_DIM:]),
                       jnp.where(even, jnp.concatenate([wb[..., main:], zeros], -1),
                                 jnp.concatenate([zeros, wb[..., :HEAD_DIM]], -1))]
            kinds += ["win_main", "slot"]
            names += [name, name + "_strad"]
        else:
            shards.append(wb)
            kinds.append(kind)
            names.append(name)
    return names, kinds, shards


def _finish_w_in(full):
    full = dict(full)
    strad = full.pop("w_in_strad")
    main = full["w_in"].shape[1] // N_CHIPS - HEAD_DIM
    for i in range(N_CHIPS // 2):
        lo = main + 2 * (main + HEAD_DIM) * i
        full["w_in"] = full["w_in"].at[:, lo:lo + LANES].set(strad[2 * i] + strad[2 * i + 1])
    return full


def _scatter_pieces(shards):
    names, kinds, sizes, srcs = [], [], [], []
    for name, kind in SHARDED:
        shp = shards[name].shape
        if kind == "win":
            names += [name, name + "_strad"]
            kinds += ["win_main", "win_strad"]
            sizes += [shp[2] - HEAD_DIM] * 2
            srcs += [name, name]
        else:
            names.append(name)
            kinds.append(kind)
            sizes.append(shp[1] if kind == "row" else shp[2])
            srcs.append(name)
    return names, kinds, sizes, srcs


def _finish_weight_grads(reduced, names, tag):
    out = dict(zip(names, _exchange_layers(reduced, name=f"{tag}_layers")))
    if "w_in_strad" in out:
        strad = out.pop("w_in_strad")
        even = lax.axis_index("y") == 0
        out["w_in"] = jnp.where(even, jnp.concatenate([out["w_in"], strad[..., :HEAD_DIM]], -1),
                                jnp.concatenate([strad[..., HEAD_DIM:], out["w_in"]], -1))
    return out


class _Grads:
    def __init__(self):
        self.arrays = {}

    def put(self, weight, layer, a, b, *, cols=None, col_off=0, **kw):
        self.arrays[weight, layer] = _mm(a, b, mode="tn", out_dtype=BF16, out_cols=cols, out_col_off=col_off,
                                         out_into=self.arrays.get((weight, layer)), **kw)


def _ffn_fwd(x, h, w_up, w_down, tag):
    t, d = x.shape
    f = w_down.shape[0]
    a, gate, up = _mm_swiglu_fwd(h, w_up, tm=_div_tile(t, ROWS_NARROW, 8), tn=MXU_N, name=f"{tag}_up")
    x_out = _mm(a, w_down, mode="nn", out_dtype=F32, tm=_div_tile(t, ROWS_WIDE, 8), tn=d, tk=f, alpha=0.5, res=x, name=f"{tag}_down")
    return x_out, (x, h, a, gate, up)


def _ffn_bwd(dx, dxb, saved, norm_g, w_up, w_down, layer, grads, wname, tag, scatter):
    x, h, a, gate, up = saved
    t, d = x.shape
    f = w_down.shape[0]
    tn = _div_tile(f, 1408)
    grads.put(f"{wname}_w_down", layer, a, dxb, tm=tn, tn=d, tk=ROWS_CONTRACTED, alpha=0.5, name=f"{tag}_dwd")
    d_gate, d_up = _mm_swiglu_bwd(dxb, w_down, gate, up, alpha=0.5, tm=_div_tile(t, ROWS_NARROW, 8), tn=MXU_N, name=f"{tag}_da")
    grads.put(f"{wname}_w_up", layer, h, d_gate, cols=2 * f, tm=d, tn=tn, tk=ROWS_CONTRACTED, name=f"{tag}_dwg")
    grads.put(f"{wname}_w_up", layer, h, d_up, cols=2 * f, col_off=f // tn, tm=d, tn=tn, tk=ROWS_CONTRACTED, name=f"{tag}_dwu")
    started = scatter(layer, [f"{wname}_w_up", f"{wname}_w_down"])
    dh = _mm(d_gate, w_up, mode="nt", out_dtype=F32, tm=_div_tile(t, ROWS_WIDE, 8), tn=d, tk=f, name=f"{tag}_dh1")
    dh = _mm(d_up, w_up, mode="nt", out_dtype=F32, tm=_div_tile(t, ROWS_WIDE, 8), tn=d, tk=f, b_k_off=1, res=dh, name=f"{tag}_dh2")
    return _rms_bwd(dh, x, norm_g + started, dx, tt=512, name=f"{tag}_dnorm")


N_QKV = 3 * (DIL_HEADS + NA_HEADS) * HEAD_DIM


def _mixer_fwd(x, b, norm_g, full, bias, tabs, tag):
    t, d = x.shape
    s = t // b
    n_in = full["w_in"].shape[1]
    h = _rms_fwd(x, norm_g, tt=512, name=f"{tag}_norm")
    proj = _mm(h, full["w_in"], mode="nn", out_dtype=F32, tm=_div_tile(t, ROWS_NARROW, 8), tn=MXU_N, tk=d, name=f"{tag}_in")
    heads = _split_heads(proj.reshape(b, s, -1), *tabs, n_pairs=N_QKV // LANES, rot_pairs=DIL_HEADS,
                         scale_ranges=((0, DIL_HEADS // 2), (3 * DIL_HEADS // 2, (3 * DIL_HEADS + NA_HEADS) // 2)),
                         name=f"{tag}_heads")
    ya, lse_a = _dil_attn_fwd(heads, name=f"{tag}_dil")
    yb, lse_b = _na_attn_fwd(heads, bias, first=3 * DIL_HEADS, name=f"{tag}_na")
    ya2, yb2 = ya.reshape(t, -1), yb.reshape(t, -1)
    z = _mm(ya2, full["w_branch_a"], mode="nn", out_dtype=F32, tm=_div_tile(t, ROWS_NARROW, 8), tn=MXU_N, tk=ya2.shape[1],
            out_slab=(0, 2), name=f"{tag}_za")
    z = _mm(yb2, full["w_branch_b"], mode="nn", out_dtype=F32, tm=_div_tile(t, ROWS_NARROW, 8), tn=MXU_N, tk=yb2.shape[1],
            out_slab=(1, 2), out_into=z, name=f"{tag}_zb")
    merged = _gate_fwd(proj, z, gate_col=N_QKV, tt=1024, name=f"{tag}_gate")
    x_out = _mm(merged, full["w_out"], mode="nn", out_dtype=F32, tm=_div_tile(t, ROWS_NARROW, 8), tn=MXU_N, tk=d, res=x, name=f"{tag}_out")
    return x_out, (x, h, proj, heads, ya, lse_a, yb, lse_b, ya2, yb2, z, merged)


def _mixer_bwd(dx, dob, b, saved, norm_g, full, layer, bias, tabs, grads, tag, scatter):
    x, h, proj, heads, ya, lse_a, yb, lse_b, ya2, yb2, z, merged = saved
    t, d = x.shape
    s = t // b
    n_in = full["w_in"].shape[1]
    grads.put("w_out", layer, merged, dob, tm=d, tn=d, tk=ROWS_CONTRACTED, name=f"{tag}_dwo")
    dm = _mm(dob, full["w_out"], mode="nt", out_dtype=F32, tm=_div_tile(t, ROWS_NARROW, 8), tn=MXU_N, tk=d, name=f"{tag}_dm")
    dz, dproj = _gate_bwd(dm, proj, z, gate_col=N_QKV, tt=1024, name=f"{tag}_dgate")
    grads.put("w_branch_a", layer, ya2, dz, b_sel=0, tm=ya2.shape[1], tn=d, tk=ROWS_CONTRACTED, name=f"{tag}_dwa")
    grads.put("w_branch_b", layer, yb2, dz, b_sel=1, tm=yb2.shape[1], tn=d, tk=ROWS_CONTRACTED, name=f"{tag}_dwb")
    started = scatter(layer, ["w_out", "w_branch_a", "w_branch_b"])
    dya = _mm(dz, full["w_branch_a"], mode="nt", out_dtype=F32, tm=_div_tile(t, ROWS_NARROW, 8), tn=MXU_N, tk=d, a_sel=0, name=f"{tag}_dya")
    dyb = _mm(dz, full["w_branch_b"], mode="nt", out_dtype=F32, tm=_div_tile(t, ROWS_NARROW, 8), tn=MXU_N, tk=d, a_sel=1, name=f"{tag}_dyb")
    d_dil = _dil_attn_bwd(heads, ya, lse_a, dya.reshape(b, s, -1), name=f"{tag}_ddil")
    d_na, d_bias = _na_attn_bwd(heads, bias, yb, lse_b, dyb.reshape(b, s, -1), first=3 * DIL_HEADS, name=f"{tag}_dna")
    dproj = _merge_heads(d_dil, *tabs, heads_per_row=DIL_GROUP_HEADS, rot_pairs=DIL_HEADS, scale_pairs=DIL_HEADS // 2,
                         dilated=True, out_cols=n_in, tile_off=0, into=dproj.reshape(b, s, n_in), name=f"{tag}_dheads_a")
    dproj = _merge_heads(d_na, *tabs, heads_per_row=NA_HEADS, rot_pairs=0, scale_pairs=NA_HEADS // 2, dilated=False,
                         out_cols=n_in, tile_off=3 * DIL_HEADS // 2, into=dproj, name=f"{tag}_dheads_b").reshape(t, n_in)
    grads.put("w_in", layer, h, dproj, tm=_div_tile(d, 512), tn=_div_tile(n_in, 2944), tk=ROWS_CONTRACTED // 2, name=f"{tag}_dwin")
    started = started + scatter(layer, ["w_in"])
    dh = _mm(dproj, full["w_in"], mode="nt", out_dtype=F32, tm=_div_tile(t, ROWS_WIDE, 8), tn=d, tk=_div_tile(n_in, 2944), name=f"{tag}_dh")
    dx_in, dxb_in, d_norm = _rms_bwd(dh, x, norm_g + started, dx, tt=512, name=f"{tag}_dnorm")
    d_rb = _na_collapse_bias(d_bias, name=f"{tag}_dbias")
    return dx_in, dxb_in, d_norm, d_rb


def kernel(x, ffn1_norm, ffn1_w_up, ffn1_w_down, mix_norm, w_in, na_rel_bias, w_branch_a, w_branch_b, w_out, ffn2_norm, ffn2_w_up, ffn2_w_down, final_norm, loss_target, m_ffn1_norm, m_ffn1_w_up, m_ffn1_w_down, m_mix_norm, m_w_in, m_na_rel_bias, m_w_branch_a, m_w_branch_b, m_w_out, m_ffn2_norm, m_ffn2_w_up, m_ffn2_w_down, m_final_norm, v_ffn1_norm, v_ffn1_w_up, v_ffn1_w_down, v_mix_norm, v_w_in, v_na_rel_bias, v_w_branch_a, v_w_branch_b, v_w_out, v_ffn2_norm, v_ffn2_w_up, v_ffn2_w_down, v_final_norm):
    w = dict(ffn1_norm=ffn1_norm, ffn1_w_up=ffn1_w_up, ffn1_w_down=ffn1_w_down, mix_norm=mix_norm, w_in=w_in,
             na_rel_bias=na_rel_bias, w_branch_a=w_branch_a, w_branch_b=w_branch_b, w_out=w_out, ffn2_norm=ffn2_norm,
             ffn2_w_up=ffn2_w_up, ffn2_w_down=ffn2_w_down, final_norm=final_norm)
    mom = dict(ffn1_norm=m_ffn1_norm, ffn1_w_up=m_ffn1_w_up, ffn1_w_down=m_ffn1_w_down, mix_norm=m_mix_norm, w_in=m_w_in,
               na_rel_bias=m_na_rel_bias, w_branch_a=m_w_branch_a, w_branch_b=m_w_branch_b, w_out=m_w_out,
               ffn2_norm=m_ffn2_norm, ffn2_w_up=m_ffn2_w_up, ffn2_w_down=m_ffn2_w_down, final_norm=m_final_norm)
    var = dict(ffn1_norm=v_ffn1_norm, ffn1_w_up=v_ffn1_w_up, ffn1_w_down=v_ffn1_w_down, mix_norm=v_mix_norm, w_in=v_w_in,
               na_rel_bias=v_na_rel_bias, w_branch_a=v_w_branch_a, w_branch_b=v_w_branch_b, w_out=v_w_out,
               ffn2_norm=v_ffn2_norm, ffn2_w_up=v_ffn2_w_up, ffn2_w_down=v_ffn2_w_down, final_norm=v_final_norm)
    b, s, d = x.shape
    t = b * s
    depth = ffn1_norm.shape[0]
    assert depth == 2, "core c of a chip sends / reduces layer c"
    shards = {name: w[name] for name, _ in SHARDED}

    names, kinds, pieces = _weight_pieces(w)
    by_layer = [[p[l:l + 1] for p in pieces] for l in range(depth)]
    own = [[_place_own(p, kind, 0, name=f"own{l}_{nm}") for nm, kind, p in zip(names, kinds, by_layer[l])] for l in range(depth)]
    full = [{}, {}]

    def gather_start(layer, group, after, tag):
        idx = [i for i, nm in enumerate(names) if nm in group]
        pick = lambda seq: [seq[i] for i in idx]
        *state, token = _gather_layer_start(pick(by_layer[layer]), pick(kinds), pick(own[layer]), 0, after, name=f"{tag}_start")
        return (layer, idx, tag, state), token[:1, :1]

    def gather_finish(started, after):
        layer, idx, tag, state = started
        pick = lambda seq: [seq[i] for i in idx]
        landed = _gather_layer_wait(*state, pick(kinds), 0, after, name=f"{tag}_wait")
        done = _gather_layer_forward([by_layer[layer][i].shape for i in idx], pick(kinds), landed, name=f"{tag}_forward")
        full[layer].update(zip(pick(names), done))
        return done[0]

    ffn1, mixer, ffn2 = names[:2], names[2:7], names[7:]
    assert mixer[0] == "w_in" and ffn2[0] == "ffn2_w_up", names
    xc = x.reshape(t, d)
    l0_ffn1, token_ffn1 = gather_start(0, ffn1, xc, "gather_l0_ffn1")
    tabs = _rope_tables(s)
    bias = _na_expand_bias(na_rel_bias, name="na_bias")

    saved = []
    h = _rms_fwd(xc, ffn1_norm[:1] + token_ffn1, tt=512, name="l0_ffn1_norm")
    landed = gather_finish(l0_ffn1, h)
    l0_mixer, token_mixer = gather_start(0, mixer, landed, "gather_l0_mixer")
    xc, s1 = _ffn_fwd(xc, h + token_mixer.astype(BF16), full[0]["ffn1_w_up"], full[0]["ffn1_w_down"], "l0_ffn1")
    landed = gather_finish(l0_mixer, xc)
    full[0] = _finish_w_in(full[0])
    l0_ffn2, token_ffn2 = gather_start(0, ffn2, landed, "gather_l0_ffn2")
    layer1, token_layer1 = gather_start(1, names, landed, "gather_l1")
    xc, s2 = _mixer_fwd(xc, b, mix_norm[:1] + token_ffn2 + token_layer1, full[0], bias[0], tabs, "l0_mix")
    gather_finish(l0_ffn2, xc)
    xc, s3 = _ffn_fwd(xc, _rms_fwd(xc, ffn2_norm[:1], tt=512, name="l0_ffn2_norm"), full[0]["ffn2_w_up"], full[0]["ffn2_w_down"],
                      "l0_ffn2")
    saved.append((s1, s2, s3))
    gather_finish(layer1, xc)
    full[1] = _finish_w_in(full[1])
    for l in range(1, depth):
        xc, s1 = _ffn_fwd(xc, _rms_fwd(xc, ffn1_norm[l:l + 1], tt=512, name=f"l{l}_ffn1_norm"), full[l]["ffn1_w_up"],
                          full[l]["ffn1_w_down"], f"l{l}_ffn1")
        xc, s2 = _mixer_fwd(xc, b, mix_norm[l:l + 1], full[l], bias[l], tabs, f"l{l}_mix")
        xc, s3 = _ffn_fwd(xc, _rms_fwd(xc, ffn2_norm[l:l + 1], tt=512, name=f"l{l}_ffn2_norm"), full[l]["ffn2_w_up"],
                          full[l]["ffn2_w_down"], f"l{l}_ffn2")
        saved.append((s1, s2, s3))

    dx, dxb, d_final, loss_part = _final_loss(xc, final_norm.reshape(1, d), loss_target.reshape(t, d), tt=512, name="final_loss")
    grads = _Grads()
    piece_names, piece_kinds, piece_sizes, piece_srcs = _scatter_pieces(shards)
    scattered = []

    def scatter(layer, weights):
        tag = f"grads{layer}_{weights[0]}"
        idx = [i for i, src in enumerate(piece_srcs) if src in weights]
        pick = lambda seq: [seq[i] for i in idx]
        *state, token = _grads_to_chips_start([grads.arrays[src, layer] for src in pick(piece_srcs)], pick(piece_kinds),
                                              pick(piece_sizes), layer, name=f"{tag}_to_chips_start")
        scattered.append((layer, idx, state))
        return token[:1, :1]
    small = {name: [None] * depth for name in REPLICATED[:-1]}
    for l in reversed(range(depth)):
        s1, s2, s3 = saved[l]
        dx, dxb, small["ffn2_norm"][l] = _ffn_bwd(dx, dxb, s3, ffn2_norm[l:l + 1], full[l]["ffn2_w_up"], full[l]["ffn2_w_down"],
                                                  l, grads, "ffn2", f"l{l}_ffn2", scatter)
        dx, dxb, small["mix_norm"][l], small["na_rel_bias"][l] = _mixer_bwd(
            dx, dxb, b, s2, mix_norm[l:l + 1], full[l], l, bias[l], tabs, grads, f"l{l}_mix", scatter)
        dx, dxb, small["ffn1_norm"][l] = _ffn_bwd(dx, dxb, s1, ffn1_norm[l:l + 1], full[l]["ffn1_w_up"], full[l]["ffn1_w_down"],
                                                  l, grads, "ffn1", f"l{l}_ffn1", scatter)
    grad_x = dx.reshape(b, s, d)
    reduced = [None] * len(piece_names)

    def arrive(group, after):
        layer, idx, state = group
        state = _grads_to_chips_wait(*state, [piece_kinds[i] for i in idx], [piece_sizes[i] for i in idx], layer, after,
                                     name=f"grads{layer}_{piece_names[idx[0]]}_to_chips_wait")
        for i, p, sl in zip(idx, *state):
            reduced[i] = _sum_slabs(sl, p, piece_kinds[i], piece_sizes[i], layer, reduced[i],
                                    name=f"grads{layer}_sum_{piece_names[i]}")
        return idx

    for group in scattered[:-1]:
        arrive(group, dx)
    late = scattered[-1][1]
    early = [i for i in range(len(piece_names)) if i not in late]
    g_out = _finish_weight_grads([reduced[i] for i in early], [piece_names[i] for i in early], "grads_early")

    parts = [jnp.stack(small[name]).reshape(-1) for name in REPLICATED[:-1]] + [d_final.reshape(-1), loss_part[0, :1]]
    sizes = [v.shape[0] for v in parts]
    flat = jnp.concatenate(parts)
    flat = jnp.pad(flat, (0, -flat.shape[0] % (8 * LANES)))
    small_sum = _all_sum_small(flat.reshape(-1, LANES), name="small_all_sum").reshape(-1)
    off = 0
    for name, n in zip(REPLICATED, sizes[:-1]):
        g_out[name] = small_sum[off:off + n].reshape(w[name].shape)
        off += n
    loss = small_sum[off]

    names = list(w)
    delta, new_m, new_v = {}, {}, {}
    for name in [n for n in names if n in g_out]:
        delta[name], new_m[name], new_v[name] = _adamw(w[name], g_out[name], mom[name], var[name], name=f"adamw_{name}")
    arrive(scattered[-1], delta["w_in"])
    g_out.update(_finish_weight_grads([reduced[i] for i in late], [piece_names[i] for i in late], "grads_late"))
    for name in [n for n in names if n not in delta]:
        delta[name], new_m[name], new_v[name] = _adamw(w[name], g_out[name], mom[name], var[name], name=f"adamw_{name}")
    return (loss, grad_x, *[g_out[n] for n in names], *[delta[n] for n in names], *[new_m[n] for n in names],
            *[new_v[n] for n in names])
```

```python
import functools

import numpy as np
import jax
import jax.numpy as jnp
from jax import lax
from jax.experimental import pallas as pl
from jax.experimental.pallas import tpu as pltpu

F32, BF16 = jnp.float32, jnp.bfloat16
MESH = pl.DeviceIdType.MESH

HEAD_DIM = 64
DILATIONS = (1, 4, 16)
DIL_HALF = 64
DIL_GROUP_HEADS = 4
DIL_HEADS = 12
NA_HEADS = 8
GRID_W = 64
NA_ROWS = 8
NA_COLS = 16
ROPE_THETA = 10000.0
RMS_EPS = 1e-6
NEG_INF = -1e30
ADAM_LR, ADAM_B1, ADAM_B2, ADAM_EPS, ADAM_WD, ADAM_STEP = 0.001, 0.9, 0.999, 1e-08, 0.01, 10
QK_SCALE = HEAD_DIM ** -0.5

N_CHIPS = 4
LANES = 128
BF16_ROWS = 16
VMEM_LIMIT = 56 * 1024 * 1024
MXU_N = 256
ROWS_NARROW = 4096
ROWS_WIDE = 512
NORM_ROWS = 1024
ROWS_CONTRACTED = 4096
BLOCKS_IN_FLIGHT = 8

_NN = (((1,), (0,)), ((), ()))
_NT = (((1,), (1,)), ((), ()))
_TN = (((0,), (0,)), ((), ()))

HBM = pl.BlockSpec(memory_space=pl.ANY)


def _params(**kw):
    return pltpu.CompilerParams(vmem_limit_bytes=VMEM_LIMIT, **kw)


def _dot(a, b, dims):
    return lax.dot_general(a, b, dims, preferred_element_type=F32)


def _div_tile(n, cap, mult=LANES):
    best = None
    for t in range(mult, min(n, cap) + 1, mult):
        if n % t == 0:
            best = t
    return n if best is None else best


def _stacked(block, index, sel):
    if sel is None:
        return pl.BlockSpec(block, index)
    return pl.BlockSpec((None,) + block, lambda *g: (sel,) + index(*g))


def _mm(a, b, *, mode, out_dtype, tm, tn, tk, name, alpha=1.0, res=None, a_sel=None, b_sel=None, b_k_off=0,
        out_slab=None, out_cols=None, out_col_off=0, out_into=None):
    a2, b2 = a.shape[-2:], b.shape[-2:]
    if mode == "nn":
        (m, k), n = a2, b2[1]
        a_spec = _stacked((tm, tk), lambda i, j, kk: (i, kk), a_sel)
        b_spec = _stacked((tk, tn), lambda i, j, kk: (kk + b_k_off, j), b_sel)
        dims = _NN
    elif mode == "nt":
        (m, k), n = a2, b2[0]
        a_spec = _stacked((tm, tk), lambda i, j, kk: (i, kk), a_sel)
        b_spec = _stacked((tn, tk), lambda i, j, kk: (j, kk + b_k_off), b_sel)
        dims = _NT
    else:
        (k, m), n = a2, b2[1]
        a_spec = _stacked((tk, tm), lambda i, j, kk: (kk, i), a_sel)
        b_spec = _stacked((tk, tn), lambda i, j, kk: (kk + b_k_off, j), b_sel)
        dims = _TN
    assert m % tm == 0 and n % tn == 0 and k % tk == 0, (name, a.shape, b.shape)
    nk = k // tk
    has_res = res is not None
    if out_slab is None:
        o_spec = pl.BlockSpec((tm, tn), lambda i, j, kk: (i, j + out_col_off))
        out_shape = jax.ShapeDtypeStruct((m, n if out_cols is None else out_cols), out_dtype)
    else:
        o_spec = _stacked((tm, tn), lambda i, j, kk: (i, j + out_col_off), out_slab[0])
        out_shape = jax.ShapeDtypeStruct((out_slab[1], m, n if out_cols is None else out_cols), out_dtype)
    r_spec = pl.BlockSpec((tm, tn), lambda i, j, kk: (i, j))
    n_in = 2 + has_res + (out_into is not None)

    def body(*refs):
        a_ref, b_ref = refs[0], refs[1]
        r_ref = refs[2] if has_res else None
        o_ref = refs[n_in]
        p = _dot(a_ref[...], b_ref[...], dims)

        def finish(acc):
            y = acc * alpha if alpha != 1.0 else acc
            if has_res:
                y = y + r_ref[...].astype(F32)
            o_ref[...] = y.astype(o_ref.dtype)

        if nk == 1:
            finish(p)
        else:
            acc_ref = refs[n_in + 1]
            kk = pl.program_id(2)

            @pl.when(kk == 0)
            def _():
                acc_ref[...] = p

            @pl.when(kk > 0)
            def _():
                acc_ref[...] += p

            @pl.when(kk == nk - 1)
            def _():
                finish(acc_ref[...])

    operands = [a, b] + ([res] if has_res else [])
    in_specs = [a_spec, b_spec] + ([r_spec] if has_res else [])
    aliases = {}
    if out_into is not None:
        aliases = {len(operands): 0}
        operands.append(out_into)
        in_specs.append(HBM)
    return pl.pallas_call(
        body, name=name, grid=(m // tm, n // tn, nk), in_specs=in_specs, out_specs=o_spec, out_shape=out_shape,
        scratch_shapes=[pltpu.VMEM((tm, tn), F32)] if nk > 1 else [], input_output_aliases=aliases,
        compiler_params=_params(dimension_semantics=("parallel", "parallel", "arbitrary")),
    )(*operands)


def _mm_swiglu_fwd(h, w_up, *, tm, tn, name):
    m, k = h.shape
    n = w_up.shape[1] // 2
    h_spec = pl.BlockSpec((tm, k), lambda i, j: (i, 0))
    wg_spec = pl.BlockSpec((k, tn), lambda i, j: (0, j))
    wu_spec = pl.BlockSpec((k, tn), lambda i, j: (0, j + n // tn))
    o_spec = pl.BlockSpec((tm, tn), lambda i, j: (i, j))

    def body(h_ref, wg_ref, wu_ref, a_ref, g_ref, u_ref):
        hb = h_ref[...]
        g = _dot(hb, wg_ref[...], _NN)
        u = _dot(hb, wu_ref[...], _NN)
        a_ref[...] = (g * jax.nn.sigmoid(g) * u).astype(BF16)
        g_ref[...] = g.astype(BF16)
        u_ref[...] = u.astype(BF16)

    out = jax.ShapeDtypeStruct((m, n), BF16)
    return pl.pallas_call(
        body, name=name, grid=(m // tm, n // tn), in_specs=[h_spec, wg_spec, wu_spec],
        out_specs=[o_spec] * 3, out_shape=[out] * 3,
        compiler_params=_params(dimension_semantics=("parallel", "parallel")),
    )(h, w_up, w_up)


def _mm_swiglu_bwd(dy, w_down, gate, up, *, alpha, tm, tn, name):
    m, k = dy.shape
    n = w_down.shape[0]
    dy_spec = pl.BlockSpec((tm, k), lambda i, j: (i, 0))
    w_spec = pl.BlockSpec((tn, k), lambda i, j: (j, 0))
    o_spec = pl.BlockSpec((tm, tn), lambda i, j: (i, j))

    def body(dy_ref, w_ref, g_ref, u_ref, dg_ref, du_ref):
        da = _dot(dy_ref[...], w_ref[...], _NT) * alpha
        g = g_ref[...].astype(F32)
        u = u_ref[...].astype(F32)
        sg = jax.nn.sigmoid(g)
        dg_ref[...] = (da * u * (sg * (1.0 + g * (1.0 - sg)))).astype(BF16)
        du_ref[...] = (da * (g * sg)).astype(BF16)

    out = jax.ShapeDtypeStruct((m, n), BF16)
    return pl.pallas_call(
        body, name=name, grid=(m // tm, n // tn), in_specs=[dy_spec, w_spec, o_spec, o_spec],
        out_specs=[o_spec] * 2, out_shape=[out] * 2,
        compiler_params=_params(dimension_semantics=("parallel", "parallel")),
    )(dy, w_down, gate, up)


def _rms_fwd(x, g, *, tt, name):
    t, d = x.shape

    def body(x_ref, g_ref, h_ref):
        xv = x_ref[...]
        rstd = lax.rsqrt(jnp.mean(xv * xv, axis=1, keepdims=True) + RMS_EPS)
        h_ref[...] = (xv * rstd * g_ref[...]).astype(BF16)

    return pl.pallas_call(
        body, name=name, grid=(t // tt,),
        in_specs=[pl.BlockSpec((tt, d), lambda i: (i, 0)), pl.BlockSpec((1, d), lambda i: (0, 0))],
        out_specs=pl.BlockSpec((tt, d), lambda i: (i, 0)), out_shape=jax.ShapeDtypeStruct((t, d), BF16),
        compiler_params=_params(dimension_semantics=("parallel",)),
    )(x, g)


def _rms_bwd(dh, x, g, dres, *, tt, name):
    t, d = x.shape

    def body(dh_ref, x_ref, g_ref, r_ref, dx_ref, dxb_ref, dg_ref):
        xv = x_ref[...]
        rstd = lax.rsqrt(jnp.mean(xv * xv, axis=1, keepdims=True) + RMS_EPS)
        xhat = xv * rstd
        dhv = dh_ref[...]
        dxhat = dhv * g_ref[...]
        dx = r_ref[...] + rstd * (dxhat - xhat * jnp.mean(dxhat * xhat, axis=1, keepdims=True))
        dx_ref[...] = dx
        dxb_ref[...] = dx.astype(BF16)

        @pl.when(pl.program_id(0) == 0)
        def _():
            dg_ref[...] = jnp.zeros_like(dg_ref)

        dg_ref[...] += jnp.sum(dhv * xhat, axis=0, keepdims=True)

    row = pl.BlockSpec((tt, d), lambda i: (i, 0))
    vec = pl.BlockSpec((1, d), lambda i: (0, 0))
    return pl.pallas_call(
        body, name=name, grid=(t // tt,), in_specs=[row, row, vec, row], out_specs=[row, row, vec],
        out_shape=[jax.ShapeDtypeStruct((t, d), F32), jax.ShapeDtypeStruct((t, d), BF16), jax.ShapeDtypeStruct((1, d), F32)],
        compiler_params=_params(dimension_semantics=("arbitrary",)),
    )(dh, x, g, dres)


def _final_loss(x, g, target, *, tt, name):
    t, d = x.shape

    def body(x_ref, g_ref, t_ref, dx_ref, dxb_ref, dg_ref, loss_ref):
        xv = x_ref[...]
        gv = g_ref[...]
        rstd = lax.rsqrt(jnp.mean(xv * xv, axis=1, keepdims=True) + RMS_EPS)
        xhat = xv * rstd
        err = xhat * gv - t_ref[...]
        dy = err * (1.0 / d)
        dxhat = dy * gv
        dx = rstd * (dxhat - xhat * jnp.mean(dxhat * xhat, axis=1, keepdims=True))
        dx_ref[...] = dx
        dxb_ref[...] = dx.astype(BF16)

        @pl.when(pl.program_id(0) == 0)
        def _():
            dg_ref[...] = jnp.zeros_like(dg_ref)
            loss_ref[...] = jnp.zeros_like(loss_ref)

        dg_ref[...] += jnp.sum(dy * xhat, axis=0, keepdims=True)
        part = 0.5 * jnp.sum(jnp.mean(err * err, axis=1, keepdims=True), axis=0, keepdims=True)
        loss_ref[...] += jnp.broadcast_to(part, loss_ref.shape)

    row = pl.BlockSpec((tt, d), lambda i: (i, 0))
    vec = pl.BlockSpec((1, d), lambda i: (0, 0))
    one = pl.BlockSpec((1, LANES), lambda i: (0, 0))
    return pl.pallas_call(
        body, name=name, grid=(t // tt,), in_specs=[row, vec, row], out_specs=[row, row, vec, one],
        out_shape=[jax.ShapeDtypeStruct((t, d), F32), jax.ShapeDtypeStruct((t, d), BF16), jax.ShapeDtypeStruct((1, d), F32),
                   jax.ShapeDtypeStruct((1, LANES), F32)],
        compiler_params=_params(dimension_semantics=("arbitrary",)),
    )(x, g, target)


def _swap_halves(x):
    lane = lax.broadcasted_iota(jnp.int32, x.shape, 1)
    return jnp.where((lane // 32) % 2 == 0, pltpu.roll(x, 96, 1), pltpu.roll(x, 32, 1))


def _rope_tables(s):
    half = HEAD_DIM // 2
    inv_freq = ROPE_THETA ** (-jnp.arange(half, dtype=F32) / half)
    ang = jnp.arange(s).astype(F32)[:, None] * inv_freq[None, :]
    cos, sin = jnp.cos(ang), jnp.sin(ang)
    return jnp.tile(cos, (1, 4)), jnp.concatenate([-sin, sin, -sin, sin], axis=1)


def _dilation_of_tile(p):
    dilated = p < 3 * DIL_HEADS // 2
    g = (p % (DIL_HEADS // 2)) // (DIL_GROUP_HEADS // 2)
    return [(dilated & (g == gi)) | (jnp.logical_not(dilated) if gi == 0 else False) for gi in range(len(DILATIONS))]


def _residue_major(ref, d):
    s = ref.shape[0]
    if d == 1:
        return ref[...]
    return jnp.concatenate([ref[pl.ds(r, s // d, stride=d), :] for r in range(d)], axis=0)


def _split_heads(proj, cos4, sin4, *, n_pairs, rot_pairs, scale_ranges, name):
    b, s, _ = proj.shape

    def body(x_ref, c_ref, s_ref, o_ref):
        p = pl.program_id(1)
        is_q = functools.reduce(jnp.logical_or, [(p >= lo) & (p < hi) for lo, hi in scale_ranges])
        scale = jnp.where(is_q, QK_SCALE, 1.0)

        def put(y):
            o_ref[0] = y[:, :HEAD_DIM].astype(BF16)
            o_ref[1] = y[:, HEAD_DIM:].astype(BF16)

        for d, in_group in zip(DILATIONS, _dilation_of_tile(p)):
            @pl.when(in_group & (p < rot_pairs))
            def _(d=d):
                x = _residue_major(x_ref, d)
                put((x * _residue_major(c_ref, d) + _swap_halves(x) * _residue_major(s_ref, d)) * scale)

            @pl.when(in_group & (p >= rot_pairs))
            def _(d=d):
                put(_residue_major(x_ref, d) * scale)

    tab = pl.BlockSpec((s, LANES), lambda bi, p: (0, 0))
    return pl.pallas_call(
        body, name=name, grid=(b, n_pairs),
        in_specs=[pl.BlockSpec((None, s, LANES), lambda bi, p: (bi, 0, p)), tab, tab],
        out_specs=pl.BlockSpec((None, 2, s, HEAD_DIM), lambda bi, p: (bi, p, 0, 0)),
        out_shape=jax.ShapeDtypeStruct((b, 2 * n_pairs, s, HEAD_DIM), BF16),
        compiler_params=_params(dimension_semantics=("parallel", "parallel")),
    )(proj, cos4, sin4)


def _merge_heads(dheads, cos4, sin4, *, heads_per_row, rot_pairs, scale_pairs, dilated, out_cols, tile_off, into, name):
    b, hpr, r, s, _ = dheads.shape
    n_pairs = hpr * r // 2
    ppr = hpr // 2

    def body(d_ref, c_ref, s_ref, *rest):
        o_ref, t_ref = rest[-2:]
        p = pl.program_id(1)
        scale = jnp.where(p < scale_pairs, QK_SCALE, 1.0)

        def tokens(d):
            dy = jnp.concatenate([d_ref[0], d_ref[1]], axis=1)
            if d == 1:
                return dy
            for res in range(d):
                t_ref[pl.ds(res, s // d, stride=d), :] = dy[res * (s // d):(res + 1) * (s // d), :]
            return t_ref[...]

        groups = _dilation_of_tile(p) if dilated else [p >= 0]
        for d, in_group in zip(DILATIONS, groups):
            @pl.when(in_group & (p < rot_pairs))
            def _(d=d):
                dy = tokens(d)
                o_ref[...] = ((dy * c_ref[...] - _swap_halves(dy) * s_ref[...]) * scale).astype(BF16)

            @pl.when(in_group & (p >= rot_pairs))
            def _(d=d):
                o_ref[...] = (tokens(d) * scale).astype(BF16)

    tab = pl.BlockSpec((s, LANES), lambda bi, p: (0, 0))
    operands = [dheads, cos4, sin4] + ([] if into is None else [into])
    return pl.pallas_call(
        body, name=name, grid=(b, n_pairs),
        in_specs=[pl.BlockSpec((None, 2, None, s, HEAD_DIM), lambda bi, p: (bi, p % ppr, p // ppr, 0, 0)), tab, tab]
        + ([] if into is None else [HBM]),
        out_specs=pl.BlockSpec((None, s, LANES), lambda bi, p: (bi, 0, p + tile_off)),
        out_shape=jax.ShapeDtypeStruct((b, s, out_cols), BF16),
        input_output_aliases={} if into is None else {3: 0},
        scratch_shapes=[pltpu.VMEM((s, LANES), F32)],
        compiler_params=_params(dimension_semantics=("parallel", "parallel")),
    )(*operands)


DIL_TQ = 256


def _dil_block(g, s):
    run = s // DILATIONS[g]
    return DIL_TQ if run <= DIL_TQ else min(run, DIL_TQ + 2 * LANES)


def _dil_keys(g, q0, s):
    run = max(s // DILATIONS[g], DIL_TQ)
    lo = (q0 // run) * run
    return pl.multiple_of(jnp.clip(q0 - LANES, lo, lo + run - _dil_block(g, s)), LANES)


def _dil_band(g, q0, start, shape, s):
    row = q0 + lax.broadcasted_iota(jnp.int32, shape, 0)
    col = start + lax.broadcasted_iota(jnp.int32, shape, 1)
    ok = jnp.abs(row - col) <= DIL_HALF
    run = s // DILATIONS[g]
    if run < DIL_TQ:
        shift = run.bit_length() - 1
        ok = ok & ((row >> shift) == (col >> shift))
    return ok


def _dil_tokens(g, q0, s):
    d = DILATIONS[g]
    if d == 1:
        return [(0, DIL_TQ, pl.ds(q0, DIL_TQ))]
    run = s // d
    n = min(run, DIL_TQ)
    return [(lo, n, pl.ds(((q0 + lo) % run) * d + (q0 + lo) // run, n, stride=d)) for lo in range(0, DIL_TQ, n)]


def _dil_gather(ref, pieces):
    return jnp.concatenate([ref[rows, :] for _, _, rows in pieces], axis=0) if len(pieces) > 1 else ref[pieces[0][2], :]


def _dil_head_spec(part, g, s):
    return pl.BlockSpec((None, None, s, HEAD_DIM), lambda b, j: (b, part * DIL_HEADS + g * DIL_GROUP_HEADS + j, 0, 0))


def _dil_attn_fwd(heads, *, name):
    b, _, s, _ = heads.shape
    n_g = len(DILATIONS)

    def body(*refs):
        qkv = refs[:3 * n_g]
        o_ref, l_ref, og_ref, lg_ref = refs[3 * n_g:]
        for g in range(n_g):
            q_ref, k_ref, v_ref = qkv[3 * g:3 * g + 3]
            width = _dil_block(g, s)

            def step(i, carry, g=g, q_ref=q_ref, k_ref=k_ref, v_ref=v_ref, width=width):
                q0 = pl.multiple_of(i * DIL_TQ, DIL_TQ)
                start = _dil_keys(g, q0, s)
                sc = _dot(q_ref[pl.ds(q0, DIL_TQ), :], k_ref[pl.ds(start, width), :], _NT)
                sc = jnp.where(_dil_band(g, q0, start, sc.shape, s), sc, NEG_INF)
                m = jnp.max(sc, axis=1, keepdims=True)
                p = jnp.exp(sc - m)
                den = jnp.sum(p, axis=1, keepdims=True)
                o = _dot(p.astype(BF16), v_ref[pl.ds(start, width), :], _NN) / den
                lse = m + jnp.log(den)
                for lo, n, rows in _dil_tokens(g, q0, s):
                    og_ref[g, rows, :] = o[lo:lo + n]
                    lg_ref[g, rows, :] = lse[lo:lo + n]
                return carry

            lax.fori_loop(0, s // DIL_TQ, step, 0, unroll=BLOCKS_IN_FLIGHT)
        lses = [lg_ref[g] for g in range(n_g)]
        m = functools.reduce(jnp.maximum, lses)
        ws = [jnp.exp(l - m) for l in lses]
        den = functools.reduce(jnp.add, ws)
        o_ref[...] = (functools.reduce(jnp.add, [w * og_ref[g] for g, w in enumerate(ws)]) / den).astype(o_ref.dtype)
        l_ref[...] = m + jnp.log(den)

    out = pl.BlockSpec((None, None, s, HEAD_DIM), lambda bi, j: (bi, j, 0, 0))
    lse = pl.BlockSpec((None, None, s, 1), lambda bi, j: (bi, j, 0, 0))
    return pl.pallas_call(
        body, name=name, grid=(b, DIL_GROUP_HEADS),
        in_specs=[_dil_head_spec(part, g, s) for g in range(n_g) for part in range(3)],
        out_specs=[out, lse],
        out_shape=[jax.ShapeDtypeStruct((b, DIL_GROUP_HEADS, s, HEAD_DIM), BF16),
                   jax.ShapeDtypeStruct((b, DIL_GROUP_HEADS, s, 1), F32)],
        scratch_shapes=[pltpu.VMEM((n_g, s, HEAD_DIM), F32), pltpu.VMEM((n_g, s, 1), F32)],
        compiler_params=_params(dimension_semantics=("parallel", "parallel")),
    )(*([heads] * (3 * n_g)))


def _dil_attn_bwd(heads, out, lse, dout, *, name):
    b, _, s, _ = heads.shape
    n_g = len(DILATIONS)

    def body(*refs):
        qkv = refs[:3 * n_g]
        o_ref, l_ref, do_ref, d_ref, delta_ref = refs[3 * n_g:]
        d_ref[...] = jnp.zeros_like(d_ref)
        delta_ref[...] = jnp.sum(do_ref[...] * o_ref[...].astype(F32), axis=1, keepdims=True)
        for g in range(n_g):
            q_ref, k_ref, v_ref = qkv[3 * g:3 * g + 3]
            width = _dil_block(g, s)

            def step(i, carry, g=g, q_ref=q_ref, k_ref=k_ref, v_ref=v_ref, width=width):
                q0 = pl.multiple_of(i * DIL_TQ, DIL_TQ)
                start = _dil_keys(g, q0, s)
                win = pl.ds(start, width)
                pieces = _dil_tokens(g, q0, s)
                do_b = _dil_gather(do_ref, pieces).astype(BF16)
                q, k, v = q_ref[pl.ds(q0, DIL_TQ), :], k_ref[win, :], v_ref[win, :]
                sc = _dot(q, k, _NT)
                p = jnp.where(_dil_band(g, q0, start, sc.shape, s), jnp.exp(sc - _dil_gather(l_ref, pieces)), 0.0)
                ds = (p * (_dot(do_b, v, _NT) - _dil_gather(delta_ref, pieces))).astype(BF16)
                d_ref[g, pl.ds(q0, DIL_TQ), :] = _dot(ds, k, _NN)
                d_ref[n_g + g, win, :] += _dot(ds, q, _TN)
                d_ref[2 * n_g + g, win, :] += _dot(p.astype(BF16), do_b, _TN)
                return carry

            lax.fori_loop(0, s // DIL_TQ, step, 0, unroll=BLOCKS_IN_FLIGHT)

    per_head = lambda bi, j: (bi, j, 0, 0)
    return pl.pallas_call(
        body, name=name, grid=(b, DIL_GROUP_HEADS),
        in_specs=[_dil_head_spec(part, g, s) for g in range(n_g) for part in range(3)]
        + [pl.BlockSpec((None, None, s, HEAD_DIM), per_head), pl.BlockSpec((None, None, s, 1), per_head),
           pl.BlockSpec((None, None, s, HEAD_DIM), per_head)],
        out_specs=pl.BlockSpec((None, None, 3 * n_g, s, HEAD_DIM), lambda bi, j: (bi, j, 0, 0, 0)),
        out_shape=jax.ShapeDtypeStruct((b, DIL_GROUP_HEADS, 3 * n_g, s, HEAD_DIM), F32),
        scratch_shapes=[pltpu.VMEM((s, 1), F32)],
        compiler_params=_params(dimension_semantics=("parallel", "parallel")),
    )(*([heads] * (3 * n_g)), out, lse, dout)


NA_BIAS_ROWS = 2 * NA_ROWS - 1
NA_BIAS_COLS = 2 * NA_COLS - 1
NA_BLOCK = 4
NA_SPAN = NA_ROWS + NA_BLOCK - 1
NA_Q = NA_BLOCK * GRID_W
NA_KEYS = NA_SPAN * GRID_W
NA_FORMS = 3


def _na_onehot():
    c = np.arange(GRID_W)[:, None]
    k = np.arange(GRID_W)[None, :]
    lo = np.clip(c - NA_COLS // 2, 0, GRID_W - NA_COLS)
    valid = (k >= lo) & (k < lo + NA_COLS)
    onehot = np.zeros((GRID_W, GRID_W, LANES), np.float32)
    cc, kk = np.nonzero(valid)
    onehot[cc, kk, kk - cc + NA_COLS - 1] = 1.0
    return onehot.reshape(GRID_W * GRID_W, LANES), valid.reshape(1, GRID_W * GRID_W)


def _na_block_rows(n_rows):
    table = np.full((NA_FORMS, NA_BLOCK, NA_SPAN), NA_BIAS_ROWS, np.int64)
    n_blocks = n_rows // NA_BLOCK
    for form, ib in enumerate((0, 1, n_blocks - 1)):
        base = min(max(NA_BLOCK * ib - NA_ROWS // 2, 0), n_rows - NA_SPAN)
        for rl in range(NA_BLOCK):
            r = NA_BLOCK * ib + rl
            row_lo = min(max(r - NA_ROWS // 2, 0), n_rows - NA_ROWS)
            for kl in range(NA_SPAN):
                if row_lo <= base + kl < row_lo + NA_ROWS:
                    table[form, rl, kl] = base + kl - r + NA_ROWS - 1
    return table


def _na_block(ib, n_rows):
    n_blocks = n_rows // NA_BLOCK
    base = jnp.clip(NA_BLOCK * ib - NA_ROWS // 2, 0, n_rows - NA_SPAN)
    return base, jnp.where(ib == 0, 0, jnp.where(ib == n_blocks - 1, 2, 1))


def _na_expand_bias(rel_bias, *, name):
    l, h, nr, nc = rel_bias.shape
    onehot, valid = _na_onehot()
    rb = jnp.pad(rel_bias, ((0, 0), (0, 0), (0, 1), (0, LANES - nc))).reshape(l * h * (nr + 1), LANES)
    live = jnp.asarray(np.tile(np.arange(nr + 1) < nr, l * h).astype(np.float32)[:, None])

    def body(rb_ref, oh_ref, valid_ref, live_ref, e_ref):
        e = lax.dot_general(rb_ref[...], oh_ref[...], _NT, precision=lax.Precision.HIGHEST, preferred_element_type=F32)
        e_ref[...] = jnp.where((valid_ref[...] > 0) & (live_ref[...] > 0), e, NEG_INF)

    e = pl.pallas_call(
        body, name=name, out_shape=jax.ShapeDtypeStruct((l * h * (nr + 1), GRID_W * GRID_W), F32), compiler_params=_params(),
    )(rb, jnp.asarray(onehot), jnp.asarray(valid.astype(np.float32)), live)
    return e.reshape(l, h, nr + 1, GRID_W, GRID_W)


def _na_collapse_bias(de, *, name):
    b, h = de.shape[:2]
    onehot, _ = _na_onehot()
    rows = h * NA_BIAS_ROWS

    def diag(e_ref, oh_ref, o_ref):
        e = e_ref[0]
        for bi in range(1, b):
            e = e + e_ref[bi]
        o_ref[...] = lax.dot_general(e, oh_ref[...], _NN, precision=lax.Precision.HIGHEST, preferred_element_type=F32)

    drb = pl.pallas_call(
        diag, name=name, out_shape=jax.ShapeDtypeStruct((rows, LANES), F32), compiler_params=_params(),
    )(de.reshape(b, rows, GRID_W * GRID_W), jnp.asarray(onehot))
    return drb[:, :NA_BIAS_COLS].reshape(h, NA_BIAS_ROWS, NA_BIAS_COLS)


def _na_tiles(n_rows):
    table = _na_block_rows(n_rows)
    return [(f, rl, kl, int(table[f, rl, kl])) for f in range(NA_FORMS) for rl in range(NA_BLOCK) for kl in range(NA_SPAN)]


def _na_tile(ref, form, rl, kl):
    return ref.at[form, rl * GRID_W:(rl + 1) * GRID_W, kl * GRID_W:(kl + 1) * GRID_W]


def _na_head_spec(part, first, s):
    return pl.BlockSpec((None, None, s, HEAD_DIM), lambda b, h: (b, first + part * NA_HEADS + h, 0, 0))


def _na_attn_fwd(heads, bias, *, first, name):
    b, _, s, _ = heads.shape
    n_rows = s // GRID_W
    tiles = _na_tiles(n_rows)

    def body(q_ref, k_ref, v_ref, e_ref, o_ref, l_ref, b_ref):
        for form, rl, kl, i in tiles:
            _na_tile(b_ref, form, rl, kl)[...] = e_ref[i]

        def step(ib, carry):
            base, form = _na_block(ib, n_rows)
            rows = pl.ds(pl.multiple_of(ib * NA_Q, NA_Q), NA_Q)
            win = pl.ds(pl.multiple_of(base * GRID_W, GRID_W), NA_KEYS)
            sc = _dot(q_ref[rows, :], k_ref[win, :], _NT) + b_ref[form]
            m = jnp.max(sc, axis=1, keepdims=True)
            p = jnp.exp(sc - m)
            den = jnp.sum(p, axis=1, keepdims=True)
            o_ref[rows, :] = (_dot(p.astype(BF16), v_ref[win, :], _NN) / den).astype(o_ref.dtype)
            l_ref[rows, :] = m + jnp.log(den)
            return carry

        lax.fori_loop(0, n_rows // NA_BLOCK, step, 0, unroll=BLOCKS_IN_FLIGHT)

    per_head = lambda bi, h: (bi, h, 0, 0)
    return pl.pallas_call(
        body, name=name, grid=(b, NA_HEADS),
        in_specs=[_na_head_spec(part, first, s) for part in range(3)]
        + [pl.BlockSpec((None, NA_BIAS_ROWS + 1, GRID_W, GRID_W), lambda bi, h: (h, 0, 0, 0))],
        out_specs=[pl.BlockSpec((None, None, s, HEAD_DIM), per_head), pl.BlockSpec((None, None, s, 1), per_head)],
        out_shape=[jax.ShapeDtypeStruct((b, NA_HEADS, s, HEAD_DIM), BF16), jax.ShapeDtypeStruct((b, NA_HEADS, s, 1), F32)],
        scratch_shapes=[pltpu.VMEM((NA_FORMS, NA_Q, NA_KEYS), F32)],
        compiler_params=_params(dimension_semantics=("parallel", "parallel")),
    )(heads, heads, heads, bias)


def _na_attn_bwd(heads, bias, out, lse, dout, *, first, name):
    b, _, s, _ = heads.shape
    n_rows = s // GRID_W
    tiles = _na_tiles(n_rows)

    def body(q_ref, k_ref, v_ref, e_ref, o_ref, l_ref, do_ref, d_ref, de_ref, b_ref, db_ref):
        for form, rl, kl, i in tiles:
            _na_tile(b_ref, form, rl, kl)[...] = e_ref[i]
        d_ref[...] = jnp.zeros_like(d_ref)
        db_ref[...] = jnp.zeros_like(db_ref)

        def step(ib, carry):
            base, form = _na_block(ib, n_rows)
            rows = pl.ds(pl.multiple_of(ib * NA_Q, NA_Q), NA_Q)
            win = pl.ds(pl.multiple_of(base * GRID_W, GRID_W), NA_KEYS)
            q, k, v = q_ref[rows, :], k_ref[win, :], v_ref[win, :]
            do = do_ref[rows, :]
            delta = jnp.sum(do * o_ref[rows, :].astype(F32), axis=1, keepdims=True)
            do_b = do.astype(BF16)
            p = jnp.exp(_dot(q, k, _NT) + b_ref[form] - l_ref[rows, :])
            ds = p * (_dot(do_b, v, _NT) - delta)
            db_ref[form] += ds
            ds_b = ds.astype(BF16)
            d_ref[0, rows, :] = _dot(ds_b, k, _NN)
            d_ref[1, win, :] += _dot(ds_b, q, _TN)
            d_ref[2, win, :] += _dot(p.astype(BF16), do_b, _TN)
            return carry

        lax.fori_loop(0, n_rows // NA_BLOCK, step, 0, unroll=BLOCKS_IN_FLIGHT)
        acc = [None] * NA_BIAS_ROWS
        for form, rl, kl, i in tiles:
            if i < NA_BIAS_ROWS:
                t = _na_tile(db_ref, form, rl, kl)[...]
                acc[i] = t if acc[i] is None else acc[i] + t
        for i in range(NA_BIAS_ROWS):
            de_ref[i] = acc[i]

    per_head = lambda bi, h: (bi, h, 0, 0)
    return pl.pallas_call(
        body, name=name, grid=(b, NA_HEADS),
        in_specs=[_na_head_spec(part, first, s) for part in range(3)]
        + [pl.BlockSpec((None, NA_BIAS_ROWS + 1, GRID_W, GRID_W), lambda bi, h: (h, 0, 0, 0)),
           pl.BlockSpec((None, None, s, HEAD_DIM), per_head), pl.BlockSpec((None, None, s, 1), per_head),
           pl.BlockSpec((None, None, s, HEAD_DIM), per_head)],
        out_specs=[pl.BlockSpec((None, None, 3, s, HEAD_DIM), lambda bi, h: (bi, h, 0, 0, 0)),
                   pl.BlockSpec((None, None, NA_BIAS_ROWS, GRID_W, GRID_W), lambda bi, h: (bi, h, 0, 0, 0))],
        out_shape=[jax.ShapeDtypeStruct((b, NA_HEADS, 3, s, HEAD_DIM), F32),
                   jax.ShapeDtypeStruct((b, NA_HEADS, NA_BIAS_ROWS, GRID_W, GRID_W), F32)],
        scratch_shapes=[pltpu.VMEM((NA_FORMS, NA_Q, NA_KEYS), F32), pltpu.VMEM((NA_FORMS, NA_Q, NA_KEYS), F32)],
        compiler_params=_params(dimension_semantics=("parallel", "parallel")),
    )(heads, heads, heads, bias, out, lse, dout)


GATE_TILE = 256


def _gate_fwd(proj, z, *, gate_col, tt, name):
    _, t, d = z.shape
    nj = d // GATE_TILE
    c0 = gate_col // GATE_TILE

    def body(ga_ref, gb_ref, za_ref, zb_ref, o_ref):
        o_ref[...] = (jax.nn.sigmoid(ga_ref[...]) * za_ref[...] + jax.nn.sigmoid(gb_ref[...]) * zb_ref[...]).astype(BF16)

    return pl.pallas_call(
        body, name=name, grid=(t // tt, nj),
        in_specs=[pl.BlockSpec((tt, GATE_TILE), lambda i, j: (i, c0 + j)),
                  pl.BlockSpec((tt, GATE_TILE), lambda i, j: (i, c0 + nj + j)),
                  pl.BlockSpec((None, tt, GATE_TILE), lambda i, j: (0, i, j)),
                  pl.BlockSpec((None, tt, GATE_TILE), lambda i, j: (1, i, j))],
        out_specs=pl.BlockSpec((tt, GATE_TILE), lambda i, j: (i, j)), out_shape=jax.ShapeDtypeStruct((t, d), BF16),
        compiler_params=_params(dimension_semantics=("parallel", "parallel")),
    )(proj, proj, z, z)


def _gate_bwd(dm, proj, z, *, gate_col, tt, name):
    _, t, d = z.shape
    nj = d // GATE_TILE
    c0 = gate_col // GATE_TILE

    def body(dm_ref, g_ref, z_ref, dz_ref, dg_ref):
        dmv = dm_ref[...]
        sg = jax.nn.sigmoid(g_ref[...])
        dz_ref[...] = (dmv * sg).astype(BF16)
        dg_ref[...] = (dmv * z_ref[...] * sg * (1.0 - sg)).astype(BF16)

    return pl.pallas_call(
        body, name=name, grid=(t // tt, 2 * nj),
        in_specs=[pl.BlockSpec((tt, GATE_TILE), lambda i, j: (i, j % nj)),
                  pl.BlockSpec((tt, GATE_TILE), lambda i, j: (i, c0 + j)),
                  pl.BlockSpec((None, tt, GATE_TILE), lambda i, j: (j // nj, i, j % nj))],
        out_specs=[pl.BlockSpec((None, tt, GATE_TILE), lambda i, j: (j // nj, i, j % nj)),
                   pl.BlockSpec((tt, GATE_TILE), lambda i, j: (i, c0 + j))],
        out_shape=[jax.ShapeDtypeStruct((2, t, d), BF16), jax.ShapeDtypeStruct(proj.shape, BF16)],
        compiler_params=_params(dimension_semantics=("parallel", "parallel")),
    )(dm, proj, z)


def _adamw(w, g, m, v, *, name):
    shape = w.shape
    if w.ndim == 3:
        w2, g2, m2, v2 = w, g, m, v
    else:
        w2, g2, m2, v2 = (t.reshape(1, -1, shape[-1]) for t in (w, g, m, v))
    lead, rows, cols = w2.shape
    tr = rows
    for cand in (512, 256, 128, 64, 32, 16, 8):
        if rows % cand == 0:
            tr = cand
            break

    def body(w_ref, g_ref, m_ref, v_ref, d_ref, nm_ref, nv_ref):
        gv = g_ref[...]
        nm = ADAM_B1 * m_ref[...] + (1.0 - ADAM_B1) * gv
        nv = ADAM_B2 * v_ref[...] + (1.0 - ADAM_B2) * (gv * gv)
        m_hat = nm / (1.0 - ADAM_B1 ** ADAM_STEP)
        v_hat = nv / (1.0 - ADAM_B2 ** ADAM_STEP)
        d_ref[...] = -ADAM_LR * (m_hat / (jnp.sqrt(v_hat) + ADAM_EPS) + ADAM_WD * w_ref[...])
        nm_ref[...] = nm
        nv_ref[...] = nv

    blk = pl.BlockSpec((None, tr, cols), lambda l, i: (l, i, 0))
    out = jax.ShapeDtypeStruct((lead, rows, cols), F32)
    res = pl.pallas_call(
        body, name=name, grid=(lead, rows // tr), in_specs=[blk] * 4, out_specs=[blk] * 3, out_shape=[out] * 3,
        compiler_params=_params(dimension_semantics=("parallel", "parallel")),
    )(w2, g2, m2, v2)
    return tuple(t.reshape(shape) for t in res)


def _my_place():
    return lax.axis_index("x"), lax.axis_index("y"), lax.axis_index("c")


def _other_chips(x, y):
    return [(1 - x, y), (x, 1 - y), (1 - x, 1 - y)]


def _chip_no(chip):
    return 2 * chip[0] + chip[1]


def _window(ref, kind, size, chip, lead):
    if kind == "col":
        return ref.at[(*lead, slice(None), pl.ds(pl.multiple_of(chip * size, LANES), size))]
    if kind == "row":
        return ref.at[(*lead, pl.ds(pl.multiple_of(chip * size, BF16_ROWS), size), slice(None))]
    shard = size + HEAD_DIM
    if kind == "win_main":
        return ref.at[(*lead, slice(None), pl.ds(pl.multiple_of(chip * shard + HEAD_DIM * (chip % 2), LANES), size))]
    assert kind == "win_strad"
    return ref.at[(*lead, slice(None), pl.ds(pl.multiple_of(size + 2 * shard * (chip // 2), LANES), LANES))]


def _full_shape(shard, kind):
    _, k, n = shard.shape
    return {"col": (k, N_CHIPS * n), "row": (N_CHIPS * k, n), "win_main": (k, N_CHIPS * (n + HEAD_DIM)),
            "slot": (N_CHIPS, k, n)}[kind]


def _place_own(shard, kind, layer, *, name):
    _, k, n = shard.shape
    tr = _div_tile(k, 512, BF16_ROWS)
    tc = LANES if kind == "win_main" else n
    mine = 2 * lax.axis_index("x") + lax.axis_index("y")
    row0 = mine * (k // tr) if kind == "row" else 0
    col0 = {"col": mine, "row": 0, "slot": 0, "win_main": (mine * (n + HEAD_DIM) + HEAD_DIM * (mine % 2)) // LANES}[kind]
    scalars = jnp.stack([mine, row0, col0]).astype(jnp.int32)

    def body(s_ref, i_ref, o_ref):
        o_ref[...] = i_ref[...]

    if kind == "slot":
        o_spec = pl.BlockSpec((None, tr, tc), lambda i, j, s: (s[0], i, j))
    else:
        o_spec = pl.BlockSpec((tr, tc), lambda i, j, s: (s[1] + i, s[2] + j))
    return pl.pallas_call(
        body, name=name,
        grid_spec=pltpu.PrefetchScalarGridSpec(
            num_scalar_prefetch=1, grid=(k // tr, n // tc),
            in_specs=[pl.BlockSpec((None, tr, tc), lambda i, j, s: (layer, i, j))], out_specs=o_spec),
        out_shape=jax.ShapeDtypeStruct(_full_shape(shard, kind), shard.dtype),
        compiler_params=_params(dimension_semantics=("parallel", "parallel")),
    )(scalars, shard)


class _GatherPlan:
    def __init__(self, src, dst, shapes, kinds, layer, send_sems, recv_sems):
        self.src, self.dst, self.shapes, self.kinds, self.layer = src, dst, shapes, kinds, layer
        self.send_sems, self.recv_sems = send_sems, recv_sems
        self.x, self.y, self.c = _my_place()
        self.mine = 2 * self.x + self.y
        self.chips = _other_chips(self.x, self.y)
        self.n = len(src)

    def half(self, i, chip, half):
        _, k, n = self.shapes[i]
        kind, dst, hk = self.kinds[i], self.dst[i], k // 2
        if kind == "slot":
            return dst.at[chip, pl.ds(pl.multiple_of(half * hk, BF16_ROWS), hk), :]
        if kind == "row":
            return dst.at[pl.ds(pl.multiple_of(chip * k + half * hk, BF16_ROWS), hk), :]
        col0 = chip * n if kind == "col" else chip * (n + HEAD_DIM) + HEAD_DIM * (chip % 2)
        return dst.at[pl.ds(pl.multiple_of(half * hk, BF16_ROWS), hk), pl.ds(pl.multiple_of(col0, LANES), n)]

    def _copy(self, sem, window, to, source=None):
        return pltpu.make_async_remote_copy(src_ref=window if source is None else source, dst_ref=window,
                                            send_sem=self.send_sems.at[sem], recv_sem=self.recv_sems.at[sem],
                                            device_id=to, device_id_type=MESH)

    def sends(self):
        out = []
        for k, chip in enumerate(self.chips):
            for i in range(self.n):
                hk = self.shapes[i][1] // 2
                mine = self.src[i].at[self.layer, pl.ds(pl.multiple_of(self.c * hk, BF16_ROWS), hk), :]
                out.append(self._copy(3 * i + k, self.half(i, self.mine, self.c), (*chip, self.c), source=mine))
        return out

    def arrivals(self):
        return [self._copy(3 * i + k, self.half(i, _chip_no(chip), self.c), (*chip, self.c))
                for k, chip in enumerate(self.chips) for i in range(self.n)]

    def forwards(self, first_sem):
        sibling = (self.x, self.y, 1 - self.c)
        return [self._copy(first_sem + 3 * i + k, self.half(i, _chip_no(chip), self.c), sibling)
                for k, chip in enumerate(self.chips) for i in range(self.n)]

    def forwarded(self, first_sem):
        sibling = (self.x, self.y, 1 - self.c)
        return [self._copy(first_sem + 3 * i + k, self.half(i, _chip_no(chip), 1 - self.c), sibling)
                for k, chip in enumerate(self.chips) for i in range(self.n)]


IN_HBM = pl.BlockSpec(memory_space=pltpu.HBM)
IN_SEM = pl.BlockSpec(memory_space=pltpu.SEMAPHORE)
DATAFLOW = pltpu.SideEffectType.DATAFLOW_SIDE_EFFECTING


def _gather_layer_start(shards, kinds, fulls, layer, after, *, name):
    n_w = len(shards)
    shapes = [sh.shape for sh in shards]

    def body(*refs):
        plan = _GatherPlan(refs[:n_w], refs[n_w:2 * n_w], shapes, kinds, layer, refs[2 * n_w + 1], refs[2 * n_w + 2])
        for cp in plan.sends():
            cp.start()
        token = refs[-1]
        token[...] = jnp.zeros_like(token)

    operands = [pltpu.with_memory_space_constraint(a, pltpu.HBM) for a in (*shards, *fulls)]
    res = pl.pallas_call(
        body, name=name, in_specs=[IN_HBM] * (2 * n_w) + [pl.BlockSpec(memory_space=pl.ANY)],
        out_specs=(IN_SEM, IN_SEM, *([IN_HBM] * (2 * n_w)), pl.BlockSpec(memory_space=pltpu.VMEM)),
        out_shape=(pltpu.SemaphoreType.DMA((3 * n_w,)), pltpu.SemaphoreType.DMA((3 * n_w,)),
                   *[pltpu.HBM(a.shape, a.dtype) for a in operands], jax.ShapeDtypeStruct((8, LANES), F32)),
        input_output_aliases={i: 2 + i for i in range(2 * n_w)},
        compiler_params=pltpu.CompilerParams(has_side_effects=DATAFLOW),
    )(*operands, after)
    return res[0], res[1], res[2:2 + n_w], res[2 + n_w:2 + 2 * n_w], res[-1]


def _gather_layer_wait(send_sems, recv_sems, shards, fulls, kinds, layer, after, *, name):
    n_w = len(shards)
    shapes = [sh.shape for sh in shards]

    def body(*refs):
        plan = _GatherPlan(refs[:n_w], refs[n_w:2 * n_w], shapes, kinds, layer, refs[2 * n_w], refs[2 * n_w + 1])
        for cp in plan.sends():
            cp.wait_send()
        for cp in plan.arrivals():
            cp.wait_recv()

    res = pl.pallas_call(
        body, name=name, in_specs=[IN_HBM] * (2 * n_w) + [IN_SEM, IN_SEM, pl.BlockSpec(memory_space=pl.ANY)],
        out_specs=[IN_HBM] * (2 * n_w), out_shape=[pltpu.HBM(a.shape, a.dtype) for a in (*shards, *fulls)],
        input_output_aliases={i: i for i in range(2 * n_w)},
        compiler_params=pltpu.CompilerParams(has_side_effects=DATAFLOW),
    )(*shards, *fulls, send_sems, recv_sems, after)
    return res[n_w:]


def _gather_layer_forward(shapes, kinds, fulls, *, name):
    n_w = len(fulls)

    def body(*refs):
        plan = _GatherPlan([None] * n_w, refs[n_w:2 * n_w], shapes, kinds, 0, *refs[2 * n_w:])
        passed = plan.forwards(0)
        for cp in passed:
            cp.start()
        for cp in plan.forwarded(0):
            cp.wait_recv()
        for cp in passed:
            cp.wait_send()

    return pl.pallas_call(
        body, name=name, in_specs=[HBM] * n_w, out_specs=[HBM] * n_w,
        out_shape=[jax.ShapeDtypeStruct(f.shape, f.dtype) for f in fulls],
        input_output_aliases={i: i for i in range(n_w)},
        scratch_shapes=[pltpu.SemaphoreType.DMA((3 * n_w,)), pltpu.SemaphoreType.DMA((3 * n_w,))],
    )(*fulls)


def _on_core(layer):
    return (lax.axis_index("c") == layer).astype(jnp.int32).reshape(1)


N_DEVICES = 2 * N_CHIPS


class _ScatterPlan:
    def __init__(self, src, dst, kinds, sizes, layer, send_sems, recv_sems):
        self.src, self.dst, self.kinds, self.sizes, self.layer = src, dst, kinds, sizes, layer
        self.send_sems, self.recv_sems = send_sems, recv_sems
        self.x, self.y, self.c = _my_place()
        self.mine = 2 * self.x + self.y
        self.chips = _other_chips(self.x, self.y)
        self.n = len(src)

    def _copy(self, i, k, window_of, from_chip, from_core, to):
        return pltpu.make_async_remote_copy(src_ref=_window(self.src[i], self.kinds[i], self.sizes[i], window_of, ()),
                                            dst_ref=self.dst[i].at[2 * from_chip + from_core],
                                            send_sem=self.send_sems.at[4 * i + k],
                                            recv_sem=self.recv_sems.at[2 * (4 * i + k) + from_core],
                                            device_id=to, device_id_type=MESH)

    def to_chips(self):
        return [self._copy(i, k, _chip_no(chip), self.mine, self.c, (*chip, self.layer))
                for k, chip in enumerate(self.chips) for i in range(self.n)]

    def to_sibling(self):
        return [self._copy(i, 3, self.mine, self.mine, self.c, (self.x, self.y, self.layer)) for i in range(self.n)]

    def arrivals(self):
        out = [self._copy(i, k, self.mine, _chip_no(chip), core, (*chip, core))
               for k, chip in enumerate(self.chips) for core in (0, 1) for i in range(self.n)]
        return out + [self._copy(i, 3, self.mine, self.mine, 1 - self.layer, (self.x, self.y, 1 - self.layer))
                      for i in range(self.n)]


def _slab_shape(p, kind, size):
    return (N_DEVICES,) + {"col": (p.shape[0], size), "row": (size, p.shape[1]), "win_main": (p.shape[0], size),
                           "win_strad": (p.shape[0], LANES)}[kind]


def _grads_to_chips_start(pairs, kinds, sizes, layer, *, name):
    n_w = len(pairs)

    def body(*refs):
        plan = _ScatterPlan(refs[:n_w], refs[n_w:2 * n_w], kinds, sizes, layer, refs[2 * n_w], refs[2 * n_w + 1])
        for cp in plan.to_chips():
            cp.start()

        @pl.when(plan.c != layer)
        def _():
            for cp in plan.to_sibling():
                cp.start()

        token = refs[-1]
        token[...] = jnp.zeros_like(token)

    slabs = [lax.empty(_slab_shape(p, kind, size), p.dtype) for p, kind, size in zip(pairs, kinds, sizes)]
    operands = [pltpu.with_memory_space_constraint(a, pltpu.HBM) for a in (*pairs, *slabs)]
    res = pl.pallas_call(
        body, name=name, in_specs=[IN_HBM] * (2 * n_w),
        out_specs=(IN_SEM, IN_SEM, *([IN_HBM] * (2 * n_w)), pl.BlockSpec(memory_space=pltpu.VMEM)),
        out_shape=(pltpu.SemaphoreType.DMA((4 * n_w,)), pltpu.SemaphoreType.DMA((8 * n_w,)),
                   *[pltpu.HBM(a.shape, a.dtype) for a in operands], jax.ShapeDtypeStruct((8, LANES), F32)),
        input_output_aliases={i: 2 + i for i in range(2 * n_w)},
        compiler_params=pltpu.CompilerParams(has_side_effects=DATAFLOW),
    )(*operands)
    return res[0], res[1], res[2:2 + n_w], res[2 + n_w:2 + 2 * n_w], res[-1]


def _grads_to_chips_wait(send_sems, recv_sems, pairs, slabs, kinds, sizes, layer, after, *, name):
    n_w = len(pairs)

    def body(*refs):
        plan = _ScatterPlan(refs[:n_w], refs[n_w:2 * n_w], kinds, sizes, layer, refs[2 * n_w], refs[2 * n_w + 1])
        for cp in plan.to_chips():
            cp.wait_send()

        @pl.when(plan.c != layer)
        def _():
            for cp in plan.to_sibling():
                cp.wait_send()

        @pl.when(plan.c == layer)
        def _():
            for cp in plan.arrivals():
                cp.wait_recv()

    res = pl.pallas_call(
        body, name=name, in_specs=[IN_HBM] * (2 * n_w) + [IN_SEM, IN_SEM, pl.BlockSpec(memory_space=pl.ANY)],
        out_specs=[IN_HBM] * (2 * n_w), out_shape=[pltpu.HBM(a.shape, a.dtype) for a in (*pairs, *slabs)],
        input_output_aliases={i: i for i in range(2 * n_w)},
        compiler_params=pltpu.CompilerParams(has_side_effects=DATAFLOW),
    )(*pairs, *slabs, send_sems, recv_sems, after)
    return res[:n_w], res[n_w:]


def _sum_slabs(slabs, pair, kind, size, layer, into, *, name):
    n_s, k, n = slabs.shape
    tr = _div_tile(k, 512, BF16_ROWS)
    tc = n if kind in ("col", "row") else LANES
    x, y, _ = _my_place()
    mine = 2 * x + y
    shard = size + HEAD_DIM
    row0 = mine * (k // tr) if kind == "row" else 0
    col0 = {"col": mine, "row": 0, "win_main": (mine * shard + HEAD_DIM * (mine % 2)) // LANES,
            "win_strad": (size + 2 * shard * (mine // 2)) // LANES}[kind]
    on = _on_core(layer)[0]
    scalars = jnp.stack([2 * mine + layer, row0 * on, col0 * on, on]).astype(jnp.int32)

    def body(s_ref, slab_ref, own_ref, *rest):
        o_ref = rest[-1]
        me = s_ref[0]

        @pl.when(s_ref[3] == 1)
        def _():
            acc = jnp.zeros(o_ref.shape, F32)
            for i in range(n_s):
                acc = acc + jnp.where(me == i, own_ref[...], slab_ref[i]).astype(F32)
            o_ref[...] = acc

    operands = [scalars, slabs, pair] + ([] if into is None else [into])
    return pl.pallas_call(
        body, name=name,
        grid_spec=pltpu.PrefetchScalarGridSpec(
            num_scalar_prefetch=1, grid=(k // tr, n // tc),
            in_specs=[pl.BlockSpec((n_s, tr, tc), lambda i, j, s: (0, i * s[3], j * s[3])),
                      pl.BlockSpec((tr, tc), lambda i, j, s: (s[1] + i * s[3], s[2] + j * s[3]))]
            + ([] if into is None else [HBM]),
            out_specs=pl.BlockSpec((None, tr, tc), lambda i, j, s: (layer, i * s[3], j * s[3]))),
        out_shape=jax.ShapeDtypeStruct((2, k, n), F32),
        input_output_aliases={} if into is None else {3: 0},
        compiler_params=_params(dimension_semantics=("arbitrary", "arbitrary")),
    )(*operands)


def _exchange_layers(bufs, *, name):
    n_w = len(bufs)

    def body(*refs):
        dst = refs[n_w:2 * n_w]
        send_sems, recv_sems = refs[2 * n_w:]
        x, y, c = _my_place()

        def copy(i, layer):
            return pltpu.make_async_remote_copy(src_ref=dst[i].at[layer], dst_ref=dst[i].at[layer], send_sem=send_sems.at[i],
                                                recv_sem=recv_sems.at[i], device_id=(x, y, 1 - c), device_id_type=MESH)

        sends = [copy(i, c) for i in range(n_w)]
        for cp in sends:
            cp.start()
        for i in range(n_w):
            copy(i, 1 - c).wait_recv()
        for cp in sends:
            cp.wait_send()

    return pl.pallas_call(
        body, name=name, in_specs=[HBM] * n_w, out_specs=[HBM] * n_w,
        out_shape=[jax.ShapeDtypeStruct(b.shape, b.dtype) for b in bufs],
        input_output_aliases={i: i for i in range(n_w)},
        scratch_shapes=[pltpu.SemaphoreType.DMA((n_w,)), pltpu.SemaphoreType.DMA((n_w,))],
    )(*bufs)


def _all_sum_small(v, *, name):
    r = v.shape[0]
    relations = [(dx, dy, dc) for dx in (0, 1) for dy in (0, 1) for dc in (0, 1)][1:]

    def body(v_ref, o_ref, buf, send_sems, recv_sems):
        x, y, c = _my_place()
        me = 4 * x + 2 * y + c
        buf[me] = v_ref[...]
        peers = [(x + dx - 2 * x * dx, y + dy - 2 * y * dy, c + dc - 2 * c * dc) for dx, dy, dc in relations]

        def copy(k, slot):
            return pltpu.make_async_remote_copy(src_ref=v_ref, dst_ref=buf.at[slot], send_sem=send_sems.at[k],
                                                recv_sem=recv_sems.at[k], device_id=peers[k], device_id_type=MESH)

        sends = [copy(k, me) for k in range(len(relations))]
        for cp in sends:
            cp.start()
        for k, (px, py, pc) in enumerate(peers):
            copy(k, 4 * px + 2 * py + pc).wait_recv()
        for cp in sends:
            cp.wait_send()
        acc = buf[0]
        for i in range(1, 8):
            acc = acc + buf[i]
        o_ref[...] = acc

    vm = pl.BlockSpec(memory_space=pltpu.VMEM)
    return pl.pallas_call(
        body, name=name, in_specs=[vm], out_specs=vm, out_shape=jax.ShapeDtypeStruct((r, LANES), F32),
        scratch_shapes=[pltpu.VMEM((8, r, LANES), F32), pltpu.SemaphoreType.DMA((7,)), pltpu.SemaphoreType.DMA((7,))],
    )(v)


SHARDED = (("ffn1_w_up", "col"), ("ffn1_w_down", "row"), ("w_in", "win"), ("w_branch_a", "col"),
           ("w_branch_b", "col"), ("w_out", "row"), ("ffn2_w_up", "col"), ("ffn2_w_down", "row"))
REPLICATED = ("ffn1_norm", "mix_norm", "na_rel_bias", "ffn2_norm", "final_norm")


def _weight_pieces(w):
    even = lax.axis_index("y") == 0
    shards, kinds, names = [], [], []
    for name, kind in SHARDED:
        wb = w[name].astype(BF16)
        if kind == "win":
            main = wb.shape[-1] - HEAD_DIM
            assert main % LANES == 0
            zeros = jnp.zeros(wb.shape[:-1] + (HEAD_DIM,), BF16)
            shards += [jnp.where(even, wb[..., :main], wb[..., HEAD_DIM:]),
                       jnp.where(even, jnp.concatenate([wb[..., main:], zeros], -1),
                                 jnp.concatenate([zeros, wb[..., :HEAD_DIM]], -1))]
            kinds += ["win_main", "slot"]
            names += [name, name + "_strad"]
        else:
            shards.append(wb)
            kinds.append(kind)
            names.append(name)
    return names, kinds, shards


def _finish_w_in(full):
    full = dict(full)
    strad = full.pop("w_in_strad")
    main = full["w_in"].shape[1] // N_CHIPS - HEAD_DIM
    for i in range(N_CHIPS // 2):
        lo = main + 2 * (main + HEAD_DIM) * i
        full["w_in"] = full["w_in"].at[:, lo:lo + LANES].set(strad[2 * i] + strad[2 * i + 1])
    return full


def _scatter_pieces(shards):
    names, kinds, sizes, srcs = [], [], [], []
    for name, kind in SHARDED:
        shp = shards[name].shape
        if kind == "win":
            names += [name, name + "_strad"]
            kinds += ["win_main", "win_strad"]
            sizes += [shp[2] - HEAD_DIM] * 2
            srcs += [name, name]
        else:
            names.append(name)
            kinds.append(kind)
            sizes.append(shp[1] if kind == "row" else shp[2])
            srcs.append(name)
    return names, kinds, sizes, srcs


def _finish_weight_grads(reduced, names, tag):
    out = dict(zip(names, _exchange_layers(reduced, name=f"{tag}_layers")))
    if "w_in_strad" in out:
        strad = out.pop("w_in_strad")
        even = lax.axis_index("y") == 0
        out["w_in"] = jnp.where(even, jnp.concatenate([out["w_in"], strad[..., :HEAD_DIM]], -1),
                                jnp.concatenate([strad[..., HEAD_DIM:], out["w_in"]], -1))
    return out


class _Grads:
    def __init__(self):
        self.arrays = {}

    def put(self, weight, layer, a, b, *, cols=None, col_off=0, **kw):
        self.arrays[weight, layer] = _mm(a, b, mode="tn", out_dtype=BF16, out_cols=cols, out_col_off=col_off,
                                         out_into=self.arrays.get((weight, layer)), **kw)


def _ffn_fwd(x, h, w_up, w_down, tag):
    t, d = x.shape
    f = w_down.shape[0]
    a, gate, up = _mm_swiglu_fwd(h, w_up, tm=_div_tile(t, ROWS_NARROW, 8), tn=MXU_N, name=f"{tag}_up")
    x_out = _mm(a, w_down, mode="nn", out_dtype=F32, tm=_div_tile(t, ROWS_WIDE, 8), tn=d, tk=f, alpha=0.5, res=x, name=f"{tag}_down")
    return x_out, (x, h, a, gate, up)


def _ffn_bwd(dx, dxb, saved, norm_g, w_up, w_down, layer, grads, wname, tag, scatter):
    x, h, a, gate, up = saved
    t, d = x.shape
    f = w_down.shape[0]
    tn = _div_tile(f, 1408)
    grads.put(f"{wname}_w_down", layer, a, dxb, tm=tn, tn=d, tk=ROWS_CONTRACTED, alpha=0.5, name=f"{tag}_dwd")
    d_gate, d_up = _mm_swiglu_bwd(dxb, w_down, gate, up, alpha=0.5, tm=_div_tile(t, ROWS_NARROW, 8), tn=MXU_N, name=f"{tag}_da")
    grads.put(f"{wname}_w_up", layer, h, d_gate, cols=2 * f, tm=d, tn=tn, tk=ROWS_CONTRACTED, name=f"{tag}_dwg")
    grads.put(f"{wname}_w_up", layer, h, d_up, cols=2 * f, col_off=f // tn, tm=d, tn=tn, tk=ROWS_CONTRACTED, name=f"{tag}_dwu")
    started = scatter(layer, [f"{wname}_w_up", f"{wname}_w_down"])
    dh = _mm(d_gate, w_up, mode="nt", out_dtype=F32, tm=_div_tile(t, ROWS_WIDE, 8), tn=d, tk=f, name=f"{tag}_dh1")
    dh = _mm(d_up, w_up, mode="nt", out_dtype=F32, tm=_div_tile(t, ROWS_WIDE, 8), tn=d, tk=f, b_k_off=1, res=dh, name=f"{tag}_dh2")
    return _rms_bwd(dh, x, norm_g + started, dx, tt=NORM_ROWS, name=f"{tag}_dnorm")


def _to_heads(y, b, n_heads):
    t, w = y.shape
    return y.reshape(b, t // b, n_heads, HEAD_DIM).transpose(0, 2, 1, 3)


def _from_heads(y):
    b, n, s, hd = y.shape
    return y.transpose(0, 2, 1, 3).reshape(b * s, n * hd)


N_QKV = 3 * (DIL_HEADS + NA_HEADS) * HEAD_DIM


def _mixer_fwd(x, b, norm_g, full, bias, tabs, tag):
    t, d = x.shape
    s = t // b
    n_in = full["w_in"].shape[1]
    h = _rms_fwd(x, norm_g, tt=NORM_ROWS, name=f"{tag}_norm")
    proj = _mm(h, full["w_in"], mode="nn", out_dtype=F32, tm=_div_tile(t, ROWS_NARROW, 8), tn=MXU_N, tk=d, name=f"{tag}_in")
    heads = _split_heads(proj.reshape(b, s, -1), *tabs, n_pairs=N_QKV // LANES, rot_pairs=DIL_HEADS,
                         scale_ranges=((0, DIL_HEADS // 2), (3 * DIL_HEADS // 2, (3 * DIL_HEADS + NA_HEADS) // 2)),
                         name=f"{tag}_heads")
    ya, lse_a = _dil_attn_fwd(heads, name=f"{tag}_dil")
    yb, lse_b = _na_attn_fwd(heads, bias, first=3 * DIL_HEADS, name=f"{tag}_na")
    ya2, yb2 = _from_heads(ya), _from_heads(yb)
    z = _mm(ya2, full["w_branch_a"], mode="nn", out_dtype=F32, tm=_div_tile(t, ROWS_NARROW, 8), tn=MXU_N, tk=ya2.shape[1],
            out_slab=(0, 2), name=f"{tag}_za")
    z = _mm(yb2, full["w_branch_b"], mode="nn", out_dtype=F32, tm=_div_tile(t, ROWS_NARROW, 8), tn=MXU_N, tk=yb2.shape[1],
            out_slab=(1, 2), out_into=z, name=f"{tag}_zb")
    merged = _gate_fwd(proj, z, gate_col=N_QKV, tt=1024, name=f"{tag}_gate")
    x_out = _mm(merged, full["w_out"], mode="nn", out_dtype=F32, tm=_div_tile(t, ROWS_NARROW, 8), tn=MXU_N, tk=d, res=x, name=f"{tag}_out")
    return x_out, (x, h, proj, heads, ya, lse_a, yb, lse_b, ya2, yb2, z, merged)


def _mixer_bwd(dx, dob, b, saved, norm_g, full, layer, bias, tabs, grads, tag, scatter):
    x, h, proj, heads, ya, lse_a, yb, lse_b, ya2, yb2, z, merged = saved
    t, d = x.shape
    s = t // b
    n_in = full["w_in"].shape[1]
    grads.put("w_out", layer, merged, dob, tm=d, tn=d, tk=ROWS_CONTRACTED, name=f"{tag}_dwo")
    dm = _mm(dob, full["w_out"], mode="nt", out_dtype=F32, tm=_div_tile(t, ROWS_NARROW, 8), tn=MXU_N, tk=d, name=f"{tag}_dm")
    dz, dproj = _gate_bwd(dm, proj, z, gate_col=N_QKV, tt=1024, name=f"{tag}_dgate")
    grads.put("w_branch_a", layer, ya2, dz, b_sel=0, tm=ya2.shape[1], tn=d, tk=ROWS_CONTRACTED, name=f"{tag}_dwa")
    grads.put("w_branch_b", layer, yb2, dz, b_sel=1, tm=yb2.shape[1], tn=d, tk=ROWS_CONTRACTED, name=f"{tag}_dwb")
    started = scatter(layer, ["w_out", "w_branch_a", "w_branch_b"])
    dya = _mm(dz, full["w_branch_a"], mode="nt", out_dtype=F32, tm=_div_tile(t, ROWS_NARROW, 8), tn=MXU_N, tk=d, a_sel=0, name=f"{tag}_dya")
    dyb = _mm(dz, full["w_branch_b"], mode="nt", out_dtype=F32, tm=_div_tile(t, ROWS_NARROW, 8), tn=MXU_N, tk=d, a_sel=1, name=f"{tag}_dyb")
    d_dil = _dil_attn_bwd(heads, ya, lse_a, _to_heads(dya, b, DIL_GROUP_HEADS), name=f"{tag}_ddil")
    d_na, d_bias = _na_attn_bwd(heads, bias, yb, lse_b, _to_heads(dyb, b, NA_HEADS), first=3 * DIL_HEADS, name=f"{tag}_dna")
    dproj = _merge_heads(d_dil, *tabs, heads_per_row=DIL_GROUP_HEADS, rot_pairs=DIL_HEADS, scale_pairs=DIL_HEADS // 2,
                         dilated=True, out_cols=n_in, tile_off=0, into=dproj.reshape(b, s, n_in), name=f"{tag}_dheads_a")
    dproj = _merge_heads(d_na, *tabs, heads_per_row=NA_HEADS, rot_pairs=0, scale_pairs=NA_HEADS // 2, dilated=False,
                         out_cols=n_in, tile_off=3 * DIL_HEADS // 2, into=dproj, name=f"{tag}_dheads_b").reshape(t, n_in)
    grads.put("w_in", layer, h, dproj, tm=_div_tile(d, 512), tn=_div_tile(n_in, 2944), tk=ROWS_CONTRACTED // 2, name=f"{tag}_dwin")
    started = started + scatter(layer, ["w_in"])
    dh = _mm(dproj, full["w_in"], mode="nt", out_dtype=F32, tm=_div_tile(t, ROWS_WIDE, 8), tn=d, tk=_div_tile(n_in, 2944), name=f"{tag}_dh")
    dx_in, dxb_in, d_norm = _rms_bwd(dh, x, norm_g + started, dx, tt=NORM_ROWS, name=f"{tag}_dnorm")
    d_rb = _na_collapse_bias(d_bias, name=f"{tag}_dbias")
    return dx_in, dxb_in, d_norm, d_rb


def kernel(x, ffn1_norm, ffn1_w_up, ffn1_w_down, mix_norm, w_in, na_rel_bias, w_branch_a, w_branch_b, w_out, ffn2_norm, ffn2_w_up, ffn2_w_down, final_norm, loss_target, m_ffn1_norm, m_ffn1_w_up, m_ffn1_w_down, m_mix_norm, m_w_in, m_na_rel_bias, m_w_branch_a, m_w_branch_b, m_w_out, m_ffn2_norm, m_ffn2_w_up, m_ffn2_w_down, m_final_norm, v_ffn1_norm, v_ffn1_w_up, v_ffn1_w_down, v_mix_norm, v_w_in, v_na_rel_bias, v_w_branch_a, v_w_branch_b, v_w_out, v_ffn2_norm, v_ffn2_w_up, v_ffn2_w_down, v_final_norm):
    w = dict(ffn1_norm=ffn1_norm, ffn1_w_up=ffn1_w_up, ffn1_w_down=ffn1_w_down, mix_norm=mix_norm, w_in=w_in,
             na_rel_bias=na_rel_bias, w_branch_a=w_branch_a, w_branch_b=w_branch_b, w_out=w_out, ffn2_norm=ffn2_norm,
             ffn2_w_up=ffn2_w_up, ffn2_w_down=ffn2_w_down, final_norm=final_norm)
    mom = dict(ffn1_norm=m_ffn1_norm, ffn1_w_up=m_ffn1_w_up, ffn1_w_down=m_ffn1_w_down, mix_norm=m_mix_norm, w_in=m_w_in,
               na_rel_bias=m_na_rel_bias, w_branch_a=m_w_branch_a, w_branch_b=m_w_branch_b, w_out=m_w_out,
               ffn2_norm=m_ffn2_norm, ffn2_w_up=m_ffn2_w_up, ffn2_w_down=m_ffn2_w_down, final_norm=m_final_norm)
    var = dict(ffn1_norm=v_ffn1_norm, ffn1_w_up=v_ffn1_w_up, ffn1_w_down=v_ffn1_w_down, mix_norm=v_mix_norm, w_in=v_w_in,
               na_rel_bias=v_na_rel_bias, w_branch_a=v_w_branch_a, w_branch_b=v_w_branch_b, w_out=v_w_out,
               ffn2_norm=v_ffn2_norm, ffn2_w_up=v_ffn2_w_up, ffn2_w_down=v_ffn2_w_down, final_norm=v_final_norm)
    b, s, d = x.shape
    t = b * s
    depth = ffn1_norm.shape[0]
    assert depth == 2, "core c of a chip sends / reduces layer c"
    shards = {name: w[name] for name, _ in SHARDED}

    names, kinds, pieces = _weight_pieces(w)
    by_layer = [[p[l:l + 1] for p in pieces] for l in range(depth)]
    own = [[_place_own(p, kind, 0, name=f"own{l}_{nm}") for nm, kind, p in zip(names, kinds, by_layer[l])] for l in range(depth)]
    full = [{}, {}]

    def gather_start(layer, group, after, tag):
        idx = [i for i, nm in enumerate(names) if nm in group]
        pick = lambda seq: [seq[i] for i in idx]
        *state, token = _gather_layer_start(pick(by_layer[layer]), pick(kinds), pick(own[layer]), 0, after, name=f"{tag}_start")
        return (layer, idx, tag, state), token[:1, :1]

    def gather_finish(started, after):
        layer, idx, tag, state = started
        pick = lambda seq: [seq[i] for i in idx]
        landed = _gather_layer_wait(*state, pick(kinds), 0, after, name=f"{tag}_wait")
        done = _gather_layer_forward([by_layer[layer][i].shape for i in idx], pick(kinds), landed, name=f"{tag}_forward")
        full[layer].update(zip(pick(names), done))
        return done[0]

    ffn1, mixer, ffn2 = names[:2], names[2:7], names[7:]
    assert mixer[0] == "w_in" and ffn2[0] == "ffn2_w_up", names
    xc = x.reshape(t, d)
    l0_ffn1, token_ffn1 = gather_start(0, ffn1, xc, "gather_l0_ffn1")
    tabs = _rope_tables(s)
    bias = _na_expand_bias(na_rel_bias, name="na_bias")

    saved = []
    h = _rms_fwd(xc, ffn1_norm[:1] + token_ffn1, tt=NORM_ROWS, name="l0_ffn1_norm")
    landed = gather_finish(l0_ffn1, h)
    l0_mixer, token_mixer = gather_start(0, mixer, landed, "gather_l0_mixer")
    xc, s1 = _ffn_fwd(xc, h + token_mixer.astype(BF16), full[0]["ffn1_w_up"], full[0]["ffn1_w_down"], "l0_ffn1")
    landed = gather_finish(l0_mixer, xc)
    full[0] = _finish_w_in(full[0])
    l0_ffn2, token_ffn2 = gather_start(0, ffn2, landed, "gather_l0_ffn2")
    layer1, token_layer1 = gather_start(1, names, landed, "gather_l1")
    xc, s2 = _mixer_fwd(xc, b, mix_norm[:1] + token_ffn2 + token_layer1, full[0], bias[0], tabs, "l0_mix")
    gather_finish(l0_ffn2, xc)
    xc, s3 = _ffn_fwd(xc, _rms_fwd(xc, ffn2_norm[:1], tt=NORM_ROWS, name="l0_ffn2_norm"), full[0]["ffn2_w_up"], full[0]["ffn2_w_down"],
                      "l0_ffn2")
    saved.append((s1, s2, s3))
    gather_finish(layer1, xc)
    full[1] = _finish_w_in(full[1])
    for l in range(1, depth):
        xc, s1 = _ffn_fwd(xc, _rms_fwd(xc, ffn1_norm[l:l + 1], tt=NORM_ROWS, name=f"l{l}_ffn1_norm"), full[l]["ffn1_w_up"],
                          full[l]["ffn1_w_down"], f"l{l}_ffn1")
        xc, s2 = _mixer_fwd(xc, b, mix_norm[l:l + 1], full[l], bias[l], tabs, f"l{l}_mix")
        xc, s3 = _ffn_fwd(xc, _rms_fwd(xc, ffn2_norm[l:l + 1], tt=NORM_ROWS, name=f"l{l}_ffn2_norm"), full[l]["ffn2_w_up"],
                          full[l]["ffn2_w_down"], f"l{l}_ffn2")
        saved.append((s1, s2, s3))

    dx, dxb, d_final, loss_part = _final_loss(xc, final_norm.reshape(1, d), loss_target.reshape(t, d), tt=NORM_ROWS, name="final_loss")
    grads = _Grads()
    piece_names, piece_kinds, piece_sizes, piece_srcs = _scatter_pieces(shards)
    scattered = []

    def scatter(layer, weights):
        tag = f"grads{layer}_{weights[0]}"
        idx = [i for i, src in enumerate(piece_srcs) if src in weights]
        pick = lambda seq: [seq[i] for i in idx]
        *state, token = _grads_to_chips_start([grads.arrays[src, layer] for src in pick(piece_srcs)], pick(piece_kinds),
                                              pick(piece_sizes), layer, name=f"{tag}_to_chips_start")
        scattered.append((layer, idx, state))
        return token[:1, :1]
    small = {name: [None] * depth for name in REPLICATED[:-1]}
    for l in reversed(range(depth)):
        s1, s2, s3 = saved[l]
        dx, dxb, small["ffn2_norm"][l] = _ffn_bwd(dx, dxb, s3, ffn2_norm[l:l + 1], full[l]["ffn2_w_up"], full[l]["ffn2_w_down"],
                                                  l, grads, "ffn2", f"l{l}_ffn2", scatter)
        dx, dxb, small["mix_norm"][l], small["na_rel_bias"][l] = _mixer_bwd(
            dx, dxb, b, s2, mix_norm[l:l + 1], full[l], l, bias[l], tabs, grads, f"l{l}_mix", scatter)
        dx, dxb, small["ffn1_norm"][l] = _ffn_bwd(dx, dxb, s1, ffn1_norm[l:l + 1], full[l]["ffn1_w_up"], full[l]["ffn1_w_down"],
                                                  l, grads, "ffn1", f"l{l}_ffn1", scatter)
    grad_x = dx.reshape(b, s, d)
    reduced = [None] * len(piece_names)

    def arrive(group, after):
        layer, idx, state = group
        state = _grads_to_chips_wait(*state, [piece_kinds[i] for i in idx], [piece_sizes[i] for i in idx], layer, after,
                                     name=f"grads{layer}_{piece_names[idx[0]]}_to_chips_wait")
        for i, p, sl in zip(idx, *state):
            reduced[i] = _sum_slabs(sl, p, piece_kinds[i], piece_sizes[i], layer, reduced[i],
                                    name=f"grads{layer}_sum_{piece_names[i]}")
        return idx

    for group in scattered[:-1]:
        arrive(group, dx)
    late = scattered[-1][1]
    early = [i for i in range(len(piece_names)) if i not in late]
    g_out = _finish_weight_grads([reduced[i] for i in early], [piece_names[i] for i in early], "grads_early")

    parts = [jnp.stack(small[name]).reshape(-1) for name in REPLICATED[:-1]] + [d_final.reshape(-1), loss_part[0, :1]]
    sizes = [v.shape[0] for v in parts]
    flat = jnp.concatenate(parts)
    flat = jnp.pad(flat, (0, -flat.shape[0] % (8 * LANES)))
    small_sum = _all_sum_small(flat.reshape(-1, LANES), name="small_all_sum").reshape(-1)
    off = 0
    for name, n in zip(REPLICATED, sizes[:-1]):
        g_out[name] = small_sum[off:off + n].reshape(w[name].shape)
        off += n
    loss = small_sum[off]

    names = list(w)
    delta, new_m, new_v = {}, {}, {}
    for name in [n for n in names if n in g_out]:
        delta[name], new_m[name], new_v[name] = _adamw(w[name], g_out[name], mom[name], var[name], name=f"adamw_{name}")
    arrive(scattered[-1], delta["w_in"])
    g_out.update(_finish_weight_grads([reduced[i] for i in late], [piece_names[i] for i in late], "grads_late"))
    for name in [n for n in names if n not in delta]:
        delta[name], new_m[name], new_v[name] = _adamw(w[name], g_out[name], mom[name], var[name], name=f"adamw_{name}")
    return (loss, grad_x, *[g_out[n] for n in names], *[delta[n] for n in names], *[new_m[n] for n in names],
            *[new_v[n] for n in names])
```

```python
import functools

import numpy as np
import jax
import jax.numpy as jnp
from jax import lax
from jax.experimental import pallas as pl
from jax.experimental.pallas import tpu as pltpu

F32, BF16 = jnp.float32, jnp.bfloat16
MESH = pl.DeviceIdType.MESH

HEAD_DIM = 64
DILATIONS = (1, 4, 16)
DIL_HALF = 64
DIL_GROUP_HEADS = 4
DIL_HEADS = 12
NA_HEADS = 8
GRID_W = 64
NA_ROWS = 8
NA_COLS = 16
ROPE_THETA = 10000.0
RMS_EPS = 1e-6
NEG_INF = -1e30
ADAM_LR, ADAM_B1, ADAM_B2, ADAM_EPS, ADAM_WD, ADAM_STEP = 0.001, 0.9, 0.999, 1e-08, 0.01, 10
QK_SCALE = HEAD_DIM ** -0.5

N_CHIPS = 4
LANES = 128
BF16_ROWS = 16
VMEM_LIMIT = 56 * 1024 * 1024
MXU_N = 256
ROWS_NARROW = 4096
ROWS_WIDE = 1024
NORM_ROWS = 1024
GATE_ROWS = 2048
ROWS_CONTRACTED = 4096
BLOCKS_IN_FLIGHT = 8

_NN = (((1,), (0,)), ((), ()))
_NT = (((1,), (1,)), ((), ()))
_TN = (((0,), (0,)), ((), ()))

HBM = pl.BlockSpec(memory_space=pl.ANY)


def _params(**kw):
    return pltpu.CompilerParams(vmem_limit_bytes=VMEM_LIMIT, **kw)


def _dot(a, b, dims):
    return lax.dot_general(a, b, dims, preferred_element_type=F32)


def _div_tile(n, cap, mult=LANES):
    best = None
    for t in range(mult, min(n, cap) + 1, mult):
        if n % t == 0:
            best = t
    return n if best is None else best


def _stacked(block, index, sel):
    if sel is None:
        return pl.BlockSpec(block, index)
    return pl.BlockSpec((None,) + block, lambda *g: (sel,) + index(*g))


def _mm(a, b, *, mode, out_dtype, tm, tn, tk, name, alpha=1.0, res=None, a_sel=None, b_sel=None, b_k_off=0,
        out_slab=None, out_cols=None, out_col_off=0, out_into=None):
    a2, b2 = a.shape[-2:], b.shape[-2:]
    if mode == "nn":
        (m, k), n = a2, b2[1]
        a_spec = _stacked((tm, tk), lambda i, j, kk: (i, kk), a_sel)
        b_spec = _stacked((tk, tn), lambda i, j, kk: (kk + b_k_off, j), b_sel)
        dims = _NN
    elif mode == "nt":
        (m, k), n = a2, b2[0]
        a_spec = _stacked((tm, tk), lambda i, j, kk: (i, kk), a_sel)
        b_spec = _stacked((tn, tk), lambda i, j, kk: (j, kk + b_k_off), b_sel)
        dims = _NT
    else:
        (k, m), n = a2, b2[1]
        a_spec = _stacked((tk, tm), lambda i, j, kk: (kk, i), a_sel)
        b_spec = _stacked((tk, tn), lambda i, j, kk: (kk + b_k_off, j), b_sel)
        dims = _TN
    assert m % tm == 0 and n % tn == 0 and k % tk == 0, (name, a.shape, b.shape)
    nk = k // tk
    has_res = res is not None
    if out_slab is None:
        o_spec = pl.BlockSpec((tm, tn), lambda i, j, kk: (i, j + out_col_off))
        out_shape = jax.ShapeDtypeStruct((m, n if out_cols is None else out_cols), out_dtype)
    else:
        o_spec = _stacked((tm, tn), lambda i, j, kk: (i, j + out_col_off), out_slab[0])
        out_shape = jax.ShapeDtypeStruct((out_slab[1], m, n if out_cols is None else out_cols), out_dtype)
    r_spec = pl.BlockSpec((tm, tn), lambda i, j, kk: (i, j))
    n_in = 2 + has_res + (out_into is not None)

    def body(*refs):
        a_ref, b_ref = refs[0], refs[1]
        r_ref = refs[2] if has_res else None
        o_ref = refs[n_in]
        p = _dot(a_ref[...], b_ref[...], dims)

        def finish(acc):
            y = acc * alpha if alpha != 1.0 else acc
            if has_res:
                y = y + r_ref[...].astype(F32)
            o_ref[...] = y.astype(o_ref.dtype)

        if nk == 1:
            finish(p)
        else:
            acc_ref = refs[n_in + 1]
            kk = pl.program_id(2)

            @pl.when(kk == 0)
            def _():
                acc_ref[...] = p

            @pl.when(kk > 0)
            def _():
                acc_ref[...] += p

            @pl.when(kk == nk - 1)
            def _():
                finish(acc_ref[...])

    operands = [a, b] + ([res] if has_res else [])
    in_specs = [a_spec, b_spec] + ([r_spec] if has_res else [])
    aliases = {}
    if out_into is not None:
        aliases = {len(operands): 0}
        operands.append(out_into)
        in_specs.append(HBM)
    return pl.pallas_call(
        body, name=name, grid=(m // tm, n // tn, nk), in_specs=in_specs, out_specs=o_spec, out_shape=out_shape,
        scratch_shapes=[pltpu.VMEM((tm, tn), F32)] if nk > 1 else [], input_output_aliases=aliases,
        compiler_params=_params(dimension_semantics=("parallel", "parallel", "arbitrary")),
    )(*operands)


def _mm_swiglu_fwd(h, w_up, *, tm, tn, name):
    m, k = h.shape
    n = w_up.shape[1] // 2
    h_spec = pl.BlockSpec((tm, k), lambda i, j: (i, 0))
    wg_spec = pl.BlockSpec((k, tn), lambda i, j: (0, j))
    wu_spec = pl.BlockSpec((k, tn), lambda i, j: (0, j + n // tn))
    o_spec = pl.BlockSpec((tm, tn), lambda i, j: (i, j))

    def body(h_ref, wg_ref, wu_ref, a_ref, g_ref, u_ref):
        hb = h_ref[...]
        g = _dot(hb, wg_ref[...], _NN)
        u = _dot(hb, wu_ref[...], _NN)
        a_ref[...] = (g * jax.nn.sigmoid(g) * u).astype(BF16)
        g_ref[...] = g.astype(BF16)
        u_ref[...] = u.astype(BF16)

    out = jax.ShapeDtypeStruct((m, n), BF16)
    return pl.pallas_call(
        body, name=name, grid=(m // tm, n // tn), in_specs=[h_spec, wg_spec, wu_spec],
        out_specs=[o_spec] * 3, out_shape=[out] * 3,
        compiler_params=_params(dimension_semantics=("parallel", "parallel")),
    )(h, w_up, w_up)


def _mm_swiglu_bwd(dy, w_down, gate, up, *, alpha, tm, tn, name):
    m, k = dy.shape
    n = w_down.shape[0]
    dy_spec = pl.BlockSpec((tm, k), lambda i, j: (i, 0))
    w_spec = pl.BlockSpec((tn, k), lambda i, j: (j, 0))
    o_spec = pl.BlockSpec((tm, tn), lambda i, j: (i, j))

    def body(dy_ref, w_ref, g_ref, u_ref, dg_ref, du_ref):
        da = _dot(dy_ref[...], w_ref[...], _NT) * alpha
        g = g_ref[...].astype(F32)
        u = u_ref[...].astype(F32)
        sg = jax.nn.sigmoid(g)
        dg_ref[...] = (da * u * (sg * (1.0 + g * (1.0 - sg)))).astype(BF16)
        du_ref[...] = (da * (g * sg)).astype(BF16)

    out = jax.ShapeDtypeStruct((m, n), BF16)
    return pl.pallas_call(
        body, name=name, grid=(m // tm, n // tn), in_specs=[dy_spec, w_spec, o_spec, o_spec],
        out_specs=[o_spec] * 2, out_shape=[out] * 2,
        compiler_params=_params(dimension_semantics=("parallel", "parallel")),
    )(dy, w_down, gate, up)


def _rms_fwd(x, g, *, tt, name):
    t, d = x.shape

    def body(x_ref, g_ref, h_ref):
        xv = x_ref[...]
        rstd = lax.rsqrt(jnp.mean(xv * xv, axis=1, keepdims=True) + RMS_EPS)
        h_ref[...] = (xv * rstd * g_ref[...]).astype(BF16)

    return pl.pallas_call(
        body, name=name, grid=(t // tt,),
        in_specs=[pl.BlockSpec((tt, d), lambda i: (i, 0)), pl.BlockSpec((1, d), lambda i: (0, 0))],
        out_specs=pl.BlockSpec((tt, d), lambda i: (i, 0)), out_shape=jax.ShapeDtypeStruct((t, d), BF16),
        compiler_params=_params(dimension_semantics=("parallel",)),
    )(x, g)


def _rms_bwd(dh, x, g, dres, *, tt, name):
    t, d = x.shape

    def body(dh_ref, x_ref, g_ref, r_ref, dx_ref, dxb_ref, dg_ref):
        xv = x_ref[...]
        rstd = lax.rsqrt(jnp.mean(xv * xv, axis=1, keepdims=True) + RMS_EPS)
        xhat = xv * rstd
        dhv = dh_ref[...]
        dxhat = dhv * g_ref[...]
        dx = r_ref[...] + rstd * (dxhat - xhat * jnp.mean(dxhat * xhat, axis=1, keepdims=True))
        dx_ref[...] = dx
        dxb_ref[...] = dx.astype(BF16)

        @pl.when(pl.program_id(0) == 0)
        def _():
            dg_ref[...] = jnp.zeros_like(dg_ref)

        dg_ref[...] += jnp.sum(dhv * xhat, axis=0, keepdims=True)

    row = pl.BlockSpec((tt, d), lambda i: (i, 0))
    vec = pl.BlockSpec((1, d), lambda i: (0, 0))
    return pl.pallas_call(
        body, name=name, grid=(t // tt,), in_specs=[row, row, vec, row], out_specs=[row, row, vec],
        out_shape=[jax.ShapeDtypeStruct((t, d), F32), jax.ShapeDtypeStruct((t, d), BF16), jax.ShapeDtypeStruct((1, d), F32)],
        compiler_params=_params(dimension_semantics=("arbitrary",)),
    )(dh, x, g, dres)


def _final_loss(x, g, target, *, tt, name):
    t, d = x.shape

    def body(x_ref, g_ref, t_ref, dx_ref, dxb_ref, dg_ref, loss_ref):
        xv = x_ref[...]
        gv = g_ref[...]
        rstd = lax.rsqrt(jnp.mean(xv * xv, axis=1, keepdims=True) + RMS_EPS)
        xhat = xv * rstd
        err = xhat * gv - t_ref[...]
        dy = err * (1.0 / d)
        dxhat = dy * gv
        dx = rstd * (dxhat - xhat * jnp.mean(dxhat * xhat, axis=1, keepdims=True))
        dx_ref[...] = dx
        dxb_ref[...] = dx.astype(BF16)

        @pl.when(pl.program_id(0) == 0)
        def _():
            dg_ref[...] = jnp.zeros_like(dg_ref)
            loss_ref[...] = jnp.zeros_like(loss_ref)

        dg_ref[...] += jnp.sum(dy * xhat, axis=0, keepdims=True)
        part = 0.5 * jnp.sum(jnp.mean(err * err, axis=1, keepdims=True), axis=0, keepdims=True)
        loss_ref[...] += jnp.broadcast_to(part, loss_ref.shape)

    row = pl.BlockSpec((tt, d), lambda i: (i, 0))
    vec = pl.BlockSpec((1, d), lambda i: (0, 0))
    one = pl.BlockSpec((1, LANES), lambda i: (0, 0))
    return pl.pallas_call(
        body, name=name, grid=(t // tt,), in_specs=[row, vec, row], out_specs=[row, row, vec, one],
        out_shape=[jax.ShapeDtypeStruct((t, d), F32), jax.ShapeDtypeStruct((t, d), BF16), jax.ShapeDtypeStruct((1, d), F32),
                   jax.ShapeDtypeStruct((1, LANES), F32)],
        compiler_params=_params(dimension_semantics=("arbitrary",)),
    )(x, g, target)


def _swap_halves(x):
    lane = lax.broadcasted_iota(jnp.int32, x.shape, 1)
    return jnp.where((lane // 32) % 2 == 0, pltpu.roll(x, 96, 1), pltpu.roll(x, 32, 1))


def _rope_tables(s):
    half = HEAD_DIM // 2
    inv_freq = ROPE_THETA ** (-jnp.arange(half, dtype=F32) / half)
    ang = jnp.arange(s).astype(F32)[:, None] * inv_freq[None, :]
    cos, sin = jnp.cos(ang), jnp.sin(ang)
    return jnp.tile(cos, (1, 4)), jnp.concatenate([-sin, sin, -sin, sin], axis=1)


def _dilation_of_tile(p):
    dilated = p < 3 * DIL_HEADS // 2
    g = (p % (DIL_HEADS // 2)) // (DIL_GROUP_HEADS // 2)
    return [(dilated & (g == gi)) | (jnp.logical_not(dilated) if gi == 0 else False) for gi in range(len(DILATIONS))]


def _residue_major(ref, d):
    s = ref.shape[0]
    if d == 1:
        return ref[...]
    return jnp.concatenate([ref[pl.ds(r, s // d, stride=d), :] for r in range(d)], axis=0)


def _split_heads(proj, cos4, sin4, *, n_pairs, rot_pairs, scale_ranges, name):
    b, s, _ = proj.shape

    def body(x_ref, c_ref, s_ref, o_ref):
        p = pl.program_id(1)
        is_q = functools.reduce(jnp.logical_or, [(p >= lo) & (p < hi) for lo, hi in scale_ranges])
        scale = jnp.where(is_q, QK_SCALE, 1.0)

        def put(y):
            o_ref[0] = y[:, :HEAD_DIM].astype(BF16)
            o_ref[1] = y[:, HEAD_DIM:].astype(BF16)

        for d, in_group in zip(DILATIONS, _dilation_of_tile(p)):
            @pl.when(in_group & (p < rot_pairs))
            def _(d=d):
                x = _residue_major(x_ref, d)
                put((x * _residue_major(c_ref, d) + _swap_halves(x) * _residue_major(s_ref, d)) * scale)

            @pl.when(in_group & (p >= rot_pairs))
            def _(d=d):
                put(_residue_major(x_ref, d) * scale)

    tab = pl.BlockSpec((s, LANES), lambda bi, p: (0, 0))
    return pl.pallas_call(
        body, name=name, grid=(b, n_pairs),
        in_specs=[pl.BlockSpec((None, s, LANES), lambda bi, p: (bi, 0, p)), tab, tab],
        out_specs=pl.BlockSpec((None, 2, s, HEAD_DIM), lambda bi, p: (bi, p, 0, 0)),
        out_shape=jax.ShapeDtypeStruct((b, 2 * n_pairs, s, HEAD_DIM), BF16),
        compiler_params=_params(dimension_semantics=("parallel", "parallel")),
    )(proj, cos4, sin4)


def _merge_heads(dheads, cos4, sin4, *, heads_per_row, rot_pairs, scale_pairs, dilated, out_cols, tile_off, into, name):
    b, hpr, r, s, _ = dheads.shape
    n_pairs = hpr * r // 2
    ppr = hpr // 2

    def body(d_ref, c_ref, s_ref, *rest):
        o_ref, t_ref = rest[-2:]
        p = pl.program_id(1)
        scale = jnp.where(p < scale_pairs, QK_SCALE, 1.0)

        def tokens(d):
            dy = jnp.concatenate([d_ref[0], d_ref[1]], axis=1)
            if d == 1:
                return dy
            for res in range(d):
                t_ref[pl.ds(res, s // d, stride=d), :] = dy[res * (s // d):(res + 1) * (s // d), :]
            return t_ref[...]

        groups = _dilation_of_tile(p) if dilated else [p >= 0]
        for d, in_group in zip(DILATIONS, groups):
            @pl.when(in_group & (p < rot_pairs))
            def _(d=d):
                dy = tokens(d)
                o_ref[...] = ((dy * c_ref[...] - _swap_halves(dy) * s_ref[...]) * scale).astype(BF16)

            @pl.when(in_group & (p >= rot_pairs))
            def _(d=d):
                o_ref[...] = (tokens(d) * scale).astype(BF16)

    tab = pl.BlockSpec((s, LANES), lambda bi, p: (0, 0))
    operands = [dheads, cos4, sin4] + ([] if into is None else [into])
    return pl.pallas_call(
        body, name=name, grid=(b, n_pairs),
        in_specs=[pl.BlockSpec((None, 2, None, s, HEAD_DIM), lambda bi, p: (bi, p % ppr, p // ppr, 0, 0)), tab, tab]
        + ([] if into is None else [HBM]),
        out_specs=pl.BlockSpec((None, s, LANES), lambda bi, p: (bi, 0, p + tile_off)),
        out_shape=jax.ShapeDtypeStruct((b, s, out_cols), BF16),
        input_output_aliases={} if into is None else {3: 0},
        scratch_shapes=[pltpu.VMEM((s, LANES), F32)],
        compiler_params=_params(dimension_semantics=("parallel", "parallel")),
    )(*operands)


DIL_TQ = 256


def _dil_block(g, s):
    run = s // DILATIONS[g]
    return DIL_TQ if run <= DIL_TQ else min(run, DIL_TQ + 2 * LANES)


def _dil_keys(g, q0, s):
    run = max(s // DILATIONS[g], DIL_TQ)
    lo = (q0 // run) * run
    return pl.multiple_of(jnp.clip(q0 - LANES, lo, lo + run - _dil_block(g, s)), LANES)


def _dil_band(g, q0, start, shape, s):
    row = q0 + lax.broadcasted_iota(jnp.int32, shape, 0)
    col = start + lax.broadcasted_iota(jnp.int32, shape, 1)
    ok = jnp.abs(row - col) <= DIL_HALF
    run = s // DILATIONS[g]
    if run < DIL_TQ:
        shift = run.bit_length() - 1
        ok = ok & ((row >> shift) == (col >> shift))
    return ok


def _dil_tokens(g, q0, s):
    d = DILATIONS[g]
    if d == 1:
        return [(0, DIL_TQ, pl.ds(q0, DIL_TQ))]
    run = s // d
    n = min(run, DIL_TQ)
    return [(lo, n, pl.ds(((q0 + lo) % run) * d + (q0 + lo) // run, n, stride=d)) for lo in range(0, DIL_TQ, n)]


def _dil_gather(ref, pieces):
    return jnp.concatenate([ref[rows, :] for _, _, rows in pieces], axis=0) if len(pieces) > 1 else ref[pieces[0][2], :]


def _dil_head_spec(part, g, s):
    return pl.BlockSpec((None, None, s, HEAD_DIM), lambda b, j: (b, part * DIL_HEADS + g * DIL_GROUP_HEADS + j, 0, 0))


def _dil_attn_fwd(heads, *, name):
    b, _, s, _ = heads.shape
    n_g = len(DILATIONS)

    def body(*refs):
        qkv = refs[:3 * n_g]
        o_ref, l_ref, og_ref, lg_ref = refs[3 * n_g:]
        for g in range(n_g):
            q_ref, k_ref, v_ref = qkv[3 * g:3 * g + 3]
            width = _dil_block(g, s)

            def step(i, carry, g=g, q_ref=q_ref, k_ref=k_ref, v_ref=v_ref, width=width):
                q0 = pl.multiple_of(i * DIL_TQ, DIL_TQ)
                start = _dil_keys(g, q0, s)
                sc = _dot(q_ref[pl.ds(q0, DIL_TQ), :], k_ref[pl.ds(start, width), :], _NT)
                sc = jnp.where(_dil_band(g, q0, start, sc.shape, s), sc, NEG_INF)
                m = jnp.max(sc, axis=1, keepdims=True)
                p = jnp.exp(sc - m)
                den = jnp.sum(p, axis=1, keepdims=True)
                o = _dot(p.astype(BF16), v_ref[pl.ds(start, width), :], _NN) / den
                lse = m + jnp.log(den)
                for lo, n, rows in _dil_tokens(g, q0, s):
                    og_ref[g, rows, :] = o[lo:lo + n]
                    lg_ref[g, rows, :] = lse[lo:lo + n]
                return carry

            lax.fori_loop(0, s // DIL_TQ, step, 0, unroll=BLOCKS_IN_FLIGHT)
        lses = [lg_ref[g] for g in range(n_g)]
        m = functools.reduce(jnp.maximum, lses)
        ws = [jnp.exp(l - m) for l in lses]
        den = functools.reduce(jnp.add, ws)
        o_ref[...] = (functools.reduce(jnp.add, [w * og_ref[g] for g, w in enumerate(ws)]) / den).astype(o_ref.dtype)
        l_ref[...] = m + jnp.log(den)

    out = pl.BlockSpec((None, None, s, HEAD_DIM), lambda bi, j: (bi, j, 0, 0))
    lse = pl.BlockSpec((None, None, s, 1), lambda bi, j: (bi, j, 0, 0))
    return pl.pallas_call(
        body, name=name, grid=(b, DIL_GROUP_HEADS),
        in_specs=[_dil_head_spec(part, g, s) for g in range(n_g) for part in range(3)],
        out_specs=[out, lse],
        out_shape=[jax.ShapeDtypeStruct((b, DIL_GROUP_HEADS, s, HEAD_DIM), BF16),
                   jax.ShapeDtypeStruct((b, DIL_GROUP_HEADS, s, 1), F32)],
        scratch_shapes=[pltpu.VMEM((n_g, s, HEAD_DIM), F32), pltpu.VMEM((n_g, s, 1), F32)],
        compiler_params=_params(dimension_semantics=("parallel", "parallel")),
    )(*([heads] * (3 * n_g)))


def _dil_attn_bwd(heads, out, lse, dout, *, name):
    b, _, s, _ = heads.shape
    n_g = len(DILATIONS)

    def body(*refs):
        qkv = refs[:3 * n_g]
        o_ref, l_ref, do_ref, d_ref, delta_ref = refs[3 * n_g:]
        d_ref[...] = jnp.zeros_like(d_ref)
        delta_ref[...] = jnp.sum(do_ref[...] * o_ref[...].astype(F32), axis=1, keepdims=True)
        for g in range(n_g):
            q_ref, k_ref, v_ref = qkv[3 * g:3 * g + 3]
            width = _dil_block(g, s)

            def step(i, carry, g=g, q_ref=q_ref, k_ref=k_ref, v_ref=v_ref, width=width):
                q0 = pl.multiple_of(i * DIL_TQ, DIL_TQ)
                start = _dil_keys(g, q0, s)
                win = pl.ds(start, width)
                pieces = _dil_tokens(g, q0, s)
                do_b = _dil_gather(do_ref, pieces).astype(BF16)
                q, k, v = q_ref[pl.ds(q0, DIL_TQ), :], k_ref[win, :], v_ref[win, :]
                sc = _dot(q, k, _NT)
                p = jnp.where(_dil_band(g, q0, start, sc.shape, s), jnp.exp(sc - _dil_gather(l_ref, pieces)), 0.0)
                ds = (p * (_dot(do_b, v, _NT) - _dil_gather(delta_ref, pieces))).astype(BF16)
                d_ref[g, pl.ds(q0, DIL_TQ), :] = _dot(ds, k, _NN)
                d_ref[n_g + g, win, :] += _dot(ds, q, _TN)
                d_ref[2 * n_g + g, win, :] += _dot(p.astype(BF16), do_b, _TN)
                return carry

            lax.fori_loop(0, s // DIL_TQ, step, 0, unroll=BLOCKS_IN_FLIGHT)

    per_head = lambda bi, j: (bi, j, 0, 0)
    return pl.pallas_call(
        body, name=name, grid=(b, DIL_GROUP_HEADS),
        in_specs=[_dil_head_spec(part, g, s) for g in range(n_g) for part in range(3)]
        + [pl.BlockSpec((None, None, s, HEAD_DIM), per_head), pl.BlockSpec((None, None, s, 1), per_head),
           pl.BlockSpec((None, None, s, HEAD_DIM), per_head)],
        out_specs=pl.BlockSpec((None, None, 3 * n_g, s, HEAD_DIM), lambda bi, j: (bi, j, 0, 0, 0)),
        out_shape=jax.ShapeDtypeStruct((b, DIL_GROUP_HEADS, 3 * n_g, s, HEAD_DIM), F32),
        scratch_shapes=[pltpu.VMEM((s, 1), F32)],
        compiler_params=_params(dimension_semantics=("parallel", "parallel")),
    )(*([heads] * (3 * n_g)), out, lse, dout)


NA_BIAS_ROWS = 2 * NA_ROWS - 1
NA_BIAS_COLS = 2 * NA_COLS - 1
NA_BLOCK = 4
NA_SPAN = NA_ROWS + NA_BLOCK - 1
NA_Q = NA_BLOCK * GRID_W
NA_KEYS = NA_SPAN * GRID_W
NA_FORMS = 3


def _na_onehot():
    c = np.arange(GRID_W)[:, None]
    k = np.arange(GRID_W)[None, :]
    lo = np.clip(c - NA_COLS // 2, 0, GRID_W - NA_COLS)
    valid = (k >= lo) & (k < lo + NA_COLS)
    onehot = np.zeros((GRID_W, GRID_W, LANES), np.float32)
    cc, kk = np.nonzero(valid)
    onehot[cc, kk, kk - cc + NA_COLS - 1] = 1.0
    return onehot.reshape(GRID_W * GRID_W, LANES), valid.reshape(1, GRID_W * GRID_W)


def _na_block_rows(n_rows):
    table = np.full((NA_FORMS, NA_BLOCK, NA_SPAN), NA_BIAS_ROWS, np.int64)
    n_blocks = n_rows // NA_BLOCK
    for form, ib in enumerate((0, 1, n_blocks - 1)):
        base = min(max(NA_BLOCK * ib - NA_ROWS // 2, 0), n_rows - NA_SPAN)
        for rl in range(NA_BLOCK):
            r = NA_BLOCK * ib + rl
            row_lo = min(max(r - NA_ROWS // 2, 0), n_rows - NA_ROWS)
            for kl in range(NA_SPAN):
                if row_lo <= base + kl < row_lo + NA_ROWS:
                    table[form, rl, kl] = base + kl - r + NA_ROWS - 1
    return table


def _na_block(ib, n_rows):
    n_blocks = n_rows // NA_BLOCK
    base = jnp.clip(NA_BLOCK * ib - NA_ROWS // 2, 0, n_rows - NA_SPAN)
    return base, jnp.where(ib == 0, 0, jnp.where(ib == n_blocks - 1, 2, 1))


def _na_expand_bias(rel_bias, *, name):
    l, h, nr, nc = rel_bias.shape
    onehot, valid = _na_onehot()
    rb = jnp.pad(rel_bias, ((0, 0), (0, 0), (0, 1), (0, LANES - nc))).reshape(l * h * (nr + 1), LANES)
    live = jnp.asarray(np.tile(np.arange(nr + 1) < nr, l * h).astype(np.float32)[:, None])

    def body(rb_ref, oh_ref, valid_ref, live_ref, e_ref):
        e = lax.dot_general(rb_ref[...], oh_ref[...], _NT, precision=lax.Precision.HIGHEST, preferred_element_type=F32)
        e_ref[...] = jnp.where((valid_ref[...] > 0) & (live_ref[...] > 0), e, NEG_INF)

    e = pl.pallas_call(
        body, name=name, out_shape=jax.ShapeDtypeStruct((l * h * (nr + 1), GRID_W * GRID_W), F32), compiler_params=_params(),
    )(rb, jnp.asarray(onehot), jnp.asarray(valid.astype(np.float32)), live)
    return e.reshape(l, h, nr + 1, GRID_W, GRID_W)


def _na_collapse_bias(de, *, name):
    b, h = de.shape[:2]
    onehot, _ = _na_onehot()
    rows = h * NA_BIAS_ROWS

    def diag(e_ref, oh_ref, o_ref):
        e = e_ref[0]
        for bi in range(1, b):
            e = e + e_ref[bi]
        o_ref[...] = lax.dot_general(e, oh_ref[...], _NN, precision=lax.Precision.HIGHEST, preferred_element_type=F32)

    drb = pl.pallas_call(
        diag, name=name, out_shape=jax.ShapeDtypeStruct((rows, LANES), F32), compiler_params=_params(),
    )(de.reshape(b, rows, GRID_W * GRID_W), jnp.asarray(onehot))
    return drb[:, :NA_BIAS_COLS].reshape(h, NA_BIAS_ROWS, NA_BIAS_COLS)


def _na_tiles(n_rows):
    table = _na_block_rows(n_rows)
    return [(f, rl, kl, int(table[f, rl, kl])) for f in range(NA_FORMS) for rl in range(NA_BLOCK) for kl in range(NA_SPAN)]


def _na_tile(ref, form, rl, kl):
    return ref.at[form, rl * GRID_W:(rl + 1) * GRID_W, kl * GRID_W:(kl + 1) * GRID_W]


def _na_head_spec(part, first, s):
    return pl.BlockSpec((None, None, s, HEAD_DIM), lambda b, h: (b, first + part * NA_HEADS + h, 0, 0))


def _na_attn_fwd(heads, bias, *, first, name):
    b, _, s, _ = heads.shape
    n_rows = s // GRID_W
    tiles = _na_tiles(n_rows)

    def body(q_ref, k_ref, v_ref, e_ref, o_ref, l_ref, b_ref):
        for form, rl, kl, i in tiles:
            _na_tile(b_ref, form, rl, kl)[...] = e_ref[i]

        def step(ib, carry):
            base, form = _na_block(ib, n_rows)
            rows = pl.ds(pl.multiple_of(ib * NA_Q, NA_Q), NA_Q)
            win = pl.ds(pl.multiple_of(base * GRID_W, GRID_W), NA_KEYS)
            sc = _dot(q_ref[rows, :], k_ref[win, :], _NT) + b_ref[form]
            m = jnp.max(sc, axis=1, keepdims=True)
            p = jnp.exp(sc - m)
            den = jnp.sum(p, axis=1, keepdims=True)
            o_ref[rows, :] = (_dot(p.astype(BF16), v_ref[win, :], _NN) / den).astype(o_ref.dtype)
            l_ref[rows, :] = m + jnp.log(den)
            return carry

        lax.fori_loop(0, n_rows // NA_BLOCK, step, 0, unroll=BLOCKS_IN_FLIGHT)

    per_head = lambda bi, h: (bi, h, 0, 0)
    return pl.pallas_call(
        body, name=name, grid=(b, NA_HEADS),
        in_specs=[_na_head_spec(part, first, s) for part in range(3)]
        + [pl.BlockSpec((None, NA_BIAS_ROWS + 1, GRID_W, GRID_W), lambda bi, h: (h, 0, 0, 0))],
        out_specs=[pl.BlockSpec((None, None, s, HEAD_DIM), per_head), pl.BlockSpec((None, None, s, 1), per_head)],
        out_shape=[jax.ShapeDtypeStruct((b, NA_HEADS, s, HEAD_DIM), BF16), jax.ShapeDtypeStruct((b, NA_HEADS, s, 1), F32)],
        scratch_shapes=[pltpu.VMEM((NA_FORMS, NA_Q, NA_KEYS), F32)],
        compiler_params=_params(dimension_semantics=("parallel", "parallel")),
    )(heads, heads, heads, bias)


def _na_attn_bwd(heads, bias, out, lse, dout, *, first, name):
    b, _, s, _ = heads.shape
    n_rows = s // GRID_W
    tiles = _na_tiles(n_rows)

    def body(q_ref, k_ref, v_ref, e_ref, o_ref, l_ref, do_ref, d_ref, de_ref, b_ref, db_ref):
        for form, rl, kl, i in tiles:
            _na_tile(b_ref, form, rl, kl)[...] = e_ref[i]
        d_ref[...] = jnp.zeros_like(d_ref)
        db_ref[...] = jnp.zeros_like(db_ref)

        def step(ib, carry):
            base, form = _na_block(ib, n_rows)
            rows = pl.ds(pl.multiple_of(ib * NA_Q, NA_Q), NA_Q)
            win = pl.ds(pl.multiple_of(base * GRID_W, GRID_W), NA_KEYS)
            q, k, v = q_ref[rows, :], k_ref[win, :], v_ref[win, :]
            do = do_ref[rows, :]
            delta = jnp.sum(do * o_ref[rows, :].astype(F32), axis=1, keepdims=True)
            do_b = do.astype(BF16)
            p = jnp.exp(_dot(q, k, _NT) + b_ref[form] - l_ref[rows, :])
            ds = p * (_dot(do_b, v, _NT) - delta)
            db_ref[form] += ds
            ds_b = ds.astype(BF16)
            d_ref[0, rows, :] = _dot(ds_b, k, _NN)
            d_ref[1, win, :] += _dot(ds_b, q, _TN)
            d_ref[2, win, :] += _dot(p.astype(BF16), do_b, _TN)
            return carry

        lax.fori_loop(0, n_rows // NA_BLOCK, step, 0, unroll=BLOCKS_IN_FLIGHT)
        acc = [None] * NA_BIAS_ROWS
        for form, rl, kl, i in tiles:
            if i < NA_BIAS_ROWS:
                t = _na_tile(db_ref, form, rl, kl)[...]
                acc[i] = t if acc[i] is None else acc[i] + t
        for i in range(NA_BIAS_ROWS):
            de_ref[i] = acc[i]

    per_head = lambda bi, h: (bi, h, 0, 0)
    return pl.pallas_call(
        body, name=name, grid=(b, NA_HEADS),
        in_specs=[_na_head_spec(part, first, s) for part in range(3)]
        + [pl.BlockSpec((None, NA_BIAS_ROWS + 1, GRID_W, GRID_W), lambda bi, h: (h, 0, 0, 0)),
           pl.BlockSpec((None, None, s, HEAD_DIM), per_head), pl.BlockSpec((None, None, s, 1), per_head),
           pl.BlockSpec((None, None, s, HEAD_DIM), per_head)],
        out_specs=[pl.BlockSpec((None, None, 3, s, HEAD_DIM), lambda bi, h: (bi, h, 0, 0, 0)),
                   pl.BlockSpec((None, None, NA_BIAS_ROWS, GRID_W, GRID_W), lambda bi, h: (bi, h, 0, 0, 0))],
        out_shape=[jax.ShapeDtypeStruct((b, NA_HEADS, 3, s, HEAD_DIM), F32),
                   jax.ShapeDtypeStruct((b, NA_HEADS, NA_BIAS_ROWS, GRID_W, GRID_W), F32)],
        scratch_shapes=[pltpu.VMEM((NA_FORMS, NA_Q, NA_KEYS), F32), pltpu.VMEM((NA_FORMS, NA_Q, NA_KEYS), F32)],
        compiler_params=_params(dimension_semantics=("parallel", "parallel")),
    )(heads, heads, heads, bias, out, lse, dout)


GATE_TILE = 256


def _gate_fwd(proj, z, *, gate_col, tt, name):
    _, t, d = z.shape
    nj = d // GATE_TILE
    c0 = gate_col // GATE_TILE

    def body(ga_ref, gb_ref, za_ref, zb_ref, o_ref):
        o_ref[...] = (jax.nn.sigmoid(ga_ref[...]) * za_ref[...] + jax.nn.sigmoid(gb_ref[...]) * zb_ref[...]).astype(BF16)

    return pl.pallas_call(
        body, name=name, grid=(t // tt, nj),
        in_specs=[pl.BlockSpec((tt, GATE_TILE), lambda i, j: (i, c0 + j)),
                  pl.BlockSpec((tt, GATE_TILE), lambda i, j: (i, c0 + nj + j)),
                  pl.BlockSpec((None, tt, GATE_TILE), lambda i, j: (0, i, j)),
                  pl.BlockSpec((None, tt, GATE_TILE), lambda i, j: (1, i, j))],
        out_specs=pl.BlockSpec((tt, GATE_TILE), lambda i, j: (i, j)), out_shape=jax.ShapeDtypeStruct((t, d), BF16),
        compiler_params=_params(dimension_semantics=("parallel", "parallel")),
    )(proj, proj, z, z)


def _gate_bwd(dm, proj, z, *, gate_col, tt, name):
    _, t, d = z.shape
    nj = d // GATE_TILE
    c0 = gate_col // GATE_TILE

    def body(dm_ref, g_ref, z_ref, dz_ref, dg_ref):
        dmv = dm_ref[...]
        sg = jax.nn.sigmoid(g_ref[...])
        dz_ref[...] = (dmv * sg).astype(BF16)
        dg_ref[...] = (dmv * z_ref[...] * sg * (1.0 - sg)).astype(BF16)

    return pl.pallas_call(
        body, name=name, grid=(t // tt, 2 * nj),
        in_specs=[pl.BlockSpec((tt, GATE_TILE), lambda i, j: (i, j % nj)),
                  pl.BlockSpec((tt, GATE_TILE), lambda i, j: (i, c0 + j)),
                  pl.BlockSpec((None, tt, GATE_TILE), lambda i, j: (j // nj, i, j % nj))],
        out_specs=[pl.BlockSpec((None, tt, GATE_TILE), lambda i, j: (j // nj, i, j % nj)),
                   pl.BlockSpec((tt, GATE_TILE), lambda i, j: (i, c0 + j))],
        out_shape=[jax.ShapeDtypeStruct((2, t, d), BF16), jax.ShapeDtypeStruct(proj.shape, BF16)],
        compiler_params=_params(dimension_semantics=("parallel", "parallel")),
    )(dm, proj, z)


def _adamw(w, g, m, v, *, name):
    shape = w.shape
    if w.ndim == 3:
        w2, g2, m2, v2 = w, g, m, v
    else:
        w2, g2, m2, v2 = (t.reshape(1, -1, shape[-1]) for t in (w, g, m, v))
    lead, rows, cols = w2.shape
    tr = rows
    for cand in (512, 256, 128, 64, 32, 16, 8):
        if rows % cand == 0:
            tr = cand
            break

    def body(w_ref, g_ref, m_ref, v_ref, d_ref, nm_ref, nv_ref):
        gv = g_ref[...]
        nm = ADAM_B1 * m_ref[...] + (1.0 - ADAM_B1) * gv
        nv = ADAM_B2 * v_ref[...] + (1.0 - ADAM_B2) * (gv * gv)
        m_hat = nm / (1.0 - ADAM_B1 ** ADAM_STEP)
        v_hat = nv / (1.0 - ADAM_B2 ** ADAM_STEP)
        d_ref[...] = -ADAM_LR * (m_hat / (jnp.sqrt(v_hat) + ADAM_EPS) + ADAM_WD * w_ref[...])
        nm_ref[...] = nm
        nv_ref[...] = nv

    blk = pl.BlockSpec((None, tr, cols), lambda l, i: (l, i, 0))
    out = jax.ShapeDtypeStruct((lead, rows, cols), F32)
    res = pl.pallas_call(
        body, name=name, grid=(lead, rows // tr), in_specs=[blk] * 4, out_specs=[blk] * 3, out_shape=[out] * 3,
        compiler_params=_params(dimension_semantics=("parallel", "parallel")),
    )(w2, g2, m2, v2)
    return tuple(t.reshape(shape) for t in res)


def _my_place():
    return lax.axis_index("x"), lax.axis_index("y"), lax.axis_index("c")


def _other_chips(x, y):
    return [(1 - x, y), (x, 1 - y), (1 - x, 1 - y)]


def _chip_no(chip):
    return 2 * chip[0] + chip[1]


def _window(ref, kind, size, chip, lead):
    if kind == "col":
        return ref.at[(*lead, slice(None), pl.ds(pl.multiple_of(chip * size, LANES), size))]
    if kind == "row":
        return ref.at[(*lead, pl.ds(pl.multiple_of(chip * size, BF16_ROWS), size), slice(None))]
    shard = size + HEAD_DIM
    if kind == "win_main":
        return ref.at[(*lead, slice(None), pl.ds(pl.multiple_of(chip * shard + HEAD_DIM * (chip % 2), LANES), size))]
    assert kind == "win_strad"
    return ref.at[(*lead, slice(None), pl.ds(pl.multiple_of(size + 2 * shard * (chip // 2), LANES), LANES))]


def _full_shape(shard, kind):
    _, k, n = shard.shape
    return {"col": (k, N_CHIPS * n), "row": (N_CHIPS * k, n), "win_main": (k, N_CHIPS * (n + HEAD_DIM)),
            "slot": (N_CHIPS, k, n)}[kind]


def _place_own(shard, kind, layer, *, name):
    _, k, n = shard.shape
    tr = _div_tile(k, 512, BF16_ROWS)
    tc = LANES if kind == "win_main" else n
    mine = 2 * lax.axis_index("x") + lax.axis_index("y")
    row0 = mine * (k // tr) if kind == "row" else 0
    col0 = {"col": mine, "row": 0, "slot": 0, "win_main": (mine * (n + HEAD_DIM) + HEAD_DIM * (mine % 2)) // LANES}[kind]
    scalars = jnp.stack([mine, row0, col0]).astype(jnp.int32)

    def body(s_ref, i_ref, o_ref):
        o_ref[...] = i_ref[...]

    if kind == "slot":
        o_spec = pl.BlockSpec((None, tr, tc), lambda i, j, s: (s[0], i, j))
    else:
        o_spec = pl.BlockSpec((tr, tc), lambda i, j, s: (s[1] + i, s[2] + j))
    return pl.pallas_call(
        body, name=name,
        grid_spec=pltpu.PrefetchScalarGridSpec(
            num_scalar_prefetch=1, grid=(k // tr, n // tc),
            in_specs=[pl.BlockSpec((None, tr, tc), lambda i, j, s: (layer, i, j))], out_specs=o_spec),
        out_shape=jax.ShapeDtypeStruct(_full_shape(shard, kind), shard.dtype),
        compiler_params=_params(dimension_semantics=("parallel", "parallel")),
    )(scalars, shard)


class _GatherPlan:
    def __init__(self, src, dst, shapes, kinds, layer, send_sems, recv_sems):
        self.src, self.dst, self.shapes, self.kinds, self.layer = src, dst, shapes, kinds, layer
        self.send_sems, self.recv_sems = send_sems, recv_sems
        self.x, self.y, self.c = _my_place()
        self.mine = 2 * self.x + self.y
        self.chips = _other_chips(self.x, self.y)
        self.n = len(src)

    def half(self, i, chip, half):
        _, k, n = self.shapes[i]
        kind, dst, hk = self.kinds[i], self.dst[i], k // 2
        if kind == "slot":
            return dst.at[chip, pl.ds(pl.multiple_of(half * hk, BF16_ROWS), hk), :]
        if kind == "row":
            return dst.at[pl.ds(pl.multiple_of(chip * k + half * hk, BF16_ROWS), hk), :]
        col0 = chip * n if kind == "col" else chip * (n + HEAD_DIM) + HEAD_DIM * (chip % 2)
        return dst.at[pl.ds(pl.multiple_of(half * hk, BF16_ROWS), hk), pl.ds(pl.multiple_of(col0, LANES), n)]

    def _copy(self, sem, window, to, source=None):
        return pltpu.make_async_remote_copy(src_ref=window if source is None else source, dst_ref=window,
                                            send_sem=self.send_sems.at[sem], recv_sem=self.recv_sems.at[sem],
                                            device_id=to, device_id_type=MESH)

    def sends(self):
        out = []
        for k, chip in enumerate(self.chips):
            for i in range(self.n):
                hk = self.shapes[i][1] // 2
                mine = self.src[i].at[self.layer, pl.ds(pl.multiple_of(self.c * hk, BF16_ROWS), hk), :]
                out.append(self._copy(3 * i + k, self.half(i, self.mine, self.c), (*chip, self.c), source=mine))
        return out

    def arrivals(self):
        return [self._copy(3 * i + k, self.half(i, _chip_no(chip), self.c), (*chip, self.c))
                for k, chip in enumerate(self.chips) for i in range(self.n)]

    def forwards(self, first_sem):
        sibling = (self.x, self.y, 1 - self.c)
        return [self._copy(first_sem + 3 * i + k, self.half(i, _chip_no(chip), self.c), sibling)
                for k, chip in enumerate(self.chips) for i in range(self.n)]

    def forwarded(self, first_sem):
        sibling = (self.x, self.y, 1 - self.c)
        return [self._copy(first_sem + 3 * i + k, self.half(i, _chip_no(chip), 1 - self.c), sibling)
                for k, chip in enumerate(self.chips) for i in range(self.n)]


IN_HBM = pl.BlockSpec(memory_space=pltpu.HBM)
IN_SEM = pl.BlockSpec(memory_space=pltpu.SEMAPHORE)
DATAFLOW = pltpu.SideEffectType.DATAFLOW_SIDE_EFFECTING


def _gather_layer_start(shards, kinds, fulls, layer, after, *, name):
    n_w = len(shards)
    shapes = [sh.shape for sh in shards]

    def body(*refs):
        plan = _GatherPlan(refs[:n_w], refs[n_w:2 * n_w], shapes, kinds, layer, refs[2 * n_w + 1], refs[2 * n_w + 2])
        for cp in plan.sends():
            cp.start()
        token = refs[-1]
        token[...] = jnp.zeros_like(token)

    operands = [pltpu.with_memory_space_constraint(a, pltpu.HBM) for a in (*shards, *fulls)]
    res = pl.pallas_call(
        body, name=name, in_specs=[IN_HBM] * (2 * n_w) + [pl.BlockSpec(memory_space=pl.ANY)],
        out_specs=(IN_SEM, IN_SEM, *([IN_HBM] * (2 * n_w)), pl.BlockSpec(memory_space=pltpu.VMEM)),
        out_shape=(pltpu.SemaphoreType.DMA((3 * n_w,)), pltpu.SemaphoreType.DMA((3 * n_w,)),
                   *[pltpu.HBM(a.shape, a.dtype) for a in operands], jax.ShapeDtypeStruct((8, LANES), F32)),
        input_output_aliases={i: 2 + i for i in range(2 * n_w)},
        compiler_params=pltpu.CompilerParams(has_side_effects=DATAFLOW),
    )(*operands, after)
    return res[0], res[1], res[2:2 + n_w], res[2 + n_w:2 + 2 * n_w], res[-1]


def _gather_layer_wait(send_sems, recv_sems, shards, fulls, kinds, layer, after, *, name):
    n_w = len(shards)
    shapes = [sh.shape for sh in shards]

    def body(*refs):
        plan = _GatherPlan(refs[:n_w], refs[n_w:2 * n_w], shapes, kinds, layer, refs[2 * n_w], refs[2 * n_w + 1])
        for cp in plan.sends():
            cp.wait_send()
        for cp in plan.arrivals():
            cp.wait_recv()

    res = pl.pallas_call(
        body, name=name, in_specs=[IN_HBM] * (2 * n_w) + [IN_SEM, IN_SEM, pl.BlockSpec(memory_space=pl.ANY)],
        out_specs=[IN_HBM] * (2 * n_w), out_shape=[pltpu.HBM(a.shape, a.dtype) for a in (*shards, *fulls)],
        input_output_aliases={i: i for i in range(2 * n_w)},
        compiler_params=pltpu.CompilerParams(has_side_effects=DATAFLOW),
    )(*shards, *fulls, send_sems, recv_sems, after)
    return res[n_w:]


def _gather_layer_forward(shapes, kinds, fulls, *, name):
    n_w = len(fulls)

    def body(*refs):
        plan = _GatherPlan([None] * n_w, refs[n_w:2 * n_w], shapes, kinds, 0, *refs[2 * n_w:])
        passed = plan.forwards(0)
        for cp in passed:
            cp.start()
        for cp in plan.forwarded(0):
            cp.wait_recv()
        for cp in passed:
            cp.wait_send()

    return pl.pallas_call(
        body, name=name, in_specs=[HBM] * n_w, out_specs=[HBM] * n_w,
        out_shape=[jax.ShapeDtypeStruct(f.shape, f.dtype) for f in fulls],
        input_output_aliases={i: i for i in range(n_w)},
        scratch_shapes=[pltpu.SemaphoreType.DMA((3 * n_w,)), pltpu.SemaphoreType.DMA((3 * n_w,))],
    )(*fulls)


def _on_core(layer):
    return (lax.axis_index("c") == layer).astype(jnp.int32).reshape(1)


N_DEVICES = 2 * N_CHIPS


class _ScatterPlan:
    def __init__(self, src, dst, kinds, sizes, layer, send_sems, recv_sems):
        self.src, self.dst, self.kinds, self.sizes, self.layer = src, dst, kinds, sizes, layer
        self.send_sems, self.recv_sems = send_sems, recv_sems
        self.x, self.y, self.c = _my_place()
        self.mine = 2 * self.x + self.y
        self.chips = _other_chips(self.x, self.y)
        self.n = len(src)

    def _copy(self, i, k, window_of, from_chip, from_core, to):
        return pltpu.make_async_remote_copy(src_ref=_window(self.src[i], self.kinds[i], self.sizes[i], window_of, ()),
                                            dst_ref=self.dst[i].at[2 * from_chip + from_core],
                                            send_sem=self.send_sems.at[4 * i + k],
                                            recv_sem=self.recv_sems.at[2 * (4 * i + k) + from_core],
                                            device_id=to, device_id_type=MESH)

    def to_chips(self):
        return [self._copy(i, k, _chip_no(chip), self.mine, self.c, (*chip, self.layer))
                for k, chip in enumerate(self.chips) for i in range(self.n)]

    def to_sibling(self):
        return [self._copy(i, 3, self.mine, self.mine, self.c, (self.x, self.y, self.layer)) for i in range(self.n)]

    def arrivals(self):
        out = [self._copy(i, k, self.mine, _chip_no(chip), core, (*chip, core))
               for k, chip in enumerate(self.chips) for core in (0, 1) for i in range(self.n)]
        return out + [self._copy(i, 3, self.mine, self.mine, 1 - self.layer, (self.x, self.y, 1 - self.layer))
                      for i in range(self.n)]


def _slab_shape(p, kind, size):
    return (N_DEVICES,) + {"col": (p.shape[0], size), "row": (size, p.shape[1]), "win_main": (p.shape[0], size),
                           "win_strad": (p.shape[0], LANES)}[kind]


def _grads_to_chips_start(pairs, kinds, sizes, layer, *, name):
    n_w = len(pairs)

    def body(*refs):
        plan = _ScatterPlan(refs[:n_w], refs[n_w:2 * n_w], kinds, sizes, layer, refs[2 * n_w], refs[2 * n_w + 1])
        for cp in plan.to_chips():
            cp.start()

        @pl.when(plan.c != layer)
        def _():
            for cp in plan.to_sibling():
                cp.start()

        token = refs[-1]
        token[...] = jnp.zeros_like(token)

    slabs = [lax.empty(_slab_shape(p, kind, size), p.dtype) for p, kind, size in zip(pairs, kinds, sizes)]
    operands = [pltpu.with_memory_space_constraint(a, pltpu.HBM) for a in (*pairs, *slabs)]
    res = pl.pallas_call(
        body, name=name, in_specs=[IN_HBM] * (2 * n_w),
        out_specs=(IN_SEM, IN_SEM, *([IN_HBM] * (2 * n_w)), pl.BlockSpec(memory_space=pltpu.VMEM)),
        out_shape=(pltpu.SemaphoreType.DMA((4 * n_w,)), pltpu.SemaphoreType.DMA((8 * n_w,)),
                   *[pltpu.HBM(a.shape, a.dtype) for a in operands], jax.ShapeDtypeStruct((8, LANES), F32)),
        input_output_aliases={i: 2 + i for i in range(2 * n_w)},
        compiler_params=pltpu.CompilerParams(has_side_effects=DATAFLOW),
    )(*operands)
    return res[0], res[1], res[2:2 + n_w], res[2 + n_w:2 + 2 * n_w], res[-1]


def _grads_to_chips_wait(send_sems, recv_sems, pairs, slabs, kinds, sizes, layer, after, *, name):
    n_w = len(pairs)

    def body(*refs):
        plan = _ScatterPlan(refs[:n_w], refs[n_w:2 * n_w], kinds, sizes, layer, refs[2 * n_w], refs[2 * n_w + 1])
        for cp in plan.to_chips():
            cp.wait_send()

        @pl.when(plan.c != layer)
        def _():
            for cp in plan.to_sibling():
                cp.wait_send()

        @pl.when(plan.c == layer)
        def _():
            for cp in plan.arrivals():
                cp.wait_recv()

    res = pl.pallas_call(
        body, name=name, in_specs=[IN_HBM] * (2 * n_w) + [IN_SEM, IN_SEM, pl.BlockSpec(memory_space=pl.ANY)],
        out_specs=[IN_HBM] * (2 * n_w), out_shape=[pltpu.HBM(a.shape, a.dtype) for a in (*pairs, *slabs)],
        input_output_aliases={i: i for i in range(2 * n_w)},
        compiler_params=pltpu.CompilerParams(has_side_effects=DATAFLOW),
    )(*pairs, *slabs, send_sems, recv_sems, after)
    return res[:n_w], res[n_w:]


def _sum_slabs(slabs, pair, kind, size, layer, into, *, name):
    n_s, k, n = slabs.shape
    tr = _div_tile(k, 512, BF16_ROWS)
    tc = n if kind in ("col", "row") else LANES
    x, y, _ = _my_place()
    mine = 2 * x + y
    shard = size + HEAD_DIM
    row0 = mine * (k // tr) if kind == "row" else 0
    col0 = {"col": mine, "row": 0, "win_main": (mine * shard + HEAD_DIM * (mine % 2)) // LANES,
            "win_strad": (size + 2 * shard * (mine // 2)) // LANES}[kind]
    on = _on_core(layer)[0]
    scalars = jnp.stack([2 * mine + layer, row0 * on, col0 * on, on]).astype(jnp.int32)

    def body(s_ref, slab_ref, own_ref, *rest):
        o_ref = rest[-1]
        me = s_ref[0]

        @pl.when(s_ref[3] == 1)
        def _():
            acc = jnp.zeros(o_ref.shape, F32)
            for i in range(n_s):
                acc = acc + jnp.where(me == i, own_ref[...], slab_ref[i]).astype(F32)
            o_ref[...] = acc

    operands = [scalars, slabs, pair] + ([] if into is None else [into])
    return pl.pallas_call(
        body, name=name,
        grid_spec=pltpu.PrefetchScalarGridSpec(
            num_scalar_prefetch=1, grid=(k // tr, n // tc),
            in_specs=[pl.BlockSpec((n_s, tr, tc), lambda i, j, s: (0, i * s[3], j * s[3])),
                      pl.BlockSpec((tr, tc), lambda i, j, s: (s[1] + i * s[3], s[2] + j * s[3]))]
            + ([] if into is None else [HBM]),
            out_specs=pl.BlockSpec((None, tr, tc), lambda i, j, s: (layer, i * s[3], j * s[3]))),
        out_shape=jax.ShapeDtypeStruct((2, k, n), F32),
        input_output_aliases={} if into is None else {3: 0},
        compiler_params=_params(dimension_semantics=("arbitrary", "arbitrary")),
    )(*operands)


def _exchange_layers(bufs, *, name):
    n_w = len(bufs)

    def body(*refs):
        dst = refs[n_w:2 * n_w]
        send_sems, recv_sems = refs[2 * n_w:]
        x, y, c = _my_place()

        def copy(i, layer):
            return pltpu.make_async_remote_copy(src_ref=dst[i].at[layer], dst_ref=dst[i].at[layer], send_sem=send_sems.at[i],
                                                recv_sem=recv_sems.at[i], device_id=(x, y, 1 - c), device_id_type=MESH)

        sends = [copy(i, c) for i in range(n_w)]
        for cp in sends:
            cp.start()
        for i in range(n_w):
            copy(i, 1 - c).wait_recv()
        for cp in sends:
            cp.wait_send()

    return pl.pallas_call(
        body, name=name, in_specs=[HBM] * n_w, out_specs=[HBM] * n_w,
        out_shape=[jax.ShapeDtypeStruct(b.shape, b.dtype) for b in bufs],
        input_output_aliases={i: i for i in range(n_w)},
        scratch_shapes=[pltpu.SemaphoreType.DMA((n_w,)), pltpu.SemaphoreType.DMA((n_w,))],
    )(*bufs)


def _all_sum_small(v, *, name):
    r = v.shape[0]
    relations = [(dx, dy, dc) for dx in (0, 1) for dy in (0, 1) for dc in (0, 1)][1:]

    def body(v_ref, o_ref, buf, send_sems, recv_sems):
        x, y, c = _my_place()
        me = 4 * x + 2 * y + c
        buf[me] = v_ref[...]
        peers = [(x + dx - 2 * x * dx, y + dy - 2 * y * dy, c + dc - 2 * c * dc) for dx, dy, dc in relations]

        def copy(k, slot):
            return pltpu.make_async_remote_copy(src_ref=v_ref, dst_ref=buf.at[slot], send_sem=send_sems.at[k],
                                                recv_sem=recv_sems.at[k], device_id=peers[k], device_id_type=MESH)

        sends = [copy(k, me) for k in range(len(relations))]
        for cp in sends:
            cp.start()
        for k, (px, py, pc) in enumerate(peers):
            copy(k, 4 * px + 2 * py + pc).wait_recv()
        for cp in sends:
            cp.wait_send()
        acc = buf[0]
        for i in range(1, 8):
            acc = acc + buf[i]
        o_ref[...] = acc

    vm = pl.BlockSpec(memory_space=pltpu.VMEM)
    return pl.pallas_call(
        body, name=name, in_specs=[vm], out_specs=vm, out_shape=jax.ShapeDtypeStruct((r, LANES), F32),
        scratch_shapes=[pltpu.VMEM((8, r, LANES), F32), pltpu.SemaphoreType.DMA((7,)), pltpu.SemaphoreType.DMA((7,))],
    )(v)


SHARDED = (("ffn1_w_up", "col"), ("ffn1_w_down", "row"), ("w_in", "win"), ("w_branch_a", "col"),
           ("w_branch_b", "col"), ("w_out", "row"), ("ffn2_w_up", "col"), ("ffn2_w_down", "row"))
REPLICATED = ("ffn1_norm", "mix_norm", "na_rel_bias", "ffn2_norm", "final_norm")


def _weight_pieces(w):
    even = lax.axis_index("y") == 0
    shards, kinds, names = [], [], []
    for name, kind in SHARDED:
        wb = w[name].astype(BF16)
        if kind == "win":
            main = wb.shape[-1] - HEAD_DIM
            assert main % LANES == 0
            zeros = jnp.zeros(wb.shape[:-1] + (HEAD_DIM,), BF16)
            shards += [jnp.where(even, wb[..., :main], wb[..., HEAD_DIM:]),
                       jnp.where(even, jnp.concatenate([wb[..., main:], zeros], -1),
                                 jnp.concatenate([zeros, wb[..., :HEAD_DIM]], -1))]
            kinds += ["win_main", "slot"]
            names += [name, name + "_strad"]
        else:
            shards.append(wb)
            kinds.append(kind)
            names.append(name)
    return names, kinds, shards


def _finish_w_in(full):
    full = dict(full)
    strad = full.pop("w_in_strad")
    main = full["w_in"].shape[1] // N_CHIPS - HEAD_DIM
    for i in range(N_CHIPS // 2):
        lo = main + 2 * (main + HEAD_DIM) * i
        full["w_in"] = full["w_in"].at[:, lo:lo + LANES].set(strad[2 * i] + strad[2 * i + 1])
    return full


def _scatter_pieces(shards):
    names, kinds, sizes, srcs = [], [], [], []
    for name, kind in SHARDED:
        shp = shards[name].shape
        if kind == "win":
            names += [name, name + "_strad"]
            kinds += ["win_main", "win_strad"]
            sizes += [shp[2] - HEAD_DIM] * 2
            srcs += [name, name]
        else:
            names.append(name)
            kinds.append(kind)
            sizes.append(shp[1] if kind == "row" else shp[2])
            srcs.append(name)
    return names, kinds, sizes, srcs


def _finish_weight_grads(reduced, names, tag):
    out = dict(zip(names, _exchange_layers(reduced, name=f"{tag}_layers")))
    if "w_in_strad" in out:
        strad = out.pop("w_in_strad")
        even = lax.axis_index("y") == 0
        out["w_in"] = jnp.where(even, jnp.concatenate([out["w_in"], strad[..., :HEAD_DIM]], -1),
                                jnp.concatenate([strad[..., HEAD_DIM:], out["w_in"]], -1))
    return out


class _Grads:
    def __init__(self):
        self.arrays = {}

    def put(self, weight, layer, a, b, *, cols=None, col_off=0, **kw):
        self.arrays[weight, layer] = _mm(a, b, mode="tn", out_dtype=BF16, out_cols=cols, out_col_off=col_off,
                                         out_into=self.arrays.get((weight, layer)), **kw)


def _ffn_fwd(x, h, w_up, w_down, tag):
    t, d = x.shape
    f = w_down.shape[0]
    a, gate, up = _mm_swiglu_fwd(h, w_up, tm=_div_tile(t, ROWS_NARROW, 8), tn=MXU_N, name=f"{tag}_up")
    x_out = _mm(a, w_down, mode="nn", out_dtype=F32, tm=_div_tile(t, ROWS_WIDE, 8), tn=d, tk=f, alpha=0.5, res=x, name=f"{tag}_down")
    return x_out, (x, h, a, gate, up)


def _ffn_bwd(dx, dxb, saved, norm_g, w_up, w_down, layer, grads, wname, tag, scatter):
    x, h, a, gate, up = saved
    t, d = x.shape
    f = w_down.shape[0]
    tn = _div_tile(f, 1408)
    grads.put(f"{wname}_w_down", layer, a, dxb, tm=tn, tn=d, tk=ROWS_CONTRACTED, alpha=0.5, name=f"{tag}_dwd")
    d_gate, d_up = _mm_swiglu_bwd(dxb, w_down, gate, up, alpha=0.5, tm=_div_tile(t, ROWS_NARROW, 8), tn=MXU_N, name=f"{tag}_da")
    grads.put(f"{wname}_w_up", layer, h, d_gate, cols=2 * f, tm=d, tn=tn, tk=ROWS_CONTRACTED, name=f"{tag}_dwg")
    grads.put(f"{wname}_w_up", layer, h, d_up, cols=2 * f, col_off=f // tn, tm=d, tn=tn, tk=ROWS_CONTRACTED, name=f"{tag}_dwu")
    started = scatter(layer, [f"{wname}_w_up", f"{wname}_w_down"])
    dh = _mm(d_gate, w_up, mode="nt", out_dtype=F32, tm=_div_tile(t, ROWS_WIDE, 8), tn=d, tk=f, name=f"{tag}_dh1")
    dh = _mm(d_up, w_up, mode="nt", out_dtype=F32, tm=_div_tile(t, ROWS_WIDE, 8), tn=d, tk=f, b_k_off=1, res=dh, name=f"{tag}_dh2")
    return _rms_bwd(dh, x, norm_g + started, dx, tt=NORM_ROWS, name=f"{tag}_dnorm")


def _to_heads(y, b, n_heads):
    t, w = y.shape
    return y.reshape(b, t // b, n_heads, HEAD_DIM).transpose(0, 2, 1, 3)


def _from_heads(y):
    b, n, s, hd = y.shape
    return y.transpose(0, 2, 1, 3).reshape(b * s, n * hd)


N_QKV = 3 * (DIL_HEADS + NA_HEADS) * HEAD_DIM


def _mixer_fwd(x, b, norm_g, full, bias, tabs, tag):
    t, d = x.shape
    s = t // b
    n_in = full["w_in"].shape[1]
    h = _rms_fwd(x, norm_g, tt=NORM_ROWS, name=f"{tag}_norm")
    proj = _mm(h, full["w_in"], mode="nn", out_dtype=F32, tm=_div_tile(t, ROWS_NARROW, 8), tn=MXU_N, tk=d, name=f"{tag}_in")
    heads = _split_heads(proj.reshape(b, s, -1), *tabs, n_pairs=N_QKV // LANES, rot_pairs=DIL_HEADS,
                         scale_ranges=((0, DIL_HEADS // 2), (3 * DIL_HEADS // 2, (3 * DIL_HEADS + NA_HEADS) // 2)),
                         name=f"{tag}_heads")
    ya, lse_a = _dil_attn_fwd(heads, name=f"{tag}_dil")
    yb, lse_b = _na_attn_fwd(heads, bias, first=3 * DIL_HEADS, name=f"{tag}_na")
    ya2, yb2 = _from_heads(ya), _from_heads(yb)
    z = _mm(ya2, full["w_branch_a"], mode="nn", out_dtype=F32, tm=_div_tile(t, ROWS_NARROW, 8), tn=MXU_N, tk=ya2.shape[1],
            out_slab=(0, 2), name=f"{tag}_za")
    z = _mm(yb2, full["w_branch_b"], mode="nn", out_dtype=F32, tm=_div_tile(t, ROWS_NARROW, 8), tn=MXU_N, tk=yb2.shape[1],
            out_slab=(1, 2), out_into=z, name=f"{tag}_zb")
    merged = _gate_fwd(proj, z, gate_col=N_QKV, tt=GATE_ROWS, name=f"{tag}_gate")
    x_out = _mm(merged, full["w_out"], mode="nn", out_dtype=F32, tm=_div_tile(t, ROWS_NARROW, 8), tn=MXU_N, tk=d, res=x, name=f"{tag}_out")
    return x_out, (x, h, proj, heads, ya, lse_a, yb, lse_b, ya2, yb2, z, merged)


def _mixer_bwd(dx, dob, b, saved, norm_g, full, layer, bias, tabs, grads, tag, scatter):
    x, h, proj, heads, ya, lse_a, yb, lse_b, ya2, yb2, z, merged = saved
    t, d = x.shape
    s = t // b
    n_in = full["w_in"].shape[1]
    grads.put("w_out", layer, merged, dob, tm=d, tn=d, tk=ROWS_CONTRACTED, name=f"{tag}_dwo")
    dm = _mm(dob, full["w_out"], mode="nt", out_dtype=F32, tm=_div_tile(t, ROWS_NARROW, 8), tn=MXU_N, tk=d, name=f"{tag}_dm")
    dz, dproj = _gate_bwd(dm, proj, z, gate_col=N_QKV, tt=GATE_ROWS, name=f"{tag}_dgate")
    grads.put("w_branch_a", layer, ya2, dz, b_sel=0, tm=ya2.shape[1], tn=d, tk=ROWS_CONTRACTED, name=f"{tag}_dwa")
    grads.put("w_branch_b", layer, yb2, dz, b_sel=1, tm=yb2.shape[1], tn=d, tk=ROWS_CONTRACTED, name=f"{tag}_dwb")
    started = scatter(layer, ["w_out", "w_branch_a", "w_branch_b"])
    dya = _mm(dz, full["w_branch_a"], mode="nt", out_dtype=F32, tm=_div_tile(t, ROWS_NARROW, 8), tn=MXU_N, tk=d, a_sel=0, name=f"{tag}_dya")
    dyb = _mm(dz, full["w_branch_b"], mode="nt", out_dtype=F32, tm=_div_tile(t, ROWS_NARROW, 8), tn=MXU_N, tk=d, a_sel=1, name=f"{tag}_dyb")
    d_dil = _dil_attn_bwd(heads, ya, lse_a, _to_heads(dya, b, DIL_GROUP_HEADS), name=f"{tag}_ddil")
    d_na, d_bias = _na_attn_bwd(heads, bias, yb, lse_b, _to_heads(dyb, b, NA_HEADS), first=3 * DIL_HEADS, name=f"{tag}_dna")
    dproj = _merge_heads(d_dil, *tabs, heads_per_row=DIL_GROUP_HEADS, rot_pairs=DIL_HEADS, scale_pairs=DIL_HEADS // 2,
                         dilated=True, out_cols=n_in, tile_off=0, into=dproj.reshape(b, s, n_in), name=f"{tag}_dheads_a")
    dproj = _merge_heads(d_na, *tabs, heads_per_row=NA_HEADS, rot_pairs=0, scale_pairs=NA_HEADS // 2, dilated=False,
                         out_cols=n_in, tile_off=3 * DIL_HEADS // 2, into=dproj, name=f"{tag}_dheads_b").reshape(t, n_in)
    grads.put("w_in", layer, h, dproj, tm=_div_tile(d, 512), tn=_div_tile(n_in, 2944), tk=ROWS_CONTRACTED // 2, name=f"{tag}_dwin")
    started = started + scatter(layer, ["w_in"])
    dh = _mm(dproj, full["w_in"], mode="nt", out_dtype=F32, tm=_div_tile(t, ROWS_WIDE, 8), tn=d, tk=_div_tile(n_in, 2944), name=f"{tag}_dh")
    dx_in, dxb_in, d_norm = _rms_bwd(dh, x, norm_g + started, dx, tt=NORM_ROWS, name=f"{tag}_dnorm")
    d_rb = _na_collapse_bias(d_bias, name=f"{tag}_dbias")
    return dx_in, dxb_in, d_norm, d_rb


def kernel(x, ffn1_norm, ffn1_w_up, ffn1_w_down, mix_norm, w_in, na_rel_bias, w_branch_a, w_branch_b, w_out, ffn2_norm, ffn2_w_up, ffn2_w_down, final_norm, loss_target, m_ffn1_norm, m_ffn1_w_up, m_ffn1_w_down, m_mix_norm, m_w_in, m_na_rel_bias, m_w_branch_a, m_w_branch_b, m_w_out, m_ffn2_norm, m_ffn2_w_up, m_ffn2_w_down, m_final_norm, v_ffn1_norm, v_ffn1_w_up, v_ffn1_w_down, v_mix_norm, v_w_in, v_na_rel_bias, v_w_branch_a, v_w_branch_b, v_w_out, v_ffn2_norm, v_ffn2_w_up, v_ffn2_w_down, v_final_norm):
    w = dict(ffn1_norm=ffn1_norm, ffn1_w_up=ffn1_w_up, ffn1_w_down=ffn1_w_down, mix_norm=mix_norm, w_in=w_in,
             na_rel_bias=na_rel_bias, w_branch_a=w_branch_a, w_branch_b=w_branch_b, w_out=w_out, ffn2_norm=ffn2_norm,
             ffn2_w_up=ffn2_w_up, ffn2_w_down=ffn2_w_down, final_norm=final_norm)
    mom = dict(ffn1_norm=m_ffn1_norm, ffn1_w_up=m_ffn1_w_up, ffn1_w_down=m_ffn1_w_down, mix_norm=m_mix_norm, w_in=m_w_in,
               na_rel_bias=m_na_rel_bias, w_branch_a=m_w_branch_a, w_branch_b=m_w_branch_b, w_out=m_w_out,
               ffn2_norm=m_ffn2_norm, ffn2_w_up=m_ffn2_w_up, ffn2_w_down=m_ffn2_w_down, final_norm=m_final_norm)
    var = dict(ffn1_norm=v_ffn1_norm, ffn1_w_up=v_ffn1_w_up, ffn1_w_down=v_ffn1_w_down, mix_norm=v_mix_norm, w_in=v_w_in,
               na_rel_bias=v_na_rel_bias, w_branch_a=v_w_branch_a, w_branch_b=v_w_branch_b, w_out=v_w_out,
               ffn2_norm=v_ffn2_norm, ffn2_w_up=v_ffn2_w_up, ffn2_w_down=v_ffn2_w_down, final_norm=v_final_norm)
    b, s, d = x.shape
    t = b * s
    depth = ffn1_norm.shape[0]
    assert depth == 2, "core c of a chip sends / reduces layer c"
    shards = {name: w[name] for name, _ in SHARDED}

    names, kinds, pieces = _weight_pieces(w)
    by_layer = [[p[l:l + 1] for p in pieces] for l in range(depth)]
    own = [[_place_own(p, kind, 0, name=f"own{l}_{nm}") for nm, kind, p in zip(names, kinds, by_layer[l])] for l in range(depth)]
    full = [{}, {}]

    def gather_start(layer, group, after, tag):
        idx = [i for i, nm in enumerate(names) if nm in group]
        pick = lambda seq: [seq[i] for i in idx]
        *state, token = _gather_layer_start(pick(by_layer[layer]), pick(kinds), pick(own[layer]), 0, after, name=f"{tag}_start")
        return (layer, idx, tag, state), token[:1, :1]

    def gather_finish(started, after):
        layer, idx, tag, state = started
        pick = lambda seq: [seq[i] for i in idx]
        landed = _gather_layer_wait(*state, pick(kinds), 0, after, name=f"{tag}_wait")
        done = _gather_layer_forward([by_layer[layer][i].shape for i in idx], pick(kinds), landed, name=f"{tag}_forward")
        full[layer].update(zip(pick(names), done))
        return done[0]

    ffn1, mixer, ffn2 = names[:2], names[2:7], names[7:]
    assert mixer[0] == "w_in" and ffn2[0] == "ffn2_w_up", names
    xc = x.reshape(t, d)
    l0_ffn1, token_ffn1 = gather_start(0, ffn1, xc, "gather_l0_ffn1")
    tabs = _rope_tables(s)
    bias = _na_expand_bias(na_rel_bias, name="na_bias")

    saved = []
    h = _rms_fwd(xc, ffn1_norm[:1] + token_ffn1, tt=NORM_ROWS, name="l0_ffn1_norm")
    landed = gather_finish(l0_ffn1, h)
    l0_mixer, token_mixer = gather_start(0, mixer, landed, "gather_l0_mixer")
    xc, s1 = _ffn_fwd(xc, h + token_mixer.astype(BF16), full[0]["ffn1_w_up"], full[0]["ffn1_w_down"], "l0_ffn1")
    landed = gather_finish(l0_mixer, xc)
    full[0] = _finish_w_in(full[0])
    l0_ffn2, token_ffn2 = gather_start(0, ffn2, landed, "gather_l0_ffn2")
    layer1, token_layer1 = gather_start(1, names, landed, "gather_l1")
    xc, s2 = _mixer_fwd(xc, b, mix_norm[:1] + token_ffn2 + token_layer1, full[0], bias[0], tabs, "l0_mix")
    gather_finish(l0_ffn2, xc)
    xc, s3 = _ffn_fwd(xc, _rms_fwd(xc, ffn2_norm[:1], tt=NORM_ROWS, name="l0_ffn2_norm"), full[0]["ffn2_w_up"], full[0]["ffn2_w_down"],
                      "l0_ffn2")
    saved.append((s1, s2, s3))
    gather_finish(layer1, xc)
    full[1] = _finish_w_in(full[1])
    for l in range(1, depth):
        xc, s1 = _ffn_fwd(xc, _rms_fwd(xc, ffn1_norm[l:l + 1], tt=NORM_ROWS, name=f"l{l}_ffn1_norm"), full[l]["ffn1_w_up"],
                          full[l]["ffn1_w_down"], f"l{l}_ffn1")
        xc, s2 = _mixer_fwd(xc, b, mix_norm[l:l + 1], full[l], bias[l], tabs, f"l{l}_mix")
        xc, s3 = _ffn_fwd(xc, _rms_fwd(xc, ffn2_norm[l:l + 1], tt=NORM_ROWS, name=f"l{l}_ffn2_norm"), full[l]["ffn2_w_up"],
                          full[l]["ffn2_w_down"], f"l{l}_ffn2")
        saved.append((s1, s2, s3))

    dx, dxb, d_final, loss_part = _final_loss(xc, final_norm.reshape(1, d), loss_target.reshape(t, d), tt=NORM_ROWS, name="final_loss")
    grads = _Grads()
    piece_names, piece_kinds, piece_sizes, piece_srcs = _scatter_pieces(shards)
    scattered = []

    def scatter(layer, weights):
        tag = f"grads{layer}_{weights[0]}"
        idx = [i for i, src in enumerate(piece_srcs) if src in weights]
        pick = lambda seq: [seq[i] for i in idx]
        *state, token = _grads_to_chips_start([grads.arrays[src, layer] for src in pick(piece_srcs)], pick(piece_kinds),
                                              pick(piece_sizes), layer, name=f"{tag}_to_chips_start")
        scattered.append((layer, idx, state))
        return token[:1, :1]
    small = {name: [None] * depth for name in REPLICATED[:-1]}
    for l in reversed(range(depth)):
        s1, s2, s3 = saved[l]
        dx, dxb, small["ffn2_norm"][l] = _ffn_bwd(dx, dxb, s3, ffn2_norm[l:l + 1], full[l]["ffn2_w_up"], full[l]["ffn2_w_down"],
                                                  l, grads, "ffn2", f"l{l}_ffn2", scatter)
        dx, dxb, small["mix_norm"][l], small["na_rel_bias"][l] = _mixer_bwd(
            dx, dxb, b, s2, mix_norm[l:l + 1], full[l], l, bias[l], tabs, grads, f"l{l}_mix", scatter)
        dx, dxb, small["ffn1_norm"][l] = _ffn_bwd(dx, dxb, s1, ffn1_norm[l:l + 1], full[l]["ffn1_w_up"], full[l]["ffn1_w_down"],
                                                  l, grads, "ffn1", f"l{l}_ffn1", scatter)
    grad_x = dx.reshape(b, s, d)
    reduced = [None] * len(piece_names)

    def arrive(group, after):
        layer, idx, state = group
        state = _grads_to_chips_wait(*state, [piece_kinds[i] for i in idx], [piece_sizes[i] for i in idx], layer, after,
                                     name=f"grads{layer}_{piece_names[idx[0]]}_to_chips_wait")
        for i, p, sl in zip(idx, *state):
            reduced[i] = _sum_slabs(sl, p, piece_kinds[i], piece_sizes[i], layer, reduced[i],
                                    name=f"grads{layer}_sum_{piece_names[i]}")
        return idx

    for group in scattered[:-1]:
        arrive(group, dx)
    late = scattered[-1][1]
    early = [i for i in range(len(piece_names)) if i not in late]
    g_out = _finish_weight_grads([reduced[i] for i in early], [piece_names[i] for i in early], "grads_early")

    parts = [jnp.stack(small[name]).reshape(-1) for name in REPLICATED[:-1]] + [d_final.reshape(-1), loss_part[0, :1]]
    sizes = [v.shape[0] for v in parts]
    flat = jnp.concatenate(parts)
    flat = jnp.pad(flat, (0, -flat.shape[0] % (8 * LANES)))
    small_sum = _all_sum_small(flat.reshape(-1, LANES), name="small_all_sum").reshape(-1)
    off = 0
    for name, n in zip(REPLICATED, sizes[:-1]):
        g_out[name] = small_sum[off:off + n].reshape(w[name].shape)
        off += n
    loss = small_sum[off]

    names = list(w)
    delta, new_m, new_v = {}, {}, {}
    for name in [n for n in names if n in g_out]:
        delta[name], new_m[name], new_v[name] = _adamw(w[name], g_out[name], mom[name], var[name], name=f"adamw_{name}")
    arrive(scattered[-1], delta["w_in"])
    g_out.update(_finish_weight_grads([reduced[i] for i in late], [piece_names[i] for i in late], "grads_late"))
    for name in [n for n in names if n not in delta]:
        delta[name], new_m[name], new_v[name] = _adamw(w[name], g_out[name], mom[name], var[name], name=f"adamw_{name}")
    return (loss, grad_x, *[g_out[n] for n in names], *[delta[n] for n in names], *[new_m[n] for n in names],
            *[new_v[n] for n in names])
```

```python
import functools

import numpy as np
import jax
import jax.numpy as jnp
from jax import lax
from jax.experimental import pallas as pl
from jax.experimental.pallas import tpu as pltpu

F32, BF16 = jnp.float32, jnp.bfloat16
MESH = pl.DeviceIdType.MESH

HEAD_DIM = 64
DILATIONS = (1, 4, 16)
DIL_HALF = 64
DIL_GROUP_HEADS = 4
DIL_HEADS = 12
NA_HEADS = 8
GRID_W = 64
NA_ROWS = 8
NA_COLS = 16
ROPE_THETA = 10000.0
RMS_EPS = 1e-6
NEG_INF = -1e30
ADAM_LR, ADAM_B1, ADAM_B2, ADAM_EPS, ADAM_WD, ADAM_STEP = 0.001, 0.9, 0.999, 1e-08, 0.01, 10
QK_SCALE = HEAD_DIM ** -0.5

N_CHIPS = 4
LANES = 128
BF16_ROWS = 16
VMEM_LIMIT = 56 * 1024 * 1024
MXU_N = 256
ROWS_NARROW = 4096
ROWS_WIDE = 512
NORM_ROWS = 1024
GATE_ROWS = 2048
ROWS_CONTRACTED = 4096
BLOCKS_IN_FLIGHT = 8

_NN = (((1,), (0,)), ((), ()))
_NT = (((1,), (1,)), ((), ()))
_TN = (((0,), (0,)), ((), ()))

HBM = pl.BlockSpec(memory_space=pl.ANY)


def _params(**kw):
    return pltpu.CompilerParams(vmem_limit_bytes=VMEM_LIMIT, **kw)


def _dot(a, b, dims):
    return lax.dot_general(a, b, dims, preferred_element_type=F32)


def _div_tile(n, cap, mult=LANES):
    best = None
    for t in range(mult, min(n, cap) + 1, mult):
        if n % t == 0:
            best = t
    return n if best is None else best


def _stacked(block, index, sel):
    if sel is None:
        return pl.BlockSpec(block, index)
    return pl.BlockSpec((None,) + block, lambda *g: (sel,) + index(*g))


def _mm(a, b, *, mode, out_dtype, tm, tn, tk, name, alpha=1.0, res=None, a_sel=None, b_sel=None, b_k_off=0,
        out_slab=None, out_cols=None, out_col_off=0, out_into=None):
    a2, b2 = a.shape[-2:], b.shape[-2:]
    if mode == "nn":
        (m, k), n = a2, b2[1]
        a_spec = _stacked((tm, tk), lambda i, j, kk: (i, kk), a_sel)
        b_spec = _stacked((tk, tn), lambda i, j, kk: (kk + b_k_off, j), b_sel)
        dims = _NN
    elif mode == "nt":
        (m, k), n = a2, b2[0]
        a_spec = _stacked((tm, tk), lambda i, j, kk: (i, kk), a_sel)
        b_spec = _stacked((tn, tk), lambda i, j, kk: (j, kk + b_k_off), b_sel)
        dims = _NT
    else:
        (k, m), n = a2, b2[1]
        a_spec = _stacked((tk, tm), lambda i, j, kk: (kk, i), a_sel)
        b_spec = _stacked((tk, tn), lambda i, j, kk: (kk + b_k_off, j), b_sel)
        dims = _TN
    assert m % tm == 0 and n % tn == 0 and k % tk == 0, (name, a.shape, b.shape)
    nk = k // tk
    has_res = res is not None
    if out_slab is None:
        o_spec = pl.BlockSpec((tm, tn), lambda i, j, kk: (i, j + out_col_off))
        out_shape = jax.ShapeDtypeStruct((m, n if out_cols is None else out_cols), out_dtype)
    else:
        o_spec = _stacked((tm, tn), lambda i, j, kk: (i, j + out_col_off), out_slab[0])
        out_shape = jax.ShapeDtypeStruct((out_slab[1], m, n if out_cols is None else out_cols), out_dtype)
    r_spec = pl.BlockSpec((tm, tn), lambda i, j, kk: (i, j))
    n_in = 2 + has_res + (out_into is not None)

    def body(*refs):
        a_ref, b_ref = refs[0], refs[1]
        r_ref = refs[2] if has_res else None
        o_ref = refs[n_in]
        p = _dot(a_ref[...], b_ref[...], dims)

        def finish(acc):
            y = acc * alpha if alpha != 1.0 else acc
            if has_res:
                y = y + r_ref[...].astype(F32)
            o_ref[...] = y.astype(o_ref.dtype)

        if nk == 1:
            finish(p)
        else:
            acc_ref = refs[n_in + 1]
            kk = pl.program_id(2)

            @pl.when(kk == 0)
            def _():
                acc_ref[...] = p

            @pl.when(kk > 0)
            def _():
                acc_ref[...] += p

            @pl.when(kk == nk - 1)
            def _():
                finish(acc_ref[...])

    operands = [a, b] + ([res] if has_res else [])
    in_specs = [a_spec, b_spec] + ([r_spec] if has_res else [])
    aliases = {}
    if out_into is not None:
        aliases = {len(operands): 0}
        operands.append(out_into)
        in_specs.append(HBM)
    return pl.pallas_call(
        body, name=name, grid=(m // tm, n // tn, nk), in_specs=in_specs, out_specs=o_spec, out_shape=out_shape,
        scratch_shapes=[pltpu.VMEM((tm, tn), F32)] if nk > 1 else [], input_output_aliases=aliases,
        compiler_params=_params(dimension_semantics=("parallel", "parallel", "arbitrary")),
    )(*operands)


def _mm_swiglu_fwd(h, w_up, *, tm, tn, name):
    m, k = h.shape
    n = w_up.shape[1] // 2
    h_spec = pl.BlockSpec((tm, k), lambda i, j: (i, 0))
    wg_spec = pl.BlockSpec((k, tn), lambda i, j: (0, j))
    wu_spec = pl.BlockSpec((k, tn), lambda i, j: (0, j + n // tn))
    o_spec = pl.BlockSpec((tm, tn), lambda i, j: (i, j))

    def body(h_ref, wg_ref, wu_ref, a_ref, g_ref, u_ref):
        hb = h_ref[...]
        g = _dot(hb, wg_ref[...], _NN)
        u = _dot(hb, wu_ref[...], _NN)
        a_ref[...] = (g * jax.nn.sigmoid(g) * u).astype(BF16)
        g_ref[...] = g.astype(BF16)
        u_ref[...] = u.astype(BF16)

    out = jax.ShapeDtypeStruct((m, n), BF16)
    return pl.pallas_call(
        body, name=name, grid=(m // tm, n // tn), in_specs=[h_spec, wg_spec, wu_spec],
        out_specs=[o_spec] * 3, out_shape=[out] * 3,
        compiler_params=_params(dimension_semantics=("parallel", "parallel")),
    )(h, w_up, w_up)


def _mm_swiglu_bwd(dy, w_down, gate, up, *, alpha, tm, tn, name):
    m, k = dy.shape
    n = w_down.shape[0]
    dy_spec = pl.BlockSpec((tm, k), lambda i, j: (i, 0))
    w_spec = pl.BlockSpec((tn, k), lambda i, j: (j, 0))
    o_spec = pl.BlockSpec((tm, tn), lambda i, j: (i, j))

    def body(dy_ref, w_ref, g_ref, u_ref, dg_ref, du_ref):
        da = _dot(dy_ref[...], w_ref[...], _NT) * alpha
        g = g_ref[...].astype(F32)
        u = u_ref[...].astype(F32)
        sg = jax.nn.sigmoid(g)
        dg_ref[...] = (da * u * (sg * (1.0 + g * (1.0 - sg)))).astype(BF16)
        du_ref[...] = (da * (g * sg)).astype(BF16)

    out = jax.ShapeDtypeStruct((m, n), BF16)
    return pl.pallas_call(
        body, name=name, grid=(m // tm, n // tn), in_specs=[dy_spec, w_spec, o_spec, o_spec],
        out_specs=[o_spec] * 2, out_shape=[out] * 2,
        compiler_params=_params(dimension_semantics=("parallel", "parallel")),
    )(dy, w_down, gate, up)


def _rms_fwd(x, g, *, tt, name):
    t, d = x.shape

    def body(x_ref, g_ref, h_ref):
        xv = x_ref[...]
        rstd = lax.rsqrt(jnp.mean(xv * xv, axis=1, keepdims=True) + RMS_EPS)
        h_ref[...] = (xv * rstd * g_ref[...]).astype(BF16)

    return pl.pallas_call(
        body, name=name, grid=(t // tt,),
        in_specs=[pl.BlockSpec((tt, d), lambda i: (i, 0)), pl.BlockSpec((1, d), lambda i: (0, 0))],
        out_specs=pl.BlockSpec((tt, d), lambda i: (i, 0)), out_shape=jax.ShapeDtypeStruct((t, d), BF16),
        compiler_params=_params(dimension_semantics=("parallel",)),
    )(x, g)


def _rms_bwd(dh, x, g, dres, *, tt, name):
    t, d = x.shape

    def body(dh_ref, x_ref, g_ref, r_ref, dx_ref, dxb_ref, dg_ref):
        xv = x_ref[...]
        rstd = lax.rsqrt(jnp.mean(xv * xv, axis=1, keepdims=True) + RMS_EPS)
        xhat = xv * rstd
        dhv = dh_ref[...]
        dxhat = dhv * g_ref[...]
        dx = r_ref[...] + rstd * (dxhat - xhat * jnp.mean(dxhat * xhat, axis=1, keepdims=True))
        dx_ref[...] = dx
        dxb_ref[...] = dx.astype(BF16)

        @pl.when(pl.program_id(0) == 0)
        def _():
            dg_ref[...] = jnp.zeros_like(dg_ref)

        dg_ref[...] += jnp.sum(dhv * xhat, axis=0, keepdims=True)

    row = pl.BlockSpec((tt, d), lambda i: (i, 0))
    vec = pl.BlockSpec((1, d), lambda i: (0, 0))
    return pl.pallas_call(
        body, name=name, grid=(t // tt,), in_specs=[row, row, vec, row], out_specs=[row, row, vec],
        out_shape=[jax.ShapeDtypeStruct((t, d), F32), jax.ShapeDtypeStruct((t, d), BF16), jax.ShapeDtypeStruct((1, d), F32)],
        compiler_params=_params(dimension_semantics=("arbitrary",)),
    )(dh, x, g, dres)


def _final_loss(x, g, target, *, tt, name):
    t, d = x.shape

    def body(x_ref, g_ref, t_ref, dx_ref, dxb_ref, dg_ref, loss_ref):
        xv = x_ref[...]
        gv = g_ref[...]
        rstd = lax.rsqrt(jnp.mean(xv * xv, axis=1, keepdims=True) + RMS_EPS)
        xhat = xv * rstd
        err = xhat * gv - t_ref[...]
        dy = err * (1.0 / d)
        dxhat = dy * gv
        dx = rstd * (dxhat - xhat * jnp.mean(dxhat * xhat, axis=1, keepdims=True))
        dx_ref[...] = dx
        dxb_ref[...] = dx.astype(BF16)

        @pl.when(pl.program_id(0) == 0)
        def _():
            dg_ref[...] = jnp.zeros_like(dg_ref)
            loss_ref[...] = jnp.zeros_like(loss_ref)

        dg_ref[...] += jnp.sum(dy * xhat, axis=0, keepdims=True)
        part = 0.5 * jnp.sum(jnp.mean(err * err, axis=1, keepdims=True), axis=0, keepdims=True)
        loss_ref[...] += jnp.broadcast_to(part, loss_ref.shape)

    row = pl.BlockSpec((tt, d), lambda i: (i, 0))
    vec = pl.BlockSpec((1, d), lambda i: (0, 0))
    one = pl.BlockSpec((1, LANES), lambda i: (0, 0))
    return pl.pallas_call(
        body, name=name, grid=(t // tt,), in_specs=[row, vec, row], out_specs=[row, row, vec, one],
        out_shape=[jax.ShapeDtypeStruct((t, d), F32), jax.ShapeDtypeStruct((t, d), BF16), jax.ShapeDtypeStruct((1, d), F32),
                   jax.ShapeDtypeStruct((1, LANES), F32)],
        compiler_params=_params(dimension_semantics=("arbitrary",)),
    )(x, g, target)


def _swap_halves(x):
    lane = lax.broadcasted_iota(jnp.int32, x.shape, 1)
    return jnp.where((lane // 32) % 2 == 0, pltpu.roll(x, 96, 1), pltpu.roll(x, 32, 1))


def _rope_tables(s):
    half = HEAD_DIM // 2
    inv_freq = ROPE_THETA ** (-jnp.arange(half, dtype=F32) / half)
    ang = jnp.arange(s).astype(F32)[:, None] * inv_freq[None, :]
    cos, sin = jnp.cos(ang), jnp.sin(ang)
    return jnp.tile(cos, (1, 4)), jnp.concatenate([-sin, sin, -sin, sin], axis=1)


def _dilation_of_tile(p):
    dilated = p < 3 * DIL_HEADS // 2
    g = (p % (DIL_HEADS // 2)) // (DIL_GROUP_HEADS // 2)
    return [(dilated & (g == gi)) | (jnp.logical_not(dilated) if gi == 0 else False) for gi in range(len(DILATIONS))]


def _residue_major(ref, d):
    s = ref.shape[0]
    if d == 1:
        return ref[...]
    return jnp.concatenate([ref[pl.ds(r, s // d, stride=d), :] for r in range(d)], axis=0)


def _split_heads(proj, cos4, sin4, *, n_pairs, rot_pairs, scale_ranges, name):
    b, s, _ = proj.shape

    def body(x_ref, c_ref, s_ref, o_ref):
        p = pl.program_id(1)
        is_q = functools.reduce(jnp.logical_or, [(p >= lo) & (p < hi) for lo, hi in scale_ranges])
        scale = jnp.where(is_q, QK_SCALE, 1.0)

        def put(y):
            o_ref[0] = y[:, :HEAD_DIM].astype(BF16)
            o_ref[1] = y[:, HEAD_DIM:].astype(BF16)

        for d, in_group in zip(DILATIONS, _dilation_of_tile(p)):
            @pl.when(in_group & (p < rot_pairs))
            def _(d=d):
                x = _residue_major(x_ref, d)
                put((x * _residue_major(c_ref, d) + _swap_halves(x) * _residue_major(s_ref, d)) * scale)

            @pl.when(in_group & (p >= rot_pairs))
            def _(d=d):
                put(_residue_major(x_ref, d) * scale)

    tab = pl.BlockSpec((s, LANES), lambda bi, p: (0, 0))
    return pl.pallas_call(
        body, name=name, grid=(b, n_pairs),
        in_specs=[pl.BlockSpec((None, s, LANES), lambda bi, p: (bi, 0, p)), tab, tab],
        out_specs=pl.BlockSpec((None, 2, s, HEAD_DIM), lambda bi, p: (bi, p, 0, 0)),
        out_shape=jax.ShapeDtypeStruct((b, 2 * n_pairs, s, HEAD_DIM), BF16),
        compiler_params=_params(dimension_semantics=("parallel", "parallel")),
    )(proj, cos4, sin4)


def _merge_heads(dheads, cos4, sin4, *, heads_per_row, rot_pairs, scale_pairs, dilated, out_cols, tile_off, into, name):
    b, hpr, r, s, _ = dheads.shape
    n_pairs = hpr * r // 2
    ppr = hpr // 2

    def body(d_ref, c_ref, s_ref, *rest):
        o_ref, t_ref = rest[-2:]
        p = pl.program_id(1)
        scale = jnp.where(p < scale_pairs, QK_SCALE, 1.0)

        def tokens(d):
            dy = jnp.concatenate([d_ref[0], d_ref[1]], axis=1)
            if d == 1:
                return dy
            for res in range(d):
                t_ref[pl.ds(res, s // d, stride=d), :] = dy[res * (s // d):(res + 1) * (s // d), :]
            return t_ref[...]

        groups = _dilation_of_tile(p) if dilated else [p >= 0]
        for d, in_group in zip(DILATIONS, groups):
            @pl.when(in_group & (p < rot_pairs))
            def _(d=d):
                dy = tokens(d)
                o_ref[...] = ((dy * c_ref[...] - _swap_halves(dy) * s_ref[...]) * scale).astype(BF16)

            @pl.when(in_group & (p >= rot_pairs))
            def _(d=d):
                o_ref[...] = (tokens(d) * scale).astype(BF16)

    tab = pl.BlockSpec((s, LANES), lambda bi, p: (0, 0))
    operands = [dheads, cos4, sin4] + ([] if into is None else [into])
    return pl.pallas_call(
        body, name=name, grid=(b, n_pairs),
        in_specs=[pl.BlockSpec((None, 2, None, s, HEAD_DIM), lambda bi, p: (bi, p % ppr, p // ppr, 0, 0)), tab, tab]
        + ([] if into is None else [HBM]),
        out_specs=pl.BlockSpec((None, s, LANES), lambda bi, p: (bi, 0, p + tile_off)),
        out_shape=jax.ShapeDtypeStruct((b, s, out_cols), BF16),
        input_output_aliases={} if into is None else {3: 0},
        scratch_shapes=[pltpu.VMEM((s, LANES), F32)],
        compiler_params=_params(dimension_semantics=("parallel", "parallel")),
    )(*operands)


DIL_TQ = 256


def _pair_half(ref, odd):
    return jnp.where(odd, ref[:, HEAD_DIM:], ref[:, :HEAD_DIM])


def _dil_block(g, s):
    run = s // DILATIONS[g]
    return DIL_TQ if run <= DIL_TQ else min(run, DIL_TQ + 2 * LANES)


def _dil_keys(g, q0, s):
    run = max(s // DILATIONS[g], DIL_TQ)
    lo = (q0 // run) * run
    return pl.multiple_of(jnp.clip(q0 - LANES, lo, lo + run - _dil_block(g, s)), LANES)


def _dil_band(g, q0, start, shape, s):
    row = q0 + lax.broadcasted_iota(jnp.int32, shape, 0)
    col = start + lax.broadcasted_iota(jnp.int32, shape, 1)
    ok = jnp.abs(row - col) <= DIL_HALF
    run = s // DILATIONS[g]
    if run < DIL_TQ:
        shift = run.bit_length() - 1
        ok = ok & ((row >> shift) == (col >> shift))
    return ok


def _dil_tokens(g, q0, s):
    d = DILATIONS[g]
    if d == 1:
        return [(0, DIL_TQ, pl.ds(q0, DIL_TQ))]
    run = s // d
    n = min(run, DIL_TQ)
    return [(lo, n, pl.ds(((q0 + lo) % run) * d + (q0 + lo) // run, n, stride=d)) for lo in range(0, DIL_TQ, n)]


def _dil_gather(ref, pieces):
    return jnp.concatenate([ref[rows, :] for _, _, rows in pieces], axis=0) if len(pieces) > 1 else ref[pieces[0][2], :]


def _dil_head_spec(part, g, s):
    return pl.BlockSpec((None, None, s, HEAD_DIM), lambda b, j: (b, part * DIL_HEADS + g * DIL_GROUP_HEADS + j, 0, 0))


def _dil_attn_fwd(heads, *, name):
    b, _, s, _ = heads.shape
    n_g = len(DILATIONS)

    def body(*refs):
        qkv = refs[:3 * n_g]
        o_ref, l_ref, og_ref, lg_ref = refs[3 * n_g:]
        for g in range(n_g):
            q_ref, k_ref, v_ref = qkv[3 * g:3 * g + 3]
            width = _dil_block(g, s)

            def step(i, carry, g=g, q_ref=q_ref, k_ref=k_ref, v_ref=v_ref, width=width):
                q0 = pl.multiple_of(i * DIL_TQ, DIL_TQ)
                start = _dil_keys(g, q0, s)
                sc = _dot(q_ref[pl.ds(q0, DIL_TQ), :], k_ref[pl.ds(start, width), :], _NT)
                sc = jnp.where(_dil_band(g, q0, start, sc.shape, s), sc, NEG_INF)
                m = jnp.max(sc, axis=1, keepdims=True)
                p = jnp.exp(sc - m)
                den = jnp.sum(p, axis=1, keepdims=True)
                o = _dot(p.astype(BF16), v_ref[pl.ds(start, width), :], _NN) / den
                lse = m + jnp.log(den)
                for lo, n, rows in _dil_tokens(g, q0, s):
                    og_ref[g, rows, :] = o[lo:lo + n]
                    lg_ref[g, rows, :] = lse[lo:lo + n]
                return carry

            lax.fori_loop(0, s // DIL_TQ, step, 0, unroll=BLOCKS_IN_FLIGHT)
        lses = [lg_ref[g] for g in range(n_g)]
        m = functools.reduce(jnp.maximum, lses)
        ws = [jnp.exp(l - m) for l in lses]
        den = functools.reduce(jnp.add, ws)
        mixed = (functools.reduce(jnp.add, [w * og_ref[g] for g, w in enumerate(ws)]) / den).astype(o_ref.dtype)
        l_ref[...] = m + jnp.log(den)

        @pl.when(pl.program_id(1) % 2 == 0)
        def _():
            o_ref[:, :HEAD_DIM] = mixed

        @pl.when(pl.program_id(1) % 2 == 1)
        def _():
            o_ref[:, HEAD_DIM:] = mixed

    out = pl.BlockSpec((None, s, 2 * HEAD_DIM), lambda bi, j: (bi, 0, j // 2))
    lse = pl.BlockSpec((None, None, s, 1), lambda bi, j: (bi, j, 0, 0))
    return pl.pallas_call(
        body, name=name, grid=(b, DIL_GROUP_HEADS),
        in_specs=[_dil_head_spec(part, g, s) for g in range(n_g) for part in range(3)],
        out_specs=[out, lse],
        out_shape=[jax.ShapeDtypeStruct((b, s, DIL_GROUP_HEADS * HEAD_DIM), BF16),
                   jax.ShapeDtypeStruct((b, DIL_GROUP_HEADS, s, 1), F32)],
        scratch_shapes=[pltpu.VMEM((n_g, s, HEAD_DIM), F32), pltpu.VMEM((n_g, s, 1), F32)],
        compiler_params=_params(dimension_semantics=("parallel", "arbitrary")),
    )(*([heads] * (3 * n_g)))


def _dil_attn_bwd(heads, out, lse, dout, *, name):
    b, _, s, _ = heads.shape
    n_g = len(DILATIONS)

    def body(*refs):
        qkv = refs[:3 * n_g]
        o_ref, l_ref, do_ref, d_ref, delta_ref = refs[3 * n_g:]
        d_ref[...] = jnp.zeros_like(d_ref)
        o_half = _pair_half(o_ref, pl.program_id(1) % 2 == 1)
        delta_ref[...] = jnp.sum(do_ref[...] * o_half.astype(F32), axis=1, keepdims=True)
        for g in range(n_g):
            q_ref, k_ref, v_ref = qkv[3 * g:3 * g + 3]
            width = _dil_block(g, s)

            def step(i, carry, g=g, q_ref=q_ref, k_ref=k_ref, v_ref=v_ref, width=width):
                q0 = pl.multiple_of(i * DIL_TQ, DIL_TQ)
                start = _dil_keys(g, q0, s)
                win = pl.ds(start, width)
                pieces = _dil_tokens(g, q0, s)
                do_b = _dil_gather(do_ref, pieces).astype(BF16)
                q, k, v = q_ref[pl.ds(q0, DIL_TQ), :], k_ref[win, :], v_ref[win, :]
                sc = _dot(q, k, _NT)
                p = jnp.where(_dil_band(g, q0, start, sc.shape, s), jnp.exp(sc - _dil_gather(l_ref, pieces)), 0.0)
                ds = (p * (_dot(do_b, v, _NT) - _dil_gather(delta_ref, pieces))).astype(BF16)
                d_ref[g, pl.ds(q0, DIL_TQ), :] = _dot(ds, k, _NN)
                d_ref[n_g + g, win, :] += _dot(ds, q, _TN)
                d_ref[2 * n_g + g, win, :] += _dot(p.astype(BF16), do_b, _TN)
                return carry

            lax.fori_loop(0, s // DIL_TQ, step, 0, unroll=BLOCKS_IN_FLIGHT)

    per_head = lambda bi, j: (bi, j, 0, 0)
    return pl.pallas_call(
        body, name=name, grid=(b, DIL_GROUP_HEADS),
        in_specs=[_dil_head_spec(part, g, s) for g in range(n_g) for part in range(3)]
        + [pl.BlockSpec((None, s, 2 * HEAD_DIM), lambda bi, j: (bi, 0, j // 2)), pl.BlockSpec((None, None, s, 1), per_head),
           pl.BlockSpec((None, None, s, HEAD_DIM), per_head)],
        out_specs=pl.BlockSpec((None, None, 3 * n_g, s, HEAD_DIM), lambda bi, j: (bi, j, 0, 0, 0)),
        out_shape=jax.ShapeDtypeStruct((b, DIL_GROUP_HEADS, 3 * n_g, s, HEAD_DIM), F32),
        scratch_shapes=[pltpu.VMEM((s, 1), F32)],
        compiler_params=_params(dimension_semantics=("parallel", "parallel")),
    )(*([heads] * (3 * n_g)), out, lse, dout)


NA_BIAS_ROWS = 2 * NA_ROWS - 1
NA_BIAS_COLS = 2 * NA_COLS - 1
NA_BLOCK = 4
NA_SPAN = NA_ROWS + NA_BLOCK - 1
NA_Q = NA_BLOCK * GRID_W
NA_KEYS = NA_SPAN * GRID_W
NA_FORMS = 3


def _na_onehot():
    c = np.arange(GRID_W)[:, None]
    k = np.arange(GRID_W)[None, :]
    lo = np.clip(c - NA_COLS // 2, 0, GRID_W - NA_COLS)
    valid = (k >= lo) & (k < lo + NA_COLS)
    onehot = np.zeros((GRID_W, GRID_W, LANES), np.float32)
    cc, kk = np.nonzero(valid)
    onehot[cc, kk, kk - cc + NA_COLS - 1] = 1.0
    return onehot.reshape(GRID_W * GRID_W, LANES), valid.reshape(1, GRID_W * GRID_W)


def _na_block_rows(n_rows):
    table = np.full((NA_FORMS, NA_BLOCK, NA_SPAN), NA_BIAS_ROWS, np.int64)
    n_blocks = n_rows // NA_BLOCK
    for form, ib in enumerate((0, 1, n_blocks - 1)):
        base = min(max(NA_BLOCK * ib - NA_ROWS // 2, 0), n_rows - NA_SPAN)
        for rl in range(NA_BLOCK):
            r = NA_BLOCK * ib + rl
            row_lo = min(max(r - NA_ROWS // 2, 0), n_rows - NA_ROWS)
            for kl in range(NA_SPAN):
                if row_lo <= base + kl < row_lo + NA_ROWS:
                    table[form, rl, kl] = base + kl - r + NA_ROWS - 1
    return table


def _na_block(ib, n_rows):
    n_blocks = n_rows // NA_BLOCK
    base = jnp.clip(NA_BLOCK * ib - NA_ROWS // 2, 0, n_rows - NA_SPAN)
    return base, jnp.where(ib == 0, 0, jnp.where(ib == n_blocks - 1, 2, 1))


def _na_expand_bias(rel_bias, *, name):
    l, h, nr, nc = rel_bias.shape
    onehot, valid = _na_onehot()
    rb = jnp.pad(rel_bias, ((0, 0), (0, 0), (0, 1), (0, LANES - nc))).reshape(l * h * (nr + 1), LANES)
    live = jnp.asarray(np.tile(np.arange(nr + 1) < nr, l * h).astype(np.float32)[:, None])

    def body(rb_ref, oh_ref, valid_ref, live_ref, e_ref):
        e = lax.dot_general(rb_ref[...], oh_ref[...], _NT, precision=lax.Precision.HIGHEST, preferred_element_type=F32)
        e_ref[...] = jnp.where((valid_ref[...] > 0) & (live_ref[...] > 0), e, NEG_INF)

    e = pl.pallas_call(
        body, name=name, out_shape=jax.ShapeDtypeStruct((l * h * (nr + 1), GRID_W * GRID_W), F32), compiler_params=_params(),
    )(rb, jnp.asarray(onehot), jnp.asarray(valid.astype(np.float32)), live)
    return e.reshape(l, h, nr + 1, GRID_W, GRID_W)


def _na_collapse_bias(de, *, name):
    b, h = de.shape[:2]
    onehot, _ = _na_onehot()
    rows = h * NA_BIAS_ROWS

    def diag(e_ref, oh_ref, o_ref):
        e = e_ref[0]
        for bi in range(1, b):
            e = e + e_ref[bi]
        o_ref[...] = lax.dot_general(e, oh_ref[...], _NN, precision=lax.Precision.HIGHEST, preferred_element_type=F32)

    drb = pl.pallas_call(
        diag, name=name, out_shape=jax.ShapeDtypeStruct((rows, LANES), F32), compiler_params=_params(),
    )(de.reshape(b, rows, GRID_W * GRID_W), jnp.asarray(onehot))
    return drb[:, :NA_BIAS_COLS].reshape(h, NA_BIAS_ROWS, NA_BIAS_COLS)


def _na_tiles(n_rows):
    table = _na_block_rows(n_rows)
    return [(f, rl, kl, int(table[f, rl, kl])) for f in range(NA_FORMS) for rl in range(NA_BLOCK) for kl in range(NA_SPAN)]


def _na_tile(ref, form, rl, kl):
    return ref.at[form, rl * GRID_W:(rl + 1) * GRID_W, kl * GRID_W:(kl + 1) * GRID_W]


def _na_head_spec(part, first, s):
    return pl.BlockSpec((None, None, s, HEAD_DIM), lambda b, h: (b, first + part * NA_HEADS + h, 0, 0))


def _na_attn_fwd(heads, bias, *, first, name):
    b, _, s, _ = heads.shape
    n_rows = s // GRID_W
    tiles = _na_tiles(n_rows)

    def body(q_ref, k_ref, v_ref, e_ref, o_ref, l_ref, b_ref):
        for form, rl, kl, i in tiles:
            _na_tile(b_ref, form, rl, kl)[...] = e_ref[i]

        def step(ib, carry):
            base, form = _na_block(ib, n_rows)
            rows = pl.ds(pl.multiple_of(ib * NA_Q, NA_Q), NA_Q)
            win = pl.ds(pl.multiple_of(base * GRID_W, GRID_W), NA_KEYS)
            sc = _dot(q_ref[rows, :], k_ref[win, :], _NT) + b_ref[form]
            m = jnp.max(sc, axis=1, keepdims=True)
            p = jnp.exp(sc - m)
            den = jnp.sum(p, axis=1, keepdims=True)
            o_ref[rows, :] = (_dot(p.astype(BF16), v_ref[win, :], _NN) / den).astype(o_ref.dtype)
            l_ref[rows, :] = m + jnp.log(den)
            return carry

        lax.fori_loop(0, n_rows // NA_BLOCK, step, 0, unroll=BLOCKS_IN_FLIGHT)

    per_head = lambda bi, h: (bi, h, 0, 0)
    return pl.pallas_call(
        body, name=name, grid=(b, NA_HEADS),
        in_specs=[_na_head_spec(part, first, s) for part in range(3)]
        + [pl.BlockSpec((None, NA_BIAS_ROWS + 1, GRID_W, GRID_W), lambda bi, h: (h, 0, 0, 0))],
        out_specs=[pl.BlockSpec((None, None, s, HEAD_DIM), per_head), pl.BlockSpec((None, None, s, 1), per_head)],
        out_shape=[jax.ShapeDtypeStruct((b, NA_HEADS, s, HEAD_DIM), BF16), jax.ShapeDtypeStruct((b, NA_HEADS, s, 1), F32)],
        scratch_shapes=[pltpu.VMEM((NA_FORMS, NA_Q, NA_KEYS), F32)],
        compiler_params=_params(dimension_semantics=("parallel", "parallel")),
    )(heads, heads, heads, bias)


def _na_attn_bwd(heads, bias, out, lse, dout, *, first, name):
    b, _, s, _ = heads.shape
    n_rows = s // GRID_W
    tiles = _na_tiles(n_rows)

    def body(q_ref, k_ref, v_ref, e_ref, o_ref, l_ref, do_ref, d_ref, de_ref, b_ref, db_ref):
        for form, rl, kl, i in tiles:
            _na_tile(b_ref, form, rl, kl)[...] = e_ref[i]
        d_ref[...] = jnp.zeros_like(d_ref)
        db_ref[...] = jnp.zeros_like(db_ref)

        def step(ib, carry):
            base, form = _na_block(ib, n_rows)
            rows = pl.ds(pl.multiple_of(ib * NA_Q, NA_Q), NA_Q)
            win = pl.ds(pl.multiple_of(base * GRID_W, GRID_W), NA_KEYS)
            q, k, v = q_ref[rows, :], k_ref[win, :], v_ref[win, :]
            do = do_ref[rows, :]
            delta = jnp.sum(do * o_ref[rows, :].astype(F32), axis=1, keepdims=True)
            do_b = do.astype(BF16)
            p = jnp.exp(_dot(q, k, _NT) + b_ref[form] - l_ref[rows, :])
            ds = p * (_dot(do_b, v, _NT) - delta)
            db_ref[form] += ds
            ds_b = ds.astype(BF16)
            d_ref[0, rows, :] = _dot(ds_b, k, _NN)
            d_ref[1, win, :] += _dot(ds_b, q, _TN)
            d_ref[2, win, :] += _dot(p.astype(BF16), do_b, _TN)
            return carry

        lax.fori_loop(0, n_rows // NA_BLOCK, step, 0, unroll=BLOCKS_IN_FLIGHT)
        acc = [None] * NA_BIAS_ROWS
        for form, rl, kl, i in tiles:
            if i < NA_BIAS_ROWS:
                t = _na_tile(db_ref, form, rl, kl)[...]
                acc[i] = t if acc[i] is None else acc[i] + t
        for i in range(NA_BIAS_ROWS):
            de_ref[i] = acc[i]

    per_head = lambda bi, h: (bi, h, 0, 0)
    return pl.pallas_call(
        body, name=name, grid=(b, NA_HEADS),
        in_specs=[_na_head_spec(part, first, s) for part in range(3)]
        + [pl.BlockSpec((None, NA_BIAS_ROWS + 1, GRID_W, GRID_W), lambda bi, h: (h, 0, 0, 0)),
           pl.BlockSpec((None, None, s, HEAD_DIM), per_head), pl.BlockSpec((None, None, s, 1), per_head),
           pl.BlockSpec((None, None, s, HEAD_DIM), per_head)],
        out_specs=[pl.BlockSpec((None, None, 3, s, HEAD_DIM), lambda bi, h: (bi, h, 0, 0, 0)),
                   pl.BlockSpec((None, None, NA_BIAS_ROWS, GRID_W, GRID_W), lambda bi, h: (bi, h, 0, 0, 0))],
        out_shape=[jax.ShapeDtypeStruct((b, NA_HEADS, 3, s, HEAD_DIM), F32),
                   jax.ShapeDtypeStruct((b, NA_HEADS, NA_BIAS_ROWS, GRID_W, GRID_W), F32)],
        scratch_shapes=[pltpu.VMEM((NA_FORMS, NA_Q, NA_KEYS), F32), pltpu.VMEM((NA_FORMS, NA_Q, NA_KEYS), F32)],
        compiler_params=_params(dimension_semantics=("parallel", "parallel")),
    )(heads, heads, heads, bias, out, lse, dout)


GATE_TILE = 256


def _gate_fwd(proj, z, *, gate_col, tt, name):
    _, t, d = z.shape
    nj = d // GATE_TILE
    c0 = gate_col // GATE_TILE

    def body(ga_ref, gb_ref, za_ref, zb_ref, o_ref):
        o_ref[...] = (jax.nn.sigmoid(ga_ref[...]) * za_ref[...] + jax.nn.sigmoid(gb_ref[...]) * zb_ref[...]).astype(BF16)

    return pl.pallas_call(
        body, name=name, grid=(t // tt, nj),
        in_specs=[pl.BlockSpec((tt, GATE_TILE), lambda i, j: (i, c0 + j)),
                  pl.BlockSpec((tt, GATE_TILE), lambda i, j: (i, c0 + nj + j)),
                  pl.BlockSpec((None, tt, GATE_TILE), lambda i, j: (0, i, j)),
                  pl.BlockSpec((None, tt, GATE_TILE), lambda i, j: (1, i, j))],
        out_specs=pl.BlockSpec((tt, GATE_TILE), lambda i, j: (i, j)), out_shape=jax.ShapeDtypeStruct((t, d), BF16),
        compiler_params=_params(dimension_semantics=("parallel", "parallel")),
    )(proj, proj, z, z)


def _gate_bwd(dm, proj, z, *, gate_col, tt, name):
    _, t, d = z.shape
    nj = d // GATE_TILE
    c0 = gate_col // GATE_TILE

    def body(dm_ref, g_ref, z_ref, dz_ref, dg_ref):
        dmv = dm_ref[...]
        sg = jax.nn.sigmoid(g_ref[...])
        dz_ref[...] = (dmv * sg).astype(BF16)
        dg_ref[...] = (dmv * z_ref[...] * sg * (1.0 - sg)).astype(BF16)

    return pl.pallas_call(
        body, name=name, grid=(t // tt, 2 * nj),
        in_specs=[pl.BlockSpec((tt, GATE_TILE), lambda i, j: (i, j % nj)),
                  pl.BlockSpec((tt, GATE_TILE), lambda i, j: (i, c0 + j)),
                  pl.BlockSpec((None, tt, GATE_TILE), lambda i, j: (j // nj, i, j % nj))],
        out_specs=[pl.BlockSpec((None, tt, GATE_TILE), lambda i, j: (j // nj, i, j % nj)),
                   pl.BlockSpec((tt, GATE_TILE), lambda i, j: (i, c0 + j))],
        out_shape=[jax.ShapeDtypeStruct((2, t, d), BF16), jax.ShapeDtypeStruct(proj.shape, BF16)],
        compiler_params=_params(dimension_semantics=("parallel", "parallel")),
    )(dm, proj, z)


def _adamw(w, g, m, v, *, name):
    shape = w.shape
    if w.ndim == 3:
        w2, g2, m2, v2 = w, g, m, v
    else:
        w2, g2, m2, v2 = (t.reshape(1, -1, shape[-1]) for t in (w, g, m, v))
    lead, rows, cols = w2.shape
    tr = rows
    for cand in (512, 256, 128, 64, 32, 16, 8):
        if rows % cand == 0:
            tr = cand
            break

    def body(w_ref, g_ref, m_ref, v_ref, d_ref, nm_ref, nv_ref):
        gv = g_ref[...]
        nm = ADAM_B1 * m_ref[...] + (1.0 - ADAM_B1) * gv
        nv = ADAM_B2 * v_ref[...] + (1.0 - ADAM_B2) * (gv * gv)
        m_hat = nm / (1.0 - ADAM_B1 ** ADAM_STEP)
        v_hat = nv / (1.0 - ADAM_B2 ** ADAM_STEP)
        d_ref[...] = -ADAM_LR * (m_hat / (jnp.sqrt(v_hat) + ADAM_EPS) + ADAM_WD * w_ref[...])
        nm_ref[...] = nm
        nv_ref[...] = nv

    blk = pl.BlockSpec((None, tr, cols), lambda l, i: (l, i, 0))
    out = jax.ShapeDtypeStruct((lead, rows, cols), F32)
    res = pl.pallas_call(
        body, name=name, grid=(lead, rows // tr), in_specs=[blk] * 4, out_specs=[blk] * 3, out_shape=[out] * 3,
        compiler_params=_params(dimension_semantics=("parallel", "parallel")),
    )(w2, g2, m2, v2)
    return tuple(t.reshape(shape) for t in res)


def _my_place():
    return lax.axis_index("x"), lax.axis_index("y"), lax.axis_index("c")


def _other_chips(x, y):
    return [(1 - x, y), (x, 1 - y), (1 - x, 1 - y)]


def _chip_no(chip):
    return 2 * chip[0] + chip[1]


def _window(ref, kind, size, chip, lead):
    if kind == "col":
        return ref.at[(*lead, slice(None), pl.ds(pl.multiple_of(chip * size, LANES), size))]
    if kind == "row":
        return ref.at[(*lead, pl.ds(pl.multiple_of(chip * size, BF16_ROWS), size), slice(None))]
    shard = size + HEAD_DIM
    if kind == "win_main":
        return ref.at[(*lead, slice(None), pl.ds(pl.multiple_of(chip * shard + HEAD_DIM * (chip % 2), LANES), size))]
    assert kind == "win_strad"
    return ref.at[(*lead, slice(None), pl.ds(pl.multiple_of(size + 2 * shard * (chip // 2), LANES), LANES))]


def _full_shape(shard, kind):
    _, k, n = shard.shape
    return {"col": (k, N_CHIPS * n), "row": (N_CHIPS * k, n), "win_main": (k, N_CHIPS * (n + HEAD_DIM)),
            "slot": (N_CHIPS, k, n)}[kind]


def _place_own(shard, kind, layer, *, name):
    _, k, n = shard.shape
    tr = _div_tile(k, 512, BF16_ROWS)
    tc = LANES if kind == "win_main" else n
    mine = 2 * lax.axis_index("x") + lax.axis_index("y")
    row0 = mine * (k // tr) if kind == "row" else 0
    col0 = {"col": mine, "row": 0, "slot": 0, "win_main": (mine * (n + HEAD_DIM) + HEAD_DIM * (mine % 2)) // LANES}[kind]
    scalars = jnp.stack([mine, row0, col0]).astype(jnp.int32)

    def body(s_ref, i_ref, o_ref):
        o_ref[...] = i_ref[...]

    if kind == "slot":
        o_spec = pl.BlockSpec((None, tr, tc), lambda i, j, s: (s[0], i, j))
    else:
        o_spec = pl.BlockSpec((tr, tc), lambda i, j, s: (s[1] + i, s[2] + j))
    return pl.pallas_call(
        body, name=name,
        grid_spec=pltpu.PrefetchScalarGridSpec(
            num_scalar_prefetch=1, grid=(k // tr, n // tc),
            in_specs=[pl.BlockSpec((None, tr, tc), lambda i, j, s: (layer, i, j))], out_specs=o_spec),
        out_shape=jax.ShapeDtypeStruct(_full_shape(shard, kind), shard.dtype),
        compiler_params=_params(dimension_semantics=("parallel", "parallel")),
    )(scalars, shard)


class _GatherPlan:
    def __init__(self, src, dst, shapes, kinds, layer, send_sems, recv_sems):
        self.src, self.dst, self.shapes, self.kinds, self.layer = src, dst, shapes, kinds, layer
        self.send_sems, self.recv_sems = send_sems, recv_sems
        self.x, self.y, self.c = _my_place()
        self.mine = 2 * self.x + self.y
        self.chips = _other_chips(self.x, self.y)
        self.n = len(src)

    def half(self, i, chip, half):
        _, k, n = self.shapes[i]
        kind, dst, hk = self.kinds[i], self.dst[i], k // 2
        if kind == "slot":
            return dst.at[chip, pl.ds(pl.multiple_of(half * hk, BF16_ROWS), hk), :]
        if kind == "row":
            return dst.at[pl.ds(pl.multiple_of(chip * k + half * hk, BF16_ROWS), hk), :]
        col0 = chip * n if kind == "col" else chip * (n + HEAD_DIM) + HEAD_DIM * (chip % 2)
        return dst.at[pl.ds(pl.multiple_of(half * hk, BF16_ROWS), hk), pl.ds(pl.multiple_of(col0, LANES), n)]

    def _copy(self, sem, window, to, source=None):
        return pltpu.make_async_remote_copy(src_ref=window if source is None else source, dst_ref=window,
                                            send_sem=self.send_sems.at[sem], recv_sem=self.recv_sems.at[sem],
                                            device_id=to, device_id_type=MESH)

    def sends(self):
        out = []
        for k, chip in enumerate(self.chips):
            for i in range(self.n):
                hk = self.shapes[i][1] // 2
                mine = self.src[i].at[self.layer, pl.ds(pl.multiple_of(self.c * hk, BF16_ROWS), hk), :]
                out.append(self._copy(3 * i + k, self.half(i, self.mine, self.c), (*chip, self.c), source=mine))
        return out

    def arrivals(self):
        return [self._copy(3 * i + k, self.half(i, _chip_no(chip), self.c), (*chip, self.c))
                for k, chip in enumerate(self.chips) for i in range(self.n)]

    def forwards(self, first_sem):
        sibling = (self.x, self.y, 1 - self.c)
        return [self._copy(first_sem + 3 * i + k, self.half(i, _chip_no(chip), self.c), sibling)
                for k, chip in enumerate(self.chips) for i in range(self.n)]

    def forwarded(self, first_sem):
        sibling = (self.x, self.y, 1 - self.c)
        return [self._copy(first_sem + 3 * i + k, self.half(i, _chip_no(chip), 1 - self.c), sibling)
                for k, chip in enumerate(self.chips) for i in range(self.n)]


IN_HBM = pl.BlockSpec(memory_space=pltpu.HBM)
IN_SEM = pl.BlockSpec(memory_space=pltpu.SEMAPHORE)
DATAFLOW = pltpu.SideEffectType.DATAFLOW_SIDE_EFFECTING


def _gather_layer_start(shards, kinds, fulls, layer, after, *, name):
    n_w = len(shards)
    shapes = [sh.shape for sh in shards]

    def body(*refs):
        plan = _GatherPlan(refs[:n_w], refs[n_w:2 * n_w], shapes, kinds, layer, refs[2 * n_w + 1], refs[2 * n_w + 2])
        for cp in plan.sends():
            cp.start()
        token = refs[-1]
        token[...] = jnp.zeros_like(token)

    operands = [pltpu.with_memory_space_constraint(a, pltpu.HBM) for a in (*shards, *fulls)]
    res = pl.pallas_call(
        body, name=name, in_specs=[IN_HBM] * (2 * n_w) + [pl.BlockSpec(memory_space=pl.ANY)],
        out_specs=(IN_SEM, IN_SEM, *([IN_HBM] * (2 * n_w)), pl.BlockSpec(memory_space=pltpu.VMEM)),
        out_shape=(pltpu.SemaphoreType.DMA((3 * n_w,)), pltpu.SemaphoreType.DMA((3 * n_w,)),
                   *[pltpu.HBM(a.shape, a.dtype) for a in operands], jax.ShapeDtypeStruct((8, LANES), F32)),
        input_output_aliases={i: 2 + i for i in range(2 * n_w)},
        compiler_params=pltpu.CompilerParams(has_side_effects=DATAFLOW),
    )(*operands, after)
    return res[0], res[1], res[2:2 + n_w], res[2 + n_w:2 + 2 * n_w], res[-1]


def _gather_layer_wait(send_sems, recv_sems, shards, fulls, kinds, layer, after, *, name):
    n_w = len(shards)
    shapes = [sh.shape for sh in shards]

    def body(*refs):
        plan = _GatherPlan(refs[:n_w], refs[n_w:2 * n_w], shapes, kinds, layer, refs[2 * n_w], refs[2 * n_w + 1])
        for cp in plan.sends():
            cp.wait_send()
        for cp in plan.arrivals():
            cp.wait_recv()

    res = pl.pallas_call(
        body, name=name, in_specs=[IN_HBM] * (2 * n_w) + [IN_SEM, IN_SEM, pl.BlockSpec(memory_space=pl.ANY)],
        out_specs=[IN_HBM] * (2 * n_w), out_shape=[pltpu.HBM(a.shape, a.dtype) for a in (*shards, *fulls)],
        input_output_aliases={i: i for i in range(2 * n_w)},
        compiler_params=pltpu.CompilerParams(has_side_effects=DATAFLOW),
    )(*shards, *fulls, send_sems, recv_sems, after)
    return res[n_w:]


def _gather_layer_forward(shapes, kinds, fulls, *, name):
    n_w = len(fulls)

    def body(*refs):
        plan = _GatherPlan([None] * n_w, refs[n_w:2 * n_w], shapes, kinds, 0, *refs[2 * n_w:])
        passed = plan.forwards(0)
        for cp in passed:
            cp.start()
        for cp in plan.forwarded(0):
            cp.wait_recv()
        for cp in passed:
            cp.wait_send()

    return pl.pallas_call(
        body, name=name, in_specs=[HBM] * n_w, out_specs=[HBM] * n_w,
        out_shape=[jax.ShapeDtypeStruct(f.shape, f.dtype) for f in fulls],
        input_output_aliases={i: i for i in range(n_w)},
        scratch_shapes=[pltpu.SemaphoreType.DMA((3 * n_w,)), pltpu.SemaphoreType.DMA((3 * n_w,))],
    )(*fulls)


def _on_core(layer):
    return (lax.axis_index("c") == layer).astype(jnp.int32).reshape(1)


N_DEVICES = 2 * N_CHIPS


class _ScatterPlan:
    def __init__(self, src, dst, kinds, sizes, layer, send_sems, recv_sems):
        self.src, self.dst, self.kinds, self.sizes, self.layer = src, dst, kinds, sizes, layer
        self.send_sems, self.recv_sems = send_sems, recv_sems
        self.x, self.y, self.c = _my_place()
        self.mine = 2 * self.x + self.y
        self.chips = _other_chips(self.x, self.y)
        self.n = len(src)

    def _copy(self, i, k, window_of, from_chip, from_core, to):
        return pltpu.make_async_remote_copy(src_ref=_window(self.src[i], self.kinds[i], self.sizes[i], window_of, ()),
                                            dst_ref=self.dst[i].at[2 * from_chip + from_core],
                                            send_sem=self.send_sems.at[4 * i + k],
                                            recv_sem=self.recv_sems.at[2 * (4 * i + k) + from_core],
                                            device_id=to, device_id_type=MESH)

    def to_chips(self):
        return [self._copy(i, k, _chip_no(chip), self.mine, self.c, (*chip, self.layer))
                for k, chip in enumerate(self.chips) for i in range(self.n)]

    def to_sibling(self):
        return [self._copy(i, 3, self.mine, self.mine, self.c, (self.x, self.y, self.layer)) for i in range(self.n)]

    def arrivals(self):
        out = [self._copy(i, k, self.mine, _chip_no(chip), core, (*chip, core))
               for k, chip in enumerate(self.chips) for core in (0, 1) for i in range(self.n)]
        return out + [self._copy(i, 3, self.mine, self.mine, 1 - self.layer, (self.x, self.y, 1 - self.layer))
                      for i in range(self.n)]


def _slab_shape(p, kind, size):
    return (N_DEVICES,) + {"col": (p.shape[0], size), "row": (size, p.shape[1]), "win_main": (p.shape[0], size),
                           "win_strad": (p.shape[0], LANES)}[kind]


def _grads_to_chips_start(pairs, kinds, sizes, layer, *, name):
    n_w = len(pairs)

    def body(*refs):
        plan = _ScatterPlan(refs[:n_w], refs[n_w:2 * n_w], kinds, sizes, layer, refs[2 * n_w], refs[2 * n_w + 1])
        for cp in plan.to_chips():
            cp.start()

        @pl.when(plan.c != layer)
        def _():
            for cp in plan.to_sibling():
                cp.start()

        token = refs[-1]
        token[...] = jnp.zeros_like(token)

    slabs = [lax.empty(_slab_shape(p, kind, size), p.dtype) for p, kind, size in zip(pairs, kinds, sizes)]
    operands = [pltpu.with_memory_space_constraint(a, pltpu.HBM) for a in (*pairs, *slabs)]
    res = pl.pallas_call(
        body, name=name, in_specs=[IN_HBM] * (2 * n_w),
        out_specs=(IN_SEM, IN_SEM, *([IN_HBM] * (2 * n_w)), pl.BlockSpec(memory_space=pltpu.VMEM)),
        out_shape=(pltpu.SemaphoreType.DMA((4 * n_w,)), pltpu.SemaphoreType.DMA((8 * n_w,)),
                   *[pltpu.HBM(a.shape, a.dtype) for a in operands], jax.ShapeDtypeStruct((8, LANES), F32)),
        input_output_aliases={i: 2 + i for i in range(2 * n_w)},
        compiler_params=pltpu.CompilerParams(has_side_effects=DATAFLOW),
    )(*operands)
    return res[0], res[1], res[2:2 + n_w], res[2 + n_w:2 + 2 * n_w], res[-1]


def _grads_to_chips_wait(send_sems, recv_sems, pairs, slabs, kinds, sizes, layer, after, *, name):
    n_w = len(pairs)

    def body(*refs):
        plan = _ScatterPlan(refs[:n_w], refs[n_w:2 * n_w], kinds, sizes, layer, refs[2 * n_w], refs[2 * n_w + 1])
        for cp in plan.to_chips():
            cp.wait_send()

        @pl.when(plan.c != layer)
        def _():
            for cp in plan.to_sibling():
                cp.wait_send()

        @pl.when(plan.c == layer)
        def _():
            for cp in plan.arrivals():
                cp.wait_recv()

    res = pl.pallas_call(
        body, name=name, in_specs=[IN_HBM] * (2 * n_w) + [IN_SEM, IN_SEM, pl.BlockSpec(memory_space=pl.ANY)],
        out_specs=[IN_HBM] * (2 * n_w), out_shape=[pltpu.HBM(a.shape, a.dtype) for a in (*pairs, *slabs)],
        input_output_aliases={i: i for i in range(2 * n_w)},
        compiler_params=pltpu.CompilerParams(has_side_effects=DATAFLOW),
    )(*pairs, *slabs, send_sems, recv_sems, after)
    return res[:n_w], res[n_w:]


def _sum_slabs(slabs, pair, kind, size, layer, into, *, name):
    n_s, k, n = slabs.shape
    tr = _div_tile(k, 512, BF16_ROWS)
    tc = n if kind in ("col", "row") else LANES
    x, y, _ = _my_place()
    mine = 2 * x + y
    shard = size + HEAD_DIM
    row0 = mine * (k // tr) if kind == "row" else 0
    col0 = {"col": mine, "row": 0, "win_main": (mine * shard + HEAD_DIM * (mine % 2)) // LANES,
            "win_strad": (size + 2 * shard * (mine // 2)) // LANES}[kind]
    on = _on_core(layer)[0]
    scalars = jnp.stack([2 * mine + layer, row0 * on, col0 * on, on]).astype(jnp.int32)

    def body(s_ref, slab_ref, own_ref, *rest):
        o_ref = rest[-1]
        me = s_ref[0]

        @pl.when(s_ref[3] == 1)
        def _():
            acc = jnp.zeros(o_ref.shape, F32)
            for i in range(n_s):
                acc = acc + jnp.where(me == i, own_ref[...], slab_ref[i]).astype(F32)
            o_ref[...] = acc

    operands = [scalars, slabs, pair] + ([] if into is None else [into])
    return pl.pallas_call(
        body, name=name,
        grid_spec=pltpu.PrefetchScalarGridSpec(
            num_scalar_prefetch=1, grid=(k // tr, n // tc),
            in_specs=[pl.BlockSpec((n_s, tr, tc), lambda i, j, s: (0, i * s[3], j * s[3])),
                      pl.BlockSpec((tr, tc), lambda i, j, s: (s[1] + i * s[3], s[2] + j * s[3]))]
            + ([] if into is None else [HBM]),
            out_specs=pl.BlockSpec((None, tr, tc), lambda i, j, s: (layer, i * s[3], j * s[3]))),
        out_shape=jax.ShapeDtypeStruct((2, k, n), F32),
        input_output_aliases={} if into is None else {3: 0},
        compiler_params=_params(dimension_semantics=("arbitrary", "arbitrary")),
    )(*operands)


def _exchange_layers(bufs, *, name):
    n_w = len(bufs)

    def body(*refs):
        dst = refs[n_w:2 * n_w]
        send_sems, recv_sems = refs[2 * n_w:]
        x, y, c = _my_place()

        def copy(i, layer):
            return pltpu.make_async_remote_copy(src_ref=dst[i].at[layer], dst_ref=dst[i].at[layer], send_sem=send_sems.at[i],
                                                recv_sem=recv_sems.at[i], device_id=(x, y, 1 - c), device_id_type=MESH)

        sends = [copy(i, c) for i in range(n_w)]
        for cp in sends:
            cp.start()
        for i in range(n_w):
            copy(i, 1 - c).wait_recv()
        for cp in sends:
            cp.wait_send()

    return pl.pallas_call(
        body, name=name, in_specs=[HBM] * n_w, out_specs=[HBM] * n_w,
        out_shape=[jax.ShapeDtypeStruct(b.shape, b.dtype) for b in bufs],
        input_output_aliases={i: i for i in range(n_w)},
        scratch_shapes=[pltpu.SemaphoreType.DMA((n_w,)), pltpu.SemaphoreType.DMA((n_w,))],
    )(*bufs)


def _all_sum_small(v, *, name):
    r = v.shape[0]
    relations = [(dx, dy, dc) for dx in (0, 1) for dy in (0, 1) for dc in (0, 1)][1:]

    def body(v_ref, o_ref, buf, send_sems, recv_sems):
        x, y, c = _my_place()
        me = 4 * x + 2 * y + c
        buf[me] = v_ref[...]
        peers = [(x + dx - 2 * x * dx, y + dy - 2 * y * dy, c + dc - 2 * c * dc) for dx, dy, dc in relations]

        def copy(k, slot):
            return pltpu.make_async_remote_copy(src_ref=v_ref, dst_ref=buf.at[slot], send_sem=send_sems.at[k],
                                                recv_sem=recv_sems.at[k], device_id=peers[k], device_id_type=MESH)

        sends = [copy(k, me) for k in range(len(relations))]
        for cp in sends:
            cp.start()
        for k, (px, py, pc) in enumerate(peers):
            copy(k, 4 * px + 2 * py + pc).wait_recv()
        for cp in sends:
            cp.wait_send()
        acc = buf[0]
        for i in range(1, 8):
            acc = acc + buf[i]
        o_ref[...] = acc

    vm = pl.BlockSpec(memory_space=pltpu.VMEM)
    return pl.pallas_call(
        body, name=name, in_specs=[vm], out_specs=vm, out_shape=jax.ShapeDtypeStruct((r, LANES), F32),
        scratch_shapes=[pltpu.VMEM((8, r, LANES), F32), pltpu.SemaphoreType.DMA((7,)), pltpu.SemaphoreType.DMA((7,))],
    )(v)


SHARDED = (("ffn1_w_up", "col"), ("ffn1_w_down", "row"), ("w_in", "win"), ("w_branch_a", "col"),
           ("w_branch_b", "col"), ("w_out", "row"), ("ffn2_w_up", "col"), ("ffn2_w_down", "row"))
REPLICATED = ("ffn1_norm", "mix_norm", "na_rel_bias", "ffn2_norm", "final_norm")


def _weight_pieces(w):
    even = lax.axis_index("y") == 0
    shards, kinds, names = [], [], []
    for name, kind in SHARDED:
        wb = w[name].astype(BF16)
        if kind == "win":
            main = wb.shape[-1] - HEAD_DIM
            assert main % LANES == 0
            zeros = jnp.zeros(wb.shape[:-1] + (HEAD_DIM,), BF16)
            shards += [jnp.where(even, wb[..., :main], wb[..., HEAD_DIM:]),
                       jnp.where(even, jnp.concatenate([wb[..., main:], zeros], -1),
                                 jnp.concatenate([zeros, wb[..., :HEAD_DIM]], -1))]
            kinds += ["win_main", "slot"]
            names += [name, name + "_strad"]
        else:
            shards.append(wb)
            kinds.append(kind)
            names.append(name)
    return names, kinds, shards


def _finish_w_in(full):
    full = dict(full)
    strad = full.pop("w_in_strad")
    main = full["w_in"].shape[1] // N_CHIPS - HEAD_DIM
    for i in range(N_CHIPS // 2):
        lo = main + 2 * (main + HEAD_DIM) * i
        full["w_in"] = full["w_in"].at[:, lo:lo + LANES].set(strad[2 * i] + strad[2 * i + 1])
    return full


def _scatter_pieces(shards):
    names, kinds, sizes, srcs = [], [], [], []
    for name, kind in SHARDED:
        shp = shards[name].shape
        if kind == "win":
            names += [name, name + "_strad"]
            kinds += ["win_main", "win_strad"]
            sizes += [shp[2] - HEAD_DIM] * 2
            srcs += [name, name]
        else:
            names.append(name)
            kinds.append(kind)
            sizes.append(shp[1] if kind == "row" else shp[2])
            srcs.append(name)
    return names, kinds, sizes, srcs


def _finish_weight_grads(reduced, names, tag):
    out = dict(zip(names, _exchange_layers(reduced, name=f"{tag}_layers")))
    if "w_in_strad" in out:
        strad = out.pop("w_in_strad")
        even = lax.axis_index("y") == 0
        out["w_in"] = jnp.where(even, jnp.concatenate([out["w_in"], strad[..., :HEAD_DIM]], -1),
                                jnp.concatenate([strad[..., HEAD_DIM:], out["w_in"]], -1))
    return out


class _Grads:
    def __init__(self):
        self.arrays = {}

    def put(self, weight, layer, a, b, *, cols=None, col_off=0, **kw):
        self.arrays[weight, layer] = _mm(a, b, mode="tn", out_dtype=BF16, out_cols=cols, out_col_off=col_off,
                                         out_into=self.arrays.get((weight, layer)), **kw)


def _ffn_fwd(x, h, w_up, w_down, tag):
    t, d = x.shape
    f = w_down.shape[0]
    a, gate, up = _mm_swiglu_fwd(h, w_up, tm=_div_tile(t, ROWS_NARROW, 8), tn=MXU_N, name=f"{tag}_up")
    x_out = _mm(a, w_down, mode="nn", out_dtype=F32, tm=_div_tile(t, ROWS_WIDE, 8), tn=d, tk=f, alpha=0.5, res=x, name=f"{tag}_down")
    return x_out, (x, h, a, gate, up)


def _ffn_bwd(dx, dxb, saved, norm_g, w_up, w_down, layer, grads, wname, tag, scatter):
    x, h, a, gate, up = saved
    t, d = x.shape
    f = w_down.shape[0]
    tn = _div_tile(f, 1408)
    grads.put(f"{wname}_w_down", layer, a, dxb, tm=tn, tn=d, tk=ROWS_CONTRACTED, alpha=0.5, name=f"{tag}_dwd")
    d_gate, d_up = _mm_swiglu_bwd(dxb, w_down, gate, up, alpha=0.5, tm=_div_tile(t, ROWS_NARROW, 8), tn=MXU_N, name=f"{tag}_da")
    grads.put(f"{wname}_w_up", layer, h, d_gate, cols=2 * f, tm=d, tn=tn, tk=ROWS_CONTRACTED, name=f"{tag}_dwg")
    grads.put(f"{wname}_w_up", layer, h, d_up, cols=2 * f, col_off=f // tn, tm=d, tn=tn, tk=ROWS_CONTRACTED, name=f"{tag}_dwu")
    started = scatter(layer, [f"{wname}_w_up", f"{wname}_w_down"])
    dh = _mm(d_gate, w_up, mode="nt", out_dtype=F32, tm=_div_tile(t, ROWS_WIDE, 8), tn=d, tk=f, name=f"{tag}_dh1")
    dh = _mm(d_up, w_up, mode="nt", out_dtype=F32, tm=_div_tile(t, ROWS_WIDE, 8), tn=d, tk=f, b_k_off=1, res=dh, name=f"{tag}_dh2")
    return _rms_bwd(dh, x, norm_g + started, dx, tt=NORM_ROWS, name=f"{tag}_dnorm")


def _to_heads(y, b, n_heads):
    t, w = y.shape
    return y.reshape(b, t // b, n_heads, HEAD_DIM).transpose(0, 2, 1, 3)


def _from_heads(y):
    b, n, s, hd = y.shape
    return y.transpose(0, 2, 1, 3).reshape(b * s, n * hd)


N_QKV = 3 * (DIL_HEADS + NA_HEADS) * HEAD_DIM


def _mixer_fwd(x, b, norm_g, full, bias, tabs, tag):
    t, d = x.shape
    s = t // b
    n_in = full["w_in"].shape[1]
    h = _rms_fwd(x, norm_g, tt=NORM_ROWS, name=f"{tag}_norm")
    proj = _mm(h, full["w_in"], mode="nn", out_dtype=F32, tm=_div_tile(t, ROWS_NARROW, 8), tn=MXU_N, tk=d, name=f"{tag}_in")
    heads = _split_heads(proj.reshape(b, s, -1), *tabs, n_pairs=N_QKV // LANES, rot_pairs=DIL_HEADS,
                         scale_ranges=((0, DIL_HEADS // 2), (3 * DIL_HEADS // 2, (3 * DIL_HEADS + NA_HEADS) // 2)),
                         name=f"{tag}_heads")
    ya, lse_a = _dil_attn_fwd(heads, name=f"{tag}_dil")
    yb, lse_b = _na_attn_fwd(heads, bias, first=3 * DIL_HEADS, name=f"{tag}_na")
    ya2, yb2 = ya.reshape(t, -1), _from_heads(yb)
    z = _mm(ya2, full["w_branch_a"], mode="nn", out_dtype=F32, tm=_div_tile(t, ROWS_NARROW, 8), tn=MXU_N, tk=ya2.shape[1],
            out_slab=(0, 2), name=f"{tag}_za")
    z = _mm(yb2, full["w_branch_b"], mode="nn", out_dtype=F32, tm=_div_tile(t, ROWS_NARROW, 8), tn=MXU_N, tk=yb2.shape[1],
            out_slab=(1, 2), out_into=z, name=f"{tag}_zb")
    merged = _gate_fwd(proj, z, gate_col=N_QKV, tt=GATE_ROWS, name=f"{tag}_gate")
    x_out = _mm(merged, full["w_out"], mode="nn", out_dtype=F32, tm=_div_tile(t, ROWS_NARROW, 8), tn=MXU_N, tk=d, res=x, name=f"{tag}_out")
    return x_out, (x, h, proj, heads, ya, lse_a, yb, lse_b, ya2, yb2, z, merged)


def _mixer_bwd(dx, dob, b, saved, norm_g, full, layer, bias, tabs, grads, tag, scatter):
    x, h, proj, heads, ya, lse_a, yb, lse_b, ya2, yb2, z, merged = saved
    t, d = x.shape
    s = t // b
    n_in = full["w_in"].shape[1]
    grads.put("w_out", layer, merged, dob, tm=d, tn=d, tk=ROWS_CONTRACTED, name=f"{tag}_dwo")
    dm = _mm(dob, full["w_out"], mode="nt", out_dtype=F32, tm=_div_tile(t, ROWS_NARROW, 8), tn=MXU_N, tk=d, name=f"{tag}_dm")
    dz, dproj = _gate_bwd(dm, proj, z, gate_col=N_QKV, tt=GATE_ROWS, name=f"{tag}_dgate")
    grads.put("w_branch_a", layer, ya2, dz, b_sel=0, tm=ya2.shape[1], tn=d, tk=ROWS_CONTRACTED, name=f"{tag}_dwa")
    grads.put("w_branch_b", layer, yb2, dz, b_sel=1, tm=yb2.shape[1], tn=d, tk=ROWS_CONTRACTED, name=f"{tag}_dwb")
    started = scatter(layer, ["w_out", "w_branch_a", "w_branch_b"])
    dya = _mm(dz, full["w_branch_a"], mode="nt", out_dtype=F32, tm=_div_tile(t, ROWS_NARROW, 8), tn=MXU_N, tk=d, a_sel=0, name=f"{tag}_dya")
    dyb = _mm(dz, full["w_branch_b"], mode="nt", out_dtype=F32, tm=_div_tile(t, ROWS_NARROW, 8), tn=MXU_N, tk=d, a_sel=1, name=f"{tag}_dyb")
    d_dil = _dil_attn_bwd(heads, ya, lse_a, _to_heads(dya, b, DIL_GROUP_HEADS), name=f"{tag}_ddil")
    d_na, d_bias = _na_attn_bwd(heads, bias, yb, lse_b, _to_heads(dyb, b, NA_HEADS), first=3 * DIL_HEADS, name=f"{tag}_dna")
    dproj = _merge_heads(d_dil, *tabs, heads_per_row=DIL_GROUP_HEADS, rot_pairs=DIL_HEADS, scale_pairs=DIL_HEADS // 2,
                         dilated=True, out_cols=n_in, tile_off=0, into=dproj.reshape(b, s, n_in), name=f"{tag}_dheads_a")
    dproj = _merge_heads(d_na, *tabs, heads_per_row=NA_HEADS, rot_pairs=0, scale_pairs=NA_HEADS // 2, dilated=False,
                         out_cols=n_in, tile_off=3 * DIL_HEADS // 2, into=dproj, name=f"{tag}_dheads_b").reshape(t, n_in)
    grads.put("w_in", layer, h, dproj, tm=_div_tile(d, 512), tn=_div_tile(n_in, 2944), tk=ROWS_CONTRACTED // 2, name=f"{tag}_dwin")
    started = started + scatter(layer, ["w_in"])
    dh = _mm(dproj, full["w_in"], mode="nt", out_dtype=F32, tm=_div_tile(t, 2 * ROWS_WIDE, 8), tn=d, tk=_div_tile(n_in, 2944),
             name=f"{tag}_dh")
    dx_in, dxb_in, d_norm = _rms_bwd(dh, x, norm_g + started, dx, tt=NORM_ROWS, name=f"{tag}_dnorm")
    d_rb = _na_collapse_bias(d_bias, name=f"{tag}_dbias")
    return dx_in, dxb_in, d_norm, d_rb


def kernel(x, ffn1_norm, ffn1_w_up, ffn1_w_down, mix_norm, w_in, na_rel_bias, w_branch_a, w_branch_b, w_out, ffn2_norm, ffn2_w_up, ffn2_w_down, final_norm, loss_target, m_ffn1_norm, m_ffn1_w_up, m_ffn1_w_down, m_mix_norm, m_w_in, m_na_rel_bias, m_w_branch_a, m_w_branch_b, m_w_out, m_ffn2_norm, m_ffn2_w_up, m_ffn2_w_down, m_final_norm, v_ffn1_norm, v_ffn1_w_up, v_ffn1_w_down, v_mix_norm, v_w_in, v_na_rel_bias, v_w_branch_a, v_w_branch_b, v_w_out, v_ffn2_norm, v_ffn2_w_up, v_ffn2_w_down, v_final_norm):
    w = dict(ffn1_norm=ffn1_norm, ffn1_w_up=ffn1_w_up, ffn1_w_down=ffn1_w_down, mix_norm=mix_norm, w_in=w_in,
             na_rel_bias=na_rel_bias, w_branch_a=w_branch_a, w_branch_b=w_branch_b, w_out=w_out, ffn2_norm=ffn2_norm,
             ffn2_w_up=ffn2_w_up, ffn2_w_down=ffn2_w_down, final_norm=final_norm)
    mom = dict(ffn1_norm=m_ffn1_norm, ffn1_w_up=m_ffn1_w_up, ffn1_w_down=m_ffn1_w_down, mix_norm=m_mix_norm, w_in=m_w_in,
               na_rel_bias=m_na_rel_bias, w_branch_a=m_w_branch_a, w_branch_b=m_w_branch_b, w_out=m_w_out,
               ffn2_norm=m_ffn2_norm, ffn2_w_up=m_ffn2_w_up, ffn2_w_down=m_ffn2_w_down, final_norm=m_final_norm)
    var = dict(ffn1_norm=v_ffn1_norm, ffn1_w_up=v_ffn1_w_up, ffn1_w_down=v_ffn1_w_down, mix_norm=v_mix_norm, w_in=v_w_in,
               na_rel_bias=v_na_rel_bias, w_branch_a=v_w_branch_a, w_branch_b=v_w_branch_b, w_out=v_w_out,
               ffn2_norm=v_ffn2_norm, ffn2_w_up=v_ffn2_w_up, ffn2_w_down=v_ffn2_w_down, final_norm=v_final_norm)
    b, s, d = x.shape
    t = b * s
    depth = ffn1_norm.shape[0]
    assert depth == 2, "core c of a chip sends / reduces layer c"
    shards = {name: w[name] for name, _ in SHARDED}

    names, kinds, pieces = _weight_pieces(w)
    by_layer = [[p[l:l + 1] for p in pieces] for l in range(depth)]
    own = [[_place_own(p, kind, 0, name=f"own{l}_{nm}") for nm, kind, p in zip(names, kinds, by_layer[l])] for l in range(depth)]
    full = [{}, {}]

    def gather_start(layer, group, after, tag):
        idx = [i for i, nm in enumerate(names) if nm in group]
        pick = lambda seq: [seq[i] for i in idx]
        *state, token = _gather_layer_start(pick(by_layer[layer]), pick(kinds), pick(own[layer]), 0, after, name=f"{tag}_start")
        return (layer, idx, tag, state), token[:1, :1]

    def gather_finish(started, after):
        layer, idx, tag, state = started
        pick = lambda seq: [seq[i] for i in idx]
        landed = _gather_layer_wait(*state, pick(kinds), 0, after, name=f"{tag}_wait")
        done = _gather_layer_forward([by_layer[layer][i].shape for i in idx], pick(kinds), landed, name=f"{tag}_forward")
        full[layer].update(zip(pick(names), done))
        return done[0]

    ffn1, mixer, ffn2 = names[:2], names[2:7], names[7:]
    assert mixer[0] == "w_in" and ffn2[0] == "ffn2_w_up", names
    xc = x.reshape(t, d)
    l0_ffn1, token_ffn1 = gather_start(0, ffn1, xc, "gather_l0_ffn1")
    tabs = _rope_tables(s)
    bias = _na_expand_bias(na_rel_bias, name="na_bias")

    saved = []
    h = _rms_fwd(xc, ffn1_norm[:1] + token_ffn1, tt=NORM_ROWS, name="l0_ffn1_norm")
    landed = gather_finish(l0_ffn1, h)
    l0_mixer, token_mixer = gather_start(0, mixer, landed, "gather_l0_mixer")
    xc, s1 = _ffn_fwd(xc, h + token_mixer.astype(BF16), full[0]["ffn1_w_up"], full[0]["ffn1_w_down"], "l0_ffn1")
    landed = gather_finish(l0_mixer, xc)
    full[0] = _finish_w_in(full[0])
    l0_ffn2, token_ffn2 = gather_start(0, ffn2, landed, "gather_l0_ffn2")
    layer1, token_layer1 = gather_start(1, names, landed, "gather_l1")
    xc, s2 = _mixer_fwd(xc, b, mix_norm[:1] + token_ffn2 + token_layer1, full[0], bias[0], tabs, "l0_mix")
    gather_finish(l0_ffn2, xc)
    xc, s3 = _ffn_fwd(xc, _rms_fwd(xc, ffn2_norm[:1], tt=NORM_ROWS, name="l0_ffn2_norm"), full[0]["ffn2_w_up"], full[0]["ffn2_w_down"],
                      "l0_ffn2")
    saved.append((s1, s2, s3))
    gather_finish(layer1, xc)
    full[1] = _finish_w_in(full[1])
    for l in range(1, depth):
        xc, s1 = _ffn_fwd(xc, _rms_fwd(xc, ffn1_norm[l:l + 1], tt=NORM_ROWS, name=f"l{l}_ffn1_norm"), full[l]["ffn1_w_up"],
                          full[l]["ffn1_w_down"], f"l{l}_ffn1")
        xc, s2 = _mixer_fwd(xc, b, mix_norm[l:l + 1], full[l], bias[l], tabs, f"l{l}_mix")
        xc, s3 = _ffn_fwd(xc, _rms_fwd(xc, ffn2_norm[l:l + 1], tt=NORM_ROWS, name=f"l{l}_ffn2_norm"), full[l]["ffn2_w_up"],
                          full[l]["ffn2_w_down"], f"l{l}_ffn2")
        saved.append((s1, s2, s3))

    dx, dxb, d_final, loss_part = _final_loss(xc, final_norm.reshape(1, d), loss_target.reshape(t, d), tt=NORM_ROWS, name="final_loss")
    grads = _Grads()
    piece_names, piece_kinds, piece_sizes, piece_srcs = _scatter_pieces(shards)
    scattered = []

    def scatter(layer, weights):
        tag = f"grads{layer}_{weights[0]}"
        idx = [i for i, src in enumerate(piece_srcs) if src in weights]
        pick = lambda seq: [seq[i] for i in idx]
        *state, token = _grads_to_chips_start([grads.arrays[src, layer] for src in pick(piece_srcs)], pick(piece_kinds),
                                              pick(piece_sizes), layer, name=f"{tag}_to_chips_start")
        scattered.append((layer, idx, state))
        return token[:1, :1]
    small = {name: [None] * depth for name in REPLICATED[:-1]}
    for l in reversed(range(depth)):
        s1, s2, s3 = saved[l]
        dx, dxb, small["ffn2_norm"][l] = _ffn_bwd(dx, dxb, s3, ffn2_norm[l:l + 1], full[l]["ffn2_w_up"], full[l]["ffn2_w_down"],
                                                  l, grads, "ffn2", f"l{l}_ffn2", scatter)
        dx, dxb, small["mix_norm"][l], small["na_rel_bias"][l] = _mixer_bwd(
            dx, dxb, b, s2, mix_norm[l:l + 1], full[l], l, bias[l], tabs, grads, f"l{l}_mix", scatter)
        dx, dxb, small["ffn1_norm"][l] = _ffn_bwd(dx, dxb, s1, ffn1_norm[l:l + 1], full[l]["ffn1_w_up"], full[l]["ffn1_w_down"],
                                                  l, grads, "ffn1", f"l{l}_ffn1", scatter)
    grad_x = dx.reshape(b, s, d)
    reduced = [None] * len(piece_names)

    def arrive(group, after):
        layer, idx, state = group
        state = _grads_to_chips_wait(*state, [piece_kinds[i] for i in idx], [piece_sizes[i] for i in idx], layer, after,
                                     name=f"grads{layer}_{piece_names[idx[0]]}_to_chips_wait")
        for i, p, sl in zip(idx, *state):
            reduced[i] = _sum_slabs(sl, p, piece_kinds[i], piece_sizes[i], layer, reduced[i],
                                    name=f"grads{layer}_sum_{piece_names[i]}")
        return idx

    for group in scattered[:-1]:
        arrive(group, dx)
    late = scattered[-1][1]
    early = [i for i in range(len(piece_names)) if i not in late]
    g_out = _finish_weight_grads([reduced[i] for i in early], [piece_names[i] for i in early], "grads_early")

    parts = [jnp.stack(small[name]).reshape(-1) for name in REPLICATED[:-1]] + [d_final.reshape(-1), loss_part[0, :1]]
    sizes = [v.shape[0] for v in parts]
    flat = jnp.concatenate(parts)
    flat = jnp.pad(flat, (0, -flat.shape[0] % (8 * LANES)))
    small_sum = _all_sum_small(flat.reshape(-1, LANES), name="small_all_sum").reshape(-1)
    off = 0
    for name, n in zip(REPLICATED, sizes[:-1]):
        g_out[name] = small_sum[off:off + n].reshape(w[name].shape)
        off += n
    loss = small_sum[off]

    names = list(w)
    delta, new_m, new_v = {}, {}, {}
    for name in [n for n in names if n in g_out]:
        delta[name], new_m[name], new_v[name] = _adamw(w[name], g_out[name], mom[name], var[name], name=f"adamw_{name}")
    arrive(scattered[-1], delta["w_in"])
    g_out.update(_finish_weight_grads([reduced[i] for i in late], [piece_names[i] for i in late], "grads_late"))
    for name in [n for n in names if n not in delta]:
        delta[name], new_m[name], new_v[name] = _adamw(w[name], g_out[name], mom[name], var[name], name=f"adamw_{name}")
    return (loss, grad_x, *[g_out[n] for n in names], *[delta[n] for n in names], *[new_m[n] for n in names],
            *[new_v[n] for n in names])
```

```python
import functools

import numpy as np
import jax
import jax.numpy as jnp
from jax import lax
from jax.experimental import pallas as pl
from jax.experimental.pallas import tpu as pltpu

F32, BF16 = jnp.float32, jnp.bfloat16
MESH = pl.DeviceIdType.MESH

HEAD_DIM = 64
DILATIONS = (1, 4, 16)
DIL_HALF = 64
DIL_GROUP_HEADS = 4
DIL_HEADS = 12
NA_HEADS = 8
GRID_W = 64
NA_ROWS = 8
NA_COLS = 16
ROPE_THETA = 10000.0
RMS_EPS = 1e-6
NEG_INF = -1e30
ADAM_LR, ADAM_B1, ADAM_B2, ADAM_EPS, ADAM_WD, ADAM_STEP = 0.001, 0.9, 0.999, 1e-08, 0.01, 10
QK_SCALE = HEAD_DIM ** -0.5

N_CHIPS = 4
LANES = 128
BF16_ROWS = 16
VMEM_LIMIT = 56 * 1024 * 1024
MXU_N = 256
ROWS_NARROW = 4096
ROWS_WIDE = 512
NORM_ROWS = 1024
GATE_ROWS = 2048
ROWS_CONTRACTED = 4096
BLOCKS_IN_FLIGHT = 8

_NN = (((1,), (0,)), ((), ()))
_NT = (((1,), (1,)), ((), ()))
_TN = (((0,), (0,)), ((), ()))

HBM = pl.BlockSpec(memory_space=pl.ANY)


def _params(**kw):
    return pltpu.CompilerParams(vmem_limit_bytes=VMEM_LIMIT, **kw)


def _dot(a, b, dims):
    return lax.dot_general(a, b, dims, preferred_element_type=F32)


def _div_tile(n, cap, mult=LANES):
    best = None
    for t in range(mult, min(n, cap) + 1, mult):
        if n % t == 0:
            best = t
    return n if best is None else best


def _stacked(block, index, sel):
    if sel is None:
        return pl.BlockSpec(block, index)
    return pl.BlockSpec((None,) + block, lambda *g: (sel,) + index(*g))


def _mm(a, b, *, mode, out_dtype, tm, tn, tk, name, alpha=1.0, res=None, a_sel=None, b_sel=None, b_k_off=0,
        out_slab=None, out_cols=None, out_col_off=0, out_into=None):
    a2, b2 = a.shape[-2:], b.shape[-2:]
    if mode == "nn":
        (m, k), n = a2, b2[1]
        a_spec = _stacked((tm, tk), lambda i, j, kk: (i, kk), a_sel)
        b_spec = _stacked((tk, tn), lambda i, j, kk: (kk + b_k_off, j), b_sel)
        dims = _NN
    elif mode == "nt":
        (m, k), n = a2, b2[0]
        a_spec = _stacked((tm, tk), lambda i, j, kk: (i, kk), a_sel)
        b_spec = _stacked((tn, tk), lambda i, j, kk: (j, kk + b_k_off), b_sel)
        dims = _NT
    else:
        (k, m), n = a2, b2[1]
        a_spec = _stacked((tk, tm), lambda i, j, kk: (kk, i), a_sel)
        b_spec = _stacked((tk, tn), lambda i, j, kk: (kk + b_k_off, j), b_sel)
        dims = _TN
    assert m % tm == 0 and n % tn == 0 and k % tk == 0, (name, a.shape, b.shape)
    nk = k // tk
    has_res = res is not None
    if out_slab is None:
        o_spec = pl.BlockSpec((tm, tn), lambda i, j, kk: (i, j + out_col_off))
        out_shape = jax.ShapeDtypeStruct((m, n if out_cols is None else out_cols), out_dtype)
    else:
        o_spec = _stacked((tm, tn), lambda i, j, kk: (i, j + out_col_off), out_slab[0])
        out_shape = jax.ShapeDtypeStruct((out_slab[1], m, n if out_cols is None else out_cols), out_dtype)
    r_spec = pl.BlockSpec((tm, tn), lambda i, j, kk: (i, j))
    n_in = 2 + has_res + (out_into is not None)

    def body(*refs):
        a_ref, b_ref = refs[0], refs[1]
        r_ref = refs[2] if has_res else None
        o_ref = refs[n_in]
        p = _dot(a_ref[...], b_ref[...], dims)

        def finish(acc):
            y = acc * alpha if alpha != 1.0 else acc
            if has_res:
                y = y + r_ref[...].astype(F32)
            o_ref[...] = y.astype(o_ref.dtype)

        if nk == 1:
            finish(p)
        else:
            acc_ref = refs[n_in + 1]
            kk = pl.program_id(2)

            @pl.when(kk == 0)
            def _():
                acc_ref[...] = p

            @pl.when(kk > 0)
            def _():
                acc_ref[...] += p

            @pl.when(kk == nk - 1)
            def _():
                finish(acc_ref[...])

    operands = [a, b] + ([res] if has_res else [])
    in_specs = [a_spec, b_spec] + ([r_spec] if has_res else [])
    aliases = {}
    if out_into is not None:
        aliases = {len(operands): 0}
        operands.append(out_into)
        in_specs.append(HBM)
    return pl.pallas_call(
        body, name=name, grid=(m // tm, n // tn, nk), in_specs=in_specs, out_specs=o_spec, out_shape=out_shape,
        scratch_shapes=[pltpu.VMEM((tm, tn), F32)] if nk > 1 else [], input_output_aliases=aliases,
        compiler_params=_params(dimension_semantics=("parallel", "parallel", "arbitrary")),
    )(*operands)


def _mm_swiglu_fwd(h, w_up, *, tm, tn, name):
    m, k = h.shape
    n = w_up.shape[1] // 2
    h_spec = pl.BlockSpec((tm, k), lambda i, j: (i, 0))
    wg_spec = pl.BlockSpec((k, tn), lambda i, j: (0, j))
    wu_spec = pl.BlockSpec((k, tn), lambda i, j: (0, j + n // tn))
    o_spec = pl.BlockSpec((tm, tn), lambda i, j: (i, j))

    def body(h_ref, wg_ref, wu_ref, a_ref, g_ref, u_ref):
        hb = h_ref[...]
        g = _dot(hb, wg_ref[...], _NN)
        u = _dot(hb, wu_ref[...], _NN)
        a_ref[...] = (g * jax.nn.sigmoid(g) * u).astype(BF16)
        g_ref[...] = g.astype(BF16)
        u_ref[...] = u.astype(BF16)

    out = jax.ShapeDtypeStruct((m, n), BF16)
    return pl.pallas_call(
        body, name=name, grid=(m // tm, n // tn), in_specs=[h_spec, wg_spec, wu_spec],
        out_specs=[o_spec] * 3, out_shape=[out] * 3,
        compiler_params=_params(dimension_semantics=("parallel", "parallel")),
    )(h, w_up, w_up)


def _mm_swiglu_bwd(dy, w_down, gate, up, *, alpha, tm, tn, name):
    m, k = dy.shape
    n = w_down.shape[0]
    dy_spec = pl.BlockSpec((tm, k), lambda i, j: (i, 0))
    w_spec = pl.BlockSpec((tn, k), lambda i, j: (j, 0))
    o_spec = pl.BlockSpec((tm, tn), lambda i, j: (i, j))

    def body(dy_ref, w_ref, g_ref, u_ref, dg_ref, du_ref):
        da = _dot(dy_ref[...], w_ref[...], _NT) * alpha
        g = g_ref[...].astype(F32)
        u = u_ref[...].astype(F32)
        sg = jax.nn.sigmoid(g)
        dg_ref[...] = (da * u * (sg * (1.0 + g * (1.0 - sg)))).astype(BF16)
        du_ref[...] = (da * (g * sg)).astype(BF16)

    out = jax.ShapeDtypeStruct((m, n), BF16)
    return pl.pallas_call(
        body, name=name, grid=(m // tm, n // tn), in_specs=[dy_spec, w_spec, o_spec, o_spec],
        out_specs=[o_spec] * 2, out_shape=[out] * 2,
        compiler_params=_params(dimension_semantics=("parallel", "parallel")),
    )(dy, w_down, gate, up)


def _rms_fwd(x, g, *, tt, name):
    t, d = x.shape

    def body(x_ref, g_ref, h_ref):
        xv = x_ref[...]
        rstd = lax.rsqrt(jnp.mean(xv * xv, axis=1, keepdims=True) + RMS_EPS)
        h_ref[...] = (xv * rstd * g_ref[...]).astype(BF16)

    return pl.pallas_call(
        body, name=name, grid=(t // tt,),
        in_specs=[pl.BlockSpec((tt, d), lambda i: (i, 0)), pl.BlockSpec((1, d), lambda i: (0, 0))],
        out_specs=pl.BlockSpec((tt, d), lambda i: (i, 0)), out_shape=jax.ShapeDtypeStruct((t, d), BF16),
        compiler_params=_params(dimension_semantics=("parallel",)),
    )(x, g)


def _rms_bwd(dh, x, g, dres, *, tt, name):
    t, d = x.shape

    def body(dh_ref, x_ref, g_ref, r_ref, dx_ref, dxb_ref, dg_ref):
        xv = x_ref[...]
        rstd = lax.rsqrt(jnp.mean(xv * xv, axis=1, keepdims=True) + RMS_EPS)
        xhat = xv * rstd
        dhv = dh_ref[...]
        dxhat = dhv * g_ref[...]
        dx = r_ref[...] + rstd * (dxhat - xhat * jnp.mean(dxhat * xhat, axis=1, keepdims=True))
        dx_ref[...] = dx
        dxb_ref[...] = dx.astype(BF16)

        @pl.when(pl.program_id(0) == 0)
        def _():
            dg_ref[...] = jnp.zeros_like(dg_ref)

        dg_ref[...] += jnp.sum(dhv * xhat, axis=0, keepdims=True)

    row = pl.BlockSpec((tt, d), lambda i: (i, 0))
    vec = pl.BlockSpec((1, d), lambda i: (0, 0))
    return pl.pallas_call(
        body, name=name, grid=(t // tt,), in_specs=[row, row, vec, row], out_specs=[row, row, vec],
        out_shape=[jax.ShapeDtypeStruct((t, d), F32), jax.ShapeDtypeStruct((t, d), BF16), jax.ShapeDtypeStruct((1, d), F32)],
        compiler_params=_params(dimension_semantics=("arbitrary",)),
    )(dh, x, g, dres)


def _final_loss(x, g, target, *, tt, name):
    t, d = x.shape

    def body(x_ref, g_ref, t_ref, dx_ref, dxb_ref, dg_ref, loss_ref):
        xv = x_ref[...]
        gv = g_ref[...]
        rstd = lax.rsqrt(jnp.mean(xv * xv, axis=1, keepdims=True) + RMS_EPS)
        xhat = xv * rstd
        err = xhat * gv - t_ref[...]
        dy = err * (1.0 / d)
        dxhat = dy * gv
        dx = rstd * (dxhat - xhat * jnp.mean(dxhat * xhat, axis=1, keepdims=True))
        dx_ref[...] = dx
        dxb_ref[...] = dx.astype(BF16)

        @pl.when(pl.program_id(0) == 0)
        def _():
            dg_ref[...] = jnp.zeros_like(dg_ref)
            loss_ref[...] = jnp.zeros_like(loss_ref)

        dg_ref[...] += jnp.sum(dy * xhat, axis=0, keepdims=True)
        part = 0.5 * jnp.sum(jnp.mean(err * err, axis=1, keepdims=True), axis=0, keepdims=True)
        loss_ref[...] += jnp.broadcast_to(part, loss_ref.shape)

    row = pl.BlockSpec((tt, d), lambda i: (i, 0))
    vec = pl.BlockSpec((1, d), lambda i: (0, 0))
    one = pl.BlockSpec((1, LANES), lambda i: (0, 0))
    return pl.pallas_call(
        body, name=name, grid=(t // tt,), in_specs=[row, vec, row], out_specs=[row, row, vec, one],
        out_shape=[jax.ShapeDtypeStruct((t, d), F32), jax.ShapeDtypeStruct((t, d), BF16), jax.ShapeDtypeStruct((1, d), F32),
                   jax.ShapeDtypeStruct((1, LANES), F32)],
        compiler_params=_params(dimension_semantics=("arbitrary",)),
    )(x, g, target)


def _swap_halves(x):
    lane = lax.broadcasted_iota(jnp.int32, x.shape, 1)
    return jnp.where((lane // 32) % 2 == 0, pltpu.roll(x, 96, 1), pltpu.roll(x, 32, 1))


def _rope_tables(s):
    half = HEAD_DIM // 2
    inv_freq = ROPE_THETA ** (-jnp.arange(half, dtype=F32) / half)
    ang = jnp.arange(s).astype(F32)[:, None] * inv_freq[None, :]
    cos, sin = jnp.cos(ang), jnp.sin(ang)
    return jnp.tile(cos, (1, 4)), jnp.concatenate([-sin, sin, -sin, sin], axis=1)


def _dilation_of_tile(p):
    dilated = p < 3 * DIL_HEADS // 2
    g = (p % (DIL_HEADS // 2)) // (DIL_GROUP_HEADS // 2)
    return [(dilated & (g == gi)) | (jnp.logical_not(dilated) if gi == 0 else False) for gi in range(len(DILATIONS))]


def _residue_major(ref, d):
    s = ref.shape[0]
    if d == 1:
        return ref[...]
    return jnp.concatenate([ref[pl.ds(r, s // d, stride=d), :] for r in range(d)], axis=0)


def _split_heads(proj, cos4, sin4, *, n_pairs, rot_pairs, scale_ranges, name):
    b, s, _ = proj.shape

    def body(x_ref, c_ref, s_ref, o_ref):
        p = pl.program_id(1)
        is_q = functools.reduce(jnp.logical_or, [(p >= lo) & (p < hi) for lo, hi in scale_ranges])
        scale = jnp.where(is_q, QK_SCALE, 1.0)

        def put(y):
            o_ref[0] = y[:, :HEAD_DIM].astype(BF16)
            o_ref[1] = y[:, HEAD_DIM:].astype(BF16)

        for d, in_group in zip(DILATIONS, _dilation_of_tile(p)):
            @pl.when(in_group & (p < rot_pairs))
            def _(d=d):
                x = _residue_major(x_ref, d)
                put((x * _residue_major(c_ref, d) + _swap_halves(x) * _residue_major(s_ref, d)) * scale)

            @pl.when(in_group & (p >= rot_pairs))
            def _(d=d):
                put(_residue_major(x_ref, d) * scale)

    tab = pl.BlockSpec((s, LANES), lambda bi, p: (0, 0))
    return pl.pallas_call(
        body, name=name, grid=(b, n_pairs),
        in_specs=[pl.BlockSpec((None, s, LANES), lambda bi, p: (bi, 0, p)), tab, tab],
        out_specs=pl.BlockSpec((None, 2, s, HEAD_DIM), lambda bi, p: (bi, p, 0, 0)),
        out_shape=jax.ShapeDtypeStruct((b, 2 * n_pairs, s, HEAD_DIM), BF16),
        compiler_params=_params(dimension_semantics=("parallel", "parallel")),
    )(proj, cos4, sin4)


def _merge_heads(dheads, cos4, sin4, *, heads_per_row, rot_pairs, scale_pairs, dilated, out_cols, tile_off, into, name):
    b, hpr, r, s, _ = dheads.shape
    n_pairs = hpr * r // 2
    ppr = hpr // 2

    def body(d_ref, c_ref, s_ref, *rest):
        o_ref, t_ref = rest[-2:]
        p = pl.program_id(1)
        scale = jnp.where(p < scale_pairs, QK_SCALE, 1.0)

        def tokens(d):
            dy = jnp.concatenate([d_ref[0], d_ref[1]], axis=1)
            if d == 1:
                return dy
            for res in range(d):
                t_ref[pl.ds(res, s // d, stride=d), :] = dy[res * (s // d):(res + 1) * (s // d), :]
            return t_ref[...]

        groups = _dilation_of_tile(p) if dilated else [p >= 0]
        for d, in_group in zip(DILATIONS, groups):
            @pl.when(in_group & (p < rot_pairs))
            def _(d=d):
                dy = tokens(d)
                o_ref[...] = ((dy * c_ref[...] - _swap_halves(dy) * s_ref[...]) * scale).astype(BF16)

            @pl.when(in_group & (p >= rot_pairs))
            def _(d=d):
                o_ref[...] = (tokens(d) * scale).astype(BF16)

    tab = pl.BlockSpec((s, LANES), lambda bi, p: (0, 0))
    operands = [dheads, cos4, sin4] + ([] if into is None else [into])
    return pl.pallas_call(
        body, name=name, grid=(b, n_pairs),
        in_specs=[pl.BlockSpec((None, 2, None, s, HEAD_DIM), lambda bi, p: (bi, p % ppr, p // ppr, 0, 0)), tab, tab]
        + ([] if into is None else [HBM]),
        out_specs=pl.BlockSpec((None, s, LANES), lambda bi, p: (bi, 0, p + tile_off)),
        out_shape=jax.ShapeDtypeStruct((b, s, out_cols), BF16),
        input_output_aliases={} if into is None else {3: 0},
        scratch_shapes=[pltpu.VMEM((s, LANES), F32)],
        compiler_params=_params(dimension_semantics=("parallel", "parallel")),
    )(*operands)


DIL_TQ = 256


def _dil_block(g, s):
    run = s // DILATIONS[g]
    return DIL_TQ if run <= DIL_TQ else min(run, DIL_TQ + 2 * LANES)


def _dil_keys(g, q0, s):
    run = max(s // DILATIONS[g], DIL_TQ)
    lo = (q0 // run) * run
    return pl.multiple_of(jnp.clip(q0 - LANES, lo, lo + run - _dil_block(g, s)), LANES)


def _dil_band(g, q0, start, shape, s):
    row = q0 + lax.broadcasted_iota(jnp.int32, shape, 0)
    col = start + lax.broadcasted_iota(jnp.int32, shape, 1)
    ok = jnp.abs(row - col) <= DIL_HALF
    run = s // DILATIONS[g]
    if run < DIL_TQ:
        shift = run.bit_length() - 1
        ok = ok & ((row >> shift) == (col >> shift))
    return ok


def _dil_tokens(g, q0, s):
    d = DILATIONS[g]
    if d == 1:
        return [(0, DIL_TQ, pl.ds(q0, DIL_TQ))]
    run = s // d
    n = min(run, DIL_TQ)
    return [(lo, n, pl.ds(((q0 + lo) % run) * d + (q0 + lo) // run, n, stride=d)) for lo in range(0, DIL_TQ, n)]


def _dil_gather(ref, pieces):
    return jnp.concatenate([ref[rows, :] for _, _, rows in pieces], axis=0) if len(pieces) > 1 else ref[pieces[0][2], :]


def _dil_head_spec(part, g, s):
    return pl.BlockSpec((None, None, s, HEAD_DIM), lambda b, j: (b, part * DIL_HEADS + g * DIL_GROUP_HEADS + j, 0, 0))


def _dil_attn_fwd(heads, *, name):
    b, _, s, _ = heads.shape
    n_g = len(DILATIONS)

    def body(*refs):
        qkv = refs[:3 * n_g]
        o_ref, oh_ref, l_ref, og_ref, lg_ref = refs[3 * n_g:]
        for g in range(n_g):
            q_ref, k_ref, v_ref = qkv[3 * g:3 * g + 3]
            width = _dil_block(g, s)

            def step(i, carry, g=g, q_ref=q_ref, k_ref=k_ref, v_ref=v_ref, width=width):
                q0 = pl.multiple_of(i * DIL_TQ, DIL_TQ)
                start = _dil_keys(g, q0, s)
                sc = _dot(q_ref[pl.ds(q0, DIL_TQ), :], k_ref[pl.ds(start, width), :], _NT)
                sc = jnp.where(_dil_band(g, q0, start, sc.shape, s), sc, NEG_INF)
                m = jnp.max(sc, axis=1, keepdims=True)
                p = jnp.exp(sc - m)
                den = jnp.sum(p, axis=1, keepdims=True)
                o = _dot(p.astype(BF16), v_ref[pl.ds(start, width), :], _NN) / den
                lse = m + jnp.log(den)
                for lo, n, rows in _dil_tokens(g, q0, s):
                    og_ref[g, rows, :] = o[lo:lo + n]
                    lg_ref[g, rows, :] = lse[lo:lo + n]
                return carry

            lax.fori_loop(0, s // DIL_TQ, step, 0, unroll=BLOCKS_IN_FLIGHT)
        lses = [lg_ref[g] for g in range(n_g)]
        m = functools.reduce(jnp.maximum, lses)
        ws = [jnp.exp(l - m) for l in lses]
        den = functools.reduce(jnp.add, ws)
        mixed = (functools.reduce(jnp.add, [w * og_ref[g] for g, w in enumerate(ws)]) / den).astype(o_ref.dtype)
        l_ref[...] = m + jnp.log(den)
        oh_ref[...] = mixed

        @pl.when(pl.program_id(1) % 2 == 0)
        def _():
            o_ref[:, :HEAD_DIM] = mixed

        @pl.when(pl.program_id(1) % 2 == 1)
        def _():
            o_ref[:, HEAD_DIM:] = mixed

    out = pl.BlockSpec((None, s, 2 * HEAD_DIM), lambda bi, j: (bi, 0, j // 2))
    lse = pl.BlockSpec((None, None, s, 1), lambda bi, j: (bi, j, 0, 0))
    return pl.pallas_call(
        body, name=name, grid=(b, DIL_GROUP_HEADS),
        in_specs=[_dil_head_spec(part, g, s) for g in range(n_g) for part in range(3)],
        out_specs=[out, pl.BlockSpec((None, None, s, HEAD_DIM), lambda bi, j: (bi, j, 0, 0)), lse],
        out_shape=[jax.ShapeDtypeStruct((b, s, DIL_GROUP_HEADS * HEAD_DIM), BF16),
                   jax.ShapeDtypeStruct((b, DIL_GROUP_HEADS, s, HEAD_DIM), BF16),
                   jax.ShapeDtypeStruct((b, DIL_GROUP_HEADS, s, 1), F32)],
        scratch_shapes=[pltpu.VMEM((n_g, s, HEAD_DIM), F32), pltpu.VMEM((n_g, s, 1), F32)],
        compiler_params=_params(dimension_semantics=("parallel", "arbitrary")),
    )(*([heads] * (3 * n_g)))


def _dil_attn_bwd(heads, out, lse, dout, *, name):
    b, _, s, _ = heads.shape
    n_g = len(DILATIONS)

    def body(*refs):
        qkv = refs[:3 * n_g]
        o_ref, l_ref, do_ref, d_ref, delta_ref = refs[3 * n_g:]
        d_ref[...] = jnp.zeros_like(d_ref)
        delta_ref[...] = jnp.sum(do_ref[...] * o_ref[...].astype(F32), axis=1, keepdims=True)
        for g in range(n_g):
            q_ref, k_ref, v_ref = qkv[3 * g:3 * g + 3]
            width = _dil_block(g, s)

            def step(i, carry, g=g, q_ref=q_ref, k_ref=k_ref, v_ref=v_ref, width=width):
                q0 = pl.multiple_of(i * DIL_TQ, DIL_TQ)
                start = _dil_keys(g, q0, s)
                win = pl.ds(start, width)
                pieces = _dil_tokens(g, q0, s)
                do_b = _dil_gather(do_ref, pieces).astype(BF16)
                q, k, v = q_ref[pl.ds(q0, DIL_TQ), :], k_ref[win, :], v_ref[win, :]
                sc = _dot(q, k, _NT)
                p = jnp.where(_dil_band(g, q0, start, sc.shape, s), jnp.exp(sc - _dil_gather(l_ref, pieces)), 0.0)
                ds = (p * (_dot(do_b, v, _NT) - _dil_gather(delta_ref, pieces))).astype(BF16)
                d_ref[g, pl.ds(q0, DIL_TQ), :] = _dot(ds, k, _NN)
                d_ref[n_g + g, win, :] += _dot(ds, q, _TN)
                d_ref[2 * n_g + g, win, :] += _dot(p.astype(BF16), do_b, _TN)
                return carry

            lax.fori_loop(0, s // DIL_TQ, step, 0, unroll=BLOCKS_IN_FLIGHT)

    per_head = lambda bi, j: (bi, j, 0, 0)
    return pl.pallas_call(
        body, name=name, grid=(b, DIL_GROUP_HEADS),
        in_specs=[_dil_head_spec(part, g, s) for g in range(n_g) for part in range(3)]
        + [pl.BlockSpec((None, None, s, HEAD_DIM), per_head), pl.BlockSpec((None, None, s, 1), per_head),
           pl.BlockSpec((None, None, s, HEAD_DIM), per_head)],
        out_specs=pl.BlockSpec((None, None, 3 * n_g, s, HEAD_DIM), lambda bi, j: (bi, j, 0, 0, 0)),
        out_shape=jax.ShapeDtypeStruct((b, DIL_GROUP_HEADS, 3 * n_g, s, HEAD_DIM), F32),
        scratch_shapes=[pltpu.VMEM((s, 1), F32)],
        compiler_params=_params(dimension_semantics=("parallel", "parallel")),
    )(*([heads] * (3 * n_g)), out, lse, dout)


NA_BIAS_ROWS = 2 * NA_ROWS - 1
NA_BIAS_COLS = 2 * NA_COLS - 1
NA_BLOCK = 4
NA_SPAN = NA_ROWS + NA_BLOCK - 1
NA_Q = NA_BLOCK * GRID_W
NA_KEYS = NA_SPAN * GRID_W
NA_FORMS = 3


def _na_onehot():
    c = np.arange(GRID_W)[:, None]
    k = np.arange(GRID_W)[None, :]
    lo = np.clip(c - NA_COLS // 2, 0, GRID_W - NA_COLS)
    valid = (k >= lo) & (k < lo + NA_COLS)
    onehot = np.zeros((GRID_W, GRID_W, LANES), np.float32)
    cc, kk = np.nonzero(valid)
    onehot[cc, kk, kk - cc + NA_COLS - 1] = 1.0
    return onehot.reshape(GRID_W * GRID_W, LANES), valid.reshape(1, GRID_W * GRID_W)


def _na_block_rows(n_rows):
    table = np.full((NA_FORMS, NA_BLOCK, NA_SPAN), NA_BIAS_ROWS, np.int64)
    n_blocks = n_rows // NA_BLOCK
    for form, ib in enumerate((0, 1, n_blocks - 1)):
        base = min(max(NA_BLOCK * ib - NA_ROWS // 2, 0), n_rows - NA_SPAN)
        for rl in range(NA_BLOCK):
            r = NA_BLOCK * ib + rl
            row_lo = min(max(r - NA_ROWS // 2, 0), n_rows - NA_ROWS)
            for kl in range(NA_SPAN):
                if row_lo <= base + kl < row_lo + NA_ROWS:
                    table[form, rl, kl] = base + kl - r + NA_ROWS - 1
    return table


def _na_block(ib, n_rows):
    n_blocks = n_rows // NA_BLOCK
    base = jnp.clip(NA_BLOCK * ib - NA_ROWS // 2, 0, n_rows - NA_SPAN)
    return base, jnp.where(ib == 0, 0, jnp.where(ib == n_blocks - 1, 2, 1))


def _na_expand_bias(rel_bias, *, name):
    l, h, nr, nc = rel_bias.shape
    onehot, valid = _na_onehot()
    rb = jnp.pad(rel_bias, ((0, 0), (0, 0), (0, 1), (0, LANES - nc))).reshape(l * h * (nr + 1), LANES)
    live = jnp.asarray(np.tile(np.arange(nr + 1) < nr, l * h).astype(np.float32)[:, None])

    def body(rb_ref, oh_ref, valid_ref, live_ref, e_ref):
        e = lax.dot_general(rb_ref[...], oh_ref[...], _NT, precision=lax.Precision.HIGHEST, preferred_element_type=F32)
        e_ref[...] = jnp.where((valid_ref[...] > 0) & (live_ref[...] > 0), e, NEG_INF)

    e = pl.pallas_call(
        body, name=name, out_shape=jax.ShapeDtypeStruct((l * h * (nr + 1), GRID_W * GRID_W), F32), compiler_params=_params(),
    )(rb, jnp.asarray(onehot), jnp.asarray(valid.astype(np.float32)), live)
    return e.reshape(l, h, nr + 1, GRID_W, GRID_W)


def _na_collapse_bias(de, *, name):
    b, h = de.shape[:2]
    onehot, _ = _na_onehot()
    rows = h * NA_BIAS_ROWS

    def diag(e_ref, oh_ref, o_ref):
        e = e_ref[0]
        for bi in range(1, b):
            e = e + e_ref[bi]
        o_ref[...] = lax.dot_general(e, oh_ref[...], _NN, precision=lax.Precision.HIGHEST, preferred_element_type=F32)

    drb = pl.pallas_call(
        diag, name=name, out_shape=jax.ShapeDtypeStruct((rows, LANES), F32), compiler_params=_params(),
    )(de.reshape(b, rows, GRID_W * GRID_W), jnp.asarray(onehot))
    return drb[:, :NA_BIAS_COLS].reshape(h, NA_BIAS_ROWS, NA_BIAS_COLS)


def _na_tiles(n_rows):
    table = _na_block_rows(n_rows)
    return [(f, rl, kl, int(table[f, rl, kl])) for f in range(NA_FORMS) for rl in range(NA_BLOCK) for kl in range(NA_SPAN)]


def _na_tile(ref, form, rl, kl):
    return ref.at[form, rl * GRID_W:(rl + 1) * GRID_W, kl * GRID_W:(kl + 1) * GRID_W]


def _na_head_spec(part, first, s):
    return pl.BlockSpec((None, None, s, HEAD_DIM), lambda b, h: (b, first + part * NA_HEADS + h, 0, 0))


def _na_attn_fwd(heads, bias, *, first, name):
    b, _, s, _ = heads.shape
    n_rows = s // GRID_W
    tiles = _na_tiles(n_rows)

    def body(q_ref, k_ref, v_ref, e_ref, o_ref, l_ref, b_ref):
        for form, rl, kl, i in tiles:
            _na_tile(b_ref, form, rl, kl)[...] = e_ref[i]

        def step(ib, carry):
            base, form = _na_block(ib, n_rows)
            rows = pl.ds(pl.multiple_of(ib * NA_Q, NA_Q), NA_Q)
            win = pl.ds(pl.multiple_of(base * GRID_W, GRID_W), NA_KEYS)
            sc = _dot(q_ref[rows, :], k_ref[win, :], _NT) + b_ref[form]
            m = jnp.max(sc, axis=1, keepdims=True)
            p = jnp.exp(sc - m)
            den = jnp.sum(p, axis=1, keepdims=True)
            o_ref[rows, :] = (_dot(p.astype(BF16), v_ref[win, :], _NN) / den).astype(o_ref.dtype)
            l_ref[rows, :] = m + jnp.log(den)
            return carry

        lax.fori_loop(0, n_rows // NA_BLOCK, step, 0, unroll=BLOCKS_IN_FLIGHT)

    per_head = lambda bi, h: (bi, h, 0, 0)
    return pl.pallas_call(
        body, name=name, grid=(b, NA_HEADS),
        in_specs=[_na_head_spec(part, first, s) for part in range(3)]
        + [pl.BlockSpec((None, NA_BIAS_ROWS + 1, GRID_W, GRID_W), lambda bi, h: (h, 0, 0, 0))],
        out_specs=[pl.BlockSpec((None, None, s, HEAD_DIM), per_head), pl.BlockSpec((None, None, s, 1), per_head)],
        out_shape=[jax.ShapeDtypeStruct((b, NA_HEADS, s, HEAD_DIM), BF16), jax.ShapeDtypeStruct((b, NA_HEADS, s, 1), F32)],
        scratch_shapes=[pltpu.VMEM((NA_FORMS, NA_Q, NA_KEYS), F32)],
        compiler_params=_params(dimension_semantics=("parallel", "parallel")),
    )(heads, heads, heads, bias)


def _na_attn_bwd(heads, bias, out, lse, dout, *, first, name):
    b, _, s, _ = heads.shape
    n_rows = s // GRID_W
    tiles = _na_tiles(n_rows)

    def body(q_ref, k_ref, v_ref, e_ref, o_ref, l_ref, do_ref, d_ref, de_ref, b_ref, db_ref):
        for form, rl, kl, i in tiles:
            _na_tile(b_ref, form, rl, kl)[...] = e_ref[i]
        d_ref[...] = jnp.zeros_like(d_ref)
        db_ref[...] = jnp.zeros_like(db_ref)

        def step(ib, carry):
            base, form = _na_block(ib, n_rows)
            rows = pl.ds(pl.multiple_of(ib * NA_Q, NA_Q), NA_Q)
            win = pl.ds(pl.multiple_of(base * GRID_W, GRID_W), NA_KEYS)
            q, k, v = q_ref[rows, :], k_ref[win, :], v_ref[win, :]
            do = do_ref[rows, :]
            delta = jnp.sum(do * o_ref[rows, :].astype(F32), axis=1, keepdims=True)
            do_b = do.astype(BF16)
            p = jnp.exp(_dot(q, k, _NT) + b_ref[form] - l_ref[rows, :])
            ds = p * (_dot(do_b, v, _NT) - delta)
            db_ref[form] += ds
            ds_b = ds.astype(BF16)
            d_ref[0, rows, :] = _dot(ds_b, k, _NN)
            d_ref[1, win, :] += _dot(ds_b, q, _TN)
            d_ref[2, win, :] += _dot(p.astype(BF16), do_b, _TN)
            return carry

        lax.fori_loop(0, n_rows // NA_BLOCK, step, 0, unroll=BLOCKS_IN_FLIGHT)
        acc = [None] * NA_BIAS_ROWS
        for form, rl, kl, i in tiles:
            if i < NA_BIAS_ROWS:
                t = _na_tile(db_ref, form, rl, kl)[...]
                acc[i] = t if acc[i] is None else acc[i] + t
        for i in range(NA_BIAS_ROWS):
            de_ref[i] = acc[i]

    per_head = lambda bi, h: (bi, h, 0, 0)
    return pl.pallas_call(
        body, name=name, grid=(b, NA_HEADS),
        in_specs=[_na_head_spec(part, first, s) for part in range(3)]
        + [pl.BlockSpec((None, NA_BIAS_ROWS + 1, GRID_W, GRID_W), lambda bi, h: (h, 0, 0, 0)),
           pl.BlockSpec((None, None, s, HEAD_DIM), per_head), pl.BlockSpec((None, None, s, 1), per_head),
           pl.BlockSpec((None, None, s, HEAD_DIM), per_head)],
        out_specs=[pl.BlockSpec((None, None, 3, s, HEAD_DIM), lambda bi, h: (bi, h, 0, 0, 0)),
                   pl.BlockSpec((None, None, NA_BIAS_ROWS, GRID_W, GRID_W), lambda bi, h: (bi, h, 0, 0, 0))],
        out_shape=[jax.ShapeDtypeStruct((b, NA_HEADS, 3, s, HEAD_DIM), F32),
                   jax.ShapeDtypeStruct((b, NA_HEADS, NA_BIAS_ROWS, GRID_W, GRID_W), F32)],
        scratch_shapes=[pltpu.VMEM((NA_FORMS, NA_Q, NA_KEYS), F32), pltpu.VMEM((NA_FORMS, NA_Q, NA_KEYS), F32)],
        compiler_params=_params(dimension_semantics=("parallel", "parallel")),
    )(heads, heads, heads, bias, out, lse, dout)


GATE_TILE = 256


def _gate_fwd(proj, z, *, gate_col, tt, name):
    _, t, d = z.shape
    nj = d // GATE_TILE
    c0 = gate_col // GATE_TILE

    def body(ga_ref, gb_ref, za_ref, zb_ref, o_ref):
        o_ref[...] = (jax.nn.sigmoid(ga_ref[...]) * za_ref[...] + jax.nn.sigmoid(gb_ref[...]) * zb_ref[...]).astype(BF16)

    return pl.pallas_call(
        body, name=name, grid=(t // tt, nj),
        in_specs=[pl.BlockSpec((tt, GATE_TILE), lambda i, j: (i, c0 + j)),
                  pl.BlockSpec((tt, GATE_TILE), lambda i, j: (i, c0 + nj + j)),
                  pl.BlockSpec((None, tt, GATE_TILE), lambda i, j: (0, i, j)),
                  pl.BlockSpec((None, tt, GATE_TILE), lambda i, j: (1, i, j))],
        out_specs=pl.BlockSpec((tt, GATE_TILE), lambda i, j: (i, j)), out_shape=jax.ShapeDtypeStruct((t, d), BF16),
        compiler_params=_params(dimension_semantics=("parallel", "parallel")),
    )(proj, proj, z, z)


def _gate_bwd(dm, proj, z, *, gate_col, tt, name):
    _, t, d = z.shape
    nj = d // GATE_TILE
    c0 = gate_col // GATE_TILE

    def body(dm_ref, g_ref, z_ref, dz_ref, dg_ref):
        dmv = dm_ref[...]
        sg = jax.nn.sigmoid(g_ref[...])
        dz_ref[...] = (dmv * sg).astype(BF16)
        dg_ref[...] = (dmv * z_ref[...] * sg * (1.0 - sg)).astype(BF16)

    return pl.pallas_call(
        body, name=name, grid=(t // tt, 2 * nj),
        in_specs=[pl.BlockSpec((tt, GATE_TILE), lambda i, j: (i, j % nj)),
                  pl.BlockSpec((tt, GATE_TILE), lambda i, j: (i, c0 + j)),
                  pl.BlockSpec((None, tt, GATE_TILE), lambda i, j: (j // nj, i, j % nj))],
        out_specs=[pl.BlockSpec((None, tt, GATE_TILE), lambda i, j: (j // nj, i, j % nj)),
                   pl.BlockSpec((tt, GATE_TILE), lambda i, j: (i, c0 + j))],
        out_shape=[jax.ShapeDtypeStruct((2, t, d), BF16), jax.ShapeDtypeStruct(proj.shape, BF16)],
        compiler_params=_params(dimension_semantics=("parallel", "parallel")),
    )(dm, proj, z)


def _adamw(w, g, m, v, *, name):
    shape = w.shape
    if w.ndim == 3:
        w2, g2, m2, v2 = w, g, m, v
    else:
        w2, g2, m2, v2 = (t.reshape(1, -1, shape[-1]) for t in (w, g, m, v))
    lead, rows, cols = w2.shape
    tr = rows
    for cand in (512, 256, 128, 64, 32, 16, 8):
        if rows % cand == 0:
            tr = cand
            break

    def body(w_ref, g_ref, m_ref, v_ref, d_ref, nm_ref, nv_ref):
        gv = g_ref[...]
        nm = ADAM_B1 * m_ref[...] + (1.0 - ADAM_B1) * gv
        nv = ADAM_B2 * v_ref[...] + (1.0 - ADAM_B2) * (gv * gv)
        m_hat = nm / (1.0 - ADAM_B1 ** ADAM_STEP)
        v_hat = nv / (1.0 - ADAM_B2 ** ADAM_STEP)
        d_ref[...] = -ADAM_LR * (m_hat / (jnp.sqrt(v_hat) + ADAM_EPS) + ADAM_WD * w_ref[...])
        nm_ref[...] = nm
        nv_ref[...] = nv

    blk = pl.BlockSpec((None, tr, cols), lambda l, i: (l, i, 0))
    out = jax.ShapeDtypeStruct((lead, rows, cols), F32)
    res = pl.pallas_call(
        body, name=name, grid=(lead, rows // tr), in_specs=[blk] * 4, out_specs=[blk] * 3, out_shape=[out] * 3,
        compiler_params=_params(dimension_semantics=("parallel", "parallel")),
    )(w2, g2, m2, v2)
    return tuple(t.reshape(shape) for t in res)


def _my_place():
    return lax.axis_index("x"), lax.axis_index("y"), lax.axis_index("c")


def _other_chips(x, y):
    return [(1 - x, y), (x, 1 - y), (1 - x, 1 - y)]


def _chip_no(chip):
    return 2 * chip[0] + chip[1]


def _window(ref, kind, size, chip, lead):
    if kind == "col":
        return ref.at[(*lead, slice(None), pl.ds(pl.multiple_of(chip * size, LANES), size))]
    if kind == "row":
        return ref.at[(*lead, pl.ds(pl.multiple_of(chip * size, BF16_ROWS), size), slice(None))]
    shard = size + HEAD_DIM
    if kind == "win_main":
        return ref.at[(*lead, slice(None), pl.ds(pl.multiple_of(chip * shard + HEAD_DIM * (chip % 2), LANES), size))]
    assert kind == "win_strad"
    return ref.at[(*lead, slice(None), pl.ds(pl.multiple_of(size + 2 * shard * (chip // 2), LANES), LANES))]


def _full_shape(shard, kind):
    _, k, n = shard.shape
    return {"col": (k, N_CHIPS * n), "row": (N_CHIPS * k, n), "win_main": (k, N_CHIPS * (n + HEAD_DIM)),
            "slot": (N_CHIPS, k, n)}[kind]


def _place_own(shard, kind, layer, *, name):
    _, k, n = shard.shape
    tr = _div_tile(k, 512, BF16_ROWS)
    tc = LANES if kind == "win_main" else n
    mine = 2 * lax.axis_index("x") + lax.axis_index("y")
    row0 = mine * (k // tr) if kind == "row" else 0
    col0 = {"col": mine, "row": 0, "slot": 0, "win_main": (mine * (n + HEAD_DIM) + HEAD_DIM * (mine % 2)) // LANES}[kind]
    scalars = jnp.stack([mine, row0, col0]).astype(jnp.int32)

    def body(s_ref, i_ref, o_ref):
        o_ref[...] = i_ref[...]

    if kind == "slot":
        o_spec = pl.BlockSpec((None, tr, tc), lambda i, j, s: (s[0], i, j))
    else:
        o_spec = pl.BlockSpec((tr, tc), lambda i, j, s: (s[1] + i, s[2] + j))
    return pl.pallas_call(
        body, name=name,
        grid_spec=pltpu.PrefetchScalarGridSpec(
            num_scalar_prefetch=1, grid=(k // tr, n // tc),
            in_specs=[pl.BlockSpec((None, tr, tc), lambda i, j, s: (layer, i, j))], out_specs=o_spec),
        out_shape=jax.ShapeDtypeStruct(_full_shape(shard, kind), shard.dtype),
        compiler_params=_params(dimension_semantics=("parallel", "parallel")),
    )(scalars, shard)


class _GatherPlan:
    def __init__(self, src, dst, shapes, kinds, layer, send_sems, recv_sems):
        self.src, self.dst, self.shapes, self.kinds, self.layer = src, dst, shapes, kinds, layer
        self.send_sems, self.recv_sems = send_sems, recv_sems
        self.x, self.y, self.c = _my_place()
        self.mine = 2 * self.x + self.y
        self.chips = _other_chips(self.x, self.y)
        self.n = len(src)

    def half(self, i, chip, half):
        _, k, n = self.shapes[i]
        kind, dst, hk = self.kinds[i], self.dst[i], k // 2
        if kind == "slot":
            return dst.at[chip, pl.ds(pl.multiple_of(half * hk, BF16_ROWS), hk), :]
        if kind == "row":
            return dst.at[pl.ds(pl.multiple_of(chip * k + half * hk, BF16_ROWS), hk), :]
        col0 = chip * n if kind == "col" else chip * (n + HEAD_DIM) + HEAD_DIM * (chip % 2)
        return dst.at[pl.ds(pl.multiple_of(half * hk, BF16_ROWS), hk), pl.ds(pl.multiple_of(col0, LANES), n)]

    def _copy(self, sem, window, to, source=None):
        return pltpu.make_async_remote_copy(src_ref=window if source is None else source, dst_ref=window,
                                            send_sem=self.send_sems.at[sem], recv_sem=self.recv_sems.at[sem],
                                            device_id=to, device_id_type=MESH)

    def sends(self):
        out = []
        for k, chip in enumerate(self.chips):
            for i in range(self.n):
                hk = self.shapes[i][1] // 2
                mine = self.src[i].at[self.layer, pl.ds(pl.multiple_of(self.c * hk, BF16_ROWS), hk), :]
                out.append(self._copy(3 * i + k, self.half(i, self.mine, self.c), (*chip, self.c), source=mine))
        return out

    def arrivals(self):
        return [self._copy(3 * i + k, self.half(i, _chip_no(chip), self.c), (*chip, self.c))
                for k, chip in enumerate(self.chips) for i in range(self.n)]

    def forwards(self, first_sem):
        sibling = (self.x, self.y, 1 - self.c)
        return [self._copy(first_sem + 3 * i + k, self.half(i, _chip_no(chip), self.c), sibling)
                for k, chip in enumerate(self.chips) for i in range(self.n)]

    def forwarded(self, first_sem):
        sibling = (self.x, self.y, 1 - self.c)
        return [self._copy(first_sem + 3 * i + k, self.half(i, _chip_no(chip), 1 - self.c), sibling)
                for k, chip in enumerate(self.chips) for i in range(self.n)]


IN_HBM = pl.BlockSpec(memory_space=pltpu.HBM)
IN_SEM = pl.BlockSpec(memory_space=pltpu.SEMAPHORE)
DATAFLOW = pltpu.SideEffectType.DATAFLOW_SIDE_EFFECTING


def _gather_layer_start(shards, kinds, fulls, layer, after, *, name):
    n_w = len(shards)
    shapes = [sh.shape for sh in shards]

    def body(*refs):
        plan = _GatherPlan(refs[:n_w], refs[n_w:2 * n_w], shapes, kinds, layer, refs[2 * n_w + 1], refs[2 * n_w + 2])
        for cp in plan.sends():
            cp.start()
        token = refs[-1]
        token[...] = jnp.zeros_like(token)

    operands = [pltpu.with_memory_space_constraint(a, pltpu.HBM) for a in (*shards, *fulls)]
    res = pl.pallas_call(
        body, name=name, in_specs=[IN_HBM] * (2 * n_w) + [pl.BlockSpec(memory_space=pl.ANY)],
        out_specs=(IN_SEM, IN_SEM, *([IN_HBM] * (2 * n_w)), pl.BlockSpec(memory_space=pltpu.VMEM)),
        out_shape=(pltpu.SemaphoreType.DMA((3 * n_w,)), pltpu.SemaphoreType.DMA((3 * n_w,)),
                   *[pltpu.HBM(a.shape, a.dtype) for a in operands], jax.ShapeDtypeStruct((8, LANES), F32)),
        input_output_aliases={i: 2 + i for i in range(2 * n_w)},
        compiler_params=pltpu.CompilerParams(has_side_effects=DATAFLOW),
    )(*operands, after)
    return res[0], res[1], res[2:2 + n_w], res[2 + n_w:2 + 2 * n_w], res[-1]


def _gather_layer_wait(send_sems, recv_sems, shards, fulls, kinds, layer, after, *, name):
    n_w = len(shards)
    shapes = [sh.shape for sh in shards]

    def body(*refs):
        plan = _GatherPlan(refs[:n_w], refs[n_w:2 * n_w], shapes, kinds, layer, refs[2 * n_w], refs[2 * n_w + 1])
        for cp in plan.sends():
            cp.wait_send()
        for cp in plan.arrivals():
            cp.wait_recv()

    res = pl.pallas_call(
        body, name=name, in_specs=[IN_HBM] * (2 * n_w) + [IN_SEM, IN_SEM, pl.BlockSpec(memory_space=pl.ANY)],
        out_specs=[IN_HBM] * (2 * n_w), out_shape=[pltpu.HBM(a.shape, a.dtype) for a in (*shards, *fulls)],
        input_output_aliases={i: i for i in range(2 * n_w)},
        compiler_params=pltpu.CompilerParams(has_side_effects=DATAFLOW),
    )(*shards, *fulls, send_sems, recv_sems, after)
    return res[n_w:]


def _gather_layer_forward(shapes, kinds, fulls, *, name):
    n_w = len(fulls)

    def body(*refs):
        plan = _GatherPlan([None] * n_w, refs[n_w:2 * n_w], shapes, kinds, 0, *refs[2 * n_w:])
        passed = plan.forwards(0)
        for cp in passed:
            cp.start()
        for cp in plan.forwarded(0):
            cp.wait_recv()
        for cp in passed:
            cp.wait_send()

    return pl.pallas_call(
        body, name=name, in_specs=[HBM] * n_w, out_specs=[HBM] * n_w,
        out_shape=[jax.ShapeDtypeStruct(f.shape, f.dtype) for f in fulls],
        input_output_aliases={i: i for i in range(n_w)},
        scratch_shapes=[pltpu.SemaphoreType.DMA((3 * n_w,)), pltpu.SemaphoreType.DMA((3 * n_w,))],
    )(*fulls)


def _on_core(layer):
    return (lax.axis_index("c") == layer).astype(jnp.int32).reshape(1)


N_DEVICES = 2 * N_CHIPS


class _ScatterPlan:
    def __init__(self, src, dst, kinds, sizes, layer, send_sems, recv_sems):
        self.src, self.dst, self.kinds, self.sizes, self.layer = src, dst, kinds, sizes, layer
        self.send_sems, self.recv_sems = send_sems, recv_sems
        self.x, self.y, self.c = _my_place()
        self.mine = 2 * self.x + self.y
        self.chips = _other_chips(self.x, self.y)
        self.n = len(src)

    def _copy(self, i, k, window_of, from_chip, from_core, to):
        return pltpu.make_async_remote_copy(src_ref=_window(self.src[i], self.kinds[i], self.sizes[i], window_of, ()),
                                            dst_ref=self.dst[i].at[2 * from_chip + from_core],
                                            send_sem=self.send_sems.at[4 * i + k],
                                            recv_sem=self.recv_sems.at[2 * (4 * i + k) + from_core],
                                            device_id=to, device_id_type=MESH)

    def to_chips(self):
        return [self._copy(i, k, _chip_no(chip), self.mine, self.c, (*chip, self.layer))
                for k, chip in enumerate(self.chips) for i in range(self.n)]

    def to_sibling(self):
        return [self._copy(i, 3, self.mine, self.mine, self.c, (self.x, self.y, self.layer)) for i in range(self.n)]

    def arrivals(self):
        out = [self._copy(i, k, self.mine, _chip_no(chip), core, (*chip, core))
               for k, chip in enumerate(self.chips) for core in (0, 1) for i in range(self.n)]
        return out + [self._copy(i, 3, self.mine, self.mine, 1 - self.layer, (self.x, self.y, 1 - self.layer))
                      for i in range(self.n)]


def _slab_shape(p, kind, size):
    return (N_DEVICES,) + {"col": (p.shape[0], size), "row": (size, p.shape[1]), "win_main": (p.shape[0], size),
                           "win_strad": (p.shape[0], LANES)}[kind]


def _grads_to_chips_start(pairs, kinds, sizes, layer, *, name):
    n_w = len(pairs)

    def body(*refs):
        plan = _ScatterPlan(refs[:n_w], refs[n_w:2 * n_w], kinds, sizes, layer, refs[2 * n_w], refs[2 * n_w + 1])
        for cp in plan.to_chips():
            cp.start()

        @pl.when(plan.c != layer)
        def _():
            for cp in plan.to_sibling():
                cp.start()

        token = refs[-1]
        token[...] = jnp.zeros_like(token)

    slabs = [lax.empty(_slab_shape(p, kind, size), p.dtype) for p, kind, size in zip(pairs, kinds, sizes)]
    operands = [pltpu.with_memory_space_constraint(a, pltpu.HBM) for a in (*pairs, *slabs)]
    res = pl.pallas_call(
        body, name=name, in_specs=[IN_HBM] * (2 * n_w),
        out_specs=(IN_SEM, IN_SEM, *([IN_HBM] * (2 * n_w)), pl.BlockSpec(memory_space=pltpu.VMEM)),
        out_shape=(pltpu.SemaphoreType.DMA((4 * n_w,)), pltpu.SemaphoreType.DMA((8 * n_w,)),
                   *[pltpu.HBM(a.shape, a.dtype) for a in operands], jax.ShapeDtypeStruct((8, LANES), F32)),
        input_output_aliases={i: 2 + i for i in range(2 * n_w)},
        compiler_params=pltpu.CompilerParams(has_side_effects=DATAFLOW),
    )(*operands)
    return res[0], res[1], res[2:2 + n_w], res[2 + n_w:2 + 2 * n_w], res[-1]


def _grads_to_chips_wait(send_sems, recv_sems, pairs, slabs, kinds, sizes, layer, after, *, name):
    n_w = len(pairs)

    def body(*refs):
        plan = _ScatterPlan(refs[:n_w], refs[n_w:2 * n_w], kinds, sizes, layer, refs[2 * n_w], refs[2 * n_w + 1])
        for cp in plan.to_chips():
            cp.wait_send()

        @pl.when(plan.c != layer)
        def _():
            for cp in plan.to_sibling():
                cp.wait_send()

        @pl.when(plan.c == layer)
        def _():
            for cp in plan.arrivals():
                cp.wait_recv()

    res = pl.pallas_call(
        body, name=name, in_specs=[IN_HBM] * (2 * n_w) + [IN_SEM, IN_SEM, pl.BlockSpec(memory_space=pl.ANY)],
        out_specs=[IN_HBM] * (2 * n_w), out_shape=[pltpu.HBM(a.shape, a.dtype) for a in (*pairs, *slabs)],
        input_output_aliases={i: i for i in range(2 * n_w)},
        compiler_params=pltpu.CompilerParams(has_side_effects=DATAFLOW),
    )(*pairs, *slabs, send_sems, recv_sems, after)
    return res[:n_w], res[n_w:]


def _sum_slabs(slabs, pair, kind, size, layer, into, *, name):
    n_s, k, n = slabs.shape
    tr = _div_tile(k, 512, BF16_ROWS)
    tc = n if kind in ("col", "row") else LANES
    x, y, _ = _my_place()
    mine = 2 * x + y
    shard = size + HEAD_DIM
    row0 = mine * (k // tr) if kind == "row" else 0
    col0 = {"col": mine, "row": 0, "win_main": (mine * shard + HEAD_DIM * (mine % 2)) // LANES,
            "win_strad": (size + 2 * shard * (mine // 2)) // LANES}[kind]
    on = _on_core(layer)[0]
    scalars = jnp.stack([2 * mine + layer, row0 * on, col0 * on, on]).astype(jnp.int32)

    def body(s_ref, slab_ref, own_ref, *rest):
        o_ref = rest[-1]
        me = s_ref[0]

        @pl.when(s_ref[3] == 1)
        def _():
            acc = jnp.zeros(o_ref.shape, F32)
            for i in range(n_s):
                acc = acc + jnp.where(me == i, own_ref[...], slab_ref[i]).astype(F32)
            o_ref[...] = acc

    operands = [scalars, slabs, pair] + ([] if into is None else [into])
    return pl.pallas_call(
        body, name=name,
        grid_spec=pltpu.PrefetchScalarGridSpec(
            num_scalar_prefetch=1, grid=(k // tr, n // tc),
            in_specs=[pl.BlockSpec((n_s, tr, tc), lambda i, j, s: (0, i * s[3], j * s[3])),
                      pl.BlockSpec((tr, tc), lambda i, j, s: (s[1] + i * s[3], s[2] + j * s[3]))]
            + ([] if into is None else [HBM]),
            out_specs=pl.BlockSpec((None, tr, tc), lambda i, j, s: (layer, i * s[3], j * s[3]))),
        out_shape=jax.ShapeDtypeStruct((2, k, n), F32),
        input_output_aliases={} if into is None else {3: 0},
        compiler_params=_params(dimension_semantics=("arbitrary", "arbitrary")),
    )(*operands)


def _exchange_layers(bufs, *, name):
    n_w = len(bufs)

    def body(*refs):
        dst = refs[n_w:2 * n_w]
        send_sems, recv_sems = refs[2 * n_w:]
        x, y, c = _my_place()

        def copy(i, layer):
            return pltpu.make_async_remote_copy(src_ref=dst[i].at[layer], dst_ref=dst[i].at[layer], send_sem=send_sems.at[i],
                                                recv_sem=recv_sems.at[i], device_id=(x, y, 1 - c), device_id_type=MESH)

        sends = [copy(i, c) for i in range(n_w)]
        for cp in sends:
            cp.start()
        for i in range(n_w):
            copy(i, 1 - c).wait_recv()
        for cp in sends:
            cp.wait_send()

    return pl.pallas_call(
        body, name=name, in_specs=[HBM] * n_w, out_specs=[HBM] * n_w,
        out_shape=[jax.ShapeDtypeStruct(b.shape, b.dtype) for b in bufs],
        input_output_aliases={i: i for i in range(n_w)},
        scratch_shapes=[pltpu.SemaphoreType.DMA((n_w,)), pltpu.SemaphoreType.DMA((n_w,))],
    )(*bufs)


def _all_sum_small(v, *, name):
    r = v.shape[0]
    relations = [(dx, dy, dc) for dx in (0, 1) for dy in (0, 1) for dc in (0, 1)][1:]

    def body(v_ref, o_ref, buf, send_sems, recv_sems):
        x, y, c = _my_place()
        me = 4 * x + 2 * y + c
        buf[me] = v_ref[...]
        peers = [(x + dx - 2 * x * dx, y + dy - 2 * y * dy, c + dc - 2 * c * dc) for dx, dy, dc in relations]

        def copy(k, slot):
            return pltpu.make_async_remote_copy(src_ref=v_ref, dst_ref=buf.at[slot], send_sem=send_sems.at[k],
                                                recv_sem=recv_sems.at[k], device_id=peers[k], device_id_type=MESH)

        sends = [copy(k, me) for k in range(len(relations))]
        for cp in sends:
            cp.start()
        for k, (px, py, pc) in enumerate(peers):
            copy(k, 4 * px + 2 * py + pc).wait_recv()
        for cp in sends:
            cp.wait_send()
        acc = buf[0]
        for i in range(1, 8):
            acc = acc + buf[i]
        o_ref[...] = acc

    vm = pl.BlockSpec(memory_space=pltpu.VMEM)
    return pl.pallas_call(
        body, name=name, in_specs=[vm], out_specs=vm, out_shape=jax.ShapeDtypeStruct((r, LANES), F32),
        scratch_shapes=[pltpu.VMEM((8, r, LANES), F32), pltpu.SemaphoreType.DMA((7,)), pltpu.SemaphoreType.DMA((7,))],
    )(v)


SHARDED = (("ffn1_w_up", "col"), ("ffn1_w_down", "row"), ("w_in", "win"), ("w_branch_a", "col"),
           ("w_branch_b", "col"), ("w_out", "row"), ("ffn2_w_up", "col"), ("ffn2_w_down", "row"))
REPLICATED = ("ffn1_norm", "mix_norm", "na_rel_bias", "ffn2_norm", "final_norm")


def _weight_pieces(w):
    even = lax.axis_index("y") == 0
    shards, kinds, names = [], [], []
    for name, kind in SHARDED:
        wb = w[name].astype(BF16)
        if kind == "win":
            main = wb.shape[-1] - HEAD_DIM
            assert main % LANES == 0
            zeros = jnp.zeros(wb.shape[:-1] + (HEAD_DIM,), BF16)
            shards += [jnp.where(even, wb[..., :main], wb[..., HEAD_DIM:]),
                       jnp.where(even, jnp.concatenate([wb[..., main:], zeros], -1),
                                 jnp.concatenate([zeros, wb[..., :HEAD_DIM]], -1))]
            kinds += ["win_main", "slot"]
            names += [name, name + "_strad"]
        else:
            shards.append(wb)
            kinds.append(kind)
            names.append(name)
    return names, kinds, shards


def _finish_w_in(full):
    full = dict(full)
    strad = full.pop("w_in_strad")
    main = full["w_in"].shape[1] // N_CHIPS - HEAD_DIM
    for i in range(N_CHIPS // 2):
        lo = main + 2 * (main + HEAD_DIM) * i
        full["w_in"] = full["w_in"].at[:, lo:lo + LANES].set(strad[2 * i] + strad[2 * i + 1])
    return full


def _scatter_pieces(shards):
    names, kinds, sizes, srcs = [], [], [], []
    for name, kind in SHARDED:
        shp = shards[name].shape
        if kind == "win":
            names += [name, name + "_strad"]
            kinds += ["win_main", "win_strad"]
            sizes += [shp[2] - HEAD_DIM] * 2
            srcs += [name, name]
        else:
            names.append(name)
            kinds.append(kind)
            sizes.append(shp[1] if kind == "row" else shp[2])
            srcs.append(name)
    return names, kinds, sizes, srcs


def _finish_weight_grads(reduced, names, tag):
    out = dict(zip(names, _exchange_layers(reduced, name=f"{tag}_layers")))
    if "w_in_strad" in out:
        strad = out.pop("w_in_strad")
        even = lax.axis_index("y") == 0
        out["w_in"] = jnp.where(even, jnp.concatenate([out["w_in"], strad[..., :HEAD_DIM]], -1),
                                jnp.concatenate([strad[..., HEAD_DIM:], out["w_in"]], -1))
    return out


class _Grads:
    def __init__(self):
        self.arrays = {}

    def put(self, weight, layer, a, b, *, cols=None, col_off=0, **kw):
        self.arrays[weight, layer] = _mm(a, b, mode="tn", out_dtype=BF16, out_cols=cols, out_col_off=col_off,
                                         out_into=self.arrays.get((weight, layer)), **kw)


def _ffn_fwd(x, h, w_up, w_down, tag):
    t, d = x.shape
    f = w_down.shape[0]
    a, gate, up = _mm_swiglu_fwd(h, w_up, tm=_div_tile(t, ROWS_NARROW, 8), tn=MXU_N, name=f"{tag}_up")
    x_out = _mm(a, w_down, mode="nn", out_dtype=F32, tm=_div_tile(t, ROWS_WIDE, 8), tn=d, tk=f, alpha=0.5, res=x, name=f"{tag}_down")
    return x_out, (x, h, a, gate, up)


def _ffn_bwd(dx, dxb, saved, norm_g, w_up, w_down, layer, grads, wname, tag, scatter):
    x, h, a, gate, up = saved
    t, d = x.shape
    f = w_down.shape[0]
    tn = _div_tile(f, 1408)
    grads.put(f"{wname}_w_down", layer, a, dxb, tm=tn, tn=d, tk=ROWS_CONTRACTED, alpha=0.5, name=f"{tag}_dwd")
    d_gate, d_up = _mm_swiglu_bwd(dxb, w_down, gate, up, alpha=0.5, tm=_div_tile(t, ROWS_NARROW, 8), tn=MXU_N, name=f"{tag}_da")
    grads.put(f"{wname}_w_up", layer, h, d_gate, cols=2 * f, tm=d, tn=tn, tk=ROWS_CONTRACTED, name=f"{tag}_dwg")
    grads.put(f"{wname}_w_up", layer, h, d_up, cols=2 * f, col_off=f // tn, tm=d, tn=tn, tk=ROWS_CONTRACTED, name=f"{tag}_dwu")
    started = scatter(layer, [f"{wname}_w_up", f"{wname}_w_down"])
    dh = _mm(d_gate, w_up, mode="nt", out_dtype=F32, tm=_div_tile(t, ROWS_WIDE, 8), tn=d, tk=f, name=f"{tag}_dh1")
    dh = _mm(d_up, w_up, mode="nt", out_dtype=F32, tm=_div_tile(t, ROWS_WIDE, 8), tn=d, tk=f, b_k_off=1, res=dh, name=f"{tag}_dh2")
    return _rms_bwd(dh, x, norm_g + started, dx, tt=NORM_ROWS, name=f"{tag}_dnorm")


def _to_heads(y, b, n_heads):
    t, w = y.shape
    return y.reshape(b, t // b, n_heads, HEAD_DIM).transpose(0, 2, 1, 3)


def _from_heads(y):
    b, n, s, hd = y.shape
    return y.transpose(0, 2, 1, 3).reshape(b * s, n * hd)


N_QKV = 3 * (DIL_HEADS + NA_HEADS) * HEAD_DIM


def _mixer_fwd(x, b, norm_g, full, bias, tabs, tag):
    t, d = x.shape
    s = t // b
    n_in = full["w_in"].shape[1]
    h = _rms_fwd(x, norm_g, tt=NORM_ROWS, name=f"{tag}_norm")
    proj = _mm(h, full["w_in"], mode="nn", out_dtype=F32, tm=_div_tile(t, ROWS_NARROW, 8), tn=MXU_N, tk=d, name=f"{tag}_in")
    heads = _split_heads(proj.reshape(b, s, -1), *tabs, n_pairs=N_QKV // LANES, rot_pairs=DIL_HEADS,
                         scale_ranges=((0, DIL_HEADS // 2), (3 * DIL_HEADS // 2, (3 * DIL_HEADS + NA_HEADS) // 2)),
                         name=f"{tag}_heads")
    ya, ya_heads, lse_a = _dil_attn_fwd(heads, name=f"{tag}_dil")
    yb, lse_b = _na_attn_fwd(heads, bias, first=3 * DIL_HEADS, name=f"{tag}_na")
    ya2, yb2 = ya.reshape(t, -1), _from_heads(yb)
    z = _mm(ya2, full["w_branch_a"], mode="nn", out_dtype=F32, tm=_div_tile(t, ROWS_NARROW, 8), tn=MXU_N, tk=ya2.shape[1],
            out_slab=(0, 2), name=f"{tag}_za")
    z = _mm(yb2, full["w_branch_b"], mode="nn", out_dtype=F32, tm=_div_tile(t, ROWS_NARROW, 8), tn=MXU_N, tk=yb2.shape[1],
            out_slab=(1, 2), out_into=z, name=f"{tag}_zb")
    merged = _gate_fwd(proj, z, gate_col=N_QKV, tt=GATE_ROWS, name=f"{tag}_gate")
    x_out = _mm(merged, full["w_out"], mode="nn", out_dtype=F32, tm=_div_tile(t, ROWS_NARROW, 8), tn=MXU_N, tk=d, res=x, name=f"{tag}_out")
    return x_out, (x, h, proj, heads, ya_heads, lse_a, yb, lse_b, ya2, yb2, z, merged)


def _mixer_bwd(dx, dob, b, saved, norm_g, full, layer, bias, tabs, grads, tag, scatter):
    x, h, proj, heads, ya, lse_a, yb, lse_b, ya2, yb2, z, merged = saved
    t, d = x.shape
    s = t // b
    n_in = full["w_in"].shape[1]
    grads.put("w_out", layer, merged, dob, tm=d, tn=d, tk=ROWS_CONTRACTED, name=f"{tag}_dwo")
    dm = _mm(dob, full["w_out"], mode="nt", out_dtype=F32, tm=_div_tile(t, ROWS_NARROW, 8), tn=MXU_N, tk=d, name=f"{tag}_dm")
    dz, dproj = _gate_bwd(dm, proj, z, gate_col=N_QKV, tt=GATE_ROWS, name=f"{tag}_dgate")
    grads.put("w_branch_a", layer, ya2, dz, b_sel=0, tm=ya2.shape[1], tn=d, tk=ROWS_CONTRACTED, name=f"{tag}_dwa")
    grads.put("w_branch_b", layer, yb2, dz, b_sel=1, tm=yb2.shape[1], tn=d, tk=ROWS_CONTRACTED, name=f"{tag}_dwb")
    started = scatter(layer, ["w_out", "w_branch_a", "w_branch_b"])
    dya = _mm(dz, full["w_branch_a"], mode="nt", out_dtype=F32, tm=_div_tile(t, ROWS_NARROW, 8), tn=MXU_N, tk=d, a_sel=0, name=f"{tag}_dya")
    dyb = _mm(dz, full["w_branch_b"], mode="nt", out_dtype=F32, tm=_div_tile(t, ROWS_NARROW, 8), tn=MXU_N, tk=d, a_sel=1, name=f"{tag}_dyb")
    d_dil = _dil_attn_bwd(heads, ya, lse_a, _to_heads(dya, b, DIL_GROUP_HEADS), name=f"{tag}_ddil")
    d_na, d_bias = _na_attn_bwd(heads, bias, yb, lse_b, _to_heads(dyb, b, NA_HEADS), first=3 * DIL_HEADS, name=f"{tag}_dna")
    dproj = _merge_heads(d_dil, *tabs, heads_per_row=DIL_GROUP_HEADS, rot_pairs=DIL_HEADS, scale_pairs=DIL_HEADS // 2,
                         dilated=True, out_cols=n_in, tile_off=0, into=dproj.reshape(b, s, n_in), name=f"{tag}_dheads_a")
    dproj = _merge_heads(d_na, *tabs, heads_per_row=NA_HEADS, rot_pairs=0, scale_pairs=NA_HEADS // 2, dilated=False,
                         out_cols=n_in, tile_off=3 * DIL_HEADS // 2, into=dproj, name=f"{tag}_dheads_b").reshape(t, n_in)
    grads.put("w_in", layer, h, dproj, tm=_div_tile(d, 512), tn=_div_tile(n_in, 2944), tk=ROWS_CONTRACTED // 2, name=f"{tag}_dwin")
    started = started + scatter(layer, ["w_in"])
    dh = _mm(dproj, full["w_in"], mode="nt", out_dtype=F32, tm=_div_tile(t, 2 * ROWS_WIDE, 8), tn=d, tk=_div_tile(n_in, 2944),
             name=f"{tag}_dh")
    dx_in, dxb_in, d_norm = _rms_bwd(dh, x, norm_g + started, dx, tt=NORM_ROWS, name=f"{tag}_dnorm")
    d_rb = _na_collapse_bias(d_bias, name=f"{tag}_dbias")
    return dx_in, dxb_in, d_norm, d_rb


def kernel(x, ffn1_norm, ffn1_w_up, ffn1_w_down, mix_norm, w_in, na_rel_bias, w_branch_a, w_branch_b, w_out, ffn2_norm, ffn2_w_up, ffn2_w_down, final_norm, loss_target, m_ffn1_norm, m_ffn1_w_up, m_ffn1_w_down, m_mix_norm, m_w_in, m_na_rel_bias, m_w_branch_a, m_w_branch_b, m_w_out, m_ffn2_norm, m_ffn2_w_up, m_ffn2_w_down, m_final_norm, v_ffn1_norm, v_ffn1_w_up, v_ffn1_w_down, v_mix_norm, v_w_in, v_na_rel_bias, v_w_branch_a, v_w_branch_b, v_w_out, v_ffn2_norm, v_ffn2_w_up, v_ffn2_w_down, v_final_norm):
    w = dict(ffn1_norm=ffn1_norm, ffn1_w_up=ffn1_w_up, ffn1_w_down=ffn1_w_down, mix_norm=mix_norm, w_in=w_in,
             na_rel_bias=na_rel_bias, w_branch_a=w_branch_a, w_branch_b=w_branch_b, w_out=w_out, ffn2_norm=ffn2_norm,
             ffn2_w_up=ffn2_w_up, ffn2_w_down=ffn2_w_down, final_norm=final_norm)
    mom = dict(ffn1_norm=m_ffn1_norm, ffn1_w_up=m_ffn1_w_up, ffn1_w_down=m_ffn1_w_down, mix_norm=m_mix_norm, w_in=m_w_in,
               na_rel_bias=m_na_rel_bias, w_branch_a=m_w_branch_a, w_branch_b=m_w_branch_b, w_out=m_w_out,
               ffn2_norm=m_ffn2_norm, ffn2_w_up=m_ffn2_w_up, ffn2_w_down=m_ffn2_w_down, final_norm=m_final_norm)
    var = dict(ffn1_norm=v_ffn1_norm, ffn1_w_up=v_ffn1_w_up, ffn1_w_down=v_ffn1_w_down, mix_norm=v_mix_norm, w_in=v_w_in,
               na_rel_bias=v_na_rel_bias, w_branch_a=v_w_branch_a, w_branch_b=v_w_branch_b, w_out=v_w_out,
               ffn2_norm=v_ffn2_norm, ffn2_w_up=v_ffn2_w_up, ffn2_w_down=v_ffn2_w_down, final_norm=v_final_norm)
    b, s, d = x.shape
    t = b * s
    depth = ffn1_norm.shape[0]
    assert depth == 2, "core c of a chip sends / reduces layer c"
    shards = {name: w[name] for name, _ in SHARDED}

    names, kinds, pieces = _weight_pieces(w)
    by_layer = [[p[l:l + 1] for p in pieces] for l in range(depth)]
    own = [[_place_own(p, kind, 0, name=f"own{l}_{nm}") for nm, kind, p in zip(names, kinds, by_layer[l])] for l in range(depth)]
    full = [{}, {}]

    def gather_start(layer, group, after, tag):
        idx = [i for i, nm in enumerate(names) if nm in group]
        pick = lambda seq: [seq[i] for i in idx]
        *state, token = _gather_layer_start(pick(by_layer[layer]), pick(kinds), pick(own[layer]), 0, after, name=f"{tag}_start")
        return (layer, idx, tag, state), token[:1, :1]

    def gather_finish(started, after):
        layer, idx, tag, state = started
        pick = lambda seq: [seq[i] for i in idx]
        landed = _gather_layer_wait(*state, pick(kinds), 0, after, name=f"{tag}_wait")
        done = _gather_layer_forward([by_layer[layer][i].shape for i in idx], pick(kinds), landed, name=f"{tag}_forward")
        full[layer].update(zip(pick(names), done))
        return done[0]

    ffn1, mixer, ffn2 = names[:2], names[2:7], names[7:]
    assert mixer[0] == "w_in" and ffn2[0] == "ffn2_w_up", names
    xc = x.reshape(t, d)
    l0_ffn1, token_ffn1 = gather_start(0, ffn1, xc, "gather_l0_ffn1")
    tabs = _rope_tables(s)
    bias = _na_expand_bias(na_rel_bias, name="na_bias")

    saved = []
    h = _rms_fwd(xc, ffn1_norm[:1] + token_ffn1, tt=NORM_ROWS, name="l0_ffn1_norm")
    landed = gather_finish(l0_ffn1, h)
    l0_mixer, token_mixer = gather_start(0, mixer, landed, "gather_l0_mixer")
    xc, s1 = _ffn_fwd(xc, h + token_mixer.astype(BF16), full[0]["ffn1_w_up"], full[0]["ffn1_w_down"], "l0_ffn1")
    landed = gather_finish(l0_mixer, xc)
    full[0] = _finish_w_in(full[0])
    l0_ffn2, token_ffn2 = gather_start(0, ffn2, landed, "gather_l0_ffn2")
    layer1, token_layer1 = gather_start(1, names, landed, "gather_l1")
    xc, s2 = _mixer_fwd(xc, b, mix_norm[:1] + token_ffn2 + token_layer1, full[0], bias[0], tabs, "l0_mix")
    gather_finish(l0_ffn2, xc)
    xc, s3 = _ffn_fwd(xc, _rms_fwd(xc, ffn2_norm[:1], tt=NORM_ROWS, name="l0_ffn2_norm"), full[0]["ffn2_w_up"], full[0]["ffn2_w_down"],
                      "l0_ffn2")
    saved.append((s1, s2, s3))
    gather_finish(layer1, xc)
    full[1] = _finish_w_in(full[1])
    for l in range(1, depth):
        xc, s1 = _ffn_fwd(xc, _rms_fwd(xc, ffn1_norm[l:l + 1], tt=NORM_ROWS, name=f"l{l}_ffn1_norm"), full[l]["ffn1_w_up"],
                          full[l]["ffn1_w_down"], f"l{l}_ffn1")
        xc, s2 = _mixer_fwd(xc, b, mix_norm[l:l + 1], full[l], bias[l], tabs, f"l{l}_mix")
        xc, s3 = _ffn_fwd(xc, _rms_fwd(xc, ffn2_norm[l:l + 1], tt=NORM_ROWS, name=f"l{l}_ffn2_norm"), full[l]["ffn2_w_up"],
                          full[l]["ffn2_w_down"], f"l{l}_ffn2")
        saved.append((s1, s2, s3))

    dx, dxb, d_final, loss_part = _final_loss(xc, final_norm.reshape(1, d), loss_target.reshape(t, d), tt=NORM_ROWS, name="final_loss")
    grads = _Grads()
    piece_names, piece_kinds, piece_sizes, piece_srcs = _scatter_pieces(shards)
    scattered = []

    def scatter(layer, weights):
        tag = f"grads{layer}_{weights[0]}"
        idx = [i for i, src in enumerate(piece_srcs) if src in weights]
        pick = lambda seq: [seq[i] for i in idx]
        *state, token = _grads_to_chips_start([grads.arrays[src, layer] for src in pick(piece_srcs)], pick(piece_kinds),
                                              pick(piece_sizes), layer, name=f"{tag}_to_chips_start")
        scattered.append((layer, idx, state))
        return token[:1, :1]
    small = {name: [None] * depth for name in REPLICATED[:-1]}
    for l in reversed(range(depth)):
        s1, s2, s3 = saved[l]
        dx, dxb, small["ffn2_norm"][l] = _ffn_bwd(dx, dxb, s3, ffn2_norm[l:l + 1], full[l]["ffn2_w_up"], full[l]["ffn2_w_down"],
                                                  l, grads, "ffn2", f"l{l}_ffn2", scatter)
        dx, dxb, small["mix_norm"][l], small["na_rel_bias"][l] = _mixer_bwd(
            dx, dxb, b, s2, mix_norm[l:l + 1], full[l], l, bias[l], tabs, grads, f"l{l}_mix", scatter)
        dx, dxb, small["ffn1_norm"][l] = _ffn_bwd(dx, dxb, s1, ffn1_norm[l:l + 1], full[l]["ffn1_w_up"], full[l]["ffn1_w_down"],
                                                  l, grads, "ffn1", f"l{l}_ffn1", scatter)
    grad_x = dx.reshape(b, s, d)
    reduced = [None] * len(piece_names)

    def arrive(group, after):
        layer, idx, state = group
        state = _grads_to_chips_wait(*state, [piece_kinds[i] for i in idx], [piece_sizes[i] for i in idx], layer, after,
                                     name=f"grads{layer}_{piece_names[idx[0]]}_to_chips_wait")
        for i, p, sl in zip(idx, *state):
            reduced[i] = _sum_slabs(sl, p, piece_kinds[i], piece_sizes[i], layer, reduced[i],
                                    name=f"grads{layer}_sum_{piece_names[i]}")
        return idx

    for group in scattered[:-1]:
        arrive(group, dx)
    late = scattered[-1][1]
    early = [i for i in range(len(piece_names)) if i not in late]
    g_out = _finish_weight_grads([reduced[i] for i in early], [piece_names[i] for i in early], "grads_early")

    parts = [jnp.stack(small[name]).reshape(-1) for name in REPLICATED[:-1]] + [d_final.reshape(-1), loss_part[0, :1]]
    sizes = [v.shape[0] for v in parts]
    flat = jnp.concatenate(parts)
    flat = jnp.pad(flat, (0, -flat.shape[0] % (8 * LANES)))
    small_sum = _all_sum_small(flat.reshape(-1, LANES), name="small_all_sum").reshape(-1)
    off = 0
    for name, n in zip(REPLICATED, sizes[:-1]):
        g_out[name] = small_sum[off:off + n].reshape(w[name].shape)
        off += n
    loss = small_sum[off]

    names = list(w)
    delta, new_m, new_v = {}, {}, {}
    for name in [n for n in names if n in g_out]:
        delta[name], new_m[name], new_v[name] = _adamw(w[name], g_out[name], mom[name], var[name], name=f"adamw_{name}")
    arrive(scattered[-1], delta["w_in"])
    g_out.update(_finish_weight_grads([reduced[i] for i in late], [piece_names[i] for i in late], "grads_late"))
    for name in [n for n in names if n not in delta]:
        delta[name], new_m[name], new_v[name] = _adamw(w[name], g_out[name], mom[name], var[name], name=f"adamw_{name}")
    return (loss, grad_x, *[g_out[n] for n in names], *[delta[n] for n in names], *[new_m[n] for n in names],
            *[new_v[n] for n in names])
```

```python
import functools

import numpy as np
import jax
import jax.numpy as jnp
from jax import lax
from jax.experimental import pallas as pl
from jax.experimental.pallas import tpu as pltpu

F32, BF16 = jnp.float32, jnp.bfloat16
MESH = pl.DeviceIdType.MESH

HEAD_DIM = 64
DILATIONS = (1, 4, 16)
DIL_HALF = 64
DIL_GROUP_HEADS = 4
DIL_HEADS = 12
NA_HEADS = 8
GRID_W = 64
NA_ROWS = 8
NA_COLS = 16
ROPE_THETA = 10000.0
RMS_EPS = 1e-6
NEG_INF = -1e30
ADAM_LR, ADAM_B1, ADAM_B2, ADAM_EPS, ADAM_WD, ADAM_STEP = 0.001, 0.9, 0.999, 1e-08, 0.01, 10
QK_SCALE = HEAD_DIM ** -0.5

N_CHIPS = 4
LANES = 128
BF16_ROWS = 16
VMEM_LIMIT = 56 * 1024 * 1024
MXU_N = 256
ROWS_NARROW = 4096
ROWS_WIDE = 512
NORM_ROWS = 1024
GATE_ROWS = 2048
ROWS_CONTRACTED = 4096
BLOCKS_IN_FLIGHT = 8

_NN = (((1,), (0,)), ((), ()))
_NT = (((1,), (1,)), ((), ()))
_TN = (((0,), (0,)), ((), ()))

HBM = pl.BlockSpec(memory_space=pl.ANY)


def _params(**kw):
    return pltpu.CompilerParams(vmem_limit_bytes=VMEM_LIMIT, **kw)


def _dot(a, b, dims):
    return lax.dot_general(a, b, dims, preferred_element_type=F32)


def _div_tile(n, cap, mult=LANES):
    best = None
    for t in range(mult, min(n, cap) + 1, mult):
        if n % t == 0:
            best = t
    return n if best is None else best


def _stacked(block, index, sel):
    if sel is None:
        return pl.BlockSpec(block, index)
    return pl.BlockSpec((None,) + block, lambda *g: (sel,) + index(*g))


def _mm(a, b, *, mode, out_dtype, tm, tn, tk, name, alpha=1.0, res=None, a_sel=None, b_sel=None, b_k_off=0,
        out_slab=None, out_cols=None, out_col_off=0, out_into=None):
    a2, b2 = a.shape[-2:], b.shape[-2:]
    if mode == "nn":
        (m, k), n = a2, b2[1]
        a_spec = _stacked((tm, tk), lambda i, j, kk: (i, kk), a_sel)
        b_spec = _stacked((tk, tn), lambda i, j, kk: (kk + b_k_off, j), b_sel)
        dims = _NN
    elif mode == "nt":
        (m, k), n = a2, b2[0]
        a_spec = _stacked((tm, tk), lambda i, j, kk: (i, kk), a_sel)
        b_spec = _stacked((tn, tk), lambda i, j, kk: (j, kk + b_k_off), b_sel)
        dims = _NT
    else:
        (k, m), n = a2, b2[1]
        a_spec = _stacked((tk, tm), lambda i, j, kk: (kk, i), a_sel)
        b_spec = _stacked((tk, tn), lambda i, j, kk: (kk + b_k_off, j), b_sel)
        dims = _TN
    assert m % tm == 0 and n % tn == 0 and k % tk == 0, (name, a.shape, b.shape)
    nk = k // tk
    has_res = res is not None
    if out_slab is None:
        o_spec = pl.BlockSpec((tm, tn), lambda i, j, kk: (i, j + out_col_off))
        out_shape = jax.ShapeDtypeStruct((m, n if out_cols is None else out_cols), out_dtype)
    else:
        o_spec = _stacked((tm, tn), lambda i, j, kk: (i, j + out_col_off), out_slab[0])
        out_shape = jax.ShapeDtypeStruct((out_slab[1], m, n if out_cols is None else out_cols), out_dtype)
    r_spec = pl.BlockSpec((tm, tn), lambda i, j, kk: (i, j))
    n_in = 2 + has_res + (out_into is not None)

    def body(*refs):
        a_ref, b_ref = refs[0], refs[1]
        r_ref = refs[2] if has_res else None
        o_ref = refs[n_in]
        p = _dot(a_ref[...], b_ref[...], dims)

        def finish(acc):
            y = acc * alpha if alpha != 1.0 else acc
            if has_res:
                y = y + r_ref[...].astype(F32)
            o_ref[...] = y.astype(o_ref.dtype)

        if nk == 1:
            finish(p)
        else:
            acc_ref = refs[n_in + 1]
            kk = pl.program_id(2)

            @pl.when(kk == 0)
            def _():
                acc_ref[...] = p

            @pl.when(kk > 0)
            def _():
                acc_ref[...] += p

            @pl.when(kk == nk - 1)
            def _():
                finish(acc_ref[...])

    operands = [a, b] + ([res] if has_res else [])
    in_specs = [a_spec, b_spec] + ([r_spec] if has_res else [])
    aliases = {}
    if out_into is not None:
        aliases = {len(operands): 0}
        operands.append(out_into)
        in_specs.append(HBM)
    return pl.pallas_call(
        body, name=name, grid=(m // tm, n // tn, nk), in_specs=in_specs, out_specs=o_spec, out_shape=out_shape,
        scratch_shapes=[pltpu.VMEM((tm, tn), F32)] if nk > 1 else [], input_output_aliases=aliases,
        compiler_params=_params(dimension_semantics=("parallel", "parallel", "arbitrary")),
    )(*operands)


def _mm_swiglu_fwd(h, w_up, *, tm, tn, name):
    m, k = h.shape
    n = w_up.shape[1] // 2
    h_spec = pl.BlockSpec((tm, k), lambda i, j: (i, 0))
    wg_spec = pl.BlockSpec((k, tn), lambda i, j: (0, j))
    wu_spec = pl.BlockSpec((k, tn), lambda i, j: (0, j + n // tn))
    o_spec = pl.BlockSpec((tm, tn), lambda i, j: (i, j))

    def body(h_ref, wg_ref, wu_ref, a_ref, g_ref, u_ref):
        hb = h_ref[...]
        g = _dot(hb, wg_ref[...], _NN)
        u = _dot(hb, wu_ref[...], _NN)
        a_ref[...] = (g * jax.nn.sigmoid(g) * u).astype(BF16)
        g_ref[...] = g.astype(BF16)
        u_ref[...] = u.astype(BF16)

    out = jax.ShapeDtypeStruct((m, n), BF16)
    return pl.pallas_call(
        body, name=name, grid=(m // tm, n // tn), in_specs=[h_spec, wg_spec, wu_spec],
        out_specs=[o_spec] * 3, out_shape=[out] * 3,
        compiler_params=_params(dimension_semantics=("parallel", "parallel")),
    )(h, w_up, w_up)


def _mm_swiglu_bwd(dy, w_down, gate, up, *, alpha, tm, tn, name):
    m, k = dy.shape
    n = w_down.shape[0]
    dy_spec = pl.BlockSpec((tm, k), lambda i, j: (i, 0))
    w_spec = pl.BlockSpec((tn, k), lambda i, j: (j, 0))
    o_spec = pl.BlockSpec((tm, tn), lambda i, j: (i, j))

    def body(dy_ref, w_ref, g_ref, u_ref, dg_ref, du_ref):
        da = _dot(dy_ref[...], w_ref[...], _NT) * alpha
        g = g_ref[...].astype(F32)
        u = u_ref[...].astype(F32)
        sg = jax.nn.sigmoid(g)
        dg_ref[...] = (da * u * (sg * (1.0 + g * (1.0 - sg)))).astype(BF16)
        du_ref[...] = (da * (g * sg)).astype(BF16)

    out = jax.ShapeDtypeStruct((m, n), BF16)
    return pl.pallas_call(
        body, name=name, grid=(m // tm, n // tn), in_specs=[dy_spec, w_spec, o_spec, o_spec],
        out_specs=[o_spec] * 2, out_shape=[out] * 2,
        compiler_params=_params(dimension_semantics=("parallel", "parallel")),
    )(dy, w_down, gate, up)


def _rms_fwd(x, g, *, tt, name):
    t, d = x.shape

    def body(x_ref, g_ref, h_ref):
        xv = x_ref[...]
        rstd = lax.rsqrt(jnp.mean(xv * xv, axis=1, keepdims=True) + RMS_EPS)
        h_ref[...] = (xv * rstd * g_ref[...]).astype(BF16)

    return pl.pallas_call(
        body, name=name, grid=(t // tt,),
        in_specs=[pl.BlockSpec((tt, d), lambda i: (i, 0)), pl.BlockSpec((1, d), lambda i: (0, 0))],
        out_specs=pl.BlockSpec((tt, d), lambda i: (i, 0)), out_shape=jax.ShapeDtypeStruct((t, d), BF16),
        compiler_params=_params(dimension_semantics=("parallel",)),
    )(x, g)


def _rms_bwd(dh, x, g, dres, *, tt, name):
    t, d = x.shape

    def body(dh_ref, x_ref, g_ref, r_ref, dx_ref, dxb_ref, dg_ref):
        xv = x_ref[...]
        rstd = lax.rsqrt(jnp.mean(xv * xv, axis=1, keepdims=True) + RMS_EPS)
        xhat = xv * rstd
        dhv = dh_ref[...]
        dxhat = dhv * g_ref[...]
        dx = r_ref[...] + rstd * (dxhat - xhat * jnp.mean(dxhat * xhat, axis=1, keepdims=True))
        dx_ref[...] = dx
        dxb_ref[...] = dx.astype(BF16)

        @pl.when(pl.program_id(0) == 0)
        def _():
            dg_ref[...] = jnp.zeros_like(dg_ref)

        dg_ref[...] += jnp.sum(dhv * xhat, axis=0, keepdims=True)

    row = pl.BlockSpec((tt, d), lambda i: (i, 0))
    vec = pl.BlockSpec((1, d), lambda i: (0, 0))
    return pl.pallas_call(
        body, name=name, grid=(t // tt,), in_specs=[row, row, vec, row], out_specs=[row, row, vec],
        out_shape=[jax.ShapeDtypeStruct((t, d), F32), jax.ShapeDtypeStruct((t, d), BF16), jax.ShapeDtypeStruct((1, d), F32)],
        compiler_params=_params(dimension_semantics=("arbitrary",)),
    )(dh, x, g, dres)


def _final_loss(x, g, target, *, tt, name):
    t, d = x.shape

    def body(x_ref, g_ref, t_ref, dx_ref, dxb_ref, dg_ref, loss_ref):
        xv = x_ref[...]
        gv = g_ref[...]
        rstd = lax.rsqrt(jnp.mean(xv * xv, axis=1, keepdims=True) + RMS_EPS)
        xhat = xv * rstd
        err = xhat * gv - t_ref[...]
        dy = err * (1.0 / d)
        dxhat = dy * gv
        dx = rstd * (dxhat - xhat * jnp.mean(dxhat * xhat, axis=1, keepdims=True))
        dx_ref[...] = dx
        dxb_ref[...] = dx.astype(BF16)

        @pl.when(pl.program_id(0) == 0)
        def _():
            dg_ref[...] = jnp.zeros_like(dg_ref)
            loss_ref[...] = jnp.zeros_like(loss_ref)

        dg_ref[...] += jnp.sum(dy * xhat, axis=0, keepdims=True)
        part = 0.5 * jnp.sum(jnp.mean(err * err, axis=1, keepdims=True), axis=0, keepdims=True)
        loss_ref[...] += jnp.broadcast_to(part, loss_ref.shape)

    row = pl.BlockSpec((tt, d), lambda i: (i, 0))
    vec = pl.BlockSpec((1, d), lambda i: (0, 0))
    one = pl.BlockSpec((1, LANES), lambda i: (0, 0))
    return pl.pallas_call(
        body, name=name, grid=(t // tt,), in_specs=[row, vec, row], out_specs=[row, row, vec, one],
        out_shape=[jax.ShapeDtypeStruct((t, d), F32), jax.ShapeDtypeStruct((t, d), BF16), jax.ShapeDtypeStruct((1, d), F32),
                   jax.ShapeDtypeStruct((1, LANES), F32)],
        compiler_params=_params(dimension_semantics=("arbitrary",)),
    )(x, g, target)


def _swap_halves(x):
    lane = lax.broadcasted_iota(jnp.int32, x.shape, 1)
    return jnp.where((lane // 32) % 2 == 0, pltpu.roll(x, 96, 1), pltpu.roll(x, 32, 1))


def _rope_tables(s):
    half = HEAD_DIM // 2
    inv_freq = ROPE_THETA ** (-jnp.arange(half, dtype=F32) / half)
    ang = jnp.arange(s).astype(F32)[:, None] * inv_freq[None, :]
    cos, sin = jnp.cos(ang), jnp.sin(ang)
    return jnp.tile(cos, (1, 4)), jnp.concatenate([-sin, sin, -sin, sin], axis=1)


def _dilation_of_tile(p):
    dilated = p < 3 * DIL_HEADS // 2
    g = (p % (DIL_HEADS // 2)) // (DIL_GROUP_HEADS // 2)
    return [(dilated & (g == gi)) | (jnp.logical_not(dilated) if gi == 0 else False) for gi in range(len(DILATIONS))]


def _residue_major(ref, d):
    s = ref.shape[0]
    if d == 1:
        return ref[...]
    return jnp.concatenate([ref[pl.ds(r, s // d, stride=d), :] for r in range(d)], axis=0)


def _split_heads(proj, cos4, sin4, *, n_pairs, rot_pairs, scale_ranges, name):
    b, s, _ = proj.shape

    def body(x_ref, c_ref, s_ref, o_ref):
        p = pl.program_id(1)
        is_q = functools.reduce(jnp.logical_or, [(p >= lo) & (p < hi) for lo, hi in scale_ranges])
        scale = jnp.where(is_q, QK_SCALE, 1.0)

        def put(y):
            o_ref[0] = y[:, :HEAD_DIM].astype(BF16)
            o_ref[1] = y[:, HEAD_DIM:].astype(BF16)

        for d, in_group in zip(DILATIONS, _dilation_of_tile(p)):
            @pl.when(in_group & (p < rot_pairs))
            def _(d=d):
                x = _residue_major(x_ref, d)
                put((x * _residue_major(c_ref, d) + _swap_halves(x) * _residue_major(s_ref, d)) * scale)

            @pl.when(in_group & (p >= rot_pairs))
            def _(d=d):
                put(_residue_major(x_ref, d) * scale)

    tab = pl.BlockSpec((s, LANES), lambda bi, p: (0, 0))
    return pl.pallas_call(
        body, name=name, grid=(b, n_pairs),
        in_specs=[pl.BlockSpec((None, s, LANES), lambda bi, p: (bi, 0, p)), tab, tab],
        out_specs=pl.BlockSpec((None, 2, s, HEAD_DIM), lambda bi, p: (bi, p, 0, 0)),
        out_shape=jax.ShapeDtypeStruct((b, 2 * n_pairs, s, HEAD_DIM), BF16),
        compiler_params=_params(dimension_semantics=("parallel", "parallel")),
    )(proj, cos4, sin4)


def _merge_heads(dheads, cos4, sin4, *, heads_per_row, rot_pairs, scale_pairs, dilated, out_cols, tile_off, into, name):
    b, hpr, r, s, _ = dheads.shape
    n_pairs = hpr * r // 2
    ppr = hpr // 2

    def body(d_ref, c_ref, s_ref, *rest):
        o_ref, t_ref = rest[-2:]
        p = pl.program_id(1)
        scale = jnp.where(p < scale_pairs, QK_SCALE, 1.0)

        def tokens(d):
            dy = jnp.concatenate([d_ref[0], d_ref[1]], axis=1)
            if d == 1:
                return dy
            for res in range(d):
                t_ref[pl.ds(res, s // d, stride=d), :] = dy[res * (s // d):(res + 1) * (s // d), :]
            return t_ref[...]

        groups = _dilation_of_tile(p) if dilated else [p >= 0]
        for d, in_group in zip(DILATIONS, groups):
            @pl.when(in_group & (p < rot_pairs))
            def _(d=d):
                dy = tokens(d)
                o_ref[...] = ((dy * c_ref[...] - _swap_halves(dy) * s_ref[...]) * scale).astype(BF16)

            @pl.when(in_group & (p >= rot_pairs))
            def _(d=d):
                o_ref[...] = (tokens(d) * scale).astype(BF16)

    tab = pl.BlockSpec((s, LANES), lambda bi, p: (0, 0))
    operands = [dheads, cos4, sin4] + ([] if into is None else [into])
    return pl.pallas_call(
        body, name=name, grid=(b, n_pairs),
        in_specs=[pl.BlockSpec((None, 2, None, s, HEAD_DIM), lambda bi, p: (bi, p % ppr, p // ppr, 0, 0)), tab, tab]
        + ([] if into is None else [HBM]),
        out_specs=pl.BlockSpec((None, s, LANES), lambda bi, p: (bi, 0, p + tile_off)),
        out_shape=jax.ShapeDtypeStruct((b, s, out_cols), BF16),
        input_output_aliases={} if into is None else {3: 0},
        scratch_shapes=[pltpu.VMEM((s, LANES), F32)],
        compiler_params=_params(dimension_semantics=("parallel", "parallel")),
    )(*operands)


DIL_TQ = 256


def _dil_block(g, s):
    run = s // DILATIONS[g]
    return DIL_TQ if run <= DIL_TQ else min(run, DIL_TQ + 2 * LANES)


def _dil_keys(g, q0, s):
    run = max(s // DILATIONS[g], DIL_TQ)
    lo = (q0 // run) * run
    return pl.multiple_of(jnp.clip(q0 - LANES, lo, lo + run - _dil_block(g, s)), LANES)


def _dil_band(g, q0, start, shape, s):
    row = q0 + lax.broadcasted_iota(jnp.int32, shape, 0)
    col = start + lax.broadcasted_iota(jnp.int32, shape, 1)
    ok = jnp.abs(row - col) <= DIL_HALF
    run = s // DILATIONS[g]
    if run < DIL_TQ:
        shift = run.bit_length() - 1
        ok = ok & ((row >> shift) == (col >> shift))
    return ok


def _dil_tokens(g, q0, s):
    d = DILATIONS[g]
    if d == 1:
        return [(0, DIL_TQ, pl.ds(q0, DIL_TQ))]
    run = s // d
    n = min(run, DIL_TQ)
    return [(lo, n, pl.ds(((q0 + lo) % run) * d + (q0 + lo) // run, n, stride=d)) for lo in range(0, DIL_TQ, n)]


def _dil_gather(ref, pieces):
    return jnp.concatenate([ref[rows, :] for _, _, rows in pieces], axis=0) if len(pieces) > 1 else ref[pieces[0][2], :]


def _dil_head_spec(part, g, s):
    return pl.BlockSpec((None, None, s, HEAD_DIM), lambda b, j: (b, part * DIL_HEADS + g * DIL_GROUP_HEADS + j, 0, 0))


def _dil_attn_fwd(heads, *, name):
    b, _, s, _ = heads.shape
    n_g = len(DILATIONS)

    def body(*refs):
        qkv = refs[:3 * n_g]
        o_ref, oh_ref, l_ref, og_ref, lg_ref = refs[3 * n_g:]
        for g in range(n_g):
            q_ref, k_ref, v_ref = qkv[3 * g:3 * g + 3]
            width = _dil_block(g, s)

            def step(i, carry, g=g, q_ref=q_ref, k_ref=k_ref, v_ref=v_ref, width=width):
                q0 = pl.multiple_of(i * DIL_TQ, DIL_TQ)
                start = _dil_keys(g, q0, s)
                sc = _dot(q_ref[pl.ds(q0, DIL_TQ), :], k_ref[pl.ds(start, width), :], _NT)
                sc = jnp.where(_dil_band(g, q0, start, sc.shape, s), sc, NEG_INF)
                m = jnp.max(sc, axis=1, keepdims=True)
                p = jnp.exp(sc - m)
                den = jnp.sum(p, axis=1, keepdims=True)
                o = _dot(p.astype(BF16), v_ref[pl.ds(start, width), :], _NN) / den
                lse = m + jnp.log(den)
                for lo, n, rows in _dil_tokens(g, q0, s):
                    og_ref[g, rows, :] = o[lo:lo + n]
                    lg_ref[g, rows, :] = lse[lo:lo + n]
                return carry

            lax.fori_loop(0, s // DIL_TQ, step, 0, unroll=BLOCKS_IN_FLIGHT)
        lses = [lg_ref[g] for g in range(n_g)]
        m = functools.reduce(jnp.maximum, lses)
        ws = [jnp.exp(l - m) for l in lses]
        den = functools.reduce(jnp.add, ws)
        mixed = (functools.reduce(jnp.add, [w * og_ref[g] for g, w in enumerate(ws)]) / den).astype(o_ref.dtype)
        l_ref[...] = m + jnp.log(den)
        oh_ref[...] = mixed

        @pl.when(pl.program_id(1) % 2 == 0)
        def _():
            o_ref[:, :HEAD_DIM] = mixed

        @pl.when(pl.program_id(1) % 2 == 1)
        def _():
            o_ref[:, HEAD_DIM:] = mixed

    out = pl.BlockSpec((None, s, 2 * HEAD_DIM), lambda bi, j: (bi, 0, j // 2))
    lse = pl.BlockSpec((None, None, s, 1), lambda bi, j: (bi, j, 0, 0))
    return pl.pallas_call(
        body, name=name, grid=(b, DIL_GROUP_HEADS),
        in_specs=[_dil_head_spec(part, g, s) for g in range(n_g) for part in range(3)],
        out_specs=[out, pl.BlockSpec((None, None, s, HEAD_DIM), lambda bi, j: (bi, j, 0, 0)), lse],
        out_shape=[jax.ShapeDtypeStruct((b, s, DIL_GROUP_HEADS * HEAD_DIM), BF16),
                   jax.ShapeDtypeStruct((b, DIL_GROUP_HEADS, s, HEAD_DIM), BF16),
                   jax.ShapeDtypeStruct((b, DIL_GROUP_HEADS, s, 1), F32)],
        scratch_shapes=[pltpu.VMEM((n_g, s, HEAD_DIM), F32), pltpu.VMEM((n_g, s, 1), F32)],
        compiler_params=_params(dimension_semantics=("parallel", "arbitrary")),
    )(*([heads] * (3 * n_g)))


def _dil_attn_bwd(heads, out, lse, dout, *, name):
    b, _, s, _ = heads.shape
    n_g = len(DILATIONS)

    def body(*refs):
        qkv = refs[:3 * n_g]
        o_ref, l_ref, do_ref, d_ref, delta_ref = refs[3 * n_g:]
        d_ref[...] = jnp.zeros_like(d_ref)
        delta_ref[...] = jnp.sum(do_ref[...] * o_ref[...].astype(F32), axis=1, keepdims=True)
        for g in range(n_g):
            q_ref, k_ref, v_ref = qkv[3 * g:3 * g + 3]
            width = _dil_block(g, s)

            def step(i, carry, g=g, q_ref=q_ref, k_ref=k_ref, v_ref=v_ref, width=width):
                q0 = pl.multiple_of(i * DIL_TQ, DIL_TQ)
                start = _dil_keys(g, q0, s)
                win = pl.ds(start, width)
                pieces = _dil_tokens(g, q0, s)
                do_b = _dil_gather(do_ref, pieces).astype(BF16)
                q, k, v = q_ref[pl.ds(q0, DIL_TQ), :], k_ref[win, :], v_ref[win, :]
                sc = _dot(q, k, _NT)
                p = jnp.where(_dil_band(g, q0, start, sc.shape, s), jnp.exp(sc - _dil_gather(l_ref, pieces)), 0.0)
                ds = (p * (_dot(do_b, v, _NT) - _dil_gather(delta_ref, pieces))).astype(BF16)
                d_ref[g, pl.ds(q0, DIL_TQ), :] = _dot(ds, k, _NN)
                d_ref[n_g + g, win, :] += _dot(ds, q, _TN)
                d_ref[2 * n_g + g, win, :] += _dot(p.astype(BF16), do_b, _TN)
                return carry

            lax.fori_loop(0, s // DIL_TQ, step, 0, unroll=BLOCKS_IN_FLIGHT)

    per_head = lambda bi, j: (bi, j, 0, 0)
    return pl.pallas_call(
        body, name=name, grid=(b, DIL_GROUP_HEADS),
        in_specs=[_dil_head_spec(part, g, s) for g in range(n_g) for part in range(3)]
        + [pl.BlockSpec((None, None, s, HEAD_DIM), per_head), pl.BlockSpec((None, None, s, 1), per_head),
           pl.BlockSpec((None, None, s, HEAD_DIM), per_head)],
        out_specs=pl.BlockSpec((None, None, 3 * n_g, s, HEAD_DIM), lambda bi, j: (bi, j, 0, 0, 0)),
        out_shape=jax.ShapeDtypeStruct((b, DIL_GROUP_HEADS, 3 * n_g, s, HEAD_DIM), F32),
        scratch_shapes=[pltpu.VMEM((s, 1), F32)],
        compiler_params=_params(dimension_semantics=("parallel", "parallel")),
    )(*([heads] * (3 * n_g)), out, lse, dout)


NA_BIAS_ROWS = 2 * NA_ROWS - 1
NA_BIAS_COLS = 2 * NA_COLS - 1
NA_BLOCK = 4
NA_SPAN = NA_ROWS + NA_BLOCK - 1
NA_Q = NA_BLOCK * GRID_W
NA_KEYS = NA_SPAN * GRID_W
NA_FORMS = 3


def _na_onehot():
    c = np.arange(GRID_W)[:, None]
    k = np.arange(GRID_W)[None, :]
    lo = np.clip(c - NA_COLS // 2, 0, GRID_W - NA_COLS)
    valid = (k >= lo) & (k < lo + NA_COLS)
    onehot = np.zeros((GRID_W, GRID_W, LANES), np.float32)
    cc, kk = np.nonzero(valid)
    onehot[cc, kk, kk - cc + NA_COLS - 1] = 1.0
    return onehot.reshape(GRID_W * GRID_W, LANES), valid.reshape(1, GRID_W * GRID_W)


def _na_block_rows(n_rows):
    table = np.full((NA_FORMS, NA_BLOCK, NA_SPAN), NA_BIAS_ROWS, np.int64)
    n_blocks = n_rows // NA_BLOCK
    for form, ib in enumerate((0, 1, n_blocks - 1)):
        base = min(max(NA_BLOCK * ib - NA_ROWS // 2, 0), n_rows - NA_SPAN)
        for rl in range(NA_BLOCK):
            r = NA_BLOCK * ib + rl
            row_lo = min(max(r - NA_ROWS // 2, 0), n_rows - NA_ROWS)
            for kl in range(NA_SPAN):
                if row_lo <= base + kl < row_lo + NA_ROWS:
                    table[form, rl, kl] = base + kl - r + NA_ROWS - 1
    return table


def _na_block(ib, n_rows):
    n_blocks = n_rows // NA_BLOCK
    base = jnp.clip(NA_BLOCK * ib - NA_ROWS // 2, 0, n_rows - NA_SPAN)
    return base, jnp.where(ib == 0, 0, jnp.where(ib == n_blocks - 1, 2, 1))


def _na_expand_bias(rel_bias, *, name):
    l, h, nr, nc = rel_bias.shape
    onehot, valid = _na_onehot()
    rb = jnp.pad(rel_bias, ((0, 0), (0, 0), (0, 1), (0, LANES - nc))).reshape(l * h * (nr + 1), LANES)
    live = jnp.asarray(np.tile(np.arange(nr + 1) < nr, l * h).astype(np.float32)[:, None])

    def body(rb_ref, oh_ref, valid_ref, live_ref, e_ref):
        e = lax.dot_general(rb_ref[...], oh_ref[...], _NT, precision=lax.Precision.HIGHEST, preferred_element_type=F32)
        e_ref[...] = jnp.where((valid_ref[...] > 0) & (live_ref[...] > 0), e, NEG_INF)

    e = pl.pallas_call(
        body, name=name, out_shape=jax.ShapeDtypeStruct((l * h * (nr + 1), GRID_W * GRID_W), F32), compiler_params=_params(),
    )(rb, jnp.asarray(onehot), jnp.asarray(valid.astype(np.float32)), live)
    return e.reshape(l, h, nr + 1, GRID_W, GRID_W)


def _na_collapse_bias(de, *, name):
    b, h = de.shape[:2]
    onehot, _ = _na_onehot()
    rows = h * NA_BIAS_ROWS

    def diag(e_ref, oh_ref, o_ref):
        e = e_ref[0]
        for bi in range(1, b):
            e = e + e_ref[bi]
        o_ref[...] = lax.dot_general(e, oh_ref[...], _NN, precision=lax.Precision.HIGHEST, preferred_element_type=F32)

    drb = pl.pallas_call(
        diag, name=name, out_shape=jax.ShapeDtypeStruct((rows, LANES), F32), compiler_params=_params(),
    )(de.reshape(b, rows, GRID_W * GRID_W), jnp.asarray(onehot))
    return drb[:, :NA_BIAS_COLS].reshape(h, NA_BIAS_ROWS, NA_BIAS_COLS)


def _na_tiles(n_rows):
    table = _na_block_rows(n_rows)
    return [(f, rl, kl, int(table[f, rl, kl])) for f in range(NA_FORMS) for rl in range(NA_BLOCK) for kl in range(NA_SPAN)]


def _na_tile(ref, form, rl, kl):
    return ref.at[form, rl * GRID_W:(rl + 1) * GRID_W, kl * GRID_W:(kl + 1) * GRID_W]


def _na_head_spec(part, first, s):
    return pl.BlockSpec((None, None, s, HEAD_DIM), lambda b, h: (b, first + part * NA_HEADS + h, 0, 0))


def _na_attn_fwd(heads, bias, *, first, name):
    b, _, s, _ = heads.shape
    n_rows = s // GRID_W
    tiles = _na_tiles(n_rows)

    def body(q_ref, k_ref, v_ref, e_ref, ot_ref, o_ref, l_ref, b_ref):
        for form, rl, kl, i in tiles:
            _na_tile(b_ref, form, rl, kl)[...] = e_ref[i]

        def step(ib, carry):
            base, form = _na_block(ib, n_rows)
            rows = pl.ds(pl.multiple_of(ib * NA_Q, NA_Q), NA_Q)
            win = pl.ds(pl.multiple_of(base * GRID_W, GRID_W), NA_KEYS)
            sc = _dot(q_ref[rows, :], k_ref[win, :], _NT) + b_ref[form]
            m = jnp.max(sc, axis=1, keepdims=True)
            p = jnp.exp(sc - m)
            den = jnp.sum(p, axis=1, keepdims=True)
            o_ref[rows, :] = (_dot(p.astype(BF16), v_ref[win, :], _NN) / den).astype(o_ref.dtype)
            l_ref[rows, :] = m + jnp.log(den)
            return carry

        lax.fori_loop(0, n_rows // NA_BLOCK, step, 0, unroll=BLOCKS_IN_FLIGHT)

        @pl.when(pl.program_id(1) % 2 == 0)
        def _():
            ot_ref[:, :HEAD_DIM] = o_ref[...]

        @pl.when(pl.program_id(1) % 2 == 1)
        def _():
            ot_ref[:, HEAD_DIM:] = o_ref[...]

    per_head = lambda bi, h: (bi, h, 0, 0)
    return pl.pallas_call(
        body, name=name, grid=(b, NA_HEADS),
        in_specs=[_na_head_spec(part, first, s) for part in range(3)]
        + [pl.BlockSpec((None, NA_BIAS_ROWS + 1, GRID_W, GRID_W), lambda bi, h: (h, 0, 0, 0))],
        out_specs=[pl.BlockSpec((None, s, 2 * HEAD_DIM), lambda bi, h: (bi, 0, h // 2)),
                   pl.BlockSpec((None, None, s, HEAD_DIM), per_head), pl.BlockSpec((None, None, s, 1), per_head)],
        out_shape=[jax.ShapeDtypeStruct((b, s, NA_HEADS * HEAD_DIM), BF16), jax.ShapeDtypeStruct((b, NA_HEADS, s, HEAD_DIM), BF16),
                   jax.ShapeDtypeStruct((b, NA_HEADS, s, 1), F32)],
        scratch_shapes=[pltpu.VMEM((NA_FORMS, NA_Q, NA_KEYS), F32)],
        compiler_params=_params(dimension_semantics=("parallel", "arbitrary")),
    )(heads, heads, heads, bias)


def _na_attn_bwd(heads, bias, out, lse, dout, *, first, name):
    b, _, s, _ = heads.shape
    n_rows = s // GRID_W
    tiles = _na_tiles(n_rows)

    def body(q_ref, k_ref, v_ref, e_ref, o_ref, l_ref, do_ref, d_ref, de_ref, b_ref, db_ref):
        for form, rl, kl, i in tiles:
            _na_tile(b_ref, form, rl, kl)[...] = e_ref[i]
        d_ref[...] = jnp.zeros_like(d_ref)
        db_ref[...] = jnp.zeros_like(db_ref)

        def step(ib, carry):
            base, form = _na_block(ib, n_rows)
            rows = pl.ds(pl.multiple_of(ib * NA_Q, NA_Q), NA_Q)
            win = pl.ds(pl.multiple_of(base * GRID_W, GRID_W), NA_KEYS)
            q, k, v = q_ref[rows, :], k_ref[win, :], v_ref[win, :]
            do = do_ref[rows, :]
            delta = jnp.sum(do * o_ref[rows, :].astype(F32), axis=1, keepdims=True)
            do_b = do.astype(BF16)
            p = jnp.exp(_dot(q, k, _NT) + b_ref[form] - l_ref[rows, :])
            ds = p * (_dot(do_b, v, _NT) - delta)
            db_ref[form] += ds
            ds_b = ds.astype(BF16)
            d_ref[0, rows, :] = _dot(ds_b, k, _NN)
            d_ref[1, win, :] += _dot(ds_b, q, _TN)
            d_ref[2, win, :] += _dot(p.astype(BF16), do_b, _TN)
            return carry

        lax.fori_loop(0, n_rows // NA_BLOCK, step, 0, unroll=BLOCKS_IN_FLIGHT)
        acc = [None] * NA_BIAS_ROWS
        for form, rl, kl, i in tiles:
            if i < NA_BIAS_ROWS:
                t = _na_tile(db_ref, form, rl, kl)[...]
                acc[i] = t if acc[i] is None else acc[i] + t
        for i in range(NA_BIAS_ROWS):
            de_ref[i] = acc[i]

    per_head = lambda bi, h: (bi, h, 0, 0)
    return pl.pallas_call(
        body, name=name, grid=(b, NA_HEADS),
        in_specs=[_na_head_spec(part, first, s) for part in range(3)]
        + [pl.BlockSpec((None, NA_BIAS_ROWS + 1, GRID_W, GRID_W), lambda bi, h: (h, 0, 0, 0)),
           pl.BlockSpec((None, None, s, HEAD_DIM), per_head), pl.BlockSpec((None, None, s, 1), per_head),
           pl.BlockSpec((None, None, s, HEAD_DIM), per_head)],
        out_specs=[pl.BlockSpec((None, None, 3, s, HEAD_DIM), lambda bi, h: (bi, h, 0, 0, 0)),
                   pl.BlockSpec((None, None, NA_BIAS_ROWS, GRID_W, GRID_W), lambda bi, h: (bi, h, 0, 0, 0))],
        out_shape=[jax.ShapeDtypeStruct((b, NA_HEADS, 3, s, HEAD_DIM), F32),
                   jax.ShapeDtypeStruct((b, NA_HEADS, NA_BIAS_ROWS, GRID_W, GRID_W), F32)],
        scratch_shapes=[pltpu.VMEM((NA_FORMS, NA_Q, NA_KEYS), F32), pltpu.VMEM((NA_FORMS, NA_Q, NA_KEYS), F32)],
        compiler_params=_params(dimension_semantics=("parallel", "parallel")),
    )(heads, heads, heads, bias, out, lse, dout)


GATE_TILE = 256


def _gate_fwd(proj, z, *, gate_col, tt, name):
    _, t, d = z.shape
    nj = d // GATE_TILE
    c0 = gate_col // GATE_TILE

    def body(ga_ref, gb_ref, za_ref, zb_ref, o_ref):
        o_ref[...] = (jax.nn.sigmoid(ga_ref[...]) * za_ref[...] + jax.nn.sigmoid(gb_ref[...]) * zb_ref[...]).astype(BF16)

    return pl.pallas_call(
        body, name=name, grid=(t // tt, nj),
        in_specs=[pl.BlockSpec((tt, GATE_TILE), lambda i, j: (i, c0 + j)),
                  pl.BlockSpec((tt, GATE_TILE), lambda i, j: (i, c0 + nj + j)),
                  pl.BlockSpec((None, tt, GATE_TILE), lambda i, j: (0, i, j)),
                  pl.BlockSpec((None, tt, GATE_TILE), lambda i, j: (1, i, j))],
        out_specs=pl.BlockSpec((tt, GATE_TILE), lambda i, j: (i, j)), out_shape=jax.ShapeDtypeStruct((t, d), BF16),
        compiler_params=_params(dimension_semantics=("parallel", "parallel")),
    )(proj, proj, z, z)


def _gate_bwd(dm, proj, z, *, gate_col, tt, name):
    _, t, d = z.shape
    nj = d // GATE_TILE
    c0 = gate_col // GATE_TILE

    def body(dm_ref, g_ref, z_ref, dz_ref, dg_ref):
        dmv = dm_ref[...]
        sg = jax.nn.sigmoid(g_ref[...])
        dz_ref[...] = (dmv * sg).astype(BF16)
        dg_ref[...] = (dmv * z_ref[...] * sg * (1.0 - sg)).astype(BF16)

    return pl.pallas_call(
        body, name=name, grid=(t // tt, 2 * nj),
        in_specs=[pl.BlockSpec((tt, GATE_TILE), lambda i, j: (i, j % nj)),
                  pl.BlockSpec((tt, GATE_TILE), lambda i, j: (i, c0 + j)),
                  pl.BlockSpec((None, tt, GATE_TILE), lambda i, j: (j // nj, i, j % nj))],
        out_specs=[pl.BlockSpec((None, tt, GATE_TILE), lambda i, j: (j // nj, i, j % nj)),
                   pl.BlockSpec((tt, GATE_TILE), lambda i, j: (i, c0 + j))],
        out_shape=[jax.ShapeDtypeStruct((2, t, d), BF16), jax.ShapeDtypeStruct(proj.shape, BF16)],
        compiler_params=_params(dimension_semantics=("parallel", "parallel")),
    )(dm, proj, z)


def _adamw(w, g, m, v, *, name):
    shape = w.shape
    if w.ndim == 3:
        w2, g2, m2, v2 = w, g, m, v
    else:
        w2, g2, m2, v2 = (t.reshape(1, -1, shape[-1]) for t in (w, g, m, v))
    lead, rows, cols = w2.shape
    tr = rows
    for cand in (512, 256, 128, 64, 32, 16, 8):
        if rows % cand == 0:
            tr = cand
            break

    def body(w_ref, g_ref, m_ref, v_ref, d_ref, nm_ref, nv_ref):
        gv = g_ref[...]
        nm = ADAM_B1 * m_ref[...] + (1.0 - ADAM_B1) * gv
        nv = ADAM_B2 * v_ref[...] + (1.0 - ADAM_B2) * (gv * gv)
        m_hat = nm / (1.0 - ADAM_B1 ** ADAM_STEP)
        v_hat = nv / (1.0 - ADAM_B2 ** ADAM_STEP)
        d_ref[...] = -ADAM_LR * (m_hat / (jnp.sqrt(v_hat) + ADAM_EPS) + ADAM_WD * w_ref[...])
        nm_ref[...] = nm
        nv_ref[...] = nv

    blk = pl.BlockSpec((None, tr, cols), lambda l, i: (l, i, 0))
    out = jax.ShapeDtypeStruct((lead, rows, cols), F32)
    res = pl.pallas_call(
        body, name=name, grid=(lead, rows // tr), in_specs=[blk] * 4, out_specs=[blk] * 3, out_shape=[out] * 3,
        compiler_params=_params(dimension_semantics=("parallel", "parallel")),
    )(w2, g2, m2, v2)
    return tuple(t.reshape(shape) for t in res)


def _my_place():
    return lax.axis_index("x"), lax.axis_index("y"), lax.axis_index("c")


def _other_chips(x, y):
    return [(1 - x, y), (x, 1 - y), (1 - x, 1 - y)]


def _chip_no(chip):
    return 2 * chip[0] + chip[1]


def _window(ref, kind, size, chip, lead):
    if kind == "col":
        return ref.at[(*lead, slice(None), pl.ds(pl.multiple_of(chip * size, LANES), size))]
    if kind == "row":
        return ref.at[(*lead, pl.ds(pl.multiple_of(chip * size, BF16_ROWS), size), slice(None))]
    shard = size + HEAD_DIM
    if kind == "win_main":
        return ref.at[(*lead, slice(None), pl.ds(pl.multiple_of(chip * shard + HEAD_DIM * (chip % 2), LANES), size))]
    assert kind == "win_strad"
    return ref.at[(*lead, slice(None), pl.ds(pl.multiple_of(size + 2 * shard * (chip // 2), LANES), LANES))]


def _full_shape(shard, kind):
    _, k, n = shard.shape
    return {"col": (k, N_CHIPS * n), "row": (N_CHIPS * k, n), "win_main": (k, N_CHIPS * (n + HEAD_DIM)),
            "slot": (N_CHIPS, k, n)}[kind]


def _place_own(shard, kind, layer, *, name):
    _, k, n = shard.shape
    tr = _div_tile(k, 512, BF16_ROWS)
    tc = LANES if kind == "win_main" else n
    mine = 2 * lax.axis_index("x") + lax.axis_index("y")
    row0 = mine * (k // tr) if kind == "row" else 0
    col0 = {"col": mine, "row": 0, "slot": 0, "win_main": (mine * (n + HEAD_DIM) + HEAD_DIM * (mine % 2)) // LANES}[kind]
    scalars = jnp.stack([mine, row0, col0]).astype(jnp.int32)

    def body(s_ref, i_ref, o_ref):
        o_ref[...] = i_ref[...]

    if kind == "slot":
        o_spec = pl.BlockSpec((None, tr, tc), lambda i, j, s: (s[0], i, j))
    else:
        o_spec = pl.BlockSpec((tr, tc), lambda i, j, s: (s[1] + i, s[2] + j))
    return pl.pallas_call(
        body, name=name,
        grid_spec=pltpu.PrefetchScalarGridSpec(
            num_scalar_prefetch=1, grid=(k // tr, n // tc),
            in_specs=[pl.BlockSpec((None, tr, tc), lambda i, j, s: (layer, i, j))], out_specs=o_spec),
        out_shape=jax.ShapeDtypeStruct(_full_shape(shard, kind), shard.dtype),
        compiler_params=_params(dimension_semantics=("parallel", "parallel")),
    )(scalars, shard)


class _GatherPlan:
    def __init__(self, src, dst, shapes, kinds, layer, send_sems, recv_sems):
        self.src, self.dst, self.shapes, self.kinds, self.layer = src, dst, shapes, kinds, layer
        self.send_sems, self.recv_sems = send_sems, recv_sems
        self.x, self.y, self.c = _my_place()
        self.mine = 2 * self.x + self.y
        self.chips = _other_chips(self.x, self.y)
        self.n = len(src)

    def half(self, i, chip, half):
        _, k, n = self.shapes[i]
        kind, dst, hk = self.kinds[i], self.dst[i], k // 2
        if kind == "slot":
            return dst.at[chip, pl.ds(pl.multiple_of(half * hk, BF16_ROWS), hk), :]
        if kind == "row":
            return dst.at[pl.ds(pl.multiple_of(chip * k + half * hk, BF16_ROWS), hk), :]
        col0 = chip * n if kind == "col" else chip * (n + HEAD_DIM) + HEAD_DIM * (chip % 2)
        return dst.at[pl.ds(pl.multiple_of(half * hk, BF16_ROWS), hk), pl.ds(pl.multiple_of(col0, LANES), n)]

    def _copy(self, sem, window, to, source=None):
        return pltpu.make_async_remote_copy(src_ref=window if source is None else source, dst_ref=window,
                                            send_sem=self.send_sems.at[sem], recv_sem=self.recv_sems.at[sem],
                                            device_id=to, device_id_type=MESH)

    def sends(self):
        out = []
        for k, chip in enumerate(self.chips):
            for i in range(self.n):
                hk = self.shapes[i][1] // 2
                mine = self.src[i].at[self.layer, pl.ds(pl.multiple_of(self.c * hk, BF16_ROWS), hk), :]
                out.append(self._copy(3 * i + k, self.half(i, self.mine, self.c), (*chip, self.c), source=mine))
        return out

    def arrivals(self):
        return [self._copy(3 * i + k, self.half(i, _chip_no(chip), self.c), (*chip, self.c))
                for k, chip in enumerate(self.chips) for i in range(self.n)]

    def forwards(self, first_sem):
        sibling = (self.x, self.y, 1 - self.c)
        return [self._copy(first_sem + 3 * i + k, self.half(i, _chip_no(chip), self.c), sibling)
                for k, chip in enumerate(self.chips) for i in range(self.n)]

    def forwarded(self, first_sem):
        sibling = (self.x, self.y, 1 - self.c)
        return [self._copy(first_sem + 3 * i + k, self.half(i, _chip_no(chip), 1 - self.c), sibling)
                for k, chip in enumerate(self.chips) for i in range(self.n)]


IN_HBM = pl.BlockSpec(memory_space=pltpu.HBM)
IN_SEM = pl.BlockSpec(memory_space=pltpu.SEMAPHORE)
DATAFLOW = pltpu.SideEffectType.DATAFLOW_SIDE_EFFECTING


def _gather_layer_start(shards, kinds, fulls, layer, after, *, name):
    n_w = len(shards)
    shapes = [sh.shape for sh in shards]

    def body(*refs):
        plan = _GatherPlan(refs[:n_w], refs[n_w:2 * n_w], shapes, kinds, layer, refs[2 * n_w + 1], refs[2 * n_w + 2])
        for cp in plan.sends():
            cp.start()
        token = refs[-1]
        token[...] = jnp.zeros_like(token)

    operands = [pltpu.with_memory_space_constraint(a, pltpu.HBM) for a in (*shards, *fulls)]
    res = pl.pallas_call(
        body, name=name, in_specs=[IN_HBM] * (2 * n_w) + [pl.BlockSpec(memory_space=pl.ANY)],
        out_specs=(IN_SEM, IN_SEM, *([IN_HBM] * (2 * n_w)), pl.BlockSpec(memory_space=pltpu.VMEM)),
        out_shape=(pltpu.SemaphoreType.DMA((3 * n_w,)), pltpu.SemaphoreType.DMA((3 * n_w,)),
                   *[pltpu.HBM(a.shape, a.dtype) for a in operands], jax.ShapeDtypeStruct((8, LANES), F32)),
        input_output_aliases={i: 2 + i for i in range(2 * n_w)},
        compiler_params=pltpu.CompilerParams(has_side_effects=DATAFLOW),
    )(*operands, after)
    return res[0], res[1], res[2:2 + n_w], res[2 + n_w:2 + 2 * n_w], res[-1]


def _gather_layer_wait(send_sems, recv_sems, shards, fulls, kinds, layer, after, *, name):
    n_w = len(shards)
    shapes = [sh.shape for sh in shards]

    def body(*refs):
        plan = _GatherPlan(refs[:n_w], refs[n_w:2 * n_w], shapes, kinds, layer, refs[2 * n_w], refs[2 * n_w + 1])
        for cp in plan.sends():
            cp.wait_send()
        for cp in plan.arrivals():
            cp.wait_recv()

    res = pl.pallas_call(
        body, name=name, in_specs=[IN_HBM] * (2 * n_w) + [IN_SEM, IN_SEM, pl.BlockSpec(memory_space=pl.ANY)],
        out_specs=[IN_HBM] * (2 * n_w), out_shape=[pltpu.HBM(a.shape, a.dtype) for a in (*shards, *fulls)],
        input_output_aliases={i: i for i in range(2 * n_w)},
        compiler_params=pltpu.CompilerParams(has_side_effects=DATAFLOW),
    )(*shards, *fulls, send_sems, recv_sems, after)
    return res[n_w:]


def _gather_layer_forward(shapes, kinds, fulls, *, name):
    n_w = len(fulls)

    def body(*refs):
        plan = _GatherPlan([None] * n_w, refs[n_w:2 * n_w], shapes, kinds, 0, *refs[2 * n_w:])
        passed = plan.forwards(0)
        for cp in passed:
            cp.start()
        for cp in plan.forwarded(0):
            cp.wait_recv()
        for cp in passed:
            cp.wait_send()

    return pl.pallas_call(
        body, name=name, in_specs=[HBM] * n_w, out_specs=[HBM] * n_w,
        out_shape=[jax.ShapeDtypeStruct(f.shape, f.dtype) for f in fulls],
        input_output_aliases={i: i for i in range(n_w)},
        scratch_shapes=[pltpu.SemaphoreType.DMA((3 * n_w,)), pltpu.SemaphoreType.DMA((3 * n_w,))],
    )(*fulls)


def _on_core(layer):
    return (lax.axis_index("c") == layer).astype(jnp.int32).reshape(1)


N_DEVICES = 2 * N_CHIPS


class _ScatterPlan:
    def __init__(self, src, dst, kinds, sizes, layer, send_sems, recv_sems):
        self.src, self.dst, self.kinds, self.sizes, self.layer = src, dst, kinds, sizes, layer
        self.send_sems, self.recv_sems = send_sems, recv_sems
        self.x, self.y, self.c = _my_place()
        self.mine = 2 * self.x + self.y
        self.chips = _other_chips(self.x, self.y)
        self.n = len(src)

    def _copy(self, i, k, window_of, from_chip, from_core, to):
        return pltpu.make_async_remote_copy(src_ref=_window(self.src[i], self.kinds[i], self.sizes[i], window_of, ()),
                                            dst_ref=self.dst[i].at[2 * from_chip + from_core],
                                            send_sem=self.send_sems.at[4 * i + k],
                                            recv_sem=self.recv_sems.at[2 * (4 * i + k) + from_core],
                                            device_id=to, device_id_type=MESH)

    def to_chips(self):
        return [self._copy(i, k, _chip_no(chip), self.mine, self.c, (*chip, self.layer))
                for k, chip in enumerate(self.chips) for i in range(self.n)]

    def to_sibling(self):
        return [self._copy(i, 3, self.mine, self.mine, self.c, (self.x, self.y, self.layer)) for i in range(self.n)]

    def arrivals(self):
        out = [self._copy(i, k, self.mine, _chip_no(chip), core, (*chip, core))
               for k, chip in enumerate(self.chips) for core in (0, 1) for i in range(self.n)]
        return out + [self._copy(i, 3, self.mine, self.mine, 1 - self.layer, (self.x, self.y, 1 - self.layer))
                      for i in range(self.n)]


def _slab_shape(p, kind, size):
    return (N_DEVICES,) + {"col": (p.shape[0], size), "row": (size, p.shape[1]), "win_main": (p.shape[0], size),
                           "win_strad": (p.shape[0], LANES)}[kind]


def _grads_to_chips_start(pairs, kinds, sizes, layer, *, name):
    n_w = len(pairs)

    def body(*refs):
        plan = _ScatterPlan(refs[:n_w], refs[n_w:2 * n_w], kinds, sizes, layer, refs[2 * n_w], refs[2 * n_w + 1])
        for cp in plan.to_chips():
            cp.start()

        @pl.when(plan.c != layer)
        def _():
            for cp in plan.to_sibling():
                cp.start()

        token = refs[-1]
        token[...] = jnp.zeros_like(token)

    slabs = [lax.empty(_slab_shape(p, kind, size), p.dtype) for p, kind, size in zip(pairs, kinds, sizes)]
    operands = [pltpu.with_memory_space_constraint(a, pltpu.HBM) for a in (*pairs, *slabs)]
    res = pl.pallas_call(
        body, name=name, in_specs=[IN_HBM] * (2 * n_w),
        out_specs=(IN_SEM, IN_SEM, *([IN_HBM] * (2 * n_w)), pl.BlockSpec(memory_space=pltpu.VMEM)),
        out_shape=(pltpu.SemaphoreType.DMA((4 * n_w,)), pltpu.SemaphoreType.DMA((8 * n_w,)),
                   *[pltpu.HBM(a.shape, a.dtype) for a in operands], jax.ShapeDtypeStruct((8, LANES), F32)),
        input_output_aliases={i: 2 + i for i in range(2 * n_w)},
        compiler_params=pltpu.CompilerParams(has_side_effects=DATAFLOW),
    )(*operands)
    return res[0], res[1], res[2:2 + n_w], res[2 + n_w:2 + 2 * n_w], res[-1]


def _grads_to_chips_wait(send_sems, recv_sems, pairs, slabs, kinds, sizes, layer, after, *, name):
    n_w = len(pairs)

    def body(*refs):
        plan = _ScatterPlan(refs[:n_w], refs[n_w:2 * n_w], kinds, sizes, layer, refs[2 * n_w], refs[2 * n_w + 1])
        for cp in plan.to_chips():
            cp.wait_send()

        @pl.when(plan.c != layer)
        def _():
            for cp in plan.to_sibling():
                cp.wait_send()

        @pl.when(plan.c == layer)
        def _():
            for cp in plan.arrivals():
                cp.wait_recv()

    res = pl.pallas_call(
        body, name=name, in_specs=[IN_HBM] * (2 * n_w) + [IN_SEM, IN_SEM, pl.BlockSpec(memory_space=pl.ANY)],
        out_specs=[IN_HBM] * (2 * n_w), out_shape=[pltpu.HBM(a.shape, a.dtype) for a in (*pairs, *slabs)],
        input_output_aliases={i: i for i in range(2 * n_w)},
        compiler_params=pltpu.CompilerParams(has_side_effects=DATAFLOW),
    )(*pairs, *slabs, send_sems, recv_sems, after)
    return res[:n_w], res[n_w:]


def _sum_slabs(slabs, pair, kind, size, layer, into, *, name):
    n_s, k, n = slabs.shape
    tr = _div_tile(k, 512, BF16_ROWS)
    tc = n if kind in ("col", "row") else LANES
    x, y, _ = _my_place()
    mine = 2 * x + y
    shard = size + HEAD_DIM
    row0 = mine * (k // tr) if kind == "row" else 0
    col0 = {"col": mine, "row": 0, "win_main": (mine * shard + HEAD_DIM * (mine % 2)) // LANES,
            "win_strad": (size + 2 * shard * (mine // 2)) // LANES}[kind]
    on = _on_core(layer)[0]
    scalars = jnp.stack([2 * mine + layer, row0 * on, col0 * on, on]).astype(jnp.int32)

    def body(s_ref, slab_ref, own_ref, *rest):
        o_ref = rest[-1]
        me = s_ref[0]

        @pl.when(s_ref[3] == 1)
        def _():
            acc = jnp.zeros(o_ref.shape, F32)
            for i in range(n_s):
                acc = acc + jnp.where(me == i, own_ref[...], slab_ref[i]).astype(F32)
            o_ref[...] = acc

    operands = [scalars, slabs, pair] + ([] if into is None else [into])
    return pl.pallas_call(
        body, name=name,
        grid_spec=pltpu.PrefetchScalarGridSpec(
            num_scalar_prefetch=1, grid=(k // tr, n // tc),
            in_specs=[pl.BlockSpec((n_s, tr, tc), lambda i, j, s: (0, i * s[3], j * s[3])),
                      pl.BlockSpec((tr, tc), lambda i, j, s: (s[1] + i * s[3], s[2] + j * s[3]))]
            + ([] if into is None else [HBM]),
            out_specs=pl.BlockSpec((None, tr, tc), lambda i, j, s: (layer, i * s[3], j * s[3]))),
        out_shape=jax.ShapeDtypeStruct((2, k, n), F32),
        input_output_aliases={} if into is None else {3: 0},
        compiler_params=_params(dimension_semantics=("arbitrary", "arbitrary")),
    )(*operands)


def _exchange_layers(bufs, *, name):
    n_w = len(bufs)

    def body(*refs):
        dst = refs[n_w:2 * n_w]
        send_sems, recv_sems = refs[2 * n_w:]
        x, y, c = _my_place()

        def copy(i, layer):
            return pltpu.make_async_remote_copy(src_ref=dst[i].at[layer], dst_ref=dst[i].at[layer], send_sem=send_sems.at[i],
                                                recv_sem=recv_sems.at[i], device_id=(x, y, 1 - c), device_id_type=MESH)

        sends = [copy(i, c) for i in range(n_w)]
        for cp in sends:
            cp.start()
        for i in range(n_w):
            copy(i, 1 - c).wait_recv()
        for cp in sends:
            cp.wait_send()

    return pl.pallas_call(
        body, name=name, in_specs=[HBM] * n_w, out_specs=[HBM] * n_w,
        out_shape=[jax.ShapeDtypeStruct(b.shape, b.dtype) for b in bufs],
        input_output_aliases={i: i for i in range(n_w)},
        scratch_shapes=[pltpu.SemaphoreType.DMA((n_w,)), pltpu.SemaphoreType.DMA((n_w,))],
    )(*bufs)


def _all_sum_small(v, *, name):
    r = v.shape[0]
    relations = [(dx, dy, dc) for dx in (0, 1) for dy in (0, 1) for dc in (0, 1)][1:]

    def body(v_ref, o_ref, buf, send_sems, recv_sems):
        x, y, c = _my_place()
        me = 4 * x + 2 * y + c
        buf[me] = v_ref[...]
        peers = [(x + dx - 2 * x * dx, y + dy - 2 * y * dy, c + dc - 2 * c * dc) for dx, dy, dc in relations]

        def copy(k, slot):
            return pltpu.make_async_remote_copy(src_ref=v_ref, dst_ref=buf.at[slot], send_sem=send_sems.at[k],
                                                recv_sem=recv_sems.at[k], device_id=peers[k], device_id_type=MESH)

        sends = [copy(k, me) for k in range(len(relations))]
        for cp in sends:
            cp.start()
        for k, (px, py, pc) in enumerate(peers):
            copy(k, 4 * px + 2 * py + pc).wait_recv()
        for cp in sends:
            cp.wait_send()
        acc = buf[0]
        for i in range(1, 8):
            acc = acc + buf[i]
        o_ref[...] = acc

    vm = pl.BlockSpec(memory_space=pltpu.VMEM)
    return pl.pallas_call(
        body, name=name, in_specs=[vm], out_specs=vm, out_shape=jax.ShapeDtypeStruct((r, LANES), F32),
        scratch_shapes=[pltpu.VMEM((8, r, LANES), F32), pltpu.SemaphoreType.DMA((7,)), pltpu.SemaphoreType.DMA((7,))],
    )(v)


SHARDED = (("ffn1_w_up", "col"), ("ffn1_w_down", "row"), ("w_in", "win"), ("w_branch_a", "col"),
           ("w_branch_b", "col"), ("w_out", "row"), ("ffn2_w_up", "col"), ("ffn2_w_down", "row"))
REPLICATED = ("ffn1_norm", "mix_norm", "na_rel_bias", "ffn2_norm", "final_norm")


def _weight_pieces(w):
    even = lax.axis_index("y") == 0
    shards, kinds, names = [], [], []
    for name, kind in SHARDED:
        wb = w[name].astype(BF16)
        if kind == "win":
            main = wb.shape[-1] - HEAD_DIM
            assert main % LANES == 0
            zeros = jnp.zeros(wb.shape[:-1] + (HEAD_DIM,), BF16)
            shards += [jnp.where(even, wb[..., :main], wb[..., HEAD_DIM:]),
                       jnp.where(even, jnp.concatenate([wb[..., main:], zeros], -1),
                                 jnp.concatenate([zeros, wb[..., :HEAD_DIM]], -1))]
            kinds += ["win_main", "slot"]
            names += [name, name + "_strad"]
        else:
            shards.append(wb)
            kinds.append(kind)
            names.append(name)
    return names, kinds, shards


def _finish_w_in(full):
    full = dict(full)
    strad = full.pop("w_in_strad")
    main = full["w_in"].shape[1] // N_CHIPS - HEAD_DIM
    for i in range(N_CHIPS // 2):
        lo = main + 2 * (main + HEAD_DIM) * i
        full["w_in"] = full["w_in"].at[:, lo:lo + LANES].set(strad[2 * i] + strad[2 * i + 1])
    return full


def _scatter_pieces(shards):
    names, kinds, sizes, srcs = [], [], [], []
    for name, kind in SHARDED:
        shp = shards[name].shape
        if kind == "win":
            names += [name, name + "_strad"]
            kinds += ["win_main", "win_strad"]
            sizes += [shp[2] - HEAD_DIM] * 2
            srcs += [name, name]
        else:
            names.append(name)
            kinds.append(kind)
            sizes.append(shp[1] if kind == "row" else shp[2])
            srcs.append(name)
    return names, kinds, sizes, srcs


def _finish_weight_grads(reduced, names, tag):
    out = dict(zip(names, _exchange_layers(reduced, name=f"{tag}_layers")))
    if "w_in_strad" in out:
        strad = out.pop("w_in_strad")
        even = lax.axis_index("y") == 0
        out["w_in"] = jnp.where(even, jnp.concatenate([out["w_in"], strad[..., :HEAD_DIM]], -1),
                                jnp.concatenate([strad[..., HEAD_DIM:], out["w_in"]], -1))
    return out


class _Grads:
    def __init__(self):
        self.arrays = {}

    def put(self, weight, layer, a, b, *, cols=None, col_off=0, **kw):
        self.arrays[weight, layer] = _mm(a, b, mode="tn", out_dtype=BF16, out_cols=cols, out_col_off=col_off,
                                         out_into=self.arrays.get((weight, layer)), **kw)


def _ffn_fwd(x, h, w_up, w_down, tag):
    t, d = x.shape
    f = w_down.shape[0]
    a, gate, up = _mm_swiglu_fwd(h, w_up, tm=_div_tile(t, ROWS_NARROW, 8), tn=MXU_N, name=f"{tag}_up")
    x_out = _mm(a, w_down, mode="nn", out_dtype=F32, tm=_div_tile(t, ROWS_WIDE, 8), tn=d, tk=f, alpha=0.5, res=x, name=f"{tag}_down")
    return x_out, (x, h, a, gate, up)


def _ffn_bwd(dx, dxb, saved, norm_g, w_up, w_down, layer, grads, wname, tag, scatter):
    x, h, a, gate, up = saved
    t, d = x.shape
    f = w_down.shape[0]
    tn = _div_tile(f, 1408)
    grads.put(f"{wname}_w_down", layer, a, dxb, tm=tn, tn=d, tk=ROWS_CONTRACTED, alpha=0.5, name=f"{tag}_dwd")
    d_gate, d_up = _mm_swiglu_bwd(dxb, w_down, gate, up, alpha=0.5, tm=_div_tile(t, ROWS_NARROW, 8), tn=MXU_N, name=f"{tag}_da")
    grads.put(f"{wname}_w_up", layer, h, d_gate, cols=2 * f, tm=d, tn=tn, tk=ROWS_CONTRACTED, name=f"{tag}_dwg")
    grads.put(f"{wname}_w_up", layer, h, d_up, cols=2 * f, col_off=f // tn, tm=d, tn=tn, tk=ROWS_CONTRACTED, name=f"{tag}_dwu")
    started = scatter(layer, [f"{wname}_w_up", f"{wname}_w_down"])
    dh = _mm(d_gate, w_up, mode="nt", out_dtype=F32, tm=_div_tile(t, ROWS_WIDE, 8), tn=d, tk=f, name=f"{tag}_dh1")
    dh = _mm(d_up, w_up, mode="nt", out_dtype=F32, tm=_div_tile(t, ROWS_WIDE, 8), tn=d, tk=f, b_k_off=1, res=dh, name=f"{tag}_dh2")
    return _rms_bwd(dh, x, norm_g + started, dx, tt=NORM_ROWS, name=f"{tag}_dnorm")


def _to_heads(y, b, n_heads):
    t, w = y.shape
    return y.reshape(b, t // b, n_heads, HEAD_DIM).transpose(0, 2, 1, 3)


N_QKV = 3 * (DIL_HEADS + NA_HEADS) * HEAD_DIM


def _mixer_fwd(x, b, norm_g, full, bias, tabs, tag):
    t, d = x.shape
    s = t // b
    n_in = full["w_in"].shape[1]
    h = _rms_fwd(x, norm_g, tt=NORM_ROWS, name=f"{tag}_norm")
    proj = _mm(h, full["w_in"], mode="nn", out_dtype=F32, tm=_div_tile(t, ROWS_NARROW, 8), tn=MXU_N, tk=d, name=f"{tag}_in")
    heads = _split_heads(proj.reshape(b, s, -1), *tabs, n_pairs=N_QKV // LANES, rot_pairs=DIL_HEADS,
                         scale_ranges=((0, DIL_HEADS // 2), (3 * DIL_HEADS // 2, (3 * DIL_HEADS + NA_HEADS) // 2)),
                         name=f"{tag}_heads")
    ya, ya_heads, lse_a = _dil_attn_fwd(heads, name=f"{tag}_dil")
    yb_tokens, yb, lse_b = _na_attn_fwd(heads, bias, first=3 * DIL_HEADS, name=f"{tag}_na")
    ya2, yb2 = ya.reshape(t, -1), yb_tokens.reshape(t, -1)
    z = _mm(ya2, full["w_branch_a"], mode="nn", out_dtype=F32, tm=_div_tile(t, ROWS_NARROW, 8), tn=MXU_N, tk=ya2.shape[1],
            out_slab=(0, 2), name=f"{tag}_za")
    z = _mm(yb2, full["w_branch_b"], mode="nn", out_dtype=F32, tm=_div_tile(t, ROWS_NARROW, 8), tn=MXU_N, tk=yb2.shape[1],
            out_slab=(1, 2), out_into=z, name=f"{tag}_zb")
    merged = _gate_fwd(proj, z, gate_col=N_QKV, tt=GATE_ROWS, name=f"{tag}_gate")
    x_out = _mm(merged, full["w_out"], mode="nn", out_dtype=F32, tm=_div_tile(t, ROWS_NARROW, 8), tn=MXU_N, tk=d, res=x, name=f"{tag}_out")
    return x_out, (x, h, proj, heads, ya_heads, lse_a, yb, lse_b, ya2, yb2, z, merged)


def _mixer_bwd(dx, dob, b, saved, norm_g, full, layer, bias, tabs, grads, tag, scatter):
    x, h, proj, heads, ya, lse_a, yb, lse_b, ya2, yb2, z, merged = saved
    t, d = x.shape
    s = t // b
    n_in = full["w_in"].shape[1]
    grads.put("w_out", layer, merged, dob, tm=d, tn=d, tk=ROWS_CONTRACTED, name=f"{tag}_dwo")
    dm = _mm(dob, full["w_out"], mode="nt", out_dtype=F32, tm=_div_tile(t, ROWS_NARROW, 8), tn=MXU_N, tk=d, name=f"{tag}_dm")
    dz, dproj = _gate_bwd(dm, proj, z, gate_col=N_QKV, tt=GATE_ROWS, name=f"{tag}_dgate")
    grads.put("w_branch_a", layer, ya2, dz, b_sel=0, tm=ya2.shape[1], tn=d, tk=ROWS_CONTRACTED, name=f"{tag}_dwa")
    grads.put("w_branch_b", layer, yb2, dz, b_sel=1, tm=yb2.shape[1], tn=d, tk=ROWS_CONTRACTED, name=f"{tag}_dwb")
    started = scatter(layer, ["w_out", "w_branch_a", "w_branch_b"])
    dya = _mm(dz, full["w_branch_a"], mode="nt", out_dtype=F32, tm=_div_tile(t, ROWS_NARROW, 8), tn=MXU_N, tk=d, a_sel=0, name=f"{tag}_dya")
    dyb = _mm(dz, full["w_branch_b"], mode="nt", out_dtype=F32, tm=_div_tile(t, ROWS_NARROW, 8), tn=MXU_N, tk=d, a_sel=1, name=f"{tag}_dyb")
    d_dil = _dil_attn_bwd(heads, ya, lse_a, _to_heads(dya, b, DIL_GROUP_HEADS), name=f"{tag}_ddil")
    d_na, d_bias = _na_attn_bwd(heads, bias, yb, lse_b, _to_heads(dyb, b, NA_HEADS), first=3 * DIL_HEADS, name=f"{tag}_dna")
    dproj = _merge_heads(d_dil, *tabs, heads_per_row=DIL_GROUP_HEADS, rot_pairs=DIL_HEADS, scale_pairs=DIL_HEADS // 2,
                         dilated=True, out_cols=n_in, tile_off=0, into=dproj.reshape(b, s, n_in), name=f"{tag}_dheads_a")
    dproj = _merge_heads(d_na, *tabs, heads_per_row=NA_HEADS, rot_pairs=0, scale_pairs=NA_HEADS // 2, dilated=False,
                         out_cols=n_in, tile_off=3 * DIL_HEADS // 2, into=dproj, name=f"{tag}_dheads_b").reshape(t, n_in)
    grads.put("w_in", layer, h, dproj, tm=_div_tile(d, 512), tn=_div_tile(n_in, 2944), tk=ROWS_CONTRACTED // 2, name=f"{tag}_dwin")
    started = started + scatter(layer, ["w_in"])
    dh = _mm(dproj, full["w_in"], mode="nt", out_dtype=F32, tm=_div_tile(t, 2 * ROWS_WIDE, 8), tn=d, tk=_div_tile(n_in, 2944),
             name=f"{tag}_dh")
    dx_in, dxb_in, d_norm = _rms_bwd(dh, x, norm_g + started, dx, tt=NORM_ROWS, name=f"{tag}_dnorm")
    d_rb = _na_collapse_bias(d_bias, name=f"{tag}_dbias")
    return dx_in, dxb_in, d_norm, d_rb


def kernel(x, ffn1_norm, ffn1_w_up, ffn1_w_down, mix_norm, w_in, na_rel_bias, w_branch_a, w_branch_b, w_out, ffn2_norm, ffn2_w_up, ffn2_w_down, final_norm, loss_target, m_ffn1_norm, m_ffn1_w_up, m_ffn1_w_down, m_mix_norm, m_w_in, m_na_rel_bias, m_w_branch_a, m_w_branch_b, m_w_out, m_ffn2_norm, m_ffn2_w_up, m_ffn2_w_down, m_final_norm, v_ffn1_norm, v_ffn1_w_up, v_ffn1_w_down, v_mix_norm, v_w_in, v_na_rel_bias, v_w_branch_a, v_w_branch_b, v_w_out, v_ffn2_norm, v_ffn2_w_up, v_ffn2_w_down, v_final_norm):
    w = dict(ffn1_norm=ffn1_norm, ffn1_w_up=ffn1_w_up, ffn1_w_down=ffn1_w_down, mix_norm=mix_norm, w_in=w_in,
             na_rel_bias=na_rel_bias, w_branch_a=w_branch_a, w_branch_b=w_branch_b, w_out=w_out, ffn2_norm=ffn2_norm,
             ffn2_w_up=ffn2_w_up, ffn2_w_down=ffn2_w_down, final_norm=final_norm)
    mom = dict(ffn1_norm=m_ffn1_norm, ffn1_w_up=m_ffn1_w_up, ffn1_w_down=m_ffn1_w_down, mix_norm=m_mix_norm, w_in=m_w_in,
               na_rel_bias=m_na_rel_bias, w_branch_a=m_w_branch_a, w_branch_b=m_w_branch_b, w_out=m_w_out,
               ffn2_norm=m_ffn2_norm, ffn2_w_up=m_ffn2_w_up, ffn2_w_down=m_ffn2_w_down, final_norm=m_final_norm)
    var = dict(ffn1_norm=v_ffn1_norm, ffn1_w_up=v_ffn1_w_up, ffn1_w_down=v_ffn1_w_down, mix_norm=v_mix_norm, w_in=v_w_in,
               na_rel_bias=v_na_rel_bias, w_branch_a=v_w_branch_a, w_branch_b=v_w_branch_b, w_out=v_w_out,
               ffn2_norm=v_ffn2_norm, ffn2_w_up=v_ffn2_w_up, ffn2_w_down=v_ffn2_w_down, final_norm=v_final_norm)
    b, s, d = x.shape
    t = b * s
    depth = ffn1_norm.shape[0]
    assert depth == 2, "core c of a chip sends / reduces layer c"
    shards = {name: w[name] for name, _ in SHARDED}

    names, kinds, pieces = _weight_pieces(w)
    by_layer = [[p[l:l + 1] for p in pieces] for l in range(depth)]
    own = [[_place_own(p, kind, 0, name=f"own{l}_{nm}") for nm, kind, p in zip(names, kinds, by_layer[l])] for l in range(depth)]
    full = [{}, {}]

    def gather_start(layer, group, after, tag):
        idx = [i for i, nm in enumerate(names) if nm in group]
        pick = lambda seq: [seq[i] for i in idx]
        *state, token = _gather_layer_start(pick(by_layer[layer]), pick(kinds), pick(own[layer]), 0, after, name=f"{tag}_start")
        return (layer, idx, tag, state), token[:1, :1]

    def gather_finish(started, after):
        layer, idx, tag, state = started
        pick = lambda seq: [seq[i] for i in idx]
        landed = _gather_layer_wait(*state, pick(kinds), 0, after, name=f"{tag}_wait")
        done = _gather_layer_forward([by_layer[layer][i].shape for i in idx], pick(kinds), landed, name=f"{tag}_forward")
        full[layer].update(zip(pick(names), done))
        return done[0]

    ffn1, mixer, ffn2 = names[:2], names[2:7], names[7:]
    assert mixer[0] == "w_in" and ffn2[0] == "ffn2_w_up", names
    xc = x.reshape(t, d)
    l0_ffn1, token_ffn1 = gather_start(0, ffn1, xc, "gather_l0_ffn1")
    tabs = _rope_tables(s)
    bias = _na_expand_bias(na_rel_bias, name="na_bias")

    saved = []
    h = _rms_fwd(xc, ffn1_norm[:1] + token_ffn1, tt=NORM_ROWS, name="l0_ffn1_norm")
    landed = gather_finish(l0_ffn1, h)
    l0_mixer, token_mixer = gather_start(0, mixer, landed, "gather_l0_mixer")
    xc, s1 = _ffn_fwd(xc, h + token_mixer.astype(BF16), full[0]["ffn1_w_up"], full[0]["ffn1_w_down"], "l0_ffn1")
    landed = gather_finish(l0_mixer, xc)
    full[0] = _finish_w_in(full[0])
    l0_ffn2, token_ffn2 = gather_start(0, ffn2, landed, "gather_l0_ffn2")
    layer1, token_layer1 = gather_start(1, names, landed, "gather_l1")
    xc, s2 = _mixer_fwd(xc, b, mix_norm[:1] + token_ffn2 + token_layer1, full[0], bias[0], tabs, "l0_mix")
    gather_finish(l0_ffn2, xc)
    xc, s3 = _ffn_fwd(xc, _rms_fwd(xc, ffn2_norm[:1], tt=NORM_ROWS, name="l0_ffn2_norm"), full[0]["ffn2_w_up"], full[0]["ffn2_w_down"],
                      "l0_ffn2")
    saved.append((s1, s2, s3))
    gather_finish(layer1, xc)
    full[1] = _finish_w_in(full[1])
    for l in range(1, depth):
        xc, s1 = _ffn_fwd(xc, _rms_fwd(xc, ffn1_norm[l:l + 1], tt=NORM_ROWS, name=f"l{l}_ffn1_norm"), full[l]["ffn1_w_up"],
                          full[l]["ffn1_w_down"], f"l{l}_ffn1")
        xc, s2 = _mixer_fwd(xc, b, mix_norm[l:l + 1], full[l], bias[l], tabs, f"l{l}_mix")
        xc, s3 = _ffn_fwd(xc, _rms_fwd(xc, ffn2_norm[l:l + 1], tt=NORM_ROWS, name=f"l{l}_ffn2_norm"), full[l]["ffn2_w_up"],
                          full[l]["ffn2_w_down"], f"l{l}_ffn2")
        saved.append((s1, s2, s3))

    dx, dxb, d_final, loss_part = _final_loss(xc, final_norm.reshape(1, d), loss_target.reshape(t, d), tt=NORM_ROWS, name="final_loss")
    grads = _Grads()
    piece_names, piece_kinds, piece_sizes, piece_srcs = _scatter_pieces(shards)
    scattered = []

    def scatter(layer, weights):
        tag = f"grads{layer}_{weights[0]}"
        idx = [i for i, src in enumerate(piece_srcs) if src in weights]
        pick = lambda seq: [seq[i] for i in idx]
        *state, token = _grads_to_chips_start([grads.arrays[src, layer] for src in pick(piece_srcs)], pick(piece_kinds),
                                              pick(piece_sizes), layer, name=f"{tag}_to_chips_start")
        scattered.append((layer, idx, state))
        return token[:1, :1]
    small = {name: [None] * depth for name in REPLICATED[:-1]}
    for l in reversed(range(depth)):
        s1, s2, s3 = saved[l]
        dx, dxb, small["ffn2_norm"][l] = _ffn_bwd(dx, dxb, s3, ffn2_norm[l:l + 1], full[l]["ffn2_w_up"], full[l]["ffn2_w_down"],
                                                  l, grads, "ffn2", f"l{l}_ffn2", scatter)
        dx, dxb, small["mix_norm"][l], small["na_rel_bias"][l] = _mixer_bwd(
            dx, dxb, b, s2, mix_norm[l:l + 1], full[l], l, bias[l], tabs, grads, f"l{l}_mix", scatter)
        dx, dxb, small["ffn1_norm"][l] = _ffn_bwd(dx, dxb, s1, ffn1_norm[l:l + 1], full[l]["ffn1_w_up"], full[l]["ffn1_w_down"],
                                                  l, grads, "ffn1", f"l{l}_ffn1", scatter)
    grad_x = dx.reshape(b, s, d)
    reduced = [None] * len(piece_names)

    def arrive(group, after):
        layer, idx, state = group
        state = _grads_to_chips_wait(*state, [piece_kinds[i] for i in idx], [piece_sizes[i] for i in idx], layer, after,
                                     name=f"grads{layer}_{piece_names[idx[0]]}_to_chips_wait")
        for i, p, sl in zip(idx, *state):
            reduced[i] = _sum_slabs(sl, p, piece_kinds[i], piece_sizes[i], layer, reduced[i],
                                    name=f"grads{layer}_sum_{piece_names[i]}")
        return idx

    for group in scattered[:-1]:
        arrive(group, dx)
    late = scattered[-1][1]
    early = [i for i in range(len(piece_names)) if i not in late]
    g_out = _finish_weight_grads([reduced[i] for i in early], [piece_names[i] for i in early], "grads_early")

    parts = [jnp.stack(small[name]).reshape(-1) for name in REPLICATED[:-1]] + [d_final.reshape(-1), loss_part[0, :1]]
    sizes = [v.shape[0] for v in parts]
    flat = jnp.concatenate(parts)
    flat = jnp.pad(flat, (0, -flat.shape[0] % (8 * LANES)))
    small_sum = _all_sum_small(flat.reshape(-1, LANES), name="small_all_sum").reshape(-1)
    off = 0
    for name, n in zip(REPLICATED, sizes[:-1]):
        g_out[name] = small_sum[off:off + n].reshape(w[name].shape)
        off += n
    loss = small_sum[off]

    names = list(w)
    delta, new_m, new_v = {}, {}, {}
    for name in [n for n in names if n in g_out]:
        delta[name], new_m[name], new_v[name] = _adamw(w[name], g_out[name], mom[name], var[name], name=f"adamw_{name}")
    arrive(scattered[-1], delta["w_in"])
    g_out.update(_finish_weight_grads([reduced[i] for i in late], [piece_names[i] for i in late], "grads_late"))
    for name in [n for n in names if n not in delta]:
        delta[name], new_m[name], new_v[name] = _adamw(w[name], g_out[name], mom[name], var[name], name=f"adamw_{name}")
    return (loss, grad_x, *[g_out[n] for n in names], *[delta[n] for n in names], *[new_m[n] for n in names],
            *[new_v[n] for n in names])
```

```python
import functools

import numpy as np
import jax
import jax.numpy as jnp
from jax import lax
from jax.experimental import pallas as pl
from jax.experimental.pallas import tpu as pltpu

F32, BF16 = jnp.float32, jnp.bfloat16
MESH = pl.DeviceIdType.MESH

HEAD_DIM = 64
DILATIONS = (1, 4, 16)
DIL_HALF = 64
DIL_GROUP_HEADS = 4
DIL_HEADS = 12
NA_HEADS = 8
GRID_W = 64
NA_ROWS = 8
NA_COLS = 16
ROPE_THETA = 10000.0
RMS_EPS = 1e-6
NEG_INF = -1e30
ADAM_LR, ADAM_B1, ADAM_B2, ADAM_EPS, ADAM_WD, ADAM_STEP = 0.001, 0.9, 0.999, 1e-08, 0.01, 10
QK_SCALE = HEAD_DIM ** -0.5

N_CHIPS = 4
LANES = 128
BF16_ROWS = 16
VMEM_LIMIT = 56 * 1024 * 1024
MXU_N = 256
ROWS_NARROW = 4096
ROWS_WIDE = 512
NORM_ROWS = 1024
GATE_ROWS = 4096
ROWS_CONTRACTED = 4096
BLOCKS_IN_FLIGHT = 8

_NN = (((1,), (0,)), ((), ()))
_NT = (((1,), (1,)), ((), ()))
_TN = (((0,), (0,)), ((), ()))

HBM = pl.BlockSpec(memory_space=pl.ANY)


def _params(**kw):
    return pltpu.CompilerParams(vmem_limit_bytes=VMEM_LIMIT, **kw)


def _dot(a, b, dims):
    return lax.dot_general(a, b, dims, preferred_element_type=F32)


def _div_tile(n, cap, mult=LANES):
    best = None
    for t in range(mult, min(n, cap) + 1, mult):
        if n % t == 0:
            best = t
    return n if best is None else best


def _stacked(block, index, sel):
    if sel is None:
        return pl.BlockSpec(block, index)
    return pl.BlockSpec((None,) + block, lambda *g: (sel,) + index(*g))


def _mm(a, b, *, mode, out_dtype, tm, tn, tk, name, alpha=1.0, res=None, a_sel=None, b_sel=None, b_k_off=0,
        out_slab=None, out_cols=None, out_col_off=0, out_into=None):
    a2, b2 = a.shape[-2:], b.shape[-2:]
    if mode == "nn":
        (m, k), n = a2, b2[1]
        a_spec = _stacked((tm, tk), lambda i, j, kk: (i, kk), a_sel)
        b_spec = _stacked((tk, tn), lambda i, j, kk: (kk + b_k_off, j), b_sel)
        dims = _NN
    elif mode == "nt":
        (m, k), n = a2, b2[0]
        a_spec = _stacked((tm, tk), lambda i, j, kk: (i, kk), a_sel)
        b_spec = _stacked((tn, tk), lambda i, j, kk: (j, kk + b_k_off), b_sel)
        dims = _NT
    else:
        (k, m), n = a2, b2[1]
        a_spec = _stacked((tk, tm), lambda i, j, kk: (kk, i), a_sel)
        b_spec = _stacked((tk, tn), lambda i, j, kk: (kk + b_k_off, j), b_sel)
        dims = _TN
    assert m % tm == 0 and n % tn == 0 and k % tk == 0, (name, a.shape, b.shape)
    nk = k // tk
    has_res = res is not None
    if out_slab is None:
        o_spec = pl.BlockSpec((tm, tn), lambda i, j, kk: (i, j + out_col_off))
        out_shape = jax.ShapeDtypeStruct((m, n if out_cols is None else out_cols), out_dtype)
    else:
        o_spec = _stacked((tm, tn), lambda i, j, kk: (i, j + out_col_off), out_slab[0])
        out_shape = jax.ShapeDtypeStruct((out_slab[1], m, n if out_cols is None else out_cols), out_dtype)
    r_spec = pl.BlockSpec((tm, tn), lambda i, j, kk: (i, j))
    n_in = 2 + has_res + (out_into is not None)

    def body(*refs):
        a_ref, b_ref = refs[0], refs[1]
        r_ref = refs[2] if has_res else None
        o_ref = refs[n_in]
        p = _dot(a_ref[...], b_ref[...], dims)

        def finish(acc):
            y = acc * alpha if alpha != 1.0 else acc
            if has_res:
                y = y + r_ref[...].astype(F32)
            o_ref[...] = y.astype(o_ref.dtype)

        if nk == 1:
            finish(p)
        else:
            acc_ref = refs[n_in + 1]
            kk = pl.program_id(2)

            @pl.when(kk == 0)
            def _():
                acc_ref[...] = p

            @pl.when(kk > 0)
            def _():
                acc_ref[...] += p

            @pl.when(kk == nk - 1)
            def _():
                finish(acc_ref[...])

    operands = [a, b] + ([res] if has_res else [])
    in_specs = [a_spec, b_spec] + ([r_spec] if has_res else [])
    aliases = {}
    if out_into is not None:
        aliases = {len(operands): 0}
        operands.append(out_into)
        in_specs.append(HBM)
    return pl.pallas_call(
        body, name=name, grid=(m // tm, n // tn, nk), in_specs=in_specs, out_specs=o_spec, out_shape=out_shape,
        scratch_shapes=[pltpu.VMEM((tm, tn), F32)] if nk > 1 else [], input_output_aliases=aliases,
        compiler_params=_params(dimension_semantics=("parallel", "parallel", "arbitrary")),
    )(*operands)


def _mm_swiglu_fwd(h, w_up, *, tm, tn, name):
    m, k = h.shape
    n = w_up.shape[1] // 2
    h_spec = pl.BlockSpec((tm, k), lambda i, j: (i, 0))
    wg_spec = pl.BlockSpec((k, tn), lambda i, j: (0, j))
    wu_spec = pl.BlockSpec((k, tn), lambda i, j: (0, j + n // tn))
    o_spec = pl.BlockSpec((tm, tn), lambda i, j: (i, j))

    def body(h_ref, wg_ref, wu_ref, a_ref, g_ref, u_ref):
        hb = h_ref[...]
        g = _dot(hb, wg_ref[...], _NN)
        u = _dot(hb, wu_ref[...], _NN)
        a_ref[...] = (g * jax.nn.sigmoid(g) * u).astype(BF16)
        g_ref[...] = g.astype(BF16)
        u_ref[...] = u.astype(BF16)

    out = jax.ShapeDtypeStruct((m, n), BF16)
    return pl.pallas_call(
        body, name=name, grid=(m // tm, n // tn), in_specs=[h_spec, wg_spec, wu_spec],
        out_specs=[o_spec] * 3, out_shape=[out] * 3,
        compiler_params=_params(dimension_semantics=("parallel", "parallel")),
    )(h, w_up, w_up)


def _mm_swiglu_bwd(dy, w_down, gate, up, *, alpha, tm, tn, name):
    m, k = dy.shape
    n = w_down.shape[0]
    dy_spec = pl.BlockSpec((tm, k), lambda i, j: (i, 0))
    w_spec = pl.BlockSpec((tn, k), lambda i, j: (j, 0))
    o_spec = pl.BlockSpec((tm, tn), lambda i, j: (i, j))

    def body(dy_ref, w_ref, g_ref, u_ref, dg_ref, du_ref):
        da = _dot(dy_ref[...], w_ref[...], _NT) * alpha
        g = g_ref[...].astype(F32)
        u = u_ref[...].astype(F32)
        sg = jax.nn.sigmoid(g)
        dg_ref[...] = (da * u * (sg * (1.0 + g * (1.0 - sg)))).astype(BF16)
        du_ref[...] = (da * (g * sg)).astype(BF16)

    out = jax.ShapeDtypeStruct((m, n), BF16)
    return pl.pallas_call(
        body, name=name, grid=(m // tm, n // tn), in_specs=[dy_spec, w_spec, o_spec, o_spec],
        out_specs=[o_spec] * 2, out_shape=[out] * 2,
        compiler_params=_params(dimension_semantics=("parallel", "parallel")),
    )(dy, w_down, gate, up)


def _rms_fwd(x, g, *, tt, name):
    t, d = x.shape

    def body(x_ref, g_ref, h_ref):
        xv = x_ref[...]
        rstd = lax.rsqrt(jnp.mean(xv * xv, axis=1, keepdims=True) + RMS_EPS)
        h_ref[...] = (xv * rstd * g_ref[...]).astype(BF16)

    return pl.pallas_call(
        body, name=name, grid=(t // tt,),
        in_specs=[pl.BlockSpec((tt, d), lambda i: (i, 0)), pl.BlockSpec((1, d), lambda i: (0, 0))],
        out_specs=pl.BlockSpec((tt, d), lambda i: (i, 0)), out_shape=jax.ShapeDtypeStruct((t, d), BF16),
        compiler_params=_params(dimension_semantics=("parallel",)),
    )(x, g)


def _rms_bwd(dh, x, g, dres, *, tt, name):
    t, d = x.shape

    def body(dh_ref, x_ref, g_ref, r_ref, dx_ref, dxb_ref, dg_ref):
        xv = x_ref[...]
        rstd = lax.rsqrt(jnp.mean(xv * xv, axis=1, keepdims=True) + RMS_EPS)
        xhat = xv * rstd
        dhv = dh_ref[...]
        dxhat = dhv * g_ref[...]
        dx = r_ref[...] + rstd * (dxhat - xhat * jnp.mean(dxhat * xhat, axis=1, keepdims=True))
        dx_ref[...] = dx
        dxb_ref[...] = dx.astype(BF16)

        @pl.when(pl.program_id(0) == 0)
        def _():
            dg_ref[...] = jnp.zeros_like(dg_ref)

        dg_ref[...] += jnp.sum(dhv * xhat, axis=0, keepdims=True)

    row = pl.BlockSpec((tt, d), lambda i: (i, 0))
    vec = pl.BlockSpec((1, d), lambda i: (0, 0))
    return pl.pallas_call(
        body, name=name, grid=(t // tt,), in_specs=[row, row, vec, row], out_specs=[row, row, vec],
        out_shape=[jax.ShapeDtypeStruct((t, d), F32), jax.ShapeDtypeStruct((t, d), BF16), jax.ShapeDtypeStruct((1, d), F32)],
        compiler_params=_params(dimension_semantics=("arbitrary",)),
    )(dh, x, g, dres)


def _final_loss(x, g, target, *, tt, name):
    t, d = x.shape

    def body(x_ref, g_ref, t_ref, dx_ref, dxb_ref, dg_ref, loss_ref):
        xv = x_ref[...]
        gv = g_ref[...]
        rstd = lax.rsqrt(jnp.mean(xv * xv, axis=1, keepdims=True) + RMS_EPS)
        xhat = xv * rstd
        err = xhat * gv - t_ref[...]
        dy = err * (1.0 / d)
        dxhat = dy * gv
        dx = rstd * (dxhat - xhat * jnp.mean(dxhat * xhat, axis=1, keepdims=True))
        dx_ref[...] = dx
        dxb_ref[...] = dx.astype(BF16)

        @pl.when(pl.program_id(0) == 0)
        def _():
            dg_ref[...] = jnp.zeros_like(dg_ref)
            loss_ref[...] = jnp.zeros_like(loss_ref)

        dg_ref[...] += jnp.sum(dy * xhat, axis=0, keepdims=True)
        part = 0.5 * jnp.sum(jnp.mean(err * err, axis=1, keepdims=True), axis=0, keepdims=True)
        loss_ref[...] += jnp.broadcast_to(part, loss_ref.shape)

    row = pl.BlockSpec((tt, d), lambda i: (i, 0))
    vec = pl.BlockSpec((1, d), lambda i: (0, 0))
    one = pl.BlockSpec((1, LANES), lambda i: (0, 0))
    return pl.pallas_call(
        body, name=name, grid=(t // tt,), in_specs=[row, vec, row], out_specs=[row, row, vec, one],
        out_shape=[jax.ShapeDtypeStruct((t, d), F32), jax.ShapeDtypeStruct((t, d), BF16), jax.ShapeDtypeStruct((1, d), F32),
                   jax.ShapeDtypeStruct((1, LANES), F32)],
        compiler_params=_params(dimension_semantics=("arbitrary",)),
    )(x, g, target)


def _swap_halves(x):
    lane = lax.broadcasted_iota(jnp.int32, x.shape, 1)
    return jnp.where((lane // 32) % 2 == 0, pltpu.roll(x, 96, 1), pltpu.roll(x, 32, 1))


def _rope_tables(s):
    half = HEAD_DIM // 2
    inv_freq = ROPE_THETA ** (-jnp.arange(half, dtype=F32) / half)
    ang = jnp.arange(s).astype(F32)[:, None] * inv_freq[None, :]
    cos, sin = jnp.cos(ang), jnp.sin(ang)
    return jnp.tile(cos, (1, 4)), jnp.concatenate([-sin, sin, -sin, sin], axis=1)


def _dilation_of_tile(p):
    dilated = p < 3 * DIL_HEADS // 2
    g = (p % (DIL_HEADS // 2)) // (DIL_GROUP_HEADS // 2)
    return [(dilated & (g == gi)) | (jnp.logical_not(dilated) if gi == 0 else False) for gi in range(len(DILATIONS))]


def _residue_major(ref, d):
    s = ref.shape[0]
    if d == 1:
        return ref[...]
    return jnp.concatenate([ref[pl.ds(r, s // d, stride=d), :] for r in range(d)], axis=0)


def _split_heads(proj, cos4, sin4, *, n_pairs, rot_pairs, scale_ranges, name):
    b, s, _ = proj.shape

    def body(x_ref, c_ref, s_ref, o_ref):
        p = pl.program_id(1)
        is_q = functools.reduce(jnp.logical_or, [(p >= lo) & (p < hi) for lo, hi in scale_ranges])
        scale = jnp.where(is_q, QK_SCALE, 1.0)

        def put(y):
            o_ref[0] = y[:, :HEAD_DIM].astype(BF16)
            o_ref[1] = y[:, HEAD_DIM:].astype(BF16)

        for d, in_group in zip(DILATIONS, _dilation_of_tile(p)):
            @pl.when(in_group & (p < rot_pairs))
            def _(d=d):
                x = _residue_major(x_ref, d)
                put((x * _residue_major(c_ref, d) + _swap_halves(x) * _residue_major(s_ref, d)) * scale)

            @pl.when(in_group & (p >= rot_pairs))
            def _(d=d):
                put(_residue_major(x_ref, d) * scale)

    tab = pl.BlockSpec((s, LANES), lambda bi, p: (0, 0))
    return pl.pallas_call(
        body, name=name, grid=(b, n_pairs),
        in_specs=[pl.BlockSpec((None, s, LANES), lambda bi, p: (bi, 0, p)), tab, tab],
        out_specs=pl.BlockSpec((None, 2, s, HEAD_DIM), lambda bi, p: (bi, p, 0, 0)),
        out_shape=jax.ShapeDtypeStruct((b, 2 * n_pairs, s, HEAD_DIM), BF16),
        compiler_params=_params(dimension_semantics=("parallel", "parallel")),
    )(proj, cos4, sin4)


def _merge_heads(dheads, cos4, sin4, *, heads_per_row, rot_pairs, scale_pairs, dilated, out_cols, tile_off, into, name):
    b, hpr, r, s, _ = dheads.shape
    n_pairs = hpr * r // 2
    ppr = hpr // 2

    def body(d_ref, c_ref, s_ref, *rest):
        o_ref, t_ref = rest[-2:]
        p = pl.program_id(1)
        scale = jnp.where(p < scale_pairs, QK_SCALE, 1.0)

        def tokens(d):
            dy = jnp.concatenate([d_ref[0], d_ref[1]], axis=1)
            if d == 1:
                return dy
            for res in range(d):
                t_ref[pl.ds(res, s // d, stride=d), :] = dy[res * (s // d):(res + 1) * (s // d), :]
            return t_ref[...]

        groups = _dilation_of_tile(p) if dilated else [p >= 0]
        for d, in_group in zip(DILATIONS, groups):
            @pl.when(in_group & (p < rot_pairs))
            def _(d=d):
                dy = tokens(d)
                o_ref[...] = ((dy * c_ref[...] - _swap_halves(dy) * s_ref[...]) * scale).astype(BF16)

            @pl.when(in_group & (p >= rot_pairs))
            def _(d=d):
                o_ref[...] = (tokens(d) * scale).astype(BF16)

    tab = pl.BlockSpec((s, LANES), lambda bi, p: (0, 0))
    operands = [dheads, cos4, sin4] + ([] if into is None else [into])
    return pl.pallas_call(
        body, name=name, grid=(b, n_pairs),
        in_specs=[pl.BlockSpec((None, 2, None, s, HEAD_DIM), lambda bi, p: (bi, p % ppr, p // ppr, 0, 0)), tab, tab]
        + ([] if into is None else [HBM]),
        out_specs=pl.BlockSpec((None, s, LANES), lambda bi, p: (bi, 0, p + tile_off)),
        out_shape=jax.ShapeDtypeStruct((b, s, out_cols), BF16),
        input_output_aliases={} if into is None else {3: 0},
        scratch_shapes=[pltpu.VMEM((s, LANES), F32)],
        compiler_params=_params(dimension_semantics=("parallel", "parallel")),
    )(*operands)


DIL_TQ = 256


def _dil_block(g, s):
    run = s // DILATIONS[g]
    return DIL_TQ if run <= DIL_TQ else min(run, DIL_TQ + 2 * LANES)


def _dil_keys(g, q0, s):
    run = max(s // DILATIONS[g], DIL_TQ)
    lo = (q0 // run) * run
    return pl.multiple_of(jnp.clip(q0 - LANES, lo, lo + run - _dil_block(g, s)), LANES)


def _dil_band(g, q0, start, shape, s):
    row = q0 + lax.broadcasted_iota(jnp.int32, shape, 0)
    col = start + lax.broadcasted_iota(jnp.int32, shape, 1)
    ok = jnp.abs(row - col) <= DIL_HALF
    run = s // DILATIONS[g]
    if run < DIL_TQ:
        shift = run.bit_length() - 1
        ok = ok & ((row >> shift) == (col >> shift))
    return ok


def _dil_tokens(g, q0, s):
    d = DILATIONS[g]
    if d == 1:
        return [(0, DIL_TQ, pl.ds(q0, DIL_TQ))]
    run = s // d
    n = min(run, DIL_TQ)
    return [(lo, n, pl.ds(((q0 + lo) % run) * d + (q0 + lo) // run, n, stride=d)) for lo in range(0, DIL_TQ, n)]


def _dil_gather(ref, pieces):
    return jnp.concatenate([ref[rows, :] for _, _, rows in pieces], axis=0) if len(pieces) > 1 else ref[pieces[0][2], :]


def _dil_head_spec(part, g, s):
    return pl.BlockSpec((None, None, s, HEAD_DIM), lambda b, j: (b, part * DIL_HEADS + g * DIL_GROUP_HEADS + j, 0, 0))


def _dil_attn_fwd(heads, *, name):
    b, _, s, _ = heads.shape
    n_g = len(DILATIONS)

    def body(*refs):
        qkv = refs[:3 * n_g]
        o_ref, oh_ref, l_ref, og_ref, lg_ref = refs[3 * n_g:]
        for g in range(n_g):
            q_ref, k_ref, v_ref = qkv[3 * g:3 * g + 3]
            width = _dil_block(g, s)

            def step(i, carry, g=g, q_ref=q_ref, k_ref=k_ref, v_ref=v_ref, width=width):
                q0 = pl.multiple_of(i * DIL_TQ, DIL_TQ)
                start = _dil_keys(g, q0, s)
                sc = _dot(q_ref[pl.ds(q0, DIL_TQ), :], k_ref[pl.ds(start, width), :], _NT)
                sc = jnp.where(_dil_band(g, q0, start, sc.shape, s), sc, NEG_INF)
                m = jnp.max(sc, axis=1, keepdims=True)
                p = jnp.exp(sc - m)
                den = jnp.sum(p, axis=1, keepdims=True)
                o = _dot(p.astype(BF16), v_ref[pl.ds(start, width), :], _NN) / den
                lse = m + jnp.log(den)
                for lo, n, rows in _dil_tokens(g, q0, s):
                    og_ref[g, rows, :] = o[lo:lo + n]
                    lg_ref[g, rows, :] = lse[lo:lo + n]
                return carry

            lax.fori_loop(0, s // DIL_TQ, step, 0, unroll=BLOCKS_IN_FLIGHT)
        lses = [lg_ref[g] for g in range(n_g)]
        m = functools.reduce(jnp.maximum, lses)
        ws = [jnp.exp(l - m) for l in lses]
        den = functools.reduce(jnp.add, ws)
        mixed = (functools.reduce(jnp.add, [w * og_ref[g] for g, w in enumerate(ws)]) / den).astype(o_ref.dtype)
        l_ref[...] = m + jnp.log(den)
        oh_ref[...] = mixed

        @pl.when(pl.program_id(1) % 2 == 0)
        def _():
            o_ref[:, :HEAD_DIM] = mixed

        @pl.when(pl.program_id(1) % 2 == 1)
        def _():
            o_ref[:, HEAD_DIM:] = mixed

    out = pl.BlockSpec((None, s, 2 * HEAD_DIM), lambda bi, j: (bi, 0, j // 2))
    lse = pl.BlockSpec((None, None, s, 1), lambda bi, j: (bi, j, 0, 0))
    return pl.pallas_call(
        body, name=name, grid=(b, DIL_GROUP_HEADS),
        in_specs=[_dil_head_spec(part, g, s) for g in range(n_g) for part in range(3)],
        out_specs=[out, pl.BlockSpec((None, None, s, HEAD_DIM), lambda bi, j: (bi, j, 0, 0)), lse],
        out_shape=[jax.ShapeDtypeStruct((b, s, DIL_GROUP_HEADS * HEAD_DIM), BF16),
                   jax.ShapeDtypeStruct((b, DIL_GROUP_HEADS, s, HEAD_DIM), BF16),
                   jax.ShapeDtypeStruct((b, DIL_GROUP_HEADS, s, 1), F32)],
        scratch_shapes=[pltpu.VMEM((n_g, s, HEAD_DIM), F32), pltpu.VMEM((n_g, s, 1), F32)],
        compiler_params=_params(dimension_semantics=("parallel", "arbitrary")),
    )(*([heads] * (3 * n_g)))


def _dil_attn_bwd(heads, out, lse, dout, *, name):
    b, _, s, _ = heads.shape
    n_g = len(DILATIONS)

    def body(*refs):
        qkv = refs[:3 * n_g]
        o_ref, l_ref, do_ref, d_ref, delta_ref = refs[3 * n_g:]
        d_ref[...] = jnp.zeros_like(d_ref)
        delta_ref[...] = jnp.sum(do_ref[...] * o_ref[...].astype(F32), axis=1, keepdims=True)
        for g in range(n_g):
            q_ref, k_ref, v_ref = qkv[3 * g:3 * g + 3]
            width = _dil_block(g, s)

            def step(i, carry, g=g, q_ref=q_ref, k_ref=k_ref, v_ref=v_ref, width=width):
                q0 = pl.multiple_of(i * DIL_TQ, DIL_TQ)
                start = _dil_keys(g, q0, s)
                win = pl.ds(start, width)
                pieces = _dil_tokens(g, q0, s)
                do_b = _dil_gather(do_ref, pieces).astype(BF16)
                q, k, v = q_ref[pl.ds(q0, DIL_TQ), :], k_ref[win, :], v_ref[win, :]
                sc = _dot(q, k, _NT)
                p = jnp.where(_dil_band(g, q0, start, sc.shape, s), jnp.exp(sc - _dil_gather(l_ref, pieces)), 0.0)
                ds = (p * (_dot(do_b, v, _NT) - _dil_gather(delta_ref, pieces))).astype(BF16)
                d_ref[g, pl.ds(q0, DIL_TQ), :] = _dot(ds, k, _NN)
                d_ref[n_g + g, win, :] += _dot(ds, q, _TN)
                d_ref[2 * n_g + g, win, :] += _dot(p.astype(BF16), do_b, _TN)
                return carry

            lax.fori_loop(0, s // DIL_TQ, step, 0, unroll=BLOCKS_IN_FLIGHT)

    per_head = lambda bi, j: (bi, j, 0, 0)
    return pl.pallas_call(
        body, name=name, grid=(b, DIL_GROUP_HEADS),
        in_specs=[_dil_head_spec(part, g, s) for g in range(n_g) for part in range(3)]
        + [pl.BlockSpec((None, None, s, HEAD_DIM), per_head), pl.BlockSpec((None, None, s, 1), per_head),
           pl.BlockSpec((None, None, s, HEAD_DIM), per_head)],
        out_specs=pl.BlockSpec((None, None, 3 * n_g, s, HEAD_DIM), lambda bi, j: (bi, j, 0, 0, 0)),
        out_shape=jax.ShapeDtypeStruct((b, DIL_GROUP_HEADS, 3 * n_g, s, HEAD_DIM), F32),
        scratch_shapes=[pltpu.VMEM((s, 1), F32)],
        compiler_params=_params(dimension_semantics=("parallel", "parallel")),
    )(*([heads] * (3 * n_g)), out, lse, dout)


NA_BIAS_ROWS = 2 * NA_ROWS - 1
NA_BIAS_COLS = 2 * NA_COLS - 1
NA_BLOCK = 4
NA_SPAN = NA_ROWS + NA_BLOCK - 1
NA_Q = NA_BLOCK * GRID_W
NA_KEYS = NA_SPAN * GRID_W
NA_FORMS = 3


def _na_onehot():
    c = np.arange(GRID_W)[:, None]
    k = np.arange(GRID_W)[None, :]
    lo = np.clip(c - NA_COLS // 2, 0, GRID_W - NA_COLS)
    valid = (k >= lo) & (k < lo + NA_COLS)
    onehot = np.zeros((GRID_W, GRID_W, LANES), np.float32)
    cc, kk = np.nonzero(valid)
    onehot[cc, kk, kk - cc + NA_COLS - 1] = 1.0
    return onehot.reshape(GRID_W * GRID_W, LANES), valid.reshape(1, GRID_W * GRID_W)


def _na_block_rows(n_rows):
    table = np.full((NA_FORMS, NA_BLOCK, NA_SPAN), NA_BIAS_ROWS, np.int64)
    n_blocks = n_rows // NA_BLOCK
    for form, ib in enumerate((0, 1, n_blocks - 1)):
        base = min(max(NA_BLOCK * ib - NA_ROWS // 2, 0), n_rows - NA_SPAN)
        for rl in range(NA_BLOCK):
            r = NA_BLOCK * ib + rl
            row_lo = min(max(r - NA_ROWS // 2, 0), n_rows - NA_ROWS)
            for kl in range(NA_SPAN):
                if row_lo <= base + kl < row_lo + NA_ROWS:
                    table[form, rl, kl] = base + kl - r + NA_ROWS - 1
    return table


def _na_block(ib, n_rows):
    n_blocks = n_rows // NA_BLOCK
    base = jnp.clip(NA_BLOCK * ib - NA_ROWS // 2, 0, n_rows - NA_SPAN)
    return base, jnp.where(ib == 0, 0, jnp.where(ib == n_blocks - 1, 2, 1))


def _na_expand_bias(rel_bias, *, name):
    l, h, nr, nc = rel_bias.shape
    onehot, valid = _na_onehot()
    rb = jnp.pad(rel_bias, ((0, 0), (0, 0), (0, 1), (0, LANES - nc))).reshape(l * h * (nr + 1), LANES)
    live = jnp.asarray(np.tile(np.arange(nr + 1) < nr, l * h).astype(np.float32)[:, None])

    def body(rb_ref, oh_ref, valid_ref, live_ref, e_ref):
        e = lax.dot_general(rb_ref[...], oh_ref[...], _NT, precision=lax.Precision.HIGHEST, preferred_element_type=F32)
        e_ref[...] = jnp.where((valid_ref[...] > 0) & (live_ref[...] > 0), e, NEG_INF)

    e = pl.pallas_call(
        body, name=name, out_shape=jax.ShapeDtypeStruct((l * h * (nr + 1), GRID_W * GRID_W), F32), compiler_params=_params(),
    )(rb, jnp.asarray(onehot), jnp.asarray(valid.astype(np.float32)), live)
    return e.reshape(l, h, nr + 1, GRID_W, GRID_W)


def _na_collapse_bias(de, *, name):
    b, h = de.shape[:2]
    onehot, _ = _na_onehot()
    rows = h * NA_BIAS_ROWS

    def diag(e_ref, oh_ref, o_ref):
        e = e_ref[0]
        for bi in range(1, b):
            e = e + e_ref[bi]
        o_ref[...] = lax.dot_general(e, oh_ref[...], _NN, precision=lax.Precision.HIGHEST, preferred_element_type=F32)

    drb = pl.pallas_call(
        diag, name=name, out_shape=jax.ShapeDtypeStruct((rows, LANES), F32), compiler_params=_params(),
    )(de.reshape(b, rows, GRID_W * GRID_W), jnp.asarray(onehot))
    return drb[:, :NA_BIAS_COLS].reshape(h, NA_BIAS_ROWS, NA_BIAS_COLS)


def _na_tiles(n_rows):
    table = _na_block_rows(n_rows)
    return [(f, rl, kl, int(table[f, rl, kl])) for f in range(NA_FORMS) for rl in range(NA_BLOCK) for kl in range(NA_SPAN)]


def _na_tile(ref, form, rl, kl):
    return ref.at[form, rl * GRID_W:(rl + 1) * GRID_W, kl * GRID_W:(kl + 1) * GRID_W]


def _na_head_spec(part, first, s):
    return pl.BlockSpec((None, None, s, HEAD_DIM), lambda b, h: (b, first + part * NA_HEADS + h, 0, 0))


def _na_attn_fwd(heads, bias, *, first, name):
    b, _, s, _ = heads.shape
    n_rows = s // GRID_W
    tiles = _na_tiles(n_rows)

    def body(q_ref, k_ref, v_ref, e_ref, ot_ref, o_ref, l_ref, b_ref):
        for form, rl, kl, i in tiles:
            _na_tile(b_ref, form, rl, kl)[...] = e_ref[i]

        def step(ib, carry):
            base, form = _na_block(ib, n_rows)
            rows = pl.ds(pl.multiple_of(ib * NA_Q, NA_Q), NA_Q)
            win = pl.ds(pl.multiple_of(base * GRID_W, GRID_W), NA_KEYS)
            sc = _dot(q_ref[rows, :], k_ref[win, :], _NT) + b_ref[form]
            m = jnp.max(sc, axis=1, keepdims=True)
            p = jnp.exp(sc - m)
            den = jnp.sum(p, axis=1, keepdims=True)
            o_ref[rows, :] = (_dot(p.astype(BF16), v_ref[win, :], _NN) / den).astype(o_ref.dtype)
            l_ref[rows, :] = m + jnp.log(den)
            return carry

        lax.fori_loop(0, n_rows // NA_BLOCK, step, 0, unroll=BLOCKS_IN_FLIGHT)

        @pl.when(pl.program_id(1) % 2 == 0)
        def _():
            ot_ref[:, :HEAD_DIM] = o_ref[...]

        @pl.when(pl.program_id(1) % 2 == 1)
        def _():
            ot_ref[:, HEAD_DIM:] = o_ref[...]

    per_head = lambda bi, h: (bi, h, 0, 0)
    return pl.pallas_call(
        body, name=name, grid=(b, NA_HEADS),
        in_specs=[_na_head_spec(part, first, s) for part in range(3)]
        + [pl.BlockSpec((None, NA_BIAS_ROWS + 1, GRID_W, GRID_W), lambda bi, h: (h, 0, 0, 0))],
        out_specs=[pl.BlockSpec((None, s, 2 * HEAD_DIM), lambda bi, h: (bi, 0, h // 2)),
                   pl.BlockSpec((None, None, s, HEAD_DIM), per_head), pl.BlockSpec((None, None, s, 1), per_head)],
        out_shape=[jax.ShapeDtypeStruct((b, s, NA_HEADS * HEAD_DIM), BF16), jax.ShapeDtypeStruct((b, NA_HEADS, s, HEAD_DIM), BF16),
                   jax.ShapeDtypeStruct((b, NA_HEADS, s, 1), F32)],
        scratch_shapes=[pltpu.VMEM((NA_FORMS, NA_Q, NA_KEYS), F32)],
        compiler_params=_params(dimension_semantics=("parallel", "arbitrary")),
    )(heads, heads, heads, bias)


def _na_attn_bwd(heads, bias, out, lse, dout, *, first, name):
    b, _, s, _ = heads.shape
    n_rows = s // GRID_W
    tiles = _na_tiles(n_rows)

    def body(q_ref, k_ref, v_ref, e_ref, o_ref, l_ref, do_ref, d_ref, de_ref, b_ref, db_ref):
        for form, rl, kl, i in tiles:
            _na_tile(b_ref, form, rl, kl)[...] = e_ref[i]
        d_ref[...] = jnp.zeros_like(d_ref)
        db_ref[...] = jnp.zeros_like(db_ref)

        def step(ib, carry):
            base, form = _na_block(ib, n_rows)
            rows = pl.ds(pl.multiple_of(ib * NA_Q, NA_Q), NA_Q)
            win = pl.ds(pl.multiple_of(base * GRID_W, GRID_W), NA_KEYS)
            q, k, v = q_ref[rows, :], k_ref[win, :], v_ref[win, :]
            do = do_ref[rows, :]
            delta = jnp.sum(do * o_ref[rows, :].astype(F32), axis=1, keepdims=True)
            do_b = do.astype(BF16)
            p = jnp.exp(_dot(q, k, _NT) + b_ref[form] - l_ref[rows, :])
            ds = p * (_dot(do_b, v, _NT) - delta)
            db_ref[form] += ds
            ds_b = ds.astype(BF16)
            d_ref[0, rows, :] = _dot(ds_b, k, _NN)
            d_ref[1, win, :] += _dot(ds_b, q, _TN)
            d_ref[2, win, :] += _dot(p.astype(BF16), do_b, _TN)
            return carry

        lax.fori_loop(0, n_rows // NA_BLOCK, step, 0, unroll=BLOCKS_IN_FLIGHT)
        acc = [None] * NA_BIAS_ROWS
        for form, rl, kl, i in tiles:
            if i < NA_BIAS_ROWS:
                t = _na_tile(db_ref, form, rl, kl)[...]
                acc[i] = t if acc[i] is None else acc[i] + t
        for i in range(NA_BIAS_ROWS):
            de_ref[i] = acc[i]

    per_head = lambda bi, h: (bi, h, 0, 0)
    return pl.pallas_call(
        body, name=name, grid=(b, NA_HEADS),
        in_specs=[_na_head_spec(part, first, s) for part in range(3)]
        + [pl.BlockSpec((None, NA_BIAS_ROWS + 1, GRID_W, GRID_W), lambda bi, h: (h, 0, 0, 0)),
           pl.BlockSpec((None, None, s, HEAD_DIM), per_head), pl.BlockSpec((None, None, s, 1), per_head),
           pl.BlockSpec((None, None, s, HEAD_DIM), per_head)],
        out_specs=[pl.BlockSpec((None, None, 3, s, HEAD_DIM), lambda bi, h: (bi, h, 0, 0, 0)),
                   pl.BlockSpec((None, None, NA_BIAS_ROWS, GRID_W, GRID_W), lambda bi, h: (bi, h, 0, 0, 0))],
        out_shape=[jax.ShapeDtypeStruct((b, NA_HEADS, 3, s, HEAD_DIM), F32),
                   jax.ShapeDtypeStruct((b, NA_HEADS, NA_BIAS_ROWS, GRID_W, GRID_W), F32)],
        scratch_shapes=[pltpu.VMEM((NA_FORMS, NA_Q, NA_KEYS), F32), pltpu.VMEM((NA_FORMS, NA_Q, NA_KEYS), F32)],
        compiler_params=_params(dimension_semantics=("parallel", "parallel")),
    )(heads, heads, heads, bias, out, lse, dout)


GATE_TILE = 256


def _gate_fwd(proj, z, *, gate_col, tt, name):
    _, t, d = z.shape
    nj = d // GATE_TILE
    c0 = gate_col // GATE_TILE

    def body(ga_ref, gb_ref, za_ref, zb_ref, o_ref):
        o_ref[...] = (jax.nn.sigmoid(ga_ref[...]) * za_ref[...] + jax.nn.sigmoid(gb_ref[...]) * zb_ref[...]).astype(BF16)

    return pl.pallas_call(
        body, name=name, grid=(t // tt, nj),
        in_specs=[pl.BlockSpec((tt, GATE_TILE), lambda i, j: (i, c0 + j)),
                  pl.BlockSpec((tt, GATE_TILE), lambda i, j: (i, c0 + nj + j)),
                  pl.BlockSpec((None, tt, GATE_TILE), lambda i, j: (0, i, j)),
                  pl.BlockSpec((None, tt, GATE_TILE), lambda i, j: (1, i, j))],
        out_specs=pl.BlockSpec((tt, GATE_TILE), lambda i, j: (i, j)), out_shape=jax.ShapeDtypeStruct((t, d), BF16),
        compiler_params=_params(dimension_semantics=("parallel", "parallel")),
    )(proj, proj, z, z)


def _gate_bwd(dm, proj, z, *, gate_col, tt, name):
    _, t, d = z.shape
    nj = d // GATE_TILE
    c0 = gate_col // GATE_TILE

    def body(dm_ref, g_ref, z_ref, dz_ref, dg_ref):
        dmv = dm_ref[...]
        sg = jax.nn.sigmoid(g_ref[...])
        dz_ref[...] = (dmv * sg).astype(BF16)
        dg_ref[...] = (dmv * z_ref[...] * sg * (1.0 - sg)).astype(BF16)

    return pl.pallas_call(
        body, name=name, grid=(t // tt, 2 * nj),
        in_specs=[pl.BlockSpec((tt, GATE_TILE), lambda i, j: (i, j % nj)),
                  pl.BlockSpec((tt, GATE_TILE), lambda i, j: (i, c0 + j)),
                  pl.BlockSpec((None, tt, GATE_TILE), lambda i, j: (j // nj, i, j % nj))],
        out_specs=[pl.BlockSpec((None, tt, GATE_TILE), lambda i, j: (j // nj, i, j % nj)),
                   pl.BlockSpec((tt, GATE_TILE), lambda i, j: (i, c0 + j))],
        out_shape=[jax.ShapeDtypeStruct((2, t, d), BF16), jax.ShapeDtypeStruct(proj.shape, BF16)],
        compiler_params=_params(dimension_semantics=("parallel", "parallel")),
    )(dm, proj, z)


def _adamw(w, g, m, v, *, name):
    shape = w.shape
    if w.ndim == 3:
        w2, g2, m2, v2 = w, g, m, v
    else:
        w2, g2, m2, v2 = (t.reshape(1, -1, shape[-1]) for t in (w, g, m, v))
    lead, rows, cols = w2.shape
    tr = rows
    for cand in (512, 256, 128, 64, 32, 16, 8):
        if rows % cand == 0:
            tr = cand
            break

    def body(w_ref, g_ref, m_ref, v_ref, d_ref, nm_ref, nv_ref):
        gv = g_ref[...]
        nm = ADAM_B1 * m_ref[...] + (1.0 - ADAM_B1) * gv
        nv = ADAM_B2 * v_ref[...] + (1.0 - ADAM_B2) * (gv * gv)
        m_hat = nm / (1.0 - ADAM_B1 ** ADAM_STEP)
        v_hat = nv / (1.0 - ADAM_B2 ** ADAM_STEP)
        d_ref[...] = -ADAM_LR * (m_hat / (jnp.sqrt(v_hat) + ADAM_EPS) + ADAM_WD * w_ref[...])
        nm_ref[...] = nm
        nv_ref[...] = nv

    blk = pl.BlockSpec((None, tr, cols), lambda l, i: (l, i, 0))
    out = jax.ShapeDtypeStruct((lead, rows, cols), F32)
    res = pl.pallas_call(
        body, name=name, grid=(lead, rows // tr), in_specs=[blk] * 4, out_specs=[blk] * 3, out_shape=[out] * 3,
        compiler_params=_params(dimension_semantics=("parallel", "parallel")),
    )(w2, g2, m2, v2)
    return tuple(t.reshape(shape) for t in res)


def _my_place():
    return lax.axis_index("x"), lax.axis_index("y"), lax.axis_index("c")


def _other_chips(x, y):
    return [(1 - x, y), (x, 1 - y), (1 - x, 1 - y)]


def _chip_no(chip):
    return 2 * chip[0] + chip[1]


def _window(ref, kind, size, chip, lead):
    if kind == "col":
        return ref.at[(*lead, slice(None), pl.ds(pl.multiple_of(chip * size, LANES), size))]
    if kind == "row":
        return ref.at[(*lead, pl.ds(pl.multiple_of(chip * size, BF16_ROWS), size), slice(None))]
    shard = size + HEAD_DIM
    if kind == "win_main":
        return ref.at[(*lead, slice(None), pl.ds(pl.multiple_of(chip * shard + HEAD_DIM * (chip % 2), LANES), size))]
    assert kind == "win_strad"
    return ref.at[(*lead, slice(None), pl.ds(pl.multiple_of(size + 2 * shard * (chip // 2), LANES), LANES))]


def _full_shape(shard, kind):
    _, k, n = shard.shape
    return {"col": (k, N_CHIPS * n), "row": (N_CHIPS * k, n), "win_main": (k, N_CHIPS * (n + HEAD_DIM)),
            "slot": (N_CHIPS, k, n)}[kind]


def _place_own(shard, kind, layer, *, name):
    _, k, n = shard.shape
    tr = _div_tile(k, 512, BF16_ROWS)
    tc = LANES if kind == "win_main" else n
    mine = 2 * lax.axis_index("x") + lax.axis_index("y")
    row0 = mine * (k // tr) if kind == "row" else 0
    col0 = {"col": mine, "row": 0, "slot": 0, "win_main": (mine * (n + HEAD_DIM) + HEAD_DIM * (mine % 2)) // LANES}[kind]
    scalars = jnp.stack([mine, row0, col0]).astype(jnp.int32)

    def body(s_ref, i_ref, o_ref):
        o_ref[...] = i_ref[...]

    if kind == "slot":
        o_spec = pl.BlockSpec((None, tr, tc), lambda i, j, s: (s[0], i, j))
    else:
        o_spec = pl.BlockSpec((tr, tc), lambda i, j, s: (s[1] + i, s[2] + j))
    return pl.pallas_call(
        body, name=name,
        grid_spec=pltpu.PrefetchScalarGridSpec(
            num_scalar_prefetch=1, grid=(k // tr, n // tc),
            in_specs=[pl.BlockSpec((None, tr, tc), lambda i, j, s: (layer, i, j))], out_specs=o_spec),
        out_shape=jax.ShapeDtypeStruct(_full_shape(shard, kind), shard.dtype),
        compiler_params=_params(dimension_semantics=("parallel", "parallel")),
    )(scalars, shard)


class _GatherPlan:
    def __init__(self, src, dst, shapes, kinds, layer, send_sems, recv_sems):
        self.src, self.dst, self.shapes, self.kinds, self.layer = src, dst, shapes, kinds, layer
        self.send_sems, self.recv_sems = send_sems, recv_sems
        self.x, self.y, self.c = _my_place()
        self.mine = 2 * self.x + self.y
        self.chips = _other_chips(self.x, self.y)
        self.n = len(src)

    def half(self, i, chip, half):
        _, k, n = self.shapes[i]
        kind, dst, hk = self.kinds[i], self.dst[i], k // 2
        if kind == "slot":
            return dst.at[chip, pl.ds(pl.multiple_of(half * hk, BF16_ROWS), hk), :]
        if kind == "row":
            return dst.at[pl.ds(pl.multiple_of(chip * k + half * hk, BF16_ROWS), hk), :]
        col0 = chip * n if kind == "col" else chip * (n + HEAD_DIM) + HEAD_DIM * (chip % 2)
        return dst.at[pl.ds(pl.multiple_of(half * hk, BF16_ROWS), hk), pl.ds(pl.multiple_of(col0, LANES), n)]

    def _copy(self, sem, window, to, source=None):
        return pltpu.make_async_remote_copy(src_ref=window if source is None else source, dst_ref=window,
                                            send_sem=self.send_sems.at[sem], recv_sem=self.recv_sems.at[sem],
                                            device_id=to, device_id_type=MESH)

    def sends(self):
        out = []
        for k, chip in enumerate(self.chips):
            for i in range(self.n):
                hk = self.shapes[i][1] // 2
                mine = self.src[i].at[self.layer, pl.ds(pl.multiple_of(self.c * hk, BF16_ROWS), hk), :]
                out.append(self._copy(3 * i + k, self.half(i, self.mine, self.c), (*chip, self.c), source=mine))
        return out

    def arrivals(self):
        return [self._copy(3 * i + k, self.half(i, _chip_no(chip), self.c), (*chip, self.c))
                for k, chip in enumerate(self.chips) for i in range(self.n)]

    def forwards(self, first_sem):
        sibling = (self.x, self.y, 1 - self.c)
        return [self._copy(first_sem + 3 * i + k, self.half(i, _chip_no(chip), self.c), sibling)
                for k, chip in enumerate(self.chips) for i in range(self.n)]

    def forwarded(self, first_sem):
        sibling = (self.x, self.y, 1 - self.c)
        return [self._copy(first_sem + 3 * i + k, self.half(i, _chip_no(chip), 1 - self.c), sibling)
                for k, chip in enumerate(self.chips) for i in range(self.n)]


IN_HBM = pl.BlockSpec(memory_space=pltpu.HBM)
IN_SEM = pl.BlockSpec(memory_space=pltpu.SEMAPHORE)
DATAFLOW = pltpu.SideEffectType.DATAFLOW_SIDE_EFFECTING


def _gather_layer_start(shards, kinds, fulls, layer, after, *, name):
    n_w = len(shards)
    shapes = [sh.shape for sh in shards]

    def body(*refs):
        plan = _GatherPlan(refs[:n_w], refs[n_w:2 * n_w], shapes, kinds, layer, refs[2 * n_w + 1], refs[2 * n_w + 2])
        for cp in plan.sends():
            cp.start()
        token = refs[-1]
        token[...] = jnp.zeros_like(token)

    operands = [pltpu.with_memory_space_constraint(a, pltpu.HBM) for a in (*shards, *fulls)]
    res = pl.pallas_call(
        body, name=name, in_specs=[IN_HBM] * (2 * n_w) + [pl.BlockSpec(memory_space=pl.ANY)],
        out_specs=(IN_SEM, IN_SEM, *([IN_HBM] * (2 * n_w)), pl.BlockSpec(memory_space=pltpu.VMEM)),
        out_shape=(pltpu.SemaphoreType.DMA((3 * n_w,)), pltpu.SemaphoreType.DMA((3 * n_w,)),
                   *[pltpu.HBM(a.shape, a.dtype) for a in operands], jax.ShapeDtypeStruct((8, LANES), F32)),
        input_output_aliases={i: 2 + i for i in range(2 * n_w)},
        compiler_params=pltpu.CompilerParams(has_side_effects=DATAFLOW),
    )(*operands, after)
    return res[0], res[1], res[2:2 + n_w], res[2 + n_w:2 + 2 * n_w], res[-1]


def _gather_layer_wait(send_sems, recv_sems, shards, fulls, kinds, layer, after, *, name):
    n_w = len(shards)
    shapes = [sh.shape for sh in shards]

    def body(*refs):
        plan = _GatherPlan(refs[:n_w], refs[n_w:2 * n_w], shapes, kinds, layer, refs[2 * n_w], refs[2 * n_w + 1])
        for cp in plan.sends():
            cp.wait_send()
        for cp in plan.arrivals():
            cp.wait_recv()

    res = pl.pallas_call(
        body, name=name, in_specs=[IN_HBM] * (2 * n_w) + [IN_SEM, IN_SEM, pl.BlockSpec(memory_space=pl.ANY)],
        out_specs=[IN_HBM] * (2 * n_w), out_shape=[pltpu.HBM(a.shape, a.dtype) for a in (*shards, *fulls)],
        input_output_aliases={i: i for i in range(2 * n_w)},
        compiler_params=pltpu.CompilerParams(has_side_effects=DATAFLOW),
    )(*shards, *fulls, send_sems, recv_sems, after)
    return res[n_w:]


def _gather_layer_forward(shapes, kinds, fulls, *, name):
    n_w = len(fulls)

    def body(*refs):
        plan = _GatherPlan([None] * n_w, refs[n_w:2 * n_w], shapes, kinds, 0, *refs[2 * n_w:])
        passed = plan.forwards(0)
        for cp in passed:
            cp.start()
        for cp in plan.forwarded(0):
            cp.wait_recv()
        for cp in passed:
            cp.wait_send()

    return pl.pallas_call(
        body, name=name, in_specs=[HBM] * n_w, out_specs=[HBM] * n_w,
        out_shape=[jax.ShapeDtypeStruct(f.shape, f.dtype) for f in fulls],
        input_output_aliases={i: i for i in range(n_w)},
        scratch_shapes=[pltpu.SemaphoreType.DMA((3 * n_w,)), pltpu.SemaphoreType.DMA((3 * n_w,))],
    )(*fulls)


def _on_core(layer):
    return (lax.axis_index("c") == layer).astype(jnp.int32).reshape(1)


N_DEVICES = 2 * N_CHIPS


class _ScatterPlan:
    def __init__(self, src, dst, kinds, sizes, layer, send_sems, recv_sems):
        self.src, self.dst, self.kinds, self.sizes, self.layer = src, dst, kinds, sizes, layer
        self.send_sems, self.recv_sems = send_sems, recv_sems
        self.x, self.y, self.c = _my_place()
        self.mine = 2 * self.x + self.y
        self.chips = _other_chips(self.x, self.y)
        self.n = len(src)

    def _copy(self, i, k, window_of, from_chip, from_core, to):
        return pltpu.make_async_remote_copy(src_ref=_window(self.src[i], self.kinds[i], self.sizes[i], window_of, ()),
                                            dst_ref=self.dst[i].at[2 * from_chip + from_core],
                                            send_sem=self.send_sems.at[4 * i + k],
                                            recv_sem=self.recv_sems.at[2 * (4 * i + k) + from_core],
                                            device_id=to, device_id_type=MESH)

    def to_chips(self):
        return [self._copy(i, k, _chip_no(chip), self.mine, self.c, (*chip, self.layer))
                for k, chip in enumerate(self.chips) for i in range(self.n)]

    def to_sibling(self):
        return [self._copy(i, 3, self.mine, self.mine, self.c, (self.x, self.y, self.layer)) for i in range(self.n)]

    def arrivals(self):
        out = [self._copy(i, k, self.mine, _chip_no(chip), core, (*chip, core))
               for k, chip in enumerate(self.chips) for core in (0, 1) for i in range(self.n)]
        return out + [self._copy(i, 3, self.mine, self.mine, 1 - self.layer, (self.x, self.y, 1 - self.layer))
                      for i in range(self.n)]


def _slab_shape(p, kind, size):
    return (N_DEVICES,) + {"col": (p.shape[0], size), "row": (size, p.shape[1]), "win_main": (p.shape[0], size),
                           "win_strad": (p.shape[0], LANES)}[kind]


def _grads_to_chips_start(pairs, kinds, sizes, layer, *, name):
    n_w = len(pairs)

    def body(*refs):
        plan = _ScatterPlan(refs[:n_w], refs[n_w:2 * n_w], kinds, sizes, layer, refs[2 * n_w], refs[2 * n_w + 1])
        for cp in plan.to_chips():
            cp.start()

        @pl.when(plan.c != layer)
        def _():
            for cp in plan.to_sibling():
                cp.start()

        token = refs[-1]
        token[...] = jnp.zeros_like(token)

    slabs = [lax.empty(_slab_shape(p, kind, size), p.dtype) for p, kind, size in zip(pairs, kinds, sizes)]
    operands = [pltpu.with_memory_space_constraint(a, pltpu.HBM) for a in (*pairs, *slabs)]
    res = pl.pallas_call(
        body, name=name, in_specs=[IN_HBM] * (2 * n_w),
        out_specs=(IN_SEM, IN_SEM, *([IN_HBM] * (2 * n_w)), pl.BlockSpec(memory_space=pltpu.VMEM)),
        out_shape=(pltpu.SemaphoreType.DMA((4 * n_w,)), pltpu.SemaphoreType.DMA((8 * n_w,)),
                   *[pltpu.HBM(a.shape, a.dtype) for a in operands], jax.ShapeDtypeStruct((8, LANES), F32)),
        input_output_aliases={i: 2 + i for i in range(2 * n_w)},
        compiler_params=pltpu.CompilerParams(has_side_effects=DATAFLOW),
    )(*operands)
    return res[0], res[1], res[2:2 + n_w], res[2 + n_w:2 + 2 * n_w], res[-1]


def _grads_to_chips_wait(send_sems, recv_sems, pairs, slabs, kinds, sizes, layer, after, *, name):
    n_w = len(pairs)

    def body(*refs):
        plan = _ScatterPlan(refs[:n_w], refs[n_w:2 * n_w], kinds, sizes, layer, refs[2 * n_w], refs[2 * n_w + 1])
        for cp in plan.to_chips():
            cp.wait_send()

        @pl.when(plan.c != layer)
        def _():
            for cp in plan.to_sibling():
                cp.wait_send()

        @pl.when(plan.c == layer)
        def _():
            for cp in plan.arrivals():
                cp.wait_recv()

    res = pl.pallas_call(
        body, name=name, in_specs=[IN_HBM] * (2 * n_w) + [IN_SEM, IN_SEM, pl.BlockSpec(memory_space=pl.ANY)],
        out_specs=[IN_HBM] * (2 * n_w), out_shape=[pltpu.HBM(a.shape, a.dtype) for a in (*pairs, *slabs)],
        input_output_aliases={i: i for i in range(2 * n_w)},
        compiler_params=pltpu.CompilerParams(has_side_effects=DATAFLOW),
    )(*pairs, *slabs, send_sems, recv_sems, after)
    return res[:n_w], res[n_w:]


def _sum_slabs(slabs, pair, kind, size, layer, into, *, name):
    n_s, k, n = slabs.shape
    tr = _div_tile(k, 512, BF16_ROWS)
    tc = n if kind in ("col", "row") else LANES
    x, y, _ = _my_place()
    mine = 2 * x + y
    shard = size + HEAD_DIM
    row0 = mine * (k // tr) if kind == "row" else 0
    col0 = {"col": mine, "row": 0, "win_main": (mine * shard + HEAD_DIM * (mine % 2)) // LANES,
            "win_strad": (size + 2 * shard * (mine // 2)) // LANES}[kind]
    on = _on_core(layer)[0]
    scalars = jnp.stack([2 * mine + layer, row0 * on, col0 * on, on]).astype(jnp.int32)

    def body(s_ref, slab_ref, own_ref, *rest):
        o_ref = rest[-1]
        me = s_ref[0]

        @pl.when(s_ref[3] == 1)
        def _():
            acc = jnp.zeros(o_ref.shape, F32)
            for i in range(n_s):
                acc = acc + jnp.where(me == i, own_ref[...], slab_ref[i]).astype(F32)
            o_ref[...] = acc

    operands = [scalars, slabs, pair] + ([] if into is None else [into])
    return pl.pallas_call(
        body, name=name,
        grid_spec=pltpu.PrefetchScalarGridSpec(
            num_scalar_prefetch=1, grid=(k // tr, n // tc),
            in_specs=[pl.BlockSpec((n_s, tr, tc), lambda i, j, s: (0, i * s[3], j * s[3])),
                      pl.BlockSpec((tr, tc), lambda i, j, s: (s[1] + i * s[3], s[2] + j * s[3]))]
            + ([] if into is None else [HBM]),
            out_specs=pl.BlockSpec((None, tr, tc), lambda i, j, s: (layer, i * s[3], j * s[3]))),
        out_shape=jax.ShapeDtypeStruct((2, k, n), F32),
        input_output_aliases={} if into is None else {3: 0},
        compiler_params=_params(dimension_semantics=("arbitrary", "arbitrary")),
    )(*operands)


def _exchange_layers(bufs, *, name):
    n_w = len(bufs)

    def body(*refs):
        dst = refs[n_w:2 * n_w]
        send_sems, recv_sems = refs[2 * n_w:]
        x, y, c = _my_place()

        def copy(i, layer):
            return pltpu.make_async_remote_copy(src_ref=dst[i].at[layer], dst_ref=dst[i].at[layer], send_sem=send_sems.at[i],
                                                recv_sem=recv_sems.at[i], device_id=(x, y, 1 - c), device_id_type=MESH)

        sends = [copy(i, c) for i in range(n_w)]
        for cp in sends:
            cp.start()
        for i in range(n_w):
            copy(i, 1 - c).wait_recv()
        for cp in sends:
            cp.wait_send()

    return pl.pallas_call(
        body, name=name, in_specs=[HBM] * n_w, out_specs=[HBM] * n_w,
        out_shape=[jax.ShapeDtypeStruct(b.shape, b.dtype) for b in bufs],
        input_output_aliases={i: i for i in range(n_w)},
        scratch_shapes=[pltpu.SemaphoreType.DMA((n_w,)), pltpu.SemaphoreType.DMA((n_w,))],
    )(*bufs)


def _all_sum_small(v, *, name):
    r = v.shape[0]
    relations = [(dx, dy, dc) for dx in (0, 1) for dy in (0, 1) for dc in (0, 1)][1:]

    def body(v_ref, o_ref, buf, send_sems, recv_sems):
        x, y, c = _my_place()
        me = 4 * x + 2 * y + c
        buf[me] = v_ref[...]
        peers = [(x + dx - 2 * x * dx, y + dy - 2 * y * dy, c + dc - 2 * c * dc) for dx, dy, dc in relations]

        def copy(k, slot):
            return pltpu.make_async_remote_copy(src_ref=v_ref, dst_ref=buf.at[slot], send_sem=send_sems.at[k],
                                                recv_sem=recv_sems.at[k], device_id=peers[k], device_id_type=MESH)

        sends = [copy(k, me) for k in range(len(relations))]
        for cp in sends:
            cp.start()
        for k, (px, py, pc) in enumerate(peers):
            copy(k, 4 * px + 2 * py + pc).wait_recv()
        for cp in sends:
            cp.wait_send()
        acc = buf[0]
        for i in range(1, 8):
            acc = acc + buf[i]
        o_ref[...] = acc

    vm = pl.BlockSpec(memory_space=pltpu.VMEM)
    return pl.pallas_call(
        body, name=name, in_specs=[vm], out_specs=vm, out_shape=jax.ShapeDtypeStruct((r, LANES), F32),
        scratch_shapes=[pltpu.VMEM((8, r, LANES), F32), pltpu.SemaphoreType.DMA((7,)), pltpu.SemaphoreType.DMA((7,))],
    )(v)


SHARDED = (("ffn1_w_up", "col"), ("ffn1_w_down", "row"), ("w_in", "win"), ("w_branch_a", "col"),
           ("w_branch_b", "col"), ("w_out", "row"), ("ffn2_w_up", "col"), ("ffn2_w_down", "row"))
REPLICATED = ("ffn1_norm", "mix_norm", "na_rel_bias", "ffn2_norm", "final_norm")


def _weight_pieces(w):
    even = lax.axis_index("y") == 0
    shards, kinds, names = [], [], []
    for name, kind in SHARDED:
        wb = w[name].astype(BF16)
        if kind == "win":
            main = wb.shape[-1] - HEAD_DIM
            assert main % LANES == 0
            zeros = jnp.zeros(wb.shape[:-1] + (HEAD_DIM,), BF16)
            shards += [jnp.where(even, wb[..., :main], wb[..., HEAD_DIM:]),
                       jnp.where(even, jnp.concatenate([wb[..., main:], zeros], -1),
                                 jnp.concatenate([zeros, wb[..., :HEAD_DIM]], -1))]
            kinds += ["win_main", "slot"]
            names += [name, name + "_strad"]
        else:
            shards.append(wb)
            kinds.append(kind)
            names.append(name)
    return names, kinds, shards


def _finish_w_in(full):
    full = dict(full)
    strad = full.pop("w_in_strad")
    main = full["w_in"].shape[1] // N_CHIPS - HEAD_DIM
    for i in range(N_CHIPS // 2):
        lo = main + 2 * (main + HEAD_DIM) * i
        full["w_in"] = full["w_in"].at[:, lo:lo + LANES].set(strad[2 * i] + strad[2 * i + 1])
    return full


def _scatter_pieces(shards):
    names, kinds, sizes, srcs = [], [], [], []
    for name, kind in SHARDED:
        shp = shards[name].shape
        if kind == "win":
            names += [name, name + "_strad"]
            kinds += ["win_main", "win_strad"]
            sizes += [shp[2] - HEAD_DIM] * 2
            srcs += [name, name]
        else:
            names.append(name)
            kinds.append(kind)
            sizes.append(shp[1] if kind == "row" else shp[2])
            srcs.append(name)
    return names, kinds, sizes, srcs


def _finish_weight_grads(reduced, names, tag):
    out = dict(zip(names, _exchange_layers(reduced, name=f"{tag}_layers")))
    if "w_in_strad" in out:
        strad = out.pop("w_in_strad")
        even = lax.axis_index("y") == 0
        out["w_in"] = jnp.where(even, jnp.concatenate([out["w_in"], strad[..., :HEAD_DIM]], -1),
                                jnp.concatenate([strad[..., HEAD_DIM:], out["w_in"]], -1))
    return out


class _Grads:
    def __init__(self):
        self.arrays = {}

    def put(self, weight, layer, a, b, *, cols=None, col_off=0, **kw):
        self.arrays[weight, layer] = _mm(a, b, mode="tn", out_dtype=BF16, out_cols=cols, out_col_off=col_off,
                                         out_into=self.arrays.get((weight, layer)), **kw)


def _ffn_fwd(x, h, w_up, w_down, tag):
    t, d = x.shape
    f = w_down.shape[0]
    a, gate, up = _mm_swiglu_fwd(h, w_up, tm=_div_tile(t, ROWS_NARROW, 8), tn=MXU_N, name=f"{tag}_up")
    x_out = _mm(a, w_down, mode="nn", out_dtype=F32, tm=_div_tile(t, ROWS_WIDE, 8), tn=d, tk=f, alpha=0.5, res=x, name=f"{tag}_down")
    return x_out, (x, h, a, gate, up)


def _ffn_bwd(dx, dxb, saved, norm_g, w_up, w_down, layer, grads, wname, tag, scatter):
    x, h, a, gate, up = saved
    t, d = x.shape
    f = w_down.shape[0]
    tn = _div_tile(f, 1408)
    grads.put(f"{wname}_w_down", layer, a, dxb, tm=tn, tn=d, tk=ROWS_CONTRACTED, alpha=0.5, name=f"{tag}_dwd")
    d_gate, d_up = _mm_swiglu_bwd(dxb, w_down, gate, up, alpha=0.5, tm=_div_tile(t, ROWS_NARROW, 8), tn=MXU_N, name=f"{tag}_da")
    grads.put(f"{wname}_w_up", layer, h, d_gate, cols=2 * f, tm=d, tn=tn, tk=ROWS_CONTRACTED, name=f"{tag}_dwg")
    grads.put(f"{wname}_w_up", layer, h, d_up, cols=2 * f, col_off=f // tn, tm=d, tn=tn, tk=ROWS_CONTRACTED, name=f"{tag}_dwu")
    started = scatter(layer, [f"{wname}_w_up", f"{wname}_w_down"])
    dh = _mm(d_gate, w_up, mode="nt", out_dtype=F32, tm=_div_tile(t, ROWS_WIDE, 8), tn=d, tk=f, name=f"{tag}_dh1")
    dh = _mm(d_up, w_up, mode="nt", out_dtype=F32, tm=_div_tile(t, ROWS_WIDE, 8), tn=d, tk=f, b_k_off=1, res=dh, name=f"{tag}_dh2")
    return _rms_bwd(dh, x, norm_g + started, dx, tt=NORM_ROWS, name=f"{tag}_dnorm")


def _to_heads(y, b, n_heads):
    t, w = y.shape
    return y.reshape(b, t // b, n_heads, HEAD_DIM).transpose(0, 2, 1, 3)


N_QKV = 3 * (DIL_HEADS + NA_HEADS) * HEAD_DIM


def _mixer_fwd(x, b, norm_g, full, bias, tabs, tag):
    t, d = x.shape
    s = t // b
    n_in = full["w_in"].shape[1]
    h = _rms_fwd(x, norm_g, tt=NORM_ROWS, name=f"{tag}_norm")
    proj = _mm(h, full["w_in"], mode="nn", out_dtype=F32, tm=_div_tile(t, ROWS_NARROW, 8), tn=MXU_N, tk=d, name=f"{tag}_in")
    heads = _split_heads(proj.reshape(b, s, -1), *tabs, n_pairs=N_QKV // LANES, rot_pairs=DIL_HEADS,
                         scale_ranges=((0, DIL_HEADS // 2), (3 * DIL_HEADS // 2, (3 * DIL_HEADS + NA_HEADS) // 2)),
                         name=f"{tag}_heads")
    ya, ya_heads, lse_a = _dil_attn_fwd(heads, name=f"{tag}_dil")
    yb_tokens, yb, lse_b = _na_attn_fwd(heads, bias, first=3 * DIL_HEADS, name=f"{tag}_na")
    ya2, yb2 = ya.reshape(t, -1), yb_tokens.reshape(t, -1)
    z = _mm(ya2, full["w_branch_a"], mode="nn", out_dtype=F32, tm=_div_tile(t, ROWS_NARROW, 8), tn=MXU_N, tk=ya2.shape[1],
            out_slab=(0, 2), name=f"{tag}_za")
    z = _mm(yb2, full["w_branch_b"], mode="nn", out_dtype=F32, tm=_div_tile(t, ROWS_NARROW, 8), tn=MXU_N, tk=yb2.shape[1],
            out_slab=(1, 2), out_into=z, name=f"{tag}_zb")
    merged = _gate_fwd(proj, z, gate_col=N_QKV, tt=GATE_ROWS, name=f"{tag}_gate")
    x_out = _mm(merged, full["w_out"], mode="nn", out_dtype=F32, tm=_div_tile(t, ROWS_NARROW, 8), tn=MXU_N, tk=d, res=x, name=f"{tag}_out")
    return x_out, (x, h, proj, heads, ya_heads, lse_a, yb, lse_b, ya2, yb2, z, merged)


def _mixer_bwd(dx, dob, b, saved, norm_g, full, layer, bias, tabs, grads, tag, scatter):
    x, h, proj, heads, ya, lse_a, yb, lse_b, ya2, yb2, z, merged = saved
    t, d = x.shape
    s = t // b
    n_in = full["w_in"].shape[1]
    grads.put("w_out", layer, merged, dob, tm=d, tn=d, tk=ROWS_CONTRACTED, name=f"{tag}_dwo")
    dm = _mm(dob, full["w_out"], mode="nt", out_dtype=F32, tm=_div_tile(t, ROWS_NARROW, 8), tn=MXU_N, tk=d, name=f"{tag}_dm")
    dz, dproj = _gate_bwd(dm, proj, z, gate_col=N_QKV, tt=GATE_ROWS, name=f"{tag}_dgate")
    grads.put("w_branch_a", layer, ya2, dz, b_sel=0, tm=ya2.shape[1], tn=d, tk=ROWS_CONTRACTED, name=f"{tag}_dwa")
    grads.put("w_branch_b", layer, yb2, dz, b_sel=1, tm=yb2.shape[1], tn=d, tk=ROWS_CONTRACTED, name=f"{tag}_dwb")
    started = scatter(layer, ["w_out", "w_branch_a", "w_branch_b"])
    dya = _mm(dz, full["w_branch_a"], mode="nt", out_dtype=F32, tm=_div_tile(t, ROWS_NARROW, 8), tn=MXU_N, tk=d, a_sel=0, name=f"{tag}_dya")
    dyb = _mm(dz, full["w_branch_b"], mode="nt", out_dtype=F32, tm=_div_tile(t, ROWS_NARROW, 8), tn=MXU_N, tk=d, a_sel=1, name=f"{tag}_dyb")
    d_dil = _dil_attn_bwd(heads, ya, lse_a, _to_heads(dya, b, DIL_GROUP_HEADS), name=f"{tag}_ddil")
    d_na, d_bias = _na_attn_bwd(heads, bias, yb, lse_b, _to_heads(dyb, b, NA_HEADS), first=3 * DIL_HEADS, name=f"{tag}_dna")
    dproj = _merge_heads(d_dil, *tabs, heads_per_row=DIL_GROUP_HEADS, rot_pairs=DIL_HEADS, scale_pairs=DIL_HEADS // 2,
                         dilated=True, out_cols=n_in, tile_off=0, into=dproj.reshape(b, s, n_in), name=f"{tag}_dheads_a")
    dproj = _merge_heads(d_na, *tabs, heads_per_row=NA_HEADS, rot_pairs=0, scale_pairs=NA_HEADS // 2, dilated=False,
                         out_cols=n_in, tile_off=3 * DIL_HEADS // 2, into=dproj, name=f"{tag}_dheads_b").reshape(t, n_in)
    grads.put("w_in", layer, h, dproj, tm=_div_tile(d, 512), tn=_div_tile(n_in, 2944), tk=ROWS_CONTRACTED // 2, name=f"{tag}_dwin")
    started = started + scatter(layer, ["w_in"])
    dh = _mm(dproj, full["w_in"], mode="nt", out_dtype=F32, tm=_div_tile(t, 2 * ROWS_WIDE, 8), tn=d, tk=_div_tile(n_in, 2944),
             name=f"{tag}_dh")
    dx_in, dxb_in, d_norm = _rms_bwd(dh, x, norm_g + started, dx, tt=NORM_ROWS, name=f"{tag}_dnorm")
    d_rb = _na_collapse_bias(d_bias, name=f"{tag}_dbias")
    return dx_in, dxb_in, d_norm, d_rb


def kernel(x, ffn1_norm, ffn1_w_up, ffn1_w_down, mix_norm, w_in, na_rel_bias, w_branch_a, w_branch_b, w_out, ffn2_norm, ffn2_w_up, ffn2_w_down, final_norm, loss_target, m_ffn1_norm, m_ffn1_w_up, m_ffn1_w_down, m_mix_norm, m_w_in, m_na_rel_bias, m_w_branch_a, m_w_branch_b, m_w_out, m_ffn2_norm, m_ffn2_w_up, m_ffn2_w_down, m_final_norm, v_ffn1_norm, v_ffn1_w_up, v_ffn1_w_down, v_mix_norm, v_w_in, v_na_rel_bias, v_w_branch_a, v_w_branch_b, v_w_out, v_ffn2_norm, v_ffn2_w_up, v_ffn2_w_down, v_final_norm):
    w = dict(ffn1_norm=ffn1_norm, ffn1_w_up=ffn1_w_up, ffn1_w_down=ffn1_w_down, mix_norm=mix_norm, w_in=w_in,
             na_rel_bias=na_rel_bias, w_branch_a=w_branch_a, w_branch_b=w_branch_b, w_out=w_out, ffn2_norm=ffn2_norm,
             ffn2_w_up=ffn2_w_up, ffn2_w_down=ffn2_w_down, final_norm=final_norm)
    mom = dict(ffn1_norm=m_ffn1_norm, ffn1_w_up=m_ffn1_w_up, ffn1_w_down=m_ffn1_w_down, mix_norm=m_mix_norm, w_in=m_w_in,
               na_rel_bias=m_na_rel_bias, w_branch_a=m_w_branch_a, w_branch_b=m_w_branch_b, w_out=m_w_out,
               ffn2_norm=m_ffn2_norm, ffn2_w_up=m_ffn2_w_up, ffn2_w_down=m_ffn2_w_down, final_norm=m_final_norm)
    var = dict(ffn1_norm=v_ffn1_norm, ffn1_w_up=v_ffn1_w_up, ffn1_w_down=v_ffn1_w_down, mix_norm=v_mix_norm, w_in=v_w_in,
               na_rel_bias=v_na_rel_bias, w_branch_a=v_w_branch_a, w_branch_b=v_w_branch_b, w_out=v_w_out,
               ffn2_norm=v_ffn2_norm, ffn2_w_up=v_ffn2_w_up, ffn2_w_down=v_ffn2_w_down, final_norm=v_final_norm)
    b, s, d = x.shape
    t = b * s
    depth = ffn1_norm.shape[0]
    assert depth == 2, "core c of a chip sends / reduces layer c"
    shards = {name: w[name] for name, _ in SHARDED}

    names, kinds, pieces = _weight_pieces(w)
    by_layer = [[p[l:l + 1] for p in pieces] for l in range(depth)]
    own = [[_place_own(p, kind, 0, name=f"own{l}_{nm}") for nm, kind, p in zip(names, kinds, by_layer[l])] for l in range(depth)]
    full = [{}, {}]

    def gather_start(layer, group, after, tag):
        idx = [i for i, nm in enumerate(names) if nm in group]
        pick = lambda seq: [seq[i] for i in idx]
        *state, token = _gather_layer_start(pick(by_layer[layer]), pick(kinds), pick(own[layer]), 0, after, name=f"{tag}_start")
        return (layer, idx, tag, state), token[:1, :1]

    def gather_finish(started, after):
        layer, idx, tag, state = started
        pick = lambda seq: [seq[i] for i in idx]
        landed = _gather_layer_wait(*state, pick(kinds), 0, after, name=f"{tag}_wait")
        done = _gather_layer_forward([by_layer[layer][i].shape for i in idx], pick(kinds), landed, name=f"{tag}_forward")
        full[layer].update(zip(pick(names), done))
        return done[0]

    ffn1, mixer, ffn2 = names[:2], names[2:7], names[7:]
    assert mixer[0] == "w_in" and ffn2[0] == "ffn2_w_up", names
    xc = x.reshape(t, d)
    l0_ffn1, token_ffn1 = gather_start(0, ffn1, xc, "gather_l0_ffn1")
    tabs = _rope_tables(s)
    bias = _na_expand_bias(na_rel_bias, name="na_bias")

    saved = []
    h = _rms_fwd(xc, ffn1_norm[:1] + token_ffn1, tt=NORM_ROWS, name="l0_ffn1_norm")
    landed = gather_finish(l0_ffn1, h)
    l0_mixer, token_mixer = gather_start(0, mixer, landed, "gather_l0_mixer")
    xc, s1 = _ffn_fwd(xc, h + token_mixer.astype(BF16), full[0]["ffn1_w_up"], full[0]["ffn1_w_down"], "l0_ffn1")
    landed = gather_finish(l0_mixer, xc)
    full[0] = _finish_w_in(full[0])
    l0_ffn2, token_ffn2 = gather_start(0, ffn2, landed, "gather_l0_ffn2")
    layer1, token_layer1 = gather_start(1, names, landed, "gather_l1")
    xc, s2 = _mixer_fwd(xc, b, mix_norm[:1] + token_ffn2 + token_layer1, full[0], bias[0], tabs, "l0_mix")
    gather_finish(l0_ffn2, xc)
    xc, s3 = _ffn_fwd(xc, _rms_fwd(xc, ffn2_norm[:1], tt=NORM_ROWS, name="l0_ffn2_norm"), full[0]["ffn2_w_up"], full[0]["ffn2_w_down"],
                      "l0_ffn2")
    saved.append((s1, s2, s3))
    gather_finish(layer1, xc)
    full[1] = _finish_w_in(full[1])
    for l in range(1, depth):
        xc, s1 = _ffn_fwd(xc, _rms_fwd(xc, ffn1_norm[l:l + 1], tt=NORM_ROWS, name=f"l{l}_ffn1_norm"), full[l]["ffn1_w_up"],
                          full[l]["ffn1_w_down"], f"l{l}_ffn1")
        xc, s2 = _mixer_fwd(xc, b, mix_norm[l:l + 1], full[l], bias[l], tabs, f"l{l}_mix")
        xc, s3 = _ffn_fwd(xc, _rms_fwd(xc, ffn2_norm[l:l + 1], tt=NORM_ROWS, name=f"l{l}_ffn2_norm"), full[l]["ffn2_w_up"],
                          full[l]["ffn2_w_down"], f"l{l}_ffn2")
        saved.append((s1, s2, s3))

    dx, dxb, d_final, loss_part = _final_loss(xc, final_norm.reshape(1, d), loss_target.reshape(t, d), tt=NORM_ROWS, name="final_loss")
    grads = _Grads()
    piece_names, piece_kinds, piece_sizes, piece_srcs = _scatter_pieces(shards)
    scattered = []

    def scatter(layer, weights):
        tag = f"grads{layer}_{weights[0]}"
        idx = [i for i, src in enumerate(piece_srcs) if src in weights]
        pick = lambda seq: [seq[i] for i in idx]
        *state, token = _grads_to_chips_start([grads.arrays[src, layer] for src in pick(piece_srcs)], pick(piece_kinds),
                                              pick(piece_sizes), layer, name=f"{tag}_to_chips_start")
        scattered.append((layer, idx, state))
        return token[:1, :1]
    small = {name: [None] * depth for name in REPLICATED[:-1]}
    for l in reversed(range(depth)):
        s1, s2, s3 = saved[l]
        dx, dxb, small["ffn2_norm"][l] = _ffn_bwd(dx, dxb, s3, ffn2_norm[l:l + 1], full[l]["ffn2_w_up"], full[l]["ffn2_w_down"],
                                                  l, grads, "ffn2", f"l{l}_ffn2", scatter)
        dx, dxb, small["mix_norm"][l], small["na_rel_bias"][l] = _mixer_bwd(
            dx, dxb, b, s2, mix_norm[l:l + 1], full[l], l, bias[l], tabs, grads, f"l{l}_mix", scatter)
        dx, dxb, small["ffn1_norm"][l] = _ffn_bwd(dx, dxb, s1, ffn1_norm[l:l + 1], full[l]["ffn1_w_up"], full[l]["ffn1_w_down"],
                                                  l, grads, "ffn1", f"l{l}_ffn1", scatter)
    grad_x = dx.reshape(b, s, d)
    reduced = [None] * len(piece_names)

    def arrive(group, after):
        layer, idx, state = group
        state = _grads_to_chips_wait(*state, [piece_kinds[i] for i in idx], [piece_sizes[i] for i in idx], layer, after,
                                     name=f"grads{layer}_{piece_names[idx[0]]}_to_chips_wait")
        for i, p, sl in zip(idx, *state):
            reduced[i] = _sum_slabs(sl, p, piece_kinds[i], piece_sizes[i], layer, reduced[i],
                                    name=f"grads{layer}_sum_{piece_names[i]}")
        return idx

    for group in scattered[:-1]:
        arrive(group, dx)
    late = scattered[-1][1]
    early = [i for i in range(len(piece_names)) if i not in late]
    g_out = _finish_weight_grads([reduced[i] for i in early], [piece_names[i] for i in early], "grads_early")

    parts = [jnp.stack(small[name]).reshape(-1) for name in REPLICATED[:-1]] + [d_final.reshape(-1), loss_part[0, :1]]
    sizes = [v.shape[0] for v in parts]
    flat = jnp.concatenate(parts)
    flat = jnp.pad(flat, (0, -flat.shape[0] % (8 * LANES)))
    small_sum = _all_sum_small(flat.reshape(-1, LANES), name="small_all_sum").reshape(-1)
    off = 0
    for name, n in zip(REPLICATED, sizes[:-1]):
        g_out[name] = small_sum[off:off + n].reshape(w[name].shape)
        off += n
    loss = small_sum[off]

    names = list(w)
    delta, new_m, new_v = {}, {}, {}
    for name in [n for n in names if n in g_out]:
        delta[name], new_m[name], new_v[name] = _adamw(w[name], g_out[name], mom[name], var[name], name=f"adamw_{name}")
    arrive(scattered[-1], delta["w_in"])
    g_out.update(_finish_weight_grads([reduced[i] for i in late], [piece_names[i] for i in late], "grads_late"))
    for name in [n for n in names if n not in delta]:
        delta[name], new_m[name], new_v[name] = _adamw(w[name], g_out[name], mom[name], var[name], name=f"adamw_{name}")
    return (loss, grad_x, *[g_out[n] for n in names], *[delta[n] for n in names], *[new_m[n] for n in names],
            *[new_v[n] for n in names])
```
